```python
import jax, jax.numpy as jnp
from jax import lax
import numpy as np

D_MODEL = 1024
BATCH = 32
SEQ = 2048
DEPTH = 1

HGRN_HEADS = 8
HGRN_DK = 128
HGRN_DV = D_MODEL // HGRN_HEADS
HGRN_F = HGRN_HEADS * HGRN_DK
HGRN_V = HGRN_HEADS * HGRN_DV
CHUNK = 64
ATT_HEADS = 16
ATT_KV_HEADS = 2
ATT_HD = 64
ATT_GROUP = ATT_HEADS // ATT_KV_HEADS
WINDOW = 128
ROPE_DIM = ATT_HD // 4
ROPE_THETA = 500000.0
D_FF = 2816
CONV_W = 3
EPS = 1e-6
NEG_INF = -1e30
COL_SIZES = (HGRN_F, HGRN_F, HGRN_V, HGRN_V, ATT_HEADS * ATT_HD, ATT_KV_HEADS * ATT_HD, ATT_KV_HEADS * ATT_HD, D_MODEL, D_MODEL)
D_IN = HGRN_F + HGRN_F + HGRN_V + HGRN_V + ATT_HEADS * ATT_HD + 2 * ATT_KV_HEADS * ATT_HD + 2 * D_MODEL

kernel_name = 'hybrid_hgrn2_swa_convffn'


def rmsnorm(x, g):
    xf = x.astype(jnp.float32)
    y = xf * lax.rsqrt(jnp.mean(xf * xf, axis=-1, keepdims=True) + EPS)
    return (y * g.astype(jnp.float32)).astype(x.dtype)


def split_cols(z):
    idx = []
    acc = 0
    for s in COL_SIZES[:-1]:
        acc += s
        idx.append(acc)
    return jnp.split(z, idx, axis=-1)


def partial_rope(x, pos):
    half = ROPE_DIM // 2
    inv = ROPE_THETA ** (-2.0 * jnp.arange(half, dtype=jnp.float32) / ROPE_DIM)
    ang = pos.astype(jnp.float32)[..., None] * inv
    cos = jnp.cos(ang)[:, :, None, :]
    sin = jnp.sin(ang)[:, :, None, :]
    xr = x[..., :ROPE_DIM].astype(jnp.float32)
    x1, x2 = xr[..., :half], xr[..., half:]
    rot = jnp.concatenate([x1 * cos - x2 * sin, x2 * cos + x1 * sin], axis=-1).astype(x.dtype)
    return jnp.concatenate([rot, x[..., ROPE_DIM:]], axis=-1)


def hgrn2_chunkwise(q, fz, i, lb):
    B, S = q.shape[0], q.shape[1]
    n = S // CHUNK
    qf = jax.nn.silu(q.astype(jnp.float32))
    f = lb + (1.0 - lb) * jax.nn.sigmoid(fz.astype(jnp.float32))
    logf = jnp.log(f)
    k = 1.0 - f

    def chunks(t, d):
        return t.reshape(B, n, CHUNK, HGRN_HEADS, d).transpose(0, 3, 1, 2, 4)

    qc = chunks(qf, HGRN_DK)
    kc = chunks(k, HGRN_DK)
    vc = chunks(i.astype(jnp.float32), HGRN_DV)
    b = jnp.cumsum(chunks(logf, HGRN_DK), axis=3)
    bref = b[:, :, :, CHUNK // 2:CHUNK // 2 + 1]
    q_in = qc * jnp.exp(b - bref)
    k_in = kc * jnp.exp(bref - b)
    causal = jnp.tril(jnp.ones((CHUNK, CHUNK), dtype=bool))
    a = jnp.where(causal, jnp.einsum('bhncd,bhnsd->bhncs', q_in, k_in), 0.0)
    o_intra = jnp.einsum('bhncs,bhnse->bhnce', a, vc)
    blast = b[:, :, :, -1:]
    q_out = qc * jnp.exp(b)
    k_st = kc * jnp.exp(blast - b)
    dec = jnp.exp(blast[:, :, :, 0])

    def step(state, xs):
        qo, ks, vv, d = xs
        o = jnp.einsum('bhcd,bhde->bhce', qo, state)
        state = state * d[..., None] + jnp.einsum('bhcd,bhce->bhde', ks, vv)
        return state, o

    s0 = jnp.zeros((B, HGRN_HEADS, HGRN_DK, HGRN_DV), jnp.float32)
    mv = lambda t: jnp.moveaxis(t, 2, 0)
    _, o_inter = lax.scan(step, s0, (mv(q_out), mv(k_st), mv(vc), mv(dec)))
    o = o_intra + jnp.moveaxis(o_inter, 0, 2)
    return o.transpose(0, 2, 3, 1, 4).reshape(B, S, HGRN_HEADS, HGRN_DV)


def sliding_window_gqa(q, k, v, sinks, pos):
    B, S = q.shape[0], q.shape[1]
    nb = S // WINDOW
    q = partial_rope(q, pos)
    k = partial_rope(k, pos)
    qb = q.reshape(B, nb, WINDOW, ATT_KV_HEADS, ATT_GROUP, ATT_HD)

    def band_keys(t):
        tp = jnp.pad(t, ((0, 0), (WINDOW, 0), (0, 0), (0, 0))).reshape(B, nb + 1, WINDOW, ATT_KV_HEADS, ATT_HD)
        return jnp.concatenate([tp[:, :-1], tp[:, 1:]], axis=2)

    kb = band_keys(k)
    vb = band_keys(v)
    s = jnp.einsum('bnqkgd,bnmkd->bnkgqm', qb, kb).astype(jnp.float32) * (ATT_HD ** -0.5)
    qi = jnp.arange(WINDOW)[:, None]
    mi = jnp.arange(2 * WINDOW)[None, :]
    band = (mi > qi) & (mi <= qi + WINDOW)
    blk = jnp.arange(nb)[:, None, None]
    mask = band[None] & ((blk > 0) | (mi >= WINDOW)[None])
    s = jnp.where(mask[None, :, None, None], s, NEG_INF)
    sink = sinks.astype(jnp.float32).reshape(1, 1, ATT_KV_HEADS, ATT_GROUP, 1, 1)
    m = jnp.maximum(jnp.max(s, axis=-1, keepdims=True), sink)
    p = jnp.exp(s - m)
    den = jnp.sum(p, axis=-1, keepdims=True) + jnp.exp(sink - m)
    p = (p / den).astype(v.dtype)
    o = jnp.einsum('bnkgqm,bnmkd->bnqkgd', p, vb)
    return o.reshape(B, S, ATT_HEADS * ATT_HD)


def causal_dwconv(x, w, bias):
    y = lax.conv_general_dilated(x, w.astype(x.dtype)[:, None, :], window_strides=(1,), padding=[(CONV_W - 1, 0)], dimension_numbers=('NWC', 'WIO', 'NWC'), feature_group_count=x.shape[-1])
    return y + bias.astype(x.dtype)


def _fwd_setup_inputs(seed: int = 0) -> dict:
    key = jax.random.key(seed)
    ks = jax.random.split(key, 18)
    nrm = lambda k, shape, fan_in: jax.random.normal(k, shape, jnp.float32) * (fan_in ** -0.5)
    x = jax.random.normal(ks[0], (BATCH, SEQ, D_MODEL), jnp.float32)
    offs = jax.random.randint(ks[1], (BATCH, 1), 0, 4096, dtype=jnp.int32)
    positions = (jnp.arange(SEQ, dtype=jnp.int32)[None, :] + offs).astype(jnp.int32)
    return {
        'x': x,
        'positions': positions,
        'norm1_g': 1.0 + 0.02 * jax.random.normal(ks[2], (DEPTH, D_MODEL), jnp.float32),
        'w_in': nrm(ks[3], (DEPTH, D_MODEL, D_IN), D_MODEL),
        'lb_logits': 0.1 * jax.random.normal(ks[4], (DEPTH + 1, HGRN_F), jnp.float32),
        'hgrn_norm_g': 1.0 + 0.02 * jax.random.normal(ks[5], (DEPTH, HGRN_DV), jnp.float32),
        'w_a': nrm(ks[6], (DEPTH, HGRN_V, D_MODEL), HGRN_V),
        'attn_sinks': 0.5 * jax.random.normal(ks[7], (DEPTH, ATT_HEADS), jnp.float32),
        'w_b': nrm(ks[8], (DEPTH, ATT_HEADS * ATT_HD, D_MODEL), ATT_HEADS * ATT_HD),
        'w_out': nrm(ks[9], (DEPTH, D_MODEL, D_MODEL), D_MODEL),
        'norm2_g': 1.0 + 0.02 * jax.random.normal(ks[10], (DEPTH, D_MODEL), jnp.float32),
        'w_ffn_in': nrm(ks[11], (DEPTH, D_MODEL, 2 * D_FF), D_MODEL),
        'conv_w': nrm(ks[12], (DEPTH, CONV_W, D_FF), CONV_W),
        'conv_b': 0.02 * jax.random.normal(ks[13], (DEPTH, D_FF), jnp.float32),
        'w_down': nrm(ks[14], (DEPTH, D_FF, D_MODEL), D_FF),
        'final_g': 1.0 + 0.02 * jax.random.normal(ks[15], (D_MODEL,), jnp.float32),
    }


def _fwd_reference(x, positions, norm1_g, w_in, lb_logits, hgrn_norm_g, w_a, attn_sinks, w_b, w_out, norm2_g, w_ffn_in, conv_w, conv_b, w_down, final_g):
    B, S = x.shape[0], x.shape[1]
    lb_all = jnp.cumsum(jax.nn.softmax(lb_logits.astype(jnp.float32), axis=0), axis=0)
    h = x
    for l in range(DEPTH):
        u = rmsnorm(h, norm1_g[l])
        z = u @ w_in[l]
        hq, hf, hi, hg, aq, ak, av, ga, gb = split_cols(z)
        o_a = hgrn2_chunkwise(hq, hf, hi, lb_all[l])
        o_a = (rmsnorm(o_a, hgrn_norm_g[l]).reshape(B, S, HGRN_V) * jax.nn.silu(hg.astype(jnp.float32))).astype(x.dtype)
        o_b = sliding_window_gqa(aq.reshape(B, S, ATT_HEADS, ATT_HD), ak.reshape(B, S, ATT_KV_HEADS, ATT_HD), av.reshape(B, S, ATT_KV_HEADS, ATT_HD), attn_sinks[l], positions)
        merged = jax.nn.sigmoid(ga) * (o_a @ w_a[l]) + jax.nn.sigmoid(gb) * (o_b @ w_b[l])
        h = h + merged @ w_out[l]
        u = rmsnorm(h, norm2_g[l])
        gu = u @ w_ffn_in[l]
        g, up = gu[..., :D_FF], gu[..., D_FF:]
        a = causal_dwconv(g, conv_w[l], conv_b[l])
        h = h + (jax.nn.silu(a) * up) @ w_down[l]
    return rmsnorm(h, final_g)


import jax as _jax
import jax.numpy as _jnp

TWIN_FORMAT = 'train_step'
FWD_PARAMS = ['x', 'positions', 'norm1_g', 'w_in', 'lb_logits', 'hgrn_norm_g', 'w_a', 'attn_sinks', 'w_b', 'w_out', 'norm2_g', 'w_ffn_in', 'conv_w', 'conv_b', 'w_down', 'final_g']
TWIN_WEIGHTS = ['norm1_g', 'w_in', 'lb_logits', 'hgrn_norm_g', 'w_a', 'attn_sinks', 'w_b', 'w_out', 'norm2_g', 'w_ffn_in', 'conv_w', 'conv_b', 'w_down', 'final_g']
TWIN_DIFF_INPUT = 'x'
TWIN_INPUTS = ['x', 'positions', 'norm1_g', 'w_in', 'lb_logits', 'hgrn_norm_g', 'w_a', 'attn_sinks', 'w_b', 'w_out', 'norm2_g', 'w_ffn_in', 'conv_w', 'conv_b', 'w_down', 'final_g', 'loss_target', 'm_norm1_g', 'm_w_in', 'm_lb_logits', 'm_hgrn_norm_g', 'm_w_a', 'm_attn_sinks', 'm_w_b', 'm_w_out', 'm_norm2_g', 'm_w_ffn_in', 'm_conv_w', 'm_conv_b', 'm_w_down', 'm_final_g', 'v_norm1_g', 'v_w_in', 'v_lb_logits', 'v_hgrn_norm_g', 'v_w_a', 'v_attn_sinks', 'v_w_b', 'v_w_out', 'v_norm2_g', 'v_w_ffn_in', 'v_conv_w', 'v_conv_b', 'v_w_down', 'v_final_g']
TWIN_OUTPUTS = ['loss', 'grad_x', 'grad_norm1_g', 'grad_w_in', 'grad_lb_logits', 'grad_hgrn_norm_g', 'grad_w_a', 'grad_attn_sinks', 'grad_w_b', 'grad_w_out', 'grad_norm2_g', 'grad_w_ffn_in', 'grad_conv_w', 'grad_conv_b', 'grad_w_down', 'grad_final_g', 'delta_norm1_g', 'delta_w_in', 'delta_lb_logits', 'delta_hgrn_norm_g', 'delta_w_a', 'delta_attn_sinks', 'delta_w_b', 'delta_w_out', 'delta_norm2_g', 'delta_w_ffn_in', 'delta_conv_w', 'delta_conv_b', 'delta_w_down', 'delta_final_g', 'new_m_norm1_g', 'new_m_w_in', 'new_m_lb_logits', 'new_m_hgrn_norm_g', 'new_m_w_a', 'new_m_attn_sinks', 'new_m_w_b', 'new_m_w_out', 'new_m_norm2_g', 'new_m_w_ffn_in', 'new_m_conv_w', 'new_m_conv_b', 'new_m_w_down', 'new_m_final_g', 'new_v_norm1_g', 'new_v_w_in', 'new_v_lb_logits', 'new_v_hgrn_norm_g', 'new_v_w_a', 'new_v_attn_sinks', 'new_v_w_b', 'new_v_w_out', 'new_v_norm2_g', 'new_v_w_ffn_in', 'new_v_conv_w', 'new_v_conv_b', 'new_v_w_down', 'new_v_final_g']
TWIN_LEAF_KINDS = {'loss': 'loss', 'grad_x': 'grad_x', 'grad_norm1_g': 'grad_w', 'grad_w_in': 'grad_w', 'grad_lb_logits': 'grad_w', 'grad_hgrn_norm_g': 'grad_w', 'grad_w_a': 'grad_w', 'grad_attn_sinks': 'grad_w', 'grad_w_b': 'grad_w', 'grad_w_out': 'grad_w', 'grad_norm2_g': 'grad_w', 'grad_w_ffn_in': 'grad_w', 'grad_conv_w': 'grad_w', 'grad_conv_b': 'grad_w', 'grad_w_down': 'grad_w', 'grad_final_g': 'grad_w', 'delta_norm1_g': 'delta_w', 'delta_w_in': 'delta_w', 'delta_lb_logits': 'delta_w', 'delta_hgrn_norm_g': 'delta_w', 'delta_w_a': 'delta_w', 'delta_attn_sinks': 'delta_w', 'delta_w_b': 'delta_w', 'delta_w_out': 'delta_w', 'delta_norm2_g': 'delta_w', 'delta_w_ffn_in': 'delta_w', 'delta_conv_w': 'delta_w', 'delta_conv_b': 'delta_w', 'delta_w_down': 'delta_w', 'delta_final_g': 'delta_w', 'new_m_norm1_g': 'new_m', 'new_m_w_in': 'new_m', 'new_m_lb_logits': 'new_m', 'new_m_hgrn_norm_g': 'new_m', 'new_m_w_a': 'new_m', 'new_m_attn_sinks': 'new_m', 'new_m_w_b': 'new_m', 'new_m_w_out': 'new_m', 'new_m_norm2_g': 'new_m', 'new_m_w_ffn_in': 'new_m', 'new_m_conv_w': 'new_m', 'new_m_conv_b': 'new_m', 'new_m_w_down': 'new_m', 'new_m_final_g': 'new_m', 'new_v_norm1_g': 'new_v', 'new_v_w_in': 'new_v', 'new_v_lb_logits': 'new_v', 'new_v_hgrn_norm_g': 'new_v', 'new_v_w_a': 'new_v', 'new_v_attn_sinks': 'new_v', 'new_v_w_b': 'new_v', 'new_v_w_out': 'new_v', 'new_v_norm2_g': 'new_v', 'new_v_w_ffn_in': 'new_v', 'new_v_conv_w': 'new_v', 'new_v_conv_b': 'new_v', 'new_v_w_down': 'new_v', 'new_v_final_g': 'new_v'}


def _forward(args):
    return _fwd_reference(*[args[k] for k in FWD_PARAMS])


def _output_shape():
    out = _jax.eval_shape(lambda: _forward(_fwd_setup_inputs(0)))
    return out.shape, out.dtype

N_MICROBATCH = 1
ADAM_LR = 0.001
ADAM_B1 = 0.9
ADAM_B2 = 0.999
ADAM_EPS = 1e-08
ADAM_WD = 0.01
ADAM_STEP = 10
PER_EXAMPLE_BATCH_AXIS = {'x': 0, 'positions': 0, 'loss_target': 0}
SHARED_INPUTS = []
_WEIGHT_DTYPES = {'norm1_g': _jnp.float32, 'w_in': _jnp.float32, 'lb_logits': _jnp.float32, 'hgrn_norm_g': _jnp.float32, 'w_a': _jnp.float32, 'attn_sinks': _jnp.float32, 'w_b': _jnp.float32, 'w_out': _jnp.float32, 'norm2_g': _jnp.float32, 'w_ffn_in': _jnp.float32, 'conv_w': _jnp.float32, 'conv_b': _jnp.float32, 'w_down': _jnp.float32, 'final_g': _jnp.float32}
MOMENT_SCALE = {'norm1_g': 1.579841e-01, 'w_in': 5.145020e-02, 'lb_logits': 8.584946e-03, 'hgrn_norm_g': 2.367292e-01, 'w_a': 8.732853e-02, 'attn_sinks': 2.472060e-02, 'w_b': 2.988829e-02, 'w_out': 9.133479e-02, 'norm2_g': 1.871142e-01, 'w_ffn_in': 7.730987e-02, 'conv_w': 7.860130e-02, 'conv_b': 7.480888e-02, 'w_down': 1.258990e-01, 'final_g': 6.375233e+01}


def _to_microbatches(a, axis):
    t = _jnp.moveaxis(a, axis, 0)
    t = t.reshape((N_MICROBATCH, t.shape[0] // N_MICROBATCH) + t.shape[1:])
    return _jnp.moveaxis(t, 1, axis + 1)


def setup_inputs(seed: int = 0) -> dict:
    inp = _fwd_setup_inputs(seed)
    key = _jax.random.fold_in(_jax.random.key(seed), 7919)
    shape, _ = _output_shape()
    out = dict(inp)
    out["loss_target"] = _jax.random.normal(_jax.random.fold_in(key, 0), shape, _jnp.float32)
    for i, name in enumerate(TWIN_WEIGHTS):
        w = inp[name].astype(_jnp.float32)
        if MOMENT_SCALE is None:
            s = _jnp.sqrt(_jnp.mean(_jnp.square(w)) + 1e-30)
        else:
            s = MOMENT_SCALE[name]
        km, kv = _jax.random.split(_jax.random.fold_in(key, i + 1))
        out[name] = w
        out["m_" + name] = s * _jax.random.normal(km, w.shape, _jnp.float32)
        out["v_" + name] = (s * s) * _jax.random.uniform(kv, w.shape, _jnp.float32, 0.5, 1.5)
    if N_MICROBATCH > 1:
        for name, axis in PER_EXAMPLE_BATCH_AXIS.items():
            out[name] = _to_microbatches(out[name], axis)
    return {'x': out['x'], 'positions': out['positions'], 'norm1_g': out['norm1_g'], 'w_in': out['w_in'], 'lb_logits': out['lb_logits'], 'hgrn_norm_g': out['hgrn_norm_g'], 'w_a': out['w_a'], 'attn_sinks': out['attn_sinks'], 'w_b': out['w_b'], 'w_out': out['w_out'], 'norm2_g': out['norm2_g'], 'w_ffn_in': out['w_ffn_in'], 'conv_w': out['conv_w'], 'conv_b': out['conv_b'], 'w_down': out['w_down'], 'final_g': out['final_g'], 'loss_target': out['loss_target'], 'm_norm1_g': out['m_norm1_g'], 'm_w_in': out['m_w_in'], 'm_lb_logits': out['m_lb_logits'], 'm_hgrn_norm_g': out['m_hgrn_norm_g'], 'm_w_a': out['m_w_a'], 'm_attn_sinks': out['m_attn_sinks'], 'm_w_b': out['m_w_b'], 'm_w_out': out['m_w_out'], 'm_norm2_g': out['m_norm2_g'], 'm_w_ffn_in': out['m_w_ffn_in'], 'm_conv_w': out['m_conv_w'], 'm_conv_b': out['m_conv_b'], 'm_w_down': out['m_w_down'], 'm_final_g': out['m_final_g'], 'v_norm1_g': out['v_norm1_g'], 'v_w_in': out['v_w_in'], 'v_lb_logits': out['v_lb_logits'], 'v_hgrn_norm_g': out['v_hgrn_norm_g'], 'v_w_a': out['v_w_a'], 'v_attn_sinks': out['v_attn_sinks'], 'v_w_b': out['v_w_b'], 'v_w_out': out['v_w_out'], 'v_norm2_g': out['v_norm2_g'], 'v_w_ffn_in': out['v_w_ffn_in'], 'v_conv_w': out['v_conv_w'], 'v_conv_b': out['v_conv_b'], 'v_w_down': out['v_w_down'], 'v_final_g': out['v_final_g']}


def _loss(weights, diff, rest, loss_target):
    with _jax.named_scope("forward"):
        args = {**rest, TWIN_DIFF_INPUT: diff, **{k: w.astype(_WEIGHT_DTYPES[k]) for k, w in weights.items()}}
        y = _forward(args)
    with _jax.named_scope("loss_head"):
        err = _jnp.square(y.astype(_jnp.float32) - loss_target)
        return 0.5 * _jnp.sum(_jnp.mean(err, axis=-1)) if err.ndim else 0.5 * err


def _adamw(w, g, m, v):
    m = ADAM_B1 * m + (1.0 - ADAM_B1) * g
    v = ADAM_B2 * v + (1.0 - ADAM_B2) * _jnp.square(g)
    m_hat = m / (1.0 - ADAM_B1 ** ADAM_STEP)
    v_hat = v / (1.0 - ADAM_B2 ** ADAM_STEP)
    delta = -ADAM_LR * (m_hat / (_jnp.sqrt(v_hat) + ADAM_EPS) + ADAM_WD * w)
    return delta, m, v


def reference(x, positions, norm1_g, w_in, lb_logits, hgrn_norm_g, w_a, attn_sinks, w_b, w_out, norm2_g, w_ffn_in, conv_w, conv_b, w_down, final_g, loss_target, m_norm1_g, m_w_in, m_lb_logits, m_hgrn_norm_g, m_w_a, m_attn_sinks, m_w_b, m_w_out, m_norm2_g, m_w_ffn_in, m_conv_w, m_conv_b, m_w_down, m_final_g, v_norm1_g, v_w_in, v_lb_logits, v_hgrn_norm_g, v_w_a, v_attn_sinks, v_w_b, v_w_out, v_norm2_g, v_w_ffn_in, v_conv_w, v_conv_b, v_w_down, v_final_g):
    given = dict(x=x, positions=positions, norm1_g=norm1_g, w_in=w_in, lb_logits=lb_logits, hgrn_norm_g=hgrn_norm_g, w_a=w_a, attn_sinks=attn_sinks, w_b=w_b, w_out=w_out, norm2_g=norm2_g, w_ffn_in=w_ffn_in, conv_w=conv_w, conv_b=conv_b, w_down=w_down, final_g=final_g, loss_target=loss_target, m_norm1_g=m_norm1_g, m_w_in=m_w_in, m_lb_logits=m_lb_logits, m_hgrn_norm_g=m_hgrn_norm_g, m_w_a=m_w_a, m_attn_sinks=m_attn_sinks, m_w_b=m_w_b, m_w_out=m_w_out, m_norm2_g=m_norm2_g, m_w_ffn_in=m_w_ffn_in, m_conv_w=m_conv_w, m_conv_b=m_conv_b, m_w_down=m_w_down, m_final_g=m_final_g, v_norm1_g=v_norm1_g, v_w_in=v_w_in, v_lb_logits=v_lb_logits, v_hgrn_norm_g=v_hgrn_norm_g, v_w_a=v_w_a, v_attn_sinks=v_attn_sinks, v_w_b=v_w_b, v_w_out=v_w_out, v_norm2_g=v_norm2_g, v_w_ffn_in=v_w_ffn_in, v_conv_w=v_conv_w, v_conv_b=v_conv_b, v_w_down=v_w_down, v_final_g=v_final_g)
    weights = {n: given[n] for n in TWIN_WEIGHTS}
    shared = {n: given[n] for n in SHARED_INPUTS}
    per_example = {n: given[n] for n in ['x', 'positions']}
    grad_fn = _jax.value_and_grad(_loss, argnums=(0, 1))

    def one_microbatch(ex, loss_target):
        ex = dict(ex)
        diff = ex.pop(TWIN_DIFF_INPUT)
        return grad_fn(weights, diff, {**shared, **ex}, loss_target)

    if N_MICROBATCH == 1:
        loss, (grad_w, grad_x) = one_microbatch(per_example, given["loss_target"])
    else:
        def body(carry, xs):
            loss_sum, grad_sum = carry
            l_k, (gw_k, gx_k) = one_microbatch(xs[0], xs[1])
            with _jax.named_scope("update"):
                return (loss_sum + l_k, _jax.tree.map(_jnp.add, grad_sum, gw_k)), gx_k

        init = (_jnp.zeros((), _jnp.float32), _jax.tree.map(_jnp.zeros_like, weights))
        (loss, grad_w), grad_x = _jax.lax.scan(body, init, (per_example, given["loss_target"]))
    with _jax.named_scope("update"):
        delta_w, new_m, new_v = {}, {}, {}
        for n in TWIN_WEIGHTS:
            delta_w[n], new_m[n], new_v[n] = _adamw(weights[n], grad_w[n], given["m_" + n], given["v_" + n])
    return (loss, grad_x, *[grad_w[n] for n in TWIN_WEIGHTS], *[delta_w[n] for n in TWIN_WEIGHTS],
            *[new_m[n] for n in TWIN_WEIGHTS], *[new_v[n] for n in TWIN_WEIGHTS])
```

```python
import functools

import jax
import jax.numpy as jnp
from jax import lax
from jax.experimental import pallas as pl
from jax.experimental.pallas import tpu as pltpu

F32 = jnp.float32
BF16 = jnp.bfloat16

D_MODEL = 1024
HGRN_HEADS = 8
HGRN_DK = 128
CHUNK = 64
ATT_HEADS = 16
ATT_KV_HEADS = 2
ATT_HD = 64
ATT_GROUP = ATT_HEADS // ATT_KV_HEADS
WINDOW = 128
ROPE_DIM = ATT_HD // 4
ROPE_THETA = 500000.0
D_FF = 2816
EPS = 1e-6
NEG_INF = -1e30
N_DEV = 8

ADAM_LR = 0.001
ADAM_B1 = 0.9
ADAM_B2 = 0.999
ADAM_EPS = 1e-08
ADAM_WD = 0.01
ADAM_STEP = 10

MESH = pl.DeviceIdType.MESH
ANY = pl.BlockSpec(memory_space=pl.ANY)


def _pick(n, cands):
    for c in cands:
        if n % c == 0:
            return c
    return n


def _sigmoid(x):
    return 1.0 / (1.0 + jnp.exp(-x))


def _silu(x):
    return x * _sigmoid(x)


def _rms(x, g):
    return x * lax.rsqrt(jnp.mean(x * x, axis=-1, keepdims=True) + EPS) * g


def _dot(a, b, dims):
    return lax.dot_general(a, b, (dims, ((), ())), preferred_element_type=F32)


def _nn(a, b):
    return _dot(a, b, ((1,), (0,)))


def _nt(a, b):
    return _dot(a, b, ((1,), (1,)))


def _tn(a, b):
    return _dot(a, b, ((0,), (0,)))


def _params(*sem):
    return pltpu.CompilerParams(dimension_semantics=sem, vmem_limit_bytes=56 * 1024 * 1024)


def _matmul(a, b, *, ta=False, tb=False, out_dtype=F32, addend=None, name, tm=None, tn=None, tk=None):
    M, K = (a.shape[1], a.shape[0]) if ta else a.shape
    N = b.shape[0] if tb else b.shape[1]
    assert (b.shape[1] if tb else b.shape[0]) == K
    tm = tm or _pick(M, (512, 256, 128))
    tn = tn or _pick(N, (1024, 512, 256, 128))
    tk = tk or (K if K <= 4096 and not ta else _pick(K, (1024, 512)))
    nk = K // tk
    grid = (M // tm, N // tn, nk)
    a_spec = pl.BlockSpec((tk, tm), lambda i, j, k: (k, i)) if ta else pl.BlockSpec((tm, tk), lambda i, j, k: (i, k))
    b_spec = pl.BlockSpec((tn, tk), lambda i, j, k: (j, k)) if tb else pl.BlockSpec((tk, tn), lambda i, j, k: (k, j))
    o_spec = pl.BlockSpec((tm, tn), lambda i, j, k: (i, j))
    dims = ((0 if ta else 1,), (1 if tb else 0,))
    has_add = addend is not None

    def body(*refs):
        if has_add:
            a_ref, b_ref, c_ref, o_ref = refs[:4]
        else:
            a_ref, b_ref, o_ref = refs[:3]
            c_ref = None
        part = _dot(a_ref[...], b_ref[...], dims)
        if nk == 1:
            if has_add:
                part = part + c_ref[...].astype(F32)
            o_ref[...] = part.astype(out_dtype)
        else:
            acc_ref = refs[-1]
            k = pl.program_id(2)

            @pl.when(k == 0)
            def _():
                acc_ref[...] = part + c_ref[...].astype(F32) if has_add else part

            @pl.when(k > 0)
            def _():
                acc_ref[...] += part

            @pl.when(k == nk - 1)
            def _():
                o_ref[...] = acc_ref[...].astype(out_dtype)

    in_specs = [a_spec, b_spec] + ([o_spec] if has_add else [])
    args = (a, b) + ((addend,) if has_add else ())
    return pl.pallas_call(
        body,
        name=name,
        grid=grid,
        in_specs=in_specs,
        out_specs=o_spec,
        out_shape=jax.ShapeDtypeStruct((M, N), out_dtype),
        scratch_shapes=[pltpu.VMEM((tm, tn), F32)] if nk > 1 else [],
        compiler_params=_params("parallel", "parallel", "arbitrary"),
    )(*args)


def _row_spec(tm, n):
    return pl.BlockSpec((tm, n), lambda i: (i, 0))


def _full_spec(shape):
    return pl.BlockSpec(shape, lambda i: tuple(0 for _ in shape))


def _norm_cast(x, g, *, name):
    T, D = x.shape
    tm = _pick(T, (512, 256, 128))

    def body(x_ref, g_ref, u_ref):
        u_ref[...] = _rms(x_ref[...], g_ref[...]).astype(BF16)

    return pl.pallas_call(
        body, name=name, grid=(T // tm,),
        in_specs=[_row_spec(tm, D), _full_spec((1, D))],
        out_specs=_row_spec(tm, D),
        out_shape=jax.ShapeDtypeStruct((T, D), BF16),
        compiler_params=_params("parallel"),
    )(x, g)


def _norm_bwd_add(x, g, du, dres, *, name):
    T, D = x.shape
    tm = _pick(T, (512, 256, 128))
    has_res = dres is not None

    def body(*refs):
        if has_res:
            x_ref, g_ref, du_ref, dr_ref, dx_ref, dxb_ref, dg_ref = refs
        else:
            x_ref, g_ref, du_ref, dx_ref, dxb_ref, dg_ref = refs
        _, vjp = jax.vjp(_rms, x_ref[...], g_ref[...])
        dx, dg = vjp(du_ref[...].astype(F32))
        if has_res:
            dx = dx + dr_ref[...]
        dx_ref[...] = dx
        dxb_ref[...] = dx.astype(BF16)

        @pl.when(pl.program_id(0) == 0)
        def _():
            dg_ref[...] = jnp.zeros_like(dg_ref)

        dg_ref[...] += dg

    ins = [x, g, du] + ([dres] if has_res else [])
    in_specs = [_row_spec(tm, D), _full_spec((1, D)), _row_spec(tm, D)] + ([_row_spec(tm, D)] if has_res else [])
    return pl.pallas_call(
        body, name=name, grid=(T // tm,),
        in_specs=in_specs,
        out_specs=[_row_spec(tm, D), _row_spec(tm, D), _full_spec((1, D))],
        out_shape=[jax.ShapeDtypeStruct((T, D), F32), jax.ShapeDtypeStruct((T, D), BF16), jax.ShapeDtypeStruct((1, D), F32)],
        compiler_params=_params("arbitrary"),
    )(*ins)


def _final_loss_bwd(h2, g, target, *, name):
    T, D = h2.shape
    tm = _pick(T, (512, 256, 128))

    def body(h_ref, g_ref, t_ref, dx_ref, dxb_ref, dg_ref, loss_ref):
        y, vjp = jax.vjp(_rms, h_ref[...], g_ref[...])
        err = y - t_ref[...]
        dx, dg = vjp(err * (1.0 / D))
        dx_ref[...] = dx
        dxb_ref[...] = dx.astype(BF16)

        @pl.when(pl.program_id(0) == 0)
        def _():
            dg_ref[...] = jnp.zeros_like(dg_ref)
            loss_ref[...] = jnp.zeros_like(loss_ref)

        dg_ref[...] += dg
        loss_ref[...] += (0.5 / D) * jnp.sum(jnp.sum(err * err, axis=1, keepdims=True), axis=0, keepdims=True)

    return pl.pallas_call(
        body, name=name, grid=(T // tm,),
        in_specs=[_row_spec(tm, D), _full_spec((1, D)), _row_spec(tm, D)],
        out_specs=[_row_spec(tm, D), _row_spec(tm, D), _full_spec((1, D)), _full_spec((1, 1))],
        out_shape=[jax.ShapeDtypeStruct((T, D), F32), jax.ShapeDtypeStruct((T, D), BF16), jax.ShapeDtypeStruct((1, D), F32), jax.ShapeDtypeStruct((1, 1), F32)],
        compiler_params=_params("arbitrary"),
    )(h2, g, target)


def _merge_fn(gates, a, b):
    ga = gates[:, :D_MODEL].astype(F32)
    gb = gates[:, D_MODEL:].astype(F32)
    return _sigmoid(ga) * a.astype(F32) + _sigmoid(gb) * b.astype(F32)


def _merge_fwd(gates, a, b, *, name):
    T = a.shape[0]
    tm = _pick(T, (512, 256, 128))

    def body(g_ref, a_ref, b_ref, o_ref):
        o_ref[...] = _merge_fn(g_ref[...], a_ref[...], b_ref[...]).astype(BF16)

    return pl.pallas_call(
        body, name=name, grid=(T // tm,),
        in_specs=[_row_spec(tm, 2 * D_MODEL), _row_spec(tm, D_MODEL), _row_spec(tm, D_MODEL)],
        out_specs=_row_spec(tm, D_MODEL),
        out_shape=jax.ShapeDtypeStruct((T, D_MODEL), BF16),
        compiler_params=_params("parallel"),
    )(gates, a, b)


def _merge_bwd(gates, a, b, dmerged, *, name):
    T = a.shape[0]
    tm = _pick(T, (512, 256, 128))

    def body(g_ref, a_ref, b_ref, dm_ref, dg_ref, da_ref, db_ref):
        g = g_ref[...].astype(F32)
        dm = dm_ref[...].astype(F32)
        sa = _sigmoid(g[:, :D_MODEL])
        sb = _sigmoid(g[:, D_MODEL:])
        da_ref[...] = (dm * sa).astype(BF16)
        db_ref[...] = (dm * sb).astype(BF16)
        dg_ref[:, :D_MODEL] = (dm * a_ref[...].astype(F32) * sa * (1.0 - sa)).astype(BF16)
        dg_ref[:, D_MODEL:] = (dm * b_ref[...].astype(F32) * sb * (1.0 - sb)).astype(BF16)

    return pl.pallas_call(
        body, name=name, grid=(T // tm,),
        in_specs=[_row_spec(tm, 2 * D_MODEL), _row_spec(tm, D_MODEL), _row_spec(tm, D_MODEL), _row_spec(tm, D_MODEL)],
        out_specs=[_row_spec(tm, 2 * D_MODEL), _row_spec(tm, D_MODEL), _row_spec(tm, D_MODEL)],
        out_shape=[jax.ShapeDtypeStruct((T, 2 * D_MODEL), BF16), jax.ShapeDtypeStruct((T, D_MODEL), BF16), jax.ShapeDtypeStruct((T, D_MODEL), BF16)],
        compiler_params=_params("parallel"),
    )(gates, a, b, dmerged)


CONV_TC = 128


def _shift_down(x, n, rows):
    return jnp.where(rows >= n, pltpu.roll(x, n, 0), 0.0)


def _shift_up(x, n, rows, S):
    return jnp.where(rows < S - n, pltpu.roll(x, S - n, 0), 0.0)


def _conv_act_fwd(gu, conv_w, conv_b, *, name):
    B, S, _ = gu.shape
    tc = CONV_TC
    nc = D_FF // tc

    def body(g_ref, up_ref, w_ref, b_ref, o_ref):
        g = g_ref[...].astype(F32)
        rows = lax.broadcasted_iota(jnp.int32, g.shape, 0)
        w = w_ref[...]
        a = w[2:3] * g + w[1:2] * _shift_down(g, 1, rows) + w[0:1] * _shift_down(g, 2, rows) + b_ref[...]
        o_ref[...] = (_silu(a) * up_ref[...].astype(F32)).astype(BF16)

    return pl.pallas_call(
        body, name=name, grid=(B, nc),
        in_specs=[pl.BlockSpec((None, S, tc), lambda b, j: (b, 0, j)),
                  pl.BlockSpec((None, S, tc), lambda b, j: (b, 0, j + nc)),
                  pl.BlockSpec((3, tc), lambda b, j: (0, j)),
                  pl.BlockSpec((1, tc), lambda b, j: (0, j))],
        out_specs=pl.BlockSpec((None, S, tc), lambda b, j: (b, 0, j)),
        out_shape=jax.ShapeDtypeStruct((B, S, D_FF), BF16),
        compiler_params=_params("parallel", "parallel"),
    )(gu, gu, conv_w, conv_b)


def _conv_act_bwd(gu, conv_w, conv_b, dact, *, name):
    B, S, _ = gu.shape
    tc = CONV_TC
    nc = D_FF // tc

    def body(g_ref, up_ref, w_ref, b_ref, da_ref, dg_ref, dup_ref, dw_ref, db_ref):
        g = g_ref[...].astype(F32)
        up = up_ref[...].astype(F32)
        dact = da_ref[...].astype(F32)
        rows = lax.broadcasted_iota(jnp.int32, g.shape, 0)
        w = w_ref[...]
        g1 = _shift_down(g, 1, rows)
        g2 = _shift_down(g, 2, rows)
        a = w[2:3] * g + w[1:2] * g1 + w[0:1] * g2 + b_ref[...]
        sg = _sigmoid(a)
        dup_ref[...] = (dact * a * sg).astype(BF16)
        da = dact * up * sg * (1.0 + a * (1.0 - sg))
        dg = w[2:3] * da + w[1:2] * _shift_up(da, 1, rows, S) + w[0:1] * _shift_up(da, 2, rows, S)
        dg_ref[...] = dg.astype(BF16)

        @pl.when(pl.program_id(1) == 0)
        def _():
            dw_ref[...] = jnp.zeros_like(dw_ref)
            db_ref[...] = jnp.zeros_like(db_ref)

        dw_ref[0:1, :] += jnp.sum(da * g2, axis=0, keepdims=True)
        dw_ref[1:2, :] += jnp.sum(da * g1, axis=0, keepdims=True)
        dw_ref[2:3, :] += jnp.sum(da * g, axis=0, keepdims=True)
        db_ref[...] += jnp.sum(da, axis=0, keepdims=True)

    col = lambda j, b: (b, 0, j)
    return pl.pallas_call(
        body, name=name, grid=(nc, B),
        in_specs=[pl.BlockSpec((None, S, tc), col),
                  pl.BlockSpec((None, S, tc), lambda j, b: (b, 0, j + nc)),
                  pl.BlockSpec((3, tc), lambda j, b: (0, j)),
                  pl.BlockSpec((1, tc), lambda j, b: (0, j)),
                  pl.BlockSpec((None, S, tc), col)],
        out_specs=[pl.BlockSpec((None, S, tc), col), pl.BlockSpec((None, S, tc), col),
                   pl.BlockSpec((3, tc), lambda j, b: (0, j)), pl.BlockSpec((1, tc), lambda j, b: (0, j))],
        out_shape=[jax.ShapeDtypeStruct((B, S, D_FF), BF16), jax.ShapeDtypeStruct((B, S, D_FF), BF16),
                   jax.ShapeDtypeStruct((3, D_FF), F32), jax.ShapeDtypeStruct((1, D_FF), F32)],
        compiler_params=_params("parallel", "arbitrary"),
    )(gu, gu, conv_w, conv_b, dact)


HGRN_CPB = 4
HF = HGRN_HEADS * HGRN_DK


def _tri(n, upper=False):
    r = lax.broadcasted_iota(jnp.int32, (n, n), 0)
    c = lax.broadcasted_iota(jnp.int32, (n, n), 1)
    return (c >= r) if upper else (r >= c)


def _hgrn_post(o, hg, gn):
    return _rms(o, gn) * _silu(hg)


def _hgrn_chunk_fwd(q, fz, lbh, tril_f32):
    qf = _silu(q)
    sg = _sigmoid(fz)
    f = lbh + (1.0 - lbh) * sg
    k = 1.0 - f
    b = lax.dot_general(tril_f32, jnp.log(f), (((1,), (0,)), ((), ())), precision=lax.Precision.HIGHEST, preferred_element_type=F32)
    bref = b[CHUNK // 2:CHUNK // 2 + 1, :]
    blast = b[CHUNK - 1:CHUNK, :]
    e1 = jnp.exp(b - bref)
    e2 = jnp.exp(bref - b)
    e3 = jnp.exp(b)
    e4 = jnp.exp(blast - b)
    dec = jnp.exp(blast)
    return sg, f, (e1, e2, e3, e4), qf * e1, k * e2, qf * e3, k * e4, dec


def _hgrn_fwd(zh, lb, gn, *, name):
    B, S, _ = zh.shape
    cpb = HGRN_CPB
    ts = cpb * CHUNK
    nblk = S // ts

    def body(z_ref, lb_ref, gn_ref, o_ref, st_ref, state):
        @pl.when(pl.program_id(1) == 0)
        def _():
            state[...] = jnp.zeros_like(state)

        causal = _tri(CHUNK)
        tril = causal.astype(F32)
        for c in range(cpb):
            rows = slice(c * CHUNK, (c + 1) * CHUNK)
            for h in range(HGRN_HEADS):
                col = lambda seg: slice(seg * HF + h * HGRN_DK, seg * HF + (h + 1) * HGRN_DK)
                q = z_ref[rows, col(0)].astype(F32)
                fz = z_ref[rows, col(1)].astype(F32)
                v = z_ref[rows, col(2)]
                hg = z_ref[rows, col(3)].astype(F32)
                lbh = lb_ref[:, col(0)]
                _, _, _, q_in, k_in, q_out, k_st, dec = _hgrn_chunk_fwd(q, fz, lbh, tril)
                a = jnp.where(causal, _nt(q_in.astype(BF16), k_in.astype(BF16)), 0.0)
                st = state[h]
                st_ref[c, h] = st
                o = _nn(a.astype(BF16), v) + _nt(q_out.astype(BF16), st.astype(BF16))
                state[h] = st * dec + _tn(v, k_st.astype(BF16))
                o_ref[rows, col(0)] = _hgrn_post(o, hg, gn_ref[...]).astype(BF16)

    return pl.pallas_call(
        body, name=name, grid=(B, nblk),
        in_specs=[pl.BlockSpec((None, ts, 4 * HF), lambda b, s: (b, s, 0)),
                  pl.BlockSpec((1, HF), lambda b, s: (0, 0)),
                  pl.BlockSpec((1, HGRN_DK), lambda b, s: (0, 0))],
        out_specs=[pl.BlockSpec((None, ts, HF), lambda b, s: (b, s, 0)),
                   pl.BlockSpec((None, cpb, HGRN_HEADS, HGRN_DK, HGRN_DK), lambda b, s: (b, s, 0, 0, 0))],
        out_shape=[jax.ShapeDtypeStruct((B, S, HF), BF16),
                   jax.ShapeDtypeStruct((B, S // CHUNK, HGRN_HEADS, HGRN_DK, HGRN_DK), F32)],
        scratch_shapes=[pltpu.VMEM((HGRN_HEADS, HGRN_DK, HGRN_DK), F32)],
        compiler_params=_params("arbitrary", "arbitrary"),
    )(zh, lb, gn)


def _hgrn_bwd(zh, lb, gn, states, doa, *, name):
    B, S, _ = zh.shape
    cpb = HGRN_CPB
    ts = cpb * CHUNK
    nblk = S // ts
    rev = lambda b, s: (b, nblk - 1 - s, 0)

    def body(z_ref, lb_ref, gn_ref, st_ref, do_ref, dz_ref, dlb_ref, dgn_ref, dstate):
        @pl.when(pl.program_id(1) == 0)
        def _():
            dstate[...] = jnp.zeros_like(dstate)

        @pl.when((pl.program_id(0) == 0) & (pl.program_id(1) == 0))
        def _():
            dlb_ref[...] = jnp.zeros_like(dlb_ref)
            dgn_ref[...] = jnp.zeros_like(dgn_ref)

        causal = _tri(CHUNK)
        tril = causal.astype(F32)
        triu = _tri(CHUNK, upper=True).astype(F32)
        rowid = lax.broadcasted_iota(jnp.int32, (CHUNK, HGRN_DK), 0)
        for c in reversed(range(cpb)):
            rows = slice(c * CHUNK, (c + 1) * CHUNK)
            for h in range(HGRN_HEADS):
                col = lambda seg: slice(seg * HF + h * HGRN_DK, seg * HF + (h + 1) * HGRN_DK)
                q = z_ref[rows, col(0)].astype(F32)
                fz = z_ref[rows, col(1)].astype(F32)
                v = z_ref[rows, col(2)]
                hg = z_ref[rows, col(3)].astype(F32)
                lbh = lb_ref[:, col(0)]
                sg, f, (e1, e2, e3, e4), q_in, k_in, q_out, k_st, dec = _hgrn_chunk_fwd(q, fz, lbh, tril)
                q_in_b, k_in_b, q_out_b, k_st_b = (t.astype(BF16) for t in (q_in, k_in, q_out, k_st))
                a = jnp.where(causal, _nt(q_in_b, k_in_b), 0.0)
                a_b = a.astype(BF16)
                st = st_ref[c, h]
                st_b = st.astype(BF16)
                o = _nn(a_b, v) + _nt(q_out_b, st_b)
                _, post_vjp = jax.vjp(_hgrn_post, o, hg, gn_ref[...])
                do, dhg, dgn = post_vjp(do_ref[rows, col(0)].astype(F32))
                dgn_ref[...] += dgn
                do_b = do.astype(BF16)
                dst = dstate[h]
                dst_b = dst.astype(BF16)
                da_b = jnp.where(causal, _nt(do_b, v), 0.0).astype(BF16)
                dv = _tn(a_b, do_b) + _nt(k_st_b, dst_b)
                dq_in = _nn(da_b, k_in_b)
                dk_in = _tn(da_b, q_in_b)
                dq_out = _nn(do_b, st_b)
                dk_st = _nn(v, dst_b)
                ddec = jnp.sum(st * dst, axis=0, keepdims=True)
                dstate[h] = dst * dec + _tn(do_b, q_out_b)
                t_qin = dq_in * q_in
                t_kin = dk_in * k_in
                t_kst = dk_st * k_st
                db = t_qin - t_kin + dq_out * q_out - t_kst
                dbref = jnp.sum(t_kin - t_qin, axis=0, keepdims=True)
                dblast = jnp.sum(t_kst, axis=0, keepdims=True) + ddec * dec
                db = db + jnp.where(rowid == CHUNK // 2, dbref, 0.0) + jnp.where(rowid == CHUNK - 1, dblast, 0.0)
                dlogf = lax.dot_general(triu, db, (((1,), (0,)), ((), ())), precision=lax.Precision.HIGHEST, preferred_element_type=F32)
                dqf = dq_in * e1 + dq_out * e3
                dk = dk_in * e2 + dk_st * e4
                df = dlogf / f - dk
                dfz = df * (1.0 - lbh) * sg * (1.0 - sg)
                dlb_ref[:, col(0)] += jnp.sum(df * (1.0 - sg), axis=0, keepdims=True)
                sq = _sigmoid(q)
                dq = dqf * sq * (1.0 + q * (1.0 - sq))
                dz_ref[rows, col(0)] = dq.astype(BF16)
                dz_ref[rows, col(1)] = dfz.astype(BF16)
                dz_ref[rows, col(2)] = dv.astype(BF16)
                dz_ref[rows, col(3)] = dhg.astype(BF16)

    return pl.pallas_call(
        body, name=name, grid=(B, nblk),
        in_specs=[pl.BlockSpec((None, ts, 4 * HF), rev),
                  pl.BlockSpec((1, HF), lambda b, s: (0, 0)),
                  pl.BlockSpec((1, HGRN_DK), lambda b, s: (0, 0)),
                  pl.BlockSpec((None, cpb, HGRN_HEADS, HGRN_DK, HGRN_DK), lambda b, s: (b, nblk - 1 - s, 0, 0, 0)),
                  pl.BlockSpec((None, ts, HF), rev)],
        out_specs=[pl.BlockSpec((None, ts, 4 * HF), rev),
                   pl.BlockSpec((1, HF), lambda b, s: (0, 0)),
                   pl.BlockSpec((1, HGRN_DK), lambda b, s: (0, 0))],
        out_shape=[jax.ShapeDtypeStruct((B, S, 4 * HF), BF16),
                   jax.ShapeDtypeStruct((1, HF), F32),
                   jax.ShapeDtypeStruct((1, HGRN_DK), F32)],
        scratch_shapes=[pltpu.VMEM((HGRN_HEADS, HGRN_DK, HGRN_DK), F32)],
        compiler_params=_params("arbitrary", "arbitrary"),
    )(zh, lb, gn, states, doa)


KV_W = ATT_KV_HEADS * ATT_HD
ATT_SCALE = ATT_HD ** -0.5


def _rope(x, cos, sin, inverse=False):
    half = ROPE_DIM // 2
    outs = []
    for p in range(x.shape[1] // 128):
        xp = x[:, p * 128:(p + 1) * 128]
        lane = lax.broadcasted_iota(jnp.int32, xp.shape, 1) % ATT_HD
        sw = jnp.where(lane < half, pltpu.roll(xp, 128 - half, 1), pltpu.roll(xp, half, 1))
        outs.append(xp * cos - sw * sin if inverse else xp * cos + sw * sin)
    return outs[0] if len(outs) == 1 else jnp.concatenate(outs, axis=1)


def _swa_mask(first_block):
    qi = lax.broadcasted_iota(jnp.int32, (WINDOW, 2 * WINDOW), 0)
    mi = lax.broadcasted_iota(jnp.int32, (WINDOW, 2 * WINDOW), 1)
    band = (mi > qi) & (mi <= qi + WINDOW)
    return band & (jnp.logical_not(first_block) | (mi >= WINDOW))


def _swa_specs(nb):
    cur = lambda b, i: (b, i, 0)
    prev = lambda b, i: (b, jnp.maximum(i - 1, 0), 0)
    return cur, prev


def _swa_fwd(aq, akv, cos, sin, sinks, *, name):
    B, S, _ = aq.shape
    nb = S // WINDOW
    cur, prev = _swa_specs(nb)

    def body(q_ref, kvp_ref, kvc_ref, cp_ref, sp_ref, cc_ref, sc_ref, sink_ref, o_ref, lse_ref):
        cos_c, sin_c = cc_ref[...], sc_ref[...]
        q = _rope(q_ref[...].astype(F32), cos_c, sin_c).astype(BF16)
        k = jnp.concatenate([_rope(kvp_ref[:, :KV_W].astype(F32), cp_ref[...], sp_ref[...]),
                             _rope(kvc_ref[:, :KV_W].astype(F32), cos_c, sin_c)], axis=0).astype(BF16)
        v = jnp.concatenate([kvp_ref[:, KV_W:], kvc_ref[:, KV_W:]], axis=0)
        mask = _swa_mask(pl.program_id(1) == 0)
        for h in range(ATT_HEADS):
            g = h // ATT_GROUP
            qh = q[:, h * ATT_HD:(h + 1) * ATT_HD]
            kg = k[:, g * ATT_HD:(g + 1) * ATT_HD]
            vg = v[:, g * ATT_HD:(g + 1) * ATT_HD]
            s = jnp.where(mask, _nt(qh, kg) * ATT_SCALE, NEG_INF)
            sink = sink_ref[0, h]
            m = jnp.maximum(jnp.max(s, axis=1, keepdims=True), sink)
            p = jnp.exp(s - m)
            den = jnp.sum(p, axis=1, keepdims=True) + jnp.exp(sink - m)
            o = _nn((p / den).astype(BF16), vg)
            o_ref[:, h * ATT_HD:(h + 1) * ATT_HD] = o.astype(BF16)
            lse_ref[:, h:h + 1] = m + jnp.log(den)

    tab = lambda im: pl.BlockSpec((None, WINDOW, 128), im)
    return pl.pallas_call(
        body, name=name, grid=(B, nb),
        in_specs=[pl.BlockSpec((None, WINDOW, D_MODEL), cur),
                  pl.BlockSpec((None, WINDOW, 2 * KV_W), prev), pl.BlockSpec((None, WINDOW, 2 * KV_W), cur),
                  tab(prev), tab(prev), tab(cur), tab(cur),
                  pl.BlockSpec(memory_space=pltpu.SMEM)],
        out_specs=[pl.BlockSpec((None, WINDOW, D_MODEL), cur), pl.BlockSpec((None, WINDOW, ATT_HEADS), cur)],
        out_shape=[jax.ShapeDtypeStruct((B, S, D_MODEL), BF16), jax.ShapeDtypeStruct((B, S, ATT_HEADS), F32)],
        compiler_params=_params("parallel", "parallel"),
    )(aq, akv, akv, cos, sin, cos, sin, sinks)


def _swa_bwd(aq, akv, cos, sin, sinks, lse, dob, *, name):
    B, S, _ = aq.shape
    nb = S // WINDOW
    cur, prev = _swa_specs(nb)

    def body(q_ref, kvp_ref, kvc_ref, cp_ref, sp_ref, cc_ref, sc_ref, sink_ref, lse_ref, do_ref,
             dq_ref, dkc_ref, dkp_ref, dsink_ref):
        @pl.when((pl.program_id(0) == 0) & (pl.program_id(1) == 0))
        def _():
            dsink_ref[...] = jnp.zeros_like(dsink_ref)

        cos_c, sin_c, cos_p, sin_p = cc_ref[...], sc_ref[...], cp_ref[...], sp_ref[...]
        q = _rope(q_ref[...].astype(F32), cos_c, sin_c).astype(BF16)
        k = jnp.concatenate([_rope(kvp_ref[:, :KV_W].astype(F32), cos_p, sin_p),
                             _rope(kvc_ref[:, :KV_W].astype(F32), cos_c, sin_c)], axis=0).astype(BF16)
        v = jnp.concatenate([kvp_ref[:, KV_W:], kvc_ref[:, KV_W:]], axis=0)
        mask = _swa_mask(pl.program_id(1) == 0)
        dqs = []
        dks = []
        dvs = []
        for g in range(ATT_KV_HEADS):
            kg = k[:, g * ATT_HD:(g + 1) * ATT_HD]
            vg = v[:, g * ATT_HD:(g + 1) * ATT_HD]
            dk = jnp.zeros((2 * WINDOW, ATT_HD), F32)
            dv = jnp.zeros((2 * WINDOW, ATT_HD), F32)
            for h in range(g * ATT_GROUP, (g + 1) * ATT_GROUP):
                qh = q[:, h * ATT_HD:(h + 1) * ATT_HD]
                doh = do_ref[:, h * ATT_HD:(h + 1) * ATT_HD]
                lse_h = lse_ref[:, h:h + 1]
                s = jnp.where(mask, _nt(qh, kg) * ATT_SCALE, NEG_INF)
                p = jnp.exp(s - lse_h)
                dp = _nt(doh, vg)
                delta = jnp.sum(p * dp, axis=1, keepdims=True)
                ds = (p * (dp - delta) * ATT_SCALE).astype(BF16)
                p_sink = jnp.exp(sink_ref[0, h] - lse_h)
                dsink_ref[h:h + 1, :] += jnp.broadcast_to(-jnp.sum(p_sink * delta, axis=0, keepdims=True), (1, 128))
                dqs.append(_nn(ds, kg))
                dk = dk + _tn(ds, qh)
                dv = dv + _tn(p.astype(BF16), doh)
            dks.append(dk)
            dvs.append(dv)
        dq_ref[...] = _rope(jnp.concatenate(dqs, axis=1), cos_c, sin_c, inverse=True).astype(BF16)
        dk = jnp.concatenate(dks, axis=1)
        dv = jnp.concatenate(dvs, axis=1)
        dkp_ref[:, :KV_W] = _rope(dk[:WINDOW], cos_p, sin_p, inverse=True)
        dkp_ref[:, KV_W:] = dv[:WINDOW]
        dkc_ref[:, :KV_W] = _rope(dk[WINDOW:], cos_c, sin_c, inverse=True)
        dkc_ref[:, KV_W:] = dv[WINDOW:]

    tab = lambda im: pl.BlockSpec((None, WINDOW, 128), im)
    return pl.pallas_call(
        body, name=name, grid=(B, nb),
        in_specs=[pl.BlockSpec((None, WINDOW, D_MODEL), cur),
                  pl.BlockSpec((None, WINDOW, 2 * KV_W), prev), pl.BlockSpec((None, WINDOW, 2 * KV_W), cur),
                  tab(prev), tab(prev), tab(cur), tab(cur),
                  pl.BlockSpec(memory_space=pltpu.SMEM),
                  pl.BlockSpec((None, WINDOW, ATT_HEADS), cur),
                  pl.BlockSpec((None, WINDOW, D_MODEL), cur)],
        out_specs=[pl.BlockSpec((None, WINDOW, D_MODEL), cur),
                   pl.BlockSpec((None, WINDOW, 2 * KV_W), cur), pl.BlockSpec((None, WINDOW, 2 * KV_W), cur),
                   pl.BlockSpec((ATT_HEADS, 128), lambda b, i: (0, 0))],
        out_shape=[jax.ShapeDtypeStruct((B, S, D_MODEL), BF16),
                   jax.ShapeDtypeStruct((B, S, 2 * KV_W), F32), jax.ShapeDtypeStruct((B, S, 2 * KV_W), F32),
                   jax.ShapeDtypeStruct((ATT_HEADS, 128), F32)],
        compiler_params=_params("arbitrary", "arbitrary"),
    )(aq, akv, akv, cos, sin, cos, sin, sinks, lse, dob)


def _swa_dkv_combine(dkv_cur, dkv_prev, *, name):
    B, S, W = dkv_cur.shape
    nb = S // WINDOW

    def body(c_ref, p_ref, o_ref):
        nxt = jnp.where(pl.program_id(1) < nb - 1, p_ref[...], 0.0)
        o_ref[...] = (c_ref[...] + nxt).astype(BF16)

    cur = lambda b, j: (b, j, 0)
    return pl.pallas_call(
        body, name=name, grid=(B, nb),
        in_specs=[pl.BlockSpec((None, WINDOW, W), cur),
                  pl.BlockSpec((None, WINDOW, W), lambda b, j: (b, jnp.minimum(j + 1, nb - 1), 0))],
        out_specs=pl.BlockSpec((None, WINDOW, W), cur),
        out_shape=jax.ShapeDtypeStruct((B, S, W), BF16),
        compiler_params=_params("parallel", "parallel"),
    )(dkv_cur, dkv_prev)


def _rope_tables(positions):
    half = ROPE_DIM // 2
    inv = ROPE_THETA ** (-2.0 * jnp.arange(half, dtype=F32) / ROPE_DIM)
    ang = positions.astype(F32)[..., None] * inv
    c, s = jnp.cos(ang), jnp.sin(ang)
    pad = jnp.zeros(ang.shape[:-1] + (ATT_HD - ROPE_DIM,), F32)
    cos = jnp.concatenate([c, c, pad + 1.0], axis=-1)
    sin = jnp.concatenate([-s, s, pad], axis=-1)
    return jnp.tile(cos, (1, 1, 2)), jnp.tile(sin, (1, 1, 2))


def _lower_bound(lb_logits, *, name):
    def body(l_ref, o_ref):
        l = l_ref[...]
        e = jnp.exp(l - jnp.max(l, axis=0, keepdims=True))
        o_ref[...] = e[0:1] / jnp.sum(e, axis=0, keepdims=True)

    return pl.pallas_call(body, name=name, out_shape=jax.ShapeDtypeStruct((1, lb_logits.shape[1]), F32))(lb_logits)


def _local_step(x, positions, target, small, W):
    B, S, D = x.shape
    T = B * S
    x2 = x.reshape(T, D)
    cos, sin = _rope_tables(positions)
    lb = _lower_bound(small["lb_logits"], name="lb_fwd")

    u1 = _norm_cast(x2, small["norm1_g"], name="norm1")
    zh = _matmul(u1, W["w_h"], out_dtype=BF16, name="mm_zh")
    aq = _matmul(u1, W["w_aq"], out_dtype=BF16, name="mm_aq")
    akv = _matmul(u1, W["w_akv"], out_dtype=BF16, name="mm_akv")
    gates = _matmul(u1, W["w_g"], out_dtype=BF16, name="mm_gates")
    zh3 = zh.reshape(B, S, 4 * HF)
    aq3 = aq.reshape(B, S, D)
    akv3 = akv.reshape(B, S, 2 * KV_W)
    oa, states = _hgrn_fwd(zh3, lb, small["hgrn_norm_g"], name="hgrn_fwd")
    ob, lse = _swa_fwd(aq3, akv3, cos, sin, small["attn_sinks"], name="swa_fwd")
    oa2 = oa.reshape(T, D)
    ob2 = ob.reshape(T, D)
    pa = _matmul(oa2, W["w_a"], out_dtype=BF16, name="mm_pa")
    pb = _matmul(ob2, W["w_b"], out_dtype=BF16, name="mm_pb")
    merged = _merge_fwd(gates, pa, pb, name="merge_fwd")
    h = _matmul(merged, W["w_out"], addend=x2, name="mm_h")
    u2 = _norm_cast(h, small["norm2_g"], name="norm2")
    gu = _matmul(u2, W["w_ffn"], out_dtype=BF16, name="mm_gu")
    gu3 = gu.reshape(B, S, 2 * D_FF)
    act = _conv_act_fwd(gu3, W["conv_w"], small["conv_b"], name="conv_act_fwd")
    act2 = act.reshape(T, D_FF)
    h2 = _matmul(act2, W["w_down"], addend=h, name="mm_h2")

    g = {}
    dh2, dh2b, g["final_g"], loss = _final_loss_bwd(h2, small["final_g"].reshape(1, D), target.reshape(T, D), name="final_loss_bwd")
    dact = _matmul(dh2b, W["w_down"], tb=True, out_dtype=BF16, name="mm_dact")
    g["w_down"] = _matmul(act2, dh2b, ta=True, name="mm_dw_down")
    dg_, dup, g["conv_w"], g["conv_b"] = _conv_act_bwd(gu3, W["conv_w"], small["conv_b"], dact.reshape(B, S, D_FF), name="conv_act_bwd")
    dg2 = dg_.reshape(T, D_FF)
    dup2 = dup.reshape(T, D_FF)
    w_fg, w_fu = W["w_ffn"][:, :D_FF], W["w_ffn"][:, D_FF:]
    du2 = _matmul(dg2, w_fg, tb=True, name="mm_du2_g")
    du2 = _matmul(dup2, w_fu, tb=True, addend=du2, name="mm_du2_u")
    g["w_ffn_g"] = _matmul(u2, dg2, ta=True, name="mm_dw_ffn_g")
    g["w_ffn_u"] = _matmul(u2, dup2, ta=True, name="mm_dw_ffn_u")
    dh, dhb, g["norm2_g"] = _norm_bwd_add(h, small["norm2_g"], du2, dh2, name="norm2_bwd")
    dmerged = _matmul(dhb, W["w_out"], tb=True, out_dtype=BF16, name="mm_dmerged")
    g["w_out"] = _matmul(merged, dhb, ta=True, name="mm_dw_out")
    dgates, dpa, dpb = _merge_bwd(gates, pa, pb, dmerged, name="merge_bwd")
    doa = _matmul(dpa, W["w_a"], tb=True, out_dtype=BF16, name="mm_doa")
    g["w_a"] = _matmul(oa2, dpa, ta=True, name="mm_dw_a")
    dob = _matmul(dpb, W["w_b"], tb=True, out_dtype=BF16, name="mm_dob")
    g["w_b"] = _matmul(ob2, dpb, ta=True, name="mm_dw_b")
    daq, dkv_cur, dkv_prev, dsinks = _swa_bwd(aq3, akv3, cos, sin, small["attn_sinks"], lse, dob.reshape(B, S, D), name="swa_bwd")
    dakv = _swa_dkv_combine(dkv_cur, dkv_prev, name="swa_dkv").reshape(T, 2 * KV_W)
    daq2 = daq.reshape(T, D)
    g["attn_sinks"] = dsinks
    dzh, g["lb"], g["hgrn_norm_g"] = _hgrn_bwd(zh3, lb, small["hgrn_norm_g"], states, doa.reshape(B, S, D), name="hgrn_bwd")
    dzh2 = dzh.reshape(T, 4 * HF)
    du1 = _matmul(dzh2, W["w_h"], tb=True, name="mm_du1_h")
    du1 = _matmul(daq2, W["w_aq"], tb=True, addend=du1, name="mm_du1_aq")
    du1 = _matmul(dakv, W["w_akv"], tb=True, addend=du1, name="mm_du1_akv")
    du1 = _matmul(dgates, W["w_g"], tb=True, addend=du1, name="mm_du1_g")
    g["w_h"] = _matmul(u1, dzh2, ta=True, name="mm_dw_h")
    g["w_aq"] = _matmul(u1, daq2, ta=True, name="mm_dw_aq")
    g["w_akv"] = _matmul(u1, dakv, ta=True, name="mm_dw_akv")
    g["w_g"] = _matmul(u1, dgates, ta=True, name="mm_dw_g")
    dx, _, g["norm1_g"] = _norm_bwd_add(x2, small["norm1_g"], du1, dh, name="norm1_bwd")
    g["lb_fwd"] = lb
    return loss, dx.reshape(B, S, D), g


def _my_place():
    return lax.axis_index("x"), lax.axis_index("y"), lax.axis_index("c")


def _all_gather(blk, *, in_vmem, reduce_sum=False, name):
    m, n = blk.shape
    space = pltpu.VMEM if in_vmem else pl.ANY

    def body(x_ref, out_ref, *rest):
        if reduce_sum:
            tot_ref, send_sems, recv_sems, local_sem = rest
        else:
            send_sems, recv_sems, local_sem = rest
        x, y, c = _my_place()
        me, sibling = (x, y, c), (x, y, 1 - c)
        chips = [(1 - x, y), (x, 1 - y), (1 - x, 1 - y)]

        def slot(px, py, pc):
            return out_ref.at[4 * px + 2 * py + pc]

        def copy(k, block, to, src=None):
            return pltpu.make_async_remote_copy(
                src_ref=slot(*block) if src is None else src, dst_ref=slot(*block),
                send_sem=send_sems.at[k], recv_sem=recv_sems.at[k], device_id=to, device_id_type=MESH)

        mine = pltpu.make_async_copy(x_ref, slot(*me), local_sem)
        mine.start()
        first = [copy(0, me, sibling, src=x_ref)]
        first += [copy(1 + j, me, (*chip, c), src=x_ref) for j, chip in enumerate(chips)]
        for cp in first:
            cp.start()
        passed = [copy(4 + j, (*chip, c), sibling) for j, chip in enumerate(chips)]
        for j, chip in enumerate(chips):
            copy(1 + j, (*chip, c), me).wait_recv()
            passed[j].start()
        copy(0, sibling, me).wait_recv()
        for j, chip in enumerate(chips):
            copy(4 + j, (*chip, 1 - c), me).wait_recv()
        for cp in first + passed:
            cp.wait_send()
        mine.wait()
        if reduce_sum:
            acc = out_ref[0]
            for p in range(1, N_DEV):
                acc = acc + out_ref[p]
            tot_ref[...] = acc

    out_shape = [jax.ShapeDtypeStruct((N_DEV, m, n), blk.dtype)]
    out_specs = [pl.BlockSpec(memory_space=space)]
    if reduce_sum:
        out_shape.append(jax.ShapeDtypeStruct((m, n), blk.dtype))
        out_specs.append(pl.BlockSpec(memory_space=pltpu.VMEM))
    res = pl.pallas_call(
        body, name=name,
        out_shape=out_shape,
        in_specs=[pl.BlockSpec(memory_space=space)],
        out_specs=out_specs,
        scratch_shapes=[pltpu.SemaphoreType.DMA((7,)), pltpu.SemaphoreType.DMA((7,)), pltpu.SemaphoreType.DMA],
    )(blk)
    return res if reduce_sum else res[0]


def _rs_sibling(g5, *, name):
    _, _, R, n = g5.shape

    def body(g_ref, r_ref, send_sems, recv_sems):
        x, y, c = _my_place()
        copies = [pltpu.make_async_remote_copy(
            src_ref=g_ref.at[k, 1 - c], dst_ref=r_ref.at[k], send_sem=send_sems.at[k], recv_sem=recv_sems.at[k],
            device_id=(x, y, 1 - c), device_id_type=MESH) for k in range(4)]
        for cp in copies:
            cp.start()
        for cp in copies:
            cp.wait_recv()
        for cp in copies:
            cp.wait_send()

    return pl.pallas_call(
        body, name=name,
        out_shape=jax.ShapeDtypeStruct((4, R, n), g5.dtype),
        in_specs=[ANY], out_specs=ANY,
        scratch_shapes=[pltpu.SemaphoreType.DMA((4,)), pltpu.SemaphoreType.DMA((4,))],
    )(g5)


def _rs_chips(p1, *, name):
    _, R, n = p1.shape

    def body(p_ref, r_ref, send_sems, recv_sems):
        x, y, c = _my_place()
        chips = [(1 - x, y), (x, 1 - y), (1 - x, 1 - y)]
        copies = [pltpu.make_async_remote_copy(
            src_ref=p_ref.at[2 * cx + cy], dst_ref=r_ref.at[j], send_sem=send_sems.at[j], recv_sem=recv_sems.at[j],
            device_id=(cx, cy, c), device_id_type=MESH) for j, (cx, cy) in enumerate(chips)]
        for cp in copies:
            cp.start()
        for cp in copies:
            cp.wait_recv()
        for cp in copies:
            cp.wait_send()

    return pl.pallas_call(
        body, name=name,
        out_shape=jax.ShapeDtypeStruct((3, R, n), p1.dtype),
        in_specs=[ANY], out_specs=ANY,
        scratch_shapes=[pltpu.SemaphoreType.DMA((3,)), pltpu.SemaphoreType.DMA((3,))],
    )(p1)


def _add_sibling(g5, r1, core, *, name):
    _, _, R, n = g5.shape
    tr = _pick(R, (296, 128, 64, 8))

    def body(c_ref, g_ref, r_ref, o_ref):
        o_ref[...] = g_ref[...] + r_ref[...]

    return pl.pallas_call(
        body, name=name,
        grid_spec=pltpu.PrefetchScalarGridSpec(
            num_scalar_prefetch=1, grid=(4, R // tr),
            in_specs=[pl.BlockSpec((None, None, tr, n), lambda k, i, c: (k, c[0], i, 0)),
                      pl.BlockSpec((None, tr, n), lambda k, i, c: (k, i, 0))],
            out_specs=pl.BlockSpec((None, tr, n), lambda k, i, c: (k, i, 0))),
        out_shape=jax.ShapeDtypeStruct((4, R, n), F32),
        compiler_params=_params("parallel", "parallel"),
    )(core, g5, r1)


def _adamw_math(w, g, m, v):
    m = ADAM_B1 * m + (1.0 - ADAM_B1) * g
    v = ADAM_B2 * v + (1.0 - ADAM_B2) * (g * g)
    m_hat = m / (1.0 - ADAM_B1 ** ADAM_STEP)
    v_hat = v / (1.0 - ADAM_B2 ** ADAM_STEP)
    delta = -ADAM_LR * (m_hat / (jnp.sqrt(v_hat) + ADAM_EPS) + ADAM_WD * w)
    return delta, m, v


def _adamw_big(p1, r2, chip, w, m, v, *, name):
    R, n = w.shape
    tr = _pick(R, (296, 128, 64, 8))

    def body(c_ref, p_ref, r_ref, w_ref, m_ref, v_ref, g_ref, d_ref, mo_ref, vo_ref):
        g = p_ref[...] + r_ref[0] + r_ref[1] + r_ref[2]
        d, mn, vn = _adamw_math(w_ref[...], g, m_ref[...], v_ref[...])
        g_ref[...] = g
        d_ref[...] = d
        mo_ref[...] = mn
        vo_ref[...] = vn

    row = pl.BlockSpec((tr, n), lambda i, c: (i, 0))
    return pl.pallas_call(
        body, name=name,
        grid_spec=pltpu.PrefetchScalarGridSpec(
            num_scalar_prefetch=1, grid=(R // tr,),
            in_specs=[pl.BlockSpec((None, tr, n), lambda i, c: (c[0], i, 0)),
                      pl.BlockSpec((3, tr, n), lambda i, c: (0, i, 0)), row, row, row],
            out_specs=[row, row, row, row]),
        out_shape=[jax.ShapeDtypeStruct((R, n), F32)] * 4,
        compiler_params=_params("parallel"),
    )(chip, p1, r2, w, m, v)


def _adamw_small(g, w, m, v, *, name):
    def body(g_ref, w_ref, m_ref, v_ref, d_ref, mo_ref, vo_ref):
        d, mn, vn = _adamw_math(w_ref[...], g_ref[...], m_ref[...], v_ref[...])
        d_ref[...] = d
        mo_ref[...] = mn
        vo_ref[...] = vn

    return pl.pallas_call(body, name=name, out_shape=[jax.ShapeDtypeStruct(w.shape, F32)] * 3)(g, w, m, v)


def _lb_bwd(dlb, lb, *, name):
    def body(d_ref, lb_ref, o_ref):
        t = d_ref[...] * lb_ref[...] * (1.0 - lb_ref[...])
        o_ref[0:1, :] = t
        o_ref[1:2, :] = -t

    return pl.pallas_call(body, name=name, out_shape=jax.ShapeDtypeStruct((2, lb.shape[1]), F32))(dlb, lb)


LANES = 128
N_IN, N_FFN = 7424, 5632
IN_BLK, FFN_BLK, DOWN_BLK, ROW_BLK = N_IN // N_DEV, N_FFN // N_DEV, D_FF // N_DEV, D_MODEL // N_DEV
CONVW_BLK = D_FF // N_DEV
PACK_ROWS = IN_BLK + 3 * ROW_BLK + FFN_BLK + DOWN_BLK
SMALL_NAMES = ("norm1_g", "lb_logits", "hgrn_norm_g", "attn_sinks", "norm2_g", "conv_b", "final_g")


def _pack_shards(w_in, w_a, w_b, w_out, w_ffn_in, w_down):
    return jnp.concatenate([w_in.reshape(IN_BLK, D_MODEL), w_a.reshape(ROW_BLK, D_MODEL), w_b.reshape(ROW_BLK, D_MODEL),
                            w_out.reshape(ROW_BLK, D_MODEL), w_ffn_in.reshape(FFN_BLK, D_MODEL), w_down.reshape(DOWN_BLK, D_MODEL)], axis=0)


def _unpack_shards(p):
    o = [0, IN_BLK, IN_BLK + ROW_BLK, IN_BLK + 2 * ROW_BLK, IN_BLK + 3 * ROW_BLK, IN_BLK + 3 * ROW_BLK + FFN_BLK, PACK_ROWS]
    return (p[o[0]:o[1]].reshape(1, D_MODEL, IN_BLK), p[o[1]:o[2]].reshape(1, ROW_BLK, D_MODEL), p[o[2]:o[3]].reshape(1, ROW_BLK, D_MODEL),
            p[o[3]:o[4]].reshape(1, ROW_BLK, D_MODEL), p[o[4]:o[5]].reshape(1, D_MODEL, FFN_BLK), p[o[5]:o[6]].reshape(1, DOWN_BLK, D_MODEL))


def _cols_from_blocks(gathered, lo, width):
    rows = width * D_MODEL // D_MODEL
    blk = gathered[:, lo:lo + rows].reshape(N_DEV, D_MODEL, width)
    return blk.transpose(1, 0, 2).reshape(D_MODEL, N_DEV * width)


def _blocks_from_cols(full, width):
    return full.reshape(D_MODEL, N_DEV, width).transpose(1, 0, 2).reshape(N_DEV, width, D_MODEL)


def _to_rows(vec, rows):
    vec = vec.reshape(-1)
    return jnp.pad(vec, (0, rows * LANES - vec.shape[0])).reshape(rows, LANES)


def kernel(x, positions, norm1_g, w_in, lb_logits, hgrn_norm_g, w_a, attn_sinks, w_b, w_out, norm2_g, w_ffn_in, conv_w, conv_b, w_down, final_g, loss_target, m_norm1_g, m_w_in, m_lb_logits, m_hgrn_norm_g, m_w_a, m_attn_sinks, m_w_b, m_w_out, m_norm2_g, m_w_ffn_in, m_conv_w, m_conv_b, m_w_down, m_final_g, v_norm1_g, v_w_in, v_lb_logits, v_hgrn_norm_g, v_w_a, v_attn_sinks, v_w_b, v_w_out, v_norm2_g, v_w_ffn_in, v_conv_w, v_conv_b, v_w_down, v_final_g):
    xi, yi, ci = _my_place()
    dev = 4 * xi + 2 * yi + ci
    core = ci.astype(jnp.int32).reshape(1)
    chip = (2 * xi + yi).astype(jnp.int32).reshape(1)

    packed_w = _pack_shards(w_in, w_a, w_b, w_out, w_ffn_in, w_down)
    gathered = _all_gather(packed_w.astype(BF16), in_vmem=False, name="ag_weights")
    o_a = IN_BLK
    o_ffn = IN_BLK + 3 * ROW_BLK
    o_down = o_ffn + FFN_BLK
    w_in_full = _cols_from_blocks(gathered, 0, IN_BLK)
    conv_rows = 16
    conv_all = _all_gather(_to_rows(conv_w, conv_rows), in_vmem=True, name="ag_conv_w")
    conv_full = conv_all.reshape(N_DEV, conv_rows * LANES)[:, :3 * CONVW_BLK].reshape(N_DEV, 3, CONVW_BLK).transpose(1, 0, 2).reshape(3, D_FF)
    W = dict(
        w_h=w_in_full[:, :4096], w_aq=w_in_full[:, 4096:5120], w_akv=w_in_full[:, 5120:5376], w_g=w_in_full[:, 5376:],
        w_a=gathered[:, o_a:o_a + ROW_BLK].reshape(D_MODEL, D_MODEL),
        w_b=gathered[:, o_a + ROW_BLK:o_a + 2 * ROW_BLK].reshape(D_MODEL, D_MODEL),
        w_out=gathered[:, o_a + 2 * ROW_BLK:o_a + 3 * ROW_BLK].reshape(D_MODEL, D_MODEL),
        w_ffn=_cols_from_blocks(gathered, o_ffn, FFN_BLK),
        w_down=gathered[:, o_down:o_down + DOWN_BLK].reshape(D_FF, D_MODEL),
        conv_w=conv_full,
    )
    small = dict(norm1_g=norm1_g, lb_logits=lb_logits, hgrn_norm_g=hgrn_norm_g, attn_sinks=attn_sinks, norm2_g=norm2_g,
                 conv_b=conv_b, final_g=final_g)

    loss, grad_x, g = _local_step(x, positions, loss_target, small, W)

    dw_in = jnp.concatenate([g["w_h"], g["w_aq"], g["w_akv"], g["w_g"]], axis=1)
    dw_ffn = jnp.concatenate([g["w_ffn_g"], g["w_ffn_u"]], axis=1)
    gp = jnp.concatenate([
        _blocks_from_cols(dw_in, IN_BLK), g["w_a"].reshape(N_DEV, ROW_BLK, D_MODEL), g["w_b"].reshape(N_DEV, ROW_BLK, D_MODEL),
        g["w_out"].reshape(N_DEV, ROW_BLK, D_MODEL), _blocks_from_cols(dw_ffn, FFN_BLK), g["w_down"].reshape(N_DEV, DOWN_BLK, D_MODEL)], axis=1)
    g5 = gp.reshape(4, 2, PACK_ROWS, D_MODEL)
    r1 = _rs_sibling(g5, name="rs_sibling")
    p1 = _add_sibling(g5, r1, core, name="rs_add_sibling")
    r2 = _rs_chips(p1, name="rs_chips")
    packed_m = _pack_shards(m_w_in, m_w_a, m_w_b, m_w_out, m_w_ffn_in, m_w_down)
    packed_v = _pack_shards(v_w_in, v_w_a, v_w_b, v_w_out, v_w_ffn_in, v_w_down)
    big = [_unpack_shards(t) for t in _adamw_big(p1, r2, chip, packed_w, packed_m, packed_v, name="adamw_big")]

    dlogits = _lb_bwd(g["lb"], g["lb_fwd"], name="lb_bwd")
    sm_g = dict(norm1_g=g["norm1_g"], lb_logits=dlogits, hgrn_norm_g=g["hgrn_norm_g"], attn_sinks=g["attn_sinks"][:, 0],
                norm2_g=g["norm2_g"], conv_b=g["conv_b"], final_g=g["final_g"])
    vec = jnp.concatenate([sm_g[n].reshape(-1) for n in SMALL_NAMES] + [g["conv_w"].reshape(-1), loss.reshape(-1)])
    sm_rows = 136
    _, total = _all_gather(_to_rows(vec, sm_rows), in_vmem=True, reduce_sum=True, name="ar_small")
    total = total.reshape(-1)
    sm_w = dict(norm1_g=norm1_g, lb_logits=lb_logits, hgrn_norm_g=hgrn_norm_g, attn_sinks=attn_sinks, norm2_g=norm2_g,
                conv_b=conv_b, final_g=final_g)
    sm_m = dict(norm1_g=m_norm1_g, lb_logits=m_lb_logits, hgrn_norm_g=m_hgrn_norm_g, attn_sinks=m_attn_sinks, norm2_g=m_norm2_g,
                conv_b=m_conv_b, final_g=m_final_g)
    sm_v = dict(norm1_g=v_norm1_g, lb_logits=v_lb_logits, hgrn_norm_g=v_hgrn_norm_g, attn_sinks=v_attn_sinks, norm2_g=v_norm2_g,
                conv_b=v_conv_b, final_g=v_final_g)
    sizes = [sm_w[n].size for n in SMALL_NAMES]
    n_rep = sum(sizes)
    g_conv_full = total[n_rep:n_rep + 3 * D_FF].reshape(3, D_FF)
    g_conv = lax.dynamic_slice_in_dim(g_conv_full, dev * CONVW_BLK, CONVW_BLK, axis=1)
    loss_total = total[n_rep + 3 * D_FF]
    ad_rows = 72
    pack_small = lambda d, cw: _to_rows(jnp.concatenate([d[n].reshape(-1) for n in SMALL_NAMES] + [cw.reshape(-1)]), ad_rows)
    g_small = _to_rows(jnp.concatenate([total[:n_rep], g_conv.reshape(-1)]), ad_rows)
    d_s, m_s, v_s = _adamw_small(g_small, pack_small(sm_w, conv_w), pack_small(sm_m, m_conv_w), pack_small(sm_v, v_conv_w), name="adamw_small")

    def unpack_small(t):
        t = t.reshape(-1)
        out, off = {}, 0
        for n, s in zip(SMALL_NAMES, sizes):
            out[n] = t[off:off + s].reshape(sm_w[n].shape)
            off += s
        out["conv_w"] = t[off:off + 3 * CONVW_BLK].reshape(1, 3, CONVW_BLK)
        return out

    names = ("norm1_g", "w_in", "lb_logits", "hgrn_norm_g", "w_a", "attn_sinks", "w_b", "w_out", "norm2_g", "w_ffn_in", "conv_w", "conv_b", "w_down", "final_g")
    big_names = ("w_in", "w_a", "w_b", "w_out", "w_ffn_in", "w_down")
    outs = [loss_total.reshape(()), grad_x]
    for kind, s_vec in zip(range(4), (g_small, d_s, m_s, v_s)):
        s_un = unpack_small(s_vec)
        b_un = dict(zip(big_names, big[kind]))
        outs += [b_un[n] if n in b_un else s_un[n] for n in names]
    return tuple(outs)
```

```python
import functools

import jax
import jax.numpy as jnp
from jax import lax
from jax.experimental import pallas as pl
from jax.experimental.pallas import tpu as pltpu

F32 = jnp.float32
BF16 = jnp.bfloat16

D_MODEL = 1024
HGRN_HEADS = 8
HGRN_DK = 128
CHUNK = 64
ATT_HEADS = 16
ATT_KV_HEADS = 2
ATT_HD = 64
ATT_GROUP = ATT_HEADS // ATT_KV_HEADS
WINDOW = 128
ROPE_DIM = ATT_HD // 4
ROPE_THETA = 500000.0
D_FF = 2816
EPS = 1e-6
NEG_INF = -1e30
N_DEV = 8

ADAM_LR = 0.001
ADAM_B1 = 0.9
ADAM_B2 = 0.999
ADAM_EPS = 1e-08
ADAM_WD = 0.01
ADAM_STEP = 10

MESH = pl.DeviceIdType.MESH
ANY = pl.BlockSpec(memory_space=pl.ANY)


def _pick(n, cands):
    for c in cands:
        if n % c == 0:
            return c
    return n


def _sigmoid(x):
    return 1.0 / (1.0 + jnp.exp(-x))


def _silu(x):
    return x * _sigmoid(x)


def _rms(x, g):
    return x * lax.rsqrt(jnp.mean(x * x, axis=-1, keepdims=True) + EPS) * g


def _dot(a, b, dims):
    return lax.dot_general(a, b, (dims, ((), ())), preferred_element_type=F32)


def _nn(a, b):
    return _dot(a, b, ((1,), (0,)))


def _nt(a, b):
    return _dot(a, b, ((1,), (1,)))


def _tn(a, b):
    return _dot(a, b, ((0,), (0,)))


def _params(*sem):
    return pltpu.CompilerParams(dimension_semantics=sem, vmem_limit_bytes=56 * 1024 * 1024)


def _matmul(a, b, *, ta=False, tb=False, out_dtype=F32, addend=None, name, tm, tn, tk=None, n_extent=None, b_koff=0, b_noff=0):
    M, K = (a.shape[1], a.shape[0]) if ta else a.shape
    N = n_extent or (b.shape[0] if tb else b.shape[1])
    tm, tn, tk = min(tm, M), min(tn, N), min(tk or K, K)
    assert M % tm == 0 and N % tn == 0 and K % tk == 0, (name, M, N, K, tm, tn, tk)
    nk = K // tk
    assert nk == 1 or out_dtype == F32
    grid = (M // tm, N // tn, nk)
    a_spec = pl.BlockSpec((tk, tm), lambda i, j, k: (k, i)) if ta else pl.BlockSpec((tm, tk), lambda i, j, k: (i, k))
    b_spec = pl.BlockSpec((tn, tk), lambda i, j, k: (j + b_noff, k + b_koff)) if tb else pl.BlockSpec((tk, tn), lambda i, j, k: (k + b_koff, j + b_noff))
    o_spec = pl.BlockSpec((tm, tn), lambda i, j, k: (i, j))
    dims = ((0 if ta else 1,), (1 if tb else 0,))
    has_add = addend is not None

    def body(*refs):
        if has_add:
            a_ref, b_ref, c_ref, o_ref = refs
        else:
            a_ref, b_ref, o_ref = refs
            c_ref = None
        part = _dot(a_ref[...], b_ref[...], dims)
        if nk == 1:
            if has_add:
                part = part + c_ref[...].astype(F32)
            o_ref[...] = part.astype(out_dtype)
        else:
            k = pl.program_id(2)

            @pl.when(k == 0)
            def _():
                o_ref[...] = part + c_ref[...].astype(F32) if has_add else part

            @pl.when(k > 0)
            def _():
                o_ref[...] += part

    in_specs = [a_spec, b_spec] + ([o_spec] if has_add else [])
    args = (a, b) + ((addend,) if has_add else ())
    return pl.pallas_call(
        body,
        name=name,
        grid=grid,
        in_specs=in_specs,
        out_specs=o_spec,
        out_shape=jax.ShapeDtypeStruct((M, N), out_dtype),
        compiler_params=_params("parallel", "parallel", "arbitrary"),
    )(*args)


def _row_spec(tm, n):
    return pl.BlockSpec((tm, n), lambda i: (i, 0))


def _full_spec(shape):
    return pl.BlockSpec(shape, lambda i: tuple(0 for _ in shape))


def _norm_cast(x, g, *, name):
    T, D = x.shape
    tm = _pick(T, (512, 256, 128))

    def body(x_ref, g_ref, u_ref):
        u_ref[...] = _rms(x_ref[...], g_ref[...]).astype(BF16)

    return pl.pallas_call(
        body, name=name, grid=(T // tm,),
        in_specs=[_row_spec(tm, D), _full_spec((1, D))],
        out_specs=_row_spec(tm, D),
        out_shape=jax.ShapeDtypeStruct((T, D), BF16),
        compiler_params=_params("parallel"),
    )(x, g)


def _norm_bwd_add(x, g, du, dres, *, name):
    T, D = x.shape
    tm = _pick(T, (512, 256, 128))
    has_res = dres is not None

    def body(*refs):
        if has_res:
            x_ref, g_ref, du_ref, dr_ref, dx_ref, dxb_ref, dg_ref = refs
        else:
            x_ref, g_ref, du_ref, dx_ref, dxb_ref, dg_ref = refs
        _, vjp = jax.vjp(_rms, x_ref[...], g_ref[...])
        dx, dg = vjp(du_ref[...].astype(F32))
        if has_res:
            dx = dx + dr_ref[...]
        dx_ref[...] = dx
        dxb_ref[...] = dx.astype(BF16)

        @pl.when(pl.program_id(0) == 0)
        def _():
            dg_ref[...] = jnp.zeros_like(dg_ref)

        dg_ref[...] += dg

    ins = [x, g, du] + ([dres] if has_res else [])
    in_specs = [_row_spec(tm, D), _full_spec((1, D)), _row_spec(tm, D)] + ([_row_spec(tm, D)] if has_res else [])
    return pl.pallas_call(
        body, name=name, grid=(T // tm,),
        in_specs=in_specs,
        out_specs=[_row_spec(tm, D), _row_spec(tm, D), _full_spec((1, D))],
        out_shape=[jax.ShapeDtypeStruct((T, D), F32), jax.ShapeDtypeStruct((T, D), BF16), jax.ShapeDtypeStruct((1, D), F32)],
        compiler_params=_params("arbitrary"),
    )(*ins)


def _final_loss_bwd(h2, g, target, *, name):
    T, D = h2.shape
    tm = _pick(T, (512, 256, 128))

    def body(h_ref, g_ref, t_ref, dx_ref, dxb_ref, dg_ref, loss_ref):
        y, vjp = jax.vjp(_rms, h_ref[...], g_ref[...])
        err = y - t_ref[...]
        dx, dg = vjp(err * (1.0 / D))
        dx_ref[...] = dx
        dxb_ref[...] = dx.astype(BF16)

        @pl.when(pl.program_id(0) == 0)
        def _():
            dg_ref[...] = jnp.zeros_like(dg_ref)
            loss_ref[...] = jnp.zeros_like(loss_ref)

        dg_ref[...] += dg
        loss_ref[...] += (0.5 / D) * jnp.sum(jnp.sum(err * err, axis=1, keepdims=True), axis=0, keepdims=True)

    return pl.pallas_call(
        body, name=name, grid=(T // tm,),
        in_specs=[_row_spec(tm, D), _full_spec((1, D)), _row_spec(tm, D)],
        out_specs=[_row_spec(tm, D), _row_spec(tm, D), _full_spec((1, D)), _full_spec((1, 1))],
        out_shape=[jax.ShapeDtypeStruct((T, D), F32), jax.ShapeDtypeStruct((T, D), BF16), jax.ShapeDtypeStruct((1, D), F32), jax.ShapeDtypeStruct((1, 1), F32)],
        compiler_params=_params("arbitrary"),
    )(h2, g, target)


def _merge_fn(gates, a, b):
    ga = gates[:, :D_MODEL].astype(F32)
    gb = gates[:, D_MODEL:].astype(F32)
    return _sigmoid(ga) * a.astype(F32) + _sigmoid(gb) * b.astype(F32)


def _merge_fwd(gates, a, b, *, name):
    T = a.shape[0]
    tm = _pick(T, (512, 256, 128))

    def body(g_ref, a_ref, b_ref, o_ref):
        o_ref[...] = _merge_fn(g_ref[...], a_ref[...], b_ref[...]).astype(BF16)

    return pl.pallas_call(
        body, name=name, grid=(T // tm,),
        in_specs=[_row_spec(tm, 2 * D_MODEL), _row_spec(tm, D_MODEL), _row_spec(tm, D_MODEL)],
        out_specs=_row_spec(tm, D_MODEL),
        out_shape=jax.ShapeDtypeStruct((T, D_MODEL), BF16),
        compiler_params=_params("parallel"),
    )(gates, a, b)


def _merge_bwd(gates, a, b, dmerged, *, name):
    T = a.shape[0]
    tm = _pick(T, (512, 256, 128))

    def body(g_ref, a_ref, b_ref, dm_ref, dg_ref, da_ref, db_ref):
        g = g_ref[...].astype(F32)
        dm = dm_ref[...].astype(F32)
        sa = _sigmoid(g[:, :D_MODEL])
        sb = _sigmoid(g[:, D_MODEL:])
        da_ref[...] = (dm * sa).astype(BF16)
        db_ref[...] = (dm * sb).astype(BF16)
        dg_ref[:, :D_MODEL] = (dm * a_ref[...].astype(F32) * sa * (1.0 - sa)).astype(BF16)
        dg_ref[:, D_MODEL:] = (dm * b_ref[...].astype(F32) * sb * (1.0 - sb)).astype(BF16)

    return pl.pallas_call(
        body, name=name, grid=(T // tm,),
        in_specs=[_row_spec(tm, 2 * D_MODEL), _row_spec(tm, D_MODEL), _row_spec(tm, D_MODEL), _row_spec(tm, D_MODEL)],
        out_specs=[_row_spec(tm, 2 * D_MODEL), _row_spec(tm, D_MODEL), _row_spec(tm, D_MODEL)],
        out_shape=[jax.ShapeDtypeStruct((T, 2 * D_MODEL), BF16), jax.ShapeDtypeStruct((T, D_MODEL), BF16), jax.ShapeDtypeStruct((T, D_MODEL), BF16)],
        compiler_params=_params("parallel"),
    )(gates, a, b, dmerged)


CONV_TC = 128


def _shift_down(x, n, rows):
    return jnp.where(rows >= n, pltpu.roll(x, n, 0), 0.0)


def _shift_up(x, n, rows, S):
    return jnp.where(rows < S - n, pltpu.roll(x, S - n, 0), 0.0)


def _conv_act_fwd(gu, conv_w, conv_b, *, name):
    B, S, _ = gu.shape
    tc = CONV_TC
    nc = D_FF // tc

    def body(g_ref, up_ref, w_ref, b_ref, o_ref):
        g = g_ref[...].astype(F32)
        rows = lax.broadcasted_iota(jnp.int32, g.shape, 0)
        w = w_ref[...]
        a = w[2:3] * g + w[1:2] * _shift_down(g, 1, rows) + w[0:1] * _shift_down(g, 2, rows) + b_ref[...]
        o_ref[...] = (_silu(a) * up_ref[...].astype(F32)).astype(BF16)

    return pl.pallas_call(
        body, name=name, grid=(B, nc),
        in_specs=[pl.BlockSpec((None, S, tc), lambda b, j: (b, 0, j)),
                  pl.BlockSpec((None, S, tc), lambda b, j: (b, 0, j + nc)),
                  pl.BlockSpec((3, tc), lambda b, j: (0, j)),
                  pl.BlockSpec((1, tc), lambda b, j: (0, j))],
        out_specs=pl.BlockSpec((None, S, tc), lambda b, j: (b, 0, j)),
        out_shape=jax.ShapeDtypeStruct((B, S, D_FF), BF16),
        compiler_params=_params("parallel", "parallel"),
    )(gu, gu, conv_w, conv_b)


def _conv_act_bwd(gu, conv_w, conv_b, dact, *, name):
    B, S, _ = gu.shape
    tc = CONV_TC
    nc = D_FF // tc

    def body(g_ref, up_ref, w_ref, b_ref, da_ref, dg_ref, dup_ref, dw_ref, db_ref):
        g = g_ref[...].astype(F32)
        up = up_ref[...].astype(F32)
        dact = da_ref[...].astype(F32)
        rows = lax.broadcasted_iota(jnp.int32, g.shape, 0)
        w = w_ref[...]
        g1 = _shift_down(g, 1, rows)
        g2 = _shift_down(g, 2, rows)
        a = w[2:3] * g + w[1:2] * g1 + w[0:1] * g2 + b_ref[...]
        sg = _sigmoid(a)
        dup_ref[...] = (dact * a * sg).astype(BF16)
        da = dact * up * sg * (1.0 + a * (1.0 - sg))
        dg = w[2:3] * da + w[1:2] * _shift_up(da, 1, rows, S) + w[0:1] * _shift_up(da, 2, rows, S)
        dg_ref[...] = dg.astype(BF16)

        @pl.when(pl.program_id(1) == 0)
        def _():
            dw_ref[...] = jnp.zeros_like(dw_ref)
            db_ref[...] = jnp.zeros_like(db_ref)

        dw_ref[0:1, :] += jnp.sum(da * g2, axis=0, keepdims=True)
        dw_ref[1:2, :] += jnp.sum(da * g1, axis=0, keepdims=True)
        dw_ref[2:3, :] += jnp.sum(da * g, axis=0, keepdims=True)
        db_ref[...] += jnp.sum(da, axis=0, keepdims=True)

    col = lambda j, b: (b, 0, j)
    return pl.pallas_call(
        body, name=name, grid=(nc, B),
        in_specs=[pl.BlockSpec((None, S, tc), col),
                  pl.BlockSpec((None, S, tc), lambda j, b: (b, 0, j + nc)),
                  pl.BlockSpec((3, tc), lambda j, b: (0, j)),
                  pl.BlockSpec((1, tc), lambda j, b: (0, j)),
                  pl.BlockSpec((None, S, tc), col)],
        out_specs=[pl.BlockSpec((None, S, tc), col), pl.BlockSpec((None, S, tc), col),
                   pl.BlockSpec((3, tc), lambda j, b: (0, j)), pl.BlockSpec((1, tc), lambda j, b: (0, j))],
        out_shape=[jax.ShapeDtypeStruct((B, S, D_FF), BF16), jax.ShapeDtypeStruct((B, S, D_FF), BF16),
                   jax.ShapeDtypeStruct((3, D_FF), F32), jax.ShapeDtypeStruct((1, D_FF), F32)],
        compiler_params=_params("parallel", "arbitrary"),
    )(gu, gu, conv_w, conv_b, dact)


HGRN_CPB = 4
HF = HGRN_HEADS * HGRN_DK


def _tri(n, upper=False):
    r = lax.broadcasted_iota(jnp.int32, (n, n), 0)
    c = lax.broadcasted_iota(jnp.int32, (n, n), 1)
    return (c >= r) if upper else (r >= c)


def _hs(h):
    return slice(h * HGRN_DK, (h + 1) * HGRN_DK)


def _cumsum_rows(tri_b, x):
    hi = x.astype(BF16)
    lo = (x - hi.astype(F32)).astype(BF16)
    return _nn(tri_b, hi) + _nn(tri_b, lo)


def _hgrn_pre(q, fz, lb, tril_b):
    qf = _silu(q)
    sg = _sigmoid(fz)
    f = lb + (1.0 - lb) * sg
    k = 1.0 - f
    b = _cumsum_rows(tril_b, jnp.log(f))
    bref = b[CHUNK // 2:CHUNK // 2 + 1, :]
    blast = b[CHUNK - 1:CHUNK, :]
    e1 = jnp.exp(b - bref)
    e2 = jnp.exp(bref - b)
    e3 = jnp.exp(b)
    e4 = jnp.exp(blast - b)
    dec = jnp.exp(blast)
    return sg, f, (e1, e2, e3, e4), qf * e1, k * e2, qf * e3, k * e4, dec


def _hgrn_fwd(zh, lb, gn, *, name):
    B, S, _ = zh.shape
    cpb = HGRN_CPB
    ts = cpb * CHUNK
    nblk = S // ts

    def body(z_ref, lb_ref, gn_ref, o_ref, st_ref, state):
        @pl.when(pl.program_id(1) == 0)
        def _():
            state[...] = jnp.zeros_like(state)

        H = HGRN_HEADS
        causal = _tri(CHUNK)
        tril_b = causal.astype(BF16)
        lb = lb_ref[...]
        for c in range(cpb):
            rows = slice(c * CHUNK, (c + 1) * CHUNK)
            q = z_ref[rows, 0:HF].astype(F32)
            fz = z_ref[rows, HF:2 * HF].astype(F32)
            v = z_ref[rows, 2 * HF:3 * HF]
            hg = z_ref[rows, 3 * HF:4 * HF].astype(F32)
            _, _, _, q_in, k_in, q_out, k_st, dec = _hgrn_pre(q, fz, lb, tril_b)
            q_in, k_in, q_out, k_st = (t.astype(BF16) for t in (q_in, k_in, q_out, k_st))
            a = [jnp.where(causal, _nt(q_in[:, _hs(h)], k_in[:, _hs(h)]), 0.0).astype(BF16) for h in range(H)]
            st = [state[h] for h in range(H)]
            for h in range(H):
                st_ref[c, h] = st[h]
            o = [_nn(a[h], v[:, _hs(h)]) + _nt(q_out[:, _hs(h)], st[h].astype(BF16)) for h in range(H)]
            for h in range(H):
                state[h] = st[h] * dec[:, _hs(h)] + _tn(v[:, _hs(h)], k_st[:, _hs(h)])
            gate = _silu(hg)
            for h in range(H):
                o_ref[rows, _hs(h)] = (_rms(o[h], gn_ref[...]) * gate[:, _hs(h)]).astype(BF16)

    return pl.pallas_call(
        body, name=name, grid=(B, nblk),
        in_specs=[pl.BlockSpec((None, ts, 4 * HF), lambda b, s: (b, s, 0)),
                  pl.BlockSpec((1, HF), lambda b, s: (0, 0)),
                  pl.BlockSpec((1, HGRN_DK), lambda b, s: (0, 0))],
        out_specs=[pl.BlockSpec((None, ts, HF), lambda b, s: (b, s, 0)),
                   pl.BlockSpec((None, cpb, HGRN_HEADS, HGRN_DK, HGRN_DK), lambda b, s: (b, s, 0, 0, 0))],
        out_shape=[jax.ShapeDtypeStruct((B, S, HF), BF16),
                   jax.ShapeDtypeStruct((B, S // CHUNK, HGRN_HEADS, HGRN_DK, HGRN_DK), F32)],
        scratch_shapes=[pltpu.VMEM((HGRN_HEADS, HGRN_DK, HGRN_DK), F32)],
        compiler_params=_params("arbitrary", "arbitrary"),
    )(zh, lb, gn)


def _hgrn_bwd(zh, lb, gn, states, doa, *, name):
    B, S, _ = zh.shape
    cpb = HGRN_CPB
    ts = cpb * CHUNK
    nblk = S // ts
    rev = lambda b, s: (b, nblk - 1 - s, 0)

    def body(z_ref, lb_ref, gn_ref, st_ref, do_ref, dz_ref, dlb_ref, dgn_ref, dstate):
        @pl.when(pl.program_id(1) == 0)
        def _():
            dstate[...] = jnp.zeros_like(dstate)

        @pl.when((pl.program_id(0) == 0) & (pl.program_id(1) == 0))
        def _():
            dlb_ref[...] = jnp.zeros_like(dlb_ref)
            dgn_ref[...] = jnp.zeros_like(dgn_ref)

        H = HGRN_HEADS
        cat = lambda xs: jnp.concatenate(xs, axis=1)
        causal = _tri(CHUNK)
        tril_b = causal.astype(BF16)
        triu_b = _tri(CHUNK, upper=True).astype(BF16)
        rowid = lax.broadcasted_iota(jnp.int32, (CHUNK, HF), 0)
        lb = lb_ref[...]
        gn = gn_ref[...]
        for c in reversed(range(cpb)):
            rows = slice(c * CHUNK, (c + 1) * CHUNK)
            q = z_ref[rows, 0:HF].astype(F32)
            fz = z_ref[rows, HF:2 * HF].astype(F32)
            v = z_ref[rows, 2 * HF:3 * HF]
            hg = z_ref[rows, 3 * HF:4 * HF].astype(F32)
            sg, f, (e1, e2, e3, e4), q_in, k_in, q_out, k_st, dec = _hgrn_pre(q, fz, lb, tril_b)
            q_in_b, k_in_b, q_out_b, k_st_b = (t.astype(BF16) for t in (q_in, k_in, q_out, k_st))
            a_b = [jnp.where(causal, _nt(q_in_b[:, _hs(h)], k_in_b[:, _hs(h)]), 0.0).astype(BF16) for h in range(H)]
            st = [st_ref[c, h] for h in range(H)]
            st_b = [t.astype(BF16) for t in st]
            o = [_nn(a_b[h], v[:, _hs(h)]) + _nt(q_out_b[:, _hs(h)], st_b[h]) for h in range(H)]
            dout = do_ref[rows, :].astype(F32)
            shg = _sigmoid(hg)
            gate = hg * shg
            do_l, dgn_acc = [], jnp.zeros_like(gn)
            for h in range(H):
                _, norm_vjp = jax.vjp(_rms, o[h], gn)
                d_o, d_gn = norm_vjp(dout[:, _hs(h)] * gate[:, _hs(h)])
                do_l.append(d_o)
                dgn_acc = dgn_acc + d_gn
            dgn_ref[...] += dgn_acc
            on = cat([_rms(o[h], gn) for h in range(H)])
            dhg = dout * on * shg * (1.0 + hg * (1.0 - shg))
            do_b = [t.astype(BF16) for t in do_l]
            dst = [dstate[h] for h in range(H)]
            dst_b = [t.astype(BF16) for t in dst]
            da_b = [jnp.where(causal, _nt(do_b[h], v[:, _hs(h)]), 0.0).astype(BF16) for h in range(H)]
            dv = cat([_tn(a_b[h], do_b[h]) + _nt(k_st_b[:, _hs(h)], dst_b[h]) for h in range(H)])
            dq_in = cat([_nn(da_b[h], k_in_b[:, _hs(h)]) for h in range(H)])
            dk_in = cat([_tn(da_b[h], q_in_b[:, _hs(h)]) for h in range(H)])
            dq_out = cat([_nn(do_b[h], st_b[h]) for h in range(H)])
            dk_st = cat([_nn(v[:, _hs(h)], dst_b[h]) for h in range(H)])
            ddec = cat([jnp.sum(st[h] * dst[h], axis=0, keepdims=True) for h in range(H)])
            for h in range(H):
                dstate[h] = dst[h] * dec[:, _hs(h)] + _tn(do_b[h], q_out_b[:, _hs(h)])
            t_qin = dq_in * q_in
            t_kin = dk_in * k_in
            t_kst = dk_st * k_st
            db = t_qin - t_kin + dq_out * q_out - t_kst
            dbref = jnp.sum(t_kin - t_qin, axis=0, keepdims=True)
            dblast = jnp.sum(t_kst, axis=0, keepdims=True) + ddec * dec
            db = db + jnp.where(rowid == CHUNK // 2, dbref, 0.0) + jnp.where(rowid == CHUNK - 1, dblast, 0.0)
            dlogf = _cumsum_rows(triu_b, db)
            dqf = dq_in * e1 + dq_out * e3
            dk = dk_in * e2 + dk_st * e4
            df = dlogf / f - dk
            dfz = df * (1.0 - lb) * sg * (1.0 - sg)
            dlb_ref[...] += jnp.sum(df * (1.0 - sg), axis=0, keepdims=True)
            sq = _sigmoid(q)
            dq = dqf * sq * (1.0 + q * (1.0 - sq))
            dz_ref[rows, 0:HF] = dq.astype(BF16)
            dz_ref[rows, HF:2 * HF] = dfz.astype(BF16)
            dz_ref[rows, 2 * HF:3 * HF] = dv.astype(BF16)
            dz_ref[rows, 3 * HF:4 * HF] = dhg.astype(BF16)

    return pl.pallas_call(
        body, name=name, grid=(B, nblk),
        in_specs=[pl.BlockSpec((None, ts, 4 * HF), rev),
                  pl.BlockSpec((1, HF), lambda b, s: (0, 0)),
                  pl.BlockSpec((1, HGRN_DK), lambda b, s: (0, 0)),
                  pl.BlockSpec((None, cpb, HGRN_HEADS, HGRN_DK, HGRN_DK), lambda b, s: (b, nblk - 1 - s, 0, 0, 0)),
                  pl.BlockSpec((None, ts, HF), rev)],
        out_specs=[pl.BlockSpec((None, ts, 4 * HF), rev),
                   pl.BlockSpec((1, HF), lambda b, s: (0, 0)),
                   pl.BlockSpec((1, HGRN_DK), lambda b, s: (0, 0))],
        out_shape=[jax.ShapeDtypeStruct((B, S, 4 * HF), BF16),
                   jax.ShapeDtypeStruct((1, HF), F32),
                   jax.ShapeDtypeStruct((1, HGRN_DK), F32)],
        scratch_shapes=[pltpu.VMEM((HGRN_HEADS, HGRN_DK, HGRN_DK), F32)],
        compiler_params=_params("arbitrary", "arbitrary"),
    )(zh, lb, gn, states, doa)


KV_W = ATT_KV_HEADS * ATT_HD
ATT_SCALE = ATT_HD ** -0.5


def _rope(x, cos, sin, inverse=False):
    half = ROPE_DIM // 2
    outs = []
    for p in range(x.shape[1] // 128):
        xp = x[:, p * 128:(p + 1) * 128]
        lane = lax.broadcasted_iota(jnp.int32, xp.shape, 1) % ATT_HD
        sw = jnp.where(lane < half, pltpu.roll(xp, 128 - half, 1), pltpu.roll(xp, half, 1))
        outs.append(xp * cos - sw * sin if inverse else xp * cos + sw * sin)
    return outs[0] if len(outs) == 1 else jnp.concatenate(outs, axis=1)


def _swa_mask(first_block):
    qi = lax.broadcasted_iota(jnp.int32, (WINDOW, 2 * WINDOW), 0)
    mi = lax.broadcasted_iota(jnp.int32, (WINDOW, 2 * WINDOW), 1)
    band = (mi > qi) & (mi <= qi + WINDOW)
    return band & (jnp.logical_not(first_block) | (mi >= WINDOW))


def _swa_specs(nb):
    cur = lambda b, i: (b, i, 0)
    prev = lambda b, i: (b, jnp.maximum(i - 1, 0), 0)
    return cur, prev


def _swa_fwd(aq, akv, cos, sin, sinks, *, name):
    B, S, _ = aq.shape
    nb = S // WINDOW
    cur, prev = _swa_specs(nb)

    def body(q_ref, kvp_ref, kvc_ref, cp_ref, sp_ref, cc_ref, sc_ref, sink_ref, o_ref, lse_ref):
        cos_c, sin_c = cc_ref[...], sc_ref[...]
        q = _rope(q_ref[...].astype(F32), cos_c, sin_c).astype(BF16)
        k = jnp.concatenate([_rope(kvp_ref[:, :KV_W].astype(F32), cp_ref[...], sp_ref[...]),
                             _rope(kvc_ref[:, :KV_W].astype(F32), cos_c, sin_c)], axis=0).astype(BF16)
        v = jnp.concatenate([kvp_ref[:, KV_W:], kvc_ref[:, KV_W:]], axis=0)
        mask = _swa_mask(pl.program_id(1) == 0)
        for h in range(ATT_HEADS):
            g = h // ATT_GROUP
            qh = q[:, h * ATT_HD:(h + 1) * ATT_HD]
            kg = k[:, g * ATT_HD:(g + 1) * ATT_HD]
            vg = v[:, g * ATT_HD:(g + 1) * ATT_HD]
            s = jnp.where(mask, _nt(qh, kg) * ATT_SCALE, NEG_INF)
            sink = sink_ref[0, h]
            m = jnp.maximum(jnp.max(s, axis=1, keepdims=True), sink)
            p = jnp.exp(s - m)
            den = jnp.sum(p, axis=1, keepdims=True) + jnp.exp(sink - m)
            o = _nn((p / den).astype(BF16), vg)
            o_ref[:, h * ATT_HD:(h + 1) * ATT_HD] = o.astype(BF16)
            lse_ref[:, h:h + 1] = m + jnp.log(den)

    tab = lambda im: pl.BlockSpec((None, WINDOW, 128), im)
    return pl.pallas_call(
        body, name=name, grid=(B, nb),
        in_specs=[pl.BlockSpec((None, WINDOW, D_MODEL), cur),
                  pl.BlockSpec((None, WINDOW, 2 * KV_W), prev), pl.BlockSpec((None, WINDOW, 2 * KV_W), cur),
                  tab(prev), tab(prev), tab(cur), tab(cur),
                  pl.BlockSpec(memory_space=pltpu.SMEM)],
        out_specs=[pl.BlockSpec((None, WINDOW, D_MODEL), cur), pl.BlockSpec((None, WINDOW, ATT_HEADS), cur)],
        out_shape=[jax.ShapeDtypeStruct((B, S, D_MODEL), BF16), jax.ShapeDtypeStruct((B, S, ATT_HEADS), F32)],
        compiler_params=_params("parallel", "parallel"),
    )(aq, akv, akv, cos, sin, cos, sin, sinks)


def _swa_bwd(aq, akv, cos, sin, sinks, lse, dob, *, name):
    B, S, _ = aq.shape
    nb = S // WINDOW
    cur, prev = _swa_specs(nb)

    def body(q_ref, kvp_ref, kvc_ref, cp_ref, sp_ref, cc_ref, sc_ref, sink_ref, lse_ref, do_ref,
             dq_ref, dkc_ref, dkp_ref, dsink_ref):
        @pl.when((pl.program_id(0) == 0) & (pl.program_id(1) == 0))
        def _():
            dsink_ref[...] = jnp.zeros_like(dsink_ref)

        cos_c, sin_c, cos_p, sin_p = cc_ref[...], sc_ref[...], cp_ref[...], sp_ref[...]
        q = _rope(q_ref[...].astype(F32), cos_c, sin_c).astype(BF16)
        k = jnp.concatenate([_rope(kvp_ref[:, :KV_W].astype(F32), cos_p, sin_p),
                             _rope(kvc_ref[:, :KV_W].astype(F32), cos_c, sin_c)], axis=0).astype(BF16)
        v = jnp.concatenate([kvp_ref[:, KV_W:], kvc_ref[:, KV_W:]], axis=0)
        mask = _swa_mask(pl.program_id(1) == 0)
        dqs = []
        dks = []
        dvs = []
        for g in range(ATT_KV_HEADS):
            kg = k[:, g * ATT_HD:(g + 1) * ATT_HD]
            vg = v[:, g * ATT_HD:(g + 1) * ATT_HD]
            dk = jnp.zeros((2 * WINDOW, ATT_HD), F32)
            dv = jnp.zeros((2 * WINDOW, ATT_HD), F32)
            for h in range(g * ATT_GROUP, (g + 1) * ATT_GROUP):
                qh = q[:, h * ATT_HD:(h + 1) * ATT_HD]
                doh = do_ref[:, h * ATT_HD:(h + 1) * ATT_HD]
                lse_h = lse_ref[:, h:h + 1]
                s = jnp.where(mask, _nt(qh, kg) * ATT_SCALE, NEG_INF)
                p = jnp.exp(s - lse_h)
                dp = _nt(doh, vg)
                delta = jnp.sum(p * dp, axis=1, keepdims=True)
                ds = (p * (dp - delta) * ATT_SCALE).astype(BF16)
                p_sink = jnp.exp(sink_ref[0, h] - lse_h)
                dsink_ref[h:h + 1, :] += jnp.broadcast_to(-jnp.sum(p_sink * delta, axis=0, keepdims=True), (1, 128))
                dqs.append(_nn(ds, kg))
                dk = dk + _tn(ds, qh)
                dv = dv + _tn(p.astype(BF16), doh)
            dks.append(dk)
            dvs.append(dv)
        dq_ref[...] = _rope(jnp.concatenate(dqs, axis=1), cos_c, sin_c, inverse=True).astype(BF16)
        dk = jnp.concatenate(dks, axis=1)
        dv = jnp.concatenate(dvs, axis=1)
        dkp_ref[:, :KV_W] = _rope(dk[:WINDOW], cos_p, sin_p, inverse=True)
        dkp_ref[:, KV_W:] = dv[:WINDOW]
        dkc_ref[:, :KV_W] = _rope(dk[WINDOW:], cos_c, sin_c, inverse=True)
        dkc_ref[:, KV_W:] = dv[WINDOW:]

    tab = lambda im: pl.BlockSpec((None, WINDOW, 128), im)
    return pl.pallas_call(
        body, name=name, grid=(B, nb),
        in_specs=[pl.BlockSpec((None, WINDOW, D_MODEL), cur),
                  pl.BlockSpec((None, WINDOW, 2 * KV_W), prev), pl.BlockSpec((None, WINDOW, 2 * KV_W), cur),
                  tab(prev), tab(prev), tab(cur), tab(cur),
                  pl.BlockSpec(memory_space=pltpu.SMEM),
                  pl.BlockSpec((None, WINDOW, ATT_HEADS), cur),
                  pl.BlockSpec((None, WINDOW, D_MODEL), cur)],
        out_specs=[pl.BlockSpec((None, WINDOW, D_MODEL), cur),
                   pl.BlockSpec((None, WINDOW, 2 * KV_W), cur), pl.BlockSpec((None, WINDOW, 2 * KV_W), cur),
                   pl.BlockSpec((ATT_HEADS, 128), lambda b, i: (0, 0))],
        out_shape=[jax.ShapeDtypeStruct((B, S, D_MODEL), BF16),
                   jax.ShapeDtypeStruct((B, S, 2 * KV_W), F32), jax.ShapeDtypeStruct((B, S, 2 * KV_W), F32),
                   jax.ShapeDtypeStruct((ATT_HEADS, 128), F32)],
        compiler_params=_params("arbitrary", "arbitrary"),
    )(aq, akv, akv, cos, sin, cos, sin, sinks, lse, dob)


def _swa_dkv_combine(dkv_cur, dkv_prev, *, name):
    B, S, W = dkv_cur.shape
    nb = S // WINDOW

    def body(c_ref, p_ref, o_ref):
        nxt = jnp.where(pl.program_id(1) < nb - 1, p_ref[...], 0.0)
        o_ref[...] = (c_ref[...] + nxt).astype(BF16)

    cur = lambda b, j: (b, j, 0)
    return pl.pallas_call(
        body, name=name, grid=(B, nb),
        in_specs=[pl.BlockSpec((None, WINDOW, W), cur),
                  pl.BlockSpec((None, WINDOW, W), lambda b, j: (b, jnp.minimum(j + 1, nb - 1), 0))],
        out_specs=pl.BlockSpec((None, WINDOW, W), cur),
        out_shape=jax.ShapeDtypeStruct((B, S, W), BF16),
        compiler_params=_params("parallel", "parallel"),
    )(dkv_cur, dkv_prev)


def _rope_tables(positions):
    half = ROPE_DIM // 2
    inv = ROPE_THETA ** (-2.0 * jnp.arange(half, dtype=F32) / ROPE_DIM)
    ang = positions.astype(F32)[..., None] * inv
    c, s = jnp.cos(ang), jnp.sin(ang)
    pad = jnp.zeros(ang.shape[:-1] + (ATT_HD - ROPE_DIM,), F32)
    cos = jnp.concatenate([c, c, pad + 1.0], axis=-1)
    sin = jnp.concatenate([-s, s, pad], axis=-1)
    return jnp.tile(cos, (1, 1, 2)), jnp.tile(sin, (1, 1, 2))


def _lower_bound(lb_logits, *, name):
    def body(l_ref, o_ref):
        l = l_ref[...]
        e = jnp.exp(l - jnp.max(l, axis=0, keepdims=True))
        o_ref[...] = e[0:1] / jnp.sum(e, axis=0, keepdims=True)

    return pl.pallas_call(body, name=name, out_shape=jax.ShapeDtypeStruct((1, lb_logits.shape[1]), F32))(lb_logits)


W_ZH, W_GATES, W_AQ, W_AKV = 4 * HF, 2 * D_MODEL, ATT_HEADS * ATT_HD, 2 * KV_W
O_ZH, O_GATES, O_AQ, O_AKV = 0, W_ZH, W_ZH + W_GATES, W_ZH + W_GATES + W_AQ


def _reorder_w_in(w_in_full):
    return jnp.concatenate([w_in_full[:, :W_ZH], w_in_full[:, W_ZH + W_AQ + W_AKV:], w_in_full[:, W_ZH:W_ZH + W_AQ + W_AKV]], axis=1)


def _local_step(x, positions, target, small, W):
    B, S, D = x.shape
    T = B * S
    x2 = x.reshape(T, D)
    cos, sin = _rope_tables(positions)
    lb = _lower_bound(small["lb_logits"], name="lb_fwd")

    u1 = _norm_cast(x2, small["norm1_g"], name="norm1")
    w_in = W["w_in"]
    zh = _matmul(u1, w_in, out_dtype=BF16, name="mm_zh", tm=2048, tn=512, n_extent=W_ZH, b_noff=O_ZH // 512)
    gates = _matmul(u1, w_in, out_dtype=BF16, name="mm_gates", tm=2048, tn=512, n_extent=W_GATES, b_noff=O_GATES // 512)
    aq = _matmul(u1, w_in, out_dtype=BF16, name="mm_aq", tm=2048, tn=512, n_extent=W_AQ, b_noff=O_AQ // 512)
    akv = _matmul(u1, w_in, out_dtype=BF16, name="mm_akv", tm=2048, tn=256, n_extent=W_AKV, b_noff=O_AKV // 256)
    zh3 = zh.reshape(B, S, 4 * HF)
    aq3 = aq.reshape(B, S, D)
    akv3 = akv.reshape(B, S, 2 * KV_W)
    oa, states = _hgrn_fwd(zh3, lb, small["hgrn_norm_g"], name="hgrn_fwd")
    ob, lse = _swa_fwd(aq3, akv3, cos, sin, small["attn_sinks"], name="swa_fwd")
    oa2 = oa.reshape(T, D)
    ob2 = ob.reshape(T, D)
    pa = _matmul(oa2, W["w_a"], out_dtype=BF16, name="mm_pa", tm=2048, tn=512)
    pb = _matmul(ob2, W["w_b"], out_dtype=BF16, name="mm_pb", tm=2048, tn=512)
    merged = _merge_fwd(gates, pa, pb, name="merge_fwd")
    h = _matmul(merged, W["w_out"], addend=x2, name="mm_h", tm=1024, tn=512)
    u2 = _norm_cast(h, small["norm2_g"], name="norm2")
    gu = _matmul(u2, W["w_ffn"], out_dtype=BF16, name="mm_gu", tm=2048, tn=512)
    gu3 = gu.reshape(B, S, 2 * D_FF)
    act = _conv_act_fwd(gu3, W["conv_w"], small["conv_b"], name="conv_act_fwd")
    act2 = act.reshape(T, D_FF)
    h2 = _matmul(act2, W["w_down"], addend=h, name="mm_h2", tm=1024, tn=512)

    g = {}
    dh2, dh2b, g["final_g"], loss = _final_loss_bwd(h2, small["final_g"].reshape(1, D), target.reshape(T, D), name="final_loss_bwd")
    dact = _matmul(dh2b, W["w_down"], tb=True, out_dtype=BF16, name="mm_dact", tm=1024, tn=D_FF)
    g["w_down"] = _matmul(act2, dh2b, ta=True, name="mm_dw_down", tm=D_FF, tn=1024, tk=1024)
    dg_, dup, g["conv_w"], g["conv_b"] = _conv_act_bwd(gu3, W["conv_w"], small["conv_b"], dact.reshape(B, S, D_FF), name="conv_act_bwd")
    dg2 = dg_.reshape(T, D_FF)
    dup2 = dup.reshape(T, D_FF)
    du2 = _matmul(dg2, W["w_ffn"], tb=True, name="mm_du2_g", tm=1024, tn=512, b_koff=0)
    du2 = _matmul(dup2, W["w_ffn"], tb=True, addend=du2, name="mm_du2_u", tm=1024, tn=512, b_koff=1)
    g["w_ffn_g"] = _matmul(u2, dg2, ta=True, name="mm_dw_ffn_g", tm=1024, tn=D_FF, tk=1024)
    g["w_ffn_u"] = _matmul(u2, dup2, ta=True, name="mm_dw_ffn_u", tm=1024, tn=D_FF, tk=1024)
    dh, dhb, g["norm2_g"] = _norm_bwd_add(h, small["norm2_g"], du2, dh2, name="norm2_bwd")
    dmerged = _matmul(dhb, W["w_out"], tb=True, out_dtype=BF16, name="mm_dmerged", tm=2048, tn=512)
    g["w_out"] = _matmul(merged, dhb, ta=True, name="mm_dw_out", tm=1024, tn=1024, tk=2048)
    dgates, dpa, dpb = _merge_bwd(gates, pa, pb, dmerged, name="merge_bwd")
    doa = _matmul(dpa, W["w_a"], tb=True, out_dtype=BF16, name="mm_doa", tm=2048, tn=512)
    g["w_a"] = _matmul(oa2, dpa, ta=True, name="mm_dw_a", tm=1024, tn=1024, tk=2048)
    dob = _matmul(dpb, W["w_b"], tb=True, out_dtype=BF16, name="mm_dob", tm=2048, tn=512)
    g["w_b"] = _matmul(ob2, dpb, ta=True, name="mm_dw_b", tm=1024, tn=1024, tk=2048)
    daq, dkv_cur, dkv_prev, dsinks = _swa_bwd(aq3, akv3, cos, sin, small["attn_sinks"], lse, dob.reshape(B, S, D), name="swa_bwd")
    dakv = _swa_dkv_combine(dkv_cur, dkv_prev, name="swa_dkv").reshape(T, 2 * KV_W)
    daq2 = daq.reshape(T, D)
    g["attn_sinks"] = dsinks
    dzh, g["lb"], g["hgrn_norm_g"] = _hgrn_bwd(zh3, lb, small["hgrn_norm_g"], states, doa.reshape(B, S, D), name="hgrn_bwd")
    dzh2 = dzh.reshape(T, 4 * HF)
    du1 = _matmul(dzh2, w_in, tb=True, name="mm_du1_h", tm=1024, tn=512, b_koff=O_ZH // W_ZH)
    du1 = _matmul(dgates, w_in, tb=True, addend=du1, name="mm_du1_g", tm=1024, tn=512, b_koff=O_GATES // W_GATES)
    du1 = _matmul(daq2, w_in, tb=True, addend=du1, name="mm_du1_aq", tm=2048, tn=512, b_koff=O_AQ // W_AQ)
    du1 = _matmul(dakv, w_in, tb=True, addend=du1, name="mm_du1_akv", tm=2048, tn=512, b_koff=O_AKV // W_AKV)
    g["w_h"] = _matmul(u1, dzh2, ta=True, name="mm_dw_h", tm=1024, tn=2048, tk=1024)
    g["w_g"] = _matmul(u1, dgates, ta=True, name="mm_dw_g", tm=1024, tn=2048, tk=1024)
    g["w_aq"] = _matmul(u1, daq2, ta=True, name="mm_dw_aq", tm=1024, tn=1024, tk=2048)
    g["w_akv"] = _matmul(u1, dakv, ta=True, name="mm_dw_akv", tm=1024, tn=256, tk=2048)
    dx, _, g["norm1_g"] = _norm_bwd_add(x2, small["norm1_g"], du1, dh, name="norm1_bwd")
    g["lb_fwd"] = lb
    return loss, dx.reshape(B, S, D), g


def _my_place():
    return lax.axis_index("x"), lax.axis_index("y"), lax.axis_index("c")


def _all_gather(blk, *, in_vmem, reduce_sum=False, name):
    m, n = blk.shape
    space = pltpu.VMEM if in_vmem else pl.ANY

    def body(x_ref, out_ref, *rest):
        if reduce_sum:
            tot_ref, send_sems, recv_sems, local_sem = rest
        else:
            send_sems, recv_sems, local_sem = rest
        x, y, c = _my_place()
        me, sibling = (x, y, c), (x, y, 1 - c)
        chips = [(1 - x, y), (x, 1 - y), (1 - x, 1 - y)]

        def slot(px, py, pc):
            return out_ref.at[4 * px + 2 * py + pc]

        def copy(k, block, to, src=None):
            return pltpu.make_async_remote_copy(
                src_ref=slot(*block) if src is None else src, dst_ref=slot(*block),
                send_sem=send_sems.at[k], recv_sem=recv_sems.at[k], device_id=to, device_id_type=MESH)

        mine = pltpu.make_async_copy(x_ref, slot(*me), local_sem)
        mine.start()
        first = [copy(0, me, sibling, src=x_ref)]
        first += [copy(1 + j, me, (*chip, c), src=x_ref) for j, chip in enumerate(chips)]
        for cp in first:
            cp.start()
        passed = [copy(4 + j, (*chip, c), sibling) for j, chip in enumerate(chips)]
        for j, chip in enumerate(chips):
            copy(1 + j, (*chip, c), me).wait_recv()
            passed[j].start()
        copy(0, sibling, me).wait_recv()
        for j, chip in enumerate(chips):
            copy(4 + j, (*chip, 1 - c), me).wait_recv()
        for cp in first + passed:
            cp.wait_send()
        mine.wait()
        if reduce_sum:
            acc = out_ref[0]
            for p in range(1, N_DEV):
                acc = acc + out_ref[p]
            tot_ref[...] = acc

    out_shape = [jax.ShapeDtypeStruct((N_DEV, m, n), blk.dtype)]
    out_specs = [pl.BlockSpec(memory_space=space)]
    if reduce_sum:
        out_shape.append(jax.ShapeDtypeStruct((m, n), blk.dtype))
        out_specs.append(pl.BlockSpec(memory_space=pltpu.VMEM))
    res = pl.pallas_call(
        body, name=name,
        out_shape=out_shape,
        in_specs=[pl.BlockSpec(memory_space=space)],
        out_specs=out_specs,
        scratch_shapes=[pltpu.SemaphoreType.DMA((7,)), pltpu.SemaphoreType.DMA((7,)), pltpu.SemaphoreType.DMA],
    )(blk)
    return res if reduce_sum else res[0]


def _rs_sibling(g5, *, name):
    _, _, R, n = g5.shape

    def body(g_ref, r_ref, send_sems, recv_sems):
        x, y, c = _my_place()
        copies = [pltpu.make_async_remote_copy(
            src_ref=g_ref.at[k, 1 - c], dst_ref=r_ref.at[k], send_sem=send_sems.at[k], recv_sem=recv_sems.at[k],
            device_id=(x, y, 1 - c), device_id_type=MESH) for k in range(4)]
        for cp in copies:
            cp.start()
        for cp in copies:
            cp.wait_recv()
        for cp in copies:
            cp.wait_send()

    return pl.pallas_call(
        body, name=name,
        out_shape=jax.ShapeDtypeStruct((4, R, n), g5.dtype),
        in_specs=[ANY], out_specs=ANY,
        scratch_shapes=[pltpu.SemaphoreType.DMA((4,)), pltpu.SemaphoreType.DMA((4,))],
    )(g5)


def _rs_chips(p1, *, name):
    _, R, n = p1.shape

    def body(p_ref, r_ref, send_sems, recv_sems):
        x, y, c = _my_place()
        chips = [(1 - x, y), (x, 1 - y), (1 - x, 1 - y)]
        copies = [pltpu.make_async_remote_copy(
            src_ref=p_ref.at[2 * cx + cy], dst_ref=r_ref.at[j], send_sem=send_sems.at[j], recv_sem=recv_sems.at[j],
            device_id=(cx, cy, c), device_id_type=MESH) for j, (cx, cy) in enumerate(chips)]
        for cp in copies:
            cp.start()
        for cp in copies:
            cp.wait_recv()
        for cp in copies:
            cp.wait_send()

    return pl.pallas_call(
        body, name=name,
        out_shape=jax.ShapeDtypeStruct((3, R, n), p1.dtype),
        in_specs=[ANY], out_specs=ANY,
        scratch_shapes=[pltpu.SemaphoreType.DMA((3,)), pltpu.SemaphoreType.DMA((3,))],
    )(p1)


def _add_sibling(g5, r1, core, *, name):
    _, _, R, n = g5.shape
    tr = _pick(R, (296, 128, 64, 8))

    def body(c_ref, g_ref, r_ref, o_ref):
        o_ref[...] = g_ref[...] + r_ref[...]

    return pl.pallas_call(
        body, name=name,
        grid_spec=pltpu.PrefetchScalarGridSpec(
            num_scalar_prefetch=1, grid=(4, R // tr),
            in_specs=[pl.BlockSpec((None, None, tr, n), lambda k, i, c: (k, c[0], i, 0)),
                      pl.BlockSpec((None, tr, n), lambda k, i, c: (k, i, 0))],
            out_specs=pl.BlockSpec((None, tr, n), lambda k, i, c: (k, i, 0))),
        out_shape=jax.ShapeDtypeStruct((4, R, n), F32),
        compiler_params=_params("parallel", "parallel"),
    )(core, g5, r1)


def _adamw_math(w, g, m, v):
    m = ADAM_B1 * m + (1.0 - ADAM_B1) * g
    v = ADAM_B2 * v + (1.0 - ADAM_B2) * (g * g)
    m_hat = m / (1.0 - ADAM_B1 ** ADAM_STEP)
    v_hat = v / (1.0 - ADAM_B2 ** ADAM_STEP)
    delta = -ADAM_LR * (m_hat / (jnp.sqrt(v_hat) + ADAM_EPS) + ADAM_WD * w)
    return delta, m, v


def _adamw_big(p1, r2, chip, w, m, v, *, name):
    R, n = w.shape
    tr = _pick(R, (296, 128, 64, 8))

    def body(c_ref, p_ref, r_ref, w_ref, m_ref, v_ref, g_ref, d_ref, mo_ref, vo_ref):
        g = p_ref[...] + r_ref[0] + r_ref[1] + r_ref[2]
        d, mn, vn = _adamw_math(w_ref[...], g, m_ref[...], v_ref[...])
        g_ref[...] = g
        d_ref[...] = d
        mo_ref[...] = mn
        vo_ref[...] = vn

    row = pl.BlockSpec((tr, n), lambda i, c: (i, 0))
    return pl.pallas_call(
        body, name=name,
        grid_spec=pltpu.PrefetchScalarGridSpec(
            num_scalar_prefetch=1, grid=(R // tr,),
            in_specs=[pl.BlockSpec((None, tr, n), lambda i, c: (c[0], i, 0)),
                      pl.BlockSpec((3, tr, n), lambda i, c: (0, i, 0)), row, row, row],
            out_specs=[row, row, row, row]),
        out_shape=[jax.ShapeDtypeStruct((R, n), F32)] * 4,
        compiler_params=_params("parallel"),
    )(chip, p1, r2, w, m, v)


def _adamw_small(g, w, m, v, *, name):
    def body(g_ref, w_ref, m_ref, v_ref, d_ref, mo_ref, vo_ref):
        d, mn, vn = _adamw_math(w_ref[...], g_ref[...], m_ref[...], v_ref[...])
        d_ref[...] = d
        mo_ref[...] = mn
        vo_ref[...] = vn

    return pl.pallas_call(body, name=name, out_shape=[jax.ShapeDtypeStruct(w.shape, F32)] * 3)(g, w, m, v)


def _lb_bwd(dlb, lb, *, name):
    def body(d_ref, lb_ref, o_ref):
        t = d_ref[...] * lb_ref[...] * (1.0 - lb_ref[...])
        o_ref[0:1, :] = t
        o_ref[1:2, :] = -t

    return pl.pallas_call(body, name=name, out_shape=jax.ShapeDtypeStruct((2, lb.shape[1]), F32))(dlb, lb)


LANES = 128
N_IN, N_FFN = 7424, 5632
IN_BLK, FFN_BLK, DOWN_BLK, ROW_BLK = N_IN // N_DEV, N_FFN // N_DEV, D_FF // N_DEV, D_MODEL // N_DEV
CONVW_BLK = D_FF // N_DEV
PACK_ROWS = IN_BLK + 3 * ROW_BLK + FFN_BLK + DOWN_BLK
SMALL_NAMES = ("norm1_g", "lb_logits", "hgrn_norm_g", "attn_sinks", "norm2_g", "conv_b", "final_g")


def _pack_shards(w_in, w_a, w_b, w_out, w_ffn_in, w_down):
    return jnp.concatenate([w_in.reshape(IN_BLK, D_MODEL), w_a.reshape(ROW_BLK, D_MODEL), w_b.reshape(ROW_BLK, D_MODEL),
                            w_out.reshape(ROW_BLK, D_MODEL), w_ffn_in.reshape(FFN_BLK, D_MODEL), w_down.reshape(DOWN_BLK, D_MODEL)], axis=0)


def _unpack_shards(p):
    o = [0, IN_BLK, IN_BLK + ROW_BLK, IN_BLK + 2 * ROW_BLK, IN_BLK + 3 * ROW_BLK, IN_BLK + 3 * ROW_BLK + FFN_BLK, PACK_ROWS]
    return (p[o[0]:o[1]].reshape(1, D_MODEL, IN_BLK), p[o[1]:o[2]].reshape(1, ROW_BLK, D_MODEL), p[o[2]:o[3]].reshape(1, ROW_BLK, D_MODEL),
            p[o[3]:o[4]].reshape(1, ROW_BLK, D_MODEL), p[o[4]:o[5]].reshape(1, D_MODEL, FFN_BLK), p[o[5]:o[6]].reshape(1, DOWN_BLK, D_MODEL))


def _cols_from_blocks(gathered, lo, width):
    rows = width * D_MODEL // D_MODEL
    blk = gathered[:, lo:lo + rows].reshape(N_DEV, D_MODEL, width)
    return blk.transpose(1, 0, 2).reshape(D_MODEL, N_DEV * width)


def _blocks_from_cols(full, width):
    return full.reshape(D_MODEL, N_DEV, width).transpose(1, 0, 2).reshape(N_DEV, width, D_MODEL)


def _to_rows(vec, rows):
    vec = vec.reshape(-1)
    return jnp.pad(vec, (0, rows * LANES - vec.shape[0])).reshape(rows, LANES)


def kernel(x, positions, norm1_g, w_in, lb_logits, hgrn_norm_g, w_a, attn_sinks, w_b, w_out, norm2_g, w_ffn_in, conv_w, conv_b, w_down, final_g, loss_target, m_norm1_g, m_w_in, m_lb_logits, m_hgrn_norm_g, m_w_a, m_attn_sinks, m_w_b, m_w_out, m_norm2_g, m_w_ffn_in, m_conv_w, m_conv_b, m_w_down, m_final_g, v_norm1_g, v_w_in, v_lb_logits, v_hgrn_norm_g, v_w_a, v_attn_sinks, v_w_b, v_w_out, v_norm2_g, v_w_ffn_in, v_conv_w, v_conv_b, v_w_down, v_final_g):
    xi, yi, ci = _my_place()
    dev = 4 * xi + 2 * yi + ci
    core = ci.astype(jnp.int32).reshape(1)
    chip = (2 * xi + yi).astype(jnp.int32).reshape(1)

    packed_w = _pack_shards(w_in, w_a, w_b, w_out, w_ffn_in, w_down)
    gathered = _all_gather(packed_w.astype(BF16), in_vmem=False, name="ag_weights")
    o_a = IN_BLK
    o_ffn = IN_BLK + 3 * ROW_BLK
    o_down = o_ffn + FFN_BLK
    w_in_full = _cols_from_blocks(gathered, 0, IN_BLK)
    conv_rows = 16
    conv_all = _all_gather(_to_rows(conv_w, conv_rows), in_vmem=True, name="ag_conv_w")
    conv_full = conv_all.reshape(N_DEV, conv_rows * LANES)[:, :3 * CONVW_BLK].reshape(N_DEV, 3, CONVW_BLK).transpose(1, 0, 2).reshape(3, D_FF)
    W = dict(
        w_in=_reorder_w_in(w_in_full),
        w_a=gathered[:, o_a:o_a + ROW_BLK].reshape(D_MODEL, D_MODEL),
        w_b=gathered[:, o_a + ROW_BLK:o_a + 2 * ROW_BLK].reshape(D_MODEL, D_MODEL),
        w_out=gathered[:, o_a + 2 * ROW_BLK:o_a + 3 * ROW_BLK].reshape(D_MODEL, D_MODEL),
        w_ffn=_cols_from_blocks(gathered, o_ffn, FFN_BLK),
        w_down=gathered[:, o_down:o_down + DOWN_BLK].reshape(D_FF, D_MODEL),
        conv_w=conv_full,
    )
    small = dict(norm1_g=norm1_g, lb_logits=lb_logits, hgrn_norm_g=hgrn_norm_g, attn_sinks=attn_sinks, norm2_g=norm2_g,
                 conv_b=conv_b, final_g=final_g)

    loss, grad_x, g = _local_step(x, positions, loss_target, small, W)

    dw_in = jnp.concatenate([g["w_h"], g["w_aq"], g["w_akv"], g["w_g"]], axis=1)
    dw_ffn = jnp.concatenate([g["w_ffn_g"], g["w_ffn_u"]], axis=1)
    gp = jnp.concatenate([
        _blocks_from_cols(dw_in, IN_BLK), g["w_a"].reshape(N_DEV, ROW_BLK, D_MODEL), g["w_b"].reshape(N_DEV, ROW_BLK, D_MODEL),
        g["w_out"].reshape(N_DEV, ROW_BLK, D_MODEL), _blocks_from_cols(dw_ffn, FFN_BLK), g["w_down"].reshape(N_DEV, DOWN_BLK, D_MODEL)], axis=1)
    g5 = gp.reshape(4, 2, PACK_ROWS, D_MODEL)
    r1 = _rs_sibling(g5, name="rs_sibling")
    p1 = _add_sibling(g5, r1, core, name="rs_add_sibling")
    r2 = _rs_chips(p1, name="rs_chips")
    packed_m = _pack_shards(m_w_in, m_w_a, m_w_b, m_w_out, m_w_ffn_in, m_w_down)
    packed_v = _pack_shards(v_w_in, v_w_a, v_w_b, v_w_out, v_w_ffn_in, v_w_down)
    big = [_unpack_shards(t) for t in _adamw_big(p1, r2, chip, packed_w, packed_m, packed_v, name="adamw_big")]

    dlogits = _lb_bwd(g["lb"], g["lb_fwd"], name="lb_bwd")
    sm_g = dict(norm1_g=g["norm1_g"], lb_logits=dlogits, hgrn_norm_g=g["hgrn_norm_g"], attn_sinks=g["attn_sinks"][:, 0],
                norm2_g=g["norm2_g"], conv_b=g["conv_b"], final_g=g["final_g"])
    vec = jnp.concatenate([sm_g[n].reshape(-1) for n in SMALL_NAMES] + [g["conv_w"].reshape(-1), loss.reshape(-1)])
    sm_rows = 136
    _, total = _all_gather(_to_rows(vec, sm_rows), in_vmem=True, reduce_sum=True, name="ar_small")
    total = total.reshape(-1)
    sm_w = dict(norm1_g=norm1_g, lb_logits=lb_logits, hgrn_norm_g=hgrn_norm_g, attn_sinks=attn_sinks, norm2_g=norm2_g,
                conv_b=conv_b, final_g=final_g)
    sm_m = dict(norm1_g=m_norm1_g, lb_logits=m_lb_logits, hgrn_norm_g=m_hgrn_norm_g, attn_sinks=m_attn_sinks, norm2_g=m_norm2_g,
                conv_b=m_conv_b, final_g=m_final_g)
    sm_v = dict(norm1_g=v_norm1_g, lb_logits=v_lb_logits, hgrn_norm_g=v_hgrn_norm_g, attn_sinks=v_attn_sinks, norm2_g=v_norm2_g,
                conv_b=v_conv_b, final_g=v_final_g)
    sizes = [sm_w[n].size for n in SMALL_NAMES]
    n_rep = sum(sizes)
    g_conv_full = total[n_rep:n_rep + 3 * D_FF].reshape(3, D_FF)
    g_conv = lax.dynamic_slice_in_dim(g_conv_full, dev * CONVW_BLK, CONVW_BLK, axis=1)
    loss_total = total[n_rep + 3 * D_FF]
    ad_rows = 72
    pack_small = lambda d, cw: _to_rows(jnp.concatenate([d[n].reshape(-1) for n in SMALL_NAMES] + [cw.reshape(-1)]), ad_rows)
    g_small = _to_rows(jnp.concatenate([total[:n_rep], g_conv.reshape(-1)]), ad_rows)
    d_s, m_s, v_s = _adamw_small(g_small, pack_small(sm_w, conv_w), pack_small(sm_m, m_conv_w), pack_small(sm_v, v_conv_w), name="adamw_small")

    def unpack_small(t):
        t = t.reshape(-1)
        out, off = {}, 0
        for n, s in zip(SMALL_NAMES, sizes):
            out[n] = t[off:off + s].reshape(sm_w[n].shape)
            off += s
        out["conv_w"] = t[off:off + 3 * CONVW_BLK].reshape(1, 3, CONVW_BLK)
        return out

    names = ("norm1_g", "w_in", "lb_logits", "hgrn_norm_g", "w_a", "attn_sinks", "w_b", "w_out", "norm2_g", "w_ffn_in", "conv_w", "conv_b", "w_down", "final_g")
    big_names = ("w_in", "w_a", "w_b", "w_out", "w_ffn_in", "w_down")
    outs = [loss_total.reshape(()), grad_x]
    for kind, s_vec in zip(range(4), (g_small, d_s, m_s, v_s)):
        s_un = unpack_small(s_vec)
        b_un = dict(zip(big_names, big[kind]))
        outs += [b_un[n] if n in b_un else s_un[n] for n in names]
    return tuple(outs)
```

```python
import functools

import jax
import jax.numpy as jnp
from jax import lax
from jax.experimental import pallas as pl
from jax.experimental.pallas import tpu as pltpu

F32 = jnp.float32
BF16 = jnp.bfloat16

D_MODEL = 1024
HGRN_HEADS = 8
HGRN_DK = 128
CHUNK = 64
ATT_HEADS = 16
ATT_KV_HEADS = 2
ATT_HD = 64
ATT_GROUP = ATT_HEADS // ATT_KV_HEADS
WINDOW = 128
ROPE_DIM = ATT_HD // 4
ROPE_THETA = 500000.0
D_FF = 2816
EPS = 1e-6
NEG_INF = -1e30
N_DEV = 8

ADAM_LR = 0.001
ADAM_B1 = 0.9
ADAM_B2 = 0.999
ADAM_EPS = 1e-08
ADAM_WD = 0.01
ADAM_STEP = 10

MESH = pl.DeviceIdType.MESH
ANY = pl.BlockSpec(memory_space=pl.ANY)


def _pick(n, cands):
    for c in cands:
        if n % c == 0:
            return c
    return n


def _sigmoid(x):
    return 1.0 / (1.0 + jnp.exp(-x))


def _silu(x):
    return x * _sigmoid(x)


def _rms(x, g):
    return x * lax.rsqrt(jnp.mean(x * x, axis=-1, keepdims=True) + EPS) * g


def _dot(a, b, dims):
    return lax.dot_general(a, b, (dims, ((), ())), preferred_element_type=F32)


def _nn(a, b):
    return _dot(a, b, ((1,), (0,)))


def _nt(a, b):
    return _dot(a, b, ((1,), (1,)))


def _tn(a, b):
    return _dot(a, b, ((0,), (0,)))


def _params(*sem):
    return pltpu.CompilerParams(dimension_semantics=sem, vmem_limit_bytes=56 * 1024 * 1024)


def _matmul(a, b, *, ta=False, tb=False, out_dtype=F32, addend=None, name, tm, tn, tk=None, n_extent=None, b_koff=0, b_noff=0):
    M, K = (a.shape[1], a.shape[0]) if ta else a.shape
    N = n_extent or (b.shape[0] if tb else b.shape[1])
    tm, tn, tk = min(tm, M), min(tn, N), min(tk or K, K)
    assert M % tm == 0 and N % tn == 0 and K % tk == 0, (name, M, N, K, tm, tn, tk)
    nk = K // tk
    assert nk == 1 or out_dtype == F32
    grid = (M // tm, N // tn, nk)
    a_spec = pl.BlockSpec((tk, tm), lambda i, j, k: (k, i)) if ta else pl.BlockSpec((tm, tk), lambda i, j, k: (i, k))
    b_spec = pl.BlockSpec((tn, tk), lambda i, j, k: (j + b_noff, k + b_koff)) if tb else pl.BlockSpec((tk, tn), lambda i, j, k: (k + b_koff, j + b_noff))
    o_spec = pl.BlockSpec((tm, tn), lambda i, j, k: (i, j))
    dims = ((0 if ta else 1,), (1 if tb else 0,))
    has_add = addend is not None

    def body(*refs):
        if has_add:
            a_ref, b_ref, c_ref, o_ref = refs
        else:
            a_ref, b_ref, o_ref = refs
            c_ref = None
        part = _dot(a_ref[...], b_ref[...], dims)
        if nk == 1:
            if has_add:
                part = part + c_ref[...].astype(F32)
            o_ref[...] = part.astype(out_dtype)
        else:
            k = pl.program_id(2)

            @pl.when(k == 0)
            def _():
                o_ref[...] = part + c_ref[...].astype(F32) if has_add else part

            @pl.when(k > 0)
            def _():
                o_ref[...] += part

    in_specs = [a_spec, b_spec] + ([o_spec] if has_add else [])
    args = (a, b) + ((addend,) if has_add else ())
    return pl.pallas_call(
        body,
        name=name,
        grid=grid,
        in_specs=in_specs,
        out_specs=o_spec,
        out_shape=jax.ShapeDtypeStruct((M, N), out_dtype),
        compiler_params=_params("parallel", "parallel", "arbitrary"),
    )(*args)


def _row_spec(tm, n):
    return pl.BlockSpec((tm, n), lambda i: (i, 0))


def _full_spec(shape):
    return pl.BlockSpec(shape, lambda i: tuple(0 for _ in shape))


def _norm_cast(x, g, *, name):
    T, D = x.shape
    tm = _pick(T, (512, 256, 128))

    def body(x_ref, g_ref, u_ref):
        u_ref[...] = _rms(x_ref[...], g_ref[...]).astype(BF16)

    return pl.pallas_call(
        body, name=name, grid=(T // tm,),
        in_specs=[_row_spec(tm, D), _full_spec((1, D))],
        out_specs=_row_spec(tm, D),
        out_shape=jax.ShapeDtypeStruct((T, D), BF16),
        compiler_params=_params("parallel"),
    )(x, g)


def _norm_bwd_add(x, g, du, dres, *, name):
    T, D = x.shape
    tm = _pick(T, (512, 256, 128))
    has_res = dres is not None

    def body(*refs):
        if has_res:
            x_ref, g_ref, du_ref, dr_ref, dx_ref, dxb_ref, dg_ref = refs
        else:
            x_ref, g_ref, du_ref, dx_ref, dxb_ref, dg_ref = refs
        _, vjp = jax.vjp(_rms, x_ref[...], g_ref[...])
        dx, dg = vjp(du_ref[...].astype(F32))
        if has_res:
            dx = dx + dr_ref[...]
        dx_ref[...] = dx
        dxb_ref[...] = dx.astype(BF16)

        @pl.when(pl.program_id(0) == 0)
        def _():
            dg_ref[...] = jnp.zeros_like(dg_ref)

        dg_ref[...] += dg

    ins = [x, g, du] + ([dres] if has_res else [])
    in_specs = [_row_spec(tm, D), _full_spec((1, D)), _row_spec(tm, D)] + ([_row_spec(tm, D)] if has_res else [])
    return pl.pallas_call(
        body, name=name, grid=(T // tm,),
        in_specs=in_specs,
        out_specs=[_row_spec(tm, D), _row_spec(tm, D), _full_spec((1, D))],
        out_shape=[jax.ShapeDtypeStruct((T, D), F32), jax.ShapeDtypeStruct((T, D), BF16), jax.ShapeDtypeStruct((1, D), F32)],
        compiler_params=_params("arbitrary"),
    )(*ins)


def _final_loss_bwd(h2, g, target, *, name):
    T, D = h2.shape
    tm = _pick(T, (512, 256, 128))

    def body(h_ref, g_ref, t_ref, dx_ref, dxb_ref, dg_ref, loss_ref):
        y, vjp = jax.vjp(_rms, h_ref[...], g_ref[...])
        err = y - t_ref[...]
        dx, dg = vjp(err * (1.0 / D))
        dx_ref[...] = dx
        dxb_ref[...] = dx.astype(BF16)

        @pl.when(pl.program_id(0) == 0)
        def _():
            dg_ref[...] = jnp.zeros_like(dg_ref)
            loss_ref[...] = jnp.zeros_like(loss_ref)

        dg_ref[...] += dg
        loss_ref[...] += (0.5 / D) * jnp.sum(jnp.sum(err * err, axis=1, keepdims=True), axis=0, keepdims=True)

    return pl.pallas_call(
        body, name=name, grid=(T // tm,),
        in_specs=[_row_spec(tm, D), _full_spec((1, D)), _row_spec(tm, D)],
        out_specs=[_row_spec(tm, D), _row_spec(tm, D), _full_spec((1, D)), _full_spec((1, 1))],
        out_shape=[jax.ShapeDtypeStruct((T, D), F32), jax.ShapeDtypeStruct((T, D), BF16), jax.ShapeDtypeStruct((1, D), F32), jax.ShapeDtypeStruct((1, 1), F32)],
        compiler_params=_params("arbitrary"),
    )(h2, g, target)


def _merge_fn(gates, a, b):
    ga = gates[:, :D_MODEL].astype(F32)
    gb = gates[:, D_MODEL:].astype(F32)
    return _sigmoid(ga) * a.astype(F32) + _sigmoid(gb) * b.astype(F32)


def _merge_fwd(gates, a, b, *, name):
    T = a.shape[0]
    tm = _pick(T, (512, 256, 128))

    def body(g_ref, a_ref, b_ref, o_ref):
        o_ref[...] = _merge_fn(g_ref[...], a_ref[...], b_ref[...]).astype(BF16)

    return pl.pallas_call(
        body, name=name, grid=(T // tm,),
        in_specs=[_row_spec(tm, 2 * D_MODEL), _row_spec(tm, D_MODEL), _row_spec(tm, D_MODEL)],
        out_specs=_row_spec(tm, D_MODEL),
        out_shape=jax.ShapeDtypeStruct((T, D_MODEL), BF16),
        compiler_params=_params("parallel"),
    )(gates, a, b)


def _merge_bwd(gates, a, b, dmerged, *, name):
    T = a.shape[0]
    tm = _pick(T, (512, 256, 128))

    def body(g_ref, a_ref, b_ref, dm_ref, dg_ref, da_ref, db_ref):
        g = g_ref[...].astype(F32)
        dm = dm_ref[...].astype(F32)
        sa = _sigmoid(g[:, :D_MODEL])
        sb = _sigmoid(g[:, D_MODEL:])
        da_ref[...] = (dm * sa).astype(BF16)
        db_ref[...] = (dm * sb).astype(BF16)
        dg_ref[:, :D_MODEL] = (dm * a_ref[...].astype(F32) * sa * (1.0 - sa)).astype(BF16)
        dg_ref[:, D_MODEL:] = (dm * b_ref[...].astype(F32) * sb * (1.0 - sb)).astype(BF16)

    return pl.pallas_call(
        body, name=name, grid=(T // tm,),
        in_specs=[_row_spec(tm, 2 * D_MODEL), _row_spec(tm, D_MODEL), _row_spec(tm, D_MODEL), _row_spec(tm, D_MODEL)],
        out_specs=[_row_spec(tm, 2 * D_MODEL), _row_spec(tm, D_MODEL), _row_spec(tm, D_MODEL)],
        out_shape=[jax.ShapeDtypeStruct((T, 2 * D_MODEL), BF16), jax.ShapeDtypeStruct((T, D_MODEL), BF16), jax.ShapeDtypeStruct((T, D_MODEL), BF16)],
        compiler_params=_params("parallel"),
    )(gates, a, b, dmerged)


CONV_TC = 256


def _shift_down(x, n, rows):
    return jnp.where(rows >= n, pltpu.roll(x, n, 0), 0.0)


def _shift_up(x, n, rows, S):
    return jnp.where(rows < S - n, pltpu.roll(x, S - n, 0), 0.0)


def _conv_act_fwd(gu, conv_w, conv_b, *, name):
    B, S, _ = gu.shape
    tc = CONV_TC
    nc = D_FF // tc

    def body(g_ref, up_ref, w_ref, b_ref, o_ref):
        g = g_ref[...].astype(F32)
        rows = lax.broadcasted_iota(jnp.int32, g.shape, 0)
        w = w_ref[...]
        a = w[2:3] * g + w[1:2] * _shift_down(g, 1, rows) + w[0:1] * _shift_down(g, 2, rows) + b_ref[...]
        o_ref[...] = (_silu(a) * up_ref[...].astype(F32)).astype(BF16)

    return pl.pallas_call(
        body, name=name, grid=(B, nc),
        in_specs=[pl.BlockSpec((None, S, tc), lambda b, j: (b, 0, j)),
                  pl.BlockSpec((None, S, tc), lambda b, j: (b, 0, j + nc)),
                  pl.BlockSpec((3, tc), lambda b, j: (0, j)),
                  pl.BlockSpec((1, tc), lambda b, j: (0, j))],
        out_specs=pl.BlockSpec((None, S, tc), lambda b, j: (b, 0, j)),
        out_shape=jax.ShapeDtypeStruct((B, S, D_FF), BF16),
        compiler_params=_params("parallel", "parallel"),
    )(gu, gu, conv_w, conv_b)


def _conv_act_bwd(gu, conv_w, conv_b, dact, *, name):
    B, S, _ = gu.shape
    tc = CONV_TC
    nc = D_FF // tc

    def body(g_ref, up_ref, w_ref, b_ref, da_ref, dg_ref, dup_ref, dw_ref, db_ref):
        g = g_ref[...].astype(F32)
        up = up_ref[...].astype(F32)
        dact = da_ref[...].astype(F32)
        rows = lax.broadcasted_iota(jnp.int32, g.shape, 0)
        w = w_ref[...]
        g1 = _shift_down(g, 1, rows)
        g2 = _shift_down(g, 2, rows)
        a = w[2:3] * g + w[1:2] * g1 + w[0:1] * g2 + b_ref[...]
        sg = _sigmoid(a)
        dup_ref[...] = (dact * a * sg).astype(BF16)
        da = dact * up * sg * (1.0 + a * (1.0 - sg))
        dg = w[2:3] * da + w[1:2] * _shift_up(da, 1, rows, S) + w[0:1] * _shift_up(da, 2, rows, S)
        dg_ref[...] = dg.astype(BF16)

        @pl.when(pl.program_id(1) == 0)
        def _():
            dw_ref[...] = jnp.zeros_like(dw_ref)
            db_ref[...] = jnp.zeros_like(db_ref)

        dw_ref[0:1, :] += jnp.sum(da * g2, axis=0, keepdims=True)
        dw_ref[1:2, :] += jnp.sum(da * g1, axis=0, keepdims=True)
        dw_ref[2:3, :] += jnp.sum(da * g, axis=0, keepdims=True)
        db_ref[...] += jnp.sum(da, axis=0, keepdims=True)

    col = lambda j, b: (b, 0, j)
    return pl.pallas_call(
        body, name=name, grid=(nc, B),
        in_specs=[pl.BlockSpec((None, S, tc), col),
                  pl.BlockSpec((None, S, tc), lambda j, b: (b, 0, j + nc)),
                  pl.BlockSpec((3, tc), lambda j, b: (0, j)),
                  pl.BlockSpec((1, tc), lambda j, b: (0, j)),
                  pl.BlockSpec((None, S, tc), col)],
        out_specs=[pl.BlockSpec((None, S, tc), col), pl.BlockSpec((None, S, tc), col),
                   pl.BlockSpec((3, tc), lambda j, b: (0, j)), pl.BlockSpec((1, tc), lambda j, b: (0, j))],
        out_shape=[jax.ShapeDtypeStruct((B, S, D_FF), BF16), jax.ShapeDtypeStruct((B, S, D_FF), BF16),
                   jax.ShapeDtypeStruct((3, D_FF), F32), jax.ShapeDtypeStruct((1, D_FF), F32)],
        compiler_params=_params("parallel", "arbitrary"),
    )(gu, gu, conv_w, conv_b, dact)


HGRN_CPB = 4
HF = HGRN_HEADS * HGRN_DK


def _tri(n, upper=False):
    r = lax.broadcasted_iota(jnp.int32, (n, n), 0)
    c = lax.broadcasted_iota(jnp.int32, (n, n), 1)
    return (c >= r) if upper else (r >= c)


def _hs(h):
    return slice(h * HGRN_DK, (h + 1) * HGRN_DK)


def _cumsum_rows(tri_b, x):
    hi = x.astype(BF16)
    lo = (x - hi.astype(F32)).astype(BF16)
    return _nn(tri_b, hi) + _nn(tri_b, lo)


def _hgrn_pre(q, fz, lb, tril_b):
    qf = _silu(q)
    sg = _sigmoid(fz)
    f = lb + (1.0 - lb) * sg
    k = 1.0 - f
    b = _cumsum_rows(tril_b, jnp.log(f))
    bref = b[CHUNK // 2:CHUNK // 2 + 1, :]
    blast = b[CHUNK - 1:CHUNK, :]
    e1 = jnp.exp(b - bref)
    e2 = jnp.exp(bref - b)
    e3 = jnp.exp(b)
    e4 = jnp.exp(blast - b)
    dec = jnp.exp(blast)
    return sg, f, (e1, e2, e3, e4), qf * e1, k * e2, qf * e3, k * e4, dec


def _hgrn_fwd(zh, lb, gn, *, name):
    B, S, _ = zh.shape
    cpb = HGRN_CPB
    ts = cpb * CHUNK
    nblk = S // ts

    def body(z_ref, lb_ref, gn_ref, o_ref, st_ref, state):
        @pl.when(pl.program_id(1) == 0)
        def _():
            state[...] = jnp.zeros_like(state)

        H = HGRN_HEADS
        causal = _tri(CHUNK)
        tril_b = causal.astype(BF16)
        lb = lb_ref[...]
        for c in range(cpb):
            rows = slice(c * CHUNK, (c + 1) * CHUNK)
            q = z_ref[rows, 0:HF].astype(F32)
            fz = z_ref[rows, HF:2 * HF].astype(F32)
            v = z_ref[rows, 2 * HF:3 * HF]
            hg = z_ref[rows, 3 * HF:4 * HF].astype(F32)
            _, _, _, q_in, k_in, q_out, k_st, dec = _hgrn_pre(q, fz, lb, tril_b)
            q_in, k_in, q_out, k_st = (t.astype(BF16) for t in (q_in, k_in, q_out, k_st))
            a = [jnp.where(causal, _nt(q_in[:, _hs(h)], k_in[:, _hs(h)]), 0.0).astype(BF16) for h in range(H)]
            st = [state[h] for h in range(H)]
            for h in range(H):
                st_ref[c, h] = st[h]
            o = [_nn(a[h], v[:, _hs(h)]) + _nt(q_out[:, _hs(h)], st[h].astype(BF16)) for h in range(H)]
            for h in range(H):
                state[h] = st[h] * dec[:, _hs(h)] + _tn(v[:, _hs(h)], k_st[:, _hs(h)])
            gate = _silu(hg)
            for h in range(H):
                o_ref[rows, _hs(h)] = (_rms(o[h], gn_ref[...]) * gate[:, _hs(h)]).astype(BF16)

    return pl.pallas_call(
        body, name=name, grid=(B, nblk),
        in_specs=[pl.BlockSpec((None, ts, 4 * HF), lambda b, s: (b, s, 0)),
                  pl.BlockSpec((1, HF), lambda b, s: (0, 0)),
                  pl.BlockSpec((1, HGRN_DK), lambda b, s: (0, 0))],
        out_specs=[pl.BlockSpec((None, ts, HF), lambda b, s: (b, s, 0)),
                   pl.BlockSpec((None, cpb, HGRN_HEADS, HGRN_DK, HGRN_DK), lambda b, s: (b, s, 0, 0, 0))],
        out_shape=[jax.ShapeDtypeStruct((B, S, HF), BF16),
                   jax.ShapeDtypeStruct((B, S // CHUNK, HGRN_HEADS, HGRN_DK, HGRN_DK), F32)],
        scratch_shapes=[pltpu.VMEM((HGRN_HEADS, HGRN_DK, HGRN_DK), F32)],
        compiler_params=_params("arbitrary", "arbitrary"),
    )(zh, lb, gn)


def _hgrn_bwd(zh, lb, gn, states, doa, *, name):
    B, S, _ = zh.shape
    cpb = HGRN_CPB
    ts = cpb * CHUNK
    nblk = S // ts
    rev = lambda b, s: (b, nblk - 1 - s, 0)

    def body(z_ref, lb_ref, gn_ref, st_ref, do_ref, dz_ref, dlb_ref, dgn_ref, dstate):
        @pl.when(pl.program_id(1) == 0)
        def _():
            dstate[...] = jnp.zeros_like(dstate)

        @pl.when((pl.program_id(0) == 0) & (pl.program_id(1) == 0))
        def _():
            dlb_ref[...] = jnp.zeros_like(dlb_ref)
            dgn_ref[...] = jnp.zeros_like(dgn_ref)

        H = HGRN_HEADS
        cat = lambda xs: jnp.concatenate(xs, axis=1)
        causal = _tri(CHUNK)
        tril_b = causal.astype(BF16)
        triu_b = _tri(CHUNK, upper=True).astype(BF16)
        rowid = lax.broadcasted_iota(jnp.int32, (CHUNK, HF), 0)
        lb = lb_ref[...]
        gn = gn_ref[...]
        for c in reversed(range(cpb)):
            rows = slice(c * CHUNK, (c + 1) * CHUNK)
            q = z_ref[rows, 0:HF].astype(F32)
            fz = z_ref[rows, HF:2 * HF].astype(F32)
            v = z_ref[rows, 2 * HF:3 * HF]
            hg = z_ref[rows, 3 * HF:4 * HF].astype(F32)
            sg, f, (e1, e2, e3, e4), q_in, k_in, q_out, k_st, dec = _hgrn_pre(q, fz, lb, tril_b)
            q_in_b, k_in_b, q_out_b, k_st_b = (t.astype(BF16) for t in (q_in, k_in, q_out, k_st))
            a_b = [jnp.where(causal, _nt(q_in_b[:, _hs(h)], k_in_b[:, _hs(h)]), 0.0).astype(BF16) for h in range(H)]
            st = [st_ref[c, h] for h in range(H)]
            st_b = [t.astype(BF16) for t in st]
            o = [_nn(a_b[h], v[:, _hs(h)]) + _nt(q_out_b[:, _hs(h)], st_b[h]) for h in range(H)]
            dout = do_ref[rows, :].astype(F32)
            shg = _sigmoid(hg)
            gate = hg * shg
            do_l, dgn_acc = [], jnp.zeros_like(gn)
            for h in range(H):
                _, norm_vjp = jax.vjp(_rms, o[h], gn)
                d_o, d_gn = norm_vjp(dout[:, _hs(h)] * gate[:, _hs(h)])
                do_l.append(d_o)
                dgn_acc = dgn_acc + d_gn
            dgn_ref[...] += dgn_acc
            on = cat([_rms(o[h], gn) for h in range(H)])
            dhg = dout * on * shg * (1.0 + hg * (1.0 - shg))
            do_b = [t.astype(BF16) for t in do_l]
            dst = [dstate[h] for h in range(H)]
            dst_b = [t.astype(BF16) for t in dst]
            da_b = [jnp.where(causal, _nt(do_b[h], v[:, _hs(h)]), 0.0).astype(BF16) for h in range(H)]
            dv = cat([_tn(a_b[h], do_b[h]) + _nt(k_st_b[:, _hs(h)], dst_b[h]) for h in range(H)])
            dq_in = cat([_nn(da_b[h], k_in_b[:, _hs(h)]) for h in range(H)])
            dk_in = cat([_tn(da_b[h], q_in_b[:, _hs(h)]) for h in range(H)])
            dq_out = cat([_nn(do_b[h], st_b[h]) for h in range(H)])
            dk_st = cat([_nn(v[:, _hs(h)], dst_b[h]) for h in range(H)])
            ddec = cat([jnp.sum(st[h] * dst[h], axis=0, keepdims=True) for h in range(H)])
            for h in range(H):
                dstate[h] = dst[h] * dec[:, _hs(h)] + _tn(do_b[h], q_out_b[:, _hs(h)])
            t_qin = dq_in * q_in
            t_kin = dk_in * k_in
            t_kst = dk_st * k_st
            db = t_qin - t_kin + dq_out * q_out - t_kst
            dbref = jnp.sum(t_kin - t_qin, axis=0, keepdims=True)
            dblast = jnp.sum(t_kst, axis=0, keepdims=True) + ddec * dec
            db = db + jnp.where(rowid == CHUNK // 2, dbref, 0.0) + jnp.where(rowid == CHUNK - 1, dblast, 0.0)
            dlogf = _cumsum_rows(triu_b, db)
            dqf = dq_in * e1 + dq_out * e3
            dk = dk_in * e2 + dk_st * e4
            df = dlogf / f - dk
            dfz = df * (1.0 - lb) * sg * (1.0 - sg)
            dlb_ref[...] += jnp.sum(df * (1.0 - sg), axis=0, keepdims=True)
            sq = _sigmoid(q)
            dq = dqf * sq * (1.0 + q * (1.0 - sq))
            dz_ref[rows, 0:HF] = dq.astype(BF16)
            dz_ref[rows, HF:2 * HF] = dfz.astype(BF16)
            dz_ref[rows, 2 * HF:3 * HF] = dv.astype(BF16)
            dz_ref[rows, 3 * HF:4 * HF] = dhg.astype(BF16)

    return pl.pallas_call(
        body, name=name, grid=(B, nblk),
        in_specs=[pl.BlockSpec((None, ts, 4 * HF), rev),
                  pl.BlockSpec((1, HF), lambda b, s: (0, 0)),
                  pl.BlockSpec((1, HGRN_DK), lambda b, s: (0, 0)),
                  pl.BlockSpec((None, cpb, HGRN_HEADS, HGRN_DK, HGRN_DK), lambda b, s: (b, nblk - 1 - s, 0, 0, 0)),
                  pl.BlockSpec((None, ts, HF), rev)],
        out_specs=[pl.BlockSpec((None, ts, 4 * HF), rev),
                   pl.BlockSpec((1, HF), lambda b, s: (0, 0)),
                   pl.BlockSpec((1, HGRN_DK), lambda b, s: (0, 0))],
        out_shape=[jax.ShapeDtypeStruct((B, S, 4 * HF), BF16),
                   jax.ShapeDtypeStruct((1, HF), F32),
                   jax.ShapeDtypeStruct((1, HGRN_DK), F32)],
        scratch_shapes=[pltpu.VMEM((HGRN_HEADS, HGRN_DK, HGRN_DK), F32)],
        compiler_params=_params("arbitrary", "arbitrary"),
    )(zh, lb, gn, states, doa)


KV_W = ATT_KV_HEADS * ATT_HD
ATT_SCALE = ATT_HD ** -0.5


def _rope(x, cos, sin, inverse=False):
    half = ROPE_DIM // 2
    outs = []
    for p in range(x.shape[1] // 128):
        xp = x[:, p * 128:(p + 1) * 128]
        lane = lax.broadcasted_iota(jnp.int32, xp.shape, 1) % ATT_HD
        sw = jnp.where(lane < half, pltpu.roll(xp, 128 - half, 1), pltpu.roll(xp, half, 1))
        outs.append(xp * cos - sw * sin if inverse else xp * cos + sw * sin)
    return outs[0] if len(outs) == 1 else jnp.concatenate(outs, axis=1)


PAIRS_PER_KV = ATT_GROUP // 2


def _swap_halves(x):
    return pltpu.roll(x, ATT_HD, 1)


def _kv_padded(t, low):
    sw = _swap_halves(t)
    zero = jnp.zeros_like(t)
    out = []
    for g in range(ATT_KV_HEADS):
        in_low, in_high = (t, sw) if g == 0 else (sw, t)
        out.append((jnp.where(low, in_low, zero).astype(BF16), jnp.where(low, zero, in_high).astype(BF16)))
    return out


def _swa_mask(first_block):
    qi = lax.broadcasted_iota(jnp.int32, (WINDOW, 2 * WINDOW), 0)
    mi = lax.broadcasted_iota(jnp.int32, (WINDOW, 2 * WINDOW), 1)
    band = (mi > qi) & (mi <= qi + WINDOW)
    return band & (jnp.logical_not(first_block) | (mi >= WINDOW))


def _swa_specs(nb):
    cur = lambda b, i: (b, i, 0)
    prev = lambda b, i: (b, jnp.maximum(i - 1, 0), 0)
    return cur, prev


def _swa_fwd(aq, akv, cos, sin, sinks, *, name):
    B, S, _ = aq.shape
    nb = S // WINDOW
    cur, prev = _swa_specs(nb)

    def body(q_ref, kvp_ref, kvc_ref, cp_ref, sp_ref, cc_ref, sc_ref, sink_ref, o_ref, lse_ref):
        cos_c, sin_c = cc_ref[...], sc_ref[...]
        q = (_rope(q_ref[...].astype(F32), cos_c, sin_c) * ATT_SCALE).astype(BF16)
        k = jnp.concatenate([_rope(kvp_ref[:, :KV_W].astype(F32), cp_ref[...], sp_ref[...]),
                             _rope(kvc_ref[:, :KV_W].astype(F32), cos_c, sin_c)], axis=0)
        v = jnp.concatenate([kvp_ref[:, KV_W:], kvc_ref[:, KV_W:]], axis=0).astype(F32)
        low = lax.broadcasted_iota(jnp.int32, k.shape, 1) < ATT_HD
        kpad = _kv_padded(k, low)
        vpad = _kv_padded(v, low)
        mask = _swa_mask(pl.program_id(1) == 0)
        lses = []
        for g in range(ATT_KV_HEADS):
            pairs = range(g * PAIRS_PER_KV, (g + 1) * PAIRS_PER_KV)
            keys = [(p, e) for p in pairs for e in (0, 1)]
            qp = {p: q[:, p * 128:(p + 1) * 128] for p in pairs}
            s = {pe: jnp.where(mask, _nt(qp[pe[0]], kpad[g][pe[1]]), NEG_INF) for pe in keys}
            pr = {}
            for pe in keys:
                sink = sink_ref[0, 2 * pe[0] + pe[1]]
                m = jnp.maximum(jnp.max(s[pe], axis=1, keepdims=True), sink)
                ex = jnp.exp(s[pe] - m)
                den = jnp.sum(ex, axis=1, keepdims=True) + jnp.exp(sink - m)
                pr[pe] = (ex * (1.0 / den)).astype(BF16)
                lses.append(m + jnp.log(den))
            for p in pairs:
                o_ref[:, p * 128:(p + 1) * 128] = (_nn(pr[p, 0], vpad[g][0]) + _nn(pr[p, 1], vpad[g][1])).astype(BF16)
        lse_ref[...] = jnp.concatenate(lses, axis=1)

    tab = lambda im: pl.BlockSpec((None, WINDOW, 128), im)
    return pl.pallas_call(
        body, name=name, grid=(B, nb),
        in_specs=[pl.BlockSpec((None, WINDOW, D_MODEL), cur),
                  pl.BlockSpec((None, WINDOW, 2 * KV_W), prev), pl.BlockSpec((None, WINDOW, 2 * KV_W), cur),
                  tab(prev), tab(prev), tab(cur), tab(cur),
                  pl.BlockSpec(memory_space=pltpu.SMEM)],
        out_specs=[pl.BlockSpec((None, WINDOW, D_MODEL), cur), pl.BlockSpec((None, WINDOW, ATT_HEADS), cur)],
        out_shape=[jax.ShapeDtypeStruct((B, S, D_MODEL), BF16), jax.ShapeDtypeStruct((B, S, ATT_HEADS), F32)],
        compiler_params=_params("parallel", "parallel"),
    )(aq, akv, akv, cos, sin, cos, sin, sinks)


def _swa_bwd(aq, akv, cos, sin, sinks, lse, dob, *, name):
    B, S, _ = aq.shape
    nb = S // WINDOW
    cur, prev = _swa_specs(nb)

    def body(q_ref, kvp_ref, kvc_ref, cp_ref, sp_ref, cc_ref, sc_ref, sink_ref, lse_ref, do_ref,
             dq_ref, dkc_ref, dkp_ref, dsink_ref):
        @pl.when((pl.program_id(0) == 0) & (pl.program_id(1) == 0))
        def _():
            dsink_ref[...] = jnp.zeros_like(dsink_ref)

        cos_c, sin_c, cos_p, sin_p = cc_ref[...], sc_ref[...], cp_ref[...], sp_ref[...]
        q = (_rope(q_ref[...].astype(F32), cos_c, sin_c) * ATT_SCALE).astype(BF16)
        k = jnp.concatenate([_rope(kvp_ref[:, :KV_W].astype(F32), cos_p, sin_p),
                             _rope(kvc_ref[:, :KV_W].astype(F32), cos_c, sin_c)], axis=0)
        v = jnp.concatenate([kvp_ref[:, KV_W:], kvc_ref[:, KV_W:]], axis=0).astype(F32)
        low = lax.broadcasted_iota(jnp.int32, k.shape, 1) < ATT_HD
        kpad = _kv_padded(k, low)
        vpad = _kv_padded(v, low)
        mask = _swa_mask(pl.program_id(1) == 0)
        lse = lse_ref[...]
        dq_parts, dk_sum, dv_sum, dsinks = [], [], [], []
        for g in range(ATT_KV_HEADS):
            pairs = range(g * PAIRS_PER_KV, (g + 1) * PAIRS_PER_KV)
            keys = [(p, e) for p in pairs for e in (0, 1)]
            qp = {p: q[:, p * 128:(p + 1) * 128] for p in pairs}
            dop = {p: do_ref[:, p * 128:(p + 1) * 128] for p in pairs}
            s = {pe: jnp.where(mask, _nt(qp[pe[0]], kpad[g][pe[1]]), NEG_INF) for pe in keys}
            dp = {pe: _nt(dop[pe[0]], vpad[g][pe[1]]) for pe in keys}
            pr, ds = {}, {}
            for pe in keys:
                h = 2 * pe[0] + pe[1]
                lse_h = lse[:, h:h + 1]
                pf = jnp.exp(s[pe] - lse_h)
                delta = jnp.sum(pf * dp[pe], axis=1, keepdims=True)
                ds[pe] = (pf * (dp[pe] - delta)).astype(BF16)
                pr[pe] = pf.astype(BF16)
                p_sink = jnp.exp(sink_ref[0, h] - lse_h)
                dsinks.append(-jnp.sum(p_sink * delta, axis=0, keepdims=True))
            for p in pairs:
                dq_parts.append((_nn(ds[p, 0], kpad[g][0]) + _nn(ds[p, 1], kpad[g][1])) * ATT_SCALE)
            x = [sum(_tn(ds[p, e], qp[p]) for p in pairs) for e in (0, 1)]
            y = [sum(_tn(pr[p, e], dop[p]) for p in pairs) for e in (0, 1)]
            zk = jnp.where(low, x[0], x[1])
            zv = jnp.where(low, y[0], y[1])
            dk_sum.append(zk + _swap_halves(zk))
            dv_sum.append(zv + _swap_halves(zv))
        dq_ref[...] = _rope(jnp.concatenate(dq_parts, axis=1), cos_c, sin_c, inverse=True).astype(BF16)
        dk = jnp.where(low, dk_sum[0], dk_sum[1])
        dv = jnp.where(low, dv_sum[0], dv_sum[1])
        dkp_ref[:, :KV_W] = _rope(dk[:WINDOW], cos_p, sin_p, inverse=True)
        dkp_ref[:, KV_W:] = dv[:WINDOW]
        dkc_ref[:, :KV_W] = _rope(dk[WINDOW:], cos_c, sin_c, inverse=True)
        dkc_ref[:, KV_W:] = dv[WINDOW:]
        dsink_ref[...] += jnp.broadcast_to(jnp.concatenate(dsinks, axis=0), (ATT_HEADS, 128))

    tab = lambda im: pl.BlockSpec((None, WINDOW, 128), im)
    return pl.pallas_call(
        body, name=name, grid=(B, nb),
        in_specs=[pl.BlockSpec((None, WINDOW, D_MODEL), cur),
                  pl.BlockSpec((None, WINDOW, 2 * KV_W), prev), pl.BlockSpec((None, WINDOW, 2 * KV_W), cur),
                  tab(prev), tab(prev), tab(cur), tab(cur),
                  pl.BlockSpec(memory_space=pltpu.SMEM),
                  pl.BlockSpec((None, WINDOW, ATT_HEADS), cur),
                  pl.BlockSpec((None, WINDOW, D_MODEL), cur)],
        out_specs=[pl.BlockSpec((None, WINDOW, D_MODEL), cur),
                   pl.BlockSpec((None, WINDOW, 2 * KV_W), cur), pl.BlockSpec((None, WINDOW, 2 * KV_W), cur),
                   pl.BlockSpec((ATT_HEADS, 128), lambda b, i: (0, 0))],
        out_shape=[jax.ShapeDtypeStruct((B, S, D_MODEL), BF16),
                   jax.ShapeDtypeStruct((B, S, 2 * KV_W), F32), jax.ShapeDtypeStruct((B, S, 2 * KV_W), F32),
                   jax.ShapeDtypeStruct((ATT_HEADS, 128), F32)],
        compiler_params=_params("arbitrary", "arbitrary"),
    )(aq, akv, akv, cos, sin, cos, sin, sinks, lse, dob)


def _swa_dkv_combine(dkv_cur, dkv_prev, *, name):
    B, S, W = dkv_cur.shape
    nb = S // WINDOW

    def body(c_ref, p_ref, o_ref):
        nxt = jnp.where(pl.program_id(1) < nb - 1, p_ref[...], 0.0)
        o_ref[...] = (c_ref[...] + nxt).astype(BF16)

    cur = lambda b, j: (b, j, 0)
    return pl.pallas_call(
        body, name=name, grid=(B, nb),
        in_specs=[pl.BlockSpec((None, WINDOW, W), cur),
                  pl.BlockSpec((None, WINDOW, W), lambda b, j: (b, jnp.minimum(j + 1, nb - 1), 0))],
        out_specs=pl.BlockSpec((None, WINDOW, W), cur),
        out_shape=jax.ShapeDtypeStruct((B, S, W), BF16),
        compiler_params=_params("parallel", "parallel"),
    )(dkv_cur, dkv_prev)


def _rope_tables(positions):
    half = ROPE_DIM // 2
    inv = ROPE_THETA ** (-2.0 * jnp.arange(half, dtype=F32) / ROPE_DIM)
    ang = positions.astype(F32)[..., None] * inv
    c, s = jnp.cos(ang), jnp.sin(ang)
    pad = jnp.zeros(ang.shape[:-1] + (ATT_HD - ROPE_DIM,), F32)
    cos = jnp.concatenate([c, c, pad + 1.0], axis=-1)
    sin = jnp.concatenate([-s, s, pad], axis=-1)
    return jnp.tile(cos, (1, 1, 2)), jnp.tile(sin, (1, 1, 2))


def _lower_bound(lb_logits, *, name):
    def body(l_ref, o_ref):
        l = l_ref[...]
        e = jnp.exp(l - jnp.max(l, axis=0, keepdims=True))
        o_ref[...] = e[0:1] / jnp.sum(e, axis=0, keepdims=True)

    return pl.pallas_call(body, name=name, out_shape=jax.ShapeDtypeStruct((1, lb_logits.shape[1]), F32))(lb_logits)


W_ZH, W_GATES, W_AQ, W_AKV = 4 * HF, 2 * D_MODEL, ATT_HEADS * ATT_HD, 2 * KV_W
O_ZH, O_GATES, O_AQ, O_AKV = 0, W_ZH, W_ZH + W_GATES, W_ZH + W_GATES + W_AQ


def _reorder_w_in(w_in_full):
    return jnp.concatenate([w_in_full[:, :W_ZH], w_in_full[:, W_ZH + W_AQ + W_AKV:], w_in_full[:, W_ZH:W_ZH + W_AQ + W_AKV]], axis=1)


def _local_step(x, positions, target, small, W):
    B, S, D = x.shape
    T = B * S
    x2 = x.reshape(T, D)
    cos, sin = _rope_tables(positions)
    lb = _lower_bound(small["lb_logits"], name="lb_fwd")

    u1 = _norm_cast(x2, small["norm1_g"], name="norm1")
    w_in = W["w_in"]
    zh = _matmul(u1, w_in, out_dtype=BF16, name="mm_zh", tm=2048, tn=512, n_extent=W_ZH, b_noff=O_ZH // 512)
    gates = _matmul(u1, w_in, out_dtype=BF16, name="mm_gates", tm=2048, tn=512, n_extent=W_GATES, b_noff=O_GATES // 512)
    aq = _matmul(u1, w_in, out_dtype=BF16, name="mm_aq", tm=2048, tn=512, n_extent=W_AQ, b_noff=O_AQ // 512)
    akv = _matmul(u1, w_in, out_dtype=BF16, name="mm_akv", tm=2048, tn=256, n_extent=W_AKV, b_noff=O_AKV // 256)
    zh3 = zh.reshape(B, S, 4 * HF)
    aq3 = aq.reshape(B, S, D)
    akv3 = akv.reshape(B, S, 2 * KV_W)
    oa, states = _hgrn_fwd(zh3, lb, small["hgrn_norm_g"], name="hgrn_fwd")
    ob, lse = _swa_fwd(aq3, akv3, cos, sin, small["attn_sinks"], name="swa_fwd")
    oa2 = oa.reshape(T, D)
    ob2 = ob.reshape(T, D)
    pa = _matmul(oa2, W["w_a"], out_dtype=BF16, name="mm_pa", tm=2048, tn=512)
    pb = _matmul(ob2, W["w_b"], out_dtype=BF16, name="mm_pb", tm=2048, tn=512)
    merged = _merge_fwd(gates, pa, pb, name="merge_fwd")
    h = _matmul(merged, W["w_out"], addend=x2, name="mm_h", tm=1024, tn=512)
    u2 = _norm_cast(h, small["norm2_g"], name="norm2")
    gu = _matmul(u2, W["w_ffn"], out_dtype=BF16, name="mm_gu", tm=2048, tn=512)
    gu3 = gu.reshape(B, S, 2 * D_FF)
    act = _conv_act_fwd(gu3, W["conv_w"], small["conv_b"], name="conv_act_fwd")
    act2 = act.reshape(T, D_FF)
    h2 = _matmul(act2, W["w_down"], addend=h, name="mm_h2", tm=1024, tn=512)

    g = {}
    dh2, dh2b, g["final_g"], loss = _final_loss_bwd(h2, small["final_g"].reshape(1, D), target.reshape(T, D), name="final_loss_bwd")
    dact = _matmul(dh2b, W["w_down"], tb=True, out_dtype=BF16, name="mm_dact", tm=1024, tn=D_FF)
    g["w_down"] = _matmul(act2, dh2b, ta=True, name="mm_dw_down", tm=D_FF, tn=1024, tk=1024)
    dg_, dup, g["conv_w"], g["conv_b"] = _conv_act_bwd(gu3, W["conv_w"], small["conv_b"], dact.reshape(B, S, D_FF), name="conv_act_bwd")
    dg2 = dg_.reshape(T, D_FF)
    dup2 = dup.reshape(T, D_FF)
    du2 = _matmul(dg2, W["w_ffn"], tb=True, name="mm_du2_g", tm=1024, tn=512, b_koff=0)
    du2 = _matmul(dup2, W["w_ffn"], tb=True, addend=du2, name="mm_du2_u", tm=1024, tn=512, b_koff=1)
    g["w_ffn_g"] = _matmul(u2, dg2, ta=True, name="mm_dw_ffn_g", tm=1024, tn=D_FF, tk=1024)
    g["w_ffn_u"] = _matmul(u2, dup2, ta=True, name="mm_dw_ffn_u", tm=1024, tn=D_FF, tk=1024)
    dh, dhb, g["norm2_g"] = _norm_bwd_add(h, small["norm2_g"], du2, dh2, name="norm2_bwd")
    dmerged = _matmul(dhb, W["w_out"], tb=True, out_dtype=BF16, name="mm_dmerged", tm=2048, tn=512)
    g["w_out"] = _matmul(merged, dhb, ta=True, name="mm_dw_out", tm=1024, tn=1024, tk=2048)
    dgates, dpa, dpb = _merge_bwd(gates, pa, pb, dmerged, name="merge_bwd")
    doa = _matmul(dpa, W["w_a"], tb=True, out_dtype=BF16, name="mm_doa", tm=2048, tn=512)
    g["w_a"] = _matmul(oa2, dpa, ta=True, name="mm_dw_a", tm=1024, tn=1024, tk=2048)
    dob = _matmul(dpb, W["w_b"], tb=True, out_dtype=BF16, name="mm_dob", tm=2048, tn=512)
    g["w_b"] = _matmul(ob2, dpb, ta=True, name="mm_dw_b", tm=1024, tn=1024, tk=2048)
    daq, dkv_cur, dkv_prev, dsinks = _swa_bwd(aq3, akv3, cos, sin, small["attn_sinks"], lse, dob.reshape(B, S, D), name="swa_bwd")
    dakv = _swa_dkv_combine(dkv_cur, dkv_prev, name="swa_dkv").reshape(T, 2 * KV_W)
    daq2 = daq.reshape(T, D)
    g["attn_sinks"] = dsinks
    dzh, g["lb"], g["hgrn_norm_g"] = _hgrn_bwd(zh3, lb, small["hgrn_norm_g"], states, doa.reshape(B, S, D), name="hgrn_bwd")
    dzh2 = dzh.reshape(T, 4 * HF)
    du1 = _matmul(dzh2, w_in, tb=True, name="mm_du1_h", tm=1024, tn=512, b_koff=O_ZH // W_ZH)
    du1 = _matmul(dgates, w_in, tb=True, addend=du1, name="mm_du1_g", tm=1024, tn=512, b_koff=O_GATES // W_GATES)
    du1 = _matmul(daq2, w_in, tb=True, addend=du1, name="mm_du1_aq", tm=2048, tn=512, b_koff=O_AQ // W_AQ)
    du1 = _matmul(dakv, w_in, tb=True, addend=du1, name="mm_du1_akv", tm=2048, tn=512, b_koff=O_AKV // W_AKV)
    g["w_h"] = _matmul(u1, dzh2, ta=True, name="mm_dw_h", tm=1024, tn=2048, tk=1024)
    g["w_g"] = _matmul(u1, dgates, ta=True, name="mm_dw_g", tm=1024, tn=2048, tk=1024)
    g["w_aq"] = _matmul(u1, daq2, ta=True, name="mm_dw_aq", tm=1024, tn=1024, tk=2048)
    g["w_akv"] = _matmul(u1, dakv, ta=True, name="mm_dw_akv", tm=1024, tn=256, tk=2048)
    dx, _, g["norm1_g"] = _norm_bwd_add(x2, small["norm1_g"], du1, dh, name="norm1_bwd")
    g["lb_fwd"] = lb
    return loss, dx.reshape(B, S, D), g


def _my_place():
    return lax.axis_index("x"), lax.axis_index("y"), lax.axis_index("c")


def _all_gather(blk, *, in_vmem, reduce_sum=False, name):
    m, n = blk.shape
    space = pltpu.VMEM if in_vmem else pl.ANY

    def body(x_ref, out_ref, *rest):
        if reduce_sum:
            tot_ref, send_sems, recv_sems, local_sem = rest
        else:
            send_sems, recv_sems, local_sem = rest
        x, y, c = _my_place()
        me, sibling = (x, y, c), (x, y, 1 - c)
        chips = [(1 - x, y), (x, 1 - y), (1 - x, 1 - y)]

        def slot(px, py, pc):
            return out_ref.at[4 * px + 2 * py + pc]

        def copy(k, block, to, src=None):
            return pltpu.make_async_remote_copy(
                src_ref=slot(*block) if src is None else src, dst_ref=slot(*block),
                send_sem=send_sems.at[k], recv_sem=recv_sems.at[k], device_id=to, device_id_type=MESH)

        mine = pltpu.make_async_copy(x_ref, slot(*me), local_sem)
        mine.start()
        first = [copy(0, me, sibling, src=x_ref)]
        first += [copy(1 + j, me, (*chip, c), src=x_ref) for j, chip in enumerate(chips)]
        for cp in first:
            cp.start()
        passed = [copy(4 + j, (*chip, c), sibling) for j, chip in enumerate(chips)]
        for j, chip in enumerate(chips):
            copy(1 + j, (*chip, c), me).wait_recv()
            passed[j].start()
        copy(0, sibling, me).wait_recv()
        for j, chip in enumerate(chips):
            copy(4 + j, (*chip, 1 - c), me).wait_recv()
        for cp in first + passed:
            cp.wait_send()
        mine.wait()
        if reduce_sum:
            acc = out_ref[0]
            for p in range(1, N_DEV):
                acc = acc + out_ref[p]
            tot_ref[...] = acc

    out_shape = [jax.ShapeDtypeStruct((N_DEV, m, n), blk.dtype)]
    out_specs = [pl.BlockSpec(memory_space=space)]
    if reduce_sum:
        out_shape.append(jax.ShapeDtypeStruct((m, n), blk.dtype))
        out_specs.append(pl.BlockSpec(memory_space=pltpu.VMEM))
    res = pl.pallas_call(
        body, name=name,
        out_shape=out_shape,
        in_specs=[pl.BlockSpec(memory_space=space)],
        out_specs=out_specs,
        scratch_shapes=[pltpu.SemaphoreType.DMA((7,)), pltpu.SemaphoreType.DMA((7,)), pltpu.SemaphoreType.DMA],
    )(blk)
    return res if reduce_sum else res[0]


def _rs_sibling(g5, *, name):
    _, _, R, n = g5.shape

    def body(g_ref, r_ref, send_sems, recv_sems):
        x, y, c = _my_place()
        copies = [pltpu.make_async_remote_copy(
            src_ref=g_ref.at[k, 1 - c], dst_ref=r_ref.at[k], send_sem=send_sems.at[k], recv_sem=recv_sems.at[k],
            device_id=(x, y, 1 - c), device_id_type=MESH) for k in range(4)]
        for cp in copies:
            cp.start()
        for cp in copies:
            cp.wait_recv()
        for cp in copies:
            cp.wait_send()

    return pl.pallas_call(
        body, name=name,
        out_shape=jax.ShapeDtypeStruct((4, R, n), g5.dtype),
        in_specs=[ANY], out_specs=ANY,
        scratch_shapes=[pltpu.SemaphoreType.DMA((4,)), pltpu.SemaphoreType.DMA((4,))],
    )(g5)


def _rs_chips(p1, *, name):
    _, R, n = p1.shape

    def body(p_ref, r_ref, send_sems, recv_sems):
        x, y, c = _my_place()
        chips = [(1 - x, y), (x, 1 - y), (1 - x, 1 - y)]
        copies = [pltpu.make_async_remote_copy(
            src_ref=p_ref.at[2 * cx + cy], dst_ref=r_ref.at[j], send_sem=send_sems.at[j], recv_sem=recv_sems.at[j],
            device_id=(cx, cy, c), device_id_type=MESH) for j, (cx, cy) in enumerate(chips)]
        for cp in copies:
            cp.start()
        for cp in copies:
            cp.wait_recv()
        for cp in copies:
            cp.wait_send()

    return pl.pallas_call(
        body, name=name,
        out_shape=jax.ShapeDtypeStruct((3, R, n), p1.dtype),
        in_specs=[ANY], out_specs=ANY,
        scratch_shapes=[pltpu.SemaphoreType.DMA((3,)), pltpu.SemaphoreType.DMA((3,))],
    )(p1)


def _add_sibling(g5, r1, core, *, name):
    _, _, R, n = g5.shape
    tr = _pick(R, (592, 16))

    def body(c_ref, g_ref, r_ref, o_ref):
        o_ref[...] = (g_ref[...].astype(F32) + r_ref[...].astype(F32)).astype(BF16)

    return pl.pallas_call(
        body, name=name,
        grid_spec=pltpu.PrefetchScalarGridSpec(
            num_scalar_prefetch=1, grid=(4, R // tr),
            in_specs=[pl.BlockSpec((None, None, tr, n), lambda k, i, c: (k, c[0], i, 0)),
                      pl.BlockSpec((None, tr, n), lambda k, i, c: (k, i, 0))],
            out_specs=pl.BlockSpec((None, tr, n), lambda k, i, c: (k, i, 0))),
        out_shape=jax.ShapeDtypeStruct((4, R, n), BF16),
        compiler_params=_params("parallel", "parallel"),
    )(core, g5, r1)


def _adamw_math(w, g, m, v):
    m = ADAM_B1 * m + (1.0 - ADAM_B1) * g
    v = ADAM_B2 * v + (1.0 - ADAM_B2) * (g * g)
    m_hat = m / (1.0 - ADAM_B1 ** ADAM_STEP)
    v_hat = v / (1.0 - ADAM_B2 ** ADAM_STEP)
    delta = -ADAM_LR * (m_hat / (jnp.sqrt(v_hat) + ADAM_EPS) + ADAM_WD * w)
    return delta, m, v


def _adamw_big(p1, r2, chip, w, m, v, *, name):
    R, n = w.shape
    tr = _pick(R, (592, 16))

    def body(c_ref, p_ref, r_ref, w_ref, m_ref, v_ref, g_ref, d_ref, mo_ref, vo_ref):
        g = p_ref[...].astype(F32) + r_ref[0].astype(F32) + r_ref[1].astype(F32) + r_ref[2].astype(F32)
        d, mn, vn = _adamw_math(w_ref[...], g, m_ref[...], v_ref[...])
        g_ref[...] = g
        d_ref[...] = d
        mo_ref[...] = mn
        vo_ref[...] = vn

    row = pl.BlockSpec((tr, n), lambda i, c: (i, 0))
    return pl.pallas_call(
        body, name=name,
        grid_spec=pltpu.PrefetchScalarGridSpec(
            num_scalar_prefetch=1, grid=(R // tr,),
            in_specs=[pl.BlockSpec((None, tr, n), lambda i, c: (c[0], i, 0)),
                      pl.BlockSpec((3, tr, n), lambda i, c: (0, i, 0)), row, row, row],
            out_specs=[row, row, row, row]),
        out_shape=[jax.ShapeDtypeStruct((R, n), F32)] * 4,
        compiler_params=_params("parallel"),
    )(chip, p1, r2, w, m, v)


def _adamw_small(g, w, m, v, *, name):
    def body(g_ref, w_ref, m_ref, v_ref, d_ref, mo_ref, vo_ref):
        d, mn, vn = _adamw_math(w_ref[...], g_ref[...], m_ref[...], v_ref[...])
        d_ref[...] = d
        mo_ref[...] = mn
        vo_ref[...] = vn

    return pl.pallas_call(body, name=name, out_shape=[jax.ShapeDtypeStruct(w.shape, F32)] * 3)(g, w, m, v)


def _lb_bwd(dlb, lb, *, name):
    def body(d_ref, lb_ref, o_ref):
        t = d_ref[...] * lb_ref[...] * (1.0 - lb_ref[...])
        o_ref[0:1, :] = t
        o_ref[1:2, :] = -t

    return pl.pallas_call(body, name=name, out_shape=jax.ShapeDtypeStruct((2, lb.shape[1]), F32))(dlb, lb)


LANES = 128
N_IN, N_FFN = 7424, 5632
IN_BLK, FFN_BLK, DOWN_BLK, ROW_BLK = N_IN // N_DEV, N_FFN // N_DEV, D_FF // N_DEV, D_MODEL // N_DEV
CONVW_BLK = D_FF // N_DEV
PACK_ROWS = IN_BLK + 3 * ROW_BLK + FFN_BLK + DOWN_BLK
SMALL_NAMES = ("norm1_g", "lb_logits", "hgrn_norm_g", "attn_sinks", "norm2_g", "conv_b", "final_g")


def _pack_shards(w_in, w_a, w_b, w_out, w_ffn_in, w_down):
    return jnp.concatenate([w_in.reshape(IN_BLK, D_MODEL), w_a.reshape(ROW_BLK, D_MODEL), w_b.reshape(ROW_BLK, D_MODEL),
                            w_out.reshape(ROW_BLK, D_MODEL), w_ffn_in.reshape(FFN_BLK, D_MODEL), w_down.reshape(DOWN_BLK, D_MODEL)], axis=0)


def _unpack_shards(p):
    o = [0, IN_BLK, IN_BLK + ROW_BLK, IN_BLK + 2 * ROW_BLK, IN_BLK + 3 * ROW_BLK, IN_BLK + 3 * ROW_BLK + FFN_BLK, PACK_ROWS]
    return (p[o[0]:o[1]].reshape(1, D_MODEL, IN_BLK), p[o[1]:o[2]].reshape(1, ROW_BLK, D_MODEL), p[o[2]:o[3]].reshape(1, ROW_BLK, D_MODEL),
            p[o[3]:o[4]].reshape(1, ROW_BLK, D_MODEL), p[o[4]:o[5]].reshape(1, D_MODEL, FFN_BLK), p[o[5]:o[6]].reshape(1, DOWN_BLK, D_MODEL))


def _cols_from_blocks(gathered, lo, width):
    rows = width * D_MODEL // D_MODEL
    blk = gathered[:, lo:lo + rows].reshape(N_DEV, D_MODEL, width)
    return blk.transpose(1, 0, 2).reshape(D_MODEL, N_DEV * width)


def _blocks_from_cols(full, width):
    return full.reshape(D_MODEL, N_DEV, width).transpose(1, 0, 2).reshape(N_DEV, width, D_MODEL)


def _to_rows(vec, rows):
    vec = vec.reshape(-1)
    return jnp.pad(vec, (0, rows * LANES - vec.shape[0])).reshape(rows, LANES)


def kernel(x, positions, norm1_g, w_in, lb_logits, hgrn_norm_g, w_a, attn_sinks, w_b, w_out, norm2_g, w_ffn_in, conv_w, conv_b, w_down, final_g, loss_target, m_norm1_g, m_w_in, m_lb_logits, m_hgrn_norm_g, m_w_a, m_attn_sinks, m_w_b, m_w_out, m_norm2_g, m_w_ffn_in, m_conv_w, m_conv_b, m_w_down, m_final_g, v_norm1_g, v_w_in, v_lb_logits, v_hgrn_norm_g, v_w_a, v_attn_sinks, v_w_b, v_w_out, v_norm2_g, v_w_ffn_in, v_conv_w, v_conv_b, v_w_down, v_final_g):
    xi, yi, ci = _my_place()
    dev = 4 * xi + 2 * yi + ci
    core = ci.astype(jnp.int32).reshape(1)
    chip = (2 * xi + yi).astype(jnp.int32).reshape(1)

    packed_w = _pack_shards(w_in, w_a, w_b, w_out, w_ffn_in, w_down)
    gathered = _all_gather(packed_w.astype(BF16), in_vmem=False, name="ag_weights")
    o_a = IN_BLK
    o_ffn = IN_BLK + 3 * ROW_BLK
    o_down = o_ffn + FFN_BLK
    w_in_full = _cols_from_blocks(gathered, 0, IN_BLK)
    conv_rows = 16
    conv_all = _all_gather(_to_rows(conv_w, conv_rows), in_vmem=True, name="ag_conv_w")
    conv_full = conv_all.reshape(N_DEV, conv_rows * LANES)[:, :3 * CONVW_BLK].reshape(N_DEV, 3, CONVW_BLK).transpose(1, 0, 2).reshape(3, D_FF)
    W = dict(
        w_in=_reorder_w_in(w_in_full),
        w_a=gathered[:, o_a:o_a + ROW_BLK].reshape(D_MODEL, D_MODEL),
        w_b=gathered[:, o_a + ROW_BLK:o_a + 2 * ROW_BLK].reshape(D_MODEL, D_MODEL),
        w_out=gathered[:, o_a + 2 * ROW_BLK:o_a + 3 * ROW_BLK].reshape(D_MODEL, D_MODEL),
        w_ffn=_cols_from_blocks(gathered, o_ffn, FFN_BLK),
        w_down=gathered[:, o_down:o_down + DOWN_BLK].reshape(D_FF, D_MODEL),
        conv_w=conv_full,
    )
    small = dict(norm1_g=norm1_g, lb_logits=lb_logits, hgrn_norm_g=hgrn_norm_g, attn_sinks=attn_sinks, norm2_g=norm2_g,
                 conv_b=conv_b, final_g=final_g)

    loss, grad_x, g = _local_step(x, positions, loss_target, small, W)

    dw_in = jnp.concatenate([g["w_h"], g["w_aq"], g["w_akv"], g["w_g"]], axis=1)
    dw_ffn = jnp.concatenate([g["w_ffn_g"], g["w_ffn_u"]], axis=1)
    gp = jnp.concatenate([
        _blocks_from_cols(dw_in, IN_BLK), g["w_a"].reshape(N_DEV, ROW_BLK, D_MODEL), g["w_b"].reshape(N_DEV, ROW_BLK, D_MODEL),
        g["w_out"].reshape(N_DEV, ROW_BLK, D_MODEL), _blocks_from_cols(dw_ffn, FFN_BLK), g["w_down"].reshape(N_DEV, DOWN_BLK, D_MODEL)], axis=1)
    g5 = gp.astype(BF16).reshape(4, 2, PACK_ROWS, D_MODEL)
    r1 = _rs_sibling(g5, name="rs_sibling")
    p1 = _add_sibling(g5, r1, core, name="rs_add_sibling")
    r2 = _rs_chips(p1, name="rs_chips")
    packed_m = _pack_shards(m_w_in, m_w_a, m_w_b, m_w_out, m_w_ffn_in, m_w_down)
    packed_v = _pack_shards(v_w_in, v_w_a, v_w_b, v_w_out, v_w_ffn_in, v_w_down)
    big = [_unpack_shards(t) for t in _adamw_big(p1, r2, chip, packed_w, packed_m, packed_v, name="adamw_big")]

    dlogits = _lb_bwd(g["lb"], g["lb_fwd"], name="lb_bwd")
    sm_g = dict(norm1_g=g["norm1_g"], lb_logits=dlogits, hgrn_norm_g=g["hgrn_norm_g"], attn_sinks=g["attn_sinks"][:, 0],
                norm2_g=g["norm2_g"], conv_b=g["conv_b"], final_g=g["final_g"])
    vec = jnp.concatenate([sm_g[n].reshape(-1) for n in SMALL_NAMES] + [g["conv_w"].reshape(-1), loss.reshape(-1)])
    sm_rows = 136
    _, total = _all_gather(_to_rows(vec, sm_rows), in_vmem=True, reduce_sum=True, name="ar_small")
    total = total.reshape(-1)
    sm_w = dict(norm1_g=norm1_g, lb_logits=lb_logits, hgrn_norm_g=hgrn_norm_g, attn_sinks=attn_sinks, norm2_g=norm2_g,
                conv_b=conv_b, final_g=final_g)
    sm_m = dict(norm1_g=m_norm1_g, lb_logits=m_lb_logits, hgrn_norm_g=m_hgrn_norm_g, attn_sinks=m_attn_sinks, norm2_g=m_norm2_g,
                conv_b=m_conv_b, final_g=m_final_g)
    sm_v = dict(norm1_g=v_norm1_g, lb_logits=v_lb_logits, hgrn_norm_g=v_hgrn_norm_g, attn_sinks=v_attn_sinks, norm2_g=v_norm2_g,
                conv_b=v_conv_b, final_g=v_final_g)
    sizes = [sm_w[n].size for n in SMALL_NAMES]
    n_rep = sum(sizes)
    g_conv_full = total[n_rep:n_rep + 3 * D_FF].reshape(3, D_FF)
    g_conv = lax.dynamic_slice_in_dim(g_conv_full, dev * CONVW_BLK, CONVW_BLK, axis=1)
    loss_total = total[n_rep + 3 * D_FF]
    ad_rows = 72
    pack_small = lambda d, cw: _to_rows(jnp.concatenate([d[n].reshape(-1) for n in SMALL_NAMES] + [cw.reshape(-1)]), ad_rows)
    g_small = _to_rows(jnp.concatenate([total[:n_rep], g_conv.reshape(-1)]), ad_rows)
    d_s, m_s, v_s = _adamw_small(g_small, pack_small(sm_w, conv_w), pack_small(sm_m, m_conv_w), pack_small(sm_v, v_conv_w), name="adamw_small")

    def unpack_small(t):
        t = t.reshape(-1)
        out, off = {}, 0
        for n, s in zip(SMALL_NAMES, sizes):
            out[n] = t[off:off + s].reshape(sm_w[n].shape)
            off += s
        out["conv_w"] = t[off:off + 3 * CONVW_BLK].reshape(1, 3, CONVW_BLK)
        return out

    names = ("norm1_g", "w_in", "lb_logits", "hgrn_norm_g", "w_a", "attn_sinks", "w_b", "w_out", "norm2_g", "w_ffn_in", "conv_w", "conv_b", "w_down", "final_g")
    big_names = ("w_in", "w_a", "w_b", "w_out", "w_ffn_in", "w_down")
    outs = [loss_total.reshape(()), grad_x]
    for kind, s_vec in zip(range(4), (g_small, d_s, m_s, v_s)):
        s_un = unpack_small(s_vec)
        b_un = dict(zip(big_names, big[kind]))
        outs += [b_un[n] if n in b_un else s_un[n] for n in names]
    return tuple(outs)
```

```python
import functools

import jax
import jax.numpy as jnp
from jax import lax
from jax.experimental import pallas as pl
from jax.experimental.pallas import tpu as pltpu

F32 = jnp.float32
BF16 = jnp.bfloat16

D_MODEL = 1024
HGRN_HEADS = 8
HGRN_DK = 128
CHUNK = 64
ATT_HEADS = 16
ATT_KV_HEADS = 2
ATT_HD = 64
ATT_GROUP = ATT_HEADS // ATT_KV_HEADS
WINDOW = 128
ROPE_DIM = ATT_HD // 4
ROPE_THETA = 500000.0
D_FF = 2816
EPS = 1e-6
NEG_INF = -1e30
N_DEV = 8

ADAM_LR = 0.001
ADAM_B1 = 0.9
ADAM_B2 = 0.999
ADAM_EPS = 1e-08
ADAM_WD = 0.01
ADAM_STEP = 10

MESH = pl.DeviceIdType.MESH
ANY = pl.BlockSpec(memory_space=pl.ANY)


def _pick(n, cands):
    for c in cands:
        if n % c == 0:
            return c
    return n


def _sigmoid(x):
    return 1.0 / (1.0 + jnp.exp(-x))


def _silu(x):
    return x * _sigmoid(x)


def _rms(x, g):
    return x * lax.rsqrt(jnp.mean(x * x, axis=-1, keepdims=True) + EPS) * g


def _dot(a, b, dims):
    return lax.dot_general(a, b, (dims, ((), ())), preferred_element_type=F32)


def _nn(a, b):
    return _dot(a, b, ((1,), (0,)))


def _nt(a, b):
    return _dot(a, b, ((1,), (1,)))


def _tn(a, b):
    return _dot(a, b, ((0,), (0,)))


def _params(*sem):
    return pltpu.CompilerParams(dimension_semantics=sem, vmem_limit_bytes=56 * 1024 * 1024)


def _matmul(a, b, *, ta=False, tb=False, out_dtype=F32, addend=None, after=None, name, tm, tn, tk=None, n_extent=None, b_koff=0, b_noff=0):
    M, K = (a.shape[1], a.shape[0]) if ta else a.shape
    N = n_extent or (b.shape[0] if tb else b.shape[1])
    tm, tn, tk = min(tm, M), min(tn, N), min(tk or K, K)
    assert M % tm == 0 and N % tn == 0 and K % tk == 0, (name, M, N, K, tm, tn, tk)
    nk = K // tk
    use_scratch = nk > 1 and out_dtype != F32
    grid = (M // tm, N // tn, nk)
    a_spec = pl.BlockSpec((tk, tm), lambda i, j, k: (k, i)) if ta else pl.BlockSpec((tm, tk), lambda i, j, k: (i, k))
    b_spec = pl.BlockSpec((tn, tk), lambda i, j, k: (j + b_noff, k + b_koff)) if tb else pl.BlockSpec((tk, tn), lambda i, j, k: (k + b_koff, j + b_noff))
    o_spec = pl.BlockSpec((tm, tn), lambda i, j, k: (i, j))
    dims = ((0 if ta else 1,), (1 if tb else 0,))
    has_add = addend is not None

    n_in = 2 + has_add + (after is not None)

    def body(*refs):
        a_ref, b_ref = refs[:2]
        c_ref = refs[2] if has_add else None
        o_ref = refs[n_in]
        part = _dot(a_ref[...], b_ref[...], dims)
        if nk == 1:
            if has_add:
                part = part + c_ref[...].astype(F32)
            o_ref[...] = part.astype(out_dtype)
        else:
            acc_ref = refs[-1] if use_scratch else o_ref
            k = pl.program_id(2)

            @pl.when(k == 0)
            def _():
                acc_ref[...] = part + c_ref[...].astype(F32) if has_add else part

            @pl.when(k > 0)
            def _():
                acc_ref[...] += part

            if use_scratch:
                @pl.when(k == nk - 1)
                def _():
                    o_ref[...] = acc_ref[...].astype(out_dtype)

    in_specs = [a_spec, b_spec] + ([o_spec] if has_add else [])
    args = (a, b) + ((addend,) if has_add else ())
    if after is not None:
        in_specs.append(pl.BlockSpec(after.shape, lambda i, j, k: (0, 0)))
        args += (after,)
    return pl.pallas_call(
        body,
        name=name,
        grid=grid,
        in_specs=in_specs,
        out_specs=o_spec,
        out_shape=jax.ShapeDtypeStruct((M, N), out_dtype),
        scratch_shapes=[pltpu.VMEM((tm, tn), F32)] if use_scratch else [],
        compiler_params=_params("parallel", "parallel", "arbitrary"),
    )(*args)


def _row_spec(tm, n):
    return pl.BlockSpec((tm, n), lambda i: (i, 0))


def _full_spec(shape):
    return pl.BlockSpec(shape, lambda i: tuple(0 for _ in shape))


def _norm_cast(x, g, *, name):
    T, D = x.shape
    tm = _pick(T, (512, 256, 128))

    def body(x_ref, g_ref, u_ref):
        u_ref[...] = _rms(x_ref[...], g_ref[...]).astype(BF16)

    return pl.pallas_call(
        body, name=name, grid=(T // tm,),
        in_specs=[_row_spec(tm, D), _full_spec((1, D))],
        out_specs=_row_spec(tm, D),
        out_shape=jax.ShapeDtypeStruct((T, D), BF16),
        compiler_params=_params("parallel"),
    )(x, g)


def _norm_bwd_add(x, g, du, dres, *, name):
    T, D = x.shape
    tm = _pick(T, (512, 256, 128))
    has_res = dres is not None

    def body(*refs):
        if has_res:
            x_ref, g_ref, du_ref, dr_ref, dx_ref, dxb_ref, dg_ref = refs
        else:
            x_ref, g_ref, du_ref, dx_ref, dxb_ref, dg_ref = refs
        _, vjp = jax.vjp(_rms, x_ref[...], g_ref[...])
        dx, dg = vjp(du_ref[...].astype(F32))
        if has_res:
            dx = dx + dr_ref[...]
        dx_ref[...] = dx
        dxb_ref[...] = dx.astype(BF16)

        @pl.when(pl.program_id(0) == 0)
        def _():
            dg_ref[...] = jnp.zeros_like(dg_ref)

        dg_ref[...] += dg

    ins = [x, g, du] + ([dres] if has_res else [])
    in_specs = [_row_spec(tm, D), _full_spec((1, D)), _row_spec(tm, D)] + ([_row_spec(tm, D)] if has_res else [])
    return pl.pallas_call(
        body, name=name, grid=(T // tm,),
        in_specs=in_specs,
        out_specs=[_row_spec(tm, D), _row_spec(tm, D), _full_spec((1, D))],
        out_shape=[jax.ShapeDtypeStruct((T, D), F32), jax.ShapeDtypeStruct((T, D), BF16), jax.ShapeDtypeStruct((1, D), F32)],
        compiler_params=_params("arbitrary"),
    )(*ins)


def _final_loss_bwd(h2, g, target, *, name):
    T, D = h2.shape
    tm = _pick(T, (512, 256, 128))

    def body(h_ref, g_ref, t_ref, dx_ref, dxb_ref, dg_ref, loss_ref):
        y, vjp = jax.vjp(_rms, h_ref[...], g_ref[...])
        err = y - t_ref[...]
        dx, dg = vjp(err * (1.0 / D))
        dx_ref[...] = dx
        dxb_ref[...] = dx.astype(BF16)

        @pl.when(pl.program_id(0) == 0)
        def _():
            dg_ref[...] = jnp.zeros_like(dg_ref)
            loss_ref[...] = jnp.zeros_like(loss_ref)

        dg_ref[...] += dg
        loss_ref[...] += (0.5 / D) * jnp.sum(jnp.sum(err * err, axis=1, keepdims=True), axis=0, keepdims=True)

    return pl.pallas_call(
        body, name=name, grid=(T // tm,),
        in_specs=[_row_spec(tm, D), _full_spec((1, D)), _row_spec(tm, D)],
        out_specs=[_row_spec(tm, D), _row_spec(tm, D), _full_spec((1, D)), _full_spec((1, 1))],
        out_shape=[jax.ShapeDtypeStruct((T, D), F32), jax.ShapeDtypeStruct((T, D), BF16), jax.ShapeDtypeStruct((1, D), F32), jax.ShapeDtypeStruct((1, 1), F32)],
        compiler_params=_params("arbitrary"),
    )(h2, g, target)


def _merge_fn(gates, a, b):
    ga = gates[:, :D_MODEL].astype(F32)
    gb = gates[:, D_MODEL:].astype(F32)
    return _sigmoid(ga) * a.astype(F32) + _sigmoid(gb) * b.astype(F32)


def _merge_fwd(gates, a, b, *, name):
    T = a.shape[0]
    tm = _pick(T, (512, 256, 128))

    def body(g_ref, a_ref, b_ref, o_ref):
        o_ref[...] = _merge_fn(g_ref[...], a_ref[...], b_ref[...]).astype(BF16)

    return pl.pallas_call(
        body, name=name, grid=(T // tm,),
        in_specs=[_row_spec(tm, 2 * D_MODEL), _row_spec(tm, D_MODEL), _row_spec(tm, D_MODEL)],
        out_specs=_row_spec(tm, D_MODEL),
        out_shape=jax.ShapeDtypeStruct((T, D_MODEL), BF16),
        compiler_params=_params("parallel"),
    )(gates, a, b)


def _merge_bwd(gates, a, b, dmerged, *, name):
    T = a.shape[0]
    tm = _pick(T, (512, 256, 128))

    def body(g_ref, a_ref, b_ref, dm_ref, dg_ref, da_ref, db_ref):
        g = g_ref[...].astype(F32)
        dm = dm_ref[...].astype(F32)
        sa = _sigmoid(g[:, :D_MODEL])
        sb = _sigmoid(g[:, D_MODEL:])
        da_ref[...] = (dm * sa).astype(BF16)
        db_ref[...] = (dm * sb).astype(BF16)
        dg_ref[:, :D_MODEL] = (dm * a_ref[...].astype(F32) * sa * (1.0 - sa)).astype(BF16)
        dg_ref[:, D_MODEL:] = (dm * b_ref[...].astype(F32) * sb * (1.0 - sb)).astype(BF16)

    return pl.pallas_call(
        body, name=name, grid=(T // tm,),
        in_specs=[_row_spec(tm, 2 * D_MODEL), _row_spec(tm, D_MODEL), _row_spec(tm, D_MODEL), _row_spec(tm, D_MODEL)],
        out_specs=[_row_spec(tm, 2 * D_MODEL), _row_spec(tm, D_MODEL), _row_spec(tm, D_MODEL)],
        out_shape=[jax.ShapeDtypeStruct((T, 2 * D_MODEL), BF16), jax.ShapeDtypeStruct((T, D_MODEL), BF16), jax.ShapeDtypeStruct((T, D_MODEL), BF16)],
        compiler_params=_params("parallel"),
    )(gates, a, b, dmerged)


CONV_TC = 256


def _shift_down(x, n, rows):
    return jnp.where(rows >= n, pltpu.roll(x, n, 0), 0.0)


def _shift_up(x, n, rows, S):
    return jnp.where(rows < S - n, pltpu.roll(x, S - n, 0), 0.0)


def _conv_act_fwd(gu, conv_w, conv_b, *, name):
    B, S, _ = gu.shape
    tc = CONV_TC
    nc = D_FF // tc

    def body(g_ref, up_ref, w_ref, b_ref, o_ref):
        g = g_ref[...].astype(F32)
        rows = lax.broadcasted_iota(jnp.int32, g.shape, 0)
        w = w_ref[...]
        a = w[2:3] * g + w[1:2] * _shift_down(g, 1, rows) + w[0:1] * _shift_down(g, 2, rows) + b_ref[...]
        o_ref[...] = (_silu(a) * up_ref[...].astype(F32)).astype(BF16)

    return pl.pallas_call(
        body, name=name, grid=(B, nc),
        in_specs=[pl.BlockSpec((None, S, tc), lambda b, j: (b, 0, j)),
                  pl.BlockSpec((None, S, tc), lambda b, j: (b, 0, j + nc)),
                  pl.BlockSpec((3, tc), lambda b, j: (0, j)),
                  pl.BlockSpec((1, tc), lambda b, j: (0, j))],
        out_specs=pl.BlockSpec((None, S, tc), lambda b, j: (b, 0, j)),
        out_shape=jax.ShapeDtypeStruct((B, S, D_FF), BF16),
        compiler_params=_params("parallel", "parallel"),
    )(gu, gu, conv_w, conv_b)


def _conv_act_bwd(gu, conv_w, conv_b, dact, *, name):
    B, S, _ = gu.shape
    tc = CONV_TC
    nc = D_FF // tc

    def body(g_ref, up_ref, w_ref, b_ref, da_ref, dg_ref, dup_ref, dw_ref, db_ref):
        g = g_ref[...].astype(F32)
        up = up_ref[...].astype(F32)
        dact = da_ref[...].astype(F32)
        rows = lax.broadcasted_iota(jnp.int32, g.shape, 0)
        w = w_ref[...]
        g1 = _shift_down(g, 1, rows)
        g2 = _shift_down(g, 2, rows)
        a = w[2:3] * g + w[1:2] * g1 + w[0:1] * g2 + b_ref[...]
        sg = _sigmoid(a)
        dup_ref[...] = (dact * a * sg).astype(BF16)
        da = dact * up * sg * (1.0 + a * (1.0 - sg))
        dg = w[2:3] * da + w[1:2] * _shift_up(da, 1, rows, S) + w[0:1] * _shift_up(da, 2, rows, S)
        dg_ref[...] = dg.astype(BF16)

        @pl.when(pl.program_id(1) == 0)
        def _():
            dw_ref[...] = jnp.zeros_like(dw_ref)
            db_ref[...] = jnp.zeros_like(db_ref)

        dw_ref[0:1, :] += jnp.sum(da * g2, axis=0, keepdims=True)
        dw_ref[1:2, :] += jnp.sum(da * g1, axis=0, keepdims=True)
        dw_ref[2:3, :] += jnp.sum(da * g, axis=0, keepdims=True)
        db_ref[...] += jnp.sum(da, axis=0, keepdims=True)

    col = lambda j, b: (b, 0, j)
    return pl.pallas_call(
        body, name=name, grid=(nc, B),
        in_specs=[pl.BlockSpec((None, S, tc), col),
                  pl.BlockSpec((None, S, tc), lambda j, b: (b, 0, j + nc)),
                  pl.BlockSpec((3, tc), lambda j, b: (0, j)),
                  pl.BlockSpec((1, tc), lambda j, b: (0, j)),
                  pl.BlockSpec((None, S, tc), col)],
        out_specs=[pl.BlockSpec((None, S, tc), col), pl.BlockSpec((None, S, tc), col),
                   pl.BlockSpec((3, tc), lambda j, b: (0, j)), pl.BlockSpec((1, tc), lambda j, b: (0, j))],
        out_shape=[jax.ShapeDtypeStruct((B, S, D_FF), BF16), jax.ShapeDtypeStruct((B, S, D_FF), BF16),
                   jax.ShapeDtypeStruct((3, D_FF), F32), jax.ShapeDtypeStruct((1, D_FF), F32)],
        compiler_params=_params("parallel", "arbitrary"),
    )(gu, gu, conv_w, conv_b, dact)


HGRN_CPB = 4
HF = HGRN_HEADS * HGRN_DK


def _tri(n, upper=False):
    r = lax.broadcasted_iota(jnp.int32, (n, n), 0)
    c = lax.broadcasted_iota(jnp.int32, (n, n), 1)
    return (c >= r) if upper else (r >= c)


def _hs(h):
    return slice(h * HGRN_DK, (h + 1) * HGRN_DK)


def _cumsum_rows(tri_b, x):
    hi = x.astype(BF16)
    lo = (x - hi.astype(F32)).astype(BF16)
    return _nn(tri_b, hi) + _nn(tri_b, lo)


def _hgrn_pre(q, fz, lb, tril_b):
    qf = _silu(q)
    sg = _sigmoid(fz)
    f = lb + (1.0 - lb) * sg
    k = 1.0 - f
    b = _cumsum_rows(tril_b, jnp.log(f))
    bref = b[CHUNK // 2:CHUNK // 2 + 1, :]
    blast = b[CHUNK - 1:CHUNK, :]
    e1 = jnp.exp(b - bref)
    e2 = jnp.exp(bref - b)
    e3 = jnp.exp(b)
    e4 = jnp.exp(blast - b)
    dec = jnp.exp(blast)
    return sg, f, (e1, e2, e3, e4), qf * e1, k * e2, qf * e3, k * e4, dec


def _hgrn_fwd(zh, lb, gn, *, name):
    B, S, _ = zh.shape
    cpb = HGRN_CPB
    ts = cpb * CHUNK
    nblk = S // ts

    def body(z_ref, lb_ref, gn_ref, o_ref, st_ref, state):
        @pl.when(pl.program_id(1) == 0)
        def _():
            state[...] = jnp.zeros_like(state)

        H = HGRN_HEADS
        causal = _tri(CHUNK)
        tril_b = causal.astype(BF16)
        lb = lb_ref[...]
        for c in range(cpb):
            rows = slice(c * CHUNK, (c + 1) * CHUNK)
            q = z_ref[rows, 0:HF].astype(F32)
            fz = z_ref[rows, HF:2 * HF].astype(F32)
            v = z_ref[rows, 2 * HF:3 * HF]
            hg = z_ref[rows, 3 * HF:4 * HF].astype(F32)
            _, _, _, q_in, k_in, q_out, k_st, dec = _hgrn_pre(q, fz, lb, tril_b)
            q_in, k_in, q_out, k_st = (t.astype(BF16) for t in (q_in, k_in, q_out, k_st))
            a = [jnp.where(causal, _nt(q_in[:, _hs(h)], k_in[:, _hs(h)]), 0.0).astype(BF16) for h in range(H)]
            st = [state[h] for h in range(H)]
            for h in range(H):
                st_ref[c, h] = st[h]
            o = [_nn(a[h], v[:, _hs(h)]) + _nt(q_out[:, _hs(h)], st[h].astype(BF16)) for h in range(H)]
            for h in range(H):
                state[h] = st[h] * dec[:, _hs(h)] + _tn(v[:, _hs(h)], k_st[:, _hs(h)])
            gate = _silu(hg)
            for h in range(H):
                o_ref[rows, _hs(h)] = (_rms(o[h], gn_ref[...]) * gate[:, _hs(h)]).astype(BF16)

    return pl.pallas_call(
        body, name=name, grid=(B, nblk),
        in_specs=[pl.BlockSpec((None, ts, 4 * HF), lambda b, s: (b, s, 0)),
                  pl.BlockSpec((1, HF), lambda b, s: (0, 0)),
                  pl.BlockSpec((1, HGRN_DK), lambda b, s: (0, 0))],
        out_specs=[pl.BlockSpec((None, ts, HF), lambda b, s: (b, s, 0)),
                   pl.BlockSpec((None, cpb, HGRN_HEADS, HGRN_DK, HGRN_DK), lambda b, s: (b, s, 0, 0, 0))],
        out_shape=[jax.ShapeDtypeStruct((B, S, HF), BF16),
                   jax.ShapeDtypeStruct((B, S // CHUNK, HGRN_HEADS, HGRN_DK, HGRN_DK), F32)],
        scratch_shapes=[pltpu.VMEM((HGRN_HEADS, HGRN_DK, HGRN_DK), F32)],
        compiler_params=_params("arbitrary", "arbitrary"),
    )(zh, lb, gn)


def _hgrn_bwd(zh, lb, gn, states, doa, *, name):
    B, S, _ = zh.shape
    cpb = HGRN_CPB
    ts = cpb * CHUNK
    nblk = S // ts
    rev = lambda b, s: (b, nblk - 1 - s, 0)

    def body(z_ref, lb_ref, gn_ref, st_ref, do_ref, dz_ref, dlb_ref, dgn_ref, dstate):
        @pl.when(pl.program_id(1) == 0)
        def _():
            dstate[...] = jnp.zeros_like(dstate)

        @pl.when((pl.program_id(0) == 0) & (pl.program_id(1) == 0))
        def _():
            dlb_ref[...] = jnp.zeros_like(dlb_ref)
            dgn_ref[...] = jnp.zeros_like(dgn_ref)

        H = HGRN_HEADS
        cat = lambda xs: jnp.concatenate(xs, axis=1)
        causal = _tri(CHUNK)
        tril_b = causal.astype(BF16)
        triu_b = _tri(CHUNK, upper=True).astype(BF16)
        rowid = lax.broadcasted_iota(jnp.int32, (CHUNK, HF), 0)
        lb = lb_ref[...]
        gn = gn_ref[...]
        for c in reversed(range(cpb)):
            rows = slice(c * CHUNK, (c + 1) * CHUNK)
            q = z_ref[rows, 0:HF].astype(F32)
            fz = z_ref[rows, HF:2 * HF].astype(F32)
            v = z_ref[rows, 2 * HF:3 * HF]
            hg = z_ref[rows, 3 * HF:4 * HF].astype(F32)
            sg, f, (e1, e2, e3, e4), q_in, k_in, q_out, k_st, dec = _hgrn_pre(q, fz, lb, tril_b)
            q_in_b, k_in_b, q_out_b, k_st_b = (t.astype(BF16) for t in (q_in, k_in, q_out, k_st))
            a_b = [jnp.where(causal, _nt(q_in_b[:, _hs(h)], k_in_b[:, _hs(h)]), 0.0).astype(BF16) for h in range(H)]
            st = [st_ref[c, h] for h in range(H)]
            st_b = [t.astype(BF16) for t in st]
            o = [_nn(a_b[h], v[:, _hs(h)]) + _nt(q_out_b[:, _hs(h)], st_b[h]) for h in range(H)]
            dout = do_ref[rows, :].astype(F32)
            shg = _sigmoid(hg)
            gate = hg * shg
            do_l, dgn_acc = [], jnp.zeros_like(gn)
            for h in range(H):
                _, norm_vjp = jax.vjp(_rms, o[h], gn)
                d_o, d_gn = norm_vjp(dout[:, _hs(h)] * gate[:, _hs(h)])
                do_l.append(d_o)
                dgn_acc = dgn_acc + d_gn
            dgn_ref[...] += dgn_acc
            on = cat([_rms(o[h], gn) for h in range(H)])
            dhg = dout * on * shg * (1.0 + hg * (1.0 - shg))
            do_b = [t.astype(BF16) for t in do_l]
            dst = [dstate[h] for h in range(H)]
            dst_b = [t.astype(BF16) for t in dst]
            da_b = [jnp.where(causal, _nt(do_b[h], v[:, _hs(h)]), 0.0).astype(BF16) for h in range(H)]
            dv = cat([_tn(a_b[h], do_b[h]) + _nt(k_st_b[:, _hs(h)], dst_b[h]) for h in range(H)])
            dq_in = cat([_nn(da_b[h], k_in_b[:, _hs(h)]) for h in range(H)])
            dk_in = cat([_tn(da_b[h], q_in_b[:, _hs(h)]) for h in range(H)])
            dq_out = cat([_nn(do_b[h], st_b[h]) for h in range(H)])
            dk_st = cat([_nn(v[:, _hs(h)], dst_b[h]) for h in range(H)])
            ddec = cat([jnp.sum(st[h] * dst[h], axis=0, keepdims=True) for h in range(H)])
            for h in range(H):
                dstate[h] = dst[h] * dec[:, _hs(h)] + _tn(do_b[h], q_out_b[:, _hs(h)])
            t_qin = dq_in * q_in
            t_kin = dk_in * k_in
            t_kst = dk_st * k_st
            db = t_qin - t_kin + dq_out * q_out - t_kst
            dbref = jnp.sum(t_kin - t_qin, axis=0, keepdims=True)
            dblast = jnp.sum(t_kst, axis=0, keepdims=True) + ddec * dec
            db = db + jnp.where(rowid == CHUNK // 2, dbref, 0.0) + jnp.where(rowid == CHUNK - 1, dblast, 0.0)
            dlogf = _cumsum_rows(triu_b, db)
            dqf = dq_in * e1 + dq_out * e3
            dk = dk_in * e2 + dk_st * e4
            df = dlogf / f - dk
            dfz = df * (1.0 - lb) * sg * (1.0 - sg)
            dlb_ref[...] += jnp.sum(df * (1.0 - sg), axis=0, keepdims=True)
            sq = _sigmoid(q)
            dq = dqf * sq * (1.0 + q * (1.0 - sq))
            dz_ref[rows, 0:HF] = dq.astype(BF16)
            dz_ref[rows, HF:2 * HF] = dfz.astype(BF16)
            dz_ref[rows, 2 * HF:3 * HF] = dv.astype(BF16)
            dz_ref[rows, 3 * HF:4 * HF] = dhg.astype(BF16)

    return pl.pallas_call(
        body, name=name, grid=(B, nblk),
        in_specs=[pl.BlockSpec((None, ts, 4 * HF), rev),
                  pl.BlockSpec((1, HF), lambda b, s: (0, 0)),
                  pl.BlockSpec((1, HGRN_DK), lambda b, s: (0, 0)),
                  pl.BlockSpec((None, cpb, HGRN_HEADS, HGRN_DK, HGRN_DK), lambda b, s: (b, nblk - 1 - s, 0, 0, 0)),
                  pl.BlockSpec((None, ts, HF), rev)],
        out_specs=[pl.BlockSpec((None, ts, 4 * HF), rev),
                   pl.BlockSpec((1, HF), lambda b, s: (0, 0)),
                   pl.BlockSpec((1, HGRN_DK), lambda b, s: (0, 0))],
        out_shape=[jax.ShapeDtypeStruct((B, S, 4 * HF), BF16),
                   jax.ShapeDtypeStruct((1, HF), F32),
                   jax.ShapeDtypeStruct((1, HGRN_DK), F32)],
        scratch_shapes=[pltpu.VMEM((HGRN_HEADS, HGRN_DK, HGRN_DK), F32)],
        compiler_params=_params("arbitrary", "arbitrary"),
    )(zh, lb, gn, states, doa)


KV_W = ATT_KV_HEADS * ATT_HD
ATT_SCALE = ATT_HD ** -0.5


def _rope(x, cos, sin, inverse=False):
    half = ROPE_DIM // 2
    outs = []
    for p in range(x.shape[1] // 128):
        xp = x[:, p * 128:(p + 1) * 128]
        lane = lax.broadcasted_iota(jnp.int32, xp.shape, 1) % ATT_HD
        sw = jnp.where(lane < half, pltpu.roll(xp, 128 - half, 1), pltpu.roll(xp, half, 1))
        outs.append(xp * cos - sw * sin if inverse else xp * cos + sw * sin)
    return outs[0] if len(outs) == 1 else jnp.concatenate(outs, axis=1)


PAIRS_PER_KV = ATT_GROUP // 2


def _swap_halves(x):
    return pltpu.roll(x, ATT_HD, 1)


def _kv_padded(t, low):
    sw = _swap_halves(t)
    zero = jnp.zeros_like(t)
    out = []
    for g in range(ATT_KV_HEADS):
        in_low, in_high = (t, sw) if g == 0 else (sw, t)
        out.append((jnp.where(low, in_low, zero).astype(BF16), jnp.where(low, zero, in_high).astype(BF16)))
    return out


def _swa_mask(first_block):
    qi = lax.broadcasted_iota(jnp.int32, (WINDOW, 2 * WINDOW), 0)
    mi = lax.broadcasted_iota(jnp.int32, (WINDOW, 2 * WINDOW), 1)
    band = (mi > qi) & (mi <= qi + WINDOW)
    return band & (jnp.logical_not(first_block) | (mi >= WINDOW))


def _swa_specs(nb):
    cur = lambda b, i: (b, i, 0)
    prev = lambda b, i: (b, jnp.maximum(i - 1, 0), 0)
    return cur, prev


def _swa_fwd(aq, akv, cos, sin, sinks, *, name):
    B, S, _ = aq.shape
    nb = S // WINDOW
    cur, prev = _swa_specs(nb)

    def body(q_ref, kvp_ref, kvc_ref, cp_ref, sp_ref, cc_ref, sc_ref, sink_ref, o_ref, lse_ref):
        cos_c, sin_c = cc_ref[...], sc_ref[...]
        q = (_rope(q_ref[...].astype(F32), cos_c, sin_c) * ATT_SCALE).astype(BF16)
        k = jnp.concatenate([_rope(kvp_ref[:, :KV_W].astype(F32), cp_ref[...], sp_ref[...]),
                             _rope(kvc_ref[:, :KV_W].astype(F32), cos_c, sin_c)], axis=0)
        v = jnp.concatenate([kvp_ref[:, KV_W:], kvc_ref[:, KV_W:]], axis=0).astype(F32)
        low = lax.broadcasted_iota(jnp.int32, k.shape, 1) < ATT_HD
        kpad = _kv_padded(k, low)
        vpad = _kv_padded(v, low)
        mask = _swa_mask(pl.program_id(1) == 0)
        lses = []
        for g in range(ATT_KV_HEADS):
            pairs = range(g * PAIRS_PER_KV, (g + 1) * PAIRS_PER_KV)
            keys = [(p, e) for p in pairs for e in (0, 1)]
            qp = {p: q[:, p * 128:(p + 1) * 128] for p in pairs}
            s = {pe: jnp.where(mask, _nt(qp[pe[0]], kpad[g][pe[1]]), NEG_INF) for pe in keys}
            pr = {}
            for pe in keys:
                sink = sink_ref[0, 2 * pe[0] + pe[1]]
                m = jnp.maximum(jnp.max(s[pe], axis=1, keepdims=True), sink)
                ex = jnp.exp(s[pe] - m)
                den = jnp.sum(ex, axis=1, keepdims=True) + jnp.exp(sink - m)
                pr[pe] = (ex * (1.0 / den)).astype(BF16)
                lses.append(m + jnp.log(den))
            for p in pairs:
                o_ref[:, p * 128:(p + 1) * 128] = (_nn(pr[p, 0], vpad[g][0]) + _nn(pr[p, 1], vpad[g][1])).astype(BF16)
        lse_ref[...] = jnp.concatenate(lses, axis=1)

    tab = lambda im: pl.BlockSpec((None, WINDOW, 128), im)
    return pl.pallas_call(
        body, name=name, grid=(B, nb),
        in_specs=[pl.BlockSpec((None, WINDOW, D_MODEL), cur),
                  pl.BlockSpec((None, WINDOW, 2 * KV_W), prev), pl.BlockSpec((None, WINDOW, 2 * KV_W), cur),
                  tab(prev), tab(prev), tab(cur), tab(cur),
                  pl.BlockSpec(memory_space=pltpu.SMEM)],
        out_specs=[pl.BlockSpec((None, WINDOW, D_MODEL), cur), pl.BlockSpec((None, WINDOW, ATT_HEADS), cur)],
        out_shape=[jax.ShapeDtypeStruct((B, S, D_MODEL), BF16), jax.ShapeDtypeStruct((B, S, ATT_HEADS), F32)],
        compiler_params=_params("parallel", "parallel"),
    )(aq, akv, akv, cos, sin, cos, sin, sinks)


def _swa_bwd(aq, akv, cos, sin, sinks, lse, dob, *, name):
    B, S, _ = aq.shape
    nb = S // WINDOW
    cur, prev = _swa_specs(nb)

    def body(q_ref, kvp_ref, kvc_ref, cp_ref, sp_ref, cc_ref, sc_ref, sink_ref, lse_ref, do_ref,
             dq_ref, dkc_ref, dkp_ref, dsink_ref):
        @pl.when((pl.program_id(0) == 0) & (pl.program_id(1) == 0))
        def _():
            dsink_ref[...] = jnp.zeros_like(dsink_ref)

        cos_c, sin_c, cos_p, sin_p = cc_ref[...], sc_ref[...], cp_ref[...], sp_ref[...]
        q = (_rope(q_ref[...].astype(F32), cos_c, sin_c) * ATT_SCALE).astype(BF16)
        k = jnp.concatenate([_rope(kvp_ref[:, :KV_W].astype(F32), cos_p, sin_p),
                             _rope(kvc_ref[:, :KV_W].astype(F32), cos_c, sin_c)], axis=0)
        v = jnp.concatenate([kvp_ref[:, KV_W:], kvc_ref[:, KV_W:]], axis=0).astype(F32)
        low = lax.broadcasted_iota(jnp.int32, k.shape, 1) < ATT_HD
        kpad = _kv_padded(k, low)
        vpad = _kv_padded(v, low)
        mask = _swa_mask(pl.program_id(1) == 0)
        lse = lse_ref[...]
        dq_parts, dk_sum, dv_sum, dsinks = [], [], [], []
        for g in range(ATT_KV_HEADS):
            pairs = range(g * PAIRS_PER_KV, (g + 1) * PAIRS_PER_KV)
            keys = [(p, e) for p in pairs for e in (0, 1)]
            qp = {p: q[:, p * 128:(p + 1) * 128] for p in pairs}
            dop = {p: do_ref[:, p * 128:(p + 1) * 128] for p in pairs}
            s = {pe: jnp.where(mask, _nt(qp[pe[0]], kpad[g][pe[1]]), NEG_INF) for pe in keys}
            dp = {pe: _nt(dop[pe[0]], vpad[g][pe[1]]) for pe in keys}
            pr, ds = {}, {}
            for pe in keys:
                h = 2 * pe[0] + pe[1]
                lse_h = lse[:, h:h + 1]
                pf = jnp.exp(s[pe] - lse_h)
                delta = jnp.sum(pf * dp[pe], axis=1, keepdims=True)
                ds[pe] = (pf * (dp[pe] - delta)).astype(BF16)
                pr[pe] = pf.astype(BF16)
                p_sink = jnp.exp(sink_ref[0, h] - lse_h)
                dsinks.append(-jnp.sum(p_sink * delta, axis=0, keepdims=True))
            for p in pairs:
                dq_parts.append((_nn(ds[p, 0], kpad[g][0]) + _nn(ds[p, 1], kpad[g][1])) * ATT_SCALE)
            x = [sum(_tn(ds[p, e], qp[p]) for p in pairs) for e in (0, 1)]
            y = [sum(_tn(pr[p, e], dop[p]) for p in pairs) for e in (0, 1)]
            zk = jnp.where(low, x[0], x[1])
            zv = jnp.where(low, y[0], y[1])
            dk_sum.append(zk + _swap_halves(zk))
            dv_sum.append(zv + _swap_halves(zv))
        dq_ref[...] = _rope(jnp.concatenate(dq_parts, axis=1), cos_c, sin_c, inverse=True).astype(BF16)
        dk = jnp.where(low, dk_sum[0], dk_sum[1])
        dv = jnp.where(low, dv_sum[0], dv_sum[1])
        dkp_ref[:, :KV_W] = _rope(dk[:WINDOW], cos_p, sin_p, inverse=True)
        dkp_ref[:, KV_W:] = dv[:WINDOW]
        dkc_ref[:, :KV_W] = _rope(dk[WINDOW:], cos_c, sin_c, inverse=True)
        dkc_ref[:, KV_W:] = dv[WINDOW:]
        dsink_ref[...] += jnp.broadcast_to(jnp.concatenate(dsinks, axis=0), (ATT_HEADS, 128))

    tab = lambda im: pl.BlockSpec((None, WINDOW, 128), im)
    return pl.pallas_call(
        body, name=name, grid=(B, nb),
        in_specs=[pl.BlockSpec((None, WINDOW, D_MODEL), cur),
                  pl.BlockSpec((None, WINDOW, 2 * KV_W), prev), pl.BlockSpec((None, WINDOW, 2 * KV_W), cur),
                  tab(prev), tab(prev), tab(cur), tab(cur),
                  pl.BlockSpec(memory_space=pltpu.SMEM),
                  pl.BlockSpec((None, WINDOW, ATT_HEADS), cur),
                  pl.BlockSpec((None, WINDOW, D_MODEL), cur)],
        out_specs=[pl.BlockSpec((None, WINDOW, D_MODEL), cur),
                   pl.BlockSpec((None, WINDOW, 2 * KV_W), cur), pl.BlockSpec((None, WINDOW, 2 * KV_W), cur),
                   pl.BlockSpec((ATT_HEADS, 128), lambda b, i: (0, 0))],
        out_shape=[jax.ShapeDtypeStruct((B, S, D_MODEL), BF16),
                   jax.ShapeDtypeStruct((B, S, 2 * KV_W), F32), jax.ShapeDtypeStruct((B, S, 2 * KV_W), F32),
                   jax.ShapeDtypeStruct((ATT_HEADS, 128), F32)],
        compiler_params=_params("arbitrary", "arbitrary"),
    )(aq, akv, akv, cos, sin, cos, sin, sinks, lse, dob)


def _swa_dkv_combine(dkv_cur, dkv_prev, *, name):
    B, S, W = dkv_cur.shape
    nb = S // WINDOW

    def body(c_ref, p_ref, o_ref):
        nxt = jnp.where(pl.program_id(1) < nb - 1, p_ref[...], 0.0)
        o_ref[...] = (c_ref[...] + nxt).astype(BF16)

    cur = lambda b, j: (b, j, 0)
    return pl.pallas_call(
        body, name=name, grid=(B, nb),
        in_specs=[pl.BlockSpec((None, WINDOW, W), cur),
                  pl.BlockSpec((None, WINDOW, W), lambda b, j: (b, jnp.minimum(j + 1, nb - 1), 0))],
        out_specs=pl.BlockSpec((None, WINDOW, W), cur),
        out_shape=jax.ShapeDtypeStruct((B, S, W), BF16),
        compiler_params=_params("parallel", "parallel"),
    )(dkv_cur, dkv_prev)


def _rope_tables(positions):
    half = ROPE_DIM // 2
    inv = ROPE_THETA ** (-2.0 * jnp.arange(half, dtype=F32) / ROPE_DIM)
    ang = positions.astype(F32)[..., None] * inv
    c, s = jnp.cos(ang), jnp.sin(ang)
    pad = jnp.zeros(ang.shape[:-1] + (ATT_HD - ROPE_DIM,), F32)
    cos = jnp.concatenate([c, c, pad + 1.0], axis=-1)
    sin = jnp.concatenate([-s, s, pad], axis=-1)
    return jnp.tile(cos, (1, 1, 2)), jnp.tile(sin, (1, 1, 2))


def _lower_bound(lb_logits, *, name):
    def body(l_ref, o_ref):
        l = l_ref[...]
        e = jnp.exp(l - jnp.max(l, axis=0, keepdims=True))
        o_ref[...] = e[0:1] / jnp.sum(e, axis=0, keepdims=True)

    return pl.pallas_call(body, name=name, out_shape=jax.ShapeDtypeStruct((1, lb_logits.shape[1]), F32))(lb_logits)


W_ZH, W_GATES, W_AQ, W_AKV = 4 * HF, 2 * D_MODEL, ATT_HEADS * ATT_HD, 2 * KV_W
O_ZH, O_GATES, O_AQ, O_AKV = 0, W_ZH, W_ZH + W_GATES, W_ZH + W_GATES + W_AQ


def _reorder_w_in(w_in_full):
    return jnp.concatenate([w_in_full[:, :W_ZH], w_in_full[:, W_ZH + W_AQ + W_AKV:], w_in_full[:, W_ZH:W_ZH + W_AQ + W_AKV]], axis=1)


def _local_step(x, positions, target, small, w_in, rest_weights, emit, start_token):
    B, S, D = x.shape
    T = B * S
    x2 = x.reshape(T, D)
    cos, sin = _rope_tables(positions)
    lb = _lower_bound(small["lb_logits"], name="lb_fwd")
    zero = lambda tok: tok[0:1, 0:1]

    u1 = _norm_cast(x2, small["norm1_g"] + zero(start_token), name="norm1")
    zh =_matmul(u1, w_in, out_dtype=BF16, name="mm_zh", tm=2048, tn=512, n_extent=W_ZH, b_noff=O_ZH // 512)
    gates = _matmul(u1, w_in, out_dtype=BF16, name="mm_gates", tm=2048, tn=512, n_extent=W_GATES, b_noff=O_GATES // 512)
    aq = _matmul(u1, w_in, out_dtype=BF16, name="mm_aq", tm=2048, tn=512, n_extent=W_AQ, b_noff=O_AQ // 512)
    akv = _matmul(u1, w_in, out_dtype=BF16, name="mm_akv", tm=2048, tn=256, n_extent=W_AKV, b_noff=O_AKV // 256)
    zh3 = zh.reshape(B, S, 4 * HF)
    aq3 = aq.reshape(B, S, D)
    akv3 = akv.reshape(B, S, 2 * KV_W)
    oa, states = _hgrn_fwd(zh3, lb, small["hgrn_norm_g"], name="hgrn_fwd")
    ob, lse = _swa_fwd(aq3, akv3, cos, sin, small["attn_sinks"], name="swa_fwd")
    oa2 = oa.reshape(T, D)
    ob2 = ob.reshape(T, D)
    W = rest_weights(ob)
    pa = _matmul(oa2, W["w_a"], out_dtype=BF16, name="mm_pa", tm=2048, tn=512)
    pb = _matmul(ob2, W["w_b"], out_dtype=BF16, name="mm_pb", tm=2048, tn=512)
    merged = _merge_fwd(gates, pa, pb, name="merge_fwd")
    h = _matmul(merged, W["w_out"], addend=x2, name="mm_h", tm=1024, tn=512)
    u2 = _norm_cast(h, small["norm2_g"], name="norm2")
    gu = _matmul(u2, W["w_ffn"], out_dtype=BF16, name="mm_gu", tm=2048, tn=512)
    gu3 = gu.reshape(B, S, 2 * D_FF)
    act = _conv_act_fwd(gu3, W["conv_w"], small["conv_b"], name="conv_act_fwd")
    act2 = act.reshape(T, D_FF)
    h2 = _matmul(act2, W["w_down"], addend=h, name="mm_h2", tm=1024, tn=512)

    g = {}
    dh2, dh2b, g["final_g"], loss = _final_loss_bwd(h2, small["final_g"].reshape(1, D), target.reshape(T, D), name="final_loss_bwd")
    dact = _matmul(dh2b, W["w_down"], tb=True, out_dtype=BF16, name="mm_dact", tm=1024, tn=D_FF)
    dw_down = _matmul(act2, dh2b, ta=True, out_dtype=BF16, name="mm_dw_down", tm=D_FF, tn=1024, tk=1024)
    dg_, dup, g["conv_w"], g["conv_b"] = _conv_act_bwd(gu3, W["conv_w"], small["conv_b"], dact.reshape(B, S, D_FF), name="conv_act_bwd")
    dg2 = dg_.reshape(T, D_FF)
    dup2 = dup.reshape(T, D_FF)
    du2 = _matmul(dg2, W["w_ffn"], tb=True, name="mm_du2_g", tm=1024, tn=512, b_koff=0)
    du2 = _matmul(dup2, W["w_ffn"], tb=True, addend=du2, name="mm_du2_u", tm=1024, tn=512, b_koff=1)
    dw_ffn_g = _matmul(u2, dg2, ta=True, out_dtype=BF16, name="mm_dw_ffn_g", tm=1024, tn=D_FF, tk=1024)
    dw_ffn_u = _matmul(u2, dup2, ta=True, out_dtype=BF16, name="mm_dw_ffn_u", tm=1024, tn=D_FF, tk=1024)
    tok = emit("ffn", dict(w_ffn_g=dw_ffn_g, w_ffn_u=dw_ffn_u, w_down=dw_down))
    dh, dhb, g["norm2_g"] = _norm_bwd_add(h, small["norm2_g"] + zero(tok), du2, dh2, name="norm2_bwd")
    dmerged = _matmul(dhb, W["w_out"], tb=True, out_dtype=BF16, name="mm_dmerged", tm=2048, tn=512)
    dw_out = _matmul(merged, dhb, ta=True, out_dtype=BF16, name="mm_dw_out", tm=1024, tn=1024, tk=2048)
    dgates, dpa, dpb = _merge_bwd(gates, pa, pb, dmerged, name="merge_bwd")
    doa = _matmul(dpa, W["w_a"], tb=True, out_dtype=BF16, name="mm_doa", tm=2048, tn=512)
    dw_a = _matmul(oa2, dpa, ta=True, out_dtype=BF16, name="mm_dw_a", tm=1024, tn=1024, tk=2048)
    dob = _matmul(dpb, W["w_b"], tb=True, out_dtype=BF16, name="mm_dob", tm=2048, tn=512)
    dw_b = _matmul(ob2, dpb, ta=True, out_dtype=BF16, name="mm_dw_b", tm=1024, tn=1024, tk=2048)
    tok = emit("mix", dict(w_out=dw_out, w_a=dw_a, w_b=dw_b))
    daq, dkv_cur, dkv_prev, dsinks = _swa_bwd(aq3, akv3, cos, sin, small["attn_sinks"] + zero(tok), lse, dob.reshape(B, S, D), name="swa_bwd")
    dakv = _swa_dkv_combine(dkv_cur, dkv_prev, name="swa_dkv").reshape(T, 2 * KV_W)
    daq2 = daq.reshape(T, D)
    g["attn_sinks"] = dsinks
    dzh, g["lb"], g["hgrn_norm_g"] = _hgrn_bwd(zh3, lb, small["hgrn_norm_g"], states, doa.reshape(B, S, D), name="hgrn_bwd")
    dzh2 = dzh.reshape(T, 4 * HF)
    dw_h = _matmul(u1, dzh2, ta=True, out_dtype=BF16, name="mm_dw_h", tm=1024, tn=2048, tk=1024)
    dw_g = _matmul(u1, dgates, ta=True, out_dtype=BF16, name="mm_dw_g", tm=1024, tn=2048, tk=1024)
    dw_aq = _matmul(u1, daq2, ta=True, out_dtype=BF16, name="mm_dw_aq", tm=1024, tn=1024, tk=2048)
    dw_akv = _matmul(u1, dakv, ta=True, out_dtype=BF16, name="mm_dw_akv", tm=1024, tn=256, tk=2048)
    tok = emit("in", dict(w_h=dw_h, w_g=dw_g, w_aq=dw_aq, w_akv=dw_akv))
    du1 = _matmul(dzh2, w_in, tb=True, after=tok, name="mm_du1_h", tm=1024, tn=512, b_koff=O_ZH // W_ZH)
    du1 = _matmul(dgates, w_in, tb=True, addend=du1, name="mm_du1_g", tm=1024, tn=512, b_koff=O_GATES // W_GATES)
    du1 = _matmul(daq2, w_in, tb=True, addend=du1, name="mm_du1_aq", tm=2048, tn=512, b_koff=O_AQ // W_AQ)
    du1 = _matmul(dakv, w_in, tb=True, addend=du1, name="mm_du1_akv", tm=2048, tn=512, b_koff=O_AKV // W_AKV)
    dx, _, g["norm1_g"] = _norm_bwd_add(x2, small["norm1_g"], du1, dh, name="norm1_bwd")
    g["lb_logits"] = _lb_bwd(g.pop("lb"), lb, name="lb_bwd")
    return loss, dx.reshape(B, S, D), g


def _my_place():
    return lax.axis_index("x"), lax.axis_index("y"), lax.axis_index("c")


def _all_gather(blk, *, in_vmem, reduce_sum=False, name):
    m, n = blk.shape
    space = pltpu.VMEM if in_vmem else pl.ANY

    def body(x_ref, out_ref, *rest):
        if reduce_sum:
            tot_ref, send_sems, recv_sems, local_sem = rest
        else:
            send_sems, recv_sems, local_sem = rest
        x, y, c = _my_place()
        me, sibling = (x, y, c), (x, y, 1 - c)
        chips = [(1 - x, y), (x, 1 - y), (1 - x, 1 - y)]

        def slot(px, py, pc):
            return out_ref.at[4 * px + 2 * py + pc]

        def copy(k, block, to, src=None):
            return pltpu.make_async_remote_copy(
                src_ref=slot(*block) if src is None else src, dst_ref=slot(*block),
                send_sem=send_sems.at[k], recv_sem=recv_sems.at[k], device_id=to, device_id_type=MESH)

        mine = pltpu.make_async_copy(x_ref, slot(*me), local_sem)
        mine.start()
        first = [copy(0, me, sibling, src=x_ref)]
        first += [copy(1 + j, me, (*chip, c), src=x_ref) for j, chip in enumerate(chips)]
        for cp in first:
            cp.start()
        passed = [copy(4 + j, (*chip, c), sibling) for j, chip in enumerate(chips)]
        for j, chip in enumerate(chips):
            copy(1 + j, (*chip, c), me).wait_recv()
            passed[j].start()
        copy(0, sibling, me).wait_recv()
        for j, chip in enumerate(chips):
            copy(4 + j, (*chip, 1 - c), me).wait_recv()
        for cp in first + passed:
            cp.wait_send()
        mine.wait()
        if reduce_sum:
            acc = out_ref[0]
            for p in range(1, N_DEV):
                acc = acc + out_ref[p]
            tot_ref[...] = acc

    out_shape = [jax.ShapeDtypeStruct((N_DEV, m, n), blk.dtype)]
    out_specs = [pl.BlockSpec(memory_space=space)]
    if reduce_sum:
        out_shape.append(jax.ShapeDtypeStruct((m, n), blk.dtype))
        out_specs.append(pl.BlockSpec(memory_space=pltpu.VMEM))
    res = pl.pallas_call(
        body, name=name,
        out_shape=out_shape,
        in_specs=[pl.BlockSpec(memory_space=space)],
        out_specs=out_specs,
        scratch_shapes=[pltpu.SemaphoreType.DMA((7,)), pltpu.SemaphoreType.DMA((7,)), pltpu.SemaphoreType.DMA],
    )(blk)
    return res if reduce_sum else res[0]


HBM_SPEC = pl.BlockSpec(memory_space=pltpu.HBM)
SEM_SPEC = pl.BlockSpec(memory_space=pltpu.SEMAPHORE)
DATAFLOW_EFFECT = pltpu.SideEffectType.DATAFLOW_SIDE_EFFECTING
N_PEERS = N_DEV - 1


def _peers(x, y, c):
    return [(1 - x if r & 4 else x, 1 - y if r & 2 else y, 1 - c if r & 1 else c) for r in range(1, N_DEV)]


def _exchange_start(srcs, scatter, *, name):
    n = len(srcs)
    lands = [lax.empty(a.shape if scatter else (N_DEV,) + a.shape, a.dtype) for a in srcs]

    def body(*refs):
        src_refs, land_refs = refs[:n], refs[n:2 * n]
        send_sems, recv_sems, token = refs[2 * n], refs[2 * n + 1], refs[-1]
        x, y, c = _my_place()
        me = 4 * x + 2 * y + c
        for i in range(n):
            for r, (tx, ty, tc) in enumerate(_peers(x, y, c)):
                src = src_refs[i].at[4 * tx + 2 * ty + tc] if scatter else src_refs[i]
                pltpu.make_async_remote_copy(
                    src_ref=src, dst_ref=land_refs[i].at[me], send_sem=send_sems.at[N_PEERS * i + r],
                    recv_sem=recv_sems.at[N_PEERS * i + r], device_id=(tx, ty, tc), device_id_type=MESH).start()
        token[...] = jnp.zeros_like(token)

    thru = [pltpu.HBM(a.shape, a.dtype) for a in list(srcs) + lands]
    res = pl.pallas_call(
        body, name=name,
        out_shape=(pltpu.SemaphoreType.DMA((N_PEERS * n,)), pltpu.SemaphoreType.DMA((N_PEERS * n,)), *thru,
                   jax.ShapeDtypeStruct((8, 128), F32)),
        in_specs=[HBM_SPEC] * (2 * n),
        out_specs=(SEM_SPEC, SEM_SPEC, *([HBM_SPEC] * (2 * n)), pl.BlockSpec(memory_space=pltpu.VMEM)),
        input_output_aliases={i: 2 + i for i in range(2 * n)},
        compiler_params=pltpu.CompilerParams(has_side_effects=DATAFLOW_EFFECT),
    )(*[pltpu.with_memory_space_constraint(a, pltpu.HBM) for a in list(srcs) + lands])
    return (res[0], res[1], list(res[2:2 + n]), list(res[2 + n:2 + 2 * n]), scatter), res[-1]


def _exchange_wait(handle, after, *, name):
    send_sems, recv_sems, srcs, lands, scatter = handle
    n = len(srcs)

    def body(*refs):
        src_refs, land_refs = refs[:n], refs[n:2 * n]
        send_sems, recv_sems = refs[2 * n], refs[2 * n + 1]
        x, y, c = _my_place()
        for i in range(n):
            for r in range(N_PEERS):
                src = src_refs[i].at[0] if scatter else src_refs[i]
                cp = pltpu.make_async_remote_copy(
                    src_ref=src, dst_ref=land_refs[i].at[0], send_sem=send_sems.at[N_PEERS * i + r],
                    recv_sem=recv_sems.at[N_PEERS * i + r], device_id=(x, y, c), device_id_type=MESH)
                cp.wait_send()
                cp.wait_recv()

    thru = [pltpu.HBM(a.shape, a.dtype) for a in srcs + lands]
    res = pl.pallas_call(
        body, name=name, out_shape=tuple(thru),
        in_specs=[HBM_SPEC] * (2 * n) + [SEM_SPEC, SEM_SPEC, ANY], out_specs=tuple([HBM_SPEC] * (2 * n)),
        input_output_aliases={i: i for i in range(2 * n)},
        compiler_params=pltpu.CompilerParams(has_side_effects=DATAFLOW_EFFECT),
    )(*srcs, *lands, send_sems, recv_sems, after)
    return list(res[:n]), list(res[n:])


def _with_own(land, own, me):
    return lax.dynamic_update_index_in_dim(land, own, me, 0)


def _adamw_math(w, g, m, v):
    m = ADAM_B1 * m + (1.0 - ADAM_B1) * g
    v = ADAM_B2 * v + (1.0 - ADAM_B2) * (g * g)
    m_hat = m / (1.0 - ADAM_B1 ** ADAM_STEP)
    v_hat = v / (1.0 - ADAM_B2 ** ADAM_STEP)
    delta = -ADAM_LR * (m_hat / (jnp.sqrt(v_hat) + ADAM_EPS) + ADAM_WD * w)
    return delta, m, v


def _adamw_sum(parts, w, m, v, *, name):
    shape = w.shape
    R, n = shape[-2], shape[-1]
    w, m, v = (t.reshape(R, n) for t in (w, m, v))
    tr = _pick(R, (256, 176, 128))

    def body(p_ref, w_ref, m_ref, v_ref, g_ref, d_ref, mo_ref, vo_ref):
        g = p_ref[0].astype(F32)
        for p in range(1, N_DEV):
            g = g + p_ref[p].astype(F32)
        d, mn, vn = _adamw_math(w_ref[...], g, m_ref[...], v_ref[...])
        g_ref[...] = g
        d_ref[...] = d
        mo_ref[...] = mn
        vo_ref[...] = vn

    row = pl.BlockSpec((tr, n), lambda i: (i, 0))
    outs = pl.pallas_call(
        body, name=name, grid=(R // tr,),
        in_specs=[pl.BlockSpec((N_DEV, tr, n), lambda i: (0, i, 0)), row, row, row],
        out_specs=[row, row, row, row],
        out_shape=[jax.ShapeDtypeStruct((R, n), F32)] * 4,
        compiler_params=_params("parallel"),
    )(parts, w, m, v)
    return [t.reshape(shape) for t in outs]


def _adamw_small(g, w, m, v, *, name):
    def body(g_ref, w_ref, m_ref, v_ref, d_ref, mo_ref, vo_ref):
        d, mn, vn = _adamw_math(w_ref[...], g_ref[...], m_ref[...], v_ref[...])
        d_ref[...] = d
        mo_ref[...] = mn
        vo_ref[...] = vn

    return pl.pallas_call(body, name=name, out_shape=[jax.ShapeDtypeStruct(w.shape, F32)] * 3)(g, w, m, v)


def _lb_bwd(dlb, lb, *, name):
    def body(d_ref, lb_ref, o_ref):
        t = d_ref[...] * lb_ref[...] * (1.0 - lb_ref[...])
        o_ref[0:1, :] = t
        o_ref[1:2, :] = -t

    return pl.pallas_call(body, name=name, out_shape=jax.ShapeDtypeStruct((2, lb.shape[1]), F32))(dlb, lb)


LANES = 128
N_IN, N_FFN = 7424, 5632
IN_BLK, FFN_BLK, DOWN_BLK, ROW_BLK = N_IN // N_DEV, N_FFN // N_DEV, D_FF // N_DEV, D_MODEL // N_DEV
CONVW_BLK = D_FF // N_DEV
SMALL_NAMES = ("norm1_g", "lb_logits", "hgrn_norm_g", "attn_sinks", "norm2_g", "conv_b", "final_g")
CONV_BITS_SHAPE = (16, 256)


def _cols_from_blocks(blocks):
    n, rows, width = blocks.shape
    return blocks.transpose(1, 0, 2).reshape(rows, n * width)


def _blocks_from_cols(full):
    rows, cols = full.shape
    return full.reshape(rows, N_DEV, cols // N_DEV).transpose(1, 0, 2)


def _to_rows(vec, rows):
    vec = vec.reshape(-1)
    return jnp.pad(vec, (0, rows * LANES - vec.shape[0])).reshape(rows, LANES)


def kernel(x, positions, norm1_g, w_in, lb_logits, hgrn_norm_g, w_a, attn_sinks, w_b, w_out, norm2_g, w_ffn_in, conv_w, conv_b, w_down, final_g, loss_target, m_norm1_g, m_w_in, m_lb_logits, m_hgrn_norm_g, m_w_a, m_attn_sinks, m_w_b, m_w_out, m_norm2_g, m_w_ffn_in, m_conv_w, m_conv_b, m_w_down, m_final_g, v_norm1_g, v_w_in, v_lb_logits, v_hgrn_norm_g, v_w_a, v_attn_sinks, v_w_b, v_w_out, v_norm2_g, v_w_ffn_in, v_conv_w, v_conv_b, v_w_down, v_final_g):
    xi, yi, ci = _my_place()
    dev = 4 * xi + 2 * yi + ci

    w_in_blocks = _all_gather(w_in[0].astype(BF16), in_vmem=False, name="ag_w_in")
    conv_bits = lax.bitcast_convert_type(conv_w, BF16).reshape(-1)
    conv_bits = jnp.pad(conv_bits, (0, CONV_BITS_SHAPE[0] * CONV_BITS_SHAPE[1] - conv_bits.shape[0])).reshape(CONV_BITS_SHAPE)
    rest_own = [w_a[0].astype(BF16), w_b[0].astype(BF16), w_out[0].astype(BF16), w_ffn_in[0].astype(BF16), w_down[0].astype(BF16), conv_bits]
    rest_handle, start_token = _exchange_start(rest_own, False, name="ag_rest_start")

    def rest_weights(after):
        own, lands = _exchange_wait(rest_handle, after, name="ag_rest_wait")
        fa, fb, fo, fffn, fdown, fconv = [_with_own(l, o, dev) for l, o in zip(lands, own)]
        bits = fconv.reshape(N_DEV, -1)[:, :3 * CONVW_BLK * 2].reshape(N_DEV, 3, CONVW_BLK, 2)
        return dict(w_a=fa.reshape(D_MODEL, D_MODEL), w_b=fb.reshape(D_MODEL, D_MODEL), w_out=fo.reshape(D_MODEL, D_MODEL),
                    w_ffn=_cols_from_blocks(fffn), w_down=fdown.reshape(D_FF, D_MODEL),
                    conv_w=_cols_from_blocks(lax.bitcast_convert_type(bits, F32)))

    handles = {}

    def emit(group, gr):
        if group == "ffn":
            srcs = [_blocks_from_cols(jnp.concatenate([gr["w_ffn_g"], gr["w_ffn_u"]], axis=1)), gr["w_down"].reshape(N_DEV, DOWN_BLK, D_MODEL)]
        elif group == "mix":
            srcs = [gr[n].reshape(N_DEV, ROW_BLK, D_MODEL) for n in ("w_out", "w_a", "w_b")]
        else:
            srcs = [_blocks_from_cols(jnp.concatenate([gr["w_h"], gr["w_aq"], gr["w_akv"], gr["w_g"]], axis=1))]
        handles[group], token = _exchange_start(srcs, True, name="rs_" + group + "_start")
        return token

    small = dict(norm1_g=norm1_g, lb_logits=lb_logits, hgrn_norm_g=hgrn_norm_g, attn_sinks=attn_sinks, norm2_g=norm2_g,
                 conv_b=conv_b, final_g=final_g)
    w_in_full = _reorder_w_in(_cols_from_blocks(w_in_blocks))
    loss, grad_x, g = _local_step(x, positions, loss_target, small, w_in_full, rest_weights, emit, start_token)

    def parts_of(group, after):
        srcs, lands = _exchange_wait(handles[group], after, name="rs_" + group + "_wait")
        return [_with_own(l, lax.dynamic_index_in_dim(s, dev, 0, keepdims=False), dev) for s, l in zip(srcs, lands)]

    p_ffn, p_down = parts_of("ffn", grad_x)
    p_out, p_a, p_b = parts_of("mix", grad_x)
    (p_in,) = parts_of("in", grad_x)
    big = dict(
        w_in=_adamw_sum(p_in, w_in, m_w_in, v_w_in, name="adamw_w_in"),
        w_a=_adamw_sum(p_a, w_a, m_w_a, v_w_a, name="adamw_w_a"),
        w_b=_adamw_sum(p_b, w_b, m_w_b, v_w_b, name="adamw_w_b"),
        w_out=_adamw_sum(p_out, w_out, m_w_out, v_w_out, name="adamw_w_out"),
        w_ffn_in=_adamw_sum(p_ffn, w_ffn_in, m_w_ffn_in, v_w_ffn_in, name="adamw_w_ffn_in"),
        w_down=_adamw_sum(p_down, w_down, m_w_down, v_w_down, name="adamw_w_down"),
    )

    sm_g = dict(norm1_g=g["norm1_g"], lb_logits=g["lb_logits"], hgrn_norm_g=g["hgrn_norm_g"], attn_sinks=g["attn_sinks"][:, 0],
                norm2_g=g["norm2_g"], conv_b=g["conv_b"], final_g=g["final_g"])
    vec = jnp.concatenate([sm_g[n].reshape(-1) for n in SMALL_NAMES] + [g["conv_w"].reshape(-1), loss.reshape(-1)])
    sm_rows = 136
    _, total = _all_gather(_to_rows(vec, sm_rows), in_vmem=True, reduce_sum=True, name="ar_small")
    total = total.reshape(-1)
    sm_w = dict(norm1_g=norm1_g, lb_logits=lb_logits, hgrn_norm_g=hgrn_norm_g, attn_sinks=attn_sinks, norm2_g=norm2_g,
                conv_b=conv_b, final_g=final_g)
    sm_m = dict(norm1_g=m_norm1_g, lb_logits=m_lb_logits, hgrn_norm_g=m_hgrn_norm_g, attn_sinks=m_attn_sinks, norm2_g=m_norm2_g,
                conv_b=m_conv_b, final_g=m_final_g)
    sm_v = dict(norm1_g=v_norm1_g, lb_logits=v_lb_logits, hgrn_norm_g=v_hgrn_norm_g, attn_sinks=v_attn_sinks, norm2_g=v_norm2_g,
                conv_b=v_conv_b, final_g=v_final_g)
    sizes = [sm_w[n].size for n in SMALL_NAMES]
    n_rep = sum(sizes)
    g_conv_full = total[n_rep:n_rep + 3 * D_FF].reshape(3, D_FF)
    g_conv = lax.dynamic_slice_in_dim(g_conv_full, dev * CONVW_BLK, CONVW_BLK, axis=1)
    loss_total = total[n_rep + 3 * D_FF]
    ad_rows = 72
    pack_small = lambda d, cw: _to_rows(jnp.concatenate([d[n].reshape(-1) for n in SMALL_NAMES] + [cw.reshape(-1)]), ad_rows)
    g_small = _to_rows(jnp.concatenate([total[:n_rep], g_conv.reshape(-1)]), ad_rows)
    d_s, m_s, v_s = _adamw_small(g_small, pack_small(sm_w, conv_w), pack_small(sm_m, m_conv_w), pack_small(sm_v, v_conv_w), name="adamw_small")

    def unpack_small(t):
        t = t.reshape(-1)
        out, off = {}, 0
        for n, s in zip(SMALL_NAMES, sizes):
            out[n] = t[off:off + s].reshape(sm_w[n].shape)
            off += s
        out["conv_w"] = t[off:off + 3 * CONVW_BLK].reshape(1, 3, CONVW_BLK)
        return out

    names = ("norm1_g", "w_in", "lb_logits", "hgrn_norm_g", "w_a", "attn_sinks", "w_b", "w_out", "norm2_g", "w_ffn_in", "conv_w", "conv_b", "w_down", "final_g")
    outs = [loss_total.reshape(()), grad_x]
    for kind, s_vec in enumerate((g_small, d_s, m_s, v_s)):
        s_un = unpack_small(s_vec)
        outs += [big[n][kind] if n in big else s_un[n] for n in names]
    return tuple(outs)
```

```python
import functools

import jax
import jax.numpy as jnp
from jax import lax
from jax.experimental import pallas as pl
from jax.experimental.pallas import tpu as pltpu

F32 = jnp.float32
BF16 = jnp.bfloat16

D_MODEL = 1024
HGRN_HEADS = 8
HGRN_DK = 128
CHUNK = 64
ATT_HEADS = 16
ATT_KV_HEADS = 2
ATT_HD = 64
ATT_GROUP = ATT_HEADS // ATT_KV_HEADS
WINDOW = 128
ROPE_DIM = ATT_HD // 4
ROPE_THETA = 500000.0
D_FF = 2816
EPS = 1e-6
NEG_INF = -1e30
N_DEV = 8

ADAM_LR = 0.001
ADAM_B1 = 0.9
ADAM_B2 = 0.999
ADAM_EPS = 1e-08
ADAM_WD = 0.01
ADAM_STEP = 10

MESH = pl.DeviceIdType.MESH
ANY = pl.BlockSpec(memory_space=pl.ANY)


def _pick(n, cands):
    for c in cands:
        if n % c == 0:
            return c
    return n


def _sigmoid(x):
    return 1.0 / (1.0 + jnp.exp(-x))


def _silu(x):
    return x * _sigmoid(x)


def _rms(x, g):
    return x * lax.rsqrt(jnp.mean(x * x, axis=-1, keepdims=True) + EPS) * g


def _dot(a, b, dims):
    return lax.dot_general(a, b, (dims, ((), ())), preferred_element_type=F32)


def _nn(a, b):
    return _dot(a, b, ((1,), (0,)))


def _nt(a, b):
    return _dot(a, b, ((1,), (1,)))


def _tn(a, b):
    return _dot(a, b, ((0,), (0,)))


def _params(*sem):
    return pltpu.CompilerParams(dimension_semantics=sem, vmem_limit_bytes=56 * 1024 * 1024)


def _matmul(a, b, *, ta=False, tb=False, out_dtype=F32, addend=None, after=None, name, tm, tn, tk=None, n_extent=None, b_koff=0, b_noff=0):
    M, K = (a.shape[1], a.shape[0]) if ta else a.shape
    N = n_extent or (b.shape[0] if tb else b.shape[1])
    tm, tn, tk = min(tm, M), min(tn, N), min(tk or K, K)
    assert M % tm == 0 and N % tn == 0 and K % tk == 0, (name, M, N, K, tm, tn, tk)
    nk = K // tk
    use_scratch = nk > 1 and out_dtype != F32
    grid = (M // tm, N // tn, nk)
    a_spec = pl.BlockSpec((tk, tm), lambda i, j, k: (k, i)) if ta else pl.BlockSpec((tm, tk), lambda i, j, k: (i, k))
    b_spec = pl.BlockSpec((tn, tk), lambda i, j, k: (j + b_noff, k + b_koff)) if tb else pl.BlockSpec((tk, tn), lambda i, j, k: (k + b_koff, j + b_noff))
    o_spec = pl.BlockSpec((tm, tn), lambda i, j, k: (i, j))
    dims = ((0 if ta else 1,), (1 if tb else 0,))
    has_add = addend is not None

    n_in = 2 + has_add + (after is not None)

    def body(*refs):
        a_ref, b_ref = refs[:2]
        c_ref = refs[2] if has_add else None
        o_ref = refs[n_in]
        part = _dot(a_ref[...], b_ref[...], dims)
        if nk == 1:
            if has_add:
                part = part + c_ref[...].astype(F32)
            o_ref[...] = part.astype(out_dtype)
        else:
            acc_ref = refs[-1] if use_scratch else o_ref
            k = pl.program_id(2)

            @pl.when(k == 0)
            def _():
                acc_ref[...] = part + c_ref[...].astype(F32) if has_add else part

            @pl.when(k > 0)
            def _():
                acc_ref[...] += part

            if use_scratch:
                @pl.when(k == nk - 1)
                def _():
                    o_ref[...] = acc_ref[...].astype(out_dtype)

    in_specs = [a_spec, b_spec] + ([o_spec] if has_add else [])
    args = (a, b) + ((addend,) if has_add else ())
    if after is not None:
        in_specs.append(pl.BlockSpec(after.shape, lambda i, j, k: (0, 0)))
        args += (after,)
    return pl.pallas_call(
        body,
        name=name,
        grid=grid,
        in_specs=in_specs,
        out_specs=o_spec,
        out_shape=jax.ShapeDtypeStruct((M, N), out_dtype),
        scratch_shapes=[pltpu.VMEM((tm, tn), F32)] if use_scratch else [],
        compiler_params=_params("parallel", "parallel", "arbitrary"),
    )(*args)


def _row_spec(tm, n):
    return pl.BlockSpec((tm, n), lambda i: (i, 0))


def _full_spec(shape):
    return pl.BlockSpec(shape, lambda i: tuple(0 for _ in shape))


def _norm_cast(x, g, *, name):
    T, D = x.shape
    tm = _pick(T, (512, 256, 128))

    def body(x_ref, g_ref, u_ref):
        u_ref[...] = _rms(x_ref[...], g_ref[...]).astype(BF16)

    return pl.pallas_call(
        body, name=name, grid=(T // tm,),
        in_specs=[_row_spec(tm, D), _full_spec((1, D))],
        out_specs=_row_spec(tm, D),
        out_shape=jax.ShapeDtypeStruct((T, D), BF16),
        compiler_params=_params("parallel"),
    )(x, g)


def _norm_bwd_add(x, g, du, dres, *, name):
    T, D = x.shape
    tm = _pick(T, (512, 256, 128))
    has_res = dres is not None

    def body(*refs):
        if has_res:
            x_ref, g_ref, du_ref, dr_ref, dx_ref, dxb_ref, dg_ref = refs
        else:
            x_ref, g_ref, du_ref, dx_ref, dxb_ref, dg_ref = refs
        _, vjp = jax.vjp(_rms, x_ref[...], g_ref[...])
        dx, dg = vjp(du_ref[...].astype(F32))
        if has_res:
            dx = dx + dr_ref[...]
        dx_ref[...] = dx
        dxb_ref[...] = dx.astype(BF16)

        @pl.when(pl.program_id(0) == 0)
        def _():
            dg_ref[...] = jnp.zeros_like(dg_ref)

        dg_ref[...] += dg

    ins = [x, g, du] + ([dres] if has_res else [])
    in_specs = [_row_spec(tm, D), _full_spec((1, D)), _row_spec(tm, D)] + ([_row_spec(tm, D)] if has_res else [])
    return pl.pallas_call(
        body, name=name, grid=(T // tm,),
        in_specs=in_specs,
        out_specs=[_row_spec(tm, D), _row_spec(tm, D), _full_spec((1, D))],
        out_shape=[jax.ShapeDtypeStruct((T, D), F32), jax.ShapeDtypeStruct((T, D), BF16), jax.ShapeDtypeStruct((1, D), F32)],
        compiler_params=_params("arbitrary"),
    )(*ins)


def _final_loss_bwd(h2, g, target, *, name):
    T, D = h2.shape
    tm = _pick(T, (512, 256, 128))

    def body(h_ref, g_ref, t_ref, dx_ref, dxb_ref, dg_ref, loss_ref):
        y, vjp = jax.vjp(_rms, h_ref[...], g_ref[...])
        err = y - t_ref[...]
        dx, dg = vjp(err * (1.0 / D))
        dx_ref[...] = dx
        dxb_ref[...] = dx.astype(BF16)

        @pl.when(pl.program_id(0) == 0)
        def _():
            dg_ref[...] = jnp.zeros_like(dg_ref)
            loss_ref[...] = jnp.zeros_like(loss_ref)

        dg_ref[...] += dg
        loss_ref[...] += (0.5 / D) * jnp.sum(jnp.sum(err * err, axis=1, keepdims=True), axis=0, keepdims=True)

    return pl.pallas_call(
        body, name=name, grid=(T // tm,),
        in_specs=[_row_spec(tm, D), _full_spec((1, D)), _row_spec(tm, D)],
        out_specs=[_row_spec(tm, D), _row_spec(tm, D), _full_spec((1, D)), _full_spec((1, 1))],
        out_shape=[jax.ShapeDtypeStruct((T, D), F32), jax.ShapeDtypeStruct((T, D), BF16), jax.ShapeDtypeStruct((1, D), F32), jax.ShapeDtypeStruct((1, 1), F32)],
        compiler_params=_params("arbitrary"),
    )(h2, g, target)


def _merge_fn(gates, a, b):
    ga = gates[:, :D_MODEL].astype(F32)
    gb = gates[:, D_MODEL:].astype(F32)
    return _sigmoid(ga) * a.astype(F32) + _sigmoid(gb) * b.astype(F32)


def _merge_fwd(gates, a, b, *, name):
    T = a.shape[0]
    tm = _pick(T, (512, 256, 128))

    def body(g_ref, a_ref, b_ref, o_ref):
        o_ref[...] = _merge_fn(g_ref[...], a_ref[...], b_ref[...]).astype(BF16)

    return pl.pallas_call(
        body, name=name, grid=(T // tm,),
        in_specs=[_row_spec(tm, 2 * D_MODEL), _row_spec(tm, D_MODEL), _row_spec(tm, D_MODEL)],
        out_specs=_row_spec(tm, D_MODEL),
        out_shape=jax.ShapeDtypeStruct((T, D_MODEL), BF16),
        compiler_params=_params("parallel"),
    )(gates, a, b)


def _merge_bwd(gates, a, b, dmerged, *, name):
    T = a.shape[0]
    tm = _pick(T, (512, 256, 128))

    def body(g_ref, a_ref, b_ref, dm_ref, dg_ref, da_ref, db_ref):
        g = g_ref[...].astype(F32)
        dm = dm_ref[...].astype(F32)
        sa = _sigmoid(g[:, :D_MODEL])
        sb = _sigmoid(g[:, D_MODEL:])
        da_ref[...] = (dm * sa).astype(BF16)
        db_ref[...] = (dm * sb).astype(BF16)
        dg_ref[:, :D_MODEL] = (dm * a_ref[...].astype(F32) * sa * (1.0 - sa)).astype(BF16)
        dg_ref[:, D_MODEL:] = (dm * b_ref[...].astype(F32) * sb * (1.0 - sb)).astype(BF16)

    return pl.pallas_call(
        body, name=name, grid=(T // tm,),
        in_specs=[_row_spec(tm, 2 * D_MODEL), _row_spec(tm, D_MODEL), _row_spec(tm, D_MODEL), _row_spec(tm, D_MODEL)],
        out_specs=[_row_spec(tm, 2 * D_MODEL), _row_spec(tm, D_MODEL), _row_spec(tm, D_MODEL)],
        out_shape=[jax.ShapeDtypeStruct((T, 2 * D_MODEL), BF16), jax.ShapeDtypeStruct((T, D_MODEL), BF16), jax.ShapeDtypeStruct((T, D_MODEL), BF16)],
        compiler_params=_params("parallel"),
    )(gates, a, b, dmerged)


CONV_TC = 256


def _shift_down(x, n, rows):
    return jnp.where(rows >= n, pltpu.roll(x, n, 0), 0.0)


def _shift_up(x, n, rows, S):
    return jnp.where(rows < S - n, pltpu.roll(x, S - n, 0), 0.0)


def _conv_act_fwd(gu, conv_w, conv_b, *, name):
    B, S, _ = gu.shape
    tc = CONV_TC
    nc = D_FF // tc

    def body(g_ref, up_ref, w_ref, b_ref, o_ref):
        g = g_ref[...].astype(F32)
        rows = lax.broadcasted_iota(jnp.int32, g.shape, 0)
        w = w_ref[...]
        a = w[2:3] * g + w[1:2] * _shift_down(g, 1, rows) + w[0:1] * _shift_down(g, 2, rows) + b_ref[...]
        o_ref[...] = (_silu(a) * up_ref[...].astype(F32)).astype(BF16)

    return pl.pallas_call(
        body, name=name, grid=(B, nc),
        in_specs=[pl.BlockSpec((None, S, tc), lambda b, j: (b, 0, j)),
                  pl.BlockSpec((None, S, tc), lambda b, j: (b, 0, j + nc)),
                  pl.BlockSpec((3, tc), lambda b, j: (0, j)),
                  pl.BlockSpec((1, tc), lambda b, j: (0, j))],
        out_specs=pl.BlockSpec((None, S, tc), lambda b, j: (b, 0, j)),
        out_shape=jax.ShapeDtypeStruct((B, S, D_FF), BF16),
        compiler_params=_params("parallel", "parallel"),
    )(gu, gu, conv_w, conv_b)


def _conv_act_bwd(gu, conv_w, conv_b, dact, *, name):
    B, S, _ = gu.shape
    tc = CONV_TC
    nc = D_FF // tc

    def body(g_ref, up_ref, w_ref, b_ref, da_ref, dg_ref, dup_ref, dw_ref, db_ref):
        g = g_ref[...].astype(F32)
        up = up_ref[...].astype(F32)
        dact = da_ref[...].astype(F32)
        rows = lax.broadcasted_iota(jnp.int32, g.shape, 0)
        w = w_ref[...]
        g1 = _shift_down(g, 1, rows)
        g2 = _shift_down(g, 2, rows)
        a = w[2:3] * g + w[1:2] * g1 + w[0:1] * g2 + b_ref[...]
        sg = _sigmoid(a)
        dup_ref[...] = (dact * a * sg).astype(BF16)
        da = dact * up * sg * (1.0 + a * (1.0 - sg))
        dg = w[2:3] * da + w[1:2] * _shift_up(da, 1, rows, S) + w[0:1] * _shift_up(da, 2, rows, S)
        dg_ref[...] = dg.astype(BF16)

        @pl.when(pl.program_id(1) == 0)
        def _():
            dw_ref[...] = jnp.zeros_like(dw_ref)
            db_ref[...] = jnp.zeros_like(db_ref)

        dw_ref[0:1, :] += jnp.sum(da * g2, axis=0, keepdims=True)
        dw_ref[1:2, :] += jnp.sum(da * g1, axis=0, keepdims=True)
        dw_ref[2:3, :] += jnp.sum(da * g, axis=0, keepdims=True)
        db_ref[...] += jnp.sum(da, axis=0, keepdims=True)

    col = lambda j, b: (b, 0, j)
    return pl.pallas_call(
        body, name=name, grid=(nc, B),
        in_specs=[pl.BlockSpec((None, S, tc), col),
                  pl.BlockSpec((None, S, tc), lambda j, b: (b, 0, j + nc)),
                  pl.BlockSpec((3, tc), lambda j, b: (0, j)),
                  pl.BlockSpec((1, tc), lambda j, b: (0, j)),
                  pl.BlockSpec((None, S, tc), col)],
        out_specs=[pl.BlockSpec((None, S, tc), col), pl.BlockSpec((None, S, tc), col),
                   pl.BlockSpec((3, tc), lambda j, b: (0, j)), pl.BlockSpec((1, tc), lambda j, b: (0, j))],
        out_shape=[jax.ShapeDtypeStruct((B, S, D_FF), BF16), jax.ShapeDtypeStruct((B, S, D_FF), BF16),
                   jax.ShapeDtypeStruct((3, D_FF), F32), jax.ShapeDtypeStruct((1, D_FF), F32)],
        compiler_params=_params("parallel", "arbitrary"),
    )(gu, gu, conv_w, conv_b, dact)


HGRN_CPB = 4
HF = HGRN_HEADS * HGRN_DK


def _tri(n, upper=False):
    r = lax.broadcasted_iota(jnp.int32, (n, n), 0)
    c = lax.broadcasted_iota(jnp.int32, (n, n), 1)
    return (c >= r) if upper else (r >= c)


def _hs(h):
    return slice(h * HGRN_DK, (h + 1) * HGRN_DK)


def _cumsum_rows(tri_b, x):
    hi = x.astype(BF16)
    lo = (x - hi.astype(F32)).astype(BF16)
    return _nn(tri_b, hi) + _nn(tri_b, lo)


def _hgrn_pre(q, fz, lb, tril_b):
    qf = _silu(q)
    sg = _sigmoid(fz)
    f = lb + (1.0 - lb) * sg
    k = 1.0 - f
    b = _cumsum_rows(tril_b, jnp.log2(f))
    bref = b[CHUNK // 2:CHUNK // 2 + 1, :]
    blast = b[CHUNK - 1:CHUNK, :]
    e1 = jnp.exp2(b - bref)
    e2 = jnp.exp2(bref - b)
    e3 = jnp.exp2(b)
    e4 = jnp.exp2(blast - b)
    dec = jnp.exp2(blast)
    return sg, f, (e1, e2, e3, e4), qf * e1, k * e2, qf * e3, k * e4, dec


def _hgrn_fwd(zh, lb, gn, *, name):
    B, S, _ = zh.shape
    cpb = HGRN_CPB
    ts = cpb * CHUNK
    nblk = S // ts

    def body(z_ref, lb_ref, gn_ref, o_ref, st_ref, state):
        @pl.when(pl.program_id(1) == 0)
        def _():
            state[...] = jnp.zeros_like(state)

        H = HGRN_HEADS
        causal = _tri(CHUNK)
        tril_b = causal.astype(BF16)
        lb = lb_ref[...]
        for c in range(cpb):
            rows = slice(c * CHUNK, (c + 1) * CHUNK)
            q = z_ref[rows, 0:HF].astype(F32)
            fz = z_ref[rows, HF:2 * HF].astype(F32)
            v = z_ref[rows, 2 * HF:3 * HF]
            hg = z_ref[rows, 3 * HF:4 * HF].astype(F32)
            _, _, _, q_in, k_in, q_out, k_st, dec = _hgrn_pre(q, fz, lb, tril_b)
            q_in, k_in, q_out, k_st = (t.astype(BF16) for t in (q_in, k_in, q_out, k_st))
            a = [jnp.where(causal, _nt(q_in[:, _hs(h)], k_in[:, _hs(h)]), 0.0).astype(BF16) for h in range(H)]
            st = [state[h] for h in range(H)]
            for h in range(H):
                st_ref[c, h] = st[h]
            o = [_nn(a[h], v[:, _hs(h)]) + _nt(q_out[:, _hs(h)], st[h].astype(BF16)) for h in range(H)]
            for h in range(H):
                state[h] = st[h] * dec[:, _hs(h)] + _tn(v[:, _hs(h)], k_st[:, _hs(h)])
            gate = _silu(hg)
            for h in range(H):
                o_ref[rows, _hs(h)] = (_rms(o[h], gn_ref[...]) * gate[:, _hs(h)]).astype(BF16)

    return pl.pallas_call(
        body, name=name, grid=(B, nblk),
        in_specs=[pl.BlockSpec((None, ts, 4 * HF), lambda b, s: (b, s, 0)),
                  pl.BlockSpec((1, HF), lambda b, s: (0, 0)),
                  pl.BlockSpec((1, HGRN_DK), lambda b, s: (0, 0))],
        out_specs=[pl.BlockSpec((None, ts, HF), lambda b, s: (b, s, 0)),
                   pl.BlockSpec((None, cpb, HGRN_HEADS, HGRN_DK, HGRN_DK), lambda b, s: (b, s, 0, 0, 0))],
        out_shape=[jax.ShapeDtypeStruct((B, S, HF), BF16),
                   jax.ShapeDtypeStruct((B, S // CHUNK, HGRN_HEADS, HGRN_DK, HGRN_DK), F32)],
        scratch_shapes=[pltpu.VMEM((HGRN_HEADS, HGRN_DK, HGRN_DK), F32)],
        compiler_params=_params("arbitrary", "arbitrary"),
    )(zh, lb, gn)


def _hgrn_bwd(zh, lb, gn, states, doa, *, name):
    B, S, _ = zh.shape
    cpb = HGRN_CPB
    ts = cpb * CHUNK
    nblk = S // ts
    rev = lambda b, s: (b, nblk - 1 - s, 0)

    def body(z_ref, lb_ref, gn_ref, st_ref, do_ref, dz_ref, dlb_ref, dgn_ref, dstate):
        @pl.when(pl.program_id(1) == 0)
        def _():
            dstate[...] = jnp.zeros_like(dstate)

        @pl.when((pl.program_id(0) == 0) & (pl.program_id(1) == 0))
        def _():
            dlb_ref[...] = jnp.zeros_like(dlb_ref)
            dgn_ref[...] = jnp.zeros_like(dgn_ref)

        H = HGRN_HEADS
        cat = lambda xs: jnp.concatenate(xs, axis=1)
        causal = _tri(CHUNK)
        tril_b = causal.astype(BF16)
        triu_b = _tri(CHUNK, upper=True).astype(BF16)
        rowid = lax.broadcasted_iota(jnp.int32, (CHUNK, HF), 0)
        lb = lb_ref[...]
        gn = gn_ref[...]
        for c in reversed(range(cpb)):
            rows = slice(c * CHUNK, (c + 1) * CHUNK)
            q = z_ref[rows, 0:HF].astype(F32)
            fz = z_ref[rows, HF:2 * HF].astype(F32)
            v = z_ref[rows, 2 * HF:3 * HF]
            hg = z_ref[rows, 3 * HF:4 * HF].astype(F32)
            sg, f, (e1, e2, e3, e4), q_in, k_in, q_out, k_st, dec = _hgrn_pre(q, fz, lb, tril_b)
            q_in_b, k_in_b, q_out_b, k_st_b = (t.astype(BF16) for t in (q_in, k_in, q_out, k_st))
            a_b = [jnp.where(causal, _nt(q_in_b[:, _hs(h)], k_in_b[:, _hs(h)]), 0.0).astype(BF16) for h in range(H)]
            st = [st_ref[c, h] for h in range(H)]
            st_b = [t.astype(BF16) for t in st]
            o = [_nn(a_b[h], v[:, _hs(h)]) + _nt(q_out_b[:, _hs(h)], st_b[h]) for h in range(H)]
            dout = do_ref[rows, :].astype(F32)
            shg = _sigmoid(hg)
            gate = hg * shg
            do_l, dgn_acc = [], jnp.zeros_like(gn)
            for h in range(H):
                _, norm_vjp = jax.vjp(_rms, o[h], gn)
                d_o, d_gn = norm_vjp(dout[:, _hs(h)] * gate[:, _hs(h)])
                do_l.append(d_o)
                dgn_acc = dgn_acc + d_gn
            dgn_ref[...] += dgn_acc
            on = cat([_rms(o[h], gn) for h in range(H)])
            dhg = dout * on * shg * (1.0 + hg * (1.0 - shg))
            do_b = [t.astype(BF16) for t in do_l]
            dst = [dstate[h] for h in range(H)]
            dst_b = [t.astype(BF16) for t in dst]
            da_b = [jnp.where(causal, _nt(do_b[h], v[:, _hs(h)]), 0.0).astype(BF16) for h in range(H)]
            dv = cat([_tn(a_b[h], do_b[h]) + _nt(k_st_b[:, _hs(h)], dst_b[h]) for h in range(H)])
            dq_in = cat([_nn(da_b[h], k_in_b[:, _hs(h)]) for h in range(H)])
            dk_in = cat([_tn(da_b[h], q_in_b[:, _hs(h)]) for h in range(H)])
            dq_out = cat([_nn(do_b[h], st_b[h]) for h in range(H)])
            dk_st = cat([_nn(v[:, _hs(h)], dst_b[h]) for h in range(H)])
            ddec = cat([jnp.sum(st[h] * dst[h], axis=0, keepdims=True) for h in range(H)])
            for h in range(H):
                dstate[h] = dst[h] * dec[:, _hs(h)] + _tn(do_b[h], q_out_b[:, _hs(h)])
            t_qin = dq_in * q_in
            t_kin = dk_in * k_in
            t_kst = dk_st * k_st
            db = t_qin - t_kin + dq_out * q_out - t_kst
            dbref = jnp.sum(t_kin - t_qin, axis=0, keepdims=True)
            dblast = jnp.sum(t_kst, axis=0, keepdims=True) + ddec * dec
            db = db + jnp.where(rowid == CHUNK // 2, dbref, 0.0) + jnp.where(rowid == CHUNK - 1, dblast, 0.0)
            dlogf = _cumsum_rows(triu_b, db)
            dqf = dq_in * e1 + dq_out * e3
            dk = dk_in * e2 + dk_st * e4
            df = dlogf / f - dk
            dfz = df * (1.0 - lb) * sg * (1.0 - sg)
            dlb_ref[...] += jnp.sum(df * (1.0 - sg), axis=0, keepdims=True)
            sq = _sigmoid(q)
            dq = dqf * sq * (1.0 + q * (1.0 - sq))
            dz_ref[rows, 0:HF] = dq.astype(BF16)
            dz_ref[rows, HF:2 * HF] = dfz.astype(BF16)
            dz_ref[rows, 2 * HF:3 * HF] = dv.astype(BF16)
            dz_ref[rows, 3 * HF:4 * HF] = dhg.astype(BF16)

    return pl.pallas_call(
        body, name=name, grid=(B, nblk),
        in_specs=[pl.BlockSpec((None, ts, 4 * HF), rev),
                  pl.BlockSpec((1, HF), lambda b, s: (0, 0)),
                  pl.BlockSpec((1, HGRN_DK), lambda b, s: (0, 0)),
                  pl.BlockSpec((None, cpb, HGRN_HEADS, HGRN_DK, HGRN_DK), lambda b, s: (b, nblk - 1 - s, 0, 0, 0)),
                  pl.BlockSpec((None, ts, HF), rev)],
        out_specs=[pl.BlockSpec((None, ts, 4 * HF), rev),
                   pl.BlockSpec((1, HF), lambda b, s: (0, 0)),
                   pl.BlockSpec((1, HGRN_DK), lambda b, s: (0, 0))],
        out_shape=[jax.ShapeDtypeStruct((B, S, 4 * HF), BF16),
                   jax.ShapeDtypeStruct((1, HF), F32),
                   jax.ShapeDtypeStruct((1, HGRN_DK), F32)],
        scratch_shapes=[pltpu.VMEM((HGRN_HEADS, HGRN_DK, HGRN_DK), F32)],
        compiler_params=_params("arbitrary", "arbitrary"),
    )(zh, lb, gn, states, doa)


KV_W = ATT_KV_HEADS * ATT_HD
ATT_SCALE = ATT_HD ** -0.5


def _rope(x, cos, sin, inverse=False):
    half = ROPE_DIM // 2
    outs = []
    for p in range(x.shape[1] // 128):
        xp = x[:, p * 128:(p + 1) * 128]
        lane = lax.broadcasted_iota(jnp.int32, xp.shape, 1) % ATT_HD
        sw = jnp.where(lane < half, pltpu.roll(xp, 128 - half, 1), pltpu.roll(xp, half, 1))
        outs.append(xp * cos - sw * sin if inverse else xp * cos + sw * sin)
    return outs[0] if len(outs) == 1 else jnp.concatenate(outs, axis=1)


PAIRS_PER_KV = ATT_GROUP // 2


def _swap_halves(x):
    return pltpu.roll(x, ATT_HD, 1)


def _kv_padded(t, low):
    sw = _swap_halves(t)
    zero = jnp.zeros_like(t)
    out = []
    for g in range(ATT_KV_HEADS):
        in_low, in_high = (t, sw) if g == 0 else (sw, t)
        out.append((jnp.where(low, in_low, zero).astype(BF16), jnp.where(low, zero, in_high).astype(BF16)))
    return out


def _swa_mask(first_block):
    qi = lax.broadcasted_iota(jnp.int32, (WINDOW, 2 * WINDOW), 0)
    mi = lax.broadcasted_iota(jnp.int32, (WINDOW, 2 * WINDOW), 1)
    band = (mi > qi) & (mi <= qi + WINDOW)
    return band & (jnp.logical_not(first_block) | (mi >= WINDOW))


def _swa_specs(nb):
    cur = lambda b, i: (b, i, 0)
    prev = lambda b, i: (b, jnp.maximum(i - 1, 0), 0)
    return cur, prev


def _swa_fwd(aq, akv, cos, sin, sinks, *, name):
    B, S, _ = aq.shape
    nb = S // WINDOW
    cur, prev = _swa_specs(nb)

    def body(q_ref, kvp_ref, kvc_ref, cp_ref, sp_ref, cc_ref, sc_ref, sink_ref, o_ref, lse_ref):
        cos_c, sin_c = cc_ref[...], sc_ref[...]
        q = (_rope(q_ref[...].astype(F32), cos_c, sin_c) * ATT_SCALE).astype(BF16)
        k = jnp.concatenate([_rope(kvp_ref[:, :KV_W].astype(F32), cp_ref[...], sp_ref[...]),
                             _rope(kvc_ref[:, :KV_W].astype(F32), cos_c, sin_c)], axis=0)
        v = jnp.concatenate([kvp_ref[:, KV_W:], kvc_ref[:, KV_W:]], axis=0).astype(F32)
        low = lax.broadcasted_iota(jnp.int32, k.shape, 1) < ATT_HD
        kpad = _kv_padded(k, low)
        vpad = _kv_padded(v, low)
        mask = _swa_mask(pl.program_id(1) == 0)
        lses = []
        for g in range(ATT_KV_HEADS):
            pairs = range(g * PAIRS_PER_KV, (g + 1) * PAIRS_PER_KV)
            keys = [(p, e) for p in pairs for e in (0, 1)]
            qp = {p: q[:, p * 128:(p + 1) * 128] for p in pairs}
            s = {pe: jnp.where(mask, _nt(qp[pe[0]], kpad[g][pe[1]]), NEG_INF) for pe in keys}
            pr = {}
            for pe in keys:
                sink = sink_ref[0, 2 * pe[0] + pe[1]]
                m = jnp.maximum(jnp.max(s[pe], axis=1, keepdims=True), sink)
                ex = jnp.exp(s[pe] - m)
                den = jnp.sum(ex, axis=1, keepdims=True) + jnp.exp(sink - m)
                pr[pe] = (ex * (1.0 / den)).astype(BF16)
                lses.append(m + jnp.log(den))
            for p in pairs:
                o_ref[:, p * 128:(p + 1) * 128] = (_nn(pr[p, 0], vpad[g][0]) + _nn(pr[p, 1], vpad[g][1])).astype(BF16)
        lse_ref[...] = jnp.concatenate(lses, axis=1)

    tab = lambda im: pl.BlockSpec((None, WINDOW, 128), im)
    return pl.pallas_call(
        body, name=name, grid=(B, nb),
        in_specs=[pl.BlockSpec((None, WINDOW, D_MODEL), cur),
                  pl.BlockSpec((None, WINDOW, 2 * KV_W), prev), pl.BlockSpec((None, WINDOW, 2 * KV_W), cur),
                  tab(prev), tab(prev), tab(cur), tab(cur),
                  pl.BlockSpec(memory_space=pltpu.SMEM)],
        out_specs=[pl.BlockSpec((None, WINDOW, D_MODEL), cur), pl.BlockSpec((None, WINDOW, ATT_HEADS), cur)],
        out_shape=[jax.ShapeDtypeStruct((B, S, D_MODEL), BF16), jax.ShapeDtypeStruct((B, S, ATT_HEADS), F32)],
        compiler_params=_params("parallel", "parallel"),
    )(aq, akv, akv, cos, sin, cos, sin, sinks)


def _swa_bwd(aq, akv, cos, sin, sinks, lse, dob, *, name):
    B, S, _ = aq.shape
    nb = S // WINDOW
    cur, prev = _swa_specs(nb)

    def body(q_ref, kvp_ref, kvc_ref, cp_ref, sp_ref, cc_ref, sc_ref, sink_ref, lse_ref, do_ref,
             dq_ref, dkc_ref, dkp_ref, dsink_ref):
        @pl.when((pl.program_id(0) == 0) & (pl.program_id(1) == 0))
        def _():
            dsink_ref[...] = jnp.zeros_like(dsink_ref)

        cos_c, sin_c, cos_p, sin_p = cc_ref[...], sc_ref[...], cp_ref[...], sp_ref[...]
        q = (_rope(q_ref[...].astype(F32), cos_c, sin_c) * ATT_SCALE).astype(BF16)
        k = jnp.concatenate([_rope(kvp_ref[:, :KV_W].astype(F32), cos_p, sin_p),
                             _rope(kvc_ref[:, :KV_W].astype(F32), cos_c, sin_c)], axis=0)
        v = jnp.concatenate([kvp_ref[:, KV_W:], kvc_ref[:, KV_W:]], axis=0).astype(F32)
        low = lax.broadcasted_iota(jnp.int32, k.shape, 1) < ATT_HD
        kpad = _kv_padded(k, low)
        vpad = _kv_padded(v, low)
        mask = _swa_mask(pl.program_id(1) == 0)
        lse = lse_ref[...]
        dq_parts, dk_sum, dv_sum, dsinks = [], [], [], []
        for g in range(ATT_KV_HEADS):
            pairs = range(g * PAIRS_PER_KV, (g + 1) * PAIRS_PER_KV)
            keys = [(p, e) for p in pairs for e in (0, 1)]
            qp = {p: q[:, p * 128:(p + 1) * 128] for p in pairs}
            dop = {p: do_ref[:, p * 128:(p + 1) * 128] for p in pairs}
            s = {pe: jnp.where(mask, _nt(qp[pe[0]], kpad[g][pe[1]]), NEG_INF) for pe in keys}
            dp = {pe: _nt(dop[pe[0]], vpad[g][pe[1]]) for pe in keys}
            pr, ds = {}, {}
            for pe in keys:
                h = 2 * pe[0] + pe[1]
                lse_h = lse[:, h:h + 1]
                pf = jnp.exp(s[pe] - lse_h)
                delta = jnp.sum(pf * dp[pe], axis=1, keepdims=True)
                ds[pe] = (pf * (dp[pe] - delta)).astype(BF16)
                pr[pe] = pf.astype(BF16)
                p_sink = jnp.exp(sink_ref[0, h] - lse_h)
                dsinks.append(-jnp.sum(p_sink * delta, axis=0, keepdims=True))
            for p in pairs:
                dq_parts.append((_nn(ds[p, 0], kpad[g][0]) + _nn(ds[p, 1], kpad[g][1])) * ATT_SCALE)
            x = [sum(_tn(ds[p, e], qp[p]) for p in pairs) for e in (0, 1)]
            y = [sum(_tn(pr[p, e], dop[p]) for p in pairs) for e in (0, 1)]
            zk = jnp.where(low, x[0], x[1])
            zv = jnp.where(low, y[0], y[1])
            dk_sum.append(zk + _swap_halves(zk))
            dv_sum.append(zv + _swap_halves(zv))
        dq_ref[...] = _rope(jnp.concatenate(dq_parts, axis=1), cos_c, sin_c, inverse=True).astype(BF16)
        dk = jnp.where(low, dk_sum[0], dk_sum[1])
        dv = jnp.where(low, dv_sum[0], dv_sum[1])
        dkp_ref[:, :KV_W] = _rope(dk[:WINDOW], cos_p, sin_p, inverse=True)
        dkp_ref[:, KV_W:] = dv[:WINDOW]
        dkc_ref[:, :KV_W] = _rope(dk[WINDOW:], cos_c, sin_c, inverse=True)
        dkc_ref[:, KV_W:] = dv[WINDOW:]
        dsink_ref[...] += jnp.broadcast_to(jnp.concatenate(dsinks, axis=0), (ATT_HEADS, 128))

    tab = lambda im: pl.BlockSpec((None, WINDOW, 128), im)
    return pl.pallas_call(
        body, name=name, grid=(B, nb),
        in_specs=[pl.BlockSpec((None, WINDOW, D_MODEL), cur),
                  pl.BlockSpec((None, WINDOW, 2 * KV_W), prev), pl.BlockSpec((None, WINDOW, 2 * KV_W), cur),
                  tab(prev), tab(prev), tab(cur), tab(cur),
                  pl.BlockSpec(memory_space=pltpu.SMEM),
                  pl.BlockSpec((None, WINDOW, ATT_HEADS), cur),
                  pl.BlockSpec((None, WINDOW, D_MODEL), cur)],
        out_specs=[pl.BlockSpec((None, WINDOW, D_MODEL), cur),
                   pl.BlockSpec((None, WINDOW, 2 * KV_W), cur), pl.BlockSpec((None, WINDOW, 2 * KV_W), cur),
                   pl.BlockSpec((ATT_HEADS, 128), lambda b, i: (0, 0))],
        out_shape=[jax.ShapeDtypeStruct((B, S, D_MODEL), BF16),
                   jax.ShapeDtypeStruct((B, S, 2 * KV_W), F32), jax.ShapeDtypeStruct((B, S, 2 * KV_W), F32),
                   jax.ShapeDtypeStruct((ATT_HEADS, 128), F32)],
        compiler_params=_params("arbitrary", "arbitrary"),
    )(aq, akv, akv, cos, sin, cos, sin, sinks, lse, dob)


def _swa_dkv_combine(dkv_cur, dkv_prev, *, name):
    B, S, W = dkv_cur.shape

    def body(c_ref, p_ref, o_ref):
        rows = lax.broadcasted_iota(jnp.int32, (S, W), 0)
        o_ref[...] = (c_ref[...] + _shift_up(p_ref[...], WINDOW, rows, S)).astype(BF16)

    seq = pl.BlockSpec((None, S, W), lambda b: (b, 0, 0))
    return pl.pallas_call(
        body, name=name, grid=(B,),
        in_specs=[seq, seq], out_specs=seq,
        out_shape=jax.ShapeDtypeStruct((B, S, W), BF16),
        compiler_params=_params("parallel"),
    )(dkv_cur, dkv_prev)


def _rope_tables(positions):
    half = ROPE_DIM // 2
    inv = ROPE_THETA ** (-2.0 * jnp.arange(half, dtype=F32) / ROPE_DIM)
    ang = positions.astype(F32)[..., None] * inv
    c, s = jnp.cos(ang), jnp.sin(ang)
    pad = jnp.zeros(ang.shape[:-1] + (ATT_HD - ROPE_DIM,), F32)
    cos = jnp.concatenate([c, c, pad + 1.0], axis=-1)
    sin = jnp.concatenate([-s, s, pad], axis=-1)
    return jnp.tile(cos, (1, 1, 2)), jnp.tile(sin, (1, 1, 2))


def _lower_bound(lb_logits, *, name):
    def body(l_ref, o_ref):
        l = l_ref[...]
        e = jnp.exp(l - jnp.max(l, axis=0, keepdims=True))
        o_ref[...] = e[0:1] / jnp.sum(e, axis=0, keepdims=True)

    return pl.pallas_call(body, name=name, out_shape=jax.ShapeDtypeStruct((1, lb_logits.shape[1]), F32))(lb_logits)


W_ZH, W_GATES, W_AQ, W_AKV = 4 * HF, 2 * D_MODEL, ATT_HEADS * ATT_HD, 2 * KV_W
O_ZH, O_GATES, O_AQ, O_AKV = 0, W_ZH, W_ZH + W_GATES, W_ZH + W_GATES + W_AQ


def _reorder_w_in(w_in_full):
    return jnp.concatenate([w_in_full[:, :W_ZH], w_in_full[:, W_ZH + W_AQ + W_AKV:], w_in_full[:, W_ZH:W_ZH + W_AQ + W_AKV]], axis=1)


def _local_step(x, positions, target, small, w_in, rest_weights, emit, start_token):
    B, S, D = x.shape
    T = B * S
    x2 = x.reshape(T, D)
    cos, sin = _rope_tables(positions)
    lb = _lower_bound(small["lb_logits"], name="lb_fwd")
    zero = lambda tok: tok[0:1, 0:1]

    u1 = _norm_cast(x2, small["norm1_g"] + zero(start_token), name="norm1")
    zh =_matmul(u1, w_in, out_dtype=BF16, name="mm_zh", tm=2048, tn=512, n_extent=W_ZH, b_noff=O_ZH // 512)
    gates = _matmul(u1, w_in, out_dtype=BF16, name="mm_gates", tm=2048, tn=512, n_extent=W_GATES, b_noff=O_GATES // 512)
    aq = _matmul(u1, w_in, out_dtype=BF16, name="mm_aq", tm=2048, tn=512, n_extent=W_AQ, b_noff=O_AQ // 512)
    akv = _matmul(u1, w_in, out_dtype=BF16, name="mm_akv", tm=2048, tn=256, n_extent=W_AKV, b_noff=O_AKV // 256)
    zh3 = zh.reshape(B, S, 4 * HF)
    aq3 = aq.reshape(B, S, D)
    akv3 = akv.reshape(B, S, 2 * KV_W)
    oa, states = _hgrn_fwd(zh3, lb, small["hgrn_norm_g"], name="hgrn_fwd")
    ob, lse = _swa_fwd(aq3, akv3, cos, sin, small["attn_sinks"], name="swa_fwd")
    oa2 = oa.reshape(T, D)
    ob2 = ob.reshape(T, D)
    W = rest_weights("mix", ob)
    pa = _matmul(oa2, W["w_a"], out_dtype=BF16, name="mm_pa", tm=2048, tn=512)
    pb = _matmul(ob2, W["w_b"], out_dtype=BF16, name="mm_pb", tm=2048, tn=512)
    merged = _merge_fwd(gates, pa, pb, name="merge_fwd")
    h = _matmul(merged, W["w_out"], addend=x2, name="mm_h", tm=1024, tn=512)
    u2 = _norm_cast(h, small["norm2_g"], name="norm2")
    W.update(rest_weights("ffn", u2))
    gu = _matmul(u2, W["w_ffn"], out_dtype=BF16, name="mm_gu", tm=2048, tn=512)
    gu3 = gu.reshape(B, S, 2 * D_FF)
    act = _conv_act_fwd(gu3, W["conv_w"], small["conv_b"], name="conv_act_fwd")
    act2 = act.reshape(T, D_FF)
    h2 = _matmul(act2, W["w_down"], addend=h, name="mm_h2", tm=1024, tn=512)

    g = {}
    dh2, dh2b, g["final_g"], loss = _final_loss_bwd(h2, small["final_g"].reshape(1, D), target.reshape(T, D), name="final_loss_bwd")
    dact = _matmul(dh2b, W["w_down"], tb=True, out_dtype=BF16, name="mm_dact", tm=1024, tn=D_FF)
    dw_down = _matmul(act2, dh2b, ta=True, out_dtype=BF16, name="mm_dw_down", tm=D_FF, tn=1024, tk=1024)
    dg_, dup, g["conv_w"], g["conv_b"] = _conv_act_bwd(gu3, W["conv_w"], small["conv_b"], dact.reshape(B, S, D_FF), name="conv_act_bwd")
    dg2 = dg_.reshape(T, D_FF)
    dup2 = dup.reshape(T, D_FF)
    du2 = _matmul(dg2, W["w_ffn"], tb=True, name="mm_du2_g", tm=1024, tn=512, b_koff=0)
    du2 = _matmul(dup2, W["w_ffn"], tb=True, addend=du2, name="mm_du2_u", tm=1024, tn=512, b_koff=1)
    dw_ffn_g = _matmul(u2, dg2, ta=True, out_dtype=BF16, name="mm_dw_ffn_g", tm=1024, tn=D_FF, tk=1024)
    dw_ffn_u = _matmul(u2, dup2, ta=True, out_dtype=BF16, name="mm_dw_ffn_u", tm=1024, tn=D_FF, tk=1024)
    tok = emit("ffn", dict(w_ffn_g=dw_ffn_g, w_ffn_u=dw_ffn_u, w_down=dw_down))
    dh, dhb, g["norm2_g"] = _norm_bwd_add(h, small["norm2_g"] + zero(tok), du2, dh2, name="norm2_bwd")
    dmerged = _matmul(dhb, W["w_out"], tb=True, out_dtype=BF16, name="mm_dmerged", tm=2048, tn=512)
    dw_out = _matmul(merged, dhb, ta=True, out_dtype=BF16, name="mm_dw_out", tm=1024, tn=1024, tk=2048)
    dgates, dpa, dpb = _merge_bwd(gates, pa, pb, dmerged, name="merge_bwd")
    doa = _matmul(dpa, W["w_a"], tb=True, out_dtype=BF16, name="mm_doa", tm=2048, tn=512)
    dw_a = _matmul(oa2, dpa, ta=True, out_dtype=BF16, name="mm_dw_a", tm=1024, tn=1024, tk=2048)
    dob = _matmul(dpb, W["w_b"], tb=True, out_dtype=BF16, name="mm_dob", tm=2048, tn=512)
    dw_b = _matmul(ob2, dpb, ta=True, out_dtype=BF16, name="mm_dw_b", tm=1024, tn=1024, tk=2048)
    tok = emit("mix", dict(w_out=dw_out, w_a=dw_a, w_b=dw_b))
    daq, dkv_cur, dkv_prev, dsinks = _swa_bwd(aq3, akv3, cos, sin, small["attn_sinks"] + zero(tok), lse, dob.reshape(B, S, D), name="swa_bwd")
    dakv = _swa_dkv_combine(dkv_cur, dkv_prev, name="swa_dkv").reshape(T, 2 * KV_W)
    daq2 = daq.reshape(T, D)
    g["attn_sinks"] = dsinks
    dzh, g["lb"], g["hgrn_norm_g"] = _hgrn_bwd(zh3, lb, small["hgrn_norm_g"], states, doa.reshape(B, S, D), name="hgrn_bwd")
    dzh2 = dzh.reshape(T, 4 * HF)
    dw_h = _matmul(u1, dzh2, ta=True, out_dtype=BF16, name="mm_dw_h", tm=1024, tn=2048, tk=1024)
    dw_g = _matmul(u1, dgates, ta=True, out_dtype=BF16, name="mm_dw_g", tm=1024, tn=2048, tk=1024)
    dw_aq = _matmul(u1, daq2, ta=True, out_dtype=BF16, name="mm_dw_aq", tm=1024, tn=1024, tk=2048)
    dw_akv = _matmul(u1, dakv, ta=True, out_dtype=BF16, name="mm_dw_akv", tm=1024, tn=256, tk=2048)
    tok = emit("in", dict(w_h=dw_h, w_g=dw_g, w_aq=dw_aq, w_akv=dw_akv))
    du1 = _matmul(dzh2, w_in, tb=True, after=tok, name="mm_du1_h", tm=1024, tn=512, b_koff=O_ZH // W_ZH)
    du1 = _matmul(dgates, w_in, tb=True, addend=du1, name="mm_du1_g", tm=1024, tn=512, b_koff=O_GATES // W_GATES)
    du1 = _matmul(daq2, w_in, tb=True, addend=du1, name="mm_du1_aq", tm=2048, tn=512, b_koff=O_AQ // W_AQ)
    du1 = _matmul(dakv, w_in, tb=True, addend=du1, name="mm_du1_akv", tm=2048, tn=512, b_koff=O_AKV // W_AKV)
    dx, _, g["norm1_g"] = _norm_bwd_add(x2, small["norm1_g"], du1, dh, name="norm1_bwd")
    g["lb_logits"] = _lb_bwd(g.pop("lb"), lb, name="lb_bwd")
    return loss, dx.reshape(B, S, D), g


def _my_place():
    return lax.axis_index("x"), lax.axis_index("y"), lax.axis_index("c")


def _all_gather(blk, *, in_vmem, reduce_sum=False, name):
    m, n = blk.shape
    space = pltpu.VMEM if in_vmem else pl.ANY

    def body(x_ref, out_ref, *rest):
        if reduce_sum:
            tot_ref, send_sems, recv_sems, local_sem = rest
        else:
            send_sems, recv_sems, local_sem = rest
        x, y, c = _my_place()
        me, sibling = (x, y, c), (x, y, 1 - c)
        chips = [(1 - x, y), (x, 1 - y), (1 - x, 1 - y)]

        def slot(px, py, pc):
            return out_ref.at[4 * px + 2 * py + pc]

        def copy(k, block, to, src=None):
            return pltpu.make_async_remote_copy(
                src_ref=slot(*block) if src is None else src, dst_ref=slot(*block),
                send_sem=send_sems.at[k], recv_sem=recv_sems.at[k], device_id=to, device_id_type=MESH)

        mine = pltpu.make_async_copy(x_ref, slot(*me), local_sem)
        mine.start()
        first = [copy(0, me, sibling, src=x_ref)]
        first += [copy(1 + j, me, (*chip, c), src=x_ref) for j, chip in enumerate(chips)]
        for cp in first:
            cp.start()
        passed = [copy(4 + j, (*chip, c), sibling) for j, chip in enumerate(chips)]
        for j, chip in enumerate(chips):
            copy(1 + j, (*chip, c), me).wait_recv()
            passed[j].start()
        copy(0, sibling, me).wait_recv()
        for j, chip in enumerate(chips):
            copy(4 + j, (*chip, 1 - c), me).wait_recv()
        for cp in first + passed:
            cp.wait_send()
        mine.wait()
        if reduce_sum:
            acc = out_ref[0]
            for p in range(1, N_DEV):
                acc = acc + out_ref[p]
            tot_ref[...] = acc

    out_shape = [jax.ShapeDtypeStruct((N_DEV, m, n), blk.dtype)]
    out_specs = [pl.BlockSpec(memory_space=space)]
    if reduce_sum:
        out_shape.append(jax.ShapeDtypeStruct((m, n), blk.dtype))
        out_specs.append(pl.BlockSpec(memory_space=pltpu.VMEM))
    res = pl.pallas_call(
        body, name=name,
        out_shape=out_shape,
        in_specs=[pl.BlockSpec(memory_space=space)],
        out_specs=out_specs,
        scratch_shapes=[pltpu.SemaphoreType.DMA((7,)), pltpu.SemaphoreType.DMA((7,)), pltpu.SemaphoreType.DMA],
    )(blk)
    return res if reduce_sum else res[0]


HBM_SPEC = pl.BlockSpec(memory_space=pltpu.HBM)
SEM_SPEC = pl.BlockSpec(memory_space=pltpu.SEMAPHORE)
DATAFLOW_EFFECT = pltpu.SideEffectType.DATAFLOW_SIDE_EFFECTING
N_PEERS = N_DEV - 1


def _peers(x, y, c):
    return [(1 - x if r & 4 else x, 1 - y if r & 2 else y, 1 - c if r & 1 else c) for r in range(1, N_DEV)]


def _exchange_start(srcs, scatter, *, name):
    n = len(srcs)
    lands = [lax.empty(a.shape if scatter else (N_DEV,) + a.shape, a.dtype) for a in srcs]

    def body(*refs):
        src_refs, land_refs = refs[:n], refs[n:2 * n]
        send_sems, recv_sems, token = refs[2 * n], refs[2 * n + 1], refs[-1]
        x, y, c = _my_place()
        me = 4 * x + 2 * y + c
        for i in range(n):
            for r, (tx, ty, tc) in enumerate(_peers(x, y, c)):
                src = src_refs[i].at[4 * tx + 2 * ty + tc] if scatter else src_refs[i]
                pltpu.make_async_remote_copy(
                    src_ref=src, dst_ref=land_refs[i].at[me], send_sem=send_sems.at[N_PEERS * i + r],
                    recv_sem=recv_sems.at[N_PEERS * i + r], device_id=(tx, ty, tc), device_id_type=MESH).start()
        token[...] = jnp.zeros_like(token)

    thru = [pltpu.HBM(a.shape, a.dtype) for a in list(srcs) + lands]
    res = pl.pallas_call(
        body, name=name,
        out_shape=(pltpu.SemaphoreType.DMA((N_PEERS * n,)), pltpu.SemaphoreType.DMA((N_PEERS * n,)), *thru,
                   jax.ShapeDtypeStruct((8, 128), F32)),
        in_specs=[HBM_SPEC] * (2 * n),
        out_specs=(SEM_SPEC, SEM_SPEC, *([HBM_SPEC] * (2 * n)), pl.BlockSpec(memory_space=pltpu.VMEM)),
        input_output_aliases={i: 2 + i for i in range(2 * n)},
        compiler_params=pltpu.CompilerParams(has_side_effects=DATAFLOW_EFFECT),
    )(*[pltpu.with_memory_space_constraint(a, pltpu.HBM) for a in list(srcs) + lands])
    return (res[0], res[1], list(res[2:2 + n]), list(res[2 + n:2 + 2 * n]), scatter), res[-1]


def _exchange_wait(handle, after, *, name):
    send_sems, recv_sems, srcs, lands, scatter = handle
    n = len(srcs)

    def body(*refs):
        src_refs, land_refs = refs[:n], refs[n:2 * n]
        send_sems, recv_sems = refs[2 * n], refs[2 * n + 1]
        x, y, c = _my_place()
        for i in range(n):
            for r in range(N_PEERS):
                src = src_refs[i].at[0] if scatter else src_refs[i]
                cp = pltpu.make_async_remote_copy(
                    src_ref=src, dst_ref=land_refs[i].at[0], send_sem=send_sems.at[N_PEERS * i + r],
                    recv_sem=recv_sems.at[N_PEERS * i + r], device_id=(x, y, c), device_id_type=MESH)
                cp.wait_send()
                cp.wait_recv()

    thru = [pltpu.HBM(a.shape, a.dtype) for a in srcs + lands]
    res = pl.pallas_call(
        body, name=name, out_shape=tuple(thru),
        in_specs=[HBM_SPEC] * (2 * n) + [SEM_SPEC, SEM_SPEC, ANY], out_specs=tuple([HBM_SPEC] * (2 * n)),
        input_output_aliases={i: i for i in range(2 * n)},
        compiler_params=pltpu.CompilerParams(has_side_effects=DATAFLOW_EFFECT),
    )(*srcs, *lands, send_sems, recv_sems, after)
    return list(res[:n]), list(res[n:])


def _with_own(land, own, me):
    return lax.dynamic_update_index_in_dim(land, own, me, 0)


def _adamw_math(w, g, m, v):
    m = ADAM_B1 * m + (1.0 - ADAM_B1) * g
    v = ADAM_B2 * v + (1.0 - ADAM_B2) * (g * g)
    m_hat = m / (1.0 - ADAM_B1 ** ADAM_STEP)
    v_hat = v / (1.0 - ADAM_B2 ** ADAM_STEP)
    delta = -ADAM_LR * (m_hat / (jnp.sqrt(v_hat) + ADAM_EPS) + ADAM_WD * w)
    return delta, m, v


def _adamw_sum(parts, w, m, v, *, name):
    shape = w.shape
    R, n = shape[-2], shape[-1]
    w, m, v = (t.reshape(R, n) for t in (w, m, v))
    tr = _pick(R, (256, 176, 128))

    def body(p_ref, w_ref, m_ref, v_ref, g_ref, d_ref, mo_ref, vo_ref):
        g = p_ref[0].astype(F32)
        for p in range(1, N_DEV):
            g = g + p_ref[p].astype(F32)
        d, mn, vn = _adamw_math(w_ref[...], g, m_ref[...], v_ref[...])
        g_ref[...] = g
        d_ref[...] = d
        mo_ref[...] = mn
        vo_ref[...] = vn

    row = pl.BlockSpec((tr, n), lambda i: (i, 0))
    outs = pl.pallas_call(
        body, name=name, grid=(R // tr,),
        in_specs=[pl.BlockSpec((N_DEV, tr, n), lambda i: (0, i, 0)), row, row, row],
        out_specs=[row, row, row, row],
        out_shape=[jax.ShapeDtypeStruct((R, n), F32)] * 4,
        compiler_params=_params("parallel"),
    )(parts, w, m, v)
    return [t.reshape(shape) for t in outs]


def _adamw_small(g, w, m, v, *, name):
    def body(g_ref, w_ref, m_ref, v_ref, d_ref, mo_ref, vo_ref):
        d, mn, vn = _adamw_math(w_ref[...], g_ref[...], m_ref[...], v_ref[...])
        d_ref[...] = d
        mo_ref[...] = mn
        vo_ref[...] = vn

    return pl.pallas_call(body, name=name, out_shape=[jax.ShapeDtypeStruct(w.shape, F32)] * 3)(g, w, m, v)


def _lb_bwd(dlb, lb, *, name):
    def body(d_ref, lb_ref, o_ref):
        t = d_ref[...] * lb_ref[...] * (1.0 - lb_ref[...])
        o_ref[0:1, :] = t
        o_ref[1:2, :] = -t

    return pl.pallas_call(body, name=name, out_shape=jax.ShapeDtypeStruct((2, lb.shape[1]), F32))(dlb, lb)


LANES = 128
N_IN, N_FFN = 7424, 5632
IN_BLK, FFN_BLK, DOWN_BLK, ROW_BLK = N_IN // N_DEV, N_FFN // N_DEV, D_FF // N_DEV, D_MODEL // N_DEV
CONVW_BLK = D_FF // N_DEV
SMALL_NAMES = ("norm1_g", "lb_logits", "hgrn_norm_g", "attn_sinks", "norm2_g", "conv_b", "final_g")
CONV_BITS_SHAPE = (16, 256)


def _cols_from_blocks(blocks):
    n, rows, width = blocks.shape
    return blocks.transpose(1, 0, 2).reshape(rows, n * width)


def _blocks_from_cols(full):
    rows, cols = full.shape
    return full.reshape(rows, N_DEV, cols // N_DEV).transpose(1, 0, 2)


def _to_rows(vec, rows):
    vec = vec.reshape(-1)
    return jnp.pad(vec, (0, rows * LANES - vec.shape[0])).reshape(rows, LANES)


def kernel(x, positions, norm1_g, w_in, lb_logits, hgrn_norm_g, w_a, attn_sinks, w_b, w_out, norm2_g, w_ffn_in, conv_w, conv_b, w_down, final_g, loss_target, m_norm1_g, m_w_in, m_lb_logits, m_hgrn_norm_g, m_w_a, m_attn_sinks, m_w_b, m_w_out, m_norm2_g, m_w_ffn_in, m_conv_w, m_conv_b, m_w_down, m_final_g, v_norm1_g, v_w_in, v_lb_logits, v_hgrn_norm_g, v_w_a, v_attn_sinks, v_w_b, v_w_out, v_norm2_g, v_w_ffn_in, v_conv_w, v_conv_b, v_w_down, v_final_g):
    xi, yi, ci = _my_place()
    dev = 4 * xi + 2 * yi + ci

    w_in_blocks = _all_gather(w_in[0].astype(BF16), in_vmem=False, name="ag_w_in")
    conv_bits = lax.bitcast_convert_type(conv_w, BF16).reshape(-1)
    conv_bits = jnp.pad(conv_bits, (0, CONV_BITS_SHAPE[0] * CONV_BITS_SHAPE[1] - conv_bits.shape[0])).reshape(CONV_BITS_SHAPE)
    gather_handles = {}
    gather_handles["mix"], tok_mix = _exchange_start([w_a[0].astype(BF16), w_b[0].astype(BF16), w_out[0].astype(BF16)], False, name="ag_mix_start")
    gather_handles["ffn"], tok_ffn = _exchange_start([w_ffn_in[0].astype(BF16), w_down[0].astype(BF16), conv_bits], False, name="ag_ffn_start")
    start_token = tok_mix + tok_ffn

    def rest_weights(group, after):
        own, lands = _exchange_wait(gather_handles[group], after, name="ag_" + group + "_wait")
        full = [_with_own(l, o, dev) for l, o in zip(lands, own)]
        if group == "mix":
            return dict(zip(("w_a", "w_b", "w_out"), [t.reshape(D_MODEL, D_MODEL) for t in full]))
        bits = full[2].reshape(N_DEV, -1)[:, :3 * CONVW_BLK * 2].reshape(N_DEV, 3, CONVW_BLK, 2)
        return dict(w_ffn=_cols_from_blocks(full[0]), w_down=full[1].reshape(D_FF, D_MODEL),
                    conv_w=_cols_from_blocks(lax.bitcast_convert_type(bits, F32)))

    handles = {}

    def emit(group, gr):
        if group == "ffn":
            srcs = [_blocks_from_cols(jnp.concatenate([gr["w_ffn_g"], gr["w_ffn_u"]], axis=1)), gr["w_down"].reshape(N_DEV, DOWN_BLK, D_MODEL)]
        elif group == "mix":
            srcs = [gr[n].reshape(N_DEV, ROW_BLK, D_MODEL) for n in ("w_out", "w_a", "w_b")]
        else:
            srcs = [_blocks_from_cols(jnp.concatenate([gr["w_h"], gr["w_aq"], gr["w_akv"], gr["w_g"]], axis=1))]
        handles[group], token = _exchange_start(srcs, True, name="rs_" + group + "_start")
        return token

    small = dict(norm1_g=norm1_g, lb_logits=lb_logits, hgrn_norm_g=hgrn_norm_g, attn_sinks=attn_sinks, norm2_g=norm2_g,
                 conv_b=conv_b, final_g=final_g)
    w_in_full = _reorder_w_in(_cols_from_blocks(w_in_blocks))
    loss, grad_x, g = _local_step(x, positions, loss_target, small, w_in_full, rest_weights, emit, start_token)

    def parts_of(group, after):
        srcs, lands = _exchange_wait(handles[group], after, name="rs_" + group + "_wait")
        return [_with_own(l, lax.dynamic_index_in_dim(s, dev, 0, keepdims=False), dev) for s, l in zip(srcs, lands)]

    p_ffn, p_down = parts_of("ffn", grad_x)
    p_out, p_a, p_b = parts_of("mix", grad_x)
    (p_in,) = parts_of("in", grad_x)
    big = dict(
        w_in=_adamw_sum(p_in, w_in, m_w_in, v_w_in, name="adamw_w_in"),
        w_a=_adamw_sum(p_a, w_a, m_w_a, v_w_a, name="adamw_w_a"),
        w_b=_adamw_sum(p_b, w_b, m_w_b, v_w_b, name="adamw_w_b"),
        w_out=_adamw_sum(p_out, w_out, m_w_out, v_w_out, name="adamw_w_out"),
        w_ffn_in=_adamw_sum(p_ffn, w_ffn_in, m_w_ffn_in, v_w_ffn_in, name="adamw_w_ffn_in"),
        w_down=_adamw_sum(p_down, w_down, m_w_down, v_w_down, name="adamw_w_down"),
    )

    sm_g = dict(norm1_g=g["norm1_g"], lb_logits=g["lb_logits"], hgrn_norm_g=g["hgrn_norm_g"], attn_sinks=g["attn_sinks"][:, 0],
                norm2_g=g["norm2_g"], conv_b=g["conv_b"], final_g=g["final_g"])
    vec = jnp.concatenate([sm_g[n].reshape(-1) for n in SMALL_NAMES] + [g["conv_w"].reshape(-1), loss.reshape(-1)])
    sm_rows = 136
    _, total = _all_gather(_to_rows(vec, sm_rows), in_vmem=True, reduce_sum=True, name="ar_small")
    total = total.reshape(-1)
    sm_w = dict(norm1_g=norm1_g, lb_logits=lb_logits, hgrn_norm_g=hgrn_norm_g, attn_sinks=attn_sinks, norm2_g=norm2_g,
                conv_b=conv_b, final_g=final_g)
    sm_m = dict(norm1_g=m_norm1_g, lb_logits=m_lb_logits, hgrn_norm_g=m_hgrn_norm_g, attn_sinks=m_attn_sinks, norm2_g=m_norm2_g,
                conv_b=m_conv_b, final_g=m_final_g)
    sm_v = dict(norm1_g=v_norm1_g, lb_logits=v_lb_logits, hgrn_norm_g=v_hgrn_norm_g, attn_sinks=v_attn_sinks, norm2_g=v_norm2_g,
                conv_b=v_conv_b, final_g=v_final_g)
    sizes = [sm_w[n].size for n in SMALL_NAMES]
    n_rep = sum(sizes)
    g_conv_full = total[n_rep:n_rep + 3 * D_FF].reshape(3, D_FF)
    g_conv = lax.dynamic_slice_in_dim(g_conv_full, dev * CONVW_BLK, CONVW_BLK, axis=1)
    loss_total = total[n_rep + 3 * D_FF]
    ad_rows = 72
    pack_small = lambda d, cw: _to_rows(jnp.concatenate([d[n].reshape(-1) for n in SMALL_NAMES] + [cw.reshape(-1)]), ad_rows)
    g_small = _to_rows(jnp.concatenate([total[:n_rep], g_conv.reshape(-1)]), ad_rows)
    d_s, m_s, v_s = _adamw_small(g_small, pack_small(sm_w, conv_w), pack_small(sm_m, m_conv_w), pack_small(sm_v, v_conv_w), name="adamw_small")

    def unpack_small(t):
        t = t.reshape(-1)
        out, off = {}, 0
        for n, s in zip(SMALL_NAMES, sizes):
            out[n] = t[off:off + s].reshape(sm_w[n].shape)
            off += s
        out["conv_w"] = t[off:off + 3 * CONVW_BLK].reshape(1, 3, CONVW_BLK)
        return out

    names = ("norm1_g", "w_in", "lb_logits", "hgrn_norm_g", "w_a", "attn_sinks", "w_b", "w_out", "norm2_g", "w_ffn_in", "conv_w", "conv_b", "w_down", "final_g")
    outs = [loss_total.reshape(()), grad_x]
    for kind, s_vec in enumerate((g_small, d_s, m_s, v_s)):
        s_un = unpack_small(s_vec)
        outs += [big[n][kind] if n in big else s_un[n] for n in names]
    return tuple(outs)
```

```python
import functools

import jax
import jax.numpy as jnp
from jax import lax
from jax.experimental import pallas as pl
from jax.experimental.pallas import tpu as pltpu

F32 = jnp.float32
BF16 = jnp.bfloat16

D_MODEL = 1024
HGRN_HEADS = 8
HGRN_DK = 128
CHUNK = 64
ATT_HEADS = 16
ATT_KV_HEADS = 2
ATT_HD = 64
ATT_GROUP = ATT_HEADS // ATT_KV_HEADS
WINDOW = 128
ROPE_DIM = ATT_HD // 4
ROPE_THETA = 500000.0
D_FF = 2816
EPS = 1e-6
NEG_INF = -1e30
N_DEV = 8

ADAM_LR = 0.001
ADAM_B1 = 0.9
ADAM_B2 = 0.999
ADAM_EPS = 1e-08
ADAM_WD = 0.01
ADAM_STEP = 10

MESH = pl.DeviceIdType.MESH
ANY = pl.BlockSpec(memory_space=pl.ANY)


def _pick(n, cands):
    for c in cands:
        if n % c == 0:
            return c
    return n


def _sigmoid(x):
    return 1.0 / (1.0 + jnp.exp(-x))


def _silu(x):
    return x * _sigmoid(x)


def _rms(x, g):
    return x * lax.rsqrt(jnp.mean(x * x, axis=-1, keepdims=True) + EPS) * g


def _dot(a, b, dims):
    return lax.dot_general(a, b, (dims, ((), ())), preferred_element_type=F32)


def _nn(a, b):
    return _dot(a, b, ((1,), (0,)))


def _nt(a, b):
    return _dot(a, b, ((1,), (1,)))


def _tn(a, b):
    return _dot(a, b, ((0,), (0,)))


def _params(*sem):
    return pltpu.CompilerParams(dimension_semantics=sem, vmem_limit_bytes=56 * 1024 * 1024)


def _matmul(a, b, *, ta=False, tb=False, out_dtype=F32, addend=None, after=None, name, tm, tn, tk=None, n_extent=None, b_koff=0, b_noff=0):
    M, K = (a.shape[1], a.shape[0]) if ta else a.shape
    N = n_extent or (b.shape[0] if tb else b.shape[1])
    tm, tn, tk = min(tm, M), min(tn, N), min(tk or K, K)
    assert M % tm == 0 and N % tn == 0 and K % tk == 0, (name, M, N, K, tm, tn, tk)
    nk = K // tk
    use_scratch = nk > 1 and out_dtype != F32
    grid = (M // tm, N // tn, nk)
    a_spec = pl.BlockSpec((tk, tm), lambda i, j, k: (k, i)) if ta else pl.BlockSpec((tm, tk), lambda i, j, k: (i, k))
    b_spec = pl.BlockSpec((tn, tk), lambda i, j, k: (j + b_noff, k + b_koff)) if tb else pl.BlockSpec((tk, tn), lambda i, j, k: (k + b_koff, j + b_noff))
    o_spec = pl.BlockSpec((tm, tn), lambda i, j, k: (i, j))
    dims = ((0 if ta else 1,), (1 if tb else 0,))
    has_add = addend is not None

    n_in = 2 + has_add + (after is not None)

    def body(*refs):
        a_ref, b_ref = refs[:2]
        c_ref = refs[2] if has_add else None
        o_ref = refs[n_in]
        part = _dot(a_ref[...], b_ref[...], dims)
        if nk == 1:
            if has_add:
                part = part + c_ref[...].astype(F32)
            o_ref[...] = part.astype(out_dtype)
        else:
            acc_ref = refs[-1] if use_scratch else o_ref
            k = pl.program_id(2)

            @pl.when(k == 0)
            def _():
                acc_ref[...] = part + c_ref[...].astype(F32) if has_add else part

            @pl.when(k > 0)
            def _():
                acc_ref[...] += part

            if use_scratch:
                @pl.when(k == nk - 1)
                def _():
                    o_ref[...] = acc_ref[...].astype(out_dtype)

    in_specs = [a_spec, b_spec] + ([o_spec] if has_add else [])
    args = (a, b) + ((addend,) if has_add else ())
    if after is not None:
        in_specs.append(pl.BlockSpec(after.shape, lambda i, j, k: (0, 0)))
        args += (after,)
    return pl.pallas_call(
        body,
        name=name,
        grid=grid,
        in_specs=in_specs,
        out_specs=o_spec,
        out_shape=jax.ShapeDtypeStruct((M, N), out_dtype),
        scratch_shapes=[pltpu.VMEM((tm, tn), F32)] if use_scratch else [],
        compiler_params=_params("parallel", "parallel", "arbitrary"),
    )(*args)


def _row_spec(tm, n):
    return pl.BlockSpec((tm, n), lambda i: (i, 0))


def _full_spec(shape):
    return pl.BlockSpec(shape, lambda i: tuple(0 for _ in shape))


def _norm_cast(x, g, *, name):
    T, D = x.shape
    tm = _pick(T, (512, 256, 128))

    def body(x_ref, g_ref, u_ref):
        u_ref[...] = _rms(x_ref[...], g_ref[...]).astype(BF16)

    return pl.pallas_call(
        body, name=name, grid=(T // tm,),
        in_specs=[_row_spec(tm, D), _full_spec((1, D))],
        out_specs=_row_spec(tm, D),
        out_shape=jax.ShapeDtypeStruct((T, D), BF16),
        compiler_params=_params("parallel"),
    )(x, g)


def _norm_bwd_add(x, g, du, dres, *, name):
    T, D = x.shape
    tm = _pick(T, (512, 256, 128))
    has_res = dres is not None

    def body(*refs):
        if has_res:
            x_ref, g_ref, du_ref, dr_ref, dx_ref, dxb_ref, dg_ref = refs
        else:
            x_ref, g_ref, du_ref, dx_ref, dxb_ref, dg_ref = refs
        _, vjp = jax.vjp(_rms, x_ref[...], g_ref[...])
        dx, dg = vjp(du_ref[...].astype(F32))
        if has_res:
            dx = dx + dr_ref[...]
        dx_ref[...] = dx
        dxb_ref[...] = dx.astype(BF16)

        @pl.when(pl.program_id(0) == 0)
        def _():
            dg_ref[...] = jnp.zeros_like(dg_ref)

        dg_ref[...] += dg

    ins = [x, g, du] + ([dres] if has_res else [])
    in_specs = [_row_spec(tm, D), _full_spec((1, D)), _row_spec(tm, D)] + ([_row_spec(tm, D)] if has_res else [])
    return pl.pallas_call(
        body, name=name, grid=(T // tm,),
        in_specs=in_specs,
        out_specs=[_row_spec(tm, D), _row_spec(tm, D), _full_spec((1, D))],
        out_shape=[jax.ShapeDtypeStruct((T, D), F32), jax.ShapeDtypeStruct((T, D), BF16), jax.ShapeDtypeStruct((1, D), F32)],
        compiler_params=_params("arbitrary"),
    )(*ins)


def _final_loss_bwd(h2, g, target, *, name):
    T, D = h2.shape
    tm = _pick(T, (512, 256, 128))

    def body(h_ref, g_ref, t_ref, dx_ref, dxb_ref, dg_ref, loss_ref):
        y, vjp = jax.vjp(_rms, h_ref[...], g_ref[...])
        err = y - t_ref[...]
        dx, dg = vjp(err * (1.0 / D))
        dx_ref[...] = dx
        dxb_ref[...] = dx.astype(BF16)

        @pl.when(pl.program_id(0) == 0)
        def _():
            dg_ref[...] = jnp.zeros_like(dg_ref)
            loss_ref[...] = jnp.zeros_like(loss_ref)

        dg_ref[...] += dg
        loss_ref[...] += (0.5 / D) * jnp.sum(jnp.sum(err * err, axis=1, keepdims=True), axis=0, keepdims=True)

    return pl.pallas_call(
        body, name=name, grid=(T // tm,),
        in_specs=[_row_spec(tm, D), _full_spec((1, D)), _row_spec(tm, D)],
        out_specs=[_row_spec(tm, D), _row_spec(tm, D), _full_spec((1, D)), _full_spec((1, 1))],
        out_shape=[jax.ShapeDtypeStruct((T, D), F32), jax.ShapeDtypeStruct((T, D), BF16), jax.ShapeDtypeStruct((1, D), F32), jax.ShapeDtypeStruct((1, 1), F32)],
        compiler_params=_params("arbitrary"),
    )(h2, g, target)


def _merge_fn(gates, a, b):
    ga = gates[:, :D_MODEL].astype(F32)
    gb = gates[:, D_MODEL:].astype(F32)
    return _sigmoid(ga) * a.astype(F32) + _sigmoid(gb) * b.astype(F32)


def _gates_spec(tm):
    return pl.BlockSpec((tm, W_GATES), lambda i: (i, O_GATES // W_GATES))


def _merge_fwd(z, a, b, *, name):
    T = a.shape[0]
    tm = _pick(T, (512, 256, 128))

    def body(g_ref, a_ref, b_ref, o_ref):
        o_ref[...] = _merge_fn(g_ref[...], a_ref[...], b_ref[...]).astype(BF16)

    return pl.pallas_call(
        body, name=name, grid=(T // tm,),
        in_specs=[_gates_spec(tm), _row_spec(tm, D_MODEL), _row_spec(tm, D_MODEL)],
        out_specs=_row_spec(tm, D_MODEL),
        out_shape=jax.ShapeDtypeStruct((T, D_MODEL), BF16),
        compiler_params=_params("parallel"),
    )(z, a, b)


def _merge_bwd(z, a, b, dmerged, dz, *, name):
    T = a.shape[0]
    tm = _pick(T, (512, 256, 128))

    def body(g_ref, a_ref, b_ref, dm_ref, dz_in, dg_ref, da_ref, db_ref):
        g = g_ref[...].astype(F32)
        dm = dm_ref[...].astype(F32)
        sa = _sigmoid(g[:, :D_MODEL])
        sb = _sigmoid(g[:, D_MODEL:])
        da_ref[...] = (dm * sa).astype(BF16)
        db_ref[...] = (dm * sb).astype(BF16)
        dg_ref[:, :D_MODEL] = (dm * a_ref[...].astype(F32) * sa * (1.0 - sa)).astype(BF16)
        dg_ref[:, D_MODEL:] = (dm * b_ref[...].astype(F32) * sb * (1.0 - sb)).astype(BF16)

    return pl.pallas_call(
        body, name=name, grid=(T // tm,),
        in_specs=[_gates_spec(tm), _row_spec(tm, D_MODEL), _row_spec(tm, D_MODEL), _row_spec(tm, D_MODEL), ANY],
        out_specs=[_gates_spec(tm), _row_spec(tm, D_MODEL), _row_spec(tm, D_MODEL)],
        out_shape=[jax.ShapeDtypeStruct(dz.shape, BF16), jax.ShapeDtypeStruct((T, D_MODEL), BF16), jax.ShapeDtypeStruct((T, D_MODEL), BF16)],
        input_output_aliases={4: 0},
        compiler_params=_params("parallel"),
    )(z, a, b, dmerged, dz)


CONV_TC = 256


def _shift_down(x, n, rows):
    return jnp.where(rows >= n, pltpu.roll(x, n, 0), 0.0)


def _shift_up(x, n, rows, S):
    return jnp.where(rows < S - n, pltpu.roll(x, S - n, 0), 0.0)


def _conv_act_fwd(gu, conv_w, conv_b, *, name):
    B, S, _ = gu.shape
    tc = CONV_TC
    nc = D_FF // tc

    def body(g_ref, up_ref, w_ref, b_ref, o_ref):
        g = g_ref[...].astype(F32)
        rows = lax.broadcasted_iota(jnp.int32, g.shape, 0)
        w = w_ref[...]
        a = w[2:3] * g + w[1:2] * _shift_down(g, 1, rows) + w[0:1] * _shift_down(g, 2, rows) + b_ref[...]
        o_ref[...] = (_silu(a) * up_ref[...].astype(F32)).astype(BF16)

    return pl.pallas_call(
        body, name=name, grid=(B, nc),
        in_specs=[pl.BlockSpec((None, S, tc), lambda b, j: (b, 0, j)),
                  pl.BlockSpec((None, S, tc), lambda b, j: (b, 0, j + nc)),
                  pl.BlockSpec((3, tc), lambda b, j: (0, j)),
                  pl.BlockSpec((1, tc), lambda b, j: (0, j))],
        out_specs=pl.BlockSpec((None, S, tc), lambda b, j: (b, 0, j)),
        out_shape=jax.ShapeDtypeStruct((B, S, D_FF), BF16),
        compiler_params=_params("parallel", "parallel"),
    )(gu, gu, conv_w, conv_b)


def _conv_act_bwd(gu, conv_w, conv_b, dact, *, name):
    B, S, _ = gu.shape
    tc = CONV_TC
    nc = D_FF // tc

    def body(g_ref, up_ref, w_ref, b_ref, da_ref, dg_ref, dup_ref, dw_ref, db_ref):
        g = g_ref[...].astype(F32)
        up = up_ref[...].astype(F32)
        dact = da_ref[...].astype(F32)
        rows = lax.broadcasted_iota(jnp.int32, g.shape, 0)
        w = w_ref[...]
        g1 = _shift_down(g, 1, rows)
        g2 = _shift_down(g, 2, rows)
        a = w[2:3] * g + w[1:2] * g1 + w[0:1] * g2 + b_ref[...]
        sg = _sigmoid(a)
        dup_ref[...] = (dact * a * sg).astype(BF16)
        da = dact * up * sg * (1.0 + a * (1.0 - sg))
        dg = w[2:3] * da + w[1:2] * _shift_up(da, 1, rows, S) + w[0:1] * _shift_up(da, 2, rows, S)
        dg_ref[...] = dg.astype(BF16)

        @pl.when(pl.program_id(1) == 0)
        def _():
            dw_ref[...] = jnp.zeros_like(dw_ref)
            db_ref[...] = jnp.zeros_like(db_ref)

        dw_ref[0:1, :] += jnp.sum(da * g2, axis=0, keepdims=True)
        dw_ref[1:2, :] += jnp.sum(da * g1, axis=0, keepdims=True)
        dw_ref[2:3, :] += jnp.sum(da * g, axis=0, keepdims=True)
        db_ref[...] += jnp.sum(da, axis=0, keepdims=True)

    col = lambda j, b: (b, 0, j)
    return pl.pallas_call(
        body, name=name, grid=(nc, B),
        in_specs=[pl.BlockSpec((None, S, tc), col),
                  pl.BlockSpec((None, S, tc), lambda j, b: (b, 0, j + nc)),
                  pl.BlockSpec((3, tc), lambda j, b: (0, j)),
                  pl.BlockSpec((1, tc), lambda j, b: (0, j)),
                  pl.BlockSpec((None, S, tc), col)],
        out_specs=[pl.BlockSpec((None, S, tc), col), pl.BlockSpec((None, S, tc), col),
                   pl.BlockSpec((3, tc), lambda j, b: (0, j)), pl.BlockSpec((1, tc), lambda j, b: (0, j))],
        out_shape=[jax.ShapeDtypeStruct((B, S, D_FF), BF16), jax.ShapeDtypeStruct((B, S, D_FF), BF16),
                   jax.ShapeDtypeStruct((3, D_FF), F32), jax.ShapeDtypeStruct((1, D_FF), F32)],
        compiler_params=_params("parallel", "arbitrary"),
    )(gu, gu, conv_w, conv_b, dact)


HGRN_CPB = 4
HF = HGRN_HEADS * HGRN_DK


def _tri(n, upper=False):
    r = lax.broadcasted_iota(jnp.int32, (n, n), 0)
    c = lax.broadcasted_iota(jnp.int32, (n, n), 1)
    return (c >= r) if upper else (r >= c)


def _hs(h):
    return slice(h * HGRN_DK, (h + 1) * HGRN_DK)


def _cumsum_rows(tri_b, x):
    hi = x.astype(BF16)
    lo = (x - hi.astype(F32)).astype(BF16)
    return _nn(tri_b, hi) + _nn(tri_b, lo)


def _hgrn_pre(q, fz, lb, tril_b):
    qf = _silu(q)
    sg = _sigmoid(fz)
    f = lb + (1.0 - lb) * sg
    k = 1.0 - f
    b = _cumsum_rows(tril_b, jnp.log2(f))
    bref = b[CHUNK // 2:CHUNK // 2 + 1, :]
    blast = b[CHUNK - 1:CHUNK, :]
    e1 = jnp.exp2(b - bref)
    e2 = jnp.exp2(bref - b)
    e3 = jnp.exp2(b)
    e4 = jnp.exp2(blast - b)
    dec = jnp.exp2(blast)
    return sg, f, (e1, e2, e3, e4), qf * e1, k * e2, qf * e3, k * e4, dec


def _hgrn_fwd(zh, lb, gn, *, name):
    B, S, _ = zh.shape
    cpb = HGRN_CPB
    ts = cpb * CHUNK
    nblk = S // ts

    def body(z_ref, lb_ref, gn_ref, o_ref, st_ref, state):
        @pl.when(pl.program_id(1) == 0)
        def _():
            state[...] = jnp.zeros_like(state)

        H = HGRN_HEADS
        causal = _tri(CHUNK)
        tril_b = causal.astype(BF16)
        lb = lb_ref[...]
        for c in range(cpb):
            rows = slice(c * CHUNK, (c + 1) * CHUNK)
            q = z_ref[rows, 0:HF].astype(F32)
            fz = z_ref[rows, HF:2 * HF].astype(F32)
            v = z_ref[rows, 2 * HF:3 * HF]
            hg = z_ref[rows, 3 * HF:4 * HF].astype(F32)
            _, _, _, q_in, k_in, q_out, k_st, dec = _hgrn_pre(q, fz, lb, tril_b)
            q_in, k_in, q_out, k_st = (t.astype(BF16) for t in (q_in, k_in, q_out, k_st))
            a = [jnp.where(causal, _nt(q_in[:, _hs(h)], k_in[:, _hs(h)]), 0.0).astype(BF16) for h in range(H)]
            st = [state[h] for h in range(H)]
            for h in range(H):
                st_ref[c, h] = st[h]
            o = [_nn(a[h], v[:, _hs(h)]) + _nt(q_out[:, _hs(h)], st[h].astype(BF16)) for h in range(H)]
            for h in range(H):
                state[h] = st[h] * dec[:, _hs(h)] + _tn(v[:, _hs(h)], k_st[:, _hs(h)])
            gate = _silu(hg)
            for h in range(H):
                o_ref[rows, _hs(h)] = (_rms(o[h], gn_ref[...]) * gate[:, _hs(h)]).astype(BF16)

    return pl.pallas_call(
        body, name=name, grid=(B, nblk),
        in_specs=[pl.BlockSpec((None, ts, 4 * HF), lambda b, s: (b, s, 0)),
                  pl.BlockSpec((1, HF), lambda b, s: (0, 0)),
                  pl.BlockSpec((1, HGRN_DK), lambda b, s: (0, 0))],
        out_specs=[pl.BlockSpec((None, ts, HF), lambda b, s: (b, s, 0)),
                   pl.BlockSpec((None, cpb, HGRN_HEADS, HGRN_DK, HGRN_DK), lambda b, s: (b, s, 0, 0, 0))],
        out_shape=[jax.ShapeDtypeStruct((B, S, HF), BF16),
                   jax.ShapeDtypeStruct((B, S // CHUNK, HGRN_HEADS, HGRN_DK, HGRN_DK), F32)],
        scratch_shapes=[pltpu.VMEM((HGRN_HEADS, HGRN_DK, HGRN_DK), F32)],
        compiler_params=_params("arbitrary", "arbitrary"),
    )(zh, lb, gn)


def _hgrn_bwd(zh, lb, gn, states, doa, dz, *, name):
    B, S, _ = zh.shape
    cpb = HGRN_CPB
    ts = cpb * CHUNK
    nblk = S // ts
    rev = lambda b, s: (b, nblk - 1 - s, 0)

    def body(z_ref, lb_ref, gn_ref, st_ref, do_ref, dz_in, dz_ref, dlb_ref, dgn_ref, dstate):
        @pl.when(pl.program_id(1) == 0)
        def _():
            dstate[...] = jnp.zeros_like(dstate)

        @pl.when((pl.program_id(0) == 0) & (pl.program_id(1) == 0))
        def _():
            dlb_ref[...] = jnp.zeros_like(dlb_ref)
            dgn_ref[...] = jnp.zeros_like(dgn_ref)

        H = HGRN_HEADS
        cat = lambda xs: jnp.concatenate(xs, axis=1)
        causal = _tri(CHUNK)
        tril_b = causal.astype(BF16)
        triu_b = _tri(CHUNK, upper=True).astype(BF16)
        rowid = lax.broadcasted_iota(jnp.int32, (CHUNK, HF), 0)
        lb = lb_ref[...]
        gn = gn_ref[...]
        for c in reversed(range(cpb)):
            rows = slice(c * CHUNK, (c + 1) * CHUNK)
            q = z_ref[rows, 0:HF].astype(F32)
            fz = z_ref[rows, HF:2 * HF].astype(F32)
            v = z_ref[rows, 2 * HF:3 * HF]
            hg = z_ref[rows, 3 * HF:4 * HF].astype(F32)
            sg, f, (e1, e2, e3, e4), q_in, k_in, q_out, k_st, dec = _hgrn_pre(q, fz, lb, tril_b)
            q_in_b, k_in_b, q_out_b, k_st_b = (t.astype(BF16) for t in (q_in, k_in, q_out, k_st))
            a_b = [jnp.where(causal, _nt(q_in_b[:, _hs(h)], k_in_b[:, _hs(h)]), 0.0).astype(BF16) for h in range(H)]
            st = [st_ref[c, h] for h in range(H)]
            st_b = [t.astype(BF16) for t in st]
            o = [_nn(a_b[h], v[:, _hs(h)]) + _nt(q_out_b[:, _hs(h)], st_b[h]) for h in range(H)]
            dout = do_ref[rows, :].astype(F32)
            shg = _sigmoid(hg)
            gate = hg * shg
            do_l, dgn_acc = [], jnp.zeros_like(gn)
            for h in range(H):
                _, norm_vjp = jax.vjp(_rms, o[h], gn)
                d_o, d_gn = norm_vjp(dout[:, _hs(h)] * gate[:, _hs(h)])
                do_l.append(d_o)
                dgn_acc = dgn_acc + d_gn
            dgn_ref[...] += dgn_acc
            on = cat([_rms(o[h], gn) for h in range(H)])
            dhg = dout * on * shg * (1.0 + hg * (1.0 - shg))
            do_b = [t.astype(BF16) for t in do_l]
            dst = [dstate[h] for h in range(H)]
            dst_b = [t.astype(BF16) for t in dst]
            da_b = [jnp.where(causal, _nt(do_b[h], v[:, _hs(h)]), 0.0).astype(BF16) for h in range(H)]
            dv = cat([_tn(a_b[h], do_b[h]) + _nt(k_st_b[:, _hs(h)], dst_b[h]) for h in range(H)])
            dq_in = cat([_nn(da_b[h], k_in_b[:, _hs(h)]) for h in range(H)])
            dk_in = cat([_tn(da_b[h], q_in_b[:, _hs(h)]) for h in range(H)])
            dq_out = cat([_nn(do_b[h], st_b[h]) for h in range(H)])
            dk_st = cat([_nn(v[:, _hs(h)], dst_b[h]) for h in range(H)])
            ddec = cat([jnp.sum(st[h] * dst[h], axis=0, keepdims=True) for h in range(H)])
            for h in range(H):
                dstate[h] = dst[h] * dec[:, _hs(h)] + _tn(do_b[h], q_out_b[:, _hs(h)])
            t_qin = dq_in * q_in
            t_kin = dk_in * k_in
            t_kst = dk_st * k_st
            db = t_qin - t_kin + dq_out * q_out - t_kst
            dbref = jnp.sum(t_kin - t_qin, axis=0, keepdims=True)
            dblast = jnp.sum(t_kst, axis=0, keepdims=True) + ddec * dec
            db = db + jnp.where(rowid == CHUNK // 2, dbref, 0.0) + jnp.where(rowid == CHUNK - 1, dblast, 0.0)
            dlogf = _cumsum_rows(triu_b, db)
            dqf = dq_in * e1 + dq_out * e3
            dk = dk_in * e2 + dk_st * e4
            df = dlogf / f - dk
            dfz = df * (1.0 - lb) * sg * (1.0 - sg)
            dlb_ref[...] += jnp.sum(df * (1.0 - sg), axis=0, keepdims=True)
            sq = _sigmoid(q)
            dq = dqf * sq * (1.0 + q * (1.0 - sq))
            dz_ref[rows, 0:HF] = dq.astype(BF16)
            dz_ref[rows, HF:2 * HF] = dfz.astype(BF16)
            dz_ref[rows, 2 * HF:3 * HF] = dv.astype(BF16)
            dz_ref[rows, 3 * HF:4 * HF] = dhg.astype(BF16)

    return pl.pallas_call(
        body, name=name, grid=(B, nblk),
        in_specs=[pl.BlockSpec((None, ts, 4 * HF), rev),
                  pl.BlockSpec((1, HF), lambda b, s: (0, 0)),
                  pl.BlockSpec((1, HGRN_DK), lambda b, s: (0, 0)),
                  pl.BlockSpec((None, cpb, HGRN_HEADS, HGRN_DK, HGRN_DK), lambda b, s: (b, nblk - 1 - s, 0, 0, 0)),
                  pl.BlockSpec((None, ts, HF), rev),
                  ANY],
        out_specs=[pl.BlockSpec((None, ts, 4 * HF), rev),
                   pl.BlockSpec((1, HF), lambda b, s: (0, 0)),
                   pl.BlockSpec((1, HGRN_DK), lambda b, s: (0, 0))],
        out_shape=[jax.ShapeDtypeStruct(dz.shape, BF16),
                   jax.ShapeDtypeStruct((1, HF), F32),
                   jax.ShapeDtypeStruct((1, HGRN_DK), F32)],
        input_output_aliases={5: 0},
        scratch_shapes=[pltpu.VMEM((HGRN_HEADS, HGRN_DK, HGRN_DK), F32)],
        compiler_params=_params("arbitrary", "arbitrary"),
    )(zh, lb, gn, states, doa, dz)


KV_W = ATT_KV_HEADS * ATT_HD
ATT_SCALE = ATT_HD ** -0.5


def _rope(x, cos, sin, inverse=False):
    half = ROPE_DIM // 2
    outs = []
    for p in range(x.shape[1] // 128):
        xp = x[:, p * 128:(p + 1) * 128]
        lane = lax.broadcasted_iota(jnp.int32, xp.shape, 1) % ATT_HD
        sw = jnp.where(lane < half, pltpu.roll(xp, 128 - half, 1), pltpu.roll(xp, half, 1))
        outs.append(xp * cos - sw * sin if inverse else xp * cos + sw * sin)
    return outs[0] if len(outs) == 1 else jnp.concatenate(outs, axis=1)


PAIRS_PER_KV = ATT_GROUP // 2


def _swap_halves(x):
    return pltpu.roll(x, ATT_HD, 1)


def _kv_padded(t, low):
    sw = _swap_halves(t)
    zero = jnp.zeros_like(t)
    out = []
    for g in range(ATT_KV_HEADS):
        in_low, in_high = (t, sw) if g == 0 else (sw, t)
        out.append((jnp.where(low, in_low, zero).astype(BF16), jnp.where(low, zero, in_high).astype(BF16)))
    return out


def _swa_mask(first_block):
    qi = lax.broadcasted_iota(jnp.int32, (WINDOW, 2 * WINDOW), 0)
    mi = lax.broadcasted_iota(jnp.int32, (WINDOW, 2 * WINDOW), 1)
    band = (mi > qi) & (mi <= qi + WINDOW)
    return band & (jnp.logical_not(first_block) | (mi >= WINDOW))


def _swa_specs(nb):
    cur = lambda b, i: (b, i, 0)
    prev = lambda b, i: (b, jnp.maximum(i - 1, 0), 0)
    return cur, prev


def _swa_z_specs():
    q = pl.BlockSpec((None, WINDOW, W_AQ), lambda b, i: (b, i, O_AQ // W_AQ))
    kv_prev = pl.BlockSpec((None, WINDOW, W_AKV), lambda b, i: (b, jnp.maximum(i - 1, 0), O_AKV // W_AKV))
    kv_cur = pl.BlockSpec((None, WINDOW, W_AKV), lambda b, i: (b, i, O_AKV // W_AKV))
    return q, kv_prev, kv_cur


def _swa_fwd(z, cos, sin, sinks, *, name):
    B, S, _ = z.shape
    nb = S // WINDOW
    cur, prev = _swa_specs(nb)

    def body(q_ref, kvp_ref, kvc_ref, cp_ref, sp_ref, cc_ref, sc_ref, sink_ref, o_ref, lse_ref):
        cos_c, sin_c = cc_ref[...], sc_ref[...]
        q = (_rope(q_ref[...].astype(F32), cos_c, sin_c) * ATT_SCALE).astype(BF16)
        k = jnp.concatenate([_rope(kvp_ref[:, :KV_W].astype(F32), cp_ref[...], sp_ref[...]),
                             _rope(kvc_ref[:, :KV_W].astype(F32), cos_c, sin_c)], axis=0)
        v = jnp.concatenate([kvp_ref[:, KV_W:], kvc_ref[:, KV_W:]], axis=0).astype(F32)
        low = lax.broadcasted_iota(jnp.int32, k.shape, 1) < ATT_HD
        kpad = _kv_padded(k, low)
        vpad = _kv_padded(v, low)
        mask = _swa_mask(pl.program_id(1) == 0)
        lses = []
        for g in range(ATT_KV_HEADS):
            pairs = range(g * PAIRS_PER_KV, (g + 1) * PAIRS_PER_KV)
            keys = [(p, e) for p in pairs for e in (0, 1)]
            qp = {p: q[:, p * 128:(p + 1) * 128] for p in pairs}
            s = {pe: jnp.where(mask, _nt(qp[pe[0]], kpad[g][pe[1]]), NEG_INF) for pe in keys}
            pr = {}
            for pe in keys:
                sink = sink_ref[0, 2 * pe[0] + pe[1]]
                m = jnp.maximum(jnp.max(s[pe], axis=1, keepdims=True), sink)
                ex = jnp.exp(s[pe] - m)
                den = jnp.sum(ex, axis=1, keepdims=True) + jnp.exp(sink - m)
                pr[pe] = (ex * (1.0 / den)).astype(BF16)
                lses.append(m + jnp.log(den))
            for p in pairs:
                o_ref[:, p * 128:(p + 1) * 128] = (_nn(pr[p, 0], vpad[g][0]) + _nn(pr[p, 1], vpad[g][1])).astype(BF16)
        lse_ref[...] = jnp.concatenate(lses, axis=1)

    tab = lambda im: pl.BlockSpec((None, WINDOW, 128), im)
    return pl.pallas_call(
        body, name=name, grid=(B, nb),
        in_specs=[*_swa_z_specs(),
                  tab(prev), tab(prev), tab(cur), tab(cur),
                  pl.BlockSpec(memory_space=pltpu.SMEM)],
        out_specs=[pl.BlockSpec((None, WINDOW, D_MODEL), cur), pl.BlockSpec((None, WINDOW, ATT_HEADS), cur)],
        out_shape=[jax.ShapeDtypeStruct((B, S, D_MODEL), BF16), jax.ShapeDtypeStruct((B, S, ATT_HEADS), F32)],
        compiler_params=_params("parallel", "parallel"),
    )(z, z, z, cos, sin, cos, sin, sinks)


def _swa_bwd(z, cos, sin, sinks, lse, dob, dz, *, name):
    B, S, _ = z.shape
    nb = S // WINDOW
    cur, prev = _swa_specs(nb)

    def body(q_ref, kvp_ref, kvc_ref, cp_ref, sp_ref, cc_ref, sc_ref, sink_ref, lse_ref, do_ref, dz_in,
             dq_ref, dkc_ref, dkp_ref, dsink_ref):
        @pl.when((pl.program_id(0) == 0) & (pl.program_id(1) == 0))
        def _():
            dsink_ref[...] = jnp.zeros_like(dsink_ref)

        cos_c, sin_c, cos_p, sin_p = cc_ref[...], sc_ref[...], cp_ref[...], sp_ref[...]
        q = (_rope(q_ref[...].astype(F32), cos_c, sin_c) * ATT_SCALE).astype(BF16)
        k = jnp.concatenate([_rope(kvp_ref[:, :KV_W].astype(F32), cos_p, sin_p),
                             _rope(kvc_ref[:, :KV_W].astype(F32), cos_c, sin_c)], axis=0)
        v = jnp.concatenate([kvp_ref[:, KV_W:], kvc_ref[:, KV_W:]], axis=0).astype(F32)
        low = lax.broadcasted_iota(jnp.int32, k.shape, 1) < ATT_HD
        kpad = _kv_padded(k, low)
        vpad = _kv_padded(v, low)
        mask = _swa_mask(pl.program_id(1) == 0)
        lse = lse_ref[...]
        dq_parts, dk_sum, dv_sum, dsinks = [], [], [], []
        for g in range(ATT_KV_HEADS):
            pairs = range(g * PAIRS_PER_KV, (g + 1) * PAIRS_PER_KV)
            keys = [(p, e) for p in pairs for e in (0, 1)]
            qp = {p: q[:, p * 128:(p + 1) * 128] for p in pairs}
            dop = {p: do_ref[:, p * 128:(p + 1) * 128] for p in pairs}
            s = {pe: jnp.where(mask, _nt(qp[pe[0]], kpad[g][pe[1]]), NEG_INF) for pe in keys}
            dp = {pe: _nt(dop[pe[0]], vpad[g][pe[1]]) for pe in keys}
            pr, ds = {}, {}
            for pe in keys:
                h = 2 * pe[0] + pe[1]
                lse_h = lse[:, h:h + 1]
                pf = jnp.exp(s[pe] - lse_h)
                delta = jnp.sum(pf * dp[pe], axis=1, keepdims=True)
                ds[pe] = (pf * (dp[pe] - delta)).astype(BF16)
                pr[pe] = pf.astype(BF16)
                p_sink = jnp.exp(sink_ref[0, h] - lse_h)
                dsinks.append(-jnp.sum(p_sink * delta, axis=0, keepdims=True))
            for p in pairs:
                dq_parts.append((_nn(ds[p, 0], kpad[g][0]) + _nn(ds[p, 1], kpad[g][1])) * ATT_SCALE)
            x = [sum(_tn(ds[p, e], qp[p]) for p in pairs) for e in (0, 1)]
            y = [sum(_tn(pr[p, e], dop[p]) for p in pairs) for e in (0, 1)]
            zk = jnp.where(low, x[0], x[1])
            zv = jnp.where(low, y[0], y[1])
            dk_sum.append(zk + _swap_halves(zk))
            dv_sum.append(zv + _swap_halves(zv))
        dq_ref[...] = _rope(jnp.concatenate(dq_parts, axis=1), cos_c, sin_c, inverse=True).astype(BF16)
        dk = jnp.where(low, dk_sum[0], dk_sum[1])
        dv = jnp.where(low, dv_sum[0], dv_sum[1])
        dkp_ref[:, :KV_W] = _rope(dk[:WINDOW], cos_p, sin_p, inverse=True)
        dkp_ref[:, KV_W:] = dv[:WINDOW]
        dkc_ref[:, :KV_W] = _rope(dk[WINDOW:], cos_c, sin_c, inverse=True)
        dkc_ref[:, KV_W:] = dv[WINDOW:]
        dsink_ref[...] += jnp.broadcast_to(jnp.concatenate(dsinks, axis=0), (ATT_HEADS, 128))

    tab = lambda im: pl.BlockSpec((None, WINDOW, 128), im)
    return pl.pallas_call(
        body, name=name, grid=(B, nb),
        in_specs=[*_swa_z_specs(),
                  tab(prev), tab(prev), tab(cur), tab(cur),
                  pl.BlockSpec(memory_space=pltpu.SMEM),
                  pl.BlockSpec((None, WINDOW, ATT_HEADS), cur),
                  pl.BlockSpec((None, WINDOW, D_MODEL), cur),
                  ANY],
        out_specs=[_swa_z_specs()[0],
                   pl.BlockSpec((None, WINDOW, 2 * KV_W), cur), pl.BlockSpec((None, WINDOW, 2 * KV_W), cur),
                   pl.BlockSpec((ATT_HEADS, 128), lambda b, i: (0, 0))],
        out_shape=[jax.ShapeDtypeStruct(dz.shape, BF16),
                   jax.ShapeDtypeStruct((B, S, 2 * KV_W), F32), jax.ShapeDtypeStruct((B, S, 2 * KV_W), F32),
                   jax.ShapeDtypeStruct((ATT_HEADS, 128), F32)],
        input_output_aliases={10: 0},
        compiler_params=_params("arbitrary", "arbitrary"),
    )(z, z, z, cos, sin, cos, sin, sinks, lse, dob, dz)


def _swa_dkv_combine(dkv_cur, dkv_prev, dz, *, name):
    B, S, W = dkv_cur.shape

    def body(c_ref, p_ref, dz_in, o_ref):
        rows = lax.broadcasted_iota(jnp.int32, (S, W), 0)
        o_ref[...] = (c_ref[...] + _shift_up(p_ref[...], WINDOW, rows, S)).astype(BF16)

    seq = pl.BlockSpec((None, S, W), lambda b: (b, 0, 0))
    return pl.pallas_call(
        body, name=name, grid=(B,),
        in_specs=[seq, seq, ANY], out_specs=pl.BlockSpec((None, S, W), lambda b: (b, 0, O_AKV // W_AKV)),
        out_shape=jax.ShapeDtypeStruct(dz.shape, BF16),
        input_output_aliases={2: 0},
        compiler_params=_params("parallel"),
    )(dkv_cur, dkv_prev, dz)


def _rope_tables(positions):
    half = ROPE_DIM // 2
    inv = ROPE_THETA ** (-2.0 * jnp.arange(half, dtype=F32) / ROPE_DIM)
    ang = positions.astype(F32)[..., None] * inv
    c, s = jnp.cos(ang), jnp.sin(ang)
    pad = jnp.zeros(ang.shape[:-1] + (ATT_HD - ROPE_DIM,), F32)
    cos = jnp.concatenate([c, c, pad + 1.0], axis=-1)
    sin = jnp.concatenate([-s, s, pad], axis=-1)
    return jnp.tile(cos, (1, 1, 2)), jnp.tile(sin, (1, 1, 2))


def _lower_bound(lb_logits, *, name):
    def body(l_ref, o_ref):
        l = l_ref[...]
        e = jnp.exp(l - jnp.max(l, axis=0, keepdims=True))
        o_ref[...] = e[0:1] / jnp.sum(e, axis=0, keepdims=True)

    return pl.pallas_call(body, name=name, out_shape=jax.ShapeDtypeStruct((1, lb_logits.shape[1]), F32))(lb_logits)


W_ZH, W_GATES, W_AQ, W_AKV = 4 * HF, 2 * D_MODEL, ATT_HEADS * ATT_HD, 2 * KV_W
O_ZH, O_GATES, O_AQ, O_AKV = 0, W_ZH, W_ZH + W_GATES, W_ZH + W_GATES + W_AQ
W_IN = W_ZH + W_GATES + W_AQ + W_AKV


def _reorder_w_in(w_in_full):
    return jnp.concatenate([w_in_full[:, :W_ZH], w_in_full[:, W_ZH + W_AQ + W_AKV:], w_in_full[:, W_ZH:W_ZH + W_AQ + W_AKV]], axis=1)


def _reference_order_w_in(w):
    return jnp.concatenate([w[:, :W_ZH], w[:, O_AQ:], w[:, O_GATES:O_AQ]], axis=1)


def _local_step(x, positions, target, small, w_in, rest_weights, emit, start_token):
    B, S, D = x.shape
    T = B * S
    x2 = x.reshape(T, D)
    cos, sin = _rope_tables(positions)
    lb = _lower_bound(small["lb_logits"], name="lb_fwd")
    zero = lambda tok: tok[0:1, 0:1]

    u1 = _norm_cast(x2, small["norm1_g"] + zero(start_token), name="norm1")
    z = _matmul(u1, w_in, out_dtype=BF16, name="mm_z", tm=1024, tn=W_IN // 2)
    z3 = z.reshape(B, S, W_IN)
    oa, states = _hgrn_fwd(z3, lb, small["hgrn_norm_g"], name="hgrn_fwd")
    ob, lse = _swa_fwd(z3, cos, sin, small["attn_sinks"], name="swa_fwd")
    oa2 = oa.reshape(T, D)
    ob2 = ob.reshape(T, D)
    W = rest_weights("mix", ob)
    pa = _matmul(oa2, W["w_a"], out_dtype=BF16, name="mm_pa", tm=2048, tn=512)
    pb = _matmul(ob2, W["w_b"], out_dtype=BF16, name="mm_pb", tm=2048, tn=512)
    merged = _merge_fwd(z, pa, pb, name="merge_fwd")
    h = _matmul(merged, W["w_out"], addend=x2, name="mm_h", tm=2048, tn=512)
    u2 = _norm_cast(h, small["norm2_g"], name="norm2")
    W.update(rest_weights("ffn", u2))
    gu = _matmul(u2, W["w_ffn"], out_dtype=BF16, name="mm_gu", tm=2048, tn=512)
    gu3 = gu.reshape(B, S, 2 * D_FF)
    act = _conv_act_fwd(gu3, W["conv_w"], small["conv_b"], name="conv_act_fwd")
    act2 = act.reshape(T, D_FF)
    h2 = _matmul(act2, W["w_down"], addend=h, name="mm_h2", tm=1024, tn=512)

    g = {}
    dh2, dh2b, g["final_g"], loss = _final_loss_bwd(h2, small["final_g"].reshape(1, D), target.reshape(T, D), name="final_loss_bwd")
    dact = _matmul(dh2b, W["w_down"], tb=True, out_dtype=BF16, name="mm_dact", tm=1024, tn=D_FF)
    dw_down = _matmul(act2, dh2b, ta=True, out_dtype=BF16, name="mm_dw_down", tm=D_FF, tn=1024, tk=1024)
    dg_, dup, g["conv_w"], g["conv_b"] = _conv_act_bwd(gu3, W["conv_w"], small["conv_b"], dact.reshape(B, S, D_FF), name="conv_act_bwd")
    dg2 = dg_.reshape(T, D_FF)
    dup2 = dup.reshape(T, D_FF)
    du2 = _matmul(dg2, W["w_ffn"], tb=True, name="mm_du2_g", tm=1024, tn=512, b_koff=0)
    du2 = _matmul(dup2, W["w_ffn"], tb=True, addend=du2, name="mm_du2_u", tm=1024, tn=512, b_koff=1)
    dw_ffn_g = _matmul(u2, dg2, ta=True, out_dtype=BF16, name="mm_dw_ffn_g", tm=1024, tn=D_FF, tk=1024)
    dw_ffn_u = _matmul(u2, dup2, ta=True, out_dtype=BF16, name="mm_dw_ffn_u", tm=1024, tn=D_FF, tk=1024)
    tok = emit("ffn", dict(w_ffn_g=dw_ffn_g, w_ffn_u=dw_ffn_u, w_down=dw_down))
    dh, dhb, g["norm2_g"] = _norm_bwd_add(h, small["norm2_g"] + zero(tok), du2, dh2, name="norm2_bwd")
    dmerged = _matmul(dhb, W["w_out"], tb=True, out_dtype=BF16, name="mm_dmerged", tm=2048, tn=512)
    dw_out = _matmul(merged, dhb, ta=True, out_dtype=BF16, name="mm_dw_out", tm=1024, tn=1024, tk=2048)
    dz, dpa, dpb = _merge_bwd(z, pa, pb, dmerged, lax.empty((T, W_IN), BF16), name="merge_bwd")
    doa =_matmul(dpa, W["w_a"], tb=True, out_dtype=BF16, name="mm_doa", tm=2048, tn=512)
    dw_a = _matmul(oa2, dpa, ta=True, out_dtype=BF16, name="mm_dw_a", tm=1024, tn=1024, tk=2048)
    dob = _matmul(dpb, W["w_b"], tb=True, out_dtype=BF16, name="mm_dob", tm=2048, tn=512)
    dw_b = _matmul(ob2, dpb, ta=True, out_dtype=BF16, name="mm_dw_b", tm=1024, tn=1024, tk=2048)
    tok = emit("mix", dict(w_out=dw_out, w_a=dw_a, w_b=dw_b))
    dz3, dkv_cur, dkv_prev, dsinks = _swa_bwd(z3, cos, sin, small["attn_sinks"] + zero(tok), lse, dob.reshape(B, S, D),
                                              dz.reshape(B, S, W_IN), name="swa_bwd")
    dz3 = _swa_dkv_combine(dkv_cur, dkv_prev, dz3, name="swa_dkv")
    g["attn_sinks"] = dsinks
    dz3, g["lb"], g["hgrn_norm_g"] = _hgrn_bwd(z3, lb, small["hgrn_norm_g"], states, doa.reshape(B, S, D), dz3, name="hgrn_bwd")
    dz = dz3.reshape(T, W_IN)
    dw_in = _matmul(u1, dz, ta=True, out_dtype=BF16, name="mm_dw_in", tm=1024, tn=W_IN // 2, tk=512)
    tok = emit("in", dict(w_in=dw_in))
    du1 = _matmul(dz, w_in, tb=True, after=tok, name="mm_du1", tm=512, tn=512)
    dx, _, g["norm1_g"] = _norm_bwd_add(x2, small["norm1_g"], du1, dh, name="norm1_bwd")
    g["lb_logits"] = _lb_bwd(g.pop("lb"), lb, name="lb_bwd")
    return loss, dx.reshape(B, S, D), g


def _my_place():
    return lax.axis_index("x"), lax.axis_index("y"), lax.axis_index("c")


def _all_gather(blk, *, in_vmem, reduce_sum=False, name):
    m, n = blk.shape
    space = pltpu.VMEM if in_vmem else pl.ANY

    def body(x_ref, out_ref, *rest):
        if reduce_sum:
            tot_ref, send_sems, recv_sems, local_sem = rest
        else:
            send_sems, recv_sems, local_sem = rest
        x, y, c = _my_place()
        me, sibling = (x, y, c), (x, y, 1 - c)
        chips = [(1 - x, y), (x, 1 - y), (1 - x, 1 - y)]

        def slot(px, py, pc):
            return out_ref.at[4 * px + 2 * py + pc]

        def copy(k, block, to, src=None):
            return pltpu.make_async_remote_copy(
                src_ref=slot(*block) if src is None else src, dst_ref=slot(*block),
                send_sem=send_sems.at[k], recv_sem=recv_sems.at[k], device_id=to, device_id_type=MESH)

        mine = pltpu.make_async_copy(x_ref, slot(*me), local_sem)
        mine.start()
        first = [copy(0, me, sibling, src=x_ref)]
        first += [copy(1 + j, me, (*chip, c), src=x_ref) for j, chip in enumerate(chips)]
        for cp in first:
            cp.start()
        passed = [copy(4 + j, (*chip, c), sibling) for j, chip in enumerate(chips)]
        for j, chip in enumerate(chips):
            copy(1 + j, (*chip, c), me).wait_recv()
            passed[j].start()
        copy(0, sibling, me).wait_recv()
        for j, chip in enumerate(chips):
            copy(4 + j, (*chip, 1 - c), me).wait_recv()
        for cp in first + passed:
            cp.wait_send()
        mine.wait()
        if reduce_sum:
            acc = out_ref[0]
            for p in range(1, N_DEV):
                acc = acc + out_ref[p]
            tot_ref[...] = acc

    out_shape = [jax.ShapeDtypeStruct((N_DEV, m, n), blk.dtype)]
    out_specs = [pl.BlockSpec(memory_space=space)]
    if reduce_sum:
        out_shape.append(jax.ShapeDtypeStruct((m, n), blk.dtype))
        out_specs.append(pl.BlockSpec(memory_space=pltpu.VMEM))
    res = pl.pallas_call(
        body, name=name,
        out_shape=out_shape,
        in_specs=[pl.BlockSpec(memory_space=space)],
        out_specs=out_specs,
        scratch_shapes=[pltpu.SemaphoreType.DMA((7,)), pltpu.SemaphoreType.DMA((7,)), pltpu.SemaphoreType.DMA],
    )(blk)
    return res if reduce_sum else res[0]


HBM_SPEC = pl.BlockSpec(memory_space=pltpu.HBM)
SEM_SPEC = pl.BlockSpec(memory_space=pltpu.SEMAPHORE)
DATAFLOW_EFFECT = pltpu.SideEffectType.DATAFLOW_SIDE_EFFECTING
N_PEERS = N_DEV - 1


def _peers(x, y, c):
    return [(1 - x if r & 4 else x, 1 - y if r & 2 else y, 1 - c if r & 1 else c) for r in range(1, N_DEV)]


def _exchange_start(srcs, scatter, *, name):
    n = len(srcs)
    lands = [lax.empty(a.shape if scatter else (N_DEV,) + a.shape, a.dtype) for a in srcs]

    def body(*refs):
        src_refs, land_refs = refs[:n], refs[n:2 * n]
        send_sems, recv_sems, token = refs[2 * n], refs[2 * n + 1], refs[-1]
        x, y, c = _my_place()
        me = 4 * x + 2 * y + c
        for i in range(n):
            for r, (tx, ty, tc) in enumerate(_peers(x, y, c)):
                src = src_refs[i].at[4 * tx + 2 * ty + tc] if scatter else src_refs[i]
                pltpu.make_async_remote_copy(
                    src_ref=src, dst_ref=land_refs[i].at[me], send_sem=send_sems.at[N_PEERS * i + r],
                    recv_sem=recv_sems.at[N_PEERS * i + r], device_id=(tx, ty, tc), device_id_type=MESH).start()
        token[...] = jnp.zeros_like(token)

    thru = [pltpu.HBM(a.shape, a.dtype) for a in list(srcs) + lands]
    res = pl.pallas_call(
        body, name=name,
        out_shape=(pltpu.SemaphoreType.DMA((N_PEERS * n,)), pltpu.SemaphoreType.DMA((N_PEERS * n,)), *thru,
                   jax.ShapeDtypeStruct((8, 128), F32)),
        in_specs=[HBM_SPEC] * (2 * n),
        out_specs=(SEM_SPEC, SEM_SPEC, *([HBM_SPEC] * (2 * n)), pl.BlockSpec(memory_space=pltpu.VMEM)),
        input_output_aliases={i: 2 + i for i in range(2 * n)},
        compiler_params=pltpu.CompilerParams(has_side_effects=DATAFLOW_EFFECT),
    )(*[pltpu.with_memory_space_constraint(a, pltpu.HBM) for a in list(srcs) + lands])
    return (res[0], res[1], list(res[2:2 + n]), list(res[2 + n:2 + 2 * n]), scatter), res[-1]


def _exchange_wait(handle, after, *, name):
    send_sems, recv_sems, srcs, lands, scatter = handle
    n = len(srcs)

    def body(*refs):
        src_refs, land_refs = refs[:n], refs[n:2 * n]
        send_sems, recv_sems = refs[2 * n], refs[2 * n + 1]
        x, y, c = _my_place()
        for i in range(n):
            for r in range(N_PEERS):
                src = src_refs[i].at[0] if scatter else src_refs[i]
                cp = pltpu.make_async_remote_copy(
                    src_ref=src, dst_ref=land_refs[i].at[0], send_sem=send_sems.at[N_PEERS * i + r],
                    recv_sem=recv_sems.at[N_PEERS * i + r], device_id=(x, y, c), device_id_type=MESH)
                cp.wait_send()
                cp.wait_recv()

    thru = [pltpu.HBM(a.shape, a.dtype) for a in srcs + lands]
    res = pl.pallas_call(
        body, name=name, out_shape=tuple(thru),
        in_specs=[HBM_SPEC] * (2 * n) + [SEM_SPEC, SEM_SPEC, ANY], out_specs=tuple([HBM_SPEC] * (2 * n)),
        input_output_aliases={i: i for i in range(2 * n)},
        compiler_params=pltpu.CompilerParams(has_side_effects=DATAFLOW_EFFECT),
    )(*srcs, *lands, send_sems, recv_sems, after)
    return list(res[:n]), list(res[n:])


def _with_own(land, own, me):
    return lax.dynamic_update_index_in_dim(land, own, me, 0)


def _adamw_math(w, g, m, v):
    m = ADAM_B1 * m + (1.0 - ADAM_B1) * g
    v = ADAM_B2 * v + (1.0 - ADAM_B2) * (g * g)
    m_hat = m / (1.0 - ADAM_B1 ** ADAM_STEP)
    v_hat = v / (1.0 - ADAM_B2 ** ADAM_STEP)
    delta = -ADAM_LR * (m_hat / (jnp.sqrt(v_hat) + ADAM_EPS) + ADAM_WD * w)
    return delta, m, v


def _adamw_sum(parts, w, m, v, *, name):
    shape = w.shape
    R, n = shape[-2], shape[-1]
    w, m, v = (t.reshape(R, n) for t in (w, m, v))
    tr = _pick(R, (256, 176, 128))

    def body(p_ref, w_ref, m_ref, v_ref, g_ref, d_ref, mo_ref, vo_ref):
        g = p_ref[0].astype(F32)
        for p in range(1, N_DEV):
            g = g + p_ref[p].astype(F32)
        d, mn, vn = _adamw_math(w_ref[...], g, m_ref[...], v_ref[...])
        g_ref[...] = g
        d_ref[...] = d
        mo_ref[...] = mn
        vo_ref[...] = vn

    row = pl.BlockSpec((tr, n), lambda i: (i, 0))
    outs = pl.pallas_call(
        body, name=name, grid=(R // tr,),
        in_specs=[pl.BlockSpec((N_DEV, tr, n), lambda i: (0, i, 0)), row, row, row],
        out_specs=[row, row, row, row],
        out_shape=[jax.ShapeDtypeStruct((R, n), F32)] * 4,
        compiler_params=_params("parallel"),
    )(parts, w, m, v)
    return [t.reshape(shape) for t in outs]


def _adamw_small(g, w, m, v, *, name):
    def body(g_ref, w_ref, m_ref, v_ref, d_ref, mo_ref, vo_ref):
        d, mn, vn = _adamw_math(w_ref[...], g_ref[...], m_ref[...], v_ref[...])
        d_ref[...] = d
        mo_ref[...] = mn
        vo_ref[...] = vn

    return pl.pallas_call(body, name=name, out_shape=[jax.ShapeDtypeStruct(w.shape, F32)] * 3)(g, w, m, v)


def _lb_bwd(dlb, lb, *, name):
    def body(d_ref, lb_ref, o_ref):
        t = d_ref[...] * lb_ref[...] * (1.0 - lb_ref[...])
        o_ref[0:1, :] = t
        o_ref[1:2, :] = -t

    return pl.pallas_call(body, name=name, out_shape=jax.ShapeDtypeStruct((2, lb.shape[1]), F32))(dlb, lb)


LANES = 128
N_IN, N_FFN = 7424, 5632
IN_BLK, FFN_BLK, DOWN_BLK, ROW_BLK = N_IN // N_DEV, N_FFN // N_DEV, D_FF // N_DEV, D_MODEL // N_DEV
CONVW_BLK = D_FF // N_DEV
SMALL_NAMES = ("norm1_g", "lb_logits", "hgrn_norm_g", "attn_sinks", "norm2_g", "conv_b", "final_g")
CONV_BITS_SHAPE = (16, 256)


def _cols_from_blocks(blocks):
    n, rows, width = blocks.shape
    return blocks.transpose(1, 0, 2).reshape(rows, n * width)


def _blocks_from_cols(full):
    rows, cols = full.shape
    return full.reshape(rows, N_DEV, cols // N_DEV).transpose(1, 0, 2)


def _to_rows(vec, rows):
    vec = vec.reshape(-1)
    return jnp.pad(vec, (0, rows * LANES - vec.shape[0])).reshape(rows, LANES)


def kernel(x, positions, norm1_g, w_in, lb_logits, hgrn_norm_g, w_a, attn_sinks, w_b, w_out, norm2_g, w_ffn_in, conv_w, conv_b, w_down, final_g, loss_target, m_norm1_g, m_w_in, m_lb_logits, m_hgrn_norm_g, m_w_a, m_attn_sinks, m_w_b, m_w_out, m_norm2_g, m_w_ffn_in, m_conv_w, m_conv_b, m_w_down, m_final_g, v_norm1_g, v_w_in, v_lb_logits, v_hgrn_norm_g, v_w_a, v_attn_sinks, v_w_b, v_w_out, v_norm2_g, v_w_ffn_in, v_conv_w, v_conv_b, v_w_down, v_final_g):
    xi, yi, ci = _my_place()
    dev = 4 * xi + 2 * yi + ci

    w_in_blocks = _all_gather(w_in[0].astype(BF16), in_vmem=False, name="ag_w_in")
    conv_bits = lax.bitcast_convert_type(conv_w, BF16).reshape(-1)
    conv_bits = jnp.pad(conv_bits, (0, CONV_BITS_SHAPE[0] * CONV_BITS_SHAPE[1] - conv_bits.shape[0])).reshape(CONV_BITS_SHAPE)
    gather_handles = {}
    gather_handles["mix"], tok_mix = _exchange_start([w_a[0].astype(BF16), w_b[0].astype(BF16), w_out[0].astype(BF16)], False, name="ag_mix_start")
    gather_handles["ffn"], tok_ffn = _exchange_start([w_ffn_in[0].astype(BF16), w_down[0].astype(BF16), conv_bits], False, name="ag_ffn_start")
    start_token = tok_mix + tok_ffn

    def rest_weights(group, after):
        own, lands = _exchange_wait(gather_handles[group], after, name="ag_" + group + "_wait")
        full = [_with_own(l, o, dev) for l, o in zip(lands, own)]
        if group == "mix":
            return dict(zip(("w_a", "w_b", "w_out"), [t.reshape(D_MODEL, D_MODEL) for t in full]))
        bits = full[2].reshape(N_DEV, -1)[:, :3 * CONVW_BLK * 2].reshape(N_DEV, 3, CONVW_BLK, 2)
        return dict(w_ffn=_cols_from_blocks(full[0]), w_down=full[1].reshape(D_FF, D_MODEL),
                    conv_w=_cols_from_blocks(lax.bitcast_convert_type(bits, F32)))

    handles = {}

    def emit(group, gr):
        if group == "ffn":
            srcs = [_blocks_from_cols(jnp.concatenate([gr["w_ffn_g"], gr["w_ffn_u"]], axis=1)), gr["w_down"].reshape(N_DEV, DOWN_BLK, D_MODEL)]
        elif group == "mix":
            srcs = [gr[n].reshape(N_DEV, ROW_BLK, D_MODEL) for n in ("w_out", "w_a", "w_b")]
        else:
            srcs = [_blocks_from_cols(_reference_order_w_in(gr["w_in"]))]
        handles[group], token = _exchange_start(srcs, True, name="rs_" + group + "_start")
        return token

    small = dict(norm1_g=norm1_g, lb_logits=lb_logits, hgrn_norm_g=hgrn_norm_g, attn_sinks=attn_sinks, norm2_g=norm2_g,
                 conv_b=conv_b, final_g=final_g)
    w_in_full = _reorder_w_in(_cols_from_blocks(w_in_blocks))
    loss, grad_x, g = _local_step(x, positions, loss_target, small, w_in_full, rest_weights, emit, start_token)

    def parts_of(group, after):
        srcs, lands = _exchange_wait(handles[group], after, name="rs_" + group + "_wait")
        return [_with_own(l, lax.dynamic_index_in_dim(s, dev, 0, keepdims=False), dev) for s, l in zip(srcs, lands)]

    p_ffn, p_down = parts_of("ffn", grad_x)
    p_out, p_a, p_b = parts_of("mix", grad_x)
    (p_in,) = parts_of("in", grad_x)
    big = dict(
        w_in=_adamw_sum(p_in, w_in, m_w_in, v_w_in, name="adamw_w_in"),
        w_a=_adamw_sum(p_a, w_a, m_w_a, v_w_a, name="adamw_w_a"),
        w_b=_adamw_sum(p_b, w_b, m_w_b, v_w_b, name="adamw_w_b"),
        w_out=_adamw_sum(p_out, w_out, m_w_out, v_w_out, name="adamw_w_out"),
        w_ffn_in=_adamw_sum(p_ffn, w_ffn_in, m_w_ffn_in, v_w_ffn_in, name="adamw_w_ffn_in"),
        w_down=_adamw_sum(p_down, w_down, m_w_down, v_w_down, name="adamw_w_down"),
    )

    sm_g = dict(norm1_g=g["norm1_g"], lb_logits=g["lb_logits"], hgrn_norm_g=g["hgrn_norm_g"], attn_sinks=g["attn_sinks"][:, 0],
                norm2_g=g["norm2_g"], conv_b=g["conv_b"], final_g=g["final_g"])
    vec = jnp.concatenate([sm_g[n].reshape(-1) for n in SMALL_NAMES] + [g["conv_w"].reshape(-1), loss.reshape(-1)])
    sm_rows = 136
    _, total = _all_gather(_to_rows(vec, sm_rows), in_vmem=True, reduce_sum=True, name="ar_small")
    total = total.reshape(-1)
    sm_w = dict(norm1_g=norm1_g, lb_logits=lb_logits, hgrn_norm_g=hgrn_norm_g, attn_sinks=attn_sinks, norm2_g=norm2_g,
                conv_b=conv_b, final_g=final_g)
    sm_m = dict(norm1_g=m_norm1_g, lb_logits=m_lb_logits, hgrn_norm_g=m_hgrn_norm_g, attn_sinks=m_attn_sinks, norm2_g=m_norm2_g,
                conv_b=m_conv_b, final_g=m_final_g)
    sm_v = dict(norm1_g=v_norm1_g, lb_logits=v_lb_logits, hgrn_norm_g=v_hgrn_norm_g, attn_sinks=v_attn_sinks, norm2_g=v_norm2_g,
                conv_b=v_conv_b, final_g=v_final_g)
    sizes = [sm_w[n].size for n in SMALL_NAMES]
    n_rep = sum(sizes)
    g_conv_full = total[n_rep:n_rep + 3 * D_FF].reshape(3, D_FF)
    g_conv = lax.dynamic_slice_in_dim(g_conv_full, dev * CONVW_BLK, CONVW_BLK, axis=1)
    loss_total = total[n_rep + 3 * D_FF]
    ad_rows = 72
    pack_small = lambda d, cw: _to_rows(jnp.concatenate([d[n].reshape(-1) for n in SMALL_NAMES] + [cw.reshape(-1)]), ad_rows)
    g_small = _to_rows(jnp.concatenate([total[:n_rep], g_conv.reshape(-1)]), ad_rows)
    d_s, m_s, v_s = _adamw_small(g_small, pack_small(sm_w, conv_w), pack_small(sm_m, m_conv_w), pack_small(sm_v, v_conv_w), name="adamw_small")

    def unpack_small(t):
        t = t.reshape(-1)
        out, off = {}, 0
        for n, s in zip(SMALL_NAMES, sizes):
            out[n] = t[off:off + s].reshape(sm_w[n].shape)
            off += s
        out["conv_w"] = t[off:off + 3 * CONVW_BLK].reshape(1, 3, CONVW_BLK)
        return out

    names = ("norm1_g", "w_in", "lb_logits", "hgrn_norm_g", "w_a", "attn_sinks", "w_b", "w_out", "norm2_g", "w_ffn_in", "conv_w", "conv_b", "w_down", "final_g")
    outs = [loss_total.reshape(()), grad_x]
    for kind, s_vec in enumerate((g_small, d_s, m_s, v_s)):
        s_un = unpack_small(s_vec)
        outs += [big[n][kind] if n in big else s_un[n] for n in names]
    return tuple(outs)
```

```python
import functools

import jax
import jax.numpy as jnp
from jax import lax
from jax.experimental import pallas as pl
from jax.experimental.pallas import tpu as pltpu

F32 = jnp.float32
BF16 = jnp.bfloat16

D_MODEL = 1024
HGRN_HEADS = 8
HGRN_DK = 128
CHUNK = 64
ATT_HEADS = 16
ATT_KV_HEADS = 2
ATT_HD = 64
ATT_GROUP = ATT_HEADS // ATT_KV_HEADS
WINDOW = 128
ROPE_DIM = ATT_HD // 4
ROPE_THETA = 500000.0
D_FF = 2816
EPS = 1e-6
NEG_INF = -1e30
N_DEV = 8

ADAM_LR = 0.001
ADAM_B1 = 0.9
ADAM_B2 = 0.999
ADAM_EPS = 1e-08
ADAM_WD = 0.01
ADAM_STEP = 10

MESH = pl.DeviceIdType.MESH
ANY = pl.BlockSpec(memory_space=pl.ANY)


def _pick(n, cands):
    for c in cands:
        if n % c == 0:
            return c
    return n


def _sigmoid(x):
    return 0.5 * jnp.tanh(0.5 * x) + 0.5


def _silu(x):
    hx = 0.5 * x
    return hx * jnp.tanh(hx) + hx


def _rms(x, g):
    return x * lax.rsqrt(jnp.mean(x * x, axis=-1, keepdims=True) + EPS) * g


def _dot(a, b, dims):
    return lax.dot_general(a, b, (dims, ((), ())), preferred_element_type=F32)


def _nn(a, b):
    return _dot(a, b, ((1,), (0,)))


def _nt(a, b):
    return _dot(a, b, ((1,), (1,)))


def _tn(a, b):
    return _dot(a, b, ((0,), (0,)))


def _params(*sem):
    return pltpu.CompilerParams(dimension_semantics=sem, vmem_limit_bytes=56 * 1024 * 1024)


def _matmul(a, b, *, ta=False, tb=False, out_dtype=F32, addend=None, after=None, into=None, o_noff=0, name, tm, tn, tk=None,
            n_extent=None, b_koff=0, b_noff=0):
    M, K = (a.shape[1], a.shape[0]) if ta else a.shape
    N = n_extent or (b.shape[0] if tb else b.shape[1])
    tm, tn, tk = min(tm, M), min(tn, N), min(tk or K, K)
    assert M % tm == 0 and N % tn == 0 and K % tk == 0, (name, M, N, K, tm, tn, tk)
    nk = K // tk
    use_scratch = nk > 1 and out_dtype != F32
    grid = (M // tm, N // tn, nk)
    a_spec = pl.BlockSpec((tk, tm), lambda i, j, k: (k, i)) if ta else pl.BlockSpec((tm, tk), lambda i, j, k: (i, k))
    b_spec = pl.BlockSpec((tn, tk), lambda i, j, k: (j + b_noff, k + b_koff)) if tb else pl.BlockSpec((tk, tn), lambda i, j, k: (k + b_koff, j + b_noff))
    o_spec = pl.BlockSpec((tm, tn), lambda i, j, k: (i, j))
    dims = ((0 if ta else 1,), (1 if tb else 0,))
    has_add = addend is not None

    n_in = 2 + has_add + (after is not None) + (into is not None)

    def body(*refs):
        a_ref, b_ref = refs[:2]
        c_ref = refs[2] if has_add else None
        o_ref = refs[n_in]
        part = _dot(a_ref[...], b_ref[...], dims)
        if nk == 1:
            if has_add:
                part = part + c_ref[...].astype(F32)
            o_ref[...] = part.astype(out_dtype)
        else:
            acc_ref = refs[-1] if use_scratch else o_ref
            k = pl.program_id(2)

            @pl.when(k == 0)
            def _():
                acc_ref[...] = part + c_ref[...].astype(F32) if has_add else part

            @pl.when(k > 0)
            def _():
                acc_ref[...] += part

            if use_scratch:
                @pl.when(k == nk - 1)
                def _():
                    o_ref[...] = acc_ref[...].astype(out_dtype)

    in_specs = [a_spec, b_spec] + ([o_spec] if has_add else [])
    args = (a, b) + ((addend,) if has_add else ())
    if after is not None:
        in_specs.append(pl.BlockSpec(after.shape, lambda i, j, k: (0, 0)))
        args += (after,)
    aliases = {}
    if into is not None:
        in_specs.append(ANY)
        args += (into,)
        aliases = {len(args) - 1: 0}
        o_spec = pl.BlockSpec((tm, tn), lambda i, j, k: (i, j + o_noff))
    return pl.pallas_call(
        body,
        name=name,
        grid=grid,
        in_specs=in_specs,
        out_specs=o_spec,
        out_shape=jax.ShapeDtypeStruct((M, N) if into is None else into.shape, out_dtype),
        input_output_aliases=aliases,
        scratch_shapes=[pltpu.VMEM((tm, tn), F32)] if use_scratch else [],
        compiler_params=_params("parallel", "parallel", "arbitrary"),
    )(*args)


def _row_spec(tm, n):
    return pl.BlockSpec((tm, n), lambda i: (i, 0))


def _full_spec(shape):
    return pl.BlockSpec(shape, lambda i: tuple(0 for _ in shape))


def _norm_cast(x, g, *, name):
    T, D = x.shape
    tm = _pick(T, (512, 256, 128))

    def body(x_ref, g_ref, u_ref):
        u_ref[...] = _rms(x_ref[...], g_ref[...]).astype(BF16)

    return pl.pallas_call(
        body, name=name, grid=(T // tm,),
        in_specs=[_row_spec(tm, D), _full_spec((1, D))],
        out_specs=_row_spec(tm, D),
        out_shape=jax.ShapeDtypeStruct((T, D), BF16),
        compiler_params=_params("parallel"),
    )(x, g)


def _norm_bwd_add(x, g, du, dres, *, name):
    T, D = x.shape
    tm = _pick(T, (512, 256, 128))
    has_res = dres is not None

    def body(*refs):
        if has_res:
            x_ref, g_ref, du_ref, dr_ref, dx_ref, dxb_ref, dg_ref = refs
        else:
            x_ref, g_ref, du_ref, dx_ref, dxb_ref, dg_ref = refs
        _, vjp = jax.vjp(_rms, x_ref[...], g_ref[...])
        dx, dg = vjp(du_ref[...].astype(F32))
        if has_res:
            dx = dx + dr_ref[...]
        dx_ref[...] = dx
        dxb_ref[...] = dx.astype(BF16)

        @pl.when(pl.program_id(0) == 0)
        def _():
            dg_ref[...] = jnp.zeros_like(dg_ref)

        dg_ref[...] += dg

    ins = [x, g, du] + ([dres] if has_res else [])
    in_specs = [_row_spec(tm, D), _full_spec((1, D)), _row_spec(tm, D)] + ([_row_spec(tm, D)] if has_res else [])
    return pl.pallas_call(
        body, name=name, grid=(T // tm,),
        in_specs=in_specs,
        out_specs=[_row_spec(tm, D), _row_spec(tm, D), _full_spec((1, D))],
        out_shape=[jax.ShapeDtypeStruct((T, D), F32), jax.ShapeDtypeStruct((T, D), BF16), jax.ShapeDtypeStruct((1, D), F32)],
        compiler_params=_params("arbitrary"),
    )(*ins)


def _final_loss_bwd(h2, g, target, *, name):
    T, D = h2.shape
    tm = _pick(T, (512, 256, 128))

    def body(h_ref, g_ref, t_ref, dx_ref, dxb_ref, dg_ref, loss_ref):
        y, vjp = jax.vjp(_rms, h_ref[...], g_ref[...])
        err = y - t_ref[...]
        dx, dg = vjp(err * (1.0 / D))
        dx_ref[...] = dx
        dxb_ref[...] = dx.astype(BF16)

        @pl.when(pl.program_id(0) == 0)
        def _():
            dg_ref[...] = jnp.zeros_like(dg_ref)
            loss_ref[...] = jnp.zeros_like(loss_ref)

        dg_ref[...] += dg
        loss_ref[...] += (0.5 / D) * jnp.sum(jnp.sum(err * err, axis=1, keepdims=True), axis=0, keepdims=True)

    return pl.pallas_call(
        body, name=name, grid=(T // tm,),
        in_specs=[_row_spec(tm, D), _full_spec((1, D)), _row_spec(tm, D)],
        out_specs=[_row_spec(tm, D), _row_spec(tm, D), _full_spec((1, D)), _full_spec((1, 1))],
        out_shape=[jax.ShapeDtypeStruct((T, D), F32), jax.ShapeDtypeStruct((T, D), BF16), jax.ShapeDtypeStruct((1, D), F32), jax.ShapeDtypeStruct((1, 1), F32)],
        compiler_params=_params("arbitrary"),
    )(h2, g, target)


def _merge_fn(gates, a, b):
    ga = gates[:, :D_MODEL].astype(F32)
    gb = gates[:, D_MODEL:].astype(F32)
    return _sigmoid(ga) * a.astype(F32) + _sigmoid(gb) * b.astype(F32)


def _gates_spec(tm):
    return pl.BlockSpec((tm, W_GATES), lambda i: (i, O_GATES // W_GATES))


def _merge_fwd(z, a, b, *, name):
    T = a.shape[0]
    tm = _pick(T, (512, 256, 128))

    def body(g_ref, a_ref, b_ref, o_ref):
        o_ref[...] = _merge_fn(g_ref[...], a_ref[...], b_ref[...]).astype(BF16)

    return pl.pallas_call(
        body, name=name, grid=(T // tm,),
        in_specs=[_gates_spec(tm), _row_spec(tm, D_MODEL), _row_spec(tm, D_MODEL)],
        out_specs=_row_spec(tm, D_MODEL),
        out_shape=jax.ShapeDtypeStruct((T, D_MODEL), BF16),
        compiler_params=_params("parallel"),
    )(z, a, b)


def _merge_bwd(z, a, b, dmerged, dz, *, name):
    T = a.shape[0]
    tm = _pick(T, (512, 256, 128))

    def body(g_ref, a_ref, b_ref, dm_ref, dz_in, dg_ref, da_ref, db_ref):
        g = g_ref[...].astype(F32)
        dm = dm_ref[...].astype(F32)
        sa = _sigmoid(g[:, :D_MODEL])
        sb = _sigmoid(g[:, D_MODEL:])
        da_ref[...] = (dm * sa).astype(BF16)
        db_ref[...] = (dm * sb).astype(BF16)
        dg_ref[:, :D_MODEL] = (dm * a_ref[...].astype(F32) * sa * (1.0 - sa)).astype(BF16)
        dg_ref[:, D_MODEL:] = (dm * b_ref[...].astype(F32) * sb * (1.0 - sb)).astype(BF16)

    return pl.pallas_call(
        body, name=name, grid=(T // tm,),
        in_specs=[_gates_spec(tm), _row_spec(tm, D_MODEL), _row_spec(tm, D_MODEL), _row_spec(tm, D_MODEL), ANY],
        out_specs=[_gates_spec(tm), _row_spec(tm, D_MODEL), _row_spec(tm, D_MODEL)],
        out_shape=[jax.ShapeDtypeStruct(dz.shape, BF16), jax.ShapeDtypeStruct((T, D_MODEL), BF16), jax.ShapeDtypeStruct((T, D_MODEL), BF16)],
        input_output_aliases={4: 0},
        compiler_params=_params("parallel"),
    )(z, a, b, dmerged, dz)


CONV_TC = 256


def _shift_down(x, n, rows):
    return jnp.where(rows >= n, pltpu.roll(x, n, 0), 0.0)


def _shift_up(x, n, rows, S):
    return jnp.where(rows < S - n, pltpu.roll(x, S - n, 0), 0.0)


def _conv_act_fwd(gu, conv_w, conv_b, *, name):
    B, S, _ = gu.shape
    tc = CONV_TC
    nc = D_FF // tc

    def body(g_ref, up_ref, w_ref, b_ref, o_ref):
        g = g_ref[...].astype(F32)
        rows = lax.broadcasted_iota(jnp.int32, g.shape, 0)
        w = w_ref[...]
        a = w[2:3] * g + w[1:2] * _shift_down(g, 1, rows) + w[0:1] * _shift_down(g, 2, rows) + b_ref[...]
        o_ref[...] = (_silu(a) * up_ref[...].astype(F32)).astype(BF16)

    return pl.pallas_call(
        body, name=name, grid=(B, nc),
        in_specs=[pl.BlockSpec((None, S, tc), lambda b, j: (b, 0, j)),
                  pl.BlockSpec((None, S, tc), lambda b, j: (b, 0, j + nc)),
                  pl.BlockSpec((3, tc), lambda b, j: (0, j)),
                  pl.BlockSpec((1, tc), lambda b, j: (0, j))],
        out_specs=pl.BlockSpec((None, S, tc), lambda b, j: (b, 0, j)),
        out_shape=jax.ShapeDtypeStruct((B, S, D_FF), BF16),
        compiler_params=_params("parallel", "parallel"),
    )(gu, gu, conv_w, conv_b)


def _conv_act_bwd(gu, conv_w, conv_b, dact, *, name):
    B, S, _ = gu.shape
    tc = CONV_TC
    nc = D_FF // tc

    def body(g_ref, up_ref, w_ref, b_ref, da_ref, dg_ref, dup_ref, dw_ref, db_ref):
        g = g_ref[...].astype(F32)
        up = up_ref[...].astype(F32)
        dact = da_ref[...].astype(F32)
        rows = lax.broadcasted_iota(jnp.int32, g.shape, 0)
        w = w_ref[...]
        g1 = _shift_down(g, 1, rows)
        g2 = _shift_down(g, 2, rows)
        a = w[2:3] * g + w[1:2] * g1 + w[0:1] * g2 + b_ref[...]
        sg = _sigmoid(a)
        dup_ref[...] = (dact * a * sg).astype(BF16)
        da = dact * up * sg * (1.0 + a * (1.0 - sg))
        dg = w[2:3] * da + w[1:2] * _shift_up(da, 1, rows, S) + w[0:1] * _shift_up(da, 2, rows, S)
        dg_ref[...] = dg.astype(BF16)

        @pl.when(pl.program_id(1) == 0)
        def _():
            dw_ref[...] = jnp.zeros_like(dw_ref)
            db_ref[...] = jnp.zeros_like(db_ref)

        dw_ref[0:1, :] += jnp.sum(da * g2, axis=0, keepdims=True)
        dw_ref[1:2, :] += jnp.sum(da * g1, axis=0, keepdims=True)
        dw_ref[2:3, :] += jnp.sum(da * g, axis=0, keepdims=True)
        db_ref[...] += jnp.sum(da, axis=0, keepdims=True)

    col = lambda j, b: (b, 0, j)
    return pl.pallas_call(
        body, name=name, grid=(nc, B),
        in_specs=[pl.BlockSpec((None, S, tc), col),
                  pl.BlockSpec((None, S, tc), lambda j, b: (b, 0, j + nc)),
                  pl.BlockSpec((3, tc), lambda j, b: (0, j)),
                  pl.BlockSpec((1, tc), lambda j, b: (0, j)),
                  pl.BlockSpec((None, S, tc), col)],
        out_specs=[pl.BlockSpec((None, S, tc), col), pl.BlockSpec((None, S, tc), col),
                   pl.BlockSpec((3, tc), lambda j, b: (0, j)), pl.BlockSpec((1, tc), lambda j, b: (0, j))],
        out_shape=[jax.ShapeDtypeStruct((B, S, D_FF), BF16), jax.ShapeDtypeStruct((B, S, D_FF), BF16),
                   jax.ShapeDtypeStruct((3, D_FF), F32), jax.ShapeDtypeStruct((1, D_FF), F32)],
        compiler_params=_params("parallel", "arbitrary"),
    )(gu, gu, conv_w, conv_b, dact)


HGRN_CPB = 4
HF = HGRN_HEADS * HGRN_DK


def _tri(n, upper=False):
    r = lax.broadcasted_iota(jnp.int32, (n, n), 0)
    c = lax.broadcasted_iota(jnp.int32, (n, n), 1)
    return (c >= r) if upper else (r >= c)


def _hs(h):
    return slice(h * HGRN_DK, (h + 1) * HGRN_DK)


def _cumsum_rows(tri_b, x):
    hi = x.astype(BF16)
    lo = (x - hi.astype(F32)).astype(BF16)
    return _nn(tri_b, hi) + _nn(tri_b, lo)


def _hgrn_pre(q, fz, lb, tril_b):
    qf = _silu(q)
    sg = _sigmoid(fz)
    f = lb + (1.0 - lb) * sg
    k = 1.0 - f
    b = _cumsum_rows(tril_b, jnp.log2(f))
    bref = b[CHUNK // 2:CHUNK // 2 + 1, :]
    blast = b[CHUNK - 1:CHUNK, :]
    e1 = jnp.exp2(b - bref)
    e2 = jnp.exp2(bref - b)
    e3 = e1 * jnp.exp2(bref)
    e4 = e2 * jnp.exp2(blast - bref)
    dec = jnp.exp2(blast)
    return sg, f, (e1, e2, e3, e4), qf * e1, k * e2, qf * e3, k * e4, dec


def _hgrn_fwd(zh, lb, gn, *, name):
    B, S, _ = zh.shape
    cpb = HGRN_CPB
    ts = cpb * CHUNK
    nblk = S // ts

    def body(z_ref, lb_ref, gn_ref, o_ref, st_ref, state):
        @pl.when(pl.program_id(1) == 0)
        def _():
            state[...] = jnp.zeros_like(state)

        H = HGRN_HEADS
        causal = _tri(CHUNK)
        tril_b = causal.astype(BF16)
        lb = lb_ref[...]
        for c in range(cpb):
            rows = slice(c * CHUNK, (c + 1) * CHUNK)
            q = z_ref[rows, 0:HF].astype(F32)
            fz = z_ref[rows, HF:2 * HF].astype(F32)
            v = z_ref[rows, 2 * HF:3 * HF]
            hg = z_ref[rows, 3 * HF:4 * HF].astype(F32)
            _, _, _, q_in, k_in, q_out, k_st, dec = _hgrn_pre(q, fz, lb, tril_b)
            q_in, k_in, q_out, k_st = (t.astype(BF16) for t in (q_in, k_in, q_out, k_st))
            a = [jnp.where(causal, _nt(q_in[:, _hs(h)], k_in[:, _hs(h)]), 0.0).astype(BF16) for h in range(H)]
            st = [state[h] for h in range(H)]
            for h in range(H):
                st_ref[c, h] = st[h]
            o = [_nn(a[h], v[:, _hs(h)]) + _nt(q_out[:, _hs(h)], st[h].astype(BF16)) for h in range(H)]
            for h in range(H):
                state[h] = st[h] * dec[:, _hs(h)] + _tn(v[:, _hs(h)], k_st[:, _hs(h)])
            gate = _silu(hg)
            for h in range(H):
                o_ref[rows, _hs(h)] = (_rms(o[h], gn_ref[...]) * gate[:, _hs(h)]).astype(BF16)

    return pl.pallas_call(
        body, name=name, grid=(B, nblk),
        in_specs=[pl.BlockSpec((None, ts, 4 * HF), lambda b, s: (b, s, 0)),
                  pl.BlockSpec((1, HF), lambda b, s: (0, 0)),
                  pl.BlockSpec((1, HGRN_DK), lambda b, s: (0, 0))],
        out_specs=[pl.BlockSpec((None, ts, HF), lambda b, s: (b, s, 0)),
                   pl.BlockSpec((None, cpb, HGRN_HEADS, HGRN_DK, HGRN_DK), lambda b, s: (b, s, 0, 0, 0))],
        out_shape=[jax.ShapeDtypeStruct((B, S, HF), BF16),
                   jax.ShapeDtypeStruct((B, S // CHUNK, HGRN_HEADS, HGRN_DK, HGRN_DK), F32)],
        scratch_shapes=[pltpu.VMEM((HGRN_HEADS, HGRN_DK, HGRN_DK), F32)],
        compiler_params=_params("arbitrary", "arbitrary"),
    )(zh, lb, gn)


def _hgrn_bwd(zh, lb, gn, states, doa, dz, *, name):
    B, S, _ = zh.shape
    cpb = HGRN_CPB
    ts = cpb * CHUNK
    nblk = S // ts
    rev = lambda b, s: (b, nblk - 1 - s, 0)

    def body(z_ref, lb_ref, gn_ref, st_ref, do_ref, dz_in, dz_ref, dlb_ref, dgn_ref, dstate):
        @pl.when(pl.program_id(1) == 0)
        def _():
            dstate[...] = jnp.zeros_like(dstate)

        @pl.when((pl.program_id(0) == 0) & (pl.program_id(1) == 0))
        def _():
            dlb_ref[...] = jnp.zeros_like(dlb_ref)
            dgn_ref[...] = jnp.zeros_like(dgn_ref)

        H = HGRN_HEADS
        cat = lambda xs: jnp.concatenate(xs, axis=1)
        causal = _tri(CHUNK)
        tril_b = causal.astype(BF16)
        triu_b = _tri(CHUNK, upper=True).astype(BF16)
        rowid = lax.broadcasted_iota(jnp.int32, (CHUNK, HF), 0)
        lb = lb_ref[...]
        gn = gn_ref[...]
        for c in reversed(range(cpb)):
            rows = slice(c * CHUNK, (c + 1) * CHUNK)
            q = z_ref[rows, 0:HF].astype(F32)
            fz = z_ref[rows, HF:2 * HF].astype(F32)
            v = z_ref[rows, 2 * HF:3 * HF]
            hg = z_ref[rows, 3 * HF:4 * HF].astype(F32)
            sg, f, (e1, e2, e3, e4), q_in, k_in, q_out, k_st, dec = _hgrn_pre(q, fz, lb, tril_b)
            q_in_b, k_in_b, q_out_b, k_st_b = (t.astype(BF16) for t in (q_in, k_in, q_out, k_st))
            a_b = [jnp.where(causal, _nt(q_in_b[:, _hs(h)], k_in_b[:, _hs(h)]), 0.0).astype(BF16) for h in range(H)]
            st = [st_ref[c, h] for h in range(H)]
            st_b = [t.astype(BF16) for t in st]
            o = [_nn(a_b[h], v[:, _hs(h)]) + _nt(q_out_b[:, _hs(h)], st_b[h]) for h in range(H)]
            dout = do_ref[rows, :].astype(F32)
            shg = _sigmoid(hg)
            gate = hg * shg
            do_l, dgn_acc = [], jnp.zeros_like(gn)
            for h in range(H):
                _, norm_vjp = jax.vjp(_rms, o[h], gn)
                d_o, d_gn = norm_vjp(dout[:, _hs(h)] * gate[:, _hs(h)])
                do_l.append(d_o)
                dgn_acc = dgn_acc + d_gn
            dgn_ref[...] += dgn_acc
            on = cat([_rms(o[h], gn) for h in range(H)])
            dhg = dout * on * shg * (1.0 + hg * (1.0 - shg))
            do_b = [t.astype(BF16) for t in do_l]
            dst = [dstate[h] for h in range(H)]
            dst_b = [t.astype(BF16) for t in dst]
            da_b = [jnp.where(causal, _nt(do_b[h], v[:, _hs(h)]), 0.0).astype(BF16) for h in range(H)]
            dv = cat([_tn(a_b[h], do_b[h]) + _nt(k_st_b[:, _hs(h)], dst_b[h]) for h in range(H)])
            dq_in = cat([_nn(da_b[h], k_in_b[:, _hs(h)]) for h in range(H)])
            dk_in = cat([_tn(da_b[h], q_in_b[:, _hs(h)]) for h in range(H)])
            dq_out = cat([_nn(do_b[h], st_b[h]) for h in range(H)])
            dk_st = cat([_nn(v[:, _hs(h)], dst_b[h]) for h in range(H)])
            ddec = cat([jnp.sum(st[h] * dst[h], axis=0, keepdims=True) for h in range(H)])
            for h in range(H):
                dstate[h] = dst[h] * dec[:, _hs(h)] + _tn(do_b[h], q_out_b[:, _hs(h)])
            t_qin = dq_in * q_in
            t_kin = dk_in * k_in
            t_kst = dk_st * k_st
            db = t_qin - t_kin + dq_out * q_out - t_kst
            dbref = jnp.sum(t_kin - t_qin, axis=0, keepdims=True)
            dblast = jnp.sum(t_kst, axis=0, keepdims=True) + ddec * dec
            db = db + jnp.where(rowid == CHUNK // 2, dbref, 0.0) + jnp.where(rowid == CHUNK - 1, dblast, 0.0)
            dlogf = _cumsum_rows(triu_b, db)
            dqf = dq_in * e1 + dq_out * e3
            dk = dk_in * e2 + dk_st * e4
            df = dlogf / f - dk
            dfz = df * (1.0 - lb) * sg * (1.0 - sg)
            dlb_ref[...] += jnp.sum(df * (1.0 - sg), axis=0, keepdims=True)
            sq = _sigmoid(q)
            dq = dqf * sq * (1.0 + q * (1.0 - sq))
            dz_ref[rows, 0:HF] = dq.astype(BF16)
            dz_ref[rows, HF:2 * HF] = dfz.astype(BF16)
            dz_ref[rows, 2 * HF:3 * HF] = dv.astype(BF16)
            dz_ref[rows, 3 * HF:4 * HF] = dhg.astype(BF16)

    return pl.pallas_call(
        body, name=name, grid=(B, nblk),
        in_specs=[pl.BlockSpec((None, ts, 4 * HF), rev),
                  pl.BlockSpec((1, HF), lambda b, s: (0, 0)),
                  pl.BlockSpec((1, HGRN_DK), lambda b, s: (0, 0)),
                  pl.BlockSpec((None, cpb, HGRN_HEADS, HGRN_DK, HGRN_DK), lambda b, s: (b, nblk - 1 - s, 0, 0, 0)),
                  pl.BlockSpec((None, ts, HF), rev),
                  ANY],
        out_specs=[pl.BlockSpec((None, ts, 4 * HF), rev),
                   pl.BlockSpec((1, HF), lambda b, s: (0, 0)),
                   pl.BlockSpec((1, HGRN_DK), lambda b, s: (0, 0))],
        out_shape=[jax.ShapeDtypeStruct(dz.shape, BF16),
                   jax.ShapeDtypeStruct((1, HF), F32),
                   jax.ShapeDtypeStruct((1, HGRN_DK), F32)],
        input_output_aliases={5: 0},
        scratch_shapes=[pltpu.VMEM((HGRN_HEADS, HGRN_DK, HGRN_DK), F32)],
        compiler_params=_params("arbitrary", "arbitrary"),
    )(zh, lb, gn, states, doa, dz)


KV_W = ATT_KV_HEADS * ATT_HD
ATT_SCALE = ATT_HD ** -0.5


def _rope(x, cos, sin, inverse=False):
    half = ROPE_DIM // 2
    outs = []
    for p in range(x.shape[1] // 128):
        xp = x[:, p * 128:(p + 1) * 128]
        lane = lax.broadcasted_iota(jnp.int32, xp.shape, 1) % ATT_HD
        sw = jnp.where(lane < half, pltpu.roll(xp, 128 - half, 1), pltpu.roll(xp, half, 1))
        outs.append(xp * cos - sw * sin if inverse else xp * cos + sw * sin)
    return outs[0] if len(outs) == 1 else jnp.concatenate(outs, axis=1)


PAIRS_PER_KV = ATT_GROUP // 2


def _swap_halves(x):
    return pltpu.roll(x, ATT_HD, 1)


def _kv_padded(t, low):
    sw = _swap_halves(t)
    zero = jnp.zeros_like(t)
    out = []
    for g in range(ATT_KV_HEADS):
        in_low, in_high = (t, sw) if g == 0 else (sw, t)
        out.append((jnp.where(low, in_low, zero).astype(BF16), jnp.where(low, zero, in_high).astype(BF16)))
    return out


def _swa_mask(first_block):
    qi = lax.broadcasted_iota(jnp.int32, (WINDOW, 2 * WINDOW), 0)
    mi = lax.broadcasted_iota(jnp.int32, (WINDOW, 2 * WINDOW), 1)
    band = (mi > qi) & (mi <= qi + WINDOW)
    return band & (jnp.logical_not(first_block) | (mi >= WINDOW))


def _swa_specs(nb):
    cur = lambda b, i: (b, i, 0)
    prev = lambda b, i: (b, jnp.maximum(i - 1, 0), 0)
    return cur, prev


def _swa_z_specs():
    q = pl.BlockSpec((None, WINDOW, W_AQ), lambda b, i: (b, i, O_AQ // W_AQ))
    kv_prev = pl.BlockSpec((None, WINDOW, W_AKV), lambda b, i: (b, jnp.maximum(i - 1, 0), O_AKV // W_AKV))
    kv_cur = pl.BlockSpec((None, WINDOW, W_AKV), lambda b, i: (b, i, O_AKV // W_AKV))
    return q, kv_prev, kv_cur


def _swa_fwd(z, cos, sin, sinks, *, name):
    B, S, _ = z.shape
    nb = S // WINDOW
    cur, prev = _swa_specs(nb)

    def body(q_ref, kvp_ref, kvc_ref, cp_ref, sp_ref, cc_ref, sc_ref, sink_ref, o_ref, lse_ref):
        cos_c, sin_c = cc_ref[...], sc_ref[...]
        q = (_rope(q_ref[...].astype(F32), cos_c, sin_c) * ATT_SCALE).astype(BF16)
        k = jnp.concatenate([_rope(kvp_ref[:, :KV_W].astype(F32), cp_ref[...], sp_ref[...]),
                             _rope(kvc_ref[:, :KV_W].astype(F32), cos_c, sin_c)], axis=0)
        v = jnp.concatenate([kvp_ref[:, KV_W:], kvc_ref[:, KV_W:]], axis=0).astype(F32)
        low = lax.broadcasted_iota(jnp.int32, k.shape, 1) < ATT_HD
        kpad = _kv_padded(k, low)
        vpad = _kv_padded(v, low)
        mask = _swa_mask(pl.program_id(1) == 0)
        lses = []
        for g in range(ATT_KV_HEADS):
            pairs = range(g * PAIRS_PER_KV, (g + 1) * PAIRS_PER_KV)
            keys = [(p, e) for p in pairs for e in (0, 1)]
            qp = {p: q[:, p * 128:(p + 1) * 128] for p in pairs}
            s = {pe: jnp.where(mask, _nt(qp[pe[0]], kpad[g][pe[1]]), NEG_INF) for pe in keys}
            pr = {}
            for pe in keys:
                sink = sink_ref[0, 2 * pe[0] + pe[1]]
                m = jnp.maximum(jnp.max(s[pe], axis=1, keepdims=True), sink)
                ex = jnp.exp(s[pe] - m)
                den = jnp.sum(ex, axis=1, keepdims=True) + jnp.exp(sink - m)
                pr[pe] = (ex * (1.0 / den)).astype(BF16)
                lses.append(m + jnp.log(den))
            for p in pairs:
                o_ref[:, p * 128:(p + 1) * 128] = (_nn(pr[p, 0], vpad[g][0]) + _nn(pr[p, 1], vpad[g][1])).astype(BF16)
        lse_ref[...] = jnp.concatenate(lses, axis=1)

    tab = lambda im: pl.BlockSpec((None, WINDOW, 128), im)
    return pl.pallas_call(
        body, name=name, grid=(B, nb),
        in_specs=[*_swa_z_specs(),
                  tab(prev), tab(prev), tab(cur), tab(cur),
                  pl.BlockSpec(memory_space=pltpu.SMEM)],
        out_specs=[pl.BlockSpec((None, WINDOW, D_MODEL), cur), pl.BlockSpec((None, WINDOW, ATT_HEADS), cur)],
        out_shape=[jax.ShapeDtypeStruct((B, S, D_MODEL), BF16), jax.ShapeDtypeStruct((B, S, ATT_HEADS), F32)],
        compiler_params=_params("parallel", "parallel"),
    )(z, z, z, cos, sin, cos, sin, sinks)


def _swa_bwd(z, cos, sin, sinks, lse, dob, dz, *, name):
    B, S, _ = z.shape
    nb = S // WINDOW
    cur, prev = _swa_specs(nb)

    def body(q_ref, kvp_ref, kvc_ref, cp_ref, sp_ref, cc_ref, sc_ref, sink_ref, lse_ref, do_ref, dz_in,
             dq_ref, dkc_ref, dkp_ref, dsink_ref):
        @pl.when((pl.program_id(0) == 0) & (pl.program_id(1) == 0))
        def _():
            dsink_ref[...] = jnp.zeros_like(dsink_ref)

        cos_c, sin_c, cos_p, sin_p = cc_ref[...], sc_ref[...], cp_ref[...], sp_ref[...]
        q = (_rope(q_ref[...].astype(F32), cos_c, sin_c) * ATT_SCALE).astype(BF16)
        k = jnp.concatenate([_rope(kvp_ref[:, :KV_W].astype(F32), cos_p, sin_p),
                             _rope(kvc_ref[:, :KV_W].astype(F32), cos_c, sin_c)], axis=0)
        v = jnp.concatenate([kvp_ref[:, KV_W:], kvc_ref[:, KV_W:]], axis=0).astype(F32)
        low = lax.broadcasted_iota(jnp.int32, k.shape, 1) < ATT_HD
        kpad = _kv_padded(k, low)
        vpad = _kv_padded(v, low)
        mask = _swa_mask(pl.program_id(1) == 0)
        lse = lse_ref[...]
        dq_parts, dk_sum, dv_sum, dsinks = [], [], [], []
        for g in range(ATT_KV_HEADS):
            pairs = range(g * PAIRS_PER_KV, (g + 1) * PAIRS_PER_KV)
            keys = [(p, e) for p in pairs for e in (0, 1)]
            qp = {p: q[:, p * 128:(p + 1) * 128] for p in pairs}
            dop = {p: do_ref[:, p * 128:(p + 1) * 128] for p in pairs}
            s = {pe: jnp.where(mask, _nt(qp[pe[0]], kpad[g][pe[1]]), NEG_INF) for pe in keys}
            dp = {pe: _nt(dop[pe[0]], vpad[g][pe[1]]) for pe in keys}
            pr, ds = {}, {}
            for pe in keys:
                h = 2 * pe[0] + pe[1]
                lse_h = lse[:, h:h + 1]
                pf = jnp.exp(s[pe] - lse_h)
                delta = jnp.sum(pf * dp[pe], axis=1, keepdims=True)
                ds[pe] = (pf * (dp[pe] - delta)).astype(BF16)
                pr[pe] = pf.astype(BF16)
                p_sink = jnp.exp(sink_ref[0, h] - lse_h)
                dsinks.append(-jnp.sum(p_sink * delta, axis=0, keepdims=True))
            for p in pairs:
                dq_parts.append((_nn(ds[p, 0], kpad[g][0]) + _nn(ds[p, 1], kpad[g][1])) * ATT_SCALE)
            x = [sum(_tn(ds[p, e], qp[p]) for p in pairs) for e in (0, 1)]
            y = [sum(_tn(pr[p, e], dop[p]) for p in pairs) for e in (0, 1)]
            zk = jnp.where(low, x[0], x[1])
            zv = jnp.where(low, y[0], y[1])
            dk_sum.append(zk + _swap_halves(zk))
            dv_sum.append(zv + _swap_halves(zv))
        dq_ref[...] = _rope(jnp.concatenate(dq_parts, axis=1), cos_c, sin_c, inverse=True).astype(BF16)
        dk = jnp.where(low, dk_sum[0], dk_sum[1])
        dv = jnp.where(low, dv_sum[0], dv_sum[1])
        dkp_ref[:, :KV_W] = _rope(dk[:WINDOW], cos_p, sin_p, inverse=True)
        dkp_ref[:, KV_W:] = dv[:WINDOW]
        dkc_ref[:, :KV_W] = _rope(dk[WINDOW:], cos_c, sin_c, inverse=True)
        dkc_ref[:, KV_W:] = dv[WINDOW:]
        dsink_ref[...] += jnp.broadcast_to(jnp.concatenate(dsinks, axis=0), (ATT_HEADS, 128))

    tab = lambda im: pl.BlockSpec((None, WINDOW, 128), im)
    return pl.pallas_call(
        body, name=name, grid=(B, nb),
        in_specs=[*_swa_z_specs(),
                  tab(prev), tab(prev), tab(cur), tab(cur),
                  pl.BlockSpec(memory_space=pltpu.SMEM),
                  pl.BlockSpec((None, WINDOW, ATT_HEADS), cur),
                  pl.BlockSpec((None, WINDOW, D_MODEL), cur),
                  ANY],
        out_specs=[_swa_z_specs()[0],
                   pl.BlockSpec((None, WINDOW, 2 * KV_W), cur), pl.BlockSpec((None, WINDOW, 2 * KV_W), cur),
                   pl.BlockSpec((ATT_HEADS, 128), lambda b, i: (0, 0))],
        out_shape=[jax.ShapeDtypeStruct(dz.shape, BF16),
                   jax.ShapeDtypeStruct((B, S, 2 * KV_W), F32), jax.ShapeDtypeStruct((B, S, 2 * KV_W), F32),
                   jax.ShapeDtypeStruct((ATT_HEADS, 128), F32)],
        input_output_aliases={10: 0},
        compiler_params=_params("arbitrary", "arbitrary"),
    )(z, z, z, cos, sin, cos, sin, sinks, lse, dob, dz)


def _swa_dkv_combine(dkv_cur, dkv_prev, dz, *, name):
    B, S, W = dkv_cur.shape

    def body(c_ref, p_ref, dz_in, o_ref):
        rows = lax.broadcasted_iota(jnp.int32, (S, W), 0)
        o_ref[...] = (c_ref[...] + _shift_up(p_ref[...], WINDOW, rows, S)).astype(BF16)

    seq = pl.BlockSpec((None, S, W), lambda b: (b, 0, 0))
    return pl.pallas_call(
        body, name=name, grid=(B,),
        in_specs=[seq, seq, ANY], out_specs=pl.BlockSpec((None, S, W), lambda b: (b, 0, O_AKV // W_AKV)),
        out_shape=jax.ShapeDtypeStruct(dz.shape, BF16),
        input_output_aliases={2: 0},
        compiler_params=_params("parallel"),
    )(dkv_cur, dkv_prev, dz)


def _rope_tables(positions):
    half = ROPE_DIM // 2
    inv = ROPE_THETA ** (-2.0 * jnp.arange(half, dtype=F32) / ROPE_DIM)
    ang = positions.astype(F32)[..., None] * inv
    c, s = jnp.cos(ang), jnp.sin(ang)
    pad = jnp.zeros(ang.shape[:-1] + (ATT_HD - ROPE_DIM,), F32)
    cos = jnp.concatenate([c, c, pad + 1.0], axis=-1)
    sin = jnp.concatenate([-s, s, pad], axis=-1)
    return jnp.tile(cos, (1, 1, 2)), jnp.tile(sin, (1, 1, 2))


def _lower_bound(lb_logits, *, name):
    def body(l_ref, o_ref):
        l = l_ref[...]
        e = jnp.exp(l - jnp.max(l, axis=0, keepdims=True))
        o_ref[...] = e[0:1] / jnp.sum(e, axis=0, keepdims=True)

    return pl.pallas_call(body, name=name, out_shape=jax.ShapeDtypeStruct((1, lb_logits.shape[1]), F32))(lb_logits)


W_ZH, W_GATES, W_AQ, W_AKV = 4 * HF, 2 * D_MODEL, ATT_HEADS * ATT_HD, 2 * KV_W
O_ZH, O_GATES, O_AQ, O_AKV = 0, W_ZH, W_ZH + W_GATES, W_ZH + W_GATES + W_AQ
W_IN = W_ZH + W_GATES + W_AQ + W_AKV


def _reorder_w_in(w_in_full):
    return jnp.concatenate([w_in_full[:, :W_ZH], w_in_full[:, W_ZH + W_AQ + W_AKV:], w_in_full[:, W_ZH:W_ZH + W_AQ + W_AKV]], axis=1)


def _reference_order_w_in(w):
    return jnp.concatenate([w[:, :W_ZH], w[:, O_AQ:], w[:, O_GATES:O_AQ]], axis=1)


def _local_step(x, positions, target, small, w_in, rest_weights, emit, start_token):
    B, S, D = x.shape
    T = B * S
    x2 = x.reshape(T, D)
    cos, sin = _rope_tables(positions)
    lb = _lower_bound(small["lb_logits"], name="lb_fwd")
    zero = lambda tok: tok[0:1, 0:1]

    u1 = _norm_cast(x2, small["norm1_g"] + zero(start_token), name="norm1")
    z = _matmul(u1, w_in, out_dtype=BF16, name="mm_z", tm=1024, tn=W_IN // 2)
    z3 = z.reshape(B, S, W_IN)
    oa, states = _hgrn_fwd(z3, lb, small["hgrn_norm_g"], name="hgrn_fwd")
    ob, lse = _swa_fwd(z3, cos, sin, small["attn_sinks"], name="swa_fwd")
    oa2 = oa.reshape(T, D)
    ob2 = ob.reshape(T, D)
    W = rest_weights("mix", ob)
    pa = _matmul(oa2, W["w_a"], out_dtype=BF16, name="mm_pa", tm=2048, tn=512)
    pb = _matmul(ob2, W["w_b"], out_dtype=BF16, name="mm_pb", tm=2048, tn=512)
    merged = _merge_fwd(z, pa, pb, name="merge_fwd")
    h = _matmul(merged, W["w_out"], addend=x2, name="mm_h", tm=2048, tn=512)
    u2 = _norm_cast(h, small["norm2_g"], name="norm2")
    W.update(rest_weights("ffn", u2))
    gu = _matmul(u2, W["w_ffn"], out_dtype=BF16, name="mm_gu", tm=2048, tn=512)
    gu3 = gu.reshape(B, S, 2 * D_FF)
    act = _conv_act_fwd(gu3, W["conv_w"], small["conv_b"], name="conv_act_fwd")
    act2 = act.reshape(T, D_FF)
    h2 = _matmul(act2, W["w_down"], addend=h, name="mm_h2", tm=1024, tn=512)

    g = {}
    dh2, dh2b, g["final_g"], loss = _final_loss_bwd(h2, small["final_g"].reshape(1, D), target.reshape(T, D), name="final_loss_bwd")
    dact = _matmul(dh2b, W["w_down"], tb=True, out_dtype=BF16, name="mm_dact", tm=1024, tn=D_FF)
    dw_down = _matmul(act2, dh2b, ta=True, out_dtype=BF16, name="mm_dw_down", tm=D_FF, tn=1024, tk=1024)
    dg_, dup, g["conv_w"], g["conv_b"] = _conv_act_bwd(gu3, W["conv_w"], small["conv_b"], dact.reshape(B, S, D_FF), name="conv_act_bwd")
    dg2 = dg_.reshape(T, D_FF)
    dup2 = dup.reshape(T, D_FF)
    du2 = _matmul(dg2, W["w_ffn"], tb=True, name="mm_du2_g", tm=1024, tn=512, b_koff=0)
    du2 = _matmul(dup2, W["w_ffn"], tb=True, addend=du2, name="mm_du2_u", tm=1024, tn=512, b_koff=1)
    dw_ffn = _matmul(u2, dg2, ta=True, out_dtype=BF16, into=lax.empty((D, 2 * D_FF), BF16), o_noff=0, name="mm_dw_ffn_g", tm=1024, tn=256, tk=8192)
    dw_ffn = _matmul(u2, dup2, ta=True, out_dtype=BF16, into=dw_ffn, o_noff=D_FF // 256, name="mm_dw_ffn_u", tm=1024, tn=256, tk=8192)
    tok = emit("ffn", dict(w_ffn=dw_ffn, w_down=dw_down))
    dh, dhb, g["norm2_g"] = _norm_bwd_add(h, small["norm2_g"] + zero(tok), du2, dh2, name="norm2_bwd")
    dmerged = _matmul(dhb, W["w_out"], tb=True, out_dtype=BF16, name="mm_dmerged", tm=2048, tn=512)
    dw_out = _matmul(merged, dhb, ta=True, out_dtype=BF16, name="mm_dw_out", tm=1024, tn=1024, tk=2048)
    dz, dpa, dpb = _merge_bwd(z, pa, pb, dmerged, lax.empty((T, W_IN), BF16), name="merge_bwd")
    doa =_matmul(dpa, W["w_a"], tb=True, out_dtype=BF16, name="mm_doa", tm=2048, tn=512)
    dw_a = _matmul(oa2, dpa, ta=True, out_dtype=BF16, name="mm_dw_a", tm=1024, tn=1024, tk=2048)
    dob = _matmul(dpb, W["w_b"], tb=True, out_dtype=BF16, name="mm_dob", tm=2048, tn=512)
    dw_b = _matmul(ob2, dpb, ta=True, out_dtype=BF16, name="mm_dw_b", tm=1024, tn=1024, tk=2048)
    tok = emit("mix", dict(w_out=dw_out, w_a=dw_a, w_b=dw_b))
    dz3, dkv_cur, dkv_prev, dsinks = _swa_bwd(z3, cos, sin, small["attn_sinks"] + zero(tok), lse, dob.reshape(B, S, D),
                                              dz.reshape(B, S, W_IN), name="swa_bwd")
    dz3 = _swa_dkv_combine(dkv_cur, dkv_prev, dz3, name="swa_dkv")
    g["attn_sinks"] = dsinks
    dz3, g["lb"], g["hgrn_norm_g"] = _hgrn_bwd(z3, lb, small["hgrn_norm_g"], states, doa.reshape(B, S, D), dz3, name="hgrn_bwd")
    dz = dz3.reshape(T, W_IN)
    dw_in = _matmul(u1, dz, ta=True, out_dtype=BF16, name="mm_dw_in", tm=1024, tn=256, tk=8192)
    tok = emit("in", dict(w_in=dw_in))
    du1 = _matmul(dz, w_in, tb=True, after=tok, name="mm_du1", tm=1024, tn=512)
    dx, _, g["norm1_g"] = _norm_bwd_add(x2, small["norm1_g"], du1, dh, name="norm1_bwd")
    g["lb_logits"] = _lb_bwd(g.pop("lb"), lb, name="lb_bwd")
    return loss, dx.reshape(B, S, D), g


def _my_place():
    return lax.axis_index("x"), lax.axis_index("y"), lax.axis_index("c")


def _all_gather(blk, *, in_vmem, reduce_sum=False, name):
    m, n = blk.shape
    space = pltpu.VMEM if in_vmem else pl.ANY

    def body(x_ref, out_ref, *rest):
        if reduce_sum:
            tot_ref, send_sems, recv_sems, local_sem = rest
        else:
            send_sems, recv_sems, local_sem = rest
        x, y, c = _my_place()
        me, sibling = (x, y, c), (x, y, 1 - c)
        chips = [(1 - x, y), (x, 1 - y), (1 - x, 1 - y)]

        def slot(px, py, pc):
            return out_ref.at[4 * px + 2 * py + pc]

        def copy(k, block, to, src=None):
            return pltpu.make_async_remote_copy(
                src_ref=slot(*block) if src is None else src, dst_ref=slot(*block),
                send_sem=send_sems.at[k], recv_sem=recv_sems.at[k], device_id=to, device_id_type=MESH)

        mine = pltpu.make_async_copy(x_ref, slot(*me), local_sem)
        mine.start()
        first = [copy(0, me, sibling, src=x_ref)]
        first += [copy(1 + j, me, (*chip, c), src=x_ref) for j, chip in enumerate(chips)]
        for cp in first:
            cp.start()
        passed = [copy(4 + j, (*chip, c), sibling) for j, chip in enumerate(chips)]
        for j, chip in enumerate(chips):
            copy(1 + j, (*chip, c), me).wait_recv()
            passed[j].start()
        copy(0, sibling, me).wait_recv()
        for j, chip in enumerate(chips):
            copy(4 + j, (*chip, 1 - c), me).wait_recv()
        for cp in first + passed:
            cp.wait_send()
        mine.wait()
        if reduce_sum:
            acc = out_ref[0]
            for p in range(1, N_DEV):
                acc = acc + out_ref[p]
            tot_ref[...] = acc

    out_shape = [jax.ShapeDtypeStruct((N_DEV, m, n), blk.dtype)]
    out_specs = [pl.BlockSpec(memory_space=space)]
    if reduce_sum:
        out_shape.append(jax.ShapeDtypeStruct((m, n), blk.dtype))
        out_specs.append(pl.BlockSpec(memory_space=pltpu.VMEM))
    res = pl.pallas_call(
        body, name=name,
        out_shape=out_shape,
        in_specs=[pl.BlockSpec(memory_space=space)],
        out_specs=out_specs,
        scratch_shapes=[pltpu.SemaphoreType.DMA((7,)), pltpu.SemaphoreType.DMA((7,)), pltpu.SemaphoreType.DMA],
    )(blk)
    return res if reduce_sum else res[0]


HBM_SPEC = pl.BlockSpec(memory_space=pltpu.HBM)
SEM_SPEC = pl.BlockSpec(memory_space=pltpu.SEMAPHORE)
DATAFLOW_EFFECT = pltpu.SideEffectType.DATAFLOW_SIDE_EFFECTING
N_PEERS = N_DEV - 1


def _peers(x, y, c):
    return [(1 - x if r & 4 else x, 1 - y if r & 2 else y, 1 - c if r & 1 else c) for r in range(1, N_DEV)]


def _exchange_start(srcs, scatter, *, name):
    n = len(srcs)
    lands = [lax.empty(a.shape if scatter else (N_DEV,) + a.shape, a.dtype) for a in srcs]

    def body(*refs):
        src_refs, land_refs = refs[:n], refs[n:2 * n]
        send_sems, recv_sems, token = refs[2 * n], refs[2 * n + 1], refs[-1]
        x, y, c = _my_place()
        me = 4 * x + 2 * y + c
        for i in range(n):
            for r, (tx, ty, tc) in enumerate(_peers(x, y, c)):
                src = src_refs[i].at[4 * tx + 2 * ty + tc] if scatter else src_refs[i]
                pltpu.make_async_remote_copy(
                    src_ref=src, dst_ref=land_refs[i].at[me], send_sem=send_sems.at[N_PEERS * i + r],
                    recv_sem=recv_sems.at[N_PEERS * i + r], device_id=(tx, ty, tc), device_id_type=MESH).start()
        token[...] = jnp.zeros_like(token)

    thru = [pltpu.HBM(a.shape, a.dtype) for a in list(srcs) + lands]
    res = pl.pallas_call(
        body, name=name,
        out_shape=(pltpu.SemaphoreType.DMA((N_PEERS * n,)), pltpu.SemaphoreType.DMA((N_PEERS * n,)), *thru,
                   jax.ShapeDtypeStruct((8, 128), F32)),
        in_specs=[HBM_SPEC] * (2 * n),
        out_specs=(SEM_SPEC, SEM_SPEC, *([HBM_SPEC] * (2 * n)), pl.BlockSpec(memory_space=pltpu.VMEM)),
        input_output_aliases={i: 2 + i for i in range(2 * n)},
        compiler_params=pltpu.CompilerParams(has_side_effects=DATAFLOW_EFFECT),
    )(*[pltpu.with_memory_space_constraint(a, pltpu.HBM) for a in list(srcs) + lands])
    return (res[0], res[1], list(res[2:2 + n]), list(res[2 + n:2 + 2 * n]), scatter), res[-1]


def _exchange_wait(handle, after, *, name):
    send_sems, recv_sems, srcs, lands, scatter = handle
    n = len(srcs)

    def body(*refs):
        src_refs, land_refs = refs[:n], refs[n:2 * n]
        send_sems, recv_sems = refs[2 * n], refs[2 * n + 1]
        x, y, c = _my_place()
        for i in range(n):
            for r in range(N_PEERS):
                src = src_refs[i].at[0] if scatter else src_refs[i]
                cp = pltpu.make_async_remote_copy(
                    src_ref=src, dst_ref=land_refs[i].at[0], send_sem=send_sems.at[N_PEERS * i + r],
                    recv_sem=recv_sems.at[N_PEERS * i + r], device_id=(x, y, c), device_id_type=MESH)
                cp.wait_send()
                cp.wait_recv()

    thru = [pltpu.HBM(a.shape, a.dtype) for a in srcs + lands]
    res = pl.pallas_call(
        body, name=name, out_shape=tuple(thru),
        in_specs=[HBM_SPEC] * (2 * n) + [SEM_SPEC, SEM_SPEC, ANY], out_specs=tuple([HBM_SPEC] * (2 * n)),
        input_output_aliases={i: i for i in range(2 * n)},
        compiler_params=pltpu.CompilerParams(has_side_effects=DATAFLOW_EFFECT),
    )(*srcs, *lands, send_sems, recv_sems, after)
    return list(res[:n]), list(res[n:])


def _with_own(land, own, me):
    return lax.dynamic_update_index_in_dim(land, own, me, 0)


def _adamw_math(w, g, m, v):
    m = ADAM_B1 * m + (1.0 - ADAM_B1) * g
    v = ADAM_B2 * v + (1.0 - ADAM_B2) * (g * g)
    m_hat = m / (1.0 - ADAM_B1 ** ADAM_STEP)
    v_hat = v / (1.0 - ADAM_B2 ** ADAM_STEP)
    delta = -ADAM_LR * (m_hat / (jnp.sqrt(v_hat) + ADAM_EPS) + ADAM_WD * w)
    return delta, m, v


def _adamw_sum(parts, w, m, v, *, name):
    shape = w.shape
    R, n = shape[-2], shape[-1]
    w, m, v = (t.reshape(R, n) for t in (w, m, v))
    tr = _pick(R, (256, 176, 128))

    def body(p_ref, w_ref, m_ref, v_ref, g_ref, d_ref, mo_ref, vo_ref):
        g = p_ref[0].astype(F32)
        for p in range(1, N_DEV):
            g = g + p_ref[p].astype(F32)
        d, mn, vn = _adamw_math(w_ref[...], g, m_ref[...], v_ref[...])
        g_ref[...] = g
        d_ref[...] = d
        mo_ref[...] = mn
        vo_ref[...] = vn

    row = pl.BlockSpec((tr, n), lambda i: (i, 0))
    outs = pl.pallas_call(
        body, name=name, grid=(R // tr,),
        in_specs=[pl.BlockSpec((N_DEV, tr, n), lambda i: (0, i, 0)), row, row, row],
        out_specs=[row, row, row, row],
        out_shape=[jax.ShapeDtypeStruct((R, n), F32)] * 4,
        compiler_params=_params("parallel"),
    )(parts, w, m, v)
    return [t.reshape(shape) for t in outs]


def _adamw_small(g, w, m, v, *, name):
    def body(g_ref, w_ref, m_ref, v_ref, d_ref, mo_ref, vo_ref):
        d, mn, vn = _adamw_math(w_ref[...], g_ref[...], m_ref[...], v_ref[...])
        d_ref[...] = d
        mo_ref[...] = mn
        vo_ref[...] = vn

    return pl.pallas_call(body, name=name, out_shape=[jax.ShapeDtypeStruct(w.shape, F32)] * 3)(g, w, m, v)


def _lb_bwd(dlb, lb, *, name):
    def body(d_ref, lb_ref, o_ref):
        t = d_ref[...] * lb_ref[...] * (1.0 - lb_ref[...])
        o_ref[0:1, :] = t
        o_ref[1:2, :] = -t

    return pl.pallas_call(body, name=name, out_shape=jax.ShapeDtypeStruct((2, lb.shape[1]), F32))(dlb, lb)


LANES = 128
N_IN, N_FFN = 7424, 5632
IN_BLK, FFN_BLK, DOWN_BLK, ROW_BLK = N_IN // N_DEV, N_FFN // N_DEV, D_FF // N_DEV, D_MODEL // N_DEV
CONVW_BLK = D_FF // N_DEV
SMALL_NAMES = ("norm1_g", "lb_logits", "hgrn_norm_g", "attn_sinks", "norm2_g", "conv_b", "final_g")
CONV_BITS_SHAPE = (16, 256)


def _cols_from_blocks(blocks):
    n, rows, width = blocks.shape
    return blocks.transpose(1, 0, 2).reshape(rows, n * width)


def _blocks_from_cols(full):
    rows, cols = full.shape
    return full.reshape(rows, N_DEV, cols // N_DEV).transpose(1, 0, 2)


def _to_rows(vec, rows):
    vec = vec.reshape(-1)
    return jnp.pad(vec, (0, rows * LANES - vec.shape[0])).reshape(rows, LANES)


def kernel(x, positions, norm1_g, w_in, lb_logits, hgrn_norm_g, w_a, attn_sinks, w_b, w_out, norm2_g, w_ffn_in, conv_w, conv_b, w_down, final_g, loss_target, m_norm1_g, m_w_in, m_lb_logits, m_hgrn_norm_g, m_w_a, m_attn_sinks, m_w_b, m_w_out, m_norm2_g, m_w_ffn_in, m_conv_w, m_conv_b, m_w_down, m_final_g, v_norm1_g, v_w_in, v_lb_logits, v_hgrn_norm_g, v_w_a, v_attn_sinks, v_w_b, v_w_out, v_norm2_g, v_w_ffn_in, v_conv_w, v_conv_b, v_w_down, v_final_g):
    xi, yi, ci = _my_place()
    dev = 4 * xi + 2 * yi + ci

    w_in_blocks = _all_gather(w_in[0].astype(BF16), in_vmem=False, name="ag_w_in")
    conv_bits = lax.bitcast_convert_type(conv_w, BF16).reshape(-1)
    conv_bits = jnp.pad(conv_bits, (0, CONV_BITS_SHAPE[0] * CONV_BITS_SHAPE[1] - conv_bits.shape[0])).reshape(CONV_BITS_SHAPE)
    gather_handles = {}
    gather_handles["mix"], tok_mix = _exchange_start([w_a[0].astype(BF16), w_b[0].astype(BF16), w_out[0].astype(BF16)], False, name="ag_mix_start")
    gather_handles["ffn"], tok_ffn = _exchange_start([w_ffn_in[0].astype(BF16), w_down[0].astype(BF16), conv_bits], False, name="ag_ffn_start")
    start_token = tok_mix + tok_ffn

    def rest_weights(group, after):
        own, lands = _exchange_wait(gather_handles[group], after, name="ag_" + group + "_wait")
        full = [_with_own(l, o, dev) for l, o in zip(lands, own)]
        if group == "mix":
            return dict(zip(("w_a", "w_b", "w_out"), [t.reshape(D_MODEL, D_MODEL) for t in full]))
        bits = full[2].reshape(N_DEV, -1)[:, :3 * CONVW_BLK * 2].reshape(N_DEV, 3, CONVW_BLK, 2)
        return dict(w_ffn=_cols_from_blocks(full[0]), w_down=full[1].reshape(D_FF, D_MODEL),
                    conv_w=_cols_from_blocks(lax.bitcast_convert_type(bits, F32)))

    handles = {}

    def emit(group, gr):
        if group == "ffn":
            srcs = [_blocks_from_cols(gr["w_ffn"]), gr["w_down"].reshape(N_DEV, DOWN_BLK, D_MODEL)]
        elif group == "mix":
            srcs = [gr[n].reshape(N_DEV, ROW_BLK, D_MODEL) for n in ("w_out", "w_a", "w_b")]
        else:
            srcs = [_blocks_from_cols(_reference_order_w_in(gr["w_in"]))]
        handles[group], token = _exchange_start(srcs, True, name="rs_" + group + "_start")
        return token

    small = dict(norm1_g=norm1_g, lb_logits=lb_logits, hgrn_norm_g=hgrn_norm_g, attn_sinks=attn_sinks, norm2_g=norm2_g,
                 conv_b=conv_b, final_g=final_g)
    w_in_full = _reorder_w_in(_cols_from_blocks(w_in_blocks))
    loss, grad_x, g = _local_step(x, positions, loss_target, small, w_in_full, rest_weights, emit, start_token)

    def parts_of(group, after):
        srcs, lands = _exchange_wait(handles[group], after, name="rs_" + group + "_wait")
        return [_with_own(l, lax.dynamic_index_in_dim(s, dev, 0, keepdims=False), dev) for s, l in zip(srcs, lands)]

    p_ffn, p_down = parts_of("ffn", grad_x)
    p_out, p_a, p_b = parts_of("mix", grad_x)
    (p_in,) = parts_of("in", grad_x)
    big = dict(
        w_in=_adamw_sum(p_in, w_in, m_w_in, v_w_in, name="adamw_w_in"),
        w_a=_adamw_sum(p_a, w_a, m_w_a, v_w_a, name="adamw_w_a"),
        w_b=_adamw_sum(p_b, w_b, m_w_b, v_w_b, name="adamw_w_b"),
        w_out=_adamw_sum(p_out, w_out, m_w_out, v_w_out, name="adamw_w_out"),
        w_ffn_in=_adamw_sum(p_ffn, w_ffn_in, m_w_ffn_in, v_w_ffn_in, name="adamw_w_ffn_in"),
        w_down=_adamw_sum(p_down, w_down, m_w_down, v_w_down, name="adamw_w_down"),
    )

    sm_g = dict(norm1_g=g["norm1_g"], lb_logits=g["lb_logits"], hgrn_norm_g=g["hgrn_norm_g"], attn_sinks=g["attn_sinks"][:, 0],
                norm2_g=g["norm2_g"], conv_b=g["conv_b"], final_g=g["final_g"])
    vec = jnp.concatenate([sm_g[n].reshape(-1) for n in SMALL_NAMES] + [g["conv_w"].reshape(-1), loss.reshape(-1)])
    sm_rows = 136
    _, total = _all_gather(_to_rows(vec, sm_rows), in_vmem=True, reduce_sum=True, name="ar_small")
    total = total.reshape(-1)
    sm_w = dict(norm1_g=norm1_g, lb_logits=lb_logits, hgrn_norm_g=hgrn_norm_g, attn_sinks=attn_sinks, norm2_g=norm2_g,
                conv_b=conv_b, final_g=final_g)
    sm_m = dict(norm1_g=m_norm1_g, lb_logits=m_lb_logits, hgrn_norm_g=m_hgrn_norm_g, attn_sinks=m_attn_sinks, norm2_g=m_norm2_g,
                conv_b=m_conv_b, final_g=m_final_g)
    sm_v = dict(norm1_g=v_norm1_g, lb_logits=v_lb_logits, hgrn_norm_g=v_hgrn_norm_g, attn_sinks=v_attn_sinks, norm2_g=v_norm2_g,
                conv_b=v_conv_b, final_g=v_final_g)
    sizes = [sm_w[n].size for n in SMALL_NAMES]
    n_rep = sum(sizes)
    g_conv_full = total[n_rep:n_rep + 3 * D_FF].reshape(3, D_FF)
    g_conv = lax.dynamic_slice_in_dim(g_conv_full, dev * CONVW_BLK, CONVW_BLK, axis=1)
    loss_total = total[n_rep + 3 * D_FF]
    ad_rows = 72
    pack_small = lambda d, cw: _to_rows(jnp.concatenate([d[n].reshape(-1) for n in SMALL_NAMES] + [cw.reshape(-1)]), ad_rows)
    g_small = _to_rows(jnp.concatenate([total[:n_rep], g_conv.reshape(-1)]), ad_rows)
    d_s, m_s, v_s = _adamw_small(g_small, pack_small(sm_w, conv_w), pack_small(sm_m, m_conv_w), pack_small(sm_v, v_conv_w), name="adamw_small")

    def unpack_small(t):
        t = t.reshape(-1)
        out, off = {}, 0
        for n, s in zip(SMALL_NAMES, sizes):
            out[n] = t[off:off + s].reshape(sm_w[n].shape)
            off += s
        out["conv_w"] = t[off:off + 3 * CONVW_BLK].reshape(1, 3, CONVW_BLK)
        return out

    names = ("norm1_g", "w_in", "lb_logits", "hgrn_norm_g", "w_a", "attn_sinks", "w_b", "w_out", "norm2_g", "w_ffn_in", "conv_w", "conv_b", "w_down", "final_g")
    outs = [loss_total.reshape(()), grad_x]
    for kind, s_vec in enumerate((g_small, d_s, m_s, v_s)):
        s_un = unpack_small(s_vec)
        outs += [big[n][kind] if n in big else s_un[n] for n in names]
    return tuple(outs)
```

```python
import functools

import jax
import jax.numpy as jnp
from jax import lax
from jax.experimental import pallas as pl
from jax.experimental.pallas import tpu as pltpu

F32 = jnp.float32
BF16 = jnp.bfloat16

D_MODEL = 1024
HGRN_HEADS = 8
HGRN_DK = 128
CHUNK = 64
ATT_HEADS = 16
ATT_KV_HEADS = 2
ATT_HD = 64
ATT_GROUP = ATT_HEADS // ATT_KV_HEADS
WINDOW = 128
ROPE_DIM = ATT_HD // 4
ROPE_THETA = 500000.0
D_FF = 2816
EPS = 1e-6
NEG_INF = -1e30
N_DEV = 8

ADAM_LR = 0.001
ADAM_B1 = 0.9
ADAM_B2 = 0.999
ADAM_EPS = 1e-08
ADAM_WD = 0.01
ADAM_STEP = 10

MESH = pl.DeviceIdType.MESH
ANY = pl.BlockSpec(memory_space=pl.ANY)


def _pick(n, cands):
    for c in cands:
        if n % c == 0:
            return c
    return n


def _sigmoid(x):
    return 0.5 * jnp.tanh(0.5 * x) + 0.5


def _silu(x):
    hx = 0.5 * x
    return hx * jnp.tanh(hx) + hx


def _rms(x, g):
    return x * lax.rsqrt(jnp.mean(x * x, axis=-1, keepdims=True) + EPS) * g


def _dot(a, b, dims):
    return lax.dot_general(a, b, (dims, ((), ())), preferred_element_type=F32)


def _nn(a, b):
    return _dot(a, b, ((1,), (0,)))


def _nt(a, b):
    return _dot(a, b, ((1,), (1,)))


def _tn(a, b):
    return _dot(a, b, ((0,), (0,)))


def _params(*sem):
    return pltpu.CompilerParams(dimension_semantics=sem, vmem_limit_bytes=56 * 1024 * 1024)


def _matmul(a, b, *, ta=False, tb=False, out_dtype=F32, addend=None, after=None, into=None, o_noff=0, name, tm, tn, tk=None,
            n_extent=None, b_koff=0, b_noff=0):
    M, K = (a.shape[1], a.shape[0]) if ta else a.shape
    N = n_extent or (b.shape[0] if tb else b.shape[1])
    tm, tn, tk = min(tm, M), min(tn, N), min(tk or K, K)
    assert M % tm == 0 and N % tn == 0 and K % tk == 0, (name, M, N, K, tm, tn, tk)
    nk = K // tk
    use_scratch = nk > 1 and out_dtype != F32
    grid = (M // tm, N // tn, nk)
    a_spec = pl.BlockSpec((tk, tm), lambda i, j, k: (k, i)) if ta else pl.BlockSpec((tm, tk), lambda i, j, k: (i, k))
    b_spec = pl.BlockSpec((tn, tk), lambda i, j, k: (j + b_noff, k + b_koff)) if tb else pl.BlockSpec((tk, tn), lambda i, j, k: (k + b_koff, j + b_noff))
    o_spec = pl.BlockSpec((tm, tn), lambda i, j, k: (i, j))
    dims = ((0 if ta else 1,), (1 if tb else 0,))
    has_add = addend is not None

    n_in = 2 + has_add + (after is not None) + (into is not None)

    def body(*refs):
        a_ref, b_ref = refs[:2]
        c_ref = refs[2] if has_add else None
        o_ref = refs[n_in]
        part = _dot(a_ref[...], b_ref[...], dims)
        if nk == 1:
            if has_add:
                part = part + c_ref[...].astype(F32)
            o_ref[...] = part.astype(out_dtype)
        else:
            acc_ref = refs[-1] if use_scratch else o_ref
            k = pl.program_id(2)

            @pl.when(k == 0)
            def _():
                acc_ref[...] = part + c_ref[...].astype(F32) if has_add else part

            @pl.when(k > 0)
            def _():
                acc_ref[...] += part

            if use_scratch:
                @pl.when(k == nk - 1)
                def _():
                    o_ref[...] = acc_ref[...].astype(out_dtype)

    in_specs = [a_spec, b_spec] + ([o_spec] if has_add else [])
    args = (a, b) + ((addend,) if has_add else ())
    if after is not None:
        in_specs.append(pl.BlockSpec(after.shape, lambda i, j, k: (0, 0)))
        args += (after,)
    aliases = {}
    if into is not None:
        in_specs.append(ANY)
        args += (into,)
        aliases = {len(args) - 1: 0}
        o_spec = pl.BlockSpec((tm, tn), lambda i, j, k: (i, j + o_noff))
    return pl.pallas_call(
        body,
        name=name,
        grid=grid,
        in_specs=in_specs,
        out_specs=o_spec,
        out_shape=jax.ShapeDtypeStruct((M, N) if into is None else into.shape, out_dtype),
        input_output_aliases=aliases,
        scratch_shapes=[pltpu.VMEM((tm, tn), F32)] if use_scratch else [],
        compiler_params=_params("parallel", "parallel", "arbitrary"),
    )(*args)


def _row_spec(tm, n):
    return pl.BlockSpec((tm, n), lambda i: (i, 0))


def _full_spec(shape):
    return pl.BlockSpec(shape, lambda i: tuple(0 for _ in shape))


def _norm_cast(x, g, *, name):
    T, D = x.shape
    tm = _pick(T, (512, 256, 128))

    def body(x_ref, g_ref, u_ref):
        u_ref[...] = _rms(x_ref[...], g_ref[...]).astype(BF16)

    return pl.pallas_call(
        body, name=name, grid=(T // tm,),
        in_specs=[_row_spec(tm, D), _full_spec((1, D))],
        out_specs=_row_spec(tm, D),
        out_shape=jax.ShapeDtypeStruct((T, D), BF16),
        compiler_params=_params("parallel"),
    )(x, g)


def _norm_bwd_add(x, g, du, dres, *, name):
    T, D = x.shape
    tm = _pick(T, (512, 256, 128))
    has_res = dres is not None

    def body(*refs):
        if has_res:
            x_ref, g_ref, du_ref, dr_ref, dx_ref, dxb_ref, dg_ref = refs
        else:
            x_ref, g_ref, du_ref, dx_ref, dxb_ref, dg_ref = refs
        _, vjp = jax.vjp(_rms, x_ref[...], g_ref[...])
        dx, dg = vjp(du_ref[...].astype(F32))
        if has_res:
            dx = dx + dr_ref[...]
        dx_ref[...] = dx
        dxb_ref[...] = dx.astype(BF16)

        @pl.when(pl.program_id(0) == 0)
        def _():
            dg_ref[...] = jnp.zeros_like(dg_ref)

        dg_ref[...] += dg

    ins = [x, g, du] + ([dres] if has_res else [])
    in_specs = [_row_spec(tm, D), _full_spec((1, D)), _row_spec(tm, D)] + ([_row_spec(tm, D)] if has_res else [])
    return pl.pallas_call(
        body, name=name, grid=(T // tm,),
        in_specs=in_specs,
        out_specs=[_row_spec(tm, D), _row_spec(tm, D), _full_spec((1, D))],
        out_shape=[jax.ShapeDtypeStruct((T, D), F32), jax.ShapeDtypeStruct((T, D), BF16), jax.ShapeDtypeStruct((1, D), F32)],
        compiler_params=_params("arbitrary"),
    )(*ins)


def _final_loss_bwd(h2, g, target, *, name):
    T, D = h2.shape
    tm = _pick(T, (512, 256, 128))

    def body(h_ref, g_ref, t_ref, dx_ref, dxb_ref, dg_ref, loss_ref):
        y, vjp = jax.vjp(_rms, h_ref[...], g_ref[...])
        err = y - t_ref[...]
        dx, dg = vjp(err * (1.0 / D))
        dx_ref[...] = dx
        dxb_ref[...] = dx.astype(BF16)

        @pl.when(pl.program_id(0) == 0)
        def _():
            dg_ref[...] = jnp.zeros_like(dg_ref)
            loss_ref[...] = jnp.zeros_like(loss_ref)

        dg_ref[...] += dg
        loss_ref[...] += (0.5 / D) * jnp.sum(jnp.sum(err * err, axis=1, keepdims=True), axis=0, keepdims=True)

    return pl.pallas_call(
        body, name=name, grid=(T // tm,),
        in_specs=[_row_spec(tm, D), _full_spec((1, D)), _row_spec(tm, D)],
        out_specs=[_row_spec(tm, D), _row_spec(tm, D), _full_spec((1, D)), _full_spec((1, 1))],
        out_shape=[jax.ShapeDtypeStruct((T, D), F32), jax.ShapeDtypeStruct((T, D), BF16), jax.ShapeDtypeStruct((1, D), F32), jax.ShapeDtypeStruct((1, 1), F32)],
        compiler_params=_params("arbitrary"),
    )(h2, g, target)


def _merge_fn(gates, a, b):
    ga = gates[:, :D_MODEL].astype(F32)
    gb = gates[:, D_MODEL:].astype(F32)
    return _sigmoid(ga) * a.astype(F32) + _sigmoid(gb) * b.astype(F32)


def _gates_spec(tm):
    return pl.BlockSpec((tm, W_GATES), lambda i: (i, O_GATES // W_GATES))


def _merge_fwd(z, a, b, *, name):
    T = a.shape[0]
    tm = _pick(T, (512, 256, 128))

    def body(g_ref, a_ref, b_ref, o_ref):
        o_ref[...] = _merge_fn(g_ref[...], a_ref[...], b_ref[...]).astype(BF16)

    return pl.pallas_call(
        body, name=name, grid=(T // tm,),
        in_specs=[_gates_spec(tm), _row_spec(tm, D_MODEL), _row_spec(tm, D_MODEL)],
        out_specs=_row_spec(tm, D_MODEL),
        out_shape=jax.ShapeDtypeStruct((T, D_MODEL), BF16),
        compiler_params=_params("parallel"),
    )(z, a, b)


def _merge_bwd(z, a, b, dmerged, dz, *, name):
    T = a.shape[0]
    tm = _pick(T, (512, 256, 128))

    def body(g_ref, a_ref, b_ref, dm_ref, dz_in, dg_ref, da_ref, db_ref):
        g = g_ref[...].astype(F32)
        dm = dm_ref[...].astype(F32)
        sa = _sigmoid(g[:, :D_MODEL])
        sb = _sigmoid(g[:, D_MODEL:])
        da_ref[...] = (dm * sa).astype(BF16)
        db_ref[...] = (dm * sb).astype(BF16)
        dg_ref[:, :D_MODEL] = (dm * a_ref[...].astype(F32) * sa * (1.0 - sa)).astype(BF16)
        dg_ref[:, D_MODEL:] = (dm * b_ref[...].astype(F32) * sb * (1.0 - sb)).astype(BF16)

    return pl.pallas_call(
        body, name=name, grid=(T // tm,),
        in_specs=[_gates_spec(tm), _row_spec(tm, D_MODEL), _row_spec(tm, D_MODEL), _row_spec(tm, D_MODEL), ANY],
        out_specs=[_gates_spec(tm), _row_spec(tm, D_MODEL), _row_spec(tm, D_MODEL)],
        out_shape=[jax.ShapeDtypeStruct(dz.shape, BF16), jax.ShapeDtypeStruct((T, D_MODEL), BF16), jax.ShapeDtypeStruct((T, D_MODEL), BF16)],
        input_output_aliases={4: 0},
        compiler_params=_params("parallel"),
    )(z, a, b, dmerged, dz)


CONV_TC = 256


def _shift_down(x, n, rows):
    return jnp.where(rows >= n, pltpu.roll(x, n, 0), 0.0)


def _shift_up(x, n, rows, S):
    return jnp.where(rows < S - n, pltpu.roll(x, S - n, 0), 0.0)


def _conv_act_fwd(gu, conv_w, conv_b, *, name):
    B, S, _ = gu.shape
    tc = CONV_TC
    nc = D_FF // tc

    def body(g_ref, up_ref, w_ref, b_ref, o_ref):
        g = g_ref[...].astype(F32)
        rows = lax.broadcasted_iota(jnp.int32, g.shape, 0)
        w = w_ref[...]
        a = w[2:3] * g + w[1:2] * _shift_down(g, 1, rows) + w[0:1] * _shift_down(g, 2, rows) + b_ref[...]
        o_ref[...] = (_silu(a) * up_ref[...].astype(F32)).astype(BF16)

    return pl.pallas_call(
        body, name=name, grid=(B, nc),
        in_specs=[pl.BlockSpec((None, S, tc), lambda b, j: (b, 0, j)),
                  pl.BlockSpec((None, S, tc), lambda b, j: (b, 0, j + nc)),
                  pl.BlockSpec((3, tc), lambda b, j: (0, j)),
                  pl.BlockSpec((1, tc), lambda b, j: (0, j))],
        out_specs=pl.BlockSpec((None, S, tc), lambda b, j: (b, 0, j)),
        out_shape=jax.ShapeDtypeStruct((B, S, D_FF), BF16),
        compiler_params=_params("parallel", "parallel"),
    )(gu, gu, conv_w, conv_b)


def _conv_act_bwd(gu, conv_w, conv_b, dact, *, name):
    B, S, _ = gu.shape
    tc = CONV_TC
    nc = D_FF // tc

    def body(g_ref, up_ref, w_ref, b_ref, da_ref, dg_ref, dup_ref, dw_ref, db_ref):
        g = g_ref[...].astype(F32)
        up = up_ref[...].astype(F32)
        dact = da_ref[...].astype(F32)
        rows = lax.broadcasted_iota(jnp.int32, g.shape, 0)
        w = w_ref[...]
        g1 = _shift_down(g, 1, rows)
        g2 = _shift_down(g, 2, rows)
        a = w[2:3] * g + w[1:2] * g1 + w[0:1] * g2 + b_ref[...]
        sg = _sigmoid(a)
        dup_ref[...] = (dact * a * sg).astype(BF16)
        da = dact * up * sg * (1.0 + a * (1.0 - sg))
        dg = w[2:3] * da + w[1:2] * _shift_up(da, 1, rows, S) + w[0:1] * _shift_up(da, 2, rows, S)
        dg_ref[...] = dg.astype(BF16)

        @pl.when(pl.program_id(1) == 0)
        def _():
            dw_ref[...] = jnp.zeros_like(dw_ref)
            db_ref[...] = jnp.zeros_like(db_ref)

        dw_ref[0:1, :] += jnp.sum(da * g2, axis=0, keepdims=True)
        dw_ref[1:2, :] += jnp.sum(da * g1, axis=0, keepdims=True)
        dw_ref[2:3, :] += jnp.sum(da * g, axis=0, keepdims=True)
        db_ref[...] += jnp.sum(da, axis=0, keepdims=True)

    col = lambda j, b: (b, 0, j)
    return pl.pallas_call(
        body, name=name, grid=(nc, B),
        in_specs=[pl.BlockSpec((None, S, tc), col),
                  pl.BlockSpec((None, S, tc), lambda j, b: (b, 0, j + nc)),
                  pl.BlockSpec((3, tc), lambda j, b: (0, j)),
                  pl.BlockSpec((1, tc), lambda j, b: (0, j)),
                  pl.BlockSpec((None, S, tc), col)],
        out_specs=[pl.BlockSpec((None, S, tc), col), pl.BlockSpec((None, S, tc), col),
                   pl.BlockSpec((3, tc), lambda j, b: (0, j)), pl.BlockSpec((1, tc), lambda j, b: (0, j))],
        out_shape=[jax.ShapeDtypeStruct((B, S, D_FF), BF16), jax.ShapeDtypeStruct((B, S, D_FF), BF16),
                   jax.ShapeDtypeStruct((3, D_FF), F32), jax.ShapeDtypeStruct((1, D_FF), F32)],
        compiler_params=_params("parallel", "arbitrary"),
    )(gu, gu, conv_w, conv_b, dact)


HGRN_CPB = 4
HF = HGRN_HEADS * HGRN_DK


def _tri(n, upper=False):
    r = lax.broadcasted_iota(jnp.int32, (n, n), 0)
    c = lax.broadcasted_iota(jnp.int32, (n, n), 1)
    return (c >= r) if upper else (r >= c)


def _hs(h):
    return slice(h * HGRN_DK, (h + 1) * HGRN_DK)


def _cumsum_rows(tri_b, x):
    hi = x.astype(BF16)
    lo = (x - hi.astype(F32)).astype(BF16)
    return _nn(tri_b, hi) + _nn(tri_b, lo)


def _hgrn_pre(q, fz, lb, tril_b):
    qf = _silu(q)
    sg = _sigmoid(fz)
    f = lb + (1.0 - lb) * sg
    k = 1.0 - f
    b = _cumsum_rows(tril_b, jnp.log2(f))
    bref = b[CHUNK // 2:CHUNK // 2 + 1, :]
    blast = b[CHUNK - 1:CHUNK, :]
    e1 = jnp.exp2(b - bref)
    e2 = jnp.exp2(bref - b)
    e3 = e1 * jnp.exp2(bref)
    e4 = e2 * jnp.exp2(blast - bref)
    dec = jnp.exp2(blast)
    return sg, f, (e1, e2, e3, e4), qf * e1, k * e2, qf * e3, k * e4, dec


def _hgrn_fwd(zh, lb, gn, *, name):
    B, S, _ = zh.shape
    cpb = HGRN_CPB
    ts = cpb * CHUNK
    nblk = S // ts

    def body(z_ref, lb_ref, gn_ref, o_ref, st_ref, state):
        @pl.when(pl.program_id(1) == 0)
        def _():
            state[...] = jnp.zeros_like(state)

        H = HGRN_HEADS
        causal = _tri(CHUNK)
        tril_b = causal.astype(BF16)
        lb = lb_ref[...]
        for c in range(cpb):
            rows = slice(c * CHUNK, (c + 1) * CHUNK)
            q = z_ref[rows, 0:HF].astype(F32)
            fz = z_ref[rows, HF:2 * HF].astype(F32)
            v = z_ref[rows, 2 * HF:3 * HF]
            hg = z_ref[rows, 3 * HF:4 * HF].astype(F32)
            _, _, _, q_in, k_in, q_out, k_st, dec = _hgrn_pre(q, fz, lb, tril_b)
            q_in, k_in, q_out, k_st = (t.astype(BF16) for t in (q_in, k_in, q_out, k_st))
            a = [jnp.where(causal, _nt(q_in[:, _hs(h)], k_in[:, _hs(h)]), 0.0).astype(BF16) for h in range(H)]
            st = [state[h] for h in range(H)]
            for h in range(H):
                st_ref[c, h] = st[h]
            o = [_nn(a[h], v[:, _hs(h)]) + _nt(q_out[:, _hs(h)], st[h].astype(BF16)) for h in range(H)]
            for h in range(H):
                state[h] = st[h] * dec[:, _hs(h)] + _tn(v[:, _hs(h)], k_st[:, _hs(h)])
            gate = _silu(hg)
            for h in range(H):
                o_ref[rows, _hs(h)] = (_rms(o[h], gn_ref[...]) * gate[:, _hs(h)]).astype(BF16)

    return pl.pallas_call(
        body, name=name, grid=(B, nblk),
        in_specs=[pl.BlockSpec((None, ts, 4 * HF), lambda b, s: (b, s, 0)),
                  pl.BlockSpec((1, HF), lambda b, s: (0, 0)),
                  pl.BlockSpec((1, HGRN_DK), lambda b, s: (0, 0))],
        out_specs=[pl.BlockSpec((None, ts, HF), lambda b, s: (b, s, 0)),
                   pl.BlockSpec((None, cpb, HGRN_HEADS, HGRN_DK, HGRN_DK), lambda b, s: (b, s, 0, 0, 0))],
        out_shape=[jax.ShapeDtypeStruct((B, S, HF), BF16),
                   jax.ShapeDtypeStruct((B, S // CHUNK, HGRN_HEADS, HGRN_DK, HGRN_DK), F32)],
        scratch_shapes=[pltpu.VMEM((HGRN_HEADS, HGRN_DK, HGRN_DK), F32)],
        compiler_params=_params("arbitrary", "arbitrary"),
    )(zh, lb, gn)


def _hgrn_bwd(zh, lb, gn, states, doa, dz, *, name):
    B, S, _ = zh.shape
    cpb = HGRN_CPB
    ts = cpb * CHUNK
    nblk = S // ts
    rev = lambda b, s: (b, nblk - 1 - s, 0)

    def body(z_ref, lb_ref, gn_ref, st_ref, do_ref, dz_in, dz_ref, dlb_ref, dgn_ref, dstate):
        @pl.when(pl.program_id(1) == 0)
        def _():
            dstate[...] = jnp.zeros_like(dstate)

        @pl.when((pl.program_id(0) == 0) & (pl.program_id(1) == 0))
        def _():
            dlb_ref[...] = jnp.zeros_like(dlb_ref)
            dgn_ref[...] = jnp.zeros_like(dgn_ref)

        H = HGRN_HEADS
        cat = lambda xs: jnp.concatenate(xs, axis=1)
        causal = _tri(CHUNK)
        tril_b = causal.astype(BF16)
        triu_b = _tri(CHUNK, upper=True).astype(BF16)
        rowid = lax.broadcasted_iota(jnp.int32, (CHUNK, HF), 0)
        lb = lb_ref[...]
        gn = gn_ref[...]
        for c in reversed(range(cpb)):
            rows = slice(c * CHUNK, (c + 1) * CHUNK)
            q = z_ref[rows, 0:HF].astype(F32)
            fz = z_ref[rows, HF:2 * HF].astype(F32)
            v = z_ref[rows, 2 * HF:3 * HF]
            hg = z_ref[rows, 3 * HF:4 * HF].astype(F32)
            sg, f, (e1, e2, e3, e4), q_in, k_in, q_out, k_st, dec = _hgrn_pre(q, fz, lb, tril_b)
            q_in_b, k_in_b, q_out_b, k_st_b = (t.astype(BF16) for t in (q_in, k_in, q_out, k_st))
            a_b = [jnp.where(causal, _nt(q_in_b[:, _hs(h)], k_in_b[:, _hs(h)]), 0.0).astype(BF16) for h in range(H)]
            st = [st_ref[c, h] for h in range(H)]
            st_b = [t.astype(BF16) for t in st]
            o = [_nn(a_b[h], v[:, _hs(h)]) + _nt(q_out_b[:, _hs(h)], st_b[h]) for h in range(H)]
            dout = do_ref[rows, :].astype(F32)
            shg = _sigmoid(hg)
            gate = hg * shg
            do_l, dgn_acc = [], jnp.zeros_like(gn)
            for h in range(H):
                _, norm_vjp = jax.vjp(_rms, o[h], gn)
                d_o, d_gn = norm_vjp(dout[:, _hs(h)] * gate[:, _hs(h)])
                do_l.append(d_o)
                dgn_acc = dgn_acc + d_gn
            dgn_ref[...] += dgn_acc
            on = cat([_rms(o[h], gn) for h in range(H)])
            dhg = dout * on * shg * (1.0 + hg * (1.0 - shg))
            do_b = [t.astype(BF16) for t in do_l]
            dst = [dstate[h] for h in range(H)]
            dst_b = [t.astype(BF16) for t in dst]
            da_b = [jnp.where(causal, _nt(do_b[h], v[:, _hs(h)]), 0.0).astype(BF16) for h in range(H)]
            dv = cat([_tn(a_b[h], do_b[h]) + _nt(k_st_b[:, _hs(h)], dst_b[h]) for h in range(H)])
            dq_in = cat([_nn(da_b[h], k_in_b[:, _hs(h)]) for h in range(H)])
            dk_in = cat([_tn(da_b[h], q_in_b[:, _hs(h)]) for h in range(H)])
            dq_out = cat([_nn(do_b[h], st_b[h]) for h in range(H)])
            dk_st = cat([_nn(v[:, _hs(h)], dst_b[h]) for h in range(H)])
            ddec = cat([jnp.sum(st[h] * dst[h], axis=0, keepdims=True) for h in range(H)])
            for h in range(H):
                dstate[h] = dst[h] * dec[:, _hs(h)] + _tn(do_b[h], q_out_b[:, _hs(h)])
            t_qin = dq_in * q_in
            t_kin = dk_in * k_in
            t_kst = dk_st * k_st
            db = t_qin - t_kin + dq_out * q_out - t_kst
            dbref = jnp.sum(t_kin - t_qin, axis=0, keepdims=True)
            dblast = jnp.sum(t_kst, axis=0, keepdims=True) + ddec * dec
            db = db + jnp.where(rowid == CHUNK // 2, dbref, 0.0) + jnp.where(rowid == CHUNK - 1, dblast, 0.0)
            dlogf = _cumsum_rows(triu_b, db)
            dqf = dq_in * e1 + dq_out * e3
            dk = dk_in * e2 + dk_st * e4
            df = dlogf / f - dk
            dfz = df * (1.0 - lb) * sg * (1.0 - sg)
            dlb_ref[...] += jnp.sum(df * (1.0 - sg), axis=0, keepdims=True)
            sq = _sigmoid(q)
            dq = dqf * sq * (1.0 + q * (1.0 - sq))
            dz_ref[rows, 0:HF] = dq.astype(BF16)
            dz_ref[rows, HF:2 * HF] = dfz.astype(BF16)
            dz_ref[rows, 2 * HF:3 * HF] = dv.astype(BF16)
            dz_ref[rows, 3 * HF:4 * HF] = dhg.astype(BF16)

    return pl.pallas_call(
        body, name=name, grid=(B, nblk),
        in_specs=[pl.BlockSpec((None, ts, 4 * HF), rev),
                  pl.BlockSpec((1, HF), lambda b, s: (0, 0)),
                  pl.BlockSpec((1, HGRN_DK), lambda b, s: (0, 0)),
                  pl.BlockSpec((None, cpb, HGRN_HEADS, HGRN_DK, HGRN_DK), lambda b, s: (b, nblk - 1 - s, 0, 0, 0)),
                  pl.BlockSpec((None, ts, HF), rev),
                  ANY],
        out_specs=[pl.BlockSpec((None, ts, 4 * HF), rev),
                   pl.BlockSpec((1, HF), lambda b, s: (0, 0)),
                   pl.BlockSpec((1, HGRN_DK), lambda b, s: (0, 0))],
        out_shape=[jax.ShapeDtypeStruct(dz.shape, BF16),
                   jax.ShapeDtypeStruct((1, HF), F32),
                   jax.ShapeDtypeStruct((1, HGRN_DK), F32)],
        input_output_aliases={5: 0},
        scratch_shapes=[pltpu.VMEM((HGRN_HEADS, HGRN_DK, HGRN_DK), F32)],
        compiler_params=_params("arbitrary", "arbitrary"),
    )(zh, lb, gn, states, doa, dz)


KV_W = ATT_KV_HEADS * ATT_HD
ATT_SCALE = ATT_HD ** -0.5


def _rope(x, cos, sin, inverse=False):
    half = ROPE_DIM // 2
    outs = []
    for p in range(x.shape[1] // 128):
        xp = x[:, p * 128:(p + 1) * 128]
        lane = lax.broadcasted_iota(jnp.int32, xp.shape, 1) % ATT_HD
        sw = jnp.where(lane < half, pltpu.roll(xp, 128 - half, 1), pltpu.roll(xp, half, 1))
        outs.append(xp * cos - sw * sin if inverse else xp * cos + sw * sin)
    return outs[0] if len(outs) == 1 else jnp.concatenate(outs, axis=1)


PAIRS_PER_KV = ATT_GROUP // 2


def _swap_halves(x):
    return pltpu.roll(x, ATT_HD, 1)


def _kv_padded(t, low):
    sw = _swap_halves(t)
    zero = jnp.zeros_like(t)
    out = []
    for g in range(ATT_KV_HEADS):
        in_low, in_high = (t, sw) if g == 0 else (sw, t)
        out.append((jnp.where(low, in_low, zero).astype(BF16), jnp.where(low, zero, in_high).astype(BF16)))
    return out


def _swa_mask(first_block):
    qi = lax.broadcasted_iota(jnp.int32, (WINDOW, 2 * WINDOW), 0)
    mi = lax.broadcasted_iota(jnp.int32, (WINDOW, 2 * WINDOW), 1)
    band = (mi > qi) & (mi <= qi + WINDOW)
    return band & (jnp.logical_not(first_block) | (mi >= WINDOW))


def _swa_specs(nb):
    cur = lambda b, i: (b, i, 0)
    prev = lambda b, i: (b, jnp.maximum(i - 1, 0), 0)
    return cur, prev


def _swa_z_specs():
    q = pl.BlockSpec((None, WINDOW, W_AQ), lambda b, i: (b, i, O_AQ // W_AQ))
    kv_prev = pl.BlockSpec((None, WINDOW, W_AKV), lambda b, i: (b, jnp.maximum(i - 1, 0), O_AKV // W_AKV))
    kv_cur = pl.BlockSpec((None, WINDOW, W_AKV), lambda b, i: (b, i, O_AKV // W_AKV))
    return q, kv_prev, kv_cur


def _swa_fwd(z, cos, sin, sinks, *, name):
    B, S, _ = z.shape
    nb = S // WINDOW
    cur, prev = _swa_specs(nb)

    def body(q_ref, kvp_ref, kvc_ref, cp_ref, sp_ref, cc_ref, sc_ref, sink_ref, o_ref, lse_ref):
        cos_c, sin_c = cc_ref[...], sc_ref[...]
        q = (_rope(q_ref[...].astype(F32), cos_c, sin_c) * ATT_SCALE).astype(BF16)
        k = jnp.concatenate([_rope(kvp_ref[:, :KV_W].astype(F32), cp_ref[...], sp_ref[...]),
                             _rope(kvc_ref[:, :KV_W].astype(F32), cos_c, sin_c)], axis=0)
        v = jnp.concatenate([kvp_ref[:, KV_W:], kvc_ref[:, KV_W:]], axis=0).astype(F32)
        low = lax.broadcasted_iota(jnp.int32, k.shape, 1) < ATT_HD
        kpad = _kv_padded(k, low)
        vpad = _kv_padded(v, low)
        mask = _swa_mask(pl.program_id(1) == 0)
        lses = []
        for g in range(ATT_KV_HEADS):
            pairs = range(g * PAIRS_PER_KV, (g + 1) * PAIRS_PER_KV)
            keys = [(p, e) for p in pairs for e in (0, 1)]
            qp = {p: q[:, p * 128:(p + 1) * 128] for p in pairs}
            s = {pe: jnp.where(mask, _nt(qp[pe[0]], kpad[g][pe[1]]), NEG_INF) for pe in keys}
            pr = {}
            for pe in keys:
                sink = sink_ref[0, 2 * pe[0] + pe[1]]
                m = jnp.maximum(jnp.max(s[pe], axis=1, keepdims=True), sink)
                ex = jnp.exp(s[pe] - m)
                den = jnp.sum(ex, axis=1, keepdims=True) + jnp.exp(sink - m)
                pr[pe] = (ex * (1.0 / den)).astype(BF16)
                lses.append(m + jnp.log(den))
            for p in pairs:
                o_ref[:, p * 128:(p + 1) * 128] = (_nn(pr[p, 0], vpad[g][0]) + _nn(pr[p, 1], vpad[g][1])).astype(BF16)
        lse_ref[...] = jnp.concatenate(lses, axis=1)

    tab = lambda im: pl.BlockSpec((None, WINDOW, 128), im)
    return pl.pallas_call(
        body, name=name, grid=(B, nb),
        in_specs=[*_swa_z_specs(),
                  tab(prev), tab(prev), tab(cur), tab(cur),
                  pl.BlockSpec(memory_space=pltpu.SMEM)],
        out_specs=[pl.BlockSpec((None, WINDOW, D_MODEL), cur), pl.BlockSpec((None, WINDOW, ATT_HEADS), cur)],
        out_shape=[jax.ShapeDtypeStruct((B, S, D_MODEL), BF16), jax.ShapeDtypeStruct((B, S, ATT_HEADS), F32)],
        compiler_params=_params("parallel", "parallel"),
    )(z, z, z, cos, sin, cos, sin, sinks)


def _swa_bwd(z, cos, sin, sinks, lse, dob, dz, *, name):
    B, S, _ = z.shape
    nb = S // WINDOW
    cur, prev = _swa_specs(nb)

    def body(q_ref, kvp_ref, kvc_ref, cp_ref, sp_ref, cc_ref, sc_ref, sink_ref, lse_ref, do_ref, dz_in,
             dq_ref, dkc_ref, dkp_ref, dsink_ref):
        @pl.when((pl.program_id(0) == 0) & (pl.program_id(1) == 0))
        def _():
            dsink_ref[...] = jnp.zeros_like(dsink_ref)

        cos_c, sin_c, cos_p, sin_p = cc_ref[...], sc_ref[...], cp_ref[...], sp_ref[...]
        q = (_rope(q_ref[...].astype(F32), cos_c, sin_c) * ATT_SCALE).astype(BF16)
        k = jnp.concatenate([_rope(kvp_ref[:, :KV_W].astype(F32), cos_p, sin_p),
                             _rope(kvc_ref[:, :KV_W].astype(F32), cos_c, sin_c)], axis=0)
        v = jnp.concatenate([kvp_ref[:, KV_W:], kvc_ref[:, KV_W:]], axis=0).astype(F32)
        low = lax.broadcasted_iota(jnp.int32, k.shape, 1) < ATT_HD
        kpad = _kv_padded(k, low)
        vpad = _kv_padded(v, low)
        mask = _swa_mask(pl.program_id(1) == 0)
        lse = lse_ref[...]
        dq_parts, dk_sum, dv_sum, dsinks = [], [], [], []
        for g in range(ATT_KV_HEADS):
            pairs = range(g * PAIRS_PER_KV, (g + 1) * PAIRS_PER_KV)
            keys = [(p, e) for p in pairs for e in (0, 1)]
            qp = {p: q[:, p * 128:(p + 1) * 128] for p in pairs}
            dop = {p: do_ref[:, p * 128:(p + 1) * 128] for p in pairs}
            s = {pe: jnp.where(mask, _nt(qp[pe[0]], kpad[g][pe[1]]), NEG_INF) for pe in keys}
            dp = {pe: _nt(dop[pe[0]], vpad[g][pe[1]]) for pe in keys}
            pr, ds = {}, {}
            for pe in keys:
                h = 2 * pe[0] + pe[1]
                lse_h = lse[:, h:h + 1]
                pf = jnp.exp(s[pe] - lse_h)
                delta = jnp.sum(pf * dp[pe], axis=1, keepdims=True)
                ds[pe] = (pf * (dp[pe] - delta)).astype(BF16)
                pr[pe] = pf.astype(BF16)
                p_sink = jnp.exp(sink_ref[0, h] - lse_h)
                dsinks.append(-jnp.sum(p_sink * delta, axis=0, keepdims=True))
            for p in pairs:
                dq_parts.append((_nn(ds[p, 0], kpad[g][0]) + _nn(ds[p, 1], kpad[g][1])) * ATT_SCALE)
            x = [sum(_tn(ds[p, e], qp[p]) for p in pairs) for e in (0, 1)]
            y = [sum(_tn(pr[p, e], dop[p]) for p in pairs) for e in (0, 1)]
            zk = jnp.where(low, x[0], x[1])
            zv = jnp.where(low, y[0], y[1])
            dk_sum.append(zk + _swap_halves(zk))
            dv_sum.append(zv + _swap_halves(zv))
        dq_ref[...] = _rope(jnp.concatenate(dq_parts, axis=1), cos_c, sin_c, inverse=True).astype(BF16)
        dk = jnp.where(low, dk_sum[0], dk_sum[1])
        dv = jnp.where(low, dv_sum[0], dv_sum[1])
        dkp_ref[:, :KV_W] = _rope(dk[:WINDOW], cos_p, sin_p, inverse=True)
        dkp_ref[:, KV_W:] = dv[:WINDOW]
        dkc_ref[:, :KV_W] = _rope(dk[WINDOW:], cos_c, sin_c, inverse=True)
        dkc_ref[:, KV_W:] = dv[WINDOW:]
        dsink_ref[...] += jnp.concatenate(dsinks, axis=1)

    tab = lambda im: pl.BlockSpec((None, WINDOW, 128), im)
    return pl.pallas_call(
        body, name=name, grid=(B, nb),
        in_specs=[*_swa_z_specs(),
                  tab(prev), tab(prev), tab(cur), tab(cur),
                  pl.BlockSpec(memory_space=pltpu.SMEM),
                  pl.BlockSpec((None, WINDOW, ATT_HEADS), cur),
                  pl.BlockSpec((None, WINDOW, D_MODEL), cur),
                  ANY],
        out_specs=[_swa_z_specs()[0],
                   pl.BlockSpec((None, WINDOW, 2 * KV_W), cur), pl.BlockSpec((None, WINDOW, 2 * KV_W), cur),
                   pl.BlockSpec((1, ATT_HEADS), lambda b, i: (0, 0))],
        out_shape=[jax.ShapeDtypeStruct(dz.shape, BF16),
                   jax.ShapeDtypeStruct((B, S, 2 * KV_W), F32), jax.ShapeDtypeStruct((B, S, 2 * KV_W), F32),
                   jax.ShapeDtypeStruct((1, ATT_HEADS), F32)],
        input_output_aliases={10: 0},
        compiler_params=_params("arbitrary", "arbitrary"),
    )(z, z, z, cos, sin, cos, sin, sinks, lse, dob, dz)


def _swa_dkv_combine(dkv_cur, dkv_prev, dz, *, name):
    B, S, W = dkv_cur.shape

    def body(c_ref, p_ref, dz_in, o_ref):
        rows = lax.broadcasted_iota(jnp.int32, (S, W), 0)
        o_ref[...] = (c_ref[...] + _shift_up(p_ref[...], WINDOW, rows, S)).astype(BF16)

    seq = pl.BlockSpec((None, S, W), lambda b: (b, 0, 0))
    return pl.pallas_call(
        body, name=name, grid=(B,),
        in_specs=[seq, seq, ANY], out_specs=pl.BlockSpec((None, S, W), lambda b: (b, 0, O_AKV // W_AKV)),
        out_shape=jax.ShapeDtypeStruct(dz.shape, BF16),
        input_output_aliases={2: 0},
        compiler_params=_params("parallel"),
    )(dkv_cur, dkv_prev, dz)


def _rope_tables(positions):
    half = ROPE_DIM // 2
    inv = ROPE_THETA ** (-2.0 * jnp.arange(half, dtype=F32) / ROPE_DIM)
    ang = positions.astype(F32)[..., None] * inv
    c, s = jnp.cos(ang), jnp.sin(ang)
    pad = jnp.zeros(ang.shape[:-1] + (ATT_HD - ROPE_DIM,), F32)
    cos = jnp.concatenate([c, c, pad + 1.0], axis=-1)
    sin = jnp.concatenate([-s, s, pad], axis=-1)
    return jnp.tile(cos, (1, 1, 2)), jnp.tile(sin, (1, 1, 2))


def _lower_bound(lb_logits, *, name):
    def body(l_ref, o_ref):
        l = l_ref[...]
        e = jnp.exp(l - jnp.max(l, axis=0, keepdims=True))
        o_ref[...] = e[0:1] / jnp.sum(e, axis=0, keepdims=True)

    return pl.pallas_call(body, name=name, out_shape=jax.ShapeDtypeStruct((1, lb_logits.shape[1]), F32))(lb_logits)


W_ZH, W_GATES, W_AQ, W_AKV = 4 * HF, 2 * D_MODEL, ATT_HEADS * ATT_HD, 2 * KV_W
O_ZH, O_GATES, O_AQ, O_AKV = 0, W_ZH, W_ZH + W_GATES, W_ZH + W_GATES + W_AQ
W_IN = W_ZH + W_GATES + W_AQ + W_AKV


def _reorder_w_in(w_in_full):
    return jnp.concatenate([w_in_full[:, :W_ZH], w_in_full[:, W_ZH + W_AQ + W_AKV:], w_in_full[:, W_ZH:W_ZH + W_AQ + W_AKV]], axis=1)


def _reference_order_w_in(w):
    return jnp.concatenate([w[:, :W_ZH], w[:, O_AQ:], w[:, O_GATES:O_AQ]], axis=1)


def _local_step(x, positions, target, small, w_in, rest_weights, emit, start_token):
    B, S, D = x.shape
    T = B * S
    x2 = x.reshape(T, D)
    cos, sin = _rope_tables(positions)
    lb = _lower_bound(small["lb_logits"], name="lb_fwd")
    zero = lambda tok: tok[0:1, 0:1]

    u1 = _norm_cast(x2, small["norm1_g"] + zero(start_token), name="norm1")
    z = _matmul(u1, w_in, out_dtype=BF16, name="mm_z", tm=1024, tn=W_IN // 2)
    z3 = z.reshape(B, S, W_IN)
    oa, states = _hgrn_fwd(z3, lb, small["hgrn_norm_g"], name="hgrn_fwd")
    ob, lse = _swa_fwd(z3, cos, sin, small["attn_sinks"], name="swa_fwd")
    oa2 = oa.reshape(T, D)
    ob2 = ob.reshape(T, D)
    W = rest_weights("mix", ob)
    pa = _matmul(oa2, W["w_a"], out_dtype=BF16, name="mm_pa", tm=2048, tn=512)
    pb = _matmul(ob2, W["w_b"], out_dtype=BF16, name="mm_pb", tm=2048, tn=512)
    merged = _merge_fwd(z, pa, pb, name="merge_fwd")
    h = _matmul(merged, W["w_out"], addend=x2, name="mm_h", tm=2048, tn=512)
    u2 = _norm_cast(h, small["norm2_g"], name="norm2")
    W.update(rest_weights("ffn", u2))
    gu = _matmul(u2, W["w_ffn"], out_dtype=BF16, name="mm_gu", tm=2048, tn=512)
    gu3 = gu.reshape(B, S, 2 * D_FF)
    act = _conv_act_fwd(gu3, W["conv_w"], small["conv_b"], name="conv_act_fwd")
    act2 = act.reshape(T, D_FF)
    h2 = _matmul(act2, W["w_down"], addend=h, name="mm_h2", tm=1024, tn=512)

    g = {}
    dh2, dh2b, g["final_g"], loss = _final_loss_bwd(h2, small["final_g"].reshape(1, D), target.reshape(T, D), name="final_loss_bwd")
    dact = _matmul(dh2b, W["w_down"], tb=True, out_dtype=BF16, name="mm_dact", tm=1024, tn=D_FF)
    dw_down = _matmul(act2, dh2b, ta=True, out_dtype=BF16, name="mm_dw_down", tm=D_FF, tn=1024, tk=1024)
    dg_, dup, g["conv_w"], g["conv_b"] = _conv_act_bwd(gu3, W["conv_w"], small["conv_b"], dact.reshape(B, S, D_FF), name="conv_act_bwd")
    dg2 = dg_.reshape(T, D_FF)
    dup2 = dup.reshape(T, D_FF)
    du2 = _matmul(dg2, W["w_ffn"], tb=True, name="mm_du2_g", tm=1024, tn=512, b_koff=0)
    du2 = _matmul(dup2, W["w_ffn"], tb=True, addend=du2, name="mm_du2_u", tm=1024, tn=512, b_koff=1)
    dw_ffn = _matmul(u2, dg2, ta=True, out_dtype=BF16, into=lax.empty((D, 2 * D_FF), BF16), o_noff=0, name="mm_dw_ffn_g", tm=1024, tn=256, tk=8192)
    dw_ffn = _matmul(u2, dup2, ta=True, out_dtype=BF16, into=dw_ffn, o_noff=D_FF // 256, name="mm_dw_ffn_u", tm=1024, tn=256, tk=8192)
    tok = emit("ffn", dict(w_ffn=dw_ffn, w_down=dw_down))
    dh, dhb, g["norm2_g"] = _norm_bwd_add(h, small["norm2_g"] + zero(tok), du2, dh2, name="norm2_bwd")
    dmerged = _matmul(dhb, W["w_out"], tb=True, out_dtype=BF16, name="mm_dmerged", tm=2048, tn=512)
    dw_out = _matmul(merged, dhb, ta=True, out_dtype=BF16, name="mm_dw_out", tm=1024, tn=1024, tk=2048)
    dz, dpa, dpb = _merge_bwd(z, pa, pb, dmerged, lax.empty((T, W_IN), BF16), name="merge_bwd")
    doa =_matmul(dpa, W["w_a"], tb=True, out_dtype=BF16, name="mm_doa", tm=2048, tn=512)
    dw_a = _matmul(oa2, dpa, ta=True, out_dtype=BF16, name="mm_dw_a", tm=1024, tn=1024, tk=2048)
    dob = _matmul(dpb, W["w_b"], tb=True, out_dtype=BF16, name="mm_dob", tm=2048, tn=512)
    dw_b = _matmul(ob2, dpb, ta=True, out_dtype=BF16, name="mm_dw_b", tm=1024, tn=1024, tk=2048)
    tok = emit("mix", dict(w_out=dw_out, w_a=dw_a, w_b=dw_b))
    dz3, dkv_cur, dkv_prev, dsinks = _swa_bwd(z3, cos, sin, small["attn_sinks"] + zero(tok), lse, dob.reshape(B, S, D),
                                              dz.reshape(B, S, W_IN), name="swa_bwd")
    dz3 = _swa_dkv_combine(dkv_cur, dkv_prev, dz3, name="swa_dkv")
    g["attn_sinks"] = dsinks
    dz3, g["lb"], g["hgrn_norm_g"] = _hgrn_bwd(z3, lb, small["hgrn_norm_g"], states, doa.reshape(B, S, D), dz3, name="hgrn_bwd")
    dz = dz3.reshape(T, W_IN)
    dw_in = _matmul(u1, dz, ta=True, out_dtype=BF16, name="mm_dw_in", tm=1024, tn=256, tk=8192)
    tok = emit("in", dict(w_in=dw_in))
    du1 = _matmul(dz, w_in, tb=True, after=tok, name="mm_du1", tm=1024, tn=512)
    dx, _, g["norm1_g"] = _norm_bwd_add(x2, small["norm1_g"], du1, dh, name="norm1_bwd")
    g["lb_logits"] = _lb_bwd(g.pop("lb"), lb, name="lb_bwd")
    return loss, dx.reshape(B, S, D), g


def _my_place():
    return lax.axis_index("x"), lax.axis_index("y"), lax.axis_index("c")


def _gather_blocks(x_ref, out_ref, send_sems, recv_sems, local_sem):
    x, y, c = _my_place()
    me, sibling = (x, y, c), (x, y, 1 - c)
    chips = [(1 - x, y), (x, 1 - y), (1 - x, 1 - y)]

    def slot(px, py, pc):
        return out_ref.at[4 * px + 2 * py + pc]

    def copy(k, block, to, src=None):
        return pltpu.make_async_remote_copy(
            src_ref=slot(*block) if src is None else src, dst_ref=slot(*block),
            send_sem=send_sems.at[k], recv_sem=recv_sems.at[k], device_id=to, device_id_type=MESH)

    mine = pltpu.make_async_copy(x_ref, slot(*me), local_sem)
    mine.start()
    first = [copy(0, me, sibling, src=x_ref)]
    first += [copy(1 + j, me, (*chip, c), src=x_ref) for j, chip in enumerate(chips)]
    for cp in first:
        cp.start()
    passed = [copy(4 + j, (*chip, c), sibling) for j, chip in enumerate(chips)]
    for j, chip in enumerate(chips):
        copy(1 + j, (*chip, c), me).wait_recv()
        passed[j].start()
    copy(0, sibling, me).wait_recv()
    for j, chip in enumerate(chips):
        copy(4 + j, (*chip, 1 - c), me).wait_recv()
    for cp in first + passed:
        cp.wait_send()
    mine.wait()


GATHER_SEMS = [pltpu.SemaphoreType.DMA((7,)), pltpu.SemaphoreType.DMA((7,)), pltpu.SemaphoreType.DMA]


def _all_gather(blk, *, name):
    return pl.pallas_call(
        _gather_body_fn(), name=name,
        out_shape=jax.ShapeDtypeStruct((N_DEV,) + blk.shape, blk.dtype),
        in_specs=[ANY], out_specs=ANY,
        scratch_shapes=GATHER_SEMS,
    )(blk)


def _gather_body_fn():
    def body(x_ref, out_ref, send_sems, recv_sems, local_sem):
        _gather_blocks(x_ref, out_ref, send_sems, recv_sems, local_sem)
    return body


SLAB_W = D_FF
SMALL_ROWS = dict(norm1_g=(0, 1, D_MODEL), lb_logits=(1, 2, HGRN_HEADS * HGRN_DK), hgrn_norm_g=(3, 1, HGRN_DK), attn_sinks=(4, 1, ATT_HEADS),
                  norm2_g=(5, 1, D_MODEL), conv_b=(6, 1, D_FF), final_g=(7, 1, D_MODEL))
CONV_ROW0 = 8
LOSS_ROW = CONV_ROW0 + 8 * N_DEV
SLAB_ROWS = LOSS_ROW + 8
CONVW_BLK = D_FF // N_DEV


def _small_step(grads, g_conv_w, loss, params, moments, variances, dev, *, name):
    names = list(SMALL_ROWS)
    n = len(names)

    def body(dev_ref, *refs):
        g_refs = dict(zip(names, refs[:n]))
        gc_ref, loss_ref = refs[n], refs[n + 1]
        base = n + 2
        w_refs, m_refs, v_refs = (dict(zip(names + ["conv_w"], refs[base + i * (n + 1):base + (i + 1) * (n + 1)])) for i in range(3))
        o = base + 3 * (n + 1)
        gath_ref, loss_out = refs[o], refs[o + 1]
        outs = {nm: refs[o + 2 + 4 * i:o + 6 + 4 * i] for i, nm in enumerate(names + ["conv_w"])}
        slab, total, send_sems, recv_sems, local_sem = refs[-5:]

        slab[...] = jnp.zeros_like(slab)
        for nm, (r0, nr, w) in SMALL_ROWS.items():
            slab[r0:r0 + nr, 0:w] = g_refs[nm][...]
        for p in range(N_DEV):
            slab[CONV_ROW0 + 8 * p:CONV_ROW0 + 8 * p + 3, 0:CONVW_BLK] = gc_ref[:, p * CONVW_BLK:(p + 1) * CONVW_BLK]
        slab[LOSS_ROW:LOSS_ROW + 1, 0:1] = loss_ref[...]
        _gather_blocks(slab, gath_ref, send_sems, recv_sems, local_sem)
        acc = gath_ref[0]
        for p in range(1, N_DEV):
            acc = acc + gath_ref[p]
        total[...] = acc
        loss_out[...] = total[LOSS_ROW:LOSS_ROW + 1, 0:1]

        def update(nm, g):
            d, mn, vn = _adamw_math(w_refs[nm][...], g, m_refs[nm][...], v_refs[nm][...])
            for ref, val in zip(outs[nm], (g, d, mn, vn)):
                ref[...] = val

        for nm, (r0, nr, w) in SMALL_ROWS.items():
            update(nm, total[r0:r0 + nr, 0:w])
        mine = total[pl.ds(pl.multiple_of(CONV_ROW0 + 8 * dev_ref[0], 8), 8), :]
        update("conv_w", mine[0:3, 0:CONVW_BLK])

    order = names + ["conv_w"]
    ins = [grads[nm] for nm in names] + [g_conv_w, loss]
    for d in (params, moments, variances):
        ins += [d[nm] for nm in order]
    vmem = pl.BlockSpec(memory_space=pltpu.VMEM)
    out_shape = [jax.ShapeDtypeStruct((N_DEV, SLAB_ROWS, SLAB_W), F32), jax.ShapeDtypeStruct((1, 1), F32)]
    for nm in order:
        out_shape += [jax.ShapeDtypeStruct(params[nm].shape, F32)] * 4
    res = pl.pallas_call(
        body, name=name,
        grid_spec=pltpu.PrefetchScalarGridSpec(
            num_scalar_prefetch=1, grid=(1,),
            in_specs=[vmem] * len(ins), out_specs=[vmem] * len(out_shape),
            scratch_shapes=[pltpu.VMEM((SLAB_ROWS, SLAB_W), F32), pltpu.VMEM((SLAB_ROWS, SLAB_W), F32)] + GATHER_SEMS),
        out_shape=out_shape,
    )(dev, *ins)
    return res[1], {nm: tuple(res[2 + 4 * i:6 + 4 * i]) for i, nm in enumerate(order)}


HBM_SPEC = pl.BlockSpec(memory_space=pltpu.HBM)
SEM_SPEC = pl.BlockSpec(memory_space=pltpu.SEMAPHORE)
DATAFLOW_EFFECT = pltpu.SideEffectType.DATAFLOW_SIDE_EFFECTING
N_PEERS = N_DEV - 1


def _peers(x, y, c):
    return [(1 - x if r & 4 else x, 1 - y if r & 2 else y, 1 - c if r & 1 else c) for r in range(1, N_DEV)]


def _exchange_start(srcs, scatter, *, name):
    n = len(srcs)
    lands = [lax.empty(a.shape if scatter else (N_DEV,) + a.shape, a.dtype) for a in srcs]

    def body(*refs):
        src_refs, land_refs = refs[:n], refs[n:2 * n]
        send_sems, recv_sems, token = refs[2 * n], refs[2 * n + 1], refs[-1]
        x, y, c = _my_place()
        me = 4 * x + 2 * y + c
        for i in range(n):
            for r, (tx, ty, tc) in enumerate(_peers(x, y, c)):
                src = src_refs[i].at[4 * tx + 2 * ty + tc] if scatter else src_refs[i]
                pltpu.make_async_remote_copy(
                    src_ref=src, dst_ref=land_refs[i].at[me], send_sem=send_sems.at[N_PEERS * i + r],
                    recv_sem=recv_sems.at[N_PEERS * i + r], device_id=(tx, ty, tc), device_id_type=MESH).start()
        token[...] = jnp.zeros_like(token)

    thru = [pltpu.HBM(a.shape, a.dtype) for a in list(srcs) + lands]
    res = pl.pallas_call(
        body, name=name,
        out_shape=(pltpu.SemaphoreType.DMA((N_PEERS * n,)), pltpu.SemaphoreType.DMA((N_PEERS * n,)), *thru,
                   jax.ShapeDtypeStruct((8, 128), F32)),
        in_specs=[HBM_SPEC] * (2 * n),
        out_specs=(SEM_SPEC, SEM_SPEC, *([HBM_SPEC] * (2 * n)), pl.BlockSpec(memory_space=pltpu.VMEM)),
        input_output_aliases={i: 2 + i for i in range(2 * n)},
        compiler_params=pltpu.CompilerParams(has_side_effects=DATAFLOW_EFFECT),
    )(*[pltpu.with_memory_space_constraint(a, pltpu.HBM) for a in list(srcs) + lands])
    return (res[0], res[1], list(res[2:2 + n]), list(res[2 + n:2 + 2 * n]), scatter), res[-1]


def _exchange_wait(handle, after, *, name):
    send_sems, recv_sems, srcs, lands, scatter = handle
    n = len(srcs)

    def body(*refs):
        src_refs, land_refs = refs[:n], refs[n:2 * n]
        send_sems, recv_sems = refs[2 * n], refs[2 * n + 1]
        x, y, c = _my_place()
        for i in range(n):
            for r in range(N_PEERS):
                src = src_refs[i].at[0] if scatter else src_refs[i]
                cp = pltpu.make_async_remote_copy(
                    src_ref=src, dst_ref=land_refs[i].at[0], send_sem=send_sems.at[N_PEERS * i + r],
                    recv_sem=recv_sems.at[N_PEERS * i + r], device_id=(x, y, c), device_id_type=MESH)
                cp.wait_send()
                cp.wait_recv()

    thru = [pltpu.HBM(a.shape, a.dtype) for a in srcs + lands]
    res = pl.pallas_call(
        body, name=name, out_shape=tuple(thru),
        in_specs=[HBM_SPEC] * (2 * n) + [SEM_SPEC, SEM_SPEC, ANY], out_specs=tuple([HBM_SPEC] * (2 * n)),
        input_output_aliases={i: i for i in range(2 * n)},
        compiler_params=pltpu.CompilerParams(has_side_effects=DATAFLOW_EFFECT),
    )(*srcs, *lands, send_sems, recv_sems, after)
    return list(res[:n]), list(res[n:])


def _with_own(land, own, me):
    return lax.dynamic_update_index_in_dim(land, own, me, 0)


def _adamw_math(w, g, m, v):
    m = ADAM_B1 * m + (1.0 - ADAM_B1) * g
    v = ADAM_B2 * v + (1.0 - ADAM_B2) * (g * g)
    m_hat = m / (1.0 - ADAM_B1 ** ADAM_STEP)
    v_hat = v / (1.0 - ADAM_B2 ** ADAM_STEP)
    delta = -ADAM_LR * (m_hat / (jnp.sqrt(v_hat) + ADAM_EPS) + ADAM_WD * w)
    return delta, m, v


def _adamw_sum(parts, w, m, v, *, name):
    shape = w.shape
    R, n = shape[-2], shape[-1]
    w, m, v = (t.reshape(R, n) for t in (w, m, v))
    tr = _pick(R, (256, 176, 128))

    def body(p_ref, w_ref, m_ref, v_ref, g_ref, d_ref, mo_ref, vo_ref):
        g = p_ref[0].astype(F32)
        for p in range(1, N_DEV):
            g = g + p_ref[p].astype(F32)
        d, mn, vn = _adamw_math(w_ref[...], g, m_ref[...], v_ref[...])
        g_ref[...] = g
        d_ref[...] = d
        mo_ref[...] = mn
        vo_ref[...] = vn

    row = pl.BlockSpec((tr, n), lambda i: (i, 0))
    outs = pl.pallas_call(
        body, name=name, grid=(R // tr,),
        in_specs=[pl.BlockSpec((N_DEV, tr, n), lambda i: (0, i, 0)), row, row, row],
        out_specs=[row, row, row, row],
        out_shape=[jax.ShapeDtypeStruct((R, n), F32)] * 4,
        compiler_params=_params("parallel"),
    )(parts, w, m, v)
    return [t.reshape(shape) for t in outs]


def _lb_bwd(dlb, lb, *, name):
    def body(d_ref, lb_ref, o_ref):
        t = d_ref[...] * lb_ref[...] * (1.0 - lb_ref[...])
        o_ref[0:1, :] = t
        o_ref[1:2, :] = -t

    return pl.pallas_call(body, name=name, out_shape=jax.ShapeDtypeStruct((2, lb.shape[1]), F32))(dlb, lb)


DOWN_BLK, ROW_BLK = D_FF // N_DEV, D_MODEL // N_DEV
CONV_BITS_SHAPE = (16, 256)


def _cols_from_blocks(blocks):
    n, rows, width = blocks.shape
    return blocks.transpose(1, 0, 2).reshape(rows, n * width)


def _blocks_from_cols(full):
    rows, cols = full.shape
    return full.reshape(rows, N_DEV, cols // N_DEV).transpose(1, 0, 2)


def kernel(x, positions, norm1_g, w_in, lb_logits, hgrn_norm_g, w_a, attn_sinks, w_b, w_out, norm2_g, w_ffn_in, conv_w, conv_b, w_down, final_g, loss_target, m_norm1_g, m_w_in, m_lb_logits, m_hgrn_norm_g, m_w_a, m_attn_sinks, m_w_b, m_w_out, m_norm2_g, m_w_ffn_in, m_conv_w, m_conv_b, m_w_down, m_final_g, v_norm1_g, v_w_in, v_lb_logits, v_hgrn_norm_g, v_w_a, v_attn_sinks, v_w_b, v_w_out, v_norm2_g, v_w_ffn_in, v_conv_w, v_conv_b, v_w_down, v_final_g):
    xi, yi, ci = _my_place()
    dev = 4 * xi + 2 * yi + ci

    w_in_blocks = _all_gather(w_in[0].astype(BF16), name="ag_w_in")
    conv_bits = lax.bitcast_convert_type(conv_w, BF16).reshape(-1)
    conv_bits = jnp.pad(conv_bits, (0, CONV_BITS_SHAPE[0] * CONV_BITS_SHAPE[1] - conv_bits.shape[0])).reshape(CONV_BITS_SHAPE)
    gather_handles = {}
    gather_handles["mix"], tok_mix = _exchange_start([w_a[0].astype(BF16), w_b[0].astype(BF16), w_out[0].astype(BF16)], False, name="ag_mix_start")
    gather_handles["ffn"], tok_ffn = _exchange_start([w_ffn_in[0].astype(BF16), w_down[0].astype(BF16), conv_bits], False, name="ag_ffn_start")
    start_token = tok_mix + tok_ffn

    def rest_weights(group, after):
        own, lands = _exchange_wait(gather_handles[group], after, name="ag_" + group + "_wait")
        full = [_with_own(l, o, dev) for l, o in zip(lands, own)]
        if group == "mix":
            return dict(zip(("w_a", "w_b", "w_out"), [t.reshape(D_MODEL, D_MODEL) for t in full]))
        bits = full[2].reshape(N_DEV, -1)[:, :3 * CONVW_BLK * 2].reshape(N_DEV, 3, CONVW_BLK, 2)
        return dict(w_ffn=_cols_from_blocks(full[0]), w_down=full[1].reshape(D_FF, D_MODEL),
                    conv_w=_cols_from_blocks(lax.bitcast_convert_type(bits, F32)))

    handles = {}

    def emit(group, gr):
        if group == "ffn":
            srcs = [_blocks_from_cols(gr["w_ffn"]), gr["w_down"].reshape(N_DEV, DOWN_BLK, D_MODEL)]
        elif group == "mix":
            srcs = [gr[n].reshape(N_DEV, ROW_BLK, D_MODEL) for n in ("w_out", "w_a", "w_b")]
        else:
            srcs = [_blocks_from_cols(_reference_order_w_in(gr["w_in"]))]
        handles[group], token = _exchange_start(srcs, True, name="rs_" + group + "_start")
        return token

    small = dict(norm1_g=norm1_g, lb_logits=lb_logits, hgrn_norm_g=hgrn_norm_g, attn_sinks=attn_sinks, norm2_g=norm2_g,
                 conv_b=conv_b, final_g=final_g)
    w_in_full = _reorder_w_in(_cols_from_blocks(w_in_blocks))
    loss, grad_x, g = _local_step(x, positions, loss_target, small, w_in_full, rest_weights, emit, start_token)

    def parts_of(group, after):
        srcs, lands = _exchange_wait(handles[group], after, name="rs_" + group + "_wait")
        return [_with_own(l, lax.dynamic_index_in_dim(s, dev, 0, keepdims=False), dev) for s, l in zip(srcs, lands)]

    p_ffn, p_down = parts_of("ffn", grad_x)
    p_out, p_a, p_b = parts_of("mix", grad_x)
    (p_in,) = parts_of("in", grad_x)
    big = dict(
        w_in=_adamw_sum(p_in, w_in, m_w_in, v_w_in, name="adamw_w_in"),
        w_a=_adamw_sum(p_a, w_a, m_w_a, v_w_a, name="adamw_w_a"),
        w_b=_adamw_sum(p_b, w_b, m_w_b, v_w_b, name="adamw_w_b"),
        w_out=_adamw_sum(p_out, w_out, m_w_out, v_w_out, name="adamw_w_out"),
        w_ffn_in=_adamw_sum(p_ffn, w_ffn_in, m_w_ffn_in, v_w_ffn_in, name="adamw_w_ffn_in"),
        w_down=_adamw_sum(p_down, w_down, m_w_down, v_w_down, name="adamw_w_down"),
    )

    row = lambda t: t.reshape(1, -1) if t.ndim == 1 else t
    shard = lambda t: t.reshape(3, CONVW_BLK)
    sm_g = {nm: g[nm] for nm in SMALL_ROWS}
    sm_w = dict(norm1_g=norm1_g, lb_logits=lb_logits, hgrn_norm_g=hgrn_norm_g, attn_sinks=attn_sinks, norm2_g=norm2_g,
                conv_b=conv_b, final_g=row(final_g), conv_w=shard(conv_w))
    sm_m = dict(norm1_g=m_norm1_g, lb_logits=m_lb_logits, hgrn_norm_g=m_hgrn_norm_g, attn_sinks=m_attn_sinks, norm2_g=m_norm2_g,
                conv_b=m_conv_b, final_g=row(m_final_g), conv_w=shard(m_conv_w))
    sm_v = dict(norm1_g=v_norm1_g, lb_logits=v_lb_logits, hgrn_norm_g=v_hgrn_norm_g, attn_sinks=v_attn_sinks, norm2_g=v_norm2_g,
                conv_b=v_conv_b, final_g=row(v_final_g), conv_w=shard(v_conv_w))
    loss_total, sm_out = _small_step(sm_g, g["conv_w"], loss, sm_w, sm_m, sm_v, dev.astype(jnp.int32).reshape(1), name="small_step")
    shapes = dict(final_g=final_g.shape, conv_w=conv_w.shape)

    names = ("norm1_g", "w_in", "lb_logits", "hgrn_norm_g", "w_a", "attn_sinks", "w_b", "w_out", "norm2_g", "w_ffn_in", "conv_w", "conv_b", "w_down", "final_g")
    outs = [loss_total.reshape(()), grad_x]
    for kind in range(4):
        outs += [big[n][kind] if n in big else sm_out[n][kind].reshape(shapes.get(n, sm_out[n][kind].shape)) for n in names]
    return tuple(outs)
```

```python
import functools

import jax
import jax.numpy as jnp
from jax import lax
from jax.experimental import pallas as pl
from jax.experimental.pallas import tpu as pltpu

F32 = jnp.float32
BF16 = jnp.bfloat16

D_MODEL = 1024
HGRN_HEADS = 8
HGRN_DK = 128
CHUNK = 64
ATT_HEADS = 16
ATT_KV_HEADS = 2
ATT_HD = 64
ATT_GROUP = ATT_HEADS // ATT_KV_HEADS
WINDOW = 128
ROPE_DIM = ATT_HD // 4
ROPE_THETA = 500000.0
D_FF = 2816
EPS = 1e-6
NEG_INF = -1e30
N_DEV = 8

ADAM_LR = 0.001
ADAM_B1 = 0.9
ADAM_B2 = 0.999
ADAM_EPS = 1e-08
ADAM_WD = 0.01
ADAM_STEP = 10

MESH = pl.DeviceIdType.MESH
ANY = pl.BlockSpec(memory_space=pl.ANY)


def _pick(n, cands):
    for c in cands:
        if n % c == 0:
            return c
    return n


def _sigmoid(x):
    return 0.5 * jnp.tanh(0.5 * x) + 0.5


def _silu(x):
    hx = 0.5 * x
    return hx * jnp.tanh(hx) + hx


def _rms(x, g):
    return x * lax.rsqrt(jnp.mean(x * x, axis=-1, keepdims=True) + EPS) * g


def _dot(a, b, dims):
    return lax.dot_general(a, b, (dims, ((), ())), preferred_element_type=F32)


def _nn(a, b):
    return _dot(a, b, ((1,), (0,)))


def _nt(a, b):
    return _dot(a, b, ((1,), (1,)))


def _tn(a, b):
    return _dot(a, b, ((0,), (0,)))


def _params(*sem):
    return pltpu.CompilerParams(dimension_semantics=sem, vmem_limit_bytes=56 * 1024 * 1024)


def _matmul(a, b, *, ta=False, tb=False, out_dtype=F32, addend=None, after=None, into=None, o_noff=0, name, tm, tn, tk=None,
            n_extent=None, b_koff=0, b_noff=0):
    M, K = (a.shape[1], a.shape[0]) if ta else a.shape
    N = n_extent or (b.shape[0] if tb else b.shape[1])
    tm, tn, tk = min(tm, M), min(tn, N), min(tk or K, K)
    assert M % tm == 0 and N % tn == 0 and K % tk == 0, (name, M, N, K, tm, tn, tk)
    nk = K // tk
    use_scratch = nk > 1 and out_dtype != F32
    grid = (M // tm, N // tn, nk)
    a_spec = pl.BlockSpec((tk, tm), lambda i, j, k: (k, i)) if ta else pl.BlockSpec((tm, tk), lambda i, j, k: (i, k))
    b_spec = pl.BlockSpec((tn, tk), lambda i, j, k: (j + b_noff, k + b_koff)) if tb else pl.BlockSpec((tk, tn), lambda i, j, k: (k + b_koff, j + b_noff))
    o_spec = pl.BlockSpec((tm, tn), lambda i, j, k: (i, j))
    dims = ((0 if ta else 1,), (1 if tb else 0,))
    has_add = addend is not None

    n_in = 2 + has_add + (after is not None) + (into is not None)

    def body(*refs):
        a_ref, b_ref = refs[:2]
        c_ref = refs[2] if has_add else None
        o_ref = refs[n_in]
        part = _dot(a_ref[...], b_ref[...], dims)
        if nk == 1:
            if has_add:
                part = part + c_ref[...].astype(F32)
            o_ref[...] = part.astype(out_dtype)
        else:
            acc_ref = refs[-1] if use_scratch else o_ref
            k = pl.program_id(2)

            @pl.when(k == 0)
            def _():
                acc_ref[...] = part + c_ref[...].astype(F32) if has_add else part

            @pl.when(k > 0)
            def _():
                acc_ref[...] += part

            if use_scratch:
                @pl.when(k == nk - 1)
                def _():
                    o_ref[...] = acc_ref[...].astype(out_dtype)

    in_specs = [a_spec, b_spec] + ([o_spec] if has_add else [])
    args = (a, b) + ((addend,) if has_add else ())
    if after is not None:
        in_specs.append(pl.BlockSpec(after.shape, lambda i, j, k: (0, 0)))
        args += (after,)
    aliases = {}
    if into is not None:
        in_specs.append(ANY)
        args += (into,)
        aliases = {len(args) - 1: 0}
        o_spec = pl.BlockSpec((tm, tn), lambda i, j, k: (i, j + o_noff))
    return pl.pallas_call(
        body,
        name=name,
        grid=grid,
        in_specs=in_specs,
        out_specs=o_spec,
        out_shape=jax.ShapeDtypeStruct((M, N) if into is None else into.shape, out_dtype),
        input_output_aliases=aliases,
        scratch_shapes=[pltpu.VMEM((tm, tn), F32)] if use_scratch else [],
        compiler_params=_params("parallel", "parallel", "arbitrary"),
    )(*args)


def _row_spec(tm, n):
    return pl.BlockSpec((tm, n), lambda i: (i, 0))


def _full_spec(shape):
    return pl.BlockSpec(shape, lambda i: tuple(0 for _ in shape))


def _norm_cast(x, g, *, name):
    T, D = x.shape
    tm = _pick(T, (512, 256, 128))

    def body(x_ref, g_ref, u_ref):
        u_ref[...] = _rms(x_ref[...], g_ref[...]).astype(BF16)

    return pl.pallas_call(
        body, name=name, grid=(T // tm,),
        in_specs=[_row_spec(tm, D), _full_spec((1, D))],
        out_specs=_row_spec(tm, D),
        out_shape=jax.ShapeDtypeStruct((T, D), BF16),
        compiler_params=_params("parallel"),
    )(x, g)


def _norm_bwd_add(x, g, du, dres, *, name):
    T, D = x.shape
    tm = _pick(T, (512, 256, 128))
    has_res = dres is not None

    def body(*refs):
        if has_res:
            x_ref, g_ref, du_ref, dr_ref, dx_ref, dxb_ref, dg_ref = refs
        else:
            x_ref, g_ref, du_ref, dx_ref, dxb_ref, dg_ref = refs
        _, vjp = jax.vjp(_rms, x_ref[...], g_ref[...])
        dx, dg = vjp(du_ref[...].astype(F32))
        if has_res:
            dx = dx + dr_ref[...]
        dx_ref[...] = dx
        dxb_ref[...] = dx.astype(BF16)

        @pl.when(pl.program_id(0) == 0)
        def _():
            dg_ref[...] = jnp.zeros_like(dg_ref)

        dg_ref[...] += dg

    ins = [x, g, du] + ([dres] if has_res else [])
    in_specs = [_row_spec(tm, D), _full_spec((1, D)), _row_spec(tm, D)] + ([_row_spec(tm, D)] if has_res else [])
    return pl.pallas_call(
        body, name=name, grid=(T // tm,),
        in_specs=in_specs,
        out_specs=[_row_spec(tm, D), _row_spec(tm, D), _full_spec((1, D))],
        out_shape=[jax.ShapeDtypeStruct((T, D), F32), jax.ShapeDtypeStruct((T, D), BF16), jax.ShapeDtypeStruct((1, D), F32)],
        compiler_params=_params("arbitrary"),
    )(*ins)


def _final_loss_bwd(h2, g, target, *, name):
    T, D = h2.shape
    tm = _pick(T, (512, 256, 128))

    def body(h_ref, g_ref, t_ref, dx_ref, dxb_ref, dg_ref, loss_ref):
        y, vjp = jax.vjp(_rms, h_ref[...], g_ref[...])
        err = y - t_ref[...]
        dx, dg = vjp(err * (1.0 / D))
        dx_ref[...] = dx
        dxb_ref[...] = dx.astype(BF16)

        @pl.when(pl.program_id(0) == 0)
        def _():
            dg_ref[...] = jnp.zeros_like(dg_ref)
            loss_ref[...] = jnp.zeros_like(loss_ref)

        dg_ref[...] += dg
        loss_ref[...] += (0.5 / D) * jnp.sum(jnp.sum(err * err, axis=1, keepdims=True), axis=0, keepdims=True)

    return pl.pallas_call(
        body, name=name, grid=(T // tm,),
        in_specs=[_row_spec(tm, D), _full_spec((1, D)), _row_spec(tm, D)],
        out_specs=[_row_spec(tm, D), _row_spec(tm, D), _full_spec((1, D)), _full_spec((1, 1))],
        out_shape=[jax.ShapeDtypeStruct((T, D), F32), jax.ShapeDtypeStruct((T, D), BF16), jax.ShapeDtypeStruct((1, D), F32), jax.ShapeDtypeStruct((1, 1), F32)],
        compiler_params=_params("arbitrary"),
    )(h2, g, target)


def _merge_fn(gates, a, b):
    ga = gates[:, :D_MODEL].astype(F32)
    gb = gates[:, D_MODEL:].astype(F32)
    return _sigmoid(ga) * a.astype(F32) + _sigmoid(gb) * b.astype(F32)


def _gates_spec(tm):
    return pl.BlockSpec((tm, W_GATES), lambda i: (i, O_GATES // W_GATES))


def _merge_fwd(z, a, b, *, name):
    T = a.shape[0]
    tm = _pick(T, (512, 256, 128))

    def body(g_ref, a_ref, b_ref, o_ref):
        o_ref[...] = _merge_fn(g_ref[...], a_ref[...], b_ref[...]).astype(BF16)

    return pl.pallas_call(
        body, name=name, grid=(T // tm,),
        in_specs=[_gates_spec(tm), _row_spec(tm, D_MODEL), _row_spec(tm, D_MODEL)],
        out_specs=_row_spec(tm, D_MODEL),
        out_shape=jax.ShapeDtypeStruct((T, D_MODEL), BF16),
        compiler_params=_params("parallel"),
    )(z, a, b)


def _merge_bwd(z, a, b, dmerged, dz, *, name):
    T = a.shape[0]
    tm = _pick(T, (512, 256, 128))

    def body(g_ref, a_ref, b_ref, dm_ref, dz_in, dg_ref, da_ref, db_ref):
        g = g_ref[...].astype(F32)
        dm = dm_ref[...].astype(F32)
        sa = _sigmoid(g[:, :D_MODEL])
        sb = _sigmoid(g[:, D_MODEL:])
        da_ref[...] = (dm * sa).astype(BF16)
        db_ref[...] = (dm * sb).astype(BF16)
        dg_ref[:, :D_MODEL] = (dm * a_ref[...].astype(F32) * sa * (1.0 - sa)).astype(BF16)
        dg_ref[:, D_MODEL:] = (dm * b_ref[...].astype(F32) * sb * (1.0 - sb)).astype(BF16)

    return pl.pallas_call(
        body, name=name, grid=(T // tm,),
        in_specs=[_gates_spec(tm), _row_spec(tm, D_MODEL), _row_spec(tm, D_MODEL), _row_spec(tm, D_MODEL), ANY],
        out_specs=[_gates_spec(tm), _row_spec(tm, D_MODEL), _row_spec(tm, D_MODEL)],
        out_shape=[jax.ShapeDtypeStruct(dz.shape, BF16), jax.ShapeDtypeStruct((T, D_MODEL), BF16), jax.ShapeDtypeStruct((T, D_MODEL), BF16)],
        input_output_aliases={4: 0},
        compiler_params=_params("parallel"),
    )(z, a, b, dmerged, dz)


CONV_TC = 256


def _shift_down(x, n, rows):
    return jnp.where(rows >= n, pltpu.roll(x, n, 0), 0.0)


def _shift_up(x, n, rows, S):
    return jnp.where(rows < S - n, pltpu.roll(x, S - n, 0), 0.0)


def _conv_act_fwd(gu, conv_w, conv_b, *, name):
    B, S, _ = gu.shape
    tc = CONV_TC
    nc = D_FF // tc

    def body(g_ref, up_ref, w_ref, b_ref, o_ref):
        g = g_ref[...].astype(F32)
        rows = lax.broadcasted_iota(jnp.int32, g.shape, 0)
        w = w_ref[...]
        a = w[2:3] * g + w[1:2] * _shift_down(g, 1, rows) + w[0:1] * _shift_down(g, 2, rows) + b_ref[...]
        o_ref[...] = (_silu(a) * up_ref[...].astype(F32)).astype(BF16)

    return pl.pallas_call(
        body, name=name, grid=(B, nc),
        in_specs=[pl.BlockSpec((None, S, tc), lambda b, j: (b, 0, j)),
                  pl.BlockSpec((None, S, tc), lambda b, j: (b, 0, j + nc)),
                  pl.BlockSpec((3, tc), lambda b, j: (0, j)),
                  pl.BlockSpec((1, tc), lambda b, j: (0, j))],
        out_specs=pl.BlockSpec((None, S, tc), lambda b, j: (b, 0, j)),
        out_shape=jax.ShapeDtypeStruct((B, S, D_FF), BF16),
        compiler_params=_params("parallel", "parallel"),
    )(gu, gu, conv_w, conv_b)


def _conv_act_bwd(gu, conv_w, conv_b, dact, *, name):
    B, S, _ = gu.shape
    tc = CONV_TC
    nc = D_FF // tc

    def body(g_ref, up_ref, w_ref, b_ref, da_ref, dg_ref, dup_ref, dw_ref, db_ref):
        g = g_ref[...].astype(F32)
        up = up_ref[...].astype(F32)
        dact = da_ref[...].astype(F32)
        rows = lax.broadcasted_iota(jnp.int32, g.shape, 0)
        w = w_ref[...]
        g1 = _shift_down(g, 1, rows)
        g2 = _shift_down(g, 2, rows)
        a = w[2:3] * g + w[1:2] * g1 + w[0:1] * g2 + b_ref[...]
        sg = _sigmoid(a)
        dup_ref[...] = (dact * a * sg).astype(BF16)
        da = dact * up * sg * (1.0 + a * (1.0 - sg))
        dg = w[2:3] * da + w[1:2] * _shift_up(da, 1, rows, S) + w[0:1] * _shift_up(da, 2, rows, S)
        dg_ref[...] = dg.astype(BF16)

        @pl.when(pl.program_id(1) == 0)
        def _():
            dw_ref[...] = jnp.zeros_like(dw_ref)
            db_ref[...] = jnp.zeros_like(db_ref)

        dw_ref[0:1, :] += jnp.sum(da * g2, axis=0, keepdims=True)
        dw_ref[1:2, :] += jnp.sum(da * g1, axis=0, keepdims=True)
        dw_ref[2:3, :] += jnp.sum(da * g, axis=0, keepdims=True)
        db_ref[...] += jnp.sum(da, axis=0, keepdims=True)

    col = lambda j, b: (b, 0, j)
    return pl.pallas_call(
        body, name=name, grid=(nc, B),
        in_specs=[pl.BlockSpec((None, S, tc), col),
                  pl.BlockSpec((None, S, tc), lambda j, b: (b, 0, j + nc)),
                  pl.BlockSpec((3, tc), lambda j, b: (0, j)),
                  pl.BlockSpec((1, tc), lambda j, b: (0, j)),
                  pl.BlockSpec((None, S, tc), col)],
        out_specs=[pl.BlockSpec((None, S, tc), col), pl.BlockSpec((None, S, tc), col),
                   pl.BlockSpec((3, tc), lambda j, b: (0, j)), pl.BlockSpec((1, tc), lambda j, b: (0, j))],
        out_shape=[jax.ShapeDtypeStruct((B, S, D_FF), BF16), jax.ShapeDtypeStruct((B, S, D_FF), BF16),
                   jax.ShapeDtypeStruct((3, D_FF), F32), jax.ShapeDtypeStruct((1, D_FF), F32)],
        compiler_params=_params("parallel", "arbitrary"),
    )(gu, gu, conv_w, conv_b, dact)


HGRN_CPB = 4
HF = HGRN_HEADS * HGRN_DK


def _tri(n, upper=False):
    r = lax.broadcasted_iota(jnp.int32, (n, n), 0)
    c = lax.broadcasted_iota(jnp.int32, (n, n), 1)
    return (c >= r) if upper else (r >= c)


def _hs(h):
    return slice(h * HGRN_DK, (h + 1) * HGRN_DK)


def _cumsum_rows(tri_b, x):
    hi = x.astype(BF16)
    lo = (x - hi.astype(F32)).astype(BF16)
    return _nn(tri_b, hi) + _nn(tri_b, lo)


def _hgrn_pre(q, fz, lb, tril_b):
    qf = _silu(q)
    sg = _sigmoid(fz)
    f = lb + (1.0 - lb) * sg
    k = 1.0 - f
    b = _cumsum_rows(tril_b, jnp.log2(f))
    bref = b[CHUNK // 2:CHUNK // 2 + 1, :]
    blast = b[CHUNK - 1:CHUNK, :]
    e1 = jnp.exp2(b - bref)
    e2 = jnp.exp2(bref - b)
    e3 = e1 * jnp.exp2(bref)
    e4 = e2 * jnp.exp2(blast - bref)
    dec = jnp.exp2(blast)
    return sg, f, (e1, e2, e3, e4), qf * e1, k * e2, qf * e3, k * e4, dec


def _hgrn_fwd(zh, lb, gn, *, name):
    B, S, _ = zh.shape
    cpb = HGRN_CPB
    ts = cpb * CHUNK
    nblk = S // ts

    def body(z_ref, lb_ref, gn_ref, o_ref, st_ref, state):
        @pl.when(pl.program_id(1) == 0)
        def _():
            state[...] = jnp.zeros_like(state)

        H = HGRN_HEADS
        causal = _tri(CHUNK)
        tril_b = causal.astype(BF16)
        lb = lb_ref[...]
        for c in range(cpb):
            rows = slice(c * CHUNK, (c + 1) * CHUNK)
            q = z_ref[rows, 0:HF].astype(F32)
            fz = z_ref[rows, HF:2 * HF].astype(F32)
            v = z_ref[rows, 2 * HF:3 * HF]
            hg = z_ref[rows, 3 * HF:4 * HF].astype(F32)
            _, _, _, q_in, k_in, q_out, k_st, dec = _hgrn_pre(q, fz, lb, tril_b)
            q_in, k_in, q_out, k_st = (t.astype(BF16) for t in (q_in, k_in, q_out, k_st))
            a = [jnp.where(causal, _nt(q_in[:, _hs(h)], k_in[:, _hs(h)]), 0.0).astype(BF16) for h in range(H)]
            st = [state[h] for h in range(H)]
            for h in range(H):
                st_ref[c, h] = st[h]
            o = [_nn(a[h], v[:, _hs(h)]) + _nt(q_out[:, _hs(h)], st[h].astype(BF16)) for h in range(H)]
            for h in range(H):
                state[h] = st[h] * dec[:, _hs(h)] + _tn(v[:, _hs(h)], k_st[:, _hs(h)])
            gate = _silu(hg)
            for h in range(H):
                o_ref[rows, _hs(h)] = (_rms(o[h], gn_ref[...]) * gate[:, _hs(h)]).astype(BF16)

    return pl.pallas_call(
        body, name=name, grid=(B, nblk),
        in_specs=[pl.BlockSpec((None, ts, 4 * HF), lambda b, s: (b, s, 0)),
                  pl.BlockSpec((1, HF), lambda b, s: (0, 0)),
                  pl.BlockSpec((1, HGRN_DK), lambda b, s: (0, 0))],
        out_specs=[pl.BlockSpec((None, ts, HF), lambda b, s: (b, s, 0)),
                   pl.BlockSpec((None, cpb, HGRN_HEADS, HGRN_DK, HGRN_DK), lambda b, s: (b, s, 0, 0, 0))],
        out_shape=[jax.ShapeDtypeStruct((B, S, HF), BF16),
                   jax.ShapeDtypeStruct((B, S // CHUNK, HGRN_HEADS, HGRN_DK, HGRN_DK), F32)],
        scratch_shapes=[pltpu.VMEM((HGRN_HEADS, HGRN_DK, HGRN_DK), F32)],
        compiler_params=_params("arbitrary", "arbitrary"),
    )(zh, lb, gn)


def _hgrn_bwd(zh, lb, gn, states, doa, dz, *, name):
    B, S, _ = zh.shape
    cpb = HGRN_CPB
    ts = cpb * CHUNK
    nblk = S // ts
    rev = lambda b, s: (b, nblk - 1 - s, 0)

    def body(z_ref, lb_ref, gn_ref, st_ref, do_ref, dz_in, dz_ref, dlb_ref, dgn_ref, dstate):
        @pl.when(pl.program_id(1) == 0)
        def _():
            dstate[...] = jnp.zeros_like(dstate)

        @pl.when((pl.program_id(0) == 0) & (pl.program_id(1) == 0))
        def _():
            dlb_ref[...] = jnp.zeros_like(dlb_ref)
            dgn_ref[...] = jnp.zeros_like(dgn_ref)

        H = HGRN_HEADS
        cat = lambda xs: jnp.concatenate(xs, axis=1)
        causal = _tri(CHUNK)
        tril_b = causal.astype(BF16)
        triu_b = _tri(CHUNK, upper=True).astype(BF16)
        rowid = lax.broadcasted_iota(jnp.int32, (CHUNK, HF), 0)
        lb = lb_ref[...]
        gn = gn_ref[...]
        for c in reversed(range(cpb)):
            rows = slice(c * CHUNK, (c + 1) * CHUNK)
            q = z_ref[rows, 0:HF].astype(F32)
            fz = z_ref[rows, HF:2 * HF].astype(F32)
            v = z_ref[rows, 2 * HF:3 * HF]
            hg = z_ref[rows, 3 * HF:4 * HF].astype(F32)
            sg, f, (e1, e2, e3, e4), q_in, k_in, q_out, k_st, dec = _hgrn_pre(q, fz, lb, tril_b)
            q_in_b, k_in_b, q_out_b, k_st_b = (t.astype(BF16) for t in (q_in, k_in, q_out, k_st))
            a_b = [jnp.where(causal, _nt(q_in_b[:, _hs(h)], k_in_b[:, _hs(h)]), 0.0).astype(BF16) for h in range(H)]
            st = [st_ref[c, h] for h in range(H)]
            st_b = [t.astype(BF16) for t in st]
            o = [_nn(a_b[h], v[:, _hs(h)]) + _nt(q_out_b[:, _hs(h)], st_b[h]) for h in range(H)]
            dout = do_ref[rows, :].astype(F32)
            shg = _sigmoid(hg)
            gate = hg * shg
            do_l, dgn_acc = [], jnp.zeros_like(gn)
            for h in range(H):
                _, norm_vjp = jax.vjp(_rms, o[h], gn)
                d_o, d_gn = norm_vjp(dout[:, _hs(h)] * gate[:, _hs(h)])
                do_l.append(d_o)
                dgn_acc = dgn_acc + d_gn
            dgn_ref[...] += dgn_acc
            on = cat([_rms(o[h], gn) for h in range(H)])
            dhg = dout * on * shg * (1.0 + hg * (1.0 - shg))
            do_b = [t.astype(BF16) for t in do_l]
            dst = [dstate[h] for h in range(H)]
            dst_b = [t.astype(BF16) for t in dst]
            da_b = [jnp.where(causal, _nt(do_b[h], v[:, _hs(h)]), 0.0).astype(BF16) for h in range(H)]
            dv = cat([_tn(a_b[h], do_b[h]) + _nt(k_st_b[:, _hs(h)], dst_b[h]) for h in range(H)])
            dq_in = cat([_nn(da_b[h], k_in_b[:, _hs(h)]) for h in range(H)])
            dk_in = cat([_tn(da_b[h], q_in_b[:, _hs(h)]) for h in range(H)])
            dq_out = cat([_nn(do_b[h], st_b[h]) for h in range(H)])
            dk_st = cat([_nn(v[:, _hs(h)], dst_b[h]) for h in range(H)])
            ddec = cat([jnp.sum(st[h] * dst[h], axis=0, keepdims=True) for h in range(H)])
            for h in range(H):
                dstate[h] = dst[h] * dec[:, _hs(h)] + _tn(do_b[h], q_out_b[:, _hs(h)])
            t_qin = dq_in * q_in
            t_kin = dk_in * k_in
            t_kst = dk_st * k_st
            db = t_qin - t_kin + dq_out * q_out - t_kst
            dbref = jnp.sum(t_kin - t_qin, axis=0, keepdims=True)
            dblast = jnp.sum(t_kst, axis=0, keepdims=True) + ddec * dec
            db = db + jnp.where(rowid == CHUNK // 2, dbref, 0.0) + jnp.where(rowid == CHUNK - 1, dblast, 0.0)
            dlogf = _cumsum_rows(triu_b, db)
            dqf = dq_in * e1 + dq_out * e3
            dk = dk_in * e2 + dk_st * e4
            df = dlogf / f - dk
            dfz = df * (1.0 - lb) * sg * (1.0 - sg)
            dlb_ref[...] += jnp.sum(df * (1.0 - sg), axis=0, keepdims=True)
            sq = _sigmoid(q)
            dq = dqf * sq * (1.0 + q * (1.0 - sq))
            dz_ref[rows, 0:HF] = dq.astype(BF16)
            dz_ref[rows, HF:2 * HF] = dfz.astype(BF16)
            dz_ref[rows, 2 * HF:3 * HF] = dv.astype(BF16)
            dz_ref[rows, 3 * HF:4 * HF] = dhg.astype(BF16)

    return pl.pallas_call(
        body, name=name, grid=(B, nblk),
        in_specs=[pl.BlockSpec((None, ts, 4 * HF), rev),
                  pl.BlockSpec((1, HF), lambda b, s: (0, 0)),
                  pl.BlockSpec((1, HGRN_DK), lambda b, s: (0, 0)),
                  pl.BlockSpec((None, cpb, HGRN_HEADS, HGRN_DK, HGRN_DK), lambda b, s: (b, nblk - 1 - s, 0, 0, 0)),
                  pl.BlockSpec((None, ts, HF), rev),
                  ANY],
        out_specs=[pl.BlockSpec((None, ts, 4 * HF), rev),
                   pl.BlockSpec((1, HF), lambda b, s: (0, 0)),
                   pl.BlockSpec((1, HGRN_DK), lambda b, s: (0, 0))],
        out_shape=[jax.ShapeDtypeStruct(dz.shape, BF16),
                   jax.ShapeDtypeStruct((1, HF), F32),
                   jax.ShapeDtypeStruct((1, HGRN_DK), F32)],
        input_output_aliases={5: 0},
        scratch_shapes=[pltpu.VMEM((HGRN_HEADS, HGRN_DK, HGRN_DK), F32)],
        compiler_params=_params("arbitrary", "arbitrary"),
    )(zh, lb, gn, states, doa, dz)


KV_W = ATT_KV_HEADS * ATT_HD
ATT_SCALE = ATT_HD ** -0.5


def _rope(x, cos, sin, inverse=False):
    half = ROPE_DIM // 2
    outs = []
    for p in range(x.shape[1] // 128):
        xp = x[:, p * 128:(p + 1) * 128]
        lane = lax.broadcasted_iota(jnp.int32, xp.shape, 1) % ATT_HD
        sw = jnp.where(lane < half, pltpu.roll(xp, 128 - half, 1), pltpu.roll(xp, half, 1))
        outs.append(xp * cos - sw * sin if inverse else xp * cos + sw * sin)
    return outs[0] if len(outs) == 1 else jnp.concatenate(outs, axis=1)


PAIRS_PER_KV = ATT_GROUP // 2


def _swap_halves(x):
    return pltpu.roll(x, ATT_HD, 1)


def _kv_padded(t, low):
    sw = _swap_halves(t)
    zero = jnp.zeros_like(t)
    out = []
    for g in range(ATT_KV_HEADS):
        in_low, in_high = (t, sw) if g == 0 else (sw, t)
        out.append((jnp.where(low, in_low, zero).astype(BF16), jnp.where(low, zero, in_high).astype(BF16)))
    return out


def _swa_mask(first_block):
    qi = lax.broadcasted_iota(jnp.int32, (WINDOW, 2 * WINDOW), 0)
    mi = lax.broadcasted_iota(jnp.int32, (WINDOW, 2 * WINDOW), 1)
    band = (mi > qi) & (mi <= qi + WINDOW)
    return band & (jnp.logical_not(first_block) | (mi >= WINDOW))


def _swa_specs(nb):
    cur = lambda b, i: (b, i, 0)
    prev = lambda b, i: (b, jnp.maximum(i - 1, 0), 0)
    return cur, prev


def _swa_z_specs():
    q = pl.BlockSpec((None, WINDOW, W_AQ), lambda b, i: (b, i, O_AQ // W_AQ))
    kv_prev = pl.BlockSpec((None, WINDOW, W_AKV), lambda b, i: (b, jnp.maximum(i - 1, 0), O_AKV // W_AKV))
    kv_cur = pl.BlockSpec((None, WINDOW, W_AKV), lambda b, i: (b, i, O_AKV // W_AKV))
    return q, kv_prev, kv_cur


def _swa_fwd(z, cos, sin, sinks, *, name):
    B, S, _ = z.shape
    nb = S // WINDOW
    cur, prev = _swa_specs(nb)

    def body(q_ref, kvp_ref, kvc_ref, cp_ref, sp_ref, cc_ref, sc_ref, sink_ref, o_ref, lse_ref):
        cos_c, sin_c = cc_ref[...], sc_ref[...]
        q = (_rope(q_ref[...].astype(F32), cos_c, sin_c) * ATT_SCALE).astype(BF16)
        k = jnp.concatenate([_rope(kvp_ref[:, :KV_W].astype(F32), cp_ref[...], sp_ref[...]),
                             _rope(kvc_ref[:, :KV_W].astype(F32), cos_c, sin_c)], axis=0)
        v = jnp.concatenate([kvp_ref[:, KV_W:], kvc_ref[:, KV_W:]], axis=0).astype(F32)
        low = lax.broadcasted_iota(jnp.int32, k.shape, 1) < ATT_HD
        kpad = _kv_padded(k, low)
        vpad = _kv_padded(v, low)
        mask = _swa_mask(pl.program_id(1) == 0)
        lses = []
        for g in range(ATT_KV_HEADS):
            pairs = range(g * PAIRS_PER_KV, (g + 1) * PAIRS_PER_KV)
            keys = [(p, e) for p in pairs for e in (0, 1)]
            qp = {p: q[:, p * 128:(p + 1) * 128] for p in pairs}
            s = {pe: jnp.where(mask, _nt(qp[pe[0]], kpad[g][pe[1]]), NEG_INF) for pe in keys}
            pr = {}
            for pe in keys:
                sink = sink_ref[0, 2 * pe[0] + pe[1]]
                m = jnp.maximum(jnp.max(s[pe], axis=1, keepdims=True), sink)
                ex = jnp.exp(s[pe] - m)
                den = jnp.sum(ex, axis=1, keepdims=True) + jnp.exp(sink - m)
                pr[pe] = (ex * (1.0 / den)).astype(BF16)
                lses.append(m + jnp.log(den))
            for p in pairs:
                o_ref[:, p * 128:(p + 1) * 128] = (_nn(pr[p, 0], vpad[g][0]) + _nn(pr[p, 1], vpad[g][1])).astype(BF16)
        lse_ref[...] = jnp.concatenate(lses, axis=1)

    tab = lambda im: pl.BlockSpec((None, WINDOW, 128), im)
    return pl.pallas_call(
        body, name=name, grid=(B, nb),
        in_specs=[*_swa_z_specs(),
                  tab(prev), tab(prev), tab(cur), tab(cur),
                  pl.BlockSpec(memory_space=pltpu.SMEM)],
        out_specs=[pl.BlockSpec((None, WINDOW, D_MODEL), cur), pl.BlockSpec((None, WINDOW, ATT_HEADS), cur)],
        out_shape=[jax.ShapeDtypeStruct((B, S, D_MODEL), BF16), jax.ShapeDtypeStruct((B, S, ATT_HEADS), F32)],
        compiler_params=_params("parallel", "parallel"),
    )(z, z, z, cos, sin, cos, sin, sinks)


def _swa_bwd(z, cos, sin, sinks, lse, dob, dz, *, name):
    B, S, _ = z.shape
    nb = S // WINDOW
    cur, prev = _swa_specs(nb)

    def body(q_ref, kvp_ref, kvc_ref, cp_ref, sp_ref, cc_ref, sc_ref, sink_ref, lse_ref, do_ref, dz_in,
             dq_ref, dkc_ref, dkp_ref, dsink_ref):
        @pl.when((pl.program_id(0) == 0) & (pl.program_id(1) == 0))
        def _():
            dsink_ref[...] = jnp.zeros_like(dsink_ref)

        cos_c, sin_c, cos_p, sin_p = cc_ref[...], sc_ref[...], cp_ref[...], sp_ref[...]
        q = (_rope(q_ref[...].astype(F32), cos_c, sin_c) * ATT_SCALE).astype(BF16)
        k = jnp.concatenate([_rope(kvp_ref[:, :KV_W].astype(F32), cos_p, sin_p),
                             _rope(kvc_ref[:, :KV_W].astype(F32), cos_c, sin_c)], axis=0)
        v = jnp.concatenate([kvp_ref[:, KV_W:], kvc_ref[:, KV_W:]], axis=0).astype(F32)
        low = lax.broadcasted_iota(jnp.int32, k.shape, 1) < ATT_HD
        kpad = _kv_padded(k, low)
        vpad = _kv_padded(v, low)
        mask = _swa_mask(pl.program_id(1) == 0)
        lse = lse_ref[...]
        dq_parts, dk_sum, dv_sum, dsinks = [], [], [], []
        for g in range(ATT_KV_HEADS):
            pairs = range(g * PAIRS_PER_KV, (g + 1) * PAIRS_PER_KV)
            keys = [(p, e) for p in pairs for e in (0, 1)]
            qp = {p: q[:, p * 128:(p + 1) * 128] for p in pairs}
            dop = {p: do_ref[:, p * 128:(p + 1) * 128] for p in pairs}
            s = {pe: jnp.where(mask, _nt(qp[pe[0]], kpad[g][pe[1]]), NEG_INF) for pe in keys}
            dp = {pe: _nt(dop[pe[0]], vpad[g][pe[1]]) for pe in keys}
            pr, ds = {}, {}
            for pe in keys:
                h = 2 * pe[0] + pe[1]
                lse_h = lse[:, h:h + 1]
                pf = jnp.exp(s[pe] - lse_h)
                delta = jnp.sum(pf * dp[pe], axis=1, keepdims=True)
                ds[pe] = (pf * (dp[pe] - delta)).astype(BF16)
                pr[pe] = pf.astype(BF16)
                p_sink = jnp.exp(sink_ref[0, h] - lse_h)
                dsinks.append(-jnp.sum(p_sink * delta, axis=0, keepdims=True))
            for p in pairs:
                dq_parts.append((_nn(ds[p, 0], kpad[g][0]) + _nn(ds[p, 1], kpad[g][1])) * ATT_SCALE)
            x = [sum(_tn(ds[p, e], qp[p]) for p in pairs) for e in (0, 1)]
            y = [sum(_tn(pr[p, e], dop[p]) for p in pairs) for e in (0, 1)]
            zk = jnp.where(low, x[0], x[1])
            zv = jnp.where(low, y[0], y[1])
            dk_sum.append(zk + _swap_halves(zk))
            dv_sum.append(zv + _swap_halves(zv))
        dq_ref[...] = _rope(jnp.concatenate(dq_parts, axis=1), cos_c, sin_c, inverse=True).astype(BF16)
        dk = jnp.where(low, dk_sum[0], dk_sum[1])
        dv = jnp.where(low, dv_sum[0], dv_sum[1])
        dkp_ref[:, :KV_W] = _rope(dk[:WINDOW], cos_p, sin_p, inverse=True)
        dkp_ref[:, KV_W:] = dv[:WINDOW]
        dkc_ref[:, :KV_W] = _rope(dk[WINDOW:], cos_c, sin_c, inverse=True)
        dkc_ref[:, KV_W:] = dv[WINDOW:]
        dsink_ref[...] += jnp.concatenate(dsinks, axis=1)

    tab = lambda im: pl.BlockSpec((None, WINDOW, 128), im)
    return pl.pallas_call(
        body, name=name, grid=(B, nb),
        in_specs=[*_swa_z_specs(),
                  tab(prev), tab(prev), tab(cur), tab(cur),
                  pl.BlockSpec(memory_space=pltpu.SMEM),
                  pl.BlockSpec((None, WINDOW, ATT_HEADS), cur),
                  pl.BlockSpec((None, WINDOW, D_MODEL), cur),
                  ANY],
        out_specs=[_swa_z_specs()[0],
                   pl.BlockSpec((None, WINDOW, 2 * KV_W), cur), pl.BlockSpec((None, WINDOW, 2 * KV_W), cur),
                   pl.BlockSpec((1, ATT_HEADS), lambda b, i: (0, 0))],
        out_shape=[jax.ShapeDtypeStruct(dz.shape, BF16),
                   jax.ShapeDtypeStruct((B, S, 2 * KV_W), F32), jax.ShapeDtypeStruct((B, S, 2 * KV_W), F32),
                   jax.ShapeDtypeStruct((1, ATT_HEADS), F32)],
        input_output_aliases={10: 0},
        compiler_params=_params("arbitrary", "arbitrary"),
    )(z, z, z, cos, sin, cos, sin, sinks, lse, dob, dz)


def _swa_dkv_combine(dkv_cur, dkv_prev, dz, *, name):
    B, S, W = dkv_cur.shape

    def body(c_ref, p_ref, dz_in, o_ref):
        rows = lax.broadcasted_iota(jnp.int32, (S, W), 0)
        o_ref[...] = (c_ref[...] + _shift_up(p_ref[...], WINDOW, rows, S)).astype(BF16)

    seq = pl.BlockSpec((None, S, W), lambda b: (b, 0, 0))
    return pl.pallas_call(
        body, name=name, grid=(B,),
        in_specs=[seq, seq, ANY], out_specs=pl.BlockSpec((None, S, W), lambda b: (b, 0, O_AKV // W_AKV)),
        out_shape=jax.ShapeDtypeStruct(dz.shape, BF16),
        input_output_aliases={2: 0},
        compiler_params=_params("parallel"),
    )(dkv_cur, dkv_prev, dz)


def _rope_tables(positions):
    half = ROPE_DIM // 2
    inv = ROPE_THETA ** (-2.0 * jnp.arange(half, dtype=F32) / ROPE_DIM)
    ang = positions.astype(F32)[..., None] * inv
    c, s = jnp.cos(ang), jnp.sin(ang)
    pad = jnp.zeros(ang.shape[:-1] + (ATT_HD - ROPE_DIM,), F32)
    cos = jnp.concatenate([c, c, pad + 1.0], axis=-1)
    sin = jnp.concatenate([-s, s, pad], axis=-1)
    return jnp.tile(cos, (1, 1, 2)), jnp.tile(sin, (1, 1, 2))


def _lower_bound(lb_logits, *, name):
    def body(l_ref, o_ref):
        l = l_ref[...]
        e = jnp.exp(l - jnp.max(l, axis=0, keepdims=True))
        o_ref[...] = e[0:1] / jnp.sum(e, axis=0, keepdims=True)

    return pl.pallas_call(body, name=name, out_shape=jax.ShapeDtypeStruct((1, lb_logits.shape[1]), F32))(lb_logits)


W_ZH, W_GATES, W_AQ, W_AKV = 4 * HF, 2 * D_MODEL, ATT_HEADS * ATT_HD, 2 * KV_W
O_ZH, O_GATES, O_AQ, O_AKV = 0, W_ZH, W_ZH + W_GATES, W_ZH + W_GATES + W_AQ
W_IN = W_ZH + W_GATES + W_AQ + W_AKV


def _reorder_w_in(w_in_full):
    return jnp.concatenate([w_in_full[:, :W_ZH], w_in_full[:, W_ZH + W_AQ + W_AKV:], w_in_full[:, W_ZH:W_ZH + W_AQ + W_AKV]], axis=1)


def _reference_order_w_in(w):
    return jnp.concatenate([w[:, :W_ZH], w[:, O_AQ:], w[:, O_GATES:O_AQ]], axis=1)


def _local_step(x, positions, target, small, w_in, rest_weights, emit, start_token):
    B, S, D = x.shape
    T = B * S
    x2 = x.reshape(T, D)
    cos, sin = _rope_tables(positions)
    lb = _lower_bound(small["lb_logits"], name="lb_fwd")
    zero = lambda tok: tok[0:1, 0:1]

    u1 = _norm_cast(x2, small["norm1_g"] + zero(start_token), name="norm1")
    z = _matmul(u1, w_in, out_dtype=BF16, name="mm_z", tm=1024, tn=W_IN // 2)
    z3 = z.reshape(B, S, W_IN)
    oa, states = _hgrn_fwd(z3, lb, small["hgrn_norm_g"], name="hgrn_fwd")
    ob, lse = _swa_fwd(z3, cos, sin, small["attn_sinks"], name="swa_fwd")
    oa2 = oa.reshape(T, D)
    ob2 = ob.reshape(T, D)
    W = rest_weights("mix", ob)
    pa = _matmul(oa2, W["w_a"], out_dtype=BF16, name="mm_pa", tm=2048, tn=512)
    pb = _matmul(ob2, W["w_b"], out_dtype=BF16, name="mm_pb", tm=2048, tn=512)
    merged = _merge_fwd(z, pa, pb, name="merge_fwd")
    h = _matmul(merged, W["w_out"], addend=x2, name="mm_h", tm=2048, tn=512)
    u2 = _norm_cast(h, small["norm2_g"], name="norm2")
    W.update(rest_weights("ffn", u2))
    gu = _matmul(u2, W["w_ffn"], out_dtype=BF16, name="mm_gu", tm=2048, tn=512)
    gu3 = gu.reshape(B, S, 2 * D_FF)
    act = _conv_act_fwd(gu3, W["conv_w"], small["conv_b"], name="conv_act_fwd")
    act2 = act.reshape(T, D_FF)
    h2 = _matmul(act2, W["w_down"], addend=h, name="mm_h2", tm=1024, tn=1024)

    g = {}
    dh2, dh2b, g["final_g"], loss = _final_loss_bwd(h2, small["final_g"].reshape(1, D), target.reshape(T, D), name="final_loss_bwd")
    dact = _matmul(dh2b, W["w_down"], tb=True, out_dtype=BF16, name="mm_dact", tm=1024, tn=D_FF)
    dw_down_t = _matmul(dh2b, act2, ta=True, out_dtype=BF16, name="mm_dw_down", tm=1024, tn=256, tk=8192)
    dg_, dup, g["conv_w"], g["conv_b"] = _conv_act_bwd(gu3, W["conv_w"], small["conv_b"], dact.reshape(B, S, D_FF), name="conv_act_bwd")
    dg2 = dg_.reshape(T, D_FF)
    dup2 = dup.reshape(T, D_FF)
    du2 = _matmul(dg2, W["w_ffn"], tb=True, name="mm_du2_g", tm=1024, tn=1024, b_koff=0)
    du2 = _matmul(dup2, W["w_ffn"], tb=True, addend=du2, name="mm_du2_u", tm=1024, tn=1024, b_koff=1)
    dw_ffn = _matmul(u2, dg2, ta=True, out_dtype=BF16, into=lax.empty((D, 2 * D_FF), BF16), o_noff=0, name="mm_dw_ffn_g", tm=1024, tn=256, tk=8192)
    dw_ffn = _matmul(u2, dup2, ta=True, out_dtype=BF16, into=dw_ffn, o_noff=D_FF // 256, name="mm_dw_ffn_u", tm=1024, tn=256, tk=8192)
    tok = emit("ffn", dict(w_ffn=dw_ffn, w_down=dw_down_t.T))
    dh, dhb, g["norm2_g"] = _norm_bwd_add(h, small["norm2_g"] + zero(tok), du2, dh2, name="norm2_bwd")
    dmerged = _matmul(dhb, W["w_out"], tb=True, out_dtype=BF16, name="mm_dmerged", tm=2048, tn=512)
    dw_out = _matmul(merged, dhb, ta=True, out_dtype=BF16, name="mm_dw_out", tm=1024, tn=1024, tk=2048)
    dz, dpa, dpb = _merge_bwd(z, pa, pb, dmerged, lax.empty((T, W_IN), BF16), name="merge_bwd")
    doa =_matmul(dpa, W["w_a"], tb=True, out_dtype=BF16, name="mm_doa", tm=2048, tn=512)
    dw_a = _matmul(oa2, dpa, ta=True, out_dtype=BF16, name="mm_dw_a", tm=1024, tn=1024, tk=2048)
    dob = _matmul(dpb, W["w_b"], tb=True, out_dtype=BF16, name="mm_dob", tm=2048, tn=512)
    dw_b = _matmul(ob2, dpb, ta=True, out_dtype=BF16, name="mm_dw_b", tm=1024, tn=1024, tk=2048)
    tok = emit("mix", dict(w_out=dw_out, w_a=dw_a, w_b=dw_b))
    dz3, dkv_cur, dkv_prev, dsinks = _swa_bwd(z3, cos, sin, small["attn_sinks"] + zero(tok), lse, dob.reshape(B, S, D),
                                              dz.reshape(B, S, W_IN), name="swa_bwd")
    dz3 = _swa_dkv_combine(dkv_cur, dkv_prev, dz3, name="swa_dkv")
    g["attn_sinks"] = dsinks
    dz3, g["lb"], g["hgrn_norm_g"] = _hgrn_bwd(z3, lb, small["hgrn_norm_g"], states, doa.reshape(B, S, D), dz3, name="hgrn_bwd")
    dz = dz3.reshape(T, W_IN)
    dw_in = _matmul(u1, dz, ta=True, out_dtype=BF16, name="mm_dw_in", tm=1024, tn=256, tk=8192)
    tok = emit("in", dict(w_in=dw_in))
    du1 = _matmul(dz, w_in, tb=True, after=tok, name="mm_du1", tm=1024, tn=512)
    dx, _, g["norm1_g"] = _norm_bwd_add(x2, small["norm1_g"], du1, dh, name="norm1_bwd")
    g["lb_logits"] = _lb_bwd(g.pop("lb"), lb, name="lb_bwd")
    return loss, dx.reshape(B, S, D), g


def _my_place():
    return lax.axis_index("x"), lax.axis_index("y"), lax.axis_index("c")


def _gather_blocks(x_ref, out_ref, send_sems, recv_sems, local_sem):
    x, y, c = _my_place()
    me, sibling = (x, y, c), (x, y, 1 - c)
    chips = [(1 - x, y), (x, 1 - y), (1 - x, 1 - y)]

    def slot(px, py, pc):
        return out_ref.at[4 * px + 2 * py + pc]

    def copy(k, block, to, src=None):
        return pltpu.make_async_remote_copy(
            src_ref=slot(*block) if src is None else src, dst_ref=slot(*block),
            send_sem=send_sems.at[k], recv_sem=recv_sems.at[k], device_id=to, device_id_type=MESH)

    mine = pltpu.make_async_copy(x_ref, slot(*me), local_sem)
    mine.start()
    first = [copy(0, me, sibling, src=x_ref)]
    first += [copy(1 + j, me, (*chip, c), src=x_ref) for j, chip in enumerate(chips)]
    for cp in first:
        cp.start()
    passed = [copy(4 + j, (*chip, c), sibling) for j, chip in enumerate(chips)]
    for j, chip in enumerate(chips):
        copy(1 + j, (*chip, c), me).wait_recv()
        passed[j].start()
    copy(0, sibling, me).wait_recv()
    for j, chip in enumerate(chips):
        copy(4 + j, (*chip, 1 - c), me).wait_recv()
    for cp in first + passed:
        cp.wait_send()
    mine.wait()


GATHER_SEMS = [pltpu.SemaphoreType.DMA((7,)), pltpu.SemaphoreType.DMA((7,)), pltpu.SemaphoreType.DMA]


def _all_gather(blk, *, name):
    return pl.pallas_call(
        _gather_body_fn(), name=name,
        out_shape=jax.ShapeDtypeStruct((N_DEV,) + blk.shape, blk.dtype),
        in_specs=[ANY], out_specs=ANY,
        scratch_shapes=GATHER_SEMS,
    )(blk)


def _gather_body_fn():
    def body(x_ref, out_ref, send_sems, recv_sems, local_sem):
        _gather_blocks(x_ref, out_ref, send_sems, recv_sems, local_sem)
    return body


SLAB_W = 1152
SMALL_SHAPES = dict(norm1_g=(1, D_MODEL), lb_logits=(2, HGRN_HEADS * HGRN_DK), hgrn_norm_g=(1, HGRN_DK), attn_sinks=(1, ATT_HEADS),
                    norm2_g=(1, D_MODEL), conv_b=(1, D_FF), final_g=(1, D_MODEL))
CONVW_BLK = D_FF // N_DEV
CONVW_STRIDE = SLAB_W // 3


def _slab_layout():
    layout, r = {}, 0
    for nm, (nr, w) in SMALL_SHAPES.items():
        layout[nm] = []
        for i in range(nr):
            for c0 in range(0, w, SLAB_W):
                layout[nm].append((r, i, c0, min(SLAB_W, w - c0)))
                r += 1
    return layout, r


SMALL_ROWS, _N_SMALL_ROWS = _slab_layout()
CONV_ROW0 = -(-_N_SMALL_ROWS // 8) * 8
LOSS_ROW = CONV_ROW0 + N_DEV
SLAB_ROWS = LOSS_ROW + 8


def _small_step(grads, g_conv_w, loss, params, moments, variances, dev, *, name):
    names = list(SMALL_ROWS)
    n = len(names)

    def body(dev_ref, *refs):
        g_refs = dict(zip(names, refs[:n]))
        gc_ref, loss_ref = refs[n], refs[n + 1]
        base = n + 2
        w_refs, m_refs, v_refs = (dict(zip(names + ["conv_w"], refs[base + i * (n + 1):base + (i + 1) * (n + 1)])) for i in range(3))
        o = base + 3 * (n + 1)
        gath_ref, loss_out = refs[o], refs[o + 1]
        outs = {nm: refs[o + 2 + 4 * i:o + 6 + 4 * i] for i, nm in enumerate(names + ["conv_w"])}
        slab, total, send_sems, recv_sems, local_sem = refs[-5:]

        slab[...] = jnp.zeros_like(slab)
        for nm, pieces in SMALL_ROWS.items():
            for r, i, c0, w in pieces:
                slab[r:r + 1, 0:w] = g_refs[nm][i:i + 1, c0:c0 + w]
        for p in range(N_DEV):
            for j in range(3):
                slab[CONV_ROW0 + p:CONV_ROW0 + p + 1, j * CONVW_STRIDE:j * CONVW_STRIDE + CONVW_BLK] = gc_ref[j:j + 1, p * CONVW_BLK:(p + 1) * CONVW_BLK]
        slab[LOSS_ROW:LOSS_ROW + 1, 0:1] = loss_ref[...]
        _gather_blocks(slab, gath_ref, send_sems, recv_sems, local_sem)
        acc = gath_ref[0]
        for p in range(1, N_DEV):
            acc = acc + gath_ref[p]
        total[...] = acc
        loss_out[...] = total[LOSS_ROW:LOSS_ROW + 1, 0:1]

        def update(nm, g, i, c0, w):
            at = (slice(i, i + 1), slice(c0, c0 + w))
            d, mn, vn = _adamw_math(w_refs[nm][at], g, m_refs[nm][at], v_refs[nm][at])
            for ref, val in zip(outs[nm], (g, d, mn, vn)):
                ref[at] = val

        for nm, pieces in SMALL_ROWS.items():
            for r, i, c0, w in pieces:
                update(nm, total[r:r + 1, 0:w], i, c0, w)
        conv_rows = total[CONV_ROW0:CONV_ROW0 + N_DEV, :]
        rowid = lax.broadcasted_iota(jnp.int32, conv_rows.shape, 0)
        mine = jnp.sum(jnp.where(rowid == dev_ref[0], conv_rows, 0.0), axis=0, keepdims=True)
        for j in range(3):
            update("conv_w", mine[:, j * CONVW_STRIDE:j * CONVW_STRIDE + CONVW_BLK], j, 0, CONVW_BLK)

    order = names + ["conv_w"]
    ins = [grads[nm] for nm in names] + [g_conv_w, loss]
    for d in (params, moments, variances):
        ins += [d[nm] for nm in order]
    vmem = pl.BlockSpec(memory_space=pltpu.VMEM)
    out_shape = [jax.ShapeDtypeStruct((N_DEV, SLAB_ROWS, SLAB_W), F32), jax.ShapeDtypeStruct((1, 1), F32)]
    for nm in order:
        out_shape += [jax.ShapeDtypeStruct(params[nm].shape, F32)] * 4
    res = pl.pallas_call(
        body, name=name,
        grid_spec=pltpu.PrefetchScalarGridSpec(
            num_scalar_prefetch=1, grid=(1,),
            in_specs=[vmem] * len(ins), out_specs=[vmem] * len(out_shape),
            scratch_shapes=[pltpu.VMEM((SLAB_ROWS, SLAB_W), F32), pltpu.VMEM((SLAB_ROWS, SLAB_W), F32)] + GATHER_SEMS),
        out_shape=out_shape,
    )(dev, *ins)
    return res[1], {nm: tuple(res[2 + 4 * i:6 + 4 * i]) for i, nm in enumerate(order)}


HBM_SPEC = pl.BlockSpec(memory_space=pltpu.HBM)
SEM_SPEC = pl.BlockSpec(memory_space=pltpu.SEMAPHORE)
DATAFLOW_EFFECT = pltpu.SideEffectType.DATAFLOW_SIDE_EFFECTING
N_PEERS = N_DEV - 1


def _peers(x, y, c):
    return [(1 - x if r & 4 else x, 1 - y if r & 2 else y, 1 - c if r & 1 else c) for r in range(1, N_DEV)]


def _exchange_start(srcs, scatter, *, name):
    n = len(srcs)
    lands = [lax.empty(a.shape if scatter else (N_DEV,) + a.shape, a.dtype) for a in srcs]

    def body(*refs):
        src_refs, land_refs = refs[:n], refs[n:2 * n]
        send_sems, recv_sems, token = refs[2 * n], refs[2 * n + 1], refs[-1]
        x, y, c = _my_place()
        me = 4 * x + 2 * y + c
        for i in range(n):
            for r, (tx, ty, tc) in enumerate(_peers(x, y, c)):
                src = src_refs[i].at[4 * tx + 2 * ty + tc] if scatter else src_refs[i]
                pltpu.make_async_remote_copy(
                    src_ref=src, dst_ref=land_refs[i].at[me], send_sem=send_sems.at[N_PEERS * i + r],
                    recv_sem=recv_sems.at[N_PEERS * i + r], device_id=(tx, ty, tc), device_id_type=MESH).start()
        token[...] = jnp.zeros_like(token)

    thru = [pltpu.HBM(a.shape, a.dtype) for a in list(srcs) + lands]
    res = pl.pallas_call(
        body, name=name,
        out_shape=(pltpu.SemaphoreType.DMA((N_PEERS * n,)), pltpu.SemaphoreType.DMA((N_PEERS * n,)), *thru,
                   jax.ShapeDtypeStruct((8, 128), F32)),
        in_specs=[HBM_SPEC] * (2 * n),
        out_specs=(SEM_SPEC, SEM_SPEC, *([HBM_SPEC] * (2 * n)), pl.BlockSpec(memory_space=pltpu.VMEM)),
        input_output_aliases={i: 2 + i for i in range(2 * n)},
        compiler_params=pltpu.CompilerParams(has_side_effects=DATAFLOW_EFFECT),
    )(*[pltpu.with_memory_space_constraint(a, pltpu.HBM) for a in list(srcs) + lands])
    return (res[0], res[1], list(res[2:2 + n]), list(res[2 + n:2 + 2 * n]), scatter), res[-1]


def _exchange_wait(handle, after, *, name):
    send_sems, recv_sems, srcs, lands, scatter = handle
    n = len(srcs)

    def body(*refs):
        src_refs, land_refs = refs[:n], refs[n:2 * n]
        send_sems, recv_sems = refs[2 * n], refs[2 * n + 1]
        x, y, c = _my_place()
        for i in range(n):
            for r in range(N_PEERS):
                src = src_refs[i].at[0] if scatter else src_refs[i]
                cp = pltpu.make_async_remote_copy(
                    src_ref=src, dst_ref=land_refs[i].at[0], send_sem=send_sems.at[N_PEERS * i + r],
                    recv_sem=recv_sems.at[N_PEERS * i + r], device_id=(x, y, c), device_id_type=MESH)
                cp.wait_send()
                cp.wait_recv()

    thru = [pltpu.HBM(a.shape, a.dtype) for a in srcs + lands]
    res = pl.pallas_call(
        body, name=name, out_shape=tuple(thru),
        in_specs=[HBM_SPEC] * (2 * n) + [SEM_SPEC, SEM_SPEC, ANY], out_specs=tuple([HBM_SPEC] * (2 * n)),
        input_output_aliases={i: i for i in range(2 * n)},
        compiler_params=pltpu.CompilerParams(has_side_effects=DATAFLOW_EFFECT),
    )(*srcs, *lands, send_sems, recv_sems, after)
    return list(res[:n]), list(res[n:])


def _with_own(land, own, me):
    return lax.dynamic_update_index_in_dim(land, own, me, 0)


def _adamw_math(w, g, m, v):
    m = ADAM_B1 * m + (1.0 - ADAM_B1) * g
    v = ADAM_B2 * v + (1.0 - ADAM_B2) * (g * g)
    m_hat = m / (1.0 - ADAM_B1 ** ADAM_STEP)
    v_hat = v / (1.0 - ADAM_B2 ** ADAM_STEP)
    delta = -ADAM_LR * (m_hat / (jnp.sqrt(v_hat) + ADAM_EPS) + ADAM_WD * w)
    return delta, m, v


def _adamw_sum(parts, w, m, v, *, name):
    shape = w.shape
    R, n = shape[-2], shape[-1]
    w, m, v = (t.reshape(R, n) for t in (w, m, v))
    tr = _pick(R, (256, 176, 128))

    def body(p_ref, w_ref, m_ref, v_ref, g_ref, d_ref, mo_ref, vo_ref):
        g = p_ref[0].astype(F32)
        for p in range(1, N_DEV):
            g = g + p_ref[p].astype(F32)
        d, mn, vn = _adamw_math(w_ref[...], g, m_ref[...], v_ref[...])
        g_ref[...] = g
        d_ref[...] = d
        mo_ref[...] = mn
        vo_ref[...] = vn

    row = pl.BlockSpec((tr, n), lambda i: (i, 0))
    outs = pl.pallas_call(
        body, name=name, grid=(R // tr,),
        in_specs=[pl.BlockSpec((N_DEV, tr, n), lambda i: (0, i, 0)), row, row, row],
        out_specs=[row, row, row, row],
        out_shape=[jax.ShapeDtypeStruct((R, n), F32)] * 4,
        compiler_params=_params("parallel"),
    )(parts, w, m, v)
    return [t.reshape(shape) for t in outs]


def _lb_bwd(dlb, lb, *, name):
    def body(d_ref, lb_ref, o_ref):
        t = d_ref[...] * lb_ref[...] * (1.0 - lb_ref[...])
        o_ref[0:1, :] = t
        o_ref[1:2, :] = -t

    return pl.pallas_call(body, name=name, out_shape=jax.ShapeDtypeStruct((2, lb.shape[1]), F32))(dlb, lb)


DOWN_BLK, ROW_BLK = D_FF // N_DEV, D_MODEL // N_DEV
CONV_BITS_SHAPE = (16, 256)


def _cols_from_blocks(blocks):
    n, rows, width = blocks.shape
    return blocks.transpose(1, 0, 2).reshape(rows, n * width)


def _blocks_from_cols(full):
    rows, cols = full.shape
    return full.reshape(rows, N_DEV, cols // N_DEV).transpose(1, 0, 2)


def kernel(x, positions, norm1_g, w_in, lb_logits, hgrn_norm_g, w_a, attn_sinks, w_b, w_out, norm2_g, w_ffn_in, conv_w, conv_b, w_down, final_g, loss_target, m_norm1_g, m_w_in, m_lb_logits, m_hgrn_norm_g, m_w_a, m_attn_sinks, m_w_b, m_w_out, m_norm2_g, m_w_ffn_in, m_conv_w, m_conv_b, m_w_down, m_final_g, v_norm1_g, v_w_in, v_lb_logits, v_hgrn_norm_g, v_w_a, v_attn_sinks, v_w_b, v_w_out, v_norm2_g, v_w_ffn_in, v_conv_w, v_conv_b, v_w_down, v_final_g):
    xi, yi, ci = _my_place()
    dev = 4 * xi + 2 * yi + ci

    w_in_blocks = _all_gather(w_in[0].astype(BF16), name="ag_w_in")
    conv_bits = lax.bitcast_convert_type(conv_w, BF16).reshape(-1)
    conv_bits = jnp.pad(conv_bits, (0, CONV_BITS_SHAPE[0] * CONV_BITS_SHAPE[1] - conv_bits.shape[0])).reshape(CONV_BITS_SHAPE)
    gather_handles = {}
    gather_handles["mix"], tok_mix = _exchange_start([w_a[0].astype(BF16), w_b[0].astype(BF16), w_out[0].astype(BF16)], False, name="ag_mix_start")
    gather_handles["ffn"], tok_ffn = _exchange_start([w_ffn_in[0].astype(BF16), w_down[0].astype(BF16), conv_bits], False, name="ag_ffn_start")
    start_token = tok_mix + tok_ffn

    def rest_weights(group, after):
        own, lands = _exchange_wait(gather_handles[group], after, name="ag_" + group + "_wait")
        full = [_with_own(l, o, dev) for l, o in zip(lands, own)]
        if group == "mix":
            return dict(zip(("w_a", "w_b", "w_out"), [t.reshape(D_MODEL, D_MODEL) for t in full]))
        bits = full[2].reshape(N_DEV, -1)[:, :3 * CONVW_BLK * 2].reshape(N_DEV, 3, CONVW_BLK, 2)
        return dict(w_ffn=_cols_from_blocks(full[0]), w_down=full[1].reshape(D_FF, D_MODEL),
                    conv_w=_cols_from_blocks(lax.bitcast_convert_type(bits, F32)))

    handles = {}

    def emit(group, gr):
        if group == "ffn":
            srcs = [_blocks_from_cols(gr["w_ffn"]), gr["w_down"].reshape(N_DEV, DOWN_BLK, D_MODEL)]
        elif group == "mix":
            srcs = [gr[n].reshape(N_DEV, ROW_BLK, D_MODEL) for n in ("w_out", "w_a", "w_b")]
        else:
            srcs = [_blocks_from_cols(_reference_order_w_in(gr["w_in"]))]
        handles[group], token = _exchange_start(srcs, True, name="rs_" + group + "_start")
        return token

    small = dict(norm1_g=norm1_g, lb_logits=lb_logits, hgrn_norm_g=hgrn_norm_g, attn_sinks=attn_sinks, norm2_g=norm2_g,
                 conv_b=conv_b, final_g=final_g)
    w_in_full = _reorder_w_in(_cols_from_blocks(w_in_blocks))
    loss, grad_x, g = _local_step(x, positions, loss_target, small, w_in_full, rest_weights, emit, start_token)

    def parts_of(group, after):
        srcs, lands = _exchange_wait(handles[group], after, name="rs_" + group + "_wait")
        return [_with_own(l, lax.dynamic_index_in_dim(s, dev, 0, keepdims=False), dev) for s, l in zip(srcs, lands)]

    p_ffn, p_down = parts_of("ffn", grad_x)
    p_out, p_a, p_b = parts_of("mix", grad_x)
    (p_in,) = parts_of("in", grad_x)
    big = dict(
        w_in=_adamw_sum(p_in, w_in, m_w_in, v_w_in, name="adamw_w_in"),
        w_a=_adamw_sum(p_a, w_a, m_w_a, v_w_a, name="adamw_w_a"),
        w_b=_adamw_sum(p_b, w_b, m_w_b, v_w_b, name="adamw_w_b"),
        w_out=_adamw_sum(p_out, w_out, m_w_out, v_w_out, name="adamw_w_out"),
        w_ffn_in=_adamw_sum(p_ffn, w_ffn_in, m_w_ffn_in, v_w_ffn_in, name="adamw_w_ffn_in"),
        w_down=_adamw_sum(p_down, w_down, m_w_down, v_w_down, name="adamw_w_down"),
    )

    row = lambda t: t.reshape(1, -1) if t.ndim == 1 else t
    shard = lambda t: t.reshape(3, CONVW_BLK)
    sm_g = {nm: g[nm] for nm in SMALL_ROWS}
    sm_w = dict(norm1_g=norm1_g, lb_logits=lb_logits, hgrn_norm_g=hgrn_norm_g, attn_sinks=attn_sinks, norm2_g=norm2_g,
                conv_b=conv_b, final_g=row(final_g), conv_w=shard(conv_w))
    sm_m = dict(norm1_g=m_norm1_g, lb_logits=m_lb_logits, hgrn_norm_g=m_hgrn_norm_g, attn_sinks=m_attn_sinks, norm2_g=m_norm2_g,
                conv_b=m_conv_b, final_g=row(m_final_g), conv_w=shard(m_conv_w))
    sm_v = dict(norm1_g=v_norm1_g, lb_logits=v_lb_logits, hgrn_norm_g=v_hgrn_norm_g, attn_sinks=v_attn_sinks, norm2_g=v_norm2_g,
                conv_b=v_conv_b, final_g=row(v_final_g), conv_w=shard(v_conv_w))
    loss_total, sm_out = _small_step(sm_g, g["conv_w"], loss, sm_w, sm_m, sm_v, dev.astype(jnp.int32).reshape(1), name="small_step")
    shapes = dict(final_g=final_g.shape, conv_w=conv_w.shape)

    names = ("norm1_g", "w_in", "lb_logits", "hgrn_norm_g", "w_a", "attn_sinks", "w_b", "w_out", "norm2_g", "w_ffn_in", "conv_w", "conv_b", "w_down", "final_g")
    outs = [loss_total.reshape(()), grad_x]
    for kind in range(4):
        outs += [big[n][kind] if n in big else sm_out[n][kind].reshape(shapes.get(n, sm_out[n][kind].shape)) for n in names]
    return tuple(outs)
```

```python
import functools

import jax
import jax.numpy as jnp
from jax import lax
from jax.experimental import pallas as pl
from jax.experimental.pallas import tpu as pltpu

F32 = jnp.float32
BF16 = jnp.bfloat16

D_MODEL = 1024
HGRN_HEADS = 8
HGRN_DK = 128
CHUNK = 64
ATT_HEADS = 16
ATT_KV_HEADS = 2
ATT_HD = 64
ATT_GROUP = ATT_HEADS // ATT_KV_HEADS
WINDOW = 128
ROPE_DIM = ATT_HD // 4
ROPE_THETA = 500000.0
D_FF = 2816
EPS = 1e-6
NEG_INF = -1e30
N_DEV = 8

ADAM_LR = 0.001
ADAM_B1 = 0.9
ADAM_B2 = 0.999
ADAM_EPS = 1e-08
ADAM_WD = 0.01
ADAM_STEP = 10

MESH = pl.DeviceIdType.MESH
ANY = pl.BlockSpec(memory_space=pl.ANY)


def _pick(n, cands):
    for c in cands:
        if n % c == 0:
            return c
    return n


def _sigmoid(x):
    return 0.5 * jnp.tanh(0.5 * x) + 0.5


def _silu(x):
    hx = 0.5 * x
    return hx * jnp.tanh(hx) + hx


def _rms(x, g):
    return x * lax.rsqrt(jnp.mean(x * x, axis=-1, keepdims=True) + EPS) * g


def _dot(a, b, dims):
    return lax.dot_general(a, b, (dims, ((), ())), preferred_element_type=F32)


def _nn(a, b):
    return _dot(a, b, ((1,), (0,)))


def _nt(a, b):
    return _dot(a, b, ((1,), (1,)))


def _tn(a, b):
    return _dot(a, b, ((0,), (0,)))


def _params(*sem):
    return pltpu.CompilerParams(dimension_semantics=sem, vmem_limit_bytes=56 * 1024 * 1024)


def _matmul(a, b, *, ta=False, tb=False, out_dtype=F32, addend=None, after=None, into=None, o_noff=0, name, tm, tn, tk=None,
            n_extent=None, b_koff=0, b_noff=0):
    M, K = (a.shape[1], a.shape[0]) if ta else a.shape
    N = n_extent or (b.shape[0] if tb else b.shape[1])
    tm, tn, tk = min(tm, M), min(tn, N), min(tk or K, K)
    assert M % tm == 0 and N % tn == 0 and K % tk == 0, (name, M, N, K, tm, tn, tk)
    nk = K // tk
    use_scratch = nk > 1 and out_dtype != F32
    grid = (M // tm, N // tn, nk)
    a_spec = pl.BlockSpec((tk, tm), lambda i, j, k: (k, i)) if ta else pl.BlockSpec((tm, tk), lambda i, j, k: (i, k))
    b_spec = pl.BlockSpec((tn, tk), lambda i, j, k: (j + b_noff, k + b_koff)) if tb else pl.BlockSpec((tk, tn), lambda i, j, k: (k + b_koff, j + b_noff))
    o_spec = pl.BlockSpec((tm, tn), lambda i, j, k: (i, j))
    dims = ((0 if ta else 1,), (1 if tb else 0,))
    has_add = addend is not None

    n_in = 2 + has_add + (after is not None) + (into is not None)

    def body(*refs):
        a_ref, b_ref = refs[:2]
        c_ref = refs[2] if has_add else None
        o_ref = refs[n_in]
        part = _dot(a_ref[...], b_ref[...], dims)
        if nk == 1:
            if has_add:
                part = part + c_ref[...].astype(F32)
            o_ref[...] = part.astype(out_dtype)
        else:
            acc_ref = refs[-1] if use_scratch else o_ref
            k = pl.program_id(2)

            @pl.when(k == 0)
            def _():
                acc_ref[...] = part + c_ref[...].astype(F32) if has_add else part

            @pl.when(k > 0)
            def _():
                acc_ref[...] += part

            if use_scratch:
                @pl.when(k == nk - 1)
                def _():
                    o_ref[...] = acc_ref[...].astype(out_dtype)

    in_specs = [a_spec, b_spec] + ([o_spec] if has_add else [])
    args = (a, b) + ((addend,) if has_add else ())
    if after is not None:
        in_specs.append(pl.BlockSpec(after.shape, lambda i, j, k: (0, 0)))
        args += (after,)
    aliases = {}
    if into is not None:
        in_specs.append(ANY)
        args += (into,)
        aliases = {len(args) - 1: 0}
        o_spec = pl.BlockSpec((tm, tn), lambda i, j, k: (i, j + o_noff))
    return pl.pallas_call(
        body,
        name=name,
        grid=grid,
        in_specs=in_specs,
        out_specs=o_spec,
        out_shape=jax.ShapeDtypeStruct((M, N) if into is None else into.shape, out_dtype),
        input_output_aliases=aliases,
        scratch_shapes=[pltpu.VMEM((tm, tn), F32)] if use_scratch else [],
        compiler_params=_params("parallel", "parallel", "arbitrary"),
    )(*args)


def _row_spec(tm, n):
    return pl.BlockSpec((tm, n), lambda i: (i, 0))


def _full_spec(shape):
    return pl.BlockSpec(shape, lambda i: tuple(0 for _ in shape))


def _norm_cast(x, g, *, name):
    T, D = x.shape
    tm = _pick(T, (512, 256, 128))

    def body(x_ref, g_ref, u_ref):
        u_ref[...] = _rms(x_ref[...], g_ref[...]).astype(BF16)

    return pl.pallas_call(
        body, name=name, grid=(T // tm,),
        in_specs=[_row_spec(tm, D), _full_spec((1, D))],
        out_specs=_row_spec(tm, D),
        out_shape=jax.ShapeDtypeStruct((T, D), BF16),
        compiler_params=_params("parallel"),
    )(x, g)


def _norm_bwd_add(x, g, du, dres, *, with_bf16=True, name):
    T, D = x.shape
    tm = _pick(T, (512, 256, 128))

    def body(x_ref, g_ref, du_ref, dr_ref, dx_ref, *rest):
        dg_ref = rest[-1]
        _, vjp = jax.vjp(_rms, x_ref[...], g_ref[...])
        dx, dg = vjp(du_ref[...].astype(F32))
        dx = dx + dr_ref[...]
        dx_ref[...] = dx
        if with_bf16:
            rest[0][...] = dx.astype(BF16)

        @pl.when(pl.program_id(0) == 0)
        def _():
            dg_ref[...] = jnp.zeros_like(dg_ref)

        dg_ref[...] += dg

    row = _row_spec(tm, D)
    return pl.pallas_call(
        body, name=name, grid=(T // tm,),
        in_specs=[row, _full_spec((1, D)), row, row],
        out_specs=[row] + ([row] if with_bf16 else []) + [_full_spec((1, D))],
        out_shape=[jax.ShapeDtypeStruct((T, D), F32)] + ([jax.ShapeDtypeStruct((T, D), BF16)] if with_bf16 else []) + [jax.ShapeDtypeStruct((1, D), F32)],
        compiler_params=_params("arbitrary"),
    )(x, g, du, dres)


def _final_loss_bwd(h2, g, target, *, name):
    T, D = h2.shape
    tm = _pick(T, (512, 256, 128))

    def body(h_ref, g_ref, t_ref, dx_ref, dxb_ref, dg_ref, loss_ref):
        y, vjp = jax.vjp(_rms, h_ref[...], g_ref[...])
        err = y - t_ref[...]
        dx, dg = vjp(err * (1.0 / D))
        dx_ref[...] = dx
        dxb_ref[...] = dx.astype(BF16)

        @pl.when(pl.program_id(0) == 0)
        def _():
            dg_ref[...] = jnp.zeros_like(dg_ref)
            loss_ref[...] = jnp.zeros_like(loss_ref)

        dg_ref[...] += dg
        loss_ref[...] += (0.5 / D) * jnp.sum(jnp.sum(err * err, axis=1, keepdims=True), axis=0, keepdims=True)

    return pl.pallas_call(
        body, name=name, grid=(T // tm,),
        in_specs=[_row_spec(tm, D), _full_spec((1, D)), _row_spec(tm, D)],
        out_specs=[_row_spec(tm, D), _row_spec(tm, D), _full_spec((1, D)), _full_spec((1, 1))],
        out_shape=[jax.ShapeDtypeStruct((T, D), F32), jax.ShapeDtypeStruct((T, D), BF16), jax.ShapeDtypeStruct((1, D), F32), jax.ShapeDtypeStruct((1, 1), F32)],
        compiler_params=_params("arbitrary"),
    )(h2, g, target)


def _merge_fn(gates, a, b):
    ga = gates[:, :D_MODEL].astype(F32)
    gb = gates[:, D_MODEL:].astype(F32)
    return _sigmoid(ga) * a.astype(F32) + _sigmoid(gb) * b.astype(F32)


def _gates_spec(tm):
    return pl.BlockSpec((tm, W_GATES), lambda i: (i, O_GATES // W_GATES))


def _merge_fwd(z, a, b, *, name):
    T = a.shape[0]
    tm = _pick(T, (512, 256, 128))

    def body(g_ref, a_ref, b_ref, o_ref):
        o_ref[...] = _merge_fn(g_ref[...], a_ref[...], b_ref[...]).astype(BF16)

    return pl.pallas_call(
        body, name=name, grid=(T // tm,),
        in_specs=[_gates_spec(tm), _row_spec(tm, D_MODEL), _row_spec(tm, D_MODEL)],
        out_specs=_row_spec(tm, D_MODEL),
        out_shape=jax.ShapeDtypeStruct((T, D_MODEL), BF16),
        compiler_params=_params("parallel"),
    )(z, a, b)


def _merge_bwd(z, a, b, dmerged, dz, *, name):
    T = a.shape[0]
    tm = _pick(T, (512, 256, 128))

    def body(g_ref, a_ref, b_ref, dm_ref, dz_in, dg_ref, da_ref, db_ref):
        g = g_ref[...].astype(F32)
        dm = dm_ref[...].astype(F32)
        sa = _sigmoid(g[:, :D_MODEL])
        sb = _sigmoid(g[:, D_MODEL:])
        da_ref[...] = (dm * sa).astype(BF16)
        db_ref[...] = (dm * sb).astype(BF16)
        dg_ref[:, :D_MODEL] = (dm * a_ref[...].astype(F32) * sa * (1.0 - sa)).astype(BF16)
        dg_ref[:, D_MODEL:] = (dm * b_ref[...].astype(F32) * sb * (1.0 - sb)).astype(BF16)

    return pl.pallas_call(
        body, name=name, grid=(T // tm,),
        in_specs=[_gates_spec(tm), _row_spec(tm, D_MODEL), _row_spec(tm, D_MODEL), _row_spec(tm, D_MODEL), ANY],
        out_specs=[_gates_spec(tm), _row_spec(tm, D_MODEL), _row_spec(tm, D_MODEL)],
        out_shape=[jax.ShapeDtypeStruct(dz.shape, BF16), jax.ShapeDtypeStruct((T, D_MODEL), BF16), jax.ShapeDtypeStruct((T, D_MODEL), BF16)],
        input_output_aliases={4: 0},
        compiler_params=_params("parallel"),
    )(z, a, b, dmerged, dz)


CONV_TC = 256


def _shift_down(x, n, rows):
    return jnp.where(rows >= n, pltpu.roll(x, n, 0), 0.0)


def _shift_up(x, n, rows, S):
    return jnp.where(rows < S - n, pltpu.roll(x, S - n, 0), 0.0)


def _conv_act_fwd(gu, conv_w, conv_b, *, name):
    B, S, _ = gu.shape
    tc = CONV_TC
    nc = D_FF // tc

    def body(g_ref, up_ref, w_ref, b_ref, o_ref, a_ref):
        g = g_ref[...].astype(F32)
        rows = lax.broadcasted_iota(jnp.int32, g.shape, 0)
        w = w_ref[...]
        a = w[2:3] * g + w[1:2] * _shift_down(g, 1, rows) + w[0:1] * _shift_down(g, 2, rows) + b_ref[...]
        o_ref[...] = (_silu(a) * up_ref[...].astype(F32)).astype(BF16)
        a_ref[...] = a.astype(BF16)

    col = pl.BlockSpec((None, S, tc), lambda b, j: (b, 0, j))
    return pl.pallas_call(
        body, name=name, grid=(B, nc),
        in_specs=[col,
                  pl.BlockSpec((None, S, tc), lambda b, j: (b, 0, j + nc)),
                  pl.BlockSpec((3, tc), lambda b, j: (0, j)),
                  pl.BlockSpec((1, tc), lambda b, j: (0, j))],
        out_specs=[col, col],
        out_shape=[jax.ShapeDtypeStruct((B, S, D_FF), BF16)] * 2,
        compiler_params=_params("parallel", "parallel"),
    )(gu, gu, conv_w, conv_b)


def _conv_act_bwd(gu, a_pre, conv_w, dact, *, name):
    B, S, _ = gu.shape
    tc = CONV_TC
    nc = D_FF // tc

    def body(g_ref, up_ref, a_ref, w_ref, da_ref, dg_ref, dup_ref, dw_ref, db_ref):
        g = g_ref[...].astype(F32)
        up = up_ref[...].astype(F32)
        a = a_ref[...].astype(F32)
        dact = da_ref[...].astype(F32)
        rows = lax.broadcasted_iota(jnp.int32, g.shape, 0)
        w = w_ref[...]
        sg = _sigmoid(a)
        dup_ref[...] = (dact * a * sg).astype(BF16)
        da = dact * up * sg * (1.0 + a * (1.0 - sg))
        da1 = _shift_up(da, 1, rows, S)
        da2 = _shift_up(da, 2, rows, S)
        dg_ref[...] = (w[2:3] * da + w[1:2] * da1 + w[0:1] * da2).astype(BF16)

        @pl.when(pl.program_id(1) == 0)
        def _():
            dw_ref[...] = jnp.zeros_like(dw_ref)
            db_ref[...] = jnp.zeros_like(db_ref)

        dw_ref[0:1, :] += jnp.sum(da2 * g, axis=0, keepdims=True)
        dw_ref[1:2, :] += jnp.sum(da1 * g, axis=0, keepdims=True)
        dw_ref[2:3, :] += jnp.sum(da * g, axis=0, keepdims=True)
        db_ref[...] += jnp.sum(da, axis=0, keepdims=True)

    col = pl.BlockSpec((None, S, tc), lambda j, b: (b, 0, j))
    return pl.pallas_call(
        body, name=name, grid=(nc, B),
        in_specs=[col,
                  pl.BlockSpec((None, S, tc), lambda j, b: (b, 0, j + nc)),
                  col,
                  pl.BlockSpec((3, tc), lambda j, b: (0, j)),
                  col],
        out_specs=[col, col, pl.BlockSpec((3, tc), lambda j, b: (0, j)), pl.BlockSpec((1, tc), lambda j, b: (0, j))],
        out_shape=[jax.ShapeDtypeStruct((B, S, D_FF), BF16), jax.ShapeDtypeStruct((B, S, D_FF), BF16),
                   jax.ShapeDtypeStruct((3, D_FF), F32), jax.ShapeDtypeStruct((1, D_FF), F32)],
        compiler_params=_params("parallel", "arbitrary"),
    )(gu, gu, a_pre, conv_w, dact)


HGRN_CPB = 4
HF = HGRN_HEADS * HGRN_DK


def _tri(n, upper=False):
    r = lax.broadcasted_iota(jnp.int32, (n, n), 0)
    c = lax.broadcasted_iota(jnp.int32, (n, n), 1)
    return (c >= r) if upper else (r >= c)


def _hs(h):
    return slice(h * HGRN_DK, (h + 1) * HGRN_DK)


def _cumsum_rows(tri_b, x):
    hi = x.astype(BF16)
    lo = (x - hi.astype(F32)).astype(BF16)
    return _nn(tri_b, hi) + _nn(tri_b, lo)


def _hgrn_pre(q, fz, lb, tril_b):
    qf = _silu(q)
    sg = _sigmoid(fz)
    f = lb + (1.0 - lb) * sg
    k = 1.0 - f
    b = _cumsum_rows(tril_b, jnp.log2(f))
    bref = b[CHUNK // 2:CHUNK // 2 + 1, :]
    blast = b[CHUNK - 1:CHUNK, :]
    e1 = jnp.exp2(b - bref)
    e2 = jnp.exp2(bref - b)
    e3 = e1 * jnp.exp2(bref)
    e4 = e2 * jnp.exp2(blast - bref)
    dec = jnp.exp2(blast)
    return sg, f, (e1, e2, e3, e4), qf * e1, k * e2, qf * e3, k * e4, dec


def _hgrn_fwd(zh, lb, gn, *, name):
    B, S, _ = zh.shape
    cpb = HGRN_CPB
    ts = cpb * CHUNK
    nblk = S // ts

    def body(z_ref, lb_ref, gn_ref, o_ref, st_ref, state):
        @pl.when(pl.program_id(1) == 0)
        def _():
            state[...] = jnp.zeros_like(state)

        H = HGRN_HEADS
        causal = _tri(CHUNK)
        tril_b = causal.astype(BF16)
        lb = lb_ref[...]
        for c in range(cpb):
            rows = slice(c * CHUNK, (c + 1) * CHUNK)
            q = z_ref[rows, 0:HF].astype(F32)
            fz = z_ref[rows, HF:2 * HF].astype(F32)
            v = z_ref[rows, 2 * HF:3 * HF]
            hg = z_ref[rows, 3 * HF:4 * HF].astype(F32)
            _, _, _, q_in, k_in, q_out, k_st, dec = _hgrn_pre(q, fz, lb, tril_b)
            q_in, k_in, q_out, k_st = (t.astype(BF16) for t in (q_in, k_in, q_out, k_st))
            a = [jnp.where(causal, _nt(q_in[:, _hs(h)], k_in[:, _hs(h)]), 0.0).astype(BF16) for h in range(H)]
            st = [state[h] for h in range(H)]
            for h in range(H):
                st_ref[c, h] = st[h]
            o = [_nn(a[h], v[:, _hs(h)]) + _nt(q_out[:, _hs(h)], st[h].astype(BF16)) for h in range(H)]
            for h in range(H):
                state[h] = st[h] * dec[:, _hs(h)] + _tn(v[:, _hs(h)], k_st[:, _hs(h)])
            gate = _silu(hg)
            for h in range(H):
                o_ref[rows, _hs(h)] = (_rms(o[h], gn_ref[...]) * gate[:, _hs(h)]).astype(BF16)

    return pl.pallas_call(
        body, name=name, grid=(B, nblk),
        in_specs=[pl.BlockSpec((None, ts, 4 * HF), lambda b, s: (b, s, 0)),
                  pl.BlockSpec((1, HF), lambda b, s: (0, 0)),
                  pl.BlockSpec((1, HGRN_DK), lambda b, s: (0, 0))],
        out_specs=[pl.BlockSpec((None, ts, HF), lambda b, s: (b, s, 0)),
                   pl.BlockSpec((None, cpb, HGRN_HEADS, HGRN_DK, HGRN_DK), lambda b, s: (b, s, 0, 0, 0))],
        out_shape=[jax.ShapeDtypeStruct((B, S, HF), BF16),
                   jax.ShapeDtypeStruct((B, S // CHUNK, HGRN_HEADS, HGRN_DK, HGRN_DK), F32)],
        scratch_shapes=[pltpu.VMEM((HGRN_HEADS, HGRN_DK, HGRN_DK), F32)],
        compiler_params=_params("arbitrary", "arbitrary"),
    )(zh, lb, gn)


def _hgrn_bwd(zh, lb, gn, states, doa, dz, *, name):
    B, S, _ = zh.shape
    cpb = HGRN_CPB
    ts = cpb * CHUNK
    nblk = S // ts
    rev = lambda b, s: (b, nblk - 1 - s, 0)

    def body(z_ref, lb_ref, gn_ref, st_ref, do_ref, dz_in, dz_ref, dlb_ref, dgn_ref, dstate):
        @pl.when(pl.program_id(1) == 0)
        def _():
            dstate[...] = jnp.zeros_like(dstate)

        @pl.when((pl.program_id(0) == 0) & (pl.program_id(1) == 0))
        def _():
            dlb_ref[...] = jnp.zeros_like(dlb_ref)
            dgn_ref[...] = jnp.zeros_like(dgn_ref)

        H = HGRN_HEADS
        cat = lambda xs: jnp.concatenate(xs, axis=1)
        causal = _tri(CHUNK)
        tril_b = causal.astype(BF16)
        triu_b = _tri(CHUNK, upper=True).astype(BF16)
        rowid = lax.broadcasted_iota(jnp.int32, (CHUNK, HF), 0)
        lb = lb_ref[...]
        gn = gn_ref[...]
        for c in reversed(range(cpb)):
            rows = slice(c * CHUNK, (c + 1) * CHUNK)
            q = z_ref[rows, 0:HF].astype(F32)
            fz = z_ref[rows, HF:2 * HF].astype(F32)
            v = z_ref[rows, 2 * HF:3 * HF]
            hg = z_ref[rows, 3 * HF:4 * HF].astype(F32)
            sg, f, (e1, e2, e3, e4), q_in, k_in, q_out, k_st, dec = _hgrn_pre(q, fz, lb, tril_b)
            q_in_b, k_in_b, q_out_b, k_st_b = (t.astype(BF16) for t in (q_in, k_in, q_out, k_st))
            a_b = [jnp.where(causal, _nt(q_in_b[:, _hs(h)], k_in_b[:, _hs(h)]), 0.0).astype(BF16) for h in range(H)]
            st = [st_ref[c, h] for h in range(H)]
            st_b = [t.astype(BF16) for t in st]
            o = [_nn(a_b[h], v[:, _hs(h)]) + _nt(q_out_b[:, _hs(h)], st_b[h]) for h in range(H)]
            dout = do_ref[rows, :].astype(F32)
            shg = _sigmoid(hg)
            gate = hg * shg
            do_l, dgn_acc = [], jnp.zeros_like(gn)
            for h in range(H):
                _, norm_vjp = jax.vjp(_rms, o[h], gn)
                d_o, d_gn = norm_vjp(dout[:, _hs(h)] * gate[:, _hs(h)])
                do_l.append(d_o)
                dgn_acc = dgn_acc + d_gn
            dgn_ref[...] += dgn_acc
            on = cat([_rms(o[h], gn) for h in range(H)])
            dhg = dout * on * shg * (1.0 + hg * (1.0 - shg))
            do_b = [t.astype(BF16) for t in do_l]
            dst = [dstate[h] for h in range(H)]
            dst_b = [t.astype(BF16) for t in dst]
            da_b = [jnp.where(causal, _nt(do_b[h], v[:, _hs(h)]), 0.0).astype(BF16) for h in range(H)]
            dv = cat([_tn(a_b[h], do_b[h]) + _nt(k_st_b[:, _hs(h)], dst_b[h]) for h in range(H)])
            dq_in = cat([_nn(da_b[h], k_in_b[:, _hs(h)]) for h in range(H)])
            dk_in = cat([_tn(da_b[h], q_in_b[:, _hs(h)]) for h in range(H)])
            dq_out = cat([_nn(do_b[h], st_b[h]) for h in range(H)])
            dk_st = cat([_nn(v[:, _hs(h)], dst_b[h]) for h in range(H)])
            ddec = cat([jnp.sum(st[h] * dst[h], axis=0, keepdims=True) for h in range(H)])
            for h in range(H):
                dstate[h] = dst[h] * dec[:, _hs(h)] + _tn(do_b[h], q_out_b[:, _hs(h)])
            t_qin = dq_in * q_in
            t_kin = dk_in * k_in
            t_kst = dk_st * k_st
            db = t_qin - t_kin + dq_out * q_out - t_kst
            dbref = jnp.sum(t_kin - t_qin, axis=0, keepdims=True)
            dblast = jnp.sum(t_kst, axis=0, keepdims=True) + ddec * dec
            db = db + jnp.where(rowid == CHUNK // 2, dbref, 0.0) + jnp.where(rowid == CHUNK - 1, dblast, 0.0)
            dlogf = _cumsum_rows(triu_b, db)
            dqf = dq_in * e1 + dq_out * e3
            dk = dk_in * e2 + dk_st * e4
            df = dlogf / f - dk
            dfz = df * (1.0 - lb) * sg * (1.0 - sg)
            dlb_ref[...] += jnp.sum(df * (1.0 - sg), axis=0, keepdims=True)
            sq = _sigmoid(q)
            dq = dqf * sq * (1.0 + q * (1.0 - sq))
            dz_ref[rows, 0:HF] = dq.astype(BF16)
            dz_ref[rows, HF:2 * HF] = dfz.astype(BF16)
            dz_ref[rows, 2 * HF:3 * HF] = dv.astype(BF16)
            dz_ref[rows, 3 * HF:4 * HF] = dhg.astype(BF16)

    return pl.pallas_call(
        body, name=name, grid=(B, nblk),
        in_specs=[pl.BlockSpec((None, ts, 4 * HF), rev),
                  pl.BlockSpec((1, HF), lambda b, s: (0, 0)),
                  pl.BlockSpec((1, HGRN_DK), lambda b, s: (0, 0)),
                  pl.BlockSpec((None, cpb, HGRN_HEADS, HGRN_DK, HGRN_DK), lambda b, s: (b, nblk - 1 - s, 0, 0, 0)),
                  pl.BlockSpec((None, ts, HF), rev),
                  ANY],
        out_specs=[pl.BlockSpec((None, ts, 4 * HF), rev),
                   pl.BlockSpec((1, HF), lambda b, s: (0, 0)),
                   pl.BlockSpec((1, HGRN_DK), lambda b, s: (0, 0))],
        out_shape=[jax.ShapeDtypeStruct(dz.shape, BF16),
                   jax.ShapeDtypeStruct((1, HF), F32),
                   jax.ShapeDtypeStruct((1, HGRN_DK), F32)],
        input_output_aliases={5: 0},
        scratch_shapes=[pltpu.VMEM((HGRN_HEADS, HGRN_DK, HGRN_DK), F32)],
        compiler_params=_params("arbitrary", "arbitrary"),
    )(zh, lb, gn, states, doa, dz)


KV_W = ATT_KV_HEADS * ATT_HD
ATT_SCALE = ATT_HD ** -0.5


def _rope(x, cos, sin, inverse=False):
    half = ROPE_DIM // 2
    outs = []
    for p in range(x.shape[1] // 128):
        xp = x[:, p * 128:(p + 1) * 128]
        lane = lax.broadcasted_iota(jnp.int32, xp.shape, 1) % ATT_HD
        sw = jnp.where(lane < half, pltpu.roll(xp, 128 - half, 1), pltpu.roll(xp, half, 1))
        outs.append(xp * cos - sw * sin if inverse else xp * cos + sw * sin)
    return outs[0] if len(outs) == 1 else jnp.concatenate(outs, axis=1)


PAIRS_PER_KV = ATT_GROUP // 2


def _swap_halves(x):
    return pltpu.roll(x, ATT_HD, 1)


def _kv_padded(t, low):
    sw = _swap_halves(t)
    zero = jnp.zeros_like(t)
    out = []
    for g in range(ATT_KV_HEADS):
        in_low, in_high = (t, sw) if g == 0 else (sw, t)
        out.append((jnp.where(low, in_low, zero).astype(BF16), jnp.where(low, zero, in_high).astype(BF16)))
    return out


def _swa_mask(first_block):
    qi = lax.broadcasted_iota(jnp.int32, (WINDOW, 2 * WINDOW), 0)
    mi = lax.broadcasted_iota(jnp.int32, (WINDOW, 2 * WINDOW), 1)
    band = (mi > qi) & (mi <= qi + WINDOW)
    return band & (jnp.logical_not(first_block) | (mi >= WINDOW))


def _swa_specs(nb):
    cur = lambda b, i: (b, i, 0)
    prev = lambda b, i: (b, jnp.maximum(i - 1, 0), 0)
    return cur, prev


def _swa_z_specs():
    q = pl.BlockSpec((None, WINDOW, W_AQ), lambda b, i: (b, i, O_AQ // W_AQ))
    kv_prev = pl.BlockSpec((None, WINDOW, W_AKV), lambda b, i: (b, jnp.maximum(i - 1, 0), O_AKV // W_AKV))
    kv_cur = pl.BlockSpec((None, WINDOW, W_AKV), lambda b, i: (b, i, O_AKV // W_AKV))
    return q, kv_prev, kv_cur


def _swa_fwd(z, cos, sin, sinks, *, name):
    B, S, _ = z.shape
    nb = S // WINDOW
    cur, prev = _swa_specs(nb)

    def body(q_ref, kvp_ref, kvc_ref, cp_ref, sp_ref, cc_ref, sc_ref, sink_ref, o_ref, lse_ref):
        cos_c, sin_c = cc_ref[...], sc_ref[...]
        q = (_rope(q_ref[...].astype(F32), cos_c, sin_c) * ATT_SCALE).astype(BF16)
        k = jnp.concatenate([_rope(kvp_ref[:, :KV_W].astype(F32), cp_ref[...], sp_ref[...]),
                             _rope(kvc_ref[:, :KV_W].astype(F32), cos_c, sin_c)], axis=0)
        v = jnp.concatenate([kvp_ref[:, KV_W:], kvc_ref[:, KV_W:]], axis=0).astype(F32)
        low = lax.broadcasted_iota(jnp.int32, k.shape, 1) < ATT_HD
        kpad = _kv_padded(k, low)
        vpad = _kv_padded(v, low)
        mask = _swa_mask(pl.program_id(1) == 0)
        lses = []
        for g in range(ATT_KV_HEADS):
            pairs = range(g * PAIRS_PER_KV, (g + 1) * PAIRS_PER_KV)
            keys = [(p, e) for p in pairs for e in (0, 1)]
            qp = {p: q[:, p * 128:(p + 1) * 128] for p in pairs}
            s = {pe: jnp.where(mask, _nt(qp[pe[0]], kpad[g][pe[1]]), NEG_INF) for pe in keys}
            pr = {}
            for pe in keys:
                sink = sink_ref[0, 2 * pe[0] + pe[1]]
                m = jnp.maximum(jnp.max(s[pe], axis=1, keepdims=True), sink)
                ex = jnp.exp(s[pe] - m)
                den = jnp.sum(ex, axis=1, keepdims=True) + jnp.exp(sink - m)
                pr[pe] = (ex * (1.0 / den)).astype(BF16)
                lses.append(m + jnp.log(den))
            for p in pairs:
                o_ref[:, p * 128:(p + 1) * 128] = (_nn(pr[p, 0], vpad[g][0]) + _nn(pr[p, 1], vpad[g][1])).astype(BF16)
        lse_ref[...] = jnp.concatenate(lses, axis=1)

    tab = lambda im: pl.BlockSpec((None, WINDOW, 128), im)
    return pl.pallas_call(
        body, name=name, grid=(B, nb),
        in_specs=[*_swa_z_specs(),
                  tab(prev), tab(prev), tab(cur), tab(cur),
                  pl.BlockSpec(memory_space=pltpu.SMEM)],
        out_specs=[pl.BlockSpec((None, WINDOW, D_MODEL), cur), pl.BlockSpec((None, WINDOW, ATT_HEADS), cur)],
        out_shape=[jax.ShapeDtypeStruct((B, S, D_MODEL), BF16), jax.ShapeDtypeStruct((B, S, ATT_HEADS), F32)],
        compiler_params=_params("parallel", "parallel"),
    )(z, z, z, cos, sin, cos, sin, sinks)


def _swa_bwd(z, cos, sin, sinks, lse, dob, dz, *, name):
    B, S, _ = z.shape
    nb = S // WINDOW
    cur, prev = _swa_specs(nb)

    def body(q_ref, kvp_ref, kvc_ref, cp_ref, sp_ref, cc_ref, sc_ref, sink_ref, lse_ref, do_ref, dz_in,
             dq_ref, dkc_ref, dkp_ref, dsink_ref):
        @pl.when((pl.program_id(0) == 0) & (pl.program_id(1) == 0))
        def _():
            dsink_ref[...] = jnp.zeros_like(dsink_ref)

        cos_c, sin_c, cos_p, sin_p = cc_ref[...], sc_ref[...], cp_ref[...], sp_ref[...]
        q = (_rope(q_ref[...].astype(F32), cos_c, sin_c) * ATT_SCALE).astype(BF16)
        k = jnp.concatenate([_rope(kvp_ref[:, :KV_W].astype(F32), cos_p, sin_p),
                             _rope(kvc_ref[:, :KV_W].astype(F32), cos_c, sin_c)], axis=0)
        v = jnp.concatenate([kvp_ref[:, KV_W:], kvc_ref[:, KV_W:]], axis=0).astype(F32)
        low = lax.broadcasted_iota(jnp.int32, k.shape, 1) < ATT_HD
        kpad = _kv_padded(k, low)
        vpad = _kv_padded(v, low)
        mask = _swa_mask(pl.program_id(1) == 0)
        lse = lse_ref[...]
        dq_parts, dk_sum, dv_sum, dsinks = [], [], [], []
        for g in range(ATT_KV_HEADS):
            pairs = range(g * PAIRS_PER_KV, (g + 1) * PAIRS_PER_KV)
            keys = [(p, e) for p in pairs for e in (0, 1)]
            qp = {p: q[:, p * 128:(p + 1) * 128] for p in pairs}
            dop = {p: do_ref[:, p * 128:(p + 1) * 128] for p in pairs}
            s = {pe: jnp.where(mask, _nt(qp[pe[0]], kpad[g][pe[1]]), NEG_INF) for pe in keys}
            dp = {pe: _nt(dop[pe[0]], vpad[g][pe[1]]) for pe in keys}
            pr, ds = {}, {}
            for pe in keys:
                h = 2 * pe[0] + pe[1]
                lse_h = lse[:, h:h + 1]
                pf = jnp.exp(s[pe] - lse_h)
                delta = jnp.sum(pf * dp[pe], axis=1, keepdims=True)
                ds[pe] = (pf * (dp[pe] - delta)).astype(BF16)
                pr[pe] = pf.astype(BF16)
                p_sink = jnp.exp(sink_ref[0, h] - lse_h)
                dsinks.append(-jnp.sum(p_sink * delta, axis=0, keepdims=True))
            for p in pairs:
                dq_parts.append((_nn(ds[p, 0], kpad[g][0]) + _nn(ds[p, 1], kpad[g][1])) * ATT_SCALE)
            x = [sum(_tn(ds[p, e], qp[p]) for p in pairs) for e in (0, 1)]
            y = [sum(_tn(pr[p, e], dop[p]) for p in pairs) for e in (0, 1)]
            zk = jnp.where(low, x[0], x[1])
            zv = jnp.where(low, y[0], y[1])
            dk_sum.append(zk + _swap_halves(zk))
            dv_sum.append(zv + _swap_halves(zv))
        dq_ref[...] = _rope(jnp.concatenate(dq_parts, axis=1), cos_c, sin_c, inverse=True).astype(BF16)
        dk = jnp.where(low, dk_sum[0], dk_sum[1])
        dv = jnp.where(low, dv_sum[0], dv_sum[1])
        dkp_ref[:, :KV_W] = _rope(dk[:WINDOW], cos_p, sin_p, inverse=True)
        dkp_ref[:, KV_W:] = dv[:WINDOW]
        dkc_ref[:, :KV_W] = _rope(dk[WINDOW:], cos_c, sin_c, inverse=True)
        dkc_ref[:, KV_W:] = dv[WINDOW:]
        dsink_ref[...] += jnp.concatenate(dsinks, axis=1)

    tab = lambda im: pl.BlockSpec((None, WINDOW, 128), im)
    return pl.pallas_call(
        body, name=name, grid=(B, nb),
        in_specs=[*_swa_z_specs(),
                  tab(prev), tab(prev), tab(cur), tab(cur),
                  pl.BlockSpec(memory_space=pltpu.SMEM),
                  pl.BlockSpec((None, WINDOW, ATT_HEADS), cur),
                  pl.BlockSpec((None, WINDOW, D_MODEL), cur),
                  ANY],
        out_specs=[_swa_z_specs()[0],
                   pl.BlockSpec((None, WINDOW, 2 * KV_W), cur), pl.BlockSpec((None, WINDOW, 2 * KV_W), cur),
                   pl.BlockSpec((1, ATT_HEADS), lambda b, i: (0, 0))],
        out_shape=[jax.ShapeDtypeStruct(dz.shape, BF16),
                   jax.ShapeDtypeStruct((B, S, 2 * KV_W), F32), jax.ShapeDtypeStruct((B, S, 2 * KV_W), F32),
                   jax.ShapeDtypeStruct((1, ATT_HEADS), F32)],
        input_output_aliases={10: 0},
        compiler_params=_params("arbitrary", "arbitrary"),
    )(z, z, z, cos, sin, cos, sin, sinks, lse, dob, dz)


def _swa_dkv_combine(dkv_cur, dkv_prev, dz, *, name):
    B, S, W = dkv_cur.shape

    def body(c_ref, p_ref, dz_in, o_ref):
        rows = lax.broadcasted_iota(jnp.int32, (S, W), 0)
        o_ref[...] = (c_ref[...] + _shift_up(p_ref[...], WINDOW, rows, S)).astype(BF16)

    seq = pl.BlockSpec((None, S, W), lambda b: (b, 0, 0))
    return pl.pallas_call(
        body, name=name, grid=(B,),
        in_specs=[seq, seq, ANY], out_specs=pl.BlockSpec((None, S, W), lambda b: (b, 0, O_AKV // W_AKV)),
        out_shape=jax.ShapeDtypeStruct(dz.shape, BF16),
        input_output_aliases={2: 0},
        compiler_params=_params("parallel"),
    )(dkv_cur, dkv_prev, dz)


def _rope_tables(positions):
    half = ROPE_DIM // 2
    inv = ROPE_THETA ** (-2.0 * jnp.arange(half, dtype=F32) / ROPE_DIM)
    ang = positions.astype(F32)[..., None] * inv
    c, s = jnp.cos(ang), jnp.sin(ang)
    pad = jnp.zeros(ang.shape[:-1] + (ATT_HD - ROPE_DIM,), F32)
    cos = jnp.concatenate([c, c, pad + 1.0], axis=-1)
    sin = jnp.concatenate([-s, s, pad], axis=-1)
    return jnp.tile(cos, (1, 1, 2)), jnp.tile(sin, (1, 1, 2))


def _lower_bound(lb_logits, *, name):
    def body(l_ref, o_ref):
        l = l_ref[...]
        e = jnp.exp(l - jnp.max(l, axis=0, keepdims=True))
        o_ref[...] = e[0:1] / jnp.sum(e, axis=0, keepdims=True)

    return pl.pallas_call(body, name=name, out_shape=jax.ShapeDtypeStruct((1, lb_logits.shape[1]), F32))(lb_logits)


W_ZH, W_GATES, W_AQ, W_AKV = 4 * HF, 2 * D_MODEL, ATT_HEADS * ATT_HD, 2 * KV_W
O_ZH, O_GATES, O_AQ, O_AKV = 0, W_ZH, W_ZH + W_GATES, W_ZH + W_GATES + W_AQ
W_IN = W_ZH + W_GATES + W_AQ + W_AKV


W_IN_BLK = W_IN // N_DEV


def _w_in_pieces():
    ref_segments = [(0, W_ZH, O_ZH), (W_ZH, W_AQ, O_AQ), (W_ZH + W_AQ, W_AKV, O_AKV), (W_ZH + W_AQ + W_AKV, W_GATES, O_GATES)]
    out = []
    for p in range(N_DEV):
        lo, hi = p * W_IN_BLK, (p + 1) * W_IN_BLK
        for s0, w, d0 in ref_segments:
            a, b = max(lo, s0), min(hi, s0 + w)
            if a < b:
                out.append((p, a - lo, d0 + a - s0, b - a))
    return out


def _w_in_from_blocks(blocks, *, name):
    tr = 256

    def body(b_ref, o_ref):
        for p, s, d, w in _w_in_pieces():
            o_ref[:, d:d + w] = b_ref[p, :, s:s + w]

    return pl.pallas_call(
        body, name=name, grid=(D_MODEL // tr,),
        in_specs=[pl.BlockSpec((N_DEV, tr, W_IN_BLK), lambda i: (0, i, 0))], out_specs=pl.BlockSpec((tr, W_IN), lambda i: (i, 0)),
        out_shape=jax.ShapeDtypeStruct((D_MODEL, W_IN), blocks.dtype), compiler_params=_params("parallel"))(blocks)


def _blocks_from_w_in(w, *, name):
    tr = 256

    def body(w_ref, o_ref):
        for p, s, d, wd in _w_in_pieces():
            o_ref[p, :, s:s + wd] = w_ref[:, d:d + wd]

    return pl.pallas_call(
        body, name=name, grid=(D_MODEL // tr,),
        in_specs=[pl.BlockSpec((tr, W_IN), lambda i: (i, 0))], out_specs=pl.BlockSpec((N_DEV, tr, W_IN_BLK), lambda i: (0, i, 0)),
        out_shape=jax.ShapeDtypeStruct((N_DEV, D_MODEL, W_IN_BLK), w.dtype), compiler_params=_params("parallel"))(w)


def _local_step(x, positions, target, small, w_in, rest_weights, emit, start_token):
    B, S, D = x.shape
    T = B * S
    x2 = x.reshape(T, D)
    cos, sin = _rope_tables(positions)
    lb = _lower_bound(small["lb_logits"], name="lb_fwd")
    zero = lambda tok: tok[0:1, 0:1]

    u1 = _norm_cast(x2, small["norm1_g"] + zero(start_token), name="norm1")
    z = _matmul(u1, w_in, out_dtype=BF16, name="mm_z", tm=1024, tn=W_IN // 2)
    z3 = z.reshape(B, S, W_IN)
    oa, states = _hgrn_fwd(z3, lb, small["hgrn_norm_g"], name="hgrn_fwd")
    ob, lse = _swa_fwd(z3, cos, sin, small["attn_sinks"], name="swa_fwd")
    oa2 = oa.reshape(T, D)
    ob2 = ob.reshape(T, D)
    W = rest_weights("mix", ob)
    pa = _matmul(oa2, W["w_a"], out_dtype=BF16, name="mm_pa", tm=2048, tn=512)
    pb = _matmul(ob2, W["w_b"], out_dtype=BF16, name="mm_pb", tm=2048, tn=512)
    merged = _merge_fwd(z, pa, pb, name="merge_fwd")
    h = _matmul(merged, W["w_out"], addend=x2, name="mm_h", tm=2048, tn=512)
    u2 = _norm_cast(h, small["norm2_g"], name="norm2")
    W.update(rest_weights("ffn", u2))
    gu = _matmul(u2, W["w_ffn"], out_dtype=BF16, name="mm_gu", tm=2048, tn=512)
    gu3 = gu.reshape(B, S, 2 * D_FF)
    act, a_pre = _conv_act_fwd(gu3, W["conv_w"], small["conv_b"], name="conv_act_fwd")
    act2 = act.reshape(T, D_FF)
    h2 = _matmul(act2, W["w_down"], addend=h, name="mm_h2", tm=1024, tn=1024)

    g = {}
    dh2, dh2b, g["final_g"], loss = _final_loss_bwd(h2, small["final_g"].reshape(1, D), target.reshape(T, D), name="final_loss_bwd")
    dact = _matmul(dh2b, W["w_down"], tb=True, out_dtype=BF16, name="mm_dact", tm=1024, tn=D_FF)
    dw_down_t = _matmul(dh2b, act2, ta=True, out_dtype=BF16, name="mm_dw_down", tm=1024, tn=256, tk=8192)
    dg_, dup, g["conv_w"], g["conv_b"] = _conv_act_bwd(gu3, a_pre, W["conv_w"], dact.reshape(B, S, D_FF), name="conv_act_bwd")
    dg2 = dg_.reshape(T, D_FF)
    dup2 = dup.reshape(T, D_FF)
    du2 = _matmul(dg2, W["w_ffn"], tb=True, name="mm_du2_g", tm=1024, tn=1024, b_koff=0)
    du2 = _matmul(dup2, W["w_ffn"], tb=True, addend=du2, out_dtype=BF16, name="mm_du2_u", tm=1024, tn=1024, b_koff=1)
    dw_ffn = _matmul(u2, dg2, ta=True, out_dtype=BF16, into=lax.empty((D, 2 * D_FF), BF16), o_noff=0, name="mm_dw_ffn_g", tm=1024, tn=256, tk=8192)
    dw_ffn = _matmul(u2, dup2, ta=True, out_dtype=BF16, into=dw_ffn, o_noff=D_FF // 256, name="mm_dw_ffn_u", tm=1024, tn=256, tk=8192)
    tok = emit("ffn", dict(w_ffn=dw_ffn, w_down=dw_down_t.T))
    dh, dhb, g["norm2_g"] = _norm_bwd_add(h, small["norm2_g"] + zero(tok), du2, dh2, name="norm2_bwd")
    dmerged = _matmul(dhb, W["w_out"], tb=True, out_dtype=BF16, name="mm_dmerged", tm=2048, tn=512)
    dw_out = _matmul(merged, dhb, ta=True, out_dtype=BF16, name="mm_dw_out", tm=1024, tn=1024, tk=2048)
    dz, dpa, dpb = _merge_bwd(z, pa, pb, dmerged, lax.empty((T, W_IN), BF16), name="merge_bwd")
    doa =_matmul(dpa, W["w_a"], tb=True, out_dtype=BF16, name="mm_doa", tm=2048, tn=512)
    dw_a = _matmul(oa2, dpa, ta=True, out_dtype=BF16, name="mm_dw_a", tm=1024, tn=1024, tk=2048)
    dob = _matmul(dpb, W["w_b"], tb=True, out_dtype=BF16, name="mm_dob", tm=2048, tn=512)
    dw_b = _matmul(ob2, dpb, ta=True, out_dtype=BF16, name="mm_dw_b", tm=1024, tn=1024, tk=2048)
    tok = emit("mix", dict(w_out=dw_out, w_a=dw_a, w_b=dw_b))
    dz3, dkv_cur, dkv_prev, dsinks = _swa_bwd(z3, cos, sin, small["attn_sinks"] + zero(tok), lse, dob.reshape(B, S, D),
                                              dz.reshape(B, S, W_IN), name="swa_bwd")
    dz3 = _swa_dkv_combine(dkv_cur, dkv_prev, dz3, name="swa_dkv")
    g["attn_sinks"] = dsinks
    dz3, g["lb"], g["hgrn_norm_g"] = _hgrn_bwd(z3, lb, small["hgrn_norm_g"], states, doa.reshape(B, S, D), dz3, name="hgrn_bwd")
    dz = dz3.reshape(T, W_IN)
    dw_in = _matmul(u1, dz, ta=True, out_dtype=BF16, name="mm_dw_in", tm=1024, tn=256, tk=8192)
    tok = emit("in", dict(w_in=dw_in))
    du1 = _matmul(dz, w_in, tb=True, after=tok, out_dtype=BF16, name="mm_du1", tm=1024, tn=512)
    dx, g["norm1_g"] = _norm_bwd_add(x2, small["norm1_g"], du1, dh, with_bf16=False, name="norm1_bwd")
    g["lb_logits"] = _lb_bwd(g.pop("lb"), lb, name="lb_bwd")
    return loss, dx.reshape(B, S, D), g


def _my_place():
    return lax.axis_index("x"), lax.axis_index("y"), lax.axis_index("c")


def _gather_blocks(x_ref, out_ref, send_sems, recv_sems, local_sem):
    x, y, c = _my_place()
    me, sibling = (x, y, c), (x, y, 1 - c)
    chips = [(1 - x, y), (x, 1 - y), (1 - x, 1 - y)]

    def slot(px, py, pc):
        return out_ref.at[4 * px + 2 * py + pc]

    def copy(k, block, to, src=None):
        return pltpu.make_async_remote_copy(
            src_ref=slot(*block) if src is None else src, dst_ref=slot(*block),
            send_sem=send_sems.at[k], recv_sem=recv_sems.at[k], device_id=to, device_id_type=MESH)

    mine = pltpu.make_async_copy(x_ref, slot(*me), local_sem)
    mine.start()
    first = [copy(0, me, sibling, src=x_ref)]
    first += [copy(1 + j, me, (*chip, c), src=x_ref) for j, chip in enumerate(chips)]
    for cp in first:
        cp.start()
    passed = [copy(4 + j, (*chip, c), sibling) for j, chip in enumerate(chips)]
    for j, chip in enumerate(chips):
        copy(1 + j, (*chip, c), me).wait_recv()
        passed[j].start()
    copy(0, sibling, me).wait_recv()
    for j, chip in enumerate(chips):
        copy(4 + j, (*chip, 1 - c), me).wait_recv()
    for cp in first + passed:
        cp.wait_send()
    mine.wait()


GATHER_SEMS = [pltpu.SemaphoreType.DMA((7,)), pltpu.SemaphoreType.DMA((7,)), pltpu.SemaphoreType.DMA]


def _all_gather(blk, *, name):
    return pl.pallas_call(
        _gather_body_fn(), name=name,
        out_shape=jax.ShapeDtypeStruct((N_DEV,) + blk.shape, blk.dtype),
        in_specs=[ANY], out_specs=ANY,
        scratch_shapes=GATHER_SEMS,
    )(blk)


def _gather_body_fn():
    def body(x_ref, out_ref, send_sems, recv_sems, local_sem):
        _gather_blocks(x_ref, out_ref, send_sems, recv_sems, local_sem)
    return body


SLAB_W = 1152
SMALL_SHAPES = dict(norm1_g=(1, D_MODEL), lb_logits=(2, HGRN_HEADS * HGRN_DK), hgrn_norm_g=(1, HGRN_DK), attn_sinks=(1, ATT_HEADS),
                    norm2_g=(1, D_MODEL), conv_b=(1, D_FF), final_g=(1, D_MODEL))
CONVW_BLK = D_FF // N_DEV
CONVW_STRIDE = SLAB_W // 3


def _slab_layout():
    layout, r = {}, 0
    for nm, (nr, w) in SMALL_SHAPES.items():
        layout[nm] = []
        for i in range(nr):
            for c0 in range(0, w, SLAB_W):
                layout[nm].append((r, i, c0, min(SLAB_W, w - c0)))
                r += 1
    return layout, r


SMALL_ROWS, _N_SMALL_ROWS = _slab_layout()
CONV_ROW0 = -(-_N_SMALL_ROWS // 8) * 8
LOSS_ROW = CONV_ROW0 + N_DEV
SLAB_ROWS = LOSS_ROW + 8


def _small_step(grads, g_conv_w, loss, params, moments, variances, dev, *, name):
    names = list(SMALL_ROWS)
    n = len(names)

    def body(dev_ref, *refs):
        g_refs = dict(zip(names, refs[:n]))
        gc_ref, loss_ref = refs[n], refs[n + 1]
        base = n + 2
        w_refs, m_refs, v_refs = (dict(zip(names + ["conv_w"], refs[base + i * (n + 1):base + (i + 1) * (n + 1)])) for i in range(3))
        o = base + 3 * (n + 1)
        gath_ref, loss_out = refs[o], refs[o + 1]
        outs = {nm: refs[o + 2 + 4 * i:o + 6 + 4 * i] for i, nm in enumerate(names + ["conv_w"])}
        slab, total, send_sems, recv_sems, local_sem = refs[-5:]

        slab[...] = jnp.zeros_like(slab)
        for nm, pieces in SMALL_ROWS.items():
            for r, i, c0, w in pieces:
                slab[r:r + 1, 0:w] = g_refs[nm][i:i + 1, c0:c0 + w]
        for p in range(N_DEV):
            for j in range(3):
                slab[CONV_ROW0 + p:CONV_ROW0 + p + 1, j * CONVW_STRIDE:j * CONVW_STRIDE + CONVW_BLK] = gc_ref[j:j + 1, p * CONVW_BLK:(p + 1) * CONVW_BLK]
        slab[LOSS_ROW:LOSS_ROW + 1, 0:1] = loss_ref[...]
        _gather_blocks(slab, gath_ref, send_sems, recv_sems, local_sem)
        acc = gath_ref[0]
        for p in range(1, N_DEV):
            acc = acc + gath_ref[p]
        total[...] = acc
        loss_out[...] = total[LOSS_ROW:LOSS_ROW + 1, 0:1]

        def update(nm, g, i, c0, w):
            at = (slice(i, i + 1), slice(c0, c0 + w))
            d, mn, vn = _adamw_math(w_refs[nm][at], g, m_refs[nm][at], v_refs[nm][at])
            for ref, val in zip(outs[nm], (g, d, mn, vn)):
                ref[at] = val

        for nm, pieces in SMALL_ROWS.items():
            for r, i, c0, w in pieces:
                update(nm, total[r:r + 1, 0:w], i, c0, w)
        conv_rows = total[CONV_ROW0:CONV_ROW0 + N_DEV, :]
        rowid = lax.broadcasted_iota(jnp.int32, conv_rows.shape, 0)
        mine = jnp.sum(jnp.where(rowid == dev_ref[0], conv_rows, 0.0), axis=0, keepdims=True)
        for j in range(3):
            update("conv_w", mine[:, j * CONVW_STRIDE:j * CONVW_STRIDE + CONVW_BLK], j, 0, CONVW_BLK)

    order = names + ["conv_w"]
    ins = [grads[nm] for nm in names] + [g_conv_w, loss]
    for d in (params, moments, variances):
        ins += [d[nm] for nm in order]
    vmem = pl.BlockSpec(memory_space=pltpu.VMEM)
    out_shape = [jax.ShapeDtypeStruct((N_DEV, SLAB_ROWS, SLAB_W), F32), jax.ShapeDtypeStruct((1, 1), F32)]
    for nm in order:
        out_shape += [jax.ShapeDtypeStruct(params[nm].shape, F32)] * 4
    res = pl.pallas_call(
        body, name=name,
        grid_spec=pltpu.PrefetchScalarGridSpec(
            num_scalar_prefetch=1, grid=(1,),
            in_specs=[vmem] * len(ins), out_specs=[vmem] * len(out_shape),
            scratch_shapes=[pltpu.VMEM((SLAB_ROWS, SLAB_W), F32), pltpu.VMEM((SLAB_ROWS, SLAB_W), F32)] + GATHER_SEMS),
        out_shape=out_shape,
    )(dev, *ins)
    return res[1], {nm: tuple(res[2 + 4 * i:6 + 4 * i]) for i, nm in enumerate(order)}


HBM_SPEC = pl.BlockSpec(memory_space=pltpu.HBM)
SEM_SPEC = pl.BlockSpec(memory_space=pltpu.SEMAPHORE)
DATAFLOW_EFFECT = pltpu.SideEffectType.DATAFLOW_SIDE_EFFECTING
N_PEERS = N_DEV - 1


def _peers(x, y, c):
    return [(1 - x if r & 4 else x, 1 - y if r & 2 else y, 1 - c if r & 1 else c) for r in range(1, N_DEV)]


def _exchange_start(srcs, scatter, *, name):
    n = len(srcs)
    lands = [lax.empty(a.shape if scatter else (N_DEV,) + a.shape, a.dtype) for a in srcs]

    def body(*refs):
        src_refs, land_refs = refs[:n], refs[n:2 * n]
        send_sems, recv_sems, token = refs[2 * n], refs[2 * n + 1], refs[-1]
        x, y, c = _my_place()
        me = 4 * x + 2 * y + c
        for i in range(n):
            for r, (tx, ty, tc) in enumerate(_peers(x, y, c)):
                src = src_refs[i].at[4 * tx + 2 * ty + tc] if scatter else src_refs[i]
                pltpu.make_async_remote_copy(
                    src_ref=src, dst_ref=land_refs[i].at[me], send_sem=send_sems.at[N_PEERS * i + r],
                    recv_sem=recv_sems.at[N_PEERS * i + r], device_id=(tx, ty, tc), device_id_type=MESH).start()
        token[...] = jnp.zeros_like(token)

    thru = [pltpu.HBM(a.shape, a.dtype) for a in list(srcs) + lands]
    res = pl.pallas_call(
        body, name=name,
        out_shape=(pltpu.SemaphoreType.DMA((N_PEERS * n,)), pltpu.SemaphoreType.DMA((N_PEERS * n,)), *thru,
                   jax.ShapeDtypeStruct((8, 128), F32)),
        in_specs=[HBM_SPEC] * (2 * n),
        out_specs=(SEM_SPEC, SEM_SPEC, *([HBM_SPEC] * (2 * n)), pl.BlockSpec(memory_space=pltpu.VMEM)),
        input_output_aliases={i: 2 + i for i in range(2 * n)},
        compiler_params=pltpu.CompilerParams(has_side_effects=DATAFLOW_EFFECT),
    )(*[pltpu.with_memory_space_constraint(a, pltpu.HBM) for a in list(srcs) + lands])
    return (res[0], res[1], list(res[2:2 + n]), list(res[2 + n:2 + 2 * n]), scatter), res[-1]


def _exchange_wait(handle, after, *, name):
    send_sems, recv_sems, srcs, lands, scatter = handle
    n = len(srcs)

    def body(*refs):
        src_refs, land_refs = refs[:n], refs[n:2 * n]
        send_sems, recv_sems = refs[2 * n], refs[2 * n + 1]
        x, y, c = _my_place()
        for i in range(n):
            for r in range(N_PEERS):
                src = src_refs[i].at[0] if scatter else src_refs[i]
                cp = pltpu.make_async_remote_copy(
                    src_ref=src, dst_ref=land_refs[i].at[0], send_sem=send_sems.at[N_PEERS * i + r],
                    recv_sem=recv_sems.at[N_PEERS * i + r], device_id=(x, y, c), device_id_type=MESH)
                cp.wait_send()
                cp.wait_recv()

    thru = [pltpu.HBM(a.shape, a.dtype) for a in srcs + lands]
    res = pl.pallas_call(
        body, name=name, out_shape=tuple(thru),
        in_specs=[HBM_SPEC] * (2 * n) + [SEM_SPEC, SEM_SPEC, ANY], out_specs=tuple([HBM_SPEC] * (2 * n)),
        input_output_aliases={i: i for i in range(2 * n)},
        compiler_params=pltpu.CompilerParams(has_side_effects=DATAFLOW_EFFECT),
    )(*srcs, *lands, send_sems, recv_sems, after)
    return list(res[:n]), list(res[n:])


def _with_own(land, own, me):
    return lax.dynamic_update_index_in_dim(land, own, me, 0)


def _adamw_math(w, g, m, v):
    m = ADAM_B1 * m + (1.0 - ADAM_B1) * g
    v = ADAM_B2 * v + (1.0 - ADAM_B2) * (g * g)
    m_hat = m / (1.0 - ADAM_B1 ** ADAM_STEP)
    v_hat = v / (1.0 - ADAM_B2 ** ADAM_STEP)
    delta = -ADAM_LR * (m_hat / (jnp.sqrt(v_hat) + ADAM_EPS) + ADAM_WD * w)
    return delta, m, v


def _adamw_sum(parts, w, m, v, *, name):
    shape = w.shape
    R, n = shape[-2], shape[-1]
    w, m, v = (t.reshape(R, n) for t in (w, m, v))
    tr = _pick(R, (256, 176, 128))

    def body(p_ref, w_ref, m_ref, v_ref, g_ref, d_ref, mo_ref, vo_ref):
        g = p_ref[0].astype(F32)
        for p in range(1, N_DEV):
            g = g + p_ref[p].astype(F32)
        d, mn, vn = _adamw_math(w_ref[...], g, m_ref[...], v_ref[...])
        g_ref[...] = g
        d_ref[...] = d
        mo_ref[...] = mn
        vo_ref[...] = vn

    row = pl.BlockSpec((tr, n), lambda i: (i, 0))
    outs = pl.pallas_call(
        body, name=name, grid=(R // tr,),
        in_specs=[pl.BlockSpec((N_DEV, tr, n), lambda i: (0, i, 0)), row, row, row],
        out_specs=[row, row, row, row],
        out_shape=[jax.ShapeDtypeStruct((R, n), F32)] * 4,
        compiler_params=_params("parallel"),
    )(parts, w, m, v)
    return [t.reshape(shape) for t in outs]


def _lb_bwd(dlb, lb, *, name):
    def body(d_ref, lb_ref, o_ref):
        t = d_ref[...] * lb_ref[...] * (1.0 - lb_ref[...])
        o_ref[0:1, :] = t
        o_ref[1:2, :] = -t

    return pl.pallas_call(body, name=name, out_shape=jax.ShapeDtypeStruct((2, lb.shape[1]), F32))(dlb, lb)


DOWN_BLK, ROW_BLK = D_FF // N_DEV, D_MODEL // N_DEV
CONV_BITS_SHAPE = (16, 256)


def _cols_from_blocks(blocks):
    n, rows, width = blocks.shape
    return blocks.transpose(1, 0, 2).reshape(rows, n * width)


def _blocks_from_cols(full):
    rows, cols = full.shape
    return full.reshape(rows, N_DEV, cols // N_DEV).transpose(1, 0, 2)


def kernel(x, positions, norm1_g, w_in, lb_logits, hgrn_norm_g, w_a, attn_sinks, w_b, w_out, norm2_g, w_ffn_in, conv_w, conv_b, w_down, final_g, loss_target, m_norm1_g, m_w_in, m_lb_logits, m_hgrn_norm_g, m_w_a, m_attn_sinks, m_w_b, m_w_out, m_norm2_g, m_w_ffn_in, m_conv_w, m_conv_b, m_w_down, m_final_g, v_norm1_g, v_w_in, v_lb_logits, v_hgrn_norm_g, v_w_a, v_attn_sinks, v_w_b, v_w_out, v_norm2_g, v_w_ffn_in, v_conv_w, v_conv_b, v_w_down, v_final_g):
    xi, yi, ci = _my_place()
    dev = 4 * xi + 2 * yi + ci

    w_in_blocks = _all_gather(w_in[0].astype(BF16), name="ag_w_in")
    conv_bits = lax.bitcast_convert_type(conv_w, BF16).reshape(-1)
    conv_bits = jnp.pad(conv_bits, (0, CONV_BITS_SHAPE[0] * CONV_BITS_SHAPE[1] - conv_bits.shape[0])).reshape(CONV_BITS_SHAPE)
    gather_handles = {}
    gather_handles["mix"], tok_mix = _exchange_start([w_a[0].astype(BF16), w_b[0].astype(BF16), w_out[0].astype(BF16)], False, name="ag_mix_start")
    gather_handles["ffn"], tok_ffn = _exchange_start([w_ffn_in[0].astype(BF16), w_down[0].astype(BF16), conv_bits], False, name="ag_ffn_start")
    start_token = tok_mix + tok_ffn

    def rest_weights(group, after):
        own, lands = _exchange_wait(gather_handles[group], after, name="ag_" + group + "_wait")
        full = [_with_own(l, o, dev) for l, o in zip(lands, own)]
        if group == "mix":
            return dict(zip(("w_a", "w_b", "w_out"), [t.reshape(D_MODEL, D_MODEL) for t in full]))
        bits = full[2].reshape(N_DEV, -1)[:, :3 * CONVW_BLK * 2].reshape(N_DEV, 3, CONVW_BLK, 2)
        return dict(w_ffn=_cols_from_blocks(full[0]), w_down=full[1].reshape(D_FF, D_MODEL),
                    conv_w=_cols_from_blocks(lax.bitcast_convert_type(bits, F32)))

    handles = {}

    def emit(group, gr):
        if group == "ffn":
            srcs = [_blocks_from_cols(gr["w_ffn"]), gr["w_down"].reshape(N_DEV, DOWN_BLK, D_MODEL)]
        elif group == "mix":
            srcs = [gr[n].reshape(N_DEV, ROW_BLK, D_MODEL) for n in ("w_out", "w_a", "w_b")]
        else:
            srcs = [_blocks_from_w_in(gr["w_in"], name="dw_in_blocks")]
        handles[group], token = _exchange_start(srcs, True, name="rs_" + group + "_start")
        return token

    small = dict(norm1_g=norm1_g, lb_logits=lb_logits, hgrn_norm_g=hgrn_norm_g, attn_sinks=attn_sinks, norm2_g=norm2_g,
                 conv_b=conv_b, final_g=final_g)
    w_in_full = _w_in_from_blocks(w_in_blocks, name="w_in_layout")
    loss, grad_x, g = _local_step(x, positions, loss_target, small, w_in_full, rest_weights, emit, start_token)

    def parts_of(group, after):
        srcs, lands = _exchange_wait(handles[group], after, name="rs_" + group + "_wait")
        return [_with_own(l, lax.dynamic_index_in_dim(s, dev, 0, keepdims=False), dev) for s, l in zip(srcs, lands)]

    p_ffn, p_down = parts_of("ffn", grad_x)
    p_out, p_a, p_b = parts_of("mix", grad_x)
    (p_in,) = parts_of("in", grad_x)
    big = dict(
        w_in=_adamw_sum(p_in, w_in, m_w_in, v_w_in, name="adamw_w_in"),
        w_a=_adamw_sum(p_a, w_a, m_w_a, v_w_a, name="adamw_w_a"),
        w_b=_adamw_sum(p_b, w_b, m_w_b, v_w_b, name="adamw_w_b"),
        w_out=_adamw_sum(p_out, w_out, m_w_out, v_w_out, name="adamw_w_out"),
        w_ffn_in=_adamw_sum(p_ffn, w_ffn_in, m_w_ffn_in, v_w_ffn_in, name="adamw_w_ffn_in"),
        w_down=_adamw_sum(p_down, w_down, m_w_down, v_w_down, name="adamw_w_down"),
    )

    row = lambda t: t.reshape(1, -1) if t.ndim == 1 else t
    shard = lambda t: t.reshape(3, CONVW_BLK)
    sm_g = {nm: g[nm] for nm in SMALL_ROWS}
    sm_w = dict(norm1_g=norm1_g, lb_logits=lb_logits, hgrn_norm_g=hgrn_norm_g, attn_sinks=attn_sinks, norm2_g=norm2_g,
                conv_b=conv_b, final_g=row(final_g), conv_w=shard(conv_w))
    sm_m = dict(norm1_g=m_norm1_g, lb_logits=m_lb_logits, hgrn_norm_g=m_hgrn_norm_g, attn_sinks=m_attn_sinks, norm2_g=m_norm2_g,
                conv_b=m_conv_b, final_g=row(m_final_g), conv_w=shard(m_conv_w))
    sm_v = dict(norm1_g=v_norm1_g, lb_logits=v_lb_logits, hgrn_norm_g=v_hgrn_norm_g, attn_sinks=v_attn_sinks, norm2_g=v_norm2_g,
                conv_b=v_conv_b, final_g=row(v_final_g), conv_w=shard(v_conv_w))
    loss_total, sm_out = _small_step(sm_g, g["conv_w"], loss, sm_w, sm_m, sm_v, dev.astype(jnp.int32).reshape(1), name="small_step")
    shapes = dict(final_g=final_g.shape, conv_w=conv_w.shape)

    names = ("norm1_g", "w_in", "lb_logits", "hgrn_norm_g", "w_a", "attn_sinks", "w_b", "w_out", "norm2_g", "w_ffn_in", "conv_w", "conv_b", "w_down", "final_g")
    outs = [loss_total.reshape(()), grad_x]
    for kind in range(4):
        outs += [big[n][kind] if n in big else sm_out[n][kind].reshape(shapes.get(n, sm_out[n][kind].shape)) for n in names]
    return tuple(outs)
```

```python
import functools

import jax
import jax.numpy as jnp
from jax import lax
from jax.experimental import pallas as pl
from jax.experimental.pallas import tpu as pltpu

F32 = jnp.float32
BF16 = jnp.bfloat16

D_MODEL = 1024
HGRN_HEADS = 8
HGRN_DK = 128
CHUNK = 64
ATT_HEADS = 16
ATT_KV_HEADS = 2
ATT_HD = 64
ATT_GROUP = ATT_HEADS // ATT_KV_HEADS
WINDOW = 128
ROPE_DIM = ATT_HD // 4
ROPE_THETA = 500000.0
D_FF = 2816
EPS = 1e-6
NEG_INF = -1e30
N_DEV = 8

ADAM_LR = 0.001
ADAM_B1 = 0.9
ADAM_B2 = 0.999
ADAM_EPS = 1e-08
ADAM_WD = 0.01
ADAM_STEP = 10

MESH = pl.DeviceIdType.MESH
ANY = pl.BlockSpec(memory_space=pl.ANY)


def _pick(n, cands):
    for c in cands:
        if n % c == 0:
            return c
    return n


def _sigmoid(x):
    return 0.5 * jnp.tanh(0.5 * x) + 0.5


def _silu(x):
    hx = 0.5 * x
    return hx * jnp.tanh(hx) + hx


def _rms(x, g):
    return x * lax.rsqrt(jnp.mean(x * x, axis=-1, keepdims=True) + EPS) * g


def _dot(a, b, dims):
    return lax.dot_general(a, b, (dims, ((), ())), preferred_element_type=F32)


def _nn(a, b):
    return _dot(a, b, ((1,), (0,)))


def _nt(a, b):
    return _dot(a, b, ((1,), (1,)))


def _tn(a, b):
    return _dot(a, b, ((0,), (0,)))


def _params(*sem):
    return pltpu.CompilerParams(dimension_semantics=sem, vmem_limit_bytes=56 * 1024 * 1024)


def _matmul(a, b, *, ta=False, tb=False, out_dtype=F32, addend=None, after=None, into=None, o_noff=0, name, tm, tn, tk=None,
            n_extent=None, b_koff=0, b_noff=0):
    M, K = (a.shape[1], a.shape[0]) if ta else a.shape
    N = n_extent or (b.shape[0] if tb else b.shape[1])
    tm, tn, tk = min(tm, M), min(tn, N), min(tk or K, K)
    assert M % tm == 0 and N % tn == 0 and K % tk == 0, (name, M, N, K, tm, tn, tk)
    nk = K // tk
    use_scratch = nk > 1 and out_dtype != F32
    grid = (M // tm, N // tn, nk)
    a_spec = pl.BlockSpec((tk, tm), lambda i, j, k: (k, i)) if ta else pl.BlockSpec((tm, tk), lambda i, j, k: (i, k))
    b_spec = pl.BlockSpec((tn, tk), lambda i, j, k: (j + b_noff, k + b_koff)) if tb else pl.BlockSpec((tk, tn), lambda i, j, k: (k + b_koff, j + b_noff))
    o_spec = pl.BlockSpec((tm, tn), lambda i, j, k: (i, j))
    dims = ((0 if ta else 1,), (1 if tb else 0,))
    has_add = addend is not None

    n_in = 2 + has_add + (after is not None) + (into is not None)

    def body(*refs):
        a_ref, b_ref = refs[:2]
        c_ref = refs[2] if has_add else None
        o_ref = refs[n_in]
        part = _dot(a_ref[...], b_ref[...], dims)
        if nk == 1:
            if has_add:
                part = part + c_ref[...].astype(F32)
            o_ref[...] = part.astype(out_dtype)
        else:
            acc_ref = refs[-1] if use_scratch else o_ref
            k = pl.program_id(2)

            @pl.when(k == 0)
            def _():
                acc_ref[...] = part + c_ref[...].astype(F32) if has_add else part

            @pl.when(k > 0)
            def _():
                acc_ref[...] += part

            if use_scratch:
                @pl.when(k == nk - 1)
                def _():
                    o_ref[...] = acc_ref[...].astype(out_dtype)

    in_specs = [a_spec, b_spec] + ([o_spec] if has_add else [])
    args = (a, b) + ((addend,) if has_add else ())
    if after is not None:
        in_specs.append(pl.BlockSpec(after.shape, lambda i, j, k: (0, 0)))
        args += (after,)
    aliases = {}
    if into is not None:
        in_specs.append(ANY)
        args += (into,)
        aliases = {len(args) - 1: 0}
        o_spec = pl.BlockSpec((tm, tn), lambda i, j, k: (i, j + o_noff))
    return pl.pallas_call(
        body,
        name=name,
        grid=grid,
        in_specs=in_specs,
        out_specs=o_spec,
        out_shape=jax.ShapeDtypeStruct((M, N) if into is None else into.shape, out_dtype),
        input_output_aliases=aliases,
        scratch_shapes=[pltpu.VMEM((tm, tn), F32)] if use_scratch else [],
        compiler_params=_params("parallel", "parallel", "arbitrary"),
    )(*args)


def _row_spec(tm, n):
    return pl.BlockSpec((tm, n), lambda i: (i, 0))


def _full_spec(shape):
    return pl.BlockSpec(shape, lambda i: tuple(0 for _ in shape))


def _norm_cast(x, g, *, name):
    T, D = x.shape
    tm = _pick(T, (512, 256, 128))

    def body(x_ref, g_ref, u_ref):
        u_ref[...] = _rms(x_ref[...], g_ref[...]).astype(BF16)

    return pl.pallas_call(
        body, name=name, grid=(T // tm,),
        in_specs=[_row_spec(tm, D), _full_spec((1, D))],
        out_specs=_row_spec(tm, D),
        out_shape=jax.ShapeDtypeStruct((T, D), BF16),
        compiler_params=_params("parallel"),
    )(x, g)


def _norm_bwd_add(x, g, du, dres, *, with_bf16=True, name):
    T, D = x.shape
    tm = _pick(T, (512, 256, 128))

    def body(x_ref, g_ref, du_ref, dr_ref, dx_ref, *rest):
        dg_ref = rest[-1]
        _, vjp = jax.vjp(_rms, x_ref[...], g_ref[...])
        dx, dg = vjp(du_ref[...].astype(F32))
        dx = dx + dr_ref[...]
        dx_ref[...] = dx
        if with_bf16:
            rest[0][...] = dx.astype(BF16)

        @pl.when(pl.program_id(0) == 0)
        def _():
            dg_ref[...] = jnp.zeros_like(dg_ref)

        dg_ref[...] += dg

    row = _row_spec(tm, D)
    return pl.pallas_call(
        body, name=name, grid=(T // tm,),
        in_specs=[row, _full_spec((1, D)), row, row],
        out_specs=[row] + ([row] if with_bf16 else []) + [_full_spec((1, D))],
        out_shape=[jax.ShapeDtypeStruct((T, D), F32)] + ([jax.ShapeDtypeStruct((T, D), BF16)] if with_bf16 else []) + [jax.ShapeDtypeStruct((1, D), F32)],
        compiler_params=_params("arbitrary"),
    )(x, g, du, dres)


def _final_loss_bwd(h2, g, target, *, name):
    T, D = h2.shape
    tm = _pick(T, (512, 256, 128))

    def body(h_ref, g_ref, t_ref, dx_ref, dxb_ref, dg_ref, loss_ref):
        y, vjp = jax.vjp(_rms, h_ref[...], g_ref[...])
        err = y - t_ref[...]
        dx, dg = vjp(err * (1.0 / D))
        dx_ref[...] = dx
        dxb_ref[...] = dx.astype(BF16)

        @pl.when(pl.program_id(0) == 0)
        def _():
            dg_ref[...] = jnp.zeros_like(dg_ref)
            loss_ref[...] = jnp.zeros_like(loss_ref)

        dg_ref[...] += dg
        loss_ref[...] += (0.5 / D) * jnp.sum(jnp.sum(err * err, axis=1, keepdims=True), axis=0, keepdims=True)

    return pl.pallas_call(
        body, name=name, grid=(T // tm,),
        in_specs=[_row_spec(tm, D), _full_spec((1, D)), _row_spec(tm, D)],
        out_specs=[_row_spec(tm, D), _row_spec(tm, D), _full_spec((1, D)), _full_spec((1, 1))],
        out_shape=[jax.ShapeDtypeStruct((T, D), F32), jax.ShapeDtypeStruct((T, D), BF16), jax.ShapeDtypeStruct((1, D), F32), jax.ShapeDtypeStruct((1, 1), F32)],
        compiler_params=_params("arbitrary"),
    )(h2, g, target)


def _merge_fn(gates, a, b):
    ga = gates[:, :D_MODEL].astype(F32)
    gb = gates[:, D_MODEL:].astype(F32)
    return _sigmoid(ga) * a.astype(F32) + _sigmoid(gb) * b.astype(F32)


def _gates_spec(tm):
    return pl.BlockSpec((tm, W_GATES), lambda i: (i, O_GATES // W_GATES))


def _merge_fwd(z, a, b, *, name):
    T = a.shape[0]
    tm = _pick(T, (512, 256, 128))

    def body(g_ref, a_ref, b_ref, o_ref):
        o_ref[...] = _merge_fn(g_ref[...], a_ref[...], b_ref[...]).astype(BF16)

    return pl.pallas_call(
        body, name=name, grid=(T // tm,),
        in_specs=[_gates_spec(tm), _row_spec(tm, D_MODEL), _row_spec(tm, D_MODEL)],
        out_specs=_row_spec(tm, D_MODEL),
        out_shape=jax.ShapeDtypeStruct((T, D_MODEL), BF16),
        compiler_params=_params("parallel"),
    )(z, a, b)


def _merge_bwd(z, a, b, dmerged, dz, *, name):
    T = a.shape[0]
    tm = _pick(T, (512, 256, 128))

    def body(g_ref, a_ref, b_ref, dm_ref, dz_in, dg_ref, da_ref, db_ref):
        g = g_ref[...].astype(F32)
        dm = dm_ref[...].astype(F32)
        sa = _sigmoid(g[:, :D_MODEL])
        sb = _sigmoid(g[:, D_MODEL:])
        da_ref[...] = (dm * sa).astype(BF16)
        db_ref[...] = (dm * sb).astype(BF16)
        dg_ref[:, :D_MODEL] = (dm * a_ref[...].astype(F32) * sa * (1.0 - sa)).astype(BF16)
        dg_ref[:, D_MODEL:] = (dm * b_ref[...].astype(F32) * sb * (1.0 - sb)).astype(BF16)

    return pl.pallas_call(
        body, name=name, grid=(T // tm,),
        in_specs=[_gates_spec(tm), _row_spec(tm, D_MODEL), _row_spec(tm, D_MODEL), _row_spec(tm, D_MODEL), ANY],
        out_specs=[_gates_spec(tm), _row_spec(tm, D_MODEL), _row_spec(tm, D_MODEL)],
        out_shape=[jax.ShapeDtypeStruct(dz.shape, BF16), jax.ShapeDtypeStruct((T, D_MODEL), BF16), jax.ShapeDtypeStruct((T, D_MODEL), BF16)],
        input_output_aliases={4: 0},
        compiler_params=_params("parallel"),
    )(z, a, b, dmerged, dz)


CONV_TC = 256


def _shift_down(x, n, rows):
    return jnp.where(rows >= n, pltpu.roll(x, n, 0), 0.0)


def _shift_up(x, n, rows, S):
    return jnp.where(rows < S - n, pltpu.roll(x, S - n, 0), 0.0)


def _conv_act_fwd(gu, conv_w, conv_b, *, name):
    B, S, _ = gu.shape
    tc = CONV_TC
    nc = D_FF // tc

    def body(g_ref, up_ref, w_ref, b_ref, o_ref, a_ref):
        g = g_ref[...].astype(F32)
        rows = lax.broadcasted_iota(jnp.int32, g.shape, 0)
        w = w_ref[...]
        a = w[2:3] * g + w[1:2] * _shift_down(g, 1, rows) + w[0:1] * _shift_down(g, 2, rows) + b_ref[...]
        o_ref[...] = (_silu(a) * up_ref[...].astype(F32)).astype(BF16)
        a_ref[...] = a.astype(BF16)

    col = pl.BlockSpec((None, S, tc), lambda b, j: (b, 0, j))
    return pl.pallas_call(
        body, name=name, grid=(B, nc),
        in_specs=[col,
                  pl.BlockSpec((None, S, tc), lambda b, j: (b, 0, j + nc)),
                  pl.BlockSpec((3, tc), lambda b, j: (0, j)),
                  pl.BlockSpec((1, tc), lambda b, j: (0, j))],
        out_specs=[col, col],
        out_shape=[jax.ShapeDtypeStruct((B, S, D_FF), BF16)] * 2,
        compiler_params=_params("parallel", "parallel"),
    )(gu, gu, conv_w, conv_b)


def _conv_act_bwd(gu, a_pre, conv_w, dact, *, name):
    B, S, _ = gu.shape
    tc = CONV_TC
    nc = D_FF // tc

    def body(g_ref, up_ref, a_ref, w_ref, da_ref, dg_ref, dup_ref, dw_ref, db_ref):
        g = g_ref[...].astype(F32)
        up = up_ref[...].astype(F32)
        a = a_ref[...].astype(F32)
        dact = da_ref[...].astype(F32)
        rows = lax.broadcasted_iota(jnp.int32, g.shape, 0)
        w = w_ref[...]
        sg = _sigmoid(a)
        dup_ref[...] = (dact * a * sg).astype(BF16)
        da = dact * up * sg * (1.0 + a * (1.0 - sg))
        da1 = _shift_up(da, 1, rows, S)
        da2 = _shift_up(da, 2, rows, S)
        dg_ref[...] = (w[2:3] * da + w[1:2] * da1 + w[0:1] * da2).astype(BF16)

        @pl.when(pl.program_id(1) == 0)
        def _():
            dw_ref[...] = jnp.zeros_like(dw_ref)
            db_ref[...] = jnp.zeros_like(db_ref)

        dw_ref[0:1, :] += jnp.sum(da2 * g, axis=0, keepdims=True)
        dw_ref[1:2, :] += jnp.sum(da1 * g, axis=0, keepdims=True)
        dw_ref[2:3, :] += jnp.sum(da * g, axis=0, keepdims=True)
        db_ref[...] += jnp.sum(da, axis=0, keepdims=True)

    col = pl.BlockSpec((None, S, tc), lambda j, b: (b, 0, j))
    return pl.pallas_call(
        body, name=name, grid=(nc, B),
        in_specs=[col,
                  pl.BlockSpec((None, S, tc), lambda j, b: (b, 0, j + nc)),
                  col,
                  pl.BlockSpec((3, tc), lambda j, b: (0, j)),
                  col],
        out_specs=[col, col, pl.BlockSpec((3, tc), lambda j, b: (0, j)), pl.BlockSpec((1, tc), lambda j, b: (0, j))],
        out_shape=[jax.ShapeDtypeStruct((B, S, D_FF), BF16), jax.ShapeDtypeStruct((B, S, D_FF), BF16),
                   jax.ShapeDtypeStruct((3, D_FF), F32), jax.ShapeDtypeStruct((1, D_FF), F32)],
        compiler_params=_params("parallel", "arbitrary"),
    )(gu, gu, a_pre, conv_w, dact)


HGRN_CPB = 4
HF = HGRN_HEADS * HGRN_DK


def _tri(n, upper=False):
    r = lax.broadcasted_iota(jnp.int32, (n, n), 0)
    c = lax.broadcasted_iota(jnp.int32, (n, n), 1)
    return (c >= r) if upper else (r >= c)


def _hs(h):
    return slice(h * HGRN_DK, (h + 1) * HGRN_DK)


def _cumsum_rows(tri_b, x):
    hi = x.astype(BF16)
    lo = (x - hi.astype(F32)).astype(BF16)
    return _nn(tri_b, hi) + _nn(tri_b, lo)


def _hgrn_pre(q, fz, lb, tril_b):
    qf = _silu(q)
    sg = _sigmoid(fz)
    f = lb + (1.0 - lb) * sg
    k = 1.0 - f
    b = _cumsum_rows(tril_b, jnp.log2(f))
    bref = b[CHUNK // 2:CHUNK // 2 + 1, :]
    blast = b[CHUNK - 1:CHUNK, :]
    e1 = jnp.exp2(b - bref)
    e2 = jnp.exp2(bref - b)
    e3 = e1 * jnp.exp2(bref)
    e4 = e2 * jnp.exp2(blast - bref)
    dec = jnp.exp2(blast)
    return sg, f, (e1, e2, e3, e4), qf * e1, k * e2, qf * e3, k * e4, dec


def _hgrn_fwd(zh, lb, gn, *, name):
    B, S, _ = zh.shape
    cpb = HGRN_CPB
    ts = cpb * CHUNK
    nblk = S // ts

    def body(z_ref, lb_ref, gn_ref, o_ref, st_ref, state):
        @pl.when(pl.program_id(1) == 0)
        def _():
            state[...] = jnp.zeros_like(state)

        H = HGRN_HEADS
        causal = _tri(CHUNK)
        tril_b = causal.astype(BF16)
        lb = lb_ref[...]
        for c in range(cpb):
            rows = slice(c * CHUNK, (c + 1) * CHUNK)
            q = z_ref[rows, 0:HF].astype(F32)
            fz = z_ref[rows, HF:2 * HF].astype(F32)
            v = z_ref[rows, 2 * HF:3 * HF]
            hg = z_ref[rows, 3 * HF:4 * HF].astype(F32)
            _, _, _, q_in, k_in, q_out, k_st, dec = _hgrn_pre(q, fz, lb, tril_b)
            q_in, k_in, q_out, k_st = (t.astype(BF16) for t in (q_in, k_in, q_out, k_st))
            a = [jnp.where(causal, _nt(q_in[:, _hs(h)], k_in[:, _hs(h)]), 0.0).astype(BF16) for h in range(H)]
            st = [state[h] for h in range(H)]
            for h in range(H):
                st_ref[c, h] = st[h]
            o = [_nn(a[h], v[:, _hs(h)]) + _nt(q_out[:, _hs(h)], st[h].astype(BF16)) for h in range(H)]
            for h in range(H):
                state[h] = st[h] * dec[:, _hs(h)] + _tn(v[:, _hs(h)], k_st[:, _hs(h)])
            gate = _silu(hg)
            for h in range(H):
                o_ref[rows, _hs(h)] = (_rms(o[h], gn_ref[...]) * gate[:, _hs(h)]).astype(BF16)

    return pl.pallas_call(
        body, name=name, grid=(B, nblk),
        in_specs=[pl.BlockSpec((None, ts, 4 * HF), lambda b, s: (b, s, 0)),
                  pl.BlockSpec((1, HF), lambda b, s: (0, 0)),
                  pl.BlockSpec((1, HGRN_DK), lambda b, s: (0, 0))],
        out_specs=[pl.BlockSpec((None, ts, HF), lambda b, s: (b, s, 0)),
                   pl.BlockSpec((None, cpb, HGRN_HEADS, HGRN_DK, HGRN_DK), lambda b, s: (b, s, 0, 0, 0))],
        out_shape=[jax.ShapeDtypeStruct((B, S, HF), BF16),
                   jax.ShapeDtypeStruct((B, S // CHUNK, HGRN_HEADS, HGRN_DK, HGRN_DK), F32)],
        scratch_shapes=[pltpu.VMEM((HGRN_HEADS, HGRN_DK, HGRN_DK), F32)],
        compiler_params=_params("arbitrary", "arbitrary"),
    )(zh, lb, gn)


def _hgrn_bwd(zh, lb, gn, states, doa, dz, *, name):
    B, S, _ = zh.shape
    cpb = HGRN_CPB
    ts = cpb * CHUNK
    nblk = S // ts
    rev = lambda b, s: (b, nblk - 1 - s, 0)

    def body(z_ref, lb_ref, gn_ref, st_ref, do_ref, dz_in, dz_ref, dlb_ref, dgn_ref, dstate):
        @pl.when(pl.program_id(1) == 0)
        def _():
            dstate[...] = jnp.zeros_like(dstate)

        @pl.when((pl.program_id(0) == 0) & (pl.program_id(1) == 0))
        def _():
            dlb_ref[...] = jnp.zeros_like(dlb_ref)
            dgn_ref[...] = jnp.zeros_like(dgn_ref)

        H = HGRN_HEADS
        cat = lambda xs: jnp.concatenate(xs, axis=1)
        causal = _tri(CHUNK)
        tril_b = causal.astype(BF16)
        triu_b = _tri(CHUNK, upper=True).astype(BF16)
        rowid = lax.broadcasted_iota(jnp.int32, (CHUNK, HF), 0)
        lb = lb_ref[...]
        gn = gn_ref[...]
        for c in reversed(range(cpb)):
            rows = slice(c * CHUNK, (c + 1) * CHUNK)
            q = z_ref[rows, 0:HF].astype(F32)
            fz = z_ref[rows, HF:2 * HF].astype(F32)
            v = z_ref[rows, 2 * HF:3 * HF]
            hg = z_ref[rows, 3 * HF:4 * HF].astype(F32)
            sg, f, (e1, e2, e3, e4), q_in, k_in, q_out, k_st, dec = _hgrn_pre(q, fz, lb, tril_b)
            q_in_b, k_in_b, q_out_b, k_st_b = (t.astype(BF16) for t in (q_in, k_in, q_out, k_st))
            a_b = [jnp.where(causal, _nt(q_in_b[:, _hs(h)], k_in_b[:, _hs(h)]), 0.0).astype(BF16) for h in range(H)]
            st = [st_ref[c, h] for h in range(H)]
            st_b = [t.astype(BF16) for t in st]
            o = [_nn(a_b[h], v[:, _hs(h)]) + _nt(q_out_b[:, _hs(h)], st_b[h]) for h in range(H)]
            dout = do_ref[rows, :].astype(F32)
            shg = _sigmoid(hg)
            gate = hg * shg
            do_l, dgn_acc = [], jnp.zeros_like(gn)
            for h in range(H):
                _, norm_vjp = jax.vjp(_rms, o[h], gn)
                d_o, d_gn = norm_vjp(dout[:, _hs(h)] * gate[:, _hs(h)])
                do_l.append(d_o)
                dgn_acc = dgn_acc + d_gn
            dgn_ref[...] += dgn_acc
            on = cat([_rms(o[h], gn) for h in range(H)])
            dhg = dout * on * shg * (1.0 + hg * (1.0 - shg))
            do_b = [t.astype(BF16) for t in do_l]
            dst = [dstate[h] for h in range(H)]
            dst_b = [t.astype(BF16) for t in dst]
            da_b = [jnp.where(causal, _nt(do_b[h], v[:, _hs(h)]), 0.0).astype(BF16) for h in range(H)]
            dv = cat([_tn(a_b[h], do_b[h]) + _nt(k_st_b[:, _hs(h)], dst_b[h]) for h in range(H)])
            dq_in = cat([_nn(da_b[h], k_in_b[:, _hs(h)]) for h in range(H)])
            dk_in = cat([_tn(da_b[h], q_in_b[:, _hs(h)]) for h in range(H)])
            dq_out = cat([_nn(do_b[h], st_b[h]) for h in range(H)])
            dk_st = cat([_nn(v[:, _hs(h)], dst_b[h]) for h in range(H)])
            ddec = cat([jnp.sum(st[h] * dst[h], axis=0, keepdims=True) for h in range(H)])
            for h in range(H):
                dstate[h] = dst[h] * dec[:, _hs(h)] + _tn(do_b[h], q_out_b[:, _hs(h)])
            t_qin = dq_in * q_in
            t_kin = dk_in * k_in
            t_kst = dk_st * k_st
            db = t_qin - t_kin + dq_out * q_out - t_kst
            dbref = jnp.sum(t_kin - t_qin, axis=0, keepdims=True)
            dblast = jnp.sum(t_kst, axis=0, keepdims=True) + ddec * dec
            db = db + jnp.where(rowid == CHUNK // 2, dbref, 0.0) + jnp.where(rowid == CHUNK - 1, dblast, 0.0)
            dlogf = _cumsum_rows(triu_b, db)
            dqf = dq_in * e1 + dq_out * e3
            dk = dk_in * e2 + dk_st * e4
            df = dlogf / f - dk
            dfz = df * (1.0 - lb) * sg * (1.0 - sg)
            dlb_ref[...] += jnp.sum(df * (1.0 - sg), axis=0, keepdims=True)
            sq = _sigmoid(q)
            dq = dqf * sq * (1.0 + q * (1.0 - sq))
            dz_ref[rows, 0:HF] = dq.astype(BF16)
            dz_ref[rows, HF:2 * HF] = dfz.astype(BF16)
            dz_ref[rows, 2 * HF:3 * HF] = dv.astype(BF16)
            dz_ref[rows, 3 * HF:4 * HF] = dhg.astype(BF16)

    return pl.pallas_call(
        body, name=name, grid=(B, nblk),
        in_specs=[pl.BlockSpec((None, ts, 4 * HF), rev),
                  pl.BlockSpec((1, HF), lambda b, s: (0, 0)),
                  pl.BlockSpec((1, HGRN_DK), lambda b, s: (0, 0)),
                  pl.BlockSpec((None, cpb, HGRN_HEADS, HGRN_DK, HGRN_DK), lambda b, s: (b, nblk - 1 - s, 0, 0, 0)),
                  pl.BlockSpec((None, ts, HF), rev),
                  ANY],
        out_specs=[pl.BlockSpec((None, ts, 4 * HF), rev),
                   pl.BlockSpec((1, HF), lambda b, s: (0, 0)),
                   pl.BlockSpec((1, HGRN_DK), lambda b, s: (0, 0))],
        out_shape=[jax.ShapeDtypeStruct(dz.shape, BF16),
                   jax.ShapeDtypeStruct((1, HF), F32),
                   jax.ShapeDtypeStruct((1, HGRN_DK), F32)],
        input_output_aliases={5: 0},
        scratch_shapes=[pltpu.VMEM((HGRN_HEADS, HGRN_DK, HGRN_DK), F32)],
        compiler_params=_params("arbitrary", "arbitrary"),
    )(zh, lb, gn, states, doa, dz)


KV_W = ATT_KV_HEADS * ATT_HD
ATT_SCALE = ATT_HD ** -0.5


def _rope(x, cos, sin, inverse=False):
    half = ROPE_DIM // 2
    outs = []
    for p in range(x.shape[1] // 128):
        xp = x[:, p * 128:(p + 1) * 128]
        lane = lax.broadcasted_iota(jnp.int32, xp.shape, 1) % ATT_HD
        sw = jnp.where(lane < half, pltpu.roll(xp, 128 - half, 1), pltpu.roll(xp, half, 1))
        outs.append(xp * cos - sw * sin if inverse else xp * cos + sw * sin)
    return outs[0] if len(outs) == 1 else jnp.concatenate(outs, axis=1)


PAIRS_PER_KV = ATT_GROUP // 2


def _swap_halves(x):
    return pltpu.roll(x, ATT_HD, 1)


def _kv_padded(t, low):
    sw = _swap_halves(t)
    zero = jnp.zeros_like(t)
    out = []
    for g in range(ATT_KV_HEADS):
        in_low, in_high = (t, sw) if g == 0 else (sw, t)
        out.append((jnp.where(low, in_low, zero).astype(BF16), jnp.where(low, zero, in_high).astype(BF16)))
    return out


def _swa_mask(first_block):
    qi = lax.broadcasted_iota(jnp.int32, (WINDOW, 2 * WINDOW), 0)
    mi = lax.broadcasted_iota(jnp.int32, (WINDOW, 2 * WINDOW), 1)
    band = (mi > qi) & (mi <= qi + WINDOW)
    return band & (jnp.logical_not(first_block) | (mi >= WINDOW))


def _swa_specs(nb):
    cur = lambda b, i: (b, i, 0)
    prev = lambda b, i: (b, jnp.maximum(i - 1, 0), 0)
    return cur, prev


def _swa_z_specs():
    q = pl.BlockSpec((None, WINDOW, W_AQ), lambda b, i: (b, i, O_AQ // W_AQ))
    kv_prev = pl.BlockSpec((None, WINDOW, W_AKV), lambda b, i: (b, jnp.maximum(i - 1, 0), O_AKV // W_AKV))
    kv_cur = pl.BlockSpec((None, WINDOW, W_AKV), lambda b, i: (b, i, O_AKV // W_AKV))
    return q, kv_prev, kv_cur


def _swa_fwd(z, cos, sin, sinks, *, name):
    B, S, _ = z.shape
    nb = S // WINDOW
    cur, prev = _swa_specs(nb)

    def body(q_ref, kvp_ref, kvc_ref, cp_ref, sp_ref, cc_ref, sc_ref, sink_ref, o_ref, lse_ref):
        cos_c, sin_c = cc_ref[...], sc_ref[...]
        q = (_rope(q_ref[...].astype(F32), cos_c, sin_c) * ATT_SCALE).astype(BF16)
        k = jnp.concatenate([_rope(kvp_ref[:, :KV_W].astype(F32), cp_ref[...], sp_ref[...]),
                             _rope(kvc_ref[:, :KV_W].astype(F32), cos_c, sin_c)], axis=0)
        v = jnp.concatenate([kvp_ref[:, KV_W:], kvc_ref[:, KV_W:]], axis=0).astype(F32)
        low = lax.broadcasted_iota(jnp.int32, k.shape, 1) < ATT_HD
        kpad = _kv_padded(k, low)
        vpad = _kv_padded(v, low)
        mask = _swa_mask(pl.program_id(1) == 0)
        lses = []
        for g in range(ATT_KV_HEADS):
            pairs = range(g * PAIRS_PER_KV, (g + 1) * PAIRS_PER_KV)
            keys = [(p, e) for p in pairs for e in (0, 1)]
            qp = {p: q[:, p * 128:(p + 1) * 128] for p in pairs}
            s = {pe: jnp.where(mask, _nt(qp[pe[0]], kpad[g][pe[1]]), NEG_INF) for pe in keys}
            pr = {}
            for pe in keys:
                sink = sink_ref[0, 2 * pe[0] + pe[1]]
                m = jnp.maximum(jnp.max(s[pe], axis=1, keepdims=True), sink)
                ex = jnp.exp(s[pe] - m)
                den = jnp.sum(ex, axis=1, keepdims=True) + jnp.exp(sink - m)
                pr[pe] = (ex * (1.0 / den)).astype(BF16)
                lses.append(m + jnp.log(den))
            for p in pairs:
                o_ref[:, p * 128:(p + 1) * 128] = (_nn(pr[p, 0], vpad[g][0]) + _nn(pr[p, 1], vpad[g][1])).astype(BF16)
        lse_ref[...] = jnp.concatenate(lses, axis=1)

    tab = lambda im: pl.BlockSpec((None, WINDOW, 128), im)
    return pl.pallas_call(
        body, name=name, grid=(B, nb),
        in_specs=[*_swa_z_specs(),
                  tab(prev), tab(prev), tab(cur), tab(cur),
                  pl.BlockSpec(memory_space=pltpu.SMEM)],
        out_specs=[pl.BlockSpec((None, WINDOW, D_MODEL), cur), pl.BlockSpec((None, WINDOW, ATT_HEADS), cur)],
        out_shape=[jax.ShapeDtypeStruct((B, S, D_MODEL), BF16), jax.ShapeDtypeStruct((B, S, ATT_HEADS), F32)],
        compiler_params=_params("parallel", "parallel"),
    )(z, z, z, cos, sin, cos, sin, sinks)


def _swa_bwd(z, cos, sin, sinks, lse, dob, dz, *, name):
    B, S, _ = z.shape
    nb = S // WINDOW
    cur, prev = _swa_specs(nb)

    def body(q_ref, kvp_ref, kvc_ref, cp_ref, sp_ref, cc_ref, sc_ref, sink_ref, lse_ref, do_ref, dz_in,
             dq_ref, dkc_ref, dkp_ref, dsink_ref):
        @pl.when((pl.program_id(0) == 0) & (pl.program_id(1) == 0))
        def _():
            dsink_ref[...] = jnp.zeros_like(dsink_ref)

        cos_c, sin_c, cos_p, sin_p = cc_ref[...], sc_ref[...], cp_ref[...], sp_ref[...]
        q = (_rope(q_ref[...].astype(F32), cos_c, sin_c) * ATT_SCALE).astype(BF16)
        k = jnp.concatenate([_rope(kvp_ref[:, :KV_W].astype(F32), cos_p, sin_p),
                             _rope(kvc_ref[:, :KV_W].astype(F32), cos_c, sin_c)], axis=0)
        v = jnp.concatenate([kvp_ref[:, KV_W:], kvc_ref[:, KV_W:]], axis=0).astype(F32)
        low = lax.broadcasted_iota(jnp.int32, k.shape, 1) < ATT_HD
        kpad = _kv_padded(k, low)
        vpad = _kv_padded(v, low)
        mask = _swa_mask(pl.program_id(1) == 0)
        lse = lse_ref[...]
        dq_parts, dk_sum, dv_sum, dsinks = [], [], [], []
        for g in range(ATT_KV_HEADS):
            pairs = range(g * PAIRS_PER_KV, (g + 1) * PAIRS_PER_KV)
            keys = [(p, e) for p in pairs for e in (0, 1)]
            qp = {p: q[:, p * 128:(p + 1) * 128] for p in pairs}
            dop = {p: do_ref[:, p * 128:(p + 1) * 128] for p in pairs}
            s = {pe: jnp.where(mask, _nt(qp[pe[0]], kpad[g][pe[1]]), NEG_INF) for pe in keys}
            dp = {pe: _nt(dop[pe[0]], vpad[g][pe[1]]) for pe in keys}
            pr, ds = {}, {}
            for pe in keys:
                h = 2 * pe[0] + pe[1]
                lse_h = lse[:, h:h + 1]
                pf = jnp.exp(s[pe] - lse_h)
                delta = jnp.sum(pf * dp[pe], axis=1, keepdims=True)
                ds[pe] = (pf * (dp[pe] - delta)).astype(BF16)
                pr[pe] = pf.astype(BF16)
                p_sink = jnp.exp(sink_ref[0, h] - lse_h)
                dsinks.append(-jnp.sum(p_sink * delta, axis=0, keepdims=True))
            for p in pairs:
                dq_parts.append((_nn(ds[p, 0], kpad[g][0]) + _nn(ds[p, 1], kpad[g][1])) * ATT_SCALE)
            x = [sum(_tn(ds[p, e], qp[p]) for p in pairs) for e in (0, 1)]
            y = [sum(_tn(pr[p, e], dop[p]) for p in pairs) for e in (0, 1)]
            zk = jnp.where(low, x[0], x[1])
            zv = jnp.where(low, y[0], y[1])
            dk_sum.append(zk + _swap_halves(zk))
            dv_sum.append(zv + _swap_halves(zv))
        dq_ref[...] = _rope(jnp.concatenate(dq_parts, axis=1), cos_c, sin_c, inverse=True).astype(BF16)
        dk = jnp.where(low, dk_sum[0], dk_sum[1])
        dv = jnp.where(low, dv_sum[0], dv_sum[1])
        dkp_ref[:, :KV_W] = _rope(dk[:WINDOW], cos_p, sin_p, inverse=True)
        dkp_ref[:, KV_W:] = dv[:WINDOW]
        dkc_ref[:, :KV_W] = _rope(dk[WINDOW:], cos_c, sin_c, inverse=True)
        dkc_ref[:, KV_W:] = dv[WINDOW:]
        dsink_ref[...] += jnp.concatenate(dsinks, axis=1)

    tab = lambda im: pl.BlockSpec((None, WINDOW, 128), im)
    return pl.pallas_call(
        body, name=name, grid=(B, nb),
        in_specs=[*_swa_z_specs(),
                  tab(prev), tab(prev), tab(cur), tab(cur),
                  pl.BlockSpec(memory_space=pltpu.SMEM),
                  pl.BlockSpec((None, WINDOW, ATT_HEADS), cur),
                  pl.BlockSpec((None, WINDOW, D_MODEL), cur),
                  ANY],
        out_specs=[_swa_z_specs()[0],
                   pl.BlockSpec((None, WINDOW, 2 * KV_W), cur), pl.BlockSpec((None, WINDOW, 2 * KV_W), cur),
                   pl.BlockSpec((1, ATT_HEADS), lambda b, i: (0, 0))],
        out_shape=[jax.ShapeDtypeStruct(dz.shape, BF16),
                   jax.ShapeDtypeStruct((B, S, 2 * KV_W), F32), jax.ShapeDtypeStruct((B, S, 2 * KV_W), F32),
                   jax.ShapeDtypeStruct((1, ATT_HEADS), F32)],
        input_output_aliases={10: 0},
        compiler_params=_params("arbitrary", "arbitrary"),
    )(z, z, z, cos, sin, cos, sin, sinks, lse, dob, dz)


def _swa_dkv_combine(dkv_cur, dkv_prev, dz, *, name):
    B, S, W = dkv_cur.shape

    def body(c_ref, p_ref, dz_in, o_ref):
        rows = lax.broadcasted_iota(jnp.int32, (S, W), 0)
        o_ref[...] = (c_ref[...] + _shift_up(p_ref[...], WINDOW, rows, S)).astype(BF16)

    seq = pl.BlockSpec((None, S, W), lambda b: (b, 0, 0))
    return pl.pallas_call(
        body, name=name, grid=(B,),
        in_specs=[seq, seq, ANY], out_specs=pl.BlockSpec((None, S, W), lambda b: (b, 0, O_AKV // W_AKV)),
        out_shape=jax.ShapeDtypeStruct(dz.shape, BF16),
        input_output_aliases={2: 0},
        compiler_params=_params("parallel"),
    )(dkv_cur, dkv_prev, dz)


def _rope_tables(positions):
    half = ROPE_DIM // 2
    inv = ROPE_THETA ** (-2.0 * jnp.arange(half, dtype=F32) / ROPE_DIM)
    ang = positions.astype(F32)[..., None] * inv
    c, s = jnp.cos(ang), jnp.sin(ang)
    pad = jnp.zeros(ang.shape[:-1] + (ATT_HD - ROPE_DIM,), F32)
    cos = jnp.concatenate([c, c, pad + 1.0], axis=-1)
    sin = jnp.concatenate([-s, s, pad], axis=-1)
    return jnp.tile(cos, (1, 1, 2)), jnp.tile(sin, (1, 1, 2))


def _lower_bound(lb_logits, *, name):
    def body(l_ref, o_ref):
        l = l_ref[...]
        e = jnp.exp(l - jnp.max(l, axis=0, keepdims=True))
        o_ref[...] = e[0:1] / jnp.sum(e, axis=0, keepdims=True)

    return pl.pallas_call(body, name=name, out_shape=jax.ShapeDtypeStruct((1, lb_logits.shape[1]), F32))(lb_logits)


W_ZH, W_GATES, W_AQ, W_AKV = 4 * HF, 2 * D_MODEL, ATT_HEADS * ATT_HD, 2 * KV_W
O_ZH, O_GATES, O_AQ, O_AKV = 0, W_ZH, W_ZH + W_GATES, W_ZH + W_GATES + W_AQ
W_IN = W_ZH + W_GATES + W_AQ + W_AKV


W_IN_BLK = W_IN // N_DEV


def _w_in_pieces():
    ref_segments = [(0, W_ZH, O_ZH), (W_ZH, W_AQ, O_AQ), (W_ZH + W_AQ, W_AKV, O_AKV), (W_ZH + W_AQ + W_AKV, W_GATES, O_GATES)]
    out = []
    for p in range(N_DEV):
        lo, hi = p * W_IN_BLK, (p + 1) * W_IN_BLK
        for s0, w, d0 in ref_segments:
            a, b = max(lo, s0), min(hi, s0 + w)
            if a < b:
                out.append((p, a - lo, d0 + a - s0, b - a))
    return out


def _w_in_from_blocks(blocks, *, name):
    tr = 256

    def body(b_ref, o_ref):
        for p, s, d, w in _w_in_pieces():
            o_ref[:, d:d + w] = b_ref[p, :, s:s + w]

    return pl.pallas_call(
        body, name=name, grid=(D_MODEL // tr,),
        in_specs=[pl.BlockSpec((N_DEV, tr, W_IN_BLK), lambda i: (0, i, 0))], out_specs=pl.BlockSpec((tr, W_IN), lambda i: (i, 0)),
        out_shape=jax.ShapeDtypeStruct((D_MODEL, W_IN), blocks.dtype), compiler_params=_params("parallel"))(blocks)


def _blocks_from_w_in(w, *, name):
    tr = 256

    def body(w_ref, o_ref):
        for p, s, d, wd in _w_in_pieces():
            o_ref[p, :, s:s + wd] = w_ref[:, d:d + wd]

    return pl.pallas_call(
        body, name=name, grid=(D_MODEL // tr,),
        in_specs=[pl.BlockSpec((tr, W_IN), lambda i: (i, 0))], out_specs=pl.BlockSpec((N_DEV, tr, W_IN_BLK), lambda i: (0, i, 0)),
        out_shape=jax.ShapeDtypeStruct((N_DEV, D_MODEL, W_IN_BLK), w.dtype), compiler_params=_params("parallel"))(w)


def _local_step(x, positions, target, small, w_in, rest_weights, emit, start_token):
    B, S, D = x.shape
    T = B * S
    x2 = x.reshape(T, D)
    cos, sin = _rope_tables(positions)
    lb = _lower_bound(small["lb_logits"], name="lb_fwd")
    zero = lambda tok: tok[0:1, 0:1]

    u1 = _norm_cast(x2, small["norm1_g"] + zero(start_token), name="norm1")
    z = _matmul(u1, w_in, out_dtype=BF16, name="mm_z", tm=1024, tn=W_IN // 2)
    z3 = z.reshape(B, S, W_IN)
    oa, states = _hgrn_fwd(z3, lb, small["hgrn_norm_g"], name="hgrn_fwd")
    ob, lse = _swa_fwd(z3, cos, sin, small["attn_sinks"], name="swa_fwd")
    oa2 = oa.reshape(T, D)
    ob2 = ob.reshape(T, D)
    W = rest_weights("mix", ob)
    pa = _matmul(oa2, W["w_a"], out_dtype=BF16, name="mm_pa", tm=2048, tn=512)
    pb = _matmul(ob2, W["w_b"], out_dtype=BF16, name="mm_pb", tm=2048, tn=512)
    merged = _merge_fwd(z, pa, pb, name="merge_fwd")
    h = _matmul(merged, W["w_out"], addend=x2, name="mm_h", tm=2048, tn=512)
    u2 = _norm_cast(h, small["norm2_g"], name="norm2")
    W.update(rest_weights("ffn", u2))
    gu = _matmul(u2, W["w_ffn"], out_dtype=BF16, name="mm_gu", tm=2048, tn=512)
    gu3 = gu.reshape(B, S, 2 * D_FF)
    act, a_pre = _conv_act_fwd(gu3, W["conv_w"], small["conv_b"], name="conv_act_fwd")
    act2 = act.reshape(T, D_FF)
    h2 = _matmul(act2, W["w_down"], addend=h, name="mm_h2", tm=1024, tn=1024)

    g = {}
    dh2, dh2b, g["final_g"], loss = _final_loss_bwd(h2, small["final_g"].reshape(1, D), target.reshape(T, D), name="final_loss_bwd")
    dact = _matmul(dh2b, W["w_down"], tb=True, out_dtype=BF16, name="mm_dact", tm=1024, tn=D_FF)
    dw_down_t = _matmul(dh2b, act2, ta=True, out_dtype=BF16, name="mm_dw_down", tm=1024, tn=256, tk=8192)
    dg_, dup, g["conv_w"], g["conv_b"] = _conv_act_bwd(gu3, a_pre, W["conv_w"], dact.reshape(B, S, D_FF), name="conv_act_bwd")
    dg2 = dg_.reshape(T, D_FF)
    dup2 = dup.reshape(T, D_FF)
    du2 = _matmul(dg2, W["w_ffn"], tb=True, name="mm_du2_g", tm=1024, tn=1024, b_koff=0)
    du2 = _matmul(dup2, W["w_ffn"], tb=True, addend=du2, out_dtype=BF16, name="mm_du2_u", tm=1024, tn=1024, b_koff=1)
    dw_ffn = _matmul(u2, dg2, ta=True, out_dtype=BF16, into=lax.empty((D, 2 * D_FF), BF16), o_noff=0, name="mm_dw_ffn_g", tm=1024, tn=256, tk=8192)
    dw_ffn = _matmul(u2, dup2, ta=True, out_dtype=BF16, into=dw_ffn, o_noff=D_FF // 256, name="mm_dw_ffn_u", tm=1024, tn=256, tk=8192)
    tok = emit("ffn", dict(w_ffn=dw_ffn, w_down=dw_down_t.T))
    dh, dhb, g["norm2_g"] = _norm_bwd_add(h, small["norm2_g"] + zero(tok), du2, dh2, name="norm2_bwd")
    dmerged = _matmul(dhb, W["w_out"], tb=True, out_dtype=BF16, name="mm_dmerged", tm=2048, tn=512)
    dw_out = _matmul(merged, dhb, ta=True, out_dtype=BF16, name="mm_dw_out", tm=1024, tn=1024, tk=2048)
    dz, dpa, dpb = _merge_bwd(z, pa, pb, dmerged, lax.empty((T, W_IN), BF16), name="merge_bwd")
    doa =_matmul(dpa, W["w_a"], tb=True, out_dtype=BF16, name="mm_doa", tm=2048, tn=512)
    dw_a = _matmul(oa2, dpa, ta=True, out_dtype=BF16, name="mm_dw_a", tm=1024, tn=1024, tk=2048)
    dob = _matmul(dpb, W["w_b"], tb=True, out_dtype=BF16, name="mm_dob", tm=2048, tn=512)
    dw_b = _matmul(ob2, dpb, ta=True, out_dtype=BF16, name="mm_dw_b", tm=1024, tn=1024, tk=2048)
    tok = emit("mix", dict(w_out=dw_out, w_a=dw_a, w_b=dw_b))
    dz3, dkv_cur, dkv_prev, dsinks = _swa_bwd(z3, cos, sin, small["attn_sinks"] + zero(tok), lse, dob.reshape(B, S, D),
                                              dz.reshape(B, S, W_IN), name="swa_bwd")
    dz3 = _swa_dkv_combine(dkv_cur, dkv_prev, dz3, name="swa_dkv")
    g["attn_sinks"] = dsinks
    dz3, g["lb"], g["hgrn_norm_g"] = _hgrn_bwd(z3, lb, small["hgrn_norm_g"], states, doa.reshape(B, S, D), dz3, name="hgrn_bwd")
    dz = dz3.reshape(T, W_IN)
    dw_in = _matmul(u1, dz, ta=True, out_dtype=BF16, name="mm_dw_in", tm=1024, tn=256, tk=8192)
    tok = emit("in", dict(w_in=dw_in))
    du1 = _matmul(dz, w_in, tb=True, after=tok, out_dtype=BF16, name="mm_du1", tm=1024, tn=512)
    dx, g["norm1_g"] = _norm_bwd_add(x2, small["norm1_g"], du1, dh, with_bf16=False, name="norm1_bwd")
    g["lb_logits"] = _lb_bwd(g.pop("lb"), lb, name="lb_bwd")
    return loss, dx.reshape(B, S, D), g


def _my_place():
    return lax.axis_index("x"), lax.axis_index("y"), lax.axis_index("c")


def _gather_blocks(x_ref, out_ref, send_sems, recv_sems, local_sem):
    x, y, c = _my_place()
    me, sibling = (x, y, c), (x, y, 1 - c)
    chips = [(1 - x, y), (x, 1 - y), (1 - x, 1 - y)]

    def slot(px, py, pc):
        return out_ref.at[4 * px + 2 * py + pc]

    def copy(k, block, to, src=None):
        return pltpu.make_async_remote_copy(
            src_ref=slot(*block) if src is None else src, dst_ref=slot(*block),
            send_sem=send_sems.at[k], recv_sem=recv_sems.at[k], device_id=to, device_id_type=MESH)

    mine = pltpu.make_async_copy(x_ref, slot(*me), local_sem)
    mine.start()
    first = [copy(0, me, sibling, src=x_ref)]
    first += [copy(1 + j, me, (*chip, c), src=x_ref) for j, chip in enumerate(chips)]
    for cp in first:
        cp.start()
    passed = [copy(4 + j, (*chip, c), sibling) for j, chip in enumerate(chips)]
    for j, chip in enumerate(chips):
        copy(1 + j, (*chip, c), me).wait_recv()
        passed[j].start()
    copy(0, sibling, me).wait_recv()
    for j, chip in enumerate(chips):
        copy(4 + j, (*chip, 1 - c), me).wait_recv()
    for cp in first + passed:
        cp.wait_send()
    mine.wait()


GATHER_SEMS = [pltpu.SemaphoreType.DMA((7,)), pltpu.SemaphoreType.DMA((7,)), pltpu.SemaphoreType.DMA]


def _all_gather(blk, *, name):
    return pl.pallas_call(
        _gather_body_fn(), name=name,
        out_shape=jax.ShapeDtypeStruct((N_DEV,) + blk.shape, blk.dtype),
        in_specs=[ANY], out_specs=ANY,
        scratch_shapes=GATHER_SEMS,
    )(blk)


def _gather_body_fn():
    def body(x_ref, out_ref, send_sems, recv_sems, local_sem):
        _gather_blocks(x_ref, out_ref, send_sems, recv_sems, local_sem)
    return body


SLAB_W = 1152
SMALL_SHAPES = dict(norm1_g=(1, D_MODEL), lb_logits=(2, HGRN_HEADS * HGRN_DK), hgrn_norm_g=(1, HGRN_DK), attn_sinks=(1, ATT_HEADS),
                    norm2_g=(1, D_MODEL), conv_b=(1, D_FF), final_g=(1, D_MODEL))
CONVW_BLK = D_FF // N_DEV
CONVW_STRIDE = SLAB_W // 3


def _slab_layout():
    layout, r = {}, 0
    for nm, (nr, w) in SMALL_SHAPES.items():
        layout[nm] = []
        for i in range(nr):
            for c0 in range(0, w, SLAB_W):
                layout[nm].append((r, i, c0, min(SLAB_W, w - c0)))
                r += 1
    return layout, r


SMALL_ROWS, _N_SMALL_ROWS = _slab_layout()
CONV_ROW0 = -(-_N_SMALL_ROWS // 8) * 8
LOSS_ROW = CONV_ROW0 + N_DEV
SLAB_ROWS = LOSS_ROW + 8


def _small_step(grads, g_conv_w, loss, params, moments, variances, dev, *, name):
    names = list(SMALL_ROWS)
    n = len(names)

    def body(dev_ref, *refs):
        g_refs = dict(zip(names, refs[:n]))
        gc_ref, loss_ref = refs[n], refs[n + 1]
        base = n + 2
        w_refs, m_refs, v_refs = (dict(zip(names + ["conv_w"], refs[base + i * (n + 1):base + (i + 1) * (n + 1)])) for i in range(3))
        o = base + 3 * (n + 1)
        gath_ref, loss_out = refs[o], refs[o + 1]
        outs = {nm: refs[o + 2 + 4 * i:o + 6 + 4 * i] for i, nm in enumerate(names + ["conv_w"])}
        slab, total, send_sems, recv_sems, local_sem = refs[-5:]

        slab[...] = jnp.zeros_like(slab)
        for nm, pieces in SMALL_ROWS.items():
            for r, i, c0, w in pieces:
                slab[r:r + 1, 0:w] = g_refs[nm][i:i + 1, c0:c0 + w]
        for p in range(N_DEV):
            for j in range(3):
                slab[CONV_ROW0 + p:CONV_ROW0 + p + 1, j * CONVW_STRIDE:j * CONVW_STRIDE + CONVW_BLK] = gc_ref[j:j + 1, p * CONVW_BLK:(p + 1) * CONVW_BLK]
        slab[LOSS_ROW:LOSS_ROW + 1, 0:1] = loss_ref[...]
        _gather_blocks(slab, gath_ref, send_sems, recv_sems, local_sem)
        acc = gath_ref[0]
        for p in range(1, N_DEV):
            acc = acc + gath_ref[p]
        total[...] = acc
        loss_out[...] = total[LOSS_ROW:LOSS_ROW + 1, 0:1]

        def update(nm, g, i, c0, w):
            at = (slice(i, i + 1), slice(c0, c0 + w))
            d, mn, vn = _adamw_math(w_refs[nm][at], g, m_refs[nm][at], v_refs[nm][at])
            for ref, val in zip(outs[nm], (g, d, mn, vn)):
                ref[at] = val

        for nm, pieces in SMALL_ROWS.items():
            for r, i, c0, w in pieces:
                update(nm, total[r:r + 1, 0:w], i, c0, w)
        conv_rows = total[CONV_ROW0:CONV_ROW0 + N_DEV, :]
        rowid = lax.broadcasted_iota(jnp.int32, conv_rows.shape, 0)
        mine = jnp.sum(jnp.where(rowid == dev_ref[0], conv_rows, 0.0), axis=0, keepdims=True)
        for j in range(3):
            update("conv_w", mine[:, j * CONVW_STRIDE:j * CONVW_STRIDE + CONVW_BLK], j, 0, CONVW_BLK)

    order = names + ["conv_w"]
    ins = [grads[nm] for nm in names] + [g_conv_w, loss]
    for d in (params, moments, variances):
        ins += [d[nm] for nm in order]
    vmem = pl.BlockSpec(memory_space=pltpu.VMEM)
    out_shape = [jax.ShapeDtypeStruct((N_DEV, SLAB_ROWS, SLAB_W), F32), jax.ShapeDtypeStruct((1, 1), F32)]
    for nm in order:
        out_shape += [jax.ShapeDtypeStruct(params[nm].shape, F32)] * 4
    res = pl.pallas_call(
        body, name=name,
        grid_spec=pltpu.PrefetchScalarGridSpec(
            num_scalar_prefetch=1, grid=(1,),
            in_specs=[vmem] * len(ins), out_specs=[vmem] * len(out_shape),
            scratch_shapes=[pltpu.VMEM((SLAB_ROWS, SLAB_W), F32), pltpu.VMEM((SLAB_ROWS, SLAB_W), F32)] + GATHER_SEMS),
        out_shape=out_shape,
    )(dev, *ins)
    return res[1], {nm: tuple(res[2 + 4 * i:6 + 4 * i]) for i, nm in enumerate(order)}


HBM_SPEC = pl.BlockSpec(memory_space=pltpu.HBM)
SEM_SPEC = pl.BlockSpec(memory_space=pltpu.SEMAPHORE)
DATAFLOW_EFFECT = pltpu.SideEffectType.DATAFLOW_SIDE_EFFECTING
N_PEERS = N_DEV - 1


def _peers(x, y, c):
    return [(1 - x if r & 4 else x, 1 - y if r & 2 else y, 1 - c if r & 1 else c) for r in range(1, N_DEV)]


def _exchange_start(srcs, scatter, *, after=None, name):
    n = len(srcs)
    lands = [lax.empty(a.shape if scatter else (N_DEV,) + a.shape, a.dtype) for a in srcs]
    extra = [] if after is None else [after]

    def body(*refs):
        src_refs, land_refs = refs[:n], refs[n:2 * n]
        send_sems, recv_sems, token = refs[2 * n + len(extra)], refs[2 * n + len(extra) + 1], refs[-1]
        x, y, c = _my_place()
        me = 4 * x + 2 * y + c
        for i in range(n):
            for r, (tx, ty, tc) in enumerate(_peers(x, y, c)):
                src = src_refs[i].at[4 * tx + 2 * ty + tc] if scatter else src_refs[i]
                pltpu.make_async_remote_copy(
                    src_ref=src, dst_ref=land_refs[i].at[me], send_sem=send_sems.at[N_PEERS * i + r],
                    recv_sem=recv_sems.at[N_PEERS * i + r], device_id=(tx, ty, tc), device_id_type=MESH).start()
        token[...] = jnp.zeros_like(token)

    thru = [pltpu.HBM(a.shape, a.dtype) for a in list(srcs) + lands]
    res = pl.pallas_call(
        body, name=name,
        out_shape=(pltpu.SemaphoreType.DMA((N_PEERS * n,)), pltpu.SemaphoreType.DMA((N_PEERS * n,)), *thru,
                   jax.ShapeDtypeStruct((8, 128), F32)),
        in_specs=[HBM_SPEC] * (2 * n) + [ANY] * len(extra),
        out_specs=(SEM_SPEC, SEM_SPEC, *([HBM_SPEC] * (2 * n)), pl.BlockSpec(memory_space=pltpu.VMEM)),
        input_output_aliases={i: 2 + i for i in range(2 * n)},
        compiler_params=pltpu.CompilerParams(has_side_effects=DATAFLOW_EFFECT),
    )(*[pltpu.with_memory_space_constraint(a, pltpu.HBM) for a in list(srcs) + lands], *extra)
    return (res[0], res[1], list(res[2:2 + n]), list(res[2 + n:2 + 2 * n]), scatter), res[-1]


def _exchange_wait(handle, after, *, name):
    send_sems, recv_sems, srcs, lands, scatter = handle
    n = len(srcs)

    def body(*refs):
        src_refs, land_refs = refs[:n], refs[n:2 * n]
        send_sems, recv_sems = refs[2 * n], refs[2 * n + 1]
        x, y, c = _my_place()
        for i in range(n):
            for r in range(N_PEERS):
                src = src_refs[i].at[0] if scatter else src_refs[i]
                cp = pltpu.make_async_remote_copy(
                    src_ref=src, dst_ref=land_refs[i].at[0], send_sem=send_sems.at[N_PEERS * i + r],
                    recv_sem=recv_sems.at[N_PEERS * i + r], device_id=(x, y, c), device_id_type=MESH)
                cp.wait_send()
                cp.wait_recv()

    thru = [pltpu.HBM(a.shape, a.dtype) for a in srcs + lands]
    res = pl.pallas_call(
        body, name=name, out_shape=tuple(thru),
        in_specs=[HBM_SPEC] * (2 * n) + [SEM_SPEC, SEM_SPEC, ANY], out_specs=tuple([HBM_SPEC] * (2 * n)),
        input_output_aliases={i: i for i in range(2 * n)},
        compiler_params=pltpu.CompilerParams(has_side_effects=DATAFLOW_EFFECT),
    )(*srcs, *lands, send_sems, recv_sems, after)
    return list(res[:n]), list(res[n:])


def _with_own(land, own, me):
    return lax.dynamic_update_index_in_dim(land, own, me, 0)


def _adamw_math(w, g, m, v):
    m = ADAM_B1 * m + (1.0 - ADAM_B1) * g
    v = ADAM_B2 * v + (1.0 - ADAM_B2) * (g * g)
    m_hat = m / (1.0 - ADAM_B1 ** ADAM_STEP)
    v_hat = v / (1.0 - ADAM_B2 ** ADAM_STEP)
    delta = -ADAM_LR * (m_hat / (jnp.sqrt(v_hat) + ADAM_EPS) + ADAM_WD * w)
    return delta, m, v


def _adamw_sum(parts, w, m, v, *, name):
    shape = w.shape
    R, n = shape[-2], shape[-1]
    w, m, v = (t.reshape(R, n) for t in (w, m, v))
    tr = _pick(R, (256, 176, 128))

    def body(p_ref, w_ref, m_ref, v_ref, g_ref, d_ref, mo_ref, vo_ref):
        g = p_ref[0].astype(F32)
        for p in range(1, N_DEV):
            g = g + p_ref[p].astype(F32)
        d, mn, vn = _adamw_math(w_ref[...], g, m_ref[...], v_ref[...])
        g_ref[...] = g
        d_ref[...] = d
        mo_ref[...] = mn
        vo_ref[...] = vn

    row = pl.BlockSpec((tr, n), lambda i: (i, 0))
    outs = pl.pallas_call(
        body, name=name, grid=(R // tr,),
        in_specs=[pl.BlockSpec((N_DEV, tr, n), lambda i: (0, i, 0)), row, row, row],
        out_specs=[row, row, row, row],
        out_shape=[jax.ShapeDtypeStruct((R, n), F32)] * 4,
        compiler_params=_params("parallel"),
    )(parts, w, m, v)
    return [t.reshape(shape) for t in outs]


def _lb_bwd(dlb, lb, *, name):
    def body(d_ref, lb_ref, o_ref):
        t = d_ref[...] * lb_ref[...] * (1.0 - lb_ref[...])
        o_ref[0:1, :] = t
        o_ref[1:2, :] = -t

    return pl.pallas_call(body, name=name, out_shape=jax.ShapeDtypeStruct((2, lb.shape[1]), F32))(dlb, lb)


DOWN_BLK, ROW_BLK = D_FF // N_DEV, D_MODEL // N_DEV
CONV_BITS_SHAPE = (16, 256)


def _cols_from_blocks(blocks):
    n, rows, width = blocks.shape
    return blocks.transpose(1, 0, 2).reshape(rows, n * width)


def _blocks_from_cols(full):
    rows, cols = full.shape
    return full.reshape(rows, N_DEV, cols // N_DEV).transpose(1, 0, 2)


def kernel(x, positions, norm1_g, w_in, lb_logits, hgrn_norm_g, w_a, attn_sinks, w_b, w_out, norm2_g, w_ffn_in, conv_w, conv_b, w_down, final_g, loss_target, m_norm1_g, m_w_in, m_lb_logits, m_hgrn_norm_g, m_w_a, m_attn_sinks, m_w_b, m_w_out, m_norm2_g, m_w_ffn_in, m_conv_w, m_conv_b, m_w_down, m_final_g, v_norm1_g, v_w_in, v_lb_logits, v_hgrn_norm_g, v_w_a, v_attn_sinks, v_w_b, v_w_out, v_norm2_g, v_w_ffn_in, v_conv_w, v_conv_b, v_w_down, v_final_g):
    xi, yi, ci = _my_place()
    dev = 4 * xi + 2 * yi + ci

    w_in_blocks = _all_gather(w_in[0].astype(BF16), name="ag_w_in")
    conv_bits = lax.bitcast_convert_type(conv_w, BF16).reshape(-1)
    conv_bits = jnp.pad(conv_bits, (0, CONV_BITS_SHAPE[0] * CONV_BITS_SHAPE[1] - conv_bits.shape[0])).reshape(CONV_BITS_SHAPE)
    gather_handles = {}
    gather_handles["mix"], tok_mix = _exchange_start([w_a[0].astype(BF16), w_b[0].astype(BF16), w_out[0].astype(BF16)], False,
                                                     after=w_in_blocks, name="ag_mix_start")
    gather_handles["ffn"], tok_ffn = _exchange_start([w_ffn_in[0].astype(BF16), w_down[0].astype(BF16), conv_bits], False,
                                                     after=tok_mix, name="ag_ffn_start")
    start_token = tok_mix + tok_ffn

    def rest_weights(group, after):
        own, lands = _exchange_wait(gather_handles[group], after, name="ag_" + group + "_wait")
        full = [_with_own(l, o, dev) for l, o in zip(lands, own)]
        if group == "mix":
            return dict(zip(("w_a", "w_b", "w_out"), [t.reshape(D_MODEL, D_MODEL) for t in full]))
        bits = full[2].reshape(N_DEV, -1)[:, :3 * CONVW_BLK * 2].reshape(N_DEV, 3, CONVW_BLK, 2)
        return dict(w_ffn=_cols_from_blocks(full[0]), w_down=full[1].reshape(D_FF, D_MODEL),
                    conv_w=_cols_from_blocks(lax.bitcast_convert_type(bits, F32)))

    handles = {}

    def emit(group, gr):
        if group == "ffn":
            srcs = [_blocks_from_cols(gr["w_ffn"]), gr["w_down"].reshape(N_DEV, DOWN_BLK, D_MODEL)]
        elif group == "mix":
            srcs = [gr[n].reshape(N_DEV, ROW_BLK, D_MODEL) for n in ("w_out", "w_a", "w_b")]
        else:
            srcs = [_blocks_from_w_in(gr["w_in"], name="dw_in_blocks")]
        handles[group], token = _exchange_start(srcs, True, name="rs_" + group + "_start")
        return token

    small = dict(norm1_g=norm1_g, lb_logits=lb_logits, hgrn_norm_g=hgrn_norm_g, attn_sinks=attn_sinks, norm2_g=norm2_g,
                 conv_b=conv_b, final_g=final_g)
    w_in_full = _w_in_from_blocks(w_in_blocks, name="w_in_layout")
    loss, grad_x, g = _local_step(x, positions, loss_target, small, w_in_full, rest_weights, emit, start_token)

    def parts_of(group, after):
        srcs, lands = _exchange_wait(handles[group], after, name="rs_" + group + "_wait")
        return [_with_own(l, lax.dynamic_index_in_dim(s, dev, 0, keepdims=False), dev) for s, l in zip(srcs, lands)]

    p_ffn, p_down = parts_of("ffn", grad_x)
    p_out, p_a, p_b = parts_of("mix", grad_x)
    (p_in,) = parts_of("in", grad_x)
    big = dict(
        w_in=_adamw_sum(p_in, w_in, m_w_in, v_w_in, name="adamw_w_in"),
        w_a=_adamw_sum(p_a, w_a, m_w_a, v_w_a, name="adamw_w_a"),
        w_b=_adamw_sum(p_b, w_b, m_w_b, v_w_b, name="adamw_w_b"),
        w_out=_adamw_sum(p_out, w_out, m_w_out, v_w_out, name="adamw_w_out"),
        w_ffn_in=_adamw_sum(p_ffn, w_ffn_in, m_w_ffn_in, v_w_ffn_in, name="adamw_w_ffn_in"),
        w_down=_adamw_sum(p_down, w_down, m_w_down, v_w_down, name="adamw_w_down"),
    )

    row = lambda t: t.reshape(1, -1) if t.ndim == 1 else t
    shard = lambda t: t.reshape(3, CONVW_BLK)
    sm_g = {nm: g[nm] for nm in SMALL_ROWS}
    sm_w = dict(norm1_g=norm1_g, lb_logits=lb_logits, hgrn_norm_g=hgrn_norm_g, attn_sinks=attn_sinks, norm2_g=norm2_g,
                conv_b=conv_b, final_g=row(final_g), conv_w=shard(conv_w))
    sm_m = dict(norm1_g=m_norm1_g, lb_logits=m_lb_logits, hgrn_norm_g=m_hgrn_norm_g, attn_sinks=m_attn_sinks, norm2_g=m_norm2_g,
                conv_b=m_conv_b, final_g=row(m_final_g), conv_w=shard(m_conv_w))
    sm_v = dict(norm1_g=v_norm1_g, lb_logits=v_lb_logits, hgrn_norm_g=v_hgrn_norm_g, attn_sinks=v_attn_sinks, norm2_g=v_norm2_g,
                conv_b=v_conv_b, final_g=row(v_final_g), conv_w=shard(v_conv_w))
    loss_total, sm_out = _small_step(sm_g, g["conv_w"], loss, sm_w, sm_m, sm_v, dev.astype(jnp.int32).reshape(1), name="small_step")
    shapes = dict(final_g=final_g.shape, conv_w=conv_w.shape)

    names = ("norm1_g", "w_in", "lb_logits", "hgrn_norm_g", "w_a", "attn_sinks", "w_b", "w_out", "norm2_g", "w_ffn_in", "conv_w", "conv_b", "w_down", "final_g")
    outs = [loss_total.reshape(()), grad_x]
    for kind in range(4):
        outs += [big[n][kind] if n in big else sm_out[n][kind].reshape(shapes.get(n, sm_out[n][kind].shape)) for n in names]
    return tuple(outs)
```

```python
import functools

import jax
import jax.numpy as jnp
from jax import lax
from jax.experimental import pallas as pl
from jax.experimental.pallas import tpu as pltpu

F32 = jnp.float32
BF16 = jnp.bfloat16

D_MODEL = 1024
HGRN_HEADS = 8
HGRN_DK = 128
CHUNK = 64
ATT_HEADS = 16
ATT_KV_HEADS = 2
ATT_HD = 64
ATT_GROUP = ATT_HEADS // ATT_KV_HEADS
WINDOW = 128
ROPE_DIM = ATT_HD // 4
ROPE_THETA = 500000.0
D_FF = 2816
EPS = 1e-6
NEG_INF = -1e30
N_DEV = 8

ADAM_LR = 0.001
ADAM_B1 = 0.9
ADAM_B2 = 0.999
ADAM_EPS = 1e-08
ADAM_WD = 0.01
ADAM_STEP = 10

MESH = pl.DeviceIdType.MESH
ANY = pl.BlockSpec(memory_space=pl.ANY)


def _pick(n, cands):
    for c in cands:
        if n % c == 0:
            return c
    return n


def _sigmoid(x):
    return 0.5 * jnp.tanh(0.5 * x) + 0.5


def _silu(x):
    hx = 0.5 * x
    return hx * jnp.tanh(hx) + hx


def _rms(x, g):
    return x * lax.rsqrt(jnp.mean(x * x, axis=-1, keepdims=True) + EPS) * g


def _dot(a, b, dims):
    return lax.dot_general(a, b, (dims, ((), ())), preferred_element_type=F32)


def _nn(a, b):
    return _dot(a, b, ((1,), (0,)))


def _nt(a, b):
    return _dot(a, b, ((1,), (1,)))


def _tn(a, b):
    return _dot(a, b, ((0,), (0,)))


def _params(*sem):
    return pltpu.CompilerParams(dimension_semantics=sem, vmem_limit_bytes=56 * 1024 * 1024)


def _matmul(a, b, *, ta=False, tb=False, out_dtype=F32, addend=None, after=None, into=None, o_noff=0, name, tm, tn, tk=None,
            n_extent=None, b_koff=0, b_noff=0):
    M, K = (a.shape[1], a.shape[0]) if ta else a.shape
    N = n_extent or (b.shape[0] if tb else b.shape[1])
    tm, tn, tk = min(tm, M), min(tn, N), min(tk or K, K)
    assert M % tm == 0 and N % tn == 0 and K % tk == 0, (name, M, N, K, tm, tn, tk)
    nk = K // tk
    use_scratch = nk > 1 and out_dtype != F32
    grid = (M // tm, N // tn, nk)
    a_spec = pl.BlockSpec((tk, tm), lambda i, j, k: (k, i)) if ta else pl.BlockSpec((tm, tk), lambda i, j, k: (i, k))
    b_spec = pl.BlockSpec((tn, tk), lambda i, j, k: (j + b_noff, k + b_koff)) if tb else pl.BlockSpec((tk, tn), lambda i, j, k: (k + b_koff, j + b_noff))
    o_spec = pl.BlockSpec((tm, tn), lambda i, j, k: (i, j))
    dims = ((0 if ta else 1,), (1 if tb else 0,))
    has_add = addend is not None

    n_in = 2 + has_add + (after is not None) + (into is not None)

    def body(*refs):
        a_ref, b_ref = refs[:2]
        c_ref = refs[2] if has_add else None
        o_ref = refs[n_in]
        part = _dot(a_ref[...], b_ref[...], dims)
        if nk == 1:
            if has_add:
                part = part + c_ref[...].astype(F32)
            o_ref[...] = part.astype(out_dtype)
        else:
            acc_ref = refs[-1] if use_scratch else o_ref
            k = pl.program_id(2)

            @pl.when(k == 0)
            def _():
                acc_ref[...] = part + c_ref[...].astype(F32) if has_add else part

            @pl.when(k > 0)
            def _():
                acc_ref[...] += part

            if use_scratch:
                @pl.when(k == nk - 1)
                def _():
                    o_ref[...] = acc_ref[...].astype(out_dtype)

    in_specs = [a_spec, b_spec] + ([o_spec] if has_add else [])
    args = (a, b) + ((addend,) if has_add else ())
    if after is not None:
        in_specs.append(pl.BlockSpec(after.shape, lambda i, j, k: (0, 0)))
        args += (after,)
    aliases = {}
    if into is not None:
        in_specs.append(ANY)
        args += (into,)
        aliases = {len(args) - 1: 0}
        o_spec = pl.BlockSpec((tm, tn), lambda i, j, k: (i, j + o_noff))
    return pl.pallas_call(
        body,
        name=name,
        grid=grid,
        in_specs=in_specs,
        out_specs=o_spec,
        out_shape=jax.ShapeDtypeStruct((M, N) if into is None else into.shape, out_dtype),
        input_output_aliases=aliases,
        scratch_shapes=[pltpu.VMEM((tm, tn), F32)] if use_scratch else [],
        compiler_params=_params("parallel", "parallel", "arbitrary"),
    )(*args)


def _row_spec(tm, n):
    return pl.BlockSpec((tm, n), lambda i: (i, 0))


def _full_spec(shape):
    return pl.BlockSpec(shape, lambda i: tuple(0 for _ in shape))


def _norm_cast(x, g, *, name):
    T, D = x.shape
    tm = _pick(T, (512, 256, 128))

    def body(x_ref, g_ref, u_ref):
        u_ref[...] = _rms(x_ref[...], g_ref[...]).astype(BF16)

    return pl.pallas_call(
        body, name=name, grid=(T // tm,),
        in_specs=[_row_spec(tm, D), _full_spec((1, D))],
        out_specs=_row_spec(tm, D),
        out_shape=jax.ShapeDtypeStruct((T, D), BF16),
        compiler_params=_params("parallel"),
    )(x, g)


def _norm_bwd_add(x, g, du, dres, *, with_bf16=True, name):
    T, D = x.shape
    tm = _pick(T, (512, 256, 128))

    def body(x_ref, g_ref, du_ref, dr_ref, dx_ref, *rest):
        dg_ref = rest[-1]
        _, vjp = jax.vjp(_rms, x_ref[...], g_ref[...])
        dx, dg = vjp(du_ref[...].astype(F32))
        dx = dx + dr_ref[...]
        dx_ref[...] = dx
        if with_bf16:
            rest[0][...] = dx.astype(BF16)

        @pl.when(pl.program_id(0) == 0)
        def _():
            dg_ref[...] = jnp.zeros_like(dg_ref)

        dg_ref[...] += dg

    row = _row_spec(tm, D)
    return pl.pallas_call(
        body, name=name, grid=(T // tm,),
        in_specs=[row, _full_spec((1, D)), row, row],
        out_specs=[row] + ([row] if with_bf16 else []) + [_full_spec((1, D))],
        out_shape=[jax.ShapeDtypeStruct((T, D), F32)] + ([jax.ShapeDtypeStruct((T, D), BF16)] if with_bf16 else []) + [jax.ShapeDtypeStruct((1, D), F32)],
        compiler_params=_params("arbitrary"),
    )(x, g, du, dres)


def _final_loss_bwd(h2, g, target, *, name):
    T, D = h2.shape
    tm = _pick(T, (512, 256, 128))

    def body(h_ref, g_ref, t_ref, dx_ref, dxb_ref, dg_ref, loss_ref):
        y, vjp = jax.vjp(_rms, h_ref[...], g_ref[...])
        err = y - t_ref[...]
        dx, dg = vjp(err * (1.0 / D))
        dx_ref[...] = dx
        dxb_ref[...] = dx.astype(BF16)

        @pl.when(pl.program_id(0) == 0)
        def _():
            dg_ref[...] = jnp.zeros_like(dg_ref)
            loss_ref[...] = jnp.zeros_like(loss_ref)

        dg_ref[...] += dg
        loss_ref[...] += (0.5 / D) * jnp.sum(jnp.sum(err * err, axis=1, keepdims=True), axis=0, keepdims=True)

    return pl.pallas_call(
        body, name=name, grid=(T // tm,),
        in_specs=[_row_spec(tm, D), _full_spec((1, D)), _row_spec(tm, D)],
        out_specs=[_row_spec(tm, D), _row_spec(tm, D), _full_spec((1, D)), _full_spec((1, 1))],
        out_shape=[jax.ShapeDtypeStruct((T, D), F32), jax.ShapeDtypeStruct((T, D), BF16), jax.ShapeDtypeStruct((1, D), F32), jax.ShapeDtypeStruct((1, 1), F32)],
        compiler_params=_params("arbitrary"),
    )(h2, g, target)


def _merge_fn(gates, a, b):
    ga = gates[:, :D_MODEL].astype(F32)
    gb = gates[:, D_MODEL:].astype(F32)
    return _sigmoid(ga) * a.astype(F32) + _sigmoid(gb) * b.astype(F32)


def _gates_spec(tm):
    return pl.BlockSpec((tm, W_GATES), lambda i: (i, O_GATES // W_GATES))


def _merge_fwd(z, a, b, *, name):
    T = a.shape[0]
    tm = _pick(T, (512, 256, 128))

    def body(g_ref, a_ref, b_ref, o_ref):
        o_ref[...] = _merge_fn(g_ref[...], a_ref[...], b_ref[...]).astype(BF16)

    return pl.pallas_call(
        body, name=name, grid=(T // tm,),
        in_specs=[_gates_spec(tm), _row_spec(tm, D_MODEL), _row_spec(tm, D_MODEL)],
        out_specs=_row_spec(tm, D_MODEL),
        out_shape=jax.ShapeDtypeStruct((T, D_MODEL), BF16),
        compiler_params=_params("parallel"),
    )(z, a, b)


def _merge_bwd(z, a, b, dmerged, dz, *, name):
    T = a.shape[0]
    tm = _pick(T, (512, 256, 128))

    def body(g_ref, a_ref, b_ref, dm_ref, dz_in, dg_ref, da_ref, db_ref):
        g = g_ref[...].astype(F32)
        dm = dm_ref[...].astype(F32)
        sa = _sigmoid(g[:, :D_MODEL])
        sb = _sigmoid(g[:, D_MODEL:])
        da_ref[...] = (dm * sa).astype(BF16)
        db_ref[...] = (dm * sb).astype(BF16)
        dg_ref[:, :D_MODEL] = (dm * a_ref[...].astype(F32) * sa * (1.0 - sa)).astype(BF16)
        dg_ref[:, D_MODEL:] = (dm * b_ref[...].astype(F32) * sb * (1.0 - sb)).astype(BF16)

    return pl.pallas_call(
        body, name=name, grid=(T // tm,),
        in_specs=[_gates_spec(tm), _row_spec(tm, D_MODEL), _row_spec(tm, D_MODEL), _row_spec(tm, D_MODEL), ANY],
        out_specs=[_gates_spec(tm), _row_spec(tm, D_MODEL), _row_spec(tm, D_MODEL)],
        out_shape=[jax.ShapeDtypeStruct(dz.shape, BF16), jax.ShapeDtypeStruct((T, D_MODEL), BF16), jax.ShapeDtypeStruct((T, D_MODEL), BF16)],
        input_output_aliases={4: 0},
        compiler_params=_params("parallel"),
    )(z, a, b, dmerged, dz)


CONV_TC = 256


def _shift_down(x, n, rows):
    return jnp.where(rows >= n, pltpu.roll(x, n, 0), 0.0)


def _shift_up(x, n, rows, S):
    return jnp.where(rows < S - n, pltpu.roll(x, S - n, 0), 0.0)


def _conv_act_fwd(gu, conv_w, conv_b, *, name):
    B, S, _ = gu.shape
    tc = CONV_TC
    nc = D_FF // tc

    def body(g_ref, up_ref, w_ref, b_ref, o_ref, a_ref):
        g = g_ref[...].astype(F32)
        rows = lax.broadcasted_iota(jnp.int32, g.shape, 0)
        w = w_ref[...]
        a = w[2:3] * g + w[1:2] * _shift_down(g, 1, rows) + w[0:1] * _shift_down(g, 2, rows) + b_ref[...]
        o_ref[...] = (_silu(a) * up_ref[...].astype(F32)).astype(BF16)
        a_ref[...] = a.astype(BF16)

    col = pl.BlockSpec((None, S, tc), lambda b, j: (b, 0, j))
    return pl.pallas_call(
        body, name=name, grid=(B, nc),
        in_specs=[col,
                  pl.BlockSpec((None, S, tc), lambda b, j: (b, 0, j + nc)),
                  pl.BlockSpec((3, tc), lambda b, j: (0, j)),
                  pl.BlockSpec((1, tc), lambda b, j: (0, j))],
        out_specs=[col, col],
        out_shape=[jax.ShapeDtypeStruct((B, S, D_FF), BF16)] * 2,
        compiler_params=_params("parallel", "parallel"),
    )(gu, gu, conv_w, conv_b)


def _conv_act_bwd(gu, a_pre, conv_w, dact, *, name):
    B, S, _ = gu.shape
    tc = CONV_TC
    nc = D_FF // tc

    def body(g_ref, up_ref, a_ref, w_ref, da_ref, dg_ref, dup_ref, dw_ref, db_ref):
        g = g_ref[...].astype(F32)
        up = up_ref[...].astype(F32)
        a = a_ref[...].astype(F32)
        dact = da_ref[...].astype(F32)
        rows = lax.broadcasted_iota(jnp.int32, g.shape, 0)
        w = w_ref[...]
        sg = _sigmoid(a)
        dup_ref[...] = (dact * a * sg).astype(BF16)
        da = dact * up * sg * (1.0 + a * (1.0 - sg))
        da1 = _shift_up(da, 1, rows, S)
        da2 = _shift_up(da, 2, rows, S)
        dg_ref[...] = (w[2:3] * da + w[1:2] * da1 + w[0:1] * da2).astype(BF16)

        @pl.when(pl.program_id(1) == 0)
        def _():
            dw_ref[...] = jnp.zeros_like(dw_ref)
            db_ref[...] = jnp.zeros_like(db_ref)

        dw_ref[0:1, :] += jnp.sum(da2 * g, axis=0, keepdims=True)
        dw_ref[1:2, :] += jnp.sum(da1 * g, axis=0, keepdims=True)
        dw_ref[2:3, :] += jnp.sum(da * g, axis=0, keepdims=True)
        db_ref[...] += jnp.sum(da, axis=0, keepdims=True)

    col = pl.BlockSpec((None, S, tc), lambda j, b: (b, 0, j))
    return pl.pallas_call(
        body, name=name, grid=(nc, B),
        in_specs=[col,
                  pl.BlockSpec((None, S, tc), lambda j, b: (b, 0, j + nc)),
                  col,
                  pl.BlockSpec((3, tc), lambda j, b: (0, j)),
                  col],
        out_specs=[col, col, pl.BlockSpec((3, tc), lambda j, b: (0, j)), pl.BlockSpec((1, tc), lambda j, b: (0, j))],
        out_shape=[jax.ShapeDtypeStruct((B, S, D_FF), BF16), jax.ShapeDtypeStruct((B, S, D_FF), BF16),
                   jax.ShapeDtypeStruct((3, D_FF), F32), jax.ShapeDtypeStruct((1, D_FF), F32)],
        compiler_params=_params("parallel", "arbitrary"),
    )(gu, gu, a_pre, conv_w, dact)


HGRN_CPB = 4
HF = HGRN_HEADS * HGRN_DK


def _tri(n, upper=False):
    r = lax.broadcasted_iota(jnp.int32, (n, n), 0)
    c = lax.broadcasted_iota(jnp.int32, (n, n), 1)
    return (c >= r) if upper else (r >= c)


def _hs(h):
    return slice(h * HGRN_DK, (h + 1) * HGRN_DK)


def _cumsum_rows(tri_b, x):
    hi = x.astype(BF16)
    lo = (x - hi.astype(F32)).astype(BF16)
    return _nn(tri_b, hi) + _nn(tri_b, lo)


def _hgrn_pre(q, fz, lb, tril_b):
    qf = _silu(q)
    sg = _sigmoid(fz)
    f = lb + (1.0 - lb) * sg
    k = 1.0 - f
    b = _cumsum_rows(tril_b, jnp.log2(f))
    bref = b[CHUNK // 2:CHUNK // 2 + 1, :]
    blast = b[CHUNK - 1:CHUNK, :]
    e1 = jnp.exp2(b - bref)
    e2 = jnp.exp2(bref - b)
    e3 = e1 * jnp.exp2(bref)
    e4 = e2 * jnp.exp2(blast - bref)
    dec = jnp.exp2(blast)
    return sg, f, (e1, e2, e3, e4), qf * e1, k * e2, qf * e3, k * e4, dec


def _hgrn_fwd(zh, lb, gn, *, name):
    B, S, _ = zh.shape
    cpb = HGRN_CPB
    ts = cpb * CHUNK
    nblk = S // ts

    def body(z_ref, lb_ref, gn_ref, o_ref, st_ref, state):
        @pl.when(pl.program_id(1) == 0)
        def _():
            state[...] = jnp.zeros_like(state)

        H = HGRN_HEADS
        causal = _tri(CHUNK)
        tril_b = causal.astype(BF16)
        lb = lb_ref[...]
        for c in range(cpb):
            rows = slice(c * CHUNK, (c + 1) * CHUNK)
            q = z_ref[rows, 0:HF].astype(F32)
            fz = z_ref[rows, HF:2 * HF].astype(F32)
            v = z_ref[rows, 2 * HF:3 * HF]
            hg = z_ref[rows, 3 * HF:4 * HF].astype(F32)
            _, _, _, q_in, k_in, q_out, k_st, dec = _hgrn_pre(q, fz, lb, tril_b)
            q_in, k_in, q_out, k_st = (t.astype(BF16) for t in (q_in, k_in, q_out, k_st))
            a = [jnp.where(causal, _nt(q_in[:, _hs(h)], k_in[:, _hs(h)]), 0.0).astype(BF16) for h in range(H)]
            st = [state[h] for h in range(H)]
            for h in range(H):
                st_ref[c, h] = st[h]
            o = [_nn(a[h], v[:, _hs(h)]) + _nt(q_out[:, _hs(h)], st[h].astype(BF16)) for h in range(H)]
            for h in range(H):
                state[h] = st[h] * dec[:, _hs(h)] + _tn(v[:, _hs(h)], k_st[:, _hs(h)])
            gate = _silu(hg)
            for h in range(H):
                o_ref[rows, _hs(h)] = (_rms(o[h], gn_ref[...]) * gate[:, _hs(h)]).astype(BF16)

    return pl.pallas_call(
        body, name=name, grid=(B, nblk),
        in_specs=[pl.BlockSpec((None, ts, 4 * HF), lambda b, s: (b, s, 0)),
                  pl.BlockSpec((1, HF), lambda b, s: (0, 0)),
                  pl.BlockSpec((1, HGRN_DK), lambda b, s: (0, 0))],
        out_specs=[pl.BlockSpec((None, ts, HF), lambda b, s: (b, s, 0)),
                   pl.BlockSpec((None, cpb, HGRN_HEADS, HGRN_DK, HGRN_DK), lambda b, s: (b, s, 0, 0, 0))],
        out_shape=[jax.ShapeDtypeStruct((B, S, HF), BF16),
                   jax.ShapeDtypeStruct((B, S // CHUNK, HGRN_HEADS, HGRN_DK, HGRN_DK), F32)],
        scratch_shapes=[pltpu.VMEM((HGRN_HEADS, HGRN_DK, HGRN_DK), F32)],
        compiler_params=_params("arbitrary", "arbitrary"),
    )(zh, lb, gn)


def _hgrn_bwd(zh, lb, gn, states, doa, dz, *, name):
    B, S, _ = zh.shape
    cpb = HGRN_CPB
    ts = cpb * CHUNK
    nblk = S // ts
    rev = lambda b, s: (b, nblk - 1 - s, 0)

    def body(z_ref, lb_ref, gn_ref, st_ref, do_ref, dz_in, dz_ref, dlb_ref, dgn_ref, dstate):
        @pl.when(pl.program_id(1) == 0)
        def _():
            dstate[...] = jnp.zeros_like(dstate)

        @pl.when((pl.program_id(0) == 0) & (pl.program_id(1) == 0))
        def _():
            dlb_ref[...] = jnp.zeros_like(dlb_ref)
            dgn_ref[...] = jnp.zeros_like(dgn_ref)

        H = HGRN_HEADS
        cat = lambda xs: jnp.concatenate(xs, axis=1)
        causal = _tri(CHUNK)
        tril_b = causal.astype(BF16)
        triu_b = _tri(CHUNK, upper=True).astype(BF16)
        rowid = lax.broadcasted_iota(jnp.int32, (CHUNK, HF), 0)
        lb = lb_ref[...]
        gn = gn_ref[...]
        for c in reversed(range(cpb)):
            rows = slice(c * CHUNK, (c + 1) * CHUNK)
            q = z_ref[rows, 0:HF].astype(F32)
            fz = z_ref[rows, HF:2 * HF].astype(F32)
            v = z_ref[rows, 2 * HF:3 * HF]
            hg = z_ref[rows, 3 * HF:4 * HF].astype(F32)
            sg, f, (e1, e2, e3, e4), q_in, k_in, q_out, k_st, dec = _hgrn_pre(q, fz, lb, tril_b)
            q_in_b, k_in_b, q_out_b, k_st_b = (t.astype(BF16) for t in (q_in, k_in, q_out, k_st))
            a_b = [jnp.where(causal, _nt(q_in_b[:, _hs(h)], k_in_b[:, _hs(h)]), 0.0).astype(BF16) for h in range(H)]
            st = [st_ref[c, h] for h in range(H)]
            st_b = [t.astype(BF16) for t in st]
            o = [_nn(a_b[h], v[:, _hs(h)]) + _nt(q_out_b[:, _hs(h)], st_b[h]) for h in range(H)]
            dout = do_ref[rows, :].astype(F32)
            shg = _sigmoid(hg)
            gate = hg * shg
            do_l, dgn_acc = [], jnp.zeros_like(gn)
            for h in range(H):
                _, norm_vjp = jax.vjp(_rms, o[h], gn)
                d_o, d_gn = norm_vjp(dout[:, _hs(h)] * gate[:, _hs(h)])
                do_l.append(d_o)
                dgn_acc = dgn_acc + d_gn
            dgn_ref[...] += dgn_acc
            on = cat([_rms(o[h], gn) for h in range(H)])
            dhg = dout * on * shg * (1.0 + hg * (1.0 - shg))
            do_b = [t.astype(BF16) for t in do_l]
            dst = [dstate[h] for h in range(H)]
            dst_b = [t.astype(BF16) for t in dst]
            da_b = [jnp.where(causal, _nt(do_b[h], v[:, _hs(h)]), 0.0).astype(BF16) for h in range(H)]
            dv = cat([_tn(a_b[h], do_b[h]) + _nt(k_st_b[:, _hs(h)], dst_b[h]) for h in range(H)])
            dq_in = cat([_nn(da_b[h], k_in_b[:, _hs(h)]) for h in range(H)])
            dk_in = cat([_tn(da_b[h], q_in_b[:, _hs(h)]) for h in range(H)])
            dq_out = cat([_nn(do_b[h], st_b[h]) for h in range(H)])
            dk_st = cat([_nn(v[:, _hs(h)], dst_b[h]) for h in range(H)])
            ddec = cat([jnp.sum(st[h] * dst[h], axis=0, keepdims=True) for h in range(H)])
            for h in range(H):
                dstate[h] = dst[h] * dec[:, _hs(h)] + _tn(do_b[h], q_out_b[:, _hs(h)])
            t_qin = dq_in * q_in
            t_kin = dk_in * k_in
            t_kst = dk_st * k_st
            db = t_qin - t_kin + dq_out * q_out - t_kst
            dbref = jnp.sum(t_kin - t_qin, axis=0, keepdims=True)
            dblast = jnp.sum(t_kst, axis=0, keepdims=True) + ddec * dec
            db = db + jnp.where(rowid == CHUNK // 2, dbref, 0.0) + jnp.where(rowid == CHUNK - 1, dblast, 0.0)
            dlogf = _cumsum_rows(triu_b, db)
            dqf = dq_in * e1 + dq_out * e3
            dk = dk_in * e2 + dk_st * e4
            df = dlogf / f - dk
            dfz = df * (1.0 - lb) * sg * (1.0 - sg)
            dlb_ref[...] += jnp.sum(df * (1.0 - sg), axis=0, keepdims=True)
            sq = _sigmoid(q)
            dq = dqf * sq * (1.0 + q * (1.0 - sq))
            dz_ref[rows, 0:HF] = dq.astype(BF16)
            dz_ref[rows, HF:2 * HF] = dfz.astype(BF16)
            dz_ref[rows, 2 * HF:3 * HF] = dv.astype(BF16)
            dz_ref[rows, 3 * HF:4 * HF] = dhg.astype(BF16)

    return pl.pallas_call(
        body, name=name, grid=(B, nblk),
        in_specs=[pl.BlockSpec((None, ts, 4 * HF), rev),
                  pl.BlockSpec((1, HF), lambda b, s: (0, 0)),
                  pl.BlockSpec((1, HGRN_DK), lambda b, s: (0, 0)),
                  pl.BlockSpec((None, cpb, HGRN_HEADS, HGRN_DK, HGRN_DK), lambda b, s: (b, nblk - 1 - s, 0, 0, 0)),
                  pl.BlockSpec((None, ts, HF), rev),
                  ANY],
        out_specs=[pl.BlockSpec((None, ts, 4 * HF), rev),
                   pl.BlockSpec((1, HF), lambda b, s: (0, 0)),
                   pl.BlockSpec((1, HGRN_DK), lambda b, s: (0, 0))],
        out_shape=[jax.ShapeDtypeStruct(dz.shape, BF16),
                   jax.ShapeDtypeStruct((1, HF), F32),
                   jax.ShapeDtypeStruct((1, HGRN_DK), F32)],
        input_output_aliases={5: 0},
        scratch_shapes=[pltpu.VMEM((HGRN_HEADS, HGRN_DK, HGRN_DK), F32)],
        compiler_params=_params("arbitrary", "arbitrary"),
    )(zh, lb, gn, states, doa, dz)


KV_W = ATT_KV_HEADS * ATT_HD
ATT_SCALE = ATT_HD ** -0.5


def _rope(x, cos, sin, inverse=False):
    half = ROPE_DIM // 2
    outs = []
    for p in range(x.shape[1] // 128):
        xp = x[:, p * 128:(p + 1) * 128]
        lane = lax.broadcasted_iota(jnp.int32, xp.shape, 1) % ATT_HD
        sw = jnp.where(lane < half, pltpu.roll(xp, 128 - half, 1), pltpu.roll(xp, half, 1))
        outs.append(xp * cos - sw * sin if inverse else xp * cos + sw * sin)
    return outs[0] if len(outs) == 1 else jnp.concatenate(outs, axis=1)


PAIRS_PER_KV = ATT_GROUP // 2


def _swap_halves(x):
    return pltpu.roll(x, ATT_HD, 1)


def _kv_padded(t, low):
    sw = _swap_halves(t)
    zero = jnp.zeros_like(t)
    out = []
    for g in range(ATT_KV_HEADS):
        in_low, in_high = (t, sw) if g == 0 else (sw, t)
        out.append((jnp.where(low, in_low, zero).astype(BF16), jnp.where(low, zero, in_high).astype(BF16)))
    return out


def _swa_mask(first_block):
    qi = lax.broadcasted_iota(jnp.int32, (WINDOW, 2 * WINDOW), 0)
    mi = lax.broadcasted_iota(jnp.int32, (WINDOW, 2 * WINDOW), 1)
    band = (mi > qi) & (mi <= qi + WINDOW)
    return band & (jnp.logical_not(first_block) | (mi >= WINDOW))


def _swa_specs(nb):
    cur = lambda b, i: (b, i, 0)
    prev = lambda b, i: (b, jnp.maximum(i - 1, 0), 0)
    return cur, prev


def _swa_z_specs():
    q = pl.BlockSpec((None, WINDOW, W_AQ), lambda b, i: (b, i, O_AQ // W_AQ))
    kv_prev = pl.BlockSpec((None, WINDOW, W_AKV), lambda b, i: (b, jnp.maximum(i - 1, 0), O_AKV // W_AKV))
    kv_cur = pl.BlockSpec((None, WINDOW, W_AKV), lambda b, i: (b, i, O_AKV // W_AKV))
    return q, kv_prev, kv_cur


def _swa_fwd(z, cos, sin, sinks, *, name):
    B, S, _ = z.shape
    nb = S // WINDOW
    cur, prev = _swa_specs(nb)

    def body(q_ref, kvp_ref, kvc_ref, cp_ref, sp_ref, cc_ref, sc_ref, sink_ref, o_ref, lse_ref, qr_ref, kr_ref):
        cos_c, sin_c = cc_ref[...], sc_ref[...]
        q = (_rope(q_ref[...].astype(F32), cos_c, sin_c) * ATT_SCALE).astype(BF16)
        k = jnp.concatenate([_rope(kvp_ref[:, :KV_W].astype(F32), cp_ref[...], sp_ref[...]),
                             _rope(kvc_ref[:, :KV_W].astype(F32), cos_c, sin_c)], axis=0)
        qr_ref[...] = q
        kr_ref[...] = k[WINDOW:].astype(BF16)
        v = jnp.concatenate([kvp_ref[:, KV_W:], kvc_ref[:, KV_W:]], axis=0).astype(F32)
        low = lax.broadcasted_iota(jnp.int32, k.shape, 1) < ATT_HD
        kpad = _kv_padded(k, low)
        vpad = _kv_padded(v, low)
        mask = _swa_mask(pl.program_id(1) == 0)
        lses = []
        for g in range(ATT_KV_HEADS):
            pairs = range(g * PAIRS_PER_KV, (g + 1) * PAIRS_PER_KV)
            keys = [(p, e) for p in pairs for e in (0, 1)]
            qp = {p: q[:, p * 128:(p + 1) * 128] for p in pairs}
            s = {pe: jnp.where(mask, _nt(qp[pe[0]], kpad[g][pe[1]]), NEG_INF) for pe in keys}
            pr = {}
            for pe in keys:
                sink = sink_ref[0, 2 * pe[0] + pe[1]]
                m = jnp.maximum(jnp.max(s[pe], axis=1, keepdims=True), sink)
                ex = jnp.exp(s[pe] - m)
                den = jnp.sum(ex, axis=1, keepdims=True) + jnp.exp(sink - m)
                pr[pe] = (ex * (1.0 / den)).astype(BF16)
                lses.append(m + jnp.log(den))
            for p in pairs:
                o_ref[:, p * 128:(p + 1) * 128] = (_nn(pr[p, 0], vpad[g][0]) + _nn(pr[p, 1], vpad[g][1])).astype(BF16)
        lse_ref[...] = jnp.concatenate(lses, axis=1)

    tab = lambda im: pl.BlockSpec((None, WINDOW, 128), im)
    return pl.pallas_call(
        body, name=name, grid=(B, nb),
        in_specs=[*_swa_z_specs(),
                  tab(prev), tab(prev), tab(cur), tab(cur),
                  pl.BlockSpec(memory_space=pltpu.SMEM)],
        out_specs=[pl.BlockSpec((None, WINDOW, D_MODEL), cur), pl.BlockSpec((None, WINDOW, ATT_HEADS), cur),
                   pl.BlockSpec((None, WINDOW, D_MODEL), cur), pl.BlockSpec((None, WINDOW, KV_W), cur)],
        out_shape=[jax.ShapeDtypeStruct((B, S, D_MODEL), BF16), jax.ShapeDtypeStruct((B, S, ATT_HEADS), F32),
                   jax.ShapeDtypeStruct((B, S, D_MODEL), BF16), jax.ShapeDtypeStruct((B, S, KV_W), BF16)],
        compiler_params=_params("parallel", "parallel"),
    )(z, z, z, cos, sin, cos, sin, sinks)


def _swa_bwd(z, qr, kr, cos, sin, sinks, lse, dob, dz, *, name):
    B, S, _ = z.shape
    nb = S // WINDOW
    cur, prev = _swa_specs(nb)

    def body(q_ref, krp_ref, krc_ref, kvp_ref, kvc_ref, cp_ref, sp_ref, cc_ref, sc_ref, sink_ref, lse_ref, do_ref, dz_in,
             dq_ref, dkc_ref, dkp_ref, dsink_ref):
        @pl.when((pl.program_id(0) == 0) & (pl.program_id(1) == 0))
        def _():
            dsink_ref[...] = jnp.zeros_like(dsink_ref)

        cos_c, sin_c, cos_p, sin_p = cc_ref[...], sc_ref[...], cp_ref[...], sp_ref[...]
        q = q_ref[...]
        k = jnp.concatenate([krp_ref[...], krc_ref[...]], axis=0).astype(F32)
        v = jnp.concatenate([kvp_ref[:, KV_W:], kvc_ref[:, KV_W:]], axis=0).astype(F32)
        low = lax.broadcasted_iota(jnp.int32, k.shape, 1) < ATT_HD
        kpad = _kv_padded(k, low)
        vpad = _kv_padded(v, low)
        mask = _swa_mask(pl.program_id(1) == 0)
        lse = lse_ref[...]
        dq_parts, dk_sum, dv_sum, dsinks = [], [], [], []
        for g in range(ATT_KV_HEADS):
            pairs = range(g * PAIRS_PER_KV, (g + 1) * PAIRS_PER_KV)
            keys = [(p, e) for p in pairs for e in (0, 1)]
            qp = {p: q[:, p * 128:(p + 1) * 128] for p in pairs}
            dop = {p: do_ref[:, p * 128:(p + 1) * 128] for p in pairs}
            s = {pe: jnp.where(mask, _nt(qp[pe[0]], kpad[g][pe[1]]), NEG_INF) for pe in keys}
            dp = {pe: _nt(dop[pe[0]], vpad[g][pe[1]]) for pe in keys}
            pr, ds = {}, {}
            for pe in keys:
                h = 2 * pe[0] + pe[1]
                lse_h = lse[:, h:h + 1]
                pf = jnp.exp(s[pe] - lse_h)
                delta = jnp.sum(pf * dp[pe], axis=1, keepdims=True)
                ds[pe] = (pf * (dp[pe] - delta)).astype(BF16)
                pr[pe] = pf.astype(BF16)
                p_sink = jnp.exp(sink_ref[0, h] - lse_h)
                dsinks.append(-jnp.sum(p_sink * delta, axis=0, keepdims=True))
            for p in pairs:
                dq_parts.append((_nn(ds[p, 0], kpad[g][0]) + _nn(ds[p, 1], kpad[g][1])) * ATT_SCALE)
            x = [sum(_tn(ds[p, e], qp[p]) for p in pairs) for e in (0, 1)]
            y = [sum(_tn(pr[p, e], dop[p]) for p in pairs) for e in (0, 1)]
            zk = jnp.where(low, x[0], x[1])
            zv = jnp.where(low, y[0], y[1])
            dk_sum.append(zk + _swap_halves(zk))
            dv_sum.append(zv + _swap_halves(zv))
        dq_ref[...] = _rope(jnp.concatenate(dq_parts, axis=1), cos_c, sin_c, inverse=True).astype(BF16)
        dk = jnp.where(low, dk_sum[0], dk_sum[1])
        dv = jnp.where(low, dv_sum[0], dv_sum[1])
        dkp_ref[:, :KV_W] = _rope(dk[:WINDOW], cos_p, sin_p, inverse=True)
        dkp_ref[:, KV_W:] = dv[:WINDOW]
        dkc_ref[:, :KV_W] = _rope(dk[WINDOW:], cos_c, sin_c, inverse=True)
        dkc_ref[:, KV_W:] = dv[WINDOW:]
        dsink_ref[...] += jnp.concatenate(dsinks, axis=1)

    tab = lambda im: pl.BlockSpec((None, WINDOW, 128), im)
    return pl.pallas_call(
        body, name=name, grid=(B, nb),
        in_specs=[pl.BlockSpec((None, WINDOW, D_MODEL), cur), tab(prev), tab(cur),
                  *_swa_z_specs()[1:],
                  tab(prev), tab(prev), tab(cur), tab(cur),
                  pl.BlockSpec(memory_space=pltpu.SMEM),
                  pl.BlockSpec((None, WINDOW, ATT_HEADS), cur),
                  pl.BlockSpec((None, WINDOW, D_MODEL), cur),
                  ANY],
        out_specs=[_swa_z_specs()[0],
                   pl.BlockSpec((None, WINDOW, 2 * KV_W), cur), pl.BlockSpec((None, WINDOW, 2 * KV_W), cur),
                   pl.BlockSpec((1, ATT_HEADS), lambda b, i: (0, 0))],
        out_shape=[jax.ShapeDtypeStruct(dz.shape, BF16),
                   jax.ShapeDtypeStruct((B, S, 2 * KV_W), F32), jax.ShapeDtypeStruct((B, S, 2 * KV_W), F32),
                   jax.ShapeDtypeStruct((1, ATT_HEADS), F32)],
        input_output_aliases={12: 0},
        compiler_params=_params("arbitrary", "arbitrary"),
    )(qr, kr, kr, z, z, cos, sin, cos, sin, sinks, lse, dob, dz)


def _swa_dkv_combine(dkv_cur, dkv_prev, dz, *, name):
    B, S, W = dkv_cur.shape

    def body(c_ref, p_ref, dz_in, o_ref):
        rows = lax.broadcasted_iota(jnp.int32, (S, W), 0)
        o_ref[...] = (c_ref[...] + _shift_up(p_ref[...], WINDOW, rows, S)).astype(BF16)

    seq = pl.BlockSpec((None, S, W), lambda b: (b, 0, 0))
    return pl.pallas_call(
        body, name=name, grid=(B,),
        in_specs=[seq, seq, ANY], out_specs=pl.BlockSpec((None, S, W), lambda b: (b, 0, O_AKV // W_AKV)),
        out_shape=jax.ShapeDtypeStruct(dz.shape, BF16),
        input_output_aliases={2: 0},
        compiler_params=_params("parallel"),
    )(dkv_cur, dkv_prev, dz)


def _rope_tables(positions):
    half = ROPE_DIM // 2
    inv = ROPE_THETA ** (-2.0 * jnp.arange(half, dtype=F32) / ROPE_DIM)
    ang = positions.astype(F32)[..., None] * inv
    c, s = jnp.cos(ang), jnp.sin(ang)
    pad = jnp.zeros(ang.shape[:-1] + (ATT_HD - ROPE_DIM,), F32)
    cos = jnp.concatenate([c, c, pad + 1.0], axis=-1)
    sin = jnp.concatenate([-s, s, pad], axis=-1)
    return jnp.tile(cos, (1, 1, 2)), jnp.tile(sin, (1, 1, 2))


def _lower_bound(lb_logits, *, name):
    def body(l_ref, o_ref):
        l = l_ref[...]
        e = jnp.exp(l - jnp.max(l, axis=0, keepdims=True))
        o_ref[...] = e[0:1] / jnp.sum(e, axis=0, keepdims=True)

    return pl.pallas_call(body, name=name, out_shape=jax.ShapeDtypeStruct((1, lb_logits.shape[1]), F32))(lb_logits)


W_ZH, W_GATES, W_AQ, W_AKV = 4 * HF, 2 * D_MODEL, ATT_HEADS * ATT_HD, 2 * KV_W
O_ZH, O_GATES, O_AQ, O_AKV = 0, W_ZH, W_ZH + W_GATES, W_ZH + W_GATES + W_AQ
W_IN = W_ZH + W_GATES + W_AQ + W_AKV


W_IN_BLK = W_IN // N_DEV


def _w_in_pieces():
    ref_segments = [(0, W_ZH, O_ZH), (W_ZH, W_AQ, O_AQ), (W_ZH + W_AQ, W_AKV, O_AKV), (W_ZH + W_AQ + W_AKV, W_GATES, O_GATES)]
    out = []
    for p in range(N_DEV):
        lo, hi = p * W_IN_BLK, (p + 1) * W_IN_BLK
        for s0, w, d0 in ref_segments:
            a, b = max(lo, s0), min(hi, s0 + w)
            if a < b:
                out.append((p, a - lo, d0 + a - s0, b - a))
    return out


def _plain_pieces(width):
    return [(p, 0, p * width, width) for p in range(N_DEV)]


def _cols_from_blocks(blocks, pieces, *, name):
    _, rows, width = blocks.shape
    tr = 256

    def body(b_ref, o_ref):
        for p, s, d, w in pieces:
            o_ref[:, d:d + w] = b_ref[p, :, s:s + w]

    return pl.pallas_call(
        body, name=name, grid=(rows // tr,),
        in_specs=[pl.BlockSpec((N_DEV, tr, width), lambda i: (0, i, 0))], out_specs=pl.BlockSpec((tr, N_DEV * width), lambda i: (i, 0)),
        out_shape=jax.ShapeDtypeStruct((rows, N_DEV * width), blocks.dtype), compiler_params=_params("parallel"))(blocks)


def _blocks_from_cols(full, pieces, *, name):
    rows, cols = full.shape
    width = cols // N_DEV
    tr = 256

    def body(w_ref, o_ref):
        for p, s, d, w in pieces:
            o_ref[p, :, s:s + w] = w_ref[:, d:d + w]

    return pl.pallas_call(
        body, name=name, grid=(rows // tr,),
        in_specs=[pl.BlockSpec((tr, cols), lambda i: (i, 0))], out_specs=pl.BlockSpec((N_DEV, tr, width), lambda i: (0, i, 0)),
        out_shape=jax.ShapeDtypeStruct((N_DEV, rows, width), full.dtype), compiler_params=_params("parallel"))(full)


def _local_step(x, positions, target, small, w_in, rest_weights, emit, start_token):
    B, S, D = x.shape
    T = B * S
    x2 = x.reshape(T, D)
    cos, sin = _rope_tables(positions)
    lb = _lower_bound(small["lb_logits"], name="lb_fwd")
    zero = lambda tok: tok[0:1, 0:1]

    u1 = _norm_cast(x2, small["norm1_g"] + zero(start_token), name="norm1")
    z = _matmul(u1, w_in, out_dtype=BF16, name="mm_z", tm=1024, tn=W_IN // 2)
    z3 = z.reshape(B, S, W_IN)
    oa, states = _hgrn_fwd(z3, lb, small["hgrn_norm_g"], name="hgrn_fwd")
    ob, lse, qr, kr = _swa_fwd(z3, cos, sin, small["attn_sinks"], name="swa_fwd")
    oa2 = oa.reshape(T, D)
    ob2 = ob.reshape(T, D)
    W = rest_weights("mix", ob)
    pa = _matmul(oa2, W["w_a"], out_dtype=BF16, name="mm_pa", tm=2048, tn=512)
    pb = _matmul(ob2, W["w_b"], out_dtype=BF16, name="mm_pb", tm=2048, tn=512)
    merged = _merge_fwd(z, pa, pb, name="merge_fwd")
    h = _matmul(merged, W["w_out"], addend=x2, name="mm_h", tm=2048, tn=512)
    u2 = _norm_cast(h, small["norm2_g"], name="norm2")
    W.update(rest_weights("ffn", u2))
    gu = _matmul(u2, W["w_ffn"], out_dtype=BF16, name="mm_gu", tm=2048, tn=512)
    gu3 = gu.reshape(B, S, 2 * D_FF)
    act, a_pre = _conv_act_fwd(gu3, W["conv_w"], small["conv_b"], name="conv_act_fwd")
    act2 = act.reshape(T, D_FF)
    h2 = _matmul(act2, W["w_down"], addend=h, name="mm_h2", tm=1024, tn=1024)

    g = {}
    dh2, dh2b, g["final_g"], loss = _final_loss_bwd(h2, small["final_g"].reshape(1, D), target.reshape(T, D), name="final_loss_bwd")
    dact = _matmul(dh2b, W["w_down"], tb=True, out_dtype=BF16, name="mm_dact", tm=1024, tn=D_FF)
    dw_down_t = _matmul(dh2b, act2, ta=True, out_dtype=BF16, name="mm_dw_down", tm=1024, tn=256, tk=8192)
    dg_, dup, g["conv_w"], g["conv_b"] = _conv_act_bwd(gu3, a_pre, W["conv_w"], dact.reshape(B, S, D_FF), name="conv_act_bwd")
    dg2 = dg_.reshape(T, D_FF)
    dup2 = dup.reshape(T, D_FF)
    du2 = _matmul(dg2, W["w_ffn"], tb=True, name="mm_du2_g", tm=1024, tn=1024, b_koff=0)
    du2 = _matmul(dup2, W["w_ffn"], tb=True, addend=du2, out_dtype=BF16, name="mm_du2_u", tm=1024, tn=1024, b_koff=1)
    dw_ffn = _matmul(u2, dg2, ta=True, out_dtype=BF16, into=lax.empty((D, 2 * D_FF), BF16), o_noff=0, name="mm_dw_ffn_g", tm=1024, tn=256, tk=8192)
    dw_ffn = _matmul(u2, dup2, ta=True, out_dtype=BF16, into=dw_ffn, o_noff=D_FF // 256, name="mm_dw_ffn_u", tm=1024, tn=256, tk=8192)
    tok = emit("ffn", dict(w_ffn=dw_ffn, w_down=dw_down_t.T))
    dh, dhb, g["norm2_g"] = _norm_bwd_add(h, small["norm2_g"] + zero(tok), du2, dh2, name="norm2_bwd")
    dmerged = _matmul(dhb, W["w_out"], tb=True, out_dtype=BF16, name="mm_dmerged", tm=2048, tn=512)
    dw_out = _matmul(merged, dhb, ta=True, out_dtype=BF16, name="mm_dw_out", tm=1024, tn=1024, tk=2048)
    dz, dpa, dpb = _merge_bwd(z, pa, pb, dmerged, lax.empty((T, W_IN), BF16), name="merge_bwd")
    doa =_matmul(dpa, W["w_a"], tb=True, out_dtype=BF16, name="mm_doa", tm=2048, tn=512)
    dw_a = _matmul(oa2, dpa, ta=True, out_dtype=BF16, name="mm_dw_a", tm=1024, tn=1024, tk=2048)
    dob = _matmul(dpb, W["w_b"], tb=True, out_dtype=BF16, name="mm_dob", tm=2048, tn=512)
    dw_b = _matmul(ob2, dpb, ta=True, out_dtype=BF16, name="mm_dw_b", tm=1024, tn=1024, tk=2048)
    tok = emit("mix", dict(w_out=dw_out, w_a=dw_a, w_b=dw_b))
    dz3, dkv_cur, dkv_prev, dsinks = _swa_bwd(z3, qr, kr, cos, sin, small["attn_sinks"] + zero(tok), lse, dob.reshape(B, S, D),
                                              dz.reshape(B, S, W_IN), name="swa_bwd")
    dz3 = _swa_dkv_combine(dkv_cur, dkv_prev, dz3, name="swa_dkv")
    g["attn_sinks"] = dsinks
    dz3, g["lb"], g["hgrn_norm_g"] = _hgrn_bwd(z3, lb, small["hgrn_norm_g"], states, doa.reshape(B, S, D), dz3, name="hgrn_bwd")
    dz = dz3.reshape(T, W_IN)
    dw_in = _matmul(u1, dz, ta=True, out_dtype=BF16, name="mm_dw_in", tm=1024, tn=256, tk=8192)
    tok = emit("in", dict(w_in=dw_in))
    du1 = _matmul(dz, w_in, tb=True, after=tok, out_dtype=BF16, name="mm_du1", tm=1024, tn=512)
    dx, g["norm1_g"] = _norm_bwd_add(x2, small["norm1_g"], du1, dh, with_bf16=False, name="norm1_bwd")
    g["lb_logits"] = _lb_bwd(g.pop("lb"), lb, name="lb_bwd")
    return loss, dx.reshape(B, S, D), g


def _my_place():
    return lax.axis_index("x"), lax.axis_index("y"), lax.axis_index("c")


def _gather_blocks(x_ref, out_ref, send_sems, recv_sems, local_sem):
    x, y, c = _my_place()
    me, sibling = (x, y, c), (x, y, 1 - c)
    chips = [(1 - x, y), (x, 1 - y), (1 - x, 1 - y)]

    def slot(px, py, pc):
        return out_ref.at[4 * px + 2 * py + pc]

    def copy(k, block, to, src=None):
        return pltpu.make_async_remote_copy(
            src_ref=slot(*block) if src is None else src, dst_ref=slot(*block),
            send_sem=send_sems.at[k], recv_sem=recv_sems.at[k], device_id=to, device_id_type=MESH)

    mine = pltpu.make_async_copy(x_ref, slot(*me), local_sem)
    mine.start()
    first = [copy(0, me, sibling, src=x_ref)]
    first += [copy(1 + j, me, (*chip, c), src=x_ref) for j, chip in enumerate(chips)]
    for cp in first:
        cp.start()
    passed = [copy(4 + j, (*chip, c), sibling) for j, chip in enumerate(chips)]
    for j, chip in enumerate(chips):
        copy(1 + j, (*chip, c), me).wait_recv()
        passed[j].start()
    copy(0, sibling, me).wait_recv()
    for j, chip in enumerate(chips):
        copy(4 + j, (*chip, 1 - c), me).wait_recv()
    for cp in first + passed:
        cp.wait_send()
    mine.wait()


GATHER_SEMS = [pltpu.SemaphoreType.DMA((7,)), pltpu.SemaphoreType.DMA((7,)), pltpu.SemaphoreType.DMA]


def _all_gather(blk, *, name):
    return pl.pallas_call(
        _gather_body_fn(), name=name,
        out_shape=jax.ShapeDtypeStruct((N_DEV,) + blk.shape, blk.dtype),
        in_specs=[ANY], out_specs=ANY,
        scratch_shapes=GATHER_SEMS,
    )(blk)


def _gather_body_fn():
    def body(x_ref, out_ref, send_sems, recv_sems, local_sem):
        _gather_blocks(x_ref, out_ref, send_sems, recv_sems, local_sem)
    return body


SLAB_W = 1152
SMALL_SHAPES = dict(norm1_g=(1, D_MODEL), lb_logits=(2, HGRN_HEADS * HGRN_DK), hgrn_norm_g=(1, HGRN_DK), attn_sinks=(1, ATT_HEADS),
                    norm2_g=(1, D_MODEL), conv_b=(1, D_FF), final_g=(1, D_MODEL))
CONVW_BLK = D_FF // N_DEV
CONVW_STRIDE = SLAB_W // 3


def _slab_layout():
    layout, r = {}, 0
    for nm, (nr, w) in SMALL_SHAPES.items():
        layout[nm] = []
        for i in range(nr):
            for c0 in range(0, w, SLAB_W):
                layout[nm].append((r, i, c0, min(SLAB_W, w - c0)))
                r += 1
    return layout, r


SMALL_ROWS, _N_SMALL_ROWS = _slab_layout()
CONV_ROW0 = -(-_N_SMALL_ROWS // 8) * 8
LOSS_ROW = CONV_ROW0 + N_DEV
SLAB_ROWS = LOSS_ROW + 8


def _small_step(grads, g_conv_w, loss, params, moments, variances, dev, *, name):
    names = list(SMALL_ROWS)
    n = len(names)

    def body(dev_ref, *refs):
        g_refs = dict(zip(names, refs[:n]))
        gc_ref, loss_ref = refs[n], refs[n + 1]
        base = n + 2
        w_refs, m_refs, v_refs = (dict(zip(names + ["conv_w"], refs[base + i * (n + 1):base + (i + 1) * (n + 1)])) for i in range(3))
        o = base + 3 * (n + 1)
        gath_ref, loss_out = refs[o], refs[o + 1]
        outs = {nm: refs[o + 2 + 4 * i:o + 6 + 4 * i] for i, nm in enumerate(names + ["conv_w"])}
        slab, total, send_sems, recv_sems, local_sem = refs[-5:]

        slab[...] = jnp.zeros_like(slab)
        for nm, pieces in SMALL_ROWS.items():
            for r, i, c0, w in pieces:
                slab[r:r + 1, 0:w] = g_refs[nm][i:i + 1, c0:c0 + w]
        for p in range(N_DEV):
            for j in range(3):
                slab[CONV_ROW0 + p:CONV_ROW0 + p + 1, j * CONVW_STRIDE:j * CONVW_STRIDE + CONVW_BLK] = gc_ref[j:j + 1, p * CONVW_BLK:(p + 1) * CONVW_BLK]
        slab[LOSS_ROW:LOSS_ROW + 1, 0:1] = loss_ref[...]
        _gather_blocks(slab, gath_ref, send_sems, recv_sems, local_sem)
        acc = gath_ref[0]
        for p in range(1, N_DEV):
            acc = acc + gath_ref[p]
        total[...] = acc
        loss_out[...] = total[LOSS_ROW:LOSS_ROW + 1, 0:1]

        def update(nm, g, i, c0, w):
            at = (slice(i, i + 1), slice(c0, c0 + w))
            d, mn, vn = _adamw_math(w_refs[nm][at], g, m_refs[nm][at], v_refs[nm][at])
            for ref, val in zip(outs[nm], (g, d, mn, vn)):
                ref[at] = val

        for nm, pieces in SMALL_ROWS.items():
            for r, i, c0, w in pieces:
                update(nm, total[r:r + 1, 0:w], i, c0, w)
        conv_rows = total[CONV_ROW0:CONV_ROW0 + N_DEV, :]
        rowid = lax.broadcasted_iota(jnp.int32, conv_rows.shape, 0)
        mine = jnp.sum(jnp.where(rowid == dev_ref[0], conv_rows, 0.0), axis=0, keepdims=True)
        for j in range(3):
            update("conv_w", mine[:, j * CONVW_STRIDE:j * CONVW_STRIDE + CONVW_BLK], j, 0, CONVW_BLK)

    order = names + ["conv_w"]
    ins = [grads[nm] for nm in names] + [g_conv_w, loss]
    for d in (params, moments, variances):
        ins += [d[nm] for nm in order]
    vmem = pl.BlockSpec(memory_space=pltpu.VMEM)
    out_shape = [jax.ShapeDtypeStruct((N_DEV, SLAB_ROWS, SLAB_W), F32), jax.ShapeDtypeStruct((1, 1), F32)]
    for nm in order:
        out_shape += [jax.ShapeDtypeStruct(params[nm].shape, F32)] * 4
    res = pl.pallas_call(
        body, name=name,
        grid_spec=pltpu.PrefetchScalarGridSpec(
            num_scalar_prefetch=1, grid=(1,),
            in_specs=[vmem] * len(ins), out_specs=[vmem] * len(out_shape),
            scratch_shapes=[pltpu.VMEM((SLAB_ROWS, SLAB_W), F32), pltpu.VMEM((SLAB_ROWS, SLAB_W), F32)] + GATHER_SEMS),
        out_shape=out_shape,
    )(dev, *ins)
    return res[1], {nm: tuple(res[2 + 4 * i:6 + 4 * i]) for i, nm in enumerate(order)}


HBM_SPEC = pl.BlockSpec(memory_space=pltpu.HBM)
SEM_SPEC = pl.BlockSpec(memory_space=pltpu.SEMAPHORE)
DATAFLOW_EFFECT = pltpu.SideEffectType.DATAFLOW_SIDE_EFFECTING
N_PEERS = N_DEV - 1


def _peers(x, y, c):
    return [(1 - x if r & 4 else x, 1 - y if r & 2 else y, 1 - c if r & 1 else c) for r in range(1, N_DEV)]


def _exchange_start(srcs, scatter, *, after=None, name):
    n = len(srcs)
    lands = [lax.empty(a.shape if scatter else (N_DEV,) + a.shape, a.dtype) for a in srcs]
    extra = [] if after is None else [after]

    def body(*refs):
        src_refs, land_refs = refs[:n], refs[n:2 * n]
        send_sems, recv_sems, token = refs[2 * n + len(extra)], refs[2 * n + len(extra) + 1], refs[-1]
        x, y, c = _my_place()
        me = 4 * x + 2 * y + c
        for i in range(n):
            for r, (tx, ty, tc) in enumerate(_peers(x, y, c)):
                src = src_refs[i].at[4 * tx + 2 * ty + tc] if scatter else src_refs[i]
                pltpu.make_async_remote_copy(
                    src_ref=src, dst_ref=land_refs[i].at[me], send_sem=send_sems.at[N_PEERS * i + r],
                    recv_sem=recv_sems.at[N_PEERS * i + r], device_id=(tx, ty, tc), device_id_type=MESH).start()
        token[...] = jnp.zeros_like(token)

    thru = [pltpu.HBM(a.shape, a.dtype) for a in list(srcs) + lands]
    res = pl.pallas_call(
        body, name=name,
        out_shape=(pltpu.SemaphoreType.DMA((N_PEERS * n,)), pltpu.SemaphoreType.DMA((N_PEERS * n,)), *thru,
                   jax.ShapeDtypeStruct((8, 128), F32)),
        in_specs=[HBM_SPEC] * (2 * n) + [ANY] * len(extra),
        out_specs=(SEM_SPEC, SEM_SPEC, *([HBM_SPEC] * (2 * n)), pl.BlockSpec(memory_space=pltpu.VMEM)),
        input_output_aliases={i: 2 + i for i in range(2 * n)},
        compiler_params=pltpu.CompilerParams(has_side_effects=DATAFLOW_EFFECT),
    )(*[pltpu.with_memory_space_constraint(a, pltpu.HBM) for a in list(srcs) + lands], *extra)
    return (res[0], res[1], list(res[2:2 + n]), list(res[2 + n:2 + 2 * n]), scatter), res[-1]


def _exchange_wait(handle, after, *, name):
    send_sems, recv_sems, srcs, lands, scatter = handle
    n = len(srcs)

    def body(*refs):
        src_refs, land_refs = refs[:n], refs[n:2 * n]
        send_sems, recv_sems = refs[2 * n], refs[2 * n + 1]
        x, y, c = _my_place()
        for i in range(n):
            for r in range(N_PEERS):
                src = src_refs[i].at[0] if scatter else src_refs[i]
                cp = pltpu.make_async_remote_copy(
                    src_ref=src, dst_ref=land_refs[i].at[0], send_sem=send_sems.at[N_PEERS * i + r],
                    recv_sem=recv_sems.at[N_PEERS * i + r], device_id=(x, y, c), device_id_type=MESH)
                cp.wait_send()
                cp.wait_recv()

    thru = [pltpu.HBM(a.shape, a.dtype) for a in srcs + lands]
    res = pl.pallas_call(
        body, name=name, out_shape=tuple(thru),
        in_specs=[HBM_SPEC] * (2 * n) + [SEM_SPEC, SEM_SPEC, ANY], out_specs=tuple([HBM_SPEC] * (2 * n)),
        input_output_aliases={i: i for i in range(2 * n)},
        compiler_params=pltpu.CompilerParams(has_side_effects=DATAFLOW_EFFECT),
    )(*srcs, *lands, send_sems, recv_sems, after)
    return list(res[:n]), list(res[n:])


def _with_own(land, own, me):
    return lax.dynamic_update_index_in_dim(land, own, me, 0)


def _adamw_math(w, g, m, v):
    m = ADAM_B1 * m + (1.0 - ADAM_B1) * g
    v = ADAM_B2 * v + (1.0 - ADAM_B2) * (g * g)
    m_hat = m / (1.0 - ADAM_B1 ** ADAM_STEP)
    v_hat = v / (1.0 - ADAM_B2 ** ADAM_STEP)
    delta = -ADAM_LR * (m_hat / (jnp.sqrt(v_hat) + ADAM_EPS) + ADAM_WD * w)
    return delta, m, v


def _adamw_sum(parts, w, m, v, *, name):
    shape = w.shape
    R, n = shape[-2], shape[-1]
    w, m, v = (t.reshape(R, n) for t in (w, m, v))
    tr = _pick(R, (256, 176, 128))

    def body(p_ref, w_ref, m_ref, v_ref, g_ref, d_ref, mo_ref, vo_ref):
        g = p_ref[0].astype(F32)
        for p in range(1, N_DEV):
            g = g + p_ref[p].astype(F32)
        d, mn, vn = _adamw_math(w_ref[...], g, m_ref[...], v_ref[...])
        g_ref[...] = g
        d_ref[...] = d
        mo_ref[...] = mn
        vo_ref[...] = vn

    row = pl.BlockSpec((tr, n), lambda i: (i, 0))
    outs = pl.pallas_call(
        body, name=name, grid=(R // tr,),
        in_specs=[pl.BlockSpec((N_DEV, tr, n), lambda i: (0, i, 0)), row, row, row],
        out_specs=[row, row, row, row],
        out_shape=[jax.ShapeDtypeStruct((R, n), F32)] * 4,
        compiler_params=_params("parallel"),
    )(parts, w, m, v)
    return [t.reshape(shape) for t in outs]


def _lb_bwd(dlb, lb, *, name):
    def body(d_ref, lb_ref, o_ref):
        t = d_ref[...] * lb_ref[...] * (1.0 - lb_ref[...])
        o_ref[0:1, :] = t
        o_ref[1:2, :] = -t

    return pl.pallas_call(body, name=name, out_shape=jax.ShapeDtypeStruct((2, lb.shape[1]), F32))(dlb, lb)


DOWN_BLK, ROW_BLK = D_FF // N_DEV, D_MODEL // N_DEV
W_FFN_BLK = 2 * D_FF // N_DEV
CONV_BITS_SHAPE = (16, 256)


def kernel(x, positions, norm1_g, w_in, lb_logits, hgrn_norm_g, w_a, attn_sinks, w_b, w_out, norm2_g, w_ffn_in, conv_w, conv_b, w_down, final_g, loss_target, m_norm1_g, m_w_in, m_lb_logits, m_hgrn_norm_g, m_w_a, m_attn_sinks, m_w_b, m_w_out, m_norm2_g, m_w_ffn_in, m_conv_w, m_conv_b, m_w_down, m_final_g, v_norm1_g, v_w_in, v_lb_logits, v_hgrn_norm_g, v_w_a, v_attn_sinks, v_w_b, v_w_out, v_norm2_g, v_w_ffn_in, v_conv_w, v_conv_b, v_w_down, v_final_g):
    xi, yi, ci = _my_place()
    dev = 4 * xi + 2 * yi + ci

    w_in_blocks = _all_gather(w_in[0].astype(BF16), name="ag_w_in")
    conv_bits = lax.bitcast_convert_type(conv_w, BF16).reshape(-1)
    conv_bits = jnp.pad(conv_bits, (0, CONV_BITS_SHAPE[0] * CONV_BITS_SHAPE[1] - conv_bits.shape[0])).reshape(CONV_BITS_SHAPE)
    gather_handles = {}
    gather_handles["mix"], tok_mix = _exchange_start([w_a[0].astype(BF16), w_b[0].astype(BF16), w_out[0].astype(BF16)], False,
                                                     after=w_in_blocks, name="ag_mix_start")
    gather_handles["ffn"], tok_ffn = _exchange_start([w_ffn_in[0].astype(BF16), w_down[0].astype(BF16), conv_bits], False,
                                                     after=tok_mix, name="ag_ffn_start")
    start_token = tok_mix + tok_ffn

    def rest_weights(group, after):
        own, lands = _exchange_wait(gather_handles[group], after, name="ag_" + group + "_wait")
        full = [_with_own(l, o, dev) for l, o in zip(lands, own)]
        if group == "mix":
            return dict(zip(("w_a", "w_b", "w_out"), [t.reshape(D_MODEL, D_MODEL) for t in full]))
        bits = full[2].reshape(N_DEV, -1)[:, :3 * CONVW_BLK * 2].reshape(N_DEV, 3, CONVW_BLK, 2)
        return dict(w_ffn=_cols_from_blocks(full[0], _plain_pieces(W_FFN_BLK), name="w_ffn_layout"), w_down=full[1].reshape(D_FF, D_MODEL),
                    conv_w=lax.bitcast_convert_type(bits, F32).transpose(1, 0, 2).reshape(3, D_FF))

    handles = {}

    def emit(group, gr):
        if group == "ffn":
            srcs = [_blocks_from_cols(gr["w_ffn"], _plain_pieces(W_FFN_BLK), name="dw_ffn_blocks"), gr["w_down"].reshape(N_DEV, DOWN_BLK, D_MODEL)]
        elif group == "mix":
            srcs = [gr[n].reshape(N_DEV, ROW_BLK, D_MODEL) for n in ("w_out", "w_a", "w_b")]
        else:
            srcs = [_blocks_from_cols(gr["w_in"], _w_in_pieces(), name="dw_in_blocks")]
        handles[group], token = _exchange_start(srcs, True, name="rs_" + group + "_start")
        return token

    small = dict(norm1_g=norm1_g, lb_logits=lb_logits, hgrn_norm_g=hgrn_norm_g, attn_sinks=attn_sinks, norm2_g=norm2_g,
                 conv_b=conv_b, final_g=final_g)
    w_in_full = _cols_from_blocks(w_in_blocks, _w_in_pieces(), name="w_in_layout")
    loss, grad_x, g = _local_step(x, positions, loss_target, small, w_in_full, rest_weights, emit, start_token)

    def parts_of(group, after):
        srcs, lands = _exchange_wait(handles[group], after, name="rs_" + group + "_wait")
        return [_with_own(l, lax.dynamic_index_in_dim(s, dev, 0, keepdims=False), dev) for s, l in zip(srcs, lands)]

    p_ffn, p_down = parts_of("ffn", grad_x)
    p_out, p_a, p_b = parts_of("mix", grad_x)
    (p_in,) = parts_of("in", grad_x)
    big = dict(
        w_in=_adamw_sum(p_in, w_in, m_w_in, v_w_in, name="adamw_w_in"),
        w_a=_adamw_sum(p_a, w_a, m_w_a, v_w_a, name="adamw_w_a"),
        w_b=_adamw_sum(p_b, w_b, m_w_b, v_w_b, name="adamw_w_b"),
        w_out=_adamw_sum(p_out, w_out, m_w_out, v_w_out, name="adamw_w_out"),
        w_ffn_in=_adamw_sum(p_ffn, w_ffn_in, m_w_ffn_in, v_w_ffn_in, name="adamw_w_ffn_in"),
        w_down=_adamw_sum(p_down, w_down, m_w_down, v_w_down, name="adamw_w_down"),
    )

    row = lambda t: t.reshape(1, -1) if t.ndim == 1 else t
    shard = lambda t: t.reshape(3, CONVW_BLK)
    sm_g = {nm: g[nm] for nm in SMALL_ROWS}
    sm_w = dict(norm1_g=norm1_g, lb_logits=lb_logits, hgrn_norm_g=hgrn_norm_g, attn_sinks=attn_sinks, norm2_g=norm2_g,
                conv_b=conv_b, final_g=row(final_g), conv_w=shard(conv_w))
    sm_m = dict(norm1_g=m_norm1_g, lb_logits=m_lb_logits, hgrn_norm_g=m_hgrn_norm_g, attn_sinks=m_attn_sinks, norm2_g=m_norm2_g,
                conv_b=m_conv_b, final_g=row(m_final_g), conv_w=shard(m_conv_w))
    sm_v = dict(norm1_g=v_norm1_g, lb_logits=v_lb_logits, hgrn_norm_g=v_hgrn_norm_g, attn_sinks=v_attn_sinks, norm2_g=v_norm2_g,
                conv_b=v_conv_b, final_g=row(v_final_g), conv_w=shard(v_conv_w))
    loss_total, sm_out = _small_step(sm_g, g["conv_w"], loss, sm_w, sm_m, sm_v, dev.astype(jnp.int32).reshape(1), name="small_step")
    shapes = dict(final_g=final_g.shape, conv_w=conv_w.shape)

    names = ("norm1_g", "w_in", "lb_logits", "hgrn_norm_g", "w_a", "attn_sinks", "w_b", "w_out", "norm2_g", "w_ffn_in", "conv_w", "conv_b", "w_down", "final_g")
    outs = [loss_total.reshape(()), grad_x]
    for kind in range(4):
        outs += [big[n][kind] if n in big else sm_out[n][kind].reshape(shapes.get(n, sm_out[n][kind].shape)) for n in names]
    return tuple(outs)
```

```python
import functools

import jax
import jax.numpy as jnp
from jax import lax
from jax.experimental import pallas as pl
from jax.experimental.pallas import tpu as pltpu

F32 = jnp.float32
BF16 = jnp.bfloat16

D_MODEL = 1024
HGRN_HEADS = 8
HGRN_DK = 128
CHUNK = 64
ATT_HEADS = 16
ATT_KV_HEADS = 2
ATT_HD = 64
ATT_GROUP = ATT_HEADS // ATT_KV_HEADS
WINDOW = 128
ROPE_DIM = ATT_HD // 4
ROPE_THETA = 500000.0
D_FF = 2816
EPS = 1e-6
NEG_INF = -1e30
N_DEV = 8

ADAM_LR = 0.001
ADAM_B1 = 0.9
ADAM_B2 = 0.999
ADAM_EPS = 1e-08
ADAM_WD = 0.01
ADAM_STEP = 10

MESH = pl.DeviceIdType.MESH
ANY = pl.BlockSpec(memory_space=pl.ANY)


def _pick(n, cands):
    for c in cands:
        if n % c == 0:
            return c
    return n


def _sigmoid(x):
    return 0.5 * jnp.tanh(0.5 * x) + 0.5


def _silu(x):
    hx = 0.5 * x
    return hx * jnp.tanh(hx) + hx


def _rms(x, g):
    return x * lax.rsqrt(jnp.mean(x * x, axis=-1, keepdims=True) + EPS) * g


def _dot(a, b, dims):
    return lax.dot_general(a, b, (dims, ((), ())), preferred_element_type=F32)


def _nn(a, b):
    return _dot(a, b, ((1,), (0,)))


def _nt(a, b):
    return _dot(a, b, ((1,), (1,)))


def _tn(a, b):
    return _dot(a, b, ((0,), (0,)))


def _params(*sem):
    return pltpu.CompilerParams(dimension_semantics=sem, vmem_limit_bytes=56 * 1024 * 1024)


def _matmul(a, b, *, ta=False, tb=False, out_dtype=F32, addend=None, after=None, into=None, o_noff=0, out_t=False, name, tm, tn, tk=None,
            n_extent=None, b_koff=0, b_noff=0):
    M, K = (a.shape[1], a.shape[0]) if ta else a.shape
    N = n_extent or (b.shape[0] if tb else b.shape[1])
    tm, tn, tk = min(tm, M), min(tn, N), min(tk or K, K)
    assert M % tm == 0 and N % tn == 0 and K % tk == 0, (name, M, N, K, tm, tn, tk)
    nk = K // tk
    use_scratch = nk > 1 and out_dtype != F32
    grid = (M // tm, N // tn, nk)
    a_spec = pl.BlockSpec((tk, tm), lambda i, j, k: (k, i)) if ta else pl.BlockSpec((tm, tk), lambda i, j, k: (i, k))
    b_spec = pl.BlockSpec((tn, tk), lambda i, j, k: (j + b_noff, k + b_koff)) if tb else pl.BlockSpec((tk, tn), lambda i, j, k: (k + b_koff, j + b_noff))
    o_spec = pl.BlockSpec((tm, tn), lambda i, j, k: (i, j))
    dims = ((0 if ta else 1,), (1 if tb else 0,))
    has_add = addend is not None

    n_in = 2 + has_add + (after is not None) + (into is not None)

    def body(*refs):
        a_ref, b_ref = refs[:2]
        c_ref = refs[2] if has_add else None
        o_ref = refs[n_in]
        part = _dot(a_ref[...], b_ref[...], dims)
        if nk == 1:
            if has_add:
                part = part + c_ref[...].astype(F32)
            o_ref[...] = (part.T if out_t else part).astype(out_dtype)
        else:
            acc_ref = refs[-1] if use_scratch else o_ref
            k = pl.program_id(2)

            @pl.when(k == 0)
            def _():
                acc_ref[...] = part + c_ref[...].astype(F32) if has_add else part

            @pl.when(k > 0)
            def _():
                acc_ref[...] += part

            if use_scratch:
                @pl.when(k == nk - 1)
                def _():
                    o_ref[...] = acc_ref[...].astype(out_dtype)

    in_specs = [a_spec, b_spec] + ([o_spec] if has_add else [])
    args = (a, b) + ((addend,) if has_add else ())
    if after is not None:
        in_specs.append(pl.BlockSpec(after.shape, lambda i, j, k: (0, 0)))
        args += (after,)
    aliases = {}
    if into is not None:
        in_specs.append(ANY)
        args += (into,)
        aliases = {len(args) - 1: 0}
    if out_t:
        assert nk == 1 and not has_add
        o_spec = pl.BlockSpec((tn, tm), lambda i, j, k: (j + o_noff, i))
    elif into is not None:
        o_spec = pl.BlockSpec((tm, tn), lambda i, j, k: (i, j + o_noff))
    return pl.pallas_call(
        body,
        name=name,
        grid=grid,
        in_specs=in_specs,
        out_specs=o_spec,
        out_shape=jax.ShapeDtypeStruct(into.shape if into is not None else ((N, M) if out_t else (M, N)), out_dtype),
        input_output_aliases=aliases,
        scratch_shapes=[pltpu.VMEM((tm, tn), F32)] if use_scratch else [],
        compiler_params=_params("parallel", "parallel", "arbitrary"),
    )(*args)


def _row_spec(tm, n):
    return pl.BlockSpec((tm, n), lambda i: (i, 0))


def _full_spec(shape):
    return pl.BlockSpec(shape, lambda i: tuple(0 for _ in shape))


def _norm_cast(x, g, *, name):
    T, D = x.shape
    tm = _pick(T, (512, 256, 128))

    def body(x_ref, g_ref, u_ref):
        u_ref[...] = _rms(x_ref[...], g_ref[...]).astype(BF16)

    return pl.pallas_call(
        body, name=name, grid=(T // tm,),
        in_specs=[_row_spec(tm, D), _full_spec((1, D))],
        out_specs=_row_spec(tm, D),
        out_shape=jax.ShapeDtypeStruct((T, D), BF16),
        compiler_params=_params("parallel"),
    )(x, g)


def _norm_bwd_add(x, g, du, dres, *, with_bf16=True, name):
    T, D = x.shape
    tm = _pick(T, (512, 256, 128))

    def body(x_ref, g_ref, du_ref, dr_ref, dx_ref, *rest):
        dg_ref = rest[-1]
        _, vjp = jax.vjp(_rms, x_ref[...], g_ref[...])
        dx, dg = vjp(du_ref[...].astype(F32))
        dx = dx + dr_ref[...]
        dx_ref[...] = dx
        if with_bf16:
            rest[0][...] = dx.astype(BF16)

        @pl.when(pl.program_id(0) == 0)
        def _():
            dg_ref[...] = jnp.zeros_like(dg_ref)

        dg_ref[...] += dg

    row = _row_spec(tm, D)
    return pl.pallas_call(
        body, name=name, grid=(T // tm,),
        in_specs=[row, _full_spec((1, D)), row, row],
        out_specs=[row] + ([row] if with_bf16 else []) + [_full_spec((1, D))],
        out_shape=[jax.ShapeDtypeStruct((T, D), F32)] + ([jax.ShapeDtypeStruct((T, D), BF16)] if with_bf16 else []) + [jax.ShapeDtypeStruct((1, D), F32)],
        compiler_params=_params("arbitrary"),
    )(x, g, du, dres)


def _final_loss_bwd(h2, g, target, *, name):
    T, D = h2.shape
    tm = _pick(T, (512, 256, 128))

    def body(h_ref, g_ref, t_ref, dx_ref, dxb_ref, dg_ref, loss_ref):
        y, vjp = jax.vjp(_rms, h_ref[...], g_ref[...])
        err = y - t_ref[...]
        dx, dg = vjp(err * (1.0 / D))
        dx_ref[...] = dx
        dxb_ref[...] = dx.astype(BF16)

        @pl.when(pl.program_id(0) == 0)
        def _():
            dg_ref[...] = jnp.zeros_like(dg_ref)
            loss_ref[...] = jnp.zeros_like(loss_ref)

        dg_ref[...] += dg
        loss_ref[...] += (0.5 / D) * jnp.sum(jnp.sum(err * err, axis=1, keepdims=True), axis=0, keepdims=True)

    return pl.pallas_call(
        body, name=name, grid=(T // tm,),
        in_specs=[_row_spec(tm, D), _full_spec((1, D)), _row_spec(tm, D)],
        out_specs=[_row_spec(tm, D), _row_spec(tm, D), _full_spec((1, D)), _full_spec((1, 1))],
        out_shape=[jax.ShapeDtypeStruct((T, D), F32), jax.ShapeDtypeStruct((T, D), BF16), jax.ShapeDtypeStruct((1, D), F32), jax.ShapeDtypeStruct((1, 1), F32)],
        compiler_params=_params("arbitrary"),
    )(h2, g, target)


def _merge_fn(gates, a, b):
    ga = gates[:, :D_MODEL].astype(F32)
    gb = gates[:, D_MODEL:].astype(F32)
    return _sigmoid(ga) * a.astype(F32) + _sigmoid(gb) * b.astype(F32)


def _gates_spec(tm):
    return pl.BlockSpec((tm, W_GATES), lambda i: (i, O_GATES // W_GATES))


def _merge_fwd(z, a, b, *, name):
    T = a.shape[0]
    tm = _pick(T, (512, 256, 128))

    def body(g_ref, a_ref, b_ref, o_ref):
        o_ref[...] = _merge_fn(g_ref[...], a_ref[...], b_ref[...]).astype(BF16)

    return pl.pallas_call(
        body, name=name, grid=(T // tm,),
        in_specs=[_gates_spec(tm), _row_spec(tm, D_MODEL), _row_spec(tm, D_MODEL)],
        out_specs=_row_spec(tm, D_MODEL),
        out_shape=jax.ShapeDtypeStruct((T, D_MODEL), BF16),
        compiler_params=_params("parallel"),
    )(z, a, b)


def _merge_bwd(z, a, b, dmerged, dz, *, name):
    T = a.shape[0]
    tm = _pick(T, (512, 256, 128))

    def body(g_ref, a_ref, b_ref, dm_ref, dz_in, dg_ref, da_ref, db_ref):
        g = g_ref[...].astype(F32)
        dm = dm_ref[...].astype(F32)
        sa = _sigmoid(g[:, :D_MODEL])
        sb = _sigmoid(g[:, D_MODEL:])
        da_ref[...] = (dm * sa).astype(BF16)
        db_ref[...] = (dm * sb).astype(BF16)
        dg_ref[:, :D_MODEL] = (dm * a_ref[...].astype(F32) * sa * (1.0 - sa)).astype(BF16)
        dg_ref[:, D_MODEL:] = (dm * b_ref[...].astype(F32) * sb * (1.0 - sb)).astype(BF16)

    return pl.pallas_call(
        body, name=name, grid=(T // tm,),
        in_specs=[_gates_spec(tm), _row_spec(tm, D_MODEL), _row_spec(tm, D_MODEL), _row_spec(tm, D_MODEL), ANY],
        out_specs=[_gates_spec(tm), _row_spec(tm, D_MODEL), _row_spec(tm, D_MODEL)],
        out_shape=[jax.ShapeDtypeStruct(dz.shape, BF16), jax.ShapeDtypeStruct((T, D_MODEL), BF16), jax.ShapeDtypeStruct((T, D_MODEL), BF16)],
        input_output_aliases={4: 0},
        compiler_params=_params("parallel"),
    )(z, a, b, dmerged, dz)


CONV_TC = 256


def _shift_down(x, n, rows):
    return jnp.where(rows >= n, pltpu.roll(x, n, 0), 0.0)


def _shift_up(x, n, rows, S):
    return jnp.where(rows < S - n, pltpu.roll(x, S - n, 0), 0.0)


def _conv_act_fwd(gu, conv_w, conv_b, *, name):
    B, S, _ = gu.shape
    tc = CONV_TC
    nc = D_FF // tc

    def body(g_ref, up_ref, w_ref, b_ref, o_ref, a_ref):
        g = g_ref[...].astype(F32)
        rows = lax.broadcasted_iota(jnp.int32, g.shape, 0)
        w = w_ref[...]
        a = w[2:3] * g + w[1:2] * _shift_down(g, 1, rows) + w[0:1] * _shift_down(g, 2, rows) + b_ref[...]
        o_ref[...] = (_silu(a) * up_ref[...].astype(F32)).astype(BF16)
        a_ref[...] = a.astype(BF16)

    col = pl.BlockSpec((None, S, tc), lambda b, j: (b, 0, j))
    return pl.pallas_call(
        body, name=name, grid=(B, nc),
        in_specs=[col,
                  pl.BlockSpec((None, S, tc), lambda b, j: (b, 0, j + nc)),
                  pl.BlockSpec((3, tc), lambda b, j: (0, j)),
                  pl.BlockSpec((1, tc), lambda b, j: (0, j))],
        out_specs=[col, col],
        out_shape=[jax.ShapeDtypeStruct((B, S, D_FF), BF16)] * 2,
        compiler_params=_params("parallel", "parallel"),
    )(gu, gu, conv_w, conv_b)


def _conv_act_bwd(gu, a_pre, conv_w, dact, *, name):
    B, S, _ = gu.shape
    tc = CONV_TC
    nc = D_FF // tc

    def body(g_ref, up_ref, a_ref, w_ref, da_ref, dg_ref, dup_ref, dw_ref, db_ref):
        g = g_ref[...].astype(F32)
        up = up_ref[...].astype(F32)
        a = a_ref[...].astype(F32)
        dact = da_ref[...].astype(F32)
        rows = lax.broadcasted_iota(jnp.int32, g.shape, 0)
        w = w_ref[...]
        sg = _sigmoid(a)
        dup_ref[...] = (dact * a * sg).astype(BF16)
        da = dact * up * sg * (1.0 + a * (1.0 - sg))
        da1 = _shift_up(da, 1, rows, S)
        da2 = _shift_up(da, 2, rows, S)
        dg_ref[...] = (w[2:3] * da + w[1:2] * da1 + w[0:1] * da2).astype(BF16)

        @pl.when(pl.program_id(1) == 0)
        def _():
            dw_ref[...] = jnp.zeros_like(dw_ref)
            db_ref[...] = jnp.zeros_like(db_ref)

        dw_ref[0:1, :] += jnp.sum(da2 * g, axis=0, keepdims=True)
        dw_ref[1:2, :] += jnp.sum(da1 * g, axis=0, keepdims=True)
        dw_ref[2:3, :] += jnp.sum(da * g, axis=0, keepdims=True)
        db_ref[...] += jnp.sum(da, axis=0, keepdims=True)

    col = pl.BlockSpec((None, S, tc), lambda j, b: (b, 0, j))
    return pl.pallas_call(
        body, name=name, grid=(nc, B),
        in_specs=[col,
                  pl.BlockSpec((None, S, tc), lambda j, b: (b, 0, j + nc)),
                  col,
                  pl.BlockSpec((3, tc), lambda j, b: (0, j)),
                  col],
        out_specs=[col, col, pl.BlockSpec((3, tc), lambda j, b: (0, j)), pl.BlockSpec((1, tc), lambda j, b: (0, j))],
        out_shape=[jax.ShapeDtypeStruct((B, S, D_FF), BF16), jax.ShapeDtypeStruct((B, S, D_FF), BF16),
                   jax.ShapeDtypeStruct((3, D_FF), F32), jax.ShapeDtypeStruct((1, D_FF), F32)],
        compiler_params=_params("parallel", "arbitrary"),
    )(gu, gu, a_pre, conv_w, dact)


HGRN_CPB = 4
HF = HGRN_HEADS * HGRN_DK


def _tri(n, upper=False):
    r = lax.broadcasted_iota(jnp.int32, (n, n), 0)
    c = lax.broadcasted_iota(jnp.int32, (n, n), 1)
    return (c >= r) if upper else (r >= c)


def _hs(h):
    return slice(h * HGRN_DK, (h + 1) * HGRN_DK)


def _cumsum_rows(tri_b, x):
    hi = x.astype(BF16)
    lo = (x - hi.astype(F32)).astype(BF16)
    return _nn(tri_b, hi) + _nn(tri_b, lo)


def _hgrn_pre(q, fz, lb, tril_b):
    qf = _silu(q)
    sg = _sigmoid(fz)
    f = lb + (1.0 - lb) * sg
    k = 1.0 - f
    b = _cumsum_rows(tril_b, jnp.log2(f))
    bref = b[CHUNK // 2:CHUNK // 2 + 1, :]
    blast = b[CHUNK - 1:CHUNK, :]
    e1 = jnp.exp2(b - bref)
    e2 = jnp.exp2(bref - b)
    e3 = e1 * jnp.exp2(bref)
    e4 = e2 * jnp.exp2(blast - bref)
    dec = jnp.exp2(blast)
    return sg, f, (e1, e2, e3, e4), qf * e1, k * e2, qf * e3, k * e4, dec


def _hgrn_fwd(zh, lb, gn, *, name):
    B, S, _ = zh.shape
    cpb = HGRN_CPB
    ts = cpb * CHUNK
    nblk = S // ts

    def body(z_ref, lb_ref, gn_ref, o_ref, st_ref, state):
        @pl.when(pl.program_id(1) == 0)
        def _():
            state[...] = jnp.zeros_like(state)

        H = HGRN_HEADS
        causal = _tri(CHUNK)
        tril_b = causal.astype(BF16)
        lb = lb_ref[...]
        for c in range(cpb):
            rows = slice(c * CHUNK, (c + 1) * CHUNK)
            q = z_ref[rows, 0:HF].astype(F32)
            fz = z_ref[rows, HF:2 * HF].astype(F32)
            v = z_ref[rows, 2 * HF:3 * HF]
            hg = z_ref[rows, 3 * HF:4 * HF].astype(F32)
            _, _, _, q_in, k_in, q_out, k_st, dec = _hgrn_pre(q, fz, lb, tril_b)
            q_in, k_in, q_out, k_st = (t.astype(BF16) for t in (q_in, k_in, q_out, k_st))
            a = [jnp.where(causal, _nt(q_in[:, _hs(h)], k_in[:, _hs(h)]), 0.0).astype(BF16) for h in range(H)]
            st = [state[h] for h in range(H)]
            for h in range(H):
                st_ref[c, h] = st[h]
            o = [_nn(a[h], v[:, _hs(h)]) + _nt(q_out[:, _hs(h)], st[h].astype(BF16)) for h in range(H)]
            for h in range(H):
                state[h] = st[h] * dec[:, _hs(h)] + _tn(v[:, _hs(h)], k_st[:, _hs(h)])
            gate = _silu(hg)
            for h in range(H):
                o_ref[rows, _hs(h)] = (_rms(o[h], gn_ref[...]) * gate[:, _hs(h)]).astype(BF16)

    return pl.pallas_call(
        body, name=name, grid=(B, nblk),
        in_specs=[pl.BlockSpec((None, ts, 4 * HF), lambda b, s: (b, s, 0)),
                  pl.BlockSpec((1, HF), lambda b, s: (0, 0)),
                  pl.BlockSpec((1, HGRN_DK), lambda b, s: (0, 0))],
        out_specs=[pl.BlockSpec((None, ts, HF), lambda b, s: (b, s, 0)),
                   pl.BlockSpec((None, cpb, HGRN_HEADS, HGRN_DK, HGRN_DK), lambda b, s: (b, s, 0, 0, 0))],
        out_shape=[jax.ShapeDtypeStruct((B, S, HF), BF16),
                   jax.ShapeDtypeStruct((B, S // CHUNK, HGRN_HEADS, HGRN_DK, HGRN_DK), F32)],
        scratch_shapes=[pltpu.VMEM((HGRN_HEADS, HGRN_DK, HGRN_DK), F32)],
        compiler_params=_params("arbitrary", "arbitrary"),
    )(zh, lb, gn)


def _hgrn_bwd(zh, lb, gn, states, doa, dz, *, name):
    B, S, _ = zh.shape
    cpb = HGRN_CPB
    ts = cpb * CHUNK
    nblk = S // ts
    rev = lambda b, s: (b, nblk - 1 - s, 0)

    def body(z_ref, lb_ref, gn_ref, st_ref, do_ref, dz_in, dz_ref, dlb_ref, dgn_ref, dstate):
        @pl.when(pl.program_id(1) == 0)
        def _():
            dstate[...] = jnp.zeros_like(dstate)

        @pl.when((pl.program_id(0) == 0) & (pl.program_id(1) == 0))
        def _():
            dlb_ref[...] = jnp.zeros_like(dlb_ref)
            dgn_ref[...] = jnp.zeros_like(dgn_ref)

        H = HGRN_HEADS
        cat = lambda xs: jnp.concatenate(xs, axis=1)
        causal = _tri(CHUNK)
        tril_b = causal.astype(BF16)
        triu_b = _tri(CHUNK, upper=True).astype(BF16)
        rowid = lax.broadcasted_iota(jnp.int32, (CHUNK, HF), 0)
        lb = lb_ref[...]
        gn = gn_ref[...]
        for c in reversed(range(cpb)):
            rows = slice(c * CHUNK, (c + 1) * CHUNK)
            q = z_ref[rows, 0:HF].astype(F32)
            fz = z_ref[rows, HF:2 * HF].astype(F32)
            v = z_ref[rows, 2 * HF:3 * HF]
            hg = z_ref[rows, 3 * HF:4 * HF].astype(F32)
            sg, f, (e1, e2, e3, e4), q_in, k_in, q_out, k_st, dec = _hgrn_pre(q, fz, lb, tril_b)
            q_in_b, k_in_b, q_out_b, k_st_b = (t.astype(BF16) for t in (q_in, k_in, q_out, k_st))
            a_b = [jnp.where(causal, _nt(q_in_b[:, _hs(h)], k_in_b[:, _hs(h)]), 0.0).astype(BF16) for h in range(H)]
            st = [st_ref[c, h] for h in range(H)]
            st_b = [t.astype(BF16) for t in st]
            o = [_nn(a_b[h], v[:, _hs(h)]) + _nt(q_out_b[:, _hs(h)], st_b[h]) for h in range(H)]
            dout = do_ref[rows, :].astype(F32)
            shg = _sigmoid(hg)
            gate = hg * shg
            do_l, dgn_acc = [], jnp.zeros_like(gn)
            for h in range(H):
                _, norm_vjp = jax.vjp(_rms, o[h], gn)
                d_o, d_gn = norm_vjp(dout[:, _hs(h)] * gate[:, _hs(h)])
                do_l.append(d_o)
                dgn_acc = dgn_acc + d_gn
            dgn_ref[...] += dgn_acc
            on = cat([_rms(o[h], gn) for h in range(H)])
            dhg = dout * on * shg * (1.0 + hg * (1.0 - shg))
            do_b = [t.astype(BF16) for t in do_l]
            dst = [dstate[h] for h in range(H)]
            dst_b = [t.astype(BF16) for t in dst]
            da_b = [jnp.where(causal, _nt(do_b[h], v[:, _hs(h)]), 0.0).astype(BF16) for h in range(H)]
            dv = cat([_tn(a_b[h], do_b[h]) + _nt(k_st_b[:, _hs(h)], dst_b[h]) for h in range(H)])
            dq_in = cat([_nn(da_b[h], k_in_b[:, _hs(h)]) for h in range(H)])
            dk_in = cat([_tn(da_b[h], q_in_b[:, _hs(h)]) for h in range(H)])
            dq_out = cat([_nn(do_b[h], st_b[h]) for h in range(H)])
            dk_st = cat([_nn(v[:, _hs(h)], dst_b[h]) for h in range(H)])
            ddec = cat([jnp.sum(st[h] * dst[h], axis=0, keepdims=True) for h in range(H)])
            for h in range(H):
                dstate[h] = dst[h] * dec[:, _hs(h)] + _tn(do_b[h], q_out_b[:, _hs(h)])
            t_qin = dq_in * q_in
            t_kin = dk_in * k_in
            t_kst = dk_st * k_st
            db = t_qin - t_kin + dq_out * q_out - t_kst
            dbref = jnp.sum(t_kin - t_qin, axis=0, keepdims=True)
            dblast = jnp.sum(t_kst, axis=0, keepdims=True) + ddec * dec
            db = db + jnp.where(rowid == CHUNK // 2, dbref, 0.0) + jnp.where(rowid == CHUNK - 1, dblast, 0.0)
            dlogf = _cumsum_rows(triu_b, db)
            dqf = dq_in * e1 + dq_out * e3
            dk = dk_in * e2 + dk_st * e4
            df = dlogf / f - dk
            dfz = df * (1.0 - lb) * sg * (1.0 - sg)
            dlb_ref[...] += jnp.sum(df * (1.0 - sg), axis=0, keepdims=True)
            sq = _sigmoid(q)
            dq = dqf * sq * (1.0 + q * (1.0 - sq))
            dz_ref[rows, 0:HF] = dq.astype(BF16)
            dz_ref[rows, HF:2 * HF] = dfz.astype(BF16)
            dz_ref[rows, 2 * HF:3 * HF] = dv.astype(BF16)
            dz_ref[rows, 3 * HF:4 * HF] = dhg.astype(BF16)

    return pl.pallas_call(
        body, name=name, grid=(B, nblk),
        in_specs=[pl.BlockSpec((None, ts, 4 * HF), rev),
                  pl.BlockSpec((1, HF), lambda b, s: (0, 0)),
                  pl.BlockSpec((1, HGRN_DK), lambda b, s: (0, 0)),
                  pl.BlockSpec((None, cpb, HGRN_HEADS, HGRN_DK, HGRN_DK), lambda b, s: (b, nblk - 1 - s, 0, 0, 0)),
                  pl.BlockSpec((None, ts, HF), rev),
                  ANY],
        out_specs=[pl.BlockSpec((None, ts, 4 * HF), rev),
                   pl.BlockSpec((1, HF), lambda b, s: (0, 0)),
                   pl.BlockSpec((1, HGRN_DK), lambda b, s: (0, 0))],
        out_shape=[jax.ShapeDtypeStruct(dz.shape, BF16),
                   jax.ShapeDtypeStruct((1, HF), F32),
                   jax.ShapeDtypeStruct((1, HGRN_DK), F32)],
        input_output_aliases={5: 0},
        scratch_shapes=[pltpu.VMEM((HGRN_HEADS, HGRN_DK, HGRN_DK), F32)],
        compiler_params=_params("arbitrary", "arbitrary"),
    )(zh, lb, gn, states, doa, dz)


KV_W = ATT_KV_HEADS * ATT_HD
ATT_SCALE = ATT_HD ** -0.5


def _rope(x, cos, sin, inverse=False):
    half = ROPE_DIM // 2
    outs = []
    for p in range(x.shape[1] // 128):
        xp = x[:, p * 128:(p + 1) * 128]
        lane = lax.broadcasted_iota(jnp.int32, xp.shape, 1) % ATT_HD
        sw = jnp.where(lane < half, pltpu.roll(xp, 128 - half, 1), pltpu.roll(xp, half, 1))
        outs.append(xp * cos - sw * sin if inverse else xp * cos + sw * sin)
    return outs[0] if len(outs) == 1 else jnp.concatenate(outs, axis=1)


PAIRS_PER_KV = ATT_GROUP // 2


def _swap_halves(x):
    return pltpu.roll(x, ATT_HD, 1)


def _kv_padded(t, low):
    sw = _swap_halves(t)
    zero = jnp.zeros_like(t)
    out = []
    for g in range(ATT_KV_HEADS):
        in_low, in_high = (t, sw) if g == 0 else (sw, t)
        out.append((jnp.where(low, in_low, zero).astype(BF16), jnp.where(low, zero, in_high).astype(BF16)))
    return out


def _swa_mask(first_block):
    qi = lax.broadcasted_iota(jnp.int32, (WINDOW, 2 * WINDOW), 0)
    mi = lax.broadcasted_iota(jnp.int32, (WINDOW, 2 * WINDOW), 1)
    band = (mi > qi) & (mi <= qi + WINDOW)
    return band & (jnp.logical_not(first_block) | (mi >= WINDOW))


def _swa_specs(nb):
    cur = lambda b, i: (b, i, 0)
    prev = lambda b, i: (b, jnp.maximum(i - 1, 0), 0)
    return cur, prev


def _swa_z_specs():
    q = pl.BlockSpec((None, WINDOW, W_AQ), lambda b, i: (b, i, O_AQ // W_AQ))
    kv_prev = pl.BlockSpec((None, WINDOW, W_AKV), lambda b, i: (b, jnp.maximum(i - 1, 0), O_AKV // W_AKV))
    kv_cur = pl.BlockSpec((None, WINDOW, W_AKV), lambda b, i: (b, i, O_AKV // W_AKV))
    return q, kv_prev, kv_cur


def _swa_fwd(z, cos, sin, sinks, *, name):
    B, S, _ = z.shape
    nb = S // WINDOW
    cur, prev = _swa_specs(nb)

    def body(q_ref, kvp_ref, kvc_ref, cp_ref, sp_ref, cc_ref, sc_ref, sink_ref, o_ref, lse_ref, qr_ref, kr_ref):
        cos_c, sin_c = cc_ref[...], sc_ref[...]
        q = (_rope(q_ref[...].astype(F32), cos_c, sin_c) * ATT_SCALE).astype(BF16)
        k = jnp.concatenate([_rope(kvp_ref[:, :KV_W].astype(F32), cp_ref[...], sp_ref[...]),
                             _rope(kvc_ref[:, :KV_W].astype(F32), cos_c, sin_c)], axis=0)
        qr_ref[...] = q
        kr_ref[...] = k[WINDOW:].astype(BF16)
        v = jnp.concatenate([kvp_ref[:, KV_W:], kvc_ref[:, KV_W:]], axis=0).astype(F32)
        low = lax.broadcasted_iota(jnp.int32, k.shape, 1) < ATT_HD
        kpad = _kv_padded(k, low)
        vpad = _kv_padded(v, low)
        mask = _swa_mask(pl.program_id(1) == 0)
        lses = []
        for g in range(ATT_KV_HEADS):
            pairs = range(g * PAIRS_PER_KV, (g + 1) * PAIRS_PER_KV)
            keys = [(p, e) for p in pairs for e in (0, 1)]
            qp = {p: q[:, p * 128:(p + 1) * 128] for p in pairs}
            s = {pe: jnp.where(mask, _nt(qp[pe[0]], kpad[g][pe[1]]), NEG_INF) for pe in keys}
            pr = {}
            for pe in keys:
                sink = sink_ref[0, 2 * pe[0] + pe[1]]
                m = jnp.maximum(jnp.max(s[pe], axis=1, keepdims=True), sink)
                ex = jnp.exp(s[pe] - m)
                den = jnp.sum(ex, axis=1, keepdims=True) + jnp.exp(sink - m)
                pr[pe] = (ex * (1.0 / den)).astype(BF16)
                lses.append(m + jnp.log(den))
            for p in pairs:
                o_ref[:, p * 128:(p + 1) * 128] = (_nn(pr[p, 0], vpad[g][0]) + _nn(pr[p, 1], vpad[g][1])).astype(BF16)
        lse_ref[...] = jnp.concatenate(lses, axis=1)

    tab = lambda im: pl.BlockSpec((None, WINDOW, 128), im)
    return pl.pallas_call(
        body, name=name, grid=(B, nb),
        in_specs=[*_swa_z_specs(),
                  tab(prev), tab(prev), tab(cur), tab(cur),
                  pl.BlockSpec(memory_space=pltpu.SMEM)],
        out_specs=[pl.BlockSpec((None, WINDOW, D_MODEL), cur), pl.BlockSpec((None, WINDOW, ATT_HEADS), cur),
                   pl.BlockSpec((None, WINDOW, D_MODEL), cur), pl.BlockSpec((None, WINDOW, KV_W), cur)],
        out_shape=[jax.ShapeDtypeStruct((B, S, D_MODEL), BF16), jax.ShapeDtypeStruct((B, S, ATT_HEADS), F32),
                   jax.ShapeDtypeStruct((B, S, D_MODEL), BF16), jax.ShapeDtypeStruct((B, S, KV_W), BF16)],
        compiler_params=_params("parallel", "parallel"),
    )(z, z, z, cos, sin, cos, sin, sinks)


def _swa_bwd(z, qr, kr, cos, sin, sinks, lse, dob, dz, *, name):
    B, S, _ = z.shape
    nb = S // WINDOW
    cur, prev = _swa_specs(nb)

    def body(q_ref, krp_ref, krc_ref, kvp_ref, kvc_ref, cp_ref, sp_ref, cc_ref, sc_ref, sink_ref, lse_ref, do_ref, dz_in,
             dq_ref, dkc_ref, dkp_ref, dsink_ref):
        @pl.when((pl.program_id(0) == 0) & (pl.program_id(1) == 0))
        def _():
            dsink_ref[...] = jnp.zeros_like(dsink_ref)

        cos_c, sin_c, cos_p, sin_p = cc_ref[...], sc_ref[...], cp_ref[...], sp_ref[...]
        q = q_ref[...]
        k = jnp.concatenate([krp_ref[...], krc_ref[...]], axis=0).astype(F32)
        v = jnp.concatenate([kvp_ref[:, KV_W:], kvc_ref[:, KV_W:]], axis=0).astype(F32)
        low = lax.broadcasted_iota(jnp.int32, k.shape, 1) < ATT_HD
        kpad = _kv_padded(k, low)
        vpad = _kv_padded(v, low)
        mask = _swa_mask(pl.program_id(1) == 0)
        lse = lse_ref[...]
        dq_parts, dk_sum, dv_sum, dsinks = [], [], [], []
        for g in range(ATT_KV_HEADS):
            pairs = range(g * PAIRS_PER_KV, (g + 1) * PAIRS_PER_KV)
            keys = [(p, e) for p in pairs for e in (0, 1)]
            qp = {p: q[:, p * 128:(p + 1) * 128] for p in pairs}
            dop = {p: do_ref[:, p * 128:(p + 1) * 128] for p in pairs}
            s = {pe: jnp.where(mask, _nt(qp[pe[0]], kpad[g][pe[1]]), NEG_INF) for pe in keys}
            dp = {pe: _nt(dop[pe[0]], vpad[g][pe[1]]) for pe in keys}
            pr, ds = {}, {}
            for pe in keys:
                h = 2 * pe[0] + pe[1]
                lse_h = lse[:, h:h + 1]
                pf = jnp.exp(s[pe] - lse_h)
                delta = jnp.sum(pf * dp[pe], axis=1, keepdims=True)
                ds[pe] = (pf * (dp[pe] - delta)).astype(BF16)
                pr[pe] = pf.astype(BF16)
                p_sink = jnp.exp(sink_ref[0, h] - lse_h)
                dsinks.append(-jnp.sum(p_sink * delta, axis=0, keepdims=True))
            for p in pairs:
                dq_parts.append((_nn(ds[p, 0], kpad[g][0]) + _nn(ds[p, 1], kpad[g][1])) * ATT_SCALE)
            x = [sum(_tn(ds[p, e], qp[p]) for p in pairs) for e in (0, 1)]
            y = [sum(_tn(pr[p, e], dop[p]) for p in pairs) for e in (0, 1)]
            zk = jnp.where(low, x[0], x[1])
            zv = jnp.where(low, y[0], y[1])
            dk_sum.append(zk + _swap_halves(zk))
            dv_sum.append(zv + _swap_halves(zv))
        dq_ref[...] = _rope(jnp.concatenate(dq_parts, axis=1), cos_c, sin_c, inverse=True).astype(BF16)
        dk = jnp.where(low, dk_sum[0], dk_sum[1])
        dv = jnp.where(low, dv_sum[0], dv_sum[1])
        dkp_ref[:, :KV_W] = _rope(dk[:WINDOW], cos_p, sin_p, inverse=True)
        dkp_ref[:, KV_W:] = dv[:WINDOW]
        dkc_ref[:, :KV_W] = _rope(dk[WINDOW:], cos_c, sin_c, inverse=True)
        dkc_ref[:, KV_W:] = dv[WINDOW:]
        dsink_ref[...] += jnp.concatenate(dsinks, axis=1)

    tab = lambda im: pl.BlockSpec((None, WINDOW, 128), im)
    return pl.pallas_call(
        body, name=name, grid=(B, nb),
        in_specs=[pl.BlockSpec((None, WINDOW, D_MODEL), cur), tab(prev), tab(cur),
                  *_swa_z_specs()[1:],
                  tab(prev), tab(prev), tab(cur), tab(cur),
                  pl.BlockSpec(memory_space=pltpu.SMEM),
                  pl.BlockSpec((None, WINDOW, ATT_HEADS), cur),
                  pl.BlockSpec((None, WINDOW, D_MODEL), cur),
                  ANY],
        out_specs=[_swa_z_specs()[0],
                   pl.BlockSpec((None, WINDOW, 2 * KV_W), cur), pl.BlockSpec((None, WINDOW, 2 * KV_W), cur),
                   pl.BlockSpec((1, ATT_HEADS), lambda b, i: (0, 0))],
        out_shape=[jax.ShapeDtypeStruct(dz.shape, BF16),
                   jax.ShapeDtypeStruct((B, S, 2 * KV_W), F32), jax.ShapeDtypeStruct((B, S, 2 * KV_W), F32),
                   jax.ShapeDtypeStruct((1, ATT_HEADS), F32)],
        input_output_aliases={12: 0},
        compiler_params=_params("arbitrary", "arbitrary"),
    )(qr, kr, kr, z, z, cos, sin, cos, sin, sinks, lse, dob, dz)


def _swa_dkv_combine(dkv_cur, dkv_prev, dz, *, name):
    B, S, W = dkv_cur.shape

    def body(c_ref, p_ref, dz_in, o_ref):
        rows = lax.broadcasted_iota(jnp.int32, (S, W), 0)
        o_ref[...] = (c_ref[...] + _shift_up(p_ref[...], WINDOW, rows, S)).astype(BF16)

    seq = pl.BlockSpec((None, S, W), lambda b: (b, 0, 0))
    return pl.pallas_call(
        body, name=name, grid=(B,),
        in_specs=[seq, seq, ANY], out_specs=pl.BlockSpec((None, S, W), lambda b: (b, 0, O_AKV // W_AKV)),
        out_shape=jax.ShapeDtypeStruct(dz.shape, BF16),
        input_output_aliases={2: 0},
        compiler_params=_params("parallel"),
    )(dkv_cur, dkv_prev, dz)


def _rope_tables(positions):
    half = ROPE_DIM // 2
    inv = ROPE_THETA ** (-2.0 * jnp.arange(half, dtype=F32) / ROPE_DIM)
    ang = positions.astype(F32)[..., None] * inv
    c, s = jnp.cos(ang), jnp.sin(ang)
    pad = jnp.zeros(ang.shape[:-1] + (ATT_HD - ROPE_DIM,), F32)
    cos = jnp.concatenate([c, c, pad + 1.0], axis=-1)
    sin = jnp.concatenate([-s, s, pad], axis=-1)
    return jnp.tile(cos, (1, 1, 2)), jnp.tile(sin, (1, 1, 2))


def _lower_bound(lb_logits, *, name):
    def body(l_ref, o_ref):
        l = l_ref[...]
        e = jnp.exp(l - jnp.max(l, axis=0, keepdims=True))
        o_ref[...] = e[0:1] / jnp.sum(e, axis=0, keepdims=True)

    return pl.pallas_call(body, name=name, out_shape=jax.ShapeDtypeStruct((1, lb_logits.shape[1]), F32))(lb_logits)


W_ZH, W_GATES, W_AQ, W_AKV = 4 * HF, 2 * D_MODEL, ATT_HEADS * ATT_HD, 2 * KV_W
O_ZH, O_GATES, O_AQ, O_AKV = 0, W_ZH, W_ZH + W_GATES, W_ZH + W_GATES + W_AQ
W_IN = W_ZH + W_GATES + W_AQ + W_AKV


W_IN_BLK = W_IN // N_DEV


def _reordered_rows(w_t):
    return jnp.concatenate([w_t[:W_ZH], w_t[W_ZH + W_AQ + W_AKV:], w_t[W_ZH:W_ZH + W_AQ + W_AKV]], axis=0)


def _reference_rows(w_t):
    return jnp.concatenate([w_t[:W_ZH], w_t[O_AQ:], w_t[O_GATES:O_AQ]], axis=0)


def _local_step(x, positions, target, small, w_in_t, rest_weights, emit, start_token):
    B, S, D = x.shape
    T = B * S
    x2 = x.reshape(T, D)
    cos, sin = _rope_tables(positions)
    lb = _lower_bound(small["lb_logits"], name="lb_fwd")
    zero = lambda tok: tok[0:1, 0:1]

    u1 = _norm_cast(x2, small["norm1_g"] + zero(start_token), name="norm1")
    z = _matmul(u1, w_in_t, tb=True, out_dtype=BF16, name="mm_z", tm=1024, tn=W_IN // 2)
    z3 = z.reshape(B, S, W_IN)
    oa, states = _hgrn_fwd(z3, lb, small["hgrn_norm_g"], name="hgrn_fwd")
    ob, lse, qr, kr = _swa_fwd(z3, cos, sin, small["attn_sinks"], name="swa_fwd")
    oa2 = oa.reshape(T, D)
    ob2 = ob.reshape(T, D)
    W = rest_weights("mix", ob)
    pa = _matmul(oa2, W["w_a"], out_dtype=BF16, name="mm_pa", tm=2048, tn=512)
    pb = _matmul(ob2, W["w_b"], out_dtype=BF16, name="mm_pb", tm=2048, tn=512)
    merged = _merge_fwd(z, pa, pb, name="merge_fwd")
    h = _matmul(merged, W["w_out"], addend=x2, name="mm_h", tm=2048, tn=512)
    u2 = _norm_cast(h, small["norm2_g"], name="norm2")
    W.update(rest_weights("ffn", u2))
    gu = _matmul(u2, W["w_ffn_t"], tb=True, out_dtype=BF16, name="mm_gu", tm=2048, tn=512)
    gu3 = gu.reshape(B, S, 2 * D_FF)
    act, a_pre = _conv_act_fwd(gu3, W["conv_w"], small["conv_b"], name="conv_act_fwd")
    act2 = act.reshape(T, D_FF)
    h2 = _matmul(act2, W["w_down"], addend=h, name="mm_h2", tm=1024, tn=1024)

    g = {}
    dh2, dh2b, g["final_g"], loss = _final_loss_bwd(h2, small["final_g"].reshape(1, D), target.reshape(T, D), name="final_loss_bwd")
    dact = _matmul(dh2b, W["w_down"], tb=True, out_dtype=BF16, name="mm_dact", tm=1024, tn=D_FF)
    dw_down_t = _matmul(dh2b, act2, ta=True, out_dtype=BF16, name="mm_dw_down", tm=1024, tn=256, tk=8192)
    dg_, dup, g["conv_w"], g["conv_b"] = _conv_act_bwd(gu3, a_pre, W["conv_w"], dact.reshape(B, S, D_FF), name="conv_act_bwd")
    dg2 = dg_.reshape(T, D_FF)
    dup2 = dup.reshape(T, D_FF)
    du2 = _matmul(dg2, W["w_ffn_t"], name="mm_du2_g", tm=1024, tn=1024, b_koff=0)
    du2 = _matmul(dup2, W["w_ffn_t"], addend=du2, out_dtype=BF16, name="mm_du2_u", tm=1024, tn=1024, b_koff=1)
    dw_ffn_t = _matmul(u2, dg2, ta=True, out_t=True, out_dtype=BF16, into=lax.empty((2 * D_FF, D), BF16), o_noff=0, name="mm_dw_ffn_g", tm=1024, tn=256, tk=8192)
    dw_ffn_t = _matmul(u2, dup2, ta=True, out_t=True, out_dtype=BF16, into=dw_ffn_t, o_noff=D_FF // 256, name="mm_dw_ffn_u", tm=1024, tn=256, tk=8192)
    tok = emit("ffn", dict(w_ffn_t=dw_ffn_t, w_down=dw_down_t.T))
    dh, dhb, g["norm2_g"] = _norm_bwd_add(h, small["norm2_g"] + zero(tok), du2, dh2, name="norm2_bwd")
    dmerged = _matmul(dhb, W["w_out"], tb=True, out_dtype=BF16, name="mm_dmerged", tm=2048, tn=512)
    dw_out = _matmul(merged, dhb, ta=True, out_dtype=BF16, name="mm_dw_out", tm=1024, tn=1024, tk=2048)
    dz, dpa, dpb = _merge_bwd(z, pa, pb, dmerged, lax.empty((T, W_IN), BF16), name="merge_bwd")
    doa =_matmul(dpa, W["w_a"], tb=True, out_dtype=BF16, name="mm_doa", tm=2048, tn=512)
    dw_a = _matmul(oa2, dpa, ta=True, out_dtype=BF16, name="mm_dw_a", tm=1024, tn=1024, tk=2048)
    dob = _matmul(dpb, W["w_b"], tb=True, out_dtype=BF16, name="mm_dob", tm=2048, tn=512)
    dw_b = _matmul(ob2, dpb, ta=True, out_dtype=BF16, name="mm_dw_b", tm=1024, tn=1024, tk=2048)
    tok = emit("mix", dict(w_out=dw_out, w_a=dw_a, w_b=dw_b))
    dz3, dkv_cur, dkv_prev, dsinks = _swa_bwd(z3, qr, kr, cos, sin, small["attn_sinks"] + zero(tok), lse, dob.reshape(B, S, D),
                                              dz.reshape(B, S, W_IN), name="swa_bwd")
    dz3 = _swa_dkv_combine(dkv_cur, dkv_prev, dz3, name="swa_dkv")
    g["attn_sinks"] = dsinks
    dz3, g["lb"], g["hgrn_norm_g"] = _hgrn_bwd(z3, lb, small["hgrn_norm_g"], states, doa.reshape(B, S, D), dz3, name="hgrn_bwd")
    dz = dz3.reshape(T, W_IN)
    dw_in_t = _matmul(u1, dz, ta=True, out_t=True, out_dtype=BF16, name="mm_dw_in", tm=1024, tn=256, tk=8192)
    tok = emit("in", dict(w_in_t=dw_in_t))
    du1 = _matmul(dz, w_in_t, after=tok, out_dtype=BF16, name="mm_du1", tm=1024, tn=512)
    dx, g["norm1_g"] = _norm_bwd_add(x2, small["norm1_g"], du1, dh, with_bf16=False, name="norm1_bwd")
    g["lb_logits"] = _lb_bwd(g.pop("lb"), lb, name="lb_bwd")
    return loss, dx.reshape(B, S, D), g


def _my_place():
    return lax.axis_index("x"), lax.axis_index("y"), lax.axis_index("c")


def _gather_blocks(x_ref, out_ref, send_sems, recv_sems, local_sem):
    x, y, c = _my_place()
    me, sibling = (x, y, c), (x, y, 1 - c)
    chips = [(1 - x, y), (x, 1 - y), (1 - x, 1 - y)]

    def slot(px, py, pc):
        return out_ref.at[4 * px + 2 * py + pc]

    def copy(k, block, to, src=None):
        return pltpu.make_async_remote_copy(
            src_ref=slot(*block) if src is None else src, dst_ref=slot(*block),
            send_sem=send_sems.at[k], recv_sem=recv_sems.at[k], device_id=to, device_id_type=MESH)

    mine = pltpu.make_async_copy(x_ref, slot(*me), local_sem)
    mine.start()
    first = [copy(0, me, sibling, src=x_ref)]
    first += [copy(1 + j, me, (*chip, c), src=x_ref) for j, chip in enumerate(chips)]
    for cp in first:
        cp.start()
    passed = [copy(4 + j, (*chip, c), sibling) for j, chip in enumerate(chips)]
    for j, chip in enumerate(chips):
        copy(1 + j, (*chip, c), me).wait_recv()
        passed[j].start()
    copy(0, sibling, me).wait_recv()
    for j, chip in enumerate(chips):
        copy(4 + j, (*chip, 1 - c), me).wait_recv()
    for cp in first + passed:
        cp.wait_send()
    mine.wait()


GATHER_SEMS = [pltpu.SemaphoreType.DMA((7,)), pltpu.SemaphoreType.DMA((7,)), pltpu.SemaphoreType.DMA]


def _all_gather(blk, *, name):
    return pl.pallas_call(
        _gather_body_fn(), name=name,
        out_shape=jax.ShapeDtypeStruct((N_DEV,) + blk.shape, blk.dtype),
        in_specs=[ANY], out_specs=ANY,
        scratch_shapes=GATHER_SEMS,
    )(blk)


def _gather_body_fn():
    def body(x_ref, out_ref, send_sems, recv_sems, local_sem):
        _gather_blocks(x_ref, out_ref, send_sems, recv_sems, local_sem)
    return body


SLAB_W = 1152
SMALL_SHAPES = dict(norm1_g=(1, D_MODEL), lb_logits=(2, HGRN_HEADS * HGRN_DK), hgrn_norm_g=(1, HGRN_DK), attn_sinks=(1, ATT_HEADS),
                    norm2_g=(1, D_MODEL), conv_b=(1, D_FF), final_g=(1, D_MODEL))
CONVW_BLK = D_FF // N_DEV
CONVW_STRIDE = SLAB_W // 3


def _slab_layout():
    layout, r = {}, 0
    for nm, (nr, w) in SMALL_SHAPES.items():
        layout[nm] = []
        for i in range(nr):
            for c0 in range(0, w, SLAB_W):
                layout[nm].append((r, i, c0, min(SLAB_W, w - c0)))
                r += 1
    return layout, r


SMALL_ROWS, _N_SMALL_ROWS = _slab_layout()
CONV_ROW0 = -(-_N_SMALL_ROWS // 8) * 8
LOSS_ROW = CONV_ROW0 + N_DEV
SLAB_ROWS = LOSS_ROW + 8


def _small_step(grads, g_conv_w, loss, params, moments, variances, dev, *, name):
    names = list(SMALL_ROWS)
    n = len(names)

    def body(dev_ref, *refs):
        g_refs = dict(zip(names, refs[:n]))
        gc_ref, loss_ref = refs[n], refs[n + 1]
        base = n + 2
        w_refs, m_refs, v_refs = (dict(zip(names + ["conv_w"], refs[base + i * (n + 1):base + (i + 1) * (n + 1)])) for i in range(3))
        o = base + 3 * (n + 1)
        gath_ref, loss_out = refs[o], refs[o + 1]
        outs = {nm: refs[o + 2 + 4 * i:o + 6 + 4 * i] for i, nm in enumerate(names + ["conv_w"])}
        slab, total, send_sems, recv_sems, local_sem = refs[-5:]

        slab[...] = jnp.zeros_like(slab)
        for nm, pieces in SMALL_ROWS.items():
            for r, i, c0, w in pieces:
                slab[r:r + 1, 0:w] = g_refs[nm][i:i + 1, c0:c0 + w]
        for p in range(N_DEV):
            for j in range(3):
                slab[CONV_ROW0 + p:CONV_ROW0 + p + 1, j * CONVW_STRIDE:j * CONVW_STRIDE + CONVW_BLK] = gc_ref[j:j + 1, p * CONVW_BLK:(p + 1) * CONVW_BLK]
        slab[LOSS_ROW:LOSS_ROW + 1, 0:1] = loss_ref[...]
        _gather_blocks(slab, gath_ref, send_sems, recv_sems, local_sem)
        acc = gath_ref[0]
        for p in range(1, N_DEV):
            acc = acc + gath_ref[p]
        total[...] = acc
        loss_out[...] = total[LOSS_ROW:LOSS_ROW + 1, 0:1]

        def update(nm, g, i, c0, w):
            at = (slice(i, i + 1), slice(c0, c0 + w))
            d, mn, vn = _adamw_math(w_refs[nm][at], g, m_refs[nm][at], v_refs[nm][at])
            for ref, val in zip(outs[nm], (g, d, mn, vn)):
                ref[at] = val

        for nm, pieces in SMALL_ROWS.items():
            for r, i, c0, w in pieces:
                update(nm, total[r:r + 1, 0:w], i, c0, w)
        conv_rows = total[CONV_ROW0:CONV_ROW0 + N_DEV, :]
        rowid = lax.broadcasted_iota(jnp.int32, conv_rows.shape, 0)
        mine = jnp.sum(jnp.where(rowid == dev_ref[0], conv_rows, 0.0), axis=0, keepdims=True)
        for j in range(3):
            update("conv_w", mine[:, j * CONVW_STRIDE:j * CONVW_STRIDE + CONVW_BLK], j, 0, CONVW_BLK)

    order = names + ["conv_w"]
    ins = [grads[nm] for nm in names] + [g_conv_w, loss]
    for d in (params, moments, variances):
        ins += [d[nm] for nm in order]
    vmem = pl.BlockSpec(memory_space=pltpu.VMEM)
    out_shape = [jax.ShapeDtypeStruct((N_DEV, SLAB_ROWS, SLAB_W), F32), jax.ShapeDtypeStruct((1, 1), F32)]
    for nm in order:
        out_shape += [jax.ShapeDtypeStruct(params[nm].shape, F32)] * 4
    res = pl.pallas_call(
        body, name=name,
        grid_spec=pltpu.PrefetchScalarGridSpec(
            num_scalar_prefetch=1, grid=(1,),
            in_specs=[vmem] * len(ins), out_specs=[vmem] * len(out_shape),
            scratch_shapes=[pltpu.VMEM((SLAB_ROWS, SLAB_W), F32), pltpu.VMEM((SLAB_ROWS, SLAB_W), F32)] + GATHER_SEMS),
        out_shape=out_shape,
    )(dev, *ins)
    return res[1], {nm: tuple(res[2 + 4 * i:6 + 4 * i]) for i, nm in enumerate(order)}


HBM_SPEC = pl.BlockSpec(memory_space=pltpu.HBM)
SEM_SPEC = pl.BlockSpec(memory_space=pltpu.SEMAPHORE)
DATAFLOW_EFFECT = pltpu.SideEffectType.DATAFLOW_SIDE_EFFECTING
N_PEERS = N_DEV - 1


def _peers(x, y, c):
    return [(1 - x if r & 4 else x, 1 - y if r & 2 else y, 1 - c if r & 1 else c) for r in range(1, N_DEV)]


def _exchange_start(srcs, scatter, *, after=None, name):
    n = len(srcs)
    lands = [lax.empty(a.shape if scatter else (N_DEV,) + a.shape, a.dtype) for a in srcs]
    extra = [] if after is None else [after]

    def body(*refs):
        src_refs, land_refs = refs[:n], refs[n:2 * n]
        send_sems, recv_sems, token = refs[2 * n + len(extra)], refs[2 * n + len(extra) + 1], refs[-1]
        x, y, c = _my_place()
        me = 4 * x + 2 * y + c
        for i in range(n):
            for r, (tx, ty, tc) in enumerate(_peers(x, y, c)):
                src = src_refs[i].at[4 * tx + 2 * ty + tc] if scatter else src_refs[i]
                pltpu.make_async_remote_copy(
                    src_ref=src, dst_ref=land_refs[i].at[me], send_sem=send_sems.at[N_PEERS * i + r],
                    recv_sem=recv_sems.at[N_PEERS * i + r], device_id=(tx, ty, tc), device_id_type=MESH).start()
        token[...] = jnp.zeros_like(token)

    thru = [pltpu.HBM(a.shape, a.dtype) for a in list(srcs) + lands]
    res = pl.pallas_call(
        body, name=name,
        out_shape=(pltpu.SemaphoreType.DMA((N_PEERS * n,)), pltpu.SemaphoreType.DMA((N_PEERS * n,)), *thru,
                   jax.ShapeDtypeStruct((8, 128), F32)),
        in_specs=[HBM_SPEC] * (2 * n) + [ANY] * len(extra),
        out_specs=(SEM_SPEC, SEM_SPEC, *([HBM_SPEC] * (2 * n)), pl.BlockSpec(memory_space=pltpu.VMEM)),
        input_output_aliases={i: 2 + i for i in range(2 * n)},
        compiler_params=pltpu.CompilerParams(has_side_effects=DATAFLOW_EFFECT),
    )(*[pltpu.with_memory_space_constraint(a, pltpu.HBM) for a in list(srcs) + lands], *extra)
    return (res[0], res[1], list(res[2:2 + n]), list(res[2 + n:2 + 2 * n]), scatter), res[-1]


def _exchange_wait(handle, after, *, name):
    send_sems, recv_sems, srcs, lands, scatter = handle
    n = len(srcs)

    def body(*refs):
        src_refs, land_refs = refs[:n], refs[n:2 * n]
        send_sems, recv_sems = refs[2 * n], refs[2 * n + 1]
        x, y, c = _my_place()
        for i in range(n):
            for r in range(N_PEERS):
                src = src_refs[i].at[0] if scatter else src_refs[i]
                cp = pltpu.make_async_remote_copy(
                    src_ref=src, dst_ref=land_refs[i].at[0], send_sem=send_sems.at[N_PEERS * i + r],
                    recv_sem=recv_sems.at[N_PEERS * i + r], device_id=(x, y, c), device_id_type=MESH)
                cp.wait_send()
                cp.wait_recv()

    thru = [pltpu.HBM(a.shape, a.dtype) for a in srcs + lands]
    res = pl.pallas_call(
        body, name=name, out_shape=tuple(thru),
        in_specs=[HBM_SPEC] * (2 * n) + [SEM_SPEC, SEM_SPEC, ANY], out_specs=tuple([HBM_SPEC] * (2 * n)),
        input_output_aliases={i: i for i in range(2 * n)},
        compiler_params=pltpu.CompilerParams(has_side_effects=DATAFLOW_EFFECT),
    )(*srcs, *lands, send_sems, recv_sems, after)
    return list(res[:n]), list(res[n:])


def _with_own(land, own, me):
    return lax.dynamic_update_index_in_dim(land, own, me, 0)


def _adamw_math(w, g, m, v):
    m = ADAM_B1 * m + (1.0 - ADAM_B1) * g
    v = ADAM_B2 * v + (1.0 - ADAM_B2) * (g * g)
    m_hat = m / (1.0 - ADAM_B1 ** ADAM_STEP)
    v_hat = v / (1.0 - ADAM_B2 ** ADAM_STEP)
    delta = -ADAM_LR * (m_hat / (jnp.sqrt(v_hat) + ADAM_EPS) + ADAM_WD * w)
    return delta, m, v


def _adamw_sum(parts, w, m, v, *, name):
    shape = w.shape
    R, n = shape[-2], shape[-1]
    w, m, v = (t.reshape(R, n) for t in (w, m, v))
    tr = _pick(R, (256, 464, 352, 128))

    def body(p_ref, w_ref, m_ref, v_ref, g_ref, d_ref, mo_ref, vo_ref):
        g = p_ref[0].astype(F32)
        for p in range(1, N_DEV):
            g = g + p_ref[p].astype(F32)
        d, mn, vn = _adamw_math(w_ref[...], g, m_ref[...], v_ref[...])
        g_ref[...] = g
        d_ref[...] = d
        mo_ref[...] = mn
        vo_ref[...] = vn

    row = pl.BlockSpec((tr, n), lambda i: (i, 0))
    outs = pl.pallas_call(
        body, name=name, grid=(R // tr,),
        in_specs=[pl.BlockSpec((N_DEV, tr, n), lambda i: (0, i, 0)), row, row, row],
        out_specs=[row, row, row, row],
        out_shape=[jax.ShapeDtypeStruct((R, n), F32)] * 4,
        compiler_params=_params("parallel"),
    )(parts, w, m, v)
    return [t.reshape(shape) for t in outs]


def _lb_bwd(dlb, lb, *, name):
    def body(d_ref, lb_ref, o_ref):
        t = d_ref[...] * lb_ref[...] * (1.0 - lb_ref[...])
        o_ref[0:1, :] = t
        o_ref[1:2, :] = -t

    return pl.pallas_call(body, name=name, out_shape=jax.ShapeDtypeStruct((2, lb.shape[1]), F32))(dlb, lb)


DOWN_BLK, ROW_BLK = D_FF // N_DEV, D_MODEL // N_DEV
W_FFN_BLK = 2 * D_FF // N_DEV
CONV_BITS_SHAPE = (16, 256)


def kernel(x, positions, norm1_g, w_in, lb_logits, hgrn_norm_g, w_a, attn_sinks, w_b, w_out, norm2_g, w_ffn_in, conv_w, conv_b, w_down, final_g, loss_target, m_norm1_g, m_w_in, m_lb_logits, m_hgrn_norm_g, m_w_a, m_attn_sinks, m_w_b, m_w_out, m_norm2_g, m_w_ffn_in, m_conv_w, m_conv_b, m_w_down, m_final_g, v_norm1_g, v_w_in, v_lb_logits, v_hgrn_norm_g, v_w_a, v_attn_sinks, v_w_b, v_w_out, v_norm2_g, v_w_ffn_in, v_conv_w, v_conv_b, v_w_down, v_final_g):
    xi, yi, ci = _my_place()
    dev = 4 * xi + 2 * yi + ci

    tr = lambda t: jnp.transpose(t[0])
    untr = lambda t: jnp.transpose(t)[None]
    w_in_blocks = _all_gather(tr(w_in).astype(BF16), name="ag_w_in")
    conv_bits = lax.bitcast_convert_type(conv_w, BF16).reshape(-1)
    conv_bits = jnp.pad(conv_bits, (0, CONV_BITS_SHAPE[0] * CONV_BITS_SHAPE[1] - conv_bits.shape[0])).reshape(CONV_BITS_SHAPE)
    gather_handles = {}
    gather_handles["mix"], tok_mix = _exchange_start([w_a[0].astype(BF16), w_b[0].astype(BF16), w_out[0].astype(BF16)], False,
                                                     after=w_in_blocks, name="ag_mix_start")
    gather_handles["ffn"], tok_ffn = _exchange_start([tr(w_ffn_in).astype(BF16), w_down[0].astype(BF16), conv_bits], False,
                                                     after=tok_mix, name="ag_ffn_start")
    start_token = tok_mix + tok_ffn

    def rest_weights(group, after):
        own, lands = _exchange_wait(gather_handles[group], after, name="ag_" + group + "_wait")
        full = [_with_own(l, o, dev) for l, o in zip(lands, own)]
        if group == "mix":
            return dict(zip(("w_a", "w_b", "w_out"), [t.reshape(D_MODEL, D_MODEL) for t in full]))
        bits = full[2].reshape(N_DEV, -1)[:, :3 * CONVW_BLK * 2].reshape(N_DEV, 3, CONVW_BLK, 2)
        return dict(w_ffn_t=full[0].reshape(2 * D_FF, D_MODEL), w_down=full[1].reshape(D_FF, D_MODEL),
                    conv_w=lax.bitcast_convert_type(bits, F32).transpose(1, 0, 2).reshape(3, D_FF))

    handles = {}

    def emit(group, gr):
        if group == "ffn":
            srcs = [gr["w_ffn_t"].reshape(N_DEV, W_FFN_BLK, D_MODEL), gr["w_down"].reshape(N_DEV, DOWN_BLK, D_MODEL)]
        elif group == "mix":
            srcs = [gr[n].reshape(N_DEV, ROW_BLK, D_MODEL) for n in ("w_out", "w_a", "w_b")]
        else:
            srcs = [_reference_rows(gr["w_in_t"]).reshape(N_DEV, W_IN_BLK, D_MODEL)]
        handles[group], token = _exchange_start(srcs, True, name="rs_" + group + "_start")
        return token

    small = dict(norm1_g=norm1_g, lb_logits=lb_logits, hgrn_norm_g=hgrn_norm_g, attn_sinks=attn_sinks, norm2_g=norm2_g,
                 conv_b=conv_b, final_g=final_g)
    w_in_full_t = _reordered_rows(w_in_blocks.reshape(W_IN, D_MODEL))
    loss, grad_x, g = _local_step(x, positions, loss_target, small, w_in_full_t, rest_weights, emit, start_token)

    def parts_of(group, after):
        srcs, lands = _exchange_wait(handles[group], after, name="rs_" + group + "_wait")
        return [_with_own(l, lax.dynamic_index_in_dim(s, dev, 0, keepdims=False), dev) for s, l in zip(srcs, lands)]

    p_ffn, p_down = parts_of("ffn", grad_x)
    p_out, p_a, p_b = parts_of("mix", grad_x)
    (p_in,) = parts_of("in", grad_x)
    big = dict(
        w_in=[untr(t) for t in _adamw_sum(p_in, tr(w_in), tr(m_w_in), tr(v_w_in), name="adamw_w_in")],
        w_a=_adamw_sum(p_a, w_a, m_w_a, v_w_a, name="adamw_w_a"),
        w_b=_adamw_sum(p_b, w_b, m_w_b, v_w_b, name="adamw_w_b"),
        w_out=_adamw_sum(p_out, w_out, m_w_out, v_w_out, name="adamw_w_out"),
        w_ffn_in=[untr(t) for t in _adamw_sum(p_ffn, tr(w_ffn_in), tr(m_w_ffn_in), tr(v_w_ffn_in), name="adamw_w_ffn_in")],
        w_down=_adamw_sum(p_down, w_down, m_w_down, v_w_down, name="adamw_w_down"),
    )

    row = lambda t: t.reshape(1, -1) if t.ndim == 1 else t
    shard = lambda t: t.reshape(3, CONVW_BLK)
    sm_g = {nm: g[nm] for nm in SMALL_ROWS}
    sm_w = dict(norm1_g=norm1_g, lb_logits=lb_logits, hgrn_norm_g=hgrn_norm_g, attn_sinks=attn_sinks, norm2_g=norm2_g,
                conv_b=conv_b, final_g=row(final_g), conv_w=shard(conv_w))
    sm_m = dict(norm1_g=m_norm1_g, lb_logits=m_lb_logits, hgrn_norm_g=m_hgrn_norm_g, attn_sinks=m_attn_sinks, norm2_g=m_norm2_g,
                conv_b=m_conv_b, final_g=row(m_final_g), conv_w=shard(m_conv_w))
    sm_v = dict(norm1_g=v_norm1_g, lb_logits=v_lb_logits, hgrn_norm_g=v_hgrn_norm_g, attn_sinks=v_attn_sinks, norm2_g=v_norm2_g,
                conv_b=v_conv_b, final_g=row(v_final_g), conv_w=shard(v_conv_w))
    loss_total, sm_out = _small_step(sm_g, g["conv_w"], loss, sm_w, sm_m, sm_v, dev.astype(jnp.int32).reshape(1), name="small_step")
    shapes = dict(final_g=final_g.shape, conv_w=conv_w.shape)

    names = ("norm1_g", "w_in", "lb_logits", "hgrn_norm_g", "w_a", "attn_sinks", "w_b", "w_out", "norm2_g", "w_ffn_in", "conv_w", "conv_b", "w_down", "final_g")
    outs = [loss_total.reshape(()), grad_x]
    for kind in range(4):
        outs += [big[n][kind] if n in big else sm_out[n][kind].reshape(shapes.get(n, sm_out[n][kind].shape)) for n in names]
    return tuple(outs)
```

```python
import functools

import jax
import jax.numpy as jnp
from jax import lax
from jax.experimental import pallas as pl
from jax.experimental.pallas import tpu as pltpu

F32 = jnp.float32
BF16 = jnp.bfloat16

D_MODEL = 1024
HGRN_HEADS = 8
HGRN_DK = 128
CHUNK = 64
ATT_HEADS = 16
ATT_KV_HEADS = 2
ATT_HD = 64
ATT_GROUP = ATT_HEADS // ATT_KV_HEADS
WINDOW = 128
ROPE_DIM = ATT_HD // 4
ROPE_THETA = 500000.0
D_FF = 2816
EPS = 1e-6
NEG_INF = -1e30
N_DEV = 8

ADAM_LR = 0.001
ADAM_B1 = 0.9
ADAM_B2 = 0.999
ADAM_EPS = 1e-08
ADAM_WD = 0.01
ADAM_STEP = 10

MESH = pl.DeviceIdType.MESH
ANY = pl.BlockSpec(memory_space=pl.ANY)


def _pick(n, cands):
    for c in cands:
        if n % c == 0:
            return c
    return n


def _sigmoid(x):
    return 0.5 * jnp.tanh(0.5 * x) + 0.5


def _silu(x):
    hx = 0.5 * x
    return hx * jnp.tanh(hx) + hx


def _rms(x, g):
    return x * lax.rsqrt(jnp.mean(x * x, axis=-1, keepdims=True) + EPS) * g


def _dot(a, b, dims):
    return lax.dot_general(a, b, (dims, ((), ())), preferred_element_type=F32)


def _nn(a, b):
    return _dot(a, b, ((1,), (0,)))


def _nt(a, b):
    return _dot(a, b, ((1,), (1,)))


def _tn(a, b):
    return _dot(a, b, ((0,), (0,)))


def _params(*sem):
    return pltpu.CompilerParams(dimension_semantics=sem, vmem_limit_bytes=56 * 1024 * 1024)


def _matmul(a, b, *, ta=False, tb=False, out_dtype=F32, addend=None, after=None, into=None, o_noff=0, out_t=False,
            o_block_perm=lambda j: j, name, tm, tn, tk=None, n_extent=None, b_koff=0, b_noff=0):
    M, K = (a.shape[1], a.shape[0]) if ta else a.shape
    N = n_extent or (b.shape[0] if tb else b.shape[1])
    tm, tn, tk = min(tm, M), min(tn, N), min(tk or K, K)
    assert M % tm == 0 and N % tn == 0 and K % tk == 0, (name, M, N, K, tm, tn, tk)
    nk = K // tk
    use_scratch = nk > 1 and out_dtype != F32
    grid = (M // tm, N // tn, nk)
    a_spec = pl.BlockSpec((tk, tm), lambda i, j, k: (k, i)) if ta else pl.BlockSpec((tm, tk), lambda i, j, k: (i, k))
    b_spec = pl.BlockSpec((tn, tk), lambda i, j, k: (j + b_noff, k + b_koff)) if tb else pl.BlockSpec((tk, tn), lambda i, j, k: (k + b_koff, j + b_noff))
    o_spec = pl.BlockSpec((tm, tn), lambda i, j, k: (i, j))
    dims = ((0 if ta else 1,), (1 if tb else 0,))
    has_add = addend is not None

    n_in = 2 + has_add + (after is not None) + (into is not None)

    def body(*refs):
        a_ref, b_ref = refs[:2]
        c_ref = refs[2] if has_add else None
        o_ref = refs[n_in]
        part = _dot(a_ref[...], b_ref[...], dims)
        if nk == 1:
            if has_add:
                part = part + c_ref[...].astype(F32)
            o_ref[...] = (part.T if out_t else part).astype(out_dtype)
        else:
            acc_ref = refs[-1] if use_scratch else o_ref
            k = pl.program_id(2)

            @pl.when(k == 0)
            def _():
                acc_ref[...] = part + c_ref[...].astype(F32) if has_add else part

            @pl.when(k > 0)
            def _():
                acc_ref[...] += part

            if use_scratch:
                @pl.when(k == nk - 1)
                def _():
                    o_ref[...] = acc_ref[...].astype(out_dtype)

    in_specs = [a_spec, b_spec] + ([o_spec] if has_add else [])
    args = (a, b) + ((addend,) if has_add else ())
    if after is not None:
        in_specs.append(pl.BlockSpec(after.shape, lambda i, j, k: (0, 0)))
        args += (after,)
    aliases = {}
    if into is not None:
        in_specs.append(ANY)
        args += (into,)
        aliases = {len(args) - 1: 0}
    if out_t:
        assert nk == 1 and not has_add
        o_spec = pl.BlockSpec((tn, tm), lambda i, j, k: (o_block_perm(j) + o_noff, i))
    elif into is not None:
        o_spec = pl.BlockSpec((tm, tn), lambda i, j, k: (i, j + o_noff))
    return pl.pallas_call(
        body,
        name=name,
        grid=grid,
        in_specs=in_specs,
        out_specs=o_spec,
        out_shape=jax.ShapeDtypeStruct(into.shape if into is not None else ((N, M) if out_t else (M, N)), out_dtype),
        input_output_aliases=aliases,
        scratch_shapes=[pltpu.VMEM((tm, tn), F32)] if use_scratch else [],
        compiler_params=_params("parallel", "parallel", "arbitrary"),
    )(*args)


def _row_spec(tm, n):
    return pl.BlockSpec((tm, n), lambda i: (i, 0))


def _full_spec(shape):
    return pl.BlockSpec(shape, lambda i: tuple(0 for _ in shape))


def _norm_cast(x, g, *, name):
    T, D = x.shape
    tm = _pick(T, (512, 256, 128))

    def body(x_ref, g_ref, u_ref):
        u_ref[...] = _rms(x_ref[...], g_ref[...]).astype(BF16)

    return pl.pallas_call(
        body, name=name, grid=(T // tm,),
        in_specs=[_row_spec(tm, D), _full_spec((1, D))],
        out_specs=_row_spec(tm, D),
        out_shape=jax.ShapeDtypeStruct((T, D), BF16),
        compiler_params=_params("parallel"),
    )(x, g)


def _norm_bwd_add(x, g, du, dres, *, with_bf16=True, name):
    T, D = x.shape
    tm = _pick(T, (512, 256, 128))

    def body(x_ref, g_ref, du_ref, dr_ref, dx_ref, *rest):
        dg_ref = rest[-1]
        _, vjp = jax.vjp(_rms, x_ref[...], g_ref[...])
        dx, dg = vjp(du_ref[...].astype(F32))
        dx = dx + dr_ref[...]
        dx_ref[...] = dx
        if with_bf16:
            rest[0][...] = dx.astype(BF16)

        @pl.when(pl.program_id(0) == 0)
        def _():
            dg_ref[...] = jnp.zeros_like(dg_ref)

        dg_ref[...] += dg

    row = _row_spec(tm, D)
    return pl.pallas_call(
        body, name=name, grid=(T // tm,),
        in_specs=[row, _full_spec((1, D)), row, row],
        out_specs=[row] + ([row] if with_bf16 else []) + [_full_spec((1, D))],
        out_shape=[jax.ShapeDtypeStruct((T, D), F32)] + ([jax.ShapeDtypeStruct((T, D), BF16)] if with_bf16 else []) + [jax.ShapeDtypeStruct((1, D), F32)],
        compiler_params=_params("arbitrary"),
    )(x, g, du, dres)


def _final_loss_bwd(h2, g, target, *, name):
    T, D = h2.shape
    tm = _pick(T, (512, 256, 128))

    def body(h_ref, g_ref, t_ref, dx_ref, dxb_ref, dg_ref, loss_ref):
        y, vjp = jax.vjp(_rms, h_ref[...], g_ref[...])
        err = y - t_ref[...]
        dx, dg = vjp(err * (1.0 / D))
        dx_ref[...] = dx
        dxb_ref[...] = dx.astype(BF16)

        @pl.when(pl.program_id(0) == 0)
        def _():
            dg_ref[...] = jnp.zeros_like(dg_ref)
            loss_ref[...] = jnp.zeros_like(loss_ref)

        dg_ref[...] += dg
        loss_ref[...] += (0.5 / D) * jnp.sum(jnp.sum(err * err, axis=1, keepdims=True), axis=0, keepdims=True)

    return pl.pallas_call(
        body, name=name, grid=(T // tm,),
        in_specs=[_row_spec(tm, D), _full_spec((1, D)), _row_spec(tm, D)],
        out_specs=[_row_spec(tm, D), _row_spec(tm, D), _full_spec((1, D)), _full_spec((1, 1))],
        out_shape=[jax.ShapeDtypeStruct((T, D), F32), jax.ShapeDtypeStruct((T, D), BF16), jax.ShapeDtypeStruct((1, D), F32), jax.ShapeDtypeStruct((1, 1), F32)],
        compiler_params=_params("arbitrary"),
    )(h2, g, target)


def _merge_fn(gates, a, b):
    ga = gates[:, :D_MODEL].astype(F32)
    gb = gates[:, D_MODEL:].astype(F32)
    return _sigmoid(ga) * a.astype(F32) + _sigmoid(gb) * b.astype(F32)


def _gates_spec(tm):
    return pl.BlockSpec((tm, W_GATES), lambda i: (i, O_GATES // W_GATES))


def _merge_fwd(z, a, b, *, name):
    T = a.shape[0]
    tm = _pick(T, (512, 256, 128))

    def body(g_ref, a_ref, b_ref, o_ref):
        o_ref[...] = _merge_fn(g_ref[...], a_ref[...], b_ref[...]).astype(BF16)

    return pl.pallas_call(
        body, name=name, grid=(T // tm,),
        in_specs=[_gates_spec(tm), _row_spec(tm, D_MODEL), _row_spec(tm, D_MODEL)],
        out_specs=_row_spec(tm, D_MODEL),
        out_shape=jax.ShapeDtypeStruct((T, D_MODEL), BF16),
        compiler_params=_params("parallel"),
    )(z, a, b)


def _merge_bwd(z, a, b, dmerged, dz, *, name):
    T = a.shape[0]
    tm = _pick(T, (512, 256, 128))

    def body(g_ref, a_ref, b_ref, dm_ref, dz_in, dg_ref, da_ref, db_ref):
        g = g_ref[...].astype(F32)
        dm = dm_ref[...].astype(F32)
        sa = _sigmoid(g[:, :D_MODEL])
        sb = _sigmoid(g[:, D_MODEL:])
        da_ref[...] = (dm * sa).astype(BF16)
        db_ref[...] = (dm * sb).astype(BF16)
        dg_ref[:, :D_MODEL] = (dm * a_ref[...].astype(F32) * sa * (1.0 - sa)).astype(BF16)
        dg_ref[:, D_MODEL:] = (dm * b_ref[...].astype(F32) * sb * (1.0 - sb)).astype(BF16)

    return pl.pallas_call(
        body, name=name, grid=(T // tm,),
        in_specs=[_gates_spec(tm), _row_spec(tm, D_MODEL), _row_spec(tm, D_MODEL), _row_spec(tm, D_MODEL), ANY],
        out_specs=[_gates_spec(tm), _row_spec(tm, D_MODEL), _row_spec(tm, D_MODEL)],
        out_shape=[jax.ShapeDtypeStruct(dz.shape, BF16), jax.ShapeDtypeStruct((T, D_MODEL), BF16), jax.ShapeDtypeStruct((T, D_MODEL), BF16)],
        input_output_aliases={4: 0},
        compiler_params=_params("parallel"),
    )(z, a, b, dmerged, dz)


CONV_TC = 256


def _shift_down(x, n, rows):
    return jnp.where(rows >= n, pltpu.roll(x, n, 0), 0.0)


def _shift_up(x, n, rows, S):
    return jnp.where(rows < S - n, pltpu.roll(x, S - n, 0), 0.0)


def _conv_act_fwd(gu, conv_w, conv_b, *, name):
    B, S, _ = gu.shape
    tc = CONV_TC
    nc = D_FF // tc

    def body(g_ref, up_ref, w_ref, b_ref, o_ref, a_ref):
        g = g_ref[...].astype(F32)
        rows = lax.broadcasted_iota(jnp.int32, g.shape, 0)
        w = w_ref[...]
        a = w[2:3] * g + w[1:2] * _shift_down(g, 1, rows) + w[0:1] * _shift_down(g, 2, rows) + b_ref[...]
        o_ref[...] = (_silu(a) * up_ref[...].astype(F32)).astype(BF16)
        a_ref[...] = a.astype(BF16)

    col = pl.BlockSpec((None, S, tc), lambda b, j: (b, 0, j))
    return pl.pallas_call(
        body, name=name, grid=(B, nc),
        in_specs=[col,
                  pl.BlockSpec((None, S, tc), lambda b, j: (b, 0, j + nc)),
                  pl.BlockSpec((3, tc), lambda b, j: (0, j)),
                  pl.BlockSpec((1, tc), lambda b, j: (0, j))],
        out_specs=[col, col],
        out_shape=[jax.ShapeDtypeStruct((B, S, D_FF), BF16)] * 2,
        compiler_params=_params("parallel", "parallel"),
    )(gu, gu, conv_w, conv_b)


def _conv_act_bwd(gu, a_pre, conv_w, dact, *, name):
    B, S, _ = gu.shape
    tc = CONV_TC
    nc = D_FF // tc

    def body(g_ref, up_ref, a_ref, w_ref, da_ref, dg_ref, dup_ref, dw_ref, db_ref):
        g = g_ref[...].astype(F32)
        up = up_ref[...].astype(F32)
        a = a_ref[...].astype(F32)
        dact = da_ref[...].astype(F32)
        rows = lax.broadcasted_iota(jnp.int32, g.shape, 0)
        w = w_ref[...]
        sg = _sigmoid(a)
        dup_ref[...] = (dact * a * sg).astype(BF16)
        da = dact * up * sg * (1.0 + a * (1.0 - sg))
        da1 = _shift_up(da, 1, rows, S)
        da2 = _shift_up(da, 2, rows, S)
        dg_ref[...] = (w[2:3] * da + w[1:2] * da1 + w[0:1] * da2).astype(BF16)

        @pl.when(pl.program_id(1) == 0)
        def _():
            dw_ref[...] = jnp.zeros_like(dw_ref)
            db_ref[...] = jnp.zeros_like(db_ref)

        dw_ref[0:1, :] += jnp.sum(da2 * g, axis=0, keepdims=True)
        dw_ref[1:2, :] += jnp.sum(da1 * g, axis=0, keepdims=True)
        dw_ref[2:3, :] += jnp.sum(da * g, axis=0, keepdims=True)
        db_ref[...] += jnp.sum(da, axis=0, keepdims=True)

    col = pl.BlockSpec((None, S, tc), lambda j, b: (b, 0, j))
    return pl.pallas_call(
        body, name=name, grid=(nc, B),
        in_specs=[col,
                  pl.BlockSpec((None, S, tc), lambda j, b: (b, 0, j + nc)),
                  col,
                  pl.BlockSpec((3, tc), lambda j, b: (0, j)),
                  col],
        out_specs=[col, col, pl.BlockSpec((3, tc), lambda j, b: (0, j)), pl.BlockSpec((1, tc), lambda j, b: (0, j))],
        out_shape=[jax.ShapeDtypeStruct((B, S, D_FF), BF16), jax.ShapeDtypeStruct((B, S, D_FF), BF16),
                   jax.ShapeDtypeStruct((3, D_FF), F32), jax.ShapeDtypeStruct((1, D_FF), F32)],
        compiler_params=_params("parallel", "arbitrary"),
    )(gu, gu, a_pre, conv_w, dact)


HGRN_CPB = 4
HF = HGRN_HEADS * HGRN_DK


def _tri(n, upper=False):
    r = lax.broadcasted_iota(jnp.int32, (n, n), 0)
    c = lax.broadcasted_iota(jnp.int32, (n, n), 1)
    return (c >= r) if upper else (r >= c)


def _hs(h):
    return slice(h * HGRN_DK, (h + 1) * HGRN_DK)


def _cumsum_rows(tri_b, x):
    hi = x.astype(BF16)
    lo = (x - hi.astype(F32)).astype(BF16)
    return _nn(tri_b, hi) + _nn(tri_b, lo)


def _hgrn_pre(q, fz, lb, tril_b):
    qf = _silu(q)
    sg = _sigmoid(fz)
    f = lb + (1.0 - lb) * sg
    k = 1.0 - f
    b = _cumsum_rows(tril_b, jnp.log2(f))
    bref = b[CHUNK // 2:CHUNK // 2 + 1, :]
    blast = b[CHUNK - 1:CHUNK, :]
    e1 = jnp.exp2(b - bref)
    e2 = jnp.exp2(bref - b)
    e3 = e1 * jnp.exp2(bref)
    e4 = e2 * jnp.exp2(blast - bref)
    dec = jnp.exp2(blast)
    return sg, f, (e1, e2, e3, e4), qf * e1, k * e2, qf * e3, k * e4, dec


def _hgrn_fwd(zh, lb, gn, *, name):
    B, S, _ = zh.shape
    cpb = HGRN_CPB
    ts = cpb * CHUNK
    nblk = S // ts

    def body(z_ref, lb_ref, gn_ref, o_ref, st_ref, state):
        @pl.when(pl.program_id(1) == 0)
        def _():
            state[...] = jnp.zeros_like(state)

        H = HGRN_HEADS
        causal = _tri(CHUNK)
        tril_b = causal.astype(BF16)
        lb = lb_ref[...]
        for c in range(cpb):
            rows = slice(c * CHUNK, (c + 1) * CHUNK)
            q = z_ref[rows, 0:HF].astype(F32)
            fz = z_ref[rows, HF:2 * HF].astype(F32)
            v = z_ref[rows, 2 * HF:3 * HF]
            hg = z_ref[rows, 3 * HF:4 * HF].astype(F32)
            _, _, _, q_in, k_in, q_out, k_st, dec = _hgrn_pre(q, fz, lb, tril_b)
            q_in, k_in, q_out, k_st = (t.astype(BF16) for t in (q_in, k_in, q_out, k_st))
            a = [jnp.where(causal, _nt(q_in[:, _hs(h)], k_in[:, _hs(h)]), 0.0).astype(BF16) for h in range(H)]
            st = [state[h] for h in range(H)]
            for h in range(H):
                st_ref[c, h] = st[h]
            o = [_nn(a[h], v[:, _hs(h)]) + _nt(q_out[:, _hs(h)], st[h].astype(BF16)) for h in range(H)]
            for h in range(H):
                state[h] = st[h] * dec[:, _hs(h)] + _tn(v[:, _hs(h)], k_st[:, _hs(h)])
            gate = _silu(hg)
            for h in range(H):
                o_ref[rows, _hs(h)] = (_rms(o[h], gn_ref[...]) * gate[:, _hs(h)]).astype(BF16)

    return pl.pallas_call(
        body, name=name, grid=(B, nblk),
        in_specs=[pl.BlockSpec((None, ts, 4 * HF), lambda b, s: (b, s, 0)),
                  pl.BlockSpec((1, HF), lambda b, s: (0, 0)),
                  pl.BlockSpec((1, HGRN_DK), lambda b, s: (0, 0))],
        out_specs=[pl.BlockSpec((None, ts, HF), lambda b, s: (b, s, 0)),
                   pl.BlockSpec((None, cpb, HGRN_HEADS, HGRN_DK, HGRN_DK), lambda b, s: (b, s, 0, 0, 0))],
        out_shape=[jax.ShapeDtypeStruct((B, S, HF), BF16),
                   jax.ShapeDtypeStruct((B, S // CHUNK, HGRN_HEADS, HGRN_DK, HGRN_DK), F32)],
        scratch_shapes=[pltpu.VMEM((HGRN_HEADS, HGRN_DK, HGRN_DK), F32)],
        compiler_params=_params("arbitrary", "arbitrary"),
    )(zh, lb, gn)


def _hgrn_bwd(zh, lb, gn, states, doa, dz, *, name):
    B, S, _ = zh.shape
    cpb = HGRN_CPB
    ts = cpb * CHUNK
    nblk = S // ts
    rev = lambda b, s: (b, nblk - 1 - s, 0)

    def body(z_ref, lb_ref, gn_ref, st_ref, do_ref, dz_in, dz_ref, dlb_ref, dgn_ref, dstate):
        @pl.when(pl.program_id(1) == 0)
        def _():
            dstate[...] = jnp.zeros_like(dstate)

        @pl.when((pl.program_id(0) == 0) & (pl.program_id(1) == 0))
        def _():
            dlb_ref[...] = jnp.zeros_like(dlb_ref)
            dgn_ref[...] = jnp.zeros_like(dgn_ref)

        H = HGRN_HEADS
        cat = lambda xs: jnp.concatenate(xs, axis=1)
        causal = _tri(CHUNK)
        tril_b = causal.astype(BF16)
        triu_b = _tri(CHUNK, upper=True).astype(BF16)
        rowid = lax.broadcasted_iota(jnp.int32, (CHUNK, HF), 0)
        lb = lb_ref[...]
        gn = gn_ref[...]
        for c in reversed(range(cpb)):
            rows = slice(c * CHUNK, (c + 1) * CHUNK)
            q = z_ref[rows, 0:HF].astype(F32)
            fz = z_ref[rows, HF:2 * HF].astype(F32)
            v = z_ref[rows, 2 * HF:3 * HF]
            hg = z_ref[rows, 3 * HF:4 * HF].astype(F32)
            sg, f, (e1, e2, e3, e4), q_in, k_in, q_out, k_st, dec = _hgrn_pre(q, fz, lb, tril_b)
            q_in_b, k_in_b, q_out_b, k_st_b = (t.astype(BF16) for t in (q_in, k_in, q_out, k_st))
            a_b = [jnp.where(causal, _nt(q_in_b[:, _hs(h)], k_in_b[:, _hs(h)]), 0.0).astype(BF16) for h in range(H)]
            st = [st_ref[c, h] for h in range(H)]
            st_b = [t.astype(BF16) for t in st]
            o = [_nn(a_b[h], v[:, _hs(h)]) + _nt(q_out_b[:, _hs(h)], st_b[h]) for h in range(H)]
            dout = do_ref[rows, :].astype(F32)
            shg = _sigmoid(hg)
            gate = hg * shg
            do_l, dgn_acc = [], jnp.zeros_like(gn)
            for h in range(H):
                _, norm_vjp = jax.vjp(_rms, o[h], gn)
                d_o, d_gn = norm_vjp(dout[:, _hs(h)] * gate[:, _hs(h)])
                do_l.append(d_o)
                dgn_acc = dgn_acc + d_gn
            dgn_ref[...] += dgn_acc
            on = cat([_rms(o[h], gn) for h in range(H)])
            dhg = dout * on * shg * (1.0 + hg * (1.0 - shg))
            do_b = [t.astype(BF16) for t in do_l]
            dst = [dstate[h] for h in range(H)]
            dst_b = [t.astype(BF16) for t in dst]
            da_b = [jnp.where(causal, _nt(do_b[h], v[:, _hs(h)]), 0.0).astype(BF16) for h in range(H)]
            dv = cat([_tn(a_b[h], do_b[h]) + _nt(k_st_b[:, _hs(h)], dst_b[h]) for h in range(H)])
            dq_in = cat([_nn(da_b[h], k_in_b[:, _hs(h)]) for h in range(H)])
            dk_in = cat([_tn(da_b[h], q_in_b[:, _hs(h)]) for h in range(H)])
            dq_out = cat([_nn(do_b[h], st_b[h]) for h in range(H)])
            dk_st = cat([_nn(v[:, _hs(h)], dst_b[h]) for h in range(H)])
            ddec = cat([jnp.sum(st[h] * dst[h], axis=0, keepdims=True) for h in range(H)])
            for h in range(H):
                dstate[h] = dst[h] * dec[:, _hs(h)] + _tn(do_b[h], q_out_b[:, _hs(h)])
            t_qin = dq_in * q_in
            t_kin = dk_in * k_in
            t_kst = dk_st * k_st
            db = t_qin - t_kin + dq_out * q_out - t_kst
            dbref = jnp.sum(t_kin - t_qin, axis=0, keepdims=True)
            dblast = jnp.sum(t_kst, axis=0, keepdims=True) + ddec * dec
            db = db + jnp.where(rowid == CHUNK // 2, dbref, 0.0) + jnp.where(rowid == CHUNK - 1, dblast, 0.0)
            dlogf = _cumsum_rows(triu_b, db)
            dqf = dq_in * e1 + dq_out * e3
            dk = dk_in * e2 + dk_st * e4
            df = dlogf / f - dk
            dfz = df * (1.0 - lb) * sg * (1.0 - sg)
            dlb_ref[...] += jnp.sum(df * (1.0 - sg), axis=0, keepdims=True)
            sq = _sigmoid(q)
            dq = dqf * sq * (1.0 + q * (1.0 - sq))
            dz_ref[rows, 0:HF] = dq.astype(BF16)
            dz_ref[rows, HF:2 * HF] = dfz.astype(BF16)
            dz_ref[rows, 2 * HF:3 * HF] = dv.astype(BF16)
            dz_ref[rows, 3 * HF:4 * HF] = dhg.astype(BF16)

    return pl.pallas_call(
        body, name=name, grid=(B, nblk),
        in_specs=[pl.BlockSpec((None, ts, 4 * HF), rev),
                  pl.BlockSpec((1, HF), lambda b, s: (0, 0)),
                  pl.BlockSpec((1, HGRN_DK), lambda b, s: (0, 0)),
                  pl.BlockSpec((None, cpb, HGRN_HEADS, HGRN_DK, HGRN_DK), lambda b, s: (b, nblk - 1 - s, 0, 0, 0)),
                  pl.BlockSpec((None, ts, HF), rev),
                  ANY],
        out_specs=[pl.BlockSpec((None, ts, 4 * HF), rev),
                   pl.BlockSpec((1, HF), lambda b, s: (0, 0)),
                   pl.BlockSpec((1, HGRN_DK), lambda b, s: (0, 0))],
        out_shape=[jax.ShapeDtypeStruct(dz.shape, BF16),
                   jax.ShapeDtypeStruct((1, HF), F32),
                   jax.ShapeDtypeStruct((1, HGRN_DK), F32)],
        input_output_aliases={5: 0},
        scratch_shapes=[pltpu.VMEM((HGRN_HEADS, HGRN_DK, HGRN_DK), F32)],
        compiler_params=_params("arbitrary", "arbitrary"),
    )(zh, lb, gn, states, doa, dz)


KV_W = ATT_KV_HEADS * ATT_HD
ATT_SCALE = ATT_HD ** -0.5


def _rope(x, cos, sin, inverse=False):
    half = ROPE_DIM // 2
    outs = []
    for p in range(x.shape[1] // 128):
        xp = x[:, p * 128:(p + 1) * 128]
        lane = lax.broadcasted_iota(jnp.int32, xp.shape, 1) % ATT_HD
        sw = jnp.where(lane < half, pltpu.roll(xp, 128 - half, 1), pltpu.roll(xp, half, 1))
        outs.append(xp * cos - sw * sin if inverse else xp * cos + sw * sin)
    return outs[0] if len(outs) == 1 else jnp.concatenate(outs, axis=1)


PAIRS_PER_KV = ATT_GROUP // 2


def _swap_halves(x):
    return pltpu.roll(x, ATT_HD, 1)


def _kv_padded(t, low):
    sw = _swap_halves(t)
    zero = jnp.zeros_like(t)
    out = []
    for g in range(ATT_KV_HEADS):
        in_low, in_high = (t, sw) if g == 0 else (sw, t)
        out.append((jnp.where(low, in_low, zero).astype(BF16), jnp.where(low, zero, in_high).astype(BF16)))
    return out


def _swa_mask(first_block):
    qi = lax.broadcasted_iota(jnp.int32, (WINDOW, 2 * WINDOW), 0)
    mi = lax.broadcasted_iota(jnp.int32, (WINDOW, 2 * WINDOW), 1)
    band = (mi > qi) & (mi <= qi + WINDOW)
    return band & (jnp.logical_not(first_block) | (mi >= WINDOW))


def _swa_specs(nb):
    cur = lambda b, i: (b, i, 0)
    prev = lambda b, i: (b, jnp.maximum(i - 1, 0), 0)
    return cur, prev


def _swa_z_specs():
    q = pl.BlockSpec((None, WINDOW, W_AQ), lambda b, i: (b, i, O_AQ // W_AQ))
    kv_prev = pl.BlockSpec((None, WINDOW, W_AKV), lambda b, i: (b, jnp.maximum(i - 1, 0), O_AKV // W_AKV))
    kv_cur = pl.BlockSpec((None, WINDOW, W_AKV), lambda b, i: (b, i, O_AKV // W_AKV))
    return q, kv_prev, kv_cur


def _swa_fwd(z, cos, sin, sinks, *, name):
    B, S, _ = z.shape
    nb = S // WINDOW
    cur, prev = _swa_specs(nb)

    def body(q_ref, kvp_ref, kvc_ref, cp_ref, sp_ref, cc_ref, sc_ref, sink_ref, o_ref, lse_ref, qr_ref, kr_ref):
        cos_c, sin_c = cc_ref[...], sc_ref[...]
        q = (_rope(q_ref[...].astype(F32), cos_c, sin_c) * ATT_SCALE).astype(BF16)
        k = jnp.concatenate([_rope(kvp_ref[:, :KV_W].astype(F32), cp_ref[...], sp_ref[...]),
                             _rope(kvc_ref[:, :KV_W].astype(F32), cos_c, sin_c)], axis=0)
        qr_ref[...] = q
        kr_ref[...] = k[WINDOW:].astype(BF16)
        v = jnp.concatenate([kvp_ref[:, KV_W:], kvc_ref[:, KV_W:]], axis=0).astype(F32)
        low = lax.broadcasted_iota(jnp.int32, k.shape, 1) < ATT_HD
        kpad = _kv_padded(k, low)
        vpad = _kv_padded(v, low)
        mask = _swa_mask(pl.program_id(1) == 0)
        lses = []
        for g in range(ATT_KV_HEADS):
            pairs = range(g * PAIRS_PER_KV, (g + 1) * PAIRS_PER_KV)
            keys = [(p, e) for p in pairs for e in (0, 1)]
            qp = {p: q[:, p * 128:(p + 1) * 128] for p in pairs}
            s = {pe: jnp.where(mask, _nt(qp[pe[0]], kpad[g][pe[1]]), NEG_INF) for pe in keys}
            pr = {}
            for pe in keys:
                sink = sink_ref[0, 2 * pe[0] + pe[1]]
                m = jnp.maximum(jnp.max(s[pe], axis=1, keepdims=True), sink)
                ex = jnp.exp(s[pe] - m)
                den = jnp.sum(ex, axis=1, keepdims=True) + jnp.exp(sink - m)
                pr[pe] = (ex * (1.0 / den)).astype(BF16)
                lses.append(m + jnp.log(den))
            for p in pairs:
                o_ref[:, p * 128:(p + 1) * 128] = (_nn(pr[p, 0], vpad[g][0]) + _nn(pr[p, 1], vpad[g][1])).astype(BF16)
        lse_ref[...] = jnp.concatenate(lses, axis=1)

    tab = lambda im: pl.BlockSpec((None, WINDOW, 128), im)
    return pl.pallas_call(
        body, name=name, grid=(B, nb),
        in_specs=[*_swa_z_specs(),
                  tab(prev), tab(prev), tab(cur), tab(cur),
                  pl.BlockSpec(memory_space=pltpu.SMEM)],
        out_specs=[pl.BlockSpec((None, WINDOW, D_MODEL), cur), pl.BlockSpec((None, WINDOW, ATT_HEADS), cur),
                   pl.BlockSpec((None, WINDOW, D_MODEL), cur), pl.BlockSpec((None, WINDOW, KV_W), cur)],
        out_shape=[jax.ShapeDtypeStruct((B, S, D_MODEL), BF16), jax.ShapeDtypeStruct((B, S, ATT_HEADS), F32),
                   jax.ShapeDtypeStruct((B, S, D_MODEL), BF16), jax.ShapeDtypeStruct((B, S, KV_W), BF16)],
        compiler_params=_params("parallel", "parallel"),
    )(z, z, z, cos, sin, cos, sin, sinks)


def _swa_bwd(z, qr, kr, cos, sin, sinks, lse, dob, dz, *, name):
    B, S, _ = z.shape
    nb = S // WINDOW
    cur, prev = _swa_specs(nb)

    def body(q_ref, krp_ref, krc_ref, kvp_ref, kvc_ref, cp_ref, sp_ref, cc_ref, sc_ref, sink_ref, lse_ref, do_ref, dz_in,
             dq_ref, dkc_ref, dkp_ref, dsink_ref):
        @pl.when((pl.program_id(0) == 0) & (pl.program_id(1) == 0))
        def _():
            dsink_ref[...] = jnp.zeros_like(dsink_ref)

        cos_c, sin_c, cos_p, sin_p = cc_ref[...], sc_ref[...], cp_ref[...], sp_ref[...]
        q = q_ref[...]
        k = jnp.concatenate([krp_ref[...], krc_ref[...]], axis=0).astype(F32)
        v = jnp.concatenate([kvp_ref[:, KV_W:], kvc_ref[:, KV_W:]], axis=0).astype(F32)
        low = lax.broadcasted_iota(jnp.int32, k.shape, 1) < ATT_HD
        kpad = _kv_padded(k, low)
        vpad = _kv_padded(v, low)
        mask = _swa_mask(pl.program_id(1) == 0)
        lse = lse_ref[...]
        dq_parts, dk_sum, dv_sum, dsinks = [], [], [], []
        for g in range(ATT_KV_HEADS):
            pairs = range(g * PAIRS_PER_KV, (g + 1) * PAIRS_PER_KV)
            keys = [(p, e) for p in pairs for e in (0, 1)]
            qp = {p: q[:, p * 128:(p + 1) * 128] for p in pairs}
            dop = {p: do_ref[:, p * 128:(p + 1) * 128] for p in pairs}
            s = {pe: jnp.where(mask, _nt(qp[pe[0]], kpad[g][pe[1]]), NEG_INF) for pe in keys}
            dp = {pe: _nt(dop[pe[0]], vpad[g][pe[1]]) for pe in keys}
            pr, ds = {}, {}
            for pe in keys:
                h = 2 * pe[0] + pe[1]
                lse_h = lse[:, h:h + 1]
                pf = jnp.exp(s[pe] - lse_h)
                delta = jnp.sum(pf * dp[pe], axis=1, keepdims=True)
                ds[pe] = (pf * (dp[pe] - delta)).astype(BF16)
                pr[pe] = pf.astype(BF16)
                p_sink = jnp.exp(sink_ref[0, h] - lse_h)
                dsinks.append(-jnp.sum(p_sink * delta, axis=0, keepdims=True))
            for p in pairs:
                dq_parts.append((_nn(ds[p, 0], kpad[g][0]) + _nn(ds[p, 1], kpad[g][1])) * ATT_SCALE)
            x = [sum(_tn(ds[p, e], qp[p]) for p in pairs) for e in (0, 1)]
            y = [sum(_tn(pr[p, e], dop[p]) for p in pairs) for e in (0, 1)]
            zk = jnp.where(low, x[0], x[1])
            zv = jnp.where(low, y[0], y[1])
            dk_sum.append(zk + _swap_halves(zk))
            dv_sum.append(zv + _swap_halves(zv))
        dq_ref[...] = _rope(jnp.concatenate(dq_parts, axis=1), cos_c, sin_c, inverse=True).astype(BF16)
        dk = jnp.where(low, dk_sum[0], dk_sum[1])
        dv = jnp.where(low, dv_sum[0], dv_sum[1])
        dkp_ref[:, :KV_W] = _rope(dk[:WINDOW], cos_p, sin_p, inverse=True)
        dkp_ref[:, KV_W:] = dv[:WINDOW]
        dkc_ref[:, :KV_W] = _rope(dk[WINDOW:], cos_c, sin_c, inverse=True)
        dkc_ref[:, KV_W:] = dv[WINDOW:]
        dsink_ref[...] += jnp.concatenate(dsinks, axis=1)

    tab = lambda im: pl.BlockSpec((None, WINDOW, 128), im)
    return pl.pallas_call(
        body, name=name, grid=(B, nb),
        in_specs=[pl.BlockSpec((None, WINDOW, D_MODEL), cur), tab(prev), tab(cur),
                  *_swa_z_specs()[1:],
                  tab(prev), tab(prev), tab(cur), tab(cur),
                  pl.BlockSpec(memory_space=pltpu.SMEM),
                  pl.BlockSpec((None, WINDOW, ATT_HEADS), cur),
                  pl.BlockSpec((None, WINDOW, D_MODEL), cur),
                  ANY],
        out_specs=[_swa_z_specs()[0],
                   pl.BlockSpec((None, WINDOW, 2 * KV_W), cur), pl.BlockSpec((None, WINDOW, 2 * KV_W), cur),
                   pl.BlockSpec((1, ATT_HEADS), lambda b, i: (0, 0))],
        out_shape=[jax.ShapeDtypeStruct(dz.shape, BF16),
                   jax.ShapeDtypeStruct((B, S, 2 * KV_W), F32), jax.ShapeDtypeStruct((B, S, 2 * KV_W), F32),
                   jax.ShapeDtypeStruct((1, ATT_HEADS), F32)],
        input_output_aliases={12: 0},
        compiler_params=_params("arbitrary", "arbitrary"),
    )(qr, kr, kr, z, z, cos, sin, cos, sin, sinks, lse, dob, dz)


def _swa_dkv_combine(dkv_cur, dkv_prev, dz, *, name):
    B, S, W = dkv_cur.shape

    def body(c_ref, p_ref, dz_in, o_ref):
        rows = lax.broadcasted_iota(jnp.int32, (S, W), 0)
        o_ref[...] = (c_ref[...] + _shift_up(p_ref[...], WINDOW, rows, S)).astype(BF16)

    seq = pl.BlockSpec((None, S, W), lambda b: (b, 0, 0))
    return pl.pallas_call(
        body, name=name, grid=(B,),
        in_specs=[seq, seq, ANY], out_specs=pl.BlockSpec((None, S, W), lambda b: (b, 0, O_AKV // W_AKV)),
        out_shape=jax.ShapeDtypeStruct(dz.shape, BF16),
        input_output_aliases={2: 0},
        compiler_params=_params("parallel"),
    )(dkv_cur, dkv_prev, dz)


def _rope_tables(positions):
    half = ROPE_DIM // 2
    inv = ROPE_THETA ** (-2.0 * jnp.arange(half, dtype=F32) / ROPE_DIM)
    ang = positions.astype(F32)[..., None] * inv
    c, s = jnp.cos(ang), jnp.sin(ang)
    pad = jnp.zeros(ang.shape[:-1] + (ATT_HD - ROPE_DIM,), F32)
    cos = jnp.concatenate([c, c, pad + 1.0], axis=-1)
    sin = jnp.concatenate([-s, s, pad], axis=-1)
    return jnp.tile(cos, (1, 1, 2)), jnp.tile(sin, (1, 1, 2))


def _lower_bound(lb_logits, *, name):
    def body(l_ref, o_ref):
        l = l_ref[...]
        e = jnp.exp(l - jnp.max(l, axis=0, keepdims=True))
        o_ref[...] = e[0:1] / jnp.sum(e, axis=0, keepdims=True)

    return pl.pallas_call(body, name=name, out_shape=jax.ShapeDtypeStruct((1, lb_logits.shape[1]), F32))(lb_logits)


W_ZH, W_GATES, W_AQ, W_AKV = 4 * HF, 2 * D_MODEL, ATT_HEADS * ATT_HD, 2 * KV_W
O_ZH, O_GATES, O_AQ, O_AKV = 0, W_ZH, W_ZH + W_GATES, W_ZH + W_GATES + W_AQ
W_IN = W_ZH + W_GATES + W_AQ + W_AKV


W_IN_BLK = W_IN // N_DEV


def _reordered_rows(w_t, *, name):
    pieces = [(0, O_ZH, W_ZH), (W_ZH + W_AQ + W_AKV, O_GATES, W_GATES), (W_ZH, O_AQ, W_AQ + W_AKV)]

    def body(src, dst, sems):
        copies = [pltpu.make_async_copy(src.at[pl.ds(s, n)], dst.at[pl.ds(d, n)], sems.at[i]) for i, (s, d, n) in enumerate(pieces)]
        for cp in copies:
            cp.start()
        for cp in copies:
            cp.wait()

    return pl.pallas_call(body, name=name, out_shape=jax.ShapeDtypeStruct(w_t.shape, w_t.dtype), in_specs=[ANY], out_specs=ANY,
                          scratch_shapes=[pltpu.SemaphoreType.DMA((len(pieces),))])(w_t)


def _reference_row_block(j, rows=256):
    nz, ng = W_ZH // rows, W_GATES // rows
    return jnp.where(j < nz, j, jnp.where(j < nz + ng, j + (W_AQ + W_AKV) // rows, j - ng))


def _local_step(x, positions, target, small, w_in_t, rest_weights, emit, start_token):
    B, S, D = x.shape
    T = B * S
    x2 = x.reshape(T, D)
    cos, sin = _rope_tables(positions)
    lb = _lower_bound(small["lb_logits"], name="lb_fwd")
    zero = lambda tok: tok[0:1, 0:1]

    u1 = _norm_cast(x2, small["norm1_g"] + zero(start_token), name="norm1")
    z = _matmul(u1, w_in_t, tb=True, out_dtype=BF16, name="mm_z", tm=1024, tn=W_IN // 2)
    z3 = z.reshape(B, S, W_IN)
    oa, states = _hgrn_fwd(z3, lb, small["hgrn_norm_g"], name="hgrn_fwd")
    ob, lse, qr, kr = _swa_fwd(z3, cos, sin, small["attn_sinks"], name="swa_fwd")
    oa2 = oa.reshape(T, D)
    ob2 = ob.reshape(T, D)
    W = rest_weights("mix", ob)
    pa = _matmul(oa2, W["w_a"], out_dtype=BF16, name="mm_pa", tm=1024, tn=1024)
    pb = _matmul(ob2, W["w_b"], out_dtype=BF16, name="mm_pb", tm=2048, tn=512)
    merged = _merge_fwd(z, pa, pb, name="merge_fwd")
    h = _matmul(merged, W["w_out"], addend=x2, name="mm_h", tm=512, tn=1024)
    u2 = _norm_cast(h, small["norm2_g"], name="norm2")
    W.update(rest_weights("ffn", u2))
    gu = _matmul(u2, W["w_ffn_t"], tb=True, out_dtype=BF16, name="mm_gu", tm=2048, tn=512)
    gu3 = gu.reshape(B, S, 2 * D_FF)
    act, a_pre = _conv_act_fwd(gu3, W["conv_w"], small["conv_b"], name="conv_act_fwd")
    act2 = act.reshape(T, D_FF)
    h2 = _matmul(act2, W["w_down"], addend=h, name="mm_h2", tm=1024, tn=1024)

    g = {}
    dh2, dh2b, g["final_g"], loss = _final_loss_bwd(h2, small["final_g"].reshape(1, D), target.reshape(T, D), name="final_loss_bwd")
    dact = _matmul(dh2b, W["w_down"], tb=True, out_dtype=BF16, name="mm_dact", tm=1024, tn=D_FF)
    dw_down_t = _matmul(dh2b, act2, ta=True, out_dtype=BF16, name="mm_dw_down", tm=1024, tn=256, tk=8192)
    dg_, dup, g["conv_w"], g["conv_b"] = _conv_act_bwd(gu3, a_pre, W["conv_w"], dact.reshape(B, S, D_FF), name="conv_act_bwd")
    dg2 = dg_.reshape(T, D_FF)
    dup2 = dup.reshape(T, D_FF)
    du2 = _matmul(dg2, W["w_ffn_t"], name="mm_du2_g", tm=1024, tn=1024, b_koff=0)
    du2 = _matmul(dup2, W["w_ffn_t"], addend=du2, out_dtype=BF16, name="mm_du2_u", tm=1024, tn=1024, b_koff=1)
    dw_ffn_t = _matmul(u2, dg2, ta=True, out_t=True, out_dtype=BF16, into=lax.empty((2 * D_FF, D), BF16), o_noff=0, name="mm_dw_ffn_g", tm=1024, tn=256, tk=8192)
    dw_ffn_t = _matmul(u2, dup2, ta=True, out_t=True, out_dtype=BF16, into=dw_ffn_t, o_noff=D_FF // 256, name="mm_dw_ffn_u", tm=1024, tn=256, tk=8192)
    tok = emit("ffn", dict(w_ffn_t=dw_ffn_t, w_down=dw_down_t.T))
    dh, dhb, g["norm2_g"] = _norm_bwd_add(h, small["norm2_g"] + zero(tok), du2, dh2, name="norm2_bwd")
    dmerged = _matmul(dhb, W["w_out"], tb=True, out_dtype=BF16, name="mm_dmerged", tm=2048, tn=512)
    dw_out = _matmul(merged, dhb, ta=True, out_dtype=BF16, name="mm_dw_out", tm=1024, tn=1024, tk=2048)
    dz, dpa, dpb = _merge_bwd(z, pa, pb, dmerged, lax.empty((T, W_IN), BF16), name="merge_bwd")
    doa =_matmul(dpa, W["w_a"], tb=True, out_dtype=BF16, name="mm_doa", tm=1024, tn=1024)
    dw_a = _matmul(oa2, dpa, ta=True, out_dtype=BF16, name="mm_dw_a", tm=1024, tn=1024, tk=2048)
    dob = _matmul(dpb, W["w_b"], tb=True, out_dtype=BF16, name="mm_dob", tm=2048, tn=512)
    dw_b = _matmul(ob2, dpb, ta=True, out_dtype=BF16, name="mm_dw_b", tm=1024, tn=1024, tk=2048)
    tok = emit("mix", dict(w_out=dw_out, w_a=dw_a, w_b=dw_b))
    dz3, dkv_cur, dkv_prev, dsinks = _swa_bwd(z3, qr, kr, cos, sin, small["attn_sinks"] + zero(tok), lse, dob.reshape(B, S, D),
                                              dz.reshape(B, S, W_IN), name="swa_bwd")
    dz3 = _swa_dkv_combine(dkv_cur, dkv_prev, dz3, name="swa_dkv")
    g["attn_sinks"] = dsinks
    dz3, g["lb"], g["hgrn_norm_g"] = _hgrn_bwd(z3, lb, small["hgrn_norm_g"], states, doa.reshape(B, S, D), dz3, name="hgrn_bwd")
    dz = dz3.reshape(T, W_IN)
    dw_in_t = _matmul(u1, dz, ta=True, out_t=True, o_block_perm=_reference_row_block, out_dtype=BF16, name="mm_dw_in", tm=1024, tn=256, tk=8192)
    tok = emit("in", dict(w_in_t=dw_in_t))
    du1 = _matmul(dz, w_in_t, after=tok, out_dtype=BF16, name="mm_du1", tm=1024, tn=512)
    dx, g["norm1_g"] = _norm_bwd_add(x2, small["norm1_g"], du1, dh, with_bf16=False, name="norm1_bwd")
    g["lb_logits"] = _lb_bwd(g.pop("lb"), lb, name="lb_bwd")
    return loss, dx.reshape(B, S, D), g


def _my_place():
    return lax.axis_index("x"), lax.axis_index("y"), lax.axis_index("c")


def _gather_blocks(x_ref, out_ref, send_sems, recv_sems, local_sem):
    x, y, c = _my_place()
    me, sibling = (x, y, c), (x, y, 1 - c)
    chips = [(1 - x, y), (x, 1 - y), (1 - x, 1 - y)]

    def slot(px, py, pc):
        return out_ref.at[4 * px + 2 * py + pc]

    def copy(k, block, to, src=None):
        return pltpu.make_async_remote_copy(
            src_ref=slot(*block) if src is None else src, dst_ref=slot(*block),
            send_sem=send_sems.at[k], recv_sem=recv_sems.at[k], device_id=to, device_id_type=MESH)

    mine = pltpu.make_async_copy(x_ref, slot(*me), local_sem)
    mine.start()
    first = [copy(0, me, sibling, src=x_ref)]
    first += [copy(1 + j, me, (*chip, c), src=x_ref) for j, chip in enumerate(chips)]
    for cp in first:
        cp.start()
    passed = [copy(4 + j, (*chip, c), sibling) for j, chip in enumerate(chips)]
    for j, chip in enumerate(chips):
        copy(1 + j, (*chip, c), me).wait_recv()
        passed[j].start()
    copy(0, sibling, me).wait_recv()
    for j, chip in enumerate(chips):
        copy(4 + j, (*chip, 1 - c), me).wait_recv()
    for cp in first + passed:
        cp.wait_send()
    mine.wait()


GATHER_SEMS = [pltpu.SemaphoreType.DMA((7,)), pltpu.SemaphoreType.DMA((7,)), pltpu.SemaphoreType.DMA]


def _all_gather(blk, *, name):
    return pl.pallas_call(
        _gather_body_fn(), name=name,
        out_shape=jax.ShapeDtypeStruct((N_DEV,) + blk.shape, blk.dtype),
        in_specs=[ANY], out_specs=ANY,
        scratch_shapes=GATHER_SEMS,
    )(blk)


def _gather_body_fn():
    def body(x_ref, out_ref, send_sems, recv_sems, local_sem):
        _gather_blocks(x_ref, out_ref, send_sems, recv_sems, local_sem)
    return body


SLAB_W = 1152
SMALL_SHAPES = dict(norm1_g=(1, D_MODEL), lb_logits=(2, HGRN_HEADS * HGRN_DK), hgrn_norm_g=(1, HGRN_DK), attn_sinks=(1, ATT_HEADS),
                    norm2_g=(1, D_MODEL), conv_b=(1, D_FF), final_g=(1, D_MODEL))
CONVW_BLK = D_FF // N_DEV
CONVW_STRIDE = SLAB_W // 3


def _slab_layout():
    layout, r = {}, 0
    for nm, (nr, w) in SMALL_SHAPES.items():
        layout[nm] = []
        for i in range(nr):
            for c0 in range(0, w, SLAB_W):
                layout[nm].append((r, i, c0, min(SLAB_W, w - c0)))
                r += 1
    return layout, r


SMALL_ROWS, _N_SMALL_ROWS = _slab_layout()
CONV_ROW0 = -(-_N_SMALL_ROWS // 8) * 8
LOSS_ROW = CONV_ROW0 + N_DEV
SLAB_ROWS = LOSS_ROW + 8


def _small_step(grads, g_conv_w, loss, params, moments, variances, dev, *, name):
    names = list(SMALL_ROWS)
    n = len(names)

    def body(dev_ref, *refs):
        g_refs = dict(zip(names, refs[:n]))
        gc_ref, loss_ref = refs[n], refs[n + 1]
        base = n + 2
        w_refs, m_refs, v_refs = (dict(zip(names + ["conv_w"], refs[base + i * (n + 1):base + (i + 1) * (n + 1)])) for i in range(3))
        o = base + 3 * (n + 1)
        gath_ref, loss_out = refs[o], refs[o + 1]
        outs = {nm: refs[o + 2 + 4 * i:o + 6 + 4 * i] for i, nm in enumerate(names + ["conv_w"])}
        slab, total, send_sems, recv_sems, local_sem = refs[-5:]

        slab[...] = jnp.zeros_like(slab)
        for nm, pieces in SMALL_ROWS.items():
            for r, i, c0, w in pieces:
                slab[r:r + 1, 0:w] = g_refs[nm][i:i + 1, c0:c0 + w]
        for p in range(N_DEV):
            for j in range(3):
                slab[CONV_ROW0 + p:CONV_ROW0 + p + 1, j * CONVW_STRIDE:j * CONVW_STRIDE + CONVW_BLK] = gc_ref[j:j + 1, p * CONVW_BLK:(p + 1) * CONVW_BLK]
        slab[LOSS_ROW:LOSS_ROW + 1, 0:1] = loss_ref[...]
        _gather_blocks(slab, gath_ref, send_sems, recv_sems, local_sem)
        acc = gath_ref[0]
        for p in range(1, N_DEV):
            acc = acc + gath_ref[p]
        total[...] = acc
        loss_out[...] = total[LOSS_ROW:LOSS_ROW + 1, 0:1]

        def update(nm, g, i, c0, w):
            at = (slice(i, i + 1), slice(c0, c0 + w))
            d, mn, vn = _adamw_math(w_refs[nm][at], g, m_refs[nm][at], v_refs[nm][at])
            for ref, val in zip(outs[nm], (g, d, mn, vn)):
                ref[at] = val

        for nm, pieces in SMALL_ROWS.items():
            for r, i, c0, w in pieces:
                update(nm, total[r:r + 1, 0:w], i, c0, w)
        conv_rows = total[CONV_ROW0:CONV_ROW0 + N_DEV, :]
        rowid = lax.broadcasted_iota(jnp.int32, conv_rows.shape, 0)
        mine = jnp.sum(jnp.where(rowid == dev_ref[0], conv_rows, 0.0), axis=0, keepdims=True)
        for j in range(3):
            update("conv_w", mine[:, j * CONVW_STRIDE:j * CONVW_STRIDE + CONVW_BLK], j, 0, CONVW_BLK)

    order = names + ["conv_w"]
    ins = [grads[nm] for nm in names] + [g_conv_w, loss]
    for d in (params, moments, variances):
        ins += [d[nm] for nm in order]
    vmem = pl.BlockSpec(memory_space=pltpu.VMEM)
    out_shape = [jax.ShapeDtypeStruct((N_DEV, SLAB_ROWS, SLAB_W), F32), jax.ShapeDtypeStruct((1, 1), F32)]
    for nm in order:
        out_shape += [jax.ShapeDtypeStruct(params[nm].shape, F32)] * 4
    res = pl.pallas_call(
        body, name=name,
        grid_spec=pltpu.PrefetchScalarGridSpec(
            num_scalar_prefetch=1, grid=(1,),
            in_specs=[vmem] * len(ins), out_specs=[vmem] * len(out_shape),
            scratch_shapes=[pltpu.VMEM((SLAB_ROWS, SLAB_W), F32), pltpu.VMEM((SLAB_ROWS, SLAB_W), F32)] + GATHER_SEMS),
        out_shape=out_shape,
    )(dev, *ins)
    return res[1], {nm: tuple(res[2 + 4 * i:6 + 4 * i]) for i, nm in enumerate(order)}


HBM_SPEC = pl.BlockSpec(memory_space=pltpu.HBM)
SEM_SPEC = pl.BlockSpec(memory_space=pltpu.SEMAPHORE)
DATAFLOW_EFFECT = pltpu.SideEffectType.DATAFLOW_SIDE_EFFECTING
N_PEERS = N_DEV - 1


def _peers(x, y, c):
    return [(1 - x if r & 4 else x, 1 - y if r & 2 else y, 1 - c if r & 1 else c) for r in range(1, N_DEV)]


def _exchange_start(srcs, scatter, *, after=None, name):
    n = len(srcs)
    lands = [lax.empty(a.shape if scatter else (N_DEV,) + a.shape, a.dtype) for a in srcs]
    extra = [] if after is None else [after]

    def body(*refs):
        src_refs, land_refs = refs[:n], refs[n:2 * n]
        send_sems, recv_sems, token = refs[2 * n + len(extra)], refs[2 * n + len(extra) + 1], refs[-1]
        x, y, c = _my_place()
        me = 4 * x + 2 * y + c
        for i in range(n):
            for r, (tx, ty, tc) in enumerate(_peers(x, y, c)):
                src = src_refs[i].at[4 * tx + 2 * ty + tc] if scatter else src_refs[i]
                pltpu.make_async_remote_copy(
                    src_ref=src, dst_ref=land_refs[i].at[me], send_sem=send_sems.at[N_PEERS * i + r],
                    recv_sem=recv_sems.at[N_PEERS * i + r], device_id=(tx, ty, tc), device_id_type=MESH).start()
        token[...] = jnp.zeros_like(token)

    thru = [pltpu.HBM(a.shape, a.dtype) for a in list(srcs) + lands]
    res = pl.pallas_call(
        body, name=name,
        out_shape=(pltpu.SemaphoreType.DMA((N_PEERS * n,)), pltpu.SemaphoreType.DMA((N_PEERS * n,)), *thru,
                   jax.ShapeDtypeStruct((8, 128), F32)),
        in_specs=[HBM_SPEC] * (2 * n) + [ANY] * len(extra),
        out_specs=(SEM_SPEC, SEM_SPEC, *([HBM_SPEC] * (2 * n)), pl.BlockSpec(memory_space=pltpu.VMEM)),
        input_output_aliases={i: 2 + i for i in range(2 * n)},
        compiler_params=pltpu.CompilerParams(has_side_effects=DATAFLOW_EFFECT),
    )(*[pltpu.with_memory_space_constraint(a, pltpu.HBM) for a in list(srcs) + lands], *extra)
    return (res[0], res[1], list(res[2:2 + n]), list(res[2 + n:2 + 2 * n]), scatter), res[-1]


def _exchange_wait(handle, after, *, name):
    send_sems, recv_sems, srcs, lands, scatter = handle
    n = len(srcs)

    def body(*refs):
        src_refs, land_refs = refs[:n], refs[n:2 * n]
        send_sems, recv_sems = refs[2 * n], refs[2 * n + 1]
        x, y, c = _my_place()
        for i in range(n):
            for r in range(N_PEERS):
                src = src_refs[i].at[0] if scatter else src_refs[i]
                cp = pltpu.make_async_remote_copy(
                    src_ref=src, dst_ref=land_refs[i].at[0], send_sem=send_sems.at[N_PEERS * i + r],
                    recv_sem=recv_sems.at[N_PEERS * i + r], device_id=(x, y, c), device_id_type=MESH)
                cp.wait_send()
                cp.wait_recv()

    thru = [pltpu.HBM(a.shape, a.dtype) for a in srcs + lands]
    res = pl.pallas_call(
        body, name=name, out_shape=tuple(thru),
        in_specs=[HBM_SPEC] * (2 * n) + [SEM_SPEC, SEM_SPEC, ANY], out_specs=tuple([HBM_SPEC] * (2 * n)),
        input_output_aliases={i: i for i in range(2 * n)},
        compiler_params=pltpu.CompilerParams(has_side_effects=DATAFLOW_EFFECT),
    )(*srcs, *lands, send_sems, recv_sems, after)
    return list(res[:n]), list(res[n:])


def _with_own(land, own, me):
    return lax.dynamic_update_index_in_dim(land, own, me, 0)


def _adamw_math(w, g, m, v):
    m = ADAM_B1 * m + (1.0 - ADAM_B1) * g
    v = ADAM_B2 * v + (1.0 - ADAM_B2) * (g * g)
    m_hat = m / (1.0 - ADAM_B1 ** ADAM_STEP)
    v_hat = v / (1.0 - ADAM_B2 ** ADAM_STEP)
    delta = -ADAM_LR * (m_hat / (jnp.sqrt(v_hat) + ADAM_EPS) + ADAM_WD * w)
    return delta, m, v


def _adamw_sum(parts, w, m, v, *, name):
    shape = w.shape
    R, n = shape[-2], shape[-1]
    w, m, v = (t.reshape(R, n) for t in (w, m, v))
    tr = _pick(R, (256, 464, 352, 128))

    def body(p_ref, w_ref, m_ref, v_ref, g_ref, d_ref, mo_ref, vo_ref):
        g = p_ref[0].astype(F32)
        for p in range(1, N_DEV):
            g = g + p_ref[p].astype(F32)
        d, mn, vn = _adamw_math(w_ref[...], g, m_ref[...], v_ref[...])
        g_ref[...] = g
        d_ref[...] = d
        mo_ref[...] = mn
        vo_ref[...] = vn

    row = pl.BlockSpec((tr, n), lambda i: (i, 0))
    outs = pl.pallas_call(
        body, name=name, grid=(R // tr,),
        in_specs=[pl.BlockSpec((N_DEV, tr, n), lambda i: (0, i, 0)), row, row, row],
        out_specs=[row, row, row, row],
        out_shape=[jax.ShapeDtypeStruct((R, n), F32)] * 4,
        compiler_params=_params("parallel"),
    )(parts, w, m, v)
    return [t.reshape(shape) for t in outs]


def _lb_bwd(dlb, lb, *, name):
    def body(d_ref, lb_ref, o_ref):
        t = d_ref[...] * lb_ref[...] * (1.0 - lb_ref[...])
        o_ref[0:1, :] = t
        o_ref[1:2, :] = -t

    return pl.pallas_call(body, name=name, out_shape=jax.ShapeDtypeStruct((2, lb.shape[1]), F32))(dlb, lb)


DOWN_BLK, ROW_BLK = D_FF // N_DEV, D_MODEL // N_DEV
W_FFN_BLK = 2 * D_FF // N_DEV
CONV_BITS_SHAPE = (16, 256)


def kernel(x, positions, norm1_g, w_in, lb_logits, hgrn_norm_g, w_a, attn_sinks, w_b, w_out, norm2_g, w_ffn_in, conv_w, conv_b, w_down, final_g, loss_target, m_norm1_g, m_w_in, m_lb_logits, m_hgrn_norm_g, m_w_a, m_attn_sinks, m_w_b, m_w_out, m_norm2_g, m_w_ffn_in, m_conv_w, m_conv_b, m_w_down, m_final_g, v_norm1_g, v_w_in, v_lb_logits, v_hgrn_norm_g, v_w_a, v_attn_sinks, v_w_b, v_w_out, v_norm2_g, v_w_ffn_in, v_conv_w, v_conv_b, v_w_down, v_final_g):
    xi, yi, ci = _my_place()
    dev = 4 * xi + 2 * yi + ci

    tr = lambda t: jnp.transpose(t[0])
    untr = lambda t: jnp.transpose(t)[None]
    w_in_blocks = _all_gather(tr(w_in).astype(BF16), name="ag_w_in")
    conv_bits = lax.bitcast_convert_type(conv_w, BF16).reshape(-1)
    conv_bits = jnp.pad(conv_bits, (0, CONV_BITS_SHAPE[0] * CONV_BITS_SHAPE[1] - conv_bits.shape[0])).reshape(CONV_BITS_SHAPE)
    gather_handles = {}
    gather_handles["mix"], tok_mix = _exchange_start([w_a[0].astype(BF16), w_b[0].astype(BF16), w_out[0].astype(BF16)], False,
                                                     after=w_in_blocks, name="ag_mix_start")
    gather_handles["ffn"], tok_ffn = _exchange_start([tr(w_ffn_in).astype(BF16), w_down[0].astype(BF16), conv_bits], False,
                                                     after=tok_mix, name="ag_ffn_start")
    start_token = tok_mix + tok_ffn

    def rest_weights(group, after):
        own, lands = _exchange_wait(gather_handles[group], after, name="ag_" + group + "_wait")
        full = [_with_own(l, o, dev) for l, o in zip(lands, own)]
        if group == "mix":
            return dict(zip(("w_a", "w_b", "w_out"), [t.reshape(D_MODEL, D_MODEL) for t in full]))
        bits = full[2].reshape(N_DEV, -1)[:, :3 * CONVW_BLK * 2].reshape(N_DEV, 3, CONVW_BLK, 2)
        return dict(w_ffn_t=full[0].reshape(2 * D_FF, D_MODEL), w_down=full[1].reshape(D_FF, D_MODEL),
                    conv_w=lax.bitcast_convert_type(bits, F32).transpose(1, 0, 2).reshape(3, D_FF))

    handles = {}

    def emit(group, gr):
        if group == "ffn":
            srcs = [gr["w_ffn_t"].reshape(N_DEV, W_FFN_BLK, D_MODEL), gr["w_down"].reshape(N_DEV, DOWN_BLK, D_MODEL)]
        elif group == "mix":
            srcs = [gr[n].reshape(N_DEV, ROW_BLK, D_MODEL) for n in ("w_out", "w_a", "w_b")]
        else:
            srcs = [gr["w_in_t"].reshape(N_DEV, W_IN_BLK, D_MODEL)]
        handles[group], token = _exchange_start(srcs, True, name="rs_" + group + "_start")
        return token

    small = dict(norm1_g=norm1_g, lb_logits=lb_logits, hgrn_norm_g=hgrn_norm_g, attn_sinks=attn_sinks, norm2_g=norm2_g,
                 conv_b=conv_b, final_g=final_g)
    w_in_full_t = _reordered_rows(w_in_blocks.reshape(W_IN, D_MODEL), name="w_in_rows")
    loss, grad_x, g = _local_step(x, positions, loss_target, small, w_in_full_t, rest_weights, emit, start_token)

    def parts_of(group, after):
        srcs, lands = _exchange_wait(handles[group], after, name="rs_" + group + "_wait")
        return [_with_own(l, lax.dynamic_index_in_dim(s, dev, 0, keepdims=False), dev) for s, l in zip(srcs, lands)]

    p_ffn, p_down = parts_of("ffn", grad_x)
    p_out, p_a, p_b = parts_of("mix", grad_x)
    (p_in,) = parts_of("in", grad_x)
    big = dict(
        w_in=[untr(t) for t in _adamw_sum(p_in, tr(w_in), tr(m_w_in), tr(v_w_in), name="adamw_w_in")],
        w_a=_adamw_sum(p_a, w_a, m_w_a, v_w_a, name="adamw_w_a"),
        w_b=_adamw_sum(p_b, w_b, m_w_b, v_w_b, name="adamw_w_b"),
        w_out=_adamw_sum(p_out, w_out, m_w_out, v_w_out, name="adamw_w_out"),
        w_ffn_in=[untr(t) for t in _adamw_sum(p_ffn, tr(w_ffn_in), tr(m_w_ffn_in), tr(v_w_ffn_in), name="adamw_w_ffn_in")],
        w_down=_adamw_sum(p_down, w_down, m_w_down, v_w_down, name="adamw_w_down"),
    )

    row = lambda t: t.reshape(1, -1) if t.ndim == 1 else t
    shard = lambda t: t.reshape(3, CONVW_BLK)
    sm_g = {nm: g[nm] for nm in SMALL_ROWS}
    sm_w = dict(norm1_g=norm1_g, lb_logits=lb_logits, hgrn_norm_g=hgrn_norm_g, attn_sinks=attn_sinks, norm2_g=norm2_g,
                conv_b=conv_b, final_g=row(final_g), conv_w=shard(conv_w))
    sm_m = dict(norm1_g=m_norm1_g, lb_logits=m_lb_logits, hgrn_norm_g=m_hgrn_norm_g, attn_sinks=m_attn_sinks, norm2_g=m_norm2_g,
                conv_b=m_conv_b, final_g=row(m_final_g), conv_w=shard(m_conv_w))
    sm_v = dict(norm1_g=v_norm1_g, lb_logits=v_lb_logits, hgrn_norm_g=v_hgrn_norm_g, attn_sinks=v_attn_sinks, norm2_g=v_norm2_g,
                conv_b=v_conv_b, final_g=row(v_final_g), conv_w=shard(v_conv_w))
    loss_total, sm_out = _small_step(sm_g, g["conv_w"], loss, sm_w, sm_m, sm_v, dev.astype(jnp.int32).reshape(1), name="small_step")
    shapes = dict(final_g=final_g.shape, conv_w=conv_w.shape)

    names = ("norm1_g", "w_in", "lb_logits", "hgrn_norm_g", "w_a", "attn_sinks", "w_b", "w_out", "norm2_g", "w_ffn_in", "conv_w", "conv_b", "w_down", "final_g")
    outs = [loss_total.reshape(()), grad_x]
    for kind in range(4):
        outs += [big[n][kind] if n in big else sm_out[n][kind].reshape(shapes.get(n, sm_out[n][kind].shape)) for n in names]
    return tuple(outs)
```

```python
import functools

import jax
import jax.numpy as jnp
from jax import lax
from jax.experimental import pallas as pl
from jax.experimental.pallas import tpu as pltpu

F32 = jnp.float32
BF16 = jnp.bfloat16

D_MODEL = 1024
HGRN_HEADS = 8
HGRN_DK = 128
CHUNK = 64
ATT_HEADS = 16
ATT_KV_HEADS = 2
ATT_HD = 64
ATT_GROUP = ATT_HEADS // ATT_KV_HEADS
WINDOW = 128
ROPE_DIM = ATT_HD // 4
ROPE_THETA = 500000.0
D_FF = 2816
EPS = 1e-6
NEG_INF = -1e30
N_DEV = 8

ADAM_LR = 0.001
ADAM_B1 = 0.9
ADAM_B2 = 0.999
ADAM_EPS = 1e-08
ADAM_WD = 0.01
ADAM_STEP = 10

MESH = pl.DeviceIdType.MESH
ANY = pl.BlockSpec(memory_space=pl.ANY)


def _pick(n, cands):
    for c in cands:
        if n % c == 0:
            return c
    return n


def _sigmoid(x):
    return 0.5 * jnp.tanh(0.5 * x) + 0.5


def _silu(x):
    hx = 0.5 * x
    return hx * jnp.tanh(hx) + hx


def _rms(x, g):
    return x * lax.rsqrt(jnp.mean(x * x, axis=-1, keepdims=True) + EPS) * g


def _dot(a, b, dims):
    return lax.dot_general(a, b, (dims, ((), ())), preferred_element_type=F32)


def _nn(a, b):
    return _dot(a, b, ((1,), (0,)))


def _nt(a, b):
    return _dot(a, b, ((1,), (1,)))


def _tn(a, b):
    return _dot(a, b, ((0,), (0,)))


def _params(*sem):
    return pltpu.CompilerParams(dimension_semantics=sem, vmem_limit_bytes=56 * 1024 * 1024)


def _matmul(a, b, *, ta=False, tb=False, out_dtype=F32, addend=None, after=None, into=None, o_noff=0, out_t=False,
            o_block_perm=lambda j: j, name, tm, tn, tk=None, n_extent=None, b_koff=0, b_noff=0):
    M, K = (a.shape[1], a.shape[0]) if ta else a.shape
    N = n_extent or (b.shape[0] if tb else b.shape[1])
    tm, tn, tk = min(tm, M), min(tn, N), min(tk or K, K)
    assert M % tm == 0 and N % tn == 0 and K % tk == 0, (name, M, N, K, tm, tn, tk)
    nk = K // tk
    use_scratch = nk > 1 and out_dtype != F32
    grid = (M // tm, N // tn, nk)
    a_spec = pl.BlockSpec((tk, tm), lambda i, j, k: (k, i)) if ta else pl.BlockSpec((tm, tk), lambda i, j, k: (i, k))
    b_spec = pl.BlockSpec((tn, tk), lambda i, j, k: (j + b_noff, k + b_koff)) if tb else pl.BlockSpec((tk, tn), lambda i, j, k: (k + b_koff, j + b_noff))
    o_spec = pl.BlockSpec((tm, tn), lambda i, j, k: (i, j))
    dims = ((0 if ta else 1,), (1 if tb else 0,))
    has_add = addend is not None

    n_in = 2 + has_add + (after is not None) + (into is not None)

    def body(*refs):
        a_ref, b_ref = refs[:2]
        c_ref = refs[2] if has_add else None
        o_ref = refs[n_in]
        part = _dot(a_ref[...], b_ref[...], dims)
        if nk == 1:
            if has_add:
                part = part + c_ref[...].astype(F32)
            o_ref[...] = (part.T if out_t else part).astype(out_dtype)
        else:
            acc_ref = refs[-1] if use_scratch else o_ref
            k = pl.program_id(2)

            @pl.when(k == 0)
            def _():
                acc_ref[...] = part + c_ref[...].astype(F32) if has_add else part

            @pl.when(k > 0)
            def _():
                acc_ref[...] += part

            if use_scratch:
                @pl.when(k == nk - 1)
                def _():
                    o_ref[...] = acc_ref[...].astype(out_dtype)

    in_specs = [a_spec, b_spec] + ([o_spec] if has_add else [])
    args = (a, b) + ((addend,) if has_add else ())
    if after is not None:
        in_specs.append(pl.BlockSpec(after.shape, lambda i, j, k: (0, 0)))
        args += (after,)
    aliases = {}
    if into is not None:
        in_specs.append(ANY)
        args += (into,)
        aliases = {len(args) - 1: 0}
    if out_t:
        assert nk == 1 and not has_add
        o_spec = pl.BlockSpec((tn, tm), lambda i, j, k: (o_block_perm(j) + o_noff, i))
    elif into is not None:
        o_spec = pl.BlockSpec((tm, tn), lambda i, j, k: (i, j + o_noff))
    return pl.pallas_call(
        body,
        name=name,
        grid=grid,
        in_specs=in_specs,
        out_specs=o_spec,
        out_shape=jax.ShapeDtypeStruct(into.shape if into is not None else ((N, M) if out_t else (M, N)), out_dtype),
        input_output_aliases=aliases,
        scratch_shapes=[pltpu.VMEM((tm, tn), F32)] if use_scratch else [],
        compiler_params=_params("parallel", "parallel", "arbitrary"),
    )(*args)


def _row_spec(tm, n):
    return pl.BlockSpec((tm, n), lambda i: (i, 0))


def _full_spec(shape):
    return pl.BlockSpec(shape, lambda i: tuple(0 for _ in shape))


def _norm_cast(x, g, *, name):
    T, D = x.shape
    tm = _pick(T, (512, 256, 128))

    def body(x_ref, g_ref, u_ref):
        u_ref[...] = _rms(x_ref[...], g_ref[...]).astype(BF16)

    return pl.pallas_call(
        body, name=name, grid=(T // tm,),
        in_specs=[_row_spec(tm, D), _full_spec((1, D))],
        out_specs=_row_spec(tm, D),
        out_shape=jax.ShapeDtypeStruct((T, D), BF16),
        compiler_params=_params("parallel"),
    )(x, g)


def _norm_bwd_add(x, g, du, dres, *, with_bf16=True, name):
    T, D = x.shape
    tm = _pick(T, (512, 256, 128))

    def body(x_ref, g_ref, du_ref, dr_ref, dx_ref, *rest):
        dg_ref = rest[-1]
        _, vjp = jax.vjp(_rms, x_ref[...], g_ref[...])
        dx, dg = vjp(du_ref[...].astype(F32))
        dx = dx + dr_ref[...]
        dx_ref[...] = dx
        if with_bf16:
            rest[0][...] = dx.astype(BF16)

        @pl.when(pl.program_id(0) == 0)
        def _():
            dg_ref[...] = jnp.zeros_like(dg_ref)

        dg_ref[...] += dg

    row = _row_spec(tm, D)
    return pl.pallas_call(
        body, name=name, grid=(T // tm,),
        in_specs=[row, _full_spec((1, D)), row, row],
        out_specs=[row] + ([row] if with_bf16 else []) + [_full_spec((1, D))],
        out_shape=[jax.ShapeDtypeStruct((T, D), F32)] + ([jax.ShapeDtypeStruct((T, D), BF16)] if with_bf16 else []) + [jax.ShapeDtypeStruct((1, D), F32)],
        compiler_params=_params("arbitrary"),
    )(x, g, du, dres)


def _final_loss_bwd(h2, g, target, *, name):
    T, D = h2.shape
    tm = _pick(T, (512, 256, 128))

    def body(h_ref, g_ref, t_ref, dx_ref, dxb_ref, dg_ref, loss_ref):
        y, vjp = jax.vjp(_rms, h_ref[...], g_ref[...])
        err = y - t_ref[...]
        dx, dg = vjp(err * (1.0 / D))
        dx_ref[...] = dx
        dxb_ref[...] = dx.astype(BF16)

        @pl.when(pl.program_id(0) == 0)
        def _():
            dg_ref[...] = jnp.zeros_like(dg_ref)
            loss_ref[...] = jnp.zeros_like(loss_ref)

        dg_ref[...] += dg
        loss_ref[...] += (0.5 / D) * jnp.sum(jnp.sum(err * err, axis=1, keepdims=True), axis=0, keepdims=True)

    return pl.pallas_call(
        body, name=name, grid=(T // tm,),
        in_specs=[_row_spec(tm, D), _full_spec((1, D)), _row_spec(tm, D)],
        out_specs=[_row_spec(tm, D), _row_spec(tm, D), _full_spec((1, D)), _full_spec((1, 1))],
        out_shape=[jax.ShapeDtypeStruct((T, D), F32), jax.ShapeDtypeStruct((T, D), BF16), jax.ShapeDtypeStruct((1, D), F32), jax.ShapeDtypeStruct((1, 1), F32)],
        compiler_params=_params("arbitrary"),
    )(h2, g, target)


def _merge_fn(gates, a, b):
    ga = gates[:, :D_MODEL].astype(F32)
    gb = gates[:, D_MODEL:].astype(F32)
    return _sigmoid(ga) * a.astype(F32) + _sigmoid(gb) * b.astype(F32)


def _gates_spec(tm):
    return pl.BlockSpec((tm, W_GATES), lambda i: (i, O_GATES // W_GATES))


def _merge_fwd(z, a, b, *, name):
    T = a.shape[0]
    tm = _pick(T, (512, 256, 128))

    def body(g_ref, a_ref, b_ref, o_ref):
        o_ref[...] = _merge_fn(g_ref[...], a_ref[...], b_ref[...]).astype(BF16)

    return pl.pallas_call(
        body, name=name, grid=(T // tm,),
        in_specs=[_gates_spec(tm), _row_spec(tm, D_MODEL), _row_spec(tm, D_MODEL)],
        out_specs=_row_spec(tm, D_MODEL),
        out_shape=jax.ShapeDtypeStruct((T, D_MODEL), BF16),
        compiler_params=_params("parallel"),
    )(z, a, b)


def _merge_bwd(z, a, b, dmerged, dz, *, name):
    T = a.shape[0]
    tm = _pick(T, (512, 256, 128))

    def body(g_ref, a_ref, b_ref, dm_ref, dz_in, dg_ref, da_ref, db_ref):
        g = g_ref[...].astype(F32)
        dm = dm_ref[...].astype(F32)
        sa = _sigmoid(g[:, :D_MODEL])
        sb = _sigmoid(g[:, D_MODEL:])
        da_ref[...] = (dm * sa).astype(BF16)
        db_ref[...] = (dm * sb).astype(BF16)
        dg_ref[:, :D_MODEL] = (dm * a_ref[...].astype(F32) * sa * (1.0 - sa)).astype(BF16)
        dg_ref[:, D_MODEL:] = (dm * b_ref[...].astype(F32) * sb * (1.0 - sb)).astype(BF16)

    return pl.pallas_call(
        body, name=name, grid=(T // tm,),
        in_specs=[_gates_spec(tm), _row_spec(tm, D_MODEL), _row_spec(tm, D_MODEL), _row_spec(tm, D_MODEL), ANY],
        out_specs=[_gates_spec(tm), _row_spec(tm, D_MODEL), _row_spec(tm, D_MODEL)],
        out_shape=[jax.ShapeDtypeStruct(dz.shape, BF16), jax.ShapeDtypeStruct((T, D_MODEL), BF16), jax.ShapeDtypeStruct((T, D_MODEL), BF16)],
        input_output_aliases={4: 0},
        compiler_params=_params("parallel"),
    )(z, a, b, dmerged, dz)


CONV_TC = 256


def _shift_down(x, n, rows):
    return jnp.where(rows >= n, pltpu.roll(x, n, 0), 0.0)


def _shift_up(x, n, rows, S):
    return jnp.where(rows < S - n, pltpu.roll(x, S - n, 0), 0.0)


def _conv_act_fwd(gu, conv_w, conv_b, *, name):
    B, S, _ = gu.shape
    tc = CONV_TC
    nc = D_FF // tc

    def body(g_ref, up_ref, w_ref, b_ref, o_ref, a_ref):
        g = g_ref[...].astype(F32)
        rows = lax.broadcasted_iota(jnp.int32, g.shape, 0)
        w = w_ref[...]
        a = w[2:3] * g + w[1:2] * _shift_down(g, 1, rows) + w[0:1] * _shift_down(g, 2, rows) + b_ref[...]
        o_ref[...] = (_silu(a) * up_ref[...].astype(F32)).astype(BF16)
        a_ref[...] = a.astype(BF16)

    col = pl.BlockSpec((None, S, tc), lambda b, j: (b, 0, j))
    return pl.pallas_call(
        body, name=name, grid=(B, nc),
        in_specs=[col,
                  pl.BlockSpec((None, S, tc), lambda b, j: (b, 0, j + nc)),
                  pl.BlockSpec((3, tc), lambda b, j: (0, j)),
                  pl.BlockSpec((1, tc), lambda b, j: (0, j))],
        out_specs=[col, col],
        out_shape=[jax.ShapeDtypeStruct((B, S, D_FF), BF16)] * 2,
        compiler_params=_params("parallel", "parallel"),
    )(gu, gu, conv_w, conv_b)


def _conv_act_bwd(gu, a_pre, conv_w, dact, *, name):
    B, S, _ = gu.shape
    tc = CONV_TC
    nc = D_FF // tc

    def body(g_ref, up_ref, a_ref, w_ref, da_ref, dg_ref, dup_ref, dw_ref, db_ref):
        g = g_ref[...].astype(F32)
        up = up_ref[...].astype(F32)
        a = a_ref[...].astype(F32)
        dact = da_ref[...].astype(F32)
        rows = lax.broadcasted_iota(jnp.int32, g.shape, 0)
        w = w_ref[...]
        sg = _sigmoid(a)
        dup_ref[...] = (dact * a * sg).astype(BF16)
        da = dact * up * sg * (1.0 + a * (1.0 - sg))
        da1 = _shift_up(da, 1, rows, S)
        da2 = _shift_up(da, 2, rows, S)
        dg_ref[...] = (w[2:3] * da + w[1:2] * da1 + w[0:1] * da2).astype(BF16)

        @pl.when(pl.program_id(1) == 0)
        def _():
            dw_ref[...] = jnp.zeros_like(dw_ref)
            db_ref[...] = jnp.zeros_like(db_ref)

        dw_ref[0:1, :] += jnp.sum(da2 * g, axis=0, keepdims=True)
        dw_ref[1:2, :] += jnp.sum(da1 * g, axis=0, keepdims=True)
        dw_ref[2:3, :] += jnp.sum(da * g, axis=0, keepdims=True)
        db_ref[...] += jnp.sum(da, axis=0, keepdims=True)

    col = pl.BlockSpec((None, S, tc), lambda j, b: (b, 0, j))
    return pl.pallas_call(
        body, name=name, grid=(nc, B),
        in_specs=[col,
                  pl.BlockSpec((None, S, tc), lambda j, b: (b, 0, j + nc)),
                  col,
                  pl.BlockSpec((3, tc), lambda j, b: (0, j)),
                  col],
        out_specs=[col, col, pl.BlockSpec((3, tc), lambda j, b: (0, j)), pl.BlockSpec((1, tc), lambda j, b: (0, j))],
        out_shape=[jax.ShapeDtypeStruct((B, S, D_FF), BF16), jax.ShapeDtypeStruct((B, S, D_FF), BF16),
                   jax.ShapeDtypeStruct((3, D_FF), F32), jax.ShapeDtypeStruct((1, D_FF), F32)],
        compiler_params=_params("parallel", "arbitrary"),
    )(gu, gu, a_pre, conv_w, dact)


HGRN_CPB = 4
HF = HGRN_HEADS * HGRN_DK


def _tri(n, upper=False):
    r = lax.broadcasted_iota(jnp.int32, (n, n), 0)
    c = lax.broadcasted_iota(jnp.int32, (n, n), 1)
    return (c >= r) if upper else (r >= c)


def _hs(h):
    return slice(h * HGRN_DK, (h + 1) * HGRN_DK)


def _cumsum_rows(tri_b, x):
    hi = x.astype(BF16)
    lo = (x - hi.astype(F32)).astype(BF16)
    return _nn(tri_b, hi) + _nn(tri_b, lo)


def _hgrn_pre(q, fz, lb, tril_b):
    qf = _silu(q)
    sg = _sigmoid(fz)
    f = lb + (1.0 - lb) * sg
    k = 1.0 - f
    b = _cumsum_rows(tril_b, jnp.log2(f))
    bref = b[CHUNK // 2:CHUNK // 2 + 1, :]
    blast = b[CHUNK - 1:CHUNK, :]
    e1 = jnp.exp2(b - bref)
    e2 = jnp.exp2(bref - b)
    e3 = e1 * jnp.exp2(bref)
    e4 = e2 * jnp.exp2(blast - bref)
    dec = jnp.exp2(blast)
    return sg, f, (e1, e2, e3, e4), qf * e1, k * e2, qf * e3, k * e4, dec


def _hgrn_fwd(zh, lb, gn, *, name):
    B, S, _ = zh.shape
    cpb = HGRN_CPB
    ts = cpb * CHUNK
    nblk = S // ts

    def body(z_ref, lb_ref, gn_ref, o_ref, st_ref, state):
        @pl.when(pl.program_id(1) == 0)
        def _():
            state[...] = jnp.zeros_like(state)

        H = HGRN_HEADS
        causal = _tri(CHUNK)
        tril_b = causal.astype(BF16)
        lb = lb_ref[...]
        for c in range(cpb):
            rows = slice(c * CHUNK, (c + 1) * CHUNK)
            q = z_ref[rows, 0:HF].astype(F32)
            fz = z_ref[rows, HF:2 * HF].astype(F32)
            v = z_ref[rows, 2 * HF:3 * HF]
            hg = z_ref[rows, 3 * HF:4 * HF].astype(F32)
            _, _, _, q_in, k_in, q_out, k_st, dec = _hgrn_pre(q, fz, lb, tril_b)
            q_in, k_in, q_out, k_st = (t.astype(BF16) for t in (q_in, k_in, q_out, k_st))
            a = [jnp.where(causal, _nt(q_in[:, _hs(h)], k_in[:, _hs(h)]), 0.0).astype(BF16) for h in range(H)]
            st = [state[h] for h in range(H)]
            for h in range(H):
                st_ref[c, h] = st[h]
            o = [_nn(a[h], v[:, _hs(h)]) + _nt(q_out[:, _hs(h)], st[h].astype(BF16)) for h in range(H)]
            for h in range(H):
                state[h] = st[h] * dec[:, _hs(h)] + _tn(v[:, _hs(h)], k_st[:, _hs(h)])
            gate = _silu(hg)
            for h in range(H):
                o_ref[rows, _hs(h)] = (_rms(o[h], gn_ref[...]) * gate[:, _hs(h)]).astype(BF16)

    return pl.pallas_call(
        body, name=name, grid=(B, nblk),
        in_specs=[pl.BlockSpec((None, ts, 4 * HF), lambda b, s: (b, s, 0)),
                  pl.BlockSpec((1, HF), lambda b, s: (0, 0)),
                  pl.BlockSpec((1, HGRN_DK), lambda b, s: (0, 0))],
        out_specs=[pl.BlockSpec((None, ts, HF), lambda b, s: (b, s, 0)),
                   pl.BlockSpec((None, cpb, HGRN_HEADS, HGRN_DK, HGRN_DK), lambda b, s: (b, s, 0, 0, 0))],
        out_shape=[jax.ShapeDtypeStruct((B, S, HF), BF16),
                   jax.ShapeDtypeStruct((B, S // CHUNK, HGRN_HEADS, HGRN_DK, HGRN_DK), F32)],
        scratch_shapes=[pltpu.VMEM((HGRN_HEADS, HGRN_DK, HGRN_DK), F32)],
        compiler_params=_params("arbitrary", "arbitrary"),
    )(zh, lb, gn)


def _hgrn_bwd(zh, lb, gn, states, doa, dz, *, name):
    B, S, _ = zh.shape
    cpb = HGRN_CPB
    ts = cpb * CHUNK
    nblk = S // ts
    rev = lambda b, s: (b, nblk - 1 - s, 0)

    def body(z_ref, lb_ref, gn_ref, st_ref, do_ref, dz_in, dz_ref, dlb_ref, dgn_ref, dstate):
        @pl.when(pl.program_id(1) == 0)
        def _():
            dstate[...] = jnp.zeros_like(dstate)

        @pl.when((pl.program_id(0) == 0) & (pl.program_id(1) == 0))
        def _():
            dlb_ref[...] = jnp.zeros_like(dlb_ref)
            dgn_ref[...] = jnp.zeros_like(dgn_ref)

        H = HGRN_HEADS
        cat = lambda xs: jnp.concatenate(xs, axis=1)
        causal = _tri(CHUNK)
        tril_b = causal.astype(BF16)
        triu_b = _tri(CHUNK, upper=True).astype(BF16)
        rowid = lax.broadcasted_iota(jnp.int32, (CHUNK, HF), 0)
        lb = lb_ref[...]
        gn = gn_ref[...]
        for c in reversed(range(cpb)):
            rows = slice(c * CHUNK, (c + 1) * CHUNK)
            q = z_ref[rows, 0:HF].astype(F32)
            fz = z_ref[rows, HF:2 * HF].astype(F32)
            v = z_ref[rows, 2 * HF:3 * HF]
            hg = z_ref[rows, 3 * HF:4 * HF].astype(F32)
            sg, f, (e1, e2, e3, e4), q_in, k_in, q_out, k_st, dec = _hgrn_pre(q, fz, lb, tril_b)
            q_in_b, k_in_b, q_out_b, k_st_b = (t.astype(BF16) for t in (q_in, k_in, q_out, k_st))
            a_b = [jnp.where(causal, _nt(q_in_b[:, _hs(h)], k_in_b[:, _hs(h)]), 0.0).astype(BF16) for h in range(H)]
            st = [st_ref[c, h] for h in range(H)]
            st_b = [t.astype(BF16) for t in st]
            o = [_nn(a_b[h], v[:, _hs(h)]) + _nt(q_out_b[:, _hs(h)], st_b[h]) for h in range(H)]
            dout = do_ref[rows, :].astype(F32)
            shg = _sigmoid(hg)
            gate = hg * shg
            do_l, dgn_acc = [], jnp.zeros_like(gn)
            for h in range(H):
                _, norm_vjp = jax.vjp(_rms, o[h], gn)
                d_o, d_gn = norm_vjp(dout[:, _hs(h)] * gate[:, _hs(h)])
                do_l.append(d_o)
                dgn_acc = dgn_acc + d_gn
            dgn_ref[...] += dgn_acc
            on = cat([_rms(o[h], gn) for h in range(H)])
            dhg = dout * on * shg * (1.0 + hg * (1.0 - shg))
            do_b = [t.astype(BF16) for t in do_l]
            dst = [dstate[h] for h in range(H)]
            dst_b = [t.astype(BF16) for t in dst]
            da_b = [jnp.where(causal, _nt(do_b[h], v[:, _hs(h)]), 0.0).astype(BF16) for h in range(H)]
            dv = cat([_tn(a_b[h], do_b[h]) + _nt(k_st_b[:, _hs(h)], dst_b[h]) for h in range(H)])
            dq_in = cat([_nn(da_b[h], k_in_b[:, _hs(h)]) for h in range(H)])
            dk_in = cat([_tn(da_b[h], q_in_b[:, _hs(h)]) for h in range(H)])
            dq_out = cat([_nn(do_b[h], st_b[h]) for h in range(H)])
            dk_st = cat([_nn(v[:, _hs(h)], dst_b[h]) for h in range(H)])
            ddec = cat([jnp.sum(st[h] * dst[h], axis=0, keepdims=True) for h in range(H)])
            for h in range(H):
                dstate[h] = dst[h] * dec[:, _hs(h)] + _tn(do_b[h], q_out_b[:, _hs(h)])
            t_qin = dq_in * q_in
            t_kin = dk_in * k_in
            t_kst = dk_st * k_st
            db = t_qin - t_kin + dq_out * q_out - t_kst
            dbref = jnp.sum(t_kin - t_qin, axis=0, keepdims=True)
            dblast = jnp.sum(t_kst, axis=0, keepdims=True) + ddec * dec
            db = db + jnp.where(rowid == CHUNK // 2, dbref, 0.0) + jnp.where(rowid == CHUNK - 1, dblast, 0.0)
            dlogf = _cumsum_rows(triu_b, db)
            dqf = dq_in * e1 + dq_out * e3
            dk = dk_in * e2 + dk_st * e4
            df = dlogf / f - dk
            dfz = df * (1.0 - lb) * sg * (1.0 - sg)
            dlb_ref[...] += jnp.sum(df * (1.0 - sg), axis=0, keepdims=True)
            sq = _sigmoid(q)
            dq = dqf * sq * (1.0 + q * (1.0 - sq))
            dz_ref[rows, 0:HF] = dq.astype(BF16)
            dz_ref[rows, HF:2 * HF] = dfz.astype(BF16)
            dz_ref[rows, 2 * HF:3 * HF] = dv.astype(BF16)
            dz_ref[rows, 3 * HF:4 * HF] = dhg.astype(BF16)

    return pl.pallas_call(
        body, name=name, grid=(B, nblk),
        in_specs=[pl.BlockSpec((None, ts, 4 * HF), rev),
                  pl.BlockSpec((1, HF), lambda b, s: (0, 0)),
                  pl.BlockSpec((1, HGRN_DK), lambda b, s: (0, 0)),
                  pl.BlockSpec((None, cpb, HGRN_HEADS, HGRN_DK, HGRN_DK), lambda b, s: (b, nblk - 1 - s, 0, 0, 0)),
                  pl.BlockSpec((None, ts, HF), rev),
                  ANY],
        out_specs=[pl.BlockSpec((None, ts, 4 * HF), rev),
                   pl.BlockSpec((1, HF), lambda b, s: (0, 0)),
                   pl.BlockSpec((1, HGRN_DK), lambda b, s: (0, 0))],
        out_shape=[jax.ShapeDtypeStruct(dz.shape, BF16),
                   jax.ShapeDtypeStruct((1, HF), F32),
                   jax.ShapeDtypeStruct((1, HGRN_DK), F32)],
        input_output_aliases={5: 0},
        scratch_shapes=[pltpu.VMEM((HGRN_HEADS, HGRN_DK, HGRN_DK), F32)],
        compiler_params=_params("arbitrary", "arbitrary"),
    )(zh, lb, gn, states, doa, dz)


KV_W = ATT_KV_HEADS * ATT_HD
ATT_SCALE = ATT_HD ** -0.5


def _rope(x, cos, sin, inverse=False):
    half = ROPE_DIM // 2
    outs = []
    for p in range(x.shape[1] // 128):
        xp = x[:, p * 128:(p + 1) * 128]
        lane = lax.broadcasted_iota(jnp.int32, xp.shape, 1) % ATT_HD
        sw = jnp.where(lane < half, pltpu.roll(xp, 128 - half, 1), pltpu.roll(xp, half, 1))
        outs.append(xp * cos - sw * sin if inverse else xp * cos + sw * sin)
    return outs[0] if len(outs) == 1 else jnp.concatenate(outs, axis=1)


PAIRS_PER_KV = ATT_GROUP // 2


def _swap_halves(x):
    return pltpu.roll(x, ATT_HD, 1)


def _kv_padded(t, low):
    sw = _swap_halves(t)
    zero = jnp.zeros_like(t)
    out = []
    for g in range(ATT_KV_HEADS):
        in_low, in_high = (t, sw) if g == 0 else (sw, t)
        out.append((jnp.where(low, in_low, zero).astype(BF16), jnp.where(low, zero, in_high).astype(BF16)))
    return out


def _swa_mask(first_block):
    qi = lax.broadcasted_iota(jnp.int32, (WINDOW, 2 * WINDOW), 0)
    mi = lax.broadcasted_iota(jnp.int32, (WINDOW, 2 * WINDOW), 1)
    band = (mi > qi) & (mi <= qi + WINDOW)
    return band & (jnp.logical_not(first_block) | (mi >= WINDOW))


def _swa_specs(nb):
    cur = lambda b, i: (b, i, 0)
    prev = lambda b, i: (b, jnp.maximum(i - 1, 0), 0)
    return cur, prev


def _swa_z_specs():
    q = pl.BlockSpec((None, WINDOW, W_AQ), lambda b, i: (b, i, O_AQ // W_AQ))
    kv_prev = pl.BlockSpec((None, WINDOW, W_AKV), lambda b, i: (b, jnp.maximum(i - 1, 0), O_AKV // W_AKV))
    kv_cur = pl.BlockSpec((None, WINDOW, W_AKV), lambda b, i: (b, i, O_AKV // W_AKV))
    return q, kv_prev, kv_cur


def _swa_fwd(z, cos, sin, sinks, *, name):
    B, S, _ = z.shape
    nb = S // WINDOW
    cur, prev = _swa_specs(nb)

    def body(q_ref, kvp_ref, kvc_ref, cp_ref, sp_ref, cc_ref, sc_ref, sink_ref, o_ref, lse_ref, qr_ref, kr_ref):
        cos_c, sin_c = cc_ref[...], sc_ref[...]
        q = (_rope(q_ref[...].astype(F32), cos_c, sin_c) * ATT_SCALE).astype(BF16)
        k = jnp.concatenate([_rope(kvp_ref[:, :KV_W].astype(F32), cp_ref[...], sp_ref[...]),
                             _rope(kvc_ref[:, :KV_W].astype(F32), cos_c, sin_c)], axis=0)
        qr_ref[...] = q
        kr_ref[...] = k[WINDOW:].astype(BF16)
        v = jnp.concatenate([kvp_ref[:, KV_W:], kvc_ref[:, KV_W:]], axis=0).astype(F32)
        low = lax.broadcasted_iota(jnp.int32, k.shape, 1) < ATT_HD
        kpad = _kv_padded(k, low)
        vpad = _kv_padded(v, low)
        mask = _swa_mask(pl.program_id(1) == 0)
        lses = []
        for g in range(ATT_KV_HEADS):
            pairs = range(g * PAIRS_PER_KV, (g + 1) * PAIRS_PER_KV)
            keys = [(p, e) for p in pairs for e in (0, 1)]
            qp = {p: q[:, p * 128:(p + 1) * 128] for p in pairs}
            s = {pe: jnp.where(mask, _nt(qp[pe[0]], kpad[g][pe[1]]), NEG_INF) for pe in keys}
            pr = {}
            for pe in keys:
                sink = sink_ref[0, 2 * pe[0] + pe[1]]
                m = jnp.maximum(jnp.max(s[pe], axis=1, keepdims=True), sink)
                ex = jnp.exp(s[pe] - m)
                den = jnp.sum(ex, axis=1, keepdims=True) + jnp.exp(sink - m)
                pr[pe] = (ex * (1.0 / den)).astype(BF16)
                lses.append(m + jnp.log(den))
            for p in pairs:
                o_ref[:, p * 128:(p + 1) * 128] = (_nn(pr[p, 0], vpad[g][0]) + _nn(pr[p, 1], vpad[g][1])).astype(BF16)
        lse_ref[...] = jnp.concatenate(lses, axis=1)

    tab = lambda im: pl.BlockSpec((None, WINDOW, 128), im)
    return pl.pallas_call(
        body, name=name, grid=(B, nb),
        in_specs=[*_swa_z_specs(),
                  tab(prev), tab(prev), tab(cur), tab(cur),
                  pl.BlockSpec(memory_space=pltpu.SMEM)],
        out_specs=[pl.BlockSpec((None, WINDOW, D_MODEL), cur), pl.BlockSpec((None, WINDOW, ATT_HEADS), cur),
                   pl.BlockSpec((None, WINDOW, D_MODEL), cur), pl.BlockSpec((None, WINDOW, KV_W), cur)],
        out_shape=[jax.ShapeDtypeStruct((B, S, D_MODEL), BF16), jax.ShapeDtypeStruct((B, S, ATT_HEADS), F32),
                   jax.ShapeDtypeStruct((B, S, D_MODEL), BF16), jax.ShapeDtypeStruct((B, S, KV_W), BF16)],
        compiler_params=_params("parallel", "parallel"),
    )(z, z, z, cos, sin, cos, sin, sinks)


def _swa_bwd(z, qr, kr, cos, sin, sinks, lse, dob, dz, *, name):
    B, S, _ = z.shape
    nb = S // WINDOW
    cur, prev = _swa_specs(nb)

    def body(q_ref, krp_ref, krc_ref, kvp_ref, kvc_ref, cp_ref, sp_ref, cc_ref, sc_ref, sink_ref, lse_ref, do_ref, dz_in,
             dq_ref, dkc_ref, dkp_ref, dsink_ref):
        @pl.when((pl.program_id(0) == 0) & (pl.program_id(1) == 0))
        def _():
            dsink_ref[...] = jnp.zeros_like(dsink_ref)

        cos_c, sin_c, cos_p, sin_p = cc_ref[...], sc_ref[...], cp_ref[...], sp_ref[...]
        q = q_ref[...]
        k = jnp.concatenate([krp_ref[...], krc_ref[...]], axis=0).astype(F32)
        v = jnp.concatenate([kvp_ref[:, KV_W:], kvc_ref[:, KV_W:]], axis=0).astype(F32)
        low = lax.broadcasted_iota(jnp.int32, k.shape, 1) < ATT_HD
        kpad = _kv_padded(k, low)
        vpad = _kv_padded(v, low)
        mask = _swa_mask(pl.program_id(1) == 0)
        lse = lse_ref[...]
        dq_parts, dk_sum, dv_sum, dsinks = [], [], [], []
        for g in range(ATT_KV_HEADS):
            pairs = range(g * PAIRS_PER_KV, (g + 1) * PAIRS_PER_KV)
            keys = [(p, e) for p in pairs for e in (0, 1)]
            qp = {p: q[:, p * 128:(p + 1) * 128] for p in pairs}
            dop = {p: do_ref[:, p * 128:(p + 1) * 128] for p in pairs}
            s = {pe: jnp.where(mask, _nt(qp[pe[0]], kpad[g][pe[1]]), NEG_INF) for pe in keys}
            dp = {pe: _nt(dop[pe[0]], vpad[g][pe[1]]) for pe in keys}
            pr, ds = {}, {}
            for pe in keys:
                h = 2 * pe[0] + pe[1]
                lse_h = lse[:, h:h + 1]
                pf = jnp.exp(s[pe] - lse_h)
                delta = jnp.sum(pf * dp[pe], axis=1, keepdims=True)
                ds[pe] = (pf * (dp[pe] - delta)).astype(BF16)
                pr[pe] = pf.astype(BF16)
                p_sink = jnp.exp(sink_ref[0, h] - lse_h)
                dsinks.append(-jnp.sum(p_sink * delta, axis=0, keepdims=True))
            for p in pairs:
                dq_parts.append((_nn(ds[p, 0], kpad[g][0]) + _nn(ds[p, 1], kpad[g][1])) * ATT_SCALE)
            x = [sum(_tn(ds[p, e], qp[p]) for p in pairs) for e in (0, 1)]
            y = [sum(_tn(pr[p, e], dop[p]) for p in pairs) for e in (0, 1)]
            zk = jnp.where(low, x[0], x[1])
            zv = jnp.where(low, y[0], y[1])
            dk_sum.append(zk + _swap_halves(zk))
            dv_sum.append(zv + _swap_halves(zv))
        dq_ref[...] = _rope(jnp.concatenate(dq_parts, axis=1), cos_c, sin_c, inverse=True).astype(BF16)
        dk = jnp.where(low, dk_sum[0], dk_sum[1])
        dv = jnp.where(low, dv_sum[0], dv_sum[1])
        dkp_ref[:, :KV_W] = _rope(dk[:WINDOW], cos_p, sin_p, inverse=True)
        dkp_ref[:, KV_W:] = dv[:WINDOW]
        dkc_ref[:, :KV_W] = _rope(dk[WINDOW:], cos_c, sin_c, inverse=True)
        dkc_ref[:, KV_W:] = dv[WINDOW:]
        dsink_ref[...] += jnp.concatenate(dsinks, axis=1)

    tab = lambda im: pl.BlockSpec((None, WINDOW, 128), im)
    return pl.pallas_call(
        body, name=name, grid=(B, nb),
        in_specs=[pl.BlockSpec((None, WINDOW, D_MODEL), cur), tab(prev), tab(cur),
                  *_swa_z_specs()[1:],
                  tab(prev), tab(prev), tab(cur), tab(cur),
                  pl.BlockSpec(memory_space=pltpu.SMEM),
                  pl.BlockSpec((None, WINDOW, ATT_HEADS), cur),
                  pl.BlockSpec((None, WINDOW, D_MODEL), cur),
                  ANY],
        out_specs=[_swa_z_specs()[0],
                   pl.BlockSpec((None, WINDOW, 2 * KV_W), cur), pl.BlockSpec((None, WINDOW, 2 * KV_W), cur),
                   pl.BlockSpec((1, ATT_HEADS), lambda b, i: (0, 0))],
        out_shape=[jax.ShapeDtypeStruct(dz.shape, BF16),
                   jax.ShapeDtypeStruct((B, S, 2 * KV_W), F32), jax.ShapeDtypeStruct((B, S, 2 * KV_W), F32),
                   jax.ShapeDtypeStruct((1, ATT_HEADS), F32)],
        input_output_aliases={12: 0},
        compiler_params=_params("arbitrary", "arbitrary"),
    )(qr, kr, kr, z, z, cos, sin, cos, sin, sinks, lse, dob, dz)


def _swa_dkv_combine(dkv_cur, dkv_prev, dz, *, name):
    B, S, W = dkv_cur.shape

    def body(c_ref, p_ref, dz_in, o_ref):
        rows = lax.broadcasted_iota(jnp.int32, (S, W), 0)
        o_ref[...] = (c_ref[...] + _shift_up(p_ref[...], WINDOW, rows, S)).astype(BF16)

    seq = pl.BlockSpec((None, S, W), lambda b: (b, 0, 0))
    return pl.pallas_call(
        body, name=name, grid=(B,),
        in_specs=[seq, seq, ANY], out_specs=pl.BlockSpec((None, S, W), lambda b: (b, 0, O_AKV // W_AKV)),
        out_shape=jax.ShapeDtypeStruct(dz.shape, BF16),
        input_output_aliases={2: 0},
        compiler_params=_params("parallel"),
    )(dkv_cur, dkv_prev, dz)


def _rope_tables(positions):
    half = ROPE_DIM // 2
    inv = ROPE_THETA ** (-2.0 * jnp.arange(half, dtype=F32) / ROPE_DIM)
    ang = positions.astype(F32)[..., None] * inv
    c, s = jnp.cos(ang), jnp.sin(ang)
    pad = jnp.zeros(ang.shape[:-1] + (ATT_HD - ROPE_DIM,), F32)
    cos = jnp.concatenate([c, c, pad + 1.0], axis=-1)
    sin = jnp.concatenate([-s, s, pad], axis=-1)
    return jnp.tile(cos, (1, 1, 2)), jnp.tile(sin, (1, 1, 2))


def _lower_bound(lb_logits, *, name):
    def body(l_ref, o_ref):
        l = l_ref[...]
        e = jnp.exp(l - jnp.max(l, axis=0, keepdims=True))
        o_ref[...] = e[0:1] / jnp.sum(e, axis=0, keepdims=True)

    return pl.pallas_call(body, name=name, out_shape=jax.ShapeDtypeStruct((1, lb_logits.shape[1]), F32))(lb_logits)


W_ZH, W_GATES, W_AQ, W_AKV = 4 * HF, 2 * D_MODEL, ATT_HEADS * ATT_HD, 2 * KV_W
O_ZH, O_GATES, O_AQ, O_AKV = 0, W_ZH, W_ZH + W_GATES, W_ZH + W_GATES + W_AQ
W_IN = W_ZH + W_GATES + W_AQ + W_AKV


W_IN_BLK = W_IN // N_DEV


def _reordered_rows(w_t, *, name):
    pieces = [(0, O_ZH, W_ZH), (W_ZH + W_AQ + W_AKV, O_GATES, W_GATES), (W_ZH, O_AQ, W_AQ + W_AKV)]

    def body(src, dst, sems):
        copies = [pltpu.make_async_copy(src.at[pl.ds(s, n)], dst.at[pl.ds(d, n)], sems.at[i]) for i, (s, d, n) in enumerate(pieces)]
        for cp in copies:
            cp.start()
        for cp in copies:
            cp.wait()

    return pl.pallas_call(body, name=name, out_shape=jax.ShapeDtypeStruct(w_t.shape, w_t.dtype), in_specs=[ANY], out_specs=ANY,
                          scratch_shapes=[pltpu.SemaphoreType.DMA((len(pieces),))])(w_t)


def _reference_row_block(j, rows=256):
    nz, ng = W_ZH // rows, W_GATES // rows
    return jnp.where(j < nz, j, jnp.where(j < nz + ng, j + (W_AQ + W_AKV) // rows, j - ng))


def _local_step(x, positions, target, small, w_in_t, rest_weights, emit, start_token):
    B, S, D = x.shape
    T = B * S
    x2 = x.reshape(T, D)
    cos, sin = _rope_tables(positions)
    lb = _lower_bound(small["lb_logits"], name="lb_fwd")
    zero = lambda tok: tok[0:1, 0:1]

    u1 = _norm_cast(x2, small["norm1_g"] + zero(start_token), name="norm1")
    z = _matmul(u1, w_in_t, tb=True, out_dtype=BF16, name="mm_z", tm=1024, tn=W_IN // 2)
    z3 = z.reshape(B, S, W_IN)
    oa, states = _hgrn_fwd(z3, lb, small["hgrn_norm_g"], name="hgrn_fwd")
    ob, lse, qr, kr = _swa_fwd(z3, cos, sin, small["attn_sinks"], name="swa_fwd")
    oa2 = oa.reshape(T, D)
    ob2 = ob.reshape(T, D)
    W = rest_weights("mix", ob)
    pa = _matmul(oa2, W["w_a"], out_dtype=BF16, name="mm_pa", tm=1024, tn=1024)
    pb = _matmul(ob2, W["w_b"], out_dtype=BF16, name="mm_pb", tm=1024, tn=1024)
    merged = _merge_fwd(z, pa, pb, name="merge_fwd")
    h = _matmul(merged, W["w_out"], addend=x2, name="mm_h", tm=1024, tn=1024)
    u2 = _norm_cast(h, small["norm2_g"], name="norm2")
    W.update(rest_weights("ffn", u2))
    gu = _matmul(u2, W["w_ffn_t"], tb=True, out_dtype=BF16, name="mm_gu", tm=2048, tn=512)
    gu3 = gu.reshape(B, S, 2 * D_FF)
    act, a_pre = _conv_act_fwd(gu3, W["conv_w"], small["conv_b"], name="conv_act_fwd")
    act2 = act.reshape(T, D_FF)
    h2 = _matmul(act2, W["w_down"], addend=h, name="mm_h2", tm=1024, tn=1024)

    g = {}
    dh2, dh2b, g["final_g"], loss = _final_loss_bwd(h2, small["final_g"].reshape(1, D), target.reshape(T, D), name="final_loss_bwd")
    dact = _matmul(dh2b, W["w_down"], tb=True, out_dtype=BF16, name="mm_dact", tm=1024, tn=D_FF)
    dw_down_t = _matmul(dh2b, act2, ta=True, out_dtype=BF16, name="mm_dw_down", tm=1024, tn=256, tk=8192)
    dg_, dup, g["conv_w"], g["conv_b"] = _conv_act_bwd(gu3, a_pre, W["conv_w"], dact.reshape(B, S, D_FF), name="conv_act_bwd")
    dg2 = dg_.reshape(T, D_FF)
    dup2 = dup.reshape(T, D_FF)
    du2 = _matmul(dg2, W["w_ffn_t"], name="mm_du2_g", tm=1024, tn=1024, b_koff=0)
    du2 = _matmul(dup2, W["w_ffn_t"], addend=du2, out_dtype=BF16, name="mm_du2_u", tm=1024, tn=1024, b_koff=1)
    dw_ffn_t = _matmul(u2, dg2, ta=True, out_t=True, out_dtype=BF16, into=lax.empty((2 * D_FF, D), BF16), o_noff=0, name="mm_dw_ffn_g", tm=1024, tn=256, tk=8192)
    dw_ffn_t = _matmul(u2, dup2, ta=True, out_t=True, out_dtype=BF16, into=dw_ffn_t, o_noff=D_FF // 256, name="mm_dw_ffn_u", tm=1024, tn=256, tk=8192)
    tok = emit("ffn", dict(w_ffn_t=dw_ffn_t, w_down=dw_down_t.T))
    dh, dhb, g["norm2_g"] = _norm_bwd_add(h, small["norm2_g"] + zero(tok), du2, dh2, name="norm2_bwd")
    dmerged = _matmul(dhb, W["w_out"], tb=True, out_dtype=BF16, name="mm_dmerged", tm=1024, tn=1024)
    dw_out = _matmul(merged, dhb, ta=True, out_dtype=BF16, name="mm_dw_out", tm=1024, tn=1024, tk=2048)
    dz, dpa, dpb = _merge_bwd(z, pa, pb, dmerged, lax.empty((T, W_IN), BF16), name="merge_bwd")
    doa =_matmul(dpa, W["w_a"], tb=True, out_dtype=BF16, name="mm_doa", tm=1024, tn=1024)
    dw_a = _matmul(oa2, dpa, ta=True, out_dtype=BF16, name="mm_dw_a", tm=1024, tn=1024, tk=2048)
    dob = _matmul(dpb, W["w_b"], tb=True, out_dtype=BF16, name="mm_dob", tm=1024, tn=1024)
    dw_b = _matmul(ob2, dpb, ta=True, out_dtype=BF16, name="mm_dw_b", tm=1024, tn=1024, tk=2048)
    tok = emit("mix", dict(w_out=dw_out, w_a=dw_a, w_b=dw_b))
    dz3, dkv_cur, dkv_prev, dsinks = _swa_bwd(z3, qr, kr, cos, sin, small["attn_sinks"] + zero(tok), lse, dob.reshape(B, S, D),
                                              dz.reshape(B, S, W_IN), name="swa_bwd")
    dz3 = _swa_dkv_combine(dkv_cur, dkv_prev, dz3, name="swa_dkv")
    g["attn_sinks"] = dsinks
    dz3, g["lb"], g["hgrn_norm_g"] = _hgrn_bwd(z3, lb, small["hgrn_norm_g"], states, doa.reshape(B, S, D), dz3, name="hgrn_bwd")
    dz = dz3.reshape(T, W_IN)
    dw_in_t = _matmul(u1, dz, ta=True, out_t=True, o_block_perm=_reference_row_block, out_dtype=BF16, name="mm_dw_in", tm=1024, tn=256, tk=8192)
    tok = emit("in", dict(w_in_t=dw_in_t))
    du1 = _matmul(dz, w_in_t, after=tok, out_dtype=BF16, name="mm_du1", tm=1024, tn=512)
    dx, g["norm1_g"] = _norm_bwd_add(x2, small["norm1_g"], du1, dh, with_bf16=False, name="norm1_bwd")
    g["lb_logits"] = _lb_bwd(g.pop("lb"), lb, name="lb_bwd")
    return loss, dx.reshape(B, S, D), g


def _my_place():
    return lax.axis_index("x"), lax.axis_index("y"), lax.axis_index("c")


def _gather_blocks(x_ref, out_ref, send_sems, recv_sems, local_sem):
    x, y, c = _my_place()
    me, sibling = (x, y, c), (x, y, 1 - c)
    chips = [(1 - x, y), (x, 1 - y), (1 - x, 1 - y)]

    def slot(px, py, pc):
        return out_ref.at[4 * px + 2 * py + pc]

    def copy(k, block, to, src=None):
        return pltpu.make_async_remote_copy(
            src_ref=slot(*block) if src is None else src, dst_ref=slot(*block),
            send_sem=send_sems.at[k], recv_sem=recv_sems.at[k], device_id=to, device_id_type=MESH)

    mine = pltpu.make_async_copy(x_ref, slot(*me), local_sem)
    mine.start()
    first = [copy(0, me, sibling, src=x_ref)]
    first += [copy(1 + j, me, (*chip, c), src=x_ref) for j, chip in enumerate(chips)]
    for cp in first:
        cp.start()
    passed = [copy(4 + j, (*chip, c), sibling) for j, chip in enumerate(chips)]
    for j, chip in enumerate(chips):
        copy(1 + j, (*chip, c), me).wait_recv()
        passed[j].start()
    copy(0, sibling, me).wait_recv()
    for j, chip in enumerate(chips):
        copy(4 + j, (*chip, 1 - c), me).wait_recv()
    for cp in first + passed:
        cp.wait_send()
    mine.wait()


GATHER_SEMS = [pltpu.SemaphoreType.DMA((7,)), pltpu.SemaphoreType.DMA((7,)), pltpu.SemaphoreType.DMA]


def _all_gather(blk, *, name):
    return pl.pallas_call(
        _gather_body_fn(), name=name,
        out_shape=jax.ShapeDtypeStruct((N_DEV,) + blk.shape, blk.dtype),
        in_specs=[ANY], out_specs=ANY,
        scratch_shapes=GATHER_SEMS,
    )(blk)


def _gather_body_fn():
    def body(x_ref, out_ref, send_sems, recv_sems, local_sem):
        _gather_blocks(x_ref, out_ref, send_sems, recv_sems, local_sem)
    return body


SLAB_W = 1152
SMALL_SHAPES = dict(norm1_g=(1, D_MODEL), lb_logits=(2, HGRN_HEADS * HGRN_DK), hgrn_norm_g=(1, HGRN_DK), attn_sinks=(1, ATT_HEADS),
                    norm2_g=(1, D_MODEL), conv_b=(1, D_FF), final_g=(1, D_MODEL))
CONVW_BLK = D_FF // N_DEV
CONVW_STRIDE = SLAB_W // 3


def _slab_layout():
    layout, r = {}, 0
    for nm, (nr, w) in SMALL_SHAPES.items():
        layout[nm] = []
        for i in range(nr):
            for c0 in range(0, w, SLAB_W):
                layout[nm].append((r, i, c0, min(SLAB_W, w - c0)))
                r += 1
    return layout, r


SMALL_ROWS, _N_SMALL_ROWS = _slab_layout()
CONV_ROW0 = -(-_N_SMALL_ROWS // 8) * 8
LOSS_ROW = CONV_ROW0 + N_DEV
SLAB_ROWS = LOSS_ROW + 8


def _small_step(grads, g_conv_w, loss, params, moments, variances, dev, *, name):
    names = list(SMALL_ROWS)
    n = len(names)

    def body(dev_ref, *refs):
        g_refs = dict(zip(names, refs[:n]))
        gc_ref, loss_ref = refs[n], refs[n + 1]
        base = n + 2
        w_refs, m_refs, v_refs = (dict(zip(names + ["conv_w"], refs[base + i * (n + 1):base + (i + 1) * (n + 1)])) for i in range(3))
        o = base + 3 * (n + 1)
        gath_ref, loss_out = refs[o], refs[o + 1]
        outs = {nm: refs[o + 2 + 4 * i:o + 6 + 4 * i] for i, nm in enumerate(names + ["conv_w"])}
        slab, total, send_sems, recv_sems, local_sem = refs[-5:]

        slab[...] = jnp.zeros_like(slab)
        for nm, pieces in SMALL_ROWS.items():
            for r, i, c0, w in pieces:
                slab[r:r + 1, 0:w] = g_refs[nm][i:i + 1, c0:c0 + w]
        for p in range(N_DEV):
            for j in range(3):
                slab[CONV_ROW0 + p:CONV_ROW0 + p + 1, j * CONVW_STRIDE:j * CONVW_STRIDE + CONVW_BLK] = gc_ref[j:j + 1, p * CONVW_BLK:(p + 1) * CONVW_BLK]
        slab[LOSS_ROW:LOSS_ROW + 1, 0:1] = loss_ref[...]
        _gather_blocks(slab, gath_ref, send_sems, recv_sems, local_sem)
        acc = gath_ref[0]
        for p in range(1, N_DEV):
            acc = acc + gath_ref[p]
        total[...] = acc
        loss_out[...] = total[LOSS_ROW:LOSS_ROW + 1, 0:1]

        def update(nm, g, i, c0, w):
            at = (slice(i, i + 1), slice(c0, c0 + w))
            d, mn, vn = _adamw_math(w_refs[nm][at], g, m_refs[nm][at], v_refs[nm][at])
            for ref, val in zip(outs[nm], (g, d, mn, vn)):
                ref[at] = val

        for nm, pieces in SMALL_ROWS.items():
            for r, i, c0, w in pieces:
                update(nm, total[r:r + 1, 0:w], i, c0, w)
        conv_rows = total[CONV_ROW0:CONV_ROW0 + N_DEV, :]
        rowid = lax.broadcasted_iota(jnp.int32, conv_rows.shape, 0)
        mine = jnp.sum(jnp.where(rowid == dev_ref[0], conv_rows, 0.0), axis=0, keepdims=True)
        for j in range(3):
            update("conv_w", mine[:, j * CONVW_STRIDE:j * CONVW_STRIDE + CONVW_BLK], j, 0, CONVW_BLK)

    order = names + ["conv_w"]
    ins = [grads[nm] for nm in names] + [g_conv_w, loss]
    for d in (params, moments, variances):
        ins += [d[nm] for nm in order]
    vmem = pl.BlockSpec(memory_space=pltpu.VMEM)
    out_shape = [jax.ShapeDtypeStruct((N_DEV, SLAB_ROWS, SLAB_W), F32), jax.ShapeDtypeStruct((1, 1), F32)]
    for nm in order:
        out_shape += [jax.ShapeDtypeStruct(params[nm].shape, F32)] * 4
    res = pl.pallas_call(
        body, name=name,
        grid_spec=pltpu.PrefetchScalarGridSpec(
            num_scalar_prefetch=1, grid=(1,),
            in_specs=[vmem] * len(ins), out_specs=[vmem] * len(out_shape),
            scratch_shapes=[pltpu.VMEM((SLAB_ROWS, SLAB_W), F32), pltpu.VMEM((SLAB_ROWS, SLAB_W), F32)] + GATHER_SEMS),
        out_shape=out_shape,
    )(dev, *ins)
    return res[1], {nm: tuple(res[2 + 4 * i:6 + 4 * i]) for i, nm in enumerate(order)}


HBM_SPEC = pl.BlockSpec(memory_space=pltpu.HBM)
SEM_SPEC = pl.BlockSpec(memory_space=pltpu.SEMAPHORE)
DATAFLOW_EFFECT = pltpu.SideEffectType.DATAFLOW_SIDE_EFFECTING
N_PEERS = N_DEV - 1


def _peers(x, y, c):
    return [(1 - x if r & 4 else x, 1 - y if r & 2 else y, 1 - c if r & 1 else c) for r in range(1, N_DEV)]


def _exchange_start(srcs, scatter, *, after=None, name):
    n = len(srcs)
    lands = [lax.empty(a.shape if scatter else (N_DEV,) + a.shape, a.dtype) for a in srcs]
    extra = [] if after is None else [after]

    def body(*refs):
        src_refs, land_refs = refs[:n], refs[n:2 * n]
        send_sems, recv_sems, token = refs[2 * n + len(extra)], refs[2 * n + len(extra) + 1], refs[-1]
        x, y, c = _my_place()
        me = 4 * x + 2 * y + c
        for i in range(n):
            for r, (tx, ty, tc) in enumerate(_peers(x, y, c)):
                src = src_refs[i].at[4 * tx + 2 * ty + tc] if scatter else src_refs[i]
                pltpu.make_async_remote_copy(
                    src_ref=src, dst_ref=land_refs[i].at[me], send_sem=send_sems.at[N_PEERS * i + r],
                    recv_sem=recv_sems.at[N_PEERS * i + r], device_id=(tx, ty, tc), device_id_type=MESH).start()
        token[...] = jnp.zeros_like(token)

    thru = [pltpu.HBM(a.shape, a.dtype) for a in list(srcs) + lands]
    res = pl.pallas_call(
        body, name=name,
        out_shape=(pltpu.SemaphoreType.DMA((N_PEERS * n,)), pltpu.SemaphoreType.DMA((N_PEERS * n,)), *thru,
                   jax.ShapeDtypeStruct((8, 128), F32)),
        in_specs=[HBM_SPEC] * (2 * n) + [ANY] * len(extra),
        out_specs=(SEM_SPEC, SEM_SPEC, *([HBM_SPEC] * (2 * n)), pl.BlockSpec(memory_space=pltpu.VMEM)),
        input_output_aliases={i: 2 + i for i in range(2 * n)},
        compiler_params=pltpu.CompilerParams(has_side_effects=DATAFLOW_EFFECT),
    )(*[pltpu.with_memory_space_constraint(a, pltpu.HBM) for a in list(srcs) + lands], *extra)
    return (res[0], res[1], list(res[2:2 + n]), list(res[2 + n:2 + 2 * n]), scatter), res[-1]


def _exchange_wait(handle, after, *, name):
    send_sems, recv_sems, srcs, lands, scatter = handle
    n = len(srcs)

    def body(*refs):
        src_refs, land_refs = refs[:n], refs[n:2 * n]
        send_sems, recv_sems = refs[2 * n], refs[2 * n + 1]
        x, y, c = _my_place()
        for i in range(n):
            for r in range(N_PEERS):
                src = src_refs[i].at[0] if scatter else src_refs[i]
                cp = pltpu.make_async_remote_copy(
                    src_ref=src, dst_ref=land_refs[i].at[0], send_sem=send_sems.at[N_PEERS * i + r],
                    recv_sem=recv_sems.at[N_PEERS * i + r], device_id=(x, y, c), device_id_type=MESH)
                cp.wait_send()
                cp.wait_recv()

    thru = [pltpu.HBM(a.shape, a.dtype) for a in srcs + lands]
    res = pl.pallas_call(
        body, name=name, out_shape=tuple(thru),
        in_specs=[HBM_SPEC] * (2 * n) + [SEM_SPEC, SEM_SPEC, ANY], out_specs=tuple([HBM_SPEC] * (2 * n)),
        input_output_aliases={i: i for i in range(2 * n)},
        compiler_params=pltpu.CompilerParams(has_side_effects=DATAFLOW_EFFECT),
    )(*srcs, *lands, send_sems, recv_sems, after)
    return list(res[:n]), list(res[n:])


def _with_own(land, own, me):
    return lax.dynamic_update_index_in_dim(land, own, me, 0)


def _adamw_math(w, g, m, v):
    m = ADAM_B1 * m + (1.0 - ADAM_B1) * g
    v = ADAM_B2 * v + (1.0 - ADAM_B2) * (g * g)
    m_hat = m / (1.0 - ADAM_B1 ** ADAM_STEP)
    v_hat = v / (1.0 - ADAM_B2 ** ADAM_STEP)
    delta = -ADAM_LR * (m_hat / (jnp.sqrt(v_hat) + ADAM_EPS) + ADAM_WD * w)
    return delta, m, v


def _adamw_sum(parts, w, m, v, *, name):
    shape = w.shape
    R, n = shape[-2], shape[-1]
    w, m, v = (t.reshape(R, n) for t in (w, m, v))
    tr = _pick(R, (256, 464, 352, 128))

    def body(p_ref, w_ref, m_ref, v_ref, g_ref, d_ref, mo_ref, vo_ref):
        g = p_ref[0].astype(F32)
        for p in range(1, N_DEV):
            g = g + p_ref[p].astype(F32)
        d, mn, vn = _adamw_math(w_ref[...], g, m_ref[...], v_ref[...])
        g_ref[...] = g
        d_ref[...] = d
        mo_ref[...] = mn
        vo_ref[...] = vn

    row = pl.BlockSpec((tr, n), lambda i: (i, 0))
    outs = pl.pallas_call(
        body, name=name, grid=(R // tr,),
        in_specs=[pl.BlockSpec((N_DEV, tr, n), lambda i: (0, i, 0)), row, row, row],
        out_specs=[row, row, row, row],
        out_shape=[jax.ShapeDtypeStruct((R, n), F32)] * 4,
        compiler_params=_params("parallel"),
    )(parts, w, m, v)
    return [t.reshape(shape) for t in outs]


def _lb_bwd(dlb, lb, *, name):
    def body(d_ref, lb_ref, o_ref):
        t = d_ref[...] * lb_ref[...] * (1.0 - lb_ref[...])
        o_ref[0:1, :] = t
        o_ref[1:2, :] = -t

    return pl.pallas_call(body, name=name, out_shape=jax.ShapeDtypeStruct((2, lb.shape[1]), F32))(dlb, lb)


DOWN_BLK, ROW_BLK = D_FF // N_DEV, D_MODEL // N_DEV
W_FFN_BLK = 2 * D_FF // N_DEV
CONV_BITS_SHAPE = (16, 256)


def kernel(x, positions, norm1_g, w_in, lb_logits, hgrn_norm_g, w_a, attn_sinks, w_b, w_out, norm2_g, w_ffn_in, conv_w, conv_b, w_down, final_g, loss_target, m_norm1_g, m_w_in, m_lb_logits, m_hgrn_norm_g, m_w_a, m_attn_sinks, m_w_b, m_w_out, m_norm2_g, m_w_ffn_in, m_conv_w, m_conv_b, m_w_down, m_final_g, v_norm1_g, v_w_in, v_lb_logits, v_hgrn_norm_g, v_w_a, v_attn_sinks, v_w_b, v_w_out, v_norm2_g, v_w_ffn_in, v_conv_w, v_conv_b, v_w_down, v_final_g):
    xi, yi, ci = _my_place()
    dev = 4 * xi + 2 * yi + ci

    tr = lambda t: jnp.transpose(t[0])
    untr = lambda t: jnp.transpose(t)[None]
    w_in_blocks = _all_gather(tr(w_in).astype(BF16), name="ag_w_in")
    conv_bits = lax.bitcast_convert_type(conv_w, BF16).reshape(-1)
    conv_bits = jnp.pad(conv_bits, (0, CONV_BITS_SHAPE[0] * CONV_BITS_SHAPE[1] - conv_bits.shape[0])).reshape(CONV_BITS_SHAPE)
    w_in_full_t = _reordered_rows(w_in_blocks.reshape(W_IN, D_MODEL), name="w_in_rows")
    gather_handles = {}
    gather_handles["mix"], tok_mix = _exchange_start([w_a[0].astype(BF16), w_b[0].astype(BF16), w_out[0].astype(BF16)], False,
                                                     after=w_in_full_t, name="ag_mix_start")
    gather_handles["ffn"], tok_ffn = _exchange_start([tr(w_ffn_in).astype(BF16), w_down[0].astype(BF16), conv_bits], False,
                                                     after=tok_mix, name="ag_ffn_start")
    start_token = tok_mix + tok_ffn

    def rest_weights(group, after):
        own, lands = _exchange_wait(gather_handles[group], after, name="ag_" + group + "_wait")
        full = [_with_own(l, o, dev) for l, o in zip(lands, own)]
        if group == "mix":
            return dict(zip(("w_a", "w_b", "w_out"), [t.reshape(D_MODEL, D_MODEL) for t in full]))
        bits = full[2].reshape(N_DEV, -1)[:, :3 * CONVW_BLK * 2].reshape(N_DEV, 3, CONVW_BLK, 2)
        return dict(w_ffn_t=full[0].reshape(2 * D_FF, D_MODEL), w_down=full[1].reshape(D_FF, D_MODEL),
                    conv_w=lax.bitcast_convert_type(bits, F32).transpose(1, 0, 2).reshape(3, D_FF))

    handles = {}

    def emit(group, gr):
        if group == "ffn":
            srcs = [gr["w_ffn_t"].reshape(N_DEV, W_FFN_BLK, D_MODEL), gr["w_down"].reshape(N_DEV, DOWN_BLK, D_MODEL)]
        elif group == "mix":
            srcs = [gr[n].reshape(N_DEV, ROW_BLK, D_MODEL) for n in ("w_out", "w_a", "w_b")]
        else:
            srcs = [gr["w_in_t"].reshape(N_DEV, W_IN_BLK, D_MODEL)]
        handles[group], token = _exchange_start(srcs, True, name="rs_" + group + "_start")
        return token

    small = dict(norm1_g=norm1_g, lb_logits=lb_logits, hgrn_norm_g=hgrn_norm_g, attn_sinks=attn_sinks, norm2_g=norm2_g,
                 conv_b=conv_b, final_g=final_g)
    loss, grad_x, g = _local_step(x, positions, loss_target, small, w_in_full_t, rest_weights, emit, start_token)

    def parts_of(group, after):
        srcs, lands = _exchange_wait(handles[group], after, name="rs_" + group + "_wait")
        return [_with_own(l, lax.dynamic_index_in_dim(s, dev, 0, keepdims=False), dev) for s, l in zip(srcs, lands)]

    p_ffn, p_down = parts_of("ffn", grad_x)
    p_out, p_a, p_b = parts_of("mix", grad_x)
    (p_in,) = parts_of("in", grad_x)
    big = dict(
        w_in=[untr(t) for t in _adamw_sum(p_in, tr(w_in), tr(m_w_in), tr(v_w_in), name="adamw_w_in")],
        w_a=_adamw_sum(p_a, w_a, m_w_a, v_w_a, name="adamw_w_a"),
        w_b=_adamw_sum(p_b, w_b, m_w_b, v_w_b, name="adamw_w_b"),
        w_out=_adamw_sum(p_out, w_out, m_w_out, v_w_out, name="adamw_w_out"),
        w_ffn_in=[untr(t) for t in _adamw_sum(p_ffn, tr(w_ffn_in), tr(m_w_ffn_in), tr(v_w_ffn_in), name="adamw_w_ffn_in")],
        w_down=_adamw_sum(p_down, w_down, m_w_down, v_w_down, name="adamw_w_down"),
    )

    row = lambda t: t.reshape(1, -1) if t.ndim == 1 else t
    shard = lambda t: t.reshape(3, CONVW_BLK)
    sm_g = {nm: g[nm] for nm in SMALL_ROWS}
    sm_w = dict(norm1_g=norm1_g, lb_logits=lb_logits, hgrn_norm_g=hgrn_norm_g, attn_sinks=attn_sinks, norm2_g=norm2_g,
                conv_b=conv_b, final_g=row(final_g), conv_w=shard(conv_w))
    sm_m = dict(norm1_g=m_norm1_g, lb_logits=m_lb_logits, hgrn_norm_g=m_hgrn_norm_g, attn_sinks=m_attn_sinks, norm2_g=m_norm2_g,
                conv_b=m_conv_b, final_g=row(m_final_g), conv_w=shard(m_conv_w))
    sm_v = dict(norm1_g=v_norm1_g, lb_logits=v_lb_logits, hgrn_norm_g=v_hgrn_norm_g, attn_sinks=v_attn_sinks, norm2_g=v_norm2_g,
                conv_b=v_conv_b, final_g=row(v_final_g), conv_w=shard(v_conv_w))
    loss_total, sm_out = _small_step(sm_g, g["conv_w"], loss, sm_w, sm_m, sm_v, dev.astype(jnp.int32).reshape(1), name="small_step")
    shapes = dict(final_g=final_g.shape, conv_w=conv_w.shape)

    names = ("norm1_g", "w_in", "lb_logits", "hgrn_norm_g", "w_a", "attn_sinks", "w_b", "w_out", "norm2_g", "w_ffn_in", "conv_w", "conv_b", "w_down", "final_g")
    outs = [loss_total.reshape(()), grad_x]
    for kind in range(4):
        outs += [big[n][kind] if n in big else sm_out[n][kind].reshape(shapes.get(n, sm_out[n][kind].shape)) for n in names]
    return tuple(outs)
```

```python
import functools

import jax
import jax.numpy as jnp
from jax import lax
from jax.experimental import pallas as pl
from jax.experimental.pallas import tpu as pltpu

F32 = jnp.float32
BF16 = jnp.bfloat16

D_MODEL = 1024
HGRN_HEADS = 8
HGRN_DK = 128
CHUNK = 64
ATT_HEADS = 16
ATT_KV_HEADS = 2
ATT_HD = 64
ATT_GROUP = ATT_HEADS // ATT_KV_HEADS
WINDOW = 128
ROPE_DIM = ATT_HD // 4
ROPE_THETA = 500000.0
D_FF = 2816
EPS = 1e-6
NEG_INF = -1e30
N_DEV = 8

ADAM_LR = 0.001
ADAM_B1 = 0.9
ADAM_B2 = 0.999
ADAM_EPS = 1e-08
ADAM_WD = 0.01
ADAM_STEP = 10

MESH = pl.DeviceIdType.MESH
ANY = pl.BlockSpec(memory_space=pl.ANY)


def _pick(n, cands):
    for c in cands:
        if n % c == 0:
            return c
    return n


def _sigmoid(x):
    return 0.5 * jnp.tanh(0.5 * x) + 0.5


def _silu(x):
    hx = 0.5 * x
    return hx * jnp.tanh(hx) + hx


def _rms(x, g):
    return x * lax.rsqrt(jnp.mean(x * x, axis=-1, keepdims=True) + EPS) * g


def _dot(a, b, dims):
    return lax.dot_general(a, b, (dims, ((), ())), preferred_element_type=F32)


def _nn(a, b):
    return _dot(a, b, ((1,), (0,)))


def _nt(a, b):
    return _dot(a, b, ((1,), (1,)))


def _tn(a, b):
    return _dot(a, b, ((0,), (0,)))


def _params(*sem):
    return pltpu.CompilerParams(dimension_semantics=sem, vmem_limit_bytes=56 * 1024 * 1024)


def _matmul(a, b, *, ta=False, tb=False, out_dtype=F32, addend=None, after=None, into=None, o_noff=0, out_t=False,
            o_block_perm=lambda j: j, name, tm, tn, tk=None, n_extent=None, b_koff=0, b_noff=0):
    M, K = (a.shape[1], a.shape[0]) if ta else a.shape
    N = n_extent or (b.shape[0] if tb else b.shape[1])
    tm, tn, tk = min(tm, M), min(tn, N), min(tk or K, K)
    assert M % tm == 0 and N % tn == 0 and K % tk == 0, (name, M, N, K, tm, tn, tk)
    nk = K // tk
    use_scratch = nk > 1 and out_dtype != F32
    grid = (M // tm, N // tn, nk)
    a_spec = pl.BlockSpec((tk, tm), lambda i, j, k: (k, i)) if ta else pl.BlockSpec((tm, tk), lambda i, j, k: (i, k))
    b_spec = pl.BlockSpec((tn, tk), lambda i, j, k: (j + b_noff, k + b_koff)) if tb else pl.BlockSpec((tk, tn), lambda i, j, k: (k + b_koff, j + b_noff))
    o_spec = pl.BlockSpec((tm, tn), lambda i, j, k: (i, j))
    dims = ((0 if ta else 1,), (1 if tb else 0,))
    has_add = addend is not None

    n_in = 2 + has_add + (after is not None) + (into is not None)

    def body(*refs):
        a_ref, b_ref = refs[:2]
        c_ref = refs[2] if has_add else None
        o_ref = refs[n_in]
        part = _dot(a_ref[...], b_ref[...], dims)
        if nk == 1:
            if has_add:
                part = part + c_ref[...].astype(F32)
            o_ref[...] = (part.T if out_t else part).astype(out_dtype)
        else:
            acc_ref = refs[-1] if use_scratch else o_ref
            k = pl.program_id(2)

            @pl.when(k == 0)
            def _():
                acc_ref[...] = part + c_ref[...].astype(F32) if has_add else part

            @pl.when(k > 0)
            def _():
                acc_ref[...] += part

            if use_scratch:
                @pl.when(k == nk - 1)
                def _():
                    o_ref[...] = acc_ref[...].astype(out_dtype)

    in_specs = [a_spec, b_spec] + ([o_spec] if has_add else [])
    args = (a, b) + ((addend,) if has_add else ())
    if after is not None:
        in_specs.append(pl.BlockSpec(after.shape, lambda i, j, k: (0, 0)))
        args += (after,)
    aliases = {}
    if into is not None:
        in_specs.append(ANY)
        args += (into,)
        aliases = {len(args) - 1: 0}
    if out_t:
        assert nk == 1 and not has_add
        o_spec = pl.BlockSpec((tn, tm), lambda i, j, k: (o_block_perm(j) + o_noff, i))
    elif into is not None:
        o_spec = pl.BlockSpec((tm, tn), lambda i, j, k: (i, j + o_noff))
    return pl.pallas_call(
        body,
        name=name,
        grid=grid,
        in_specs=in_specs,
        out_specs=o_spec,
        out_shape=jax.ShapeDtypeStruct(into.shape if into is not None else ((N, M) if out_t else (M, N)), out_dtype),
        input_output_aliases=aliases,
        scratch_shapes=[pltpu.VMEM((tm, tn), F32)] if use_scratch else [],
        compiler_params=_params("parallel", "parallel", "arbitrary"),
    )(*args)


def _row_spec(tm, n):
    return pl.BlockSpec((tm, n), lambda i: (i, 0))


def _full_spec(shape):
    return pl.BlockSpec(shape, lambda i: tuple(0 for _ in shape))


def _norm_cast(x, g, *, name):
    T, D = x.shape
    tm = _pick(T, (512, 256, 128))

    def body(x_ref, g_ref, u_ref):
        u_ref[...] = _rms(x_ref[...], g_ref[...]).astype(BF16)

    return pl.pallas_call(
        body, name=name, grid=(T // tm,),
        in_specs=[_row_spec(tm, D), _full_spec((1, D))],
        out_specs=_row_spec(tm, D),
        out_shape=jax.ShapeDtypeStruct((T, D), BF16),
        compiler_params=_params("parallel"),
    )(x, g)


def _norm_bwd_add(x, g, du, dres, *, with_bf16=True, name):
    T, D = x.shape
    tm = _pick(T, (512, 256, 128))

    def body(x_ref, g_ref, du_ref, dr_ref, dx_ref, *rest):
        dg_ref = rest[-1]
        _, vjp = jax.vjp(_rms, x_ref[...], g_ref[...])
        dx, dg = vjp(du_ref[...].astype(F32))
        dx = dx + dr_ref[...]
        dx_ref[...] = dx
        if with_bf16:
            rest[0][...] = dx.astype(BF16)

        @pl.when(pl.program_id(0) == 0)
        def _():
            dg_ref[...] = jnp.zeros_like(dg_ref)

        dg_ref[...] += dg

    row = _row_spec(tm, D)
    return pl.pallas_call(
        body, name=name, grid=(T // tm,),
        in_specs=[row, _full_spec((1, D)), row, row],
        out_specs=[row] + ([row] if with_bf16 else []) + [_full_spec((1, D))],
        out_shape=[jax.ShapeDtypeStruct((T, D), F32)] + ([jax.ShapeDtypeStruct((T, D), BF16)] if with_bf16 else []) + [jax.ShapeDtypeStruct((1, D), F32)],
        compiler_params=_params("arbitrary"),
    )(x, g, du, dres)


def _final_loss_bwd(h2, g, target, *, name):
    T, D = h2.shape
    tm = _pick(T, (512, 256, 128))

    def body(h_ref, g_ref, t_ref, dx_ref, dxb_ref, dg_ref, loss_ref):
        y, vjp = jax.vjp(_rms, h_ref[...], g_ref[...])
        err = y - t_ref[...]
        dx, dg = vjp(err * (1.0 / D))
        dx_ref[...] = dx
        dxb_ref[...] = dx.astype(BF16)

        @pl.when(pl.program_id(0) == 0)
        def _():
            dg_ref[...] = jnp.zeros_like(dg_ref)
            loss_ref[...] = jnp.zeros_like(loss_ref)

        dg_ref[...] += dg
        loss_ref[...] += (0.5 / D) * jnp.sum(jnp.sum(err * err, axis=1, keepdims=True), axis=0, keepdims=True)

    return pl.pallas_call(
        body, name=name, grid=(T // tm,),
        in_specs=[_row_spec(tm, D), _full_spec((1, D)), _row_spec(tm, D)],
        out_specs=[_row_spec(tm, D), _row_spec(tm, D), _full_spec((1, D)), _full_spec((1, 1))],
        out_shape=[jax.ShapeDtypeStruct((T, D), F32), jax.ShapeDtypeStruct((T, D), BF16), jax.ShapeDtypeStruct((1, D), F32), jax.ShapeDtypeStruct((1, 1), F32)],
        compiler_params=_params("arbitrary"),
    )(h2, g, target)


def _merge_fn(gates, a, b):
    ga = gates[:, :D_MODEL].astype(F32)
    gb = gates[:, D_MODEL:].astype(F32)
    return _sigmoid(ga) * a.astype(F32) + _sigmoid(gb) * b.astype(F32)


def _gates_spec(tm):
    return pl.BlockSpec((tm, W_GATES), lambda i: (i, O_GATES // W_GATES))


def _merge_fwd(z, a, b, *, name):
    T = a.shape[0]
    tm = _pick(T, (512, 256, 128))

    def body(g_ref, a_ref, b_ref, o_ref):
        o_ref[...] = _merge_fn(g_ref[...], a_ref[...], b_ref[...]).astype(BF16)

    return pl.pallas_call(
        body, name=name, grid=(T // tm,),
        in_specs=[_gates_spec(tm), _row_spec(tm, D_MODEL), _row_spec(tm, D_MODEL)],
        out_specs=_row_spec(tm, D_MODEL),
        out_shape=jax.ShapeDtypeStruct((T, D_MODEL), BF16),
        compiler_params=_params("parallel"),
    )(z, a, b)


def _merge_bwd(z, a, b, dmerged, dz, *, name):
    T = a.shape[0]
    tm = _pick(T, (512, 256, 128))

    def body(g_ref, a_ref, b_ref, dm_ref, dz_in, dg_ref, da_ref, db_ref):
        g = g_ref[...].astype(F32)
        dm = dm_ref[...].astype(F32)
        sa = _sigmoid(g[:, :D_MODEL])
        sb = _sigmoid(g[:, D_MODEL:])
        da_ref[...] = (dm * sa).astype(BF16)
        db_ref[...] = (dm * sb).astype(BF16)
        dg_ref[:, :D_MODEL] = (dm * a_ref[...].astype(F32) * sa * (1.0 - sa)).astype(BF16)
        dg_ref[:, D_MODEL:] = (dm * b_ref[...].astype(F32) * sb * (1.0 - sb)).astype(BF16)

    return pl.pallas_call(
        body, name=name, grid=(T // tm,),
        in_specs=[_gates_spec(tm), _row_spec(tm, D_MODEL), _row_spec(tm, D_MODEL), _row_spec(tm, D_MODEL), ANY],
        out_specs=[_gates_spec(tm), _row_spec(tm, D_MODEL), _row_spec(tm, D_MODEL)],
        out_shape=[jax.ShapeDtypeStruct(dz.shape, BF16), jax.ShapeDtypeStruct((T, D_MODEL), BF16), jax.ShapeDtypeStruct((T, D_MODEL), BF16)],
        input_output_aliases={4: 0},
        compiler_params=_params("parallel"),
    )(z, a, b, dmerged, dz)


CONV_TC = 256


def _shift_down(x, n, rows):
    return jnp.where(rows >= n, pltpu.roll(x, n, 0), 0.0)


def _shift_up(x, n, rows, S):
    return jnp.where(rows < S - n, pltpu.roll(x, S - n, 0), 0.0)


def _conv_act_fwd(gu, conv_w, conv_b, *, name):
    B, S, _ = gu.shape
    tc = CONV_TC
    nc = D_FF // tc

    def body(g_ref, up_ref, w_ref, b_ref, o_ref, a_ref):
        g = g_ref[...].astype(F32)
        rows = lax.broadcasted_iota(jnp.int32, g.shape, 0)
        w = w_ref[...]
        a = w[2:3] * g + w[1:2] * _shift_down(g, 1, rows) + w[0:1] * _shift_down(g, 2, rows) + b_ref[...]
        o_ref[...] = (_silu(a) * up_ref[...].astype(F32)).astype(BF16)
        a_ref[...] = a.astype(BF16)

    col = pl.BlockSpec((None, S, tc), lambda b, j: (b, 0, j))
    return pl.pallas_call(
        body, name=name, grid=(B, nc),
        in_specs=[col,
                  pl.BlockSpec((None, S, tc), lambda b, j: (b, 0, j + nc)),
                  pl.BlockSpec((3, tc), lambda b, j: (0, j)),
                  pl.BlockSpec((1, tc), lambda b, j: (0, j))],
        out_specs=[col, col],
        out_shape=[jax.ShapeDtypeStruct((B, S, D_FF), BF16)] * 2,
        compiler_params=_params("parallel", "parallel"),
    )(gu, gu, conv_w, conv_b)


def _conv_act_bwd(gu, a_pre, conv_w, dact, *, name):
    B, S, _ = gu.shape
    tc = CONV_TC
    nc = D_FF // tc

    def body(g_ref, up_ref, a_ref, w_ref, da_ref, dg_ref, dup_ref, dw_ref, db_ref):
        g = g_ref[...].astype(F32)
        up = up_ref[...].astype(F32)
        a = a_ref[...].astype(F32)
        dact = da_ref[...].astype(F32)
        rows = lax.broadcasted_iota(jnp.int32, g.shape, 0)
        w = w_ref[...]
        sg = _sigmoid(a)
        dup_ref[...] = (dact * a * sg).astype(BF16)
        da = dact * up * sg * (1.0 + a * (1.0 - sg))
        da1 = _shift_up(da, 1, rows, S)
        da2 = _shift_up(da, 2, rows, S)
        dg_ref[...] = (w[2:3] * da + w[1:2] * da1 + w[0:1] * da2).astype(BF16)

        @pl.when(pl.program_id(1) == 0)
        def _():
            dw_ref[...] = jnp.zeros_like(dw_ref)
            db_ref[...] = jnp.zeros_like(db_ref)

        dw_ref[0:1, :] += jnp.sum(da2 * g, axis=0, keepdims=True)
        dw_ref[1:2, :] += jnp.sum(da1 * g, axis=0, keepdims=True)
        dw_ref[2:3, :] += jnp.sum(da * g, axis=0, keepdims=True)
        db_ref[...] += jnp.sum(da, axis=0, keepdims=True)

    col = pl.BlockSpec((None, S, tc), lambda j, b: (b, 0, j))
    return pl.pallas_call(
        body, name=name, grid=(nc, B),
        in_specs=[col,
                  pl.BlockSpec((None, S, tc), lambda j, b: (b, 0, j + nc)),
                  col,
                  pl.BlockSpec((3, tc), lambda j, b: (0, j)),
                  col],
        out_specs=[col, col, pl.BlockSpec((3, tc), lambda j, b: (0, j)), pl.BlockSpec((1, tc), lambda j, b: (0, j))],
        out_shape=[jax.ShapeDtypeStruct((B, S, D_FF), BF16), jax.ShapeDtypeStruct((B, S, D_FF), BF16),
                   jax.ShapeDtypeStruct((3, D_FF), F32), jax.ShapeDtypeStruct((1, D_FF), F32)],
        compiler_params=_params("parallel", "arbitrary"),
    )(gu, gu, a_pre, conv_w, dact)


HGRN_CPB = 4
HF = HGRN_HEADS * HGRN_DK


def _tri(n, upper=False):
    r = lax.broadcasted_iota(jnp.int32, (n, n), 0)
    c = lax.broadcasted_iota(jnp.int32, (n, n), 1)
    return (c >= r) if upper else (r >= c)


def _hs(h):
    return slice(h * HGRN_DK, (h + 1) * HGRN_DK)


def _cumsum_rows(tri_b, x):
    hi = x.astype(BF16)
    lo = (x - hi.astype(F32)).astype(BF16)
    return _nn(tri_b, hi) + _nn(tri_b, lo)


def _hgrn_pre(q, fz, lb, tril_b):
    qf = _silu(q)
    sg = _sigmoid(fz)
    f = lb + (1.0 - lb) * sg
    k = 1.0 - f
    b = _cumsum_rows(tril_b, jnp.log2(f))
    bref = b[CHUNK // 2:CHUNK // 2 + 1, :]
    blast = b[CHUNK - 1:CHUNK, :]
    e1 = jnp.exp2(b - bref)
    e2 = jnp.exp2(bref - b)
    e3 = e1 * jnp.exp2(bref)
    e4 = e2 * jnp.exp2(blast - bref)
    dec = jnp.exp2(blast)
    return sg, f, (e1, e2, e3, e4), qf * e1, k * e2, qf * e3, k * e4, dec


def _hgrn_fwd(zh, lb, gn, *, name):
    B, S, _ = zh.shape
    cpb = HGRN_CPB
    ts = cpb * CHUNK
    nblk = S // ts

    def body(z_ref, lb_ref, gn_ref, o_ref, st_ref, state):
        @pl.when(pl.program_id(1) == 0)
        def _():
            state[...] = jnp.zeros_like(state)

        H = HGRN_HEADS
        causal = _tri(CHUNK)
        tril_b = causal.astype(BF16)
        lb = lb_ref[...]
        for c in range(cpb):
            rows = slice(c * CHUNK, (c + 1) * CHUNK)
            q = z_ref[rows, 0:HF].astype(F32)
            fz = z_ref[rows, HF:2 * HF].astype(F32)
            v = z_ref[rows, 2 * HF:3 * HF]
            hg = z_ref[rows, 3 * HF:4 * HF].astype(F32)
            _, _, _, q_in, k_in, q_out, k_st, dec = _hgrn_pre(q, fz, lb, tril_b)
            q_in, k_in, q_out, k_st = (t.astype(BF16) for t in (q_in, k_in, q_out, k_st))
            a = [jnp.where(causal, _nt(q_in[:, _hs(h)], k_in[:, _hs(h)]), 0.0).astype(BF16) for h in range(H)]
            st = [state[h] for h in range(H)]
            for h in range(H):
                st_ref[c, h] = st[h]
            o = [_nn(a[h], v[:, _hs(h)]) + _nt(q_out[:, _hs(h)], st[h].astype(BF16)) for h in range(H)]
            for h in range(H):
                state[h] = st[h] * dec[:, _hs(h)] + _tn(v[:, _hs(h)], k_st[:, _hs(h)])
            gate = _silu(hg)
            for h in range(H):
                o_ref[rows, _hs(h)] = (_rms(o[h], gn_ref[...]) * gate[:, _hs(h)]).astype(BF16)

    return pl.pallas_call(
        body, name=name, grid=(B, nblk),
        in_specs=[pl.BlockSpec((None, ts, 4 * HF), lambda b, s: (b, s, 0)),
                  pl.BlockSpec((1, HF), lambda b, s: (0, 0)),
                  pl.BlockSpec((1, HGRN_DK), lambda b, s: (0, 0))],
        out_specs=[pl.BlockSpec((None, ts, HF), lambda b, s: (b, s, 0)),
                   pl.BlockSpec((None, cpb, HGRN_HEADS, HGRN_DK, HGRN_DK), lambda b, s: (b, s, 0, 0, 0))],
        out_shape=[jax.ShapeDtypeStruct((B, S, HF), BF16),
                   jax.ShapeDtypeStruct((B, S // CHUNK, HGRN_HEADS, HGRN_DK, HGRN_DK), F32)],
        scratch_shapes=[pltpu.VMEM((HGRN_HEADS, HGRN_DK, HGRN_DK), F32)],
        compiler_params=_params("arbitrary", "arbitrary"),
    )(zh, lb, gn)


def _hgrn_bwd(zh, lb, gn, states, doa, dz, *, name):
    B, S, _ = zh.shape
    cpb = HGRN_CPB
    ts = cpb * CHUNK
    nblk = S // ts
    rev = lambda b, s: (b, nblk - 1 - s, 0)

    def body(z_ref, lb_ref, gn_ref, st_ref, do_ref, dz_in, dz_ref, dlb_ref, dgn_ref, dstate):
        @pl.when(pl.program_id(1) == 0)
        def _():
            dstate[...] = jnp.zeros_like(dstate)

        @pl.when((pl.program_id(0) == 0) & (pl.program_id(1) == 0))
        def _():
            dlb_ref[...] = jnp.zeros_like(dlb_ref)
            dgn_ref[...] = jnp.zeros_like(dgn_ref)

        H = HGRN_HEADS
        cat = lambda xs: jnp.concatenate(xs, axis=1)
        causal = _tri(CHUNK)
        tril_b = causal.astype(BF16)
        triu_b = _tri(CHUNK, upper=True).astype(BF16)
        rowid = lax.broadcasted_iota(jnp.int32, (CHUNK, HF), 0)
        lb = lb_ref[...]
        gn = gn_ref[...]
        for c in reversed(range(cpb)):
            rows = slice(c * CHUNK, (c + 1) * CHUNK)
            q = z_ref[rows, 0:HF].astype(F32)
            fz = z_ref[rows, HF:2 * HF].astype(F32)
            v = z_ref[rows, 2 * HF:3 * HF]
            hg = z_ref[rows, 3 * HF:4 * HF].astype(F32)
            sg, f, (e1, e2, e3, e4), q_in, k_in, q_out, k_st, dec = _hgrn_pre(q, fz, lb, tril_b)
            q_in_b, k_in_b, q_out_b, k_st_b = (t.astype(BF16) for t in (q_in, k_in, q_out, k_st))
            a_b = [jnp.where(causal, _nt(q_in_b[:, _hs(h)], k_in_b[:, _hs(h)]), 0.0).astype(BF16) for h in range(H)]
            st = [st_ref[c, h] for h in range(H)]
            st_b = [t.astype(BF16) for t in st]
            o = [_nn(a_b[h], v[:, _hs(h)]) + _nt(q_out_b[:, _hs(h)], st_b[h]) for h in range(H)]
            dout = do_ref[rows, :].astype(F32)
            shg = _sigmoid(hg)
            gate = hg * shg
            do_l, dgn_acc = [], jnp.zeros_like(gn)
            for h in range(H):
                _, norm_vjp = jax.vjp(_rms, o[h], gn)
                d_o, d_gn = norm_vjp(dout[:, _hs(h)] * gate[:, _hs(h)])
                do_l.append(d_o)
                dgn_acc = dgn_acc + d_gn
            dgn_ref[...] += dgn_acc
            on = cat([_rms(o[h], gn) for h in range(H)])
            dhg = dout * on * shg * (1.0 + hg * (1.0 - shg))
            do_b = [t.astype(BF16) for t in do_l]
            dst = [dstate[h] for h in range(H)]
            dst_b = [t.astype(BF16) for t in dst]
            da_b = [jnp.where(causal, _nt(do_b[h], v[:, _hs(h)]), 0.0).astype(BF16) for h in range(H)]
            dv = cat([_tn(a_b[h], do_b[h]) + _nt(k_st_b[:, _hs(h)], dst_b[h]) for h in range(H)])
            dq_in = cat([_nn(da_b[h], k_in_b[:, _hs(h)]) for h in range(H)])
            dk_in = cat([_tn(da_b[h], q_in_b[:, _hs(h)]) for h in range(H)])
            dq_out = cat([_nn(do_b[h], st_b[h]) for h in range(H)])
            dk_st = cat([_nn(v[:, _hs(h)], dst_b[h]) for h in range(H)])
            ddec = cat([jnp.sum(st[h] * dst[h], axis=0, keepdims=True) for h in range(H)])
            for h in range(H):
                dstate[h] = dst[h] * dec[:, _hs(h)] + _tn(do_b[h], q_out_b[:, _hs(h)])
            t_qin = dq_in * q_in
            t_kin = dk_in * k_in
            t_kst = dk_st * k_st
            db = t_qin - t_kin + dq_out * q_out - t_kst
            dbref = jnp.sum(t_kin - t_qin, axis=0, keepdims=True)
            dblast = jnp.sum(t_kst, axis=0, keepdims=True) + ddec * dec
            db = db + jnp.where(rowid == CHUNK // 2, dbref, 0.0) + jnp.where(rowid == CHUNK - 1, dblast, 0.0)
            dlogf = _cumsum_rows(triu_b, db)
            dqf = dq_in * e1 + dq_out * e3
            dk = dk_in * e2 + dk_st * e4
            df = dlogf / f - dk
            dfz = df * (1.0 - lb) * sg * (1.0 - sg)
            dlb_ref[...] += jnp.sum(df * (1.0 - sg), axis=0, keepdims=True)
            sq = _sigmoid(q)
            dq = dqf * sq * (1.0 + q * (1.0 - sq))
            dz_ref[rows, 0:HF] = dq.astype(BF16)
            dz_ref[rows, HF:2 * HF] = dfz.astype(BF16)
            dz_ref[rows, 2 * HF:3 * HF] = dv.astype(BF16)
            dz_ref[rows, 3 * HF:4 * HF] = dhg.astype(BF16)

    return pl.pallas_call(
        body, name=name, grid=(B, nblk),
        in_specs=[pl.BlockSpec((None, ts, 4 * HF), rev),
                  pl.BlockSpec((1, HF), lambda b, s: (0, 0)),
                  pl.BlockSpec((1, HGRN_DK), lambda b, s: (0, 0)),
                  pl.BlockSpec((None, cpb, HGRN_HEADS, HGRN_DK, HGRN_DK), lambda b, s: (b, nblk - 1 - s, 0, 0, 0)),
                  pl.BlockSpec((None, ts, HF), rev),
                  ANY],
        out_specs=[pl.BlockSpec((None, ts, 4 * HF), rev),
                   pl.BlockSpec((1, HF), lambda b, s: (0, 0)),
                   pl.BlockSpec((1, HGRN_DK), lambda b, s: (0, 0))],
        out_shape=[jax.ShapeDtypeStruct(dz.shape, BF16),
                   jax.ShapeDtypeStruct((1, HF), F32),
                   jax.ShapeDtypeStruct((1, HGRN_DK), F32)],
        input_output_aliases={5: 0},
        scratch_shapes=[pltpu.VMEM((HGRN_HEADS, HGRN_DK, HGRN_DK), F32)],
        compiler_params=_params("arbitrary", "arbitrary"),
    )(zh, lb, gn, states, doa, dz)


KV_W = ATT_KV_HEADS * ATT_HD
ATT_SCALE = ATT_HD ** -0.5


def _rope(x, cos, sin, inverse=False):
    half = ROPE_DIM // 2
    outs = []
    for p in range(x.shape[1] // 128):
        xp = x[:, p * 128:(p + 1) * 128]
        lane = lax.broadcasted_iota(jnp.int32, xp.shape, 1) % ATT_HD
        sw = jnp.where(lane < half, pltpu.roll(xp, 128 - half, 1), pltpu.roll(xp, half, 1))
        outs.append(xp * cos - sw * sin if inverse else xp * cos + sw * sin)
    return outs[0] if len(outs) == 1 else jnp.concatenate(outs, axis=1)


PAIRS_PER_KV = ATT_GROUP // 2


def _swap_halves(x):
    return pltpu.roll(x, ATT_HD, 1)


def _kv_padded(t, low):
    sw = _swap_halves(t)
    zero = jnp.zeros_like(t)
    out = []
    for g in range(ATT_KV_HEADS):
        in_low, in_high = (t, sw) if g == 0 else (sw, t)
        out.append((jnp.where(low, in_low, zero).astype(BF16), jnp.where(low, zero, in_high).astype(BF16)))
    return out


def _swa_mask(first_block):
    qi = lax.broadcasted_iota(jnp.int32, (WINDOW, 2 * WINDOW), 0)
    mi = lax.broadcasted_iota(jnp.int32, (WINDOW, 2 * WINDOW), 1)
    band = (mi > qi) & (mi <= qi + WINDOW)
    return band & (jnp.logical_not(first_block) | (mi >= WINDOW))


def _swa_specs(nb):
    cur = lambda b, i: (b, i, 0)
    prev = lambda b, i: (b, jnp.maximum(i - 1, 0), 0)
    return cur, prev


def _swa_z_specs():
    q = pl.BlockSpec((None, WINDOW, W_AQ), lambda b, i: (b, i, O_AQ // W_AQ))
    kv_prev = pl.BlockSpec((None, WINDOW, W_AKV), lambda b, i: (b, jnp.maximum(i - 1, 0), O_AKV // W_AKV))
    kv_cur = pl.BlockSpec((None, WINDOW, W_AKV), lambda b, i: (b, i, O_AKV // W_AKV))
    return q, kv_prev, kv_cur


def _swa_fwd(z, cos, sin, sinks, *, name):
    B, S, _ = z.shape
    nb = S // WINDOW
    cur, prev = _swa_specs(nb)

    def body(q_ref, kvp_ref, kvc_ref, cp_ref, sp_ref, cc_ref, sc_ref, sink_ref, o_ref, lse_ref, qr_ref, kr_ref):
        cos_c, sin_c = cc_ref[...], sc_ref[...]
        q = (_rope(q_ref[...].astype(F32), cos_c, sin_c) * ATT_SCALE).astype(BF16)
        k = jnp.concatenate([_rope(kvp_ref[:, :KV_W].astype(F32), cp_ref[...], sp_ref[...]),
                             _rope(kvc_ref[:, :KV_W].astype(F32), cos_c, sin_c)], axis=0)
        qr_ref[...] = q
        kr_ref[...] = k[WINDOW:].astype(BF16)
        v = jnp.concatenate([kvp_ref[:, KV_W:], kvc_ref[:, KV_W:]], axis=0).astype(F32)
        low = lax.broadcasted_iota(jnp.int32, k.shape, 1) < ATT_HD
        kpad = _kv_padded(k, low)
        vpad = _kv_padded(v, low)
        mask = _swa_mask(pl.program_id(1) == 0)
        lses = []
        for g in range(ATT_KV_HEADS):
            pairs = range(g * PAIRS_PER_KV, (g + 1) * PAIRS_PER_KV)
            keys = [(p, e) for p in pairs for e in (0, 1)]
            qp = {p: q[:, p * 128:(p + 1) * 128] for p in pairs}
            s = {pe: jnp.where(mask, _nt(qp[pe[0]], kpad[g][pe[1]]), NEG_INF) for pe in keys}
            pr = {}
            for pe in keys:
                sink = sink_ref[0, 2 * pe[0] + pe[1]]
                m = jnp.maximum(jnp.max(s[pe], axis=1, keepdims=True), sink)
                ex = jnp.exp(s[pe] - m)
                den = jnp.sum(ex, axis=1, keepdims=True) + jnp.exp(sink - m)
                pr[pe] = (ex * (1.0 / den)).astype(BF16)
                lses.append(m + jnp.log(den))
            for p in pairs:
                o_ref[:, p * 128:(p + 1) * 128] = (_nn(pr[p, 0], vpad[g][0]) + _nn(pr[p, 1], vpad[g][1])).astype(BF16)
        lse_ref[...] = jnp.concatenate(lses, axis=1)

    tab = lambda im: pl.BlockSpec((None, WINDOW, 128), im)
    return pl.pallas_call(
        body, name=name, grid=(B, nb),
        in_specs=[*_swa_z_specs(),
                  tab(prev), tab(prev), tab(cur), tab(cur),
                  pl.BlockSpec(memory_space=pltpu.SMEM)],
        out_specs=[pl.BlockSpec((None, WINDOW, D_MODEL), cur), pl.BlockSpec((None, WINDOW, ATT_HEADS), cur),
                   pl.BlockSpec((None, WINDOW, D_MODEL), cur), pl.BlockSpec((None, WINDOW, KV_W), cur)],
        out_shape=[jax.ShapeDtypeStruct((B, S, D_MODEL), BF16), jax.ShapeDtypeStruct((B, S, ATT_HEADS), F32),
                   jax.ShapeDtypeStruct((B, S, D_MODEL), BF16), jax.ShapeDtypeStruct((B, S, KV_W), BF16)],
        compiler_params=_params("parallel", "parallel"),
    )(z, z, z, cos, sin, cos, sin, sinks)


def _swa_bwd(z, qr, kr, cos, sin, sinks, lse, dob, dz, *, name):
    B, S, _ = z.shape
    nb = S // WINDOW
    cur, prev = _swa_specs(nb)

    def body(q_ref, krp_ref, krc_ref, kvp_ref, kvc_ref, cp_ref, sp_ref, cc_ref, sc_ref, sink_ref, lse_ref, do_ref, dz_in,
             dq_ref, dkc_ref, dkp_ref, dsink_ref):
        @pl.when((pl.program_id(0) == 0) & (pl.program_id(1) == 0))
        def _():
            dsink_ref[...] = jnp.zeros_like(dsink_ref)

        cos_c, sin_c, cos_p, sin_p = cc_ref[...], sc_ref[...], cp_ref[...], sp_ref[...]
        q = q_ref[...]
        k = jnp.concatenate([krp_ref[...], krc_ref[...]], axis=0).astype(F32)
        v = jnp.concatenate([kvp_ref[:, KV_W:], kvc_ref[:, KV_W:]], axis=0).astype(F32)
        low = lax.broadcasted_iota(jnp.int32, k.shape, 1) < ATT_HD
        kpad = _kv_padded(k, low)
        vpad = _kv_padded(v, low)
        mask = _swa_mask(pl.program_id(1) == 0)
        lse = lse_ref[...]
        dq_parts, dk_sum, dv_sum, dsinks = [], [], [], []
        for g in range(ATT_KV_HEADS):
            pairs = range(g * PAIRS_PER_KV, (g + 1) * PAIRS_PER_KV)
            keys = [(p, e) for p in pairs for e in (0, 1)]
            qp = {p: q[:, p * 128:(p + 1) * 128] for p in pairs}
            dop = {p: do_ref[:, p * 128:(p + 1) * 128] for p in pairs}
            s = {pe: jnp.where(mask, _nt(qp[pe[0]], kpad[g][pe[1]]), NEG_INF) for pe in keys}
            dp = {pe: _nt(dop[pe[0]], vpad[g][pe[1]]) for pe in keys}
            pr, ds = {}, {}
            for pe in keys:
                h = 2 * pe[0] + pe[1]
                lse_h = lse[:, h:h + 1]
                pf = jnp.exp(s[pe] - lse_h)
                delta = jnp.sum(pf * dp[pe], axis=1, keepdims=True)
                ds[pe] = (pf * (dp[pe] - delta)).astype(BF16)
                pr[pe] = pf.astype(BF16)
                p_sink = jnp.exp(sink_ref[0, h] - lse_h)
                dsinks.append(-jnp.sum(p_sink * delta, axis=0, keepdims=True))
            for p in pairs:
                dq_parts.append((_nn(ds[p, 0], kpad[g][0]) + _nn(ds[p, 1], kpad[g][1])) * ATT_SCALE)
            x = [sum(_tn(ds[p, e], qp[p]) for p in pairs) for e in (0, 1)]
            y = [sum(_tn(pr[p, e], dop[p]) for p in pairs) for e in (0, 1)]
            zk = jnp.where(low, x[0], x[1])
            zv = jnp.where(low, y[0], y[1])
            dk_sum.append(zk + _swap_halves(zk))
            dv_sum.append(zv + _swap_halves(zv))
        dq_ref[...] = _rope(jnp.concatenate(dq_parts, axis=1), cos_c, sin_c, inverse=True).astype(BF16)
        dk = jnp.where(low, dk_sum[0], dk_sum[1])
        dv = jnp.where(low, dv_sum[0], dv_sum[1])
        dkp_ref[:, :KV_W] = _rope(dk[:WINDOW], cos_p, sin_p, inverse=True)
        dkp_ref[:, KV_W:] = dv[:WINDOW]
        dkc_ref[:, :KV_W] = _rope(dk[WINDOW:], cos_c, sin_c, inverse=True)
        dkc_ref[:, KV_W:] = dv[WINDOW:]
        dsink_ref[...] += jnp.concatenate(dsinks, axis=1)

    tab = lambda im: pl.BlockSpec((None, WINDOW, 128), im)
    return pl.pallas_call(
        body, name=name, grid=(B, nb),
        in_specs=[pl.BlockSpec((None, WINDOW, D_MODEL), cur), tab(prev), tab(cur),
                  *_swa_z_specs()[1:],
                  tab(prev), tab(prev), tab(cur), tab(cur),
                  pl.BlockSpec(memory_space=pltpu.SMEM),
                  pl.BlockSpec((None, WINDOW, ATT_HEADS), cur),
                  pl.BlockSpec((None, WINDOW, D_MODEL), cur),
                  ANY],
        out_specs=[_swa_z_specs()[0],
                   pl.BlockSpec((None, WINDOW, 2 * KV_W), cur), pl.BlockSpec((None, WINDOW, 2 * KV_W), cur),
                   pl.BlockSpec((1, ATT_HEADS), lambda b, i: (0, 0))],
        out_shape=[jax.ShapeDtypeStruct(dz.shape, BF16),
                   jax.ShapeDtypeStruct((B, S, 2 * KV_W), F32), jax.ShapeDtypeStruct((B, S, 2 * KV_W), F32),
                   jax.ShapeDtypeStruct((1, ATT_HEADS), F32)],
        input_output_aliases={12: 0},
        compiler_params=_params("arbitrary", "arbitrary"),
    )(qr, kr, kr, z, z, cos, sin, cos, sin, sinks, lse, dob, dz)


def _swa_dkv_combine(dkv_cur, dkv_prev, dz, *, name):
    B, S, W = dkv_cur.shape

    def body(c_ref, p_ref, dz_in, o_ref):
        rows = lax.broadcasted_iota(jnp.int32, (S, W), 0)
        o_ref[...] = (c_ref[...] + _shift_up(p_ref[...], WINDOW, rows, S)).astype(BF16)

    seq = pl.BlockSpec((None, S, W), lambda b: (b, 0, 0))
    return pl.pallas_call(
        body, name=name, grid=(B,),
        in_specs=[seq, seq, ANY], out_specs=pl.BlockSpec((None, S, W), lambda b: (b, 0, O_AKV // W_AKV)),
        out_shape=jax.ShapeDtypeStruct(dz.shape, BF16),
        input_output_aliases={2: 0},
        compiler_params=_params("parallel"),
    )(dkv_cur, dkv_prev, dz)


def _rope_tables(positions):
    half = ROPE_DIM // 2
    inv = ROPE_THETA ** (-2.0 * jnp.arange(half, dtype=F32) / ROPE_DIM)
    ang = positions.astype(F32)[..., None] * inv
    c, s = jnp.cos(ang), jnp.sin(ang)
    pad = jnp.zeros(ang.shape[:-1] + (ATT_HD - ROPE_DIM,), F32)
    cos = jnp.concatenate([c, c, pad + 1.0], axis=-1)
    sin = jnp.concatenate([-s, s, pad], axis=-1)
    return jnp.tile(cos, (1, 1, 2)), jnp.tile(sin, (1, 1, 2))


def _lower_bound(lb_logits, *, name):
    def body(l_ref, o_ref):
        l = l_ref[...]
        e = jnp.exp(l - jnp.max(l, axis=0, keepdims=True))
        o_ref[...] = e[0:1] / jnp.sum(e, axis=0, keepdims=True)

    return pl.pallas_call(body, name=name, out_shape=jax.ShapeDtypeStruct((1, lb_logits.shape[1]), F32))(lb_logits)


W_ZH, W_GATES, W_AQ, W_AKV = 4 * HF, 2 * D_MODEL, ATT_HEADS * ATT_HD, 2 * KV_W
O_ZH, O_GATES, O_AQ, O_AKV = 0, W_ZH, W_ZH + W_GATES, W_ZH + W_GATES + W_AQ
W_IN = W_ZH + W_GATES + W_AQ + W_AKV


W_IN_BLK = W_IN // N_DEV


def _reference_row_block(j, rows=256):
    nz, ng = W_ZH // rows, W_GATES // rows
    return jnp.where(j < nz, j, jnp.where(j < nz + ng, j + (W_AQ + W_AKV) // rows, j - ng))


def _reordered_rows(w_t, *, name):
    rows = 256

    def body(i_ref, o_ref):
        o_ref[...] = i_ref[...]

    return pl.pallas_call(
        body, name=name, grid=(W_IN // rows,),
        in_specs=[pl.BlockSpec((rows, D_MODEL), lambda j: (_reference_row_block(j, rows), 0))],
        out_specs=pl.BlockSpec((rows, D_MODEL), lambda j: (j, 0)),
        out_shape=jax.ShapeDtypeStruct(w_t.shape, w_t.dtype), compiler_params=_params("parallel"))(w_t)


def _local_step(x, positions, target, small, w_in_t, rest_weights, emit, start_token):
    B, S, D = x.shape
    T = B * S
    x2 = x.reshape(T, D)
    cos, sin = _rope_tables(positions)
    lb = _lower_bound(small["lb_logits"], name="lb_fwd")
    zero = lambda tok: tok[0:1, 0:1]

    u1 = _norm_cast(x2, small["norm1_g"] + zero(start_token), name="norm1")
    z = _matmul(u1, w_in_t, tb=True, out_dtype=BF16, name="mm_z", tm=1024, tn=W_IN // 2)
    z3 = z.reshape(B, S, W_IN)
    oa, states = _hgrn_fwd(z3, lb, small["hgrn_norm_g"], name="hgrn_fwd")
    ob, lse, qr, kr = _swa_fwd(z3, cos, sin, small["attn_sinks"], name="swa_fwd")
    oa2 = oa.reshape(T, D)
    ob2 = ob.reshape(T, D)
    W = rest_weights("mix", ob)
    pa = _matmul(oa2, W["w_a"], out_dtype=BF16, name="mm_pa", tm=1024, tn=1024)
    pb = _matmul(ob2, W["w_b"], out_dtype=BF16, name="mm_pb", tm=1024, tn=1024)
    merged = _merge_fwd(z, pa, pb, name="merge_fwd")
    h = _matmul(merged, W["w_out"], addend=x2, name="mm_h", tm=1024, tn=1024)
    u2 = _norm_cast(h, small["norm2_g"], name="norm2")
    W.update(rest_weights("ffn", u2))
    gu = _matmul(u2, W["w_ffn_t"], tb=True, out_dtype=BF16, name="mm_gu", tm=2048, tn=512)
    gu3 = gu.reshape(B, S, 2 * D_FF)
    act, a_pre = _conv_act_fwd(gu3, W["conv_w"], small["conv_b"], name="conv_act_fwd")
    act2 = act.reshape(T, D_FF)
    h2 = _matmul(act2, W["w_down"], addend=h, name="mm_h2", tm=1024, tn=1024)

    g = {}
    dh2, dh2b, g["final_g"], loss = _final_loss_bwd(h2, small["final_g"].reshape(1, D), target.reshape(T, D), name="final_loss_bwd")
    dact = _matmul(dh2b, W["w_down"], tb=True, out_dtype=BF16, name="mm_dact", tm=1024, tn=D_FF)
    dw_down_t = _matmul(dh2b, act2, ta=True, out_dtype=BF16, name="mm_dw_down", tm=1024, tn=256, tk=8192)
    dg_, dup, g["conv_w"], g["conv_b"] = _conv_act_bwd(gu3, a_pre, W["conv_w"], dact.reshape(B, S, D_FF), name="conv_act_bwd")
    dg2 = dg_.reshape(T, D_FF)
    dup2 = dup.reshape(T, D_FF)
    du2 = _matmul(dg2, W["w_ffn_t"], name="mm_du2_g", tm=1024, tn=1024, b_koff=0)
    du2 = _matmul(dup2, W["w_ffn_t"], addend=du2, out_dtype=BF16, name="mm_du2_u", tm=1024, tn=1024, b_koff=1)
    dw_ffn_t = _matmul(u2, dg2, ta=True, out_t=True, out_dtype=BF16, into=lax.empty((2 * D_FF, D), BF16), o_noff=0, name="mm_dw_ffn_g", tm=1024, tn=256, tk=8192)
    dw_ffn_t = _matmul(u2, dup2, ta=True, out_t=True, out_dtype=BF16, into=dw_ffn_t, o_noff=D_FF // 256, name="mm_dw_ffn_u", tm=1024, tn=256, tk=8192)
    tok = emit("ffn", dict(w_ffn_t=dw_ffn_t, w_down=dw_down_t.T))
    dh, dhb, g["norm2_g"] = _norm_bwd_add(h, small["norm2_g"] + zero(tok), du2, dh2, name="norm2_bwd")
    dmerged = _matmul(dhb, W["w_out"], tb=True, out_dtype=BF16, name="mm_dmerged", tm=1024, tn=1024)
    dw_out = _matmul(merged, dhb, ta=True, out_dtype=BF16, name="mm_dw_out", tm=1024, tn=1024, tk=2048)
    dz, dpa, dpb = _merge_bwd(z, pa, pb, dmerged, lax.empty((T, W_IN), BF16), name="merge_bwd")
    doa =_matmul(dpa, W["w_a"], tb=True, out_dtype=BF16, name="mm_doa", tm=1024, tn=1024)
    dw_a = _matmul(oa2, dpa, ta=True, out_dtype=BF16, name="mm_dw_a", tm=1024, tn=1024, tk=2048)
    dob = _matmul(dpb, W["w_b"], tb=True, out_dtype=BF16, name="mm_dob", tm=1024, tn=1024)
    dw_b = _matmul(ob2, dpb, ta=True, out_dtype=BF16, name="mm_dw_b", tm=1024, tn=1024, tk=2048)
    tok = emit("mix", dict(w_out=dw_out, w_a=dw_a, w_b=dw_b))
    dz3, dkv_cur, dkv_prev, dsinks = _swa_bwd(z3, qr, kr, cos, sin, small["attn_sinks"] + zero(tok), lse, dob.reshape(B, S, D),
                                              dz.reshape(B, S, W_IN), name="swa_bwd")
    dz3 = _swa_dkv_combine(dkv_cur, dkv_prev, dz3, name="swa_dkv")
    g["attn_sinks"] = dsinks
    dz3, g["lb"], g["hgrn_norm_g"] = _hgrn_bwd(z3, lb, small["hgrn_norm_g"], states, doa.reshape(B, S, D), dz3, name="hgrn_bwd")
    dz = dz3.reshape(T, W_IN)
    dw_in_t = _matmul(u1, dz, ta=True, out_t=True, o_block_perm=_reference_row_block, out_dtype=BF16, name="mm_dw_in", tm=1024, tn=256, tk=8192)
    tok = emit("in", dict(w_in_t=dw_in_t))
    du1 = _matmul(dz, w_in_t, after=tok, out_dtype=BF16, name="mm_du1", tm=1024, tn=512)
    dx, g["norm1_g"] = _norm_bwd_add(x2, small["norm1_g"], du1, dh, with_bf16=False, name="norm1_bwd")
    g["lb_logits"] = _lb_bwd(g.pop("lb"), lb, name="lb_bwd")
    return loss, dx.reshape(B, S, D), g


def _my_place():
    return lax.axis_index("x"), lax.axis_index("y"), lax.axis_index("c")


def _gather_blocks(x_ref, out_ref, send_sems, recv_sems, local_sem):
    x, y, c = _my_place()
    me, sibling = (x, y, c), (x, y, 1 - c)
    chips = [(1 - x, y), (x, 1 - y), (1 - x, 1 - y)]

    def slot(px, py, pc):
        return out_ref.at[4 * px + 2 * py + pc]

    def copy(k, block, to, src=None):
        return pltpu.make_async_remote_copy(
            src_ref=slot(*block) if src is None else src, dst_ref=slot(*block),
            send_sem=send_sems.at[k], recv_sem=recv_sems.at[k], device_id=to, device_id_type=MESH)

    mine = pltpu.make_async_copy(x_ref, slot(*me), local_sem)
    mine.start()
    first = [copy(0, me, sibling, src=x_ref)]
    first += [copy(1 + j, me, (*chip, c), src=x_ref) for j, chip in enumerate(chips)]
    for cp in first:
        cp.start()
    passed = [copy(4 + j, (*chip, c), sibling) for j, chip in enumerate(chips)]
    for j, chip in enumerate(chips):
        copy(1 + j, (*chip, c), me).wait_recv()
        passed[j].start()
    copy(0, sibling, me).wait_recv()
    for j, chip in enumerate(chips):
        copy(4 + j, (*chip, 1 - c), me).wait_recv()
    for cp in first + passed:
        cp.wait_send()
    mine.wait()


GATHER_SEMS = [pltpu.SemaphoreType.DMA((7,)), pltpu.SemaphoreType.DMA((7,)), pltpu.SemaphoreType.DMA]


def _all_gather(blk, *, name):
    return pl.pallas_call(
        _gather_body_fn(), name=name,
        out_shape=jax.ShapeDtypeStruct((N_DEV,) + blk.shape, blk.dtype),
        in_specs=[ANY], out_specs=ANY,
        scratch_shapes=GATHER_SEMS,
    )(blk)


def _gather_body_fn():
    def body(x_ref, out_ref, send_sems, recv_sems, local_sem):
        _gather_blocks(x_ref, out_ref, send_sems, recv_sems, local_sem)
    return body


SLAB_W = 1152
SMALL_SHAPES = dict(norm1_g=(1, D_MODEL), lb_logits=(2, HGRN_HEADS * HGRN_DK), hgrn_norm_g=(1, HGRN_DK), attn_sinks=(1, ATT_HEADS),
                    norm2_g=(1, D_MODEL), conv_b=(1, D_FF), final_g=(1, D_MODEL))
CONVW_BLK = D_FF // N_DEV
CONVW_STRIDE = SLAB_W // 3


def _slab_layout():
    layout, r = {}, 0
    for nm, (nr, w) in SMALL_SHAPES.items():
        layout[nm] = []
        for i in range(nr):
            for c0 in range(0, w, SLAB_W):
                layout[nm].append((r, i, c0, min(SLAB_W, w - c0)))
                r += 1
    return layout, r


SMALL_ROWS, _N_SMALL_ROWS = _slab_layout()
CONV_ROW0 = -(-_N_SMALL_ROWS // 8) * 8
LOSS_ROW = CONV_ROW0 + N_DEV
SLAB_ROWS = LOSS_ROW + 8


def _small_step(grads, g_conv_w, loss, params, moments, variances, dev, *, name):
    names = list(SMALL_ROWS)
    n = len(names)

    def body(dev_ref, *refs):
        g_refs = dict(zip(names, refs[:n]))
        gc_ref, loss_ref = refs[n], refs[n + 1]
        base = n + 2
        w_refs, m_refs, v_refs = (dict(zip(names + ["conv_w"], refs[base + i * (n + 1):base + (i + 1) * (n + 1)])) for i in range(3))
        o = base + 3 * (n + 1)
        gath_ref, loss_out = refs[o], refs[o + 1]
        outs = {nm: refs[o + 2 + 4 * i:o + 6 + 4 * i] for i, nm in enumerate(names + ["conv_w"])}
        slab, total, send_sems, recv_sems, local_sem = refs[-5:]

        slab[...] = jnp.zeros_like(slab)
        for nm, pieces in SMALL_ROWS.items():
            for r, i, c0, w in pieces:
                slab[r:r + 1, 0:w] = g_refs[nm][i:i + 1, c0:c0 + w]
        for p in range(N_DEV):
            for j in range(3):
                slab[CONV_ROW0 + p:CONV_ROW0 + p + 1, j * CONVW_STRIDE:j * CONVW_STRIDE + CONVW_BLK] = gc_ref[j:j + 1, p * CONVW_BLK:(p + 1) * CONVW_BLK]
        slab[LOSS_ROW:LOSS_ROW + 1, 0:1] = loss_ref[...]
        _gather_blocks(slab, gath_ref, send_sems, recv_sems, local_sem)
        acc = gath_ref[0]
        for p in range(1, N_DEV):
            acc = acc + gath_ref[p]
        total[...] = acc
        loss_out[...] = total[LOSS_ROW:LOSS_ROW + 1, 0:1]

        def update(nm, g, i, c0, w):
            at = (slice(i, i + 1), slice(c0, c0 + w))
            d, mn, vn = _adamw_math(w_refs[nm][at], g, m_refs[nm][at], v_refs[nm][at])
            for ref, val in zip(outs[nm], (g, d, mn, vn)):
                ref[at] = val

        for nm, pieces in SMALL_ROWS.items():
            for r, i, c0, w in pieces:
                update(nm, total[r:r + 1, 0:w], i, c0, w)
        conv_rows = total[CONV_ROW0:CONV_ROW0 + N_DEV, :]
        rowid = lax.broadcasted_iota(jnp.int32, conv_rows.shape, 0)
        mine = jnp.sum(jnp.where(rowid == dev_ref[0], conv_rows, 0.0), axis=0, keepdims=True)
        for j in range(3):
            update("conv_w", mine[:, j * CONVW_STRIDE:j * CONVW_STRIDE + CONVW_BLK], j, 0, CONVW_BLK)

    order = names + ["conv_w"]
    ins = [grads[nm] for nm in names] + [g_conv_w, loss]
    for d in (params, moments, variances):
        ins += [d[nm] for nm in order]
    vmem = pl.BlockSpec(memory_space=pltpu.VMEM)
    out_shape = [jax.ShapeDtypeStruct((N_DEV, SLAB_ROWS, SLAB_W), F32), jax.ShapeDtypeStruct((1, 1), F32)]
    for nm in order:
        out_shape += [jax.ShapeDtypeStruct(params[nm].shape, F32)] * 4
    res = pl.pallas_call(
        body, name=name,
        grid_spec=pltpu.PrefetchScalarGridSpec(
            num_scalar_prefetch=1, grid=(1,),
            in_specs=[vmem] * len(ins), out_specs=[vmem] * len(out_shape),
            scratch_shapes=[pltpu.VMEM((SLAB_ROWS, SLAB_W), F32), pltpu.VMEM((SLAB_ROWS, SLAB_W), F32)] + GATHER_SEMS),
        out_shape=out_shape,
    )(dev, *ins)
    return res[1], {nm: tuple(res[2 + 4 * i:6 + 4 * i]) for i, nm in enumerate(order)}


HBM_SPEC = pl.BlockSpec(memory_space=pltpu.HBM)
SEM_SPEC = pl.BlockSpec(memory_space=pltpu.SEMAPHORE)
DATAFLOW_EFFECT = pltpu.SideEffectType.DATAFLOW_SIDE_EFFECTING
N_PEERS = N_DEV - 1


def _peers(x, y, c):
    return [(1 - x if r & 4 else x, 1 - y if r & 2 else y, 1 - c if r & 1 else c) for r in range(1, N_DEV)]


def _exchange_start(srcs, scatter, *, after=None, name):
    n = len(srcs)
    lands = [lax.empty(a.shape if scatter else (N_DEV,) + a.shape, a.dtype) for a in srcs]
    extra = [] if after is None else [after]

    def body(*refs):
        src_refs, land_refs = refs[:n], refs[n:2 * n]
        send_sems, recv_sems, token = refs[2 * n + len(extra)], refs[2 * n + len(extra) + 1], refs[-1]
        x, y, c = _my_place()
        me = 4 * x + 2 * y + c
        for i in range(n):
            for r, (tx, ty, tc) in enumerate(_peers(x, y, c)):
                src = src_refs[i].at[4 * tx + 2 * ty + tc] if scatter else src_refs[i]
                pltpu.make_async_remote_copy(
                    src_ref=src, dst_ref=land_refs[i].at[me], send_sem=send_sems.at[N_PEERS * i + r],
                    recv_sem=recv_sems.at[N_PEERS * i + r], device_id=(tx, ty, tc), device_id_type=MESH).start()
        token[...] = jnp.zeros_like(token)

    thru = [pltpu.HBM(a.shape, a.dtype) for a in list(srcs) + lands]
    res = pl.pallas_call(
        body, name=name,
        out_shape=(pltpu.SemaphoreType.DMA((N_PEERS * n,)), pltpu.SemaphoreType.DMA((N_PEERS * n,)), *thru,
                   jax.ShapeDtypeStruct((8, 128), F32)),
        in_specs=[HBM_SPEC] * (2 * n) + [ANY] * len(extra),
        out_specs=(SEM_SPEC, SEM_SPEC, *([HBM_SPEC] * (2 * n)), pl.BlockSpec(memory_space=pltpu.VMEM)),
        input_output_aliases={i: 2 + i for i in range(2 * n)},
        compiler_params=pltpu.CompilerParams(has_side_effects=DATAFLOW_EFFECT),
    )(*[pltpu.with_memory_space_constraint(a, pltpu.HBM) for a in list(srcs) + lands], *extra)
    return (res[0], res[1], list(res[2:2 + n]), list(res[2 + n:2 + 2 * n]), scatter), res[-1]


def _exchange_wait(handle, after, *, name):
    send_sems, recv_sems, srcs, lands, scatter = handle
    n = len(srcs)

    def body(*refs):
        src_refs, land_refs = refs[:n], refs[n:2 * n]
        send_sems, recv_sems = refs[2 * n], refs[2 * n + 1]
        x, y, c = _my_place()
        for i in range(n):
            for r in range(N_PEERS):
                src = src_refs[i].at[0] if scatter else src_refs[i]
                cp = pltpu.make_async_remote_copy(
                    src_ref=src, dst_ref=land_refs[i].at[0], send_sem=send_sems.at[N_PEERS * i + r],
                    recv_sem=recv_sems.at[N_PEERS * i + r], device_id=(x, y, c), device_id_type=MESH)
                cp.wait_send()
                cp.wait_recv()

    thru = [pltpu.HBM(a.shape, a.dtype) for a in srcs + lands]
    res = pl.pallas_call(
        body, name=name, out_shape=tuple(thru),
        in_specs=[HBM_SPEC] * (2 * n) + [SEM_SPEC, SEM_SPEC, ANY], out_specs=tuple([HBM_SPEC] * (2 * n)),
        input_output_aliases={i: i for i in range(2 * n)},
        compiler_params=pltpu.CompilerParams(has_side_effects=DATAFLOW_EFFECT),
    )(*srcs, *lands, send_sems, recv_sems, after)
    return list(res[:n]), list(res[n:])


def _with_own(land, own, me):
    return lax.dynamic_update_index_in_dim(land, own, me, 0)


def _adamw_math(w, g, m, v):
    m = ADAM_B1 * m + (1.0 - ADAM_B1) * g
    v = ADAM_B2 * v + (1.0 - ADAM_B2) * (g * g)
    m_hat = m / (1.0 - ADAM_B1 ** ADAM_STEP)
    v_hat = v / (1.0 - ADAM_B2 ** ADAM_STEP)
    delta = -ADAM_LR * (m_hat / (jnp.sqrt(v_hat) + ADAM_EPS) + ADAM_WD * w)
    return delta, m, v


def _adamw_sum(parts, w, m, v, *, name):
    shape = w.shape
    R, n = shape[-2], shape[-1]
    w, m, v = (t.reshape(R, n) for t in (w, m, v))
    tr = _pick(R, (256, 464, 352, 128))

    def body(p_ref, w_ref, m_ref, v_ref, g_ref, d_ref, mo_ref, vo_ref):
        g = p_ref[0].astype(F32)
        for p in range(1, N_DEV):
            g = g + p_ref[p].astype(F32)
        d, mn, vn = _adamw_math(w_ref[...], g, m_ref[...], v_ref[...])
        g_ref[...] = g
        d_ref[...] = d
        mo_ref[...] = mn
        vo_ref[...] = vn

    row = pl.BlockSpec((tr, n), lambda i: (i, 0))
    outs = pl.pallas_call(
        body, name=name, grid=(R // tr,),
        in_specs=[pl.BlockSpec((N_DEV, tr, n), lambda i: (0, i, 0)), row, row, row],
        out_specs=[row, row, row, row],
        out_shape=[jax.ShapeDtypeStruct((R, n), F32)] * 4,
        compiler_params=_params("parallel"),
    )(parts, w, m, v)
    return [t.reshape(shape) for t in outs]


def _lb_bwd(dlb, lb, *, name):
    def body(d_ref, lb_ref, o_ref):
        t = d_ref[...] * lb_ref[...] * (1.0 - lb_ref[...])
        o_ref[0:1, :] = t
        o_ref[1:2, :] = -t

    return pl.pallas_call(body, name=name, out_shape=jax.ShapeDtypeStruct((2, lb.shape[1]), F32))(dlb, lb)


DOWN_BLK, ROW_BLK = D_FF // N_DEV, D_MODEL // N_DEV
W_FFN_BLK = 2 * D_FF // N_DEV
CONV_BITS_SHAPE = (16, 256)


def kernel(x, positions, norm1_g, w_in, lb_logits, hgrn_norm_g, w_a, attn_sinks, w_b, w_out, norm2_g, w_ffn_in, conv_w, conv_b, w_down, final_g, loss_target, m_norm1_g, m_w_in, m_lb_logits, m_hgrn_norm_g, m_w_a, m_attn_sinks, m_w_b, m_w_out, m_norm2_g, m_w_ffn_in, m_conv_w, m_conv_b, m_w_down, m_final_g, v_norm1_g, v_w_in, v_lb_logits, v_hgrn_norm_g, v_w_a, v_attn_sinks, v_w_b, v_w_out, v_norm2_g, v_w_ffn_in, v_conv_w, v_conv_b, v_w_down, v_final_g):
    xi, yi, ci = _my_place()
    dev = 4 * xi + 2 * yi + ci

    tr = lambda t: jnp.transpose(t[0])
    untr = lambda t: jnp.transpose(t)[None]
    w_in_blocks = _all_gather(tr(w_in).astype(BF16), name="ag_w_in")
    conv_bits = lax.bitcast_convert_type(conv_w, BF16).reshape(-1)
    conv_bits = jnp.pad(conv_bits, (0, CONV_BITS_SHAPE[0] * CONV_BITS_SHAPE[1] - conv_bits.shape[0])).reshape(CONV_BITS_SHAPE)
    w_in_full_t = _reordered_rows(w_in_blocks.reshape(W_IN, D_MODEL), name="w_in_rows")
    gather_handles = {}
    gather_handles["mix"], tok_mix = _exchange_start([w_a[0].astype(BF16), w_b[0].astype(BF16), w_out[0].astype(BF16)], False,
                                                     after=w_in_full_t, name="ag_mix_start")
    gather_handles["ffn"], tok_ffn = _exchange_start([tr(w_ffn_in).astype(BF16), w_down[0].astype(BF16), conv_bits], False,
                                                     after=tok_mix, name="ag_ffn_start")
    start_token = tok_mix + tok_ffn

    def rest_weights(group, after):
        own, lands = _exchange_wait(gather_handles[group], after, name="ag_" + group + "_wait")
        full = [_with_own(l, o, dev) for l, o in zip(lands, own)]
        if group == "mix":
            return dict(zip(("w_a", "w_b", "w_out"), [t.reshape(D_MODEL, D_MODEL) for t in full]))
        bits = full[2].reshape(N_DEV, -1)[:, :3 * CONVW_BLK * 2].reshape(N_DEV, 3, CONVW_BLK, 2)
        return dict(w_ffn_t=full[0].reshape(2 * D_FF, D_MODEL), w_down=full[1].reshape(D_FF, D_MODEL),
                    conv_w=lax.bitcast_convert_type(bits, F32).transpose(1, 0, 2).reshape(3, D_FF))

    handles = {}

    def emit(group, gr):
        if group == "ffn":
            srcs = [gr["w_ffn_t"].reshape(N_DEV, W_FFN_BLK, D_MODEL), gr["w_down"].reshape(N_DEV, DOWN_BLK, D_MODEL)]
        elif group == "mix":
            srcs = [gr[n].reshape(N_DEV, ROW_BLK, D_MODEL) for n in ("w_out", "w_a", "w_b")]
        else:
            srcs = [gr["w_in_t"].reshape(N_DEV, W_IN_BLK, D_MODEL)]
        handles[group], token = _exchange_start(srcs, True, name="rs_" + group + "_start")
        return token

    small = dict(norm1_g=norm1_g, lb_logits=lb_logits, hgrn_norm_g=hgrn_norm_g, attn_sinks=attn_sinks, norm2_g=norm2_g,
                 conv_b=conv_b, final_g=final_g)
    loss, grad_x, g = _local_step(x, positions, loss_target, small, w_in_full_t, rest_weights, emit, start_token)

    def parts_of(group, after):
        srcs, lands = _exchange_wait(handles[group], after, name="rs_" + group + "_wait")
        return [_with_own(l, lax.dynamic_index_in_dim(s, dev, 0, keepdims=False), dev) for s, l in zip(srcs, lands)]

    p_ffn, p_down = parts_of("ffn", grad_x)
    p_out, p_a, p_b = parts_of("mix", grad_x)
    (p_in,) = parts_of("in", grad_x)
    big = dict(
        w_in=[untr(t) for t in _adamw_sum(p_in, tr(w_in), tr(m_w_in), tr(v_w_in), name="adamw_w_in")],
        w_a=_adamw_sum(p_a, w_a, m_w_a, v_w_a, name="adamw_w_a"),
        w_b=_adamw_sum(p_b, w_b, m_w_b, v_w_b, name="adamw_w_b"),
        w_out=_adamw_sum(p_out, w_out, m_w_out, v_w_out, name="adamw_w_out"),
        w_ffn_in=[untr(t) for t in _adamw_sum(p_ffn, tr(w_ffn_in), tr(m_w_ffn_in), tr(v_w_ffn_in), name="adamw_w_ffn_in")],
        w_down=_adamw_sum(p_down, w_down, m_w_down, v_w_down, name="adamw_w_down"),
    )

    row = lambda t: t.reshape(1, -1) if t.ndim == 1 else t
    shard = lambda t: t.reshape(3, CONVW_BLK)
    sm_g = {nm: g[nm] for nm in SMALL_ROWS}
    sm_w = dict(norm1_g=norm1_g, lb_logits=lb_logits, hgrn_norm_g=hgrn_norm_g, attn_sinks=attn_sinks, norm2_g=norm2_g,
                conv_b=conv_b, final_g=row(final_g), conv_w=shard(conv_w))
    sm_m = dict(norm1_g=m_norm1_g, lb_logits=m_lb_logits, hgrn_norm_g=m_hgrn_norm_g, attn_sinks=m_attn_sinks, norm2_g=m_norm2_g,
                conv_b=m_conv_b, final_g=row(m_final_g), conv_w=shard(m_conv_w))
    sm_v = dict(norm1_g=v_norm1_g, lb_logits=v_lb_logits, hgrn_norm_g=v_hgrn_norm_g, attn_sinks=v_attn_sinks, norm2_g=v_norm2_g,
                conv_b=v_conv_b, final_g=row(v_final_g), conv_w=shard(v_conv_w))
    loss_total, sm_out = _small_step(sm_g, g["conv_w"], loss, sm_w, sm_m, sm_v, dev.astype(jnp.int32).reshape(1), name="small_step")
    shapes = dict(final_g=final_g.shape, conv_w=conv_w.shape)

    names = ("norm1_g", "w_in", "lb_logits", "hgrn_norm_g", "w_a", "attn_sinks", "w_b", "w_out", "norm2_g", "w_ffn_in", "conv_w", "conv_b", "w_down", "final_g")
    outs = [loss_total.reshape(()), grad_x]
    for kind in range(4):
        outs += [big[n][kind] if n in big else sm_out[n][kind].reshape(shapes.get(n, sm_out[n][kind].shape)) for n in names]
    return tuple(outs)
```

```python
import functools

import jax
import jax.numpy as jnp
from jax import lax
from jax.experimental import pallas as pl
from jax.experimental.pallas import tpu as pltpu

F32 = jnp.float32
BF16 = jnp.bfloat16

D_MODEL = 1024
HGRN_HEADS = 8
HGRN_DK = 128
CHUNK = 64
ATT_HEADS = 16
ATT_KV_HEADS = 2
ATT_HD = 64
ATT_GROUP = ATT_HEADS // ATT_KV_HEADS
WINDOW = 128
ROPE_DIM = ATT_HD // 4
ROPE_THETA = 500000.0
D_FF = 2816
EPS = 1e-6
NEG_INF = -1e30
N_DEV = 8

ADAM_LR = 0.001
ADAM_B1 = 0.9
ADAM_B2 = 0.999
ADAM_EPS = 1e-08
ADAM_WD = 0.01
ADAM_STEP = 10

MESH = pl.DeviceIdType.MESH
ANY = pl.BlockSpec(memory_space=pl.ANY)


def _pick(n, cands):
    for c in cands:
        if n % c == 0:
            return c
    return n


def _sigmoid(x):
    return 0.5 * jnp.tanh(0.5 * x) + 0.5


def _silu(x):
    hx = 0.5 * x
    return hx * jnp.tanh(hx) + hx


def _rms(x, g):
    return x * lax.rsqrt(jnp.mean(x * x, axis=-1, keepdims=True) + EPS) * g


def _dot(a, b, dims):
    return lax.dot_general(a, b, (dims, ((), ())), preferred_element_type=F32)


def _nn(a, b):
    return _dot(a, b, ((1,), (0,)))


def _nt(a, b):
    return _dot(a, b, ((1,), (1,)))


def _tn(a, b):
    return _dot(a, b, ((0,), (0,)))


def _params(*sem):
    return pltpu.CompilerParams(dimension_semantics=sem, vmem_limit_bytes=56 * 1024 * 1024)


def _matmul(a, b, *, ta=False, tb=False, out_dtype=F32, addend=None, after=None, into=None, o_noff=0, out_t=False,
            o_block_perm=lambda j: j, name, tm, tn, tk=None, n_extent=None, b_koff=0, b_noff=0):
    M, K = (a.shape[1], a.shape[0]) if ta else a.shape
    N = n_extent or (b.shape[0] if tb else b.shape[1])
    tm, tn, tk = min(tm, M), min(tn, N), min(tk or K, K)
    assert M % tm == 0 and N % tn == 0 and K % tk == 0, (name, M, N, K, tm, tn, tk)
    nk = K // tk
    use_scratch = nk > 1 and out_dtype != F32
    grid = (M // tm, N // tn, nk)
    a_spec = pl.BlockSpec((tk, tm), lambda i, j, k: (k, i)) if ta else pl.BlockSpec((tm, tk), lambda i, j, k: (i, k))
    b_spec = pl.BlockSpec((tn, tk), lambda i, j, k: (j + b_noff, k + b_koff)) if tb else pl.BlockSpec((tk, tn), lambda i, j, k: (k + b_koff, j + b_noff))
    o_spec = pl.BlockSpec((tm, tn), lambda i, j, k: (i, j))
    dims = ((0 if ta else 1,), (1 if tb else 0,))
    has_add = addend is not None

    n_in = 2 + has_add + (after is not None) + (into is not None)

    def body(*refs):
        a_ref, b_ref = refs[:2]
        c_ref = refs[2] if has_add else None
        o_ref = refs[n_in]
        part = _dot(a_ref[...], b_ref[...], dims)
        if nk == 1:
            if has_add:
                part = part + c_ref[...].astype(F32)
            o_ref[...] = (part.T if out_t else part).astype(out_dtype)
        else:
            acc_ref = refs[-1] if use_scratch else o_ref
            k = pl.program_id(2)

            @pl.when(k == 0)
            def _():
                acc_ref[...] = part + c_ref[...].astype(F32) if has_add else part

            @pl.when(k > 0)
            def _():
                acc_ref[...] += part

            if use_scratch:
                @pl.when(k == nk - 1)
                def _():
                    o_ref[...] = acc_ref[...].astype(out_dtype)

    in_specs = [a_spec, b_spec] + ([o_spec] if has_add else [])
    args = (a, b) + ((addend,) if has_add else ())
    if after is not None:
        in_specs.append(pl.BlockSpec(after.shape, lambda i, j, k: (0, 0)))
        args += (after,)
    aliases = {}
    if into is not None:
        in_specs.append(ANY)
        args += (into,)
        aliases = {len(args) - 1: 0}
    if out_t:
        assert nk == 1 and not has_add
        o_spec = pl.BlockSpec((tn, tm), lambda i, j, k: (o_block_perm(j) + o_noff, i))
    elif into is not None:
        o_spec = pl.BlockSpec((tm, tn), lambda i, j, k: (i, j + o_noff))
    return pl.pallas_call(
        body,
        name=name,
        grid=grid,
        in_specs=in_specs,
        out_specs=o_spec,
        out_shape=jax.ShapeDtypeStruct(into.shape if into is not None else ((N, M) if out_t else (M, N)), out_dtype),
        input_output_aliases=aliases,
        scratch_shapes=[pltpu.VMEM((tm, tn), F32)] if use_scratch else [],
        compiler_params=_params("parallel", "parallel", "arbitrary"),
    )(*args)


def _matmul_ep(a, b, *, tb=False, b_koff=0, tm, ins, in_specs, out_shapes, out_specs, sums=(), epilogue, aliases=None, name):
    M, K = a.shape
    N = b.shape[0] if tb else b.shape[1]
    tm = min(tm, M)
    b_spec = pl.BlockSpec((N, K), lambda i: (0, b_koff)) if tb else pl.BlockSpec((K, N), lambda i: (b_koff, 0))
    dims = ((1,), (1 if tb else 0,))
    n_in = 2 + len(ins)

    def body(*refs):
        a_ref, b_ref = refs[:2]
        in_refs, out_refs = refs[2:n_in], refs[n_in:]
        outs = epilogue(_dot(a_ref[...], b_ref[...], dims), *in_refs)
        for k, (ref, val) in enumerate(zip(out_refs, outs)):
            if val is None:
                continue
            if k in sums:
                @pl.when(pl.program_id(0) == 0)
                def _():
                    ref[...] = jnp.zeros_like(ref)

                ref[...] += val
            else:
                ref[...] = val.astype(ref.dtype)

    return pl.pallas_call(
        body, name=name, grid=(M // tm,),
        in_specs=[pl.BlockSpec((tm, K), lambda i: (i, 0)), b_spec] + list(in_specs),
        out_specs=list(out_specs), out_shape=list(out_shapes),
        input_output_aliases={2 + k: v for k, v in (aliases or {}).items()},
        compiler_params=_params("arbitrary"),
    )(a, b, *ins)


def _row_spec(tm, n):
    return pl.BlockSpec((tm, n), lambda i: (i, 0))


def _full_spec(shape):
    return pl.BlockSpec(shape, lambda i: tuple(0 for _ in shape))


def _norm_cast(x, g, *, name):
    T, D = x.shape
    tm = _pick(T, (512, 256, 128))

    def body(x_ref, g_ref, u_ref):
        u_ref[...] = _rms(x_ref[...], g_ref[...]).astype(BF16)

    return pl.pallas_call(
        body, name=name, grid=(T // tm,),
        in_specs=[_row_spec(tm, D), _full_spec((1, D))],
        out_specs=_row_spec(tm, D),
        out_shape=jax.ShapeDtypeStruct((T, D), BF16),
        compiler_params=_params("parallel"),
    )(x, g)


def _norm_bwd_add(x, g, du, dres, *, with_bf16=True, name):
    T, D = x.shape
    tm = _pick(T, (512, 256, 128))

    def body(x_ref, g_ref, du_ref, dr_ref, dx_ref, *rest):
        dg_ref = rest[-1]
        _, vjp = jax.vjp(_rms, x_ref[...], g_ref[...])
        dx, dg = vjp(du_ref[...].astype(F32))
        dx = dx + dr_ref[...]
        dx_ref[...] = dx
        if with_bf16:
            rest[0][...] = dx.astype(BF16)

        @pl.when(pl.program_id(0) == 0)
        def _():
            dg_ref[...] = jnp.zeros_like(dg_ref)

        dg_ref[...] += dg

    row = _row_spec(tm, D)
    return pl.pallas_call(
        body, name=name, grid=(T // tm,),
        in_specs=[row, _full_spec((1, D)), row, row],
        out_specs=[row] + ([row] if with_bf16 else []) + [_full_spec((1, D))],
        out_shape=[jax.ShapeDtypeStruct((T, D), F32)] + ([jax.ShapeDtypeStruct((T, D), BF16)] if with_bf16 else []) + [jax.ShapeDtypeStruct((1, D), F32)],
        compiler_params=_params("arbitrary"),
    )(x, g, du, dres)


def _merge_fn(gates, a, b):
    ga = gates[:, :D_MODEL].astype(F32)
    gb = gates[:, D_MODEL:].astype(F32)
    return _sigmoid(ga) * a.astype(F32) + _sigmoid(gb) * b.astype(F32)


def _gates_spec(tm):
    return pl.BlockSpec((tm, W_GATES), lambda i: (i, O_GATES // W_GATES))


CONV_TC = 256


def _shift_down(x, n, rows):
    return jnp.where(rows >= n, pltpu.roll(x, n, 0), 0.0)


def _shift_up(x, n, rows, S):
    return jnp.where(rows < S - n, pltpu.roll(x, S - n, 0), 0.0)


def _conv_act_fwd(gu, conv_w, conv_b, *, name):
    B, S, _ = gu.shape
    tc = CONV_TC
    nc = D_FF // tc

    def body(g_ref, up_ref, w_ref, b_ref, o_ref, a_ref):
        g = g_ref[...].astype(F32)
        rows = lax.broadcasted_iota(jnp.int32, g.shape, 0)
        w = w_ref[...]
        a = w[2:3] * g + w[1:2] * _shift_down(g, 1, rows) + w[0:1] * _shift_down(g, 2, rows) + b_ref[...]
        o_ref[...] = (_silu(a) * up_ref[...].astype(F32)).astype(BF16)
        a_ref[...] = a.astype(BF16)

    col = pl.BlockSpec((None, S, tc), lambda b, j: (b, 0, j))
    return pl.pallas_call(
        body, name=name, grid=(B, nc),
        in_specs=[col,
                  pl.BlockSpec((None, S, tc), lambda b, j: (b, 0, j + nc)),
                  pl.BlockSpec((3, tc), lambda b, j: (0, j)),
                  pl.BlockSpec((1, tc), lambda b, j: (0, j))],
        out_specs=[col, col],
        out_shape=[jax.ShapeDtypeStruct((B, S, D_FF), BF16)] * 2,
        compiler_params=_params("parallel", "parallel"),
    )(gu, gu, conv_w, conv_b)


def _conv_act_bwd(gu, a_pre, conv_w, dact, *, name):
    B, S, _ = gu.shape
    tc = CONV_TC
    nc = D_FF // tc

    def body(g_ref, up_ref, a_ref, w_ref, da_ref, dg_ref, dup_ref, dw_ref, db_ref):
        g = g_ref[...].astype(F32)
        up = up_ref[...].astype(F32)
        a = a_ref[...].astype(F32)
        dact = da_ref[...].astype(F32)
        rows = lax.broadcasted_iota(jnp.int32, g.shape, 0)
        w = w_ref[...]
        sg = _sigmoid(a)
        dup_ref[...] = (dact * a * sg).astype(BF16)
        da = dact * up * sg * (1.0 + a * (1.0 - sg))
        da1 = _shift_up(da, 1, rows, S)
        da2 = _shift_up(da, 2, rows, S)
        dg_ref[...] = (w[2:3] * da + w[1:2] * da1 + w[0:1] * da2).astype(BF16)

        @pl.when(pl.program_id(1) == 0)
        def _():
            dw_ref[...] = jnp.zeros_like(dw_ref)
            db_ref[...] = jnp.zeros_like(db_ref)

        dw_ref[0:1, :] += jnp.sum(da2 * g, axis=0, keepdims=True)
        dw_ref[1:2, :] += jnp.sum(da1 * g, axis=0, keepdims=True)
        dw_ref[2:3, :] += jnp.sum(da * g, axis=0, keepdims=True)
        db_ref[...] += jnp.sum(da, axis=0, keepdims=True)

    col = pl.BlockSpec((None, S, tc), lambda j, b: (b, 0, j))
    return pl.pallas_call(
        body, name=name, grid=(nc, B),
        in_specs=[col,
                  pl.BlockSpec((None, S, tc), lambda j, b: (b, 0, j + nc)),
                  col,
                  pl.BlockSpec((3, tc), lambda j, b: (0, j)),
                  col],
        out_specs=[col, col, pl.BlockSpec((3, tc), lambda j, b: (0, j)), pl.BlockSpec((1, tc), lambda j, b: (0, j))],
        out_shape=[jax.ShapeDtypeStruct((B, S, D_FF), BF16), jax.ShapeDtypeStruct((B, S, D_FF), BF16),
                   jax.ShapeDtypeStruct((3, D_FF), F32), jax.ShapeDtypeStruct((1, D_FF), F32)],
        compiler_params=_params("parallel", "arbitrary"),
    )(gu, gu, a_pre, conv_w, dact)


HGRN_CPB = 4
HF = HGRN_HEADS * HGRN_DK


def _tri(n, upper=False):
    r = lax.broadcasted_iota(jnp.int32, (n, n), 0)
    c = lax.broadcasted_iota(jnp.int32, (n, n), 1)
    return (c >= r) if upper else (r >= c)


def _hs(h):
    return slice(h * HGRN_DK, (h + 1) * HGRN_DK)


def _cumsum_rows(tri_b, x):
    hi = x.astype(BF16)
    lo = (x - hi.astype(F32)).astype(BF16)
    return _nn(tri_b, hi) + _nn(tri_b, lo)


def _hgrn_pre(q, fz, lb, tril_b):
    qf = _silu(q)
    sg = _sigmoid(fz)
    f = lb + (1.0 - lb) * sg
    k = 1.0 - f
    b = _cumsum_rows(tril_b, jnp.log2(f))
    bref = b[CHUNK // 2:CHUNK // 2 + 1, :]
    blast = b[CHUNK - 1:CHUNK, :]
    e1 = jnp.exp2(b - bref)
    e2 = jnp.exp2(bref - b)
    e3 = e1 * jnp.exp2(bref)
    e4 = e2 * jnp.exp2(blast - bref)
    dec = jnp.exp2(blast)
    return sg, f, (e1, e2, e3, e4), qf * e1, k * e2, qf * e3, k * e4, dec


def _hgrn_fwd(zh, lb, gn, *, name):
    B, S, _ = zh.shape
    cpb = HGRN_CPB
    ts = cpb * CHUNK
    nblk = S // ts

    def body(z_ref, lb_ref, gn_ref, o_ref, st_ref, state):
        @pl.when(pl.program_id(1) == 0)
        def _():
            state[...] = jnp.zeros_like(state)

        H = HGRN_HEADS
        causal = _tri(CHUNK)
        tril_b = causal.astype(BF16)
        lb = lb_ref[...]
        for c in range(cpb):
            rows = slice(c * CHUNK, (c + 1) * CHUNK)
            q = z_ref[rows, 0:HF].astype(F32)
            fz = z_ref[rows, HF:2 * HF].astype(F32)
            v = z_ref[rows, 2 * HF:3 * HF]
            hg = z_ref[rows, 3 * HF:4 * HF].astype(F32)
            _, _, _, q_in, k_in, q_out, k_st, dec = _hgrn_pre(q, fz, lb, tril_b)
            q_in, k_in, q_out, k_st = (t.astype(BF16) for t in (q_in, k_in, q_out, k_st))
            a = [jnp.where(causal, _nt(q_in[:, _hs(h)], k_in[:, _hs(h)]), 0.0).astype(BF16) for h in range(H)]
            st = [state[h] for h in range(H)]
            for h in range(H):
                st_ref[c, h] = st[h]
            o = [_nn(a[h], v[:, _hs(h)]) + _nt(q_out[:, _hs(h)], st[h].astype(BF16)) for h in range(H)]
            for h in range(H):
                state[h] = st[h] * dec[:, _hs(h)] + _tn(v[:, _hs(h)], k_st[:, _hs(h)])
            gate = _silu(hg)
            for h in range(H):
                o_ref[rows, _hs(h)] = (_rms(o[h], gn_ref[...]) * gate[:, _hs(h)]).astype(BF16)

    return pl.pallas_call(
        body, name=name, grid=(B, nblk),
        in_specs=[pl.BlockSpec((None, ts, 4 * HF), lambda b, s: (b, s, 0)),
                  pl.BlockSpec((1, HF), lambda b, s: (0, 0)),
                  pl.BlockSpec((1, HGRN_DK), lambda b, s: (0, 0))],
        out_specs=[pl.BlockSpec((None, ts, HF), lambda b, s: (b, s, 0)),
                   pl.BlockSpec((None, cpb, HGRN_HEADS, HGRN_DK, HGRN_DK), lambda b, s: (b, s, 0, 0, 0))],
        out_shape=[jax.ShapeDtypeStruct((B, S, HF), BF16),
                   jax.ShapeDtypeStruct((B, S // CHUNK, HGRN_HEADS, HGRN_DK, HGRN_DK), F32)],
        scratch_shapes=[pltpu.VMEM((HGRN_HEADS, HGRN_DK, HGRN_DK), F32)],
        compiler_params=_params("arbitrary", "arbitrary"),
    )(zh, lb, gn)


def _hgrn_bwd(zh, lb, gn, states, doa, dz, *, name):
    B, S, _ = zh.shape
    cpb = HGRN_CPB
    ts = cpb * CHUNK
    nblk = S // ts
    rev = lambda b, s: (b, nblk - 1 - s, 0)

    def body(z_ref, lb_ref, gn_ref, st_ref, do_ref, dz_in, dz_ref, dlb_ref, dgn_ref, dstate):
        @pl.when(pl.program_id(1) == 0)
        def _():
            dstate[...] = jnp.zeros_like(dstate)

        @pl.when((pl.program_id(0) == 0) & (pl.program_id(1) == 0))
        def _():
            dlb_ref[...] = jnp.zeros_like(dlb_ref)
            dgn_ref[...] = jnp.zeros_like(dgn_ref)

        H = HGRN_HEADS
        cat = lambda xs: jnp.concatenate(xs, axis=1)
        causal = _tri(CHUNK)
        tril_b = causal.astype(BF16)
        triu_b = _tri(CHUNK, upper=True).astype(BF16)
        rowid = lax.broadcasted_iota(jnp.int32, (CHUNK, HF), 0)
        lb = lb_ref[...]
        gn = gn_ref[...]
        for c in reversed(range(cpb)):
            rows = slice(c * CHUNK, (c + 1) * CHUNK)
            q = z_ref[rows, 0:HF].astype(F32)
            fz = z_ref[rows, HF:2 * HF].astype(F32)
            v = z_ref[rows, 2 * HF:3 * HF]
            hg = z_ref[rows, 3 * HF:4 * HF].astype(F32)
            sg, f, (e1, e2, e3, e4), q_in, k_in, q_out, k_st, dec = _hgrn_pre(q, fz, lb, tril_b)
            q_in_b, k_in_b, q_out_b, k_st_b = (t.astype(BF16) for t in (q_in, k_in, q_out, k_st))
            a_b = [jnp.where(causal, _nt(q_in_b[:, _hs(h)], k_in_b[:, _hs(h)]), 0.0).astype(BF16) for h in range(H)]
            st = [st_ref[c, h] for h in range(H)]
            st_b = [t.astype(BF16) for t in st]
            o = [_nn(a_b[h], v[:, _hs(h)]) + _nt(q_out_b[:, _hs(h)], st_b[h]) for h in range(H)]
            dout = do_ref[rows, :].astype(F32)
            shg = _sigmoid(hg)
            gate = hg * shg
            do_l, dgn_acc = [], jnp.zeros_like(gn)
            for h in range(H):
                _, norm_vjp = jax.vjp(_rms, o[h], gn)
                d_o, d_gn = norm_vjp(dout[:, _hs(h)] * gate[:, _hs(h)])
                do_l.append(d_o)
                dgn_acc = dgn_acc + d_gn
            dgn_ref[...] += dgn_acc
            on = cat([_rms(o[h], gn) for h in range(H)])
            dhg = dout * on * shg * (1.0 + hg * (1.0 - shg))
            do_b = [t.astype(BF16) for t in do_l]
            dst = [dstate[h] for h in range(H)]
            dst_b = [t.astype(BF16) for t in dst]
            da_b = [jnp.where(causal, _nt(do_b[h], v[:, _hs(h)]), 0.0).astype(BF16) for h in range(H)]
            dv = cat([_tn(a_b[h], do_b[h]) + _nt(k_st_b[:, _hs(h)], dst_b[h]) for h in range(H)])
            dq_in = cat([_nn(da_b[h], k_in_b[:, _hs(h)]) for h in range(H)])
            dk_in = cat([_tn(da_b[h], q_in_b[:, _hs(h)]) for h in range(H)])
            dq_out = cat([_nn(do_b[h], st_b[h]) for h in range(H)])
            dk_st = cat([_nn(v[:, _hs(h)], dst_b[h]) for h in range(H)])
            ddec = cat([jnp.sum(st[h] * dst[h], axis=0, keepdims=True) for h in range(H)])
            for h in range(H):
                dstate[h] = dst[h] * dec[:, _hs(h)] + _tn(do_b[h], q_out_b[:, _hs(h)])
            t_qin = dq_in * q_in
            t_kin = dk_in * k_in
            t_kst = dk_st * k_st
            db = t_qin - t_kin + dq_out * q_out - t_kst
            dbref = jnp.sum(t_kin - t_qin, axis=0, keepdims=True)
            dblast = jnp.sum(t_kst, axis=0, keepdims=True) + ddec * dec
            db = db + jnp.where(rowid == CHUNK // 2, dbref, 0.0) + jnp.where(rowid == CHUNK - 1, dblast, 0.0)
            dlogf = _cumsum_rows(triu_b, db)
            dqf = dq_in * e1 + dq_out * e3
            dk = dk_in * e2 + dk_st * e4
            df = dlogf / f - dk
            dfz = df * (1.0 - lb) * sg * (1.0 - sg)
            dlb_ref[...] += jnp.sum(df * (1.0 - sg), axis=0, keepdims=True)
            sq = _sigmoid(q)
            dq = dqf * sq * (1.0 + q * (1.0 - sq))
            dz_ref[rows, 0:HF] = dq.astype(BF16)
            dz_ref[rows, HF:2 * HF] = dfz.astype(BF16)
            dz_ref[rows, 2 * HF:3 * HF] = dv.astype(BF16)
            dz_ref[rows, 3 * HF:4 * HF] = dhg.astype(BF16)

    return pl.pallas_call(
        body, name=name, grid=(B, nblk),
        in_specs=[pl.BlockSpec((None, ts, 4 * HF), rev),
                  pl.BlockSpec((1, HF), lambda b, s: (0, 0)),
                  pl.BlockSpec((1, HGRN_DK), lambda b, s: (0, 0)),
                  pl.BlockSpec((None, cpb, HGRN_HEADS, HGRN_DK, HGRN_DK), lambda b, s: (b, nblk - 1 - s, 0, 0, 0)),
                  pl.BlockSpec((None, ts, HF), rev),
                  ANY],
        out_specs=[pl.BlockSpec((None, ts, 4 * HF), rev),
                   pl.BlockSpec((1, HF), lambda b, s: (0, 0)),
                   pl.BlockSpec((1, HGRN_DK), lambda b, s: (0, 0))],
        out_shape=[jax.ShapeDtypeStruct(dz.shape, BF16),
                   jax.ShapeDtypeStruct((1, HF), F32),
                   jax.ShapeDtypeStruct((1, HGRN_DK), F32)],
        input_output_aliases={5: 0},
        scratch_shapes=[pltpu.VMEM((HGRN_HEADS, HGRN_DK, HGRN_DK), F32)],
        compiler_params=_params("arbitrary", "arbitrary"),
    )(zh, lb, gn, states, doa, dz)


KV_W = ATT_KV_HEADS * ATT_HD
ATT_SCALE = ATT_HD ** -0.5


def _rope(x, cos, sin, inverse=False):
    half = ROPE_DIM // 2
    outs = []
    for p in range(x.shape[1] // 128):
        xp = x[:, p * 128:(p + 1) * 128]
        lane = lax.broadcasted_iota(jnp.int32, xp.shape, 1) % ATT_HD
        sw = jnp.where(lane < half, pltpu.roll(xp, 128 - half, 1), pltpu.roll(xp, half, 1))
        outs.append(xp * cos - sw * sin if inverse else xp * cos + sw * sin)
    return outs[0] if len(outs) == 1 else jnp.concatenate(outs, axis=1)


PAIRS_PER_KV = ATT_GROUP // 2


def _swap_halves(x):
    return pltpu.roll(x, ATT_HD, 1)


def _kv_padded(t, low):
    sw = _swap_halves(t)
    zero = jnp.zeros_like(t)
    out = []
    for g in range(ATT_KV_HEADS):
        in_low, in_high = (t, sw) if g == 0 else (sw, t)
        out.append((jnp.where(low, in_low, zero).astype(BF16), jnp.where(low, zero, in_high).astype(BF16)))
    return out


def _swa_mask(first_block):
    qi = lax.broadcasted_iota(jnp.int32, (WINDOW, 2 * WINDOW), 0)
    mi = lax.broadcasted_iota(jnp.int32, (WINDOW, 2 * WINDOW), 1)
    band = (mi > qi) & (mi <= qi + WINDOW)
    return band & (jnp.logical_not(first_block) | (mi >= WINDOW))


def _swa_specs(nb):
    cur = lambda b, i: (b, i, 0)
    prev = lambda b, i: (b, jnp.maximum(i - 1, 0), 0)
    return cur, prev


def _swa_z_specs():
    q = pl.BlockSpec((None, WINDOW, W_AQ), lambda b, i: (b, i, O_AQ // W_AQ))
    kv_prev = pl.BlockSpec((None, WINDOW, W_AKV), lambda b, i: (b, jnp.maximum(i - 1, 0), O_AKV // W_AKV))
    kv_cur = pl.BlockSpec((None, WINDOW, W_AKV), lambda b, i: (b, i, O_AKV // W_AKV))
    return q, kv_prev, kv_cur


def _swa_fwd(z, cos, sin, sinks, *, name):
    B, S, _ = z.shape
    nb = S // WINDOW
    cur, prev = _swa_specs(nb)

    def body(q_ref, kvp_ref, kvc_ref, cp_ref, sp_ref, cc_ref, sc_ref, sink_ref, o_ref, lse_ref, qr_ref, kr_ref):
        cos_c, sin_c = cc_ref[...], sc_ref[...]
        q = (_rope(q_ref[...].astype(F32), cos_c, sin_c) * ATT_SCALE).astype(BF16)
        k = jnp.concatenate([_rope(kvp_ref[:, :KV_W].astype(F32), cp_ref[...], sp_ref[...]),
                             _rope(kvc_ref[:, :KV_W].astype(F32), cos_c, sin_c)], axis=0)
        qr_ref[...] = q
        kr_ref[...] = k[WINDOW:].astype(BF16)
        v = jnp.concatenate([kvp_ref[:, KV_W:], kvc_ref[:, KV_W:]], axis=0).astype(F32)
        low = lax.broadcasted_iota(jnp.int32, k.shape, 1) < ATT_HD
        kpad = _kv_padded(k, low)
        vpad = _kv_padded(v, low)
        mask = _swa_mask(pl.program_id(1) == 0)
        lses = []
        for g in range(ATT_KV_HEADS):
            pairs = range(g * PAIRS_PER_KV, (g + 1) * PAIRS_PER_KV)
            keys = [(p, e) for p in pairs for e in (0, 1)]
            qp = {p: q[:, p * 128:(p + 1) * 128] for p in pairs}
            s = {pe: jnp.where(mask, _nt(qp[pe[0]], kpad[g][pe[1]]), NEG_INF) for pe in keys}
            pr = {}
            for pe in keys:
                sink = sink_ref[0, 2 * pe[0] + pe[1]]
                m = jnp.maximum(jnp.max(s[pe], axis=1, keepdims=True), sink)
                ex = jnp.exp(s[pe] - m)
                den = jnp.sum(ex, axis=1, keepdims=True) + jnp.exp(sink - m)
                pr[pe] = (ex * (1.0 / den)).astype(BF16)
                lses.append(m + jnp.log(den))
            for p in pairs:
                o_ref[:, p * 128:(p + 1) * 128] = (_nn(pr[p, 0], vpad[g][0]) + _nn(pr[p, 1], vpad[g][1])).astype(BF16)
        lse_ref[...] = jnp.concatenate(lses, axis=1)

    tab = lambda im: pl.BlockSpec((None, WINDOW, 128), im)
    return pl.pallas_call(
        body, name=name, grid=(B, nb),
        in_specs=[*_swa_z_specs(),
                  tab(prev), tab(prev), tab(cur), tab(cur),
                  pl.BlockSpec(memory_space=pltpu.SMEM)],
        out_specs=[pl.BlockSpec((None, WINDOW, D_MODEL), cur), pl.BlockSpec((None, WINDOW, ATT_HEADS), cur),
                   pl.BlockSpec((None, WINDOW, D_MODEL), cur), pl.BlockSpec((None, WINDOW, KV_W), cur)],
        out_shape=[jax.ShapeDtypeStruct((B, S, D_MODEL), BF16), jax.ShapeDtypeStruct((B, S, ATT_HEADS), F32),
                   jax.ShapeDtypeStruct((B, S, D_MODEL), BF16), jax.ShapeDtypeStruct((B, S, KV_W), BF16)],
        compiler_params=_params("parallel", "parallel"),
    )(z, z, z, cos, sin, cos, sin, sinks)


def _swa_bwd(z, qr, kr, cos, sin, sinks, lse, dob, dz, *, name):
    B, S, _ = z.shape
    nb = S // WINDOW
    cur, prev = _swa_specs(nb)

    def body(q_ref, krp_ref, krc_ref, kvp_ref, kvc_ref, cp_ref, sp_ref, cc_ref, sc_ref, sink_ref, lse_ref, do_ref, dz_in,
             dq_ref, dkc_ref, dkp_ref, dsink_ref):
        @pl.when((pl.program_id(0) == 0) & (pl.program_id(1) == 0))
        def _():
            dsink_ref[...] = jnp.zeros_like(dsink_ref)

        cos_c, sin_c, cos_p, sin_p = cc_ref[...], sc_ref[...], cp_ref[...], sp_ref[...]
        q = q_ref[...]
        k = jnp.concatenate([krp_ref[...], krc_ref[...]], axis=0).astype(F32)
        v = jnp.concatenate([kvp_ref[:, KV_W:], kvc_ref[:, KV_W:]], axis=0).astype(F32)
        low = lax.broadcasted_iota(jnp.int32, k.shape, 1) < ATT_HD
        kpad = _kv_padded(k, low)
        vpad = _kv_padded(v, low)
        mask = _swa_mask(pl.program_id(1) == 0)
        lse = lse_ref[...]
        dq_parts, dk_sum, dv_sum, dsinks = [], [], [], []
        for g in range(ATT_KV_HEADS):
            pairs = range(g * PAIRS_PER_KV, (g + 1) * PAIRS_PER_KV)
            keys = [(p, e) for p in pairs for e in (0, 1)]
            qp = {p: q[:, p * 128:(p + 1) * 128] for p in pairs}
            dop = {p: do_ref[:, p * 128:(p + 1) * 128] for p in pairs}
            s = {pe: jnp.where(mask, _nt(qp[pe[0]], kpad[g][pe[1]]), NEG_INF) for pe in keys}
            dp = {pe: _nt(dop[pe[0]], vpad[g][pe[1]]) for pe in keys}
            pr, ds = {}, {}
            for pe in keys:
                h = 2 * pe[0] + pe[1]
                lse_h = lse[:, h:h + 1]
                pf = jnp.exp(s[pe] - lse_h)
                delta = jnp.sum(pf * dp[pe], axis=1, keepdims=True)
                ds[pe] = (pf * (dp[pe] - delta)).astype(BF16)
                pr[pe] = pf.astype(BF16)
                p_sink = jnp.exp(sink_ref[0, h] - lse_h)
                dsinks.append(-jnp.sum(p_sink * delta, axis=0, keepdims=True))
            for p in pairs:
                dq_parts.append((_nn(ds[p, 0], kpad[g][0]) + _nn(ds[p, 1], kpad[g][1])) * ATT_SCALE)
            x = [sum(_tn(ds[p, e], qp[p]) for p in pairs) for e in (0, 1)]
            y = [sum(_tn(pr[p, e], dop[p]) for p in pairs) for e in (0, 1)]
            zk = jnp.where(low, x[0], x[1])
            zv = jnp.where(low, y[0], y[1])
            dk_sum.append(zk + _swap_halves(zk))
            dv_sum.append(zv + _swap_halves(zv))
        dq_ref[...] = _rope(jnp.concatenate(dq_parts, axis=1), cos_c, sin_c, inverse=True).astype(BF16)
        dk = jnp.where(low, dk_sum[0], dk_sum[1])
        dv = jnp.where(low, dv_sum[0], dv_sum[1])
        dkp_ref[:, :KV_W] = _rope(dk[:WINDOW], cos_p, sin_p, inverse=True)
        dkp_ref[:, KV_W:] = dv[:WINDOW]
        dkc_ref[:, :KV_W] = _rope(dk[WINDOW:], cos_c, sin_c, inverse=True)
        dkc_ref[:, KV_W:] = dv[WINDOW:]
        dsink_ref[...] += jnp.concatenate(dsinks, axis=1)

    tab = lambda im: pl.BlockSpec((None, WINDOW, 128), im)
    return pl.pallas_call(
        body, name=name, grid=(B, nb),
        in_specs=[pl.BlockSpec((None, WINDOW, D_MODEL), cur), tab(prev), tab(cur),
                  *_swa_z_specs()[1:],
                  tab(prev), tab(prev), tab(cur), tab(cur),
                  pl.BlockSpec(memory_space=pltpu.SMEM),
                  pl.BlockSpec((None, WINDOW, ATT_HEADS), cur),
                  pl.BlockSpec((None, WINDOW, D_MODEL), cur),
                  ANY],
        out_specs=[_swa_z_specs()[0],
                   pl.BlockSpec((None, WINDOW, 2 * KV_W), cur), pl.BlockSpec((None, WINDOW, 2 * KV_W), cur),
                   pl.BlockSpec((1, ATT_HEADS), lambda b, i: (0, 0))],
        out_shape=[jax.ShapeDtypeStruct(dz.shape, BF16),
                   jax.ShapeDtypeStruct((B, S, 2 * KV_W), F32), jax.ShapeDtypeStruct((B, S, 2 * KV_W), F32),
                   jax.ShapeDtypeStruct((1, ATT_HEADS), F32)],
        input_output_aliases={12: 0},
        compiler_params=_params("arbitrary", "arbitrary"),
    )(qr, kr, kr, z, z, cos, sin, cos, sin, sinks, lse, dob, dz)


def _swa_dkv_combine(dkv_cur, dkv_prev, dz, *, name):
    B, S, W = dkv_cur.shape

    def body(c_ref, p_ref, dz_in, o_ref):
        rows = lax.broadcasted_iota(jnp.int32, (S, W), 0)
        o_ref[...] = (c_ref[...] + _shift_up(p_ref[...], WINDOW, rows, S)).astype(BF16)

    seq = pl.BlockSpec((None, S, W), lambda b: (b, 0, 0))
    return pl.pallas_call(
        body, name=name, grid=(B,),
        in_specs=[seq, seq, ANY], out_specs=pl.BlockSpec((None, S, W), lambda b: (b, 0, O_AKV // W_AKV)),
        out_shape=jax.ShapeDtypeStruct(dz.shape, BF16),
        input_output_aliases={2: 0},
        compiler_params=_params("parallel"),
    )(dkv_cur, dkv_prev, dz)


def _rope_tables(positions):
    half = ROPE_DIM // 2
    inv = ROPE_THETA ** (-2.0 * jnp.arange(half, dtype=F32) / ROPE_DIM)
    ang = positions.astype(F32)[..., None] * inv
    c, s = jnp.cos(ang), jnp.sin(ang)
    pad = jnp.zeros(ang.shape[:-1] + (ATT_HD - ROPE_DIM,), F32)
    cos = jnp.concatenate([c, c, pad + 1.0], axis=-1)
    sin = jnp.concatenate([-s, s, pad], axis=-1)
    return jnp.tile(cos, (1, 1, 2)), jnp.tile(sin, (1, 1, 2))


def _lower_bound(lb_logits, *, name):
    def body(l_ref, o_ref):
        l = l_ref[...]
        e = jnp.exp(l - jnp.max(l, axis=0, keepdims=True))
        o_ref[...] = e[0:1] / jnp.sum(e, axis=0, keepdims=True)

    return pl.pallas_call(body, name=name, out_shape=jax.ShapeDtypeStruct((1, lb_logits.shape[1]), F32))(lb_logits)


W_ZH, W_GATES, W_AQ, W_AKV = 4 * HF, 2 * D_MODEL, ATT_HEADS * ATT_HD, 2 * KV_W
O_ZH, O_GATES, O_AQ, O_AKV = 0, W_ZH, W_ZH + W_GATES, W_ZH + W_GATES + W_AQ
W_IN = W_ZH + W_GATES + W_AQ + W_AKV


W_IN_BLK = W_IN // N_DEV


def _reference_row_block(j, rows=256):
    nz, ng = W_ZH // rows, W_GATES // rows
    return jnp.where(j < nz, j, jnp.where(j < nz + ng, j + (W_AQ + W_AKV) // rows, j - ng))


def _reordered_rows(w_t, *, name):
    rows = 256

    def body(i_ref, o_ref):
        o_ref[...] = i_ref[...]

    return pl.pallas_call(
        body, name=name, grid=(W_IN // rows,),
        in_specs=[pl.BlockSpec((rows, D_MODEL), lambda j: (_reference_row_block(j, rows), 0))],
        out_specs=pl.BlockSpec((rows, D_MODEL), lambda j: (j, 0)),
        out_shape=jax.ShapeDtypeStruct(w_t.shape, w_t.dtype), compiler_params=_params("parallel"))(w_t)


def _local_step(x, positions, target, small, w_in_t, rest_weights, emit, start_token):
    B, S, D = x.shape
    T = B * S
    x2 = x.reshape(T, D)
    cos, sin = _rope_tables(positions)
    lb = _lower_bound(small["lb_logits"], name="lb_fwd")
    zero = lambda tok: tok[0:1, 0:1]

    u1 = _norm_cast(x2, small["norm1_g"] + zero(start_token), name="norm1")
    z = _matmul(u1, w_in_t, tb=True, out_dtype=BF16, name="mm_z", tm=1024, tn=W_IN // 2)
    z3 = z.reshape(B, S, W_IN)
    oa, states = _hgrn_fwd(z3, lb, small["hgrn_norm_g"], name="hgrn_fwd")
    ob, lse, qr, kr = _swa_fwd(z3, cos, sin, small["attn_sinks"], name="swa_fwd")
    oa2 = oa.reshape(T, D)
    ob2 = ob.reshape(T, D)
    W = rest_weights("mix", ob)
    pa = _matmul(oa2, W["w_a"], out_dtype=BF16, name="mm_pa", tm=1024, tn=1024)
    row = lambda tm, dtype=None: _row_spec(tm, D)
    tile = lambda dtype: jax.ShapeDtypeStruct((T, D), dtype)
    vec = _full_spec((1, D))
    vec_shape = jax.ShapeDtypeStruct((1, D), F32)

    def merge_ep(acc, g_ref, pa_ref):
        pb = acc.astype(BF16)
        return pb, _merge_fn(g_ref[...], pa_ref[...], pb)

    pb, merged = _matmul_ep(ob2, W["w_b"], tm=1024, ins=[z, pa], in_specs=[_gates_spec(1024), row(1024)],
                            out_shapes=[tile(BF16), tile(BF16)], out_specs=[row(1024), row(1024)], epilogue=merge_ep, name="mm_pb_merge")

    def resid_norm_ep(acc, x_ref, g_ref):
        hh = acc + x_ref[...]
        return hh, _rms(hh, g_ref[...])

    h, u2 = _matmul_ep(merged, W["w_out"], tm=1024, ins=[x2, small["norm2_g"]], in_specs=[row(1024), vec],
                       out_shapes=[tile(F32), tile(BF16)], out_specs=[row(1024), row(1024)], epilogue=resid_norm_ep, name="mm_h_norm2")
    W.update(rest_weights("ffn", u2))
    gu = _matmul(u2, W["w_ffn_t"], tb=True, out_dtype=BF16, name="mm_gu", tm=2048, tn=512)
    gu3 = gu.reshape(B, S, 2 * D_FF)
    act, a_pre = _conv_act_fwd(gu3, W["conv_w"], small["conv_b"], name="conv_act_fwd")
    act2 = act.reshape(T, D_FF)
    g = {}

    def loss_ep(acc, h_ref, g_ref, t_ref):
        y, vjp = jax.vjp(_rms, acc + h_ref[...], g_ref[...])
        err = y - t_ref[...]
        dx, dg = vjp(err * (1.0 / D))
        return dx, dx, dg, (0.5 / D) * jnp.sum(jnp.sum(err * err, axis=1, keepdims=True), axis=0, keepdims=True)

    dh2, dh2b, g["final_g"], loss = _matmul_ep(
        act2, W["w_down"], tm=512, ins=[h, small["final_g"].reshape(1, D), target.reshape(T, D)], in_specs=[row(512), vec, row(512)],
        out_shapes=[tile(F32), tile(BF16), vec_shape, jax.ShapeDtypeStruct((1, 1), F32)],
        out_specs=[row(512), row(512), vec, _full_spec((1, 1))], sums=(2, 3), epilogue=loss_ep, name="mm_h2_loss")
    dact = _matmul(dh2b, W["w_down"], tb=True, out_dtype=BF16, name="mm_dact", tm=1024, tn=D_FF)
    dw_down_t = _matmul(dh2b, act2, ta=True, out_dtype=BF16, name="mm_dw_down", tm=1024, tn=256, tk=8192)
    dg_, dup, g["conv_w"], g["conv_b"] = _conv_act_bwd(gu3, a_pre, W["conv_w"], dact.reshape(B, S, D_FF), name="conv_act_bwd")
    dg2 = dg_.reshape(T, D_FF)
    dup2 = dup.reshape(T, D_FF)
    du2_g = _matmul(dg2, W["w_ffn_t"], name="mm_du2_g", tm=1024, tn=1024, b_koff=0)
    dw_ffn_t = _matmul(u2, dg2, ta=True, out_t=True, out_dtype=BF16, into=lax.empty((2 * D_FF, D), BF16), o_noff=0, name="mm_dw_ffn_g", tm=1024, tn=256, tk=8192)
    dw_ffn_t = _matmul(u2, dup2, ta=True, out_t=True, out_dtype=BF16, into=dw_ffn_t, o_noff=D_FF // 256, name="mm_dw_ffn_u", tm=1024, tn=256, tk=8192)
    tok = emit("ffn", dict(w_ffn_t=dw_ffn_t, w_down=dw_down_t.T))
    def norm2_bwd_ep(acc, dug_ref, h_ref, g_ref, dh2_ref):
        _, vjp = jax.vjp(_rms, h_ref[...], g_ref[...])
        dx, dg = vjp(acc + dug_ref[...])
        dx = dx + dh2_ref[...]
        return dx, dx, dg

    dh, dhb, g["norm2_g"] = _matmul_ep(
        dup2, W["w_ffn_t"], b_koff=1, tm=512, ins=[du2_g, h, small["norm2_g"] + zero(tok), dh2], in_specs=[row(512), row(512), vec, row(512)],
        out_shapes=[tile(F32), tile(BF16), vec_shape], out_specs=[row(512), row(512), vec], sums=(2,), epilogue=norm2_bwd_ep, name="mm_du2_norm2_bwd")
    dw_out = _matmul(merged, dhb, ta=True, out_dtype=BF16, name="mm_dw_out", tm=1024, tn=1024, tk=2048)

    def merge_bwd_ep(acc, g_ref, pa_ref, pb_ref, dz_in):
        gt = g_ref[...].astype(F32)
        sa = _sigmoid(gt[:, :D_MODEL])
        sb = _sigmoid(gt[:, D_MODEL:])
        dgates = jnp.concatenate([acc * pa_ref[...].astype(F32) * sa * (1.0 - sa), acc * pb_ref[...].astype(F32) * sb * (1.0 - sb)], axis=1)
        return dgates, acc * sa, acc * sb

    dz, dpa, dpb = _matmul_ep(
        dhb, W["w_out"], tb=True, tm=512, ins=[z, pa, pb, lax.empty((T, W_IN), BF16)], in_specs=[_gates_spec(512), row(512), row(512), ANY],
        out_shapes=[jax.ShapeDtypeStruct((T, W_IN), BF16), tile(BF16), tile(BF16)], out_specs=[_gates_spec(512), row(512), row(512)],
        aliases={3: 0}, epilogue=merge_bwd_ep, name="mm_dmerged_merge_bwd")
    doa =_matmul(dpa, W["w_a"], tb=True, out_dtype=BF16, name="mm_doa", tm=1024, tn=1024)
    dw_a = _matmul(oa2, dpa, ta=True, out_dtype=BF16, name="mm_dw_a", tm=1024, tn=1024, tk=2048)
    dob = _matmul(dpb, W["w_b"], tb=True, out_dtype=BF16, name="mm_dob", tm=1024, tn=1024)
    dw_b = _matmul(ob2, dpb, ta=True, out_dtype=BF16, name="mm_dw_b", tm=1024, tn=1024, tk=2048)
    tok = emit("mix", dict(w_out=dw_out, w_a=dw_a, w_b=dw_b))
    dz3, dkv_cur, dkv_prev, dsinks = _swa_bwd(z3, qr, kr, cos, sin, small["attn_sinks"] + zero(tok), lse, dob.reshape(B, S, D),
                                              dz.reshape(B, S, W_IN), name="swa_bwd")
    dz3 = _swa_dkv_combine(dkv_cur, dkv_prev, dz3, name="swa_dkv")
    g["attn_sinks"] = dsinks
    dz3, g["lb"], g["hgrn_norm_g"] = _hgrn_bwd(z3, lb, small["hgrn_norm_g"], states, doa.reshape(B, S, D), dz3, name="hgrn_bwd")
    dz = dz3.reshape(T, W_IN)
    dw_in_t = _matmul(u1, dz, ta=True, out_t=True, o_block_perm=_reference_row_block, out_dtype=BF16, name="mm_dw_in", tm=1024, tn=256, tk=8192)
    tok = emit("in", dict(w_in_t=dw_in_t))
    du1 = _matmul(dz, w_in_t, after=tok, out_dtype=BF16, name="mm_du1", tm=1024, tn=512)
    dx, g["norm1_g"] = _norm_bwd_add(x2, small["norm1_g"], du1, dh, with_bf16=False, name="norm1_bwd")
    g["lb_logits"] = _lb_bwd(g.pop("lb"), lb, name="lb_bwd")
    return loss, dx.reshape(B, S, D), g


def _my_place():
    return lax.axis_index("x"), lax.axis_index("y"), lax.axis_index("c")


def _gather_blocks(x_ref, out_ref, send_sems, recv_sems, local_sem):
    x, y, c = _my_place()
    me, sibling = (x, y, c), (x, y, 1 - c)
    chips = [(1 - x, y), (x, 1 - y), (1 - x, 1 - y)]

    def slot(px, py, pc):
        return out_ref.at[4 * px + 2 * py + pc]

    def copy(k, block, to, src=None):
        return pltpu.make_async_remote_copy(
            src_ref=slot(*block) if src is None else src, dst_ref=slot(*block),
            send_sem=send_sems.at[k], recv_sem=recv_sems.at[k], device_id=to, device_id_type=MESH)

    mine = pltpu.make_async_copy(x_ref, slot(*me), local_sem)
    mine.start()
    first = [copy(0, me, sibling, src=x_ref)]
    first += [copy(1 + j, me, (*chip, c), src=x_ref) for j, chip in enumerate(chips)]
    for cp in first:
        cp.start()
    passed = [copy(4 + j, (*chip, c), sibling) for j, chip in enumerate(chips)]
    for j, chip in enumerate(chips):
        copy(1 + j, (*chip, c), me).wait_recv()
        passed[j].start()
    copy(0, sibling, me).wait_recv()
    for j, chip in enumerate(chips):
        copy(4 + j, (*chip, 1 - c), me).wait_recv()
    for cp in first + passed:
        cp.wait_send()
    mine.wait()


GATHER_SEMS = [pltpu.SemaphoreType.DMA((7,)), pltpu.SemaphoreType.DMA((7,)), pltpu.SemaphoreType.DMA]


def _all_gather(blk, *, name):
    return pl.pallas_call(
        _gather_body_fn(), name=name,
        out_shape=jax.ShapeDtypeStruct((N_DEV,) + blk.shape, blk.dtype),
        in_specs=[ANY], out_specs=ANY,
        scratch_shapes=GATHER_SEMS,
    )(blk)


def _gather_body_fn():
    def body(x_ref, out_ref, send_sems, recv_sems, local_sem):
        _gather_blocks(x_ref, out_ref, send_sems, recv_sems, local_sem)
    return body


SLAB_W = 1152
SMALL_SHAPES = dict(norm1_g=(1, D_MODEL), lb_logits=(2, HGRN_HEADS * HGRN_DK), hgrn_norm_g=(1, HGRN_DK), attn_sinks=(1, ATT_HEADS),
                    norm2_g=(1, D_MODEL), conv_b=(1, D_FF), final_g=(1, D_MODEL))
CONVW_BLK = D_FF // N_DEV
CONVW_STRIDE = SLAB_W // 3


def _slab_layout():
    layout, r = {}, 0
    for nm, (nr, w) in SMALL_SHAPES.items():
        layout[nm] = []
        for i in range(nr):
            for c0 in range(0, w, SLAB_W):
                layout[nm].append((r, i, c0, min(SLAB_W, w - c0)))
                r += 1
    return layout, r


SMALL_ROWS, _N_SMALL_ROWS = _slab_layout()
CONV_ROW0 = -(-_N_SMALL_ROWS // 8) * 8
LOSS_ROW = CONV_ROW0 + N_DEV
SLAB_ROWS = LOSS_ROW + 8


def _small_step(grads, g_conv_w, loss, params, moments, variances, dev, *, name):
    names = list(SMALL_ROWS)
    n = len(names)

    def body(dev_ref, *refs):
        g_refs = dict(zip(names, refs[:n]))
        gc_ref, loss_ref = refs[n], refs[n + 1]
        base = n + 2
        w_refs, m_refs, v_refs = (dict(zip(names + ["conv_w"], refs[base + i * (n + 1):base + (i + 1) * (n + 1)])) for i in range(3))
        o = base + 3 * (n + 1)
        gath_ref, loss_out = refs[o], refs[o + 1]
        outs = {nm: refs[o + 2 + 4 * i:o + 6 + 4 * i] for i, nm in enumerate(names + ["conv_w"])}
        slab, total, send_sems, recv_sems, local_sem = refs[-5:]

        slab[...] = jnp.zeros_like(slab)
        for nm, pieces in SMALL_ROWS.items():
            for r, i, c0, w in pieces:
                slab[r:r + 1, 0:w] = g_refs[nm][i:i + 1, c0:c0 + w]
        for p in range(N_DEV):
            for j in range(3):
                slab[CONV_ROW0 + p:CONV_ROW0 + p + 1, j * CONVW_STRIDE:j * CONVW_STRIDE + CONVW_BLK] = gc_ref[j:j + 1, p * CONVW_BLK:(p + 1) * CONVW_BLK]
        slab[LOSS_ROW:LOSS_ROW + 1, 0:1] = loss_ref[...]
        _gather_blocks(slab, gath_ref, send_sems, recv_sems, local_sem)
        acc = gath_ref[0]
        for p in range(1, N_DEV):
            acc = acc + gath_ref[p]
        total[...] = acc
        loss_out[...] = total[LOSS_ROW:LOSS_ROW + 1, 0:1]

        def update(nm, g, i, c0, w):
            at = (slice(i, i + 1), slice(c0, c0 + w))
            d, mn, vn = _adamw_math(w_refs[nm][at], g, m_refs[nm][at], v_refs[nm][at])
            for ref, val in zip(outs[nm], (g, d, mn, vn)):
                ref[at] = val

        for nm, pieces in SMALL_ROWS.items():
            for r, i, c0, w in pieces:
                update(nm, total[r:r + 1, 0:w], i, c0, w)
        conv_rows = total[CONV_ROW0:CONV_ROW0 + N_DEV, :]
        rowid = lax.broadcasted_iota(jnp.int32, conv_rows.shape, 0)
        mine = jnp.sum(jnp.where(rowid == dev_ref[0], conv_rows, 0.0), axis=0, keepdims=True)
        for j in range(3):
            update("conv_w", mine[:, j * CONVW_STRIDE:j * CONVW_STRIDE + CONVW_BLK], j, 0, CONVW_BLK)

    order = names + ["conv_w"]
    ins = [grads[nm] for nm in names] + [g_conv_w, loss]
    for d in (params, moments, variances):
        ins += [d[nm] for nm in order]
    vmem = pl.BlockSpec(memory_space=pltpu.VMEM)
    out_shape = [jax.ShapeDtypeStruct((N_DEV, SLAB_ROWS, SLAB_W), F32), jax.ShapeDtypeStruct((1, 1), F32)]
    for nm in order:
        out_shape += [jax.ShapeDtypeStruct(params[nm].shape, F32)] * 4
    res = pl.pallas_call(
        body, name=name,
        grid_spec=pltpu.PrefetchScalarGridSpec(
            num_scalar_prefetch=1, grid=(1,),
            in_specs=[vmem] * len(ins), out_specs=[vmem] * len(out_shape),
            scratch_shapes=[pltpu.VMEM((SLAB_ROWS, SLAB_W), F32), pltpu.VMEM((SLAB_ROWS, SLAB_W), F32)] + GATHER_SEMS),
        out_shape=out_shape,
    )(dev, *ins)
    return res[1], {nm: tuple(res[2 + 4 * i:6 + 4 * i]) for i, nm in enumerate(order)}


HBM_SPEC = pl.BlockSpec(memory_space=pltpu.HBM)
SEM_SPEC = pl.BlockSpec(memory_space=pltpu.SEMAPHORE)
DATAFLOW_EFFECT = pltpu.SideEffectType.DATAFLOW_SIDE_EFFECTING
N_PEERS = N_DEV - 1


def _peers(x, y, c):
    return [(1 - x if r & 4 else x, 1 - y if r & 2 else y, 1 - c if r & 1 else c) for r in range(1, N_DEV)]


def _exchange_start(srcs, scatter, *, after=None, name):
    n = len(srcs)
    lands = [lax.empty(a.shape if scatter else (N_DEV,) + a.shape, a.dtype) for a in srcs]
    extra = [] if after is None else [after]

    def body(*refs):
        src_refs, land_refs = refs[:n], refs[n:2 * n]
        send_sems, recv_sems, token = refs[2 * n + len(extra)], refs[2 * n + len(extra) + 1], refs[-1]
        x, y, c = _my_place()
        me = 4 * x + 2 * y + c
        for i in range(n):
            for r, (tx, ty, tc) in enumerate(_peers(x, y, c)):
                src = src_refs[i].at[4 * tx + 2 * ty + tc] if scatter else src_refs[i]
                pltpu.make_async_remote_copy(
                    src_ref=src, dst_ref=land_refs[i].at[me], send_sem=send_sems.at[N_PEERS * i + r],
                    recv_sem=recv_sems.at[N_PEERS * i + r], device_id=(tx, ty, tc), device_id_type=MESH).start()
        token[...] = jnp.zeros_like(token)

    thru = [pltpu.HBM(a.shape, a.dtype) for a in list(srcs) + lands]
    res = pl.pallas_call(
        body, name=name,
        out_shape=(pltpu.SemaphoreType.DMA((N_PEERS * n,)), pltpu.SemaphoreType.DMA((N_PEERS * n,)), *thru,
                   jax.ShapeDtypeStruct((8, 128), F32)),
        in_specs=[HBM_SPEC] * (2 * n) + [ANY] * len(extra),
        out_specs=(SEM_SPEC, SEM_SPEC, *([HBM_SPEC] * (2 * n)), pl.BlockSpec(memory_space=pltpu.VMEM)),
        input_output_aliases={i: 2 + i for i in range(2 * n)},
        compiler_params=pltpu.CompilerParams(has_side_effects=DATAFLOW_EFFECT),
    )(*[pltpu.with_memory_space_constraint(a, pltpu.HBM) for a in list(srcs) + lands], *extra)
    return (res[0], res[1], list(res[2:2 + n]), list(res[2 + n:2 + 2 * n]), scatter), res[-1]


def _exchange_wait(handle, after, *, name):
    send_sems, recv_sems, srcs, lands, scatter = handle
    n = len(srcs)

    def body(*refs):
        src_refs, land_refs = refs[:n], refs[n:2 * n]
        send_sems, recv_sems = refs[2 * n], refs[2 * n + 1]
        x, y, c = _my_place()
        for i in range(n):
            for r in range(N_PEERS):
                src = src_refs[i].at[0] if scatter else src_refs[i]
                cp = pltpu.make_async_remote_copy(
                    src_ref=src, dst_ref=land_refs[i].at[0], send_sem=send_sems.at[N_PEERS * i + r],
                    recv_sem=recv_sems.at[N_PEERS * i + r], device_id=(x, y, c), device_id_type=MESH)
                cp.wait_send()
                cp.wait_recv()

    thru = [pltpu.HBM(a.shape, a.dtype) for a in srcs + lands]
    res = pl.pallas_call(
        body, name=name, out_shape=tuple(thru),
        in_specs=[HBM_SPEC] * (2 * n) + [SEM_SPEC, SEM_SPEC, ANY], out_specs=tuple([HBM_SPEC] * (2 * n)),
        input_output_aliases={i: i for i in range(2 * n)},
        compiler_params=pltpu.CompilerParams(has_side_effects=DATAFLOW_EFFECT),
    )(*srcs, *lands, send_sems, recv_sems, after)
    return list(res[:n]), list(res[n:])


def _with_own(land, own, me):
    return lax.dynamic_update_index_in_dim(land, own, me, 0)


def _adamw_math(w, g, m, v):
    m = ADAM_B1 * m + (1.0 - ADAM_B1) * g
    v = ADAM_B2 * v + (1.0 - ADAM_B2) * (g * g)
    m_hat = m / (1.0 - ADAM_B1 ** ADAM_STEP)
    v_hat = v / (1.0 - ADAM_B2 ** ADAM_STEP)
    delta = -ADAM_LR * (m_hat / (jnp.sqrt(v_hat) + ADAM_EPS) + ADAM_WD * w)
    return delta, m, v


def _adamw_sum(parts, w, m, v, *, name):
    shape = w.shape
    R, n = shape[-2], shape[-1]
    w, m, v = (t.reshape(R, n) for t in (w, m, v))
    tr = _pick(R, (256, 464, 352, 128))

    def body(p_ref, w_ref, m_ref, v_ref, g_ref, d_ref, mo_ref, vo_ref):
        g = p_ref[0].astype(F32)
        for p in range(1, N_DEV):
            g = g + p_ref[p].astype(F32)
        d, mn, vn = _adamw_math(w_ref[...], g, m_ref[...], v_ref[...])
        g_ref[...] = g
        d_ref[...] = d
        mo_ref[...] = mn
        vo_ref[...] = vn

    row = pl.BlockSpec((tr, n), lambda i: (i, 0))
    outs = pl.pallas_call(
        body, name=name, grid=(R // tr,),
        in_specs=[pl.BlockSpec((N_DEV, tr, n), lambda i: (0, i, 0)), row, row, row],
        out_specs=[row, row, row, row],
        out_shape=[jax.ShapeDtypeStruct((R, n), F32)] * 4,
        compiler_params=_params("parallel"),
    )(parts, w, m, v)
    return [t.reshape(shape) for t in outs]


def _lb_bwd(dlb, lb, *, name):
    def body(d_ref, lb_ref, o_ref):
        t = d_ref[...] * lb_ref[...] * (1.0 - lb_ref[...])
        o_ref[0:1, :] = t
        o_ref[1:2, :] = -t

    return pl.pallas_call(body, name=name, out_shape=jax.ShapeDtypeStruct((2, lb.shape[1]), F32))(dlb, lb)


DOWN_BLK, ROW_BLK = D_FF // N_DEV, D_MODEL // N_DEV
W_FFN_BLK = 2 * D_FF // N_DEV
CONV_BITS_SHAPE = (16, 256)


def kernel(x, positions, norm1_g, w_in, lb_logits, hgrn_norm_g, w_a, attn_sinks, w_b, w_out, norm2_g, w_ffn_in, conv_w, conv_b, w_down, final_g, loss_target, m_norm1_g, m_w_in, m_lb_logits, m_hgrn_norm_g, m_w_a, m_attn_sinks, m_w_b, m_w_out, m_norm2_g, m_w_ffn_in, m_conv_w, m_conv_b, m_w_down, m_final_g, v_norm1_g, v_w_in, v_lb_logits, v_hgrn_norm_g, v_w_a, v_attn_sinks, v_w_b, v_w_out, v_norm2_g, v_w_ffn_in, v_conv_w, v_conv_b, v_w_down, v_final_g):
    xi, yi, ci = _my_place()
    dev = 4 * xi + 2 * yi + ci

    tr = lambda t: jnp.transpose(t[0])
    untr = lambda t: jnp.transpose(t)[None]
    w_in_blocks = _all_gather(tr(w_in).astype(BF16), name="ag_w_in")
    conv_bits = lax.bitcast_convert_type(conv_w, BF16).reshape(-1)
    conv_bits = jnp.pad(conv_bits, (0, CONV_BITS_SHAPE[0] * CONV_BITS_SHAPE[1] - conv_bits.shape[0])).reshape(CONV_BITS_SHAPE)
    w_in_full_t = _reordered_rows(w_in_blocks.reshape(W_IN, D_MODEL), name="w_in_rows")
    gather_handles = {}
    gather_handles["mix"], tok_mix = _exchange_start([w_a[0].astype(BF16), w_b[0].astype(BF16), w_out[0].astype(BF16)], False,
                                                     after=w_in_full_t, name="ag_mix_start")
    gather_handles["ffn"], tok_ffn = _exchange_start([tr(w_ffn_in).astype(BF16), w_down[0].astype(BF16), conv_bits], False,
                                                     after=tok_mix, name="ag_ffn_start")
    start_token = tok_mix + tok_ffn

    def rest_weights(group, after):
        own, lands = _exchange_wait(gather_handles[group], after, name="ag_" + group + "_wait")
        full = [_with_own(l, o, dev) for l, o in zip(lands, own)]
        if group == "mix":
            return dict(zip(("w_a", "w_b", "w_out"), [t.reshape(D_MODEL, D_MODEL) for t in full]))
        bits = full[2].reshape(N_DEV, -1)[:, :3 * CONVW_BLK * 2].reshape(N_DEV, 3, CONVW_BLK, 2)
        return dict(w_ffn_t=full[0].reshape(2 * D_FF, D_MODEL), w_down=full[1].reshape(D_FF, D_MODEL),
                    conv_w=lax.bitcast_convert_type(bits, F32).transpose(1, 0, 2).reshape(3, D_FF))

    handles = {}

    def emit(group, gr):
        if group == "ffn":
            srcs = [gr["w_ffn_t"].reshape(N_DEV, W_FFN_BLK, D_MODEL), gr["w_down"].reshape(N_DEV, DOWN_BLK, D_MODEL)]
        elif group == "mix":
            srcs = [gr[n].reshape(N_DEV, ROW_BLK, D_MODEL) for n in ("w_out", "w_a", "w_b")]
        else:
            srcs = [gr["w_in_t"].reshape(N_DEV, W_IN_BLK, D_MODEL)]
        handles[group], token = _exchange_start(srcs, True, name="rs_" + group + "_start")
        return token

    small = dict(norm1_g=norm1_g, lb_logits=lb_logits, hgrn_norm_g=hgrn_norm_g, attn_sinks=attn_sinks, norm2_g=norm2_g,
                 conv_b=conv_b, final_g=final_g)
    loss, grad_x, g = _local_step(x, positions, loss_target, small, w_in_full_t, rest_weights, emit, start_token)

    def parts_of(group, after):
        srcs, lands = _exchange_wait(handles[group], after, name="rs_" + group + "_wait")
        return [_with_own(l, lax.dynamic_index_in_dim(s, dev, 0, keepdims=False), dev) for s, l in zip(srcs, lands)]

    p_ffn, p_down = parts_of("ffn", grad_x)
    p_out, p_a, p_b = parts_of("mix", grad_x)
    (p_in,) = parts_of("in", grad_x)
    big = dict(
        w_in=[untr(t) for t in _adamw_sum(p_in, tr(w_in), tr(m_w_in), tr(v_w_in), name="adamw_w_in")],
        w_a=_adamw_sum(p_a, w_a, m_w_a, v_w_a, name="adamw_w_a"),
        w_b=_adamw_sum(p_b, w_b, m_w_b, v_w_b, name="adamw_w_b"),
        w_out=_adamw_sum(p_out, w_out, m_w_out, v_w_out, name="adamw_w_out"),
        w_ffn_in=[untr(t) for t in _adamw_sum(p_ffn, tr(w_ffn_in), tr(m_w_ffn_in), tr(v_w_ffn_in), name="adamw_w_ffn_in")],
        w_down=_adamw_sum(p_down, w_down, m_w_down, v_w_down, name="adamw_w_down"),
    )

    row = lambda t: t.reshape(1, -1) if t.ndim == 1 else t
    shard = lambda t: t.reshape(3, CONVW_BLK)
    sm_g = {nm: g[nm] for nm in SMALL_ROWS}
    sm_w = dict(norm1_g=norm1_g, lb_logits=lb_logits, hgrn_norm_g=hgrn_norm_g, attn_sinks=attn_sinks, norm2_g=norm2_g,
                conv_b=conv_b, final_g=row(final_g), conv_w=shard(conv_w))
    sm_m = dict(norm1_g=m_norm1_g, lb_logits=m_lb_logits, hgrn_norm_g=m_hgrn_norm_g, attn_sinks=m_attn_sinks, norm2_g=m_norm2_g,
                conv_b=m_conv_b, final_g=row(m_final_g), conv_w=shard(m_conv_w))
    sm_v = dict(norm1_g=v_norm1_g, lb_logits=v_lb_logits, hgrn_norm_g=v_hgrn_norm_g, attn_sinks=v_attn_sinks, norm2_g=v_norm2_g,
                conv_b=v_conv_b, final_g=row(v_final_g), conv_w=shard(v_conv_w))
    loss_total, sm_out = _small_step(sm_g, g["conv_w"], loss, sm_w, sm_m, sm_v, dev.astype(jnp.int32).reshape(1), name="small_step")
    shapes = dict(final_g=final_g.shape, conv_w=conv_w.shape)

    names = ("norm1_g", "w_in", "lb_logits", "hgrn_norm_g", "w_a", "attn_sinks", "w_b", "w_out", "norm2_g", "w_ffn_in", "conv_w", "conv_b", "w_down", "final_g")
    outs = [loss_total.reshape(()), grad_x]
    for kind in range(4):
        outs += [big[n][kind] if n in big else sm_out[n][kind].reshape(shapes.get(n, sm_out[n][kind].shape)) for n in names]
    return tuple(outs)
```

```python
import functools

import jax
import jax.numpy as jnp
from jax import lax
from jax.experimental import pallas as pl
from jax.experimental.pallas import tpu as pltpu

F32 = jnp.float32
BF16 = jnp.bfloat16

D_MODEL = 1024
HGRN_HEADS = 8
HGRN_DK = 128
CHUNK = 64
ATT_HEADS = 16
ATT_KV_HEADS = 2
ATT_HD = 64
ATT_GROUP = ATT_HEADS // ATT_KV_HEADS
WINDOW = 128
ROPE_DIM = ATT_HD // 4
ROPE_THETA = 500000.0
D_FF = 2816
EPS = 1e-6
NEG_INF = -1e30
N_DEV = 8

ADAM_LR = 0.001
ADAM_B1 = 0.9
ADAM_B2 = 0.999
ADAM_EPS = 1e-08
ADAM_WD = 0.01
ADAM_STEP = 10

MESH = pl.DeviceIdType.MESH
ANY = pl.BlockSpec(memory_space=pl.ANY)


def _pick(n, cands):
    for c in cands:
        if n % c == 0:
            return c
    return n


def _sigmoid(x):
    return 0.5 * jnp.tanh(0.5 * x) + 0.5


def _silu(x):
    hx = 0.5 * x
    return hx * jnp.tanh(hx) + hx


def _rms(x, g):
    return x * lax.rsqrt(jnp.mean(x * x, axis=-1, keepdims=True) + EPS) * g


def _dot(a, b, dims):
    return lax.dot_general(a, b, (dims, ((), ())), preferred_element_type=F32)


def _nn(a, b):
    return _dot(a, b, ((1,), (0,)))


def _nt(a, b):
    return _dot(a, b, ((1,), (1,)))


def _tn(a, b):
    return _dot(a, b, ((0,), (0,)))


def _params(*sem):
    return pltpu.CompilerParams(dimension_semantics=sem, vmem_limit_bytes=56 * 1024 * 1024)


def _matmul(a, b, *, ta=False, tb=False, out_dtype=F32, addend=None, after=None, into=None, o_noff=0, out_t=False,
            o_block_perm=lambda j: j, name, tm, tn, tk=None, n_extent=None, b_koff=0, b_noff=0):
    M, K = (a.shape[1], a.shape[0]) if ta else a.shape
    N = n_extent or (b.shape[0] if tb else b.shape[1])
    tm, tn, tk = min(tm, M), min(tn, N), min(tk or K, K)
    assert M % tm == 0 and N % tn == 0 and K % tk == 0, (name, M, N, K, tm, tn, tk)
    nk = K // tk
    use_scratch = nk > 1 and out_dtype != F32
    grid = (M // tm, N // tn, nk)
    a_spec = pl.BlockSpec((tk, tm), lambda i, j, k: (k, i)) if ta else pl.BlockSpec((tm, tk), lambda i, j, k: (i, k))
    b_spec = pl.BlockSpec((tn, tk), lambda i, j, k: (j + b_noff, k + b_koff)) if tb else pl.BlockSpec((tk, tn), lambda i, j, k: (k + b_koff, j + b_noff))
    o_spec = pl.BlockSpec((tm, tn), lambda i, j, k: (i, j))
    dims = ((0 if ta else 1,), (1 if tb else 0,))
    has_add = addend is not None

    n_in = 2 + has_add + (after is not None) + (into is not None)

    def body(*refs):
        a_ref, b_ref = refs[:2]
        c_ref = refs[2] if has_add else None
        o_ref = refs[n_in]
        part = _dot(a_ref[...], b_ref[...], dims)
        if nk == 1:
            if has_add:
                part = part + c_ref[...].astype(F32)
            o_ref[...] = (part.T if out_t else part).astype(out_dtype)
        else:
            acc_ref = refs[-1] if use_scratch else o_ref
            k = pl.program_id(2)

            @pl.when(k == 0)
            def _():
                acc_ref[...] = part + c_ref[...].astype(F32) if has_add else part

            @pl.when(k > 0)
            def _():
                acc_ref[...] += part

            if use_scratch:
                @pl.when(k == nk - 1)
                def _():
                    o_ref[...] = acc_ref[...].astype(out_dtype)

    in_specs = [a_spec, b_spec] + ([o_spec] if has_add else [])
    args = (a, b) + ((addend,) if has_add else ())
    if after is not None:
        in_specs.append(pl.BlockSpec(after.shape, lambda i, j, k: (0, 0)))
        args += (after,)
    aliases = {}
    if into is not None:
        in_specs.append(ANY)
        args += (into,)
        aliases = {len(args) - 1: 0}
    if out_t:
        assert nk == 1 and not has_add
        o_spec = pl.BlockSpec((tn, tm), lambda i, j, k: (o_block_perm(j) + o_noff, i))
    elif into is not None:
        o_spec = pl.BlockSpec((tm, tn), lambda i, j, k: (i, j + o_noff))
    return pl.pallas_call(
        body,
        name=name,
        grid=grid,
        in_specs=in_specs,
        out_specs=o_spec,
        out_shape=jax.ShapeDtypeStruct(into.shape if into is not None else ((N, M) if out_t else (M, N)), out_dtype),
        input_output_aliases=aliases,
        scratch_shapes=[pltpu.VMEM((tm, tn), F32)] if use_scratch else [],
        compiler_params=_params("parallel", "parallel", "arbitrary"),
    )(*args)


def _matmul_ep(pairs, *, tm, ins, in_specs, out_shapes, out_specs, sums=(), epilogue, aliases=None, name):
    M = pairs[0][0].shape[0]
    tm = min(tm, M)
    mm_specs, mm_args, dims = [], [], []
    for a, b, tb, koff in pairs:
        K = a.shape[1]
        N = b.shape[0] if tb else b.shape[1]
        mm_specs += [pl.BlockSpec((tm, K), lambda i: (i, 0)),
                     pl.BlockSpec((N, K), lambda i, koff=koff: (0, koff)) if tb else pl.BlockSpec((K, N), lambda i, koff=koff: (koff, 0))]
        mm_args += [a, b]
        dims.append(((1,), (1 if tb else 0,)))
    n_mm = len(mm_args)
    n_in = n_mm + len(ins)

    def body(*refs):
        in_refs, out_refs = refs[n_mm:n_in], refs[n_in:]
        accs = [_dot(refs[2 * p][...], refs[2 * p + 1][...], dims[p]) for p in range(len(pairs))]
        outs = epilogue(*accs, *in_refs)
        for k, (ref, val) in enumerate(zip(out_refs, outs)):
            if val is None:
                continue
            if k in sums:
                @pl.when(pl.program_id(0) == 0)
                def _():
                    ref[...] = jnp.zeros_like(ref)

                ref[...] += val
            else:
                ref[...] = val.astype(ref.dtype)

    return pl.pallas_call(
        body, name=name, grid=(M // tm,),
        in_specs=mm_specs + list(in_specs),
        out_specs=list(out_specs), out_shape=list(out_shapes),
        input_output_aliases={n_mm + k: v for k, v in (aliases or {}).items()},
        compiler_params=_params("arbitrary"),
    )(*mm_args, *ins)


def _row_spec(tm, n):
    return pl.BlockSpec((tm, n), lambda i: (i, 0))


def _full_spec(shape):
    return pl.BlockSpec(shape, lambda i: tuple(0 for _ in shape))


def _norm_cast(x, g, *, name):
    T, D = x.shape
    tm = _pick(T, (512, 256, 128))

    def body(x_ref, g_ref, u_ref):
        u_ref[...] = _rms(x_ref[...], g_ref[...]).astype(BF16)

    return pl.pallas_call(
        body, name=name, grid=(T // tm,),
        in_specs=[_row_spec(tm, D), _full_spec((1, D))],
        out_specs=_row_spec(tm, D),
        out_shape=jax.ShapeDtypeStruct((T, D), BF16),
        compiler_params=_params("parallel"),
    )(x, g)


def _norm_bwd_add(x, g, du, dres, *, with_bf16=True, name):
    T, D = x.shape
    tm = _pick(T, (512, 256, 128))

    def body(x_ref, g_ref, du_ref, dr_ref, dx_ref, *rest):
        dg_ref = rest[-1]
        _, vjp = jax.vjp(_rms, x_ref[...], g_ref[...])
        dx, dg = vjp(du_ref[...].astype(F32))
        dx = dx + dr_ref[...]
        dx_ref[...] = dx
        if with_bf16:
            rest[0][...] = dx.astype(BF16)

        @pl.when(pl.program_id(0) == 0)
        def _():
            dg_ref[...] = jnp.zeros_like(dg_ref)

        dg_ref[...] += dg

    row = _row_spec(tm, D)
    return pl.pallas_call(
        body, name=name, grid=(T // tm,),
        in_specs=[row, _full_spec((1, D)), row, row],
        out_specs=[row] + ([row] if with_bf16 else []) + [_full_spec((1, D))],
        out_shape=[jax.ShapeDtypeStruct((T, D), F32)] + ([jax.ShapeDtypeStruct((T, D), BF16)] if with_bf16 else []) + [jax.ShapeDtypeStruct((1, D), F32)],
        compiler_params=_params("arbitrary"),
    )(x, g, du, dres)


def _merge_fn(gates, a, b):
    ga = gates[:, :D_MODEL].astype(F32)
    gb = gates[:, D_MODEL:].astype(F32)
    return _sigmoid(ga) * a.astype(F32) + _sigmoid(gb) * b.astype(F32)


def _gates_spec(tm):
    return pl.BlockSpec((tm, W_GATES), lambda i: (i, O_GATES // W_GATES))


CONV_TC = 256


def _shift_down(x, n, rows):
    return jnp.where(rows >= n, pltpu.roll(x, n, 0), 0.0)


def _shift_up(x, n, rows, S):
    return jnp.where(rows < S - n, pltpu.roll(x, S - n, 0), 0.0)


def _conv_act_fwd(gu, conv_w, conv_b, *, name):
    B, S, _ = gu.shape
    tc = CONV_TC
    nc = D_FF // tc

    def body(g_ref, up_ref, w_ref, b_ref, o_ref, a_ref):
        g = g_ref[...].astype(F32)
        rows = lax.broadcasted_iota(jnp.int32, g.shape, 0)
        w = w_ref[...]
        a = w[2:3] * g + w[1:2] * _shift_down(g, 1, rows) + w[0:1] * _shift_down(g, 2, rows) + b_ref[...]
        o_ref[...] = (_silu(a) * up_ref[...].astype(F32)).astype(BF16)
        a_ref[...] = a.astype(BF16)

    col = pl.BlockSpec((None, S, tc), lambda b, j: (b, 0, j))
    return pl.pallas_call(
        body, name=name, grid=(B, nc),
        in_specs=[col,
                  pl.BlockSpec((None, S, tc), lambda b, j: (b, 0, j + nc)),
                  pl.BlockSpec((3, tc), lambda b, j: (0, j)),
                  pl.BlockSpec((1, tc), lambda b, j: (0, j))],
        out_specs=[col, col],
        out_shape=[jax.ShapeDtypeStruct((B, S, D_FF), BF16)] * 2,
        compiler_params=_params("parallel", "parallel"),
    )(gu, gu, conv_w, conv_b)


def _conv_act_bwd(gu, a_pre, conv_w, dact, *, name):
    B, S, _ = gu.shape
    tc = CONV_TC
    nc = D_FF // tc

    def body(g_ref, up_ref, a_ref, w_ref, da_ref, dg_ref, dup_ref, dw_ref, db_ref):
        g = g_ref[...].astype(F32)
        up = up_ref[...].astype(F32)
        a = a_ref[...].astype(F32)
        dact = da_ref[...].astype(F32)
        rows = lax.broadcasted_iota(jnp.int32, g.shape, 0)
        w = w_ref[...]
        sg = _sigmoid(a)
        dup_ref[...] = (dact * a * sg).astype(BF16)
        da = dact * up * sg * (1.0 + a * (1.0 - sg))
        da1 = _shift_up(da, 1, rows, S)
        da2 = _shift_up(da, 2, rows, S)
        dg_ref[...] = (w[2:3] * da + w[1:2] * da1 + w[0:1] * da2).astype(BF16)

        @pl.when(pl.program_id(1) == 0)
        def _():
            dw_ref[...] = jnp.zeros_like(dw_ref)
            db_ref[...] = jnp.zeros_like(db_ref)

        dw_ref[0:1, :] += jnp.sum(da2 * g, axis=0, keepdims=True)
        dw_ref[1:2, :] += jnp.sum(da1 * g, axis=0, keepdims=True)
        dw_ref[2:3, :] += jnp.sum(da * g, axis=0, keepdims=True)
        db_ref[...] += jnp.sum(da, axis=0, keepdims=True)

    col = pl.BlockSpec((None, S, tc), lambda j, b: (b, 0, j))
    return pl.pallas_call(
        body, name=name, grid=(nc, B),
        in_specs=[col,
                  pl.BlockSpec((None, S, tc), lambda j, b: (b, 0, j + nc)),
                  col,
                  pl.BlockSpec((3, tc), lambda j, b: (0, j)),
                  col],
        out_specs=[col, col, pl.BlockSpec((3, tc), lambda j, b: (0, j)), pl.BlockSpec((1, tc), lambda j, b: (0, j))],
        out_shape=[jax.ShapeDtypeStruct((B, S, D_FF), BF16), jax.ShapeDtypeStruct((B, S, D_FF), BF16),
                   jax.ShapeDtypeStruct((3, D_FF), F32), jax.ShapeDtypeStruct((1, D_FF), F32)],
        compiler_params=_params("parallel", "arbitrary"),
    )(gu, gu, a_pre, conv_w, dact)


HGRN_CPB = 8
HF = HGRN_HEADS * HGRN_DK


def _tri(n, upper=False):
    r = lax.broadcasted_iota(jnp.int32, (n, n), 0)
    c = lax.broadcasted_iota(jnp.int32, (n, n), 1)
    return (c >= r) if upper else (r >= c)


def _hs(h):
    return slice(h * HGRN_DK, (h + 1) * HGRN_DK)


def _cumsum_rows(tri_b, x):
    hi = x.astype(BF16)
    lo = (x - hi.astype(F32)).astype(BF16)
    return _nn(tri_b, hi) + _nn(tri_b, lo)


def _hgrn_pre(q, fz, lb, tril_b):
    qf = _silu(q)
    sg = _sigmoid(fz)
    f = lb + (1.0 - lb) * sg
    k = 1.0 - f
    b = _cumsum_rows(tril_b, jnp.log2(f))
    bref = b[CHUNK // 2:CHUNK // 2 + 1, :]
    blast = b[CHUNK - 1:CHUNK, :]
    e1 = jnp.exp2(b - bref)
    e2 = jnp.exp2(bref - b)
    e3 = e1 * jnp.exp2(bref)
    e4 = e2 * jnp.exp2(blast - bref)
    dec = jnp.exp2(blast)
    return sg, f, (e1, e2, e3, e4), qf * e1, k * e2, qf * e3, k * e4, dec


def _hgrn_fwd(zh, lb, gn, *, name):
    B, S, _ = zh.shape
    cpb = HGRN_CPB
    ts = cpb * CHUNK
    nblk = S // ts

    def body(z_ref, lb_ref, gn_ref, o_ref, st_ref, state):
        @pl.when(pl.program_id(1) == 0)
        def _():
            state[...] = jnp.zeros_like(state)

        H = HGRN_HEADS
        causal = _tri(CHUNK)
        tril_b = causal.astype(BF16)
        lb = lb_ref[...]
        for c in range(cpb):
            rows = slice(c * CHUNK, (c + 1) * CHUNK)
            q = z_ref[rows, 0:HF].astype(F32)
            fz = z_ref[rows, HF:2 * HF].astype(F32)
            v = z_ref[rows, 2 * HF:3 * HF]
            hg = z_ref[rows, 3 * HF:4 * HF].astype(F32)
            _, _, _, q_in, k_in, q_out, k_st, dec = _hgrn_pre(q, fz, lb, tril_b)
            q_in, k_in, q_out, k_st = (t.astype(BF16) for t in (q_in, k_in, q_out, k_st))
            a = [jnp.where(causal, _nt(q_in[:, _hs(h)], k_in[:, _hs(h)]), 0.0).astype(BF16) for h in range(H)]
            st = [state[h] for h in range(H)]
            for h in range(H):
                st_ref[c, h] = st[h]
            o = [_nn(a[h], v[:, _hs(h)]) + _nt(q_out[:, _hs(h)], st[h].astype(BF16)) for h in range(H)]
            for h in range(H):
                state[h] = st[h] * dec[:, _hs(h)] + _tn(v[:, _hs(h)], k_st[:, _hs(h)])
            gate = _silu(hg)
            for h in range(H):
                o_ref[rows, _hs(h)] = (_rms(o[h], gn_ref[...]) * gate[:, _hs(h)]).astype(BF16)

    return pl.pallas_call(
        body, name=name, grid=(B, nblk),
        in_specs=[pl.BlockSpec((None, ts, 4 * HF), lambda b, s: (b, s, 0)),
                  pl.BlockSpec((1, HF), lambda b, s: (0, 0)),
                  pl.BlockSpec((1, HGRN_DK), lambda b, s: (0, 0))],
        out_specs=[pl.BlockSpec((None, ts, HF), lambda b, s: (b, s, 0)),
                   pl.BlockSpec((None, cpb, HGRN_HEADS, HGRN_DK, HGRN_DK), lambda b, s: (b, s, 0, 0, 0))],
        out_shape=[jax.ShapeDtypeStruct((B, S, HF), BF16),
                   jax.ShapeDtypeStruct((B, S // CHUNK, HGRN_HEADS, HGRN_DK, HGRN_DK), F32)],
        scratch_shapes=[pltpu.VMEM((HGRN_HEADS, HGRN_DK, HGRN_DK), F32)],
        compiler_params=_params("arbitrary", "arbitrary"),
    )(zh, lb, gn)


def _hgrn_bwd(zh, lb, gn, states, doa, dz, *, name):
    B, S, _ = zh.shape
    cpb = HGRN_CPB
    ts = cpb * CHUNK
    nblk = S // ts
    rev = lambda b, s: (b, nblk - 1 - s, 0)

    def body(z_ref, lb_ref, gn_ref, st_ref, do_ref, dz_in, dz_ref, dlb_ref, dgn_ref, dstate):
        @pl.when(pl.program_id(1) == 0)
        def _():
            dstate[...] = jnp.zeros_like(dstate)

        @pl.when((pl.program_id(0) == 0) & (pl.program_id(1) == 0))
        def _():
            dlb_ref[...] = jnp.zeros_like(dlb_ref)
            dgn_ref[...] = jnp.zeros_like(dgn_ref)

        H = HGRN_HEADS
        cat = lambda xs: jnp.concatenate(xs, axis=1)
        causal = _tri(CHUNK)
        tril_b = causal.astype(BF16)
        triu_b = _tri(CHUNK, upper=True).astype(BF16)
        rowid = lax.broadcasted_iota(jnp.int32, (CHUNK, HF), 0)
        lb = lb_ref[...]
        gn = gn_ref[...]
        for c in reversed(range(cpb)):
            rows = slice(c * CHUNK, (c + 1) * CHUNK)
            q = z_ref[rows, 0:HF].astype(F32)
            fz = z_ref[rows, HF:2 * HF].astype(F32)
            v = z_ref[rows, 2 * HF:3 * HF]
            hg = z_ref[rows, 3 * HF:4 * HF].astype(F32)
            sg, f, (e1, e2, e3, e4), q_in, k_in, q_out, k_st, dec = _hgrn_pre(q, fz, lb, tril_b)
            q_in_b, k_in_b, q_out_b, k_st_b = (t.astype(BF16) for t in (q_in, k_in, q_out, k_st))
            a_b = [jnp.where(causal, _nt(q_in_b[:, _hs(h)], k_in_b[:, _hs(h)]), 0.0).astype(BF16) for h in range(H)]
            st = [st_ref[c, h] for h in range(H)]
            st_b = [t.astype(BF16) for t in st]
            o = [_nn(a_b[h], v[:, _hs(h)]) + _nt(q_out_b[:, _hs(h)], st_b[h]) for h in range(H)]
            dout = do_ref[rows, :].astype(F32)
            shg = _sigmoid(hg)
            gate = hg * shg
            do_l, dgn_acc = [], jnp.zeros_like(gn)
            for h in range(H):
                _, norm_vjp = jax.vjp(_rms, o[h], gn)
                d_o, d_gn = norm_vjp(dout[:, _hs(h)] * gate[:, _hs(h)])
                do_l.append(d_o)
                dgn_acc = dgn_acc + d_gn
            dgn_ref[...] += dgn_acc
            on = cat([_rms(o[h], gn) for h in range(H)])
            dhg = dout * on * shg * (1.0 + hg * (1.0 - shg))
            do_b = [t.astype(BF16) for t in do_l]
            dst = [dstate[h] for h in range(H)]
            dst_b = [t.astype(BF16) for t in dst]
            da_b = [jnp.where(causal, _nt(do_b[h], v[:, _hs(h)]), 0.0).astype(BF16) for h in range(H)]
            dv = cat([_tn(a_b[h], do_b[h]) + _nt(k_st_b[:, _hs(h)], dst_b[h]) for h in range(H)])
            dq_in = cat([_nn(da_b[h], k_in_b[:, _hs(h)]) for h in range(H)])
            dk_in = cat([_tn(da_b[h], q_in_b[:, _hs(h)]) for h in range(H)])
            dq_out = cat([_nn(do_b[h], st_b[h]) for h in range(H)])
            dk_st = cat([_nn(v[:, _hs(h)], dst_b[h]) for h in range(H)])
            ddec = cat([jnp.sum(st[h] * dst[h], axis=0, keepdims=True) for h in range(H)])
            for h in range(H):
                dstate[h] = dst[h] * dec[:, _hs(h)] + _tn(do_b[h], q_out_b[:, _hs(h)])
            t_qin = dq_in * q_in
            t_kin = dk_in * k_in
            t_kst = dk_st * k_st
            db = t_qin - t_kin + dq_out * q_out - t_kst
            dbref = jnp.sum(t_kin - t_qin, axis=0, keepdims=True)
            dblast = jnp.sum(t_kst, axis=0, keepdims=True) + ddec * dec
            db = db + jnp.where(rowid == CHUNK // 2, dbref, 0.0) + jnp.where(rowid == CHUNK - 1, dblast, 0.0)
            dlogf = _cumsum_rows(triu_b, db)
            dqf = dq_in * e1 + dq_out * e3
            dk = dk_in * e2 + dk_st * e4
            df = dlogf / f - dk
            dfz = df * (1.0 - lb) * sg * (1.0 - sg)
            dlb_ref[...] += jnp.sum(df * (1.0 - sg), axis=0, keepdims=True)
            sq = _sigmoid(q)
            dq = dqf * sq * (1.0 + q * (1.0 - sq))
            dz_ref[rows, 0:HF] = dq.astype(BF16)
            dz_ref[rows, HF:2 * HF] = dfz.astype(BF16)
            dz_ref[rows, 2 * HF:3 * HF] = dv.astype(BF16)
            dz_ref[rows, 3 * HF:4 * HF] = dhg.astype(BF16)

    return pl.pallas_call(
        body, name=name, grid=(B, nblk),
        in_specs=[pl.BlockSpec((None, ts, 4 * HF), rev),
                  pl.BlockSpec((1, HF), lambda b, s: (0, 0)),
                  pl.BlockSpec((1, HGRN_DK), lambda b, s: (0, 0)),
                  pl.BlockSpec((None, cpb, HGRN_HEADS, HGRN_DK, HGRN_DK), lambda b, s: (b, nblk - 1 - s, 0, 0, 0)),
                  pl.BlockSpec((None, ts, HF), rev),
                  ANY],
        out_specs=[pl.BlockSpec((None, ts, 4 * HF), rev),
                   pl.BlockSpec((1, HF), lambda b, s: (0, 0)),
                   pl.BlockSpec((1, HGRN_DK), lambda b, s: (0, 0))],
        out_shape=[jax.ShapeDtypeStruct(dz.shape, BF16),
                   jax.ShapeDtypeStruct((1, HF), F32),
                   jax.ShapeDtypeStruct((1, HGRN_DK), F32)],
        input_output_aliases={5: 0},
        scratch_shapes=[pltpu.VMEM((HGRN_HEADS, HGRN_DK, HGRN_DK), F32)],
        compiler_params=_params("arbitrary", "arbitrary"),
    )(zh, lb, gn, states, doa, dz)


KV_W = ATT_KV_HEADS * ATT_HD
ATT_SCALE = ATT_HD ** -0.5


def _rope(x, cos, sin, inverse=False):
    half = ROPE_DIM // 2
    outs = []
    for p in range(x.shape[1] // 128):
        xp = x[:, p * 128:(p + 1) * 128]
        lane = lax.broadcasted_iota(jnp.int32, xp.shape, 1) % ATT_HD
        sw = jnp.where(lane < half, pltpu.roll(xp, 128 - half, 1), pltpu.roll(xp, half, 1))
        outs.append(xp * cos - sw * sin if inverse else xp * cos + sw * sin)
    return outs[0] if len(outs) == 1 else jnp.concatenate(outs, axis=1)


PAIRS_PER_KV = ATT_GROUP // 2


def _swap_halves(x):
    return pltpu.roll(x, ATT_HD, 1)


def _kv_padded(t, low):
    sw = _swap_halves(t)
    zero = jnp.zeros_like(t)
    out = []
    for g in range(ATT_KV_HEADS):
        in_low, in_high = (t, sw) if g == 0 else (sw, t)
        out.append((jnp.where(low, in_low, zero).astype(BF16), jnp.where(low, zero, in_high).astype(BF16)))
    return out


def _swa_mask(first_block):
    qi = lax.broadcasted_iota(jnp.int32, (WINDOW, 2 * WINDOW), 0)
    mi = lax.broadcasted_iota(jnp.int32, (WINDOW, 2 * WINDOW), 1)
    band = (mi > qi) & (mi <= qi + WINDOW)
    return band & (jnp.logical_not(first_block) | (mi >= WINDOW))


def _swa_specs(nb):
    cur = lambda b, i: (b, i, 0)
    prev = lambda b, i: (b, jnp.maximum(i - 1, 0), 0)
    return cur, prev


def _swa_z_specs():
    q = pl.BlockSpec((None, WINDOW, W_AQ), lambda b, i: (b, i, O_AQ // W_AQ))
    kv_prev = pl.BlockSpec((None, WINDOW, W_AKV), lambda b, i: (b, jnp.maximum(i - 1, 0), O_AKV // W_AKV))
    kv_cur = pl.BlockSpec((None, WINDOW, W_AKV), lambda b, i: (b, i, O_AKV // W_AKV))
    return q, kv_prev, kv_cur


def _swa_fwd(z, cos, sin, sinks, *, name):
    B, S, _ = z.shape
    nb = S // WINDOW
    cur, prev = _swa_specs(nb)

    def body(q_ref, kvp_ref, kvc_ref, cp_ref, sp_ref, cc_ref, sc_ref, sink_ref, o_ref, lse_ref, qr_ref, kr_ref):
        cos_c, sin_c = cc_ref[...], sc_ref[...]
        q = (_rope(q_ref[...].astype(F32), cos_c, sin_c) * ATT_SCALE).astype(BF16)
        k = jnp.concatenate([_rope(kvp_ref[:, :KV_W].astype(F32), cp_ref[...], sp_ref[...]),
                             _rope(kvc_ref[:, :KV_W].astype(F32), cos_c, sin_c)], axis=0)
        qr_ref[...] = q
        kr_ref[...] = k[WINDOW:].astype(BF16)
        v = jnp.concatenate([kvp_ref[:, KV_W:], kvc_ref[:, KV_W:]], axis=0).astype(F32)
        low = lax.broadcasted_iota(jnp.int32, k.shape, 1) < ATT_HD
        kpad = _kv_padded(k, low)
        vpad = _kv_padded(v, low)
        mask = _swa_mask(pl.program_id(1) == 0)
        lses = []
        for g in range(ATT_KV_HEADS):
            pairs = range(g * PAIRS_PER_KV, (g + 1) * PAIRS_PER_KV)
            keys = [(p, e) for p in pairs for e in (0, 1)]
            qp = {p: q[:, p * 128:(p + 1) * 128] for p in pairs}
            s = {pe: jnp.where(mask, _nt(qp[pe[0]], kpad[g][pe[1]]), NEG_INF) for pe in keys}
            pr = {}
            for pe in keys:
                sink = sink_ref[0, 2 * pe[0] + pe[1]]
                m = jnp.maximum(jnp.max(s[pe], axis=1, keepdims=True), sink)
                ex = jnp.exp(s[pe] - m)
                den = jnp.sum(ex, axis=1, keepdims=True) + jnp.exp(sink - m)
                pr[pe] = (ex * (1.0 / den)).astype(BF16)
                lses.append(m + jnp.log(den))
            for p in pairs:
                o_ref[:, p * 128:(p + 1) * 128] = (_nn(pr[p, 0], vpad[g][0]) + _nn(pr[p, 1], vpad[g][1])).astype(BF16)
        lse_ref[...] = jnp.concatenate(lses, axis=1)

    tab = lambda im: pl.BlockSpec((None, WINDOW, 128), im)
    return pl.pallas_call(
        body, name=name, grid=(B, nb),
        in_specs=[*_swa_z_specs(),
                  tab(prev), tab(prev), tab(cur), tab(cur),
                  pl.BlockSpec(memory_space=pltpu.SMEM)],
        out_specs=[pl.BlockSpec((None, WINDOW, D_MODEL), cur), pl.BlockSpec((None, WINDOW, ATT_HEADS), cur),
                   pl.BlockSpec((None, WINDOW, D_MODEL), cur), pl.BlockSpec((None, WINDOW, KV_W), cur)],
        out_shape=[jax.ShapeDtypeStruct((B, S, D_MODEL), BF16), jax.ShapeDtypeStruct((B, S, ATT_HEADS), F32),
                   jax.ShapeDtypeStruct((B, S, D_MODEL), BF16), jax.ShapeDtypeStruct((B, S, KV_W), BF16)],
        compiler_params=_params("parallel", "parallel"),
    )(z, z, z, cos, sin, cos, sin, sinks)


def _swa_bwd(z, qr, kr, cos, sin, sinks, lse, dob, dz, *, name):
    B, S, _ = z.shape
    nb = S // WINDOW
    cur, prev = _swa_specs(nb)

    def body(q_ref, krp_ref, krc_ref, kvp_ref, kvc_ref, cp_ref, sp_ref, cc_ref, sc_ref, sink_ref, lse_ref, do_ref, dz_in,
             dq_ref, dkc_ref, dkp_ref, dsink_ref):
        @pl.when((pl.program_id(0) == 0) & (pl.program_id(1) == 0))
        def _():
            dsink_ref[...] = jnp.zeros_like(dsink_ref)

        cos_c, sin_c, cos_p, sin_p = cc_ref[...], sc_ref[...], cp_ref[...], sp_ref[...]
        q = q_ref[...]
        k = jnp.concatenate([krp_ref[...], krc_ref[...]], axis=0).astype(F32)
        v = jnp.concatenate([kvp_ref[:, KV_W:], kvc_ref[:, KV_W:]], axis=0).astype(F32)
        low = lax.broadcasted_iota(jnp.int32, k.shape, 1) < ATT_HD
        kpad = _kv_padded(k, low)
        vpad = _kv_padded(v, low)
        mask = _swa_mask(pl.program_id(1) == 0)
        lse = lse_ref[...]
        dq_parts, dk_sum, dv_sum, dsinks = [], [], [], []
        for g in range(ATT_KV_HEADS):
            pairs = range(g * PAIRS_PER_KV, (g + 1) * PAIRS_PER_KV)
            keys = [(p, e) for p in pairs for e in (0, 1)]
            qp = {p: q[:, p * 128:(p + 1) * 128] for p in pairs}
            dop = {p: do_ref[:, p * 128:(p + 1) * 128] for p in pairs}
            s = {pe: jnp.where(mask, _nt(qp[pe[0]], kpad[g][pe[1]]), NEG_INF) for pe in keys}
            dp = {pe: _nt(dop[pe[0]], vpad[g][pe[1]]) for pe in keys}
            pr, ds = {}, {}
            for pe in keys:
                h = 2 * pe[0] + pe[1]
                lse_h = lse[:, h:h + 1]
                pf = jnp.exp(s[pe] - lse_h)
                delta = jnp.sum(pf * dp[pe], axis=1, keepdims=True)
                ds[pe] = (pf * (dp[pe] - delta)).astype(BF16)
                pr[pe] = pf.astype(BF16)
                p_sink = jnp.exp(sink_ref[0, h] - lse_h)
                dsinks.append(-jnp.sum(p_sink * delta, axis=0, keepdims=True))
            for p in pairs:
                dq_parts.append((_nn(ds[p, 0], kpad[g][0]) + _nn(ds[p, 1], kpad[g][1])) * ATT_SCALE)
            x = [sum(_tn(ds[p, e], qp[p]) for p in pairs) for e in (0, 1)]
            y = [sum(_tn(pr[p, e], dop[p]) for p in pairs) for e in (0, 1)]
            zk = jnp.where(low, x[0], x[1])
            zv = jnp.where(low, y[0], y[1])
            dk_sum.append(zk + _swap_halves(zk))
            dv_sum.append(zv + _swap_halves(zv))
        dq_ref[...] = _rope(jnp.concatenate(dq_parts, axis=1), cos_c, sin_c, inverse=True).astype(BF16)
        dk = jnp.where(low, dk_sum[0], dk_sum[1])
        dv = jnp.where(low, dv_sum[0], dv_sum[1])
        dkp_ref[:, :KV_W] = _rope(dk[:WINDOW], cos_p, sin_p, inverse=True)
        dkp_ref[:, KV_W:] = dv[:WINDOW]
        dkc_ref[:, :KV_W] = _rope(dk[WINDOW:], cos_c, sin_c, inverse=True)
        dkc_ref[:, KV_W:] = dv[WINDOW:]
        dsink_ref[...] += jnp.concatenate(dsinks, axis=1)

    tab = lambda im: pl.BlockSpec((None, WINDOW, 128), im)
    return pl.pallas_call(
        body, name=name, grid=(B, nb),
        in_specs=[pl.BlockSpec((None, WINDOW, D_MODEL), cur), tab(prev), tab(cur),
                  *_swa_z_specs()[1:],
                  tab(prev), tab(prev), tab(cur), tab(cur),
                  pl.BlockSpec(memory_space=pltpu.SMEM),
                  pl.BlockSpec((None, WINDOW, ATT_HEADS), cur),
                  pl.BlockSpec((None, WINDOW, D_MODEL), cur),
                  ANY],
        out_specs=[_swa_z_specs()[0],
                   pl.BlockSpec((None, WINDOW, 2 * KV_W), cur), pl.BlockSpec((None, WINDOW, 2 * KV_W), cur),
                   pl.BlockSpec((1, ATT_HEADS), lambda b, i: (0, 0))],
        out_shape=[jax.ShapeDtypeStruct(dz.shape, BF16),
                   jax.ShapeDtypeStruct((B, S, 2 * KV_W), F32), jax.ShapeDtypeStruct((B, S, 2 * KV_W), F32),
                   jax.ShapeDtypeStruct((1, ATT_HEADS), F32)],
        input_output_aliases={12: 0},
        compiler_params=_params("arbitrary", "arbitrary"),
    )(qr, kr, kr, z, z, cos, sin, cos, sin, sinks, lse, dob, dz)


def _swa_dkv_combine(dkv_cur, dkv_prev, dz, *, name):
    B, S, W = dkv_cur.shape

    def body(c_ref, p_ref, dz_in, o_ref):
        rows = lax.broadcasted_iota(jnp.int32, (S, W), 0)
        o_ref[...] = (c_ref[...] + _shift_up(p_ref[...], WINDOW, rows, S)).astype(BF16)

    seq = pl.BlockSpec((None, S, W), lambda b: (b, 0, 0))
    return pl.pallas_call(
        body, name=name, grid=(B,),
        in_specs=[seq, seq, ANY], out_specs=pl.BlockSpec((None, S, W), lambda b: (b, 0, O_AKV // W_AKV)),
        out_shape=jax.ShapeDtypeStruct(dz.shape, BF16),
        input_output_aliases={2: 0},
        compiler_params=_params("parallel"),
    )(dkv_cur, dkv_prev, dz)


def _rope_tables(positions):
    half = ROPE_DIM // 2
    inv = ROPE_THETA ** (-2.0 * jnp.arange(half, dtype=F32) / ROPE_DIM)
    ang = positions.astype(F32)[..., None] * inv
    c, s = jnp.cos(ang), jnp.sin(ang)
    pad = jnp.zeros(ang.shape[:-1] + (ATT_HD - ROPE_DIM,), F32)
    cos = jnp.concatenate([c, c, pad + 1.0], axis=-1)
    sin = jnp.concatenate([-s, s, pad], axis=-1)
    return jnp.tile(cos, (1, 1, 2)), jnp.tile(sin, (1, 1, 2))


def _lower_bound(lb_logits, *, name):
    def body(l_ref, o_ref):
        l = l_ref[...]
        e = jnp.exp(l - jnp.max(l, axis=0, keepdims=True))
        o_ref[...] = e[0:1] / jnp.sum(e, axis=0, keepdims=True)

    return pl.pallas_call(body, name=name, out_shape=jax.ShapeDtypeStruct((1, lb_logits.shape[1]), F32))(lb_logits)


W_ZH, W_GATES, W_AQ, W_AKV = 4 * HF, 2 * D_MODEL, ATT_HEADS * ATT_HD, 2 * KV_W
O_ZH, O_GATES, O_AQ, O_AKV = 0, W_ZH, W_ZH + W_GATES, W_ZH + W_GATES + W_AQ
W_IN = W_ZH + W_GATES + W_AQ + W_AKV


W_IN_BLK = W_IN // N_DEV


def _reference_row_block(j, rows=256):
    nz, ng = W_ZH // rows, W_GATES // rows
    return jnp.where(j < nz, j, jnp.where(j < nz + ng, j + (W_AQ + W_AKV) // rows, j - ng))


def _reordered_rows(w_t, *, name):
    rows = 256

    def body(i_ref, o_ref):
        o_ref[...] = i_ref[...]

    return pl.pallas_call(
        body, name=name, grid=(W_IN // rows,),
        in_specs=[pl.BlockSpec((rows, D_MODEL), lambda j: (_reference_row_block(j, rows), 0))],
        out_specs=pl.BlockSpec((rows, D_MODEL), lambda j: (j, 0)),
        out_shape=jax.ShapeDtypeStruct(w_t.shape, w_t.dtype), compiler_params=_params("parallel"))(w_t)


def _local_step(x, positions, target, small, w_in_t, rest_weights, emit, start_token):
    B, S, D = x.shape
    T = B * S
    x2 = x.reshape(T, D)
    cos, sin = _rope_tables(positions)
    lb = _lower_bound(small["lb_logits"], name="lb_fwd")
    zero = lambda tok: tok[0:1, 0:1]

    u1 = _norm_cast(x2, small["norm1_g"] + zero(start_token), name="norm1")
    z = _matmul(u1, w_in_t, tb=True, out_dtype=BF16, name="mm_z", tm=1024, tn=W_IN // 2)
    z3 = z.reshape(B, S, W_IN)
    oa, states = _hgrn_fwd(z3, lb, small["hgrn_norm_g"], name="hgrn_fwd")
    ob, lse, qr, kr = _swa_fwd(z3, cos, sin, small["attn_sinks"], name="swa_fwd")
    oa2 = oa.reshape(T, D)
    ob2 = ob.reshape(T, D)
    W = rest_weights("mix", ob)
    row = lambda tm, dtype=None: _row_spec(tm, D)
    tile = lambda dtype: jax.ShapeDtypeStruct((T, D), dtype)
    vec = _full_spec((1, D))
    vec_shape = jax.ShapeDtypeStruct((1, D), F32)

    def merge_ep(acc_a, acc_b, g_ref):
        pa, pb = acc_a.astype(BF16), acc_b.astype(BF16)
        return pa, pb, _merge_fn(g_ref[...], pa, pb)

    pa, pb, merged = _matmul_ep([(oa2, W["w_a"], False, 0), (ob2, W["w_b"], False, 0)], tm=1024, ins=[z], in_specs=[_gates_spec(1024)],
                                out_shapes=[tile(BF16)] * 3, out_specs=[row(1024)] * 3, epilogue=merge_ep, name="mm_pa_pb_merge")

    def resid_norm_ep(acc, x_ref, g_ref):
        hh = acc + x_ref[...]
        return hh, _rms(hh, g_ref[...])

    h, u2 = _matmul_ep([(merged, W["w_out"], False, 0)], tm=1024, ins=[x2, small["norm2_g"]], in_specs=[row(1024), vec],
                       out_shapes=[tile(F32), tile(BF16)], out_specs=[row(1024), row(1024)], epilogue=resid_norm_ep, name="mm_h_norm2")
    W.update(rest_weights("ffn", u2))
    gu = _matmul(u2, W["w_ffn_t"], tb=True, out_dtype=BF16, name="mm_gu", tm=1024, tn=D_FF)
    gu3 = gu.reshape(B, S, 2 * D_FF)
    act, a_pre = _conv_act_fwd(gu3, W["conv_w"], small["conv_b"], name="conv_act_fwd")
    act2 = act.reshape(T, D_FF)
    g = {}

    def loss_ep(acc, h_ref, g_ref, t_ref):
        y, vjp = jax.vjp(_rms, acc + h_ref[...], g_ref[...])
        err = y - t_ref[...]
        dx, dg = vjp(err * (1.0 / D))
        return dx, dx, dg, (0.5 / D) * jnp.sum(jnp.sum(err * err, axis=1, keepdims=True), axis=0, keepdims=True)

    dh2, dh2b, g["final_g"], loss = _matmul_ep(
        [(act2, W["w_down"], False, 0)], tm=512, ins=[h, small["final_g"].reshape(1, D), target.reshape(T, D)], in_specs=[row(512), vec, row(512)],
        out_shapes=[tile(F32), tile(BF16), vec_shape, jax.ShapeDtypeStruct((1, 1), F32)],
        out_specs=[row(512), row(512), vec, _full_spec((1, 1))], sums=(2, 3), epilogue=loss_ep, name="mm_h2_loss")
    dact = _matmul(dh2b, W["w_down"], tb=True, out_dtype=BF16, name="mm_dact", tm=1024, tn=D_FF)
    dw_down_t = _matmul(dh2b, act2, ta=True, out_dtype=BF16, name="mm_dw_down", tm=1024, tn=256, tk=8192)
    dg_, dup, g["conv_w"], g["conv_b"] = _conv_act_bwd(gu3, a_pre, W["conv_w"], dact.reshape(B, S, D_FF), name="conv_act_bwd")
    dg2 = dg_.reshape(T, D_FF)
    dup2 = dup.reshape(T, D_FF)
    dw_ffn_t = _matmul(u2, dg2, ta=True, out_t=True, out_dtype=BF16, into=lax.empty((2 * D_FF, D), BF16), o_noff=0, name="mm_dw_ffn_g", tm=1024, tn=256, tk=8192)
    dw_ffn_t = _matmul(u2, dup2, ta=True, out_t=True, out_dtype=BF16, into=dw_ffn_t, o_noff=D_FF // 256, name="mm_dw_ffn_u", tm=1024, tn=256, tk=8192)
    tok = emit("ffn", dict(w_ffn_t=dw_ffn_t, w_down=dw_down_t.T))
    def norm2_bwd_ep(acc_g, acc_u, h_ref, g_ref, dh2_ref):
        _, vjp = jax.vjp(_rms, h_ref[...], g_ref[...])
        dx, dg = vjp(acc_g + acc_u)
        dx = dx + dh2_ref[...]
        return dx, dx, dg

    dh, dhb, g["norm2_g"] = _matmul_ep(
        [(dg2, W["w_ffn_t"], False, 0), (dup2, W["w_ffn_t"], False, 1)], tm=512, ins=[h, small["norm2_g"] + zero(tok), dh2], in_specs=[row(512), vec, row(512)],
        out_shapes=[tile(F32), tile(BF16), vec_shape], out_specs=[row(512), row(512), vec], sums=(2,), epilogue=norm2_bwd_ep, name="mm_du2_norm2_bwd")
    dw_out = _matmul(merged, dhb, ta=True, out_dtype=BF16, name="mm_dw_out", tm=1024, tn=1024, tk=2048)

    def merge_bwd_ep(acc, g_ref, pa_ref, pb_ref, dz_in):
        gt = g_ref[...].astype(F32)
        sa = _sigmoid(gt[:, :D_MODEL])
        sb = _sigmoid(gt[:, D_MODEL:])
        dgates = jnp.concatenate([acc * pa_ref[...].astype(F32) * sa * (1.0 - sa), acc * pb_ref[...].astype(F32) * sb * (1.0 - sb)], axis=1)
        return dgates, acc * sa, acc * sb

    dz, dpa, dpb = _matmul_ep(
        [(dhb, W["w_out"], True, 0)], tm=512, ins=[z, pa, pb, lax.empty((T, W_IN), BF16)], in_specs=[_gates_spec(512), row(512), row(512), ANY],
        out_shapes=[jax.ShapeDtypeStruct((T, W_IN), BF16), tile(BF16), tile(BF16)], out_specs=[_gates_spec(512), row(512), row(512)],
        aliases={3: 0}, epilogue=merge_bwd_ep, name="mm_dmerged_merge_bwd")
    doa =_matmul(dpa, W["w_a"], tb=True, out_dtype=BF16, name="mm_doa", tm=1024, tn=1024)
    dw_a = _matmul(oa2, dpa, ta=True, out_dtype=BF16, name="mm_dw_a", tm=1024, tn=1024, tk=2048)
    dob = _matmul(dpb, W["w_b"], tb=True, out_dtype=BF16, name="mm_dob", tm=1024, tn=1024)
    dw_b = _matmul(ob2, dpb, ta=True, out_dtype=BF16, name="mm_dw_b", tm=1024, tn=1024, tk=2048)
    tok = emit("mix", dict(w_out=dw_out, w_a=dw_a, w_b=dw_b))
    dz3, dkv_cur, dkv_prev, dsinks = _swa_bwd(z3, qr, kr, cos, sin, small["attn_sinks"] + zero(tok), lse, dob.reshape(B, S, D),
                                              dz.reshape(B, S, W_IN), name="swa_bwd")
    dz3 = _swa_dkv_combine(dkv_cur, dkv_prev, dz3, name="swa_dkv")
    g["attn_sinks"] = dsinks
    dz3, g["lb"], g["hgrn_norm_g"] = _hgrn_bwd(z3, lb, small["hgrn_norm_g"], states, doa.reshape(B, S, D), dz3, name="hgrn_bwd")
    dz = dz3.reshape(T, W_IN)
    dw_in_t = _matmul(u1, dz, ta=True, out_t=True, o_block_perm=_reference_row_block, out_dtype=BF16, name="mm_dw_in", tm=1024, tn=256, tk=8192)
    tok = emit("in", dict(w_in_t=dw_in_t))
    du1 = _matmul(dz, w_in_t, after=tok, out_dtype=BF16, name="mm_du1", tm=1024, tn=512)
    dx, g["norm1_g"] = _norm_bwd_add(x2, small["norm1_g"], du1, dh, with_bf16=False, name="norm1_bwd")
    g["lb_logits"] = _lb_bwd(g.pop("lb"), lb, name="lb_bwd")
    return loss, dx.reshape(B, S, D), g


def _my_place():
    return lax.axis_index("x"), lax.axis_index("y"), lax.axis_index("c")


def _gather_blocks(x_ref, out_ref, send_sems, recv_sems, local_sem):
    x, y, c = _my_place()
    me, sibling = (x, y, c), (x, y, 1 - c)
    chips = [(1 - x, y), (x, 1 - y), (1 - x, 1 - y)]

    def slot(px, py, pc):
        return out_ref.at[4 * px + 2 * py + pc]

    def copy(k, block, to, src=None):
        return pltpu.make_async_remote_copy(
            src_ref=slot(*block) if src is None else src, dst_ref=slot(*block),
            send_sem=send_sems.at[k], recv_sem=recv_sems.at[k], device_id=to, device_id_type=MESH)

    mine = pltpu.make_async_copy(x_ref, slot(*me), local_sem)
    mine.start()
    first = [copy(0, me, sibling, src=x_ref)]
    first += [copy(1 + j, me, (*chip, c), src=x_ref) for j, chip in enumerate(chips)]
    for cp in first:
        cp.start()
    passed = [copy(4 + j, (*chip, c), sibling) for j, chip in enumerate(chips)]
    for j, chip in enumerate(chips):
        copy(1 + j, (*chip, c), me).wait_recv()
        passed[j].start()
    copy(0, sibling, me).wait_recv()
    for j, chip in enumerate(chips):
        copy(4 + j, (*chip, 1 - c), me).wait_recv()
    for cp in first + passed:
        cp.wait_send()
    mine.wait()


GATHER_SEMS = [pltpu.SemaphoreType.DMA((7,)), pltpu.SemaphoreType.DMA((7,)), pltpu.SemaphoreType.DMA]


def _all_gather(blk, *, name):
    return pl.pallas_call(
        _gather_body_fn(), name=name,
        out_shape=jax.ShapeDtypeStruct((N_DEV,) + blk.shape, blk.dtype),
        in_specs=[ANY], out_specs=ANY,
        scratch_shapes=GATHER_SEMS,
    )(blk)


def _gather_body_fn():
    def body(x_ref, out_ref, send_sems, recv_sems, local_sem):
        _gather_blocks(x_ref, out_ref, send_sems, recv_sems, local_sem)
    return body


SLAB_W = 1152
SMALL_SHAPES = dict(norm1_g=(1, D_MODEL), lb_logits=(2, HGRN_HEADS * HGRN_DK), hgrn_norm_g=(1, HGRN_DK), attn_sinks=(1, ATT_HEADS),
                    norm2_g=(1, D_MODEL), conv_b=(1, D_FF), final_g=(1, D_MODEL))
CONVW_BLK = D_FF // N_DEV
CONVW_STRIDE = SLAB_W // 3


def _slab_layout():
    layout, r = {}, 0
    for nm, (nr, w) in SMALL_SHAPES.items():
        layout[nm] = []
        for i in range(nr):
            for c0 in range(0, w, SLAB_W):
                layout[nm].append((r, i, c0, min(SLAB_W, w - c0)))
                r += 1
    return layout, r


SMALL_ROWS, _N_SMALL_ROWS = _slab_layout()
CONV_ROW0 = -(-_N_SMALL_ROWS // 8) * 8
LOSS_ROW = CONV_ROW0 + N_DEV
SLAB_ROWS = LOSS_ROW + 8


def _small_step(grads, g_conv_w, loss, params, moments, variances, dev, *, name):
    names = list(SMALL_ROWS)
    n = len(names)

    def body(dev_ref, *refs):
        g_refs = dict(zip(names, refs[:n]))
        gc_ref, loss_ref = refs[n], refs[n + 1]
        base = n + 2
        w_refs, m_refs, v_refs = (dict(zip(names + ["conv_w"], refs[base + i * (n + 1):base + (i + 1) * (n + 1)])) for i in range(3))
        o = base + 3 * (n + 1)
        gath_ref, loss_out = refs[o], refs[o + 1]
        outs = {nm: refs[o + 2 + 4 * i:o + 6 + 4 * i] for i, nm in enumerate(names + ["conv_w"])}
        slab, total, send_sems, recv_sems, local_sem = refs[-5:]

        slab[...] = jnp.zeros_like(slab)
        for nm, pieces in SMALL_ROWS.items():
            for r, i, c0, w in pieces:
                slab[r:r + 1, 0:w] = g_refs[nm][i:i + 1, c0:c0 + w]
        for p in range(N_DEV):
            for j in range(3):
                slab[CONV_ROW0 + p:CONV_ROW0 + p + 1, j * CONVW_STRIDE:j * CONVW_STRIDE + CONVW_BLK] = gc_ref[j:j + 1, p * CONVW_BLK:(p + 1) * CONVW_BLK]
        slab[LOSS_ROW:LOSS_ROW + 1, 0:1] = loss_ref[...]
        _gather_blocks(slab, gath_ref, send_sems, recv_sems, local_sem)
        acc = gath_ref[0]
        for p in range(1, N_DEV):
            acc = acc + gath_ref[p]
        total[...] = acc
        loss_out[...] = total[LOSS_ROW:LOSS_ROW + 1, 0:1]

        def update(nm, g, i, c0, w):
            at = (slice(i, i + 1), slice(c0, c0 + w))
            d, mn, vn = _adamw_math(w_refs[nm][at], g, m_refs[nm][at], v_refs[nm][at])
            for ref, val in zip(outs[nm], (g, d, mn, vn)):
                ref[at] = val

        for nm, pieces in SMALL_ROWS.items():
            for r, i, c0, w in pieces:
                update(nm, total[r:r + 1, 0:w], i, c0, w)
        conv_rows = total[CONV_ROW0:CONV_ROW0 + N_DEV, :]
        rowid = lax.broadcasted_iota(jnp.int32, conv_rows.shape, 0)
        mine = jnp.sum(jnp.where(rowid == dev_ref[0], conv_rows, 0.0), axis=0, keepdims=True)
        for j in range(3):
            update("conv_w", mine[:, j * CONVW_STRIDE:j * CONVW_STRIDE + CONVW_BLK], j, 0, CONVW_BLK)

    order = names + ["conv_w"]
    ins = [grads[nm] for nm in names] + [g_conv_w, loss]
    for d in (params, moments, variances):
        ins += [d[nm] for nm in order]
    vmem = pl.BlockSpec(memory_space=pltpu.VMEM)
    out_shape = [jax.ShapeDtypeStruct((N_DEV, SLAB_ROWS, SLAB_W), F32), jax.ShapeDtypeStruct((1, 1), F32)]
    for nm in order:
        out_shape += [jax.ShapeDtypeStruct(params[nm].shape, F32)] * 4
    res = pl.pallas_call(
        body, name=name,
        grid_spec=pltpu.PrefetchScalarGridSpec(
            num_scalar_prefetch=1, grid=(1,),
            in_specs=[vmem] * len(ins), out_specs=[vmem] * len(out_shape),
            scratch_shapes=[pltpu.VMEM((SLAB_ROWS, SLAB_W), F32), pltpu.VMEM((SLAB_ROWS, SLAB_W), F32)] + GATHER_SEMS),
        out_shape=out_shape,
    )(dev, *ins)
    return res[1], {nm: tuple(res[2 + 4 * i:6 + 4 * i]) for i, nm in enumerate(order)}


HBM_SPEC = pl.BlockSpec(memory_space=pltpu.HBM)
SEM_SPEC = pl.BlockSpec(memory_space=pltpu.SEMAPHORE)
DATAFLOW_EFFECT = pltpu.SideEffectType.DATAFLOW_SIDE_EFFECTING
N_PEERS = N_DEV - 1


def _peers(x, y, c):
    return [(1 - x if r & 4 else x, 1 - y if r & 2 else y, 1 - c if r & 1 else c) for r in range(1, N_DEV)]


def _exchange_start(srcs, scatter, *, after=None, name):
    n = len(srcs)
    lands = [lax.empty(a.shape if scatter else (N_DEV,) + a.shape, a.dtype) for a in srcs]
    extra = [] if after is None else [after]

    def body(*refs):
        src_refs, land_refs = refs[:n], refs[n:2 * n]
        send_sems, recv_sems, token = refs[2 * n + len(extra)], refs[2 * n + len(extra) + 1], refs[-1]
        x, y, c = _my_place()
        me = 4 * x + 2 * y + c
        for i in range(n):
            for r, (tx, ty, tc) in enumerate(_peers(x, y, c)):
                src = src_refs[i].at[4 * tx + 2 * ty + tc] if scatter else src_refs[i]
                pltpu.make_async_remote_copy(
                    src_ref=src, dst_ref=land_refs[i].at[me], send_sem=send_sems.at[N_PEERS * i + r],
                    recv_sem=recv_sems.at[N_PEERS * i + r], device_id=(tx, ty, tc), device_id_type=MESH).start()
        token[...] = jnp.zeros_like(token)

    thru = [pltpu.HBM(a.shape, a.dtype) for a in list(srcs) + lands]
    res = pl.pallas_call(
        body, name=name,
        out_shape=(pltpu.SemaphoreType.DMA((N_PEERS * n,)), pltpu.SemaphoreType.DMA((N_PEERS * n,)), *thru,
                   jax.ShapeDtypeStruct((8, 128), F32)),
        in_specs=[HBM_SPEC] * (2 * n) + [ANY] * len(extra),
        out_specs=(SEM_SPEC, SEM_SPEC, *([HBM_SPEC] * (2 * n)), pl.BlockSpec(memory_space=pltpu.VMEM)),
        input_output_aliases={i: 2 + i for i in range(2 * n)},
        compiler_params=pltpu.CompilerParams(has_side_effects=DATAFLOW_EFFECT),
    )(*[pltpu.with_memory_space_constraint(a, pltpu.HBM) for a in list(srcs) + lands], *extra)
    return (res[0], res[1], list(res[2:2 + n]), list(res[2 + n:2 + 2 * n]), scatter), res[-1]


def _exchange_wait(handle, after, *, name):
    send_sems, recv_sems, srcs, lands, scatter = handle
    n = len(srcs)

    def body(*refs):
        src_refs, land_refs = refs[:n], refs[n:2 * n]
        send_sems, recv_sems = refs[2 * n], refs[2 * n + 1]
        x, y, c = _my_place()
        for i in range(n):
            for r in range(N_PEERS):
                src = src_refs[i].at[0] if scatter else src_refs[i]
                cp = pltpu.make_async_remote_copy(
                    src_ref=src, dst_ref=land_refs[i].at[0], send_sem=send_sems.at[N_PEERS * i + r],
                    recv_sem=recv_sems.at[N_PEERS * i + r], device_id=(x, y, c), device_id_type=MESH)
                cp.wait_send()
                cp.wait_recv()

    thru = [pltpu.HBM(a.shape, a.dtype) for a in srcs + lands]
    res = pl.pallas_call(
        body, name=name, out_shape=tuple(thru),
        in_specs=[HBM_SPEC] * (2 * n) + [SEM_SPEC, SEM_SPEC, ANY], out_specs=tuple([HBM_SPEC] * (2 * n)),
        input_output_aliases={i: i for i in range(2 * n)},
        compiler_params=pltpu.CompilerParams(has_side_effects=DATAFLOW_EFFECT),
    )(*srcs, *lands, send_sems, recv_sems, after)
    return list(res[:n]), list(res[n:])


def _with_own(land, own, me):
    return lax.dynamic_update_index_in_dim(land, own, me, 0)


def _adamw_math(w, g, m, v):
    m = ADAM_B1 * m + (1.0 - ADAM_B1) * g
    v = ADAM_B2 * v + (1.0 - ADAM_B2) * (g * g)
    m_hat = m / (1.0 - ADAM_B1 ** ADAM_STEP)
    v_hat = v / (1.0 - ADAM_B2 ** ADAM_STEP)
    delta = -ADAM_LR * (m_hat / (jnp.sqrt(v_hat) + ADAM_EPS) + ADAM_WD * w)
    return delta, m, v


def _adamw_sum(parts, w, m, v, *, name):
    shape = w.shape
    R, n = shape[-2], shape[-1]
    w, m, v = (t.reshape(R, n) for t in (w, m, v))
    tr = _pick(R, (256, 464, 352, 128))

    def body(p_ref, w_ref, m_ref, v_ref, g_ref, d_ref, mo_ref, vo_ref):
        g = p_ref[0].astype(F32)
        for p in range(1, N_DEV):
            g = g + p_ref[p].astype(F32)
        d, mn, vn = _adamw_math(w_ref[...], g, m_ref[...], v_ref[...])
        g_ref[...] = g
        d_ref[...] = d
        mo_ref[...] = mn
        vo_ref[...] = vn

    row = pl.BlockSpec((tr, n), lambda i: (i, 0))
    outs = pl.pallas_call(
        body, name=name, grid=(R // tr,),
        in_specs=[pl.BlockSpec((N_DEV, tr, n), lambda i: (0, i, 0)), row, row, row],
        out_specs=[row, row, row, row],
        out_shape=[jax.ShapeDtypeStruct((R, n), F32)] * 4,
        compiler_params=_params("parallel"),
    )(parts, w, m, v)
    return [t.reshape(shape) for t in outs]


def _lb_bwd(dlb, lb, *, name):
    def body(d_ref, lb_ref, o_ref):
        t = d_ref[...] * lb_ref[...] * (1.0 - lb_ref[...])
        o_ref[0:1, :] = t
        o_ref[1:2, :] = -t

    return pl.pallas_call(body, name=name, out_shape=jax.ShapeDtypeStruct((2, lb.shape[1]), F32))(dlb, lb)


DOWN_BLK, ROW_BLK = D_FF // N_DEV, D_MODEL // N_DEV
W_FFN_BLK = 2 * D_FF // N_DEV
CONV_BITS_SHAPE = (16, 256)


def kernel(x, positions, norm1_g, w_in, lb_logits, hgrn_norm_g, w_a, attn_sinks, w_b, w_out, norm2_g, w_ffn_in, conv_w, conv_b, w_down, final_g, loss_target, m_norm1_g, m_w_in, m_lb_logits, m_hgrn_norm_g, m_w_a, m_attn_sinks, m_w_b, m_w_out, m_norm2_g, m_w_ffn_in, m_conv_w, m_conv_b, m_w_down, m_final_g, v_norm1_g, v_w_in, v_lb_logits, v_hgrn_norm_g, v_w_a, v_attn_sinks, v_w_b, v_w_out, v_norm2_g, v_w_ffn_in, v_conv_w, v_conv_b, v_w_down, v_final_g):
    xi, yi, ci = _my_place()
    dev = 4 * xi + 2 * yi + ci

    tr = lambda t: jnp.transpose(t[0])
    untr = lambda t: jnp.transpose(t)[None]
    w_in_blocks = _all_gather(tr(w_in).astype(BF16), name="ag_w_in")
    conv_bits = lax.bitcast_convert_type(conv_w, BF16).reshape(-1)
    conv_bits = jnp.pad(conv_bits, (0, CONV_BITS_SHAPE[0] * CONV_BITS_SHAPE[1] - conv_bits.shape[0])).reshape(CONV_BITS_SHAPE)
    w_in_full_t = _reordered_rows(w_in_blocks.reshape(W_IN, D_MODEL), name="w_in_rows")
    gather_handles = {}
    gather_handles["mix"], tok_mix = _exchange_start([w_a[0].astype(BF16), w_b[0].astype(BF16), w_out[0].astype(BF16)], False,
                                                     after=w_in_full_t, name="ag_mix_start")
    gather_handles["ffn"], tok_ffn = _exchange_start([tr(w_ffn_in).astype(BF16), w_down[0].astype(BF16), conv_bits], False,
                                                     after=tok_mix, name="ag_ffn_start")
    start_token = tok_mix + tok_ffn

    def rest_weights(group, after):
        own, lands = _exchange_wait(gather_handles[group], after, name="ag_" + group + "_wait")
        full = [_with_own(l, o, dev) for l, o in zip(lands, own)]
        if group == "mix":
            return dict(zip(("w_a", "w_b", "w_out"), [t.reshape(D_MODEL, D_MODEL) for t in full]))
        bits = full[2].reshape(N_DEV, -1)[:, :3 * CONVW_BLK * 2].reshape(N_DEV, 3, CONVW_BLK, 2)
        return dict(w_ffn_t=full[0].reshape(2 * D_FF, D_MODEL), w_down=full[1].reshape(D_FF, D_MODEL),
                    conv_w=lax.bitcast_convert_type(bits, F32).transpose(1, 0, 2).reshape(3, D_FF))

    handles = {}

    def emit(group, gr):
        if group == "ffn":
            srcs = [gr["w_ffn_t"].reshape(N_DEV, W_FFN_BLK, D_MODEL), gr["w_down"].reshape(N_DEV, DOWN_BLK, D_MODEL)]
        elif group == "mix":
            srcs = [gr[n].reshape(N_DEV, ROW_BLK, D_MODEL) for n in ("w_out", "w_a", "w_b")]
        else:
            srcs = [gr["w_in_t"].reshape(N_DEV, W_IN_BLK, D_MODEL)]
        handles[group], token = _exchange_start(srcs, True, name="rs_" + group + "_start")
        return token

    small = dict(norm1_g=norm1_g, lb_logits=lb_logits, hgrn_norm_g=hgrn_norm_g, attn_sinks=attn_sinks, norm2_g=norm2_g,
                 conv_b=conv_b, final_g=final_g)
    loss, grad_x, g = _local_step(x, positions, loss_target, small, w_in_full_t, rest_weights, emit, start_token)

    def parts_of(group, after):
        srcs, lands = _exchange_wait(handles[group], after, name="rs_" + group + "_wait")
        return [_with_own(l, lax.dynamic_index_in_dim(s, dev, 0, keepdims=False), dev) for s, l in zip(srcs, lands)]

    p_ffn, p_down = parts_of("ffn", grad_x)
    p_out, p_a, p_b = parts_of("mix", grad_x)
    (p_in,) = parts_of("in", grad_x)
    big = dict(
        w_in=[untr(t) for t in _adamw_sum(p_in, tr(w_in), tr(m_w_in), tr(v_w_in), name="adamw_w_in")],
        w_a=_adamw_sum(p_a, w_a, m_w_a, v_w_a, name="adamw_w_a"),
        w_b=_adamw_sum(p_b, w_b, m_w_b, v_w_b, name="adamw_w_b"),
        w_out=_adamw_sum(p_out, w_out, m_w_out, v_w_out, name="adamw_w_out"),
        w_ffn_in=[untr(t) for t in _adamw_sum(p_ffn, tr(w_ffn_in), tr(m_w_ffn_in), tr(v_w_ffn_in), name="adamw_w_ffn_in")],
        w_down=_adamw_sum(p_down, w_down, m_w_down, v_w_down, name="adamw_w_down"),
    )

    row = lambda t: t.reshape(1, -1) if t.ndim == 1 else t
    shard = lambda t: t.reshape(3, CONVW_BLK)
    sm_g = {nm: g[nm] for nm in SMALL_ROWS}
    sm_w = dict(norm1_g=norm1_g, lb_logits=lb_logits, hgrn_norm_g=hgrn_norm_g, attn_sinks=attn_sinks, norm2_g=norm2_g,
                conv_b=conv_b, final_g=row(final_g), conv_w=shard(conv_w))
    sm_m = dict(norm1_g=m_norm1_g, lb_logits=m_lb_logits, hgrn_norm_g=m_hgrn_norm_g, attn_sinks=m_attn_sinks, norm2_g=m_norm2_g,
                conv_b=m_conv_b, final_g=row(m_final_g), conv_w=shard(m_conv_w))
    sm_v = dict(norm1_g=v_norm1_g, lb_logits=v_lb_logits, hgrn_norm_g=v_hgrn_norm_g, attn_sinks=v_attn_sinks, norm2_g=v_norm2_g,
                conv_b=v_conv_b, final_g=row(v_final_g), conv_w=shard(v_conv_w))
    loss_total, sm_out = _small_step(sm_g, g["conv_w"], loss, sm_w, sm_m, sm_v, dev.astype(jnp.int32).reshape(1), name="small_step")
    shapes = dict(final_g=final_g.shape, conv_w=conv_w.shape)

    names = ("norm1_g", "w_in", "lb_logits", "hgrn_norm_g", "w_a", "attn_sinks", "w_b", "w_out", "norm2_g", "w_ffn_in", "conv_w", "conv_b", "w_down", "final_g")
    outs = [loss_total.reshape(()), grad_x]
    for kind in range(4):
        outs += [big[n][kind] if n in big else sm_out[n][kind].reshape(shapes.get(n, sm_out[n][kind].shape)) for n in names]
    return tuple(outs)
```

```python
import jax
import jax.numpy as jnp
from jax import lax
from jax.experimental import pallas as pl
from jax.experimental.pallas import tpu as pltpu

F32 = jnp.float32
BF16 = jnp.bfloat16

D_MODEL = 1024
HGRN_HEADS = 8
HGRN_DK = 128
CHUNK = 64
ATT_HEADS = 16
ATT_KV_HEADS = 2
ATT_HD = 64
ATT_GROUP = ATT_HEADS // ATT_KV_HEADS
WINDOW = 128
ROPE_DIM = ATT_HD // 4
ROPE_THETA = 500000.0
D_FF = 2816
EPS = 1e-6
NEG_INF = -1e30
N_DEV = 8

ADAM_LR = 0.001
ADAM_B1 = 0.9
ADAM_B2 = 0.999
ADAM_EPS = 1e-08
ADAM_WD = 0.01
ADAM_STEP = 10

MESH = pl.DeviceIdType.MESH
ANY = pl.BlockSpec(memory_space=pl.ANY)


def _pick(n, cands):
    for c in cands:
        if n % c == 0:
            return c
    return n


def _sigmoid(x):
    return 0.5 * jnp.tanh(0.5 * x) + 0.5


def _silu(x):
    hx = 0.5 * x
    return hx * jnp.tanh(hx) + hx


def _rms(x, g):
    return x * lax.rsqrt(jnp.mean(x * x, axis=-1, keepdims=True) + EPS) * g


def _dot(a, b, dims):
    return lax.dot_general(a, b, (dims, ((), ())), preferred_element_type=F32)


def _nn(a, b):
    return _dot(a, b, ((1,), (0,)))


def _nt(a, b):
    return _dot(a, b, ((1,), (1,)))


def _tn(a, b):
    return _dot(a, b, ((0,), (0,)))


def _params(*sem):
    return pltpu.CompilerParams(dimension_semantics=sem, vmem_limit_bytes=56 * 1024 * 1024)


def _matmul(a, b, *, ta=False, tb=False, out_dtype=F32, addend=None, after=None, into=None, o_noff=0, out_t=False,
            o_block_perm=lambda j: j, name, tm, tn, tk=None, n_extent=None, b_koff=0, b_noff=0):
    M, K = (a.shape[1], a.shape[0]) if ta else a.shape
    N = n_extent or (b.shape[0] if tb else b.shape[1])
    tm, tn, tk = min(tm, M), min(tn, N), min(tk or K, K)
    assert M % tm == 0 and N % tn == 0 and K % tk == 0, (name, M, N, K, tm, tn, tk)
    nk = K // tk
    use_scratch = nk > 1 and out_dtype != F32
    grid = (M // tm, N // tn, nk)
    a_spec = pl.BlockSpec((tk, tm), lambda i, j, k: (k, i)) if ta else pl.BlockSpec((tm, tk), lambda i, j, k: (i, k))
    b_spec = pl.BlockSpec((tn, tk), lambda i, j, k: (j + b_noff, k + b_koff)) if tb else pl.BlockSpec((tk, tn), lambda i, j, k: (k + b_koff, j + b_noff))
    o_spec = pl.BlockSpec((tm, tn), lambda i, j, k: (i, j))
    dims = ((0 if ta else 1,), (1 if tb else 0,))
    has_add = addend is not None

    n_in = 2 + has_add + (after is not None) + (into is not None)

    def body(*refs):
        a_ref, b_ref = refs[:2]
        c_ref = refs[2] if has_add else None
        o_ref = refs[n_in]
        part = _dot(a_ref[...], b_ref[...], dims)
        if nk == 1:
            if has_add:
                part = part + c_ref[...].astype(F32)
            o_ref[...] = (part.T if out_t else part).astype(out_dtype)
        else:
            acc_ref = refs[-1] if use_scratch else o_ref
            k = pl.program_id(2)

            @pl.when(k == 0)
            def _():
                acc_ref[...] = part + c_ref[...].astype(F32) if has_add else part

            @pl.when(k > 0)
            def _():
                acc_ref[...] += part

            if use_scratch:
                @pl.when(k == nk - 1)
                def _():
                    o_ref[...] = acc_ref[...].astype(out_dtype)

    in_specs = [a_spec, b_spec] + ([o_spec] if has_add else [])
    args = (a, b) + ((addend,) if has_add else ())
    if after is not None:
        in_specs.append(pl.BlockSpec(after.shape, lambda i, j, k: (0, 0)))
        args += (after,)
    aliases = {}
    if into is not None:
        in_specs.append(ANY)
        args += (into,)
        aliases = {len(args) - 1: 0}
    if out_t:
        assert nk == 1 and not has_add
        o_spec = pl.BlockSpec((tn, tm), lambda i, j, k: (o_block_perm(j) + o_noff, i))
    elif into is not None:
        o_spec = pl.BlockSpec((tm, tn), lambda i, j, k: (i, j + o_noff))
    return pl.pallas_call(
        body,
        name=name,
        grid=grid,
        in_specs=in_specs,
        out_specs=o_spec,
        out_shape=jax.ShapeDtypeStruct(into.shape if into is not None else ((N, M) if out_t else (M, N)), out_dtype),
        input_output_aliases=aliases,
        scratch_shapes=[pltpu.VMEM((tm, tn), F32)] if use_scratch else [],
        compiler_params=_params("parallel", "parallel", "arbitrary"),
    )(*args)


def _matmul_ep(pairs, *, tm, ins, in_specs, out_shapes, out_specs, sums=(), epilogue, aliases=None, name):
    M = pairs[0][0].shape[0]
    tm = min(tm, M)
    mm_specs, mm_args, dims = [], [], []
    for a, b, tb, koff in pairs:
        K = a.shape[1]
        N = b.shape[0] if tb else b.shape[1]
        mm_specs += [pl.BlockSpec((tm, K), lambda i: (i, 0)),
                     pl.BlockSpec((N, K), lambda i, koff=koff: (0, koff)) if tb else pl.BlockSpec((K, N), lambda i, koff=koff: (koff, 0))]
        mm_args += [a, b]
        dims.append(((1,), (1 if tb else 0,)))
    n_mm = len(mm_args)
    n_in = n_mm + len(ins)

    def body(*refs):
        in_refs, out_refs = refs[n_mm:n_in], refs[n_in:]
        accs = [_dot(refs[2 * p][...], refs[2 * p + 1][...], dims[p]) for p in range(len(pairs))]
        outs = epilogue(*accs, *in_refs)
        for k, (ref, val) in enumerate(zip(out_refs, outs)):
            if val is None:
                continue
            if k in sums:
                @pl.when(pl.program_id(0) == 0)
                def _():
                    ref[...] = jnp.zeros_like(ref)

                ref[...] += val
            else:
                ref[...] = val.astype(ref.dtype)

    return pl.pallas_call(
        body, name=name, grid=(M // tm,),
        in_specs=mm_specs + list(in_specs),
        out_specs=list(out_specs), out_shape=list(out_shapes),
        input_output_aliases={n_mm + k: v for k, v in (aliases or {}).items()},
        compiler_params=_params("arbitrary"),
    )(*mm_args, *ins)


def _row_spec(tm, n):
    return pl.BlockSpec((tm, n), lambda i: (i, 0))


def _full_spec(shape):
    return pl.BlockSpec(shape, lambda i: tuple(0 for _ in shape))


def _norm_cast(x, g, *, name):
    T, D = x.shape
    tm = _pick(T, (512, 256, 128))

    def body(x_ref, g_ref, u_ref):
        u_ref[...] = _rms(x_ref[...], g_ref[...]).astype(BF16)

    return pl.pallas_call(
        body, name=name, grid=(T // tm,),
        in_specs=[_row_spec(tm, D), _full_spec((1, D))],
        out_specs=_row_spec(tm, D),
        out_shape=jax.ShapeDtypeStruct((T, D), BF16),
        compiler_params=_params("parallel"),
    )(x, g)


def _norm_bwd_add(x, g, du, dres, *, with_bf16=True, name):
    T, D = x.shape
    tm = _pick(T, (512, 256, 128))

    def body(x_ref, g_ref, du_ref, dr_ref, dx_ref, *rest):
        dg_ref = rest[-1]
        _, vjp = jax.vjp(_rms, x_ref[...], g_ref[...])
        dx, dg = vjp(du_ref[...].astype(F32))
        dx = dx + dr_ref[...]
        dx_ref[...] = dx
        if with_bf16:
            rest[0][...] = dx.astype(BF16)

        @pl.when(pl.program_id(0) == 0)
        def _():
            dg_ref[...] = jnp.zeros_like(dg_ref)

        dg_ref[...] += dg

    row = _row_spec(tm, D)
    return pl.pallas_call(
        body, name=name, grid=(T // tm,),
        in_specs=[row, _full_spec((1, D)), row, row],
        out_specs=[row] + ([row] if with_bf16 else []) + [_full_spec((1, D))],
        out_shape=[jax.ShapeDtypeStruct((T, D), F32)] + ([jax.ShapeDtypeStruct((T, D), BF16)] if with_bf16 else []) + [jax.ShapeDtypeStruct((1, D), F32)],
        compiler_params=_params("arbitrary"),
    )(x, g, du, dres)


def _merge_fn(gates, a, b):
    ga = gates[:, :D_MODEL].astype(F32)
    gb = gates[:, D_MODEL:].astype(F32)
    return _sigmoid(ga) * a.astype(F32) + _sigmoid(gb) * b.astype(F32)


def _gates_spec(tm):
    return pl.BlockSpec((tm, W_GATES), lambda i: (i, O_GATES // W_GATES))


CONV_TC = 256


def _shift_down(x, n, rows):
    return jnp.where(rows >= n, pltpu.roll(x, n, 0), 0.0)


def _shift_up(x, n, rows, S):
    return jnp.where(rows < S - n, pltpu.roll(x, S - n, 0), 0.0)


def _conv_act_fwd(gu, conv_w, conv_b, *, name):
    B, S, _ = gu.shape
    tc = CONV_TC
    nc = D_FF // tc

    def body(g_ref, up_ref, w_ref, b_ref, o_ref, a_ref):
        g = g_ref[...].astype(F32)
        rows = lax.broadcasted_iota(jnp.int32, g.shape, 0)
        w = w_ref[...]
        a = w[2:3] * g + w[1:2] * _shift_down(g, 1, rows) + w[0:1] * _shift_down(g, 2, rows) + b_ref[...]
        o_ref[...] = (_silu(a) * up_ref[...].astype(F32)).astype(BF16)
        a_ref[...] = a.astype(BF16)

    col = pl.BlockSpec((None, S, tc), lambda b, j: (b, 0, j))
    return pl.pallas_call(
        body, name=name, grid=(B, nc),
        in_specs=[col,
                  pl.BlockSpec((None, S, tc), lambda b, j: (b, 0, j + nc)),
                  pl.BlockSpec((3, tc), lambda b, j: (0, j)),
                  pl.BlockSpec((1, tc), lambda b, j: (0, j))],
        out_specs=[col, col],
        out_shape=[jax.ShapeDtypeStruct((B, S, D_FF), BF16)] * 2,
        compiler_params=_params("parallel", "parallel"),
    )(gu, gu, conv_w, conv_b)


def _conv_act_bwd(gu, a_pre, conv_w, dact, *, name):
    B, S, _ = gu.shape
    tc = CONV_TC
    nc = D_FF // tc

    def body(g_ref, up_ref, a_ref, w_ref, da_ref, dg_ref, dup_ref, dw_ref, db_ref):
        g = g_ref[...].astype(F32)
        up, a, dact = up_ref[...], a_ref[...], da_ref[...]
        rows = lax.broadcasted_iota(jnp.int32, g.shape, 0)
        w = w_ref[...]
        sg = _sigmoid(a)
        dup_ref[...] = dact * a * sg
        da = (dact * up * sg * (1.0 + a * (1.0 - sg))).astype(F32)
        da1 = _shift_up(da, 1, rows, S)
        da2 = _shift_up(da, 2, rows, S)
        dg_ref[...] = (w[2:3] * da + w[1:2] * da1 + w[0:1] * da2).astype(BF16)

        @pl.when(pl.program_id(1) == 0)
        def _():
            dw_ref[...] = jnp.zeros_like(dw_ref)
            db_ref[...] = jnp.zeros_like(db_ref)

        dw_ref[0:1, :] += jnp.sum(da2 * g, axis=0, keepdims=True)
        dw_ref[1:2, :] += jnp.sum(da1 * g, axis=0, keepdims=True)
        dw_ref[2:3, :] += jnp.sum(da * g, axis=0, keepdims=True)
        db_ref[...] += jnp.sum(da, axis=0, keepdims=True)

    col = pl.BlockSpec((None, S, tc), lambda j, b: (b, 0, j))
    return pl.pallas_call(
        body, name=name, grid=(nc, B),
        in_specs=[col,
                  pl.BlockSpec((None, S, tc), lambda j, b: (b, 0, j + nc)),
                  col,
                  pl.BlockSpec((3, tc), lambda j, b: (0, j)),
                  col],
        out_specs=[col, col, pl.BlockSpec((3, tc), lambda j, b: (0, j)), pl.BlockSpec((1, tc), lambda j, b: (0, j))],
        out_shape=[jax.ShapeDtypeStruct((B, S, D_FF), BF16), jax.ShapeDtypeStruct((B, S, D_FF), BF16),
                   jax.ShapeDtypeStruct((3, D_FF), F32), jax.ShapeDtypeStruct((1, D_FF), F32)],
        compiler_params=_params("parallel", "arbitrary"),
    )(gu, gu, a_pre, conv_w, dact)


HGRN_CPB = 8
HF = HGRN_HEADS * HGRN_DK


def _tri(n, upper=False):
    r = lax.broadcasted_iota(jnp.int32, (n, n), 0)
    c = lax.broadcasted_iota(jnp.int32, (n, n), 1)
    return (c >= r) if upper else (r >= c)


def _hs(h):
    return slice(h * HGRN_DK, (h + 1) * HGRN_DK)


def _cumsum_rows(tri_b, x):
    hi = x.astype(BF16)
    lo = (x - hi.astype(F32)).astype(BF16)
    return _nn(tri_b, hi) + _nn(tri_b, lo)


def _hgrn_pre(q, fz, lb, tril_b):
    qf = _silu(q)
    sg = _sigmoid(fz)
    f = lb + (1.0 - lb) * sg
    k = 1.0 - f
    b = _cumsum_rows(tril_b, jnp.log2(f))
    bref = b[CHUNK // 2:CHUNK // 2 + 1, :]
    blast = b[CHUNK - 1:CHUNK, :]
    e1 = jnp.exp2(b - bref)
    e2 = jnp.exp2(bref - b)
    e3 = e1 * jnp.exp2(bref)
    e4 = e2 * jnp.exp2(blast - bref)
    dec = jnp.exp2(blast)
    return sg, f, (e1, e2, e3, e4), qf * e1, k * e2, qf * e3, k * e4, dec


def _hgrn_fwd(zh, lb, gn, *, name):
    B, S, _ = zh.shape
    cpb = HGRN_CPB
    ts = cpb * CHUNK
    nblk = S // ts

    def body(z_ref, lb_ref, gn_ref, o_ref, st_ref, state):
        @pl.when(pl.program_id(1) == 0)
        def _():
            state[...] = jnp.zeros_like(state)

        H = HGRN_HEADS
        causal = _tri(CHUNK)
        tril_b = causal.astype(BF16)
        lb = lb_ref[...]
        for c in range(cpb):
            rows = slice(c * CHUNK, (c + 1) * CHUNK)
            q = z_ref[rows, 0:HF].astype(F32)
            fz = z_ref[rows, HF:2 * HF].astype(F32)
            v = z_ref[rows, 2 * HF:3 * HF]
            hg = z_ref[rows, 3 * HF:4 * HF].astype(F32)
            _, _, _, q_in, k_in, q_out, k_st, dec = _hgrn_pre(q, fz, lb, tril_b)
            q_in, k_in, q_out, k_st = (t.astype(BF16) for t in (q_in, k_in, q_out, k_st))
            a = [jnp.where(causal, _nt(q_in[:, _hs(h)], k_in[:, _hs(h)]), 0.0).astype(BF16) for h in range(H)]
            st = [state[h] for h in range(H)]
            for h in range(H):
                st_ref[c, h] = st[h]
            o = [_nn(a[h], v[:, _hs(h)]) + _nt(q_out[:, _hs(h)], st[h].astype(BF16)) for h in range(H)]
            for h in range(H):
                state[h] = st[h] * dec[:, _hs(h)] + _tn(v[:, _hs(h)], k_st[:, _hs(h)])
            gate = _silu(hg)
            for h in range(H):
                o_ref[rows, _hs(h)] = (_rms(o[h], gn_ref[...]) * gate[:, _hs(h)]).astype(BF16)

    return pl.pallas_call(
        body, name=name, grid=(B, nblk),
        in_specs=[pl.BlockSpec((None, ts, 4 * HF), lambda b, s: (b, s, 0)),
                  pl.BlockSpec((1, HF), lambda b, s: (0, 0)),
                  pl.BlockSpec((1, HGRN_DK), lambda b, s: (0, 0))],
        out_specs=[pl.BlockSpec((None, ts, HF), lambda b, s: (b, s, 0)),
                   pl.BlockSpec((None, cpb, HGRN_HEADS, HGRN_DK, HGRN_DK), lambda b, s: (b, s, 0, 0, 0))],
        out_shape=[jax.ShapeDtypeStruct((B, S, HF), BF16),
                   jax.ShapeDtypeStruct((B, S // CHUNK, HGRN_HEADS, HGRN_DK, HGRN_DK), F32)],
        scratch_shapes=[pltpu.VMEM((HGRN_HEADS, HGRN_DK, HGRN_DK), F32)],
        compiler_params=_params("arbitrary", "arbitrary"),
    )(zh, lb, gn)


def _hgrn_bwd(zh, lb, gn, states, doa, dz, *, name):
    B, S, _ = zh.shape
    cpb = HGRN_CPB
    ts = cpb * CHUNK
    nblk = S // ts
    rev = lambda b, s: (b, nblk - 1 - s, 0)

    def body(z_ref, lb_ref, gn_ref, st_ref, do_ref, dz_in, dz_ref, dlb_ref, dgn_ref, dstate):
        @pl.when(pl.program_id(1) == 0)
        def _():
            dstate[...] = jnp.zeros_like(dstate)

        @pl.when((pl.program_id(0) == 0) & (pl.program_id(1) == 0))
        def _():
            dlb_ref[...] = jnp.zeros_like(dlb_ref)
            dgn_ref[...] = jnp.zeros_like(dgn_ref)

        H = HGRN_HEADS
        cat = lambda xs: jnp.concatenate(xs, axis=1)
        causal = _tri(CHUNK)
        tril_b = causal.astype(BF16)
        triu_b = _tri(CHUNK, upper=True).astype(BF16)
        rowid = lax.broadcasted_iota(jnp.int32, (CHUNK, HF), 0)
        lb = lb_ref[...]
        gn = gn_ref[...]
        for c in reversed(range(cpb)):
            rows = slice(c * CHUNK, (c + 1) * CHUNK)
            q = z_ref[rows, 0:HF].astype(F32)
            fz = z_ref[rows, HF:2 * HF].astype(F32)
            v = z_ref[rows, 2 * HF:3 * HF]
            hg = z_ref[rows, 3 * HF:4 * HF].astype(F32)
            sg, f, (e1, e2, e3, e4), q_in, k_in, q_out, k_st, dec = _hgrn_pre(q, fz, lb, tril_b)
            q_in_b, k_in_b, q_out_b, k_st_b = (t.astype(BF16) for t in (q_in, k_in, q_out, k_st))
            a_b = [jnp.where(causal, _nt(q_in_b[:, _hs(h)], k_in_b[:, _hs(h)]), 0.0).astype(BF16) for h in range(H)]
            st = [st_ref[c, h] for h in range(H)]
            st_b = [t.astype(BF16) for t in st]
            o = [_nn(a_b[h], v[:, _hs(h)]) + _nt(q_out_b[:, _hs(h)], st_b[h]) for h in range(H)]
            dout = do_ref[rows, :].astype(F32)
            shg = _sigmoid(hg)
            gate = hg * shg
            do_l, dgn_acc = [], jnp.zeros_like(gn)
            for h in range(H):
                _, norm_vjp = jax.vjp(_rms, o[h], gn)
                d_o, d_gn = norm_vjp(dout[:, _hs(h)] * gate[:, _hs(h)])
                do_l.append(d_o)
                dgn_acc = dgn_acc + d_gn
            dgn_ref[...] += dgn_acc
            on = cat([_rms(o[h], gn) for h in range(H)])
            dhg = dout * on * shg * (1.0 + hg * (1.0 - shg))
            do_b = [t.astype(BF16) for t in do_l]
            dst = [dstate[h] for h in range(H)]
            dst_b = [t.astype(BF16) for t in dst]
            da_b = [jnp.where(causal, _nt(do_b[h], v[:, _hs(h)]), 0.0).astype(BF16) for h in range(H)]
            dv = cat([_tn(a_b[h], do_b[h]) + _nt(k_st_b[:, _hs(h)], dst_b[h]) for h in range(H)])
            dq_in = cat([_nn(da_b[h], k_in_b[:, _hs(h)]) for h in range(H)])
            dk_in = cat([_tn(da_b[h], q_in_b[:, _hs(h)]) for h in range(H)])
            dq_out = cat([_nn(do_b[h], st_b[h]) for h in range(H)])
            dk_st = cat([_nn(v[:, _hs(h)], dst_b[h]) for h in range(H)])
            ddec = cat([jnp.sum(st[h] * dst[h], axis=0, keepdims=True) for h in range(H)])
            for h in range(H):
                dstate[h] = dst[h] * dec[:, _hs(h)] + _tn(do_b[h], q_out_b[:, _hs(h)])
            t_qin = dq_in * q_in
            t_kin = dk_in * k_in
            t_kst = dk_st * k_st
            db = t_qin - t_kin + dq_out * q_out - t_kst
            dbref = jnp.sum(t_kin - t_qin, axis=0, keepdims=True)
            dblast = jnp.sum(t_kst, axis=0, keepdims=True) + ddec * dec
            db = db + jnp.where(rowid == CHUNK // 2, dbref, 0.0) + jnp.where(rowid == CHUNK - 1, dblast, 0.0)
            dlogf = _cumsum_rows(triu_b, db)
            dqf = dq_in * e1 + dq_out * e3
            dk = dk_in * e2 + dk_st * e4
            df = dlogf / f - dk
            dfz = df * (1.0 - lb) * sg * (1.0 - sg)
            dlb_ref[...] += jnp.sum(df * (1.0 - sg), axis=0, keepdims=True)
            sq = _sigmoid(q)
            dq = dqf * sq * (1.0 + q * (1.0 - sq))
            dz_ref[rows, 0:HF] = dq.astype(BF16)
            dz_ref[rows, HF:2 * HF] = dfz.astype(BF16)
            dz_ref[rows, 2 * HF:3 * HF] = dv.astype(BF16)
            dz_ref[rows, 3 * HF:4 * HF] = dhg.astype(BF16)

    return pl.pallas_call(
        body, name=name, grid=(B, nblk),
        in_specs=[pl.BlockSpec((None, ts, 4 * HF), rev),
                  pl.BlockSpec((1, HF), lambda b, s: (0, 0)),
                  pl.BlockSpec((1, HGRN_DK), lambda b, s: (0, 0)),
                  pl.BlockSpec((None, cpb, HGRN_HEADS, HGRN_DK, HGRN_DK), lambda b, s: (b, nblk - 1 - s, 0, 0, 0)),
                  pl.BlockSpec((None, ts, HF), rev),
                  ANY],
        out_specs=[pl.BlockSpec((None, ts, 4 * HF), rev),
                   pl.BlockSpec((1, HF), lambda b, s: (0, 0)),
                   pl.BlockSpec((1, HGRN_DK), lambda b, s: (0, 0))],
        out_shape=[jax.ShapeDtypeStruct(dz.shape, BF16),
                   jax.ShapeDtypeStruct((1, HF), F32),
                   jax.ShapeDtypeStruct((1, HGRN_DK), F32)],
        input_output_aliases={5: 0},
        scratch_shapes=[pltpu.VMEM((HGRN_HEADS, HGRN_DK, HGRN_DK), F32)],
        compiler_params=_params("arbitrary", "arbitrary"),
    )(zh, lb, gn, states, doa, dz)


KV_W = ATT_KV_HEADS * ATT_HD
ATT_SCALE = ATT_HD ** -0.5


def _rope(x, cos, sin, inverse=False):
    half = ROPE_DIM // 2
    outs = []
    for p in range(x.shape[1] // 128):
        xp = x[:, p * 128:(p + 1) * 128]
        lane = lax.broadcasted_iota(jnp.int32, xp.shape, 1) % ATT_HD
        sw = jnp.where(lane < half, pltpu.roll(xp, 128 - half, 1), pltpu.roll(xp, half, 1))
        outs.append(xp * cos - sw * sin if inverse else xp * cos + sw * sin)
    return outs[0] if len(outs) == 1 else jnp.concatenate(outs, axis=1)


PAIRS_PER_KV = ATT_GROUP // 2


def _swap_halves(x):
    return pltpu.roll(x, ATT_HD, 1)


def _kv_padded(t, low):
    sw = _swap_halves(t)
    zero = jnp.zeros_like(t)
    out = []
    for g in range(ATT_KV_HEADS):
        in_low, in_high = (t, sw) if g == 0 else (sw, t)
        out.append((jnp.where(low, in_low, zero).astype(BF16), jnp.where(low, zero, in_high).astype(BF16)))
    return out


def _swa_mask(first_block):
    qi = lax.broadcasted_iota(jnp.int32, (WINDOW, 2 * WINDOW), 0)
    mi = lax.broadcasted_iota(jnp.int32, (WINDOW, 2 * WINDOW), 1)
    band = (mi > qi) & (mi <= qi + WINDOW)
    return band & (jnp.logical_not(first_block) | (mi >= WINDOW))


def _swa_specs(nb):
    cur = lambda b, i: (b, i, 0)
    prev = lambda b, i: (b, jnp.maximum(i - 1, 0), 0)
    return cur, prev


def _swa_z_specs():
    q = pl.BlockSpec((None, WINDOW, W_AQ), lambda b, i: (b, i, O_AQ // W_AQ))
    kv_prev = pl.BlockSpec((None, WINDOW, W_AKV), lambda b, i: (b, jnp.maximum(i - 1, 0), O_AKV // W_AKV))
    kv_cur = pl.BlockSpec((None, WINDOW, W_AKV), lambda b, i: (b, i, O_AKV // W_AKV))
    return q, kv_prev, kv_cur


def _swa_fwd(z, cos, sin, sinks, *, name):
    B, S, _ = z.shape
    nb = S // WINDOW
    cur, prev = _swa_specs(nb)

    def body(q_ref, kvp_ref, kvc_ref, cp_ref, sp_ref, cc_ref, sc_ref, sink_ref, o_ref, lse_ref, qr_ref, kr_ref):
        cos_c, sin_c = cc_ref[...], sc_ref[...]
        q = (_rope(q_ref[...].astype(F32), cos_c, sin_c) * ATT_SCALE).astype(BF16)
        k = jnp.concatenate([_rope(kvp_ref[:, :KV_W].astype(F32), cp_ref[...], sp_ref[...]),
                             _rope(kvc_ref[:, :KV_W].astype(F32), cos_c, sin_c)], axis=0)
        qr_ref[...] = q
        kr_ref[...] = k[WINDOW:].astype(BF16)
        v = jnp.concatenate([kvp_ref[:, KV_W:], kvc_ref[:, KV_W:]], axis=0).astype(F32)
        low = lax.broadcasted_iota(jnp.int32, k.shape, 1) < ATT_HD
        kpad = _kv_padded(k, low)
        vpad = _kv_padded(v, low)
        mask = _swa_mask(pl.program_id(1) == 0)
        lses = []
        for g in range(ATT_KV_HEADS):
            pairs = range(g * PAIRS_PER_KV, (g + 1) * PAIRS_PER_KV)
            keys = [(p, e) for p in pairs for e in (0, 1)]
            qp = {p: q[:, p * 128:(p + 1) * 128] for p in pairs}
            s = {pe: jnp.where(mask, _nt(qp[pe[0]], kpad[g][pe[1]]), NEG_INF) for pe in keys}
            pr = {}
            for pe in keys:
                sink = sink_ref[0, 2 * pe[0] + pe[1]]
                m = jnp.maximum(jnp.max(s[pe], axis=1, keepdims=True), sink)
                ex = jnp.exp(s[pe] - m)
                den = jnp.sum(ex, axis=1, keepdims=True) + jnp.exp(sink - m)
                pr[pe] = (ex * (1.0 / den)).astype(BF16)
                lses.append(m + jnp.log(den))
            for p in pairs:
                o_ref[:, p * 128:(p + 1) * 128] = (_nn(pr[p, 0], vpad[g][0]) + _nn(pr[p, 1], vpad[g][1])).astype(BF16)
        lse_ref[...] = jnp.concatenate(lses, axis=1)

    tab = lambda im: pl.BlockSpec((None, WINDOW, 128), im)
    return pl.pallas_call(
        body, name=name, grid=(B, nb),
        in_specs=[*_swa_z_specs(),
                  tab(prev), tab(prev), tab(cur), tab(cur),
                  pl.BlockSpec(memory_space=pltpu.SMEM)],
        out_specs=[pl.BlockSpec((None, WINDOW, D_MODEL), cur), pl.BlockSpec((None, WINDOW, ATT_HEADS), cur),
                   pl.BlockSpec((None, WINDOW, D_MODEL), cur), pl.BlockSpec((None, WINDOW, KV_W), cur)],
        out_shape=[jax.ShapeDtypeStruct((B, S, D_MODEL), BF16), jax.ShapeDtypeStruct((B, S, ATT_HEADS), F32),
                   jax.ShapeDtypeStruct((B, S, D_MODEL), BF16), jax.ShapeDtypeStruct((B, S, KV_W), BF16)],
        compiler_params=_params("parallel", "parallel"),
    )(z, z, z, cos, sin, cos, sin, sinks)


def _swa_bwd(z, qr, kr, cos, sin, sinks, lse, dob, dz, *, name):
    B, S, _ = z.shape
    nb = S // WINDOW
    cur, prev = _swa_specs(nb)

    def body(q_ref, krp_ref, krc_ref, kvp_ref, kvc_ref, cp_ref, sp_ref, cc_ref, sc_ref, sink_ref, lse_ref, do_ref, dz_in,
             dq_ref, dkc_ref, dkp_ref, dsink_ref):
        @pl.when((pl.program_id(0) == 0) & (pl.program_id(1) == 0))
        def _():
            dsink_ref[...] = jnp.zeros_like(dsink_ref)

        cos_c, sin_c, cos_p, sin_p = cc_ref[...], sc_ref[...], cp_ref[...], sp_ref[...]
        q = q_ref[...]
        k = jnp.concatenate([krp_ref[...], krc_ref[...]], axis=0).astype(F32)
        v = jnp.concatenate([kvp_ref[:, KV_W:], kvc_ref[:, KV_W:]], axis=0).astype(F32)
        low = lax.broadcasted_iota(jnp.int32, k.shape, 1) < ATT_HD
        kpad = _kv_padded(k, low)
        vpad = _kv_padded(v, low)
        mask = _swa_mask(pl.program_id(1) == 0)
        lse = lse_ref[...]
        dq_parts, dk_sum, dv_sum, dsinks = [], [], [], []
        for g in range(ATT_KV_HEADS):
            pairs = range(g * PAIRS_PER_KV, (g + 1) * PAIRS_PER_KV)
            keys = [(p, e) for p in pairs for e in (0, 1)]
            qp = {p: q[:, p * 128:(p + 1) * 128] for p in pairs}
            dop = {p: do_ref[:, p * 128:(p + 1) * 128] for p in pairs}
            s = {pe: jnp.where(mask, _nt(qp[pe[0]], kpad[g][pe[1]]), NEG_INF) for pe in keys}
            dp = {pe: _nt(dop[pe[0]], vpad[g][pe[1]]) for pe in keys}
            pr, ds = {}, {}
            for pe in keys:
                h = 2 * pe[0] + pe[1]
                lse_h = lse[:, h:h + 1]
                pf = jnp.exp(s[pe] - lse_h)
                delta = jnp.sum(pf * dp[pe], axis=1, keepdims=True)
                ds[pe] = (pf * (dp[pe] - delta)).astype(BF16)
                pr[pe] = pf.astype(BF16)
                p_sink = jnp.exp(sink_ref[0, h] - lse_h)
                dsinks.append(-jnp.sum(p_sink * delta, axis=0, keepdims=True))
            for p in pairs:
                dq_parts.append((_nn(ds[p, 0], kpad[g][0]) + _nn(ds[p, 1], kpad[g][1])) * ATT_SCALE)
            x = [sum(_tn(ds[p, e], qp[p]) for p in pairs) for e in (0, 1)]
            y = [sum(_tn(pr[p, e], dop[p]) for p in pairs) for e in (0, 1)]
            zk = jnp.where(low, x[0], x[1])
            zv = jnp.where(low, y[0], y[1])
            dk_sum.append(zk + _swap_halves(zk))
            dv_sum.append(zv + _swap_halves(zv))
        dq_ref[...] = _rope(jnp.concatenate(dq_parts, axis=1), cos_c, sin_c, inverse=True).astype(BF16)
        dk = jnp.where(low, dk_sum[0], dk_sum[1])
        dv = jnp.where(low, dv_sum[0], dv_sum[1])
        dkp_ref[:, :KV_W] = _rope(dk[:WINDOW], cos_p, sin_p, inverse=True)
        dkp_ref[:, KV_W:] = dv[:WINDOW]
        dkc_ref[:, :KV_W] = _rope(dk[WINDOW:], cos_c, sin_c, inverse=True)
        dkc_ref[:, KV_W:] = dv[WINDOW:]
        dsink_ref[...] += jnp.concatenate(dsinks, axis=1)

    tab = lambda im: pl.BlockSpec((None, WINDOW, 128), im)
    return pl.pallas_call(
        body, name=name, grid=(B, nb),
        in_specs=[pl.BlockSpec((None, WINDOW, D_MODEL), cur), tab(prev), tab(cur),
                  *_swa_z_specs()[1:],
                  tab(prev), tab(prev), tab(cur), tab(cur),
                  pl.BlockSpec(memory_space=pltpu.SMEM),
                  pl.BlockSpec((None, WINDOW, ATT_HEADS), cur),
                  pl.BlockSpec((None, WINDOW, D_MODEL), cur),
                  ANY],
        out_specs=[_swa_z_specs()[0],
                   pl.BlockSpec((None, WINDOW, 2 * KV_W), cur), pl.BlockSpec((None, WINDOW, 2 * KV_W), cur),
                   pl.BlockSpec((1, ATT_HEADS), lambda b, i: (0, 0))],
        out_shape=[jax.ShapeDtypeStruct(dz.shape, BF16),
                   jax.ShapeDtypeStruct((B, S, 2 * KV_W), F32), jax.ShapeDtypeStruct((B, S, 2 * KV_W), F32),
                   jax.ShapeDtypeStruct((1, ATT_HEADS), F32)],
        input_output_aliases={12: 0},
        compiler_params=_params("arbitrary", "arbitrary"),
    )(qr, kr, kr, z, z, cos, sin, cos, sin, sinks, lse, dob, dz)


def _swa_dkv_combine(dkv_cur, dkv_prev, dz, *, name):
    B, S, W = dkv_cur.shape

    def body(c_ref, p_ref, dz_in, o_ref):
        rows = lax.broadcasted_iota(jnp.int32, (S, W), 0)
        o_ref[...] = (c_ref[...] + _shift_up(p_ref[...], WINDOW, rows, S)).astype(BF16)

    seq = pl.BlockSpec((None, S, W), lambda b: (b, 0, 0))
    return pl.pallas_call(
        body, name=name, grid=(B,),
        in_specs=[seq, seq, ANY], out_specs=pl.BlockSpec((None, S, W), lambda b: (b, 0, O_AKV // W_AKV)),
        out_shape=jax.ShapeDtypeStruct(dz.shape, BF16),
        input_output_aliases={2: 0},
        compiler_params=_params("parallel"),
    )(dkv_cur, dkv_prev, dz)


def _rope_tables(positions):
    half = ROPE_DIM // 2
    inv = ROPE_THETA ** (-2.0 * jnp.arange(half, dtype=F32) / ROPE_DIM)
    ang = positions.astype(F32)[..., None] * inv
    c, s = jnp.cos(ang), jnp.sin(ang)
    pad = jnp.zeros(ang.shape[:-1] + (ATT_HD - ROPE_DIM,), F32)
    cos = jnp.concatenate([c, c, pad + 1.0], axis=-1)
    sin = jnp.concatenate([-s, s, pad], axis=-1)
    return jnp.tile(cos, (1, 1, 2)), jnp.tile(sin, (1, 1, 2))


def _lower_bound(lb_logits, *, name):
    def body(l_ref, o_ref):
        l = l_ref[...]
        e = jnp.exp(l - jnp.max(l, axis=0, keepdims=True))
        o_ref[...] = e[0:1] / jnp.sum(e, axis=0, keepdims=True)

    return pl.pallas_call(body, name=name, out_shape=jax.ShapeDtypeStruct((1, lb_logits.shape[1]), F32))(lb_logits)


W_ZH, W_GATES, W_AQ, W_AKV = 4 * HF, 2 * D_MODEL, ATT_HEADS * ATT_HD, 2 * KV_W
O_ZH, O_GATES, O_AQ, O_AKV = 0, W_ZH, W_ZH + W_GATES, W_ZH + W_GATES + W_AQ
W_IN = W_ZH + W_GATES + W_AQ + W_AKV


W_IN_BLK = W_IN // N_DEV


def _reference_row_block(j, rows=256):
    nz, ng = W_ZH // rows, W_GATES // rows
    return jnp.where(j < nz, j, jnp.where(j < nz + ng, j + (W_AQ + W_AKV) // rows, j - ng))


def _reordered_rows(w_t, *, name):
    rows = 256

    def body(i_ref, o_ref):
        o_ref[...] = i_ref[...]

    return pl.pallas_call(
        body, name=name, grid=(W_IN // rows,),
        in_specs=[pl.BlockSpec((rows, D_MODEL), lambda j: (_reference_row_block(j, rows), 0))],
        out_specs=pl.BlockSpec((rows, D_MODEL), lambda j: (j, 0)),
        out_shape=jax.ShapeDtypeStruct(w_t.shape, w_t.dtype), compiler_params=_params("parallel"))(w_t)


def _local_step(x, positions, target, small, w_in_t, rest_weights, emit, start_token):
    B, S, D = x.shape
    T = B * S
    x2 = x.reshape(T, D)
    cos, sin = _rope_tables(positions)
    lb = _lower_bound(small["lb_logits"], name="lb_fwd")
    zero = lambda tok: tok[0:1, 0:1]

    u1 = _norm_cast(x2, small["norm1_g"] + zero(start_token), name="norm1")
    z = _matmul(u1, w_in_t, tb=True, out_dtype=BF16, name="mm_z", tm=1024, tn=W_IN // 2)
    z3 = z.reshape(B, S, W_IN)
    oa, states = _hgrn_fwd(z3, lb, small["hgrn_norm_g"], name="hgrn_fwd")
    ob, lse, qr, kr = _swa_fwd(z3, cos, sin, small["attn_sinks"], name="swa_fwd")
    oa2 = oa.reshape(T, D)
    ob2 = ob.reshape(T, D)
    W = rest_weights("mix", ob)
    row = lambda tm, dtype=None: _row_spec(tm, D)
    tile = lambda dtype: jax.ShapeDtypeStruct((T, D), dtype)
    vec = _full_spec((1, D))
    vec_shape = jax.ShapeDtypeStruct((1, D), F32)

    def merge_ep(acc_a, acc_b, g_ref):
        pa, pb = acc_a.astype(BF16), acc_b.astype(BF16)
        return pa, pb, _merge_fn(g_ref[...], pa, pb)

    pa, pb, merged = _matmul_ep([(oa2, W["w_a"], False, 0), (ob2, W["w_b"], False, 0)], tm=1024, ins=[z], in_specs=[_gates_spec(1024)],
                                out_shapes=[tile(BF16)] * 3, out_specs=[row(1024)] * 3, epilogue=merge_ep, name="mm_pa_pb_merge")

    def resid_norm_ep(acc, x_ref, g_ref):
        hh = acc + x_ref[...]
        return hh, _rms(hh, g_ref[...])

    h, u2 = _matmul_ep([(merged, W["w_out"], False, 0)], tm=1024, ins=[x2, small["norm2_g"]], in_specs=[row(1024), vec],
                       out_shapes=[tile(F32), tile(BF16)], out_specs=[row(1024), row(1024)], epilogue=resid_norm_ep, name="mm_h_norm2")
    W.update(rest_weights("ffn", u2))
    gu = _matmul(u2, W["w_ffn_t"], tb=True, out_dtype=BF16, name="mm_gu", tm=1024, tn=D_FF)
    gu3 = gu.reshape(B, S, 2 * D_FF)
    act, a_pre = _conv_act_fwd(gu3, W["conv_w"], small["conv_b"], name="conv_act_fwd")
    act2 = act.reshape(T, D_FF)
    g = {}

    def loss_ep(acc, h_ref, g_ref, t_ref):
        y, vjp = jax.vjp(_rms, acc + h_ref[...], g_ref[...])
        err = y - t_ref[...]
        dx, dg = vjp(err * (1.0 / D))
        return dx, dx, dg, (0.5 / D) * jnp.sum(jnp.sum(err * err, axis=1, keepdims=True), axis=0, keepdims=True)

    dh2, dh2b, g["final_g"], loss = _matmul_ep(
        [(act2, W["w_down"], False, 0)], tm=512, ins=[h, small["final_g"].reshape(1, D), target.reshape(T, D)], in_specs=[row(512), vec, row(512)],
        out_shapes=[tile(F32), tile(BF16), vec_shape, jax.ShapeDtypeStruct((1, 1), F32)],
        out_specs=[row(512), row(512), vec, _full_spec((1, 1))], sums=(2, 3), epilogue=loss_ep, name="mm_h2_loss")
    dact = _matmul(dh2b, W["w_down"], tb=True, out_dtype=BF16, name="mm_dact", tm=1024, tn=D_FF)
    dw_down_t = _matmul(dh2b, act2, ta=True, out_dtype=BF16, name="mm_dw_down", tm=1024, tn=256, tk=8192)
    dg_, dup, g["conv_w"], g["conv_b"] = _conv_act_bwd(gu3, a_pre, W["conv_w"], dact.reshape(B, S, D_FF), name="conv_act_bwd")
    dg2 = dg_.reshape(T, D_FF)
    dup2 = dup.reshape(T, D_FF)
    dw_ffn_t = _matmul(u2, dg2, ta=True, out_t=True, out_dtype=BF16, into=lax.empty((2 * D_FF, D), BF16), o_noff=0, name="mm_dw_ffn_g", tm=1024, tn=256, tk=8192)
    dw_ffn_t = _matmul(u2, dup2, ta=True, out_t=True, out_dtype=BF16, into=dw_ffn_t, o_noff=D_FF // 256, name="mm_dw_ffn_u", tm=1024, tn=256, tk=8192)
    tok = emit("ffn", dict(w_ffn_t=dw_ffn_t, w_down=dw_down_t.T))
    def norm2_bwd_ep(acc_g, acc_u, h_ref, g_ref, dh2_ref):
        _, vjp = jax.vjp(_rms, h_ref[...], g_ref[...])
        dx, dg = vjp(acc_g + acc_u)
        dx = dx + dh2_ref[...]
        return dx, dx, dg

    dh, dhb, g["norm2_g"] = _matmul_ep(
        [(dg2, W["w_ffn_t"], False, 0), (dup2, W["w_ffn_t"], False, 1)], tm=512, ins=[h, small["norm2_g"] + zero(tok), dh2], in_specs=[row(512), vec, row(512)],
        out_shapes=[tile(F32), tile(BF16), vec_shape], out_specs=[row(512), row(512), vec], sums=(2,), epilogue=norm2_bwd_ep, name="mm_du2_norm2_bwd")
    dw_out = _matmul(merged, dhb, ta=True, out_dtype=BF16, name="mm_dw_out", tm=1024, tn=1024, tk=2048)

    def merge_bwd_ep(acc, g_ref, pa_ref, pb_ref, dz_in):
        gt = g_ref[...].astype(F32)
        sa = _sigmoid(gt[:, :D_MODEL])
        sb = _sigmoid(gt[:, D_MODEL:])
        dgates = jnp.concatenate([acc * pa_ref[...].astype(F32) * sa * (1.0 - sa), acc * pb_ref[...].astype(F32) * sb * (1.0 - sb)], axis=1)
        return dgates, acc * sa, acc * sb

    dz, dpa, dpb = _matmul_ep(
        [(dhb, W["w_out"], True, 0)], tm=512, ins=[z, pa, pb, lax.empty((T, W_IN), BF16)], in_specs=[_gates_spec(512), row(512), row(512), ANY],
        out_shapes=[jax.ShapeDtypeStruct((T, W_IN), BF16), tile(BF16), tile(BF16)], out_specs=[_gates_spec(512), row(512), row(512)],
        aliases={3: 0}, epilogue=merge_bwd_ep, name="mm_dmerged_merge_bwd")
    doa =_matmul(dpa, W["w_a"], tb=True, out_dtype=BF16, name="mm_doa", tm=1024, tn=1024)
    dw_a = _matmul(oa2, dpa, ta=True, out_dtype=BF16, name="mm_dw_a", tm=1024, tn=1024, tk=2048)
    dob = _matmul(dpb, W["w_b"], tb=True, out_dtype=BF16, name="mm_dob", tm=1024, tn=1024)
    dw_b = _matmul(ob2, dpb, ta=True, out_dtype=BF16, name="mm_dw_b", tm=1024, tn=1024, tk=2048)
    tok = emit("mix", dict(w_out=dw_out, w_a=dw_a, w_b=dw_b))
    dz3, dkv_cur, dkv_prev, dsinks = _swa_bwd(z3, qr, kr, cos, sin, small["attn_sinks"] + zero(tok), lse, dob.reshape(B, S, D),
                                              dz.reshape(B, S, W_IN), name="swa_bwd")
    dz3 = _swa_dkv_combine(dkv_cur, dkv_prev, dz3, name="swa_dkv")
    g["attn_sinks"] = dsinks
    dz3, g["lb"], g["hgrn_norm_g"] = _hgrn_bwd(z3, lb, small["hgrn_norm_g"], states, doa.reshape(B, S, D), dz3, name="hgrn_bwd")
    dz = dz3.reshape(T, W_IN)
    dw_in_t = _matmul(u1, dz, ta=True, out_t=True, o_block_perm=_reference_row_block, out_dtype=BF16, name="mm_dw_in", tm=1024, tn=256, tk=8192)
    tok = emit("in", dict(w_in_t=dw_in_t))
    du1 = _matmul(dz, w_in_t, after=tok, out_dtype=BF16, name="mm_du1", tm=1024, tn=512)
    dx, g["norm1_g"] = _norm_bwd_add(x2, small["norm1_g"], du1, dh, with_bf16=False, name="norm1_bwd")
    g["lb_logits"] = _lb_bwd(g.pop("lb"), lb, name="lb_bwd")
    return loss, dx.reshape(B, S, D), g


def _my_place():
    return lax.axis_index("x"), lax.axis_index("y"), lax.axis_index("c")


def _gather_blocks(x_ref, out_ref, send_sems, recv_sems, local_sem):
    x, y, c = _my_place()
    me, sibling = (x, y, c), (x, y, 1 - c)
    chips = [(1 - x, y), (x, 1 - y), (1 - x, 1 - y)]

    def slot(px, py, pc):
        return out_ref.at[4 * px + 2 * py + pc]

    def copy(k, block, to, src=None):
        return pltpu.make_async_remote_copy(
            src_ref=slot(*block) if src is None else src, dst_ref=slot(*block),
            send_sem=send_sems.at[k], recv_sem=recv_sems.at[k], device_id=to, device_id_type=MESH)

    mine = pltpu.make_async_copy(x_ref, slot(*me), local_sem)
    mine.start()
    first = [copy(0, me, sibling, src=x_ref)]
    first += [copy(1 + j, me, (*chip, c), src=x_ref) for j, chip in enumerate(chips)]
    for cp in first:
        cp.start()
    passed = [copy(4 + j, (*chip, c), sibling) for j, chip in enumerate(chips)]
    for j, chip in enumerate(chips):
        copy(1 + j, (*chip, c), me).wait_recv()
        passed[j].start()
    copy(0, sibling, me).wait_recv()
    for j, chip in enumerate(chips):
        copy(4 + j, (*chip, 1 - c), me).wait_recv()
    for cp in first + passed:
        cp.wait_send()
    mine.wait()


GATHER_SEMS = [pltpu.SemaphoreType.DMA((7,)), pltpu.SemaphoreType.DMA((7,)), pltpu.SemaphoreType.DMA]


def _all_gather(blk, *, name):
    return pl.pallas_call(
        _gather_body_fn(), name=name,
        out_shape=jax.ShapeDtypeStruct((N_DEV,) + blk.shape, blk.dtype),
        in_specs=[ANY], out_specs=ANY,
        scratch_shapes=GATHER_SEMS,
    )(blk)


def _gather_body_fn():
    def body(x_ref, out_ref, send_sems, recv_sems, local_sem):
        _gather_blocks(x_ref, out_ref, send_sems, recv_sems, local_sem)
    return body


SLAB_W = 1152
SMALL_SHAPES = dict(norm1_g=(1, D_MODEL), lb_logits=(2, HGRN_HEADS * HGRN_DK), hgrn_norm_g=(1, HGRN_DK), attn_sinks=(1, ATT_HEADS),
                    norm2_g=(1, D_MODEL), conv_b=(1, D_FF), final_g=(1, D_MODEL))
CONVW_BLK = D_FF // N_DEV
CONVW_STRIDE = SLAB_W // 3


def _slab_layout():
    layout, r = {}, 0
    for nm, (nr, w) in SMALL_SHAPES.items():
        layout[nm] = []
        for i in range(nr):
            for c0 in range(0, w, SLAB_W):
                layout[nm].append((r, i, c0, min(SLAB_W, w - c0)))
                r += 1
    return layout, r


SMALL_ROWS, _N_SMALL_ROWS = _slab_layout()
CONV_ROW0 = -(-_N_SMALL_ROWS // 8) * 8
LOSS_ROW = CONV_ROW0 + N_DEV
SLAB_ROWS = LOSS_ROW + 8


def _small_step(grads, g_conv_w, loss, params, moments, variances, dev, *, name):
    names = list(SMALL_ROWS)
    n = len(names)

    def body(dev_ref, *refs):
        g_refs = dict(zip(names, refs[:n]))
        gc_ref, loss_ref = refs[n], refs[n + 1]
        base = n + 2
        w_refs, m_refs, v_refs = (dict(zip(names + ["conv_w"], refs[base + i * (n + 1):base + (i + 1) * (n + 1)])) for i in range(3))
        o = base + 3 * (n + 1)
        gath_ref, loss_out = refs[o], refs[o + 1]
        outs = {nm: refs[o + 2 + 4 * i:o + 6 + 4 * i] for i, nm in enumerate(names + ["conv_w"])}
        slab, total, send_sems, recv_sems, local_sem = refs[-5:]

        slab[...] = jnp.zeros_like(slab)
        for nm, pieces in SMALL_ROWS.items():
            for r, i, c0, w in pieces:
                slab[r:r + 1, 0:w] = g_refs[nm][i:i + 1, c0:c0 + w]
        for p in range(N_DEV):
            for j in range(3):
                slab[CONV_ROW0 + p:CONV_ROW0 + p + 1, j * CONVW_STRIDE:j * CONVW_STRIDE + CONVW_BLK] = gc_ref[j:j + 1, p * CONVW_BLK:(p + 1) * CONVW_BLK]
        slab[LOSS_ROW:LOSS_ROW + 1, 0:1] = loss_ref[...]
        _gather_blocks(slab, gath_ref, send_sems, recv_sems, local_sem)
        acc = gath_ref[0]
        for p in range(1, N_DEV):
            acc = acc + gath_ref[p]
        total[...] = acc
        loss_out[...] = total[LOSS_ROW:LOSS_ROW + 1, 0:1]

        def update(nm, g, i, c0, w):
            at = (slice(i, i + 1), slice(c0, c0 + w))
            d, mn, vn = _adamw_math(w_refs[nm][at], g, m_refs[nm][at], v_refs[nm][at])
            for ref, val in zip(outs[nm], (g, d, mn, vn)):
                ref[at] = val

        for nm, pieces in SMALL_ROWS.items():
            for r, i, c0, w in pieces:
                update(nm, total[r:r + 1, 0:w], i, c0, w)
        conv_rows = total[CONV_ROW0:CONV_ROW0 + N_DEV, :]
        rowid = lax.broadcasted_iota(jnp.int32, conv_rows.shape, 0)
        mine = jnp.sum(jnp.where(rowid == dev_ref[0], conv_rows, 0.0), axis=0, keepdims=True)
        for j in range(3):
            update("conv_w", mine[:, j * CONVW_STRIDE:j * CONVW_STRIDE + CONVW_BLK], j, 0, CONVW_BLK)

    order = names + ["conv_w"]
    ins = [grads[nm] for nm in names] + [g_conv_w, loss]
    for d in (params, moments, variances):
        ins += [d[nm] for nm in order]
    vmem = pl.BlockSpec(memory_space=pltpu.VMEM)
    out_shape = [jax.ShapeDtypeStruct((N_DEV, SLAB_ROWS, SLAB_W), F32), jax.ShapeDtypeStruct((1, 1), F32)]
    for nm in order:
        out_shape += [jax.ShapeDtypeStruct(params[nm].shape, F32)] * 4
    res = pl.pallas_call(
        body, name=name,
        grid_spec=pltpu.PrefetchScalarGridSpec(
            num_scalar_prefetch=1, grid=(1,),
            in_specs=[vmem] * len(ins), out_specs=[vmem] * len(out_shape),
            scratch_shapes=[pltpu.VMEM((SLAB_ROWS, SLAB_W), F32), pltpu.VMEM((SLAB_ROWS, SLAB_W), F32)] + GATHER_SEMS),
        out_shape=out_shape,
    )(dev, *ins)
    return res[1], {nm: tuple(res[2 + 4 * i:6 + 4 * i]) for i, nm in enumerate(order)}


HBM_SPEC = pl.BlockSpec(memory_space=pltpu.HBM)
SEM_SPEC = pl.BlockSpec(memory_space=pltpu.SEMAPHORE)
DATAFLOW_EFFECT = pltpu.SideEffectType.DATAFLOW_SIDE_EFFECTING
N_PEERS = N_DEV - 1


def _peers(x, y, c):
    return [(1 - x if r & 4 else x, 1 - y if r & 2 else y, 1 - c if r & 1 else c) for r in range(1, N_DEV)]


def _exchange_start(srcs, scatter, *, after=None, name):
    n = len(srcs)
    lands = [lax.empty(a.shape if scatter else (N_DEV,) + a.shape, a.dtype) for a in srcs]
    extra = [] if after is None else [after]

    def body(*refs):
        src_refs, land_refs = refs[:n], refs[n:2 * n]
        send_sems, recv_sems, token = refs[2 * n + len(extra)], refs[2 * n + len(extra) + 1], refs[-1]
        x, y, c = _my_place()
        me = 4 * x + 2 * y + c
        for i in range(n):
            for r, (tx, ty, tc) in enumerate(_peers(x, y, c)):
                src = src_refs[i].at[4 * tx + 2 * ty + tc] if scatter else src_refs[i]
                pltpu.make_async_remote_copy(
                    src_ref=src, dst_ref=land_refs[i].at[me], send_sem=send_sems.at[N_PEERS * i + r],
                    recv_sem=recv_sems.at[N_PEERS * i + r], device_id=(tx, ty, tc), device_id_type=MESH).start()
        token[...] = jnp.zeros_like(token)

    thru = [pltpu.HBM(a.shape, a.dtype) for a in list(srcs) + lands]
    res = pl.pallas_call(
        body, name=name,
        out_shape=(pltpu.SemaphoreType.DMA((N_PEERS * n,)), pltpu.SemaphoreType.DMA((N_PEERS * n,)), *thru,
                   jax.ShapeDtypeStruct((8, 128), F32)),
        in_specs=[HBM_SPEC] * (2 * n) + [ANY] * len(extra),
        out_specs=(SEM_SPEC, SEM_SPEC, *([HBM_SPEC] * (2 * n)), pl.BlockSpec(memory_space=pltpu.VMEM)),
        input_output_aliases={i: 2 + i for i in range(2 * n)},
        compiler_params=pltpu.CompilerParams(has_side_effects=DATAFLOW_EFFECT),
    )(*[pltpu.with_memory_space_constraint(a, pltpu.HBM) for a in list(srcs) + lands], *extra)
    return (res[0], res[1], list(res[2:2 + n]), list(res[2 + n:2 + 2 * n]), scatter), res[-1]


def _exchange_wait(handle, after, *, name):
    send_sems, recv_sems, srcs, lands, scatter = handle
    n = len(srcs)

    def body(*refs):
        src_refs, land_refs = refs[:n], refs[n:2 * n]
        send_sems, recv_sems = refs[2 * n], refs[2 * n + 1]
        x, y, c = _my_place()
        for i in range(n):
            for r in range(N_PEERS):
                src = src_refs[i].at[0] if scatter else src_refs[i]
                cp = pltpu.make_async_remote_copy(
                    src_ref=src, dst_ref=land_refs[i].at[0], send_sem=send_sems.at[N_PEERS * i + r],
                    recv_sem=recv_sems.at[N_PEERS * i + r], device_id=(x, y, c), device_id_type=MESH)
                cp.wait_send()
                cp.wait_recv()

    thru = [pltpu.HBM(a.shape, a.dtype) for a in srcs + lands]
    res = pl.pallas_call(
        body, name=name, out_shape=tuple(thru),
        in_specs=[HBM_SPEC] * (2 * n) + [SEM_SPEC, SEM_SPEC, ANY], out_specs=tuple([HBM_SPEC] * (2 * n)),
        input_output_aliases={i: i for i in range(2 * n)},
        compiler_params=pltpu.CompilerParams(has_side_effects=DATAFLOW_EFFECT),
    )(*srcs, *lands, send_sems, recv_sems, after)
    return list(res[:n]), list(res[n:])


def _with_own(land, own, me):
    return lax.dynamic_update_index_in_dim(land, own, me, 0)


def _adamw_math(w, g, m, v):
    m = ADAM_B1 * m + (1.0 - ADAM_B1) * g
    v = ADAM_B2 * v + (1.0 - ADAM_B2) * (g * g)
    m_hat = m / (1.0 - ADAM_B1 ** ADAM_STEP)
    v_hat = v / (1.0 - ADAM_B2 ** ADAM_STEP)
    delta = -ADAM_LR * (m_hat / (jnp.sqrt(v_hat) + ADAM_EPS) + ADAM_WD * w)
    return delta, m, v


def _adamw_sum(parts, w, m, v, *, name):
    shape = w.shape
    R, n = shape[-2], shape[-1]
    w, m, v = (t.reshape(R, n) for t in (w, m, v))
    tr = _pick(R, (256, 464, 352, 128))

    def body(p_ref, w_ref, m_ref, v_ref, g_ref, d_ref, mo_ref, vo_ref):
        g = p_ref[0].astype(F32)
        for p in range(1, N_DEV):
            g = g + p_ref[p].astype(F32)
        d, mn, vn = _adamw_math(w_ref[...], g, m_ref[...], v_ref[...])
        g_ref[...] = g
        d_ref[...] = d
        mo_ref[...] = mn
        vo_ref[...] = vn

    row = pl.BlockSpec((tr, n), lambda i: (i, 0))
    outs = pl.pallas_call(
        body, name=name, grid=(R // tr,),
        in_specs=[pl.BlockSpec((N_DEV, tr, n), lambda i: (0, i, 0)), row, row, row],
        out_specs=[row, row, row, row],
        out_shape=[jax.ShapeDtypeStruct((R, n), F32)] * 4,
        compiler_params=_params("parallel"),
    )(parts, w, m, v)
    return [t.reshape(shape) for t in outs]


def _lb_bwd(dlb, lb, *, name):
    def body(d_ref, lb_ref, o_ref):
        t = d_ref[...] * lb_ref[...] * (1.0 - lb_ref[...])
        o_ref[0:1, :] = t
        o_ref[1:2, :] = -t

    return pl.pallas_call(body, name=name, out_shape=jax.ShapeDtypeStruct((2, lb.shape[1]), F32))(dlb, lb)


DOWN_BLK, ROW_BLK = D_FF // N_DEV, D_MODEL // N_DEV
W_FFN_BLK = 2 * D_FF // N_DEV
CONV_BITS_SHAPE = (16, 256)


def kernel(x, positions, norm1_g, w_in, lb_logits, hgrn_norm_g, w_a, attn_sinks, w_b, w_out, norm2_g, w_ffn_in, conv_w, conv_b, w_down, final_g, loss_target, m_norm1_g, m_w_in, m_lb_logits, m_hgrn_norm_g, m_w_a, m_attn_sinks, m_w_b, m_w_out, m_norm2_g, m_w_ffn_in, m_conv_w, m_conv_b, m_w_down, m_final_g, v_norm1_g, v_w_in, v_lb_logits, v_hgrn_norm_g, v_w_a, v_attn_sinks, v_w_b, v_w_out, v_norm2_g, v_w_ffn_in, v_conv_w, v_conv_b, v_w_down, v_final_g):
    xi, yi, ci = _my_place()
    dev = 4 * xi + 2 * yi + ci

    tr = lambda t: jnp.transpose(t[0])
    untr = lambda t: jnp.transpose(t)[None]
    w_in_blocks = _all_gather(tr(w_in).astype(BF16), name="ag_w_in")
    conv_bits = lax.bitcast_convert_type(conv_w, BF16).reshape(-1)
    conv_bits = jnp.pad(conv_bits, (0, CONV_BITS_SHAPE[0] * CONV_BITS_SHAPE[1] - conv_bits.shape[0])).reshape(CONV_BITS_SHAPE)
    w_in_full_t = _reordered_rows(w_in_blocks.reshape(W_IN, D_MODEL), name="w_in_rows")
    gather_handles = {}
    gather_handles["mix"], tok_mix = _exchange_start([w_a[0].astype(BF16), w_b[0].astype(BF16), w_out[0].astype(BF16)], False,
                                                     after=w_in_full_t, name="ag_mix_start")
    gather_handles["ffn"], tok_ffn = _exchange_start([tr(w_ffn_in).astype(BF16), w_down[0].astype(BF16), conv_bits], False,
                                                     after=tok_mix, name="ag_ffn_start")
    start_token = tok_mix + tok_ffn

    def rest_weights(group, after):
        own, lands = _exchange_wait(gather_handles[group], after, name="ag_" + group + "_wait")
        full = [_with_own(l, o, dev) for l, o in zip(lands, own)]
        if group == "mix":
            return dict(zip(("w_a", "w_b", "w_out"), [t.reshape(D_MODEL, D_MODEL) for t in full]))
        bits = full[2].reshape(N_DEV, -1)[:, :3 * CONVW_BLK * 2].reshape(N_DEV, 3, CONVW_BLK, 2)
        return dict(w_ffn_t=full[0].reshape(2 * D_FF, D_MODEL), w_down=full[1].reshape(D_FF, D_MODEL),
                    conv_w=lax.bitcast_convert_type(bits, F32).transpose(1, 0, 2).reshape(3, D_FF))

    handles = {}

    def emit(group, gr):
        if group == "ffn":
            srcs = [gr["w_ffn_t"].reshape(N_DEV, W_FFN_BLK, D_MODEL), gr["w_down"].reshape(N_DEV, DOWN_BLK, D_MODEL)]
        elif group == "mix":
            srcs = [gr[n].reshape(N_DEV, ROW_BLK, D_MODEL) for n in ("w_out", "w_a", "w_b")]
        else:
            srcs = [gr["w_in_t"].reshape(N_DEV, W_IN_BLK, D_MODEL)]
        handles[group], token = _exchange_start(srcs, True, name="rs_" + group + "_start")
        return token

    small = dict(norm1_g=norm1_g, lb_logits=lb_logits, hgrn_norm_g=hgrn_norm_g, attn_sinks=attn_sinks, norm2_g=norm2_g,
                 conv_b=conv_b, final_g=final_g)
    loss, grad_x, g = _local_step(x, positions, loss_target, small, w_in_full_t, rest_weights, emit, start_token)

    def parts_of(group, after):
        srcs, lands = _exchange_wait(handles[group], after, name="rs_" + group + "_wait")
        return [_with_own(l, lax.dynamic_index_in_dim(s, dev, 0, keepdims=False), dev) for s, l in zip(srcs, lands)]

    p_ffn, p_down = parts_of("ffn", grad_x)
    p_out, p_a, p_b = parts_of("mix", grad_x)
    (p_in,) = parts_of("in", grad_x)
    big = dict(
        w_in=[untr(t) for t in _adamw_sum(p_in, tr(w_in), tr(m_w_in), tr(v_w_in), name="adamw_w_in")],
        w_a=_adamw_sum(p_a, w_a, m_w_a, v_w_a, name="adamw_w_a"),
        w_b=_adamw_sum(p_b, w_b, m_w_b, v_w_b, name="adamw_w_b"),
        w_out=_adamw_sum(p_out, w_out, m_w_out, v_w_out, name="adamw_w_out"),
        w_ffn_in=[untr(t) for t in _adamw_sum(p_ffn, tr(w_ffn_in), tr(m_w_ffn_in), tr(v_w_ffn_in), name="adamw_w_ffn_in")],
        w_down=_adamw_sum(p_down, w_down, m_w_down, v_w_down, name="adamw_w_down"),
    )

    row = lambda t: t.reshape(1, -1) if t.ndim == 1 else t
    shard = lambda t: t.reshape(3, CONVW_BLK)
    sm_g = {nm: g[nm] for nm in SMALL_ROWS}
    sm_w = dict(norm1_g=norm1_g, lb_logits=lb_logits, hgrn_norm_g=hgrn_norm_g, attn_sinks=attn_sinks, norm2_g=norm2_g,
                conv_b=conv_b, final_g=row(final_g), conv_w=shard(conv_w))
    sm_m = dict(norm1_g=m_norm1_g, lb_logits=m_lb_logits, hgrn_norm_g=m_hgrn_norm_g, attn_sinks=m_attn_sinks, norm2_g=m_norm2_g,
                conv_b=m_conv_b, final_g=row(m_final_g), conv_w=shard(m_conv_w))
    sm_v = dict(norm1_g=v_norm1_g, lb_logits=v_lb_logits, hgrn_norm_g=v_hgrn_norm_g, attn_sinks=v_attn_sinks, norm2_g=v_norm2_g,
                conv_b=v_conv_b, final_g=row(v_final_g), conv_w=shard(v_conv_w))
    loss_total, sm_out = _small_step(sm_g, g["conv_w"], loss, sm_w, sm_m, sm_v, dev.astype(jnp.int32).reshape(1), name="small_step")
    shapes = dict(final_g=final_g.shape, conv_w=conv_w.shape)

    names = ("norm1_g", "w_in", "lb_logits", "hgrn_norm_g", "w_a", "attn_sinks", "w_b", "w_out", "norm2_g", "w_ffn_in", "conv_w", "conv_b", "w_down", "final_g")
    outs = [loss_total.reshape(()), grad_x]
    for kind in range(4):
        outs += [big[n][kind] if n in big else sm_out[n][kind].reshape(shapes.get(n, sm_out[n][kind].shape)) for n in names]
    return tuple(outs)
```

```python
import jax
import jax.numpy as jnp
from jax import lax
from jax.experimental import pallas as pl
from jax.experimental.pallas import tpu as pltpu

F32 = jnp.float32
BF16 = jnp.bfloat16

D_MODEL = 1024
HGRN_HEADS = 8
HGRN_DK = 128
CHUNK = 64
ATT_HEADS = 16
ATT_KV_HEADS = 2
ATT_HD = 64
ATT_GROUP = ATT_HEADS // ATT_KV_HEADS
WINDOW = 128
ROPE_DIM = ATT_HD // 4
ROPE_THETA = 500000.0
D_FF = 2816
EPS = 1e-6
NEG_INF = -1e30
N_DEV = 8

ADAM_LR = 0.001
ADAM_B1 = 0.9
ADAM_B2 = 0.999
ADAM_EPS = 1e-08
ADAM_WD = 0.01
ADAM_STEP = 10

MESH = pl.DeviceIdType.MESH
ANY = pl.BlockSpec(memory_space=pl.ANY)


def _pick(n, cands):
    for c in cands:
        if n % c == 0:
            return c
    return n


def _sigmoid(x):
    return 0.5 * jnp.tanh(0.5 * x) + 0.5


def _silu(x):
    hx = 0.5 * x
    return hx * jnp.tanh(hx) + hx


def _rms(x, g):
    return x * lax.rsqrt(jnp.mean(x * x, axis=-1, keepdims=True) + EPS) * g


def _dot(a, b, dims):
    return lax.dot_general(a, b, (dims, ((), ())), preferred_element_type=F32)


def _nn(a, b):
    return _dot(a, b, ((1,), (0,)))


def _nt(a, b):
    return _dot(a, b, ((1,), (1,)))


def _tn(a, b):
    return _dot(a, b, ((0,), (0,)))


def _params(*sem):
    return pltpu.CompilerParams(dimension_semantics=sem, vmem_limit_bytes=56 * 1024 * 1024)


def _matmul(a, b, *, ta=False, tb=False, out_dtype=F32, addend=None, after=None, into=None, o_noff=0, out_t=False,
            o_block_perm=lambda j: j, name, tm, tn, tk=None, n_extent=None, b_koff=0, b_noff=0):
    M, K = (a.shape[1], a.shape[0]) if ta else a.shape
    N = n_extent or (b.shape[0] if tb else b.shape[1])
    tm, tn, tk = min(tm, M), min(tn, N), min(tk or K, K)
    assert M % tm == 0 and N % tn == 0 and K % tk == 0, (name, M, N, K, tm, tn, tk)
    nk = K // tk
    use_scratch = nk > 1 and out_dtype != F32
    grid = (M // tm, N // tn, nk)
    a_spec = pl.BlockSpec((tk, tm), lambda i, j, k: (k, i)) if ta else pl.BlockSpec((tm, tk), lambda i, j, k: (i, k))
    b_spec = pl.BlockSpec((tn, tk), lambda i, j, k: (j + b_noff, k + b_koff)) if tb else pl.BlockSpec((tk, tn), lambda i, j, k: (k + b_koff, j + b_noff))
    o_spec = pl.BlockSpec((tm, tn), lambda i, j, k: (i, j))
    dims = ((0 if ta else 1,), (1 if tb else 0,))
    has_add = addend is not None

    n_in = 2 + has_add + (after is not None) + (into is not None)

    def body(*refs):
        a_ref, b_ref = refs[:2]
        c_ref = refs[2] if has_add else None
        o_ref = refs[n_in]
        part = _dot(a_ref[...], b_ref[...], dims)
        if nk == 1:
            if has_add:
                part = part + c_ref[...].astype(F32)
            o_ref[...] = (part.T if out_t else part).astype(out_dtype)
        else:
            acc_ref = refs[-1] if use_scratch else o_ref
            k = pl.program_id(2)

            @pl.when(k == 0)
            def _():
                acc_ref[...] = part + c_ref[...].astype(F32) if has_add else part

            @pl.when(k > 0)
            def _():
                acc_ref[...] += part

            if use_scratch:
                @pl.when(k == nk - 1)
                def _():
                    o_ref[...] = acc_ref[...].astype(out_dtype)

    in_specs = [a_spec, b_spec] + ([o_spec] if has_add else [])
    args = (a, b) + ((addend,) if has_add else ())
    if after is not None:
        in_specs.append(pl.BlockSpec(after.shape, lambda i, j, k: (0, 0)))
        args += (after,)
    aliases = {}
    if into is not None:
        in_specs.append(ANY)
        args += (into,)
        aliases = {len(args) - 1: 0}
    if out_t:
        assert nk == 1 and not has_add
        o_spec = pl.BlockSpec((tn, tm), lambda i, j, k: (o_block_perm(j) + o_noff, i))
    elif into is not None:
        o_spec = pl.BlockSpec((tm, tn), lambda i, j, k: (i, j + o_noff))
    return pl.pallas_call(
        body,
        name=name,
        grid=grid,
        in_specs=in_specs,
        out_specs=o_spec,
        out_shape=jax.ShapeDtypeStruct(into.shape if into is not None else ((N, M) if out_t else (M, N)), out_dtype),
        input_output_aliases=aliases,
        scratch_shapes=[pltpu.VMEM((tm, tn), F32)] if use_scratch else [],
        compiler_params=_params("parallel", "parallel", "arbitrary"),
    )(*args)


def _matmul_ep(pairs, *, tm, ins, in_specs, out_shapes, out_specs, sums=(), epilogue, aliases=None, name):
    M = pairs[0][0].shape[0]
    tm = min(tm, M)
    mm_specs, mm_args, dims = [], [], []
    for a, b, tb, koff in pairs:
        K = a.shape[1]
        N = b.shape[0] if tb else b.shape[1]
        mm_specs += [pl.BlockSpec((tm, K), lambda i: (i, 0)),
                     pl.BlockSpec((N, K), lambda i, koff=koff: (0, koff)) if tb else pl.BlockSpec((K, N), lambda i, koff=koff: (koff, 0))]
        mm_args += [a, b]
        dims.append(((1,), (1 if tb else 0,)))
    n_mm = len(mm_args)
    n_in = n_mm + len(ins)

    def body(*refs):
        in_refs, out_refs = refs[n_mm:n_in], refs[n_in:]
        accs = [_dot(refs[2 * p][...], refs[2 * p + 1][...], dims[p]) for p in range(len(pairs))]
        outs = epilogue(*accs, *in_refs)
        for k, (ref, val) in enumerate(zip(out_refs, outs)):
            if val is None:
                continue
            if k in sums:
                @pl.when(pl.program_id(0) == 0)
                def _():
                    ref[...] = jnp.zeros_like(ref)

                ref[...] += val
            else:
                ref[...] = val.astype(ref.dtype)

    return pl.pallas_call(
        body, name=name, grid=(M // tm,),
        in_specs=mm_specs + list(in_specs),
        out_specs=list(out_specs), out_shape=list(out_shapes),
        input_output_aliases={n_mm + k: v for k, v in (aliases or {}).items()},
        compiler_params=_params("arbitrary"),
    )(*mm_args, *ins)


def _row_spec(tm, n):
    return pl.BlockSpec((tm, n), lambda i: (i, 0))


def _full_spec(shape):
    return pl.BlockSpec(shape, lambda i: tuple(0 for _ in shape))


def _norm_cast(x, g, *, name):
    T, D = x.shape
    tm = _pick(T, (512, 256, 128))

    def body(x_ref, g_ref, u_ref):
        u_ref[...] = _rms(x_ref[...], g_ref[...]).astype(BF16)

    return pl.pallas_call(
        body, name=name, grid=(T // tm,),
        in_specs=[_row_spec(tm, D), _full_spec((1, D))],
        out_specs=_row_spec(tm, D),
        out_shape=jax.ShapeDtypeStruct((T, D), BF16),
        compiler_params=_params("parallel"),
    )(x, g)


def _norm_bwd_add(x, g, du, dres, *, with_bf16=True, name):
    T, D = x.shape
    tm = _pick(T, (512, 256, 128))

    def body(x_ref, g_ref, du_ref, dr_ref, dx_ref, *rest):
        dg_ref = rest[-1]
        _, vjp = jax.vjp(_rms, x_ref[...], g_ref[...])
        dx, dg = vjp(du_ref[...].astype(F32))
        dx = dx + dr_ref[...]
        dx_ref[...] = dx
        if with_bf16:
            rest[0][...] = dx.astype(BF16)

        @pl.when(pl.program_id(0) == 0)
        def _():
            dg_ref[...] = jnp.zeros_like(dg_ref)

        dg_ref[...] += dg

    row = _row_spec(tm, D)
    return pl.pallas_call(
        body, name=name, grid=(T // tm,),
        in_specs=[row, _full_spec((1, D)), row, row],
        out_specs=[row] + ([row] if with_bf16 else []) + [_full_spec((1, D))],
        out_shape=[jax.ShapeDtypeStruct((T, D), F32)] + ([jax.ShapeDtypeStruct((T, D), BF16)] if with_bf16 else []) + [jax.ShapeDtypeStruct((1, D), F32)],
        compiler_params=_params("arbitrary"),
    )(x, g, du, dres)


def _merge_fn(gates, a, b):
    ga = gates[:, :D_MODEL].astype(F32)
    gb = gates[:, D_MODEL:].astype(F32)
    return _sigmoid(ga) * a.astype(F32) + _sigmoid(gb) * b.astype(F32)


def _gates_spec(tm):
    return pl.BlockSpec((tm, W_GATES), lambda i: (i, O_GATES // W_GATES))


CONV_TC = 256


def _shift_down(x, n, rows):
    return jnp.where(rows >= n, pltpu.roll(x, n, 0), 0.0)


def _shift_up(x, n, rows, S):
    return jnp.where(rows < S - n, pltpu.roll(x, S - n, 0), 0.0)


def _conv_act_fwd(gu, conv_w, conv_b, *, name):
    B, S, _ = gu.shape
    tc = CONV_TC
    nc = D_FF // tc

    def body(g_ref, up_ref, w_ref, b_ref, o_ref, a_ref):
        g = g_ref[...].astype(F32)
        rows = lax.broadcasted_iota(jnp.int32, g.shape, 0)
        w = w_ref[...]
        a = w[2:3] * g + w[1:2] * _shift_down(g, 1, rows) + w[0:1] * _shift_down(g, 2, rows) + b_ref[...]
        o_ref[...] = (_silu(a) * up_ref[...].astype(F32)).astype(BF16)
        a_ref[...] = a.astype(BF16)

    col = pl.BlockSpec((None, S, tc), lambda b, j: (b, 0, j))
    return pl.pallas_call(
        body, name=name, grid=(B, nc),
        in_specs=[col,
                  pl.BlockSpec((None, S, tc), lambda b, j: (b, 0, j + nc)),
                  pl.BlockSpec((3, tc), lambda b, j: (0, j)),
                  pl.BlockSpec((1, tc), lambda b, j: (0, j))],
        out_specs=[col, col],
        out_shape=[jax.ShapeDtypeStruct((B, S, D_FF), BF16)] * 2,
        compiler_params=_params("parallel", "parallel"),
    )(gu, gu, conv_w, conv_b)


def _conv_act_bwd(gu, a_pre, conv_w, dact, *, name):
    B, S, _ = gu.shape
    tc = CONV_TC
    nc = D_FF // tc

    def body(g_ref, up_ref, a_ref, w_ref, da_ref, dg_ref, dup_ref, dw_ref, db_ref):
        g = g_ref[...].astype(F32)
        up, a, dact = up_ref[...], a_ref[...], da_ref[...]
        rows = lax.broadcasted_iota(jnp.int32, g.shape, 0)
        w = w_ref[...]
        sg = _sigmoid(a)
        dup_ref[...] = dact * a * sg
        da = (dact * up * sg * (1.0 + a * (1.0 - sg))).astype(F32)
        da1 = _shift_up(da, 1, rows, S)
        da2 = _shift_up(da, 2, rows, S)
        dg_ref[...] = (w[2:3] * da + w[1:2] * da1 + w[0:1] * da2).astype(BF16)

        @pl.when(pl.program_id(1) == 0)
        def _():
            dw_ref[...] = jnp.zeros_like(dw_ref)
            db_ref[...] = jnp.zeros_like(db_ref)

        dw_ref[0:1, :] += jnp.sum(da2 * g, axis=0, keepdims=True)
        dw_ref[1:2, :] += jnp.sum(da1 * g, axis=0, keepdims=True)
        dw_ref[2:3, :] += jnp.sum(da * g, axis=0, keepdims=True)
        db_ref[...] += jnp.sum(da, axis=0, keepdims=True)

    col = pl.BlockSpec((None, S, tc), lambda j, b: (b, 0, j))
    return pl.pallas_call(
        body, name=name, grid=(nc, B),
        in_specs=[col,
                  pl.BlockSpec((None, S, tc), lambda j, b: (b, 0, j + nc)),
                  col,
                  pl.BlockSpec((3, tc), lambda j, b: (0, j)),
                  col],
        out_specs=[col, col, pl.BlockSpec((3, tc), lambda j, b: (0, j)), pl.BlockSpec((1, tc), lambda j, b: (0, j))],
        out_shape=[jax.ShapeDtypeStruct((B, S, D_FF), BF16), jax.ShapeDtypeStruct((B, S, D_FF), BF16),
                   jax.ShapeDtypeStruct((3, D_FF), F32), jax.ShapeDtypeStruct((1, D_FF), F32)],
        compiler_params=_params("parallel", "arbitrary"),
    )(gu, gu, a_pre, conv_w, dact)


HGRN_CPB = 8
HF = HGRN_HEADS * HGRN_DK


def _tri(n, upper=False):
    r = lax.broadcasted_iota(jnp.int32, (n, n), 0)
    c = lax.broadcasted_iota(jnp.int32, (n, n), 1)
    return (c >= r) if upper else (r >= c)


def _hs(h):
    return slice(h * HGRN_DK, (h + 1) * HGRN_DK)


def _cumsum_rows(tri_b, x):
    hi = x.astype(BF16)
    lo = (x - hi.astype(F32)).astype(BF16)
    return _nn(tri_b, hi) + _nn(tri_b, lo)


def _hgrn_pre(q, fz, lb, tril_b):
    qf = _silu(q)
    sg = _sigmoid(fz)
    f = lb + (1.0 - lb) * sg
    k = 1.0 - f
    b = _cumsum_rows(tril_b, jnp.log2(f))
    bref = b[CHUNK // 2:CHUNK // 2 + 1, :]
    blast = b[CHUNK - 1:CHUNK, :]
    e1 = jnp.exp2(b - bref)
    e2 = jnp.exp2(bref - b)
    e3 = e1 * jnp.exp2(bref)
    e4 = e2 * jnp.exp2(blast - bref)
    dec = jnp.exp2(blast)
    return sg, f, (e1, e2, e3, e4), qf * e1, k * e2, qf * e3, k * e4, dec


def _hgrn_fwd(zh, lb, gn, *, name):
    B, S, _ = zh.shape
    cpb = HGRN_CPB
    ts = cpb * CHUNK
    nblk = S // ts

    def body(z_ref, lb_ref, gn_ref, o_ref, st_ref, state):
        @pl.when(pl.program_id(1) == 0)
        def _():
            state[...] = jnp.zeros_like(state)

        H = HGRN_HEADS
        causal = _tri(CHUNK)
        tril_b = causal.astype(BF16)
        lb = lb_ref[...]
        for c in range(cpb):
            rows = slice(c * CHUNK, (c + 1) * CHUNK)
            q = z_ref[rows, 0:HF].astype(F32)
            fz = z_ref[rows, HF:2 * HF].astype(F32)
            v = z_ref[rows, 2 * HF:3 * HF]
            hg = z_ref[rows, 3 * HF:4 * HF].astype(F32)
            _, _, _, q_in, k_in, q_out, k_st, dec = _hgrn_pre(q, fz, lb, tril_b)
            q_in, k_in, q_out, k_st = (t.astype(BF16) for t in (q_in, k_in, q_out, k_st))
            a = [jnp.where(causal, _nt(q_in[:, _hs(h)], k_in[:, _hs(h)]), 0.0).astype(BF16) for h in range(H)]
            st = [state[h] for h in range(H)]
            for h in range(H):
                st_ref[c, h] = st[h]
            o = [_nn(a[h], v[:, _hs(h)]) + _nt(q_out[:, _hs(h)], st[h].astype(BF16)) for h in range(H)]
            for h in range(H):
                state[h] = st[h] * dec[:, _hs(h)] + _tn(v[:, _hs(h)], k_st[:, _hs(h)])
            gate = _silu(hg)
            for h in range(H):
                o_ref[rows, _hs(h)] = (_rms(o[h], gn_ref[...]) * gate[:, _hs(h)]).astype(BF16)

    return pl.pallas_call(
        body, name=name, grid=(B, nblk),
        in_specs=[pl.BlockSpec((None, ts, 4 * HF), lambda b, s: (b, s, 0)),
                  pl.BlockSpec((1, HF), lambda b, s: (0, 0)),
                  pl.BlockSpec((1, HGRN_DK), lambda b, s: (0, 0))],
        out_specs=[pl.BlockSpec((None, ts, HF), lambda b, s: (b, s, 0)),
                   pl.BlockSpec((None, cpb, HGRN_HEADS, HGRN_DK, HGRN_DK), lambda b, s: (b, s, 0, 0, 0))],
        out_shape=[jax.ShapeDtypeStruct((B, S, HF), BF16),
                   jax.ShapeDtypeStruct((B, S // CHUNK, HGRN_HEADS, HGRN_DK, HGRN_DK), F32)],
        scratch_shapes=[pltpu.VMEM((HGRN_HEADS, HGRN_DK, HGRN_DK), F32)],
        compiler_params=_params("arbitrary", "arbitrary"),
    )(zh, lb, gn)


def _hgrn_bwd(zh, lb, gn, states, doa, dz, *, name):
    B, S, _ = zh.shape
    cpb = HGRN_CPB
    ts = cpb * CHUNK
    nblk = S // ts
    rev = lambda b, s: (b, nblk - 1 - s, 0)

    def body(z_ref, lb_ref, gn_ref, st_ref, do_ref, dz_in, dz_ref, dlb_ref, dgn_ref, dstate):
        @pl.when(pl.program_id(1) == 0)
        def _():
            dstate[...] = jnp.zeros_like(dstate)

        @pl.when((pl.program_id(0) == 0) & (pl.program_id(1) == 0))
        def _():
            dlb_ref[...] = jnp.zeros_like(dlb_ref)
            dgn_ref[...] = jnp.zeros_like(dgn_ref)

        H = HGRN_HEADS
        cat = lambda xs: jnp.concatenate(xs, axis=1)
        causal = _tri(CHUNK)
        tril_b = causal.astype(BF16)
        triu_b = _tri(CHUNK, upper=True).astype(BF16)
        rowid = lax.broadcasted_iota(jnp.int32, (CHUNK, HF), 0)
        lb = lb_ref[...]
        gn = gn_ref[...]
        for c in reversed(range(cpb)):
            rows = slice(c * CHUNK, (c + 1) * CHUNK)
            q = z_ref[rows, 0:HF].astype(F32)
            fz = z_ref[rows, HF:2 * HF].astype(F32)
            v = z_ref[rows, 2 * HF:3 * HF]
            hg = z_ref[rows, 3 * HF:4 * HF].astype(F32)
            sg, f, (e1, e2, e3, e4), q_in, k_in, q_out, k_st, dec = _hgrn_pre(q, fz, lb, tril_b)
            q_in_b, k_in_b, q_out_b, k_st_b = (t.astype(BF16) for t in (q_in, k_in, q_out, k_st))
            a_b = [jnp.where(causal, _nt(q_in_b[:, _hs(h)], k_in_b[:, _hs(h)]), 0.0).astype(BF16) for h in range(H)]
            st = [st_ref[c, h] for h in range(H)]
            st_b = [t.astype(BF16) for t in st]
            o = [_nn(a_b[h], v[:, _hs(h)]) + _nt(q_out_b[:, _hs(h)], st_b[h]) for h in range(H)]
            dout = do_ref[rows, :].astype(F32)
            shg = _sigmoid(hg)
            gate = hg * shg
            do_l, dgn_acc = [], jnp.zeros_like(gn)
            for h in range(H):
                _, norm_vjp = jax.vjp(_rms, o[h], gn)
                d_o, d_gn = norm_vjp(dout[:, _hs(h)] * gate[:, _hs(h)])
                do_l.append(d_o)
                dgn_acc = dgn_acc + d_gn
            dgn_ref[...] += dgn_acc
            on = cat([_rms(o[h], gn) for h in range(H)])
            dhg = dout * on * shg * (1.0 + hg * (1.0 - shg))
            do_b = [t.astype(BF16) for t in do_l]
            dst = [dstate[h] for h in range(H)]
            dst_b = [t.astype(BF16) for t in dst]
            da_b = [jnp.where(causal, _nt(do_b[h], v[:, _hs(h)]), 0.0).astype(BF16) for h in range(H)]
            dv = cat([_tn(a_b[h], do_b[h]) + _nt(k_st_b[:, _hs(h)], dst_b[h]) for h in range(H)])
            dq_in = cat([_nn(da_b[h], k_in_b[:, _hs(h)]) for h in range(H)])
            dk_in = cat([_tn(da_b[h], q_in_b[:, _hs(h)]) for h in range(H)])
            dq_out = cat([_nn(do_b[h], st_b[h]) for h in range(H)])
            dk_st = cat([_nn(v[:, _hs(h)], dst_b[h]) for h in range(H)])
            ddec = cat([jnp.sum(st[h] * dst[h], axis=0, keepdims=True) for h in range(H)])
            for h in range(H):
                dstate[h] = dst[h] * dec[:, _hs(h)] + _tn(do_b[h], q_out_b[:, _hs(h)])
            t_qin = dq_in * q_in
            t_kin = dk_in * k_in
            t_kst = dk_st * k_st
            db = t_qin - t_kin + dq_out * q_out - t_kst
            dbref = jnp.sum(t_kin - t_qin, axis=0, keepdims=True)
            dblast = jnp.sum(t_kst, axis=0, keepdims=True) + ddec * dec
            db = db + jnp.where(rowid == CHUNK // 2, dbref, 0.0) + jnp.where(rowid == CHUNK - 1, dblast, 0.0)
            dlogf = _cumsum_rows(triu_b, db)
            dqf = dq_in * e1 + dq_out * e3
            dk = dk_in * e2 + dk_st * e4
            df = dlogf / f - dk
            dfz = df * (1.0 - lb) * sg * (1.0 - sg)
            dlb_ref[...] += jnp.sum(df * (1.0 - sg), axis=0, keepdims=True)
            sq = _sigmoid(q)
            dq = dqf * sq * (1.0 + q * (1.0 - sq))
            dz_ref[rows, 0:HF] = dq.astype(BF16)
            dz_ref[rows, HF:2 * HF] = dfz.astype(BF16)
            dz_ref[rows, 2 * HF:3 * HF] = dv.astype(BF16)
            dz_ref[rows, 3 * HF:4 * HF] = dhg.astype(BF16)

    return pl.pallas_call(
        body, name=name, grid=(B, nblk),
        in_specs=[pl.BlockSpec((None, ts, 4 * HF), rev),
                  pl.BlockSpec((1, HF), lambda b, s: (0, 0)),
                  pl.BlockSpec((1, HGRN_DK), lambda b, s: (0, 0)),
                  pl.BlockSpec((None, cpb, HGRN_HEADS, HGRN_DK, HGRN_DK), lambda b, s: (b, nblk - 1 - s, 0, 0, 0)),
                  pl.BlockSpec((None, ts, HF), rev),
                  ANY],
        out_specs=[pl.BlockSpec((None, ts, 4 * HF), rev),
                   pl.BlockSpec((1, HF), lambda b, s: (0, 0)),
                   pl.BlockSpec((1, HGRN_DK), lambda b, s: (0, 0))],
        out_shape=[jax.ShapeDtypeStruct(dz.shape, BF16),
                   jax.ShapeDtypeStruct((1, HF), F32),
                   jax.ShapeDtypeStruct((1, HGRN_DK), F32)],
        input_output_aliases={5: 0},
        scratch_shapes=[pltpu.VMEM((HGRN_HEADS, HGRN_DK, HGRN_DK), F32)],
        compiler_params=_params("arbitrary", "arbitrary"),
    )(zh, lb, gn, states, doa, dz)


KV_W = ATT_KV_HEADS * ATT_HD
ATT_SCALE = ATT_HD ** -0.5


def _rope(x, cos, sin, inverse=False):
    half = ROPE_DIM // 2
    outs = []
    for p in range(x.shape[1] // 128):
        xp = x[:, p * 128:(p + 1) * 128]
        lane = lax.broadcasted_iota(jnp.int32, xp.shape, 1) % ATT_HD
        sw = jnp.where(lane < half, pltpu.roll(xp, 128 - half, 1), pltpu.roll(xp, half, 1))
        outs.append(xp * cos - sw * sin if inverse else xp * cos + sw * sin)
    return outs[0] if len(outs) == 1 else jnp.concatenate(outs, axis=1)


PAIRS_PER_KV = ATT_GROUP // 2


def _swap_halves(x):
    return pltpu.roll(x, ATT_HD, 1)


def _kv_padded(t, low):
    sw = _swap_halves(t)
    zero = jnp.zeros_like(t)
    out = []
    for g in range(ATT_KV_HEADS):
        in_low, in_high = (t, sw) if g == 0 else (sw, t)
        out.append((jnp.where(low, in_low, zero).astype(BF16), jnp.where(low, zero, in_high).astype(BF16)))
    return out


def _swa_mask(first_block):
    qi = lax.broadcasted_iota(jnp.int32, (WINDOW, 2 * WINDOW), 0)
    mi = lax.broadcasted_iota(jnp.int32, (WINDOW, 2 * WINDOW), 1)
    band = (mi > qi) & (mi <= qi + WINDOW)
    return band & (jnp.logical_not(first_block) | (mi >= WINDOW))


def _swa_specs(nb):
    cur = lambda b, i: (b, i, 0)
    prev = lambda b, i: (b, jnp.maximum(i - 1, 0), 0)
    return cur, prev


def _swa_z_specs():
    q = pl.BlockSpec((None, WINDOW, W_AQ), lambda b, i: (b, i, O_AQ // W_AQ))
    kv_prev = pl.BlockSpec((None, WINDOW, W_AKV), lambda b, i: (b, jnp.maximum(i - 1, 0), O_AKV // W_AKV))
    kv_cur = pl.BlockSpec((None, WINDOW, W_AKV), lambda b, i: (b, i, O_AKV // W_AKV))
    return q, kv_prev, kv_cur


def _swa_fwd(z, cos, sin, sinks, *, name):
    B, S, _ = z.shape
    nb = S // WINDOW
    cur, prev = _swa_specs(nb)

    def body(q_ref, kvp_ref, kvc_ref, cp_ref, sp_ref, cc_ref, sc_ref, sink_ref, o_ref, lse_ref, qr_ref, kr_ref):
        cos_c, sin_c = cc_ref[...], sc_ref[...]
        q = (_rope(q_ref[...].astype(F32), cos_c, sin_c) * ATT_SCALE).astype(BF16)
        k = jnp.concatenate([_rope(kvp_ref[:, :KV_W].astype(F32), cp_ref[...], sp_ref[...]),
                             _rope(kvc_ref[:, :KV_W].astype(F32), cos_c, sin_c)], axis=0)
        qr_ref[...] = q
        kr_ref[...] = k[WINDOW:].astype(BF16)
        v = jnp.concatenate([kvp_ref[:, KV_W:], kvc_ref[:, KV_W:]], axis=0).astype(F32)
        low = lax.broadcasted_iota(jnp.int32, k.shape, 1) < ATT_HD
        kpad = _kv_padded(k, low)
        vpad = _kv_padded(v, low)
        mask = _swa_mask(pl.program_id(1) == 0)
        lses = []
        for g in range(ATT_KV_HEADS):
            pairs = range(g * PAIRS_PER_KV, (g + 1) * PAIRS_PER_KV)
            keys = [(p, e) for p in pairs for e in (0, 1)]
            qp = {p: q[:, p * 128:(p + 1) * 128] for p in pairs}
            s = {pe: jnp.where(mask, _nt(qp[pe[0]], kpad[g][pe[1]]), NEG_INF) for pe in keys}
            pr = {}
            for pe in keys:
                sink = sink_ref[0, 2 * pe[0] + pe[1]]
                m = jnp.maximum(jnp.max(s[pe], axis=1, keepdims=True), sink)
                ex = jnp.exp(s[pe] - m)
                den = jnp.sum(ex, axis=1, keepdims=True) + jnp.exp(sink - m)
                pr[pe] = (ex * (1.0 / den)).astype(BF16)
                lses.append(m + jnp.log(den))
            for p in pairs:
                o_ref[:, p * 128:(p + 1) * 128] = (_nn(pr[p, 0], vpad[g][0]) + _nn(pr[p, 1], vpad[g][1])).astype(BF16)
        lse_ref[...] = jnp.concatenate(lses, axis=1)

    tab = lambda im: pl.BlockSpec((None, WINDOW, 128), im)
    return pl.pallas_call(
        body, name=name, grid=(B, nb),
        in_specs=[*_swa_z_specs(),
                  tab(prev), tab(prev), tab(cur), tab(cur),
                  pl.BlockSpec(memory_space=pltpu.SMEM)],
        out_specs=[pl.BlockSpec((None, WINDOW, D_MODEL), cur), pl.BlockSpec((None, WINDOW, ATT_HEADS), cur),
                   pl.BlockSpec((None, WINDOW, D_MODEL), cur), pl.BlockSpec((None, WINDOW, KV_W), cur)],
        out_shape=[jax.ShapeDtypeStruct((B, S, D_MODEL), BF16), jax.ShapeDtypeStruct((B, S, ATT_HEADS), F32),
                   jax.ShapeDtypeStruct((B, S, D_MODEL), BF16), jax.ShapeDtypeStruct((B, S, KV_W), BF16)],
        compiler_params=_params("parallel", "parallel"),
    )(z, z, z, cos, sin, cos, sin, sinks)


def _swa_bwd(z, qr, kr, cos, sin, sinks, lse, dob, dz, *, name):
    B, S, _ = z.shape
    nb = S // WINDOW
    cur, prev = _swa_specs(nb)

    def body(q_ref, krp_ref, krc_ref, kvp_ref, kvc_ref, cp_ref, sp_ref, cc_ref, sc_ref, sink_ref, lse_ref, do_ref, dz_in,
             dq_ref, dkc_ref, dkp_ref, dsink_ref):
        @pl.when((pl.program_id(0) == 0) & (pl.program_id(1) == 0))
        def _():
            dsink_ref[...] = jnp.zeros_like(dsink_ref)

        cos_c, sin_c, cos_p, sin_p = cc_ref[...], sc_ref[...], cp_ref[...], sp_ref[...]
        q = q_ref[...]
        k = jnp.concatenate([krp_ref[...], krc_ref[...]], axis=0).astype(F32)
        v = jnp.concatenate([kvp_ref[:, KV_W:], kvc_ref[:, KV_W:]], axis=0).astype(F32)
        low = lax.broadcasted_iota(jnp.int32, k.shape, 1) < ATT_HD
        kpad = _kv_padded(k, low)
        vpad = _kv_padded(v, low)
        mask = _swa_mask(pl.program_id(1) == 0)
        lse = lse_ref[...]
        dq_parts, dk_sum, dv_sum, dsinks = [], [], [], []
        for g in range(ATT_KV_HEADS):
            pairs = range(g * PAIRS_PER_KV, (g + 1) * PAIRS_PER_KV)
            keys = [(p, e) for p in pairs for e in (0, 1)]
            qp = {p: q[:, p * 128:(p + 1) * 128] for p in pairs}
            dop = {p: do_ref[:, p * 128:(p + 1) * 128] for p in pairs}
            s = {pe: jnp.where(mask, _nt(qp[pe[0]], kpad[g][pe[1]]), NEG_INF) for pe in keys}
            dp = {pe: _nt(dop[pe[0]], vpad[g][pe[1]]) for pe in keys}
            pr, ds = {}, {}
            for pe in keys:
                h = 2 * pe[0] + pe[1]
                lse_h = lse[:, h:h + 1]
                pf = jnp.exp(s[pe] - lse_h)
                delta = jnp.sum(pf * dp[pe], axis=1, keepdims=True)
                ds[pe] = (pf * (dp[pe] - delta)).astype(BF16)
                pr[pe] = pf.astype(BF16)
                p_sink = jnp.exp(sink_ref[0, h] - lse_h)
                dsinks.append(-jnp.sum(p_sink * delta, axis=0, keepdims=True))
            for p in pairs:
                dq_parts.append((_nn(ds[p, 0], kpad[g][0]) + _nn(ds[p, 1], kpad[g][1])) * ATT_SCALE)
            x = [sum(_tn(ds[p, e], qp[p]) for p in pairs) for e in (0, 1)]
            y = [sum(_tn(pr[p, e], dop[p]) for p in pairs) for e in (0, 1)]
            zk = jnp.where(low, x[0], x[1])
            zv = jnp.where(low, y[0], y[1])
            dk_sum.append(zk + _swap_halves(zk))
            dv_sum.append(zv + _swap_halves(zv))
        dq_ref[...] = _rope(jnp.concatenate(dq_parts, axis=1), cos_c, sin_c, inverse=True).astype(BF16)
        dk = jnp.where(low, dk_sum[0], dk_sum[1])
        dv = jnp.where(low, dv_sum[0], dv_sum[1])
        dkp_ref[:, :KV_W] = _rope(dk[:WINDOW], cos_p, sin_p, inverse=True)
        dkp_ref[:, KV_W:] = dv[:WINDOW]
        dkc_ref[:, :KV_W] = _rope(dk[WINDOW:], cos_c, sin_c, inverse=True)
        dkc_ref[:, KV_W:] = dv[WINDOW:]
        dsink_ref[...] += jnp.concatenate(dsinks, axis=1)

    tab = lambda im: pl.BlockSpec((None, WINDOW, 128), im)
    return pl.pallas_call(
        body, name=name, grid=(B, nb),
        in_specs=[pl.BlockSpec((None, WINDOW, D_MODEL), cur), tab(prev), tab(cur),
                  *_swa_z_specs()[1:],
                  tab(prev), tab(prev), tab(cur), tab(cur),
                  pl.BlockSpec(memory_space=pltpu.SMEM),
                  pl.BlockSpec((None, WINDOW, ATT_HEADS), cur),
                  pl.BlockSpec((None, WINDOW, D_MODEL), cur),
                  ANY],
        out_specs=[_swa_z_specs()[0],
                   pl.BlockSpec((None, WINDOW, 2 * KV_W), cur), pl.BlockSpec((None, WINDOW, 2 * KV_W), cur),
                   pl.BlockSpec((1, ATT_HEADS), lambda b, i: (0, 0))],
        out_shape=[jax.ShapeDtypeStruct(dz.shape, BF16),
                   jax.ShapeDtypeStruct((B, S, 2 * KV_W), F32), jax.ShapeDtypeStruct((B, S, 2 * KV_W), F32),
                   jax.ShapeDtypeStruct((1, ATT_HEADS), F32)],
        input_output_aliases={12: 0},
        compiler_params=_params("arbitrary", "arbitrary"),
    )(qr, kr, kr, z, z, cos, sin, cos, sin, sinks, lse, dob, dz)


def _swa_dkv_combine(dkv_cur, dkv_prev, dz, *, name):
    B, S, W = dkv_cur.shape

    def body(c_ref, p_ref, dz_in, o_ref):
        rows = lax.broadcasted_iota(jnp.int32, (S, W), 0)
        o_ref[...] = (c_ref[...] + _shift_up(p_ref[...], WINDOW, rows, S)).astype(BF16)

    seq = pl.BlockSpec((None, S, W), lambda b: (b, 0, 0))
    return pl.pallas_call(
        body, name=name, grid=(B,),
        in_specs=[seq, seq, ANY], out_specs=pl.BlockSpec((None, S, W), lambda b: (b, 0, O_AKV // W_AKV)),
        out_shape=jax.ShapeDtypeStruct(dz.shape, BF16),
        input_output_aliases={2: 0},
        compiler_params=_params("parallel"),
    )(dkv_cur, dkv_prev, dz)


def _rope_tables(positions):
    half = ROPE_DIM // 2
    inv = ROPE_THETA ** (-2.0 * jnp.arange(half, dtype=F32) / ROPE_DIM)
    ang = positions.astype(F32)[..., None] * inv
    c, s = jnp.cos(ang), jnp.sin(ang)
    pad = jnp.zeros(ang.shape[:-1] + (ATT_HD - ROPE_DIM,), F32)
    cos = jnp.concatenate([c, c, pad + 1.0], axis=-1)
    sin = jnp.concatenate([-s, s, pad], axis=-1)
    return jnp.tile(cos, (1, 1, 2)), jnp.tile(sin, (1, 1, 2))


def _lower_bound(lb_logits, *, name):
    def body(l_ref, o_ref):
        l = l_ref[...]
        e = jnp.exp(l - jnp.max(l, axis=0, keepdims=True))
        o_ref[...] = e[0:1] / jnp.sum(e, axis=0, keepdims=True)

    return pl.pallas_call(body, name=name, out_shape=jax.ShapeDtypeStruct((1, lb_logits.shape[1]), F32))(lb_logits)


W_ZH, W_GATES, W_AQ, W_AKV = 4 * HF, 2 * D_MODEL, ATT_HEADS * ATT_HD, 2 * KV_W
O_ZH, O_GATES, O_AQ, O_AKV = 0, W_ZH, W_ZH + W_GATES, W_ZH + W_GATES + W_AQ
W_IN = W_ZH + W_GATES + W_AQ + W_AKV


W_IN_BLK = W_IN // N_DEV


def _reference_row_block(j, rows=256):
    nz, ng = W_ZH // rows, W_GATES // rows
    return jnp.where(j < nz, j, jnp.where(j < nz + ng, j + (W_AQ + W_AKV) // rows, j - ng))


def _reordered_rows(w_t, *, name):
    rows = 256

    def body(i_ref, o_ref):
        o_ref[...] = i_ref[...]

    return pl.pallas_call(
        body, name=name, grid=(W_IN // rows,),
        in_specs=[pl.BlockSpec((rows, D_MODEL), lambda j: (_reference_row_block(j, rows), 0))],
        out_specs=pl.BlockSpec((rows, D_MODEL), lambda j: (j, 0)),
        out_shape=jax.ShapeDtypeStruct(w_t.shape, w_t.dtype), compiler_params=_params("parallel"))(w_t)


def _local_step(x, positions, target, small, w_in_t, rest_weights, emit, start_token):
    B, S, D = x.shape
    T = B * S
    x2 = x.reshape(T, D)
    cos, sin = _rope_tables(positions)
    lb = _lower_bound(small["lb_logits"], name="lb_fwd")
    zero = lambda tok: tok[0:1, 0:1]

    u1 = _norm_cast(x2, small["norm1_g"] + zero(start_token), name="norm1")
    z = _matmul(u1, w_in_t, tb=True, out_dtype=BF16, name="mm_z", tm=1024, tn=W_IN // 2)
    z3 = z.reshape(B, S, W_IN)
    oa, states = _hgrn_fwd(z3, lb, small["hgrn_norm_g"], name="hgrn_fwd")
    ob, lse, qr, kr = _swa_fwd(z3, cos, sin, small["attn_sinks"], name="swa_fwd")
    oa2 = oa.reshape(T, D)
    ob2 = ob.reshape(T, D)
    W = rest_weights("mix", ob)
    row = lambda tm, dtype=None: _row_spec(tm, D)
    tile = lambda dtype: jax.ShapeDtypeStruct((T, D), dtype)
    vec = _full_spec((1, D))
    vec_shape = jax.ShapeDtypeStruct((1, D), F32)

    def merge_ep(acc_a, acc_b, g_ref):
        pa, pb = acc_a.astype(BF16), acc_b.astype(BF16)
        return pa, pb, _merge_fn(g_ref[...], pa, pb)

    pa, pb, merged = _matmul_ep([(oa2, W["w_a"], False, 0), (ob2, W["w_b"], False, 0)], tm=1024, ins=[z], in_specs=[_gates_spec(1024)],
                                out_shapes=[tile(BF16)] * 3, out_specs=[row(1024)] * 3, epilogue=merge_ep, name="mm_pa_pb_merge")

    def resid_norm_ep(acc, x_ref, g_ref):
        hh = acc + x_ref[...]
        return hh, _rms(hh, g_ref[...])

    h, u2 = _matmul_ep([(merged, W["w_out"], False, 0)], tm=1024, ins=[x2, small["norm2_g"]], in_specs=[row(1024), vec],
                       out_shapes=[tile(F32), tile(BF16)], out_specs=[row(1024), row(1024)], epilogue=resid_norm_ep, name="mm_h_norm2")
    W.update(rest_weights("ffn", u2))
    gu = _matmul(u2, W["w_ffn_t"], tb=True, out_dtype=BF16, name="mm_gu", tm=1024, tn=D_FF)
    gu3 = gu.reshape(B, S, 2 * D_FF)
    act, a_pre = _conv_act_fwd(gu3, W["conv_w"], small["conv_b"], name="conv_act_fwd")
    act2 = act.reshape(T, D_FF)
    g = {}

    def loss_ep(acc, h_ref, g_ref, t_ref):
        y, vjp = jax.vjp(_rms, acc + h_ref[...], g_ref[...])
        err = y - t_ref[...]
        dx, dg = vjp(err * (1.0 / D))
        return dx, dx, dg, (0.5 / D) * jnp.sum(jnp.sum(err * err, axis=1, keepdims=True), axis=0, keepdims=True)

    dh2, dh2b, g["final_g"], loss = _matmul_ep(
        [(act2, W["w_down"], False, 0)], tm=512, ins=[h, small["final_g"].reshape(1, D), target.reshape(T, D)], in_specs=[row(512), vec, row(512)],
        out_shapes=[tile(F32), tile(BF16), vec_shape, jax.ShapeDtypeStruct((1, 1), F32)],
        out_specs=[row(512), row(512), vec, _full_spec((1, 1))], sums=(2, 3), epilogue=loss_ep, name="mm_h2_loss")
    dact = _matmul(dh2b, W["w_down"], tb=True, out_dtype=BF16, name="mm_dact", tm=1024, tn=D_FF)
    dw_down_t = _matmul(dh2b, act2, ta=True, out_dtype=BF16, name="mm_dw_down", tm=1024, tn=256, tk=8192)
    dg_, dup, g["conv_w"], g["conv_b"] = _conv_act_bwd(gu3, a_pre, W["conv_w"], dact.reshape(B, S, D_FF), name="conv_act_bwd")
    dg2 = dg_.reshape(T, D_FF)
    dup2 = dup.reshape(T, D_FF)
    dw_ffn_t = _matmul(u2, dg2, ta=True, out_t=True, out_dtype=BF16, into=lax.empty((2 * D_FF, D), BF16), o_noff=0, name="mm_dw_ffn_g", tm=1024, tn=256, tk=8192)
    dw_ffn_t = _matmul(u2, dup2, ta=True, out_t=True, out_dtype=BF16, into=dw_ffn_t, o_noff=D_FF // 256, name="mm_dw_ffn_u", tm=1024, tn=256, tk=8192)
    tok = emit("ffn", dict(w_ffn_t=dw_ffn_t, w_down=dw_down_t.T))
    def norm2_bwd_ep(acc_g, acc_u, h_ref, g_ref, dh2_ref):
        _, vjp = jax.vjp(_rms, h_ref[...], g_ref[...])
        dx, dg = vjp(acc_g + acc_u)
        dx = dx + dh2_ref[...]
        return dx, dx, dg

    dh, dhb, g["norm2_g"] = _matmul_ep(
        [(dg2, W["w_ffn_t"], False, 0), (dup2, W["w_ffn_t"], False, 1)], tm=512, ins=[h, small["norm2_g"] + zero(tok), dh2], in_specs=[row(512), vec, row(512)],
        out_shapes=[tile(F32), tile(BF16), vec_shape], out_specs=[row(512), row(512), vec], sums=(2,), epilogue=norm2_bwd_ep, name="mm_du2_norm2_bwd")
    dw_out = _matmul(merged, dhb, ta=True, out_dtype=BF16, name="mm_dw_out", tm=1024, tn=1024, tk=4096)

    def merge_bwd_ep(acc, g_ref, pa_ref, pb_ref, dz_in):
        gt = g_ref[...].astype(F32)
        sa = _sigmoid(gt[:, :D_MODEL])
        sb = _sigmoid(gt[:, D_MODEL:])
        dgates = jnp.concatenate([acc * pa_ref[...].astype(F32) * sa * (1.0 - sa), acc * pb_ref[...].astype(F32) * sb * (1.0 - sb)], axis=1)
        return dgates, acc * sa, acc * sb

    dz, dpa, dpb = _matmul_ep(
        [(dhb, W["w_out"], True, 0)], tm=512, ins=[z, pa, pb, lax.empty((T, W_IN), BF16)], in_specs=[_gates_spec(512), row(512), row(512), ANY],
        out_shapes=[jax.ShapeDtypeStruct((T, W_IN), BF16), tile(BF16), tile(BF16)], out_specs=[_gates_spec(512), row(512), row(512)],
        aliases={3: 0}, epilogue=merge_bwd_ep, name="mm_dmerged_merge_bwd")
    doa, dob = _matmul_ep([(dpa, W["w_a"], True, 0), (dpb, W["w_b"], True, 0)], tm=1024, ins=[], in_specs=[],
                          out_shapes=[tile(BF16)] * 2, out_specs=[row(1024)] * 2, epilogue=lambda da, db: (da, db), name="mm_doa_dob")
    dw_a = _matmul(oa2, dpa, ta=True, out_dtype=BF16, name="mm_dw_a", tm=1024, tn=1024, tk=2048)
    dw_b = _matmul(ob2, dpb, ta=True, out_dtype=BF16, name="mm_dw_b", tm=1024, tn=256, tk=8192)
    tok = emit("mix", dict(w_out=dw_out, w_a=dw_a, w_b=dw_b))
    dz3, dkv_cur, dkv_prev, dsinks = _swa_bwd(z3, qr, kr, cos, sin, small["attn_sinks"] + zero(tok), lse, dob.reshape(B, S, D),
                                              dz.reshape(B, S, W_IN), name="swa_bwd")
    dz3 = _swa_dkv_combine(dkv_cur, dkv_prev, dz3, name="swa_dkv")
    g["attn_sinks"] = dsinks
    dz3, g["lb"], g["hgrn_norm_g"] = _hgrn_bwd(z3, lb, small["hgrn_norm_g"], states, doa.reshape(B, S, D), dz3, name="hgrn_bwd")
    dz = dz3.reshape(T, W_IN)
    dw_in_t = _matmul(u1, dz, ta=True, out_t=True, o_block_perm=_reference_row_block, out_dtype=BF16, name="mm_dw_in", tm=1024, tn=256, tk=8192)
    tok = emit("in", dict(w_in_t=dw_in_t))
    du1 = _matmul(dz, w_in_t, after=tok, out_dtype=BF16, name="mm_du1", tm=1024, tn=512)
    dx, g["norm1_g"] = _norm_bwd_add(x2, small["norm1_g"], du1, dh, with_bf16=False, name="norm1_bwd")
    g["lb_logits"] = _lb_bwd(g.pop("lb"), lb, name="lb_bwd")
    return loss, dx.reshape(B, S, D), g


def _my_place():
    return lax.axis_index("x"), lax.axis_index("y"), lax.axis_index("c")


def _gather_blocks(x_ref, out_ref, send_sems, recv_sems, local_sem):
    x, y, c = _my_place()
    me, sibling = (x, y, c), (x, y, 1 - c)
    chips = [(1 - x, y), (x, 1 - y), (1 - x, 1 - y)]

    def slot(px, py, pc):
        return out_ref.at[4 * px + 2 * py + pc]

    def copy(k, block, to, src=None):
        return pltpu.make_async_remote_copy(
            src_ref=slot(*block) if src is None else src, dst_ref=slot(*block),
            send_sem=send_sems.at[k], recv_sem=recv_sems.at[k], device_id=to, device_id_type=MESH)

    mine = pltpu.make_async_copy(x_ref, slot(*me), local_sem)
    mine.start()
    first = [copy(0, me, sibling, src=x_ref)]
    first += [copy(1 + j, me, (*chip, c), src=x_ref) for j, chip in enumerate(chips)]
    for cp in first:
        cp.start()
    passed = [copy(4 + j, (*chip, c), sibling) for j, chip in enumerate(chips)]
    for j, chip in enumerate(chips):
        copy(1 + j, (*chip, c), me).wait_recv()
        passed[j].start()
    copy(0, sibling, me).wait_recv()
    for j, chip in enumerate(chips):
        copy(4 + j, (*chip, 1 - c), me).wait_recv()
    for cp in first + passed:
        cp.wait_send()
    mine.wait()


GATHER_SEMS = [pltpu.SemaphoreType.DMA((7,)), pltpu.SemaphoreType.DMA((7,)), pltpu.SemaphoreType.DMA]


def _all_gather(blk, *, name):
    return pl.pallas_call(
        _gather_body_fn(), name=name,
        out_shape=jax.ShapeDtypeStruct((N_DEV,) + blk.shape, blk.dtype),
        in_specs=[ANY], out_specs=ANY,
        scratch_shapes=GATHER_SEMS,
    )(blk)


def _gather_body_fn():
    def body(x_ref, out_ref, send_sems, recv_sems, local_sem):
        _gather_blocks(x_ref, out_ref, send_sems, recv_sems, local_sem)
    return body


SLAB_W = 1152
SMALL_SHAPES = dict(norm1_g=(1, D_MODEL), lb_logits=(2, HGRN_HEADS * HGRN_DK), hgrn_norm_g=(1, HGRN_DK), attn_sinks=(1, ATT_HEADS),
                    norm2_g=(1, D_MODEL), conv_b=(1, D_FF), final_g=(1, D_MODEL))
CONVW_BLK = D_FF // N_DEV
CONVW_STRIDE = SLAB_W // 3


def _slab_layout():
    layout, r = {}, 0
    for nm, (nr, w) in SMALL_SHAPES.items():
        layout[nm] = []
        for i in range(nr):
            for c0 in range(0, w, SLAB_W):
                layout[nm].append((r, i, c0, min(SLAB_W, w - c0)))
                r += 1
    return layout, r


SMALL_ROWS, _N_SMALL_ROWS = _slab_layout()
CONV_ROW0 = -(-_N_SMALL_ROWS // 8) * 8
LOSS_ROW = CONV_ROW0 + N_DEV
SLAB_ROWS = LOSS_ROW + 8


def _small_step(grads, g_conv_w, loss, params, moments, variances, dev, *, name):
    names = list(SMALL_ROWS)
    n = len(names)

    def body(dev_ref, *refs):
        g_refs = dict(zip(names, refs[:n]))
        gc_ref, loss_ref = refs[n], refs[n + 1]
        base = n + 2
        w_refs, m_refs, v_refs = (dict(zip(names + ["conv_w"], refs[base + i * (n + 1):base + (i + 1) * (n + 1)])) for i in range(3))
        o = base + 3 * (n + 1)
        gath_ref, loss_out = refs[o], refs[o + 1]
        outs = {nm: refs[o + 2 + 4 * i:o + 6 + 4 * i] for i, nm in enumerate(names + ["conv_w"])}
        slab, total, send_sems, recv_sems, local_sem = refs[-5:]

        slab[...] = jnp.zeros_like(slab)
        for nm, pieces in SMALL_ROWS.items():
            for r, i, c0, w in pieces:
                slab[r:r + 1, 0:w] = g_refs[nm][i:i + 1, c0:c0 + w]
        for p in range(N_DEV):
            for j in range(3):
                slab[CONV_ROW0 + p:CONV_ROW0 + p + 1, j * CONVW_STRIDE:j * CONVW_STRIDE + CONVW_BLK] = gc_ref[j:j + 1, p * CONVW_BLK:(p + 1) * CONVW_BLK]
        slab[LOSS_ROW:LOSS_ROW + 1, 0:1] = loss_ref[...]
        _gather_blocks(slab, gath_ref, send_sems, recv_sems, local_sem)
        acc = gath_ref[0]
        for p in range(1, N_DEV):
            acc = acc + gath_ref[p]
        total[...] = acc
        loss_out[...] = total[LOSS_ROW:LOSS_ROW + 1, 0:1]

        def update(nm, g, i, c0, w):
            at = (slice(i, i + 1), slice(c0, c0 + w))
            d, mn, vn = _adamw_math(w_refs[nm][at], g, m_refs[nm][at], v_refs[nm][at])
            for ref, val in zip(outs[nm], (g, d, mn, vn)):
                ref[at] = val

        for nm, pieces in SMALL_ROWS.items():
            for r, i, c0, w in pieces:
                update(nm, total[r:r + 1, 0:w], i, c0, w)
        conv_rows = total[CONV_ROW0:CONV_ROW0 + N_DEV, :]
        rowid = lax.broadcasted_iota(jnp.int32, conv_rows.shape, 0)
        mine = jnp.sum(jnp.where(rowid == dev_ref[0], conv_rows, 0.0), axis=0, keepdims=True)
        for j in range(3):
            update("conv_w", mine[:, j * CONVW_STRIDE:j * CONVW_STRIDE + CONVW_BLK], j, 0, CONVW_BLK)

    order = names + ["conv_w"]
    ins = [grads[nm] for nm in names] + [g_conv_w, loss]
    for d in (params, moments, variances):
        ins += [d[nm] for nm in order]
    vmem = pl.BlockSpec(memory_space=pltpu.VMEM)
    out_shape = [jax.ShapeDtypeStruct((N_DEV, SLAB_ROWS, SLAB_W), F32), jax.ShapeDtypeStruct((1, 1), F32)]
    for nm in order:
        out_shape += [jax.ShapeDtypeStruct(params[nm].shape, F32)] * 4
    res = pl.pallas_call(
        body, name=name,
        grid_spec=pltpu.PrefetchScalarGridSpec(
            num_scalar_prefetch=1, grid=(1,),
            in_specs=[vmem] * len(ins), out_specs=[vmem] * len(out_shape),
            scratch_shapes=[pltpu.VMEM((SLAB_ROWS, SLAB_W), F32), pltpu.VMEM((SLAB_ROWS, SLAB_W), F32)] + GATHER_SEMS),
        out_shape=out_shape,
    )(dev, *ins)
    return res[1], {nm: tuple(res[2 + 4 * i:6 + 4 * i]) for i, nm in enumerate(order)}


HBM_SPEC = pl.BlockSpec(memory_space=pltpu.HBM)
SEM_SPEC = pl.BlockSpec(memory_space=pltpu.SEMAPHORE)
DATAFLOW_EFFECT = pltpu.SideEffectType.DATAFLOW_SIDE_EFFECTING
N_PEERS = N_DEV - 1


def _peers(x, y, c):
    return [(1 - x if r & 4 else x, 1 - y if r & 2 else y, 1 - c if r & 1 else c) for r in range(1, N_DEV)]


def _exchange_start(srcs, scatter, *, after=None, name):
    n = len(srcs)
    lands = [lax.empty(a.shape if scatter else (N_DEV,) + a.shape, a.dtype) for a in srcs]
    extra = [] if after is None else [after]

    def body(*refs):
        src_refs, land_refs = refs[:n], refs[n:2 * n]
        send_sems, recv_sems, token = refs[2 * n + len(extra)], refs[2 * n + len(extra) + 1], refs[-1]
        x, y, c = _my_place()
        me = 4 * x + 2 * y + c
        for i in range(n):
            for r, (tx, ty, tc) in enumerate(_peers(x, y, c)):
                src = src_refs[i].at[4 * tx + 2 * ty + tc] if scatter else src_refs[i]
                pltpu.make_async_remote_copy(
                    src_ref=src, dst_ref=land_refs[i].at[me], send_sem=send_sems.at[N_PEERS * i + r],
                    recv_sem=recv_sems.at[N_PEERS * i + r], device_id=(tx, ty, tc), device_id_type=MESH).start()
        token[...] = jnp.zeros_like(token)

    thru = [pltpu.HBM(a.shape, a.dtype) for a in list(srcs) + lands]
    res = pl.pallas_call(
        body, name=name,
        out_shape=(pltpu.SemaphoreType.DMA((N_PEERS * n,)), pltpu.SemaphoreType.DMA((N_PEERS * n,)), *thru,
                   jax.ShapeDtypeStruct((8, 128), F32)),
        in_specs=[HBM_SPEC] * (2 * n) + [ANY] * len(extra),
        out_specs=(SEM_SPEC, SEM_SPEC, *([HBM_SPEC] * (2 * n)), pl.BlockSpec(memory_space=pltpu.VMEM)),
        input_output_aliases={i: 2 + i for i in range(2 * n)},
        compiler_params=pltpu.CompilerParams(has_side_effects=DATAFLOW_EFFECT),
    )(*[pltpu.with_memory_space_constraint(a, pltpu.HBM) for a in list(srcs) + lands], *extra)
    return (res[0], res[1], list(res[2:2 + n]), list(res[2 + n:2 + 2 * n]), scatter), res[-1]


def _exchange_wait(handle, after, *, name):
    send_sems, recv_sems, srcs, lands, scatter = handle
    n = len(srcs)

    def body(*refs):
        src_refs, land_refs = refs[:n], refs[n:2 * n]
        send_sems, recv_sems = refs[2 * n], refs[2 * n + 1]
        x, y, c = _my_place()
        for i in range(n):
            for r in range(N_PEERS):
                src = src_refs[i].at[0] if scatter else src_refs[i]
                cp = pltpu.make_async_remote_copy(
                    src_ref=src, dst_ref=land_refs[i].at[0], send_sem=send_sems.at[N_PEERS * i + r],
                    recv_sem=recv_sems.at[N_PEERS * i + r], device_id=(x, y, c), device_id_type=MESH)
                cp.wait_send()
                cp.wait_recv()

    thru = [pltpu.HBM(a.shape, a.dtype) for a in srcs + lands]
    res = pl.pallas_call(
        body, name=name, out_shape=tuple(thru),
        in_specs=[HBM_SPEC] * (2 * n) + [SEM_SPEC, SEM_SPEC, ANY], out_specs=tuple([HBM_SPEC] * (2 * n)),
        input_output_aliases={i: i for i in range(2 * n)},
        compiler_params=pltpu.CompilerParams(has_side_effects=DATAFLOW_EFFECT),
    )(*srcs, *lands, send_sems, recv_sems, after)
    return list(res[:n]), list(res[n:])


def _with_own(land, own, me):
    return lax.dynamic_update_index_in_dim(land, own, me, 0)


def _adamw_math(w, g, m, v):
    m = ADAM_B1 * m + (1.0 - ADAM_B1) * g
    v = ADAM_B2 * v + (1.0 - ADAM_B2) * (g * g)
    m_hat = m / (1.0 - ADAM_B1 ** ADAM_STEP)
    v_hat = v / (1.0 - ADAM_B2 ** ADAM_STEP)
    delta = -ADAM_LR * (m_hat / (jnp.sqrt(v_hat) + ADAM_EPS) + ADAM_WD * w)
    return delta, m, v


def _adamw_sum(parts, w, m, v, *, name):
    shape = w.shape
    R, n = shape[-2], shape[-1]
    w, m, v = (t.reshape(R, n) for t in (w, m, v))
    tr = _pick(R, (256, 464, 352, 128))

    def body(p_ref, w_ref, m_ref, v_ref, g_ref, d_ref, mo_ref, vo_ref):
        g = p_ref[0].astype(F32)
        for p in range(1, N_DEV):
            g = g + p_ref[p].astype(F32)
        d, mn, vn = _adamw_math(w_ref[...], g, m_ref[...], v_ref[...])
        g_ref[...] = g
        d_ref[...] = d
        mo_ref[...] = mn
        vo_ref[...] = vn

    row = pl.BlockSpec((tr, n), lambda i: (i, 0))
    outs = pl.pallas_call(
        body, name=name, grid=(R // tr,),
        in_specs=[pl.BlockSpec((N_DEV, tr, n), lambda i: (0, i, 0)), row, row, row],
        out_specs=[row, row, row, row],
        out_shape=[jax.ShapeDtypeStruct((R, n), F32)] * 4,
        compiler_params=_params("parallel"),
    )(parts, w, m, v)
    return [t.reshape(shape) for t in outs]


def _lb_bwd(dlb, lb, *, name):
    def body(d_ref, lb_ref, o_ref):
        t = d_ref[...] * lb_ref[...] * (1.0 - lb_ref[...])
        o_ref[0:1, :] = t
        o_ref[1:2, :] = -t

    return pl.pallas_call(body, name=name, out_shape=jax.ShapeDtypeStruct((2, lb.shape[1]), F32))(dlb, lb)


DOWN_BLK, ROW_BLK = D_FF // N_DEV, D_MODEL // N_DEV
W_FFN_BLK = 2 * D_FF // N_DEV
CONV_BITS_SHAPE = (16, 256)


def kernel(x, positions, norm1_g, w_in, lb_logits, hgrn_norm_g, w_a, attn_sinks, w_b, w_out, norm2_g, w_ffn_in, conv_w, conv_b, w_down, final_g, loss_target, m_norm1_g, m_w_in, m_lb_logits, m_hgrn_norm_g, m_w_a, m_attn_sinks, m_w_b, m_w_out, m_norm2_g, m_w_ffn_in, m_conv_w, m_conv_b, m_w_down, m_final_g, v_norm1_g, v_w_in, v_lb_logits, v_hgrn_norm_g, v_w_a, v_attn_sinks, v_w_b, v_w_out, v_norm2_g, v_w_ffn_in, v_conv_w, v_conv_b, v_w_down, v_final_g):
    xi, yi, ci = _my_place()
    dev = 4 * xi + 2 * yi + ci

    tr = lambda t: jnp.transpose(t[0])
    untr = lambda t: jnp.transpose(t)[None]
    w_in_blocks = _all_gather(tr(w_in).astype(BF16), name="ag_w_in")
    conv_bits = lax.bitcast_convert_type(conv_w, BF16).reshape(-1)
    conv_bits = jnp.pad(conv_bits, (0, CONV_BITS_SHAPE[0] * CONV_BITS_SHAPE[1] - conv_bits.shape[0])).reshape(CONV_BITS_SHAPE)
    w_in_full_t = _reordered_rows(w_in_blocks.reshape(W_IN, D_MODEL), name="w_in_rows")
    gather_handles = {}
    gather_handles["mix"], tok_mix = _exchange_start([w_a[0].astype(BF16), w_b[0].astype(BF16), w_out[0].astype(BF16)], False,
                                                     after=w_in_full_t, name="ag_mix_start")
    gather_handles["ffn"], tok_ffn = _exchange_start([tr(w_ffn_in).astype(BF16), w_down[0].astype(BF16), conv_bits], False,
                                                     after=tok_mix, name="ag_ffn_start")
    start_token = tok_mix + tok_ffn

    def rest_weights(group, after):
        own, lands = _exchange_wait(gather_handles[group], after, name="ag_" + group + "_wait")
        full = [_with_own(l, o, dev) for l, o in zip(lands, own)]
        if group == "mix":
            return dict(zip(("w_a", "w_b", "w_out"), [t.reshape(D_MODEL, D_MODEL) for t in full]))
        bits = full[2].reshape(N_DEV, -1)[:, :3 * CONVW_BLK * 2].reshape(N_DEV, 3, CONVW_BLK, 2)
        return dict(w_ffn_t=full[0].reshape(2 * D_FF, D_MODEL), w_down=full[1].reshape(D_FF, D_MODEL),
                    conv_w=lax.bitcast_convert_type(bits, F32).transpose(1, 0, 2).reshape(3, D_FF))

    handles = {}

    def emit(group, gr):
        if group == "ffn":
            srcs = [gr["w_ffn_t"].reshape(N_DEV, W_FFN_BLK, D_MODEL), gr["w_down"].reshape(N_DEV, DOWN_BLK, D_MODEL)]
        elif group == "mix":
            srcs = [gr[n].reshape(N_DEV, ROW_BLK, D_MODEL) for n in ("w_out", "w_a", "w_b")]
        else:
            srcs = [gr["w_in_t"].reshape(N_DEV, W_IN_BLK, D_MODEL)]
        handles[group], token = _exchange_start(srcs, True, name="rs_" + group + "_start")
        return token

    small = dict(norm1_g=norm1_g, lb_logits=lb_logits, hgrn_norm_g=hgrn_norm_g, attn_sinks=attn_sinks, norm2_g=norm2_g,
                 conv_b=conv_b, final_g=final_g)
    loss, grad_x, g = _local_step(x, positions, loss_target, small, w_in_full_t, rest_weights, emit, start_token)

    def parts_of(group, after):
        srcs, lands = _exchange_wait(handles[group], after, name="rs_" + group + "_wait")
        return [_with_own(l, lax.dynamic_index_in_dim(s, dev, 0, keepdims=False), dev) for s, l in zip(srcs, lands)]

    p_ffn, p_down = parts_of("ffn", grad_x)
    p_out, p_a, p_b = parts_of("mix", grad_x)
    (p_in,) = parts_of("in", grad_x)
    big = dict(
        w_in=[untr(t) for t in _adamw_sum(p_in, tr(w_in), tr(m_w_in), tr(v_w_in), name="adamw_w_in")],
        w_a=_adamw_sum(p_a, w_a, m_w_a, v_w_a, name="adamw_w_a"),
        w_b=_adamw_sum(p_b, w_b, m_w_b, v_w_b, name="adamw_w_b"),
        w_out=_adamw_sum(p_out, w_out, m_w_out, v_w_out, name="adamw_w_out"),
        w_ffn_in=[untr(t) for t in _adamw_sum(p_ffn, tr(w_ffn_in), tr(m_w_ffn_in), tr(v_w_ffn_in), name="adamw_w_ffn_in")],
        w_down=_adamw_sum(p_down, w_down, m_w_down, v_w_down, name="adamw_w_down"),
    )

    row = lambda t: t.reshape(1, -1) if t.ndim == 1 else t
    shard = lambda t: t.reshape(3, CONVW_BLK)
    sm_g = {nm: g[nm] for nm in SMALL_ROWS}
    sm_w = dict(norm1_g=norm1_g, lb_logits=lb_logits, hgrn_norm_g=hgrn_norm_g, attn_sinks=attn_sinks, norm2_g=norm2_g,
                conv_b=conv_b, final_g=row(final_g), conv_w=shard(conv_w))
    sm_m = dict(norm1_g=m_norm1_g, lb_logits=m_lb_logits, hgrn_norm_g=m_hgrn_norm_g, attn_sinks=m_attn_sinks, norm2_g=m_norm2_g,
                conv_b=m_conv_b, final_g=row(m_final_g), conv_w=shard(m_conv_w))
    sm_v = dict(norm1_g=v_norm1_g, lb_logits=v_lb_logits, hgrn_norm_g=v_hgrn_norm_g, attn_sinks=v_attn_sinks, norm2_g=v_norm2_g,
                conv_b=v_conv_b, final_g=row(v_final_g), conv_w=shard(v_conv_w))
    loss_total, sm_out = _small_step(sm_g, g["conv_w"], loss, sm_w, sm_m, sm_v, dev.astype(jnp.int32).reshape(1), name="small_step")
    shapes = dict(final_g=final_g.shape, conv_w=conv_w.shape)

    names = ("norm1_g", "w_in", "lb_logits", "hgrn_norm_g", "w_a", "attn_sinks", "w_b", "w_out", "norm2_g", "w_ffn_in", "conv_w", "conv_b", "w_down", "final_g")
    outs = [loss_total.reshape(()), grad_x]
    for kind in range(4):
        outs += [big[n][kind] if n in big else sm_out[n][kind].reshape(shapes.get(n, sm_out[n][kind].shape)) for n in names]
    return tuple(outs)
```

```python
import jax
import jax.numpy as jnp
from jax import lax
from jax.experimental import pallas as pl
from jax.experimental.pallas import tpu as pltpu

F32 = jnp.float32
BF16 = jnp.bfloat16

D_MODEL = 1024
HGRN_HEADS = 8
HGRN_DK = 128
CHUNK = 64
ATT_HEADS = 16
ATT_KV_HEADS = 2
ATT_HD = 64
ATT_GROUP = ATT_HEADS // ATT_KV_HEADS
WINDOW = 128
ROPE_DIM = ATT_HD // 4
ROPE_THETA = 500000.0
D_FF = 2816
EPS = 1e-6
NEG_INF = -1e30
N_DEV = 8

ADAM_LR = 0.001
ADAM_B1 = 0.9
ADAM_B2 = 0.999
ADAM_EPS = 1e-08
ADAM_WD = 0.01
ADAM_STEP = 10

MESH = pl.DeviceIdType.MESH
ANY = pl.BlockSpec(memory_space=pl.ANY)


def _pick(n, cands):
    for c in cands:
        if n % c == 0:
            return c
    return n


def _sigmoid(x):
    return 0.5 * jnp.tanh(0.5 * x) + 0.5


def _silu(x):
    hx = 0.5 * x
    return hx * jnp.tanh(hx) + hx


def _rms(x, g):
    return x * lax.rsqrt(jnp.mean(x * x, axis=-1, keepdims=True) + EPS) * g


def _dot(a, b, dims):
    return lax.dot_general(a, b, (dims, ((), ())), preferred_element_type=F32)


def _nn(a, b):
    return _dot(a, b, ((1,), (0,)))


def _nt(a, b):
    return _dot(a, b, ((1,), (1,)))


def _tn(a, b):
    return _dot(a, b, ((0,), (0,)))


def _params(*sem):
    return pltpu.CompilerParams(dimension_semantics=sem, vmem_limit_bytes=56 * 1024 * 1024)


def _matmul(a, b, *, ta=False, tb=False, out_dtype=F32, addend=None, after=None, into=None, o_noff=0, out_t=False,
            o_block_perm=lambda j: j, name, tm, tn, tk=None, n_extent=None, b_koff=0, b_noff=0):
    M, K = (a.shape[1], a.shape[0]) if ta else a.shape
    N = n_extent or (b.shape[0] if tb else b.shape[1])
    tm, tn, tk = min(tm, M), min(tn, N), min(tk or K, K)
    assert M % tm == 0 and N % tn == 0 and K % tk == 0, (name, M, N, K, tm, tn, tk)
    nk = K // tk
    use_scratch = nk > 1 and out_dtype != F32
    grid = (M // tm, N // tn, nk)
    a_spec = pl.BlockSpec((tk, tm), lambda i, j, k: (k, i)) if ta else pl.BlockSpec((tm, tk), lambda i, j, k: (i, k))
    b_spec = pl.BlockSpec((tn, tk), lambda i, j, k: (j + b_noff, k + b_koff)) if tb else pl.BlockSpec((tk, tn), lambda i, j, k: (k + b_koff, j + b_noff))
    o_spec = pl.BlockSpec((tm, tn), lambda i, j, k: (i, j))
    dims = ((0 if ta else 1,), (1 if tb else 0,))
    has_add = addend is not None

    n_in = 2 + has_add + (after is not None) + (into is not None)

    def body(*refs):
        a_ref, b_ref = refs[:2]
        c_ref = refs[2] if has_add else None
        o_ref = refs[n_in]
        part = _dot(a_ref[...], b_ref[...], dims)
        if nk == 1:
            if has_add:
                part = part + c_ref[...].astype(F32)
            o_ref[...] = (part.T if out_t else part).astype(out_dtype)
        else:
            acc_ref = refs[-1] if use_scratch else o_ref
            k = pl.program_id(2)

            @pl.when(k == 0)
            def _():
                acc_ref[...] = part + c_ref[...].astype(F32) if has_add else part

            @pl.when(k > 0)
            def _():
                acc_ref[...] += part

            if use_scratch:
                @pl.when(k == nk - 1)
                def _():
                    o_ref[...] = acc_ref[...].astype(out_dtype)

    in_specs = [a_spec, b_spec] + ([o_spec] if has_add else [])
    args = (a, b) + ((addend,) if has_add else ())
    if after is not None:
        in_specs.append(pl.BlockSpec(after.shape, lambda i, j, k: (0, 0)))
        args += (after,)
    aliases = {}
    if into is not None:
        in_specs.append(ANY)
        args += (into,)
        aliases = {len(args) - 1: 0}
    if out_t:
        assert nk == 1 and not has_add
        o_spec = pl.BlockSpec((tn, tm), lambda i, j, k: (o_block_perm(j) + o_noff, i))
    elif into is not None:
        o_spec = pl.BlockSpec((tm, tn), lambda i, j, k: (i, j + o_noff))
    return pl.pallas_call(
        body,
        name=name,
        grid=grid,
        in_specs=in_specs,
        out_specs=o_spec,
        out_shape=jax.ShapeDtypeStruct(into.shape if into is not None else ((N, M) if out_t else (M, N)), out_dtype),
        input_output_aliases=aliases,
        scratch_shapes=[pltpu.VMEM((tm, tn), F32)] if use_scratch else [],
        compiler_params=_params("parallel", "parallel", "arbitrary"),
    )(*args)


def _matmul_ep(pairs, *, tm, ins, in_specs, out_shapes, out_specs, sums=(), epilogue, aliases=None, name):
    M = pairs[0][0].shape[0]
    tm = min(tm, M)
    mm_specs, mm_args, dims = [], [], []
    for a, b, tb, koff in pairs:
        K = a.shape[1]
        N = b.shape[0] if tb else b.shape[1]
        mm_specs += [pl.BlockSpec((tm, K), lambda i: (i, 0)),
                     pl.BlockSpec((N, K), lambda i, koff=koff: (0, koff)) if tb else pl.BlockSpec((K, N), lambda i, koff=koff: (koff, 0))]
        mm_args += [a, b]
        dims.append(((1,), (1 if tb else 0,)))
    n_mm = len(mm_args)
    n_in = n_mm + len(ins)

    def body(*refs):
        in_refs, out_refs = refs[n_mm:n_in], refs[n_in:]
        accs = [_dot(refs[2 * p][...], refs[2 * p + 1][...], dims[p]) for p in range(len(pairs))]
        outs = epilogue(*accs, *in_refs)
        for k, (ref, val) in enumerate(zip(out_refs, outs)):
            if val is None:
                continue
            if k in sums:
                @pl.when(pl.program_id(0) == 0)
                def _():
                    ref[...] = jnp.zeros_like(ref)

                ref[...] += val
            else:
                ref[...] = val.astype(ref.dtype)

    return pl.pallas_call(
        body, name=name, grid=(M // tm,),
        in_specs=mm_specs + list(in_specs),
        out_specs=list(out_specs), out_shape=list(out_shapes),
        input_output_aliases={n_mm + k: v for k, v in (aliases or {}).items()},
        compiler_params=_params("arbitrary"),
    )(*mm_args, *ins)


def _row_spec(tm, n):
    return pl.BlockSpec((tm, n), lambda i: (i, 0))


def _full_spec(shape):
    return pl.BlockSpec(shape, lambda i: tuple(0 for _ in shape))


def _norm_cast(x, g, *, name):
    T, D = x.shape
    tm = _pick(T, (512, 256, 128))

    def body(x_ref, g_ref, u_ref):
        u_ref[...] = _rms(x_ref[...], g_ref[...]).astype(BF16)

    return pl.pallas_call(
        body, name=name, grid=(T // tm,),
        in_specs=[_row_spec(tm, D), _full_spec((1, D))],
        out_specs=_row_spec(tm, D),
        out_shape=jax.ShapeDtypeStruct((T, D), BF16),
        compiler_params=_params("parallel"),
    )(x, g)


def _norm_bwd_add(x, g, du, dres, *, with_bf16=True, name):
    T, D = x.shape
    tm = _pick(T, (512, 256, 128))

    def body(x_ref, g_ref, du_ref, dr_ref, dx_ref, *rest):
        dg_ref = rest[-1]
        _, vjp = jax.vjp(_rms, x_ref[...], g_ref[...])
        dx, dg = vjp(du_ref[...].astype(F32))
        dx = dx + dr_ref[...]
        dx_ref[...] = dx
        if with_bf16:
            rest[0][...] = dx.astype(BF16)

        @pl.when(pl.program_id(0) == 0)
        def _():
            dg_ref[...] = jnp.zeros_like(dg_ref)

        dg_ref[...] += dg

    row = _row_spec(tm, D)
    return pl.pallas_call(
        body, name=name, grid=(T // tm,),
        in_specs=[row, _full_spec((1, D)), row, row],
        out_specs=[row] + ([row] if with_bf16 else []) + [_full_spec((1, D))],
        out_shape=[jax.ShapeDtypeStruct((T, D), F32)] + ([jax.ShapeDtypeStruct((T, D), BF16)] if with_bf16 else []) + [jax.ShapeDtypeStruct((1, D), F32)],
        compiler_params=_params("arbitrary"),
    )(x, g, du, dres)


def _merge_fn(gates, a, b):
    ga = gates[:, :D_MODEL].astype(F32)
    gb = gates[:, D_MODEL:].astype(F32)
    return _sigmoid(ga) * a.astype(F32) + _sigmoid(gb) * b.astype(F32)


def _gates_spec(tm):
    return pl.BlockSpec((tm, W_GATES), lambda i: (i, O_GATES // W_GATES))


CONV_TC = 256


def _shift_down(x, n, rows):
    return jnp.where(rows >= n, pltpu.roll(x, n, 0), 0.0)


def _shift_up(x, n, rows, S):
    return jnp.where(rows < S - n, pltpu.roll(x, S - n, 0), 0.0)


def _conv_act_fwd(gu, conv_w, conv_b, *, name):
    B, S, _ = gu.shape
    tc = CONV_TC
    nc = D_FF // tc

    def body(g_ref, up_ref, w_ref, b_ref, o_ref, a_ref):
        g = g_ref[...].astype(F32)
        rows = lax.broadcasted_iota(jnp.int32, g.shape, 0)
        w = w_ref[...]
        a = w[2:3] * g + w[1:2] * _shift_down(g, 1, rows) + w[0:1] * _shift_down(g, 2, rows) + b_ref[...]
        o_ref[...] = (_silu(a) * up_ref[...].astype(F32)).astype(BF16)
        a_ref[...] = a.astype(BF16)

    col = pl.BlockSpec((None, S, tc), lambda b, j: (b, 0, j))
    return pl.pallas_call(
        body, name=name, grid=(B, nc),
        in_specs=[col,
                  pl.BlockSpec((None, S, tc), lambda b, j: (b, 0, j + nc)),
                  pl.BlockSpec((3, tc), lambda b, j: (0, j)),
                  pl.BlockSpec((1, tc), lambda b, j: (0, j))],
        out_specs=[col, col],
        out_shape=[jax.ShapeDtypeStruct((B, S, D_FF), BF16)] * 2,
        compiler_params=_params("parallel", "parallel"),
    )(gu, gu, conv_w, conv_b)


def _conv_act_bwd(gu, a_pre, conv_w, dact, *, name):
    B, S, _ = gu.shape
    tc = CONV_TC
    nc = D_FF // tc

    def body(g_ref, up_ref, a_ref, w_ref, da_ref, dg_ref, dup_ref, dw_ref, db_ref):
        g = g_ref[...].astype(F32)
        up, a, dact = up_ref[...], a_ref[...], da_ref[...]
        rows = lax.broadcasted_iota(jnp.int32, g.shape, 0)
        w = w_ref[...]
        sg = _sigmoid(a)
        dup_ref[...] = dact * a * sg
        da = (dact * up * sg * (1.0 + a * (1.0 - sg))).astype(F32)
        da1 = _shift_up(da, 1, rows, S)
        da2 = _shift_up(da, 2, rows, S)
        dg_ref[...] = (w[2:3] * da + w[1:2] * da1 + w[0:1] * da2).astype(BF16)

        @pl.when(pl.program_id(1) == 0)
        def _():
            dw_ref[...] = jnp.zeros_like(dw_ref)
            db_ref[...] = jnp.zeros_like(db_ref)

        dw_ref[0:1, :] += jnp.sum(da2 * g, axis=0, keepdims=True)
        dw_ref[1:2, :] += jnp.sum(da1 * g, axis=0, keepdims=True)
        dw_ref[2:3, :] += jnp.sum(da * g, axis=0, keepdims=True)
        db_ref[...] += jnp.sum(da, axis=0, keepdims=True)

    col = pl.BlockSpec((None, S, tc), lambda j, b: (b, 0, j))
    return pl.pallas_call(
        body, name=name, grid=(nc, B),
        in_specs=[col,
                  pl.BlockSpec((None, S, tc), lambda j, b: (b, 0, j + nc)),
                  col,
                  pl.BlockSpec((3, tc), lambda j, b: (0, j)),
                  col],
        out_specs=[col, col, pl.BlockSpec((3, tc), lambda j, b: (0, j)), pl.BlockSpec((1, tc), lambda j, b: (0, j))],
        out_shape=[jax.ShapeDtypeStruct((B, S, D_FF), BF16), jax.ShapeDtypeStruct((B, S, D_FF), BF16),
                   jax.ShapeDtypeStruct((3, D_FF), F32), jax.ShapeDtypeStruct((1, D_FF), F32)],
        compiler_params=_params("parallel", "arbitrary"),
    )(gu, gu, a_pre, conv_w, dact)


HGRN_CPB = 8
HF = HGRN_HEADS * HGRN_DK


def _tri(n, upper=False):
    r = lax.broadcasted_iota(jnp.int32, (n, n), 0)
    c = lax.broadcasted_iota(jnp.int32, (n, n), 1)
    return (c >= r) if upper else (r >= c)


def _hs(h):
    return slice(h * HGRN_DK, (h + 1) * HGRN_DK)


def _cumsum_rows(tri_b, x):
    hi = x.astype(BF16)
    lo = (x - hi.astype(F32)).astype(BF16)
    return _nn(tri_b, hi) + _nn(tri_b, lo)


def _hgrn_pre(q, fz, lb, tril_b):
    sq = _sigmoid(q)
    qf = q * sq
    sg = _sigmoid(fz)
    f = lb + (1.0 - lb) * sg
    k = 1.0 - f
    b = _cumsum_rows(tril_b, jnp.log2(f))
    bref = b[CHUNK // 2:CHUNK // 2 + 1, :]
    blast = b[CHUNK - 1:CHUNK, :]
    e1 = jnp.exp2(b - bref)
    e2 = jnp.exp2(bref - b)
    e3 = e1 * jnp.exp2(bref)
    e4 = e2 * jnp.exp2(blast - bref)
    dec = jnp.exp2(blast)
    return (sq, sg), f, (e1, e2, e3, e4), qf * e1, k * e2, qf * e3, k * e4, dec


def _hgrn_fwd(zh, lb, gn, *, name):
    B, S, _ = zh.shape
    cpb = HGRN_CPB
    ts = cpb * CHUNK
    nblk = S // ts

    def body(z_ref, lb_ref, gn_ref, o_ref, st_ref, state):
        @pl.when(pl.program_id(1) == 0)
        def _():
            state[...] = jnp.zeros_like(state)

        H = HGRN_HEADS
        causal = _tri(CHUNK)
        tril_b = causal.astype(BF16)
        lb = lb_ref[...]
        for c in range(cpb):
            rows = slice(c * CHUNK, (c + 1) * CHUNK)
            q = z_ref[rows, 0:HF].astype(F32)
            fz = z_ref[rows, HF:2 * HF].astype(F32)
            v = z_ref[rows, 2 * HF:3 * HF]
            hg = z_ref[rows, 3 * HF:4 * HF].astype(F32)
            _, _, _, q_in, k_in, q_out, k_st, dec = _hgrn_pre(q, fz, lb, tril_b)
            q_in, k_in, q_out, k_st = (t.astype(BF16) for t in (q_in, k_in, q_out, k_st))
            a = [jnp.where(causal, _nt(q_in[:, _hs(h)], k_in[:, _hs(h)]), 0.0).astype(BF16) for h in range(H)]
            st = [state[h] for h in range(H)]
            for h in range(H):
                st_ref[c, h] = st[h]
            o = [_nn(a[h], v[:, _hs(h)]) + _nt(q_out[:, _hs(h)], st[h].astype(BF16)) for h in range(H)]
            for h in range(H):
                state[h] = st[h] * dec[:, _hs(h)] + _tn(v[:, _hs(h)], k_st[:, _hs(h)])
            gate = _silu(hg)
            for h in range(H):
                o_ref[rows, _hs(h)] = (_rms(o[h], gn_ref[...]) * gate[:, _hs(h)]).astype(BF16)

    return pl.pallas_call(
        body, name=name, grid=(B, nblk),
        in_specs=[pl.BlockSpec((None, ts, 4 * HF), lambda b, s: (b, s, 0)),
                  pl.BlockSpec((1, HF), lambda b, s: (0, 0)),
                  pl.BlockSpec((1, HGRN_DK), lambda b, s: (0, 0))],
        out_specs=[pl.BlockSpec((None, ts, HF), lambda b, s: (b, s, 0)),
                   pl.BlockSpec((None, cpb, HGRN_HEADS, HGRN_DK, HGRN_DK), lambda b, s: (b, s, 0, 0, 0))],
        out_shape=[jax.ShapeDtypeStruct((B, S, HF), BF16),
                   jax.ShapeDtypeStruct((B, S // CHUNK, HGRN_HEADS, HGRN_DK, HGRN_DK), F32)],
        scratch_shapes=[pltpu.VMEM((HGRN_HEADS, HGRN_DK, HGRN_DK), F32)],
        compiler_params=_params("arbitrary", "arbitrary"),
    )(zh, lb, gn)


def _hgrn_bwd(zh, lb, gn, states, doa, dz, *, name):
    B, S, _ = zh.shape
    cpb = HGRN_CPB
    ts = cpb * CHUNK
    nblk = S // ts
    rev = lambda b, s: (b, nblk - 1 - s, 0)

    def body(z_ref, lb_ref, gn_ref, st_ref, do_ref, dz_in, dz_ref, dlb_ref, dgn_ref, dstate):
        @pl.when(pl.program_id(1) == 0)
        def _():
            dstate[...] = jnp.zeros_like(dstate)

        @pl.when((pl.program_id(0) == 0) & (pl.program_id(1) == 0))
        def _():
            dlb_ref[...] = jnp.zeros_like(dlb_ref)
            dgn_ref[...] = jnp.zeros_like(dgn_ref)

        H = HGRN_HEADS
        cat = lambda xs: jnp.concatenate(xs, axis=1)
        causal = _tri(CHUNK)
        tril_b = causal.astype(BF16)
        triu_b = _tri(CHUNK, upper=True).astype(BF16)
        rowid = lax.broadcasted_iota(jnp.int32, (CHUNK, HF), 0)
        lb = lb_ref[...]
        gn = gn_ref[...]
        for c in reversed(range(cpb)):
            rows = slice(c * CHUNK, (c + 1) * CHUNK)
            q = z_ref[rows, 0:HF].astype(F32)
            fz = z_ref[rows, HF:2 * HF].astype(F32)
            v = z_ref[rows, 2 * HF:3 * HF]
            hg = z_ref[rows, 3 * HF:4 * HF].astype(F32)
            (sq, sg), f, (e1, e2, e3, e4), q_in, k_in, q_out, k_st, dec = _hgrn_pre(q, fz, lb, tril_b)
            q_in_b, k_in_b, q_out_b, k_st_b = (t.astype(BF16) for t in (q_in, k_in, q_out, k_st))
            a_b = [jnp.where(causal, _nt(q_in_b[:, _hs(h)], k_in_b[:, _hs(h)]), 0.0).astype(BF16) for h in range(H)]
            st = [st_ref[c, h] for h in range(H)]
            st_b = [t.astype(BF16) for t in st]
            o = [_nn(a_b[h], v[:, _hs(h)]) + _nt(q_out_b[:, _hs(h)], st_b[h]) for h in range(H)]
            dout = do_ref[rows, :].astype(F32)
            shg = _sigmoid(hg)
            gate = hg * shg
            do_l, on_l, dgn_acc = [], [], jnp.zeros_like(gn)
            for h in range(H):
                on_h, norm_vjp = jax.vjp(_rms, o[h], gn)
                d_o, d_gn = norm_vjp(dout[:, _hs(h)] * gate[:, _hs(h)])
                do_l.append(d_o)
                on_l.append(on_h)
                dgn_acc = dgn_acc + d_gn
            dgn_ref[...] += dgn_acc
            dhg = dout * cat(on_l) * shg * (1.0 + hg * (1.0 - shg))
            do_b = [t.astype(BF16) for t in do_l]
            dst = [dstate[h] for h in range(H)]
            dst_b = [t.astype(BF16) for t in dst]
            da_b = [jnp.where(causal, _nt(do_b[h], v[:, _hs(h)]), 0.0).astype(BF16) for h in range(H)]
            dv = cat([_tn(a_b[h], do_b[h]) + _nt(k_st_b[:, _hs(h)], dst_b[h]) for h in range(H)])
            dq_in = cat([_nn(da_b[h], k_in_b[:, _hs(h)]) for h in range(H)])
            dk_in = cat([_tn(da_b[h], q_in_b[:, _hs(h)]) for h in range(H)])
            dq_out = cat([_nn(do_b[h], st_b[h]) for h in range(H)])
            dk_st = cat([_nn(v[:, _hs(h)], dst_b[h]) for h in range(H)])
            ddec = cat([jnp.sum(st[h] * dst[h], axis=0, keepdims=True) for h in range(H)])
            for h in range(H):
                dstate[h] = dst[h] * dec[:, _hs(h)] + _tn(do_b[h], q_out_b[:, _hs(h)])
            t_qin = dq_in * q_in
            t_kin = dk_in * k_in
            t_kst = dk_st * k_st
            db = t_qin - t_kin + dq_out * q_out - t_kst
            dbref = jnp.sum(t_kin - t_qin, axis=0, keepdims=True)
            dblast = jnp.sum(t_kst, axis=0, keepdims=True) + ddec * dec
            db = db + jnp.where(rowid == CHUNK // 2, dbref, 0.0) + jnp.where(rowid == CHUNK - 1, dblast, 0.0)
            dlogf = _cumsum_rows(triu_b, db)
            dqf = dq_in * e1 + dq_out * e3
            dk = dk_in * e2 + dk_st * e4
            df_open = (dlogf / f - dk) * (1.0 - sg)
            dfz = df_open * ((1.0 - lb) * sg)
            dlb_ref[...] += jnp.sum(df_open, axis=0, keepdims=True)
            dq = dqf * sq * (1.0 + q * (1.0 - sq))
            dz_ref[rows, 0:HF] = dq.astype(BF16)
            dz_ref[rows, HF:2 * HF] = dfz.astype(BF16)
            dz_ref[rows, 2 * HF:3 * HF] = dv.astype(BF16)
            dz_ref[rows, 3 * HF:4 * HF] = dhg.astype(BF16)

    return pl.pallas_call(
        body, name=name, grid=(B, nblk),
        in_specs=[pl.BlockSpec((None, ts, 4 * HF), rev),
                  pl.BlockSpec((1, HF), lambda b, s: (0, 0)),
                  pl.BlockSpec((1, HGRN_DK), lambda b, s: (0, 0)),
                  pl.BlockSpec((None, cpb, HGRN_HEADS, HGRN_DK, HGRN_DK), lambda b, s: (b, nblk - 1 - s, 0, 0, 0)),
                  pl.BlockSpec((None, ts, HF), rev),
                  ANY],
        out_specs=[pl.BlockSpec((None, ts, 4 * HF), rev),
                   pl.BlockSpec((1, HF), lambda b, s: (0, 0)),
                   pl.BlockSpec((1, HGRN_DK), lambda b, s: (0, 0))],
        out_shape=[jax.ShapeDtypeStruct(dz.shape, BF16),
                   jax.ShapeDtypeStruct((1, HF), F32),
                   jax.ShapeDtypeStruct((1, HGRN_DK), F32)],
        input_output_aliases={5: 0},
        scratch_shapes=[pltpu.VMEM((HGRN_HEADS, HGRN_DK, HGRN_DK), F32)],
        compiler_params=_params("arbitrary", "arbitrary"),
    )(zh, lb, gn, states, doa, dz)


KV_W = ATT_KV_HEADS * ATT_HD
ATT_SCALE = ATT_HD ** -0.5


def _rope(x, cos, sin, inverse=False):
    half = ROPE_DIM // 2
    outs = []
    for p in range(x.shape[1] // 128):
        xp = x[:, p * 128:(p + 1) * 128]
        lane = lax.broadcasted_iota(jnp.int32, xp.shape, 1) % ATT_HD
        sw = jnp.where(lane < half, pltpu.roll(xp, 128 - half, 1), pltpu.roll(xp, half, 1))
        outs.append(xp * cos - sw * sin if inverse else xp * cos + sw * sin)
    return outs[0] if len(outs) == 1 else jnp.concatenate(outs, axis=1)


PAIRS_PER_KV = ATT_GROUP // 2


def _swap_halves(x):
    return pltpu.roll(x, ATT_HD, 1)


def _kv_padded(t, low):
    sw = _swap_halves(t)
    zero = jnp.zeros_like(t)
    out = []
    for g in range(ATT_KV_HEADS):
        in_low, in_high = (t, sw) if g == 0 else (sw, t)
        out.append((jnp.where(low, in_low, zero).astype(BF16), jnp.where(low, zero, in_high).astype(BF16)))
    return out


def _swa_mask(first_block):
    qi = lax.broadcasted_iota(jnp.int32, (WINDOW, 2 * WINDOW), 0)
    mi = lax.broadcasted_iota(jnp.int32, (WINDOW, 2 * WINDOW), 1)
    band = (mi > qi) & (mi <= qi + WINDOW)
    return band & (jnp.logical_not(first_block) | (mi >= WINDOW))


def _swa_specs(nb):
    cur = lambda b, i: (b, i, 0)
    prev = lambda b, i: (b, jnp.maximum(i - 1, 0), 0)
    return cur, prev


def _swa_z_specs():
    q = pl.BlockSpec((None, WINDOW, W_AQ), lambda b, i: (b, i, O_AQ // W_AQ))
    kv_prev = pl.BlockSpec((None, WINDOW, W_AKV), lambda b, i: (b, jnp.maximum(i - 1, 0), O_AKV // W_AKV))
    kv_cur = pl.BlockSpec((None, WINDOW, W_AKV), lambda b, i: (b, i, O_AKV // W_AKV))
    return q, kv_prev, kv_cur


def _swa_fwd(z, cos, sin, sinks, *, name):
    B, S, _ = z.shape
    nb = S // WINDOW
    cur, prev = _swa_specs(nb)

    def body(q_ref, kvp_ref, kvc_ref, cp_ref, sp_ref, cc_ref, sc_ref, sink_ref, o_ref, lse_ref, qr_ref, kr_ref):
        cos_c, sin_c = cc_ref[...], sc_ref[...]
        q = (_rope(q_ref[...].astype(F32), cos_c, sin_c) * ATT_SCALE).astype(BF16)
        k = jnp.concatenate([_rope(kvp_ref[:, :KV_W].astype(F32), cp_ref[...], sp_ref[...]),
                             _rope(kvc_ref[:, :KV_W].astype(F32), cos_c, sin_c)], axis=0)
        qr_ref[...] = q
        kr_ref[...] = k[WINDOW:].astype(BF16)
        v = jnp.concatenate([kvp_ref[:, KV_W:], kvc_ref[:, KV_W:]], axis=0).astype(F32)
        low = lax.broadcasted_iota(jnp.int32, k.shape, 1) < ATT_HD
        kpad = _kv_padded(k, low)
        vpad = _kv_padded(v, low)
        mask = _swa_mask(pl.program_id(1) == 0)
        lses = []
        for g in range(ATT_KV_HEADS):
            pairs = range(g * PAIRS_PER_KV, (g + 1) * PAIRS_PER_KV)
            keys = [(p, e) for p in pairs for e in (0, 1)]
            qp = {p: q[:, p * 128:(p + 1) * 128] for p in pairs}
            s = {pe: jnp.where(mask, _nt(qp[pe[0]], kpad[g][pe[1]]), NEG_INF) for pe in keys}
            pr = {}
            for pe in keys:
                sink = sink_ref[0, 2 * pe[0] + pe[1]]
                m = jnp.maximum(jnp.max(s[pe], axis=1, keepdims=True), sink)
                ex = jnp.exp(s[pe] - m)
                den = jnp.sum(ex, axis=1, keepdims=True) + jnp.exp(sink - m)
                pr[pe] = (ex * (1.0 / den)).astype(BF16)
                lses.append(m + jnp.log(den))
            for p in pairs:
                o_ref[:, p * 128:(p + 1) * 128] = (_nn(pr[p, 0], vpad[g][0]) + _nn(pr[p, 1], vpad[g][1])).astype(BF16)
        lse_ref[...] = jnp.concatenate(lses, axis=1)

    tab = lambda im: pl.BlockSpec((None, WINDOW, 128), im)
    return pl.pallas_call(
        body, name=name, grid=(B, nb),
        in_specs=[*_swa_z_specs(),
                  tab(prev), tab(prev), tab(cur), tab(cur),
                  pl.BlockSpec(memory_space=pltpu.SMEM)],
        out_specs=[pl.BlockSpec((None, WINDOW, D_MODEL), cur), pl.BlockSpec((None, WINDOW, ATT_HEADS), cur),
                   pl.BlockSpec((None, WINDOW, D_MODEL), cur), pl.BlockSpec((None, WINDOW, KV_W), cur)],
        out_shape=[jax.ShapeDtypeStruct((B, S, D_MODEL), BF16), jax.ShapeDtypeStruct((B, S, ATT_HEADS), F32),
                   jax.ShapeDtypeStruct((B, S, D_MODEL), BF16), jax.ShapeDtypeStruct((B, S, KV_W), BF16)],
        compiler_params=_params("parallel", "parallel"),
    )(z, z, z, cos, sin, cos, sin, sinks)


def _swa_bwd(z, qr, kr, cos, sin, sinks, lse, dob, dz, *, name):
    B, S, _ = z.shape
    nb = S // WINDOW
    cur, prev = _swa_specs(nb)

    def body(q_ref, krp_ref, krc_ref, kvp_ref, kvc_ref, cp_ref, sp_ref, cc_ref, sc_ref, sink_ref, lse_ref, do_ref, dz_in,
             dq_ref, dkc_ref, dkp_ref, dsink_ref):
        @pl.when((pl.program_id(0) == 0) & (pl.program_id(1) == 0))
        def _():
            dsink_ref[...] = jnp.zeros_like(dsink_ref)

        cos_c, sin_c, cos_p, sin_p = cc_ref[...], sc_ref[...], cp_ref[...], sp_ref[...]
        q = q_ref[...]
        k = jnp.concatenate([krp_ref[...], krc_ref[...]], axis=0).astype(F32)
        v = jnp.concatenate([kvp_ref[:, KV_W:], kvc_ref[:, KV_W:]], axis=0).astype(F32)
        low = lax.broadcasted_iota(jnp.int32, k.shape, 1) < ATT_HD
        kpad = _kv_padded(k, low)
        vpad = _kv_padded(v, low)
        mask = _swa_mask(pl.program_id(1) == 0)
        lse = lse_ref[...]
        dq_parts, dk_sum, dv_sum, dsinks = [], [], [], []
        for g in range(ATT_KV_HEADS):
            pairs = range(g * PAIRS_PER_KV, (g + 1) * PAIRS_PER_KV)
            keys = [(p, e) for p in pairs for e in (0, 1)]
            qp = {p: q[:, p * 128:(p + 1) * 128] for p in pairs}
            dop = {p: do_ref[:, p * 128:(p + 1) * 128] for p in pairs}
            s = {pe: jnp.where(mask, _nt(qp[pe[0]], kpad[g][pe[1]]), NEG_INF) for pe in keys}
            dp = {pe: _nt(dop[pe[0]], vpad[g][pe[1]]) for pe in keys}
            pr, ds = {}, {}
            for pe in keys:
                h = 2 * pe[0] + pe[1]
                lse_h = lse[:, h:h + 1]
                pf = jnp.exp(s[pe] - lse_h)
                delta = jnp.sum(pf * dp[pe], axis=1, keepdims=True)
                ds[pe] = (pf * (dp[pe] - delta)).astype(BF16)
                pr[pe] = pf.astype(BF16)
                p_sink = jnp.exp(sink_ref[0, h] - lse_h)
                dsinks.append(-jnp.sum(p_sink * delta, axis=0, keepdims=True))
            for p in pairs:
                dq_parts.append((_nn(ds[p, 0], kpad[g][0]) + _nn(ds[p, 1], kpad[g][1])) * ATT_SCALE)
            x = [sum(_tn(ds[p, e], qp[p]) for p in pairs) for e in (0, 1)]
            y = [sum(_tn(pr[p, e], dop[p]) for p in pairs) for e in (0, 1)]
            zk = jnp.where(low, x[0], x[1])
            zv = jnp.where(low, y[0], y[1])
            dk_sum.append(zk + _swap_halves(zk))
            dv_sum.append(zv + _swap_halves(zv))
        dq_ref[...] = _rope(jnp.concatenate(dq_parts, axis=1), cos_c, sin_c, inverse=True).astype(BF16)
        dk = jnp.where(low, dk_sum[0], dk_sum[1])
        dv = jnp.where(low, dv_sum[0], dv_sum[1])
        dkp_ref[:, :KV_W] = _rope(dk[:WINDOW], cos_p, sin_p, inverse=True)
        dkp_ref[:, KV_W:] = dv[:WINDOW]
        dkc_ref[:, :KV_W] = _rope(dk[WINDOW:], cos_c, sin_c, inverse=True)
        dkc_ref[:, KV_W:] = dv[WINDOW:]
        dsink_ref[...] += jnp.concatenate(dsinks, axis=1)

    tab = lambda im: pl.BlockSpec((None, WINDOW, 128), im)
    return pl.pallas_call(
        body, name=name, grid=(B, nb),
        in_specs=[pl.BlockSpec((None, WINDOW, D_MODEL), cur), tab(prev), tab(cur),
                  *_swa_z_specs()[1:],
                  tab(prev), tab(prev), tab(cur), tab(cur),
                  pl.BlockSpec(memory_space=pltpu.SMEM),
                  pl.BlockSpec((None, WINDOW, ATT_HEADS), cur),
                  pl.BlockSpec((None, WINDOW, D_MODEL), cur),
                  ANY],
        out_specs=[_swa_z_specs()[0],
                   pl.BlockSpec((None, WINDOW, 2 * KV_W), cur), pl.BlockSpec((None, WINDOW, 2 * KV_W), cur),
                   pl.BlockSpec((1, ATT_HEADS), lambda b, i: (0, 0))],
        out_shape=[jax.ShapeDtypeStruct(dz.shape, BF16),
                   jax.ShapeDtypeStruct((B, S, 2 * KV_W), F32), jax.ShapeDtypeStruct((B, S, 2 * KV_W), F32),
                   jax.ShapeDtypeStruct((1, ATT_HEADS), F32)],
        input_output_aliases={12: 0},
        compiler_params=_params("arbitrary", "arbitrary"),
    )(qr, kr, kr, z, z, cos, sin, cos, sin, sinks, lse, dob, dz)


def _swa_dkv_combine(dkv_cur, dkv_prev, dz, *, name):
    B, S, W = dkv_cur.shape

    def body(c_ref, p_ref, dz_in, o_ref):
        rows = lax.broadcasted_iota(jnp.int32, (S, W), 0)
        o_ref[...] = (c_ref[...] + _shift_up(p_ref[...], WINDOW, rows, S)).astype(BF16)

    seq = pl.BlockSpec((None, S, W), lambda b: (b, 0, 0))
    return pl.pallas_call(
        body, name=name, grid=(B,),
        in_specs=[seq, seq, ANY], out_specs=pl.BlockSpec((None, S, W), lambda b: (b, 0, O_AKV // W_AKV)),
        out_shape=jax.ShapeDtypeStruct(dz.shape, BF16),
        input_output_aliases={2: 0},
        compiler_params=_params("parallel"),
    )(dkv_cur, dkv_prev, dz)


def _rope_tables(positions):
    half = ROPE_DIM // 2
    inv = ROPE_THETA ** (-2.0 * jnp.arange(half, dtype=F32) / ROPE_DIM)
    ang = positions.astype(F32)[..., None] * inv
    c, s = jnp.cos(ang), jnp.sin(ang)
    pad = jnp.zeros(ang.shape[:-1] + (ATT_HD - ROPE_DIM,), F32)
    cos = jnp.concatenate([c, c, pad + 1.0], axis=-1)
    sin = jnp.concatenate([-s, s, pad], axis=-1)
    return jnp.tile(cos, (1, 1, 2)), jnp.tile(sin, (1, 1, 2))


def _lower_bound(lb_logits, *, name):
    def body(l_ref, o_ref):
        l = l_ref[...]
        e = jnp.exp(l - jnp.max(l, axis=0, keepdims=True))
        o_ref[...] = e[0:1] / jnp.sum(e, axis=0, keepdims=True)

    return pl.pallas_call(body, name=name, out_shape=jax.ShapeDtypeStruct((1, lb_logits.shape[1]), F32))(lb_logits)


W_ZH, W_GATES, W_AQ, W_AKV = 4 * HF, 2 * D_MODEL, ATT_HEADS * ATT_HD, 2 * KV_W
O_ZH, O_GATES, O_AQ, O_AKV = 0, W_ZH, W_ZH + W_GATES, W_ZH + W_GATES + W_AQ
W_IN = W_ZH + W_GATES + W_AQ + W_AKV


W_IN_BLK = W_IN // N_DEV


def _reference_row_block(j, rows=256):
    nz, ng = W_ZH // rows, W_GATES // rows
    return jnp.where(j < nz, j, jnp.where(j < nz + ng, j + (W_AQ + W_AKV) // rows, j - ng))


def _reordered_rows(w_t, *, name):
    rows = 256

    def body(i_ref, o_ref):
        o_ref[...] = i_ref[...]

    return pl.pallas_call(
        body, name=name, grid=(W_IN // rows,),
        in_specs=[pl.BlockSpec((rows, D_MODEL), lambda j: (_reference_row_block(j, rows), 0))],
        out_specs=pl.BlockSpec((rows, D_MODEL), lambda j: (j, 0)),
        out_shape=jax.ShapeDtypeStruct(w_t.shape, w_t.dtype), compiler_params=_params("parallel"))(w_t)


def _local_step(x, positions, target, small, w_in_t, rest_weights, emit, start_token):
    B, S, D = x.shape
    T = B * S
    x2 = x.reshape(T, D)
    cos, sin = _rope_tables(positions)
    lb = _lower_bound(small["lb_logits"], name="lb_fwd")
    zero = lambda tok: tok[0:1, 0:1]

    u1 = _norm_cast(x2, small["norm1_g"] + zero(start_token), name="norm1")
    z = _matmul(u1, w_in_t, tb=True, out_dtype=BF16, name="mm_z", tm=1024, tn=W_IN // 2)
    z3 = z.reshape(B, S, W_IN)
    oa, states = _hgrn_fwd(z3, lb, small["hgrn_norm_g"], name="hgrn_fwd")
    ob, lse, qr, kr = _swa_fwd(z3, cos, sin, small["attn_sinks"], name="swa_fwd")
    oa2 = oa.reshape(T, D)
    ob2 = ob.reshape(T, D)
    W = rest_weights("mix", ob)
    row = lambda tm, dtype=None: _row_spec(tm, D)
    tile = lambda dtype: jax.ShapeDtypeStruct((T, D), dtype)
    vec = _full_spec((1, D))
    vec_shape = jax.ShapeDtypeStruct((1, D), F32)

    def merge_ep(acc_a, acc_b, g_ref):
        pa, pb = acc_a.astype(BF16), acc_b.astype(BF16)
        return pa, pb, _merge_fn(g_ref[...], pa, pb)

    pa, pb, merged = _matmul_ep([(oa2, W["w_a"], False, 0), (ob2, W["w_b"], False, 0)], tm=1024, ins=[z], in_specs=[_gates_spec(1024)],
                                out_shapes=[tile(BF16)] * 3, out_specs=[row(1024)] * 3, epilogue=merge_ep, name="mm_pa_pb_merge")

    def resid_norm_ep(acc, x_ref, g_ref):
        hh = acc + x_ref[...]
        return hh, _rms(hh, g_ref[...])

    h, u2 = _matmul_ep([(merged, W["w_out"], False, 0)], tm=1024, ins=[x2, small["norm2_g"]], in_specs=[row(1024), vec],
                       out_shapes=[tile(F32), tile(BF16)], out_specs=[row(1024), row(1024)], epilogue=resid_norm_ep, name="mm_h_norm2")
    W.update(rest_weights("ffn", u2))
    gu = _matmul(u2, W["w_ffn_t"], tb=True, out_dtype=BF16, name="mm_gu", tm=1024, tn=D_FF)
    gu3 = gu.reshape(B, S, 2 * D_FF)
    act, a_pre = _conv_act_fwd(gu3, W["conv_w"], small["conv_b"], name="conv_act_fwd")
    act2 = act.reshape(T, D_FF)
    g = {}

    def loss_ep(acc, h_ref, g_ref, t_ref):
        y, vjp = jax.vjp(_rms, acc + h_ref[...], g_ref[...])
        err = y - t_ref[...]
        dx, dg = vjp(err * (1.0 / D))
        return dx, dx, dg, (0.5 / D) * jnp.sum(jnp.sum(err * err, axis=1, keepdims=True), axis=0, keepdims=True)

    dh2, dh2b, g["final_g"], loss = _matmul_ep(
        [(act2, W["w_down"], False, 0)], tm=512, ins=[h, small["final_g"].reshape(1, D), target.reshape(T, D)], in_specs=[row(512), vec, row(512)],
        out_shapes=[tile(F32), tile(BF16), vec_shape, jax.ShapeDtypeStruct((1, 1), F32)],
        out_specs=[row(512), row(512), vec, _full_spec((1, 1))], sums=(2, 3), epilogue=loss_ep, name="mm_h2_loss")
    dact = _matmul(dh2b, W["w_down"], tb=True, out_dtype=BF16, name="mm_dact", tm=1024, tn=D_FF)
    dw_down_t = _matmul(dh2b, act2, ta=True, out_dtype=BF16, name="mm_dw_down", tm=1024, tn=256, tk=8192)
    dg_, dup, g["conv_w"], g["conv_b"] = _conv_act_bwd(gu3, a_pre, W["conv_w"], dact.reshape(B, S, D_FF), name="conv_act_bwd")
    dg2 = dg_.reshape(T, D_FF)
    dup2 = dup.reshape(T, D_FF)
    dw_ffn_t = _matmul(u2, dg2, ta=True, out_t=True, out_dtype=BF16, into=lax.empty((2 * D_FF, D), BF16), o_noff=0, name="mm_dw_ffn_g", tm=1024, tn=256, tk=8192)
    dw_ffn_t = _matmul(u2, dup2, ta=True, out_t=True, out_dtype=BF16, into=dw_ffn_t, o_noff=D_FF // 256, name="mm_dw_ffn_u", tm=1024, tn=256, tk=8192)
    tok = emit("ffn", dict(w_ffn_t=dw_ffn_t, w_down=dw_down_t.T))
    def norm2_bwd_ep(acc_g, acc_u, h_ref, g_ref, dh2_ref):
        _, vjp = jax.vjp(_rms, h_ref[...], g_ref[...])
        dx, dg = vjp(acc_g + acc_u)
        dx = dx + dh2_ref[...]
        return dx, dx, dg

    dh, dhb, g["norm2_g"] = _matmul_ep(
        [(dg2, W["w_ffn_t"], False, 0), (dup2, W["w_ffn_t"], False, 1)], tm=512, ins=[h, small["norm2_g"] + zero(tok), dh2], in_specs=[row(512), vec, row(512)],
        out_shapes=[tile(F32), tile(BF16), vec_shape], out_specs=[row(512), row(512), vec], sums=(2,), epilogue=norm2_bwd_ep, name="mm_du2_norm2_bwd")
    dw_out = _matmul(merged, dhb, ta=True, out_dtype=BF16, name="mm_dw_out", tm=1024, tn=1024, tk=2048)

    def merge_bwd_ep(acc, g_ref, pa_ref, pb_ref, dz_in):
        gt = g_ref[...].astype(F32)
        sa = _sigmoid(gt[:, :D_MODEL])
        sb = _sigmoid(gt[:, D_MODEL:])
        dgates = jnp.concatenate([acc * pa_ref[...].astype(F32) * sa * (1.0 - sa), acc * pb_ref[...].astype(F32) * sb * (1.0 - sb)], axis=1)
        return dgates, acc * sa, acc * sb

    dz, dpa, dpb = _matmul_ep(
        [(dhb, W["w_out"], True, 0)], tm=512, ins=[z, pa, pb, lax.empty((T, W_IN), BF16)], in_specs=[_gates_spec(512), row(512), row(512), ANY],
        out_shapes=[jax.ShapeDtypeStruct((T, W_IN), BF16), tile(BF16), tile(BF16)], out_specs=[_gates_spec(512), row(512), row(512)],
        aliases={3: 0}, epilogue=merge_bwd_ep, name="mm_dmerged_merge_bwd")
    doa, dob = _matmul_ep([(dpa, W["w_a"], True, 0), (dpb, W["w_b"], True, 0)], tm=1024, ins=[], in_specs=[],
                          out_shapes=[tile(BF16)] * 2, out_specs=[row(1024)] * 2, epilogue=lambda da, db: (da, db), name="mm_doa_dob")
    dw_a = _matmul(oa2, dpa, ta=True, out_dtype=BF16, name="mm_dw_a", tm=1024, tn=1024, tk=2048)
    dw_b = _matmul(ob2, dpb, ta=True, out_dtype=BF16, name="mm_dw_b", tm=1024, tn=1024, tk=2048)
    tok = emit("mix", dict(w_out=dw_out, w_a=dw_a, w_b=dw_b))
    dz3, dkv_cur, dkv_prev, dsinks = _swa_bwd(z3, qr, kr, cos, sin, small["attn_sinks"] + zero(tok), lse, dob.reshape(B, S, D),
                                              dz.reshape(B, S, W_IN), name="swa_bwd")
    dz3 = _swa_dkv_combine(dkv_cur, dkv_prev, dz3, name="swa_dkv")
    g["attn_sinks"] = dsinks
    dz3, g["lb"], g["hgrn_norm_g"] = _hgrn_bwd(z3, lb, small["hgrn_norm_g"], states, doa.reshape(B, S, D), dz3, name="hgrn_bwd")
    dz = dz3.reshape(T, W_IN)
    dw_in_t = _matmul(u1, dz, ta=True, out_t=True, o_block_perm=_reference_row_block, out_dtype=BF16, name="mm_dw_in", tm=1024, tn=256, tk=8192)
    tok = emit("in", dict(w_in_t=dw_in_t))
    du1 = _matmul(dz, w_in_t, after=tok, out_dtype=BF16, name="mm_du1", tm=1024, tn=512)
    dx, g["norm1_g"] = _norm_bwd_add(x2, small["norm1_g"], du1, dh, with_bf16=False, name="norm1_bwd")
    g["lb_logits"] = _lb_bwd(g.pop("lb"), lb, name="lb_bwd")
    return loss, dx.reshape(B, S, D), g


def _my_place():
    return lax.axis_index("x"), lax.axis_index("y"), lax.axis_index("c")


def _gather_blocks(x_ref, out_ref, send_sems, recv_sems, local_sem):
    x, y, c = _my_place()
    me, sibling = (x, y, c), (x, y, 1 - c)
    chips = [(1 - x, y), (x, 1 - y), (1 - x, 1 - y)]

    def slot(px, py, pc):
        return out_ref.at[4 * px + 2 * py + pc]

    def copy(k, block, to, src=None):
        return pltpu.make_async_remote_copy(
            src_ref=slot(*block) if src is None else src, dst_ref=slot(*block),
            send_sem=send_sems.at[k], recv_sem=recv_sems.at[k], device_id=to, device_id_type=MESH)

    mine = pltpu.make_async_copy(x_ref, slot(*me), local_sem)
    mine.start()
    first = [copy(0, me, sibling, src=x_ref)]
    first += [copy(1 + j, me, (*chip, c), src=x_ref) for j, chip in enumerate(chips)]
    for cp in first:
        cp.start()
    passed = [copy(4 + j, (*chip, c), sibling) for j, chip in enumerate(chips)]
    for j, chip in enumerate(chips):
        copy(1 + j, (*chip, c), me).wait_recv()
        passed[j].start()
    copy(0, sibling, me).wait_recv()
    for j, chip in enumerate(chips):
        copy(4 + j, (*chip, 1 - c), me).wait_recv()
    for cp in first + passed:
        cp.wait_send()
    mine.wait()


GATHER_SEMS = [pltpu.SemaphoreType.DMA((7,)), pltpu.SemaphoreType.DMA((7,)), pltpu.SemaphoreType.DMA]


def _all_gather(blk, *, name):
    return pl.pallas_call(
        _gather_body_fn(), name=name,
        out_shape=jax.ShapeDtypeStruct((N_DEV,) + blk.shape, blk.dtype),
        in_specs=[ANY], out_specs=ANY,
        scratch_shapes=GATHER_SEMS,
    )(blk)


def _gather_body_fn():
    def body(x_ref, out_ref, send_sems, recv_sems, local_sem):
        _gather_blocks(x_ref, out_ref, send_sems, recv_sems, local_sem)
    return body


SLAB_W = 1152
SMALL_SHAPES = dict(norm1_g=(1, D_MODEL), lb_logits=(2, HGRN_HEADS * HGRN_DK), hgrn_norm_g=(1, HGRN_DK), attn_sinks=(1, ATT_HEADS),
                    norm2_g=(1, D_MODEL), conv_b=(1, D_FF), final_g=(1, D_MODEL))
CONVW_BLK = D_FF // N_DEV
CONVW_STRIDE = SLAB_W // 3


def _slab_layout():
    layout, r = {}, 0
    for nm, (nr, w) in SMALL_SHAPES.items():
        layout[nm] = []
        for i in range(nr):
            for c0 in range(0, w, SLAB_W):
                layout[nm].append((r, i, c0, min(SLAB_W, w - c0)))
                r += 1
    return layout, r


SMALL_ROWS, _N_SMALL_ROWS = _slab_layout()
CONV_ROW0 = -(-_N_SMALL_ROWS // 8) * 8
LOSS_ROW = CONV_ROW0 + N_DEV
SLAB_ROWS = LOSS_ROW + 8


def _small_step(grads, g_conv_w, loss, params, moments, variances, dev, *, name):
    names = list(SMALL_ROWS)
    n = len(names)

    def body(dev_ref, *refs):
        g_refs = dict(zip(names, refs[:n]))
        gc_ref, loss_ref = refs[n], refs[n + 1]
        base = n + 2
        w_refs, m_refs, v_refs = (dict(zip(names + ["conv_w"], refs[base + i * (n + 1):base + (i + 1) * (n + 1)])) for i in range(3))
        o = base + 3 * (n + 1)
        gath_ref, loss_out = refs[o], refs[o + 1]
        outs = {nm: refs[o + 2 + 4 * i:o + 6 + 4 * i] for i, nm in enumerate(names + ["conv_w"])}
        slab, total, send_sems, recv_sems, local_sem = refs[-5:]

        slab[...] = jnp.zeros_like(slab)
        for nm, pieces in SMALL_ROWS.items():
            for r, i, c0, w in pieces:
                slab[r:r + 1, 0:w] = g_refs[nm][i:i + 1, c0:c0 + w]
        for p in range(N_DEV):
            for j in range(3):
                slab[CONV_ROW0 + p:CONV_ROW0 + p + 1, j * CONVW_STRIDE:j * CONVW_STRIDE + CONVW_BLK] = gc_ref[j:j + 1, p * CONVW_BLK:(p + 1) * CONVW_BLK]
        slab[LOSS_ROW:LOSS_ROW + 1, 0:1] = loss_ref[...]
        _gather_blocks(slab, gath_ref, send_sems, recv_sems, local_sem)
        acc = gath_ref[0]
        for p in range(1, N_DEV):
            acc = acc + gath_ref[p]
        total[...] = acc
        loss_out[...] = total[LOSS_ROW:LOSS_ROW + 1, 0:1]

        def update(nm, g, i, c0, w):
            at = (slice(i, i + 1), slice(c0, c0 + w))
            d, mn, vn = _adamw_math(w_refs[nm][at], g, m_refs[nm][at], v_refs[nm][at])
            for ref, val in zip(outs[nm], (g, d, mn, vn)):
                ref[at] = val

        for nm, pieces in SMALL_ROWS.items():
            for r, i, c0, w in pieces:
                update(nm, total[r:r + 1, 0:w], i, c0, w)
        conv_rows = total[CONV_ROW0:CONV_ROW0 + N_DEV, :]
        rowid = lax.broadcasted_iota(jnp.int32, conv_rows.shape, 0)
        mine = jnp.sum(jnp.where(rowid == dev_ref[0], conv_rows, 0.0), axis=0, keepdims=True)
        for j in range(3):
            update("conv_w", mine[:, j * CONVW_STRIDE:j * CONVW_STRIDE + CONVW_BLK], j, 0, CONVW_BLK)

    order = names + ["conv_w"]
    ins = [grads[nm] for nm in names] + [g_conv_w, loss]
    for d in (params, moments, variances):
        ins += [d[nm] for nm in order]
    vmem = pl.BlockSpec(memory_space=pltpu.VMEM)
    out_shape = [jax.ShapeDtypeStruct((N_DEV, SLAB_ROWS, SLAB_W), F32), jax.ShapeDtypeStruct((1, 1), F32)]
    for nm in order:
        out_shape += [jax.ShapeDtypeStruct(params[nm].shape, F32)] * 4
    res = pl.pallas_call(
        body, name=name,
        grid_spec=pltpu.PrefetchScalarGridSpec(
            num_scalar_prefetch=1, grid=(1,),
            in_specs=[vmem] * len(ins), out_specs=[vmem] * len(out_shape),
            scratch_shapes=[pltpu.VMEM((SLAB_ROWS, SLAB_W), F32), pltpu.VMEM((SLAB_ROWS, SLAB_W), F32)] + GATHER_SEMS),
        out_shape=out_shape,
    )(dev, *ins)
    return res[1], {nm: tuple(res[2 + 4 * i:6 + 4 * i]) for i, nm in enumerate(order)}


HBM_SPEC = pl.BlockSpec(memory_space=pltpu.HBM)
SEM_SPEC = pl.BlockSpec(memory_space=pltpu.SEMAPHORE)
DATAFLOW_EFFECT = pltpu.SideEffectType.DATAFLOW_SIDE_EFFECTING
N_PEERS = N_DEV - 1


def _peers(x, y, c):
    return [(1 - x if r & 4 else x, 1 - y if r & 2 else y, 1 - c if r & 1 else c) for r in range(1, N_DEV)]


def _exchange_start(srcs, scatter, *, after=None, name):
    n = len(srcs)
    lands = [lax.empty(a.shape if scatter else (N_DEV,) + a.shape, a.dtype) for a in srcs]
    extra = [] if after is None else [after]

    def body(*refs):
        src_refs, land_refs = refs[:n], refs[n:2 * n]
        send_sems, recv_sems, token = refs[2 * n + len(extra)], refs[2 * n + len(extra) + 1], refs[-1]
        x, y, c = _my_place()
        me = 4 * x + 2 * y + c
        for i in range(n):
            for r, (tx, ty, tc) in enumerate(_peers(x, y, c)):
                src = src_refs[i].at[4 * tx + 2 * ty + tc] if scatter else src_refs[i]
                pltpu.make_async_remote_copy(
                    src_ref=src, dst_ref=land_refs[i].at[me], send_sem=send_sems.at[N_PEERS * i + r],
                    recv_sem=recv_sems.at[N_PEERS * i + r], device_id=(tx, ty, tc), device_id_type=MESH).start()
        token[...] = jnp.zeros_like(token)

    thru = [pltpu.HBM(a.shape, a.dtype) for a in list(srcs) + lands]
    res = pl.pallas_call(
        body, name=name,
        out_shape=(pltpu.SemaphoreType.DMA((N_PEERS * n,)), pltpu.SemaphoreType.DMA((N_PEERS * n,)), *thru,
                   jax.ShapeDtypeStruct((8, 128), F32)),
        in_specs=[HBM_SPEC] * (2 * n) + [ANY] * len(extra),
        out_specs=(SEM_SPEC, SEM_SPEC, *([HBM_SPEC] * (2 * n)), pl.BlockSpec(memory_space=pltpu.VMEM)),
        input_output_aliases={i: 2 + i for i in range(2 * n)},
        compiler_params=pltpu.CompilerParams(has_side_effects=DATAFLOW_EFFECT),
    )(*[pltpu.with_memory_space_constraint(a, pltpu.HBM) for a in list(srcs) + lands], *extra)
    return (res[0], res[1], list(res[2:2 + n]), list(res[2 + n:2 + 2 * n]), scatter), res[-1]


def _exchange_wait(handle, after, *, name):
    send_sems, recv_sems, srcs, lands, scatter = handle
    n = len(srcs)

    def body(*refs):
        src_refs, land_refs = refs[:n], refs[n:2 * n]
        send_sems, recv_sems = refs[2 * n], refs[2 * n + 1]
        x, y, c = _my_place()
        for i in range(n):
            for r in range(N_PEERS):
                src = src_refs[i].at[0] if scatter else src_refs[i]
                cp = pltpu.make_async_remote_copy(
                    src_ref=src, dst_ref=land_refs[i].at[0], send_sem=send_sems.at[N_PEERS * i + r],
                    recv_sem=recv_sems.at[N_PEERS * i + r], device_id=(x, y, c), device_id_type=MESH)
                cp.wait_send()
                cp.wait_recv()

    thru = [pltpu.HBM(a.shape, a.dtype) for a in srcs + lands]
    res = pl.pallas_call(
        body, name=name, out_shape=tuple(thru),
        in_specs=[HBM_SPEC] * (2 * n) + [SEM_SPEC, SEM_SPEC, ANY], out_specs=tuple([HBM_SPEC] * (2 * n)),
        input_output_aliases={i: i for i in range(2 * n)},
        compiler_params=pltpu.CompilerParams(has_side_effects=DATAFLOW_EFFECT),
    )(*srcs, *lands, send_sems, recv_sems, after)
    return list(res[:n]), list(res[n:])


def _with_own(land, own, me):
    return lax.dynamic_update_index_in_dim(land, own, me, 0)


def _adamw_math(w, g, m, v):
    m = ADAM_B1 * m + (1.0 - ADAM_B1) * g
    v = ADAM_B2 * v + (1.0 - ADAM_B2) * (g * g)
    m_hat = m / (1.0 - ADAM_B1 ** ADAM_STEP)
    v_hat = v / (1.0 - ADAM_B2 ** ADAM_STEP)
    delta = -ADAM_LR * (m_hat / (jnp.sqrt(v_hat) + ADAM_EPS) + ADAM_WD * w)
    return delta, m, v


def _adamw_sum(parts, w, m, v, *, name):
    shape = w.shape
    R, n = shape[-2], shape[-1]
    w, m, v = (t.reshape(R, n) for t in (w, m, v))
    tr = _pick(R, (256, 464, 352, 128))

    def body(p_ref, w_ref, m_ref, v_ref, g_ref, d_ref, mo_ref, vo_ref):
        g = p_ref[0].astype(F32)
        for p in range(1, N_DEV):
            g = g + p_ref[p].astype(F32)
        d, mn, vn = _adamw_math(w_ref[...], g, m_ref[...], v_ref[...])
        g_ref[...] = g
        d_ref[...] = d
        mo_ref[...] = mn
        vo_ref[...] = vn

    row = pl.BlockSpec((tr, n), lambda i: (i, 0))
    outs = pl.pallas_call(
        body, name=name, grid=(R // tr,),
        in_specs=[pl.BlockSpec((N_DEV, tr, n), lambda i: (0, i, 0)), row, row, row],
        out_specs=[row, row, row, row],
        out_shape=[jax.ShapeDtypeStruct((R, n), F32)] * 4,
        compiler_params=_params("parallel"),
    )(parts, w, m, v)
    return [t.reshape(shape) for t in outs]


def _lb_bwd(dlb, lb, *, name):
    def body(d_ref, lb_ref, o_ref):
        t = d_ref[...] * lb_ref[...] * (1.0 - lb_ref[...])
        o_ref[0:1, :] = t
        o_ref[1:2, :] = -t

    return pl.pallas_call(body, name=name, out_shape=jax.ShapeDtypeStruct((2, lb.shape[1]), F32))(dlb, lb)


DOWN_BLK, ROW_BLK = D_FF // N_DEV, D_MODEL // N_DEV
W_FFN_BLK = 2 * D_FF // N_DEV
CONV_BITS_SHAPE = (16, 256)


def kernel(x, positions, norm1_g, w_in, lb_logits, hgrn_norm_g, w_a, attn_sinks, w_b, w_out, norm2_g, w_ffn_in, conv_w, conv_b, w_down, final_g, loss_target, m_norm1_g, m_w_in, m_lb_logits, m_hgrn_norm_g, m_w_a, m_attn_sinks, m_w_b, m_w_out, m_norm2_g, m_w_ffn_in, m_conv_w, m_conv_b, m_w_down, m_final_g, v_norm1_g, v_w_in, v_lb_logits, v_hgrn_norm_g, v_w_a, v_attn_sinks, v_w_b, v_w_out, v_norm2_g, v_w_ffn_in, v_conv_w, v_conv_b, v_w_down, v_final_g):
    xi, yi, ci = _my_place()
    dev = 4 * xi + 2 * yi + ci

    tr = lambda t: jnp.transpose(t[0])
    untr = lambda t: jnp.transpose(t)[None]
    w_in_blocks = _all_gather(tr(w_in).astype(BF16), name="ag_w_in")
    conv_bits = lax.bitcast_convert_type(conv_w, BF16).reshape(-1)
    conv_bits = jnp.pad(conv_bits, (0, CONV_BITS_SHAPE[0] * CONV_BITS_SHAPE[1] - conv_bits.shape[0])).reshape(CONV_BITS_SHAPE)
    w_in_full_t = _reordered_rows(w_in_blocks.reshape(W_IN, D_MODEL), name="w_in_rows")
    gather_handles = {}
    gather_handles["mix"], tok_mix = _exchange_start([w_a[0].astype(BF16), w_b[0].astype(BF16), w_out[0].astype(BF16)], False,
                                                     after=w_in_full_t, name="ag_mix_start")
    gather_handles["ffn"], tok_ffn = _exchange_start([tr(w_ffn_in).astype(BF16), w_down[0].astype(BF16), conv_bits], False,
                                                     after=tok_mix, name="ag_ffn_start")
    start_token = tok_mix + tok_ffn

    def rest_weights(group, after):
        own, lands = _exchange_wait(gather_handles[group], after, name="ag_" + group + "_wait")
        full = [_with_own(l, o, dev) for l, o in zip(lands, own)]
        if group == "mix":
            return dict(zip(("w_a", "w_b", "w_out"), [t.reshape(D_MODEL, D_MODEL) for t in full]))
        bits = full[2].reshape(N_DEV, -1)[:, :3 * CONVW_BLK * 2].reshape(N_DEV, 3, CONVW_BLK, 2)
        return dict(w_ffn_t=full[0].reshape(2 * D_FF, D_MODEL), w_down=full[1].reshape(D_FF, D_MODEL),
                    conv_w=lax.bitcast_convert_type(bits, F32).transpose(1, 0, 2).reshape(3, D_FF))

    handles = {}

    def emit(group, gr):
        if group == "ffn":
            srcs = [gr["w_ffn_t"].reshape(N_DEV, W_FFN_BLK, D_MODEL), gr["w_down"].reshape(N_DEV, DOWN_BLK, D_MODEL)]
        elif group == "mix":
            srcs = [gr[n].reshape(N_DEV, ROW_BLK, D_MODEL) for n in ("w_out", "w_a", "w_b")]
        else:
            srcs = [gr["w_in_t"].reshape(N_DEV, W_IN_BLK, D_MODEL)]
        handles[group], token = _exchange_start(srcs, True, name="rs_" + group + "_start")
        return token

    small = dict(norm1_g=norm1_g, lb_logits=lb_logits, hgrn_norm_g=hgrn_norm_g, attn_sinks=attn_sinks, norm2_g=norm2_g,
                 conv_b=conv_b, final_g=final_g)
    loss, grad_x, g = _local_step(x, positions, loss_target, small, w_in_full_t, rest_weights, emit, start_token)

    def parts_of(group, after):
        srcs, lands = _exchange_wait(handles[group], after, name="rs_" + group + "_wait")
        return [_with_own(l, lax.dynamic_index_in_dim(s, dev, 0, keepdims=False), dev) for s, l in zip(srcs, lands)]

    p_ffn, p_down = parts_of("ffn", grad_x)
    p_out, p_a, p_b = parts_of("mix", grad_x)
    (p_in,) = parts_of("in", grad_x)
    big = dict(
        w_in=[untr(t) for t in _adamw_sum(p_in, tr(w_in), tr(m_w_in), tr(v_w_in), name="adamw_w_in")],
        w_a=_adamw_sum(p_a, w_a, m_w_a, v_w_a, name="adamw_w_a"),
        w_b=_adamw_sum(p_b, w_b, m_w_b, v_w_b, name="adamw_w_b"),
        w_out=_adamw_sum(p_out, w_out, m_w_out, v_w_out, name="adamw_w_out"),
        w_ffn_in=[untr(t) for t in _adamw_sum(p_ffn, tr(w_ffn_in), tr(m_w_ffn_in), tr(v_w_ffn_in), name="adamw_w_ffn_in")],
        w_down=_adamw_sum(p_down, w_down, m_w_down, v_w_down, name="adamw_w_down"),
    )

    row = lambda t: t.reshape(1, -1) if t.ndim == 1 else t
    shard = lambda t: t.reshape(3, CONVW_BLK)
    sm_g = {nm: g[nm] for nm in SMALL_ROWS}
    sm_w = dict(norm1_g=norm1_g, lb_logits=lb_logits, hgrn_norm_g=hgrn_norm_g, attn_sinks=attn_sinks, norm2_g=norm2_g,
                conv_b=conv_b, final_g=row(final_g), conv_w=shard(conv_w))
    sm_m = dict(norm1_g=m_norm1_g, lb_logits=m_lb_logits, hgrn_norm_g=m_hgrn_norm_g, attn_sinks=m_attn_sinks, norm2_g=m_norm2_g,
                conv_b=m_conv_b, final_g=row(m_final_g), conv_w=shard(m_conv_w))
    sm_v = dict(norm1_g=v_norm1_g, lb_logits=v_lb_logits, hgrn_norm_g=v_hgrn_norm_g, attn_sinks=v_attn_sinks, norm2_g=v_norm2_g,
                conv_b=v_conv_b, final_g=row(v_final_g), conv_w=shard(v_conv_w))
    loss_total, sm_out = _small_step(sm_g, g["conv_w"], loss, sm_w, sm_m, sm_v, dev.astype(jnp.int32).reshape(1), name="small_step")
    shapes = dict(final_g=final_g.shape, conv_w=conv_w.shape)

    names = ("norm1_g", "w_in", "lb_logits", "hgrn_norm_g", "w_a", "attn_sinks", "w_b", "w_out", "norm2_g", "w_ffn_in", "conv_w", "conv_b", "w_down", "final_g")
    outs = [loss_total.reshape(()), grad_x]
    for kind in range(4):
        outs += [big[n][kind] if n in big else sm_out[n][kind].reshape(shapes.get(n, sm_out[n][kind].shape)) for n in names]
    return tuple(outs)
```

```python
import jax
import jax.numpy as jnp
from jax import lax
from jax.experimental import pallas as pl
from jax.experimental.pallas import tpu as pltpu

F32 = jnp.float32
BF16 = jnp.bfloat16

D_MODEL = 1024
HGRN_HEADS = 8
HGRN_DK = 128
CHUNK = 64
ATT_HEADS = 16
ATT_KV_HEADS = 2
ATT_HD = 64
ATT_GROUP = ATT_HEADS // ATT_KV_HEADS
WINDOW = 128
ROPE_DIM = ATT_HD // 4
ROPE_THETA = 500000.0
D_FF = 2816
EPS = 1e-6
NEG_INF = -1e30
N_DEV = 8

ADAM_LR = 0.001
ADAM_B1 = 0.9
ADAM_B2 = 0.999
ADAM_EPS = 1e-08
ADAM_WD = 0.01
ADAM_STEP = 10

MESH = pl.DeviceIdType.MESH
ANY = pl.BlockSpec(memory_space=pl.ANY)


def _pick(n, cands):
    for c in cands:
        if n % c == 0:
            return c
    return n


def _sigmoid(x):
    return 0.5 * jnp.tanh(0.5 * x) + 0.5


def _silu(x):
    hx = 0.5 * x
    return hx * jnp.tanh(hx) + hx


def _rms(x, g):
    return x * lax.rsqrt(jnp.mean(x * x, axis=-1, keepdims=True) + EPS) * g


def _dot(a, b, dims):
    return lax.dot_general(a, b, (dims, ((), ())), preferred_element_type=F32)


def _nn(a, b):
    return _dot(a, b, ((1,), (0,)))


def _nt(a, b):
    return _dot(a, b, ((1,), (1,)))


def _tn(a, b):
    return _dot(a, b, ((0,), (0,)))


def _params(*sem):
    return pltpu.CompilerParams(dimension_semantics=sem, vmem_limit_bytes=56 * 1024 * 1024)


def _matmul(a, b, *, ta=False, tb=False, out_dtype=F32, addend=None, after=None, into=None, o_noff=0, out_t=False,
            o_block_perm=lambda j: j, name, tm, tn, tk=None, n_extent=None, b_koff=0, b_noff=0):
    M, K = (a.shape[1], a.shape[0]) if ta else a.shape
    N = n_extent or (b.shape[0] if tb else b.shape[1])
    tm, tn, tk = min(tm, M), min(tn, N), min(tk or K, K)
    assert M % tm == 0 and N % tn == 0 and K % tk == 0, (name, M, N, K, tm, tn, tk)
    nk = K // tk
    use_scratch = nk > 1 and out_dtype != F32
    grid = (M // tm, N // tn, nk)
    a_spec = pl.BlockSpec((tk, tm), lambda i, j, k: (k, i)) if ta else pl.BlockSpec((tm, tk), lambda i, j, k: (i, k))
    b_spec = pl.BlockSpec((tn, tk), lambda i, j, k: (j + b_noff, k + b_koff)) if tb else pl.BlockSpec((tk, tn), lambda i, j, k: (k + b_koff, j + b_noff))
    o_spec = pl.BlockSpec((tm, tn), lambda i, j, k: (i, j))
    dims = ((0 if ta else 1,), (1 if tb else 0,))
    has_add = addend is not None

    n_in = 2 + has_add + (after is not None) + (into is not None)

    def body(*refs):
        a_ref, b_ref = refs[:2]
        c_ref = refs[2] if has_add else None
        o_ref = refs[n_in]
        part = _dot(a_ref[...], b_ref[...], dims)
        if nk == 1:
            if has_add:
                part = part + c_ref[...].astype(F32)
            o_ref[...] = (part.T if out_t else part).astype(out_dtype)
        else:
            acc_ref = refs[-1] if use_scratch else o_ref
            k = pl.program_id(2)

            @pl.when(k == 0)
            def _():
                acc_ref[...] = part + c_ref[...].astype(F32) if has_add else part

            @pl.when(k > 0)
            def _():
                acc_ref[...] += part

            if use_scratch:
                @pl.when(k == nk - 1)
                def _():
                    o_ref[...] = acc_ref[...].astype(out_dtype)

    in_specs = [a_spec, b_spec] + ([o_spec] if has_add else [])
    args = (a, b) + ((addend,) if has_add else ())
    if after is not None:
        in_specs.append(pl.BlockSpec(after.shape, lambda i, j, k: (0, 0)))
        args += (after,)
    aliases = {}
    if into is not None:
        in_specs.append(ANY)
        args += (into,)
        aliases = {len(args) - 1: 0}
    if out_t:
        assert nk == 1 and not has_add
        o_spec = pl.BlockSpec((tn, tm), lambda i, j, k: (o_block_perm(j) + o_noff, i))
    elif into is not None:
        o_spec = pl.BlockSpec((tm, tn), lambda i, j, k: (i, j + o_noff))
    return pl.pallas_call(
        body,
        name=name,
        grid=grid,
        in_specs=in_specs,
        out_specs=o_spec,
        out_shape=jax.ShapeDtypeStruct(into.shape if into is not None else ((N, M) if out_t else (M, N)), out_dtype),
        input_output_aliases=aliases,
        scratch_shapes=[pltpu.VMEM((tm, tn), F32)] if use_scratch else [],
        compiler_params=_params("parallel", "parallel", "arbitrary"),
    )(*args)


def _matmul_ep(pairs, *, tm, ins, in_specs, out_shapes, out_specs, sums=(), epilogue, aliases=None, name):
    M = pairs[0][0].shape[0]
    tm = min(tm, M)
    mm_specs, mm_args, dims = [], [], []
    for a, b, tb, koff in pairs:
        K = a.shape[1]
        N = b.shape[0] if tb else b.shape[1]
        mm_specs += [pl.BlockSpec((tm, K), lambda i: (i, 0)),
                     pl.BlockSpec((N, K), lambda i, koff=koff: (0, koff)) if tb else pl.BlockSpec((K, N), lambda i, koff=koff: (koff, 0))]
        mm_args += [a, b]
        dims.append(((1,), (1 if tb else 0,)))
    n_mm = len(mm_args)
    n_in = n_mm + len(ins)

    def body(*refs):
        in_refs, out_refs = refs[n_mm:n_in], refs[n_in:]
        accs = [_dot(refs[2 * p][...], refs[2 * p + 1][...], dims[p]) for p in range(len(pairs))]
        outs = epilogue(*accs, *in_refs)
        for k, (ref, val) in enumerate(zip(out_refs, outs)):
            if val is None:
                continue
            if k in sums:
                @pl.when(pl.program_id(0) == 0)
                def _():
                    ref[...] = jnp.zeros_like(ref)

                ref[...] += val
            else:
                ref[...] = val.astype(ref.dtype)

    return pl.pallas_call(
        body, name=name, grid=(M // tm,),
        in_specs=mm_specs + list(in_specs),
        out_specs=list(out_specs), out_shape=list(out_shapes),
        input_output_aliases={n_mm + k: v for k, v in (aliases or {}).items()},
        compiler_params=_params("arbitrary"),
    )(*mm_args, *ins)


def _row_spec(tm, n):
    return pl.BlockSpec((tm, n), lambda i: (i, 0))


def _full_spec(shape):
    return pl.BlockSpec(shape, lambda i: tuple(0 for _ in shape))


def _norm_cast(x, g, *, name):
    T, D = x.shape
    tm = _pick(T, (512, 256, 128))

    def body(x_ref, g_ref, u_ref):
        u_ref[...] = _rms(x_ref[...], g_ref[...]).astype(BF16)

    return pl.pallas_call(
        body, name=name, grid=(T // tm,),
        in_specs=[_row_spec(tm, D), _full_spec((1, D))],
        out_specs=_row_spec(tm, D),
        out_shape=jax.ShapeDtypeStruct((T, D), BF16),
        compiler_params=_params("parallel"),
    )(x, g)


def _norm_bwd_add(x, g, du, dres, *, with_bf16=True, name):
    T, D = x.shape
    tm = _pick(T, (512, 256, 128))

    def body(x_ref, g_ref, du_ref, dr_ref, dx_ref, *rest):
        dg_ref = rest[-1]
        _, vjp = jax.vjp(_rms, x_ref[...], g_ref[...])
        dx, dg = vjp(du_ref[...].astype(F32))
        dx = dx + dr_ref[...]
        dx_ref[...] = dx
        if with_bf16:
            rest[0][...] = dx.astype(BF16)

        @pl.when(pl.program_id(0) == 0)
        def _():
            dg_ref[...] = jnp.zeros_like(dg_ref)

        dg_ref[...] += dg

    row = _row_spec(tm, D)
    return pl.pallas_call(
        body, name=name, grid=(T // tm,),
        in_specs=[row, _full_spec((1, D)), row, row],
        out_specs=[row] + ([row] if with_bf16 else []) + [_full_spec((1, D))],
        out_shape=[jax.ShapeDtypeStruct((T, D), F32)] + ([jax.ShapeDtypeStruct((T, D), BF16)] if with_bf16 else []) + [jax.ShapeDtypeStruct((1, D), F32)],
        compiler_params=_params("arbitrary"),
    )(x, g, du, dres)


def _merge_fn(gates, a, b):
    ga = gates[:, :D_MODEL].astype(F32)
    gb = gates[:, D_MODEL:].astype(F32)
    return _sigmoid(ga) * a.astype(F32) + _sigmoid(gb) * b.astype(F32)


def _gates_spec(tm):
    return pl.BlockSpec((tm, W_GATES), lambda i: (i, O_GATES // W_GATES))


CONV_TC = 256


def _shift_down(x, n, rows):
    return jnp.where(rows >= n, pltpu.roll(x, n, 0), 0.0)


def _shift_up(x, n, rows, S):
    return jnp.where(rows < S - n, pltpu.roll(x, S - n, 0), 0.0)


def _conv_act_fwd(gu, conv_w, conv_b, *, name):
    B, S, _ = gu.shape
    tc = CONV_TC
    nc = D_FF // tc

    def body(g_ref, up_ref, w_ref, b_ref, o_ref, a_ref):
        g = g_ref[...].astype(F32)
        rows = lax.broadcasted_iota(jnp.int32, g.shape, 0)
        w = w_ref[...]
        a = w[2:3] * g + w[1:2] * _shift_down(g, 1, rows) + w[0:1] * _shift_down(g, 2, rows) + b_ref[...]
        o_ref[...] = (_silu(a) * up_ref[...].astype(F32)).astype(BF16)
        a_ref[...] = a.astype(BF16)

    col = pl.BlockSpec((None, S, tc), lambda b, j: (b, 0, j))
    return pl.pallas_call(
        body, name=name, grid=(B, nc),
        in_specs=[col,
                  pl.BlockSpec((None, S, tc), lambda b, j: (b, 0, j + nc)),
                  pl.BlockSpec((3, tc), lambda b, j: (0, j)),
                  pl.BlockSpec((1, tc), lambda b, j: (0, j))],
        out_specs=[col, col],
        out_shape=[jax.ShapeDtypeStruct((B, S, D_FF), BF16)] * 2,
        compiler_params=_params("parallel", "parallel"),
    )(gu, gu, conv_w, conv_b)


def _conv_act_bwd(gu, a_pre, conv_w, dact, *, name):
    B, S, _ = gu.shape
    tc = CONV_TC
    nc = D_FF // tc

    def body(g_ref, up_ref, a_ref, w_ref, da_ref, dg_ref, dup_ref, dw_ref, db_ref):
        g = g_ref[...].astype(F32)
        up, a, dact = up_ref[...], a_ref[...], da_ref[...]
        rows = lax.broadcasted_iota(jnp.int32, g.shape, 0)
        w = w_ref[...]
        sg = _sigmoid(a)
        dup_ref[...] = dact * a * sg
        da = (dact * up * sg * (1.0 + a * (1.0 - sg))).astype(F32)
        da1 = _shift_up(da, 1, rows, S)
        da2 = _shift_up(da, 2, rows, S)
        dg_ref[...] = (w[2:3] * da + w[1:2] * da1 + w[0:1] * da2).astype(BF16)

        @pl.when(pl.program_id(1) == 0)
        def _():
            dw_ref[...] = jnp.zeros_like(dw_ref)
            db_ref[...] = jnp.zeros_like(db_ref)

        dw_ref[0:1, :] += jnp.sum(da2 * g, axis=0, keepdims=True)
        dw_ref[1:2, :] += jnp.sum(da1 * g, axis=0, keepdims=True)
        dw_ref[2:3, :] += jnp.sum(da * g, axis=0, keepdims=True)
        db_ref[...] += jnp.sum(da, axis=0, keepdims=True)

    col = pl.BlockSpec((None, S, tc), lambda j, b: (b, 0, j))
    return pl.pallas_call(
        body, name=name, grid=(nc, B),
        in_specs=[col,
                  pl.BlockSpec((None, S, tc), lambda j, b: (b, 0, j + nc)),
                  col,
                  pl.BlockSpec((3, tc), lambda j, b: (0, j)),
                  col],
        out_specs=[col, col, pl.BlockSpec((3, tc), lambda j, b: (0, j)), pl.BlockSpec((1, tc), lambda j, b: (0, j))],
        out_shape=[jax.ShapeDtypeStruct((B, S, D_FF), BF16), jax.ShapeDtypeStruct((B, S, D_FF), BF16),
                   jax.ShapeDtypeStruct((3, D_FF), F32), jax.ShapeDtypeStruct((1, D_FF), F32)],
        compiler_params=_params("parallel", "arbitrary"),
    )(gu, gu, a_pre, conv_w, dact)


HGRN_CPB = 8
HF = HGRN_HEADS * HGRN_DK


def _tri(n, upper=False):
    r = lax.broadcasted_iota(jnp.int32, (n, n), 0)
    c = lax.broadcasted_iota(jnp.int32, (n, n), 1)
    return (c >= r) if upper else (r >= c)


def _hs(h):
    return slice(h * HGRN_DK, (h + 1) * HGRN_DK)


def _cumsum_rows(tri_b, x):
    hi = x.astype(BF16)
    lo = (x - hi.astype(F32)).astype(BF16)
    return _nn(tri_b, hi) + _nn(tri_b, lo)


def _hgrn_col(seg, h):
    return slice(seg * HF + h * HGRN_DK, seg * HF + (h + 1) * HGRN_DK)


def _hgrn_gates(q, fz, lb):
    sg = _sigmoid(fz)
    return _sigmoid(q), sg, lb + (1.0 - lb) * sg


def _hgrn_decays(b, q, sq, f):
    qf = q * sq
    k = 1.0 - f
    bref = b[CHUNK // 2:CHUNK // 2 + 1, :]
    blast = b[CHUNK - 1:CHUNK, :]
    e1 = jnp.exp2(b - bref)
    e2 = jnp.exp2(bref - b)
    e3 = e1 * jnp.exp2(bref)
    e4 = e2 * jnp.exp2(blast - bref)
    return (e1, e2, e3, e4), qf * e1, k * e2, qf * e3, k * e4, jnp.exp2(blast)


def _hgrn_fwd(zh, lb, gn, *, name):
    B, S, _ = zh.shape
    cpb = HGRN_CPB
    ts = cpb * CHUNK
    nblk = S // ts

    def body(z_ref, lb_ref, gn_ref, o_ref, st_ref, state):
        @pl.when(pl.program_id(1) == 0)
        def _():
            state[...] = jnp.zeros_like(state)

        R = range(HGRN_HEADS)
        causal = _tri(CHUNK)
        tril_b = causal.astype(BF16)
        lbh = [lb_ref[:, _hs(h)] for h in R]
        for c in range(cpb):
            rows = slice(c * CHUNK, (c + 1) * CHUNK)
            q = [z_ref[rows, _hgrn_col(0, h)].astype(F32) for h in R]
            gates = [_hgrn_gates(q[h], z_ref[rows, _hgrn_col(1, h)].astype(F32), lbh[h]) for h in R]
            b = [_cumsum_rows(tril_b, jnp.log2(gates[h][2])) for h in R]
            v = [z_ref[rows, _hgrn_col(2, h)] for h in R]
            dec, q_in, k_in, q_out, k_st = [], [], [], [], []
            for h in R:
                _, qi, ki, qo, ks, d = _hgrn_decays(b[h], q[h], gates[h][0], gates[h][2])
                dec.append(d)
                for lst, t in zip((q_in, k_in, q_out, k_st), (qi, ki, qo, ks)):
                    lst.append(t.astype(BF16))
            a = [jnp.where(causal, _nt(q_in[h], k_in[h]), 0.0).astype(BF16) for h in R]
            st = [state[h] for h in R]
            for h in R:
                st_ref[c, h] = st[h]
            o = [_nn(a[h], v[h]) + _nt(q_out[h], st[h].astype(BF16)) for h in R]
            for h in R:
                state[h] = st[h] * dec[h] + _tn(v[h], k_st[h])
            for h in R:
                o_ref[rows, _hs(h)] = (_rms(o[h], gn_ref[...]) * _silu(z_ref[rows, _hgrn_col(3, h)].astype(F32))).astype(BF16)

    return pl.pallas_call(
        body, name=name, grid=(B, nblk),
        in_specs=[pl.BlockSpec((None, ts, 4 * HF), lambda b, s: (b, s, 0)),
                  pl.BlockSpec((1, HF), lambda b, s: (0, 0)),
                  pl.BlockSpec((1, HGRN_DK), lambda b, s: (0, 0))],
        out_specs=[pl.BlockSpec((None, ts, HF), lambda b, s: (b, s, 0)),
                   pl.BlockSpec((None, cpb, HGRN_HEADS, HGRN_DK, HGRN_DK), lambda b, s: (b, s, 0, 0, 0))],
        out_shape=[jax.ShapeDtypeStruct((B, S, HF), BF16),
                   jax.ShapeDtypeStruct((B, S // CHUNK, HGRN_HEADS, HGRN_DK, HGRN_DK), F32)],
        scratch_shapes=[pltpu.VMEM((HGRN_HEADS, HGRN_DK, HGRN_DK), F32)],
        compiler_params=_params("arbitrary", "arbitrary"),
    )(zh, lb, gn)


def _hgrn_bwd(zh, lb, gn, states, doa, dz, *, name):
    B, S, _ = zh.shape
    cpb = HGRN_CPB
    ts = cpb * CHUNK
    nblk = S // ts
    rev = lambda b, s: (b, nblk - 1 - s, 0)

    def body(z_ref, lb_ref, gn_ref, st_ref, do_ref, dz_in, dz_ref, dlb_ref, dgn_ref, dstate):
        @pl.when(pl.program_id(1) == 0)
        def _():
            dstate[...] = jnp.zeros_like(dstate)

        @pl.when((pl.program_id(0) == 0) & (pl.program_id(1) == 0))
        def _():
            dlb_ref[...] = jnp.zeros_like(dlb_ref)
            dgn_ref[...] = jnp.zeros_like(dgn_ref)

        R = range(HGRN_HEADS)
        causal = _tri(CHUNK)
        tril_b = causal.astype(BF16)
        triu_b = _tri(CHUNK, upper=True).astype(BF16)
        rowid = lax.broadcasted_iota(jnp.int32, (CHUNK, HGRN_DK), 0)
        lbh = [lb_ref[:, _hs(h)] for h in R]
        gn = gn_ref[...]
        for c in reversed(range(cpb)):
            rows = slice(c * CHUNK, (c + 1) * CHUNK)
            q = [z_ref[rows, _hgrn_col(0, h)].astype(F32) for h in R]
            gates = [_hgrn_gates(q[h], z_ref[rows, _hgrn_col(1, h)].astype(F32), lbh[h]) for h in R]
            b = [_cumsum_rows(tril_b, jnp.log2(gates[h][2])) for h in R]
            v = [z_ref[rows, _hgrn_col(2, h)] for h in R]
            pre = [_hgrn_decays(b[h], q[h], gates[h][0], gates[h][2]) for h in R]
            q_in_b, k_in_b, q_out_b, k_st_b = ([pre[h][i].astype(BF16) for h in R] for i in (1, 2, 3, 4))
            a_b = [jnp.where(causal, _nt(q_in_b[h], k_in_b[h]), 0.0).astype(BF16) for h in R]
            st = [st_ref[c, h] for h in R]
            st_b = [t.astype(BF16) for t in st]
            o = [_nn(a_b[h], v[h]) + _nt(q_out_b[h], st_b[h]) for h in R]
            do_l, dgn_acc = [], jnp.zeros_like(gn)
            for h in R:
                hg = z_ref[rows, _hgrn_col(3, h)].astype(F32)
                dout = do_ref[rows, _hs(h)].astype(F32)
                shg = _sigmoid(hg)
                on_h, norm_vjp = jax.vjp(_rms, o[h], gn)
                d_o, d_gn = norm_vjp(dout * (hg * shg))
                do_l.append(d_o)
                dgn_acc = dgn_acc + d_gn
                dz_ref[rows, _hgrn_col(3, h)] = (dout * on_h * shg * (1.0 + hg * (1.0 - shg))).astype(BF16)
            dgn_ref[...] += dgn_acc
            do_b = [t.astype(BF16) for t in do_l]
            dst = [dstate[h] for h in R]
            dst_b = [t.astype(BF16) for t in dst]
            da_b = [jnp.where(causal, _nt(do_b[h], v[h]), 0.0).astype(BF16) for h in R]
            dv = [_tn(a_b[h], do_b[h]) + _nt(k_st_b[h], dst_b[h]) for h in R]
            dq_in = [_nn(da_b[h], k_in_b[h]) for h in R]
            dk_in = [_tn(da_b[h], q_in_b[h]) for h in R]
            dq_out = [_nn(do_b[h], st_b[h]) for h in R]
            dk_st = [_nn(v[h], dst_b[h]) for h in R]
            for h in R:
                dz_ref[rows, _hgrn_col(2, h)] = dv[h].astype(BF16)
            db = []
            for h in R:
                _, q_in, k_in, q_out, k_st, dec = pre[h]
                ddec = jnp.sum(st[h] * dst[h], axis=0, keepdims=True)
                t_qin, t_kin, t_kst = dq_in[h] * q_in, dk_in[h] * k_in, dk_st[h] * k_st
                dbref = jnp.sum(t_kin - t_qin, axis=0, keepdims=True)
                dblast = jnp.sum(t_kst, axis=0, keepdims=True) + ddec * dec
                db.append(t_qin - t_kin + dq_out[h] * q_out - t_kst
                          + jnp.where(rowid == CHUNK // 2, dbref, 0.0) + jnp.where(rowid == CHUNK - 1, dblast, 0.0))
            for h in R:
                dstate[h] = dst[h] * pre[h][5] + _tn(do_b[h], q_out_b[h])
            dlogf = [_cumsum_rows(triu_b, db[h]) for h in R]
            for h in R:
                sq, sg, f = gates[h]
                e1, e2, e3, e4 = pre[h][0]
                dqf = dq_in[h] * e1 + dq_out[h] * e3
                dk = dk_in[h] * e2 + dk_st[h] * e4
                df_open = (dlogf[h] / f - dk) * (1.0 - sg)
                dlb_ref[:, _hs(h)] += jnp.sum(df_open, axis=0, keepdims=True)
                dz_ref[rows, _hgrn_col(1, h)] = (df_open * ((1.0 - lbh[h]) * sg)).astype(BF16)
                dz_ref[rows, _hgrn_col(0, h)] = (dqf * sq * (1.0 + q[h] * (1.0 - sq))).astype(BF16)

    return pl.pallas_call(
        body, name=name, grid=(B, nblk),
        in_specs=[pl.BlockSpec((None, ts, 4 * HF), rev),
                  pl.BlockSpec((1, HF), lambda b, s: (0, 0)),
                  pl.BlockSpec((1, HGRN_DK), lambda b, s: (0, 0)),
                  pl.BlockSpec((None, cpb, HGRN_HEADS, HGRN_DK, HGRN_DK), lambda b, s: (b, nblk - 1 - s, 0, 0, 0)),
                  pl.BlockSpec((None, ts, HF), rev),
                  ANY],
        out_specs=[pl.BlockSpec((None, ts, 4 * HF), rev),
                   pl.BlockSpec((1, HF), lambda b, s: (0, 0)),
                   pl.BlockSpec((1, HGRN_DK), lambda b, s: (0, 0))],
        out_shape=[jax.ShapeDtypeStruct(dz.shape, BF16),
                   jax.ShapeDtypeStruct((1, HF), F32),
                   jax.ShapeDtypeStruct((1, HGRN_DK), F32)],
        input_output_aliases={5: 0},
        scratch_shapes=[pltpu.VMEM((HGRN_HEADS, HGRN_DK, HGRN_DK), F32)],
        compiler_params=_params("arbitrary", "arbitrary"),
    )(zh, lb, gn, states, doa, dz)


KV_W = ATT_KV_HEADS * ATT_HD
ATT_SCALE = ATT_HD ** -0.5


def _rope(x, cos, sin, inverse=False):
    half = ROPE_DIM // 2
    outs = []
    for p in range(x.shape[1] // 128):
        xp = x[:, p * 128:(p + 1) * 128]
        lane = lax.broadcasted_iota(jnp.int32, xp.shape, 1) % ATT_HD
        sw = jnp.where(lane < half, pltpu.roll(xp, 128 - half, 1), pltpu.roll(xp, half, 1))
        outs.append(xp * cos - sw * sin if inverse else xp * cos + sw * sin)
    return outs[0] if len(outs) == 1 else jnp.concatenate(outs, axis=1)


PAIRS_PER_KV = ATT_GROUP // 2


def _swap_halves(x):
    return pltpu.roll(x, ATT_HD, 1)


def _kv_padded(t, low):
    sw = _swap_halves(t)
    zero = jnp.zeros_like(t)
    out = []
    for g in range(ATT_KV_HEADS):
        in_low, in_high = (t, sw) if g == 0 else (sw, t)
        out.append((jnp.where(low, in_low, zero).astype(BF16), jnp.where(low, zero, in_high).astype(BF16)))
    return out


def _swa_mask(first_block):
    qi = lax.broadcasted_iota(jnp.int32, (WINDOW, 2 * WINDOW), 0)
    mi = lax.broadcasted_iota(jnp.int32, (WINDOW, 2 * WINDOW), 1)
    band = (mi > qi) & (mi <= qi + WINDOW)
    return band & (jnp.logical_not(first_block) | (mi >= WINDOW))


def _swa_specs(nb):
    cur = lambda b, i: (b, i, 0)
    prev = lambda b, i: (b, jnp.maximum(i - 1, 0), 0)
    return cur, prev


def _swa_z_specs():
    q = pl.BlockSpec((None, WINDOW, W_AQ), lambda b, i: (b, i, O_AQ // W_AQ))
    kv_prev = pl.BlockSpec((None, WINDOW, W_AKV), lambda b, i: (b, jnp.maximum(i - 1, 0), O_AKV // W_AKV))
    kv_cur = pl.BlockSpec((None, WINDOW, W_AKV), lambda b, i: (b, i, O_AKV // W_AKV))
    return q, kv_prev, kv_cur


def _swa_fwd(z, cos, sin, sinks, *, name):
    B, S, _ = z.shape
    nb = S // WINDOW
    cur, prev = _swa_specs(nb)

    def body(q_ref, kvp_ref, kvc_ref, cp_ref, sp_ref, cc_ref, sc_ref, sink_ref, o_ref, lse_ref, qr_ref, kr_ref):
        cos_c, sin_c = cc_ref[...], sc_ref[...]
        q = (_rope(q_ref[...].astype(F32), cos_c, sin_c) * ATT_SCALE).astype(BF16)
        k = jnp.concatenate([_rope(kvp_ref[:, :KV_W].astype(F32), cp_ref[...], sp_ref[...]),
                             _rope(kvc_ref[:, :KV_W].astype(F32), cos_c, sin_c)], axis=0)
        qr_ref[...] = q
        kr_ref[...] = k[WINDOW:].astype(BF16)
        v = jnp.concatenate([kvp_ref[:, KV_W:], kvc_ref[:, KV_W:]], axis=0).astype(F32)
        low = lax.broadcasted_iota(jnp.int32, k.shape, 1) < ATT_HD
        kpad = _kv_padded(k, low)
        vpad = _kv_padded(v, low)
        mask = _swa_mask(pl.program_id(1) == 0)
        lses = []
        for g in range(ATT_KV_HEADS):
            pairs = range(g * PAIRS_PER_KV, (g + 1) * PAIRS_PER_KV)
            keys = [(p, e) for p in pairs for e in (0, 1)]
            qp = {p: q[:, p * 128:(p + 1) * 128] for p in pairs}
            s = {pe: jnp.where(mask, _nt(qp[pe[0]], kpad[g][pe[1]]), NEG_INF) for pe in keys}
            pr = {}
            for pe in keys:
                sink = sink_ref[0, 2 * pe[0] + pe[1]]
                m = jnp.maximum(jnp.max(s[pe], axis=1, keepdims=True), sink)
                ex = jnp.exp(s[pe] - m)
                den = jnp.sum(ex, axis=1, keepdims=True) + jnp.exp(sink - m)
                pr[pe] = (ex * (1.0 / den)).astype(BF16)
                lses.append(m + jnp.log(den))
            for p in pairs:
                o_ref[:, p * 128:(p + 1) * 128] = (_nn(pr[p, 0], vpad[g][0]) + _nn(pr[p, 1], vpad[g][1])).astype(BF16)
        lse_ref[...] = jnp.concatenate(lses, axis=1)

    tab = lambda im: pl.BlockSpec((None, WINDOW, 128), im)
    return pl.pallas_call(
        body, name=name, grid=(B, nb),
        in_specs=[*_swa_z_specs(),
                  tab(prev), tab(prev), tab(cur), tab(cur),
                  pl.BlockSpec(memory_space=pltpu.SMEM)],
        out_specs=[pl.BlockSpec((None, WINDOW, D_MODEL), cur), pl.BlockSpec((None, WINDOW, ATT_HEADS), cur),
                   pl.BlockSpec((None, WINDOW, D_MODEL), cur), pl.BlockSpec((None, WINDOW, KV_W), cur)],
        out_shape=[jax.ShapeDtypeStruct((B, S, D_MODEL), BF16), jax.ShapeDtypeStruct((B, S, ATT_HEADS), F32),
                   jax.ShapeDtypeStruct((B, S, D_MODEL), BF16), jax.ShapeDtypeStruct((B, S, KV_W), BF16)],
        compiler_params=_params("parallel", "parallel"),
    )(z, z, z, cos, sin, cos, sin, sinks)


def _swa_bwd(z, qr, kr, cos, sin, sinks, lse, dob, dz, *, name):
    B, S, _ = z.shape
    nb = S // WINDOW
    cur, prev = _swa_specs(nb)

    def body(q_ref, krp_ref, krc_ref, kvp_ref, kvc_ref, cp_ref, sp_ref, cc_ref, sc_ref, sink_ref, lse_ref, do_ref, dz_in,
             dq_ref, dkc_ref, dkp_ref, dsink_ref):
        @pl.when((pl.program_id(0) == 0) & (pl.program_id(1) == 0))
        def _():
            dsink_ref[...] = jnp.zeros_like(dsink_ref)

        cos_c, sin_c, cos_p, sin_p = cc_ref[...], sc_ref[...], cp_ref[...], sp_ref[...]
        q = q_ref[...]
        k = jnp.concatenate([krp_ref[...], krc_ref[...]], axis=0).astype(F32)
        v = jnp.concatenate([kvp_ref[:, KV_W:], kvc_ref[:, KV_W:]], axis=0).astype(F32)
        low = lax.broadcasted_iota(jnp.int32, k.shape, 1) < ATT_HD
        kpad = _kv_padded(k, low)
        vpad = _kv_padded(v, low)
        mask = _swa_mask(pl.program_id(1) == 0)
        lse = lse_ref[...]
        dq_parts, dk_sum, dv_sum, dsinks = [], [], [], []
        for g in range(ATT_KV_HEADS):
            pairs = range(g * PAIRS_PER_KV, (g + 1) * PAIRS_PER_KV)
            keys = [(p, e) for p in pairs for e in (0, 1)]
            qp = {p: q[:, p * 128:(p + 1) * 128] for p in pairs}
            dop = {p: do_ref[:, p * 128:(p + 1) * 128] for p in pairs}
            s = {pe: jnp.where(mask, _nt(qp[pe[0]], kpad[g][pe[1]]), NEG_INF) for pe in keys}
            dp = {pe: _nt(dop[pe[0]], vpad[g][pe[1]]) for pe in keys}
            pr, ds = {}, {}
            for pe in keys:
                h = 2 * pe[0] + pe[1]
                lse_h = lse[:, h:h + 1]
                pf = jnp.exp(s[pe] - lse_h)
                delta = jnp.sum(pf * dp[pe], axis=1, keepdims=True)
                ds[pe] = (pf * (dp[pe] - delta)).astype(BF16)
                pr[pe] = pf.astype(BF16)
                p_sink = jnp.exp(sink_ref[0, h] - lse_h)
                dsinks.append(-jnp.sum(p_sink * delta, axis=0, keepdims=True))
            for p in pairs:
                dq_parts.append((_nn(ds[p, 0], kpad[g][0]) + _nn(ds[p, 1], kpad[g][1])) * ATT_SCALE)
            x = [sum(_tn(ds[p, e], qp[p]) for p in pairs) for e in (0, 1)]
            y = [sum(_tn(pr[p, e], dop[p]) for p in pairs) for e in (0, 1)]
            zk = jnp.where(low, x[0], x[1])
            zv = jnp.where(low, y[0], y[1])
            dk_sum.append(zk + _swap_halves(zk))
            dv_sum.append(zv + _swap_halves(zv))
        dq_ref[...] = _rope(jnp.concatenate(dq_parts, axis=1), cos_c, sin_c, inverse=True).astype(BF16)
        dk = jnp.where(low, dk_sum[0], dk_sum[1])
        dv = jnp.where(low, dv_sum[0], dv_sum[1])
        dkp_ref[:, :KV_W] = _rope(dk[:WINDOW], cos_p, sin_p, inverse=True)
        dkp_ref[:, KV_W:] = dv[:WINDOW]
        dkc_ref[:, :KV_W] = _rope(dk[WINDOW:], cos_c, sin_c, inverse=True)
        dkc_ref[:, KV_W:] = dv[WINDOW:]
        dsink_ref[...] += jnp.concatenate(dsinks, axis=1)

    tab = lambda im: pl.BlockSpec((None, WINDOW, 128), im)
    return pl.pallas_call(
        body, name=name, grid=(B, nb),
        in_specs=[pl.BlockSpec((None, WINDOW, D_MODEL), cur), tab(prev), tab(cur),
                  *_swa_z_specs()[1:],
                  tab(prev), tab(prev), tab(cur), tab(cur),
                  pl.BlockSpec(memory_space=pltpu.SMEM),
                  pl.BlockSpec((None, WINDOW, ATT_HEADS), cur),
                  pl.BlockSpec((None, WINDOW, D_MODEL), cur),
                  ANY],
        out_specs=[_swa_z_specs()[0],
                   pl.BlockSpec((None, WINDOW, 2 * KV_W), cur), pl.BlockSpec((None, WINDOW, 2 * KV_W), cur),
                   pl.BlockSpec((1, ATT_HEADS), lambda b, i: (0, 0))],
        out_shape=[jax.ShapeDtypeStruct(dz.shape, BF16),
                   jax.ShapeDtypeStruct((B, S, 2 * KV_W), F32), jax.ShapeDtypeStruct((B, S, 2 * KV_W), F32),
                   jax.ShapeDtypeStruct((1, ATT_HEADS), F32)],
        input_output_aliases={12: 0},
        compiler_params=_params("arbitrary", "arbitrary"),
    )(qr, kr, kr, z, z, cos, sin, cos, sin, sinks, lse, dob, dz)


def _swa_dkv_combine(dkv_cur, dkv_prev, dz, *, name):
    B, S, W = dkv_cur.shape

    def body(c_ref, p_ref, dz_in, o_ref):
        rows = lax.broadcasted_iota(jnp.int32, (S, W), 0)
        o_ref[...] = (c_ref[...] + _shift_up(p_ref[...], WINDOW, rows, S)).astype(BF16)

    seq = pl.BlockSpec((None, S, W), lambda b: (b, 0, 0))
    return pl.pallas_call(
        body, name=name, grid=(B,),
        in_specs=[seq, seq, ANY], out_specs=pl.BlockSpec((None, S, W), lambda b: (b, 0, O_AKV // W_AKV)),
        out_shape=jax.ShapeDtypeStruct(dz.shape, BF16),
        input_output_aliases={2: 0},
        compiler_params=_params("parallel"),
    )(dkv_cur, dkv_prev, dz)


def _rope_tables(positions):
    half = ROPE_DIM // 2
    inv = ROPE_THETA ** (-2.0 * jnp.arange(half, dtype=F32) / ROPE_DIM)
    ang = positions.astype(F32)[..., None] * inv
    c, s = jnp.cos(ang), jnp.sin(ang)
    pad = jnp.zeros(ang.shape[:-1] + (ATT_HD - ROPE_DIM,), F32)
    cos = jnp.concatenate([c, c, pad + 1.0], axis=-1)
    sin = jnp.concatenate([-s, s, pad], axis=-1)
    return jnp.tile(cos, (1, 1, 2)), jnp.tile(sin, (1, 1, 2))


def _lower_bound(lb_logits, *, name):
    def body(l_ref, o_ref):
        l = l_ref[...]
        e = jnp.exp(l - jnp.max(l, axis=0, keepdims=True))
        o_ref[...] = e[0:1] / jnp.sum(e, axis=0, keepdims=True)

    return pl.pallas_call(body, name=name, out_shape=jax.ShapeDtypeStruct((1, lb_logits.shape[1]), F32))(lb_logits)


W_ZH, W_GATES, W_AQ, W_AKV = 4 * HF, 2 * D_MODEL, ATT_HEADS * ATT_HD, 2 * KV_W
O_ZH, O_GATES, O_AQ, O_AKV = 0, W_ZH, W_ZH + W_GATES, W_ZH + W_GATES + W_AQ
W_IN = W_ZH + W_GATES + W_AQ + W_AKV


W_IN_BLK = W_IN // N_DEV


def _reference_row_block(j, rows=256):
    nz, ng = W_ZH // rows, W_GATES // rows
    return jnp.where(j < nz, j, jnp.where(j < nz + ng, j + (W_AQ + W_AKV) // rows, j - ng))


def _reordered_rows(w_t, *, name):
    rows = 256

    def body(i_ref, o_ref):
        o_ref[...] = i_ref[...]

    return pl.pallas_call(
        body, name=name, grid=(W_IN // rows,),
        in_specs=[pl.BlockSpec((rows, D_MODEL), lambda j: (_reference_row_block(j, rows), 0))],
        out_specs=pl.BlockSpec((rows, D_MODEL), lambda j: (j, 0)),
        out_shape=jax.ShapeDtypeStruct(w_t.shape, w_t.dtype), compiler_params=_params("parallel"))(w_t)


def _local_step(x, positions, target, small, w_in_t, rest_weights, emit, start_token):
    B, S, D = x.shape
    T = B * S
    x2 = x.reshape(T, D)
    cos, sin = _rope_tables(positions)
    lb = _lower_bound(small["lb_logits"], name="lb_fwd")
    zero = lambda tok: tok[0:1, 0:1]

    u1 = _norm_cast(x2, small["norm1_g"] + zero(start_token), name="norm1")
    z = _matmul(u1, w_in_t, tb=True, out_dtype=BF16, name="mm_z", tm=1024, tn=W_IN // 2)
    z3 = z.reshape(B, S, W_IN)
    oa, states = _hgrn_fwd(z3, lb, small["hgrn_norm_g"], name="hgrn_fwd")
    ob, lse, qr, kr = _swa_fwd(z3, cos, sin, small["attn_sinks"], name="swa_fwd")
    oa2 = oa.reshape(T, D)
    ob2 = ob.reshape(T, D)
    W = rest_weights("mix", ob)
    row = lambda tm, dtype=None: _row_spec(tm, D)
    tile = lambda dtype: jax.ShapeDtypeStruct((T, D), dtype)
    vec = _full_spec((1, D))
    vec_shape = jax.ShapeDtypeStruct((1, D), F32)

    def merge_ep(acc_a, acc_b, g_ref):
        pa, pb = acc_a.astype(BF16), acc_b.astype(BF16)
        return pa, pb, _merge_fn(g_ref[...], pa, pb)

    pa, pb, merged = _matmul_ep([(oa2, W["w_a"], False, 0), (ob2, W["w_b"], False, 0)], tm=1024, ins=[z], in_specs=[_gates_spec(1024)],
                                out_shapes=[tile(BF16)] * 3, out_specs=[row(1024)] * 3, epilogue=merge_ep, name="mm_pa_pb_merge")

    def resid_norm_ep(acc, x_ref, g_ref):
        hh = acc + x_ref[...]
        return hh, _rms(hh, g_ref[...])

    h, u2 = _matmul_ep([(merged, W["w_out"], False, 0)], tm=1024, ins=[x2, small["norm2_g"]], in_specs=[row(1024), vec],
                       out_shapes=[tile(F32), tile(BF16)], out_specs=[row(1024), row(1024)], epilogue=resid_norm_ep, name="mm_h_norm2")
    W.update(rest_weights("ffn", u2))
    gu = _matmul(u2, W["w_ffn_t"], tb=True, out_dtype=BF16, name="mm_gu", tm=1024, tn=D_FF)
    gu3 = gu.reshape(B, S, 2 * D_FF)
    act, a_pre = _conv_act_fwd(gu3, W["conv_w"], small["conv_b"], name="conv_act_fwd")
    act2 = act.reshape(T, D_FF)
    g = {}

    def loss_ep(acc, h_ref, g_ref, t_ref):
        y, vjp = jax.vjp(_rms, acc + h_ref[...], g_ref[...])
        err = y - t_ref[...]
        dx, dg = vjp(err * (1.0 / D))
        return dx, dx, dg, (0.5 / D) * jnp.sum(jnp.sum(err * err, axis=1, keepdims=True), axis=0, keepdims=True)

    dh2, dh2b, g["final_g"], loss = _matmul_ep(
        [(act2, W["w_down"], False, 0)], tm=512, ins=[h, small["final_g"].reshape(1, D), target.reshape(T, D)], in_specs=[row(512), vec, row(512)],
        out_shapes=[tile(F32), tile(BF16), vec_shape, jax.ShapeDtypeStruct((1, 1), F32)],
        out_specs=[row(512), row(512), vec, _full_spec((1, 1))], sums=(2, 3), epilogue=loss_ep, name="mm_h2_loss")
    dact = _matmul(dh2b, W["w_down"], tb=True, out_dtype=BF16, name="mm_dact", tm=1024, tn=D_FF)
    dw_down_t = _matmul(dh2b, act2, ta=True, out_dtype=BF16, name="mm_dw_down", tm=1024, tn=256, tk=8192)
    dg_, dup, g["conv_w"], g["conv_b"] = _conv_act_bwd(gu3, a_pre, W["conv_w"], dact.reshape(B, S, D_FF), name="conv_act_bwd")
    dg2 = dg_.reshape(T, D_FF)
    dup2 = dup.reshape(T, D_FF)
    dw_ffn_t = _matmul(u2, dg2, ta=True, out_t=True, out_dtype=BF16, into=lax.empty((2 * D_FF, D), BF16), o_noff=0, name="mm_dw_ffn_g", tm=1024, tn=256, tk=8192)
    dw_ffn_t = _matmul(u2, dup2, ta=True, out_t=True, out_dtype=BF16, into=dw_ffn_t, o_noff=D_FF // 256, name="mm_dw_ffn_u", tm=1024, tn=256, tk=8192)
    tok = emit("ffn", dict(w_ffn_t=dw_ffn_t, w_down=dw_down_t.T))
    def norm2_bwd_ep(acc_g, acc_u, h_ref, g_ref, dh2_ref):
        _, vjp = jax.vjp(_rms, h_ref[...], g_ref[...])
        dx, dg = vjp(acc_g + acc_u)
        dx = dx + dh2_ref[...]
        return dx, dx, dg

    dh, dhb, g["norm2_g"] = _matmul_ep(
        [(dg2, W["w_ffn_t"], False, 0), (dup2, W["w_ffn_t"], False, 1)], tm=512, ins=[h, small["norm2_g"] + zero(tok), dh2], in_specs=[row(512), vec, row(512)],
        out_shapes=[tile(F32), tile(BF16), vec_shape], out_specs=[row(512), row(512), vec], sums=(2,), epilogue=norm2_bwd_ep, name="mm_du2_norm2_bwd")
    dw_out = _matmul(merged, dhb, ta=True, out_dtype=BF16, name="mm_dw_out", tm=1024, tn=1024, tk=2048)

    def merge_bwd_ep(acc, g_ref, pa_ref, pb_ref, dz_in):
        gt = g_ref[...].astype(F32)
        sa = _sigmoid(gt[:, :D_MODEL])
        sb = _sigmoid(gt[:, D_MODEL:])
        dgates = jnp.concatenate([acc * pa_ref[...].astype(F32) * sa * (1.0 - sa), acc * pb_ref[...].astype(F32) * sb * (1.0 - sb)], axis=1)
        return dgates, acc * sa, acc * sb

    dz, dpa, dpb = _matmul_ep(
        [(dhb, W["w_out"], True, 0)], tm=512, ins=[z, pa, pb, lax.empty((T, W_IN), BF16)], in_specs=[_gates_spec(512), row(512), row(512), ANY],
        out_shapes=[jax.ShapeDtypeStruct((T, W_IN), BF16), tile(BF16), tile(BF16)], out_specs=[_gates_spec(512), row(512), row(512)],
        aliases={3: 0}, epilogue=merge_bwd_ep, name="mm_dmerged_merge_bwd")
    doa, dob = _matmul_ep([(dpa, W["w_a"], True, 0), (dpb, W["w_b"], True, 0)], tm=1024, ins=[], in_specs=[],
                          out_shapes=[tile(BF16)] * 2, out_specs=[row(1024)] * 2, epilogue=lambda da, db: (da, db), name="mm_doa_dob")
    dw_a = _matmul(oa2, dpa, ta=True, out_dtype=BF16, name="mm_dw_a", tm=1024, tn=1024, tk=2048)
    dw_b = _matmul(ob2, dpb, ta=True, out_dtype=BF16, name="mm_dw_b", tm=1024, tn=1024, tk=2048)
    tok = emit("mix", dict(w_out=dw_out, w_a=dw_a, w_b=dw_b))
    dz3, dkv_cur, dkv_prev, dsinks = _swa_bwd(z3, qr, kr, cos, sin, small["attn_sinks"] + zero(tok), lse, dob.reshape(B, S, D),
                                              dz.reshape(B, S, W_IN), name="swa_bwd")
    dz3 = _swa_dkv_combine(dkv_cur, dkv_prev, dz3, name="swa_dkv")
    g["attn_sinks"] = dsinks
    dz3, g["lb"], g["hgrn_norm_g"] = _hgrn_bwd(z3, lb, small["hgrn_norm_g"], states, doa.reshape(B, S, D), dz3, name="hgrn_bwd")
    dz = dz3.reshape(T, W_IN)
    dw_in_t = _matmul(u1, dz, ta=True, out_t=True, o_block_perm=_reference_row_block, out_dtype=BF16, name="mm_dw_in", tm=1024, tn=256, tk=8192)
    tok = emit("in", dict(w_in_t=dw_in_t))
    du1 = _matmul(dz, w_in_t, after=tok, out_dtype=BF16, name="mm_du1", tm=1024, tn=512)
    dx, g["norm1_g"] = _norm_bwd_add(x2, small["norm1_g"], du1, dh, with_bf16=False, name="norm1_bwd")
    g["lb_logits"] = _lb_bwd(g.pop("lb"), lb, name="lb_bwd")
    return loss, dx.reshape(B, S, D), g


def _my_place():
    return lax.axis_index("x"), lax.axis_index("y"), lax.axis_index("c")


def _gather_blocks(x_ref, out_ref, send_sems, recv_sems, local_sem):
    x, y, c = _my_place()
    me, sibling = (x, y, c), (x, y, 1 - c)
    chips = [(1 - x, y), (x, 1 - y), (1 - x, 1 - y)]

    def slot(px, py, pc):
        return out_ref.at[4 * px + 2 * py + pc]

    def copy(k, block, to, src=None):
        return pltpu.make_async_remote_copy(
            src_ref=slot(*block) if src is None else src, dst_ref=slot(*block),
            send_sem=send_sems.at[k], recv_sem=recv_sems.at[k], device_id=to, device_id_type=MESH)

    mine = pltpu.make_async_copy(x_ref, slot(*me), local_sem)
    mine.start()
    first = [copy(0, me, sibling, src=x_ref)]
    first += [copy(1 + j, me, (*chip, c), src=x_ref) for j, chip in enumerate(chips)]
    for cp in first:
        cp.start()
    passed = [copy(4 + j, (*chip, c), sibling) for j, chip in enumerate(chips)]
    for j, chip in enumerate(chips):
        copy(1 + j, (*chip, c), me).wait_recv()
        passed[j].start()
    copy(0, sibling, me).wait_recv()
    for j, chip in enumerate(chips):
        copy(4 + j, (*chip, 1 - c), me).wait_recv()
    for cp in first + passed:
        cp.wait_send()
    mine.wait()


GATHER_SEMS = [pltpu.SemaphoreType.DMA((7,)), pltpu.SemaphoreType.DMA((7,)), pltpu.SemaphoreType.DMA]


def _all_gather(blk, *, name):
    return pl.pallas_call(
        _gather_body_fn(), name=name,
        out_shape=jax.ShapeDtypeStruct((N_DEV,) + blk.shape, blk.dtype),
        in_specs=[ANY], out_specs=ANY,
        scratch_shapes=GATHER_SEMS,
    )(blk)


def _gather_body_fn():
    def body(x_ref, out_ref, send_sems, recv_sems, local_sem):
        _gather_blocks(x_ref, out_ref, send_sems, recv_sems, local_sem)
    return body


SLAB_W = 1152
SMALL_SHAPES = dict(norm1_g=(1, D_MODEL), lb_logits=(2, HGRN_HEADS * HGRN_DK), hgrn_norm_g=(1, HGRN_DK), attn_sinks=(1, ATT_HEADS),
                    norm2_g=(1, D_MODEL), conv_b=(1, D_FF), final_g=(1, D_MODEL))
CONVW_BLK = D_FF // N_DEV
CONVW_STRIDE = SLAB_W // 3


def _slab_layout():
    layout, r = {}, 0
    for nm, (nr, w) in SMALL_SHAPES.items():
        layout[nm] = []
        for i in range(nr):
            for c0 in range(0, w, SLAB_W):
                layout[nm].append((r, i, c0, min(SLAB_W, w - c0)))
                r += 1
    return layout, r


SMALL_ROWS, _N_SMALL_ROWS = _slab_layout()
CONV_ROW0 = -(-_N_SMALL_ROWS // 8) * 8
LOSS_ROW = CONV_ROW0 + N_DEV
SLAB_ROWS = LOSS_ROW + 8


def _small_step(grads, g_conv_w, loss, params, moments, variances, dev, *, name):
    names = list(SMALL_ROWS)
    n = len(names)

    def body(dev_ref, *refs):
        g_refs = dict(zip(names, refs[:n]))
        gc_ref, loss_ref = refs[n], refs[n + 1]
        base = n + 2
        w_refs, m_refs, v_refs = (dict(zip(names + ["conv_w"], refs[base + i * (n + 1):base + (i + 1) * (n + 1)])) for i in range(3))
        o = base + 3 * (n + 1)
        gath_ref, loss_out = refs[o], refs[o + 1]
        outs = {nm: refs[o + 2 + 4 * i:o + 6 + 4 * i] for i, nm in enumerate(names + ["conv_w"])}
        slab, total, send_sems, recv_sems, local_sem = refs[-5:]

        slab[...] = jnp.zeros_like(slab)
        for nm, pieces in SMALL_ROWS.items():
            for r, i, c0, w in pieces:
                slab[r:r + 1, 0:w] = g_refs[nm][i:i + 1, c0:c0 + w]
        for p in range(N_DEV):
            for j in range(3):
                slab[CONV_ROW0 + p:CONV_ROW0 + p + 1, j * CONVW_STRIDE:j * CONVW_STRIDE + CONVW_BLK] = gc_ref[j:j + 1, p * CONVW_BLK:(p + 1) * CONVW_BLK]
        slab[LOSS_ROW:LOSS_ROW + 1, 0:1] = loss_ref[...]
        _gather_blocks(slab, gath_ref, send_sems, recv_sems, local_sem)
        acc = gath_ref[0]
        for p in range(1, N_DEV):
            acc = acc + gath_ref[p]
        total[...] = acc
        loss_out[...] = total[LOSS_ROW:LOSS_ROW + 1, 0:1]

        def update(nm, g, i, c0, w):
            at = (slice(i, i + 1), slice(c0, c0 + w))
            d, mn, vn = _adamw_math(w_refs[nm][at], g, m_refs[nm][at], v_refs[nm][at])
            for ref, val in zip(outs[nm], (g, d, mn, vn)):
                ref[at] = val

        for nm, pieces in SMALL_ROWS.items():
            for r, i, c0, w in pieces:
                update(nm, total[r:r + 1, 0:w], i, c0, w)
        conv_rows = total[CONV_ROW0:CONV_ROW0 + N_DEV, :]
        rowid = lax.broadcasted_iota(jnp.int32, conv_rows.shape, 0)
        mine = jnp.sum(jnp.where(rowid == dev_ref[0], conv_rows, 0.0), axis=0, keepdims=True)
        for j in range(3):
            update("conv_w", mine[:, j * CONVW_STRIDE:j * CONVW_STRIDE + CONVW_BLK], j, 0, CONVW_BLK)

    order = names + ["conv_w"]
    ins = [grads[nm] for nm in names] + [g_conv_w, loss]
    for d in (params, moments, variances):
        ins += [d[nm] for nm in order]
    vmem = pl.BlockSpec(memory_space=pltpu.VMEM)
    out_shape = [jax.ShapeDtypeStruct((N_DEV, SLAB_ROWS, SLAB_W), F32), jax.ShapeDtypeStruct((1, 1), F32)]
    for nm in order:
        out_shape += [jax.ShapeDtypeStruct(params[nm].shape, F32)] * 4
    res = pl.pallas_call(
        body, name=name,
        grid_spec=pltpu.PrefetchScalarGridSpec(
            num_scalar_prefetch=1, grid=(1,),
            in_specs=[vmem] * len(ins), out_specs=[vmem] * len(out_shape),
            scratch_shapes=[pltpu.VMEM((SLAB_ROWS, SLAB_W), F32), pltpu.VMEM((SLAB_ROWS, SLAB_W), F32)] + GATHER_SEMS),
        out_shape=out_shape,
    )(dev, *ins)
    return res[1], {nm: tuple(res[2 + 4 * i:6 + 4 * i]) for i, nm in enumerate(order)}


HBM_SPEC = pl.BlockSpec(memory_space=pltpu.HBM)
SEM_SPEC = pl.BlockSpec(memory_space=pltpu.SEMAPHORE)
DATAFLOW_EFFECT = pltpu.SideEffectType.DATAFLOW_SIDE_EFFECTING
N_PEERS = N_DEV - 1


def _peers(x, y, c):
    return [(1 - x if r & 4 else x, 1 - y if r & 2 else y, 1 - c if r & 1 else c) for r in range(1, N_DEV)]


def _exchange_start(srcs, scatter, *, after=None, name):
    n = len(srcs)
    lands = [lax.empty(a.shape if scatter else (N_DEV,) + a.shape, a.dtype) for a in srcs]
    extra = [] if after is None else [after]

    def body(*refs):
        src_refs, land_refs = refs[:n], refs[n:2 * n]
        send_sems, recv_sems, token = refs[2 * n + len(extra)], refs[2 * n + len(extra) + 1], refs[-1]
        x, y, c = _my_place()
        me = 4 * x + 2 * y + c
        for i in range(n):
            for r, (tx, ty, tc) in enumerate(_peers(x, y, c)):
                src = src_refs[i].at[4 * tx + 2 * ty + tc] if scatter else src_refs[i]
                pltpu.make_async_remote_copy(
                    src_ref=src, dst_ref=land_refs[i].at[me], send_sem=send_sems.at[N_PEERS * i + r],
                    recv_sem=recv_sems.at[N_PEERS * i + r], device_id=(tx, ty, tc), device_id_type=MESH).start()
        token[...] = jnp.zeros_like(token)

    thru = [pltpu.HBM(a.shape, a.dtype) for a in list(srcs) + lands]
    res = pl.pallas_call(
        body, name=name,
        out_shape=(pltpu.SemaphoreType.DMA((N_PEERS * n,)), pltpu.SemaphoreType.DMA((N_PEERS * n,)), *thru,
                   jax.ShapeDtypeStruct((8, 128), F32)),
        in_specs=[HBM_SPEC] * (2 * n) + [ANY] * len(extra),
        out_specs=(SEM_SPEC, SEM_SPEC, *([HBM_SPEC] * (2 * n)), pl.BlockSpec(memory_space=pltpu.VMEM)),
        input_output_aliases={i: 2 + i for i in range(2 * n)},
        compiler_params=pltpu.CompilerParams(has_side_effects=DATAFLOW_EFFECT),
    )(*[pltpu.with_memory_space_constraint(a, pltpu.HBM) for a in list(srcs) + lands], *extra)
    return (res[0], res[1], list(res[2:2 + n]), list(res[2 + n:2 + 2 * n]), scatter), res[-1]


def _exchange_wait(handle, after, *, name):
    send_sems, recv_sems, srcs, lands, scatter = handle
    n = len(srcs)

    def body(*refs):
        src_refs, land_refs = refs[:n], refs[n:2 * n]
        send_sems, recv_sems = refs[2 * n], refs[2 * n + 1]
        x, y, c = _my_place()
        for i in range(n):
            for r in range(N_PEERS):
                src = src_refs[i].at[0] if scatter else src_refs[i]
                cp = pltpu.make_async_remote_copy(
                    src_ref=src, dst_ref=land_refs[i].at[0], send_sem=send_sems.at[N_PEERS * i + r],
                    recv_sem=recv_sems.at[N_PEERS * i + r], device_id=(x, y, c), device_id_type=MESH)
                cp.wait_send()
                cp.wait_recv()

    thru = [pltpu.HBM(a.shape, a.dtype) for a in srcs + lands]
    res = pl.pallas_call(
        body, name=name, out_shape=tuple(thru),
        in_specs=[HBM_SPEC] * (2 * n) + [SEM_SPEC, SEM_SPEC, ANY], out_specs=tuple([HBM_SPEC] * (2 * n)),
        input_output_aliases={i: i for i in range(2 * n)},
        compiler_params=pltpu.CompilerParams(has_side_effects=DATAFLOW_EFFECT),
    )(*srcs, *lands, send_sems, recv_sems, after)
    return list(res[:n]), list(res[n:])


def _with_own(land, own, me):
    return lax.dynamic_update_index_in_dim(land, own, me, 0)


def _adamw_math(w, g, m, v):
    m = ADAM_B1 * m + (1.0 - ADAM_B1) * g
    v = ADAM_B2 * v + (1.0 - ADAM_B2) * (g * g)
    m_hat = m / (1.0 - ADAM_B1 ** ADAM_STEP)
    v_hat = v / (1.0 - ADAM_B2 ** ADAM_STEP)
    delta = -ADAM_LR * (m_hat / (jnp.sqrt(v_hat) + ADAM_EPS) + ADAM_WD * w)
    return delta, m, v


def _adamw_sum(parts, w, m, v, *, name):
    shape = w.shape
    R, n = shape[-2], shape[-1]
    w, m, v = (t.reshape(R, n) for t in (w, m, v))
    tr = _pick(R, (256, 464, 352, 128))

    def body(p_ref, w_ref, m_ref, v_ref, g_ref, d_ref, mo_ref, vo_ref):
        g = p_ref[0].astype(F32)
        for p in range(1, N_DEV):
            g = g + p_ref[p].astype(F32)
        d, mn, vn = _adamw_math(w_ref[...], g, m_ref[...], v_ref[...])
        g_ref[...] = g
        d_ref[...] = d
        mo_ref[...] = mn
        vo_ref[...] = vn

    row = pl.BlockSpec((tr, n), lambda i: (i, 0))
    outs = pl.pallas_call(
        body, name=name, grid=(R // tr,),
        in_specs=[pl.BlockSpec((N_DEV, tr, n), lambda i: (0, i, 0)), row, row, row],
        out_specs=[row, row, row, row],
        out_shape=[jax.ShapeDtypeStruct((R, n), F32)] * 4,
        compiler_params=_params("parallel"),
    )(parts, w, m, v)
    return [t.reshape(shape) for t in outs]


def _lb_bwd(dlb, lb, *, name):
    def body(d_ref, lb_ref, o_ref):
        t = d_ref[...] * lb_ref[...] * (1.0 - lb_ref[...])
        o_ref[0:1, :] = t
        o_ref[1:2, :] = -t

    return pl.pallas_call(body, name=name, out_shape=jax.ShapeDtypeStruct((2, lb.shape[1]), F32))(dlb, lb)


DOWN_BLK, ROW_BLK = D_FF // N_DEV, D_MODEL // N_DEV
W_FFN_BLK = 2 * D_FF // N_DEV
CONV_BITS_SHAPE = (16, 256)


def kernel(x, positions, norm1_g, w_in, lb_logits, hgrn_norm_g, w_a, attn_sinks, w_b, w_out, norm2_g, w_ffn_in, conv_w, conv_b, w_down, final_g, loss_target, m_norm1_g, m_w_in, m_lb_logits, m_hgrn_norm_g, m_w_a, m_attn_sinks, m_w_b, m_w_out, m_norm2_g, m_w_ffn_in, m_conv_w, m_conv_b, m_w_down, m_final_g, v_norm1_g, v_w_in, v_lb_logits, v_hgrn_norm_g, v_w_a, v_attn_sinks, v_w_b, v_w_out, v_norm2_g, v_w_ffn_in, v_conv_w, v_conv_b, v_w_down, v_final_g):
    xi, yi, ci = _my_place()
    dev = 4 * xi + 2 * yi + ci

    tr = lambda t: jnp.transpose(t[0])
    untr = lambda t: jnp.transpose(t)[None]
    w_in_blocks = _all_gather(tr(w_in).astype(BF16), name="ag_w_in")
    conv_bits = lax.bitcast_convert_type(conv_w, BF16).reshape(-1)
    conv_bits = jnp.pad(conv_bits, (0, CONV_BITS_SHAPE[0] * CONV_BITS_SHAPE[1] - conv_bits.shape[0])).reshape(CONV_BITS_SHAPE)
    w_in_full_t = _reordered_rows(w_in_blocks.reshape(W_IN, D_MODEL), name="w_in_rows")
    gather_handles = {}
    gather_handles["mix"], tok_mix = _exchange_start([w_a[0].astype(BF16), w_b[0].astype(BF16), w_out[0].astype(BF16)], False,
                                                     after=w_in_full_t, name="ag_mix_start")
    gather_handles["ffn"], tok_ffn = _exchange_start([tr(w_ffn_in).astype(BF16), w_down[0].astype(BF16), conv_bits], False,
                                                     after=tok_mix, name="ag_ffn_start")
    start_token = tok_mix + tok_ffn

    def rest_weights(group, after):
        own, lands = _exchange_wait(gather_handles[group], after, name="ag_" + group + "_wait")
        full = [_with_own(l, o, dev) for l, o in zip(lands, own)]
        if group == "mix":
            return dict(zip(("w_a", "w_b", "w_out"), [t.reshape(D_MODEL, D_MODEL) for t in full]))
        bits = full[2].reshape(N_DEV, -1)[:, :3 * CONVW_BLK * 2].reshape(N_DEV, 3, CONVW_BLK, 2)
        return dict(w_ffn_t=full[0].reshape(2 * D_FF, D_MODEL), w_down=full[1].reshape(D_FF, D_MODEL),
                    conv_w=lax.bitcast_convert_type(bits, F32).transpose(1, 0, 2).reshape(3, D_FF))

    handles = {}

    def emit(group, gr):
        if group == "ffn":
            srcs = [gr["w_ffn_t"].reshape(N_DEV, W_FFN_BLK, D_MODEL), gr["w_down"].reshape(N_DEV, DOWN_BLK, D_MODEL)]
        elif group == "mix":
            srcs = [gr[n].reshape(N_DEV, ROW_BLK, D_MODEL) for n in ("w_out", "w_a", "w_b")]
        else:
            srcs = [gr["w_in_t"].reshape(N_DEV, W_IN_BLK, D_MODEL)]
        handles[group], token = _exchange_start(srcs, True, name="rs_" + group + "_start")
        return token

    small = dict(norm1_g=norm1_g, lb_logits=lb_logits, hgrn_norm_g=hgrn_norm_g, attn_sinks=attn_sinks, norm2_g=norm2_g,
                 conv_b=conv_b, final_g=final_g)
    loss, grad_x, g = _local_step(x, positions, loss_target, small, w_in_full_t, rest_weights, emit, start_token)

    def parts_of(group, after):
        srcs, lands = _exchange_wait(handles[group], after, name="rs_" + group + "_wait")
        return [_with_own(l, lax.dynamic_index_in_dim(s, dev, 0, keepdims=False), dev) for s, l in zip(srcs, lands)]

    p_ffn, p_down = parts_of("ffn", grad_x)
    p_out, p_a, p_b = parts_of("mix", grad_x)
    (p_in,) = parts_of("in", grad_x)
    big = dict(
        w_in=[untr(t) for t in _adamw_sum(p_in, tr(w_in), tr(m_w_in), tr(v_w_in), name="adamw_w_in")],
        w_a=_adamw_sum(p_a, w_a, m_w_a, v_w_a, name="adamw_w_a"),
        w_b=_adamw_sum(p_b, w_b, m_w_b, v_w_b, name="adamw_w_b"),
        w_out=_adamw_sum(p_out, w_out, m_w_out, v_w_out, name="adamw_w_out"),
        w_ffn_in=[untr(t) for t in _adamw_sum(p_ffn, tr(w_ffn_in), tr(m_w_ffn_in), tr(v_w_ffn_in), name="adamw_w_ffn_in")],
        w_down=_adamw_sum(p_down, w_down, m_w_down, v_w_down, name="adamw_w_down"),
    )

    row = lambda t: t.reshape(1, -1) if t.ndim == 1 else t
    shard = lambda t: t.reshape(3, CONVW_BLK)
    sm_g = {nm: g[nm] for nm in SMALL_ROWS}
    sm_w = dict(norm1_g=norm1_g, lb_logits=lb_logits, hgrn_norm_g=hgrn_norm_g, attn_sinks=attn_sinks, norm2_g=norm2_g,
                conv_b=conv_b, final_g=row(final_g), conv_w=shard(conv_w))
    sm_m = dict(norm1_g=m_norm1_g, lb_logits=m_lb_logits, hgrn_norm_g=m_hgrn_norm_g, attn_sinks=m_attn_sinks, norm2_g=m_norm2_g,
                conv_b=m_conv_b, final_g=row(m_final_g), conv_w=shard(m_conv_w))
    sm_v = dict(norm1_g=v_norm1_g, lb_logits=v_lb_logits, hgrn_norm_g=v_hgrn_norm_g, attn_sinks=v_attn_sinks, norm2_g=v_norm2_g,
                conv_b=v_conv_b, final_g=row(v_final_g), conv_w=shard(v_conv_w))
    loss_total, sm_out = _small_step(sm_g, g["conv_w"], loss, sm_w, sm_m, sm_v, dev.astype(jnp.int32).reshape(1), name="small_step")
    shapes = dict(final_g=final_g.shape, conv_w=conv_w.shape)

    names = ("norm1_g", "w_in", "lb_logits", "hgrn_norm_g", "w_a", "attn_sinks", "w_b", "w_out", "norm2_g", "w_ffn_in", "conv_w", "conv_b", "w_down", "final_g")
    outs = [loss_total.reshape(()), grad_x]
    for kind in range(4):
        outs += [big[n][kind] if n in big else sm_out[n][kind].reshape(shapes.get(n, sm_out[n][kind].shape)) for n in names]
    return tuple(outs)
```

```python
import jax
import jax.numpy as jnp
from jax import lax
from jax.experimental import pallas as pl
from jax.experimental.pallas import tpu as pltpu

F32 = jnp.float32
BF16 = jnp.bfloat16

D_MODEL = 1024
HGRN_HEADS = 8
HGRN_DK = 128
CHUNK = 64
ATT_HEADS = 16
ATT_KV_HEADS = 2
ATT_HD = 64
ATT_GROUP = ATT_HEADS // ATT_KV_HEADS
WINDOW = 128
ROPE_DIM = ATT_HD // 4
ROPE_THETA = 500000.0
D_FF = 2816
EPS = 1e-6
NEG_INF = -1e30
N_DEV = 8

ADAM_LR = 0.001
ADAM_B1 = 0.9
ADAM_B2 = 0.999
ADAM_EPS = 1e-08
ADAM_WD = 0.01
ADAM_STEP = 10

MESH = pl.DeviceIdType.MESH
ANY = pl.BlockSpec(memory_space=pl.ANY)


def _pick(n, cands):
    for c in cands:
        if n % c == 0:
            return c
    return n


def _sigmoid(x):
    return 0.5 * jnp.tanh(0.5 * x) + 0.5


def _silu(x):
    hx = 0.5 * x
    return hx * jnp.tanh(hx) + hx


def _rms(x, g):
    return x * lax.rsqrt(jnp.mean(x * x, axis=-1, keepdims=True) + EPS) * g


def _dot(a, b, dims):
    return lax.dot_general(a, b, (dims, ((), ())), preferred_element_type=F32)


def _nn(a, b):
    return _dot(a, b, ((1,), (0,)))


def _nt(a, b):
    return _dot(a, b, ((1,), (1,)))


def _tn(a, b):
    return _dot(a, b, ((0,), (0,)))


def _params(*sem):
    return pltpu.CompilerParams(dimension_semantics=sem, vmem_limit_bytes=56 * 1024 * 1024)


def _matmul(a, b, *, ta=False, tb=False, out_dtype=F32, addend=None, after=None, into=None, o_noff=0, out_t=False,
            o_block_perm=lambda j: j, name, tm, tn, tk=None, n_extent=None, b_koff=0, b_noff=0):
    M, K = (a.shape[1], a.shape[0]) if ta else a.shape
    N = n_extent or (b.shape[0] if tb else b.shape[1])
    tm, tn, tk = min(tm, M), min(tn, N), min(tk or K, K)
    assert M % tm == 0 and N % tn == 0 and K % tk == 0, (name, M, N, K, tm, tn, tk)
    nk = K // tk
    use_scratch = nk > 1 and out_dtype != F32
    grid = (M // tm, N // tn, nk)
    a_spec = pl.BlockSpec((tk, tm), lambda i, j, k: (k, i)) if ta else pl.BlockSpec((tm, tk), lambda i, j, k: (i, k))
    b_spec = pl.BlockSpec((tn, tk), lambda i, j, k: (j + b_noff, k + b_koff)) if tb else pl.BlockSpec((tk, tn), lambda i, j, k: (k + b_koff, j + b_noff))
    o_spec = pl.BlockSpec((tm, tn), lambda i, j, k: (i, j))
    dims = ((0 if ta else 1,), (1 if tb else 0,))
    has_add = addend is not None

    n_in = 2 + has_add + (after is not None) + (into is not None)

    def body(*refs):
        a_ref, b_ref = refs[:2]
        c_ref = refs[2] if has_add else None
        o_ref = refs[n_in]
        part = _dot(a_ref[...], b_ref[...], dims)
        if nk == 1:
            if has_add:
                part = part + c_ref[...].astype(F32)
            o_ref[...] = (part.T if out_t else part).astype(out_dtype)
        else:
            acc_ref = refs[-1] if use_scratch else o_ref
            k = pl.program_id(2)

            @pl.when(k == 0)
            def _():
                acc_ref[...] = part + c_ref[...].astype(F32) if has_add else part

            @pl.when(k > 0)
            def _():
                acc_ref[...] += part

            if use_scratch:
                @pl.when(k == nk - 1)
                def _():
                    o_ref[...] = acc_ref[...].astype(out_dtype)

    in_specs = [a_spec, b_spec] + ([o_spec] if has_add else [])
    args = (a, b) + ((addend,) if has_add else ())
    if after is not None:
        in_specs.append(pl.BlockSpec(after.shape, lambda i, j, k: (0, 0)))
        args += (after,)
    aliases = {}
    if into is not None:
        in_specs.append(ANY)
        args += (into,)
        aliases = {len(args) - 1: 0}
    if out_t:
        assert nk == 1 and not has_add
        o_spec = pl.BlockSpec((tn, tm), lambda i, j, k: (o_block_perm(j) + o_noff, i))
    elif into is not None:
        o_spec = pl.BlockSpec((tm, tn), lambda i, j, k: (i, j + o_noff))
    return pl.pallas_call(
        body,
        name=name,
        grid=grid,
        in_specs=in_specs,
        out_specs=o_spec,
        out_shape=jax.ShapeDtypeStruct(into.shape if into is not None else ((N, M) if out_t else (M, N)), out_dtype),
        input_output_aliases=aliases,
        scratch_shapes=[pltpu.VMEM((tm, tn), F32)] if use_scratch else [],
        compiler_params=_params("parallel", "parallel", "arbitrary"),
    )(*args)


def _matmul_ep(pairs, *, tm, ins, in_specs, out_shapes, out_specs, sums=(), epilogue, aliases=None, name):
    M = pairs[0][0].shape[0]
    tm = min(tm, M)
    mm_specs, mm_args, dims = [], [], []
    for a, b, tb, koff in pairs:
        K = a.shape[1]
        N = b.shape[0] if tb else b.shape[1]
        mm_specs += [pl.BlockSpec((tm, K), lambda i: (i, 0)),
                     pl.BlockSpec((N, K), lambda i, koff=koff: (0, koff)) if tb else pl.BlockSpec((K, N), lambda i, koff=koff: (koff, 0))]
        mm_args += [a, b]
        dims.append(((1,), (1 if tb else 0,)))
    n_mm = len(mm_args)
    n_in = n_mm + len(ins)

    def body(*refs):
        in_refs, out_refs = refs[n_mm:n_in], refs[n_in:]
        accs = [_dot(refs[2 * p][...], refs[2 * p + 1][...], dims[p]) for p in range(len(pairs))]
        outs = epilogue(*accs, *in_refs)
        for k, (ref, val) in enumerate(zip(out_refs, outs)):
            if val is None:
                continue
            if k in sums:
                @pl.when(pl.program_id(0) == 0)
                def _():
                    ref[...] = jnp.zeros_like(ref)

                ref[...] += val
            else:
                ref[...] = val.astype(ref.dtype)

    return pl.pallas_call(
        body, name=name, grid=(M // tm,),
        in_specs=mm_specs + list(in_specs),
        out_specs=list(out_specs), out_shape=list(out_shapes),
        input_output_aliases={n_mm + k: v for k, v in (aliases or {}).items()},
        compiler_params=_params("arbitrary"),
    )(*mm_args, *ins)


def _row_spec(tm, n):
    return pl.BlockSpec((tm, n), lambda i: (i, 0))


def _full_spec(shape):
    return pl.BlockSpec(shape, lambda i: tuple(0 for _ in shape))


def _norm_matmul(x, g, w_t, *, tm, tn, name):
    T, D = x.shape
    N = w_t.shape[0]
    tm = min(tm, T)

    def body(x_ref, g_ref, w_ref, u_ref, z_ref, u_scr):
        @pl.when(pl.program_id(1) == 0)
        def _():
            u = _rms(x_ref[...], g_ref[...]).astype(BF16)
            u_scr[...] = u
            u_ref[...] = u

        z_ref[...] = _nt(u_scr[...], w_ref[...]).astype(BF16)

    return pl.pallas_call(
        body, name=name, grid=(T // tm, N // tn),
        in_specs=[pl.BlockSpec((tm, D), lambda i, j: (i, 0)), pl.BlockSpec((1, D), lambda i, j: (0, 0)),
                  pl.BlockSpec((tn, D), lambda i, j: (j, 0))],
        out_specs=[pl.BlockSpec((tm, D), lambda i, j: (i, 0)), pl.BlockSpec((tm, tn), lambda i, j: (i, j))],
        out_shape=[jax.ShapeDtypeStruct((T, D), BF16), jax.ShapeDtypeStruct((T, N), BF16)],
        scratch_shapes=[pltpu.VMEM((tm, D), BF16)],
        compiler_params=_params("parallel", "arbitrary"),
    )(x, g, w_t)


def _norm_bwd_add(x, g, du, dres, *, with_bf16=True, name):
    T, D = x.shape
    tm = _pick(T, (512, 256, 128))

    def body(x_ref, g_ref, du_ref, dr_ref, dx_ref, *rest):
        dg_ref = rest[-1]
        _, vjp = jax.vjp(_rms, x_ref[...], g_ref[...])
        dx, dg = vjp(du_ref[...].astype(F32))
        dx = dx + dr_ref[...]
        dx_ref[...] = dx
        if with_bf16:
            rest[0][...] = dx.astype(BF16)

        @pl.when(pl.program_id(0) == 0)
        def _():
            dg_ref[...] = jnp.zeros_like(dg_ref)

        dg_ref[...] += dg

    row = _row_spec(tm, D)
    return pl.pallas_call(
        body, name=name, grid=(T // tm,),
        in_specs=[row, _full_spec((1, D)), row, row],
        out_specs=[row] + ([row] if with_bf16 else []) + [_full_spec((1, D))],
        out_shape=[jax.ShapeDtypeStruct((T, D), F32)] + ([jax.ShapeDtypeStruct((T, D), BF16)] if with_bf16 else []) + [jax.ShapeDtypeStruct((1, D), F32)],
        compiler_params=_params("arbitrary"),
    )(x, g, du, dres)


def _merge_fn(gates, a, b):
    ga = gates[:, :D_MODEL].astype(F32)
    gb = gates[:, D_MODEL:].astype(F32)
    return _sigmoid(ga) * a.astype(F32) + _sigmoid(gb) * b.astype(F32)


def _gates_spec(tm):
    return pl.BlockSpec((tm, W_GATES), lambda i: (i, O_GATES // W_GATES))


CONV_TC = 256


def _shift_down(x, n, rows):
    return jnp.where(rows >= n, pltpu.roll(x, n, 0), 0.0)


def _shift_up(x, n, rows, S):
    return jnp.where(rows < S - n, pltpu.roll(x, S - n, 0), 0.0)


def _conv_act_fwd(gu, conv_w, conv_b, *, name):
    B, S, _ = gu.shape
    tc = CONV_TC
    nc = D_FF // tc

    def body(g_ref, up_ref, w_ref, b_ref, o_ref, a_ref):
        g = g_ref[...].astype(F32)
        rows = lax.broadcasted_iota(jnp.int32, g.shape, 0)
        w = w_ref[...]
        a = w[2:3] * g + w[1:2] * _shift_down(g, 1, rows) + w[0:1] * _shift_down(g, 2, rows) + b_ref[...]
        o_ref[...] = (_silu(a) * up_ref[...].astype(F32)).astype(BF16)
        a_ref[...] = a.astype(BF16)

    col = pl.BlockSpec((None, S, tc), lambda b, j: (b, 0, j))
    return pl.pallas_call(
        body, name=name, grid=(B, nc),
        in_specs=[col,
                  pl.BlockSpec((None, S, tc), lambda b, j: (b, 0, j + nc)),
                  pl.BlockSpec((3, tc), lambda b, j: (0, j)),
                  pl.BlockSpec((1, tc), lambda b, j: (0, j))],
        out_specs=[col, col],
        out_shape=[jax.ShapeDtypeStruct((B, S, D_FF), BF16)] * 2,
        compiler_params=_params("parallel", "parallel"),
    )(gu, gu, conv_w, conv_b)


def _conv_act_bwd(gu, a_pre, conv_w, dact, *, name):
    B, S, _ = gu.shape
    tc = CONV_TC
    nc = D_FF // tc

    def body(g_ref, up_ref, a_ref, w_ref, da_ref, dg_ref, dup_ref, dw_ref, db_ref):
        g = g_ref[...].astype(F32)
        up, a, dact = up_ref[...], a_ref[...], da_ref[...]
        rows = lax.broadcasted_iota(jnp.int32, g.shape, 0)
        w = w_ref[...]
        sg = _sigmoid(a)
        dup_ref[...] = dact * a * sg
        da = (dact * up * sg * (1.0 + a * (1.0 - sg))).astype(F32)
        da1 = _shift_up(da, 1, rows, S)
        da2 = _shift_up(da, 2, rows, S)
        dg_ref[...] = (w[2:3] * da + w[1:2] * da1 + w[0:1] * da2).astype(BF16)

        @pl.when(pl.program_id(1) == 0)
        def _():
            dw_ref[...] = jnp.zeros_like(dw_ref)
            db_ref[...] = jnp.zeros_like(db_ref)

        dw_ref[0:1, :] += jnp.sum(da2 * g, axis=0, keepdims=True)
        dw_ref[1:2, :] += jnp.sum(da1 * g, axis=0, keepdims=True)
        dw_ref[2:3, :] += jnp.sum(da * g, axis=0, keepdims=True)
        db_ref[...] += jnp.sum(da, axis=0, keepdims=True)

    col = pl.BlockSpec((None, S, tc), lambda j, b: (b, 0, j))
    return pl.pallas_call(
        body, name=name, grid=(nc, B),
        in_specs=[col,
                  pl.BlockSpec((None, S, tc), lambda j, b: (b, 0, j + nc)),
                  col,
                  pl.BlockSpec((3, tc), lambda j, b: (0, j)),
                  col],
        out_specs=[col, col, pl.BlockSpec((3, tc), lambda j, b: (0, j)), pl.BlockSpec((1, tc), lambda j, b: (0, j))],
        out_shape=[jax.ShapeDtypeStruct((B, S, D_FF), BF16), jax.ShapeDtypeStruct((B, S, D_FF), BF16),
                   jax.ShapeDtypeStruct((3, D_FF), F32), jax.ShapeDtypeStruct((1, D_FF), F32)],
        compiler_params=_params("parallel", "arbitrary"),
    )(gu, gu, a_pre, conv_w, dact)


HGRN_CPB = 8
HF = HGRN_HEADS * HGRN_DK


def _tri(n, upper=False):
    r = lax.broadcasted_iota(jnp.int32, (n, n), 0)
    c = lax.broadcasted_iota(jnp.int32, (n, n), 1)
    return (c >= r) if upper else (r >= c)


def _hs(h):
    return slice(h * HGRN_DK, (h + 1) * HGRN_DK)


def _cumsum_rows(tri_b, x):
    hi = x.astype(BF16)
    lo = (x - hi.astype(F32)).astype(BF16)
    return _nn(tri_b, hi) + _nn(tri_b, lo)


def _hgrn_col(seg, h):
    return slice(seg * HF + h * HGRN_DK, seg * HF + (h + 1) * HGRN_DK)


def _hgrn_gates(q, fz, lb):
    sg = _sigmoid(fz)
    return _sigmoid(q), sg, lb + (1.0 - lb) * sg


def _hgrn_decays(b, q, sq, f):
    qf = q * sq
    k = 1.0 - f
    bref = b[CHUNK // 2:CHUNK // 2 + 1, :]
    blast = b[CHUNK - 1:CHUNK, :]
    e1 = jnp.exp2(b - bref)
    e2 = jnp.exp2(bref - b)
    e3 = e1 * jnp.exp2(bref)
    e4 = e2 * jnp.exp2(blast - bref)
    return (e1, e2, e3, e4), qf * e1, k * e2, qf * e3, k * e4, jnp.exp2(blast)


def _hgrn_fwd(zh, lb, gn, *, name):
    B, S, _ = zh.shape
    cpb = HGRN_CPB
    ts = cpb * CHUNK
    nblk = S // ts

    def body(z_ref, lb_ref, gn_ref, o_ref, st_ref, state):
        @pl.when(pl.program_id(1) == 0)
        def _():
            state[...] = jnp.zeros_like(state)

        R = range(HGRN_HEADS)
        causal = _tri(CHUNK)
        tril_b = causal.astype(BF16)
        lbh = [lb_ref[:, _hs(h)] for h in R]
        for c in range(cpb):
            rows = slice(c * CHUNK, (c + 1) * CHUNK)
            q = [z_ref[rows, _hgrn_col(0, h)].astype(F32) for h in R]
            gates = [_hgrn_gates(q[h], z_ref[rows, _hgrn_col(1, h)].astype(F32), lbh[h]) for h in R]
            b = [_cumsum_rows(tril_b, jnp.log2(gates[h][2])) for h in R]
            v = [z_ref[rows, _hgrn_col(2, h)] for h in R]
            dec, q_in, k_in, q_out, k_st = [], [], [], [], []
            for h in R:
                _, qi, ki, qo, ks, d = _hgrn_decays(b[h], q[h], gates[h][0], gates[h][2])
                dec.append(d)
                for lst, t in zip((q_in, k_in, q_out, k_st), (qi, ki, qo, ks)):
                    lst.append(t.astype(BF16))
            a = [jnp.where(causal, _nt(q_in[h], k_in[h]), 0.0).astype(BF16) for h in R]
            st = [state[h] for h in R]
            for h in R:
                st_ref[c, h] = st[h]
            o = [_nn(a[h], v[h]) + _nt(q_out[h], st[h].astype(BF16)) for h in R]
            for h in R:
                state[h] = st[h] * dec[h] + _tn(v[h], k_st[h])
            for h in R:
                o_ref[rows, _hs(h)] = (_rms(o[h], gn_ref[...]) * _silu(z_ref[rows, _hgrn_col(3, h)].astype(F32))).astype(BF16)

    return pl.pallas_call(
        body, name=name, grid=(B, nblk),
        in_specs=[pl.BlockSpec((None, ts, 4 * HF), lambda b, s: (b, s, 0)),
                  pl.BlockSpec((1, HF), lambda b, s: (0, 0)),
                  pl.BlockSpec((1, HGRN_DK), lambda b, s: (0, 0))],
        out_specs=[pl.BlockSpec((None, ts, HF), lambda b, s: (b, s, 0)),
                   pl.BlockSpec((None, cpb, HGRN_HEADS, HGRN_DK, HGRN_DK), lambda b, s: (b, s, 0, 0, 0))],
        out_shape=[jax.ShapeDtypeStruct((B, S, HF), BF16),
                   jax.ShapeDtypeStruct((B, S // CHUNK, HGRN_HEADS, HGRN_DK, HGRN_DK), F32)],
        scratch_shapes=[pltpu.VMEM((HGRN_HEADS, HGRN_DK, HGRN_DK), F32)],
        compiler_params=_params("arbitrary", "arbitrary"),
    )(zh, lb, gn)


def _hgrn_bwd(zh, lb, gn, states, doa, dz, *, name):
    B, S, _ = zh.shape
    cpb = HGRN_CPB
    ts = cpb * CHUNK
    nblk = S // ts
    rev = lambda b, s: (b, nblk - 1 - s, 0)

    def body(z_ref, lb_ref, gn_ref, st_ref, do_ref, dz_in, dz_ref, dlb_ref, dgn_ref, dstate):
        @pl.when(pl.program_id(1) == 0)
        def _():
            dstate[...] = jnp.zeros_like(dstate)

        @pl.when((pl.program_id(0) == 0) & (pl.program_id(1) == 0))
        def _():
            dlb_ref[...] = jnp.zeros_like(dlb_ref)
            dgn_ref[...] = jnp.zeros_like(dgn_ref)

        R = range(HGRN_HEADS)
        causal = _tri(CHUNK)
        tril_b = causal.astype(BF16)
        triu_b = _tri(CHUNK, upper=True).astype(BF16)
        rowid = lax.broadcasted_iota(jnp.int32, (CHUNK, HGRN_DK), 0)
        lbh = [lb_ref[:, _hs(h)] for h in R]
        gn = gn_ref[...]
        for c in reversed(range(cpb)):
            rows = slice(c * CHUNK, (c + 1) * CHUNK)
            q = [z_ref[rows, _hgrn_col(0, h)].astype(F32) for h in R]
            gates = [_hgrn_gates(q[h], z_ref[rows, _hgrn_col(1, h)].astype(F32), lbh[h]) for h in R]
            b = [_cumsum_rows(tril_b, jnp.log2(gates[h][2])) for h in R]
            v = [z_ref[rows, _hgrn_col(2, h)] for h in R]
            pre = [_hgrn_decays(b[h], q[h], gates[h][0], gates[h][2]) for h in R]
            q_in_b, k_in_b, q_out_b, k_st_b = ([pre[h][i].astype(BF16) for h in R] for i in (1, 2, 3, 4))
            a_b = [jnp.where(causal, _nt(q_in_b[h], k_in_b[h]), 0.0).astype(BF16) for h in R]
            st = [st_ref[c, h] for h in R]
            st_b = [t.astype(BF16) for t in st]
            o = [_nn(a_b[h], v[h]) + _nt(q_out_b[h], st_b[h]) for h in R]
            do_l, dgn_acc = [], jnp.zeros_like(gn)
            for h in R:
                hg = z_ref[rows, _hgrn_col(3, h)].astype(F32)
                dout = do_ref[rows, _hs(h)].astype(F32)
                shg = _sigmoid(hg)
                on_h, norm_vjp = jax.vjp(_rms, o[h], gn)
                d_o, d_gn = norm_vjp(dout * (hg * shg))
                do_l.append(d_o)
                dgn_acc = dgn_acc + d_gn
                dz_ref[rows, _hgrn_col(3, h)] = (dout * on_h * shg * (1.0 + hg * (1.0 - shg))).astype(BF16)
            dgn_ref[...] += dgn_acc
            do_b = [t.astype(BF16) for t in do_l]
            dst = [dstate[h] for h in R]
            dst_b = [t.astype(BF16) for t in dst]
            da_b = [jnp.where(causal, _nt(do_b[h], v[h]), 0.0).astype(BF16) for h in R]
            dv = [_tn(a_b[h], do_b[h]) + _nt(k_st_b[h], dst_b[h]) for h in R]
            dq_in = [_nn(da_b[h], k_in_b[h]) for h in R]
            dk_in = [_tn(da_b[h], q_in_b[h]) for h in R]
            dq_out = [_nn(do_b[h], st_b[h]) for h in R]
            dk_st = [_nn(v[h], dst_b[h]) for h in R]
            for h in R:
                dz_ref[rows, _hgrn_col(2, h)] = dv[h].astype(BF16)
            db = []
            for h in R:
                _, q_in, k_in, q_out, k_st, dec = pre[h]
                ddec = jnp.sum(st[h] * dst[h], axis=0, keepdims=True)
                t_qin, t_kin, t_kst = dq_in[h] * q_in, dk_in[h] * k_in, dk_st[h] * k_st
                dbref = jnp.sum(t_kin - t_qin, axis=0, keepdims=True)
                dblast = jnp.sum(t_kst, axis=0, keepdims=True) + ddec * dec
                db.append(t_qin - t_kin + dq_out[h] * q_out - t_kst
                          + jnp.where(rowid == CHUNK // 2, dbref, 0.0) + jnp.where(rowid == CHUNK - 1, dblast, 0.0))
            for h in R:
                dstate[h] = dst[h] * pre[h][5] + _tn(do_b[h], q_out_b[h])
            dlogf = [_cumsum_rows(triu_b, db[h]) for h in R]
            for h in R:
                sq, sg, f = gates[h]
                e1, e2, e3, e4 = pre[h][0]
                dqf = dq_in[h] * e1 + dq_out[h] * e3
                dk = dk_in[h] * e2 + dk_st[h] * e4
                df_open = (dlogf[h] / f - dk) * (1.0 - sg)
                dlb_ref[:, _hs(h)] += jnp.sum(df_open, axis=0, keepdims=True)
                dz_ref[rows, _hgrn_col(1, h)] = (df_open * ((1.0 - lbh[h]) * sg)).astype(BF16)
                dz_ref[rows, _hgrn_col(0, h)] = (dqf * sq * (1.0 + q[h] * (1.0 - sq))).astype(BF16)

    return pl.pallas_call(
        body, name=name, grid=(B, nblk),
        in_specs=[pl.BlockSpec((None, ts, 4 * HF), rev),
                  pl.BlockSpec((1, HF), lambda b, s: (0, 0)),
                  pl.BlockSpec((1, HGRN_DK), lambda b, s: (0, 0)),
                  pl.BlockSpec((None, cpb, HGRN_HEADS, HGRN_DK, HGRN_DK), lambda b, s: (b, nblk - 1 - s, 0, 0, 0)),
                  pl.BlockSpec((None, ts, HF), rev),
                  ANY],
        out_specs=[pl.BlockSpec((None, ts, 4 * HF), rev),
                   pl.BlockSpec((1, HF), lambda b, s: (0, 0)),
                   pl.BlockSpec((1, HGRN_DK), lambda b, s: (0, 0))],
        out_shape=[jax.ShapeDtypeStruct(dz.shape, BF16),
                   jax.ShapeDtypeStruct((1, HF), F32),
                   jax.ShapeDtypeStruct((1, HGRN_DK), F32)],
        input_output_aliases={5: 0},
        scratch_shapes=[pltpu.VMEM((HGRN_HEADS, HGRN_DK, HGRN_DK), F32)],
        compiler_params=_params("arbitrary", "arbitrary"),
    )(zh, lb, gn, states, doa, dz)


KV_W = ATT_KV_HEADS * ATT_HD
ATT_SCALE = ATT_HD ** -0.5


def _rope(x, cos, sin, inverse=False):
    half = ROPE_DIM // 2
    outs = []
    for p in range(x.shape[1] // 128):
        xp = x[:, p * 128:(p + 1) * 128]
        lane = lax.broadcasted_iota(jnp.int32, xp.shape, 1) % ATT_HD
        sw = jnp.where(lane < half, pltpu.roll(xp, 128 - half, 1), pltpu.roll(xp, half, 1))
        outs.append(xp * cos - sw * sin if inverse else xp * cos + sw * sin)
    return outs[0] if len(outs) == 1 else jnp.concatenate(outs, axis=1)


PAIRS_PER_KV = ATT_GROUP // 2


def _swap_halves(x):
    return pltpu.roll(x, ATT_HD, 1)


def _kv_padded(t, low):
    sw = _swap_halves(t)
    zero = jnp.zeros_like(t)
    out = []
    for g in range(ATT_KV_HEADS):
        in_low, in_high = (t, sw) if g == 0 else (sw, t)
        out.append((jnp.where(low, in_low, zero).astype(BF16), jnp.where(low, zero, in_high).astype(BF16)))
    return out


def _swa_mask(first_block):
    qi = lax.broadcasted_iota(jnp.int32, (WINDOW, 2 * WINDOW), 0)
    mi = lax.broadcasted_iota(jnp.int32, (WINDOW, 2 * WINDOW), 1)
    band = (mi > qi) & (mi <= qi + WINDOW)
    return band & (jnp.logical_not(first_block) | (mi >= WINDOW))


def _swa_specs(nb):
    cur = lambda b, i: (b, i, 0)
    prev = lambda b, i: (b, jnp.maximum(i - 1, 0), 0)
    return cur, prev


def _swa_z_specs():
    q = pl.BlockSpec((None, WINDOW, W_AQ), lambda b, i: (b, i, O_AQ // W_AQ))
    kv_prev = pl.BlockSpec((None, WINDOW, W_AKV), lambda b, i: (b, jnp.maximum(i - 1, 0), O_AKV // W_AKV))
    kv_cur = pl.BlockSpec((None, WINDOW, W_AKV), lambda b, i: (b, i, O_AKV // W_AKV))
    return q, kv_prev, kv_cur


def _swa_fwd(z, cos, sin, sinks, *, name):
    B, S, _ = z.shape
    nb = S // WINDOW
    cur, prev = _swa_specs(nb)

    def body(q_ref, kvp_ref, kvc_ref, cp_ref, sp_ref, cc_ref, sc_ref, sink_ref, o_ref, lse_ref, qr_ref, kr_ref):
        cos_c, sin_c = cc_ref[...], sc_ref[...]
        q = (_rope(q_ref[...].astype(F32), cos_c, sin_c) * ATT_SCALE).astype(BF16)
        k = jnp.concatenate([_rope(kvp_ref[:, :KV_W].astype(F32), cp_ref[...], sp_ref[...]),
                             _rope(kvc_ref[:, :KV_W].astype(F32), cos_c, sin_c)], axis=0)
        qr_ref[...] = q
        kr_ref[...] = k[WINDOW:].astype(BF16)
        v = jnp.concatenate([kvp_ref[:, KV_W:], kvc_ref[:, KV_W:]], axis=0).astype(F32)
        low = lax.broadcasted_iota(jnp.int32, k.shape, 1) < ATT_HD
        kpad = _kv_padded(k, low)
        vpad = _kv_padded(v, low)
        mask = _swa_mask(pl.program_id(1) == 0)
        lses = []
        for g in range(ATT_KV_HEADS):
            pairs = range(g * PAIRS_PER_KV, (g + 1) * PAIRS_PER_KV)
            keys = [(p, e) for p in pairs for e in (0, 1)]
            qp = {p: q[:, p * 128:(p + 1) * 128] for p in pairs}
            s = {pe: jnp.where(mask, _nt(qp[pe[0]], kpad[g][pe[1]]), NEG_INF) for pe in keys}
            pr = {}
            for pe in keys:
                sink = sink_ref[0, 2 * pe[0] + pe[1]]
                m = jnp.maximum(jnp.max(s[pe], axis=1, keepdims=True), sink)
                ex = jnp.exp(s[pe] - m)
                den = jnp.sum(ex, axis=1, keepdims=True) + jnp.exp(sink - m)
                pr[pe] = (ex * (1.0 / den)).astype(BF16)
                lses.append(m + jnp.log(den))
            for p in pairs:
                o_ref[:, p * 128:(p + 1) * 128] = (_nn(pr[p, 0], vpad[g][0]) + _nn(pr[p, 1], vpad[g][1])).astype(BF16)
        lse_ref[...] = jnp.concatenate(lses, axis=1)

    tab = lambda im: pl.BlockSpec((None, WINDOW, 128), im)
    return pl.pallas_call(
        body, name=name, grid=(B, nb),
        in_specs=[*_swa_z_specs(),
                  tab(prev), tab(prev), tab(cur), tab(cur),
                  pl.BlockSpec(memory_space=pltpu.SMEM)],
        out_specs=[pl.BlockSpec((None, WINDOW, D_MODEL), cur), pl.BlockSpec((None, WINDOW, ATT_HEADS), cur),
                   pl.BlockSpec((None, WINDOW, D_MODEL), cur), pl.BlockSpec((None, WINDOW, KV_W), cur)],
        out_shape=[jax.ShapeDtypeStruct((B, S, D_MODEL), BF16), jax.ShapeDtypeStruct((B, S, ATT_HEADS), F32),
                   jax.ShapeDtypeStruct((B, S, D_MODEL), BF16), jax.ShapeDtypeStruct((B, S, KV_W), BF16)],
        compiler_params=_params("parallel", "parallel"),
    )(z, z, z, cos, sin, cos, sin, sinks)


def _swa_bwd(z, qr, kr, cos, sin, sinks, lse, dob, dz, *, name):
    B, S, _ = z.shape
    nb = S // WINDOW
    cur, prev = _swa_specs(nb)

    def body(q_ref, krp_ref, krc_ref, kvp_ref, kvc_ref, cp_ref, sp_ref, cc_ref, sc_ref, sink_ref, lse_ref, do_ref, dz_in,
             dq_ref, dkc_ref, dkp_ref, dsink_ref):
        @pl.when((pl.program_id(0) == 0) & (pl.program_id(1) == 0))
        def _():
            dsink_ref[...] = jnp.zeros_like(dsink_ref)

        cos_c, sin_c, cos_p, sin_p = cc_ref[...], sc_ref[...], cp_ref[...], sp_ref[...]
        q = q_ref[...]
        k = jnp.concatenate([krp_ref[...], krc_ref[...]], axis=0).astype(F32)
        v = jnp.concatenate([kvp_ref[:, KV_W:], kvc_ref[:, KV_W:]], axis=0).astype(F32)
        low = lax.broadcasted_iota(jnp.int32, k.shape, 1) < ATT_HD
        kpad = _kv_padded(k, low)
        vpad = _kv_padded(v, low)
        mask = _swa_mask(pl.program_id(1) == 0)
        lse = lse_ref[...]
        dq_parts, dk_sum, dv_sum, dsinks = [], [], [], []
        for g in range(ATT_KV_HEADS):
            pairs = range(g * PAIRS_PER_KV, (g + 1) * PAIRS_PER_KV)
            keys = [(p, e) for p in pairs for e in (0, 1)]
            qp = {p: q[:, p * 128:(p + 1) * 128] for p in pairs}
            dop = {p: do_ref[:, p * 128:(p + 1) * 128] for p in pairs}
            s = {pe: jnp.where(mask, _nt(qp[pe[0]], kpad[g][pe[1]]), NEG_INF) for pe in keys}
            dp = {pe: _nt(dop[pe[0]], vpad[g][pe[1]]) for pe in keys}
            pr, ds = {}, {}
            for pe in keys:
                h = 2 * pe[0] + pe[1]
                lse_h = lse[:, h:h + 1]
                pf = jnp.exp(s[pe] - lse_h)
                delta = jnp.sum(pf * dp[pe], axis=1, keepdims=True)
                ds[pe] = (pf * (dp[pe] - delta)).astype(BF16)
                pr[pe] = pf.astype(BF16)
                p_sink = jnp.exp(sink_ref[0, h] - lse_h)
                dsinks.append(-jnp.sum(p_sink * delta, axis=0, keepdims=True))
            for p in pairs:
                dq_parts.append((_nn(ds[p, 0], kpad[g][0]) + _nn(ds[p, 1], kpad[g][1])) * ATT_SCALE)
            x = [sum(_tn(ds[p, e], qp[p]) for p in pairs) for e in (0, 1)]
            y = [sum(_tn(pr[p, e], dop[p]) for p in pairs) for e in (0, 1)]
            zk = jnp.where(low, x[0], x[1])
            zv = jnp.where(low, y[0], y[1])
            dk_sum.append(zk + _swap_halves(zk))
            dv_sum.append(zv + _swap_halves(zv))
        dq_ref[...] = _rope(jnp.concatenate(dq_parts, axis=1), cos_c, sin_c, inverse=True).astype(BF16)
        dk = jnp.where(low, dk_sum[0], dk_sum[1])
        dv = jnp.where(low, dv_sum[0], dv_sum[1])
        dkp_ref[:, :KV_W] = _rope(dk[:WINDOW], cos_p, sin_p, inverse=True)
        dkp_ref[:, KV_W:] = dv[:WINDOW]
        dkc_ref[:, :KV_W] = _rope(dk[WINDOW:], cos_c, sin_c, inverse=True)
        dkc_ref[:, KV_W:] = dv[WINDOW:]
        dsink_ref[...] += jnp.concatenate(dsinks, axis=1)

    tab = lambda im: pl.BlockSpec((None, WINDOW, 128), im)
    return pl.pallas_call(
        body, name=name, grid=(B, nb),
        in_specs=[pl.BlockSpec((None, WINDOW, D_MODEL), cur), tab(prev), tab(cur),
                  *_swa_z_specs()[1:],
                  tab(prev), tab(prev), tab(cur), tab(cur),
                  pl.BlockSpec(memory_space=pltpu.SMEM),
                  pl.BlockSpec((None, WINDOW, ATT_HEADS), cur),
                  pl.BlockSpec((None, WINDOW, D_MODEL), cur),
                  ANY],
        out_specs=[_swa_z_specs()[0],
                   pl.BlockSpec((None, WINDOW, 2 * KV_W), cur), pl.BlockSpec((None, WINDOW, 2 * KV_W), cur),
                   pl.BlockSpec((1, ATT_HEADS), lambda b, i: (0, 0))],
        out_shape=[jax.ShapeDtypeStruct(dz.shape, BF16),
                   jax.ShapeDtypeStruct((B, S, 2 * KV_W), F32), jax.ShapeDtypeStruct((B, S, 2 * KV_W), F32),
                   jax.ShapeDtypeStruct((1, ATT_HEADS), F32)],
        input_output_aliases={12: 0},
        compiler_params=_params("arbitrary", "arbitrary"),
    )(qr, kr, kr, z, z, cos, sin, cos, sin, sinks, lse, dob, dz)


def _swa_dkv_combine(dkv_cur, dkv_prev, dz, *, name):
    B, S, W = dkv_cur.shape

    def body(c_ref, p_ref, dz_in, o_ref):
        rows = lax.broadcasted_iota(jnp.int32, (S, W), 0)
        o_ref[...] = (c_ref[...] + _shift_up(p_ref[...], WINDOW, rows, S)).astype(BF16)

    seq = pl.BlockSpec((None, S, W), lambda b: (b, 0, 0))
    return pl.pallas_call(
        body, name=name, grid=(B,),
        in_specs=[seq, seq, ANY], out_specs=pl.BlockSpec((None, S, W), lambda b: (b, 0, O_AKV // W_AKV)),
        out_shape=jax.ShapeDtypeStruct(dz.shape, BF16),
        input_output_aliases={2: 0},
        compiler_params=_params("parallel"),
    )(dkv_cur, dkv_prev, dz)


def _rope_tables(positions):
    half = ROPE_DIM // 2
    inv = ROPE_THETA ** (-2.0 * jnp.arange(half, dtype=F32) / ROPE_DIM)
    ang = positions.astype(F32)[..., None] * inv
    c, s = jnp.cos(ang), jnp.sin(ang)
    pad = jnp.zeros(ang.shape[:-1] + (ATT_HD - ROPE_DIM,), F32)
    cos = jnp.concatenate([c, c, pad + 1.0], axis=-1)
    sin = jnp.concatenate([-s, s, pad], axis=-1)
    return jnp.tile(cos, (1, 1, 2)), jnp.tile(sin, (1, 1, 2))


def _lower_bound(lb_logits, *, name):
    def body(l_ref, o_ref):
        l = l_ref[...]
        e = jnp.exp(l - jnp.max(l, axis=0, keepdims=True))
        o_ref[...] = e[0:1] / jnp.sum(e, axis=0, keepdims=True)

    return pl.pallas_call(body, name=name, out_shape=jax.ShapeDtypeStruct((1, lb_logits.shape[1]), F32))(lb_logits)


W_ZH, W_GATES, W_AQ, W_AKV = 4 * HF, 2 * D_MODEL, ATT_HEADS * ATT_HD, 2 * KV_W
O_ZH, O_GATES, O_AQ, O_AKV = 0, W_ZH, W_ZH + W_GATES, W_ZH + W_GATES + W_AQ
W_IN = W_ZH + W_GATES + W_AQ + W_AKV


W_IN_BLK = W_IN // N_DEV


def _reference_row_block(j, rows=256):
    nz, ng = W_ZH // rows, W_GATES // rows
    return jnp.where(j < nz, j, jnp.where(j < nz + ng, j + (W_AQ + W_AKV) // rows, j - ng))


def _reordered_rows(w_t, *, name):
    rows = 256

    def body(i_ref, o_ref):
        o_ref[...] = i_ref[...]

    return pl.pallas_call(
        body, name=name, grid=(W_IN // rows,),
        in_specs=[pl.BlockSpec((rows, D_MODEL), lambda j: (_reference_row_block(j, rows), 0))],
        out_specs=pl.BlockSpec((rows, D_MODEL), lambda j: (j, 0)),
        out_shape=jax.ShapeDtypeStruct(w_t.shape, w_t.dtype), compiler_params=_params("parallel"))(w_t)


def _local_step(x, positions, target, small, w_in_t, rest_weights, emit, start_token):
    B, S, D = x.shape
    T = B * S
    x2 = x.reshape(T, D)
    cos, sin = _rope_tables(positions)
    lb = _lower_bound(small["lb_logits"], name="lb_fwd")
    zero = lambda tok: tok[0:1, 0:1]

    u1, z = _norm_matmul(x2, small["norm1_g"] + zero(start_token), w_in_t, tm=1024, tn=W_IN // 2, name="norm1_mm_z")
    z3 = z.reshape(B, S, W_IN)
    oa, states = _hgrn_fwd(z3, lb, small["hgrn_norm_g"], name="hgrn_fwd")
    ob, lse, qr, kr = _swa_fwd(z3, cos, sin, small["attn_sinks"], name="swa_fwd")
    oa2 = oa.reshape(T, D)
    ob2 = ob.reshape(T, D)
    W = rest_weights("mix", ob)
    row = lambda tm, dtype=None: _row_spec(tm, D)
    tile = lambda dtype: jax.ShapeDtypeStruct((T, D), dtype)
    vec = _full_spec((1, D))
    vec_shape = jax.ShapeDtypeStruct((1, D), F32)

    def merge_ep(acc_a, acc_b, g_ref):
        pa, pb = acc_a.astype(BF16), acc_b.astype(BF16)
        return pa, pb, _merge_fn(g_ref[...], pa, pb)

    pa, pb, merged = _matmul_ep([(oa2, W["w_a"], False, 0), (ob2, W["w_b"], False, 0)], tm=1024, ins=[z], in_specs=[_gates_spec(1024)],
                                out_shapes=[tile(BF16)] * 3, out_specs=[row(1024)] * 3, epilogue=merge_ep, name="mm_pa_pb_merge")

    def resid_norm_ep(acc, x_ref, g_ref):
        hh = acc + x_ref[...]
        return hh, _rms(hh, g_ref[...])

    h, u2 = _matmul_ep([(merged, W["w_out"], False, 0)], tm=1024, ins=[x2, small["norm2_g"]], in_specs=[row(1024), vec],
                       out_shapes=[tile(F32), tile(BF16)], out_specs=[row(1024), row(1024)], epilogue=resid_norm_ep, name="mm_h_norm2")
    W.update(rest_weights("ffn", u2))
    gu = _matmul(u2, W["w_ffn_t"], tb=True, out_dtype=BF16, name="mm_gu", tm=1024, tn=D_FF)
    gu3 = gu.reshape(B, S, 2 * D_FF)
    act, a_pre = _conv_act_fwd(gu3, W["conv_w"], small["conv_b"], name="conv_act_fwd")
    act2 = act.reshape(T, D_FF)
    g = {}

    def loss_ep(acc, h_ref, g_ref, t_ref):
        y, vjp = jax.vjp(_rms, acc + h_ref[...], g_ref[...])
        err = y - t_ref[...]
        dx, dg = vjp(err * (1.0 / D))
        return dx, dx, dg, (0.5 / D) * jnp.sum(jnp.sum(err * err, axis=1, keepdims=True), axis=0, keepdims=True)

    dh2, dh2b, g["final_g"], loss = _matmul_ep(
        [(act2, W["w_down"], False, 0)], tm=512, ins=[h, small["final_g"].reshape(1, D), target.reshape(T, D)], in_specs=[row(512), vec, row(512)],
        out_shapes=[tile(F32), tile(BF16), vec_shape, jax.ShapeDtypeStruct((1, 1), F32)],
        out_specs=[row(512), row(512), vec, _full_spec((1, 1))], sums=(2, 3), epilogue=loss_ep, name="mm_h2_loss")
    dact = _matmul(dh2b, W["w_down"], tb=True, out_dtype=BF16, name="mm_dact", tm=1024, tn=D_FF)
    dw_down_t = _matmul(dh2b, act2, ta=True, out_dtype=BF16, name="mm_dw_down", tm=1024, tn=256, tk=8192)
    dg_, dup, g["conv_w"], g["conv_b"] = _conv_act_bwd(gu3, a_pre, W["conv_w"], dact.reshape(B, S, D_FF), name="conv_act_bwd")
    dg2 = dg_.reshape(T, D_FF)
    dup2 = dup.reshape(T, D_FF)
    dw_ffn_t = _matmul(u2, dg2, ta=True, out_t=True, out_dtype=BF16, into=lax.empty((2 * D_FF, D), BF16), o_noff=0, name="mm_dw_ffn_g", tm=1024, tn=256, tk=8192)
    dw_ffn_t = _matmul(u2, dup2, ta=True, out_t=True, out_dtype=BF16, into=dw_ffn_t, o_noff=D_FF // 256, name="mm_dw_ffn_u", tm=1024, tn=256, tk=8192)
    tok = emit("ffn", dict(w_ffn_t=dw_ffn_t, w_down=dw_down_t.T))
    def norm2_bwd_ep(acc_g, acc_u, h_ref, g_ref, dh2_ref):
        _, vjp = jax.vjp(_rms, h_ref[...], g_ref[...])
        dx, dg = vjp(acc_g + acc_u)
        dx = dx + dh2_ref[...]
        return dx, dx, dg

    dh, dhb, g["norm2_g"] = _matmul_ep(
        [(dg2, W["w_ffn_t"], False, 0), (dup2, W["w_ffn_t"], False, 1)], tm=512, ins=[h, small["norm2_g"] + zero(tok), dh2], in_specs=[row(512), vec, row(512)],
        out_shapes=[tile(F32), tile(BF16), vec_shape], out_specs=[row(512), row(512), vec], sums=(2,), epilogue=norm2_bwd_ep, name="mm_du2_norm2_bwd")
    dw_out = _matmul(merged, dhb, ta=True, out_dtype=BF16, name="mm_dw_out", tm=1024, tn=1024, tk=2048)

    def merge_bwd_ep(acc, g_ref, pa_ref, pb_ref, dz_in):
        gt = g_ref[...].astype(F32)
        sa = _sigmoid(gt[:, :D_MODEL])
        sb = _sigmoid(gt[:, D_MODEL:])
        dgates = jnp.concatenate([acc * pa_ref[...].astype(F32) * sa * (1.0 - sa), acc * pb_ref[...].astype(F32) * sb * (1.0 - sb)], axis=1)
        return dgates, acc * sa, acc * sb

    dz, dpa, dpb = _matmul_ep(
        [(dhb, W["w_out"], True, 0)], tm=512, ins=[z, pa, pb, lax.empty((T, W_IN), BF16)], in_specs=[_gates_spec(512), row(512), row(512), ANY],
        out_shapes=[jax.ShapeDtypeStruct((T, W_IN), BF16), tile(BF16), tile(BF16)], out_specs=[_gates_spec(512), row(512), row(512)],
        aliases={3: 0}, epilogue=merge_bwd_ep, name="mm_dmerged_merge_bwd")
    doa, dob = _matmul_ep([(dpa, W["w_a"], True, 0), (dpb, W["w_b"], True, 0)], tm=1024, ins=[], in_specs=[],
                          out_shapes=[tile(BF16)] * 2, out_specs=[row(1024)] * 2, epilogue=lambda da, db: (da, db), name="mm_doa_dob")
    dw_a = _matmul(oa2, dpa, ta=True, out_dtype=BF16, name="mm_dw_a", tm=1024, tn=1024, tk=2048)
    dw_b = _matmul(ob2, dpb, ta=True, out_dtype=BF16, name="mm_dw_b", tm=1024, tn=1024, tk=2048)
    tok = emit("mix", dict(w_out=dw_out, w_a=dw_a, w_b=dw_b))
    dz3, dkv_cur, dkv_prev, dsinks = _swa_bwd(z3, qr, kr, cos, sin, small["attn_sinks"] + zero(tok), lse, dob.reshape(B, S, D),
                                              dz.reshape(B, S, W_IN), name="swa_bwd")
    dz3 = _swa_dkv_combine(dkv_cur, dkv_prev, dz3, name="swa_dkv")
    g["attn_sinks"] = dsinks
    dz3, g["lb"], g["hgrn_norm_g"] = _hgrn_bwd(z3, lb, small["hgrn_norm_g"], states, doa.reshape(B, S, D), dz3, name="hgrn_bwd")
    dz = dz3.reshape(T, W_IN)
    dw_in_t = _matmul(u1, dz, ta=True, out_t=True, o_block_perm=_reference_row_block, out_dtype=BF16, name="mm_dw_in", tm=1024, tn=256, tk=8192)
    tok = emit("in", dict(w_in_t=dw_in_t))
    du1 = _matmul(dz, w_in_t, after=tok, out_dtype=BF16, name="mm_du1", tm=1024, tn=512)
    dx, g["norm1_g"] = _norm_bwd_add(x2, small["norm1_g"], du1, dh, with_bf16=False, name="norm1_bwd")
    g["lb_logits"] = _lb_bwd(g.pop("lb"), lb, name="lb_bwd")
    return loss, dx.reshape(B, S, D), g


def _my_place():
    return lax.axis_index("x"), lax.axis_index("y"), lax.axis_index("c")


def _gather_blocks(x_ref, out_ref, send_sems, recv_sems, local_sem):
    x, y, c = _my_place()
    me, sibling = (x, y, c), (x, y, 1 - c)
    chips = [(1 - x, y), (x, 1 - y), (1 - x, 1 - y)]

    def slot(px, py, pc):
        return out_ref.at[4 * px + 2 * py + pc]

    def copy(k, block, to, src=None):
        return pltpu.make_async_remote_copy(
            src_ref=slot(*block) if src is None else src, dst_ref=slot(*block),
            send_sem=send_sems.at[k], recv_sem=recv_sems.at[k], device_id=to, device_id_type=MESH)

    mine = pltpu.make_async_copy(x_ref, slot(*me), local_sem)
    mine.start()
    first = [copy(0, me, sibling, src=x_ref)]
    first += [copy(1 + j, me, (*chip, c), src=x_ref) for j, chip in enumerate(chips)]
    for cp in first:
        cp.start()
    passed = [copy(4 + j, (*chip, c), sibling) for j, chip in enumerate(chips)]
    for j, chip in enumerate(chips):
        copy(1 + j, (*chip, c), me).wait_recv()
        passed[j].start()
    copy(0, sibling, me).wait_recv()
    for j, chip in enumerate(chips):
        copy(4 + j, (*chip, 1 - c), me).wait_recv()
    for cp in first + passed:
        cp.wait_send()
    mine.wait()


GATHER_SEMS = [pltpu.SemaphoreType.DMA((7,)), pltpu.SemaphoreType.DMA((7,)), pltpu.SemaphoreType.DMA]


def _all_gather(blk, *, name):
    return pl.pallas_call(
        _gather_body_fn(), name=name,
        out_shape=jax.ShapeDtypeStruct((N_DEV,) + blk.shape, blk.dtype),
        in_specs=[ANY], out_specs=ANY,
        scratch_shapes=GATHER_SEMS,
    )(blk)


def _gather_body_fn():
    def body(x_ref, out_ref, send_sems, recv_sems, local_sem):
        _gather_blocks(x_ref, out_ref, send_sems, recv_sems, local_sem)
    return body


SLAB_W = 1152
SMALL_SHAPES = dict(norm1_g=(1, D_MODEL), lb_logits=(2, HGRN_HEADS * HGRN_DK), hgrn_norm_g=(1, HGRN_DK), attn_sinks=(1, ATT_HEADS),
                    norm2_g=(1, D_MODEL), conv_b=(1, D_FF), final_g=(1, D_MODEL))
CONVW_BLK = D_FF // N_DEV
CONVW_STRIDE = SLAB_W // 3


def _slab_layout():
    layout, r = {}, 0
    for nm, (nr, w) in SMALL_SHAPES.items():
        layout[nm] = []
        for i in range(nr):
            for c0 in range(0, w, SLAB_W):
                layout[nm].append((r, i, c0, min(SLAB_W, w - c0)))
                r += 1
    return layout, r


SMALL_ROWS, _N_SMALL_ROWS = _slab_layout()
CONV_ROW0 = -(-_N_SMALL_ROWS // 8) * 8
LOSS_ROW = CONV_ROW0 + N_DEV
SLAB_ROWS = LOSS_ROW + 8


def _small_step(grads, g_conv_w, loss, params, moments, variances, dev, *, name):
    names = list(SMALL_ROWS)
    n = len(names)

    def body(dev_ref, *refs):
        g_refs = dict(zip(names, refs[:n]))
        gc_ref, loss_ref = refs[n], refs[n + 1]
        base = n + 2
        w_refs, m_refs, v_refs = (dict(zip(names + ["conv_w"], refs[base + i * (n + 1):base + (i + 1) * (n + 1)])) for i in range(3))
        o = base + 3 * (n + 1)
        gath_ref, loss_out = refs[o], refs[o + 1]
        outs = {nm: refs[o + 2 + 4 * i:o + 6 + 4 * i] for i, nm in enumerate(names + ["conv_w"])}
        slab, total, send_sems, recv_sems, local_sem = refs[-5:]

        slab[...] = jnp.zeros_like(slab)
        for nm, pieces in SMALL_ROWS.items():
            for r, i, c0, w in pieces:
                slab[r:r + 1, 0:w] = g_refs[nm][i:i + 1, c0:c0 + w]
        for p in range(N_DEV):
            for j in range(3):
                slab[CONV_ROW0 + p:CONV_ROW0 + p + 1, j * CONVW_STRIDE:j * CONVW_STRIDE + CONVW_BLK] = gc_ref[j:j + 1, p * CONVW_BLK:(p + 1) * CONVW_BLK]
        slab[LOSS_ROW:LOSS_ROW + 1, 0:1] = loss_ref[...]
        _gather_blocks(slab, gath_ref, send_sems, recv_sems, local_sem)
        acc = gath_ref[0]
        for p in range(1, N_DEV):
            acc = acc + gath_ref[p]
        total[...] = acc
        loss_out[...] = total[LOSS_ROW:LOSS_ROW + 1, 0:1]

        def update(nm, g, i, c0, w):
            at = (slice(i, i + 1), slice(c0, c0 + w))
            d, mn, vn = _adamw_math(w_refs[nm][at], g, m_refs[nm][at], v_refs[nm][at])
            for ref, val in zip(outs[nm], (g, d, mn, vn)):
                ref[at] = val

        for nm, pieces in SMALL_ROWS.items():
            for r, i, c0, w in pieces:
                update(nm, total[r:r + 1, 0:w], i, c0, w)
        conv_rows = total[CONV_ROW0:CONV_ROW0 + N_DEV, :]
        rowid = lax.broadcasted_iota(jnp.int32, conv_rows.shape, 0)
        mine = jnp.sum(jnp.where(rowid == dev_ref[0], conv_rows, 0.0), axis=0, keepdims=True)
        for j in range(3):
            update("conv_w", mine[:, j * CONVW_STRIDE:j * CONVW_STRIDE + CONVW_BLK], j, 0, CONVW_BLK)

    order = names + ["conv_w"]
    ins = [grads[nm] for nm in names] + [g_conv_w, loss]
    for d in (params, moments, variances):
        ins += [d[nm] for nm in order]
    vmem = pl.BlockSpec(memory_space=pltpu.VMEM)
    out_shape = [jax.ShapeDtypeStruct((N_DEV, SLAB_ROWS, SLAB_W), F32), jax.ShapeDtypeStruct((1, 1), F32)]
    for nm in order:
        out_shape += [jax.ShapeDtypeStruct(params[nm].shape, F32)] * 4
    res = pl.pallas_call(
        body, name=name,
        grid_spec=pltpu.PrefetchScalarGridSpec(
            num_scalar_prefetch=1, grid=(1,),
            in_specs=[vmem] * len(ins), out_specs=[vmem] * len(out_shape),
            scratch_shapes=[pltpu.VMEM((SLAB_ROWS, SLAB_W), F32), pltpu.VMEM((SLAB_ROWS, SLAB_W), F32)] + GATHER_SEMS),
        out_shape=out_shape,
    )(dev, *ins)
    return res[1], {nm: tuple(res[2 + 4 * i:6 + 4 * i]) for i, nm in enumerate(order)}


HBM_SPEC = pl.BlockSpec(memory_space=pltpu.HBM)
SEM_SPEC = pl.BlockSpec(memory_space=pltpu.SEMAPHORE)
DATAFLOW_EFFECT = pltpu.SideEffectType.DATAFLOW_SIDE_EFFECTING
N_PEERS = N_DEV - 1


def _peers(x, y, c):
    return [(1 - x if r & 4 else x, 1 - y if r & 2 else y, 1 - c if r & 1 else c) for r in range(1, N_DEV)]


def _exchange_start(srcs, scatter, *, after=None, name):
    n = len(srcs)
    lands = [lax.empty(a.shape if scatter else (N_DEV,) + a.shape, a.dtype) for a in srcs]
    extra = [] if after is None else [after]

    def body(*refs):
        src_refs, land_refs = refs[:n], refs[n:2 * n]
        send_sems, recv_sems, token = refs[2 * n + len(extra)], refs[2 * n + len(extra) + 1], refs[-1]
        x, y, c = _my_place()
        me = 4 * x + 2 * y + c
        for i in range(n):
            for r, (tx, ty, tc) in enumerate(_peers(x, y, c)):
                src = src_refs[i].at[4 * tx + 2 * ty + tc] if scatter else src_refs[i]
                pltpu.make_async_remote_copy(
                    src_ref=src, dst_ref=land_refs[i].at[me], send_sem=send_sems.at[N_PEERS * i + r],
                    recv_sem=recv_sems.at[N_PEERS * i + r], device_id=(tx, ty, tc), device_id_type=MESH).start()
        token[...] = jnp.zeros_like(token)

    thru = [pltpu.HBM(a.shape, a.dtype) for a in list(srcs) + lands]
    res = pl.pallas_call(
        body, name=name,
        out_shape=(pltpu.SemaphoreType.DMA((N_PEERS * n,)), pltpu.SemaphoreType.DMA((N_PEERS * n,)), *thru,
                   jax.ShapeDtypeStruct((8, 128), F32)),
        in_specs=[HBM_SPEC] * (2 * n) + [ANY] * len(extra),
        out_specs=(SEM_SPEC, SEM_SPEC, *([HBM_SPEC] * (2 * n)), pl.BlockSpec(memory_space=pltpu.VMEM)),
        input_output_aliases={i: 2 + i for i in range(2 * n)},
        compiler_params=pltpu.CompilerParams(has_side_effects=DATAFLOW_EFFECT),
    )(*[pltpu.with_memory_space_constraint(a, pltpu.HBM) for a in list(srcs) + lands], *extra)
    return (res[0], res[1], list(res[2:2 + n]), list(res[2 + n:2 + 2 * n]), scatter), res[-1]


def _exchange_wait(handle, after, *, name):
    send_sems, recv_sems, srcs, lands, scatter = handle
    n = len(srcs)

    def body(*refs):
        src_refs, land_refs = refs[:n], refs[n:2 * n]
        send_sems, recv_sems = refs[2 * n], refs[2 * n + 1]
        x, y, c = _my_place()
        for i in range(n):
            for r in range(N_PEERS):
                src = src_refs[i].at[0] if scatter else src_refs[i]
                cp = pltpu.make_async_remote_copy(
                    src_ref=src, dst_ref=land_refs[i].at[0], send_sem=send_sems.at[N_PEERS * i + r],
                    recv_sem=recv_sems.at[N_PEERS * i + r], device_id=(x, y, c), device_id_type=MESH)
                cp.wait_send()
                cp.wait_recv()

    thru = [pltpu.HBM(a.shape, a.dtype) for a in srcs + lands]
    res = pl.pallas_call(
        body, name=name, out_shape=tuple(thru),
        in_specs=[HBM_SPEC] * (2 * n) + [SEM_SPEC, SEM_SPEC, ANY], out_specs=tuple([HBM_SPEC] * (2 * n)),
        input_output_aliases={i: i for i in range(2 * n)},
        compiler_params=pltpu.CompilerParams(has_side_effects=DATAFLOW_EFFECT),
    )(*srcs, *lands, send_sems, recv_sems, after)
    return list(res[:n]), list(res[n:])


def _with_own(land, own, me):
    return lax.dynamic_update_index_in_dim(land, own, me, 0)


def _adamw_math(w, g, m, v):
    m = ADAM_B1 * m + (1.0 - ADAM_B1) * g
    v = ADAM_B2 * v + (1.0 - ADAM_B2) * (g * g)
    m_hat = m / (1.0 - ADAM_B1 ** ADAM_STEP)
    v_hat = v / (1.0 - ADAM_B2 ** ADAM_STEP)
    delta = -ADAM_LR * (m_hat / (jnp.sqrt(v_hat) + ADAM_EPS) + ADAM_WD * w)
    return delta, m, v


def _adamw_sum(parts, w, m, v, *, name):
    shape = w.shape
    R, n = shape[-2], shape[-1]
    w, m, v = (t.reshape(R, n) for t in (w, m, v))
    tr = _pick(R, (256, 464, 352, 128))

    def body(p_ref, w_ref, m_ref, v_ref, g_ref, d_ref, mo_ref, vo_ref):
        g = p_ref[0].astype(F32)
        for p in range(1, N_DEV):
            g = g + p_ref[p].astype(F32)
        d, mn, vn = _adamw_math(w_ref[...], g, m_ref[...], v_ref[...])
        g_ref[...] = g
        d_ref[...] = d
        mo_ref[...] = mn
        vo_ref[...] = vn

    row = pl.BlockSpec((tr, n), lambda i: (i, 0))
    outs = pl.pallas_call(
        body, name=name, grid=(R // tr,),
        in_specs=[pl.BlockSpec((N_DEV, tr, n), lambda i: (0, i, 0)), row, row, row],
        out_specs=[row, row, row, row],
        out_shape=[jax.ShapeDtypeStruct((R, n), F32)] * 4,
        compiler_params=_params("parallel"),
    )(parts, w, m, v)
    return [t.reshape(shape) for t in outs]


def _lb_bwd(dlb, lb, *, name):
    def body(d_ref, lb_ref, o_ref):
        t = d_ref[...] * lb_ref[...] * (1.0 - lb_ref[...])
        o_ref[0:1, :] = t
        o_ref[1:2, :] = -t

    return pl.pallas_call(body, name=name, out_shape=jax.ShapeDtypeStruct((2, lb.shape[1]), F32))(dlb, lb)


DOWN_BLK, ROW_BLK = D_FF // N_DEV, D_MODEL // N_DEV
W_FFN_BLK = 2 * D_FF // N_DEV
CONV_BITS_SHAPE = (16, 256)


def kernel(x, positions, norm1_g, w_in, lb_logits, hgrn_norm_g, w_a, attn_sinks, w_b, w_out, norm2_g, w_ffn_in, conv_w, conv_b, w_down, final_g, loss_target, m_norm1_g, m_w_in, m_lb_logits, m_hgrn_norm_g, m_w_a, m_attn_sinks, m_w_b, m_w_out, m_norm2_g, m_w_ffn_in, m_conv_w, m_conv_b, m_w_down, m_final_g, v_norm1_g, v_w_in, v_lb_logits, v_hgrn_norm_g, v_w_a, v_attn_sinks, v_w_b, v_w_out, v_norm2_g, v_w_ffn_in, v_conv_w, v_conv_b, v_w_down, v_final_g):
    xi, yi, ci = _my_place()
    dev = 4 * xi + 2 * yi + ci

    tr = lambda t: jnp.transpose(t[0])
    untr = lambda t: jnp.transpose(t)[None]
    w_in_blocks = _all_gather(tr(w_in).astype(BF16), name="ag_w_in")
    conv_bits = lax.bitcast_convert_type(conv_w, BF16).reshape(-1)
    conv_bits = jnp.pad(conv_bits, (0, CONV_BITS_SHAPE[0] * CONV_BITS_SHAPE[1] - conv_bits.shape[0])).reshape(CONV_BITS_SHAPE)
    w_in_full_t = _reordered_rows(w_in_blocks.reshape(W_IN, D_MODEL), name="w_in_rows")
    gather_handles = {}
    gather_handles["mix"], tok_mix = _exchange_start([w_a[0].astype(BF16), w_b[0].astype(BF16), w_out[0].astype(BF16)], False,
                                                     after=w_in_full_t, name="ag_mix_start")
    gather_handles["ffn"], tok_ffn = _exchange_start([tr(w_ffn_in).astype(BF16), w_down[0].astype(BF16), conv_bits], False,
                                                     after=tok_mix, name="ag_ffn_start")
    start_token = tok_mix + tok_ffn

    def rest_weights(group, after):
        own, lands = _exchange_wait(gather_handles[group], after, name="ag_" + group + "_wait")
        full = [_with_own(l, o, dev) for l, o in zip(lands, own)]
        if group == "mix":
            return dict(zip(("w_a", "w_b", "w_out"), [t.reshape(D_MODEL, D_MODEL) for t in full]))
        bits = full[2].reshape(N_DEV, -1)[:, :3 * CONVW_BLK * 2].reshape(N_DEV, 3, CONVW_BLK, 2)
        return dict(w_ffn_t=full[0].reshape(2 * D_FF, D_MODEL), w_down=full[1].reshape(D_FF, D_MODEL),
                    conv_w=lax.bitcast_convert_type(bits, F32).transpose(1, 0, 2).reshape(3, D_FF))

    handles = {}

    def emit(group, gr):
        if group == "ffn":
            srcs = [gr["w_ffn_t"].reshape(N_DEV, W_FFN_BLK, D_MODEL), gr["w_down"].reshape(N_DEV, DOWN_BLK, D_MODEL)]
        elif group == "mix":
            srcs = [gr[n].reshape(N_DEV, ROW_BLK, D_MODEL) for n in ("w_out", "w_a", "w_b")]
        else:
            srcs = [gr["w_in_t"].reshape(N_DEV, W_IN_BLK, D_MODEL)]
        handles[group], token = _exchange_start(srcs, True, name="rs_" + group + "_start")
        return token

    small = dict(norm1_g=norm1_g, lb_logits=lb_logits, hgrn_norm_g=hgrn_norm_g, attn_sinks=attn_sinks, norm2_g=norm2_g,
                 conv_b=conv_b, final_g=final_g)
    loss, grad_x, g = _local_step(x, positions, loss_target, small, w_in_full_t, rest_weights, emit, start_token)

    def parts_of(group, after):
        srcs, lands = _exchange_wait(handles[group], after, name="rs_" + group + "_wait")
        return [_with_own(l, lax.dynamic_index_in_dim(s, dev, 0, keepdims=False), dev) for s, l in zip(srcs, lands)]

    p_ffn, p_down = parts_of("ffn", grad_x)
    p_out, p_a, p_b = parts_of("mix", grad_x)
    (p_in,) = parts_of("in", grad_x)
    big = dict(
        w_in=[untr(t) for t in _adamw_sum(p_in, tr(w_in), tr(m_w_in), tr(v_w_in), name="adamw_w_in")],
        w_a=_adamw_sum(p_a, w_a, m_w_a, v_w_a, name="adamw_w_a"),
        w_b=_adamw_sum(p_b, w_b, m_w_b, v_w_b, name="adamw_w_b"),
        w_out=_adamw_sum(p_out, w_out, m_w_out, v_w_out, name="adamw_w_out"),
        w_ffn_in=[untr(t) for t in _adamw_sum(p_ffn, tr(w_ffn_in), tr(m_w_ffn_in), tr(v_w_ffn_in), name="adamw_w_ffn_in")],
        w_down=_adamw_sum(p_down, w_down, m_w_down, v_w_down, name="adamw_w_down"),
    )

    row = lambda t: t.reshape(1, -1) if t.ndim == 1 else t
    shard = lambda t: t.reshape(3, CONVW_BLK)
    sm_g = {nm: g[nm] for nm in SMALL_ROWS}
    sm_w = dict(norm1_g=norm1_g, lb_logits=lb_logits, hgrn_norm_g=hgrn_norm_g, attn_sinks=attn_sinks, norm2_g=norm2_g,
                conv_b=conv_b, final_g=row(final_g), conv_w=shard(conv_w))
    sm_m = dict(norm1_g=m_norm1_g, lb_logits=m_lb_logits, hgrn_norm_g=m_hgrn_norm_g, attn_sinks=m_attn_sinks, norm2_g=m_norm2_g,
                conv_b=m_conv_b, final_g=row(m_final_g), conv_w=shard(m_conv_w))
    sm_v = dict(norm1_g=v_norm1_g, lb_logits=v_lb_logits, hgrn_norm_g=v_hgrn_norm_g, attn_sinks=v_attn_sinks, norm2_g=v_norm2_g,
                conv_b=v_conv_b, final_g=row(v_final_g), conv_w=shard(v_conv_w))
    loss_total, sm_out = _small_step(sm_g, g["conv_w"], loss, sm_w, sm_m, sm_v, dev.astype(jnp.int32).reshape(1), name="small_step")
    shapes = dict(final_g=final_g.shape, conv_w=conv_w.shape)

    names = ("norm1_g", "w_in", "lb_logits", "hgrn_norm_g", "w_a", "attn_sinks", "w_b", "w_out", "norm2_g", "w_ffn_in", "conv_w", "conv_b", "w_down", "final_g")
    outs = [loss_total.reshape(()), grad_x]
    for kind in range(4):
        outs += [big[n][kind] if n in big else sm_out[n][kind].reshape(shapes.get(n, sm_out[n][kind].shape)) for n in names]
    return tuple(outs)
```

```python
import jax
import jax.numpy as jnp
from jax import lax
from jax.experimental import pallas as pl
from jax.experimental.pallas import tpu as pltpu

F32 = jnp.float32
BF16 = jnp.bfloat16

D_MODEL = 1024
HGRN_HEADS = 8
HGRN_DK = 128
CHUNK = 64
ATT_HEADS = 16
ATT_KV_HEADS = 2
ATT_HD = 64
ATT_GROUP = ATT_HEADS // ATT_KV_HEADS
WINDOW = 128
ROPE_DIM = ATT_HD // 4
ROPE_THETA = 500000.0
D_FF = 2816
EPS = 1e-6
NEG_INF = -1e30
N_DEV = 8

ADAM_LR = 0.001
ADAM_B1 = 0.9
ADAM_B2 = 0.999
ADAM_EPS = 1e-08
ADAM_WD = 0.01
ADAM_STEP = 10

MESH = pl.DeviceIdType.MESH
ANY = pl.BlockSpec(memory_space=pl.ANY)


def _pick(n, cands):
    for c in cands:
        if n % c == 0:
            return c
    return n


def _sigmoid(x):
    return 0.5 * jnp.tanh(0.5 * x) + 0.5


def _silu(x):
    hx = 0.5 * x
    return hx * jnp.tanh(hx) + hx


def _rms(x, g):
    return x * lax.rsqrt(jnp.mean(x * x, axis=-1, keepdims=True) + EPS) * g


def _dot(a, b, dims):
    return lax.dot_general(a, b, (dims, ((), ())), preferred_element_type=F32)


def _nn(a, b):
    return _dot(a, b, ((1,), (0,)))


def _nt(a, b):
    return _dot(a, b, ((1,), (1,)))


def _tn(a, b):
    return _dot(a, b, ((0,), (0,)))


def _params(*sem):
    return pltpu.CompilerParams(dimension_semantics=sem, vmem_limit_bytes=56 * 1024 * 1024)


def _matmul(a, b, *, ta=False, tb=False, out_dtype=F32, addend=None, after=None, into=None, o_noff=0, out_t=False,
            o_block_perm=lambda j: j, name, tm, tn, tk=None, n_extent=None, b_koff=0, b_noff=0):
    M, K = (a.shape[1], a.shape[0]) if ta else a.shape
    N = n_extent or (b.shape[0] if tb else b.shape[1])
    tm, tn, tk = min(tm, M), min(tn, N), min(tk or K, K)
    assert M % tm == 0 and N % tn == 0 and K % tk == 0, (name, M, N, K, tm, tn, tk)
    nk = K // tk
    use_scratch = nk > 1 and out_dtype != F32
    grid = (M // tm, N // tn, nk)
    a_spec = pl.BlockSpec((tk, tm), lambda i, j, k: (k, i)) if ta else pl.BlockSpec((tm, tk), lambda i, j, k: (i, k))
    b_spec = pl.BlockSpec((tn, tk), lambda i, j, k: (j + b_noff, k + b_koff)) if tb else pl.BlockSpec((tk, tn), lambda i, j, k: (k + b_koff, j + b_noff))
    o_spec = pl.BlockSpec((tm, tn), lambda i, j, k: (i, j))
    dims = ((0 if ta else 1,), (1 if tb else 0,))
    has_add = addend is not None

    n_in = 2 + has_add + (after is not None) + (into is not None)

    def body(*refs):
        a_ref, b_ref = refs[:2]
        c_ref = refs[2] if has_add else None
        o_ref = refs[n_in]
        part = _dot(a_ref[...], b_ref[...], dims)
        if nk == 1:
            if has_add:
                part = part + c_ref[...].astype(F32)
            o_ref[...] = (part.T if out_t else part).astype(out_dtype)
        else:
            acc_ref = refs[-1] if use_scratch else o_ref
            k = pl.program_id(2)

            @pl.when(k == 0)
            def _():
                acc_ref[...] = part + c_ref[...].astype(F32) if has_add else part

            @pl.when(k > 0)
            def _():
                acc_ref[...] += part

            if use_scratch:
                @pl.when(k == nk - 1)
                def _():
                    o_ref[...] = acc_ref[...].astype(out_dtype)

    in_specs = [a_spec, b_spec] + ([o_spec] if has_add else [])
    args = (a, b) + ((addend,) if has_add else ())
    if after is not None:
        in_specs.append(pl.BlockSpec(after.shape, lambda i, j, k: (0, 0)))
        args += (after,)
    aliases = {}
    if into is not None:
        in_specs.append(ANY)
        args += (into,)
        aliases = {len(args) - 1: 0}
    if out_t:
        assert nk == 1 and not has_add
        o_spec = pl.BlockSpec((tn, tm), lambda i, j, k: (o_block_perm(j) + o_noff, i))
    elif into is not None:
        o_spec = pl.BlockSpec((tm, tn), lambda i, j, k: (i, j + o_noff))
    return pl.pallas_call(
        body,
        name=name,
        grid=grid,
        in_specs=in_specs,
        out_specs=o_spec,
        out_shape=jax.ShapeDtypeStruct(into.shape if into is not None else ((N, M) if out_t else (M, N)), out_dtype),
        input_output_aliases=aliases,
        scratch_shapes=[pltpu.VMEM((tm, tn), F32)] if use_scratch else [],
        compiler_params=_params("parallel", "parallel", "arbitrary"),
    )(*args)


def _matmul_col_tiles(a, b_t, *, tm, tn, tc, name):
    M, K = a.shape
    N = b_t.shape[0]
    tm = min(tm, M)
    per_step = tn // tc

    def body(a_ref, b_ref, o_ref):
        res = _nt(a_ref[...], b_ref[...]).astype(BF16)
        for t in range(per_step):
            o_ref[t] = res[:, t * tc:(t + 1) * tc]

    return pl.pallas_call(
        body, name=name, grid=(M // tm, N // tn),
        in_specs=[pl.BlockSpec((tm, K), lambda i, j: (i, 0)), pl.BlockSpec((tn, K), lambda i, j: (j, 0))],
        out_specs=pl.BlockSpec((per_step, tm, tc), lambda i, j: (j, i, 0)),
        out_shape=jax.ShapeDtypeStruct((N // tc, M, tc), BF16),
        compiler_params=_params("parallel", "parallel"),
    )(a, b_t)


def _matmul_ep(pairs, *, tm, ins, in_specs, out_shapes, out_specs, sums=(), epilogue, aliases=None, name):
    M = pairs[0][0].shape[0]
    tm = min(tm, M)
    mm_specs, mm_args, dims = [], [], []
    for a, b, tb, koff in pairs:
        K = a.shape[1]
        N = b.shape[0] if tb else b.shape[1]
        mm_specs += [pl.BlockSpec((tm, K), lambda i: (i, 0)),
                     pl.BlockSpec((N, K), lambda i, koff=koff: (0, koff)) if tb else pl.BlockSpec((K, N), lambda i, koff=koff: (koff, 0))]
        mm_args += [a, b]
        dims.append(((1,), (1 if tb else 0,)))
    n_mm = len(mm_args)
    n_in = n_mm + len(ins)

    def body(*refs):
        in_refs, out_refs = refs[n_mm:n_in], refs[n_in:]
        accs = [_dot(refs[2 * p][...], refs[2 * p + 1][...], dims[p]) for p in range(len(pairs))]
        outs = epilogue(*accs, *in_refs)
        for k, (ref, val) in enumerate(zip(out_refs, outs)):
            if val is None:
                continue
            if k in sums:
                @pl.when(pl.program_id(0) == 0)
                def _():
                    ref[...] = jnp.zeros_like(ref)

                ref[...] += val
            else:
                ref[...] = val.astype(ref.dtype)

    return pl.pallas_call(
        body, name=name, grid=(M // tm,),
        in_specs=mm_specs + list(in_specs),
        out_specs=list(out_specs), out_shape=list(out_shapes),
        input_output_aliases={n_mm + k: v for k, v in (aliases or {}).items()},
        compiler_params=_params("arbitrary"),
    )(*mm_args, *ins)


def _row_spec(tm, n):
    return pl.BlockSpec((tm, n), lambda i: (i, 0))


def _full_spec(shape):
    return pl.BlockSpec(shape, lambda i: tuple(0 for _ in shape))


def _norm_matmul(x, g, w_t, *, tm, tn, name):
    T, D = x.shape
    N = w_t.shape[0]
    tm = min(tm, T)

    def body(x_ref, g_ref, w_ref, u_ref, z_ref, u_scr):
        @pl.when(pl.program_id(1) == 0)
        def _():
            u = _rms(x_ref[...], g_ref[...]).astype(BF16)
            u_scr[...] = u
            u_ref[...] = u

        z_ref[...] = _nt(u_scr[...], w_ref[...]).astype(BF16)

    return pl.pallas_call(
        body, name=name, grid=(T // tm, N // tn),
        in_specs=[pl.BlockSpec((tm, D), lambda i, j: (i, 0)), pl.BlockSpec((1, D), lambda i, j: (0, 0)),
                  pl.BlockSpec((tn, D), lambda i, j: (j, 0))],
        out_specs=[pl.BlockSpec((tm, D), lambda i, j: (i, 0)), pl.BlockSpec((tm, tn), lambda i, j: (i, j))],
        out_shape=[jax.ShapeDtypeStruct((T, D), BF16), jax.ShapeDtypeStruct((T, N), BF16)],
        scratch_shapes=[pltpu.VMEM((tm, D), BF16)],
        compiler_params=_params("parallel", "arbitrary"),
    )(x, g, w_t)


def _norm_bwd_add(x, g, du, dres, *, with_bf16=True, name):
    T, D = x.shape
    tm = _pick(T, (512, 256, 128))

    def body(x_ref, g_ref, du_ref, dr_ref, dx_ref, *rest):
        dg_ref = rest[-1]
        _, vjp = jax.vjp(_rms, x_ref[...], g_ref[...])
        dx, dg = vjp(du_ref[...].astype(F32))
        dx = dx + dr_ref[...]
        dx_ref[...] = dx
        if with_bf16:
            rest[0][...] = dx.astype(BF16)

        @pl.when(pl.program_id(0) == 0)
        def _():
            dg_ref[...] = jnp.zeros_like(dg_ref)

        dg_ref[...] += dg

    row = _row_spec(tm, D)
    return pl.pallas_call(
        body, name=name, grid=(T // tm,),
        in_specs=[row, _full_spec((1, D)), row, row],
        out_specs=[row] + ([row] if with_bf16 else []) + [_full_spec((1, D))],
        out_shape=[jax.ShapeDtypeStruct((T, D), F32)] + ([jax.ShapeDtypeStruct((T, D), BF16)] if with_bf16 else []) + [jax.ShapeDtypeStruct((1, D), F32)],
        compiler_params=_params("arbitrary"),
    )(x, g, du, dres)


def _merge_fn(gates, a, b):
    ga = gates[:, :D_MODEL].astype(F32)
    gb = gates[:, D_MODEL:].astype(F32)
    return _sigmoid(ga) * a.astype(F32) + _sigmoid(gb) * b.astype(F32)


def _gates_spec(tm):
    return pl.BlockSpec((tm, W_GATES), lambda i: (i, O_GATES // W_GATES))


CONV_TC = 256


def _shift_down(x, n, rows):
    return jnp.where(rows >= n, pltpu.roll(x, n, 0), 0.0)


def _shift_up(x, n, rows, S):
    return jnp.where(rows < S - n, pltpu.roll(x, S - n, 0), 0.0)


def _conv_act_fwd(gu, conv_w, conv_b, *, name):
    _, B, S, tc = gu.shape
    nc = D_FF // tc

    def body(g_ref, up_ref, w_ref, b_ref, o_ref, a_ref):
        g = g_ref[...].astype(F32)
        rows = lax.broadcasted_iota(jnp.int32, g.shape, 0)
        w = w_ref[...]
        a = w[2:3] * g + w[1:2] * _shift_down(g, 1, rows) + w[0:1] * _shift_down(g, 2, rows) + b_ref[...]
        o_ref[...] = (_silu(a) * up_ref[...].astype(F32)).astype(BF16)
        a_ref[...] = a.astype(BF16)

    col = pl.BlockSpec((None, S, tc), lambda b, j: (b, 0, j))
    tile = lambda off: pl.BlockSpec((None, None, S, tc), lambda b, j: (j + off, b, 0, 0))
    return pl.pallas_call(
        body, name=name, grid=(B, nc),
        in_specs=[tile(0), tile(nc),
                  pl.BlockSpec((3, tc), lambda b, j: (0, j)),
                  pl.BlockSpec((1, tc), lambda b, j: (0, j))],
        out_specs=[col, tile(0)],
        out_shape=[jax.ShapeDtypeStruct((B, S, D_FF), BF16), jax.ShapeDtypeStruct((nc, B, S, tc), BF16)],
        compiler_params=_params("parallel", "parallel"),
    )(gu, gu, conv_w, conv_b)


def _conv_act_bwd(gu, a_pre, conv_w, dact, *, name):
    _, B, S, tc = gu.shape
    nc = D_FF // tc

    def body(g_ref, up_ref, a_ref, w_ref, da_ref, dg_ref, dup_ref, dw_ref, db_ref):
        g = g_ref[...].astype(F32)
        up, a, dact = up_ref[...], a_ref[...], da_ref[...]
        rows = lax.broadcasted_iota(jnp.int32, g.shape, 0)
        w = w_ref[...]
        sg = _sigmoid(a)
        dup_ref[...] = dact * a * sg
        da = (dact * up * sg * (1.0 + a * (1.0 - sg))).astype(F32)
        da1 = _shift_up(da, 1, rows, S)
        da2 = _shift_up(da, 2, rows, S)
        dg_ref[...] = (w[2:3] * da + w[1:2] * da1 + w[0:1] * da2).astype(BF16)

        @pl.when(pl.program_id(1) == 0)
        def _():
            dw_ref[...] = jnp.zeros_like(dw_ref)
            db_ref[...] = jnp.zeros_like(db_ref)

        dw_ref[0:1, :] += jnp.sum(da2 * g, axis=0, keepdims=True)
        dw_ref[1:2, :] += jnp.sum(da1 * g, axis=0, keepdims=True)
        dw_ref[2:3, :] += jnp.sum(da * g, axis=0, keepdims=True)
        db_ref[...] += jnp.sum(da, axis=0, keepdims=True)

    col = pl.BlockSpec((None, S, tc), lambda j, b: (b, 0, j))
    tile = lambda off: pl.BlockSpec((None, None, S, tc), lambda j, b: (j + off, b, 0, 0))
    return pl.pallas_call(
        body, name=name, grid=(nc, B),
        in_specs=[tile(0), tile(nc), tile(0),
                  pl.BlockSpec((3, tc), lambda j, b: (0, j)),
                  col],
        out_specs=[col, col, pl.BlockSpec((3, tc), lambda j, b: (0, j)), pl.BlockSpec((1, tc), lambda j, b: (0, j))],
        out_shape=[jax.ShapeDtypeStruct((B, S, D_FF), BF16), jax.ShapeDtypeStruct((B, S, D_FF), BF16),
                   jax.ShapeDtypeStruct((3, D_FF), F32), jax.ShapeDtypeStruct((1, D_FF), F32)],
        compiler_params=_params("parallel", "arbitrary"),
    )(gu, gu, a_pre, conv_w, dact)


HGRN_CPB = 8
HF = HGRN_HEADS * HGRN_DK


def _tri(n, upper=False):
    r = lax.broadcasted_iota(jnp.int32, (n, n), 0)
    c = lax.broadcasted_iota(jnp.int32, (n, n), 1)
    return (c >= r) if upper else (r >= c)


def _hs(h):
    return slice(h * HGRN_DK, (h + 1) * HGRN_DK)


def _cumsum_rows(tri_b, x):
    hi = x.astype(BF16)
    lo = (x - hi.astype(F32)).astype(BF16)
    return _nn(tri_b, hi) + _nn(tri_b, lo)


def _hgrn_col(seg, h):
    return slice(seg * HF + h * HGRN_DK, seg * HF + (h + 1) * HGRN_DK)


def _hgrn_gates(q, fz, lb):
    sg = _sigmoid(fz)
    return _sigmoid(q), sg, lb + (1.0 - lb) * sg


def _hgrn_decays(b, q, sq, f):
    qf = q * sq
    k = 1.0 - f
    bref = b[CHUNK // 2:CHUNK // 2 + 1, :]
    blast = b[CHUNK - 1:CHUNK, :]
    e1 = jnp.exp2(b - bref)
    e2 = jnp.exp2(bref - b)
    e3 = e1 * jnp.exp2(bref)
    e4 = e2 * jnp.exp2(blast - bref)
    return (e1, e2, e3, e4), qf * e1, k * e2, qf * e3, k * e4, jnp.exp2(blast)


def _hgrn_fwd(zh, lb, gn, *, name):
    B, S, _ = zh.shape
    cpb = HGRN_CPB
    ts = cpb * CHUNK
    nblk = S // ts

    def body(z_ref, lb_ref, gn_ref, o_ref, st_ref, state):
        @pl.when(pl.program_id(1) == 0)
        def _():
            state[...] = jnp.zeros_like(state)

        R = range(HGRN_HEADS)
        causal = _tri(CHUNK)
        tril_b = causal.astype(BF16)
        lbh = [lb_ref[:, _hs(h)] for h in R]
        for c in range(cpb):
            rows = slice(c * CHUNK, (c + 1) * CHUNK)
            q = [z_ref[rows, _hgrn_col(0, h)].astype(F32) for h in R]
            gates = [_hgrn_gates(q[h], z_ref[rows, _hgrn_col(1, h)].astype(F32), lbh[h]) for h in R]
            b = [_cumsum_rows(tril_b, jnp.log2(gates[h][2])) for h in R]
            v = [z_ref[rows, _hgrn_col(2, h)] for h in R]
            dec, q_in, k_in, q_out, k_st = [], [], [], [], []
            for h in R:
                _, qi, ki, qo, ks, d = _hgrn_decays(b[h], q[h], gates[h][0], gates[h][2])
                dec.append(d)
                for lst, t in zip((q_in, k_in, q_out, k_st), (qi, ki, qo, ks)):
                    lst.append(t.astype(BF16))
            a = [jnp.where(causal, _nt(q_in[h], k_in[h]), 0.0).astype(BF16) for h in R]
            st = [state[h] for h in R]
            for h in R:
                st_ref[c, h] = st[h]
            o = [_nn(a[h], v[h]) + _nt(q_out[h], st[h].astype(BF16)) for h in R]
            for h in R:
                state[h] = st[h] * dec[h] + _tn(v[h], k_st[h])
            for h in R:
                o_ref[rows, _hs(h)] = (_rms(o[h], gn_ref[...]) * _silu(z_ref[rows, _hgrn_col(3, h)].astype(F32))).astype(BF16)

    return pl.pallas_call(
        body, name=name, grid=(B, nblk),
        in_specs=[pl.BlockSpec((None, ts, 4 * HF), lambda b, s: (b, s, 0)),
                  pl.BlockSpec((1, HF), lambda b, s: (0, 0)),
                  pl.BlockSpec((1, HGRN_DK), lambda b, s: (0, 0))],
        out_specs=[pl.BlockSpec((None, ts, HF), lambda b, s: (b, s, 0)),
                   pl.BlockSpec((None, cpb, HGRN_HEADS, HGRN_DK, HGRN_DK), lambda b, s: (b, s, 0, 0, 0))],
        out_shape=[jax.ShapeDtypeStruct((B, S, HF), BF16),
                   jax.ShapeDtypeStruct((B, S // CHUNK, HGRN_HEADS, HGRN_DK, HGRN_DK), F32)],
        scratch_shapes=[pltpu.VMEM((HGRN_HEADS, HGRN_DK, HGRN_DK), F32)],
        compiler_params=_params("arbitrary", "arbitrary"),
    )(zh, lb, gn)


def _hgrn_bwd(zh, lb, gn, states, doa, dz, *, name):
    B, S, _ = zh.shape
    cpb = HGRN_CPB
    ts = cpb * CHUNK
    nblk = S // ts
    rev = lambda b, s: (b, nblk - 1 - s, 0)

    def body(z_ref, lb_ref, gn_ref, st_ref, do_ref, dz_in, dz_ref, dlb_ref, dgn_ref, dstate):
        @pl.when(pl.program_id(1) == 0)
        def _():
            dstate[...] = jnp.zeros_like(dstate)

        @pl.when((pl.program_id(0) == 0) & (pl.program_id(1) == 0))
        def _():
            dlb_ref[...] = jnp.zeros_like(dlb_ref)
            dgn_ref[...] = jnp.zeros_like(dgn_ref)

        R = range(HGRN_HEADS)
        causal = _tri(CHUNK)
        tril_b = causal.astype(BF16)
        triu_b = _tri(CHUNK, upper=True).astype(BF16)
        rowid = lax.broadcasted_iota(jnp.int32, (CHUNK, HGRN_DK), 0)
        lbh = [lb_ref[:, _hs(h)] for h in R]
        gn = gn_ref[...]
        for c in reversed(range(cpb)):
            rows = slice(c * CHUNK, (c + 1) * CHUNK)
            q = [z_ref[rows, _hgrn_col(0, h)].astype(F32) for h in R]
            gates = [_hgrn_gates(q[h], z_ref[rows, _hgrn_col(1, h)].astype(F32), lbh[h]) for h in R]
            b = [_cumsum_rows(tril_b, jnp.log2(gates[h][2])) for h in R]
            v = [z_ref[rows, _hgrn_col(2, h)] for h in R]
            pre = [_hgrn_decays(b[h], q[h], gates[h][0], gates[h][2]) for h in R]
            q_in_b, k_in_b, q_out_b, k_st_b = ([pre[h][i].astype(BF16) for h in R] for i in (1, 2, 3, 4))
            a_b = [jnp.where(causal, _nt(q_in_b[h], k_in_b[h]), 0.0).astype(BF16) for h in R]
            st = [st_ref[c, h] for h in R]
            st_b = [t.astype(BF16) for t in st]
            o = [_nn(a_b[h], v[h]) + _nt(q_out_b[h], st_b[h]) for h in R]
            do_l, dgn_acc = [], jnp.zeros_like(gn)
            for h in R:
                hg = z_ref[rows, _hgrn_col(3, h)].astype(F32)
                dout = do_ref[rows, _hs(h)].astype(F32)
                shg = _sigmoid(hg)
                on_h, norm_vjp = jax.vjp(_rms, o[h], gn)
                d_o, d_gn = norm_vjp(dout * (hg * shg))
                do_l.append(d_o)
                dgn_acc = dgn_acc + d_gn
                dz_ref[rows, _hgrn_col(3, h)] = (dout * on_h * shg * (1.0 + hg * (1.0 - shg))).astype(BF16)
            dgn_ref[...] += dgn_acc
            do_b = [t.astype(BF16) for t in do_l]
            dst = [dstate[h] for h in R]
            dst_b = [t.astype(BF16) for t in dst]
            da_b = [jnp.where(causal, _nt(do_b[h], v[h]), 0.0).astype(BF16) for h in R]
            dv = [_tn(a_b[h], do_b[h]) + _nt(k_st_b[h], dst_b[h]) for h in R]
            dq_in = [_nn(da_b[h], k_in_b[h]) for h in R]
            dk_in = [_tn(da_b[h], q_in_b[h]) for h in R]
            dq_out = [_nn(do_b[h], st_b[h]) for h in R]
            dk_st = [_nn(v[h], dst_b[h]) for h in R]
            for h in R:
                dz_ref[rows, _hgrn_col(2, h)] = dv[h].astype(BF16)
            db = []
            for h in R:
                _, q_in, k_in, q_out, k_st, dec = pre[h]
                ddec = jnp.sum(st[h] * dst[h], axis=0, keepdims=True)
                t_qin, t_kin, t_kst = dq_in[h] * q_in, dk_in[h] * k_in, dk_st[h] * k_st
                dbref = jnp.sum(t_kin - t_qin, axis=0, keepdims=True)
                dblast = jnp.sum(t_kst, axis=0, keepdims=True) + ddec * dec
                db.append(t_qin - t_kin + dq_out[h] * q_out - t_kst
                          + jnp.where(rowid == CHUNK // 2, dbref, 0.0) + jnp.where(rowid == CHUNK - 1, dblast, 0.0))
            for h in R:
                dstate[h] = dst[h] * pre[h][5] + _tn(do_b[h], q_out_b[h])
            dlogf = [_cumsum_rows(triu_b, db[h]) for h in R]
            for h in R:
                sq, sg, f = gates[h]
                e1, e2, e3, e4 = pre[h][0]
                dqf = dq_in[h] * e1 + dq_out[h] * e3
                dk = dk_in[h] * e2 + dk_st[h] * e4
                df_open = (dlogf[h] / f - dk) * (1.0 - sg)
                dlb_ref[:, _hs(h)] += jnp.sum(df_open, axis=0, keepdims=True)
                dz_ref[rows, _hgrn_col(1, h)] = (df_open * ((1.0 - lbh[h]) * sg)).astype(BF16)
                dz_ref[rows, _hgrn_col(0, h)] = (dqf * sq * (1.0 + q[h] * (1.0 - sq))).astype(BF16)

    return pl.pallas_call(
        body, name=name, grid=(B, nblk),
        in_specs=[pl.BlockSpec((None, ts, 4 * HF), rev),
                  pl.BlockSpec((1, HF), lambda b, s: (0, 0)),
                  pl.BlockSpec((1, HGRN_DK), lambda b, s: (0, 0)),
                  pl.BlockSpec((None, cpb, HGRN_HEADS, HGRN_DK, HGRN_DK), lambda b, s: (b, nblk - 1 - s, 0, 0, 0)),
                  pl.BlockSpec((None, ts, HF), rev),
                  ANY],
        out_specs=[pl.BlockSpec((None, ts, 4 * HF), rev),
                   pl.BlockSpec((1, HF), lambda b, s: (0, 0)),
                   pl.BlockSpec((1, HGRN_DK), lambda b, s: (0, 0))],
        out_shape=[jax.ShapeDtypeStruct(dz.shape, BF16),
                   jax.ShapeDtypeStruct((1, HF), F32),
                   jax.ShapeDtypeStruct((1, HGRN_DK), F32)],
        input_output_aliases={5: 0},
        scratch_shapes=[pltpu.VMEM((HGRN_HEADS, HGRN_DK, HGRN_DK), F32)],
        compiler_params=_params("arbitrary", "arbitrary"),
    )(zh, lb, gn, states, doa, dz)


KV_W = ATT_KV_HEADS * ATT_HD
ATT_SCALE = ATT_HD ** -0.5


def _rope(x, cos, sin, inverse=False):
    half = ROPE_DIM // 2
    outs = []
    for p in range(x.shape[1] // 128):
        xp = x[:, p * 128:(p + 1) * 128]
        lane = lax.broadcasted_iota(jnp.int32, xp.shape, 1) % ATT_HD
        sw = jnp.where(lane < half, pltpu.roll(xp, 128 - half, 1), pltpu.roll(xp, half, 1))
        outs.append(xp * cos - sw * sin if inverse else xp * cos + sw * sin)
    return outs[0] if len(outs) == 1 else jnp.concatenate(outs, axis=1)


PAIRS_PER_KV = ATT_GROUP // 2


def _swap_halves(x):
    return pltpu.roll(x, ATT_HD, 1)


def _kv_padded(t, low):
    sw = _swap_halves(t)
    zero = jnp.zeros_like(t)
    out = []
    for g in range(ATT_KV_HEADS):
        in_low, in_high = (t, sw) if g == 0 else (sw, t)
        out.append((jnp.where(low, in_low, zero).astype(BF16), jnp.where(low, zero, in_high).astype(BF16)))
    return out


def _swa_mask(first_block):
    qi = lax.broadcasted_iota(jnp.int32, (WINDOW, 2 * WINDOW), 0)
    mi = lax.broadcasted_iota(jnp.int32, (WINDOW, 2 * WINDOW), 1)
    band = (mi > qi) & (mi <= qi + WINDOW)
    return band & (jnp.logical_not(first_block) | (mi >= WINDOW))


def _swa_specs(nb):
    cur = lambda b, i: (b, i, 0)
    prev = lambda b, i: (b, jnp.maximum(i - 1, 0), 0)
    return cur, prev


def _swa_z_specs():
    q = pl.BlockSpec((None, WINDOW, W_AQ), lambda b, i: (b, i, O_AQ // W_AQ))
    kv_prev = pl.BlockSpec((None, WINDOW, W_AKV), lambda b, i: (b, jnp.maximum(i - 1, 0), O_AKV // W_AKV))
    kv_cur = pl.BlockSpec((None, WINDOW, W_AKV), lambda b, i: (b, i, O_AKV // W_AKV))
    return q, kv_prev, kv_cur


def _swa_fwd(z, cos, sin, sinks, *, name):
    B, S, _ = z.shape
    nb = S // WINDOW
    cur, prev = _swa_specs(nb)

    def body(q_ref, kvp_ref, kvc_ref, cp_ref, sp_ref, cc_ref, sc_ref, sink_ref, o_ref, lse_ref, qr_ref, kr_ref):
        cos_c, sin_c = cc_ref[...], sc_ref[...]
        q = (_rope(q_ref[...].astype(F32), cos_c, sin_c) * ATT_SCALE).astype(BF16)
        k = jnp.concatenate([_rope(kvp_ref[:, :KV_W].astype(F32), cp_ref[...], sp_ref[...]),
                             _rope(kvc_ref[:, :KV_W].astype(F32), cos_c, sin_c)], axis=0)
        qr_ref[...] = q
        kr_ref[...] = k[WINDOW:].astype(BF16)
        v = jnp.concatenate([kvp_ref[:, KV_W:], kvc_ref[:, KV_W:]], axis=0).astype(F32)
        low = lax.broadcasted_iota(jnp.int32, k.shape, 1) < ATT_HD
        kpad = _kv_padded(k, low)
        vpad = _kv_padded(v, low)
        mask = _swa_mask(pl.program_id(1) == 0)
        lses = []
        for g in range(ATT_KV_HEADS):
            pairs = range(g * PAIRS_PER_KV, (g + 1) * PAIRS_PER_KV)
            keys = [(p, e) for p in pairs for e in (0, 1)]
            qp = {p: q[:, p * 128:(p + 1) * 128] for p in pairs}
            s = {pe: jnp.where(mask, _nt(qp[pe[0]], kpad[g][pe[1]]), NEG_INF) for pe in keys}
            pr = {}
            for pe in keys:
                sink = sink_ref[0, 2 * pe[0] + pe[1]]
                m = jnp.maximum(jnp.max(s[pe], axis=1, keepdims=True), sink)
                ex = jnp.exp(s[pe] - m)
                den = jnp.sum(ex, axis=1, keepdims=True) + jnp.exp(sink - m)
                pr[pe] = (ex * (1.0 / den)).astype(BF16)
                lses.append(m + jnp.log(den))
            for p in pairs:
                o_ref[:, p * 128:(p + 1) * 128] = (_nn(pr[p, 0], vpad[g][0]) + _nn(pr[p, 1], vpad[g][1])).astype(BF16)
        lse_ref[...] = jnp.concatenate(lses, axis=1)

    tab = lambda im: pl.BlockSpec((None, WINDOW, 128), im)
    return pl.pallas_call(
        body, name=name, grid=(B, nb),
        in_specs=[*_swa_z_specs(),
                  tab(prev), tab(prev), tab(cur), tab(cur),
                  pl.BlockSpec(memory_space=pltpu.SMEM)],
        out_specs=[pl.BlockSpec((None, WINDOW, D_MODEL), cur), pl.BlockSpec((None, WINDOW, ATT_HEADS), cur),
                   pl.BlockSpec((None, WINDOW, D_MODEL), cur), pl.BlockSpec((None, WINDOW, KV_W), cur)],
        out_shape=[jax.ShapeDtypeStruct((B, S, D_MODEL), BF16), jax.ShapeDtypeStruct((B, S, ATT_HEADS), F32),
                   jax.ShapeDtypeStruct((B, S, D_MODEL), BF16), jax.ShapeDtypeStruct((B, S, KV_W), BF16)],
        compiler_params=_params("parallel", "parallel"),
    )(z, z, z, cos, sin, cos, sin, sinks)


def _swa_bwd(z, qr, kr, cos, sin, sinks, lse, dob, dz, *, name):
    B, S, _ = z.shape
    nb = S // WINDOW
    cur, prev = _swa_specs(nb)

    def body(q_ref, krp_ref, krc_ref, kvp_ref, kvc_ref, cp_ref, sp_ref, cc_ref, sc_ref, sink_ref, lse_ref, do_ref, dz_in,
             dq_ref, dkc_ref, dkp_ref, dsink_ref):
        @pl.when((pl.program_id(0) == 0) & (pl.program_id(1) == 0))
        def _():
            dsink_ref[...] = jnp.zeros_like(dsink_ref)

        cos_c, sin_c, cos_p, sin_p = cc_ref[...], sc_ref[...], cp_ref[...], sp_ref[...]
        q = q_ref[...]
        k = jnp.concatenate([krp_ref[...], krc_ref[...]], axis=0).astype(F32)
        v = jnp.concatenate([kvp_ref[:, KV_W:], kvc_ref[:, KV_W:]], axis=0).astype(F32)
        low = lax.broadcasted_iota(jnp.int32, k.shape, 1) < ATT_HD
        kpad = _kv_padded(k, low)
        vpad = _kv_padded(v, low)
        mask = _swa_mask(pl.program_id(1) == 0)
        lse = lse_ref[...]
        dq_parts, dk_sum, dv_sum, dsinks = [], [], [], []
        for g in range(ATT_KV_HEADS):
            pairs = range(g * PAIRS_PER_KV, (g + 1) * PAIRS_PER_KV)
            keys = [(p, e) for p in pairs for e in (0, 1)]
            qp = {p: q[:, p * 128:(p + 1) * 128] for p in pairs}
            dop = {p: do_ref[:, p * 128:(p + 1) * 128] for p in pairs}
            s = {pe: jnp.where(mask, _nt(qp[pe[0]], kpad[g][pe[1]]), NEG_INF) for pe in keys}
            dp = {pe: _nt(dop[pe[0]], vpad[g][pe[1]]) for pe in keys}
            pr, ds = {}, {}
            for pe in keys:
                h = 2 * pe[0] + pe[1]
                lse_h = lse[:, h:h + 1]
                pf = jnp.exp(s[pe] - lse_h)
                delta = jnp.sum(pf * dp[pe], axis=1, keepdims=True)
                ds[pe] = (pf * (dp[pe] - delta)).astype(BF16)
                pr[pe] = pf.astype(BF16)
                p_sink = jnp.exp(sink_ref[0, h] - lse_h)
                dsinks.append(-jnp.sum(p_sink * delta, axis=0, keepdims=True))
            for p in pairs:
                dq_parts.append((_nn(ds[p, 0], kpad[g][0]) + _nn(ds[p, 1], kpad[g][1])) * ATT_SCALE)
            x = [sum(_tn(ds[p, e], qp[p]) for p in pairs) for e in (0, 1)]
            y = [sum(_tn(pr[p, e], dop[p]) for p in pairs) for e in (0, 1)]
            zk = jnp.where(low, x[0], x[1])
            zv = jnp.where(low, y[0], y[1])
            dk_sum.append(zk + _swap_halves(zk))
            dv_sum.append(zv + _swap_halves(zv))
        dq_ref[...] = _rope(jnp.concatenate(dq_parts, axis=1), cos_c, sin_c, inverse=True).astype(BF16)
        dk = jnp.where(low, dk_sum[0], dk_sum[1])
        dv = jnp.where(low, dv_sum[0], dv_sum[1])
        dkp_ref[:, :KV_W] = _rope(dk[:WINDOW], cos_p, sin_p, inverse=True)
        dkp_ref[:, KV_W:] = dv[:WINDOW]
        dkc_ref[:, :KV_W] = _rope(dk[WINDOW:], cos_c, sin_c, inverse=True)
        dkc_ref[:, KV_W:] = dv[WINDOW:]
        dsink_ref[...] += jnp.concatenate(dsinks, axis=1)

    tab = lambda im: pl.BlockSpec((None, WINDOW, 128), im)
    return pl.pallas_call(
        body, name=name, grid=(B, nb),
        in_specs=[pl.BlockSpec((None, WINDOW, D_MODEL), cur), tab(prev), tab(cur),
                  *_swa_z_specs()[1:],
                  tab(prev), tab(prev), tab(cur), tab(cur),
                  pl.BlockSpec(memory_space=pltpu.SMEM),
                  pl.BlockSpec((None, WINDOW, ATT_HEADS), cur),
                  pl.BlockSpec((None, WINDOW, D_MODEL), cur),
                  ANY],
        out_specs=[_swa_z_specs()[0],
                   pl.BlockSpec((None, WINDOW, 2 * KV_W), cur), pl.BlockSpec((None, WINDOW, 2 * KV_W), cur),
                   pl.BlockSpec((1, ATT_HEADS), lambda b, i: (0, 0))],
        out_shape=[jax.ShapeDtypeStruct(dz.shape, BF16),
                   jax.ShapeDtypeStruct((B, S, 2 * KV_W), F32), jax.ShapeDtypeStruct((B, S, 2 * KV_W), F32),
                   jax.ShapeDtypeStruct((1, ATT_HEADS), F32)],
        input_output_aliases={12: 0},
        compiler_params=_params("arbitrary", "arbitrary"),
    )(qr, kr, kr, z, z, cos, sin, cos, sin, sinks, lse, dob, dz)


def _swa_dkv_combine(dkv_cur, dkv_prev, dz, *, name):
    B, S, W = dkv_cur.shape

    def body(c_ref, p_ref, dz_in, o_ref):
        rows = lax.broadcasted_iota(jnp.int32, (S, W), 0)
        o_ref[...] = (c_ref[...] + _shift_up(p_ref[...], WINDOW, rows, S)).astype(BF16)

    seq = pl.BlockSpec((None, S, W), lambda b: (b, 0, 0))
    return pl.pallas_call(
        body, name=name, grid=(B,),
        in_specs=[seq, seq, ANY], out_specs=pl.BlockSpec((None, S, W), lambda b: (b, 0, O_AKV // W_AKV)),
        out_shape=jax.ShapeDtypeStruct(dz.shape, BF16),
        input_output_aliases={2: 0},
        compiler_params=_params("parallel"),
    )(dkv_cur, dkv_prev, dz)


def _rope_tables(positions):
    half = ROPE_DIM // 2
    inv = ROPE_THETA ** (-2.0 * jnp.arange(half, dtype=F32) / ROPE_DIM)
    ang = positions.astype(F32)[..., None] * inv
    c, s = jnp.cos(ang), jnp.sin(ang)
    pad = jnp.zeros(ang.shape[:-1] + (ATT_HD - ROPE_DIM,), F32)
    cos = jnp.concatenate([c, c, pad + 1.0], axis=-1)
    sin = jnp.concatenate([-s, s, pad], axis=-1)
    return jnp.tile(cos, (1, 1, 2)), jnp.tile(sin, (1, 1, 2))


def _lower_bound(lb_logits, *, name):
    def body(l_ref, o_ref):
        l = l_ref[...]
        e = jnp.exp(l - jnp.max(l, axis=0, keepdims=True))
        o_ref[...] = e[0:1] / jnp.sum(e, axis=0, keepdims=True)

    return pl.pallas_call(body, name=name, out_shape=jax.ShapeDtypeStruct((1, lb_logits.shape[1]), F32))(lb_logits)


W_ZH, W_GATES, W_AQ, W_AKV = 4 * HF, 2 * D_MODEL, ATT_HEADS * ATT_HD, 2 * KV_W
O_ZH, O_GATES, O_AQ, O_AKV = 0, W_ZH, W_ZH + W_GATES, W_ZH + W_GATES + W_AQ
W_IN = W_ZH + W_GATES + W_AQ + W_AKV


W_IN_BLK = W_IN // N_DEV


def _reference_row_block(j, rows=256):
    nz, ng = W_ZH // rows, W_GATES // rows
    return jnp.where(j < nz, j, jnp.where(j < nz + ng, j + (W_AQ + W_AKV) // rows, j - ng))


def _reordered_rows(w_t, *, name):
    rows = 256

    def body(i_ref, o_ref):
        o_ref[...] = i_ref[...]

    return pl.pallas_call(
        body, name=name, grid=(W_IN // rows,),
        in_specs=[pl.BlockSpec((rows, D_MODEL), lambda j: (_reference_row_block(j, rows), 0))],
        out_specs=pl.BlockSpec((rows, D_MODEL), lambda j: (j, 0)),
        out_shape=jax.ShapeDtypeStruct(w_t.shape, w_t.dtype), compiler_params=_params("parallel"))(w_t)


def _local_step(x, positions, target, small, w_in_t, rest_weights, emit, start_token):
    B, S, D = x.shape
    T = B * S
    x2 = x.reshape(T, D)
    cos, sin = _rope_tables(positions)
    lb = _lower_bound(small["lb_logits"], name="lb_fwd")
    zero = lambda tok: tok[0:1, 0:1]

    u1, z = _norm_matmul(x2, small["norm1_g"] + zero(start_token), w_in_t, tm=1024, tn=W_IN // 2, name="norm1_mm_z")
    z3 = z.reshape(B, S, W_IN)
    oa, states = _hgrn_fwd(z3, lb, small["hgrn_norm_g"], name="hgrn_fwd")
    ob, lse, qr, kr = _swa_fwd(z3, cos, sin, small["attn_sinks"], name="swa_fwd")
    oa2 = oa.reshape(T, D)
    ob2 = ob.reshape(T, D)
    W = rest_weights("mix", ob)
    row = lambda tm, dtype=None: _row_spec(tm, D)
    tile = lambda dtype: jax.ShapeDtypeStruct((T, D), dtype)
    vec = _full_spec((1, D))
    vec_shape = jax.ShapeDtypeStruct((1, D), F32)

    def merge_ep(acc_a, acc_b, g_ref):
        pa, pb = acc_a.astype(BF16), acc_b.astype(BF16)
        return pa, pb, _merge_fn(g_ref[...], pa, pb)

    pa, pb, merged = _matmul_ep([(oa2, W["w_a"], False, 0), (ob2, W["w_b"], False, 0)], tm=1024, ins=[z], in_specs=[_gates_spec(1024)],
                                out_shapes=[tile(BF16)] * 3, out_specs=[row(1024)] * 3, epilogue=merge_ep, name="mm_pa_pb_merge")

    def resid_norm_ep(acc, x_ref, g_ref):
        hh = acc + x_ref[...]
        return hh, _rms(hh, g_ref[...])

    h, u2 = _matmul_ep([(merged, W["w_out"], False, 0)], tm=1024, ins=[x2, small["norm2_g"]], in_specs=[row(1024), vec],
                       out_shapes=[tile(F32), tile(BF16)], out_specs=[row(1024), row(1024)], epilogue=resid_norm_ep, name="mm_h_norm2")
    W.update(rest_weights("ffn", u2))
    gu3 = _matmul_col_tiles(u2, W["w_ffn_t"], tm=1024, tn=D_FF, tc=CONV_TC, name="mm_gu").reshape(2 * D_FF // CONV_TC, B, S, CONV_TC)
    act, a_pre = _conv_act_fwd(gu3, W["conv_w"], small["conv_b"], name="conv_act_fwd")
    act2 = act.reshape(T, D_FF)
    g = {}

    def loss_ep(acc, h_ref, g_ref, t_ref):
        y, vjp = jax.vjp(_rms, acc + h_ref[...], g_ref[...])
        err = y - t_ref[...]
        dx, dg = vjp(err * (1.0 / D))
        return dx, dx, dg, (0.5 / D) * jnp.sum(jnp.sum(err * err, axis=1, keepdims=True), axis=0, keepdims=True)

    dh2, dh2b, g["final_g"], loss = _matmul_ep(
        [(act2, W["w_down"], False, 0)], tm=512, ins=[h, small["final_g"].reshape(1, D), target.reshape(T, D)], in_specs=[row(512), vec, row(512)],
        out_shapes=[tile(F32), tile(BF16), vec_shape, jax.ShapeDtypeStruct((1, 1), F32)],
        out_specs=[row(512), row(512), vec, _full_spec((1, 1))], sums=(2, 3), epilogue=loss_ep, name="mm_h2_loss")
    dact = _matmul(dh2b, W["w_down"], tb=True, out_dtype=BF16, name="mm_dact", tm=1024, tn=D_FF)
    dw_down_t = _matmul(dh2b, act2, ta=True, out_dtype=BF16, name="mm_dw_down", tm=1024, tn=256, tk=8192)
    dg_, dup, g["conv_w"], g["conv_b"] = _conv_act_bwd(gu3, a_pre, W["conv_w"], dact.reshape(B, S, D_FF), name="conv_act_bwd")
    dg2 = dg_.reshape(T, D_FF)
    dup2 = dup.reshape(T, D_FF)
    dw_ffn_t = _matmul(u2, dg2, ta=True, out_t=True, out_dtype=BF16, into=lax.empty((2 * D_FF, D), BF16), o_noff=0, name="mm_dw_ffn_g", tm=1024, tn=256, tk=8192)
    dw_ffn_t = _matmul(u2, dup2, ta=True, out_t=True, out_dtype=BF16, into=dw_ffn_t, o_noff=D_FF // 256, name="mm_dw_ffn_u", tm=1024, tn=256, tk=8192)
    tok = emit("ffn", dict(w_ffn_t=dw_ffn_t, w_down=dw_down_t.T))
    def norm2_bwd_ep(acc_g, acc_u, h_ref, g_ref, dh2_ref):
        _, vjp = jax.vjp(_rms, h_ref[...], g_ref[...])
        dx, dg = vjp(acc_g + acc_u)
        dx = dx + dh2_ref[...]
        return dx, dx, dg

    dh, dhb, g["norm2_g"] = _matmul_ep(
        [(dg2, W["w_ffn_t"], False, 0), (dup2, W["w_ffn_t"], False, 1)], tm=512, ins=[h, small["norm2_g"] + zero(tok), dh2], in_specs=[row(512), vec, row(512)],
        out_shapes=[tile(F32), tile(BF16), vec_shape], out_specs=[row(512), row(512), vec], sums=(2,), epilogue=norm2_bwd_ep, name="mm_du2_norm2_bwd")
    dw_out = _matmul(merged, dhb, ta=True, out_dtype=BF16, name="mm_dw_out", tm=1024, tn=1024, tk=2048)

    def merge_bwd_ep(acc, g_ref, pa_ref, pb_ref, dz_in):
        gt = g_ref[...].astype(F32)
        sa = _sigmoid(gt[:, :D_MODEL])
        sb = _sigmoid(gt[:, D_MODEL:])
        dgates = jnp.concatenate([acc * pa_ref[...].astype(F32) * sa * (1.0 - sa), acc * pb_ref[...].astype(F32) * sb * (1.0 - sb)], axis=1)
        return dgates, acc * sa, acc * sb

    dz, dpa, dpb = _matmul_ep(
        [(dhb, W["w_out"], True, 0)], tm=512, ins=[z, pa, pb, lax.empty((T, W_IN), BF16)], in_specs=[_gates_spec(512), row(512), row(512), ANY],
        out_shapes=[jax.ShapeDtypeStruct((T, W_IN), BF16), tile(BF16), tile(BF16)], out_specs=[_gates_spec(512), row(512), row(512)],
        aliases={3: 0}, epilogue=merge_bwd_ep, name="mm_dmerged_merge_bwd")
    doa, dob = _matmul_ep([(dpa, W["w_a"], True, 0), (dpb, W["w_b"], True, 0)], tm=1024, ins=[], in_specs=[],
                          out_shapes=[tile(BF16)] * 2, out_specs=[row(1024)] * 2, epilogue=lambda da, db: (da, db), name="mm_doa_dob")
    dw_a = _matmul(oa2, dpa, ta=True, out_dtype=BF16, name="mm_dw_a", tm=1024, tn=1024, tk=2048)
    dw_b = _matmul(ob2, dpb, ta=True, out_dtype=BF16, name="mm_dw_b", tm=1024, tn=1024, tk=2048)
    tok = emit("mix", dict(w_out=dw_out, w_a=dw_a, w_b=dw_b))
    dz3, dkv_cur, dkv_prev, dsinks = _swa_bwd(z3, qr, kr, cos, sin, small["attn_sinks"] + zero(tok), lse, dob.reshape(B, S, D),
                                              dz.reshape(B, S, W_IN), name="swa_bwd")
    dz3 = _swa_dkv_combine(dkv_cur, dkv_prev, dz3, name="swa_dkv")
    g["attn_sinks"] = dsinks
    dz3, g["lb"], g["hgrn_norm_g"] = _hgrn_bwd(z3, lb, small["hgrn_norm_g"], states, doa.reshape(B, S, D), dz3, name="hgrn_bwd")
    dz = dz3.reshape(T, W_IN)
    dw_in_t = _matmul(u1, dz, ta=True, out_t=True, o_block_perm=_reference_row_block, out_dtype=BF16, name="mm_dw_in", tm=1024, tn=256, tk=8192)
    tok = emit("in", dict(w_in_t=dw_in_t))
    du1 = _matmul(dz, w_in_t, after=tok, out_dtype=BF16, name="mm_du1", tm=1024, tn=512)
    dx, g["norm1_g"] = _norm_bwd_add(x2, small["norm1_g"], du1, dh, with_bf16=False, name="norm1_bwd")
    g["lb_logits"] = _lb_bwd(g.pop("lb"), lb, name="lb_bwd")
    return loss, dx.reshape(B, S, D), g


def _my_place():
    return lax.axis_index("x"), lax.axis_index("y"), lax.axis_index("c")


def _gather_blocks(x_ref, out_ref, send_sems, recv_sems, local_sem):
    x, y, c = _my_place()
    me, sibling = (x, y, c), (x, y, 1 - c)
    chips = [(1 - x, y), (x, 1 - y), (1 - x, 1 - y)]

    def slot(px, py, pc):
        return out_ref.at[4 * px + 2 * py + pc]

    def copy(k, block, to, src=None):
        return pltpu.make_async_remote_copy(
            src_ref=slot(*block) if src is None else src, dst_ref=slot(*block),
            send_sem=send_sems.at[k], recv_sem=recv_sems.at[k], device_id=to, device_id_type=MESH)

    mine = pltpu.make_async_copy(x_ref, slot(*me), local_sem)
    mine.start()
    first = [copy(0, me, sibling, src=x_ref)]
    first += [copy(1 + j, me, (*chip, c), src=x_ref) for j, chip in enumerate(chips)]
    for cp in first:
        cp.start()
    passed = [copy(4 + j, (*chip, c), sibling) for j, chip in enumerate(chips)]
    for j, chip in enumerate(chips):
        copy(1 + j, (*chip, c), me).wait_recv()
        passed[j].start()
    copy(0, sibling, me).wait_recv()
    for j, chip in enumerate(chips):
        copy(4 + j, (*chip, 1 - c), me).wait_recv()
    for cp in first + passed:
        cp.wait_send()
    mine.wait()


GATHER_SEMS = [pltpu.SemaphoreType.DMA((7,)), pltpu.SemaphoreType.DMA((7,)), pltpu.SemaphoreType.DMA]


def _all_gather(blk, *, name):
    return pl.pallas_call(
        _gather_body_fn(), name=name,
        out_shape=jax.ShapeDtypeStruct((N_DEV,) + blk.shape, blk.dtype),
        in_specs=[ANY], out_specs=ANY,
        scratch_shapes=GATHER_SEMS,
    )(blk)


def _gather_body_fn():
    def body(x_ref, out_ref, send_sems, recv_sems, local_sem):
        _gather_blocks(x_ref, out_ref, send_sems, recv_sems, local_sem)
    return body


SLAB_W = 1152
SMALL_SHAPES = dict(norm1_g=(1, D_MODEL), lb_logits=(2, HGRN_HEADS * HGRN_DK), hgrn_norm_g=(1, HGRN_DK), attn_sinks=(1, ATT_HEADS),
                    norm2_g=(1, D_MODEL), conv_b=(1, D_FF), final_g=(1, D_MODEL))
CONVW_BLK = D_FF // N_DEV
CONVW_STRIDE = SLAB_W // 3


def _slab_layout():
    layout, r = {}, 0
    for nm, (nr, w) in SMALL_SHAPES.items():
        layout[nm] = []
        for i in range(nr):
            for c0 in range(0, w, SLAB_W):
                layout[nm].append((r, i, c0, min(SLAB_W, w - c0)))
                r += 1
    return layout, r


SMALL_ROWS, _N_SMALL_ROWS = _slab_layout()
CONV_ROW0 = -(-_N_SMALL_ROWS // 8) * 8
LOSS_ROW = CONV_ROW0 + N_DEV
SLAB_ROWS = LOSS_ROW + 8


def _small_step(grads, g_conv_w, loss, params, moments, variances, dev, *, name):
    names = list(SMALL_ROWS)
    n = len(names)

    def body(dev_ref, *refs):
        g_refs = dict(zip(names, refs[:n]))
        gc_ref, loss_ref = refs[n], refs[n + 1]
        base = n + 2
        w_refs, m_refs, v_refs = (dict(zip(names + ["conv_w"], refs[base + i * (n + 1):base + (i + 1) * (n + 1)])) for i in range(3))
        o = base + 3 * (n + 1)
        gath_ref, loss_out = refs[o], refs[o + 1]
        outs = {nm: refs[o + 2 + 4 * i:o + 6 + 4 * i] for i, nm in enumerate(names + ["conv_w"])}
        slab, total, send_sems, recv_sems, local_sem = refs[-5:]

        slab[...] = jnp.zeros_like(slab)
        for nm, pieces in SMALL_ROWS.items():
            for r, i, c0, w in pieces:
                slab[r:r + 1, 0:w] = g_refs[nm][i:i + 1, c0:c0 + w]
        for p in range(N_DEV):
            for j in range(3):
                slab[CONV_ROW0 + p:CONV_ROW0 + p + 1, j * CONVW_STRIDE:j * CONVW_STRIDE + CONVW_BLK] = gc_ref[j:j + 1, p * CONVW_BLK:(p + 1) * CONVW_BLK]
        slab[LOSS_ROW:LOSS_ROW + 1, 0:1] = loss_ref[...]
        _gather_blocks(slab, gath_ref, send_sems, recv_sems, local_sem)
        acc = gath_ref[0]
        for p in range(1, N_DEV):
            acc = acc + gath_ref[p]
        total[...] = acc
        loss_out[...] = total[LOSS_ROW:LOSS_ROW + 1, 0:1]

        def update(nm, g, i, c0, w):
            at = (slice(i, i + 1), slice(c0, c0 + w))
            d, mn, vn = _adamw_math(w_refs[nm][at], g, m_refs[nm][at], v_refs[nm][at])
            for ref, val in zip(outs[nm], (g, d, mn, vn)):
                ref[at] = val

        for nm, pieces in SMALL_ROWS.items():
            for r, i, c0, w in pieces:
                update(nm, total[r:r + 1, 0:w], i, c0, w)
        conv_rows = total[CONV_ROW0:CONV_ROW0 + N_DEV, :]
        rowid = lax.broadcasted_iota(jnp.int32, conv_rows.shape, 0)
        mine = jnp.sum(jnp.where(rowid == dev_ref[0], conv_rows, 0.0), axis=0, keepdims=True)
        for j in range(3):
            update("conv_w", mine[:, j * CONVW_STRIDE:j * CONVW_STRIDE + CONVW_BLK], j, 0, CONVW_BLK)

    order = names + ["conv_w"]
    ins = [grads[nm] for nm in names] + [g_conv_w, loss]
    for d in (params, moments, variances):
        ins += [d[nm] for nm in order]
    vmem = pl.BlockSpec(memory_space=pltpu.VMEM)
    out_shape = [jax.ShapeDtypeStruct((N_DEV, SLAB_ROWS, SLAB_W), F32), jax.ShapeDtypeStruct((1, 1), F32)]
    for nm in order:
        out_shape += [jax.ShapeDtypeStruct(params[nm].shape, F32)] * 4
    res = pl.pallas_call(
        body, name=name,
        grid_spec=pltpu.PrefetchScalarGridSpec(
            num_scalar_prefetch=1, grid=(1,),
            in_specs=[vmem] * len(ins), out_specs=[vmem] * len(out_shape),
            scratch_shapes=[pltpu.VMEM((SLAB_ROWS, SLAB_W), F32), pltpu.VMEM((SLAB_ROWS, SLAB_W), F32)] + GATHER_SEMS),
        out_shape=out_shape,
    )(dev, *ins)
    return res[1], {nm: tuple(res[2 + 4 * i:6 + 4 * i]) for i, nm in enumerate(order)}


HBM_SPEC = pl.BlockSpec(memory_space=pltpu.HBM)
SEM_SPEC = pl.BlockSpec(memory_space=pltpu.SEMAPHORE)
DATAFLOW_EFFECT = pltpu.SideEffectType.DATAFLOW_SIDE_EFFECTING
N_PEERS = N_DEV - 1


def _peers(x, y, c):
    return [(1 - x if r & 4 else x, 1 - y if r & 2 else y, 1 - c if r & 1 else c) for r in range(1, N_DEV)]


def _exchange_start(srcs, scatter, *, after=None, name):
    n = len(srcs)
    lands = [lax.empty(a.shape if scatter else (N_DEV,) + a.shape, a.dtype) for a in srcs]
    extra = [] if after is None else [after]

    def body(*refs):
        src_refs, land_refs = refs[:n], refs[n:2 * n]
        send_sems, recv_sems, token = refs[2 * n + len(extra)], refs[2 * n + len(extra) + 1], refs[-1]
        x, y, c = _my_place()
        me = 4 * x + 2 * y + c
        for i in range(n):
            for r, (tx, ty, tc) in enumerate(_peers(x, y, c)):
                src = src_refs[i].at[4 * tx + 2 * ty + tc] if scatter else src_refs[i]
                pltpu.make_async_remote_copy(
                    src_ref=src, dst_ref=land_refs[i].at[me], send_sem=send_sems.at[N_PEERS * i + r],
                    recv_sem=recv_sems.at[N_PEERS * i + r], device_id=(tx, ty, tc), device_id_type=MESH).start()
        token[...] = jnp.zeros_like(token)

    thru = [pltpu.HBM(a.shape, a.dtype) for a in list(srcs) + lands]
    res = pl.pallas_call(
        body, name=name,
        out_shape=(pltpu.SemaphoreType.DMA((N_PEERS * n,)), pltpu.SemaphoreType.DMA((N_PEERS * n,)), *thru,
                   jax.ShapeDtypeStruct((8, 128), F32)),
        in_specs=[HBM_SPEC] * (2 * n) + [ANY] * len(extra),
        out_specs=(SEM_SPEC, SEM_SPEC, *([HBM_SPEC] * (2 * n)), pl.BlockSpec(memory_space=pltpu.VMEM)),
        input_output_aliases={i: 2 + i for i in range(2 * n)},
        compiler_params=pltpu.CompilerParams(has_side_effects=DATAFLOW_EFFECT),
    )(*[pltpu.with_memory_space_constraint(a, pltpu.HBM) for a in list(srcs) + lands], *extra)
    return (res[0], res[1], list(res[2:2 + n]), list(res[2 + n:2 + 2 * n]), scatter), res[-1]


def _exchange_wait(handle, after, *, name):
    send_sems, recv_sems, srcs, lands, scatter = handle
    n = len(srcs)

    def body(*refs):
        src_refs, land_refs = refs[:n], refs[n:2 * n]
        send_sems, recv_sems = refs[2 * n], refs[2 * n + 1]
        x, y, c = _my_place()
        for i in range(n):
            for r in range(N_PEERS):
                src = src_refs[i].at[0] if scatter else src_refs[i]
                cp = pltpu.make_async_remote_copy(
                    src_ref=src, dst_ref=land_refs[i].at[0], send_sem=send_sems.at[N_PEERS * i + r],
                    recv_sem=recv_sems.at[N_PEERS * i + r], device_id=(x, y, c), device_id_type=MESH)
                cp.wait_send()
                cp.wait_recv()

    thru = [pltpu.HBM(a.shape, a.dtype) for a in srcs + lands]
    res = pl.pallas_call(
        body, name=name, out_shape=tuple(thru),
        in_specs=[HBM_SPEC] * (2 * n) + [SEM_SPEC, SEM_SPEC, ANY], out_specs=tuple([HBM_SPEC] * (2 * n)),
        input_output_aliases={i: i for i in range(2 * n)},
        compiler_params=pltpu.CompilerParams(has_side_effects=DATAFLOW_EFFECT),
    )(*srcs, *lands, send_sems, recv_sems, after)
    return list(res[:n]), list(res[n:])


def _with_own(land, own, me):
    return lax.dynamic_update_index_in_dim(land, own, me, 0)


def _adamw_math(w, g, m, v):
    m = ADAM_B1 * m + (1.0 - ADAM_B1) * g
    v = ADAM_B2 * v + (1.0 - ADAM_B2) * (g * g)
    m_hat = m / (1.0 - ADAM_B1 ** ADAM_STEP)
    v_hat = v / (1.0 - ADAM_B2 ** ADAM_STEP)
    delta = -ADAM_LR * (m_hat / (jnp.sqrt(v_hat) + ADAM_EPS) + ADAM_WD * w)
    return delta, m, v


def _adamw_sum(parts, w, m, v, *, name):
    shape = w.shape
    R, n = shape[-2], shape[-1]
    w, m, v = (t.reshape(R, n) for t in (w, m, v))
    tr = _pick(R, (256, 464, 352, 128))

    def body(p_ref, w_ref, m_ref, v_ref, g_ref, d_ref, mo_ref, vo_ref):
        g = p_ref[0].astype(F32)
        for p in range(1, N_DEV):
            g = g + p_ref[p].astype(F32)
        d, mn, vn = _adamw_math(w_ref[...], g, m_ref[...], v_ref[...])
        g_ref[...] = g
        d_ref[...] = d
        mo_ref[...] = mn
        vo_ref[...] = vn

    row = pl.BlockSpec((tr, n), lambda i: (i, 0))
    outs = pl.pallas_call(
        body, name=name, grid=(R // tr,),
        in_specs=[pl.BlockSpec((N_DEV, tr, n), lambda i: (0, i, 0)), row, row, row],
        out_specs=[row, row, row, row],
        out_shape=[jax.ShapeDtypeStruct((R, n), F32)] * 4,
        compiler_params=_params("parallel"),
    )(parts, w, m, v)
    return [t.reshape(shape) for t in outs]


def _lb_bwd(dlb, lb, *, name):
    def body(d_ref, lb_ref, o_ref):
        t = d_ref[...] * lb_ref[...] * (1.0 - lb_ref[...])
        o_ref[0:1, :] = t
        o_ref[1:2, :] = -t

    return pl.pallas_call(body, name=name, out_shape=jax.ShapeDtypeStruct((2, lb.shape[1]), F32))(dlb, lb)


DOWN_BLK, ROW_BLK = D_FF // N_DEV, D_MODEL // N_DEV
W_FFN_BLK = 2 * D_FF // N_DEV
CONV_BITS_SHAPE = (16, 256)


def kernel(x, positions, norm1_g, w_in, lb_logits, hgrn_norm_g, w_a, attn_sinks, w_b, w_out, norm2_g, w_ffn_in, conv_w, conv_b, w_down, final_g, loss_target, m_norm1_g, m_w_in, m_lb_logits, m_hgrn_norm_g, m_w_a, m_attn_sinks, m_w_b, m_w_out, m_norm2_g, m_w_ffn_in, m_conv_w, m_conv_b, m_w_down, m_final_g, v_norm1_g, v_w_in, v_lb_logits, v_hgrn_norm_g, v_w_a, v_attn_sinks, v_w_b, v_w_out, v_norm2_g, v_w_ffn_in, v_conv_w, v_conv_b, v_w_down, v_final_g):
    xi, yi, ci = _my_place()
    dev = 4 * xi + 2 * yi + ci

    tr = lambda t: jnp.transpose(t[0])
    untr = lambda t: jnp.transpose(t)[None]
    w_in_blocks = _all_gather(tr(w_in).astype(BF16), name="ag_w_in")
    conv_bits = lax.bitcast_convert_type(conv_w, BF16).reshape(-1)
    conv_bits = jnp.pad(conv_bits, (0, CONV_BITS_SHAPE[0] * CONV_BITS_SHAPE[1] - conv_bits.shape[0])).reshape(CONV_BITS_SHAPE)
    w_in_full_t = _reordered_rows(w_in_blocks.reshape(W_IN, D_MODEL), name="w_in_rows")
    gather_handles = {}
    gather_handles["mix"], tok_mix = _exchange_start([w_a[0].astype(BF16), w_b[0].astype(BF16), w_out[0].astype(BF16)], False,
                                                     after=w_in_full_t, name="ag_mix_start")
    gather_handles["ffn"], tok_ffn = _exchange_start([tr(w_ffn_in).astype(BF16), w_down[0].astype(BF16), conv_bits], False,
                                                     after=tok_mix, name="ag_ffn_start")
    start_token = tok_mix + tok_ffn

    def rest_weights(group, after):
        own, lands = _exchange_wait(gather_handles[group], after, name="ag_" + group + "_wait")
        full = [_with_own(l, o, dev) for l, o in zip(lands, own)]
        if group == "mix":
            return dict(zip(("w_a", "w_b", "w_out"), [t.reshape(D_MODEL, D_MODEL) for t in full]))
        bits = full[2].reshape(N_DEV, -1)[:, :3 * CONVW_BLK * 2].reshape(N_DEV, 3, CONVW_BLK, 2)
        return dict(w_ffn_t=full[0].reshape(2 * D_FF, D_MODEL), w_down=full[1].reshape(D_FF, D_MODEL),
                    conv_w=lax.bitcast_convert_type(bits, F32).transpose(1, 0, 2).reshape(3, D_FF))

    handles = {}

    def emit(group, gr):
        if group == "ffn":
            srcs = [gr["w_ffn_t"].reshape(N_DEV, W_FFN_BLK, D_MODEL), gr["w_down"].reshape(N_DEV, DOWN_BLK, D_MODEL)]
        elif group == "mix":
            srcs = [gr[n].reshape(N_DEV, ROW_BLK, D_MODEL) for n in ("w_out", "w_a", "w_b")]
        else:
            srcs = [gr["w_in_t"].reshape(N_DEV, W_IN_BLK, D_MODEL)]
        handles[group], token = _exchange_start(srcs, True, name="rs_" + group + "_start")
        return token

    small = dict(norm1_g=norm1_g, lb_logits=lb_logits, hgrn_norm_g=hgrn_norm_g, attn_sinks=attn_sinks, norm2_g=norm2_g,
                 conv_b=conv_b, final_g=final_g)
    loss, grad_x, g = _local_step(x, positions, loss_target, small, w_in_full_t, rest_weights, emit, start_token)

    def parts_of(group, after):
        srcs, lands = _exchange_wait(handles[group], after, name="rs_" + group + "_wait")
        return [_with_own(l, lax.dynamic_index_in_dim(s, dev, 0, keepdims=False), dev) for s, l in zip(srcs, lands)]

    p_ffn, p_down = parts_of("ffn", grad_x)
    p_out, p_a, p_b = parts_of("mix", grad_x)
    (p_in,) = parts_of("in", grad_x)
    big = dict(
        w_in=[untr(t) for t in _adamw_sum(p_in, tr(w_in), tr(m_w_in), tr(v_w_in), name="adamw_w_in")],
        w_a=_adamw_sum(p_a, w_a, m_w_a, v_w_a, name="adamw_w_a"),
        w_b=_adamw_sum(p_b, w_b, m_w_b, v_w_b, name="adamw_w_b"),
        w_out=_adamw_sum(p_out, w_out, m_w_out, v_w_out, name="adamw_w_out"),
        w_ffn_in=[untr(t) for t in _adamw_sum(p_ffn, tr(w_ffn_in), tr(m_w_ffn_in), tr(v_w_ffn_in), name="adamw_w_ffn_in")],
        w_down=_adamw_sum(p_down, w_down, m_w_down, v_w_down, name="adamw_w_down"),
    )

    row = lambda t: t.reshape(1, -1) if t.ndim == 1 else t
    shard = lambda t: t.reshape(3, CONVW_BLK)
    sm_g = {nm: g[nm] for nm in SMALL_ROWS}
    sm_w = dict(norm1_g=norm1_g, lb_logits=lb_logits, hgrn_norm_g=hgrn_norm_g, attn_sinks=attn_sinks, norm2_g=norm2_g,
                conv_b=conv_b, final_g=row(final_g), conv_w=shard(conv_w))
    sm_m = dict(norm1_g=m_norm1_g, lb_logits=m_lb_logits, hgrn_norm_g=m_hgrn_norm_g, attn_sinks=m_attn_sinks, norm2_g=m_norm2_g,
                conv_b=m_conv_b, final_g=row(m_final_g), conv_w=shard(m_conv_w))
    sm_v = dict(norm1_g=v_norm1_g, lb_logits=v_lb_logits, hgrn_norm_g=v_hgrn_norm_g, attn_sinks=v_attn_sinks, norm2_g=v_norm2_g,
                conv_b=v_conv_b, final_g=row(v_final_g), conv_w=shard(v_conv_w))
    loss_total, sm_out = _small_step(sm_g, g["conv_w"], loss, sm_w, sm_m, sm_v, dev.astype(jnp.int32).reshape(1), name="small_step")
    shapes = dict(final_g=final_g.shape, conv_w=conv_w.shape)

    names = ("norm1_g", "w_in", "lb_logits", "hgrn_norm_g", "w_a", "attn_sinks", "w_b", "w_out", "norm2_g", "w_ffn_in", "conv_w", "conv_b", "w_down", "final_g")
    outs = [loss_total.reshape(()), grad_x]
    for kind in range(4):
        outs += [big[n][kind] if n in big else sm_out[n][kind].reshape(shapes.get(n, sm_out[n][kind].shape)) for n in names]
    return tuple(outs)
```

```python
import jax
import jax.numpy as jnp
from jax import lax
from jax.experimental import pallas as pl
from jax.experimental.pallas import tpu as pltpu

F32 = jnp.float32
BF16 = jnp.bfloat16

D_MODEL = 1024
HGRN_HEADS = 8
HGRN_DK = 128
CHUNK = 64
ATT_HEADS = 16
ATT_KV_HEADS = 2
ATT_HD = 64
ATT_GROUP = ATT_HEADS // ATT_KV_HEADS
WINDOW = 128
ROPE_DIM = ATT_HD // 4
ROPE_THETA = 500000.0
D_FF = 2816
EPS = 1e-6
NEG_INF = -1e30
N_DEV = 8

ADAM_LR = 0.001
ADAM_B1 = 0.9
ADAM_B2 = 0.999
ADAM_EPS = 1e-08
ADAM_WD = 0.01
ADAM_STEP = 10

MESH = pl.DeviceIdType.MESH
ANY = pl.BlockSpec(memory_space=pl.ANY)


def _pick(n, cands):
    for c in cands:
        if n % c == 0:
            return c
    return n


def _sigmoid(x):
    return 0.5 * jnp.tanh(0.5 * x) + 0.5


def _silu(x):
    hx = 0.5 * x
    return hx * jnp.tanh(hx) + hx


def _rms(x, g):
    return x * lax.rsqrt(jnp.mean(x * x, axis=-1, keepdims=True) + EPS) * g


def _dot(a, b, dims):
    return lax.dot_general(a, b, (dims, ((), ())), preferred_element_type=F32)


def _nn(a, b):
    return _dot(a, b, ((1,), (0,)))


def _nt(a, b):
    return _dot(a, b, ((1,), (1,)))


def _tn(a, b):
    return _dot(a, b, ((0,), (0,)))


def _params(*sem):
    return pltpu.CompilerParams(dimension_semantics=sem, vmem_limit_bytes=56 * 1024 * 1024)


def _matmul(a, b, *, ta=False, tb=False, out_dtype=F32, addend=None, after=None, into=None, o_noff=0, out_t=False,
            o_block_perm=lambda j: j, name, tm, tn, tk=None, n_extent=None, b_koff=0, b_noff=0):
    M, K = (a.shape[1], a.shape[0]) if ta else a.shape
    N = n_extent or (b.shape[0] if tb else b.shape[1])
    tm, tn, tk = min(tm, M), min(tn, N), min(tk or K, K)
    assert M % tm == 0 and N % tn == 0 and K % tk == 0, (name, M, N, K, tm, tn, tk)
    nk = K // tk
    use_scratch = nk > 1 and out_dtype != F32
    grid = (M // tm, N // tn, nk)
    a_spec = pl.BlockSpec((tk, tm), lambda i, j, k: (k, i)) if ta else pl.BlockSpec((tm, tk), lambda i, j, k: (i, k))
    b_spec = pl.BlockSpec((tn, tk), lambda i, j, k: (j + b_noff, k + b_koff)) if tb else pl.BlockSpec((tk, tn), lambda i, j, k: (k + b_koff, j + b_noff))
    o_spec = pl.BlockSpec((tm, tn), lambda i, j, k: (i, j))
    dims = ((0 if ta else 1,), (1 if tb else 0,))
    has_add = addend is not None

    n_in = 2 + has_add + (after is not None) + (into is not None)

    def body(*refs):
        a_ref, b_ref = refs[:2]
        c_ref = refs[2] if has_add else None
        o_ref = refs[n_in]
        part = _dot(a_ref[...], b_ref[...], dims)
        if nk == 1:
            if has_add:
                part = part + c_ref[...].astype(F32)
            o_ref[...] = (part.T if out_t else part).astype(out_dtype)
        else:
            acc_ref = refs[-1] if use_scratch else o_ref
            k = pl.program_id(2)

            @pl.when(k == 0)
            def _():
                acc_ref[...] = part + c_ref[...].astype(F32) if has_add else part

            @pl.when(k > 0)
            def _():
                acc_ref[...] += part

            if use_scratch:
                @pl.when(k == nk - 1)
                def _():
                    o_ref[...] = acc_ref[...].astype(out_dtype)

    in_specs = [a_spec, b_spec] + ([o_spec] if has_add else [])
    args = (a, b) + ((addend,) if has_add else ())
    if after is not None:
        in_specs.append(pl.BlockSpec(after.shape, lambda i, j, k: (0, 0)))
        args += (after,)
    aliases = {}
    if into is not None:
        in_specs.append(ANY)
        args += (into,)
        aliases = {len(args) - 1: 0}
    if out_t:
        assert nk == 1 and not has_add
        o_spec = pl.BlockSpec((tn, tm), lambda i, j, k: (o_block_perm(j) + o_noff, i))
    elif into is not None:
        o_spec = pl.BlockSpec((tm, tn), lambda i, j, k: (i, j + o_noff))
    return pl.pallas_call(
        body,
        name=name,
        grid=grid,
        in_specs=in_specs,
        out_specs=o_spec,
        out_shape=jax.ShapeDtypeStruct(into.shape if into is not None else ((N, M) if out_t else (M, N)), out_dtype),
        input_output_aliases=aliases,
        scratch_shapes=[pltpu.VMEM((tm, tn), F32)] if use_scratch else [],
        compiler_params=_params("parallel", "parallel", "arbitrary"),
    )(*args)


def _matmul_col_tiles(a, b_t, *, tm, tn, tc, name):
    M, K = a.shape
    N = b_t.shape[0]
    tm = min(tm, M)
    per_step = tn // tc

    def body(a_ref, b_ref, o_ref):
        res = _nt(a_ref[...], b_ref[...]).astype(BF16)
        for t in range(per_step):
            o_ref[t] = res[:, t * tc:(t + 1) * tc]

    return pl.pallas_call(
        body, name=name, grid=(M // tm, N // tn),
        in_specs=[pl.BlockSpec((tm, K), lambda i, j: (i, 0)), pl.BlockSpec((tn, K), lambda i, j: (j, 0))],
        out_specs=pl.BlockSpec((per_step, tm, tc), lambda i, j: (j, i, 0)),
        out_shape=jax.ShapeDtypeStruct((N // tc, M, tc), BF16),
        compiler_params=_params("parallel", "parallel"),
    )(a, b_t)


def _matmul_ep(pairs, *, tm, ins, in_specs, out_shapes, out_specs, sums=(), epilogue, aliases=None, name):
    M = pairs[0][0].shape[0]
    tm = min(tm, M)
    mm_specs, mm_args, dims = [], [], []
    for a, b, tb, koff in pairs:
        K = a.shape[1]
        N = b.shape[0] if tb else b.shape[1]
        mm_specs += [pl.BlockSpec((tm, K), lambda i: (i, 0)),
                     pl.BlockSpec((N, K), lambda i, koff=koff: (0, koff)) if tb else pl.BlockSpec((K, N), lambda i, koff=koff: (koff, 0))]
        mm_args += [a, b]
        dims.append(((1,), (1 if tb else 0,)))
    n_mm = len(mm_args)
    n_in = n_mm + len(ins)

    def body(*refs):
        in_refs, out_refs = refs[n_mm:n_in], refs[n_in:]
        accs = [_dot(refs[2 * p][...], refs[2 * p + 1][...], dims[p]) for p in range(len(pairs))]
        outs = epilogue(*accs, *in_refs)
        for k, (ref, val) in enumerate(zip(out_refs, outs)):
            if val is None:
                continue
            if k in sums:
                @pl.when(pl.program_id(0) == 0)
                def _():
                    ref[...] = jnp.zeros_like(ref)

                ref[...] += val
            else:
                ref[...] = val.astype(ref.dtype)

    return pl.pallas_call(
        body, name=name, grid=(M // tm,),
        in_specs=mm_specs + list(in_specs),
        out_specs=list(out_specs), out_shape=list(out_shapes),
        input_output_aliases={n_mm + k: v for k, v in (aliases or {}).items()},
        compiler_params=_params("arbitrary"),
    )(*mm_args, *ins)


def _row_spec(tm, n):
    return pl.BlockSpec((tm, n), lambda i: (i, 0))


def _full_spec(shape):
    return pl.BlockSpec(shape, lambda i: tuple(0 for _ in shape))


def _norm_matmul(x, g, w_t, *, tm, tn, name):
    T, D = x.shape
    N = w_t.shape[0]
    tm = min(tm, T)

    def body(x_ref, g_ref, w_ref, u_ref, z_ref, u_scr):
        @pl.when(pl.program_id(1) == 0)
        def _():
            u = _rms(x_ref[...], g_ref[...]).astype(BF16)
            u_scr[...] = u
            u_ref[...] = u

        z_ref[...] = _nt(u_scr[...], w_ref[...]).astype(BF16)

    return pl.pallas_call(
        body, name=name, grid=(T // tm, N // tn),
        in_specs=[pl.BlockSpec((tm, D), lambda i, j: (i, 0)), pl.BlockSpec((1, D), lambda i, j: (0, 0)),
                  pl.BlockSpec((tn, D), lambda i, j: (j, 0))],
        out_specs=[pl.BlockSpec((tm, D), lambda i, j: (i, 0)), pl.BlockSpec((tm, tn), lambda i, j: (i, j))],
        out_shape=[jax.ShapeDtypeStruct((T, D), BF16), jax.ShapeDtypeStruct((T, N), BF16)],
        scratch_shapes=[pltpu.VMEM((tm, D), BF16)],
        compiler_params=_params("parallel", "arbitrary"),
    )(x, g, w_t)


def _matmul_norm_bwd(dz, w_t, x, g, dres, after, *, tm, name):
    T, K = dz.shape
    D = w_t.shape[1]
    tm = min(tm, T)

    def body(dz_ref, w_ref, x_ref, g_ref, dr_ref, after_ref, dx_ref, dg_ref):
        @pl.when(pl.program_id(0) == 0)
        def _():
            dg_ref[...] = jnp.zeros_like(dg_ref)

        _, vjp = jax.vjp(_rms, x_ref[...], g_ref[...])
        dx, dg = vjp(_nn(dz_ref[...], w_ref[...]))
        dx_ref[...] = dx + dr_ref[...]
        dg_ref[...] += dg

    row = _row_spec(tm, D)
    return pl.pallas_call(
        body, name=name, grid=(T // tm,),
        in_specs=[_row_spec(tm, K), pl.BlockSpec((K, D), lambda i: (0, 0), pipeline_mode=pl.Buffered(1)),
                  row, _full_spec((1, D)), row, _full_spec(after.shape)],
        out_specs=[row, _full_spec((1, D))],
        out_shape=[jax.ShapeDtypeStruct((T, D), F32), jax.ShapeDtypeStruct((1, D), F32)],
        compiler_params=_params("arbitrary"),
    )(dz, w_t, x, g, dres, after)


def _norm_bwd_add(x, g, du, dres, *, with_bf16=True, name):
    T, D = x.shape
    tm = _pick(T, (512, 256, 128))

    def body(x_ref, g_ref, du_ref, dr_ref, dx_ref, *rest):
        dg_ref = rest[-1]
        _, vjp = jax.vjp(_rms, x_ref[...], g_ref[...])
        dx, dg = vjp(du_ref[...].astype(F32))
        dx = dx + dr_ref[...]
        dx_ref[...] = dx
        if with_bf16:
            rest[0][...] = dx.astype(BF16)

        @pl.when(pl.program_id(0) == 0)
        def _():
            dg_ref[...] = jnp.zeros_like(dg_ref)

        dg_ref[...] += dg

    row = _row_spec(tm, D)
    return pl.pallas_call(
        body, name=name, grid=(T // tm,),
        in_specs=[row, _full_spec((1, D)), row, row],
        out_specs=[row] + ([row] if with_bf16 else []) + [_full_spec((1, D))],
        out_shape=[jax.ShapeDtypeStruct((T, D), F32)] + ([jax.ShapeDtypeStruct((T, D), BF16)] if with_bf16 else []) + [jax.ShapeDtypeStruct((1, D), F32)],
        compiler_params=_params("arbitrary"),
    )(x, g, du, dres)


def _merge_fn(gates, a, b):
    ga = gates[:, :D_MODEL].astype(F32)
    gb = gates[:, D_MODEL:].astype(F32)
    return _sigmoid(ga) * a.astype(F32) + _sigmoid(gb) * b.astype(F32)


def _gates_spec(tm):
    return pl.BlockSpec((tm, W_GATES), lambda i: (i, O_GATES // W_GATES))


CONV_TC = 256


def _shift_down(x, n, rows):
    return jnp.where(rows >= n, pltpu.roll(x, n, 0), 0.0)


def _shift_up(x, n, rows, S):
    return jnp.where(rows < S - n, pltpu.roll(x, S - n, 0), 0.0)


def _conv_act_fwd(gu, conv_w, conv_b, *, name):
    _, B, S, tc = gu.shape
    nc = D_FF // tc

    def body(g_ref, up_ref, w_ref, b_ref, o_ref, a_ref):
        g = g_ref[...].astype(F32)
        rows = lax.broadcasted_iota(jnp.int32, g.shape, 0)
        w = w_ref[...]
        a = w[2:3] * g + w[1:2] * _shift_down(g, 1, rows) + w[0:1] * _shift_down(g, 2, rows) + b_ref[...]
        o_ref[...] = (_silu(a) * up_ref[...].astype(F32)).astype(BF16)
        a_ref[...] = a.astype(BF16)

    col = pl.BlockSpec((None, S, tc), lambda b, j: (b, 0, j))
    tile = lambda off: pl.BlockSpec((None, None, S, tc), lambda b, j: (j + off, b, 0, 0))
    return pl.pallas_call(
        body, name=name, grid=(B, nc),
        in_specs=[tile(0), tile(nc),
                  pl.BlockSpec((3, tc), lambda b, j: (0, j)),
                  pl.BlockSpec((1, tc), lambda b, j: (0, j))],
        out_specs=[col, tile(0)],
        out_shape=[jax.ShapeDtypeStruct((B, S, D_FF), BF16), jax.ShapeDtypeStruct((nc, B, S, tc), BF16)],
        compiler_params=_params("parallel", "parallel"),
    )(gu, gu, conv_w, conv_b)


def _conv_act_bwd(gu, a_pre, conv_w, dact, *, name):
    _, B, S, tc = gu.shape
    nc = D_FF // tc

    def body(g_ref, up_ref, a_ref, w_ref, da_ref, dg_ref, dup_ref, dw_ref, db_ref):
        g = g_ref[...].astype(F32)
        up, a, dact = up_ref[...], a_ref[...], da_ref[...]
        rows = lax.broadcasted_iota(jnp.int32, g.shape, 0)
        w = w_ref[...]
        sg = _sigmoid(a)
        dup_ref[...] = dact * a * sg
        da = (dact * up * sg * (1.0 + a * (1.0 - sg))).astype(F32)
        da1 = _shift_up(da, 1, rows, S)
        da2 = _shift_up(da, 2, rows, S)
        dg_ref[...] = (w[2:3] * da + w[1:2] * da1 + w[0:1] * da2).astype(BF16)

        @pl.when(pl.program_id(1) == 0)
        def _():
            dw_ref[...] = jnp.zeros_like(dw_ref)
            db_ref[...] = jnp.zeros_like(db_ref)

        dw_ref[0:1, :] += jnp.sum(da2 * g, axis=0, keepdims=True)
        dw_ref[1:2, :] += jnp.sum(da1 * g, axis=0, keepdims=True)
        dw_ref[2:3, :] += jnp.sum(da * g, axis=0, keepdims=True)
        db_ref[...] += jnp.sum(da, axis=0, keepdims=True)

    col = pl.BlockSpec((None, S, tc), lambda j, b: (b, 0, j))
    tile = lambda off: pl.BlockSpec((None, None, S, tc), lambda j, b: (j + off, b, 0, 0))
    return pl.pallas_call(
        body, name=name, grid=(nc, B),
        in_specs=[tile(0), tile(nc), tile(0),
                  pl.BlockSpec((3, tc), lambda j, b: (0, j)),
                  col],
        out_specs=[col, col, pl.BlockSpec((3, tc), lambda j, b: (0, j)), pl.BlockSpec((1, tc), lambda j, b: (0, j))],
        out_shape=[jax.ShapeDtypeStruct((B, S, D_FF), BF16), jax.ShapeDtypeStruct((B, S, D_FF), BF16),
                   jax.ShapeDtypeStruct((3, D_FF), F32), jax.ShapeDtypeStruct((1, D_FF), F32)],
        compiler_params=_params("parallel", "arbitrary"),
    )(gu, gu, a_pre, conv_w, dact)


HGRN_CPB = 8
HF = HGRN_HEADS * HGRN_DK


def _tri(n, upper=False):
    r = lax.broadcasted_iota(jnp.int32, (n, n), 0)
    c = lax.broadcasted_iota(jnp.int32, (n, n), 1)
    return (c >= r) if upper else (r >= c)


def _hs(h):
    return slice(h * HGRN_DK, (h + 1) * HGRN_DK)


def _cumsum_rows(tri_b, x):
    hi = x.astype(BF16)
    lo = (x - hi.astype(F32)).astype(BF16)
    return _nn(tri_b, hi) + _nn(tri_b, lo)


def _hgrn_col(seg, h):
    return slice(seg * HF + h * HGRN_DK, seg * HF + (h + 1) * HGRN_DK)


def _hgrn_gates(q, fz, lb):
    sg = _sigmoid(fz)
    return _sigmoid(q), sg, lb + (1.0 - lb) * sg


def _hgrn_decays(b, q, sq, f):
    qf = q * sq
    k = 1.0 - f
    bref = b[CHUNK // 2:CHUNK // 2 + 1, :]
    blast = b[CHUNK - 1:CHUNK, :]
    e1 = jnp.exp2(b - bref)
    e2 = jnp.exp2(bref - b)
    e3 = e1 * jnp.exp2(bref)
    e4 = e2 * jnp.exp2(blast - bref)
    return (e1, e2, e3, e4), qf * e1, k * e2, qf * e3, k * e4, jnp.exp2(blast)


def _hgrn_fwd(zh, lb, gn, *, name):
    B, S, _ = zh.shape
    cpb = HGRN_CPB
    ts = cpb * CHUNK
    nblk = S // ts

    def body(z_ref, lb_ref, gn_ref, o_ref, st_ref, state):
        @pl.when(pl.program_id(1) == 0)
        def _():
            state[...] = jnp.zeros_like(state)

        R = range(HGRN_HEADS)
        causal = _tri(CHUNK)
        tril_b = causal.astype(BF16)
        lbh = [lb_ref[:, _hs(h)] for h in R]
        for c in range(cpb):
            rows = slice(c * CHUNK, (c + 1) * CHUNK)
            q = [z_ref[rows, _hgrn_col(0, h)].astype(F32) for h in R]
            gates = [_hgrn_gates(q[h], z_ref[rows, _hgrn_col(1, h)].astype(F32), lbh[h]) for h in R]
            b = [_cumsum_rows(tril_b, jnp.log2(gates[h][2])) for h in R]
            v = [z_ref[rows, _hgrn_col(2, h)] for h in R]
            dec, q_in, k_in, q_out, k_st = [], [], [], [], []
            for h in R:
                _, qi, ki, qo, ks, d = _hgrn_decays(b[h], q[h], gates[h][0], gates[h][2])
                dec.append(d)
                for lst, t in zip((q_in, k_in, q_out, k_st), (qi, ki, qo, ks)):
                    lst.append(t.astype(BF16))
            a = [jnp.where(causal, _nt(q_in[h], k_in[h]), 0.0).astype(BF16) for h in R]
            st = [state[h] for h in R]
            for h in R:
                st_ref[c, h] = st[h]
            o = [_nn(a[h], v[h]) + _nt(q_out[h], st[h].astype(BF16)) for h in R]
            for h in R:
                state[h] = st[h] * dec[h] + _tn(v[h], k_st[h])
            for h in R:
                o_ref[rows, _hs(h)] = (_rms(o[h], gn_ref[...]) * _silu(z_ref[rows, _hgrn_col(3, h)].astype(F32))).astype(BF16)

    return pl.pallas_call(
        body, name=name, grid=(B, nblk),
        in_specs=[pl.BlockSpec((None, ts, 4 * HF), lambda b, s: (b, s, 0)),
                  pl.BlockSpec((1, HF), lambda b, s: (0, 0)),
                  pl.BlockSpec((1, HGRN_DK), lambda b, s: (0, 0))],
        out_specs=[pl.BlockSpec((None, ts, HF), lambda b, s: (b, s, 0)),
                   pl.BlockSpec((None, cpb, HGRN_HEADS, HGRN_DK, HGRN_DK), lambda b, s: (b, s, 0, 0, 0))],
        out_shape=[jax.ShapeDtypeStruct((B, S, HF), BF16),
                   jax.ShapeDtypeStruct((B, S // CHUNK, HGRN_HEADS, HGRN_DK, HGRN_DK), F32)],
        scratch_shapes=[pltpu.VMEM((HGRN_HEADS, HGRN_DK, HGRN_DK), F32)],
        compiler_params=_params("arbitrary", "arbitrary"),
    )(zh, lb, gn)


def _hgrn_bwd(zh, lb, gn, states, doa, dz, *, name):
    B, S, _ = zh.shape
    cpb = HGRN_CPB
    ts = cpb * CHUNK
    nblk = S // ts
    rev = lambda b, s: (b, nblk - 1 - s, 0)

    def body(z_ref, lb_ref, gn_ref, st_ref, do_ref, dz_in, dz_ref, dlb_ref, dgn_ref, dstate):
        @pl.when(pl.program_id(1) == 0)
        def _():
            dstate[...] = jnp.zeros_like(dstate)

        @pl.when((pl.program_id(0) == 0) & (pl.program_id(1) == 0))
        def _():
            dlb_ref[...] = jnp.zeros_like(dlb_ref)
            dgn_ref[...] = jnp.zeros_like(dgn_ref)

        R = range(HGRN_HEADS)
        causal = _tri(CHUNK)
        tril_b = causal.astype(BF16)
        triu_b = _tri(CHUNK, upper=True).astype(BF16)
        rowid = lax.broadcasted_iota(jnp.int32, (CHUNK, HGRN_DK), 0)
        lbh = [lb_ref[:, _hs(h)] for h in R]
        gn = gn_ref[...]
        for c in reversed(range(cpb)):
            rows = slice(c * CHUNK, (c + 1) * CHUNK)
            q = [z_ref[rows, _hgrn_col(0, h)].astype(F32) for h in R]
            gates = [_hgrn_gates(q[h], z_ref[rows, _hgrn_col(1, h)].astype(F32), lbh[h]) for h in R]
            b = [_cumsum_rows(tril_b, jnp.log2(gates[h][2])) for h in R]
            v = [z_ref[rows, _hgrn_col(2, h)] for h in R]
            pre = [_hgrn_decays(b[h], q[h], gates[h][0], gates[h][2]) for h in R]
            q_in_b, k_in_b, q_out_b, k_st_b = ([pre[h][i].astype(BF16) for h in R] for i in (1, 2, 3, 4))
            a_b = [jnp.where(causal, _nt(q_in_b[h], k_in_b[h]), 0.0).astype(BF16) for h in R]
            st = [st_ref[c, h] for h in R]
            st_b = [t.astype(BF16) for t in st]
            o = [_nn(a_b[h], v[h]) + _nt(q_out_b[h], st_b[h]) for h in R]
            do_l, dgn_acc = [], jnp.zeros_like(gn)
            for h in R:
                hg = z_ref[rows, _hgrn_col(3, h)].astype(F32)
                dout = do_ref[rows, _hs(h)].astype(F32)
                shg = _sigmoid(hg)
                on_h, norm_vjp = jax.vjp(_rms, o[h], gn)
                d_o, d_gn = norm_vjp(dout * (hg * shg))
                do_l.append(d_o)
                dgn_acc = dgn_acc + d_gn
                dz_ref[rows, _hgrn_col(3, h)] = (dout * on_h * shg * (1.0 + hg * (1.0 - shg))).astype(BF16)
            dgn_ref[...] += dgn_acc
            do_b = [t.astype(BF16) for t in do_l]
            dst = [dstate[h] for h in R]
            dst_b = [t.astype(BF16) for t in dst]
            da_b = [jnp.where(causal, _nt(do_b[h], v[h]), 0.0).astype(BF16) for h in R]
            dv = [_tn(a_b[h], do_b[h]) + _nt(k_st_b[h], dst_b[h]) for h in R]
            dq_in = [_nn(da_b[h], k_in_b[h]) for h in R]
            dk_in = [_tn(da_b[h], q_in_b[h]) for h in R]
            dq_out = [_nn(do_b[h], st_b[h]) for h in R]
            dk_st = [_nn(v[h], dst_b[h]) for h in R]
            for h in R:
                dz_ref[rows, _hgrn_col(2, h)] = dv[h].astype(BF16)
            db = []
            for h in R:
                _, q_in, k_in, q_out, k_st, dec = pre[h]
                ddec = jnp.sum(st[h] * dst[h], axis=0, keepdims=True)
                t_qin, t_kin, t_kst = dq_in[h] * q_in, dk_in[h] * k_in, dk_st[h] * k_st
                dbref = jnp.sum(t_kin - t_qin, axis=0, keepdims=True)
                dblast = jnp.sum(t_kst, axis=0, keepdims=True) + ddec * dec
                db.append(t_qin - t_kin + dq_out[h] * q_out - t_kst
                          + jnp.where(rowid == CHUNK // 2, dbref, 0.0) + jnp.where(rowid == CHUNK - 1, dblast, 0.0))
            for h in R:
                dstate[h] = dst[h] * pre[h][5] + _tn(do_b[h], q_out_b[h])
            dlogf = [_cumsum_rows(triu_b, db[h]) for h in R]
            for h in R:
                sq, sg, f = gates[h]
                e1, e2, e3, e4 = pre[h][0]
                dqf = dq_in[h] * e1 + dq_out[h] * e3
                dk = dk_in[h] * e2 + dk_st[h] * e4
                df_open = (dlogf[h] / f - dk) * (1.0 - sg)
                dlb_ref[:, _hs(h)] += jnp.sum(df_open, axis=0, keepdims=True)
                dz_ref[rows, _hgrn_col(1, h)] = (df_open * ((1.0 - lbh[h]) * sg)).astype(BF16)
                dz_ref[rows, _hgrn_col(0, h)] = (dqf * sq * (1.0 + q[h] * (1.0 - sq))).astype(BF16)

    return pl.pallas_call(
        body, name=name, grid=(B, nblk),
        in_specs=[pl.BlockSpec((None, ts, 4 * HF), rev),
                  pl.BlockSpec((1, HF), lambda b, s: (0, 0)),
                  pl.BlockSpec((1, HGRN_DK), lambda b, s: (0, 0)),
                  pl.BlockSpec((None, cpb, HGRN_HEADS, HGRN_DK, HGRN_DK), lambda b, s: (b, nblk - 1 - s, 0, 0, 0)),
                  pl.BlockSpec((None, ts, HF), rev),
                  ANY],
        out_specs=[pl.BlockSpec((None, ts, 4 * HF), rev),
                   pl.BlockSpec((1, HF), lambda b, s: (0, 0)),
                   pl.BlockSpec((1, HGRN_DK), lambda b, s: (0, 0))],
        out_shape=[jax.ShapeDtypeStruct(dz.shape, BF16),
                   jax.ShapeDtypeStruct((1, HF), F32),
                   jax.ShapeDtypeStruct((1, HGRN_DK), F32)],
        input_output_aliases={5: 0},
        scratch_shapes=[pltpu.VMEM((HGRN_HEADS, HGRN_DK, HGRN_DK), F32)],
        compiler_params=_params("arbitrary", "arbitrary"),
    )(zh, lb, gn, states, doa, dz)


KV_W = ATT_KV_HEADS * ATT_HD
ATT_SCALE = ATT_HD ** -0.5


def _rope(x, cos, sin, inverse=False):
    half = ROPE_DIM // 2
    outs = []
    for p in range(x.shape[1] // 128):
        xp = x[:, p * 128:(p + 1) * 128]
        lane = lax.broadcasted_iota(jnp.int32, xp.shape, 1) % ATT_HD
        sw = jnp.where(lane < half, pltpu.roll(xp, 128 - half, 1), pltpu.roll(xp, half, 1))
        outs.append(xp * cos - sw * sin if inverse else xp * cos + sw * sin)
    return outs[0] if len(outs) == 1 else jnp.concatenate(outs, axis=1)


PAIRS_PER_KV = ATT_GROUP // 2


def _swap_halves(x):
    return pltpu.roll(x, ATT_HD, 1)


def _kv_padded(t, low):
    sw = _swap_halves(t)
    zero = jnp.zeros_like(t)
    out = []
    for g in range(ATT_KV_HEADS):
        in_low, in_high = (t, sw) if g == 0 else (sw, t)
        out.append((jnp.where(low, in_low, zero).astype(BF16), jnp.where(low, zero, in_high).astype(BF16)))
    return out


def _swa_mask(first_block):
    qi = lax.broadcasted_iota(jnp.int32, (WINDOW, 2 * WINDOW), 0)
    mi = lax.broadcasted_iota(jnp.int32, (WINDOW, 2 * WINDOW), 1)
    band = (mi > qi) & (mi <= qi + WINDOW)
    return band & (jnp.logical_not(first_block) | (mi >= WINDOW))


def _swa_specs(nb):
    cur = lambda b, i: (b, i, 0)
    prev = lambda b, i: (b, jnp.maximum(i - 1, 0), 0)
    return cur, prev


def _swa_z_specs():
    q = pl.BlockSpec((None, WINDOW, W_AQ), lambda b, i: (b, i, O_AQ // W_AQ))
    kv_prev = pl.BlockSpec((None, WINDOW, W_AKV), lambda b, i: (b, jnp.maximum(i - 1, 0), O_AKV // W_AKV))
    kv_cur = pl.BlockSpec((None, WINDOW, W_AKV), lambda b, i: (b, i, O_AKV // W_AKV))
    return q, kv_prev, kv_cur


def _swa_fwd(z, cos, sin, sinks, *, name):
    B, S, _ = z.shape
    nb = S // WINDOW
    cur, prev = _swa_specs(nb)

    def body(q_ref, kvp_ref, kvc_ref, cp_ref, sp_ref, cc_ref, sc_ref, sink_ref, o_ref, lse_ref, qr_ref, kr_ref):
        cos_c, sin_c = cc_ref[...], sc_ref[...]
        q = (_rope(q_ref[...].astype(F32), cos_c, sin_c) * ATT_SCALE).astype(BF16)
        k = jnp.concatenate([_rope(kvp_ref[:, :KV_W].astype(F32), cp_ref[...], sp_ref[...]),
                             _rope(kvc_ref[:, :KV_W].astype(F32), cos_c, sin_c)], axis=0)
        qr_ref[...] = q
        kr_ref[...] = k[WINDOW:].astype(BF16)
        v = jnp.concatenate([kvp_ref[:, KV_W:], kvc_ref[:, KV_W:]], axis=0).astype(F32)
        low = lax.broadcasted_iota(jnp.int32, k.shape, 1) < ATT_HD
        kpad = _kv_padded(k, low)
        vpad = _kv_padded(v, low)
        mask = _swa_mask(pl.program_id(1) == 0)
        lses = []
        for g in range(ATT_KV_HEADS):
            pairs = range(g * PAIRS_PER_KV, (g + 1) * PAIRS_PER_KV)
            keys = [(p, e) for p in pairs for e in (0, 1)]
            qp = {p: q[:, p * 128:(p + 1) * 128] for p in pairs}
            s = {pe: jnp.where(mask, _nt(qp[pe[0]], kpad[g][pe[1]]), NEG_INF) for pe in keys}
            pr = {}
            for pe in keys:
                sink = sink_ref[0, 2 * pe[0] + pe[1]]
                m = jnp.maximum(jnp.max(s[pe], axis=1, keepdims=True), sink)
                ex = jnp.exp(s[pe] - m)
                den = jnp.sum(ex, axis=1, keepdims=True) + jnp.exp(sink - m)
                pr[pe] = (ex * (1.0 / den)).astype(BF16)
                lses.append(m + jnp.log(den))
            for p in pairs:
                o_ref[:, p * 128:(p + 1) * 128] = (_nn(pr[p, 0], vpad[g][0]) + _nn(pr[p, 1], vpad[g][1])).astype(BF16)
        lse_ref[...] = jnp.concatenate(lses, axis=1)

    tab = lambda im: pl.BlockSpec((None, WINDOW, 128), im)
    return pl.pallas_call(
        body, name=name, grid=(B, nb),
        in_specs=[*_swa_z_specs(),
                  tab(prev), tab(prev), tab(cur), tab(cur),
                  pl.BlockSpec(memory_space=pltpu.SMEM)],
        out_specs=[pl.BlockSpec((None, WINDOW, D_MODEL), cur), pl.BlockSpec((None, WINDOW, ATT_HEADS), cur),
                   pl.BlockSpec((None, WINDOW, D_MODEL), cur), pl.BlockSpec((None, WINDOW, KV_W), cur)],
        out_shape=[jax.ShapeDtypeStruct((B, S, D_MODEL), BF16), jax.ShapeDtypeStruct((B, S, ATT_HEADS), F32),
                   jax.ShapeDtypeStruct((B, S, D_MODEL), BF16), jax.ShapeDtypeStruct((B, S, KV_W), BF16)],
        compiler_params=_params("parallel", "parallel"),
    )(z, z, z, cos, sin, cos, sin, sinks)


def _swa_bwd(z, qr, kr, cos, sin, sinks, lse, dob, dz, *, name):
    B, S, _ = z.shape
    nb = S // WINDOW
    cur, prev = _swa_specs(nb)

    def body(q_ref, krp_ref, krc_ref, kvp_ref, kvc_ref, cp_ref, sp_ref, cc_ref, sc_ref, sink_ref, lse_ref, do_ref, dz_in,
             dq_ref, dkc_ref, dkp_ref, dsink_ref):
        @pl.when((pl.program_id(0) == 0) & (pl.program_id(1) == 0))
        def _():
            dsink_ref[...] = jnp.zeros_like(dsink_ref)

        cos_c, sin_c, cos_p, sin_p = cc_ref[...], sc_ref[...], cp_ref[...], sp_ref[...]
        q = q_ref[...]
        k = jnp.concatenate([krp_ref[...], krc_ref[...]], axis=0).astype(F32)
        v = jnp.concatenate([kvp_ref[:, KV_W:], kvc_ref[:, KV_W:]], axis=0).astype(F32)
        low = lax.broadcasted_iota(jnp.int32, k.shape, 1) < ATT_HD
        kpad = _kv_padded(k, low)
        vpad = _kv_padded(v, low)
        mask = _swa_mask(pl.program_id(1) == 0)
        lse = lse_ref[...]
        dq_parts, dk_sum, dv_sum, dsinks = [], [], [], []
        for g in range(ATT_KV_HEADS):
            pairs = range(g * PAIRS_PER_KV, (g + 1) * PAIRS_PER_KV)
            keys = [(p, e) for p in pairs for e in (0, 1)]
            qp = {p: q[:, p * 128:(p + 1) * 128] for p in pairs}
            dop = {p: do_ref[:, p * 128:(p + 1) * 128] for p in pairs}
            s = {pe: jnp.where(mask, _nt(qp[pe[0]], kpad[g][pe[1]]), NEG_INF) for pe in keys}
            dp = {pe: _nt(dop[pe[0]], vpad[g][pe[1]]) for pe in keys}
            pr, ds = {}, {}
            for pe in keys:
                h = 2 * pe[0] + pe[1]
                lse_h = lse[:, h:h + 1]
                pf = jnp.exp(s[pe] - lse_h)
                delta = jnp.sum(pf * dp[pe], axis=1, keepdims=True)
                ds[pe] = (pf * (dp[pe] - delta)).astype(BF16)
                pr[pe] = pf.astype(BF16)
                p_sink = jnp.exp(sink_ref[0, h] - lse_h)
                dsinks.append(-jnp.sum(p_sink * delta, axis=0, keepdims=True))
            for p in pairs:
                dq_parts.append((_nn(ds[p, 0], kpad[g][0]) + _nn(ds[p, 1], kpad[g][1])) * ATT_SCALE)
            x = [sum(_tn(ds[p, e], qp[p]) for p in pairs) for e in (0, 1)]
            y = [sum(_tn(pr[p, e], dop[p]) for p in pairs) for e in (0, 1)]
            zk = jnp.where(low, x[0], x[1])
            zv = jnp.where(low, y[0], y[1])
            dk_sum.append(zk + _swap_halves(zk))
            dv_sum.append(zv + _swap_halves(zv))
        dq_ref[...] = _rope(jnp.concatenate(dq_parts, axis=1), cos_c, sin_c, inverse=True).astype(BF16)
        dk = jnp.where(low, dk_sum[0], dk_sum[1])
        dv = jnp.where(low, dv_sum[0], dv_sum[1])
        dkp_ref[:, :KV_W] = _rope(dk[:WINDOW], cos_p, sin_p, inverse=True)
        dkp_ref[:, KV_W:] = dv[:WINDOW]
        dkc_ref[:, :KV_W] = _rope(dk[WINDOW:], cos_c, sin_c, inverse=True)
        dkc_ref[:, KV_W:] = dv[WINDOW:]
        dsink_ref[...] += jnp.concatenate(dsinks, axis=1)

    tab = lambda im: pl.BlockSpec((None, WINDOW, 128), im)
    return pl.pallas_call(
        body, name=name, grid=(B, nb),
        in_specs=[pl.BlockSpec((None, WINDOW, D_MODEL), cur), tab(prev), tab(cur),
                  *_swa_z_specs()[1:],
                  tab(prev), tab(prev), tab(cur), tab(cur),
                  pl.BlockSpec(memory_space=pltpu.SMEM),
                  pl.BlockSpec((None, WINDOW, ATT_HEADS), cur),
                  pl.BlockSpec((None, WINDOW, D_MODEL), cur),
                  ANY],
        out_specs=[_swa_z_specs()[0],
                   pl.BlockSpec((None, WINDOW, 2 * KV_W), cur), pl.BlockSpec((None, WINDOW, 2 * KV_W), cur),
                   pl.BlockSpec((1, ATT_HEADS), lambda b, i: (0, 0))],
        out_shape=[jax.ShapeDtypeStruct(dz.shape, BF16),
                   jax.ShapeDtypeStruct((B, S, 2 * KV_W), F32), jax.ShapeDtypeStruct((B, S, 2 * KV_W), F32),
                   jax.ShapeDtypeStruct((1, ATT_HEADS), F32)],
        input_output_aliases={12: 0},
        compiler_params=_params("arbitrary", "arbitrary"),
    )(qr, kr, kr, z, z, cos, sin, cos, sin, sinks, lse, dob, dz)


def _swa_dkv_combine(dkv_cur, dkv_prev, dz, *, name):
    B, S, W = dkv_cur.shape

    def body(c_ref, p_ref, dz_in, o_ref):
        rows = lax.broadcasted_iota(jnp.int32, (S, W), 0)
        o_ref[...] = (c_ref[...] + _shift_up(p_ref[...], WINDOW, rows, S)).astype(BF16)

    seq = pl.BlockSpec((None, S, W), lambda b: (b, 0, 0))
    return pl.pallas_call(
        body, name=name, grid=(B,),
        in_specs=[seq, seq, ANY], out_specs=pl.BlockSpec((None, S, W), lambda b: (b, 0, O_AKV // W_AKV)),
        out_shape=jax.ShapeDtypeStruct(dz.shape, BF16),
        input_output_aliases={2: 0},
        compiler_params=_params("parallel"),
    )(dkv_cur, dkv_prev, dz)


def _rope_tables(positions):
    half = ROPE_DIM // 2
    inv = ROPE_THETA ** (-2.0 * jnp.arange(half, dtype=F32) / ROPE_DIM)
    ang = positions.astype(F32)[..., None] * inv
    c, s = jnp.cos(ang), jnp.sin(ang)
    pad = jnp.zeros(ang.shape[:-1] + (ATT_HD - ROPE_DIM,), F32)
    cos = jnp.concatenate([c, c, pad + 1.0], axis=-1)
    sin = jnp.concatenate([-s, s, pad], axis=-1)
    return jnp.tile(cos, (1, 1, 2)), jnp.tile(sin, (1, 1, 2))


def _lower_bound(lb_logits, *, name):
    def body(l_ref, o_ref):
        l = l_ref[...]
        e = jnp.exp(l - jnp.max(l, axis=0, keepdims=True))
        o_ref[...] = e[0:1] / jnp.sum(e, axis=0, keepdims=True)

    return pl.pallas_call(body, name=name, out_shape=jax.ShapeDtypeStruct((1, lb_logits.shape[1]), F32))(lb_logits)


W_ZH, W_GATES, W_AQ, W_AKV = 4 * HF, 2 * D_MODEL, ATT_HEADS * ATT_HD, 2 * KV_W
O_ZH, O_GATES, O_AQ, O_AKV = 0, W_ZH, W_ZH + W_GATES, W_ZH + W_GATES + W_AQ
W_IN = W_ZH + W_GATES + W_AQ + W_AKV


W_IN_BLK = W_IN // N_DEV


def _reference_row_block(j, rows=256):
    nz, ng = W_ZH // rows, W_GATES // rows
    return jnp.where(j < nz, j, jnp.where(j < nz + ng, j + (W_AQ + W_AKV) // rows, j - ng))


def _reordered_rows(w_t, *, name):
    rows = 256

    def body(i_ref, o_ref):
        o_ref[...] = i_ref[...]

    return pl.pallas_call(
        body, name=name, grid=(W_IN // rows,),
        in_specs=[pl.BlockSpec((rows, D_MODEL), lambda j: (_reference_row_block(j, rows), 0))],
        out_specs=pl.BlockSpec((rows, D_MODEL), lambda j: (j, 0)),
        out_shape=jax.ShapeDtypeStruct(w_t.shape, w_t.dtype), compiler_params=_params("parallel"))(w_t)


def _local_step(x, positions, target, small, w_in_t, rest_weights, emit, start_token):
    B, S, D = x.shape
    T = B * S
    x2 = x.reshape(T, D)
    cos, sin = _rope_tables(positions)
    lb = _lower_bound(small["lb_logits"], name="lb_fwd")
    zero = lambda tok: tok[0:1, 0:1]

    u1, z = _norm_matmul(x2, small["norm1_g"] + zero(start_token), w_in_t, tm=1024, tn=W_IN // 2, name="norm1_mm_z")
    z3 = z.reshape(B, S, W_IN)
    oa, states = _hgrn_fwd(z3, lb, small["hgrn_norm_g"], name="hgrn_fwd")
    ob, lse, qr, kr = _swa_fwd(z3, cos, sin, small["attn_sinks"], name="swa_fwd")
    oa2 = oa.reshape(T, D)
    ob2 = ob.reshape(T, D)
    W = rest_weights("mix", ob)
    row = lambda tm, dtype=None: _row_spec(tm, D)
    tile = lambda dtype: jax.ShapeDtypeStruct((T, D), dtype)
    vec = _full_spec((1, D))
    vec_shape = jax.ShapeDtypeStruct((1, D), F32)

    def merge_ep(acc_a, acc_b, g_ref):
        pa, pb = acc_a.astype(BF16), acc_b.astype(BF16)
        return pa, pb, _merge_fn(g_ref[...], pa, pb)

    pa, pb, merged = _matmul_ep([(oa2, W["w_a"], False, 0), (ob2, W["w_b"], False, 0)], tm=1024, ins=[z], in_specs=[_gates_spec(1024)],
                                out_shapes=[tile(BF16)] * 3, out_specs=[row(1024)] * 3, epilogue=merge_ep, name="mm_pa_pb_merge")

    def resid_norm_ep(acc, x_ref, g_ref):
        hh = acc + x_ref[...]
        return hh, _rms(hh, g_ref[...])

    h, u2 = _matmul_ep([(merged, W["w_out"], False, 0)], tm=1024, ins=[x2, small["norm2_g"]], in_specs=[row(1024), vec],
                       out_shapes=[tile(F32), tile(BF16)], out_specs=[row(1024), row(1024)], epilogue=resid_norm_ep, name="mm_h_norm2")
    W.update(rest_weights("ffn", u2))
    gu3 = _matmul_col_tiles(u2, W["w_ffn_t"], tm=1024, tn=D_FF, tc=CONV_TC, name="mm_gu").reshape(2 * D_FF // CONV_TC, B, S, CONV_TC)
    act, a_pre = _conv_act_fwd(gu3, W["conv_w"], small["conv_b"], name="conv_act_fwd")
    act2 = act.reshape(T, D_FF)
    g = {}

    def loss_ep(acc, h_ref, g_ref, t_ref):
        y, vjp = jax.vjp(_rms, acc + h_ref[...], g_ref[...])
        err = y - t_ref[...]
        dx, dg = vjp(err * (1.0 / D))
        return dx, dx, dg, (0.5 / D) * jnp.sum(jnp.sum(err * err, axis=1, keepdims=True), axis=0, keepdims=True)

    dh2, dh2b, g["final_g"], loss = _matmul_ep(
        [(act2, W["w_down"], False, 0)], tm=512, ins=[h, small["final_g"].reshape(1, D), target.reshape(T, D)], in_specs=[row(512), vec, row(512)],
        out_shapes=[tile(F32), tile(BF16), vec_shape, jax.ShapeDtypeStruct((1, 1), F32)],
        out_specs=[row(512), row(512), vec, _full_spec((1, 1))], sums=(2, 3), epilogue=loss_ep, name="mm_h2_loss")
    dact = _matmul(dh2b, W["w_down"], tb=True, out_dtype=BF16, name="mm_dact", tm=1024, tn=D_FF)
    dw_down_t = _matmul(dh2b, act2, ta=True, out_dtype=BF16, name="mm_dw_down", tm=1024, tn=256, tk=8192)
    dg_, dup, g["conv_w"], g["conv_b"] = _conv_act_bwd(gu3, a_pre, W["conv_w"], dact.reshape(B, S, D_FF), name="conv_act_bwd")
    dg2 = dg_.reshape(T, D_FF)
    dup2 = dup.reshape(T, D_FF)
    dw_ffn_t = _matmul(u2, dg2, ta=True, out_t=True, out_dtype=BF16, into=lax.empty((2 * D_FF, D), BF16), o_noff=0, name="mm_dw_ffn_g", tm=1024, tn=256, tk=8192)
    dw_ffn_t = _matmul(u2, dup2, ta=True, out_t=True, out_dtype=BF16, into=dw_ffn_t, o_noff=D_FF // 256, name="mm_dw_ffn_u", tm=1024, tn=256, tk=8192)
    tok = emit("ffn", dict(w_ffn_t=dw_ffn_t, w_down=dw_down_t.T))
    def norm2_bwd_ep(acc_g, acc_u, h_ref, g_ref, dh2_ref):
        _, vjp = jax.vjp(_rms, h_ref[...], g_ref[...])
        dx, dg = vjp(acc_g + acc_u)
        dx = dx + dh2_ref[...]
        return dx, dx, dg

    dh, dhb, g["norm2_g"] = _matmul_ep(
        [(dg2, W["w_ffn_t"], False, 0), (dup2, W["w_ffn_t"], False, 1)], tm=512, ins=[h, small["norm2_g"] + zero(tok), dh2], in_specs=[row(512), vec, row(512)],
        out_shapes=[tile(F32), tile(BF16), vec_shape], out_specs=[row(512), row(512), vec], sums=(2,), epilogue=norm2_bwd_ep, name="mm_du2_norm2_bwd")
    dw_out = _matmul(merged, dhb, ta=True, out_dtype=BF16, name="mm_dw_out", tm=1024, tn=1024, tk=2048)

    def merge_bwd_ep(acc, g_ref, pa_ref, pb_ref, dz_in):
        gt = g_ref[...].astype(F32)
        sa = _sigmoid(gt[:, :D_MODEL])
        sb = _sigmoid(gt[:, D_MODEL:])
        dgates = jnp.concatenate([acc * pa_ref[...].astype(F32) * sa * (1.0 - sa), acc * pb_ref[...].astype(F32) * sb * (1.0 - sb)], axis=1)
        return dgates, acc * sa, acc * sb

    dz, dpa, dpb = _matmul_ep(
        [(dhb, W["w_out"], True, 0)], tm=512, ins=[z, pa, pb, lax.empty((T, W_IN), BF16)], in_specs=[_gates_spec(512), row(512), row(512), ANY],
        out_shapes=[jax.ShapeDtypeStruct((T, W_IN), BF16), tile(BF16), tile(BF16)], out_specs=[_gates_spec(512), row(512), row(512)],
        aliases={3: 0}, epilogue=merge_bwd_ep, name="mm_dmerged_merge_bwd")
    doa, dob = _matmul_ep([(dpa, W["w_a"], True, 0), (dpb, W["w_b"], True, 0)], tm=1024, ins=[], in_specs=[],
                          out_shapes=[tile(BF16)] * 2, out_specs=[row(1024)] * 2, epilogue=lambda da, db: (da, db), name="mm_doa_dob")
    dw_a = _matmul(oa2, dpa, ta=True, out_dtype=BF16, name="mm_dw_a", tm=1024, tn=1024, tk=2048)
    dw_b = _matmul(ob2, dpb, ta=True, out_dtype=BF16, name="mm_dw_b", tm=1024, tn=1024, tk=2048)
    tok = emit("mix", dict(w_out=dw_out, w_a=dw_a, w_b=dw_b))
    dz3, dkv_cur, dkv_prev, dsinks = _swa_bwd(z3, qr, kr, cos, sin, small["attn_sinks"] + zero(tok), lse, dob.reshape(B, S, D),
                                              dz.reshape(B, S, W_IN), name="swa_bwd")
    dz3 = _swa_dkv_combine(dkv_cur, dkv_prev, dz3, name="swa_dkv")
    g["attn_sinks"] = dsinks
    dz3, g["lb"], g["hgrn_norm_g"] = _hgrn_bwd(z3, lb, small["hgrn_norm_g"], states, doa.reshape(B, S, D), dz3, name="hgrn_bwd")
    dz = dz3.reshape(T, W_IN)
    dw_in_t = _matmul(u1, dz, ta=True, out_t=True, o_block_perm=_reference_row_block, out_dtype=BF16, name="mm_dw_in", tm=1024, tn=256, tk=8192)
    tok = emit("in", dict(w_in_t=dw_in_t))
    dx, g["norm1_g"] = _matmul_norm_bwd(dz, w_in_t, x2, small["norm1_g"], dh, tok, tm=512, name="mm_du1_norm1_bwd")
    g["lb_logits"] = _lb_bwd(g.pop("lb"), lb, name="lb_bwd")
    return loss, dx.reshape(B, S, D), g


def _my_place():
    return lax.axis_index("x"), lax.axis_index("y"), lax.axis_index("c")


def _gather_blocks(x_ref, out_ref, send_sems, recv_sems, local_sem):
    x, y, c = _my_place()
    me, sibling = (x, y, c), (x, y, 1 - c)
    chips = [(1 - x, y), (x, 1 - y), (1 - x, 1 - y)]

    def slot(px, py, pc):
        return out_ref.at[4 * px + 2 * py + pc]

    def copy(k, block, to, src=None):
        return pltpu.make_async_remote_copy(
            src_ref=slot(*block) if src is None else src, dst_ref=slot(*block),
            send_sem=send_sems.at[k], recv_sem=recv_sems.at[k], device_id=to, device_id_type=MESH)

    mine = pltpu.make_async_copy(x_ref, slot(*me), local_sem)
    mine.start()
    first = [copy(0, me, sibling, src=x_ref)]
    first += [copy(1 + j, me, (*chip, c), src=x_ref) for j, chip in enumerate(chips)]
    for cp in first:
        cp.start()
    passed = [copy(4 + j, (*chip, c), sibling) for j, chip in enumerate(chips)]
    for j, chip in enumerate(chips):
        copy(1 + j, (*chip, c), me).wait_recv()
        passed[j].start()
    copy(0, sibling, me).wait_recv()
    for j, chip in enumerate(chips):
        copy(4 + j, (*chip, 1 - c), me).wait_recv()
    for cp in first + passed:
        cp.wait_send()
    mine.wait()


GATHER_SEMS = [pltpu.SemaphoreType.DMA((7,)), pltpu.SemaphoreType.DMA((7,)), pltpu.SemaphoreType.DMA]


def _all_gather(blk, *, name):
    return pl.pallas_call(
        _gather_body_fn(), name=name,
        out_shape=jax.ShapeDtypeStruct((N_DEV,) + blk.shape, blk.dtype),
        in_specs=[ANY], out_specs=ANY,
        scratch_shapes=GATHER_SEMS,
    )(blk)


def _gather_body_fn():
    def body(x_ref, out_ref, send_sems, recv_sems, local_sem):
        _gather_blocks(x_ref, out_ref, send_sems, recv_sems, local_sem)
    return body


SLAB_W = 1152
SMALL_SHAPES = dict(norm1_g=(1, D_MODEL), lb_logits=(2, HGRN_HEADS * HGRN_DK), hgrn_norm_g=(1, HGRN_DK), attn_sinks=(1, ATT_HEADS),
                    norm2_g=(1, D_MODEL), conv_b=(1, D_FF), final_g=(1, D_MODEL))
CONVW_BLK = D_FF // N_DEV
CONVW_STRIDE = SLAB_W // 3


def _slab_layout():
    layout, r = {}, 0
    for nm, (nr, w) in SMALL_SHAPES.items():
        layout[nm] = []
        for i in range(nr):
            for c0 in range(0, w, SLAB_W):
                layout[nm].append((r, i, c0, min(SLAB_W, w - c0)))
                r += 1
    return layout, r


SMALL_ROWS, _N_SMALL_ROWS = _slab_layout()
CONV_ROW0 = -(-_N_SMALL_ROWS // 8) * 8
LOSS_ROW = CONV_ROW0 + N_DEV
SLAB_ROWS = LOSS_ROW + 8


def _small_step(grads, g_conv_w, loss, params, moments, variances, dev, *, name):
    names = list(SMALL_ROWS)
    n = len(names)

    def body(dev_ref, *refs):
        g_refs = dict(zip(names, refs[:n]))
        gc_ref, loss_ref = refs[n], refs[n + 1]
        base = n + 2
        w_refs, m_refs, v_refs = (dict(zip(names + ["conv_w"], refs[base + i * (n + 1):base + (i + 1) * (n + 1)])) for i in range(3))
        o = base + 3 * (n + 1)
        gath_ref, loss_out = refs[o], refs[o + 1]
        outs = {nm: refs[o + 2 + 4 * i:o + 6 + 4 * i] for i, nm in enumerate(names + ["conv_w"])}
        slab, total, send_sems, recv_sems, local_sem = refs[-5:]

        slab[...] = jnp.zeros_like(slab)
        for nm, pieces in SMALL_ROWS.items():
            for r, i, c0, w in pieces:
                slab[r:r + 1, 0:w] = g_refs[nm][i:i + 1, c0:c0 + w]
        for p in range(N_DEV):
            for j in range(3):
                slab[CONV_ROW0 + p:CONV_ROW0 + p + 1, j * CONVW_STRIDE:j * CONVW_STRIDE + CONVW_BLK] = gc_ref[j:j + 1, p * CONVW_BLK:(p + 1) * CONVW_BLK]
        slab[LOSS_ROW:LOSS_ROW + 1, 0:1] = loss_ref[...]
        _gather_blocks(slab, gath_ref, send_sems, recv_sems, local_sem)
        acc = gath_ref[0]
        for p in range(1, N_DEV):
            acc = acc + gath_ref[p]
        total[...] = acc
        loss_out[...] = total[LOSS_ROW:LOSS_ROW + 1, 0:1]

        def update(nm, g, i, c0, w):
            at = (slice(i, i + 1), slice(c0, c0 + w))
            d, mn, vn = _adamw_math(w_refs[nm][at], g, m_refs[nm][at], v_refs[nm][at])
            for ref, val in zip(outs[nm], (g, d, mn, vn)):
                ref[at] = val

        for nm, pieces in SMALL_ROWS.items():
            for r, i, c0, w in pieces:
                update(nm, total[r:r + 1, 0:w], i, c0, w)
        conv_rows = total[CONV_ROW0:CONV_ROW0 + N_DEV, :]
        rowid = lax.broadcasted_iota(jnp.int32, conv_rows.shape, 0)
        mine = jnp.sum(jnp.where(rowid == dev_ref[0], conv_rows, 0.0), axis=0, keepdims=True)
        for j in range(3):
            update("conv_w", mine[:, j * CONVW_STRIDE:j * CONVW_STRIDE + CONVW_BLK], j, 0, CONVW_BLK)

    order = names + ["conv_w"]
    ins = [grads[nm] for nm in names] + [g_conv_w, loss]
    for d in (params, moments, variances):
        ins += [d[nm] for nm in order]
    vmem = pl.BlockSpec(memory_space=pltpu.VMEM)
    out_shape = [jax.ShapeDtypeStruct((N_DEV, SLAB_ROWS, SLAB_W), F32), jax.ShapeDtypeStruct((1, 1), F32)]
    for nm in order:
        out_shape += [jax.ShapeDtypeStruct(params[nm].shape, F32)] * 4
    res = pl.pallas_call(
        body, name=name,
        grid_spec=pltpu.PrefetchScalarGridSpec(
            num_scalar_prefetch=1, grid=(1,),
            in_specs=[vmem] * len(ins), out_specs=[vmem] * len(out_shape),
            scratch_shapes=[pltpu.VMEM((SLAB_ROWS, SLAB_W), F32), pltpu.VMEM((SLAB_ROWS, SLAB_W), F32)] + GATHER_SEMS),
        out_shape=out_shape,
    )(dev, *ins)
    return res[1], {nm: tuple(res[2 + 4 * i:6 + 4 * i]) for i, nm in enumerate(order)}


HBM_SPEC = pl.BlockSpec(memory_space=pltpu.HBM)
SEM_SPEC = pl.BlockSpec(memory_space=pltpu.SEMAPHORE)
DATAFLOW_EFFECT = pltpu.SideEffectType.DATAFLOW_SIDE_EFFECTING
N_PEERS = N_DEV - 1


def _peers(x, y, c):
    return [(1 - x if r & 4 else x, 1 - y if r & 2 else y, 1 - c if r & 1 else c) for r in range(1, N_DEV)]


def _exchange_start(srcs, scatter, *, after=None, name):
    n = len(srcs)
    lands = [lax.empty(a.shape if scatter else (N_DEV,) + a.shape, a.dtype) for a in srcs]
    extra = [] if after is None else [after]

    def body(*refs):
        src_refs, land_refs = refs[:n], refs[n:2 * n]
        send_sems, recv_sems, token = refs[2 * n + len(extra)], refs[2 * n + len(extra) + 1], refs[-1]
        x, y, c = _my_place()
        me = 4 * x + 2 * y + c
        for i in range(n):
            for r, (tx, ty, tc) in enumerate(_peers(x, y, c)):
                src = src_refs[i].at[4 * tx + 2 * ty + tc] if scatter else src_refs[i]
                pltpu.make_async_remote_copy(
                    src_ref=src, dst_ref=land_refs[i].at[me], send_sem=send_sems.at[N_PEERS * i + r],
                    recv_sem=recv_sems.at[N_PEERS * i + r], device_id=(tx, ty, tc), device_id_type=MESH).start()
        token[...] = jnp.zeros_like(token)

    thru = [pltpu.HBM(a.shape, a.dtype) for a in list(srcs) + lands]
    res = pl.pallas_call(
        body, name=name,
        out_shape=(pltpu.SemaphoreType.DMA((N_PEERS * n,)), pltpu.SemaphoreType.DMA((N_PEERS * n,)), *thru,
                   jax.ShapeDtypeStruct((8, 128), F32)),
        in_specs=[HBM_SPEC] * (2 * n) + [ANY] * len(extra),
        out_specs=(SEM_SPEC, SEM_SPEC, *([HBM_SPEC] * (2 * n)), pl.BlockSpec(memory_space=pltpu.VMEM)),
        input_output_aliases={i: 2 + i for i in range(2 * n)},
        compiler_params=pltpu.CompilerParams(has_side_effects=DATAFLOW_EFFECT),
    )(*[pltpu.with_memory_space_constraint(a, pltpu.HBM) for a in list(srcs) + lands], *extra)
    return (res[0], res[1], list(res[2:2 + n]), list(res[2 + n:2 + 2 * n]), scatter), res[-1]


def _exchange_wait(handle, after, *, name):
    send_sems, recv_sems, srcs, lands, scatter = handle
    n = len(srcs)

    def body(*refs):
        src_refs, land_refs = refs[:n], refs[n:2 * n]
        send_sems, recv_sems = refs[2 * n], refs[2 * n + 1]
        x, y, c = _my_place()
        for i in range(n):
            for r in range(N_PEERS):
                src = src_refs[i].at[0] if scatter else src_refs[i]
                cp = pltpu.make_async_remote_copy(
                    src_ref=src, dst_ref=land_refs[i].at[0], send_sem=send_sems.at[N_PEERS * i + r],
                    recv_sem=recv_sems.at[N_PEERS * i + r], device_id=(x, y, c), device_id_type=MESH)
                cp.wait_send()
                cp.wait_recv()

    thru = [pltpu.HBM(a.shape, a.dtype) for a in srcs + lands]
    res = pl.pallas_call(
        body, name=name, out_shape=tuple(thru),
        in_specs=[HBM_SPEC] * (2 * n) + [SEM_SPEC, SEM_SPEC, ANY], out_specs=tuple([HBM_SPEC] * (2 * n)),
        input_output_aliases={i: i for i in range(2 * n)},
        compiler_params=pltpu.CompilerParams(has_side_effects=DATAFLOW_EFFECT),
    )(*srcs, *lands, send_sems, recv_sems, after)
    return list(res[:n]), list(res[n:])


def _with_own(land, own, me):
    return lax.dynamic_update_index_in_dim(land, own, me, 0)


def _adamw_math(w, g, m, v):
    m = ADAM_B1 * m + (1.0 - ADAM_B1) * g
    v = ADAM_B2 * v + (1.0 - ADAM_B2) * (g * g)
    m_hat = m / (1.0 - ADAM_B1 ** ADAM_STEP)
    v_hat = v / (1.0 - ADAM_B2 ** ADAM_STEP)
    delta = -ADAM_LR * (m_hat / (jnp.sqrt(v_hat) + ADAM_EPS) + ADAM_WD * w)
    return delta, m, v


def _adamw_sum(parts, w, m, v, *, name):
    shape = w.shape
    R, n = shape[-2], shape[-1]
    w, m, v = (t.reshape(R, n) for t in (w, m, v))
    tr = _pick(R, (256, 464, 352, 128))

    def body(p_ref, w_ref, m_ref, v_ref, g_ref, d_ref, mo_ref, vo_ref):
        g = p_ref[0].astype(F32)
        for p in range(1, N_DEV):
            g = g + p_ref[p].astype(F32)
        d, mn, vn = _adamw_math(w_ref[...], g, m_ref[...], v_ref[...])
        g_ref[...] = g
        d_ref[...] = d
        mo_ref[...] = mn
        vo_ref[...] = vn

    row = pl.BlockSpec((tr, n), lambda i: (i, 0))
    outs = pl.pallas_call(
        body, name=name, grid=(R // tr,),
        in_specs=[pl.BlockSpec((N_DEV, tr, n), lambda i: (0, i, 0)), row, row, row],
        out_specs=[row, row, row, row],
        out_shape=[jax.ShapeDtypeStruct((R, n), F32)] * 4,
        compiler_params=_params("parallel"),
    )(parts, w, m, v)
    return [t.reshape(shape) for t in outs]


def _lb_bwd(dlb, lb, *, name):
    def body(d_ref, lb_ref, o_ref):
        t = d_ref[...] * lb_ref[...] * (1.0 - lb_ref[...])
        o_ref[0:1, :] = t
        o_ref[1:2, :] = -t

    return pl.pallas_call(body, name=name, out_shape=jax.ShapeDtypeStruct((2, lb.shape[1]), F32))(dlb, lb)


DOWN_BLK, ROW_BLK = D_FF // N_DEV, D_MODEL // N_DEV
W_FFN_BLK = 2 * D_FF // N_DEV
CONV_BITS_SHAPE = (16, 256)


def kernel(x, positions, norm1_g, w_in, lb_logits, hgrn_norm_g, w_a, attn_sinks, w_b, w_out, norm2_g, w_ffn_in, conv_w, conv_b, w_down, final_g, loss_target, m_norm1_g, m_w_in, m_lb_logits, m_hgrn_norm_g, m_w_a, m_attn_sinks, m_w_b, m_w_out, m_norm2_g, m_w_ffn_in, m_conv_w, m_conv_b, m_w_down, m_final_g, v_norm1_g, v_w_in, v_lb_logits, v_hgrn_norm_g, v_w_a, v_attn_sinks, v_w_b, v_w_out, v_norm2_g, v_w_ffn_in, v_conv_w, v_conv_b, v_w_down, v_final_g):
    xi, yi, ci = _my_place()
    dev = 4 * xi + 2 * yi + ci

    tr = lambda t: jnp.transpose(t[0])
    untr = lambda t: jnp.transpose(t)[None]
    w_in_blocks = _all_gather(tr(w_in).astype(BF16), name="ag_w_in")
    conv_bits = lax.bitcast_convert_type(conv_w, BF16).reshape(-1)
    conv_bits = jnp.pad(conv_bits, (0, CONV_BITS_SHAPE[0] * CONV_BITS_SHAPE[1] - conv_bits.shape[0])).reshape(CONV_BITS_SHAPE)
    w_in_full_t = _reordered_rows(w_in_blocks.reshape(W_IN, D_MODEL), name="w_in_rows")
    gather_handles = {}
    gather_handles["mix"], tok_mix = _exchange_start([w_a[0].astype(BF16), w_b[0].astype(BF16), w_out[0].astype(BF16)], False,
                                                     after=w_in_full_t, name="ag_mix_start")
    gather_handles["ffn"], tok_ffn = _exchange_start([tr(w_ffn_in).astype(BF16), w_down[0].astype(BF16), conv_bits], False,
                                                     after=tok_mix, name="ag_ffn_start")
    start_token = tok_mix + tok_ffn

    def rest_weights(group, after):
        own, lands = _exchange_wait(gather_handles[group], after, name="ag_" + group + "_wait")
        full = [_with_own(l, o, dev) for l, o in zip(lands, own)]
        if group == "mix":
            return dict(zip(("w_a", "w_b", "w_out"), [t.reshape(D_MODEL, D_MODEL) for t in full]))
        bits = full[2].reshape(N_DEV, -1)[:, :3 * CONVW_BLK * 2].reshape(N_DEV, 3, CONVW_BLK, 2)
        return dict(w_ffn_t=full[0].reshape(2 * D_FF, D_MODEL), w_down=full[1].reshape(D_FF, D_MODEL),
                    conv_w=lax.bitcast_convert_type(bits, F32).transpose(1, 0, 2).reshape(3, D_FF))

    handles = {}

    def emit(group, gr):
        if group == "ffn":
            srcs = [gr["w_ffn_t"].reshape(N_DEV, W_FFN_BLK, D_MODEL), gr["w_down"].reshape(N_DEV, DOWN_BLK, D_MODEL)]
        elif group == "mix":
            srcs = [gr[n].reshape(N_DEV, ROW_BLK, D_MODEL) for n in ("w_out", "w_a", "w_b")]
        else:
            srcs = [gr["w_in_t"].reshape(N_DEV, W_IN_BLK, D_MODEL)]
        handles[group], token = _exchange_start(srcs, True, name="rs_" + group + "_start")
        return token

    small = dict(norm1_g=norm1_g, lb_logits=lb_logits, hgrn_norm_g=hgrn_norm_g, attn_sinks=attn_sinks, norm2_g=norm2_g,
                 conv_b=conv_b, final_g=final_g)
    loss, grad_x, g = _local_step(x, positions, loss_target, small, w_in_full_t, rest_weights, emit, start_token)

    def parts_of(group, after):
        srcs, lands = _exchange_wait(handles[group], after, name="rs_" + group + "_wait")
        return [_with_own(l, lax.dynamic_index_in_dim(s, dev, 0, keepdims=False), dev) for s, l in zip(srcs, lands)]

    p_ffn, p_down = parts_of("ffn", grad_x)
    p_out, p_a, p_b = parts_of("mix", grad_x)
    (p_in,) = parts_of("in", grad_x)
    big = dict(
        w_in=[untr(t) for t in _adamw_sum(p_in, tr(w_in), tr(m_w_in), tr(v_w_in), name="adamw_w_in")],
        w_a=_adamw_sum(p_a, w_a, m_w_a, v_w_a, name="adamw_w_a"),
        w_b=_adamw_sum(p_b, w_b, m_w_b, v_w_b, name="adamw_w_b"),
        w_out=_adamw_sum(p_out, w_out, m_w_out, v_w_out, name="adamw_w_out"),
        w_ffn_in=[untr(t) for t in _adamw_sum(p_ffn, tr(w_ffn_in), tr(m_w_ffn_in), tr(v_w_ffn_in), name="adamw_w_ffn_in")],
        w_down=_adamw_sum(p_down, w_down, m_w_down, v_w_down, name="adamw_w_down"),
    )

    row = lambda t: t.reshape(1, -1) if t.ndim == 1 else t
    shard = lambda t: t.reshape(3, CONVW_BLK)
    sm_g = {nm: g[nm] for nm in SMALL_ROWS}
    sm_w = dict(norm1_g=norm1_g, lb_logits=lb_logits, hgrn_norm_g=hgrn_norm_g, attn_sinks=attn_sinks, norm2_g=norm2_g,
                conv_b=conv_b, final_g=row(final_g), conv_w=shard(conv_w))
    sm_m = dict(norm1_g=m_norm1_g, lb_logits=m_lb_logits, hgrn_norm_g=m_hgrn_norm_g, attn_sinks=m_attn_sinks, norm2_g=m_norm2_g,
                conv_b=m_conv_b, final_g=row(m_final_g), conv_w=shard(m_conv_w))
    sm_v = dict(norm1_g=v_norm1_g, lb_logits=v_lb_logits, hgrn_norm_g=v_hgrn_norm_g, attn_sinks=v_attn_sinks, norm2_g=v_norm2_g,
                conv_b=v_conv_b, final_g=row(v_final_g), conv_w=shard(v_conv_w))
    loss_total, sm_out = _small_step(sm_g, g["conv_w"], loss, sm_w, sm_m, sm_v, dev.astype(jnp.int32).reshape(1), name="small_step")
    shapes = dict(final_g=final_g.shape, conv_w=conv_w.shape)

    names = ("norm1_g", "w_in", "lb_logits", "hgrn_norm_g", "w_a", "attn_sinks", "w_b", "w_out", "norm2_g", "w_ffn_in", "conv_w", "conv_b", "w_down", "final_g")
    outs = [loss_total.reshape(()), grad_x]
    for kind in range(4):
        outs += [big[n][kind] if n in big else sm_out[n][kind].reshape(shapes.get(n, sm_out[n][kind].shape)) for n in names]
    return tuple(outs)
```

```python
import jax
import jax.numpy as jnp
from jax import lax
from jax.experimental import pallas as pl
from jax.experimental.pallas import tpu as pltpu

F32 = jnp.float32
BF16 = jnp.bfloat16

D_MODEL = 1024
HGRN_HEADS = 8
HGRN_DK = 128
CHUNK = 64
ATT_HEADS = 16
ATT_KV_HEADS = 2
ATT_HD = 64
ATT_GROUP = ATT_HEADS // ATT_KV_HEADS
WINDOW = 128
ROPE_DIM = ATT_HD // 4
ROPE_THETA = 500000.0
D_FF = 2816
EPS = 1e-6
NEG_INF = -1e30
N_DEV = 8

ADAM_LR = 0.001
ADAM_B1 = 0.9
ADAM_B2 = 0.999
ADAM_EPS = 1e-08
ADAM_WD = 0.01
ADAM_STEP = 10

MESH = pl.DeviceIdType.MESH
ANY = pl.BlockSpec(memory_space=pl.ANY)


def _pick(n, cands):
    for c in cands:
        if n % c == 0:
            return c
    return n


def _sigmoid(x):
    return 0.5 * jnp.tanh(0.5 * x) + 0.5


def _silu(x):
    hx = 0.5 * x
    return hx * jnp.tanh(hx) + hx


def _rms(x, g):
    return x * lax.rsqrt(jnp.mean(x * x, axis=-1, keepdims=True) + EPS) * g


def _dot(a, b, dims):
    return lax.dot_general(a, b, (dims, ((), ())), preferred_element_type=F32)


def _nn(a, b):
    return _dot(a, b, ((1,), (0,)))


def _nt(a, b):
    return _dot(a, b, ((1,), (1,)))


def _tn(a, b):
    return _dot(a, b, ((0,), (0,)))


def _params(*sem):
    return pltpu.CompilerParams(dimension_semantics=sem, vmem_limit_bytes=56 * 1024 * 1024)


def _matmul(a, b, *, ta=False, tb=False, out_dtype=F32, addend=None, after=None, into=None, o_noff=0, out_t=False,
            o_block_perm=lambda j: j, name, tm, tn, tk=None, n_extent=None, b_koff=0, b_noff=0):
    M, K = (a.shape[1], a.shape[0]) if ta else a.shape
    N = n_extent or (b.shape[0] if tb else b.shape[1])
    tm, tn, tk = min(tm, M), min(tn, N), min(tk or K, K)
    assert M % tm == 0 and N % tn == 0 and K % tk == 0, (name, M, N, K, tm, tn, tk)
    nk = K // tk
    use_scratch = nk > 1 and out_dtype != F32
    grid = (M // tm, N // tn, nk)
    a_spec = pl.BlockSpec((tk, tm), lambda i, j, k: (k, i)) if ta else pl.BlockSpec((tm, tk), lambda i, j, k: (i, k))
    b_spec = pl.BlockSpec((tn, tk), lambda i, j, k: (j + b_noff, k + b_koff)) if tb else pl.BlockSpec((tk, tn), lambda i, j, k: (k + b_koff, j + b_noff))
    o_spec = pl.BlockSpec((tm, tn), lambda i, j, k: (i, j))
    dims = ((0 if ta else 1,), (1 if tb else 0,))
    has_add = addend is not None

    n_in = 2 + has_add + (after is not None) + (into is not None)

    def body(*refs):
        a_ref, b_ref = refs[:2]
        c_ref = refs[2] if has_add else None
        o_ref = refs[n_in]
        part = _dot(a_ref[...], b_ref[...], dims)
        if nk == 1:
            if has_add:
                part = part + c_ref[...].astype(F32)
            o_ref[...] = (part.T if out_t else part).astype(out_dtype)
        else:
            acc_ref = refs[-1] if use_scratch else o_ref
            k = pl.program_id(2)

            @pl.when(k == 0)
            def _():
                acc_ref[...] = part + c_ref[...].astype(F32) if has_add else part

            @pl.when(k > 0)
            def _():
                acc_ref[...] += part

            if use_scratch:
                @pl.when(k == nk - 1)
                def _():
                    o_ref[...] = acc_ref[...].astype(out_dtype)

    in_specs = [a_spec, b_spec] + ([o_spec] if has_add else [])
    args = (a, b) + ((addend,) if has_add else ())
    if after is not None:
        in_specs.append(pl.BlockSpec(after.shape, lambda i, j, k: (0, 0)))
        args += (after,)
    aliases = {}
    if into is not None:
        in_specs.append(ANY)
        args += (into,)
        aliases = {len(args) - 1: 0}
    if out_t:
        assert nk == 1 and not has_add
        o_spec = pl.BlockSpec((tn, tm), lambda i, j, k: (o_block_perm(j) + o_noff, i))
    elif into is not None:
        o_spec = pl.BlockSpec((tm, tn), lambda i, j, k: (i, j + o_noff))
    return pl.pallas_call(
        body,
        name=name,
        grid=grid,
        in_specs=in_specs,
        out_specs=o_spec,
        out_shape=jax.ShapeDtypeStruct(into.shape if into is not None else ((N, M) if out_t else (M, N)), out_dtype),
        input_output_aliases=aliases,
        scratch_shapes=[pltpu.VMEM((tm, tn), F32)] if use_scratch else [],
        compiler_params=_params("parallel", "parallel", "arbitrary"),
    )(*args)


def _matmul_col_tiles(a, b_t, *, tm, tn, tc, name):
    M, K = a.shape
    N = b_t.shape[0]
    tm = min(tm, M)
    per_step = tn // tc

    def body(a_ref, b_ref, o_ref):
        res = _nt(a_ref[...], b_ref[...]).astype(BF16)
        for t in range(per_step):
            o_ref[t] = res[:, t * tc:(t + 1) * tc]

    return pl.pallas_call(
        body, name=name, grid=(M // tm, N // tn),
        in_specs=[pl.BlockSpec((tm, K), lambda i, j: (i, 0)), pl.BlockSpec((tn, K), lambda i, j: (j, 0))],
        out_specs=pl.BlockSpec((per_step, tm, tc), lambda i, j: (j, i, 0)),
        out_shape=jax.ShapeDtypeStruct((N // tc, M, tc), BF16),
        compiler_params=_params("parallel", "parallel"),
    )(a, b_t)


def _matmul_ep(pairs, *, tm, ins, in_specs, out_shapes, out_specs, sums=(), epilogue, aliases=None, name):
    M = pairs[0][0].shape[0]
    tm = min(tm, M)
    mm_specs, mm_args, dims = [], [], []
    for a, b, tb, koff in pairs:
        K = a.shape[1]
        N = b.shape[0] if tb else b.shape[1]
        mm_specs += [pl.BlockSpec((tm, K), lambda i: (i, 0)),
                     pl.BlockSpec((N, K), lambda i, koff=koff: (0, koff)) if tb else pl.BlockSpec((K, N), lambda i, koff=koff: (koff, 0))]
        mm_args += [a, b]
        dims.append(((1,), (1 if tb else 0,)))
    n_mm = len(mm_args)
    n_in = n_mm + len(ins)

    def body(*refs):
        in_refs, out_refs = refs[n_mm:n_in], refs[n_in:]
        accs = [_dot(refs[2 * p][...], refs[2 * p + 1][...], dims[p]) for p in range(len(pairs))]
        outs = epilogue(*accs, *in_refs)
        for k, (ref, val) in enumerate(zip(out_refs, outs)):
            if val is None:
                continue
            if k in sums:
                @pl.when(pl.program_id(0) == 0)
                def _():
                    ref[...] = jnp.zeros_like(ref)

                ref[...] += val
            else:
                ref[...] = val.astype(ref.dtype)

    return pl.pallas_call(
        body, name=name, grid=(M // tm,),
        in_specs=mm_specs + list(in_specs),
        out_specs=list(out_specs), out_shape=list(out_shapes),
        input_output_aliases={n_mm + k: v for k, v in (aliases or {}).items()},
        compiler_params=_params("arbitrary"),
    )(*mm_args, *ins)


def _row_spec(tm, n):
    return pl.BlockSpec((tm, n), lambda i: (i, 0))


def _full_spec(shape):
    return pl.BlockSpec(shape, lambda i: tuple(0 for _ in shape))


MAX_DOT_COLS = 2048


def _resident_spec(shape):
    return pl.BlockSpec(shape, lambda i: tuple(0 for _ in shape), pipeline_mode=pl.Buffered(1))


def _norm_matmul(x, g, w_t, segments, *, tm, name):
    T, D = x.shape
    N = w_t.shape[0]
    tm = min(tm, T)
    chunks = [(c + o, r + o, min(MAX_DOT_COLS, n - o)) for c, r, n in segments for o in range(0, n, MAX_DOT_COLS)]

    def body(x_ref, g_ref, w_ref, u_ref, z_ref):
        u = _rms(x_ref[...], g_ref[...]).astype(BF16)
        u_ref[...] = u
        for c, r, n in chunks:
            z_ref[:, c:c + n] = _nt(u, w_ref[r:r + n, :]).astype(BF16)

    return pl.pallas_call(
        body, name=name, grid=(T // tm,),
        in_specs=[_row_spec(tm, D), _full_spec((1, D)), _resident_spec((N, D))],
        out_specs=[_row_spec(tm, D), _row_spec(tm, N)],
        out_shape=[jax.ShapeDtypeStruct((T, D), BF16), jax.ShapeDtypeStruct((T, N), BF16)],
        compiler_params=_params("parallel"),
    )(x, g, w_t)


def _matmul_norm_bwd(dz, w_t, segments, x, g, dres, after, *, tm, name):
    T, K = dz.shape
    D = w_t.shape[1]
    tm = min(tm, T)

    def body(dz_ref, w_ref, x_ref, g_ref, dr_ref, after_ref, dx_ref, dg_ref):
        @pl.when(pl.program_id(0) == 0)
        def _():
            dg_ref[...] = jnp.zeros_like(dg_ref)

        du = sum(_nn(dz_ref[:, c:c + n], w_ref[r:r + n, :]) for c, r, n in segments)
        _, vjp = jax.vjp(_rms, x_ref[...], g_ref[...])
        dx, dg = vjp(du)
        dx_ref[...] = dx + dr_ref[...]
        dg_ref[...] += dg

    row = _row_spec(tm, D)
    return pl.pallas_call(
        body, name=name, grid=(T // tm,),
        in_specs=[_row_spec(tm, K), _resident_spec((K, D)), row, _full_spec((1, D)), row, _full_spec(after.shape)],
        out_specs=[row, _full_spec((1, D))],
        out_shape=[jax.ShapeDtypeStruct((T, D), F32), jax.ShapeDtypeStruct((1, D), F32)],
        compiler_params=_params("arbitrary"),
    )(dz, w_t, x, g, dres, after)


def _norm_bwd_add(x, g, du, dres, *, with_bf16=True, name):
    T, D = x.shape
    tm = _pick(T, (512, 256, 128))

    def body(x_ref, g_ref, du_ref, dr_ref, dx_ref, *rest):
        dg_ref = rest[-1]
        _, vjp = jax.vjp(_rms, x_ref[...], g_ref[...])
        dx, dg = vjp(du_ref[...].astype(F32))
        dx = dx + dr_ref[...]
        dx_ref[...] = dx
        if with_bf16:
            rest[0][...] = dx.astype(BF16)

        @pl.when(pl.program_id(0) == 0)
        def _():
            dg_ref[...] = jnp.zeros_like(dg_ref)

        dg_ref[...] += dg

    row = _row_spec(tm, D)
    return pl.pallas_call(
        body, name=name, grid=(T // tm,),
        in_specs=[row, _full_spec((1, D)), row, row],
        out_specs=[row] + ([row] if with_bf16 else []) + [_full_spec((1, D))],
        out_shape=[jax.ShapeDtypeStruct((T, D), F32)] + ([jax.ShapeDtypeStruct((T, D), BF16)] if with_bf16 else []) + [jax.ShapeDtypeStruct((1, D), F32)],
        compiler_params=_params("arbitrary"),
    )(x, g, du, dres)


def _merge_fn(gates, a, b):
    ga = gates[:, :D_MODEL].astype(F32)
    gb = gates[:, D_MODEL:].astype(F32)
    return _sigmoid(ga) * a.astype(F32) + _sigmoid(gb) * b.astype(F32)


def _gates_spec(tm):
    return pl.BlockSpec((tm, W_GATES), lambda i: (i, O_GATES // W_GATES))


CONV_TC = 256


def _shift_down(x, n, rows):
    return jnp.where(rows >= n, pltpu.roll(x, n, 0), 0.0)


def _shift_up(x, n, rows, S):
    return jnp.where(rows < S - n, pltpu.roll(x, S - n, 0), 0.0)


def _conv_act_fwd(gu, conv_w, conv_b, *, name):
    _, B, S, tc = gu.shape
    nc = D_FF // tc

    def body(g_ref, up_ref, w_ref, b_ref, o_ref, a_ref):
        g = g_ref[...].astype(F32)
        rows = lax.broadcasted_iota(jnp.int32, g.shape, 0)
        w = w_ref[...]
        a = w[2:3] * g + w[1:2] * _shift_down(g, 1, rows) + w[0:1] * _shift_down(g, 2, rows) + b_ref[...]
        o_ref[...] = (_silu(a) * up_ref[...].astype(F32)).astype(BF16)
        a_ref[...] = a.astype(BF16)

    col = pl.BlockSpec((None, S, tc), lambda b, j: (b, 0, j))
    tile = lambda off: pl.BlockSpec((None, None, S, tc), lambda b, j: (j + off, b, 0, 0))
    return pl.pallas_call(
        body, name=name, grid=(B, nc),
        in_specs=[tile(0), tile(nc),
                  pl.BlockSpec((3, tc), lambda b, j: (0, j)),
                  pl.BlockSpec((1, tc), lambda b, j: (0, j))],
        out_specs=[col, tile(0)],
        out_shape=[jax.ShapeDtypeStruct((B, S, D_FF), BF16), jax.ShapeDtypeStruct((nc, B, S, tc), BF16)],
        compiler_params=_params("parallel", "parallel"),
    )(gu, gu, conv_w, conv_b)


def _conv_act_bwd(gu, a_pre, conv_w, dact, *, name):
    _, B, S, tc = gu.shape
    nc = D_FF // tc

    def body(g_ref, up_ref, a_ref, w_ref, da_ref, dg_ref, dup_ref, dw_ref, db_ref):
        g = g_ref[...].astype(F32)
        up, a, dact = up_ref[...], a_ref[...], da_ref[...]
        rows = lax.broadcasted_iota(jnp.int32, g.shape, 0)
        w = w_ref[...]
        sg = _sigmoid(a)
        dup_ref[...] = dact * a * sg
        da = (dact * up * sg * (1.0 + a * (1.0 - sg))).astype(F32)
        da1 = _shift_up(da, 1, rows, S)
        da2 = _shift_up(da, 2, rows, S)
        dg_ref[...] = (w[2:3] * da + w[1:2] * da1 + w[0:1] * da2).astype(BF16)

        @pl.when(pl.program_id(1) == 0)
        def _():
            dw_ref[...] = jnp.zeros_like(dw_ref)
            db_ref[...] = jnp.zeros_like(db_ref)

        dw_ref[0:1, :] += jnp.sum(da2 * g, axis=0, keepdims=True)
        dw_ref[1:2, :] += jnp.sum(da1 * g, axis=0, keepdims=True)
        dw_ref[2:3, :] += jnp.sum(da * g, axis=0, keepdims=True)
        db_ref[...] += jnp.sum(da, axis=0, keepdims=True)

    col = pl.BlockSpec((None, S, tc), lambda j, b: (b, 0, j))
    tile = lambda off: pl.BlockSpec((None, None, S, tc), lambda j, b: (j + off, b, 0, 0))
    return pl.pallas_call(
        body, name=name, grid=(nc, B),
        in_specs=[tile(0), tile(nc), tile(0),
                  pl.BlockSpec((3, tc), lambda j, b: (0, j)),
                  col],
        out_specs=[col, col, pl.BlockSpec((3, tc), lambda j, b: (0, j)), pl.BlockSpec((1, tc), lambda j, b: (0, j))],
        out_shape=[jax.ShapeDtypeStruct((B, S, D_FF), BF16), jax.ShapeDtypeStruct((B, S, D_FF), BF16),
                   jax.ShapeDtypeStruct((3, D_FF), F32), jax.ShapeDtypeStruct((1, D_FF), F32)],
        compiler_params=_params("parallel", "arbitrary"),
    )(gu, gu, a_pre, conv_w, dact)


HGRN_CPB = 8
HF = HGRN_HEADS * HGRN_DK


def _tri(n, upper=False):
    r = lax.broadcasted_iota(jnp.int32, (n, n), 0)
    c = lax.broadcasted_iota(jnp.int32, (n, n), 1)
    return (c >= r) if upper else (r >= c)


def _hs(h):
    return slice(h * HGRN_DK, (h + 1) * HGRN_DK)


def _cumsum_rows(tri_b, x):
    hi = x.astype(BF16)
    lo = (x - hi.astype(F32)).astype(BF16)
    return _nn(tri_b, hi) + _nn(tri_b, lo)


def _hgrn_col(seg, h):
    return slice(seg * HF + h * HGRN_DK, seg * HF + (h + 1) * HGRN_DK)


def _hgrn_gates(q, fz, lb):
    sg = _sigmoid(fz)
    return _sigmoid(q), sg, lb + (1.0 - lb) * sg


def _hgrn_decays(b, q, sq, f):
    qf = q * sq
    k = 1.0 - f
    bref = b[CHUNK // 2:CHUNK // 2 + 1, :]
    blast = b[CHUNK - 1:CHUNK, :]
    e1 = jnp.exp2(b - bref)
    e2 = jnp.exp2(bref - b)
    e3 = e1 * jnp.exp2(bref)
    e4 = e2 * jnp.exp2(blast - bref)
    return (e1, e2, e3, e4), qf * e1, k * e2, qf * e3, k * e4, jnp.exp2(blast)


def _hgrn_fwd(zh, lb, gn, *, name):
    B, S, _ = zh.shape
    cpb = HGRN_CPB
    ts = cpb * CHUNK
    nblk = S // ts

    def body(z_ref, lb_ref, gn_ref, o_ref, st_ref, state):
        @pl.when(pl.program_id(1) == 0)
        def _():
            state[...] = jnp.zeros_like(state)

        R = range(HGRN_HEADS)
        causal = _tri(CHUNK)
        tril_b = causal.astype(BF16)
        lbh = [lb_ref[:, _hs(h)] for h in R]
        for c in range(cpb):
            rows = slice(c * CHUNK, (c + 1) * CHUNK)
            q = [z_ref[rows, _hgrn_col(0, h)].astype(F32) for h in R]
            gates = [_hgrn_gates(q[h], z_ref[rows, _hgrn_col(1, h)].astype(F32), lbh[h]) for h in R]
            b = [_cumsum_rows(tril_b, jnp.log2(gates[h][2])) for h in R]
            v = [z_ref[rows, _hgrn_col(2, h)] for h in R]
            dec, q_in, k_in, q_out, k_st = [], [], [], [], []
            for h in R:
                _, qi, ki, qo, ks, d = _hgrn_decays(b[h], q[h], gates[h][0], gates[h][2])
                dec.append(d)
                for lst, t in zip((q_in, k_in, q_out, k_st), (qi, ki, qo, ks)):
                    lst.append(t.astype(BF16))
            a = [jnp.where(causal, _nt(q_in[h], k_in[h]), 0.0).astype(BF16) for h in R]
            st = [state[h] for h in R]
            for h in R:
                st_ref[c, h] = st[h]
            o = [_nn(a[h], v[h]) + _nt(q_out[h], st[h].astype(BF16)) for h in R]
            for h in R:
                state[h] = st[h] * dec[h] + _tn(v[h], k_st[h])
            for h in R:
                o_ref[rows, _hs(h)] = (_rms(o[h], gn_ref[...]) * _silu(z_ref[rows, _hgrn_col(3, h)].astype(F32))).astype(BF16)

    return pl.pallas_call(
        body, name=name, grid=(B, nblk),
        in_specs=[pl.BlockSpec((None, ts, 4 * HF), lambda b, s: (b, s, 0)),
                  pl.BlockSpec((1, HF), lambda b, s: (0, 0)),
                  pl.BlockSpec((1, HGRN_DK), lambda b, s: (0, 0))],
        out_specs=[pl.BlockSpec((None, ts, HF), lambda b, s: (b, s, 0)),
                   pl.BlockSpec((None, cpb, HGRN_HEADS, HGRN_DK, HGRN_DK), lambda b, s: (b, s, 0, 0, 0))],
        out_shape=[jax.ShapeDtypeStruct((B, S, HF), BF16),
                   jax.ShapeDtypeStruct((B, S // CHUNK, HGRN_HEADS, HGRN_DK, HGRN_DK), F32)],
        scratch_shapes=[pltpu.VMEM((HGRN_HEADS, HGRN_DK, HGRN_DK), F32)],
        compiler_params=_params("arbitrary", "arbitrary"),
    )(zh, lb, gn)


def _hgrn_bwd(zh, lb, gn, states, doa, dz, *, name):
    B, S, _ = zh.shape
    cpb = HGRN_CPB
    ts = cpb * CHUNK
    nblk = S // ts
    rev = lambda b, s: (b, nblk - 1 - s, 0)

    def body(z_ref, lb_ref, gn_ref, st_ref, do_ref, dz_in, dz_ref, dlb_ref, dgn_ref, dstate):
        @pl.when(pl.program_id(1) == 0)
        def _():
            dstate[...] = jnp.zeros_like(dstate)

        @pl.when((pl.program_id(0) == 0) & (pl.program_id(1) == 0))
        def _():
            dlb_ref[...] = jnp.zeros_like(dlb_ref)
            dgn_ref[...] = jnp.zeros_like(dgn_ref)

        R = range(HGRN_HEADS)
        causal = _tri(CHUNK)
        tril_b = causal.astype(BF16)
        triu_b = _tri(CHUNK, upper=True).astype(BF16)
        rowid = lax.broadcasted_iota(jnp.int32, (CHUNK, HGRN_DK), 0)
        lbh = [lb_ref[:, _hs(h)] for h in R]
        gn = gn_ref[...]
        for c in reversed(range(cpb)):
            rows = slice(c * CHUNK, (c + 1) * CHUNK)
            q = [z_ref[rows, _hgrn_col(0, h)].astype(F32) for h in R]
            gates = [_hgrn_gates(q[h], z_ref[rows, _hgrn_col(1, h)].astype(F32), lbh[h]) for h in R]
            b = [_cumsum_rows(tril_b, jnp.log2(gates[h][2])) for h in R]
            v = [z_ref[rows, _hgrn_col(2, h)] for h in R]
            pre = [_hgrn_decays(b[h], q[h], gates[h][0], gates[h][2]) for h in R]
            q_in_b, k_in_b, q_out_b, k_st_b = ([pre[h][i].astype(BF16) for h in R] for i in (1, 2, 3, 4))
            a_b = [jnp.where(causal, _nt(q_in_b[h], k_in_b[h]), 0.0).astype(BF16) for h in R]
            st = [st_ref[c, h] for h in R]
            st_b = [t.astype(BF16) for t in st]
            o = [_nn(a_b[h], v[h]) + _nt(q_out_b[h], st_b[h]) for h in R]
            do_l, dgn_acc = [], jnp.zeros_like(gn)
            for h in R:
                hg = z_ref[rows, _hgrn_col(3, h)].astype(F32)
                dout = do_ref[rows, _hs(h)].astype(F32)
                shg = _sigmoid(hg)
                on_h, norm_vjp = jax.vjp(_rms, o[h], gn)
                d_o, d_gn = norm_vjp(dout * (hg * shg))
                do_l.append(d_o)
                dgn_acc = dgn_acc + d_gn
                dz_ref[rows, _hgrn_col(3, h)] = (dout * on_h * shg * (1.0 + hg * (1.0 - shg))).astype(BF16)
            dgn_ref[...] += dgn_acc
            do_b = [t.astype(BF16) for t in do_l]
            dst = [dstate[h] for h in R]
            dst_b = [t.astype(BF16) for t in dst]
            da_b = [jnp.where(causal, _nt(do_b[h], v[h]), 0.0).astype(BF16) for h in R]
            dv = [_tn(a_b[h], do_b[h]) + _nt(k_st_b[h], dst_b[h]) for h in R]
            dq_in = [_nn(da_b[h], k_in_b[h]) for h in R]
            dk_in = [_tn(da_b[h], q_in_b[h]) for h in R]
            dq_out = [_nn(do_b[h], st_b[h]) for h in R]
            dk_st = [_nn(v[h], dst_b[h]) for h in R]
            for h in R:
                dz_ref[rows, _hgrn_col(2, h)] = dv[h].astype(BF16)
            db = []
            for h in R:
                _, q_in, k_in, q_out, k_st, dec = pre[h]
                ddec = jnp.sum(st[h] * dst[h], axis=0, keepdims=True)
                t_qin, t_kin, t_kst = dq_in[h] * q_in, dk_in[h] * k_in, dk_st[h] * k_st
                dbref = jnp.sum(t_kin - t_qin, axis=0, keepdims=True)
                dblast = jnp.sum(t_kst, axis=0, keepdims=True) + ddec * dec
                db.append(t_qin - t_kin + dq_out[h] * q_out - t_kst
                          + jnp.where(rowid == CHUNK // 2, dbref, 0.0) + jnp.where(rowid == CHUNK - 1, dblast, 0.0))
            for h in R:
                dstate[h] = dst[h] * pre[h][5] + _tn(do_b[h], q_out_b[h])
            dlogf = [_cumsum_rows(triu_b, db[h]) for h in R]
            for h in R:
                sq, sg, f = gates[h]
                e1, e2, e3, e4 = pre[h][0]
                dqf = dq_in[h] * e1 + dq_out[h] * e3
                dk = dk_in[h] * e2 + dk_st[h] * e4
                df_open = (dlogf[h] / f - dk) * (1.0 - sg)
                dlb_ref[:, _hs(h)] += jnp.sum(df_open, axis=0, keepdims=True)
                dz_ref[rows, _hgrn_col(1, h)] = (df_open * ((1.0 - lbh[h]) * sg)).astype(BF16)
                dz_ref[rows, _hgrn_col(0, h)] = (dqf * sq * (1.0 + q[h] * (1.0 - sq))).astype(BF16)

    return pl.pallas_call(
        body, name=name, grid=(B, nblk),
        in_specs=[pl.BlockSpec((None, ts, 4 * HF), rev),
                  pl.BlockSpec((1, HF), lambda b, s: (0, 0)),
                  pl.BlockSpec((1, HGRN_DK), lambda b, s: (0, 0)),
                  pl.BlockSpec((None, cpb, HGRN_HEADS, HGRN_DK, HGRN_DK), lambda b, s: (b, nblk - 1 - s, 0, 0, 0)),
                  pl.BlockSpec((None, ts, HF), rev),
                  ANY],
        out_specs=[pl.BlockSpec((None, ts, 4 * HF), rev),
                   pl.BlockSpec((1, HF), lambda b, s: (0, 0)),
                   pl.BlockSpec((1, HGRN_DK), lambda b, s: (0, 0))],
        out_shape=[jax.ShapeDtypeStruct(dz.shape, BF16),
                   jax.ShapeDtypeStruct((1, HF), F32),
                   jax.ShapeDtypeStruct((1, HGRN_DK), F32)],
        input_output_aliases={5: 0},
        scratch_shapes=[pltpu.VMEM((HGRN_HEADS, HGRN_DK, HGRN_DK), F32)],
        compiler_params=_params("arbitrary", "arbitrary"),
    )(zh, lb, gn, states, doa, dz)


KV_W = ATT_KV_HEADS * ATT_HD
ATT_SCALE = ATT_HD ** -0.5


def _rope(x, cos, sin, inverse=False):
    half = ROPE_DIM // 2
    outs = []
    for p in range(x.shape[1] // 128):
        xp = x[:, p * 128:(p + 1) * 128]
        lane = lax.broadcasted_iota(jnp.int32, xp.shape, 1) % ATT_HD
        sw = jnp.where(lane < half, pltpu.roll(xp, 128 - half, 1), pltpu.roll(xp, half, 1))
        outs.append(xp * cos - sw * sin if inverse else xp * cos + sw * sin)
    return outs[0] if len(outs) == 1 else jnp.concatenate(outs, axis=1)


PAIRS_PER_KV = ATT_GROUP // 2


def _swap_halves(x):
    return pltpu.roll(x, ATT_HD, 1)


def _kv_padded(t, low):
    sw = _swap_halves(t)
    zero = jnp.zeros_like(t)
    out = []
    for g in range(ATT_KV_HEADS):
        in_low, in_high = (t, sw) if g == 0 else (sw, t)
        out.append((jnp.where(low, in_low, zero).astype(BF16), jnp.where(low, zero, in_high).astype(BF16)))
    return out


def _swa_mask(first_block):
    qi = lax.broadcasted_iota(jnp.int32, (WINDOW, 2 * WINDOW), 0)
    mi = lax.broadcasted_iota(jnp.int32, (WINDOW, 2 * WINDOW), 1)
    band = (mi > qi) & (mi <= qi + WINDOW)
    return band & (jnp.logical_not(first_block) | (mi >= WINDOW))


def _swa_specs(nb):
    cur = lambda b, i: (b, i, 0)
    prev = lambda b, i: (b, jnp.maximum(i - 1, 0), 0)
    return cur, prev


def _swa_z_specs():
    q = pl.BlockSpec((None, WINDOW, W_AQ), lambda b, i: (b, i, O_AQ // W_AQ))
    kv_prev = pl.BlockSpec((None, WINDOW, W_AKV), lambda b, i: (b, jnp.maximum(i - 1, 0), O_AKV // W_AKV))
    kv_cur = pl.BlockSpec((None, WINDOW, W_AKV), lambda b, i: (b, i, O_AKV // W_AKV))
    return q, kv_prev, kv_cur


def _swa_fwd(z, cos, sin, sinks, *, name):
    B, S, _ = z.shape
    nb = S // WINDOW
    cur, prev = _swa_specs(nb)

    def body(q_ref, kvp_ref, kvc_ref, cp_ref, sp_ref, cc_ref, sc_ref, sink_ref, o_ref, lse_ref, qr_ref, kr_ref):
        cos_c, sin_c = cc_ref[...], sc_ref[...]
        q = (_rope(q_ref[...].astype(F32), cos_c, sin_c) * ATT_SCALE).astype(BF16)
        k = jnp.concatenate([_rope(kvp_ref[:, :KV_W].astype(F32), cp_ref[...], sp_ref[...]),
                             _rope(kvc_ref[:, :KV_W].astype(F32), cos_c, sin_c)], axis=0)
        qr_ref[...] = q
        kr_ref[...] = k[WINDOW:].astype(BF16)
        v = jnp.concatenate([kvp_ref[:, KV_W:], kvc_ref[:, KV_W:]], axis=0).astype(F32)
        low = lax.broadcasted_iota(jnp.int32, k.shape, 1) < ATT_HD
        kpad = _kv_padded(k, low)
        vpad = _kv_padded(v, low)
        mask = _swa_mask(pl.program_id(1) == 0)
        lses = []
        for g in range(ATT_KV_HEADS):
            pairs = range(g * PAIRS_PER_KV, (g + 1) * PAIRS_PER_KV)
            keys = [(p, e) for p in pairs for e in (0, 1)]
            qp = {p: q[:, p * 128:(p + 1) * 128] for p in pairs}
            s = {pe: jnp.where(mask, _nt(qp[pe[0]], kpad[g][pe[1]]), NEG_INF) for pe in keys}
            pr = {}
            for pe in keys:
                sink = sink_ref[0, 2 * pe[0] + pe[1]]
                m = jnp.maximum(jnp.max(s[pe], axis=1, keepdims=True), sink)
                ex = jnp.exp(s[pe] - m)
                den = jnp.sum(ex, axis=1, keepdims=True) + jnp.exp(sink - m)
                pr[pe] = (ex * (1.0 / den)).astype(BF16)
                lses.append(m + jnp.log(den))
            for p in pairs:
                o_ref[:, p * 128:(p + 1) * 128] = (_nn(pr[p, 0], vpad[g][0]) + _nn(pr[p, 1], vpad[g][1])).astype(BF16)
        lse_ref[...] = jnp.concatenate(lses, axis=1)

    tab = lambda im: pl.BlockSpec((None, WINDOW, 128), im)
    return pl.pallas_call(
        body, name=name, grid=(B, nb),
        in_specs=[*_swa_z_specs(),
                  tab(prev), tab(prev), tab(cur), tab(cur),
                  pl.BlockSpec(memory_space=pltpu.SMEM)],
        out_specs=[pl.BlockSpec((None, WINDOW, D_MODEL), cur), pl.BlockSpec((None, WINDOW, ATT_HEADS), cur),
                   pl.BlockSpec((None, WINDOW, D_MODEL), cur), pl.BlockSpec((None, WINDOW, KV_W), cur)],
        out_shape=[jax.ShapeDtypeStruct((B, S, D_MODEL), BF16), jax.ShapeDtypeStruct((B, S, ATT_HEADS), F32),
                   jax.ShapeDtypeStruct((B, S, D_MODEL), BF16), jax.ShapeDtypeStruct((B, S, KV_W), BF16)],
        compiler_params=_params("parallel", "parallel"),
    )(z, z, z, cos, sin, cos, sin, sinks)


def _swa_bwd(z, qr, kr, cos, sin, sinks, lse, dob, dz, *, name):
    B, S, _ = z.shape
    nb = S // WINDOW
    cur, prev = _swa_specs(nb)

    def body(q_ref, krp_ref, krc_ref, kvp_ref, kvc_ref, cp_ref, sp_ref, cc_ref, sc_ref, sink_ref, lse_ref, do_ref, dz_in,
             dq_ref, dkc_ref, dkp_ref, dsink_ref):
        @pl.when((pl.program_id(0) == 0) & (pl.program_id(1) == 0))
        def _():
            dsink_ref[...] = jnp.zeros_like(dsink_ref)

        cos_c, sin_c, cos_p, sin_p = cc_ref[...], sc_ref[...], cp_ref[...], sp_ref[...]
        q = q_ref[...]
        k = jnp.concatenate([krp_ref[...], krc_ref[...]], axis=0).astype(F32)
        v = jnp.concatenate([kvp_ref[:, KV_W:], kvc_ref[:, KV_W:]], axis=0).astype(F32)
        low = lax.broadcasted_iota(jnp.int32, k.shape, 1) < ATT_HD
        kpad = _kv_padded(k, low)
        vpad = _kv_padded(v, low)
        mask = _swa_mask(pl.program_id(1) == 0)
        lse = lse_ref[...]
        dq_parts, dk_sum, dv_sum, dsinks = [], [], [], []
        for g in range(ATT_KV_HEADS):
            pairs = range(g * PAIRS_PER_KV, (g + 1) * PAIRS_PER_KV)
            keys = [(p, e) for p in pairs for e in (0, 1)]
            qp = {p: q[:, p * 128:(p + 1) * 128] for p in pairs}
            dop = {p: do_ref[:, p * 128:(p + 1) * 128] for p in pairs}
            s = {pe: jnp.where(mask, _nt(qp[pe[0]], kpad[g][pe[1]]), NEG_INF) for pe in keys}
            dp = {pe: _nt(dop[pe[0]], vpad[g][pe[1]]) for pe in keys}
            pr, ds = {}, {}
            for pe in keys:
                h = 2 * pe[0] + pe[1]
                lse_h = lse[:, h:h + 1]
                pf = jnp.exp(s[pe] - lse_h)
                delta = jnp.sum(pf * dp[pe], axis=1, keepdims=True)
                ds[pe] = (pf * (dp[pe] - delta)).astype(BF16)
                pr[pe] = pf.astype(BF16)
                p_sink = jnp.exp(sink_ref[0, h] - lse_h)
                dsinks.append(-jnp.sum(p_sink * delta, axis=0, keepdims=True))
            for p in pairs:
                dq_parts.append((_nn(ds[p, 0], kpad[g][0]) + _nn(ds[p, 1], kpad[g][1])) * ATT_SCALE)
            x = [sum(_tn(ds[p, e], qp[p]) for p in pairs) for e in (0, 1)]
            y = [sum(_tn(pr[p, e], dop[p]) for p in pairs) for e in (0, 1)]
            zk = jnp.where(low, x[0], x[1])
            zv = jnp.where(low, y[0], y[1])
            dk_sum.append(zk + _swap_halves(zk))
            dv_sum.append(zv + _swap_halves(zv))
        dq_ref[...] = _rope(jnp.concatenate(dq_parts, axis=1), cos_c, sin_c, inverse=True).astype(BF16)
        dk = jnp.where(low, dk_sum[0], dk_sum[1])
        dv = jnp.where(low, dv_sum[0], dv_sum[1])
        dkp_ref[:, :KV_W] = _rope(dk[:WINDOW], cos_p, sin_p, inverse=True)
        dkp_ref[:, KV_W:] = dv[:WINDOW]
        dkc_ref[:, :KV_W] = _rope(dk[WINDOW:], cos_c, sin_c, inverse=True)
        dkc_ref[:, KV_W:] = dv[WINDOW:]
        dsink_ref[...] += jnp.concatenate(dsinks, axis=1)

    tab = lambda im: pl.BlockSpec((None, WINDOW, 128), im)
    return pl.pallas_call(
        body, name=name, grid=(B, nb),
        in_specs=[pl.BlockSpec((None, WINDOW, D_MODEL), cur), tab(prev), tab(cur),
                  *_swa_z_specs()[1:],
                  tab(prev), tab(prev), tab(cur), tab(cur),
                  pl.BlockSpec(memory_space=pltpu.SMEM),
                  pl.BlockSpec((None, WINDOW, ATT_HEADS), cur),
                  pl.BlockSpec((None, WINDOW, D_MODEL), cur),
                  ANY],
        out_specs=[_swa_z_specs()[0],
                   pl.BlockSpec((None, WINDOW, 2 * KV_W), cur), pl.BlockSpec((None, WINDOW, 2 * KV_W), cur),
                   pl.BlockSpec((1, ATT_HEADS), lambda b, i: (0, 0))],
        out_shape=[jax.ShapeDtypeStruct(dz.shape, BF16),
                   jax.ShapeDtypeStruct((B, S, 2 * KV_W), F32), jax.ShapeDtypeStruct((B, S, 2 * KV_W), F32),
                   jax.ShapeDtypeStruct((1, ATT_HEADS), F32)],
        input_output_aliases={12: 0},
        compiler_params=_params("arbitrary", "arbitrary"),
    )(qr, kr, kr, z, z, cos, sin, cos, sin, sinks, lse, dob, dz)


def _swa_dkv_combine(dkv_cur, dkv_prev, dz, *, name):
    B, S, W = dkv_cur.shape

    def body(c_ref, p_ref, dz_in, o_ref):
        rows = lax.broadcasted_iota(jnp.int32, (S, W), 0)
        o_ref[...] = (c_ref[...] + _shift_up(p_ref[...], WINDOW, rows, S)).astype(BF16)

    seq = pl.BlockSpec((None, S, W), lambda b: (b, 0, 0))
    return pl.pallas_call(
        body, name=name, grid=(B,),
        in_specs=[seq, seq, ANY], out_specs=pl.BlockSpec((None, S, W), lambda b: (b, 0, O_AKV // W_AKV)),
        out_shape=jax.ShapeDtypeStruct(dz.shape, BF16),
        input_output_aliases={2: 0},
        compiler_params=_params("parallel"),
    )(dkv_cur, dkv_prev, dz)


def _rope_tables(positions):
    half = ROPE_DIM // 2
    inv = ROPE_THETA ** (-2.0 * jnp.arange(half, dtype=F32) / ROPE_DIM)
    ang = positions.astype(F32)[..., None] * inv
    c, s = jnp.cos(ang), jnp.sin(ang)
    pad = jnp.zeros(ang.shape[:-1] + (ATT_HD - ROPE_DIM,), F32)
    cos = jnp.concatenate([c, c, pad + 1.0], axis=-1)
    sin = jnp.concatenate([-s, s, pad], axis=-1)
    return jnp.tile(cos, (1, 1, 2)), jnp.tile(sin, (1, 1, 2))


def _lower_bound(lb_logits, *, name):
    def body(l_ref, o_ref):
        l = l_ref[...]
        e = jnp.exp(l - jnp.max(l, axis=0, keepdims=True))
        o_ref[...] = e[0:1] / jnp.sum(e, axis=0, keepdims=True)

    return pl.pallas_call(body, name=name, out_shape=jax.ShapeDtypeStruct((1, lb_logits.shape[1]), F32))(lb_logits)


W_ZH, W_GATES, W_AQ, W_AKV = 4 * HF, 2 * D_MODEL, ATT_HEADS * ATT_HD, 2 * KV_W
O_ZH, O_GATES, O_AQ, O_AKV = 0, W_ZH, W_ZH + W_GATES, W_ZH + W_GATES + W_AQ
W_IN = W_ZH + W_GATES + W_AQ + W_AKV


W_IN_BLK = W_IN // N_DEV


def _reference_row_block(j, rows=256):
    nz, ng = W_ZH // rows, W_GATES // rows
    return jnp.where(j < nz, j, jnp.where(j < nz + ng, j + (W_AQ + W_AKV) // rows, j - ng))


W_IN_SEGMENTS = ((O_ZH, 0, W_ZH), (O_GATES, W_ZH + W_AQ + W_AKV, W_GATES), (O_AQ, W_ZH, W_AQ + W_AKV))


def _local_step(x, positions, target, small, w_in_t, rest_weights, emit, start_token):
    B, S, D = x.shape
    T = B * S
    x2 = x.reshape(T, D)
    cos, sin = _rope_tables(positions)
    lb = _lower_bound(small["lb_logits"], name="lb_fwd")
    zero = lambda tok: tok[0:1, 0:1]

    u1, z = _norm_matmul(x2, small["norm1_g"] + zero(start_token), w_in_t, W_IN_SEGMENTS, tm=512, name="norm1_mm_z")
    z3 = z.reshape(B, S, W_IN)
    oa, states = _hgrn_fwd(z3, lb, small["hgrn_norm_g"], name="hgrn_fwd")
    ob, lse, qr, kr = _swa_fwd(z3, cos, sin, small["attn_sinks"], name="swa_fwd")
    oa2 = oa.reshape(T, D)
    ob2 = ob.reshape(T, D)
    W = rest_weights("mix", ob)
    row = lambda tm, dtype=None: _row_spec(tm, D)
    tile = lambda dtype: jax.ShapeDtypeStruct((T, D), dtype)
    vec = _full_spec((1, D))
    vec_shape = jax.ShapeDtypeStruct((1, D), F32)

    def merge_ep(acc_a, acc_b, g_ref):
        pa, pb = acc_a.astype(BF16), acc_b.astype(BF16)
        return pa, pb, _merge_fn(g_ref[...], pa, pb)

    pa, pb, merged = _matmul_ep([(oa2, W["w_a"], False, 0), (ob2, W["w_b"], False, 0)], tm=1024, ins=[z], in_specs=[_gates_spec(1024)],
                                out_shapes=[tile(BF16)] * 3, out_specs=[row(1024)] * 3, epilogue=merge_ep, name="mm_pa_pb_merge")

    def resid_norm_ep(acc, x_ref, g_ref):
        hh = acc + x_ref[...]
        return hh, _rms(hh, g_ref[...])

    h, u2 = _matmul_ep([(merged, W["w_out"], False, 0)], tm=1024, ins=[x2, small["norm2_g"]], in_specs=[row(1024), vec],
                       out_shapes=[tile(F32), tile(BF16)], out_specs=[row(1024), row(1024)], epilogue=resid_norm_ep, name="mm_h_norm2")
    W.update(rest_weights("ffn", u2))
    gu3 = _matmul_col_tiles(u2, W["w_ffn_t"], tm=1024, tn=D_FF, tc=CONV_TC, name="mm_gu").reshape(2 * D_FF // CONV_TC, B, S, CONV_TC)
    act, a_pre = _conv_act_fwd(gu3, W["conv_w"], small["conv_b"], name="conv_act_fwd")
    act2 = act.reshape(T, D_FF)
    g = {}

    def loss_ep(acc, h_ref, g_ref, t_ref):
        y, vjp = jax.vjp(_rms, acc + h_ref[...], g_ref[...])
        err = y - t_ref[...]
        dx, dg = vjp(err * (1.0 / D))
        return dx, dx, dg, (0.5 / D) * jnp.sum(jnp.sum(err * err, axis=1, keepdims=True), axis=0, keepdims=True)

    dh2, dh2b, g["final_g"], loss = _matmul_ep(
        [(act2, W["w_down"], False, 0)], tm=512, ins=[h, small["final_g"].reshape(1, D), target.reshape(T, D)], in_specs=[row(512), vec, row(512)],
        out_shapes=[tile(F32), tile(BF16), vec_shape, jax.ShapeDtypeStruct((1, 1), F32)],
        out_specs=[row(512), row(512), vec, _full_spec((1, 1))], sums=(2, 3), epilogue=loss_ep, name="mm_h2_loss")
    dact = _matmul(dh2b, W["w_down"], tb=True, out_dtype=BF16, name="mm_dact", tm=1024, tn=D_FF)
    dw_down_t = _matmul(dh2b, act2, ta=True, out_dtype=BF16, name="mm_dw_down", tm=1024, tn=256, tk=8192)
    dg_, dup, g["conv_w"], g["conv_b"] = _conv_act_bwd(gu3, a_pre, W["conv_w"], dact.reshape(B, S, D_FF), name="conv_act_bwd")
    dg2 = dg_.reshape(T, D_FF)
    dup2 = dup.reshape(T, D_FF)
    dw_ffn_t = _matmul(u2, dg2, ta=True, out_t=True, out_dtype=BF16, into=lax.empty((2 * D_FF, D), BF16), o_noff=0, name="mm_dw_ffn_g", tm=1024, tn=256, tk=8192)
    dw_ffn_t = _matmul(u2, dup2, ta=True, out_t=True, out_dtype=BF16, into=dw_ffn_t, o_noff=D_FF // 256, name="mm_dw_ffn_u", tm=1024, tn=256, tk=8192)
    tok = emit("ffn", dict(w_ffn_t=dw_ffn_t, w_down=dw_down_t.T))
    def norm2_bwd_ep(acc_g, acc_u, h_ref, g_ref, dh2_ref):
        _, vjp = jax.vjp(_rms, h_ref[...], g_ref[...])
        dx, dg = vjp(acc_g + acc_u)
        dx = dx + dh2_ref[...]
        return dx, dx, dg

    dh, dhb, g["norm2_g"] = _matmul_ep(
        [(dg2, W["w_ffn_t"], False, 0), (dup2, W["w_ffn_t"], False, 1)], tm=512, ins=[h, small["norm2_g"] + zero(tok), dh2], in_specs=[row(512), vec, row(512)],
        out_shapes=[tile(F32), tile(BF16), vec_shape], out_specs=[row(512), row(512), vec], sums=(2,), epilogue=norm2_bwd_ep, name="mm_du2_norm2_bwd")
    dw_out = _matmul(merged, dhb, ta=True, out_dtype=BF16, name="mm_dw_out", tm=1024, tn=1024, tk=2048)

    def merge_bwd_ep(acc, g_ref, pa_ref, pb_ref, dz_in):
        gt = g_ref[...].astype(F32)
        sa = _sigmoid(gt[:, :D_MODEL])
        sb = _sigmoid(gt[:, D_MODEL:])
        dgates = jnp.concatenate([acc * pa_ref[...].astype(F32) * sa * (1.0 - sa), acc * pb_ref[...].astype(F32) * sb * (1.0 - sb)], axis=1)
        return dgates, acc * sa, acc * sb

    dz, dpa, dpb = _matmul_ep(
        [(dhb, W["w_out"], True, 0)], tm=512, ins=[z, pa, pb, lax.empty((T, W_IN), BF16)], in_specs=[_gates_spec(512), row(512), row(512), ANY],
        out_shapes=[jax.ShapeDtypeStruct((T, W_IN), BF16), tile(BF16), tile(BF16)], out_specs=[_gates_spec(512), row(512), row(512)],
        aliases={3: 0}, epilogue=merge_bwd_ep, name="mm_dmerged_merge_bwd")
    doa, dob = _matmul_ep([(dpa, W["w_a"], True, 0), (dpb, W["w_b"], True, 0)], tm=1024, ins=[], in_specs=[],
                          out_shapes=[tile(BF16)] * 2, out_specs=[row(1024)] * 2, epilogue=lambda da, db: (da, db), name="mm_doa_dob")
    dw_a = _matmul(oa2, dpa, ta=True, out_dtype=BF16, name="mm_dw_a", tm=1024, tn=1024, tk=2048)
    dw_b = _matmul(ob2, dpb, ta=True, out_dtype=BF16, name="mm_dw_b", tm=1024, tn=1024, tk=2048)
    tok = emit("mix", dict(w_out=dw_out, w_a=dw_a, w_b=dw_b))
    dz3, dkv_cur, dkv_prev, dsinks = _swa_bwd(z3, qr, kr, cos, sin, small["attn_sinks"] + zero(tok), lse, dob.reshape(B, S, D),
                                              dz.reshape(B, S, W_IN), name="swa_bwd")
    dz3 = _swa_dkv_combine(dkv_cur, dkv_prev, dz3, name="swa_dkv")
    g["attn_sinks"] = dsinks
    dz3, g["lb"], g["hgrn_norm_g"] = _hgrn_bwd(z3, lb, small["hgrn_norm_g"], states, doa.reshape(B, S, D), dz3, name="hgrn_bwd")
    dz = dz3.reshape(T, W_IN)
    dw_in_t = _matmul(u1, dz, ta=True, out_t=True, o_block_perm=_reference_row_block, out_dtype=BF16, name="mm_dw_in", tm=1024, tn=256, tk=8192)
    tok = emit("in", dict(w_in_t=dw_in_t))
    dx, g["norm1_g"] = _matmul_norm_bwd(dz, w_in_t, W_IN_SEGMENTS, x2, small["norm1_g"], dh, tok, tm=512, name="mm_du1_norm1_bwd")
    g["lb_logits"] = _lb_bwd(g.pop("lb"), lb, name="lb_bwd")
    return loss, dx.reshape(B, S, D), g


def _my_place():
    return lax.axis_index("x"), lax.axis_index("y"), lax.axis_index("c")


def _gather_blocks(x_ref, out_ref, send_sems, recv_sems, local_sem):
    x, y, c = _my_place()
    me, sibling = (x, y, c), (x, y, 1 - c)
    chips = [(1 - x, y), (x, 1 - y), (1 - x, 1 - y)]

    def slot(px, py, pc):
        return out_ref.at[4 * px + 2 * py + pc]

    def copy(k, block, to, src=None):
        return pltpu.make_async_remote_copy(
            src_ref=slot(*block) if src is None else src, dst_ref=slot(*block),
            send_sem=send_sems.at[k], recv_sem=recv_sems.at[k], device_id=to, device_id_type=MESH)

    mine = pltpu.make_async_copy(x_ref, slot(*me), local_sem)
    mine.start()
    first = [copy(0, me, sibling, src=x_ref)]
    first += [copy(1 + j, me, (*chip, c), src=x_ref) for j, chip in enumerate(chips)]
    for cp in first:
        cp.start()
    passed = [copy(4 + j, (*chip, c), sibling) for j, chip in enumerate(chips)]
    for j, chip in enumerate(chips):
        copy(1 + j, (*chip, c), me).wait_recv()
        passed[j].start()
    copy(0, sibling, me).wait_recv()
    for j, chip in enumerate(chips):
        copy(4 + j, (*chip, 1 - c), me).wait_recv()
    for cp in first + passed:
        cp.wait_send()
    mine.wait()


GATHER_SEMS = [pltpu.SemaphoreType.DMA((7,)), pltpu.SemaphoreType.DMA((7,)), pltpu.SemaphoreType.DMA]


def _all_gather(blk, *, name):
    return pl.pallas_call(
        _gather_body_fn(), name=name,
        out_shape=jax.ShapeDtypeStruct((N_DEV,) + blk.shape, blk.dtype),
        in_specs=[ANY], out_specs=ANY,
        scratch_shapes=GATHER_SEMS,
    )(blk)


def _gather_body_fn():
    def body(x_ref, out_ref, send_sems, recv_sems, local_sem):
        _gather_blocks(x_ref, out_ref, send_sems, recv_sems, local_sem)
    return body


SLAB_W = 1152
SMALL_SHAPES = dict(norm1_g=(1, D_MODEL), lb_logits=(2, HGRN_HEADS * HGRN_DK), hgrn_norm_g=(1, HGRN_DK), attn_sinks=(1, ATT_HEADS),
                    norm2_g=(1, D_MODEL), conv_b=(1, D_FF), final_g=(1, D_MODEL))
CONVW_BLK = D_FF // N_DEV
CONVW_STRIDE = SLAB_W // 3


def _slab_layout():
    layout, r = {}, 0
    for nm, (nr, w) in SMALL_SHAPES.items():
        layout[nm] = []
        for i in range(nr):
            for c0 in range(0, w, SLAB_W):
                layout[nm].append((r, i, c0, min(SLAB_W, w - c0)))
                r += 1
    return layout, r


SMALL_ROWS, _N_SMALL_ROWS = _slab_layout()
CONV_ROW0 = -(-_N_SMALL_ROWS // 8) * 8
LOSS_ROW = CONV_ROW0 + N_DEV
SLAB_ROWS = LOSS_ROW + 8


def _small_step(grads, g_conv_w, loss, params, moments, variances, dev, *, name):
    names = list(SMALL_ROWS)
    n = len(names)

    def body(dev_ref, *refs):
        g_refs = dict(zip(names, refs[:n]))
        gc_ref, loss_ref = refs[n], refs[n + 1]
        base = n + 2
        w_refs, m_refs, v_refs = (dict(zip(names + ["conv_w"], refs[base + i * (n + 1):base + (i + 1) * (n + 1)])) for i in range(3))
        o = base + 3 * (n + 1)
        gath_ref, loss_out = refs[o], refs[o + 1]
        outs = {nm: refs[o + 2 + 4 * i:o + 6 + 4 * i] for i, nm in enumerate(names + ["conv_w"])}
        slab, total, send_sems, recv_sems, local_sem = refs[-5:]

        slab[...] = jnp.zeros_like(slab)
        for nm, pieces in SMALL_ROWS.items():
            for r, i, c0, w in pieces:
                slab[r:r + 1, 0:w] = g_refs[nm][i:i + 1, c0:c0 + w]
        for p in range(N_DEV):
            for j in range(3):
                slab[CONV_ROW0 + p:CONV_ROW0 + p + 1, j * CONVW_STRIDE:j * CONVW_STRIDE + CONVW_BLK] = gc_ref[j:j + 1, p * CONVW_BLK:(p + 1) * CONVW_BLK]
        slab[LOSS_ROW:LOSS_ROW + 1, 0:1] = loss_ref[...]
        _gather_blocks(slab, gath_ref, send_sems, recv_sems, local_sem)
        acc = gath_ref[0]
        for p in range(1, N_DEV):
            acc = acc + gath_ref[p]
        total[...] = acc
        loss_out[...] = total[LOSS_ROW:LOSS_ROW + 1, 0:1]

        def update(nm, g, i, c0, w):
            at = (slice(i, i + 1), slice(c0, c0 + w))
            d, mn, vn = _adamw_math(w_refs[nm][at], g, m_refs[nm][at], v_refs[nm][at])
            for ref, val in zip(outs[nm], (g, d, mn, vn)):
                ref[at] = val

        for nm, pieces in SMALL_ROWS.items():
            for r, i, c0, w in pieces:
                update(nm, total[r:r + 1, 0:w], i, c0, w)
        conv_rows = total[CONV_ROW0:CONV_ROW0 + N_DEV, :]
        rowid = lax.broadcasted_iota(jnp.int32, conv_rows.shape, 0)
        mine = jnp.sum(jnp.where(rowid == dev_ref[0], conv_rows, 0.0), axis=0, keepdims=True)
        for j in range(3):
            update("conv_w", mine[:, j * CONVW_STRIDE:j * CONVW_STRIDE + CONVW_BLK], j, 0, CONVW_BLK)

    order = names + ["conv_w"]
    ins = [grads[nm] for nm in names] + [g_conv_w, loss]
    for d in (params, moments, variances):
        ins += [d[nm] for nm in order]
    vmem = pl.BlockSpec(memory_space=pltpu.VMEM)
    out_shape = [jax.ShapeDtypeStruct((N_DEV, SLAB_ROWS, SLAB_W), F32), jax.ShapeDtypeStruct((1, 1), F32)]
    for nm in order:
        out_shape += [jax.ShapeDtypeStruct(params[nm].shape, F32)] * 4
    res = pl.pallas_call(
        body, name=name,
        grid_spec=pltpu.PrefetchScalarGridSpec(
            num_scalar_prefetch=1, grid=(1,),
            in_specs=[vmem] * len(ins), out_specs=[vmem] * len(out_shape),
            scratch_shapes=[pltpu.VMEM((SLAB_ROWS, SLAB_W), F32), pltpu.VMEM((SLAB_ROWS, SLAB_W), F32)] + GATHER_SEMS),
        out_shape=out_shape,
    )(dev, *ins)
    return res[1], {nm: tuple(res[2 + 4 * i:6 + 4 * i]) for i, nm in enumerate(order)}


HBM_SPEC = pl.BlockSpec(memory_space=pltpu.HBM)
SEM_SPEC = pl.BlockSpec(memory_space=pltpu.SEMAPHORE)
DATAFLOW_EFFECT = pltpu.SideEffectType.DATAFLOW_SIDE_EFFECTING
N_PEERS = N_DEV - 1


def _peers(x, y, c):
    return [(1 - x if r & 4 else x, 1 - y if r & 2 else y, 1 - c if r & 1 else c) for r in range(1, N_DEV)]


def _exchange_start(srcs, scatter, *, after=None, name):
    n = len(srcs)
    lands = [lax.empty(a.shape if scatter else (N_DEV,) + a.shape, a.dtype) for a in srcs]
    extra = [] if after is None else [after]

    def body(*refs):
        src_refs, land_refs = refs[:n], refs[n:2 * n]
        send_sems, recv_sems, token = refs[2 * n + len(extra)], refs[2 * n + len(extra) + 1], refs[-1]
        x, y, c = _my_place()
        me = 4 * x + 2 * y + c
        for i in range(n):
            for r, (tx, ty, tc) in enumerate(_peers(x, y, c)):
                src = src_refs[i].at[4 * tx + 2 * ty + tc] if scatter else src_refs[i]
                pltpu.make_async_remote_copy(
                    src_ref=src, dst_ref=land_refs[i].at[me], send_sem=send_sems.at[N_PEERS * i + r],
                    recv_sem=recv_sems.at[N_PEERS * i + r], device_id=(tx, ty, tc), device_id_type=MESH).start()
        token[...] = jnp.zeros_like(token)

    thru = [pltpu.HBM(a.shape, a.dtype) for a in list(srcs) + lands]
    res = pl.pallas_call(
        body, name=name,
        out_shape=(pltpu.SemaphoreType.DMA((N_PEERS * n,)), pltpu.SemaphoreType.DMA((N_PEERS * n,)), *thru,
                   jax.ShapeDtypeStruct((8, 128), F32)),
        in_specs=[HBM_SPEC] * (2 * n) + [ANY] * len(extra),
        out_specs=(SEM_SPEC, SEM_SPEC, *([HBM_SPEC] * (2 * n)), pl.BlockSpec(memory_space=pltpu.VMEM)),
        input_output_aliases={i: 2 + i for i in range(2 * n)},
        compiler_params=pltpu.CompilerParams(has_side_effects=DATAFLOW_EFFECT),
    )(*[pltpu.with_memory_space_constraint(a, pltpu.HBM) for a in list(srcs) + lands], *extra)
    return (res[0], res[1], list(res[2:2 + n]), list(res[2 + n:2 + 2 * n]), scatter), res[-1]


def _exchange_wait(handle, after, *, name):
    send_sems, recv_sems, srcs, lands, scatter = handle
    n = len(srcs)

    def body(*refs):
        src_refs, land_refs = refs[:n], refs[n:2 * n]
        send_sems, recv_sems = refs[2 * n], refs[2 * n + 1]
        x, y, c = _my_place()
        for i in range(n):
            for r in range(N_PEERS):
                src = src_refs[i].at[0] if scatter else src_refs[i]
                cp = pltpu.make_async_remote_copy(
                    src_ref=src, dst_ref=land_refs[i].at[0], send_sem=send_sems.at[N_PEERS * i + r],
                    recv_sem=recv_sems.at[N_PEERS * i + r], device_id=(x, y, c), device_id_type=MESH)
                cp.wait_send()
                cp.wait_recv()

    thru = [pltpu.HBM(a.shape, a.dtype) for a in srcs + lands]
    res = pl.pallas_call(
        body, name=name, out_shape=tuple(thru),
        in_specs=[HBM_SPEC] * (2 * n) + [SEM_SPEC, SEM_SPEC, ANY], out_specs=tuple([HBM_SPEC] * (2 * n)),
        input_output_aliases={i: i for i in range(2 * n)},
        compiler_params=pltpu.CompilerParams(has_side_effects=DATAFLOW_EFFECT),
    )(*srcs, *lands, send_sems, recv_sems, after)
    return list(res[:n]), list(res[n:])


def _with_own(land, own, me):
    return lax.dynamic_update_index_in_dim(land, own, me, 0)


def _adamw_math(w, g, m, v):
    m = ADAM_B1 * m + (1.0 - ADAM_B1) * g
    v = ADAM_B2 * v + (1.0 - ADAM_B2) * (g * g)
    m_hat = m / (1.0 - ADAM_B1 ** ADAM_STEP)
    v_hat = v / (1.0 - ADAM_B2 ** ADAM_STEP)
    delta = -ADAM_LR * (m_hat / (jnp.sqrt(v_hat) + ADAM_EPS) + ADAM_WD * w)
    return delta, m, v


def _adamw_sum(parts, w, m, v, *, name):
    shape = w.shape
    R, n = shape[-2], shape[-1]
    w, m, v = (t.reshape(R, n) for t in (w, m, v))
    tr = _pick(R, (256, 464, 352, 128))

    def body(p_ref, w_ref, m_ref, v_ref, g_ref, d_ref, mo_ref, vo_ref):
        g = p_ref[0].astype(F32)
        for p in range(1, N_DEV):
            g = g + p_ref[p].astype(F32)
        d, mn, vn = _adamw_math(w_ref[...], g, m_ref[...], v_ref[...])
        g_ref[...] = g
        d_ref[...] = d
        mo_ref[...] = mn
        vo_ref[...] = vn

    row = pl.BlockSpec((tr, n), lambda i: (i, 0))
    outs = pl.pallas_call(
        body, name=name, grid=(R // tr,),
        in_specs=[pl.BlockSpec((N_DEV, tr, n), lambda i: (0, i, 0)), row, row, row],
        out_specs=[row, row, row, row],
        out_shape=[jax.ShapeDtypeStruct((R, n), F32)] * 4,
        compiler_params=_params("parallel"),
    )(parts, w, m, v)
    return [t.reshape(shape) for t in outs]


def _lb_bwd(dlb, lb, *, name):
    def body(d_ref, lb_ref, o_ref):
        t = d_ref[...] * lb_ref[...] * (1.0 - lb_ref[...])
        o_ref[0:1, :] = t
        o_ref[1:2, :] = -t

    return pl.pallas_call(body, name=name, out_shape=jax.ShapeDtypeStruct((2, lb.shape[1]), F32))(dlb, lb)


DOWN_BLK, ROW_BLK = D_FF // N_DEV, D_MODEL // N_DEV
W_FFN_BLK = 2 * D_FF // N_DEV
CONV_BITS_SHAPE = (16, 256)


def kernel(x, positions, norm1_g, w_in, lb_logits, hgrn_norm_g, w_a, attn_sinks, w_b, w_out, norm2_g, w_ffn_in, conv_w, conv_b, w_down, final_g, loss_target, m_norm1_g, m_w_in, m_lb_logits, m_hgrn_norm_g, m_w_a, m_attn_sinks, m_w_b, m_w_out, m_norm2_g, m_w_ffn_in, m_conv_w, m_conv_b, m_w_down, m_final_g, v_norm1_g, v_w_in, v_lb_logits, v_hgrn_norm_g, v_w_a, v_attn_sinks, v_w_b, v_w_out, v_norm2_g, v_w_ffn_in, v_conv_w, v_conv_b, v_w_down, v_final_g):
    xi, yi, ci = _my_place()
    dev = 4 * xi + 2 * yi + ci

    tr = lambda t: jnp.transpose(t[0])
    untr = lambda t: jnp.transpose(t)[None]
    w_in_blocks = _all_gather(tr(w_in).astype(BF16), name="ag_w_in")
    conv_bits = lax.bitcast_convert_type(conv_w, BF16).reshape(-1)
    conv_bits = jnp.pad(conv_bits, (0, CONV_BITS_SHAPE[0] * CONV_BITS_SHAPE[1] - conv_bits.shape[0])).reshape(CONV_BITS_SHAPE)
    w_in_full_t = w_in_blocks.reshape(W_IN, D_MODEL)
    gather_handles = {}
    gather_handles["mix"], tok_mix = _exchange_start([w_a[0].astype(BF16), w_b[0].astype(BF16), w_out[0].astype(BF16)], False,
                                                     after=w_in_full_t, name="ag_mix_start")
    gather_handles["ffn"], tok_ffn = _exchange_start([tr(w_ffn_in).astype(BF16), w_down[0].astype(BF16), conv_bits], False,
                                                     after=tok_mix, name="ag_ffn_start")
    start_token = tok_mix + tok_ffn

    def rest_weights(group, after):
        own, lands = _exchange_wait(gather_handles[group], after, name="ag_" + group + "_wait")
        full = [_with_own(l, o, dev) for l, o in zip(lands, own)]
        if group == "mix":
            return dict(zip(("w_a", "w_b", "w_out"), [t.reshape(D_MODEL, D_MODEL) for t in full]))
        bits = full[2].reshape(N_DEV, -1)[:, :3 * CONVW_BLK * 2].reshape(N_DEV, 3, CONVW_BLK, 2)
        return dict(w_ffn_t=full[0].reshape(2 * D_FF, D_MODEL), w_down=full[1].reshape(D_FF, D_MODEL),
                    conv_w=lax.bitcast_convert_type(bits, F32).transpose(1, 0, 2).reshape(3, D_FF))

    handles = {}

    def emit(group, gr):
        if group == "ffn":
            srcs = [gr["w_ffn_t"].reshape(N_DEV, W_FFN_BLK, D_MODEL), gr["w_down"].reshape(N_DEV, DOWN_BLK, D_MODEL)]
        elif group == "mix":
            srcs = [gr[n].reshape(N_DEV, ROW_BLK, D_MODEL) for n in ("w_out", "w_a", "w_b")]
        else:
            srcs = [gr["w_in_t"].reshape(N_DEV, W_IN_BLK, D_MODEL)]
        handles[group], token = _exchange_start(srcs, True, name="rs_" + group + "_start")
        return token

    small = dict(norm1_g=norm1_g, lb_logits=lb_logits, hgrn_norm_g=hgrn_norm_g, attn_sinks=attn_sinks, norm2_g=norm2_g,
                 conv_b=conv_b, final_g=final_g)
    loss, grad_x, g = _local_step(x, positions, loss_target, small, w_in_full_t, rest_weights, emit, start_token)

    def parts_of(group, after):
        srcs, lands = _exchange_wait(handles[group], after, name="rs_" + group + "_wait")
        return [_with_own(l, lax.dynamic_index_in_dim(s, dev, 0, keepdims=False), dev) for s, l in zip(srcs, lands)]

    p_ffn, p_down = parts_of("ffn", grad_x)
    p_out, p_a, p_b = parts_of("mix", grad_x)
    (p_in,) = parts_of("in", grad_x)
    big = dict(
        w_in=[untr(t) for t in _adamw_sum(p_in, tr(w_in), tr(m_w_in), tr(v_w_in), name="adamw_w_in")],
        w_a=_adamw_sum(p_a, w_a, m_w_a, v_w_a, name="adamw_w_a"),
        w_b=_adamw_sum(p_b, w_b, m_w_b, v_w_b, name="adamw_w_b"),
        w_out=_adamw_sum(p_out, w_out, m_w_out, v_w_out, name="adamw_w_out"),
        w_ffn_in=[untr(t) for t in _adamw_sum(p_ffn, tr(w_ffn_in), tr(m_w_ffn_in), tr(v_w_ffn_in), name="adamw_w_ffn_in")],
        w_down=_adamw_sum(p_down, w_down, m_w_down, v_w_down, name="adamw_w_down"),
    )

    row = lambda t: t.reshape(1, -1) if t.ndim == 1 else t
    shard = lambda t: t.reshape(3, CONVW_BLK)
    sm_g = {nm: g[nm] for nm in SMALL_ROWS}
    sm_w = dict(norm1_g=norm1_g, lb_logits=lb_logits, hgrn_norm_g=hgrn_norm_g, attn_sinks=attn_sinks, norm2_g=norm2_g,
                conv_b=conv_b, final_g=row(final_g), conv_w=shard(conv_w))
    sm_m = dict(norm1_g=m_norm1_g, lb_logits=m_lb_logits, hgrn_norm_g=m_hgrn_norm_g, attn_sinks=m_attn_sinks, norm2_g=m_norm2_g,
                conv_b=m_conv_b, final_g=row(m_final_g), conv_w=shard(m_conv_w))
    sm_v = dict(norm1_g=v_norm1_g, lb_logits=v_lb_logits, hgrn_norm_g=v_hgrn_norm_g, attn_sinks=v_attn_sinks, norm2_g=v_norm2_g,
                conv_b=v_conv_b, final_g=row(v_final_g), conv_w=shard(v_conv_w))
    loss_total, sm_out = _small_step(sm_g, g["conv_w"], loss, sm_w, sm_m, sm_v, dev.astype(jnp.int32).reshape(1), name="small_step")
    shapes = dict(final_g=final_g.shape, conv_w=conv_w.shape)

    names = ("norm1_g", "w_in", "lb_logits", "hgrn_norm_g", "w_a", "attn_sinks", "w_b", "w_out", "norm2_g", "w_ffn_in", "conv_w", "conv_b", "w_down", "final_g")
    outs = [loss_total.reshape(()), grad_x]
    for kind in range(4):
        outs += [big[n][kind] if n in big else sm_out[n][kind].reshape(shapes.get(n, sm_out[n][kind].shape)) for n in names]
    return tuple(outs)
```

```python
import jax
import jax.numpy as jnp
from jax import lax
from jax.experimental import pallas as pl
from jax.experimental.pallas import tpu as pltpu

F32 = jnp.float32
BF16 = jnp.bfloat16

D_MODEL = 1024
HGRN_HEADS = 8
HGRN_DK = 128
CHUNK = 64
ATT_HEADS = 16
ATT_KV_HEADS = 2
ATT_HD = 64
ATT_GROUP = ATT_HEADS // ATT_KV_HEADS
WINDOW = 128
ROPE_DIM = ATT_HD // 4
ROPE_THETA = 500000.0
D_FF = 2816
EPS = 1e-6
NEG_INF = -1e30
N_DEV = 8

ADAM_LR = 0.001
ADAM_B1 = 0.9
ADAM_B2 = 0.999
ADAM_EPS = 1e-08
ADAM_WD = 0.01
ADAM_STEP = 10

MESH = pl.DeviceIdType.MESH
ANY = pl.BlockSpec(memory_space=pl.ANY)


def _pick(n, cands):
    for c in cands:
        if n % c == 0:
            return c
    return n


def _sigmoid(x):
    return 0.5 * jnp.tanh(0.5 * x) + 0.5


def _silu(x):
    hx = 0.5 * x
    return hx * jnp.tanh(hx) + hx


def _rms(x, g):
    return x * lax.rsqrt(jnp.mean(x * x, axis=-1, keepdims=True) + EPS) * g


def _dot(a, b, dims):
    return lax.dot_general(a, b, (dims, ((), ())), preferred_element_type=F32)


def _nn(a, b):
    return _dot(a, b, ((1,), (0,)))


def _nt(a, b):
    return _dot(a, b, ((1,), (1,)))


def _tn(a, b):
    return _dot(a, b, ((0,), (0,)))


def _params(*sem):
    return pltpu.CompilerParams(dimension_semantics=sem, vmem_limit_bytes=56 * 1024 * 1024)


def _matmul(a, b, *, ta=False, tb=False, out_dtype=F32, addend=None, after=None, into=None, o_noff=0, out_t=False,
            o_block_perm=lambda j: j, name, tm, tn, tk=None, n_extent=None, b_koff=0, b_noff=0):
    M, K = (a.shape[1], a.shape[0]) if ta else a.shape
    N = n_extent or (b.shape[0] if tb else b.shape[1])
    tm, tn, tk = min(tm, M), min(tn, N), min(tk or K, K)
    assert M % tm == 0 and N % tn == 0 and K % tk == 0, (name, M, N, K, tm, tn, tk)
    nk = K // tk
    use_scratch = nk > 1 and out_dtype != F32
    grid = (M // tm, N // tn, nk)
    a_spec = pl.BlockSpec((tk, tm), lambda i, j, k: (k, i)) if ta else pl.BlockSpec((tm, tk), lambda i, j, k: (i, k))
    b_spec = pl.BlockSpec((tn, tk), lambda i, j, k: (j + b_noff, k + b_koff)) if tb else pl.BlockSpec((tk, tn), lambda i, j, k: (k + b_koff, j + b_noff))
    o_spec = pl.BlockSpec((tm, tn), lambda i, j, k: (i, j))
    dims = ((0 if ta else 1,), (1 if tb else 0,))
    has_add = addend is not None

    n_in = 2 + has_add + (after is not None) + (into is not None)

    def body(*refs):
        a_ref, b_ref = refs[:2]
        c_ref = refs[2] if has_add else None
        o_ref = refs[n_in]
        part = _dot(a_ref[...], b_ref[...], dims)
        if nk == 1:
            if has_add:
                part = part + c_ref[...].astype(F32)
            o_ref[...] = (part.T if out_t else part).astype(out_dtype)
        else:
            acc_ref = refs[-1] if use_scratch else o_ref
            k = pl.program_id(2)

            @pl.when(k == 0)
            def _():
                acc_ref[...] = part + c_ref[...].astype(F32) if has_add else part

            @pl.when(k > 0)
            def _():
                acc_ref[...] += part

            if use_scratch:
                @pl.when(k == nk - 1)
                def _():
                    o_ref[...] = acc_ref[...].astype(out_dtype)

    in_specs = [a_spec, b_spec] + ([o_spec] if has_add else [])
    args = (a, b) + ((addend,) if has_add else ())
    if after is not None:
        in_specs.append(pl.BlockSpec(after.shape, lambda i, j, k: (0, 0)))
        args += (after,)
    aliases = {}
    if into is not None:
        in_specs.append(ANY)
        args += (into,)
        aliases = {len(args) - 1: 0}
    if out_t:
        assert nk == 1 and not has_add
        o_spec = pl.BlockSpec((tn, tm), lambda i, j, k: (o_block_perm(j) + o_noff, i))
    elif into is not None:
        o_spec = pl.BlockSpec((tm, tn), lambda i, j, k: (i, j + o_noff))
    return pl.pallas_call(
        body,
        name=name,
        grid=grid,
        in_specs=in_specs,
        out_specs=o_spec,
        out_shape=jax.ShapeDtypeStruct(into.shape if into is not None else ((N, M) if out_t else (M, N)), out_dtype),
        input_output_aliases=aliases,
        scratch_shapes=[pltpu.VMEM((tm, tn), F32)] if use_scratch else [],
        compiler_params=_params("parallel", "parallel", "arbitrary"),
    )(*args)


def _matmul_col_tiles(a, b_t, *, tm, tn, tc, name):
    M, K = a.shape
    N = b_t.shape[0]
    tm = min(tm, M)
    per_step = tn // tc

    def body(a_ref, b_ref, o_ref):
        res = _nt(a_ref[...], b_ref[...]).astype(BF16)
        for t in range(per_step):
            o_ref[t] = res[:, t * tc:(t + 1) * tc]

    return pl.pallas_call(
        body, name=name, grid=(M // tm, N // tn),
        in_specs=[pl.BlockSpec((tm, K), lambda i, j: (i, 0)), pl.BlockSpec((tn, K), lambda i, j: (j, 0))],
        out_specs=pl.BlockSpec((per_step, tm, tc), lambda i, j: (j, i, 0)),
        out_shape=jax.ShapeDtypeStruct((N // tc, M, tc), BF16),
        compiler_params=_params("parallel", "parallel"),
    )(a, b_t)


def _matmul_ep(pairs, *, tm, ins, in_specs, out_shapes, out_specs, sums=(), epilogue, aliases=None, name):
    M = pairs[0][0].shape[0]
    tm = min(tm, M)
    mm_specs, mm_args, dims = [], [], []
    for a, b, tb, koff in pairs:
        K = a.shape[1]
        N = b.shape[0] if tb else b.shape[1]
        mm_specs += [pl.BlockSpec((tm, K), lambda i: (i, 0)),
                     pl.BlockSpec((N, K), lambda i, koff=koff: (0, koff)) if tb else pl.BlockSpec((K, N), lambda i, koff=koff: (koff, 0))]
        mm_args += [a, b]
        dims.append(((1,), (1 if tb else 0,)))
    n_mm = len(mm_args)
    n_in = n_mm + len(ins)

    def body(*refs):
        in_refs, out_refs = refs[n_mm:n_in], refs[n_in:]
        accs = [_dot(refs[2 * p][...], refs[2 * p + 1][...], dims[p]) for p in range(len(pairs))]
        outs = epilogue(*accs, *in_refs)
        for k, (ref, val) in enumerate(zip(out_refs, outs)):
            if val is None:
                continue
            if k in sums:
                @pl.when(pl.program_id(0) == 0)
                def _():
                    ref[...] = jnp.zeros_like(ref)

                ref[...] += val
            else:
                ref[...] = val.astype(ref.dtype)

    return pl.pallas_call(
        body, name=name, grid=(M // tm,),
        in_specs=mm_specs + list(in_specs),
        out_specs=list(out_specs), out_shape=list(out_shapes),
        input_output_aliases={n_mm + k: v for k, v in (aliases or {}).items()},
        compiler_params=_params("arbitrary"),
    )(*mm_args, *ins)


def _row_spec(tm, n):
    return pl.BlockSpec((tm, n), lambda i: (i, 0))


def _full_spec(shape):
    return pl.BlockSpec(shape, lambda i: tuple(0 for _ in shape))


MAX_DOT_COLS = 2048


def _resident_spec(shape):
    return pl.BlockSpec(shape, lambda i: tuple(0 for _ in shape), pipeline_mode=pl.Buffered(1))


def _norm_matmul(x, g, w_t, segments, *, tm, name):
    T, D = x.shape
    N = w_t.shape[0]
    tm = min(tm, T)
    chunks = [(c + o, r + o, min(MAX_DOT_COLS, n - o)) for c, r, n in segments for o in range(0, n, MAX_DOT_COLS)]

    def body(x_ref, g_ref, w_ref, u_ref, z_ref):
        u = _rms(x_ref[...], g_ref[...]).astype(BF16)
        u_ref[...] = u
        for c, r, n in chunks:
            z_ref[:, c:c + n] = _nt(u, w_ref[r:r + n, :]).astype(BF16)

    return pl.pallas_call(
        body, name=name, grid=(T // tm,),
        in_specs=[_row_spec(tm, D), _full_spec((1, D)), _resident_spec((N, D))],
        out_specs=[_row_spec(tm, D), _row_spec(tm, N)],
        out_shape=[jax.ShapeDtypeStruct((T, D), BF16), jax.ShapeDtypeStruct((T, N), BF16)],
        compiler_params=_params("parallel"),
    )(x, g, w_t)


def _matmul_norm_bwd(dz, w_t, segments, x, g, dres, after, *, tm, name):
    T, K = dz.shape
    D = w_t.shape[1]
    tm = min(tm, T)

    def body(dz_ref, w_ref, x_ref, g_ref, dr_ref, after_ref, dx_ref, dg_ref):
        @pl.when(pl.program_id(0) == 0)
        def _():
            dg_ref[...] = jnp.zeros_like(dg_ref)

        du = sum(_nn(dz_ref[:, c:c + n], w_ref[r:r + n, :]) for c, r, n in segments)
        _, vjp = jax.vjp(_rms, x_ref[...], g_ref[...])
        dx, dg = vjp(du)
        dx_ref[...] = dx + dr_ref[...]
        dg_ref[...] += dg

    row = _row_spec(tm, D)
    return pl.pallas_call(
        body, name=name, grid=(T // tm,),
        in_specs=[_row_spec(tm, K), _resident_spec((K, D)), row, _full_spec((1, D)), row, _full_spec(after.shape)],
        out_specs=[row, _full_spec((1, D))],
        out_shape=[jax.ShapeDtypeStruct((T, D), F32), jax.ShapeDtypeStruct((1, D), F32)],
        compiler_params=_params("arbitrary"),
    )(dz, w_t, x, g, dres, after)


def _norm_bwd_add(x, g, du, dres, *, with_bf16=True, name):
    T, D = x.shape
    tm = _pick(T, (512, 256, 128))

    def body(x_ref, g_ref, du_ref, dr_ref, dx_ref, *rest):
        dg_ref = rest[-1]
        _, vjp = jax.vjp(_rms, x_ref[...], g_ref[...])
        dx, dg = vjp(du_ref[...].astype(F32))
        dx = dx + dr_ref[...]
        dx_ref[...] = dx
        if with_bf16:
            rest[0][...] = dx.astype(BF16)

        @pl.when(pl.program_id(0) == 0)
        def _():
            dg_ref[...] = jnp.zeros_like(dg_ref)

        dg_ref[...] += dg

    row = _row_spec(tm, D)
    return pl.pallas_call(
        body, name=name, grid=(T // tm,),
        in_specs=[row, _full_spec((1, D)), row, row],
        out_specs=[row] + ([row] if with_bf16 else []) + [_full_spec((1, D))],
        out_shape=[jax.ShapeDtypeStruct((T, D), F32)] + ([jax.ShapeDtypeStruct((T, D), BF16)] if with_bf16 else []) + [jax.ShapeDtypeStruct((1, D), F32)],
        compiler_params=_params("arbitrary"),
    )(x, g, du, dres)


def _merge_fn(gates, a, b):
    ga = gates[:, :D_MODEL].astype(F32)
    gb = gates[:, D_MODEL:].astype(F32)
    return _sigmoid(ga) * a.astype(F32) + _sigmoid(gb) * b.astype(F32)


def _gates_spec(tm):
    return pl.BlockSpec((tm, W_GATES), lambda i: (i, O_GATES // W_GATES))


CONV_TC = 256


def _shift_down(x, n, rows):
    return jnp.where(rows >= n, pltpu.roll(x, n, 0), 0.0)


def _shift_up(x, n, rows, S):
    return jnp.where(rows < S - n, pltpu.roll(x, S - n, 0), 0.0)


def _conv_act_fwd(gu, conv_w, conv_b, *, name):
    _, B, S, tc = gu.shape
    nc = D_FF // tc

    def body(g_ref, up_ref, w_ref, b_ref, o_ref, a_ref):
        g = g_ref[...].astype(F32)
        rows = lax.broadcasted_iota(jnp.int32, g.shape, 0)
        w = w_ref[...]
        a = w[2:3] * g + w[1:2] * _shift_down(g, 1, rows) + w[0:1] * _shift_down(g, 2, rows) + b_ref[...]
        o_ref[...] = (_silu(a) * up_ref[...].astype(F32)).astype(BF16)
        a_ref[...] = a.astype(BF16)

    col = pl.BlockSpec((None, S, tc), lambda b, j: (b, 0, j))
    tile = lambda off: pl.BlockSpec((None, None, S, tc), lambda b, j: (j + off, b, 0, 0))
    return pl.pallas_call(
        body, name=name, grid=(B, nc),
        in_specs=[tile(0), tile(nc),
                  pl.BlockSpec((3, tc), lambda b, j: (0, j)),
                  pl.BlockSpec((1, tc), lambda b, j: (0, j))],
        out_specs=[col, tile(0)],
        out_shape=[jax.ShapeDtypeStruct((B, S, D_FF), BF16), jax.ShapeDtypeStruct((nc, B, S, tc), BF16)],
        compiler_params=_params("parallel", "parallel"),
    )(gu, gu, conv_w, conv_b)


def _conv_act_bwd(gu, a_pre, conv_w, dact, *, name):
    _, B, S, tc = gu.shape
    nc = D_FF // tc

    def body(g_ref, up_ref, a_ref, w_ref, da_ref, dg_ref, dup_ref, dw_ref, db_ref):
        g = g_ref[...].astype(F32)
        up, a, dact = up_ref[...], a_ref[...], da_ref[...]
        rows = lax.broadcasted_iota(jnp.int32, g.shape, 0)
        w = w_ref[...]
        sg = _sigmoid(a)
        dup_ref[...] = dact * a * sg
        da = (dact * up * sg * (1.0 + a * (1.0 - sg))).astype(F32)
        da1 = _shift_up(da, 1, rows, S)
        da2 = _shift_up(da, 2, rows, S)
        dg_ref[...] = (w[2:3] * da + w[1:2] * da1 + w[0:1] * da2).astype(BF16)

        @pl.when(pl.program_id(1) == 0)
        def _():
            dw_ref[...] = jnp.zeros_like(dw_ref)
            db_ref[...] = jnp.zeros_like(db_ref)

        dw_ref[0:1, :] += jnp.sum(da2 * g, axis=0, keepdims=True)
        dw_ref[1:2, :] += jnp.sum(da1 * g, axis=0, keepdims=True)
        dw_ref[2:3, :] += jnp.sum(da * g, axis=0, keepdims=True)
        db_ref[...] += jnp.sum(da, axis=0, keepdims=True)

    col = pl.BlockSpec((None, S, tc), lambda j, b: (b, 0, j))
    tile = lambda off: pl.BlockSpec((None, None, S, tc), lambda j, b: (j + off, b, 0, 0))
    return pl.pallas_call(
        body, name=name, grid=(nc, B),
        in_specs=[tile(0), tile(nc), tile(0),
                  pl.BlockSpec((3, tc), lambda j, b: (0, j)),
                  col],
        out_specs=[col, col, pl.BlockSpec((3, tc), lambda j, b: (0, j)), pl.BlockSpec((1, tc), lambda j, b: (0, j))],
        out_shape=[jax.ShapeDtypeStruct((B, S, D_FF), BF16), jax.ShapeDtypeStruct((B, S, D_FF), BF16),
                   jax.ShapeDtypeStruct((3, D_FF), F32), jax.ShapeDtypeStruct((1, D_FF), F32)],
        compiler_params=_params("parallel", "arbitrary"),
    )(gu, gu, a_pre, conv_w, dact)


HGRN_CPB = 8
HF = HGRN_HEADS * HGRN_DK


def _tri(n, upper=False):
    r = lax.broadcasted_iota(jnp.int32, (n, n), 0)
    c = lax.broadcasted_iota(jnp.int32, (n, n), 1)
    return (c >= r) if upper else (r >= c)


def _hs(h):
    return slice(h * HGRN_DK, (h + 1) * HGRN_DK)


def _cumsum_rows(tri_b, x):
    hi = x.astype(BF16)
    lo = (x - hi.astype(F32)).astype(BF16)
    return _nn(tri_b, hi) + _nn(tri_b, lo)


def _hgrn_col(seg, h):
    return slice(seg * HF + h * HGRN_DK, seg * HF + (h + 1) * HGRN_DK)


def _hgrn_gates(q, fz, lb):
    sg = _sigmoid(fz)
    return _sigmoid(q), sg, lb + (1.0 - lb) * sg


def _hgrn_decays(b, q, sq, f):
    qf = q * sq
    k = 1.0 - f
    bref = b[CHUNK // 2:CHUNK // 2 + 1, :]
    blast = b[CHUNK - 1:CHUNK, :]
    e1 = jnp.exp2(b - bref)
    e2 = jnp.exp2(bref - b)
    e3 = e1 * jnp.exp2(bref)
    e4 = e2 * jnp.exp2(blast - bref)
    return (e1, e2, e3, e4), qf * e1, k * e2, qf * e3, k * e4, jnp.exp2(blast)


def _hgrn_fwd(zh, lb, gn, *, name):
    B, S, _ = zh.shape
    cpb = HGRN_CPB
    ts = cpb * CHUNK
    nblk = S // ts

    def body(z_ref, lb_ref, gn_ref, o_ref, st_ref, state):
        @pl.when(pl.program_id(1) == 0)
        def _():
            state[...] = jnp.zeros_like(state)

        R = range(HGRN_HEADS)
        causal = _tri(CHUNK)
        tril_b = causal.astype(BF16)
        lbh = [lb_ref[:, _hs(h)] for h in R]
        for c in range(cpb):
            rows = slice(c * CHUNK, (c + 1) * CHUNK)
            q = [z_ref[rows, _hgrn_col(0, h)].astype(F32) for h in R]
            gates = [_hgrn_gates(q[h], z_ref[rows, _hgrn_col(1, h)].astype(F32), lbh[h]) for h in R]
            b = [_cumsum_rows(tril_b, jnp.log2(gates[h][2])) for h in R]
            v = [z_ref[rows, _hgrn_col(2, h)] for h in R]
            dec, q_in, k_in, q_out, k_st = [], [], [], [], []
            for h in R:
                _, qi, ki, qo, ks, d = _hgrn_decays(b[h], q[h], gates[h][0], gates[h][2])
                dec.append(d)
                for lst, t in zip((q_in, k_in, q_out, k_st), (qi, ki, qo, ks)):
                    lst.append(t.astype(BF16))
            a = [jnp.where(causal, _nt(q_in[h], k_in[h]), 0.0).astype(BF16) for h in R]
            st = [state[h] for h in R]
            for h in R:
                st_ref[c, h] = st[h]
            o = [_nn(a[h], v[h]) + _nt(q_out[h], st[h].astype(BF16)) for h in R]
            for h in R:
                state[h] = st[h] * dec[h] + _tn(v[h], k_st[h])
            for h in R:
                o_ref[rows, _hs(h)] = (_rms(o[h], gn_ref[...]) * _silu(z_ref[rows, _hgrn_col(3, h)].astype(F32))).astype(BF16)

    return pl.pallas_call(
        body, name=name, grid=(B, nblk),
        in_specs=[pl.BlockSpec((None, ts, 4 * HF), lambda b, s: (b, s, 0)),
                  pl.BlockSpec((1, HF), lambda b, s: (0, 0)),
                  pl.BlockSpec((1, HGRN_DK), lambda b, s: (0, 0))],
        out_specs=[pl.BlockSpec((None, ts, HF), lambda b, s: (b, s, 0)),
                   pl.BlockSpec((None, cpb, HGRN_HEADS, HGRN_DK, HGRN_DK), lambda b, s: (b, s, 0, 0, 0))],
        out_shape=[jax.ShapeDtypeStruct((B, S, HF), BF16),
                   jax.ShapeDtypeStruct((B, S // CHUNK, HGRN_HEADS, HGRN_DK, HGRN_DK), F32)],
        scratch_shapes=[pltpu.VMEM((HGRN_HEADS, HGRN_DK, HGRN_DK), F32)],
        compiler_params=_params("arbitrary", "arbitrary"),
    )(zh, lb, gn)


def _hgrn_bwd(zh, lb, gn, states, doa, dz, *, name):
    B, S, _ = zh.shape
    cpb = HGRN_CPB
    ts = cpb * CHUNK
    nblk = S // ts
    rev = lambda b, s: (b, nblk - 1 - s, 0)

    def body(z_ref, lb_ref, gn_ref, st_ref, do_ref, dz_in, dz_ref, dlb_ref, dgn_ref, dstate):
        @pl.when(pl.program_id(1) == 0)
        def _():
            dstate[...] = jnp.zeros_like(dstate)

        @pl.when((pl.program_id(0) == 0) & (pl.program_id(1) == 0))
        def _():
            dlb_ref[...] = jnp.zeros_like(dlb_ref)
            dgn_ref[...] = jnp.zeros_like(dgn_ref)

        R = range(HGRN_HEADS)
        causal = _tri(CHUNK)
        tril_b = causal.astype(BF16)
        triu_b = _tri(CHUNK, upper=True).astype(BF16)
        rowid = lax.broadcasted_iota(jnp.int32, (CHUNK, HGRN_DK), 0)
        lbh = [lb_ref[:, _hs(h)] for h in R]
        gn = gn_ref[...]
        for c in reversed(range(cpb)):
            rows = slice(c * CHUNK, (c + 1) * CHUNK)
            q = [z_ref[rows, _hgrn_col(0, h)].astype(F32) for h in R]
            gates = [_hgrn_gates(q[h], z_ref[rows, _hgrn_col(1, h)].astype(F32), lbh[h]) for h in R]
            b = [_cumsum_rows(tril_b, jnp.log2(gates[h][2])) for h in R]
            v = [z_ref[rows, _hgrn_col(2, h)] for h in R]
            pre = [_hgrn_decays(b[h], q[h], gates[h][0], gates[h][2]) for h in R]
            q_in_b, k_in_b, q_out_b, k_st_b = ([pre[h][i].astype(BF16) for h in R] for i in (1, 2, 3, 4))
            a_b = [jnp.where(causal, _nt(q_in_b[h], k_in_b[h]), 0.0).astype(BF16) for h in R]
            st = [st_ref[c, h] for h in R]
            st_b = [t.astype(BF16) for t in st]
            o = [_nn(a_b[h], v[h]) + _nt(q_out_b[h], st_b[h]) for h in R]
            do_l, dgn_acc = [], jnp.zeros_like(gn)
            for h in R:
                hg = z_ref[rows, _hgrn_col(3, h)].astype(F32)
                dout = do_ref[rows, _hs(h)].astype(F32)
                shg = _sigmoid(hg)
                on_h, norm_vjp = jax.vjp(_rms, o[h], gn)
                d_o, d_gn = norm_vjp(dout * (hg * shg))
                do_l.append(d_o)
                dgn_acc = dgn_acc + d_gn
                dz_ref[rows, _hgrn_col(3, h)] = (dout * on_h * shg * (1.0 + hg * (1.0 - shg))).astype(BF16)
            dgn_ref[...] += dgn_acc
            do_b = [t.astype(BF16) for t in do_l]
            dst = [dstate[h] for h in R]
            dst_b = [t.astype(BF16) for t in dst]
            da_b = [jnp.where(causal, _nt(do_b[h], v[h]), 0.0).astype(BF16) for h in R]
            dv = [_tn(a_b[h], do_b[h]) + _nt(k_st_b[h], dst_b[h]) for h in R]
            dq_in = [_nn(da_b[h], k_in_b[h]) for h in R]
            dk_in = [_tn(da_b[h], q_in_b[h]) for h in R]
            dq_out = [_nn(do_b[h], st_b[h]) for h in R]
            dk_st = [_nn(v[h], dst_b[h]) for h in R]
            for h in R:
                dz_ref[rows, _hgrn_col(2, h)] = dv[h].astype(BF16)
            db = []
            for h in R:
                _, q_in, k_in, q_out, k_st, dec = pre[h]
                ddec = jnp.sum(st[h] * dst[h], axis=0, keepdims=True)
                t_qin, t_kin, t_kst = dq_in[h] * q_in, dk_in[h] * k_in, dk_st[h] * k_st
                dbref = jnp.sum(t_kin - t_qin, axis=0, keepdims=True)
                dblast = jnp.sum(t_kst, axis=0, keepdims=True) + ddec * dec
                db.append(t_qin - t_kin + dq_out[h] * q_out - t_kst
                          + jnp.where(rowid == CHUNK // 2, dbref, 0.0) + jnp.where(rowid == CHUNK - 1, dblast, 0.0))
            for h in R:
                dstate[h] = dst[h] * pre[h][5] + _tn(do_b[h], q_out_b[h])
            dlogf = [_cumsum_rows(triu_b, db[h]) for h in R]
            for h in R:
                sq, sg, f = gates[h]
                e1, e2, e3, e4 = pre[h][0]
                dqf = dq_in[h] * e1 + dq_out[h] * e3
                dk = dk_in[h] * e2 + dk_st[h] * e4
                df_open = (dlogf[h] / f - dk) * (1.0 - sg)
                dlb_ref[:, _hs(h)] += jnp.sum(df_open, axis=0, keepdims=True)
                dz_ref[rows, _hgrn_col(1, h)] = (df_open * ((1.0 - lbh[h]) * sg)).astype(BF16)
                dz_ref[rows, _hgrn_col(0, h)] = (dqf * sq * (1.0 + q[h] * (1.0 - sq))).astype(BF16)

    return pl.pallas_call(
        body, name=name, grid=(B, nblk),
        in_specs=[pl.BlockSpec((None, ts, 4 * HF), rev),
                  pl.BlockSpec((1, HF), lambda b, s: (0, 0)),
                  pl.BlockSpec((1, HGRN_DK), lambda b, s: (0, 0)),
                  pl.BlockSpec((None, cpb, HGRN_HEADS, HGRN_DK, HGRN_DK), lambda b, s: (b, nblk - 1 - s, 0, 0, 0)),
                  pl.BlockSpec((None, ts, HF), rev),
                  ANY],
        out_specs=[pl.BlockSpec((None, ts, 4 * HF), rev),
                   pl.BlockSpec((1, HF), lambda b, s: (0, 0)),
                   pl.BlockSpec((1, HGRN_DK), lambda b, s: (0, 0))],
        out_shape=[jax.ShapeDtypeStruct(dz.shape, BF16),
                   jax.ShapeDtypeStruct((1, HF), F32),
                   jax.ShapeDtypeStruct((1, HGRN_DK), F32)],
        input_output_aliases={5: 0},
        scratch_shapes=[pltpu.VMEM((HGRN_HEADS, HGRN_DK, HGRN_DK), F32)],
        compiler_params=_params("arbitrary", "arbitrary"),
    )(zh, lb, gn, states, doa, dz)


KV_W = ATT_KV_HEADS * ATT_HD
ATT_SCALE = ATT_HD ** -0.5


def _rope(x, cos, sin, inverse=False):
    half = ROPE_DIM // 2
    outs = []
    for p in range(x.shape[1] // 128):
        xp = x[:, p * 128:(p + 1) * 128]
        lane = lax.broadcasted_iota(jnp.int32, xp.shape, 1) % ATT_HD
        sw = jnp.where(lane < half, pltpu.roll(xp, 128 - half, 1), pltpu.roll(xp, half, 1))
        outs.append(xp * cos - sw * sin if inverse else xp * cos + sw * sin)
    return outs[0] if len(outs) == 1 else jnp.concatenate(outs, axis=1)


PAIRS_PER_KV = ATT_GROUP // 2


def _swap_halves(x):
    return pltpu.roll(x, ATT_HD, 1)


def _kv_padded(t, low):
    sw = _swap_halves(t)
    zero = jnp.zeros_like(t)
    out = []
    for g in range(ATT_KV_HEADS):
        in_low, in_high = (t, sw) if g == 0 else (sw, t)
        out.append((jnp.where(low, in_low, zero).astype(BF16), jnp.where(low, zero, in_high).astype(BF16)))
    return out


def _swa_mask(first_block):
    qi = lax.broadcasted_iota(jnp.int32, (WINDOW, 2 * WINDOW), 0)
    mi = lax.broadcasted_iota(jnp.int32, (WINDOW, 2 * WINDOW), 1)
    band = (mi > qi) & (mi <= qi + WINDOW)
    return band & (jnp.logical_not(first_block) | (mi >= WINDOW))


def _swa_specs(nb):
    cur = lambda b, i: (b, i, 0)
    prev = lambda b, i: (b, jnp.maximum(i - 1, 0), 0)
    return cur, prev


def _swa_z_specs():
    q = pl.BlockSpec((None, WINDOW, W_AQ), lambda b, i: (b, i, O_AQ // W_AQ))
    kv_prev = pl.BlockSpec((None, WINDOW, W_AKV), lambda b, i: (b, jnp.maximum(i - 1, 0), O_AKV // W_AKV))
    kv_cur = pl.BlockSpec((None, WINDOW, W_AKV), lambda b, i: (b, i, O_AKV // W_AKV))
    return q, kv_prev, kv_cur


def _swa_fwd(z, cos, sin, sinks, *, name):
    B, S, _ = z.shape
    nb = S // WINDOW
    cur, prev = _swa_specs(nb)

    def body(q_ref, kvp_ref, kvc_ref, cp_ref, sp_ref, cc_ref, sc_ref, sink_ref, o_ref, lse_ref, qr_ref, kr_ref):
        cos_c, sin_c = cc_ref[...], sc_ref[...]
        q = (_rope(q_ref[...].astype(F32), cos_c, sin_c) * ATT_SCALE).astype(BF16)
        k = jnp.concatenate([_rope(kvp_ref[:, :KV_W].astype(F32), cp_ref[...], sp_ref[...]),
                             _rope(kvc_ref[:, :KV_W].astype(F32), cos_c, sin_c)], axis=0)
        qr_ref[...] = q
        kr_ref[...] = k[WINDOW:].astype(BF16)
        v = jnp.concatenate([kvp_ref[:, KV_W:], kvc_ref[:, KV_W:]], axis=0).astype(F32)
        low = lax.broadcasted_iota(jnp.int32, k.shape, 1) < ATT_HD
        kpad = _kv_padded(k, low)
        vpad = _kv_padded(v, low)
        mask = _swa_mask(pl.program_id(1) == 0)
        lses = []
        for g in range(ATT_KV_HEADS):
            pairs = range(g * PAIRS_PER_KV, (g + 1) * PAIRS_PER_KV)
            keys = [(p, e) for p in pairs for e in (0, 1)]
            qp = {p: q[:, p * 128:(p + 1) * 128] for p in pairs}
            s = {pe: jnp.where(mask, _nt(qp[pe[0]], kpad[g][pe[1]]), NEG_INF) for pe in keys}
            pr = {}
            for pe in keys:
                sink = sink_ref[0, 2 * pe[0] + pe[1]]
                m = jnp.maximum(jnp.max(s[pe], axis=1, keepdims=True), sink)
                ex = jnp.exp(s[pe] - m)
                den = jnp.sum(ex, axis=1, keepdims=True) + jnp.exp(sink - m)
                pr[pe] = (ex * (1.0 / den)).astype(BF16)
                lses.append(m + jnp.log(den))
            for p in pairs:
                o_ref[:, p * 128:(p + 1) * 128] = (_nn(pr[p, 0], vpad[g][0]) + _nn(pr[p, 1], vpad[g][1])).astype(BF16)
        lse_ref[...] = jnp.concatenate(lses, axis=1)

    tab = lambda im: pl.BlockSpec((None, WINDOW, 128), im)
    return pl.pallas_call(
        body, name=name, grid=(B, nb),
        in_specs=[*_swa_z_specs(),
                  tab(prev), tab(prev), tab(cur), tab(cur),
                  pl.BlockSpec(memory_space=pltpu.SMEM)],
        out_specs=[pl.BlockSpec((None, WINDOW, D_MODEL), cur), pl.BlockSpec((None, WINDOW, ATT_HEADS), cur),
                   pl.BlockSpec((None, WINDOW, D_MODEL), cur), pl.BlockSpec((None, WINDOW, KV_W), cur)],
        out_shape=[jax.ShapeDtypeStruct((B, S, D_MODEL), BF16), jax.ShapeDtypeStruct((B, S, ATT_HEADS), F32),
                   jax.ShapeDtypeStruct((B, S, D_MODEL), BF16), jax.ShapeDtypeStruct((B, S, KV_W), BF16)],
        compiler_params=_params("parallel", "parallel"),
    )(z, z, z, cos, sin, cos, sin, sinks)


def _swa_bwd(z, qr, kr, cos, sin, sinks, lse, dob, dz, *, name):
    B, S, _ = z.shape
    nb = S // WINDOW
    cur, prev = _swa_specs(nb)

    def body(q_ref, krp_ref, krc_ref, kvp_ref, kvc_ref, cp_ref, sp_ref, cc_ref, sc_ref, sink_ref, lse_ref, do_ref, dz_in,
             dq_ref, dkc_ref, dkp_ref, dsink_ref):
        @pl.when((pl.program_id(0) == 0) & (pl.program_id(1) == 0))
        def _():
            dsink_ref[...] = jnp.zeros_like(dsink_ref)

        cos_c, sin_c, cos_p, sin_p = cc_ref[...], sc_ref[...], cp_ref[...], sp_ref[...]
        q = q_ref[...]
        k = jnp.concatenate([krp_ref[...], krc_ref[...]], axis=0).astype(F32)
        v = jnp.concatenate([kvp_ref[:, KV_W:], kvc_ref[:, KV_W:]], axis=0).astype(F32)
        low = lax.broadcasted_iota(jnp.int32, k.shape, 1) < ATT_HD
        kpad = _kv_padded(k, low)
        vpad = _kv_padded(v, low)
        mask = _swa_mask(pl.program_id(1) == 0)
        lse = lse_ref[...]
        dq_parts, dk_sum, dv_sum, dsinks = [], [], [], []
        for g in range(ATT_KV_HEADS):
            pairs = range(g * PAIRS_PER_KV, (g + 1) * PAIRS_PER_KV)
            keys = [(p, e) for p in pairs for e in (0, 1)]
            qp = {p: q[:, p * 128:(p + 1) * 128] for p in pairs}
            dop = {p: do_ref[:, p * 128:(p + 1) * 128] for p in pairs}
            s = {pe: jnp.where(mask, _nt(qp[pe[0]], kpad[g][pe[1]]), NEG_INF) for pe in keys}
            dp = {pe: _nt(dop[pe[0]], vpad[g][pe[1]]) for pe in keys}
            pr, ds = {}, {}
            for pe in keys:
                h = 2 * pe[0] + pe[1]
                lse_h = lse[:, h:h + 1]
                pf = jnp.exp(s[pe] - lse_h)
                delta = jnp.sum(pf * dp[pe], axis=1, keepdims=True)
                ds[pe] = (pf * (dp[pe] - delta)).astype(BF16)
                pr[pe] = pf.astype(BF16)
                p_sink = jnp.exp(sink_ref[0, h] - lse_h)
                dsinks.append(-jnp.sum(p_sink * delta, axis=0, keepdims=True))
            for p in pairs:
                dq_parts.append((_nn(ds[p, 0], kpad[g][0]) + _nn(ds[p, 1], kpad[g][1])) * ATT_SCALE)
            x = [sum(_tn(ds[p, e], qp[p]) for p in pairs) for e in (0, 1)]
            y = [sum(_tn(pr[p, e], dop[p]) for p in pairs) for e in (0, 1)]
            zk = jnp.where(low, x[0], x[1])
            zv = jnp.where(low, y[0], y[1])
            dk_sum.append(zk + _swap_halves(zk))
            dv_sum.append(zv + _swap_halves(zv))
        dq_ref[...] = _rope(jnp.concatenate(dq_parts, axis=1), cos_c, sin_c, inverse=True).astype(BF16)
        dk = jnp.where(low, dk_sum[0], dk_sum[1])
        dv = jnp.where(low, dv_sum[0], dv_sum[1])
        dkp_ref[:, :KV_W] = _rope(dk[:WINDOW], cos_p, sin_p, inverse=True)
        dkp_ref[:, KV_W:] = dv[:WINDOW]
        dkc_ref[:, :KV_W] = _rope(dk[WINDOW:], cos_c, sin_c, inverse=True)
        dkc_ref[:, KV_W:] = dv[WINDOW:]
        dsink_ref[...] += jnp.concatenate(dsinks, axis=1)

    tab = lambda im: pl.BlockSpec((None, WINDOW, 128), im)
    return pl.pallas_call(
        body, name=name, grid=(B, nb),
        in_specs=[pl.BlockSpec((None, WINDOW, D_MODEL), cur), tab(prev), tab(cur),
                  *_swa_z_specs()[1:],
                  tab(prev), tab(prev), tab(cur), tab(cur),
                  pl.BlockSpec(memory_space=pltpu.SMEM),
                  pl.BlockSpec((None, WINDOW, ATT_HEADS), cur),
                  pl.BlockSpec((None, WINDOW, D_MODEL), cur),
                  ANY],
        out_specs=[_swa_z_specs()[0],
                   pl.BlockSpec((None, WINDOW, 2 * KV_W), cur), pl.BlockSpec((None, WINDOW, 2 * KV_W), cur),
                   pl.BlockSpec((1, ATT_HEADS), lambda b, i: (0, 0))],
        out_shape=[jax.ShapeDtypeStruct(dz.shape, BF16),
                   jax.ShapeDtypeStruct((B, S, 2 * KV_W), F32), jax.ShapeDtypeStruct((B, S, 2 * KV_W), F32),
                   jax.ShapeDtypeStruct((1, ATT_HEADS), F32)],
        input_output_aliases={12: 0},
        compiler_params=_params("arbitrary", "arbitrary"),
    )(qr, kr, kr, z, z, cos, sin, cos, sin, sinks, lse, dob, dz)


def _swa_dkv_combine(dkv_cur, dkv_prev, dz, *, name):
    B, S, W = dkv_cur.shape

    def body(c_ref, p_ref, dz_in, o_ref):
        rows = lax.broadcasted_iota(jnp.int32, (S, W), 0)
        o_ref[...] = (c_ref[...] + _shift_up(p_ref[...], WINDOW, rows, S)).astype(BF16)

    seq = pl.BlockSpec((None, S, W), lambda b: (b, 0, 0))
    return pl.pallas_call(
        body, name=name, grid=(B,),
        in_specs=[seq, seq, ANY], out_specs=pl.BlockSpec((None, S, W), lambda b: (b, 0, O_AKV // W_AKV)),
        out_shape=jax.ShapeDtypeStruct(dz.shape, BF16),
        input_output_aliases={2: 0},
        compiler_params=_params("parallel"),
    )(dkv_cur, dkv_prev, dz)


def _rope_tables(positions):
    half = ROPE_DIM // 2
    inv = ROPE_THETA ** (-2.0 * jnp.arange(half, dtype=F32) / ROPE_DIM)
    ang = positions.astype(F32)[..., None] * inv
    c, s = jnp.cos(ang), jnp.sin(ang)
    pad = jnp.zeros(ang.shape[:-1] + (ATT_HD - ROPE_DIM,), F32)
    cos = jnp.concatenate([c, c, pad + 1.0], axis=-1)
    sin = jnp.concatenate([-s, s, pad], axis=-1)
    return jnp.tile(cos, (1, 1, 2)), jnp.tile(sin, (1, 1, 2))


def _lower_bound(lb_logits, *, name):
    def body(l_ref, o_ref):
        l = l_ref[...]
        e = jnp.exp(l - jnp.max(l, axis=0, keepdims=True))
        o_ref[...] = e[0:1] / jnp.sum(e, axis=0, keepdims=True)

    return pl.pallas_call(body, name=name, out_shape=jax.ShapeDtypeStruct((1, lb_logits.shape[1]), F32))(lb_logits)


W_ZH, W_GATES, W_AQ, W_AKV = 4 * HF, 2 * D_MODEL, ATT_HEADS * ATT_HD, 2 * KV_W
O_ZH, O_GATES, O_AQ, O_AKV = 0, W_ZH, W_ZH + W_GATES, W_ZH + W_GATES + W_AQ
W_IN = W_ZH + W_GATES + W_AQ + W_AKV


W_IN_BLK = W_IN // N_DEV


def _reference_row_block(j, rows=256):
    nz, ng = W_ZH // rows, W_GATES // rows
    return jnp.where(j < nz, j, jnp.where(j < nz + ng, j + (W_AQ + W_AKV) // rows, j - ng))


W_IN_SEGMENTS = ((O_ZH, 0, W_ZH), (O_GATES, W_ZH + W_AQ + W_AKV, W_GATES), (O_AQ, W_ZH, W_AQ + W_AKV))


def _local_step(x, positions, target, small, w_in_t, rest_weights, emit, start_token):
    B, S, D = x.shape
    T = B * S
    x2 = x.reshape(T, D)
    cos, sin = _rope_tables(positions)
    lb = _lower_bound(small["lb_logits"], name="lb_fwd")
    zero = lambda tok: tok[0:1, 0:1]

    u1, z = _norm_matmul(x2, small["norm1_g"] + zero(start_token), w_in_t, W_IN_SEGMENTS, tm=512, name="norm1_mm_z")
    z3 = z.reshape(B, S, W_IN)
    oa, states = _hgrn_fwd(z3, lb, small["hgrn_norm_g"], name="hgrn_fwd")
    ob, lse, qr, kr = _swa_fwd(z3, cos, sin, small["attn_sinks"], name="swa_fwd")
    oa2 = oa.reshape(T, D)
    ob2 = ob.reshape(T, D)
    W = rest_weights("mix", ob)
    row = lambda tm, dtype=None: _row_spec(tm, D)
    tile = lambda dtype: jax.ShapeDtypeStruct((T, D), dtype)
    vec = _full_spec((1, D))
    vec_shape = jax.ShapeDtypeStruct((1, D), F32)

    def merge_ep(acc_a, acc_b, g_ref):
        pa, pb = acc_a.astype(BF16), acc_b.astype(BF16)
        return pa, pb, _merge_fn(g_ref[...], pa, pb)

    pa, pb, merged = _matmul_ep([(oa2, W["w_a"], False, 0), (ob2, W["w_b"], False, 0)], tm=1024, ins=[z], in_specs=[_gates_spec(1024)],
                                out_shapes=[tile(BF16)] * 3, out_specs=[row(1024)] * 3, epilogue=merge_ep, name="mm_pa_pb_merge")

    def resid_norm_ep(acc, x_ref, g_ref):
        hh = acc + x_ref[...]
        return hh, _rms(hh, g_ref[...])

    h, u2 = _matmul_ep([(merged, W["w_out"], False, 0)], tm=1024, ins=[x2, small["norm2_g"]], in_specs=[row(1024), vec],
                       out_shapes=[tile(F32), tile(BF16)], out_specs=[row(1024), row(1024)], epilogue=resid_norm_ep, name="mm_h_norm2")
    W.update(rest_weights("ffn", u2))
    gu3 = _matmul_col_tiles(u2, W["w_ffn_t"], tm=1024, tn=D_FF, tc=CONV_TC, name="mm_gu").reshape(2 * D_FF // CONV_TC, B, S, CONV_TC)
    act, a_pre = _conv_act_fwd(gu3, W["conv_w"], small["conv_b"], name="conv_act_fwd")
    act2 = act.reshape(T, D_FF)
    g = {}

    def loss_ep(acc, h_ref, g_ref, t_ref):
        y, vjp = jax.vjp(_rms, acc + h_ref[...], g_ref[...])
        err = y - t_ref[...]
        dx, dg = vjp(err * (1.0 / D))
        return dx, dx, dg, (0.5 / D) * jnp.sum(jnp.sum(err * err, axis=1, keepdims=True), axis=0, keepdims=True)

    dh2, dh2b, g["final_g"], loss = _matmul_ep(
        [(act2, W["w_down"], False, 0)], tm=512, ins=[h, small["final_g"].reshape(1, D), target.reshape(T, D)], in_specs=[row(512), vec, row(512)],
        out_shapes=[tile(F32), tile(BF16), vec_shape, jax.ShapeDtypeStruct((1, 1), F32)],
        out_specs=[row(512), row(512), vec, _full_spec((1, 1))], sums=(2, 3), epilogue=loss_ep, name="mm_h2_loss")
    dact = _matmul(dh2b, W["w_down"], tb=True, out_dtype=BF16, name="mm_dact", tm=1024, tn=D_FF)
    dw_down_t = _matmul(dh2b, act2, ta=True, out_dtype=BF16, name="mm_dw_down", tm=1024, tn=256, tk=8192)
    dg_, dup, g["conv_w"], g["conv_b"] = _conv_act_bwd(gu3, a_pre, W["conv_w"], dact.reshape(B, S, D_FF), name="conv_act_bwd")
    dg2 = dg_.reshape(T, D_FF)
    dup2 = dup.reshape(T, D_FF)
    dw_ffn_t = _matmul(u2, dg2, ta=True, out_t=True, out_dtype=BF16, into=lax.empty((2 * D_FF, D), BF16), o_noff=0, name="mm_dw_ffn_g", tm=1024, tn=256, tk=8192)
    dw_ffn_t = _matmul(u2, dup2, ta=True, out_t=True, out_dtype=BF16, into=dw_ffn_t, o_noff=D_FF // 256, name="mm_dw_ffn_u", tm=1024, tn=256, tk=8192)
    tok = emit("ffn", dict(w_ffn_t=dw_ffn_t, w_down=dw_down_t.T))
    def norm2_bwd_ep(acc_g, acc_u, h_ref, g_ref, dh2_ref):
        _, vjp = jax.vjp(_rms, h_ref[...], g_ref[...])
        dx, dg = vjp(acc_g + acc_u)
        dx = dx + dh2_ref[...]
        return dx, dx, dg

    dh, dhb, g["norm2_g"] = _matmul_ep(
        [(dg2, W["w_ffn_t"], False, 0), (dup2, W["w_ffn_t"], False, 1)], tm=512, ins=[h, small["norm2_g"] + zero(tok), dh2], in_specs=[row(512), vec, row(512)],
        out_shapes=[tile(F32), tile(BF16), vec_shape], out_specs=[row(512), row(512), vec], sums=(2,), epilogue=norm2_bwd_ep, name="mm_du2_norm2_bwd")
    dw_out = _matmul(merged, dhb, ta=True, out_dtype=BF16, name="mm_dw_out", tm=1024, tn=1024, tk=2048)

    def merge_bwd_ep(acc, g_ref, pa_ref, pb_ref, dz_in):
        gt = g_ref[...].astype(F32)
        sa = _sigmoid(gt[:, :D_MODEL])
        sb = _sigmoid(gt[:, D_MODEL:])
        dgates = jnp.concatenate([acc * pa_ref[...].astype(F32) * sa * (1.0 - sa), acc * pb_ref[...].astype(F32) * sb * (1.0 - sb)], axis=1)
        return dgates, acc * sa, acc * sb

    dz, dpa, dpb = _matmul_ep(
        [(dhb, W["w_out"], True, 0)], tm=512, ins=[z, pa, pb, lax.empty((T, W_IN), BF16)], in_specs=[_gates_spec(512), row(512), row(512), ANY],
        out_shapes=[jax.ShapeDtypeStruct((T, W_IN), BF16), tile(BF16), tile(BF16)], out_specs=[_gates_spec(512), row(512), row(512)],
        aliases={3: 0}, epilogue=merge_bwd_ep, name="mm_dmerged_merge_bwd")
    doa, dob = _matmul_ep([(dpa, W["w_a"], True, 0), (dpb, W["w_b"], True, 0)], tm=1024, ins=[], in_specs=[],
                          out_shapes=[tile(BF16)] * 2, out_specs=[row(1024)] * 2, epilogue=lambda da, db: (da, db), name="mm_doa_dob")
    dw_a = _matmul(oa2, dpa, ta=True, out_dtype=BF16, name="mm_dw_a", tm=1024, tn=1024, tk=2048)
    dw_b = _matmul(ob2, dpb, ta=True, out_dtype=BF16, name="mm_dw_b", tm=1024, tn=1024, tk=2048)
    tok = emit("mix", dict(w_out=dw_out, w_a=dw_a, w_b=dw_b))
    dz3, dkv_cur, dkv_prev, dsinks = _swa_bwd(z3, qr, kr, cos, sin, small["attn_sinks"] + zero(tok), lse, dob.reshape(B, S, D),
                                              dz.reshape(B, S, W_IN), name="swa_bwd")
    dz3 = _swa_dkv_combine(dkv_cur, dkv_prev, dz3, name="swa_dkv")
    g["attn_sinks"] = dsinks
    dz3, g["lb"], g["hgrn_norm_g"] = _hgrn_bwd(z3, lb, small["hgrn_norm_g"], states, doa.reshape(B, S, D), dz3, name="hgrn_bwd")
    dz = dz3.reshape(T, W_IN)
    dw_in_t = _matmul(u1, dz, ta=True, out_t=True, o_block_perm=_reference_row_block, out_dtype=BF16, name="mm_dw_in", tm=1024, tn=256, tk=8192)
    tok = emit("in", dict(w_in_t=dw_in_t))
    dx, g["norm1_g"] = _matmul_norm_bwd(dz, w_in_t, W_IN_SEGMENTS, x2, small["norm1_g"], dh, tok, tm=512, name="mm_du1_norm1_bwd")
    g["lb_logits"] = _lb_bwd(g.pop("lb"), lb, name="lb_bwd")
    return loss, dx.reshape(B, S, D), g


def _my_place():
    return lax.axis_index("x"), lax.axis_index("y"), lax.axis_index("c")


def _gather_blocks(x_ref, out_ref, send_sems, recv_sems, local_sem):
    x, y, c = _my_place()
    me, sibling = (x, y, c), (x, y, 1 - c)
    chips = [(1 - x, y), (x, 1 - y), (1 - x, 1 - y)]
    relayed = tuple(jnp.where(c == 0, a, b) for a, b in zip(chips[0], chips[1]))
    relay_to = tuple(jnp.where(c == 0, b, a) for a, b in zip(chips[0], chips[1]))

    def slot(px, py, pc):
        return out_ref.at[4 * px + 2 * py + pc]

    def copy(k, block, to, src=None):
        return pltpu.make_async_remote_copy(
            src_ref=slot(*block) if src is None else src, dst_ref=slot(*block),
            send_sem=send_sems.at[k], recv_sem=recv_sems.at[k], device_id=to, device_id_type=MESH)

    mine = pltpu.make_async_copy(x_ref, slot(*me), local_sem)
    mine.start()
    first = [copy(0, me, sibling, src=x_ref)]
    first += [copy(1 + j, me, (*chip, c), src=x_ref) for j, chip in enumerate(chips[:2])]
    for cp in first:
        cp.start()
    relay = copy(3, (*relayed, c), (*relay_to, c))
    passed = [copy(4 + j, (*chip, c), sibling) for j, chip in enumerate(chips)]
    for j, chip in enumerate(chips):
        copy(1 + j, (*chip, c), me).wait_recv()
        if j < 2:
            @pl.when(c == j)
            def _():
                relay.start()

        passed[j].start()
    copy(0, sibling, me).wait_recv()
    for j, chip in enumerate(chips):
        copy(4 + j, (*chip, 1 - c), me).wait_recv()
    for cp in first + [relay] + passed:
        cp.wait_send()
    mine.wait()


GATHER_SEMS = [pltpu.SemaphoreType.DMA((7,)), pltpu.SemaphoreType.DMA((7,)), pltpu.SemaphoreType.DMA]


def _all_gather(blk, *, name):
    return pl.pallas_call(
        _gather_body_fn(), name=name,
        out_shape=jax.ShapeDtypeStruct((N_DEV,) + blk.shape, blk.dtype),
        in_specs=[ANY], out_specs=ANY,
        scratch_shapes=GATHER_SEMS,
    )(blk)


def _gather_body_fn():
    def body(x_ref, out_ref, send_sems, recv_sems, local_sem):
        _gather_blocks(x_ref, out_ref, send_sems, recv_sems, local_sem)
    return body


SLAB_W = 1152
SMALL_SHAPES = dict(norm1_g=(1, D_MODEL), lb_logits=(2, HGRN_HEADS * HGRN_DK), hgrn_norm_g=(1, HGRN_DK), attn_sinks=(1, ATT_HEADS),
                    norm2_g=(1, D_MODEL), conv_b=(1, D_FF), final_g=(1, D_MODEL))
CONVW_BLK = D_FF // N_DEV
CONVW_STRIDE = SLAB_W // 3


def _slab_layout():
    layout, r = {}, 0
    for nm, (nr, w) in SMALL_SHAPES.items():
        layout[nm] = []
        for i in range(nr):
            for c0 in range(0, w, SLAB_W):
                layout[nm].append((r, i, c0, min(SLAB_W, w - c0)))
                r += 1
    return layout, r


SMALL_ROWS, _N_SMALL_ROWS = _slab_layout()
CONV_ROW0 = -(-_N_SMALL_ROWS // 8) * 8
LOSS_ROW = CONV_ROW0 + N_DEV
SLAB_ROWS = LOSS_ROW + 8


def _small_step(grads, g_conv_w, loss, params, moments, variances, dev, *, name):
    names = list(SMALL_ROWS)
    n = len(names)

    def body(dev_ref, *refs):
        g_refs = dict(zip(names, refs[:n]))
        gc_ref, loss_ref = refs[n], refs[n + 1]
        base = n + 2
        w_refs, m_refs, v_refs = (dict(zip(names + ["conv_w"], refs[base + i * (n + 1):base + (i + 1) * (n + 1)])) for i in range(3))
        o = base + 3 * (n + 1)
        gath_ref, loss_out = refs[o], refs[o + 1]
        outs = {nm: refs[o + 2 + 4 * i:o + 6 + 4 * i] for i, nm in enumerate(names + ["conv_w"])}
        slab, total, send_sems, recv_sems, local_sem = refs[-5:]

        slab[...] = jnp.zeros_like(slab)
        for nm, pieces in SMALL_ROWS.items():
            for r, i, c0, w in pieces:
                slab[r:r + 1, 0:w] = g_refs[nm][i:i + 1, c0:c0 + w]
        for p in range(N_DEV):
            for j in range(3):
                slab[CONV_ROW0 + p:CONV_ROW0 + p + 1, j * CONVW_STRIDE:j * CONVW_STRIDE + CONVW_BLK] = gc_ref[j:j + 1, p * CONVW_BLK:(p + 1) * CONVW_BLK]
        slab[LOSS_ROW:LOSS_ROW + 1, 0:1] = loss_ref[...]
        _gather_blocks(slab, gath_ref, send_sems, recv_sems, local_sem)
        acc = gath_ref[0]
        for p in range(1, N_DEV):
            acc = acc + gath_ref[p]
        total[...] = acc
        loss_out[...] = total[LOSS_ROW:LOSS_ROW + 1, 0:1]

        def update(nm, g, i, c0, w):
            at = (slice(i, i + 1), slice(c0, c0 + w))
            d, mn, vn = _adamw_math(w_refs[nm][at], g, m_refs[nm][at], v_refs[nm][at])
            for ref, val in zip(outs[nm], (g, d, mn, vn)):
                ref[at] = val

        for nm, pieces in SMALL_ROWS.items():
            for r, i, c0, w in pieces:
                update(nm, total[r:r + 1, 0:w], i, c0, w)
        conv_rows = total[CONV_ROW0:CONV_ROW0 + N_DEV, :]
        rowid = lax.broadcasted_iota(jnp.int32, conv_rows.shape, 0)
        mine = jnp.sum(jnp.where(rowid == dev_ref[0], conv_rows, 0.0), axis=0, keepdims=True)
        for j in range(3):
            update("conv_w", mine[:, j * CONVW_STRIDE:j * CONVW_STRIDE + CONVW_BLK], j, 0, CONVW_BLK)

    order = names + ["conv_w"]
    ins = [grads[nm] for nm in names] + [g_conv_w, loss]
    for d in (params, moments, variances):
        ins += [d[nm] for nm in order]
    vmem = pl.BlockSpec(memory_space=pltpu.VMEM)
    out_shape = [jax.ShapeDtypeStruct((N_DEV, SLAB_ROWS, SLAB_W), F32), jax.ShapeDtypeStruct((1, 1), F32)]
    for nm in order:
        out_shape += [jax.ShapeDtypeStruct(params[nm].shape, F32)] * 4
    res = pl.pallas_call(
        body, name=name,
        grid_spec=pltpu.PrefetchScalarGridSpec(
            num_scalar_prefetch=1, grid=(1,),
            in_specs=[vmem] * len(ins), out_specs=[vmem] * len(out_shape),
            scratch_shapes=[pltpu.VMEM((SLAB_ROWS, SLAB_W), F32), pltpu.VMEM((SLAB_ROWS, SLAB_W), F32)] + GATHER_SEMS),
        out_shape=out_shape,
    )(dev, *ins)
    return res[1], {nm: tuple(res[2 + 4 * i:6 + 4 * i]) for i, nm in enumerate(order)}


HBM_SPEC = pl.BlockSpec(memory_space=pltpu.HBM)
SEM_SPEC = pl.BlockSpec(memory_space=pltpu.SEMAPHORE)
DATAFLOW_EFFECT = pltpu.SideEffectType.DATAFLOW_SIDE_EFFECTING
N_PEERS = N_DEV - 1


def _peers(x, y, c):
    return [(1 - x if r & 4 else x, 1 - y if r & 2 else y, 1 - c if r & 1 else c) for r in range(1, N_DEV)]


def _exchange_start(srcs, scatter, *, after=None, name):
    n = len(srcs)
    lands = [lax.empty(a.shape if scatter else (N_DEV,) + a.shape, a.dtype) for a in srcs]
    extra = [] if after is None else [after]

    def body(*refs):
        src_refs, land_refs = refs[:n], refs[n:2 * n]
        send_sems, recv_sems, token = refs[2 * n + len(extra)], refs[2 * n + len(extra) + 1], refs[-1]
        x, y, c = _my_place()
        me = 4 * x + 2 * y + c
        for i in range(n):
            for r, (tx, ty, tc) in enumerate(_peers(x, y, c)):
                src = src_refs[i].at[4 * tx + 2 * ty + tc] if scatter else src_refs[i]
                pltpu.make_async_remote_copy(
                    src_ref=src, dst_ref=land_refs[i].at[me], send_sem=send_sems.at[N_PEERS * i + r],
                    recv_sem=recv_sems.at[N_PEERS * i + r], device_id=(tx, ty, tc), device_id_type=MESH).start()
        token[...] = jnp.zeros_like(token)

    thru = [pltpu.HBM(a.shape, a.dtype) for a in list(srcs) + lands]
    res = pl.pallas_call(
        body, name=name,
        out_shape=(pltpu.SemaphoreType.DMA((N_PEERS * n,)), pltpu.SemaphoreType.DMA((N_PEERS * n,)), *thru,
                   jax.ShapeDtypeStruct((8, 128), F32)),
        in_specs=[HBM_SPEC] * (2 * n) + [ANY] * len(extra),
        out_specs=(SEM_SPEC, SEM_SPEC, *([HBM_SPEC] * (2 * n)), pl.BlockSpec(memory_space=pltpu.VMEM)),
        input_output_aliases={i: 2 + i for i in range(2 * n)},
        compiler_params=pltpu.CompilerParams(has_side_effects=DATAFLOW_EFFECT),
    )(*[pltpu.with_memory_space_constraint(a, pltpu.HBM) for a in list(srcs) + lands], *extra)
    return (res[0], res[1], list(res[2:2 + n]), list(res[2 + n:2 + 2 * n]), scatter), res[-1]


def _exchange_wait(handle, after, *, name):
    send_sems, recv_sems, srcs, lands, scatter = handle
    n = len(srcs)

    def body(*refs):
        src_refs, land_refs = refs[:n], refs[n:2 * n]
        send_sems, recv_sems = refs[2 * n], refs[2 * n + 1]
        x, y, c = _my_place()
        for i in range(n):
            for r in range(N_PEERS):
                src = src_refs[i].at[0] if scatter else src_refs[i]
                cp = pltpu.make_async_remote_copy(
                    src_ref=src, dst_ref=land_refs[i].at[0], send_sem=send_sems.at[N_PEERS * i + r],
                    recv_sem=recv_sems.at[N_PEERS * i + r], device_id=(x, y, c), device_id_type=MESH)
                cp.wait_send()
                cp.wait_recv()

    thru = [pltpu.HBM(a.shape, a.dtype) for a in srcs + lands]
    res = pl.pallas_call(
        body, name=name, out_shape=tuple(thru),
        in_specs=[HBM_SPEC] * (2 * n) + [SEM_SPEC, SEM_SPEC, ANY], out_specs=tuple([HBM_SPEC] * (2 * n)),
        input_output_aliases={i: i for i in range(2 * n)},
        compiler_params=pltpu.CompilerParams(has_side_effects=DATAFLOW_EFFECT),
    )(*srcs, *lands, send_sems, recv_sems, after)
    return list(res[:n]), list(res[n:])


def _with_own(land, own, me):
    return lax.dynamic_update_index_in_dim(land, own, me, 0)


def _adamw_math(w, g, m, v):
    m = ADAM_B1 * m + (1.0 - ADAM_B1) * g
    v = ADAM_B2 * v + (1.0 - ADAM_B2) * (g * g)
    m_hat = m / (1.0 - ADAM_B1 ** ADAM_STEP)
    v_hat = v / (1.0 - ADAM_B2 ** ADAM_STEP)
    delta = -ADAM_LR * (m_hat / (jnp.sqrt(v_hat) + ADAM_EPS) + ADAM_WD * w)
    return delta, m, v


def _adamw_sum(parts, w, m, v, *, name):
    shape = w.shape
    R, n = shape[-2], shape[-1]
    w, m, v = (t.reshape(R, n) for t in (w, m, v))
    tr = _pick(R, (256, 464, 352, 128))

    def body(p_ref, w_ref, m_ref, v_ref, g_ref, d_ref, mo_ref, vo_ref):
        g = p_ref[0].astype(F32)
        for p in range(1, N_DEV):
            g = g + p_ref[p].astype(F32)
        d, mn, vn = _adamw_math(w_ref[...], g, m_ref[...], v_ref[...])
        g_ref[...] = g
        d_ref[...] = d
        mo_ref[...] = mn
        vo_ref[...] = vn

    row = pl.BlockSpec((tr, n), lambda i: (i, 0))
    outs = pl.pallas_call(
        body, name=name, grid=(R // tr,),
        in_specs=[pl.BlockSpec((N_DEV, tr, n), lambda i: (0, i, 0)), row, row, row],
        out_specs=[row, row, row, row],
        out_shape=[jax.ShapeDtypeStruct((R, n), F32)] * 4,
        compiler_params=_params("parallel"),
    )(parts, w, m, v)
    return [t.reshape(shape) for t in outs]


def _lb_bwd(dlb, lb, *, name):
    def body(d_ref, lb_ref, o_ref):
        t = d_ref[...] * lb_ref[...] * (1.0 - lb_ref[...])
        o_ref[0:1, :] = t
        o_ref[1:2, :] = -t

    return pl.pallas_call(body, name=name, out_shape=jax.ShapeDtypeStruct((2, lb.shape[1]), F32))(dlb, lb)


DOWN_BLK, ROW_BLK = D_FF // N_DEV, D_MODEL // N_DEV
W_FFN_BLK = 2 * D_FF // N_DEV
CONV_BITS_SHAPE = (16, 256)


def kernel(x, positions, norm1_g, w_in, lb_logits, hgrn_norm_g, w_a, attn_sinks, w_b, w_out, norm2_g, w_ffn_in, conv_w, conv_b, w_down, final_g, loss_target, m_norm1_g, m_w_in, m_lb_logits, m_hgrn_norm_g, m_w_a, m_attn_sinks, m_w_b, m_w_out, m_norm2_g, m_w_ffn_in, m_conv_w, m_conv_b, m_w_down, m_final_g, v_norm1_g, v_w_in, v_lb_logits, v_hgrn_norm_g, v_w_a, v_attn_sinks, v_w_b, v_w_out, v_norm2_g, v_w_ffn_in, v_conv_w, v_conv_b, v_w_down, v_final_g):
    xi, yi, ci = _my_place()
    dev = 4 * xi + 2 * yi + ci

    tr = lambda t: jnp.transpose(t[0])
    untr = lambda t: jnp.transpose(t)[None]
    w_in_blocks = _all_gather(tr(w_in).astype(BF16), name="ag_w_in")
    conv_bits = lax.bitcast_convert_type(conv_w, BF16).reshape(-1)
    conv_bits = jnp.pad(conv_bits, (0, CONV_BITS_SHAPE[0] * CONV_BITS_SHAPE[1] - conv_bits.shape[0])).reshape(CONV_BITS_SHAPE)
    w_in_full_t = w_in_blocks.reshape(W_IN, D_MODEL)
    gather_handles = {}
    gather_handles["mix"], tok_mix = _exchange_start([w_a[0].astype(BF16), w_b[0].astype(BF16), w_out[0].astype(BF16)], False,
                                                     after=w_in_full_t, name="ag_mix_start")
    gather_handles["ffn"], tok_ffn = _exchange_start([tr(w_ffn_in).astype(BF16), w_down[0].astype(BF16), conv_bits], False,
                                                     after=tok_mix, name="ag_ffn_start")
    start_token = tok_mix + tok_ffn

    def rest_weights(group, after):
        own, lands = _exchange_wait(gather_handles[group], after, name="ag_" + group + "_wait")
        full = [_with_own(l, o, dev) for l, o in zip(lands, own)]
        if group == "mix":
            return dict(zip(("w_a", "w_b", "w_out"), [t.reshape(D_MODEL, D_MODEL) for t in full]))
        bits = full[2].reshape(N_DEV, -1)[:, :3 * CONVW_BLK * 2].reshape(N_DEV, 3, CONVW_BLK, 2)
        return dict(w_ffn_t=full[0].reshape(2 * D_FF, D_MODEL), w_down=full[1].reshape(D_FF, D_MODEL),
                    conv_w=lax.bitcast_convert_type(bits, F32).transpose(1, 0, 2).reshape(3, D_FF))

    handles = {}

    def emit(group, gr):
        if group == "ffn":
            srcs = [gr["w_ffn_t"].reshape(N_DEV, W_FFN_BLK, D_MODEL), gr["w_down"].reshape(N_DEV, DOWN_BLK, D_MODEL)]
        elif group == "mix":
            srcs = [gr[n].reshape(N_DEV, ROW_BLK, D_MODEL) for n in ("w_out", "w_a", "w_b")]
        else:
            srcs = [gr["w_in_t"].reshape(N_DEV, W_IN_BLK, D_MODEL)]
        handles[group], token = _exchange_start(srcs, True, name="rs_" + group + "_start")
        return token

    small = dict(norm1_g=norm1_g, lb_logits=lb_logits, hgrn_norm_g=hgrn_norm_g, attn_sinks=attn_sinks, norm2_g=norm2_g,
                 conv_b=conv_b, final_g=final_g)
    loss, grad_x, g = _local_step(x, positions, loss_target, small, w_in_full_t, rest_weights, emit, start_token)

    def parts_of(group, after):
        srcs, lands = _exchange_wait(handles[group], after, name="rs_" + group + "_wait")
        return [_with_own(l, lax.dynamic_index_in_dim(s, dev, 0, keepdims=False), dev) for s, l in zip(srcs, lands)]

    p_ffn, p_down = parts_of("ffn", grad_x)
    p_out, p_a, p_b = parts_of("mix", grad_x)
    (p_in,) = parts_of("in", grad_x)
    big = dict(
        w_in=[untr(t) for t in _adamw_sum(p_in, tr(w_in), tr(m_w_in), tr(v_w_in), name="adamw_w_in")],
        w_a=_adamw_sum(p_a, w_a, m_w_a, v_w_a, name="adamw_w_a"),
        w_b=_adamw_sum(p_b, w_b, m_w_b, v_w_b, name="adamw_w_b"),
        w_out=_adamw_sum(p_out, w_out, m_w_out, v_w_out, name="adamw_w_out"),
        w_ffn_in=[untr(t) for t in _adamw_sum(p_ffn, tr(w_ffn_in), tr(m_w_ffn_in), tr(v_w_ffn_in), name="adamw_w_ffn_in")],
        w_down=_adamw_sum(p_down, w_down, m_w_down, v_w_down, name="adamw_w_down"),
    )

    row = lambda t: t.reshape(1, -1) if t.ndim == 1 else t
    shard = lambda t: t.reshape(3, CONVW_BLK)
    sm_g = {nm: g[nm] for nm in SMALL_ROWS}
    sm_w = dict(norm1_g=norm1_g, lb_logits=lb_logits, hgrn_norm_g=hgrn_norm_g, attn_sinks=attn_sinks, norm2_g=norm2_g,
                conv_b=conv_b, final_g=row(final_g), conv_w=shard(conv_w))
    sm_m = dict(norm1_g=m_norm1_g, lb_logits=m_lb_logits, hgrn_norm_g=m_hgrn_norm_g, attn_sinks=m_attn_sinks, norm2_g=m_norm2_g,
                conv_b=m_conv_b, final_g=row(m_final_g), conv_w=shard(m_conv_w))
    sm_v = dict(norm1_g=v_norm1_g, lb_logits=v_lb_logits, hgrn_norm_g=v_hgrn_norm_g, attn_sinks=v_attn_sinks, norm2_g=v_norm2_g,
                conv_b=v_conv_b, final_g=row(v_final_g), conv_w=shard(v_conv_w))
    loss_total, sm_out = _small_step(sm_g, g["conv_w"], loss, sm_w, sm_m, sm_v, dev.astype(jnp.int32).reshape(1), name="small_step")
    shapes = dict(final_g=final_g.shape, conv_w=conv_w.shape)

    names = ("norm1_g", "w_in", "lb_logits", "hgrn_norm_g", "w_a", "attn_sinks", "w_b", "w_out", "norm2_g", "w_ffn_in", "conv_w", "conv_b", "w_down", "final_g")
    outs = [loss_total.reshape(()), grad_x]
    for kind in range(4):
        outs += [big[n][kind] if n in big else sm_out[n][kind].reshape(shapes.get(n, sm_out[n][kind].shape)) for n in names]
    return tuple(outs)
```

```python
import jax
import jax.numpy as jnp
from jax import lax
from jax.experimental import pallas as pl
from jax.experimental.pallas import tpu as pltpu

F32 = jnp.float32
BF16 = jnp.bfloat16

D_MODEL = 1024
HGRN_HEADS = 8
HGRN_DK = 128
CHUNK = 64
ATT_HEADS = 16
ATT_KV_HEADS = 2
ATT_HD = 64
ATT_GROUP = ATT_HEADS // ATT_KV_HEADS
WINDOW = 128
ROPE_DIM = ATT_HD // 4
ROPE_THETA = 500000.0
D_FF = 2816
EPS = 1e-6
NEG_INF = -1e30
N_DEV = 8

ADAM_LR = 0.001
ADAM_B1 = 0.9
ADAM_B2 = 0.999
ADAM_EPS = 1e-08
ADAM_WD = 0.01
ADAM_STEP = 10

MESH = pl.DeviceIdType.MESH
ANY = pl.BlockSpec(memory_space=pl.ANY)


def _pick(n, cands):
    for c in cands:
        if n % c == 0:
            return c
    return n


def _sigmoid(x):
    return 0.5 * jnp.tanh(0.5 * x) + 0.5


def _silu(x):
    hx = 0.5 * x
    return hx * jnp.tanh(hx) + hx


def _rms(x, g):
    return x * lax.rsqrt(jnp.mean(x * x, axis=-1, keepdims=True) + EPS) * g


def _dot(a, b, dims):
    return lax.dot_general(a, b, (dims, ((), ())), preferred_element_type=F32)


def _nn(a, b):
    return _dot(a, b, ((1,), (0,)))


def _nt(a, b):
    return _dot(a, b, ((1,), (1,)))


def _tn(a, b):
    return _dot(a, b, ((0,), (0,)))


def _params(*sem):
    return pltpu.CompilerParams(dimension_semantics=sem, vmem_limit_bytes=56 * 1024 * 1024)


def _matmul(a, b, *, ta=False, tb=False, out_dtype=F32, addend=None, after=None, into=None, o_noff=0, out_t=False,
            o_block_perm=lambda j: j, name, tm, tn, tk=None, n_extent=None, b_koff=0, b_noff=0):
    M, K = (a.shape[1], a.shape[0]) if ta else a.shape
    N = n_extent or (b.shape[0] if tb else b.shape[1])
    tm, tn, tk = min(tm, M), min(tn, N), min(tk or K, K)
    assert M % tm == 0 and N % tn == 0 and K % tk == 0, (name, M, N, K, tm, tn, tk)
    nk = K // tk
    use_scratch = nk > 1 and out_dtype != F32
    grid = (M // tm, N // tn, nk)
    a_spec = pl.BlockSpec((tk, tm), lambda i, j, k: (k, i)) if ta else pl.BlockSpec((tm, tk), lambda i, j, k: (i, k))
    b_spec = pl.BlockSpec((tn, tk), lambda i, j, k: (j + b_noff, k + b_koff)) if tb else pl.BlockSpec((tk, tn), lambda i, j, k: (k + b_koff, j + b_noff))
    o_spec = pl.BlockSpec((tm, tn), lambda i, j, k: (i, j))
    dims = ((0 if ta else 1,), (1 if tb else 0,))
    has_add = addend is not None

    n_in = 2 + has_add + (after is not None) + (into is not None)

    def body(*refs):
        a_ref, b_ref = refs[:2]
        c_ref = refs[2] if has_add else None
        o_ref = refs[n_in]
        part = _dot(a_ref[...], b_ref[...], dims)
        if nk == 1:
            if has_add:
                part = part + c_ref[...].astype(F32)
            o_ref[...] = (part.T if out_t else part).astype(out_dtype)
        else:
            acc_ref = refs[-1] if use_scratch else o_ref
            k = pl.program_id(2)

            @pl.when(k == 0)
            def _():
                acc_ref[...] = part + c_ref[...].astype(F32) if has_add else part

            @pl.when(k > 0)
            def _():
                acc_ref[...] += part

            if use_scratch:
                @pl.when(k == nk - 1)
                def _():
                    o_ref[...] = acc_ref[...].astype(out_dtype)

    in_specs = [a_spec, b_spec] + ([o_spec] if has_add else [])
    args = (a, b) + ((addend,) if has_add else ())
    if after is not None:
        in_specs.append(pl.BlockSpec(after.shape, lambda i, j, k: (0, 0)))
        args += (after,)
    aliases = {}
    if into is not None:
        in_specs.append(ANY)
        args += (into,)
        aliases = {len(args) - 1: 0}
    if out_t:
        assert nk == 1 and not has_add
        o_spec = pl.BlockSpec((tn, tm), lambda i, j, k: (o_block_perm(j) + o_noff, i))
    elif into is not None:
        o_spec = pl.BlockSpec((tm, tn), lambda i, j, k: (i, j + o_noff))
    return pl.pallas_call(
        body,
        name=name,
        grid=grid,
        in_specs=in_specs,
        out_specs=o_spec,
        out_shape=jax.ShapeDtypeStruct(into.shape if into is not None else ((N, M) if out_t else (M, N)), out_dtype),
        input_output_aliases=aliases,
        scratch_shapes=[pltpu.VMEM((tm, tn), F32)] if use_scratch else [],
        compiler_params=_params("parallel", "parallel", "arbitrary"),
    )(*args)


def _matmul_col_tiles(a, b_t, *, tm, tn, tc, name):
    M, K = a.shape
    N = b_t.shape[0]
    tm = min(tm, M)
    per_step = tn // tc

    def body(a_ref, b_ref, o_ref):
        res = _nt(a_ref[...], b_ref[...]).astype(BF16)
        for t in range(per_step):
            o_ref[t] = res[:, t * tc:(t + 1) * tc]

    return pl.pallas_call(
        body, name=name, grid=(M // tm, N // tn),
        in_specs=[pl.BlockSpec((tm, K), lambda i, j: (i, 0)), pl.BlockSpec((tn, K), lambda i, j: (j, 0))],
        out_specs=pl.BlockSpec((per_step, tm, tc), lambda i, j: (j, i, 0)),
        out_shape=jax.ShapeDtypeStruct((N // tc, M, tc), BF16),
        compiler_params=_params("parallel", "parallel"),
    )(a, b_t)


EPILOGUE_ROWS = 256


def _matmul_ep(pairs, *, tm, ins, in_specs, out_shapes, out_specs, sums=(), epilogue, aliases=None, name):
    M = pairs[0][0].shape[0]
    tm = min(tm, M)
    mm_specs, mm_args, dims = [], [], []
    for a, b, tb, koff in pairs:
        K = a.shape[1]
        N = b.shape[0] if tb else b.shape[1]
        mm_specs += [pl.BlockSpec((tm, K), lambda i: (i, 0)),
                     pl.BlockSpec((N, K), lambda i, koff=koff: (0, koff)) if tb else pl.BlockSpec((K, N), lambda i, koff=koff: (koff, 0))]
        mm_args += [a, b]
        dims.append(((1,), (1 if tb else 0,)))
    n_mm = len(mm_args)
    n_in = n_mm + len(ins)
    rows = min(EPILOGUE_ROWS, tm)

    def body(*refs):
        in_refs, out_refs = refs[n_mm:n_in], refs[n_in:]

        def products(s):
            return [_dot(refs[2 * p][s * rows:(s + 1) * rows, :], refs[2 * p + 1][...], dims[p]) for p in range(len(pairs))]

        totals = {}
        accs = products(0)
        for s in range(tm // rows):
            ahead = products(s + 1) if (s + 1) * rows < tm else None
            outs = epilogue(*accs, *[r.at[pl.ds(s * rows, rows)] if r.shape[0] == tm else r for r in in_refs])
            for k, (ref, val) in enumerate(zip(out_refs, outs)):
                if val is None:
                    continue
                if k in sums:
                    totals[k] = val if s == 0 else totals[k] + val
                else:
                    ref[s * rows:(s + 1) * rows, :] = val.astype(ref.dtype)
            accs = ahead
        for k, total in totals.items():
            ref = out_refs[k]

            @pl.when(pl.program_id(0) == 0)
            def _():
                ref[...] = jnp.zeros_like(ref)

            ref[...] += total

    return pl.pallas_call(
        body, name=name, grid=(M // tm,),
        in_specs=mm_specs + list(in_specs),
        out_specs=list(out_specs), out_shape=list(out_shapes),
        input_output_aliases={n_mm + k: v for k, v in (aliases or {}).items()},
        compiler_params=_params("arbitrary"),
    )(*mm_args, *ins)


def _row_spec(tm, n):
    return pl.BlockSpec((tm, n), lambda i: (i, 0))


def _full_spec(shape):
    return pl.BlockSpec(shape, lambda i: tuple(0 for _ in shape))


MAX_DOT_COLS = 2048


def _resident_spec(shape):
    return pl.BlockSpec(shape, lambda i: tuple(0 for _ in shape), pipeline_mode=pl.Buffered(1))


def _norm_matmul(x, g, w_t, segments, *, tm, name):
    T, D = x.shape
    N = w_t.shape[0]
    tm = min(tm, T)
    chunks = [(c + o, r + o, min(MAX_DOT_COLS, n - o)) for c, r, n in segments for o in range(0, n, MAX_DOT_COLS)]

    def body(x_ref, g_ref, w_ref, u_ref, z_ref):
        u = _rms(x_ref[...], g_ref[...]).astype(BF16)
        u_ref[...] = u
        for c, r, n in chunks:
            z_ref[:, c:c + n] = _nt(u, w_ref[r:r + n, :]).astype(BF16)

    return pl.pallas_call(
        body, name=name, grid=(T // tm,),
        in_specs=[_row_spec(tm, D), _full_spec((1, D)), _resident_spec((N, D))],
        out_specs=[_row_spec(tm, D), _row_spec(tm, N)],
        out_shape=[jax.ShapeDtypeStruct((T, D), BF16), jax.ShapeDtypeStruct((T, N), BF16)],
        compiler_params=_params("parallel"),
    )(x, g, w_t)


def _matmul_norm_bwd(dz, w_t, segments, x, g, dres, after, *, tm, name):
    T, K = dz.shape
    D = w_t.shape[1]
    tm = min(tm, T)

    def body(dz_ref, w_ref, x_ref, g_ref, dr_ref, after_ref, dx_ref, dg_ref):
        @pl.when(pl.program_id(0) == 0)
        def _():
            dg_ref[...] = jnp.zeros_like(dg_ref)

        du = sum(_nn(dz_ref[:, c:c + n], w_ref[r:r + n, :]) for c, r, n in segments)
        _, vjp = jax.vjp(_rms, x_ref[...], g_ref[...])
        dx, dg = vjp(du)
        dx_ref[...] = dx + dr_ref[...]
        dg_ref[...] += dg

    row = _row_spec(tm, D)
    return pl.pallas_call(
        body, name=name, grid=(T // tm,),
        in_specs=[_row_spec(tm, K), _resident_spec((K, D)), row, _full_spec((1, D)), row, _full_spec(after.shape)],
        out_specs=[row, _full_spec((1, D))],
        out_shape=[jax.ShapeDtypeStruct((T, D), F32), jax.ShapeDtypeStruct((1, D), F32)],
        compiler_params=_params("arbitrary"),
    )(dz, w_t, x, g, dres, after)


def _norm_bwd_add(x, g, du, dres, *, with_bf16=True, name):
    T, D = x.shape
    tm = _pick(T, (512, 256, 128))

    def body(x_ref, g_ref, du_ref, dr_ref, dx_ref, *rest):
        dg_ref = rest[-1]
        _, vjp = jax.vjp(_rms, x_ref[...], g_ref[...])
        dx, dg = vjp(du_ref[...].astype(F32))
        dx = dx + dr_ref[...]
        dx_ref[...] = dx
        if with_bf16:
            rest[0][...] = dx.astype(BF16)

        @pl.when(pl.program_id(0) == 0)
        def _():
            dg_ref[...] = jnp.zeros_like(dg_ref)

        dg_ref[...] += dg

    row = _row_spec(tm, D)
    return pl.pallas_call(
        body, name=name, grid=(T // tm,),
        in_specs=[row, _full_spec((1, D)), row, row],
        out_specs=[row] + ([row] if with_bf16 else []) + [_full_spec((1, D))],
        out_shape=[jax.ShapeDtypeStruct((T, D), F32)] + ([jax.ShapeDtypeStruct((T, D), BF16)] if with_bf16 else []) + [jax.ShapeDtypeStruct((1, D), F32)],
        compiler_params=_params("arbitrary"),
    )(x, g, du, dres)


def _merge_fn(gates, a, b):
    ga = gates[:, :D_MODEL].astype(F32)
    gb = gates[:, D_MODEL:].astype(F32)
    return _sigmoid(ga) * a.astype(F32) + _sigmoid(gb) * b.astype(F32)


def _gates_spec(tm):
    return pl.BlockSpec((tm, W_GATES), lambda i: (i, O_GATES // W_GATES))


CONV_TC = 256


def _shift_down(x, n, rows):
    return jnp.where(rows >= n, pltpu.roll(x, n, 0), 0.0)


def _shift_up(x, n, rows, S):
    return jnp.where(rows < S - n, pltpu.roll(x, S - n, 0), 0.0)


def _conv_act_fwd(gu, conv_w, conv_b, *, name):
    _, B, S, tc = gu.shape
    nc = D_FF // tc

    def body(g_ref, up_ref, w_ref, b_ref, o_ref, a_ref):
        g = g_ref[...].astype(F32)
        rows = lax.broadcasted_iota(jnp.int32, g.shape, 0)
        w = w_ref[...]
        a = w[2:3] * g + w[1:2] * _shift_down(g, 1, rows) + w[0:1] * _shift_down(g, 2, rows) + b_ref[...]
        o_ref[...] = (_silu(a) * up_ref[...].astype(F32)).astype(BF16)
        a_ref[...] = a.astype(BF16)

    col = pl.BlockSpec((None, S, tc), lambda b, j: (b, 0, j))
    tile = lambda off: pl.BlockSpec((None, None, S, tc), lambda b, j: (j + off, b, 0, 0))
    return pl.pallas_call(
        body, name=name, grid=(B, nc),
        in_specs=[tile(0), tile(nc),
                  pl.BlockSpec((3, tc), lambda b, j: (0, j)),
                  pl.BlockSpec((1, tc), lambda b, j: (0, j))],
        out_specs=[col, tile(0)],
        out_shape=[jax.ShapeDtypeStruct((B, S, D_FF), BF16), jax.ShapeDtypeStruct((nc, B, S, tc), BF16)],
        compiler_params=_params("parallel", "parallel"),
    )(gu, gu, conv_w, conv_b)


def _conv_act_bwd(gu, a_pre, conv_w, dact, *, name):
    _, B, S, tc = gu.shape
    nc = D_FF // tc

    def body(g_ref, up_ref, a_ref, w_ref, da_ref, dg_ref, dup_ref, dw_ref, db_ref):
        g = g_ref[...].astype(F32)
        up, a, dact = up_ref[...], a_ref[...], da_ref[...]
        rows = lax.broadcasted_iota(jnp.int32, g.shape, 0)
        w = w_ref[...]
        sg = _sigmoid(a)
        dup_ref[...] = dact * a * sg
        da = (dact * up * sg * (1.0 + a * (1.0 - sg))).astype(F32)
        da1 = _shift_up(da, 1, rows, S)
        da2 = _shift_up(da, 2, rows, S)
        dg_ref[...] = (w[2:3] * da + w[1:2] * da1 + w[0:1] * da2).astype(BF16)

        @pl.when(pl.program_id(1) == 0)
        def _():
            dw_ref[...] = jnp.zeros_like(dw_ref)
            db_ref[...] = jnp.zeros_like(db_ref)

        dw_ref[0:1, :] += jnp.sum(da2 * g, axis=0, keepdims=True)
        dw_ref[1:2, :] += jnp.sum(da1 * g, axis=0, keepdims=True)
        dw_ref[2:3, :] += jnp.sum(da * g, axis=0, keepdims=True)
        db_ref[...] += jnp.sum(da, axis=0, keepdims=True)

    col = pl.BlockSpec((None, S, tc), lambda j, b: (b, 0, j))
    tile = lambda off: pl.BlockSpec((None, None, S, tc), lambda j, b: (j + off, b, 0, 0))
    return pl.pallas_call(
        body, name=name, grid=(nc, B),
        in_specs=[tile(0), tile(nc), tile(0),
                  pl.BlockSpec((3, tc), lambda j, b: (0, j)),
                  col],
        out_specs=[col, col, pl.BlockSpec((3, tc), lambda j, b: (0, j)), pl.BlockSpec((1, tc), lambda j, b: (0, j))],
        out_shape=[jax.ShapeDtypeStruct((B, S, D_FF), BF16), jax.ShapeDtypeStruct((B, S, D_FF), BF16),
                   jax.ShapeDtypeStruct((3, D_FF), F32), jax.ShapeDtypeStruct((1, D_FF), F32)],
        compiler_params=_params("parallel", "arbitrary"),
    )(gu, gu, a_pre, conv_w, dact)


HGRN_CPB = 8
HF = HGRN_HEADS * HGRN_DK


def _tri(n, upper=False):
    r = lax.broadcasted_iota(jnp.int32, (n, n), 0)
    c = lax.broadcasted_iota(jnp.int32, (n, n), 1)
    return (c >= r) if upper else (r >= c)


def _hs(h):
    return slice(h * HGRN_DK, (h + 1) * HGRN_DK)


def _cumsum_rows(tri_b, x):
    hi = x.astype(BF16)
    lo = (x - hi.astype(F32)).astype(BF16)
    return _nn(tri_b, hi) + _nn(tri_b, lo)


def _hgrn_col(seg, h):
    return slice(seg * HF + h * HGRN_DK, seg * HF + (h + 1) * HGRN_DK)


def _hgrn_gates(q, fz, lb):
    sg = _sigmoid(fz)
    return _sigmoid(q), sg, lb + (1.0 - lb) * sg


def _hgrn_decays(b, q, sq, f):
    qf = q * sq
    k = 1.0 - f
    bref = b[CHUNK // 2:CHUNK // 2 + 1, :]
    blast = b[CHUNK - 1:CHUNK, :]
    e1 = jnp.exp2(b - bref)
    e2 = jnp.exp2(bref - b)
    e3 = e1 * jnp.exp2(bref)
    e4 = e2 * jnp.exp2(blast - bref)
    return (e1, e2, e3, e4), qf * e1, k * e2, qf * e3, k * e4, jnp.exp2(blast)


def _hgrn_fwd(zh, lb, gn, *, name):
    B, S, _ = zh.shape
    cpb = HGRN_CPB
    ts = cpb * CHUNK
    nblk = S // ts

    def body(z_ref, lb_ref, gn_ref, o_ref, st_ref, state):
        @pl.when(pl.program_id(1) == 0)
        def _():
            state[...] = jnp.zeros_like(state)

        R = range(HGRN_HEADS)
        causal = _tri(CHUNK)
        tril_b = causal.astype(BF16)
        lbh = [lb_ref[:, _hs(h)] for h in R]
        for c in range(cpb):
            rows = slice(c * CHUNK, (c + 1) * CHUNK)
            q = [z_ref[rows, _hgrn_col(0, h)].astype(F32) for h in R]
            gates = [_hgrn_gates(q[h], z_ref[rows, _hgrn_col(1, h)].astype(F32), lbh[h]) for h in R]
            b = [_cumsum_rows(tril_b, jnp.log2(gates[h][2])) for h in R]
            v = [z_ref[rows, _hgrn_col(2, h)] for h in R]
            dec, q_in, k_in, q_out, k_st = [], [], [], [], []
            for h in R:
                _, qi, ki, qo, ks, d = _hgrn_decays(b[h], q[h], gates[h][0], gates[h][2])
                dec.append(d)
                for lst, t in zip((q_in, k_in, q_out, k_st), (qi, ki, qo, ks)):
                    lst.append(t.astype(BF16))
            a = [jnp.where(causal, _nt(q_in[h], k_in[h]), 0.0).astype(BF16) for h in R]
            st = [state[h] for h in R]
            for h in R:
                st_ref[c, h] = st[h]
            o = [_nn(a[h], v[h]) + _nt(q_out[h], st[h].astype(BF16)) for h in R]
            for h in R:
                state[h] = st[h] * dec[h] + _tn(v[h], k_st[h])
            for h in R:
                o_ref[rows, _hs(h)] = (_rms(o[h], gn_ref[...]) * _silu(z_ref[rows, _hgrn_col(3, h)].astype(F32))).astype(BF16)

    return pl.pallas_call(
        body, name=name, grid=(B, nblk),
        in_specs=[pl.BlockSpec((None, ts, 4 * HF), lambda b, s: (b, s, 0)),
                  pl.BlockSpec((1, HF), lambda b, s: (0, 0)),
                  pl.BlockSpec((1, HGRN_DK), lambda b, s: (0, 0))],
        out_specs=[pl.BlockSpec((None, ts, HF), lambda b, s: (b, s, 0)),
                   pl.BlockSpec((None, cpb, HGRN_HEADS, HGRN_DK, HGRN_DK), lambda b, s: (b, s, 0, 0, 0))],
        out_shape=[jax.ShapeDtypeStruct((B, S, HF), BF16),
                   jax.ShapeDtypeStruct((B, S // CHUNK, HGRN_HEADS, HGRN_DK, HGRN_DK), F32)],
        scratch_shapes=[pltpu.VMEM((HGRN_HEADS, HGRN_DK, HGRN_DK), F32)],
        compiler_params=_params("arbitrary", "arbitrary"),
    )(zh, lb, gn)


def _hgrn_bwd(zh, lb, gn, states, doa, dz, *, name):
    B, S, _ = zh.shape
    cpb = HGRN_CPB
    ts = cpb * CHUNK
    nblk = S // ts
    rev = lambda b, s: (b, nblk - 1 - s, 0)

    def body(z_ref, lb_ref, gn_ref, st_ref, do_ref, dz_in, dz_ref, dlb_ref, dgn_ref, dstate):
        @pl.when(pl.program_id(1) == 0)
        def _():
            dstate[...] = jnp.zeros_like(dstate)

        @pl.when((pl.program_id(0) == 0) & (pl.program_id(1) == 0))
        def _():
            dlb_ref[...] = jnp.zeros_like(dlb_ref)
            dgn_ref[...] = jnp.zeros_like(dgn_ref)

        R = range(HGRN_HEADS)
        causal = _tri(CHUNK)
        tril_b = causal.astype(BF16)
        triu_b = _tri(CHUNK, upper=True).astype(BF16)
        rowid = lax.broadcasted_iota(jnp.int32, (CHUNK, HGRN_DK), 0)
        lbh = [lb_ref[:, _hs(h)] for h in R]
        gn = gn_ref[...]
        for c in reversed(range(cpb)):
            rows = slice(c * CHUNK, (c + 1) * CHUNK)
            q = [z_ref[rows, _hgrn_col(0, h)].astype(F32) for h in R]
            gates = [_hgrn_gates(q[h], z_ref[rows, _hgrn_col(1, h)].astype(F32), lbh[h]) for h in R]
            b = [_cumsum_rows(tril_b, jnp.log2(gates[h][2])) for h in R]
            v = [z_ref[rows, _hgrn_col(2, h)] for h in R]
            pre = [_hgrn_decays(b[h], q[h], gates[h][0], gates[h][2]) for h in R]
            q_in_b, k_in_b, q_out_b, k_st_b = ([pre[h][i].astype(BF16) for h in R] for i in (1, 2, 3, 4))
            a_b = [jnp.where(causal, _nt(q_in_b[h], k_in_b[h]), 0.0).astype(BF16) for h in R]
            st = [st_ref[c, h] for h in R]
            st_b = [t.astype(BF16) for t in st]
            o = [_nn(a_b[h], v[h]) + _nt(q_out_b[h], st_b[h]) for h in R]
            do_l, dgn_acc = [], jnp.zeros_like(gn)
            for h in R:
                hg = z_ref[rows, _hgrn_col(3, h)].astype(F32)
                dout = do_ref[rows, _hs(h)].astype(F32)
                shg = _sigmoid(hg)
                on_h, norm_vjp = jax.vjp(_rms, o[h], gn)
                d_o, d_gn = norm_vjp(dout * (hg * shg))
                do_l.append(d_o)
                dgn_acc = dgn_acc + d_gn
                dz_ref[rows, _hgrn_col(3, h)] = (dout * on_h * shg * (1.0 + hg * (1.0 - shg))).astype(BF16)
            dgn_ref[...] += dgn_acc
            do_b = [t.astype(BF16) for t in do_l]
            dst = [dstate[h] for h in R]
            dst_b = [t.astype(BF16) for t in dst]
            da_b = [jnp.where(causal, _nt(do_b[h], v[h]), 0.0).astype(BF16) for h in R]
            dv = [_tn(a_b[h], do_b[h]) + _nt(k_st_b[h], dst_b[h]) for h in R]
            dq_in = [_nn(da_b[h], k_in_b[h]) for h in R]
            dk_in = [_tn(da_b[h], q_in_b[h]) for h in R]
            dq_out = [_nn(do_b[h], st_b[h]) for h in R]
            dk_st = [_nn(v[h], dst_b[h]) for h in R]
            for h in R:
                dz_ref[rows, _hgrn_col(2, h)] = dv[h].astype(BF16)
            db = []
            for h in R:
                _, q_in, k_in, q_out, k_st, dec = pre[h]
                ddec = jnp.sum(st[h] * dst[h], axis=0, keepdims=True)
                t_qin, t_kin, t_kst = dq_in[h] * q_in, dk_in[h] * k_in, dk_st[h] * k_st
                dbref = jnp.sum(t_kin - t_qin, axis=0, keepdims=True)
                dblast = jnp.sum(t_kst, axis=0, keepdims=True) + ddec * dec
                db.append(t_qin - t_kin + dq_out[h] * q_out - t_kst
                          + jnp.where(rowid == CHUNK // 2, dbref, 0.0) + jnp.where(rowid == CHUNK - 1, dblast, 0.0))
            for h in R:
                dstate[h] = dst[h] * pre[h][5] + _tn(do_b[h], q_out_b[h])
            dlogf = [_cumsum_rows(triu_b, db[h]) for h in R]
            for h in R:
                sq, sg, f = gates[h]
                e1, e2, e3, e4 = pre[h][0]
                dqf = dq_in[h] * e1 + dq_out[h] * e3
                dk = dk_in[h] * e2 + dk_st[h] * e4
                df_open = (dlogf[h] / f - dk) * (1.0 - sg)
                dlb_ref[:, _hs(h)] += jnp.sum(df_open, axis=0, keepdims=True)
                dz_ref[rows, _hgrn_col(1, h)] = (df_open * ((1.0 - lbh[h]) * sg)).astype(BF16)
                dz_ref[rows, _hgrn_col(0, h)] = (dqf * sq * (1.0 + q[h] * (1.0 - sq))).astype(BF16)

    return pl.pallas_call(
        body, name=name, grid=(B, nblk),
        in_specs=[pl.BlockSpec((None, ts, 4 * HF), rev),
                  pl.BlockSpec((1, HF), lambda b, s: (0, 0)),
                  pl.BlockSpec((1, HGRN_DK), lambda b, s: (0, 0)),
                  pl.BlockSpec((None, cpb, HGRN_HEADS, HGRN_DK, HGRN_DK), lambda b, s: (b, nblk - 1 - s, 0, 0, 0)),
                  pl.BlockSpec((None, ts, HF), rev),
                  ANY],
        out_specs=[pl.BlockSpec((None, ts, 4 * HF), rev),
                   pl.BlockSpec((1, HF), lambda b, s: (0, 0)),
                   pl.BlockSpec((1, HGRN_DK), lambda b, s: (0, 0))],
        out_shape=[jax.ShapeDtypeStruct(dz.shape, BF16),
                   jax.ShapeDtypeStruct((1, HF), F32),
                   jax.ShapeDtypeStruct((1, HGRN_DK), F32)],
        input_output_aliases={5: 0},
        scratch_shapes=[pltpu.VMEM((HGRN_HEADS, HGRN_DK, HGRN_DK), F32)],
        compiler_params=_params("arbitrary", "arbitrary"),
    )(zh, lb, gn, states, doa, dz)


KV_W = ATT_KV_HEADS * ATT_HD
ATT_SCALE = ATT_HD ** -0.5


def _rope(x, cos, sin, inverse=False):
    half = ROPE_DIM // 2
    outs = []
    for p in range(x.shape[1] // 128):
        xp = x[:, p * 128:(p + 1) * 128]
        lane = lax.broadcasted_iota(jnp.int32, xp.shape, 1) % ATT_HD
        sw = jnp.where(lane < half, pltpu.roll(xp, 128 - half, 1), pltpu.roll(xp, half, 1))
        outs.append(xp * cos - sw * sin if inverse else xp * cos + sw * sin)
    return outs[0] if len(outs) == 1 else jnp.concatenate(outs, axis=1)


PAIRS_PER_KV = ATT_GROUP // 2


def _swap_halves(x):
    return pltpu.roll(x, ATT_HD, 1)


def _kv_padded(t, low):
    sw = _swap_halves(t)
    zero = jnp.zeros_like(t)
    out = []
    for g in range(ATT_KV_HEADS):
        in_low, in_high = (t, sw) if g == 0 else (sw, t)
        out.append((jnp.where(low, in_low, zero).astype(BF16), jnp.where(low, zero, in_high).astype(BF16)))
    return out


def _swa_mask(first_block):
    qi = lax.broadcasted_iota(jnp.int32, (WINDOW, 2 * WINDOW), 0)
    mi = lax.broadcasted_iota(jnp.int32, (WINDOW, 2 * WINDOW), 1)
    band = (mi > qi) & (mi <= qi + WINDOW)
    return band & (jnp.logical_not(first_block) | (mi >= WINDOW))


def _swa_specs(nb):
    cur = lambda b, i: (b, i, 0)
    prev = lambda b, i: (b, jnp.maximum(i - 1, 0), 0)
    return cur, prev


def _swa_z_specs():
    q = pl.BlockSpec((None, WINDOW, W_AQ), lambda b, i: (b, i, O_AQ // W_AQ))
    kv_prev = pl.BlockSpec((None, WINDOW, W_AKV), lambda b, i: (b, jnp.maximum(i - 1, 0), O_AKV // W_AKV))
    kv_cur = pl.BlockSpec((None, WINDOW, W_AKV), lambda b, i: (b, i, O_AKV // W_AKV))
    return q, kv_prev, kv_cur


def _swa_fwd(z, cos, sin, sinks, *, name):
    B, S, _ = z.shape
    nb = S // WINDOW
    cur, prev = _swa_specs(nb)

    def body(q_ref, kvp_ref, kvc_ref, cp_ref, sp_ref, cc_ref, sc_ref, sink_ref, o_ref, lse_ref, qr_ref, kr_ref):
        cos_c, sin_c = cc_ref[...], sc_ref[...]
        q = (_rope(q_ref[...].astype(F32), cos_c, sin_c) * ATT_SCALE).astype(BF16)
        k = jnp.concatenate([_rope(kvp_ref[:, :KV_W].astype(F32), cp_ref[...], sp_ref[...]),
                             _rope(kvc_ref[:, :KV_W].astype(F32), cos_c, sin_c)], axis=0)
        qr_ref[...] = q
        kr_ref[...] = k[WINDOW:].astype(BF16)
        v = jnp.concatenate([kvp_ref[:, KV_W:], kvc_ref[:, KV_W:]], axis=0).astype(F32)
        low = lax.broadcasted_iota(jnp.int32, k.shape, 1) < ATT_HD
        kpad = _kv_padded(k, low)
        vpad = _kv_padded(v, low)
        mask = _swa_mask(pl.program_id(1) == 0)
        lses = []
        for g in range(ATT_KV_HEADS):
            pairs = range(g * PAIRS_PER_KV, (g + 1) * PAIRS_PER_KV)
            keys = [(p, e) for p in pairs for e in (0, 1)]
            qp = {p: q[:, p * 128:(p + 1) * 128] for p in pairs}
            s = {pe: jnp.where(mask, _nt(qp[pe[0]], kpad[g][pe[1]]), NEG_INF) for pe in keys}
            pr = {}
            for pe in keys:
                sink = sink_ref[0, 2 * pe[0] + pe[1]]
                m = jnp.maximum(jnp.max(s[pe], axis=1, keepdims=True), sink)
                ex = jnp.exp(s[pe] - m)
                den = jnp.sum(ex, axis=1, keepdims=True) + jnp.exp(sink - m)
                pr[pe] = (ex * (1.0 / den)).astype(BF16)
                lses.append(m + jnp.log(den))
            for p in pairs:
                o_ref[:, p * 128:(p + 1) * 128] = (_nn(pr[p, 0], vpad[g][0]) + _nn(pr[p, 1], vpad[g][1])).astype(BF16)
        lse_ref[...] = jnp.concatenate(lses, axis=1)

    tab = lambda im: pl.BlockSpec((None, WINDOW, 128), im)
    return pl.pallas_call(
        body, name=name, grid=(B, nb),
        in_specs=[*_swa_z_specs(),
                  tab(prev), tab(prev), tab(cur), tab(cur),
                  pl.BlockSpec(memory_space=pltpu.SMEM)],
        out_specs=[pl.BlockSpec((None, WINDOW, D_MODEL), cur), pl.BlockSpec((None, WINDOW, ATT_HEADS), cur),
                   pl.BlockSpec((None, WINDOW, D_MODEL), cur), pl.BlockSpec((None, WINDOW, KV_W), cur)],
        out_shape=[jax.ShapeDtypeStruct((B, S, D_MODEL), BF16), jax.ShapeDtypeStruct((B, S, ATT_HEADS), F32),
                   jax.ShapeDtypeStruct((B, S, D_MODEL), BF16), jax.ShapeDtypeStruct((B, S, KV_W), BF16)],
        compiler_params=_params("parallel", "parallel"),
    )(z, z, z, cos, sin, cos, sin, sinks)


def _swa_bwd(z, qr, kr, cos, sin, sinks, lse, dob, dz, *, name):
    B, S, _ = z.shape
    nb = S // WINDOW
    cur, prev = _swa_specs(nb)

    def body(q_ref, krp_ref, krc_ref, kvp_ref, kvc_ref, cp_ref, sp_ref, cc_ref, sc_ref, sink_ref, lse_ref, do_ref, dz_in,
             dq_ref, dkc_ref, dkp_ref, dsink_ref):
        @pl.when((pl.program_id(0) == 0) & (pl.program_id(1) == 0))
        def _():
            dsink_ref[...] = jnp.zeros_like(dsink_ref)

        cos_c, sin_c, cos_p, sin_p = cc_ref[...], sc_ref[...], cp_ref[...], sp_ref[...]
        q = q_ref[...]
        k = jnp.concatenate([krp_ref[...], krc_ref[...]], axis=0).astype(F32)
        v = jnp.concatenate([kvp_ref[:, KV_W:], kvc_ref[:, KV_W:]], axis=0).astype(F32)
        low = lax.broadcasted_iota(jnp.int32, k.shape, 1) < ATT_HD
        kpad = _kv_padded(k, low)
        vpad = _kv_padded(v, low)
        mask = _swa_mask(pl.program_id(1) == 0)
        lse = lse_ref[...]
        dq_parts, dk_sum, dv_sum, dsinks = [], [], [], []
        for g in range(ATT_KV_HEADS):
            pairs = range(g * PAIRS_PER_KV, (g + 1) * PAIRS_PER_KV)
            keys = [(p, e) for p in pairs for e in (0, 1)]
            qp = {p: q[:, p * 128:(p + 1) * 128] for p in pairs}
            dop = {p: do_ref[:, p * 128:(p + 1) * 128] for p in pairs}
            s = {pe: jnp.where(mask, _nt(qp[pe[0]], kpad[g][pe[1]]), NEG_INF) for pe in keys}
            dp = {pe: _nt(dop[pe[0]], vpad[g][pe[1]]) for pe in keys}
            pr, ds = {}, {}
            for pe in keys:
                h = 2 * pe[0] + pe[1]
                lse_h = lse[:, h:h + 1]
                pf = jnp.exp(s[pe] - lse_h)
                delta = jnp.sum(pf * dp[pe], axis=1, keepdims=True)
                ds[pe] = (pf * (dp[pe] - delta)).astype(BF16)
                pr[pe] = pf.astype(BF16)
                p_sink = jnp.exp(sink_ref[0, h] - lse_h)
                dsinks.append(-jnp.sum(p_sink * delta, axis=0, keepdims=True))
            for p in pairs:
                dq_parts.append((_nn(ds[p, 0], kpad[g][0]) + _nn(ds[p, 1], kpad[g][1])) * ATT_SCALE)
            x = [sum(_tn(ds[p, e], qp[p]) for p in pairs) for e in (0, 1)]
            y = [sum(_tn(pr[p, e], dop[p]) for p in pairs) for e in (0, 1)]
            zk = jnp.where(low, x[0], x[1])
            zv = jnp.where(low, y[0], y[1])
            dk_sum.append(zk + _swap_halves(zk))
            dv_sum.append(zv + _swap_halves(zv))
        dq_ref[...] = _rope(jnp.concatenate(dq_parts, axis=1), cos_c, sin_c, inverse=True).astype(BF16)
        dk = jnp.where(low, dk_sum[0], dk_sum[1])
        dv = jnp.where(low, dv_sum[0], dv_sum[1])
        dkp_ref[:, :KV_W] = _rope(dk[:WINDOW], cos_p, sin_p, inverse=True)
        dkp_ref[:, KV_W:] = dv[:WINDOW]
        dkc_ref[:, :KV_W] = _rope(dk[WINDOW:], cos_c, sin_c, inverse=True)
        dkc_ref[:, KV_W:] = dv[WINDOW:]
        dsink_ref[...] += jnp.concatenate(dsinks, axis=1)

    tab = lambda im: pl.BlockSpec((None, WINDOW, 128), im)
    return pl.pallas_call(
        body, name=name, grid=(B, nb),
        in_specs=[pl.BlockSpec((None, WINDOW, D_MODEL), cur), tab(prev), tab(cur),
                  *_swa_z_specs()[1:],
                  tab(prev), tab(prev), tab(cur), tab(cur),
                  pl.BlockSpec(memory_space=pltpu.SMEM),
                  pl.BlockSpec((None, WINDOW, ATT_HEADS), cur),
                  pl.BlockSpec((None, WINDOW, D_MODEL), cur),
                  ANY],
        out_specs=[_swa_z_specs()[0],
                   pl.BlockSpec((None, WINDOW, 2 * KV_W), cur), pl.BlockSpec((None, WINDOW, 2 * KV_W), cur),
                   pl.BlockSpec((1, ATT_HEADS), lambda b, i: (0, 0))],
        out_shape=[jax.ShapeDtypeStruct(dz.shape, BF16),
                   jax.ShapeDtypeStruct((B, S, 2 * KV_W), F32), jax.ShapeDtypeStruct((B, S, 2 * KV_W), F32),
                   jax.ShapeDtypeStruct((1, ATT_HEADS), F32)],
        input_output_aliases={12: 0},
        compiler_params=_params("arbitrary", "arbitrary"),
    )(qr, kr, kr, z, z, cos, sin, cos, sin, sinks, lse, dob, dz)


def _swa_dkv_combine(dkv_cur, dkv_prev, dz, *, name):
    B, S, W = dkv_cur.shape

    def body(c_ref, p_ref, dz_in, o_ref):
        rows = lax.broadcasted_iota(jnp.int32, (S, W), 0)
        o_ref[...] = (c_ref[...] + _shift_up(p_ref[...], WINDOW, rows, S)).astype(BF16)

    seq = pl.BlockSpec((None, S, W), lambda b: (b, 0, 0))
    return pl.pallas_call(
        body, name=name, grid=(B,),
        in_specs=[seq, seq, ANY], out_specs=pl.BlockSpec((None, S, W), lambda b: (b, 0, O_AKV // W_AKV)),
        out_shape=jax.ShapeDtypeStruct(dz.shape, BF16),
        input_output_aliases={2: 0},
        compiler_params=_params("parallel"),
    )(dkv_cur, dkv_prev, dz)


def _rope_tables(positions):
    half = ROPE_DIM // 2
    inv = ROPE_THETA ** (-2.0 * jnp.arange(half, dtype=F32) / ROPE_DIM)
    ang = positions.astype(F32)[..., None] * inv
    c, s = jnp.cos(ang), jnp.sin(ang)
    pad = jnp.zeros(ang.shape[:-1] + (ATT_HD - ROPE_DIM,), F32)
    cos = jnp.concatenate([c, c, pad + 1.0], axis=-1)
    sin = jnp.concatenate([-s, s, pad], axis=-1)
    return jnp.tile(cos, (1, 1, 2)), jnp.tile(sin, (1, 1, 2))


def _lower_bound(lb_logits, *, name):
    def body(l_ref, o_ref):
        l = l_ref[...]
        e = jnp.exp(l - jnp.max(l, axis=0, keepdims=True))
        o_ref[...] = e[0:1] / jnp.sum(e, axis=0, keepdims=True)

    return pl.pallas_call(body, name=name, out_shape=jax.ShapeDtypeStruct((1, lb_logits.shape[1]), F32))(lb_logits)


W_ZH, W_GATES, W_AQ, W_AKV = 4 * HF, 2 * D_MODEL, ATT_HEADS * ATT_HD, 2 * KV_W
O_ZH, O_GATES, O_AQ, O_AKV = 0, W_ZH, W_ZH + W_GATES, W_ZH + W_GATES + W_AQ
W_IN = W_ZH + W_GATES + W_AQ + W_AKV


W_IN_BLK = W_IN // N_DEV


def _reference_row_block(j, rows=256):
    nz, ng = W_ZH // rows, W_GATES // rows
    return jnp.where(j < nz, j, jnp.where(j < nz + ng, j + (W_AQ + W_AKV) // rows, j - ng))


W_IN_SEGMENTS = ((O_ZH, 0, W_ZH), (O_GATES, W_ZH + W_AQ + W_AKV, W_GATES), (O_AQ, W_ZH, W_AQ + W_AKV))


def _local_step(x, positions, target, small, w_in_t, rest_weights, emit, start_token):
    B, S, D = x.shape
    T = B * S
    x2 = x.reshape(T, D)
    cos, sin = _rope_tables(positions)
    lb = _lower_bound(small["lb_logits"], name="lb_fwd")
    zero = lambda tok: tok[0:1, 0:1]

    u1, z = _norm_matmul(x2, small["norm1_g"] + zero(start_token), w_in_t, W_IN_SEGMENTS, tm=512, name="norm1_mm_z")
    z3 = z.reshape(B, S, W_IN)
    oa, states = _hgrn_fwd(z3, lb, small["hgrn_norm_g"], name="hgrn_fwd")
    ob, lse, qr, kr = _swa_fwd(z3, cos, sin, small["attn_sinks"], name="swa_fwd")
    oa2 = oa.reshape(T, D)
    ob2 = ob.reshape(T, D)
    W = rest_weights("mix", ob)
    row = lambda tm, dtype=None: _row_spec(tm, D)
    tile = lambda dtype: jax.ShapeDtypeStruct((T, D), dtype)
    vec = _full_spec((1, D))
    vec_shape = jax.ShapeDtypeStruct((1, D), F32)

    def merge_ep(acc_a, acc_b, g_ref):
        pa, pb = acc_a.astype(BF16), acc_b.astype(BF16)
        return pa, pb, _merge_fn(g_ref[...], pa, pb)

    pa, pb, merged = _matmul_ep([(oa2, W["w_a"], False, 0), (ob2, W["w_b"], False, 0)], tm=1024, ins=[z], in_specs=[_gates_spec(1024)],
                                out_shapes=[tile(BF16)] * 3, out_specs=[row(1024)] * 3, epilogue=merge_ep, name="mm_pa_pb_merge")

    def resid_norm_ep(acc, x_ref, g_ref):
        hh = acc + x_ref[...]
        return hh, _rms(hh, g_ref[...])

    h, u2 = _matmul_ep([(merged, W["w_out"], False, 0)], tm=1024, ins=[x2, small["norm2_g"]], in_specs=[row(1024), vec],
                       out_shapes=[tile(F32), tile(BF16)], out_specs=[row(1024), row(1024)], epilogue=resid_norm_ep, name="mm_h_norm2")
    W.update(rest_weights("ffn", u2))
    gu3 = _matmul_col_tiles(u2, W["w_ffn_t"], tm=1024, tn=D_FF, tc=CONV_TC, name="mm_gu").reshape(2 * D_FF // CONV_TC, B, S, CONV_TC)
    act, a_pre = _conv_act_fwd(gu3, W["conv_w"], small["conv_b"], name="conv_act_fwd")
    act2 = act.reshape(T, D_FF)
    g = {}

    def loss_ep(acc, h_ref, g_ref, t_ref):
        y, vjp = jax.vjp(_rms, acc + h_ref[...], g_ref[...])
        err = y - t_ref[...]
        dx, dg = vjp(err * (1.0 / D))
        return dx, dx, dg, (0.5 / D) * jnp.sum(jnp.sum(err * err, axis=1, keepdims=True), axis=0, keepdims=True)

    dh2, dh2b, g["final_g"], loss = _matmul_ep(
        [(act2, W["w_down"], False, 0)], tm=512, ins=[h, small["final_g"].reshape(1, D), target.reshape(T, D)], in_specs=[row(512), vec, row(512)],
        out_shapes=[tile(F32), tile(BF16), vec_shape, jax.ShapeDtypeStruct((1, 1), F32)],
        out_specs=[row(512), row(512), vec, _full_spec((1, 1))], sums=(2, 3), epilogue=loss_ep, name="mm_h2_loss")
    dact = _matmul(dh2b, W["w_down"], tb=True, out_dtype=BF16, name="mm_dact", tm=1024, tn=D_FF)
    dw_down_t = _matmul(dh2b, act2, ta=True, out_dtype=BF16, name="mm_dw_down", tm=1024, tn=256, tk=8192)
    dg_, dup, g["conv_w"], g["conv_b"] = _conv_act_bwd(gu3, a_pre, W["conv_w"], dact.reshape(B, S, D_FF), name="conv_act_bwd")
    dg2 = dg_.reshape(T, D_FF)
    dup2 = dup.reshape(T, D_FF)
    dw_ffn_t = _matmul(u2, dg2, ta=True, out_t=True, out_dtype=BF16, into=lax.empty((2 * D_FF, D), BF16), o_noff=0, name="mm_dw_ffn_g", tm=1024, tn=256, tk=8192)
    dw_ffn_t = _matmul(u2, dup2, ta=True, out_t=True, out_dtype=BF16, into=dw_ffn_t, o_noff=D_FF // 256, name="mm_dw_ffn_u", tm=1024, tn=256, tk=8192)
    tok = emit("ffn", dict(w_ffn_t=dw_ffn_t, w_down=dw_down_t.T))
    def norm2_bwd_ep(acc_g, acc_u, h_ref, g_ref, dh2_ref):
        _, vjp = jax.vjp(_rms, h_ref[...], g_ref[...])
        dx, dg = vjp(acc_g + acc_u)
        dx = dx + dh2_ref[...]
        return dx, dx, dg

    dh, dhb, g["norm2_g"] = _matmul_ep(
        [(dg2, W["w_ffn_t"], False, 0), (dup2, W["w_ffn_t"], False, 1)], tm=512, ins=[h, small["norm2_g"] + zero(tok), dh2], in_specs=[row(512), vec, row(512)],
        out_shapes=[tile(F32), tile(BF16), vec_shape], out_specs=[row(512), row(512), vec], sums=(2,), epilogue=norm2_bwd_ep, name="mm_du2_norm2_bwd")
    dw_out = _matmul(merged, dhb, ta=True, out_dtype=BF16, name="mm_dw_out", tm=1024, tn=1024, tk=2048)

    def merge_bwd_ep(acc, g_ref, pa_ref, pb_ref, dz_in):
        gt = g_ref[...].astype(F32)
        sa = _sigmoid(gt[:, :D_MODEL])
        sb = _sigmoid(gt[:, D_MODEL:])
        dgates = jnp.concatenate([acc * pa_ref[...].astype(F32) * sa * (1.0 - sa), acc * pb_ref[...].astype(F32) * sb * (1.0 - sb)], axis=1)
        return dgates, acc * sa, acc * sb

    dz, dpa, dpb = _matmul_ep(
        [(dhb, W["w_out"], True, 0)], tm=512, ins=[z, pa, pb, lax.empty((T, W_IN), BF16)], in_specs=[_gates_spec(512), row(512), row(512), ANY],
        out_shapes=[jax.ShapeDtypeStruct((T, W_IN), BF16), tile(BF16), tile(BF16)], out_specs=[_gates_spec(512), row(512), row(512)],
        aliases={3: 0}, epilogue=merge_bwd_ep, name="mm_dmerged_merge_bwd")
    doa, dob = _matmul_ep([(dpa, W["w_a"], True, 0), (dpb, W["w_b"], True, 0)], tm=1024, ins=[], in_specs=[],
                          out_shapes=[tile(BF16)] * 2, out_specs=[row(1024)] * 2, epilogue=lambda da, db: (da, db), name="mm_doa_dob")
    dw_a = _matmul(oa2, dpa, ta=True, out_dtype=BF16, name="mm_dw_a", tm=1024, tn=1024, tk=2048)
    dw_b = _matmul(ob2, dpb, ta=True, out_dtype=BF16, name="mm_dw_b", tm=1024, tn=1024, tk=2048)
    tok = emit("mix", dict(w_out=dw_out, w_a=dw_a, w_b=dw_b))
    dz3, dkv_cur, dkv_prev, dsinks = _swa_bwd(z3, qr, kr, cos, sin, small["attn_sinks"] + zero(tok), lse, dob.reshape(B, S, D),
                                              dz.reshape(B, S, W_IN), name="swa_bwd")
    dz3 = _swa_dkv_combine(dkv_cur, dkv_prev, dz3, name="swa_dkv")
    g["attn_sinks"] = dsinks
    dz3, g["lb"], g["hgrn_norm_g"] = _hgrn_bwd(z3, lb, small["hgrn_norm_g"], states, doa.reshape(B, S, D), dz3, name="hgrn_bwd")
    dz = dz3.reshape(T, W_IN)
    dw_in_t = _matmul(u1, dz, ta=True, out_t=True, o_block_perm=_reference_row_block, out_dtype=BF16, name="mm_dw_in", tm=1024, tn=256, tk=8192)
    tok = emit("in", dict(w_in_t=dw_in_t))
    dx, g["norm1_g"] = _matmul_norm_bwd(dz, w_in_t, W_IN_SEGMENTS, x2, small["norm1_g"], dh, tok, tm=512, name="mm_du1_norm1_bwd")
    g["lb_logits"] = _lb_bwd(g.pop("lb"), lb, name="lb_bwd")
    return loss, dx.reshape(B, S, D), g


def _my_place():
    return lax.axis_index("x"), lax.axis_index("y"), lax.axis_index("c")


def _gather_blocks(x_ref, out_ref, send_sems, recv_sems, local_sem):
    x, y, c = _my_place()
    me, sibling = (x, y, c), (x, y, 1 - c)
    chips = [(1 - x, y), (x, 1 - y), (1 - x, 1 - y)]
    relayed = tuple(jnp.where(c == 0, a, b) for a, b in zip(chips[0], chips[1]))
    relay_to = tuple(jnp.where(c == 0, b, a) for a, b in zip(chips[0], chips[1]))

    def slot(px, py, pc):
        return out_ref.at[4 * px + 2 * py + pc]

    def copy(k, block, to, src=None):
        return pltpu.make_async_remote_copy(
            src_ref=slot(*block) if src is None else src, dst_ref=slot(*block),
            send_sem=send_sems.at[k], recv_sem=recv_sems.at[k], device_id=to, device_id_type=MESH)

    mine = pltpu.make_async_copy(x_ref, slot(*me), local_sem)
    mine.start()
    first = [copy(0, me, sibling, src=x_ref)]
    first += [copy(1 + j, me, (*chip, c), src=x_ref) for j, chip in enumerate(chips[:2])]
    for cp in first:
        cp.start()
    relay = copy(3, (*relayed, c), (*relay_to, c))
    passed = [copy(4 + j, (*chip, c), sibling) for j, chip in enumerate(chips)]
    for j, chip in enumerate(chips):
        copy(1 + j, (*chip, c), me).wait_recv()
        if j < 2:
            @pl.when(c == j)
            def _():
                relay.start()

        passed[j].start()
    copy(0, sibling, me).wait_recv()
    for j, chip in enumerate(chips):
        copy(4 + j, (*chip, 1 - c), me).wait_recv()
    for cp in first + [relay] + passed:
        cp.wait_send()
    mine.wait()


GATHER_SEMS = [pltpu.SemaphoreType.DMA((7,)), pltpu.SemaphoreType.DMA((7,)), pltpu.SemaphoreType.DMA]


def _all_gather(blk, *, name):
    return pl.pallas_call(
        _gather_body_fn(), name=name,
        out_shape=jax.ShapeDtypeStruct((N_DEV,) + blk.shape, blk.dtype),
        in_specs=[ANY], out_specs=ANY,
        scratch_shapes=GATHER_SEMS,
    )(blk)


def _gather_body_fn():
    def body(x_ref, out_ref, send_sems, recv_sems, local_sem):
        _gather_blocks(x_ref, out_ref, send_sems, recv_sems, local_sem)
    return body


SLAB_W = 1152
SMALL_SHAPES = dict(norm1_g=(1, D_MODEL), lb_logits=(2, HGRN_HEADS * HGRN_DK), hgrn_norm_g=(1, HGRN_DK), attn_sinks=(1, ATT_HEADS),
                    norm2_g=(1, D_MODEL), conv_b=(1, D_FF), final_g=(1, D_MODEL))
CONVW_BLK = D_FF // N_DEV
CONVW_STRIDE = SLAB_W // 3


def _slab_layout():
    layout, r = {}, 0
    for nm, (nr, w) in SMALL_SHAPES.items():
        layout[nm] = []
        for i in range(nr):
            for c0 in range(0, w, SLAB_W):
                layout[nm].append((r, i, c0, min(SLAB_W, w - c0)))
                r += 1
    return layout, r


SMALL_ROWS, _N_SMALL_ROWS = _slab_layout()
CONV_ROW0 = -(-_N_SMALL_ROWS // 8) * 8
LOSS_ROW = CONV_ROW0 + N_DEV
SLAB_ROWS = LOSS_ROW + 8


def _small_step(grads, g_conv_w, loss, params, moments, variances, dev, *, name):
    names = list(SMALL_ROWS)
    n = len(names)

    def body(dev_ref, *refs):
        g_refs = dict(zip(names, refs[:n]))
        gc_ref, loss_ref = refs[n], refs[n + 1]
        base = n + 2
        w_refs, m_refs, v_refs = (dict(zip(names + ["conv_w"], refs[base + i * (n + 1):base + (i + 1) * (n + 1)])) for i in range(3))
        o = base + 3 * (n + 1)
        gath_ref, loss_out = refs[o], refs[o + 1]
        outs = {nm: refs[o + 2 + 4 * i:o + 6 + 4 * i] for i, nm in enumerate(names + ["conv_w"])}
        slab, total, send_sems, recv_sems, local_sem = refs[-5:]

        slab[...] = jnp.zeros_like(slab)
        for nm, pieces in SMALL_ROWS.items():
            for r, i, c0, w in pieces:
                slab[r:r + 1, 0:w] = g_refs[nm][i:i + 1, c0:c0 + w]
        for p in range(N_DEV):
            for j in range(3):
                slab[CONV_ROW0 + p:CONV_ROW0 + p + 1, j * CONVW_STRIDE:j * CONVW_STRIDE + CONVW_BLK] = gc_ref[j:j + 1, p * CONVW_BLK:(p + 1) * CONVW_BLK]
        slab[LOSS_ROW:LOSS_ROW + 1, 0:1] = loss_ref[...]
        _gather_blocks(slab, gath_ref, send_sems, recv_sems, local_sem)
        acc = gath_ref[0]
        for p in range(1, N_DEV):
            acc = acc + gath_ref[p]
        total[...] = acc
        loss_out[...] = total[LOSS_ROW:LOSS_ROW + 1, 0:1]

        def update(nm, g, i, c0, w):
            at = (slice(i, i + 1), slice(c0, c0 + w))
            d, mn, vn = _adamw_math(w_refs[nm][at], g, m_refs[nm][at], v_refs[nm][at])
            for ref, val in zip(outs[nm], (g, d, mn, vn)):
                ref[at] = val

        for nm, pieces in SMALL_ROWS.items():
            for r, i, c0, w in pieces:
                update(nm, total[r:r + 1, 0:w], i, c0, w)
        conv_rows = total[CONV_ROW0:CONV_ROW0 + N_DEV, :]
        rowid = lax.broadcasted_iota(jnp.int32, conv_rows.shape, 0)
        mine = jnp.sum(jnp.where(rowid == dev_ref[0], conv_rows, 0.0), axis=0, keepdims=True)
        for j in range(3):
            update("conv_w", mine[:, j * CONVW_STRIDE:j * CONVW_STRIDE + CONVW_BLK], j, 0, CONVW_BLK)

    order = names + ["conv_w"]
    ins = [grads[nm] for nm in names] + [g_conv_w, loss]
    for d in (params, moments, variances):
        ins += [d[nm] for nm in order]
    vmem = pl.BlockSpec(memory_space=pltpu.VMEM)
    out_shape = [jax.ShapeDtypeStruct((N_DEV, SLAB_ROWS, SLAB_W), F32), jax.ShapeDtypeStruct((1, 1), F32)]
    for nm in order:
        out_shape += [jax.ShapeDtypeStruct(params[nm].shape, F32)] * 4
    res = pl.pallas_call(
        body, name=name,
        grid_spec=pltpu.PrefetchScalarGridSpec(
            num_scalar_prefetch=1, grid=(1,),
            in_specs=[vmem] * len(ins), out_specs=[vmem] * len(out_shape),
            scratch_shapes=[pltpu.VMEM((SLAB_ROWS, SLAB_W), F32), pltpu.VMEM((SLAB_ROWS, SLAB_W), F32)] + GATHER_SEMS),
        out_shape=out_shape,
    )(dev, *ins)
    return res[1], {nm: tuple(res[2 + 4 * i:6 + 4 * i]) for i, nm in enumerate(order)}


HBM_SPEC = pl.BlockSpec(memory_space=pltpu.HBM)
SEM_SPEC = pl.BlockSpec(memory_space=pltpu.SEMAPHORE)
DATAFLOW_EFFECT = pltpu.SideEffectType.DATAFLOW_SIDE_EFFECTING
N_PEERS = N_DEV - 1


def _peers(x, y, c):
    return [(1 - x if r & 4 else x, 1 - y if r & 2 else y, 1 - c if r & 1 else c) for r in range(1, N_DEV)]


def _exchange_start(srcs, scatter, *, after=None, name):
    n = len(srcs)
    lands = [lax.empty(a.shape if scatter else (N_DEV,) + a.shape, a.dtype) for a in srcs]
    extra = [] if after is None else [after]

    def body(*refs):
        src_refs, land_refs = refs[:n], refs[n:2 * n]
        send_sems, recv_sems, token = refs[2 * n + len(extra)], refs[2 * n + len(extra) + 1], refs[-1]
        x, y, c = _my_place()
        me = 4 * x + 2 * y + c
        for i in range(n):
            for r, (tx, ty, tc) in enumerate(_peers(x, y, c)):
                src = src_refs[i].at[4 * tx + 2 * ty + tc] if scatter else src_refs[i]
                pltpu.make_async_remote_copy(
                    src_ref=src, dst_ref=land_refs[i].at[me], send_sem=send_sems.at[N_PEERS * i + r],
                    recv_sem=recv_sems.at[N_PEERS * i + r], device_id=(tx, ty, tc), device_id_type=MESH).start()
        token[...] = jnp.zeros_like(token)

    thru = [pltpu.HBM(a.shape, a.dtype) for a in list(srcs) + lands]
    res = pl.pallas_call(
        body, name=name,
        out_shape=(pltpu.SemaphoreType.DMA((N_PEERS * n,)), pltpu.SemaphoreType.DMA((N_PEERS * n,)), *thru,
                   jax.ShapeDtypeStruct((8, 128), F32)),
        in_specs=[HBM_SPEC] * (2 * n) + [ANY] * len(extra),
        out_specs=(SEM_SPEC, SEM_SPEC, *([HBM_SPEC] * (2 * n)), pl.BlockSpec(memory_space=pltpu.VMEM)),
        input_output_aliases={i: 2 + i for i in range(2 * n)},
        compiler_params=pltpu.CompilerParams(has_side_effects=DATAFLOW_EFFECT),
    )(*[pltpu.with_memory_space_constraint(a, pltpu.HBM) for a in list(srcs) + lands], *extra)
    return (res[0], res[1], list(res[2:2 + n]), list(res[2 + n:2 + 2 * n]), scatter), res[-1]


def _exchange_wait(handle, after, *, name):
    send_sems, recv_sems, srcs, lands, scatter = handle
    n = len(srcs)

    def body(*refs):
        src_refs, land_refs = refs[:n], refs[n:2 * n]
        send_sems, recv_sems = refs[2 * n], refs[2 * n + 1]
        x, y, c = _my_place()
        for i in range(n):
            for r in range(N_PEERS):
                src = src_refs[i].at[0] if scatter else src_refs[i]
                cp = pltpu.make_async_remote_copy(
                    src_ref=src, dst_ref=land_refs[i].at[0], send_sem=send_sems.at[N_PEERS * i + r],
                    recv_sem=recv_sems.at[N_PEERS * i + r], device_id=(x, y, c), device_id_type=MESH)
                cp.wait_send()
                cp.wait_recv()

    thru = [pltpu.HBM(a.shape, a.dtype) for a in srcs + lands]
    res = pl.pallas_call(
        body, name=name, out_shape=tuple(thru),
        in_specs=[HBM_SPEC] * (2 * n) + [SEM_SPEC, SEM_SPEC, ANY], out_specs=tuple([HBM_SPEC] * (2 * n)),
        input_output_aliases={i: i for i in range(2 * n)},
        compiler_params=pltpu.CompilerParams(has_side_effects=DATAFLOW_EFFECT),
    )(*srcs, *lands, send_sems, recv_sems, after)
    return list(res[:n]), list(res[n:])


def _with_own(land, own, me):
    return lax.dynamic_update_index_in_dim(land, own, me, 0)


def _adamw_math(w, g, m, v):
    m = ADAM_B1 * m + (1.0 - ADAM_B1) * g
    v = ADAM_B2 * v + (1.0 - ADAM_B2) * (g * g)
    m_hat = m / (1.0 - ADAM_B1 ** ADAM_STEP)
    v_hat = v / (1.0 - ADAM_B2 ** ADAM_STEP)
    delta = -ADAM_LR * (m_hat / (jnp.sqrt(v_hat) + ADAM_EPS) + ADAM_WD * w)
    return delta, m, v


def _adamw_sum(parts, w, m, v, *, name):
    shape = w.shape
    R, n = shape[-2], shape[-1]
    w, m, v = (t.reshape(R, n) for t in (w, m, v))
    tr = _pick(R, (256, 464, 352, 128))

    def body(p_ref, w_ref, m_ref, v_ref, g_ref, d_ref, mo_ref, vo_ref):
        g = p_ref[0].astype(F32)
        for p in range(1, N_DEV):
            g = g + p_ref[p].astype(F32)
        d, mn, vn = _adamw_math(w_ref[...], g, m_ref[...], v_ref[...])
        g_ref[...] = g
        d_ref[...] = d
        mo_ref[...] = mn
        vo_ref[...] = vn

    row = pl.BlockSpec((tr, n), lambda i: (i, 0))
    outs = pl.pallas_call(
        body, name=name, grid=(R // tr,),
        in_specs=[pl.BlockSpec((N_DEV, tr, n), lambda i: (0, i, 0)), row, row, row],
        out_specs=[row, row, row, row],
        out_shape=[jax.ShapeDtypeStruct((R, n), F32)] * 4,
        compiler_params=_params("parallel"),
    )(parts, w, m, v)
    return [t.reshape(shape) for t in outs]


def _lb_bwd(dlb, lb, *, name):
    def body(d_ref, lb_ref, o_ref):
        t = d_ref[...] * lb_ref[...] * (1.0 - lb_ref[...])
        o_ref[0:1, :] = t
        o_ref[1:2, :] = -t

    return pl.pallas_call(body, name=name, out_shape=jax.ShapeDtypeStruct((2, lb.shape[1]), F32))(dlb, lb)


DOWN_BLK, ROW_BLK = D_FF // N_DEV, D_MODEL // N_DEV
W_FFN_BLK = 2 * D_FF // N_DEV
CONV_BITS_SHAPE = (16, 256)


def kernel(x, positions, norm1_g, w_in, lb_logits, hgrn_norm_g, w_a, attn_sinks, w_b, w_out, norm2_g, w_ffn_in, conv_w, conv_b, w_down, final_g, loss_target, m_norm1_g, m_w_in, m_lb_logits, m_hgrn_norm_g, m_w_a, m_attn_sinks, m_w_b, m_w_out, m_norm2_g, m_w_ffn_in, m_conv_w, m_conv_b, m_w_down, m_final_g, v_norm1_g, v_w_in, v_lb_logits, v_hgrn_norm_g, v_w_a, v_attn_sinks, v_w_b, v_w_out, v_norm2_g, v_w_ffn_in, v_conv_w, v_conv_b, v_w_down, v_final_g):
    xi, yi, ci = _my_place()
    dev = 4 * xi + 2 * yi + ci

    tr = lambda t: jnp.transpose(t[0])
    untr = lambda t: jnp.transpose(t)[None]
    w_in_blocks = _all_gather(tr(w_in).astype(BF16), name="ag_w_in")
    conv_bits = lax.bitcast_convert_type(conv_w, BF16).reshape(-1)
    conv_bits = jnp.pad(conv_bits, (0, CONV_BITS_SHAPE[0] * CONV_BITS_SHAPE[1] - conv_bits.shape[0])).reshape(CONV_BITS_SHAPE)
    w_in_full_t = w_in_blocks.reshape(W_IN, D_MODEL)
    gather_handles = {}
    gather_handles["mix"], tok_mix = _exchange_start([w_a[0].astype(BF16), w_b[0].astype(BF16), w_out[0].astype(BF16)], False,
                                                     after=w_in_full_t, name="ag_mix_start")
    gather_handles["ffn"], tok_ffn = _exchange_start([tr(w_ffn_in).astype(BF16), w_down[0].astype(BF16), conv_bits], False,
                                                     after=tok_mix, name="ag_ffn_start")
    start_token = tok_mix + tok_ffn

    def rest_weights(group, after):
        own, lands = _exchange_wait(gather_handles[group], after, name="ag_" + group + "_wait")
        full = [_with_own(l, o, dev) for l, o in zip(lands, own)]
        if group == "mix":
            return dict(zip(("w_a", "w_b", "w_out"), [t.reshape(D_MODEL, D_MODEL) for t in full]))
        bits = full[2].reshape(N_DEV, -1)[:, :3 * CONVW_BLK * 2].reshape(N_DEV, 3, CONVW_BLK, 2)
        return dict(w_ffn_t=full[0].reshape(2 * D_FF, D_MODEL), w_down=full[1].reshape(D_FF, D_MODEL),
                    conv_w=lax.bitcast_convert_type(bits, F32).transpose(1, 0, 2).reshape(3, D_FF))

    handles = {}

    def emit(group, gr):
        if group == "ffn":
            srcs = [gr["w_ffn_t"].reshape(N_DEV, W_FFN_BLK, D_MODEL), gr["w_down"].reshape(N_DEV, DOWN_BLK, D_MODEL)]
        elif group == "mix":
            srcs = [gr[n].reshape(N_DEV, ROW_BLK, D_MODEL) for n in ("w_out", "w_a", "w_b")]
        else:
            srcs = [gr["w_in_t"].reshape(N_DEV, W_IN_BLK, D_MODEL)]
        handles[group], token = _exchange_start(srcs, True, name="rs_" + group + "_start")
        return token

    small = dict(norm1_g=norm1_g, lb_logits=lb_logits, hgrn_norm_g=hgrn_norm_g, attn_sinks=attn_sinks, norm2_g=norm2_g,
                 conv_b=conv_b, final_g=final_g)
    loss, grad_x, g = _local_step(x, positions, loss_target, small, w_in_full_t, rest_weights, emit, start_token)

    def parts_of(group, after):
        srcs, lands = _exchange_wait(handles[group], after, name="rs_" + group + "_wait")
        return [_with_own(l, lax.dynamic_index_in_dim(s, dev, 0, keepdims=False), dev) for s, l in zip(srcs, lands)]

    p_ffn, p_down = parts_of("ffn", grad_x)
    p_out, p_a, p_b = parts_of("mix", grad_x)
    (p_in,) = parts_of("in", grad_x)
    big = dict(
        w_in=[untr(t) for t in _adamw_sum(p_in, tr(w_in), tr(m_w_in), tr(v_w_in), name="adamw_w_in")],
        w_a=_adamw_sum(p_a, w_a, m_w_a, v_w_a, name="adamw_w_a"),
        w_b=_adamw_sum(p_b, w_b, m_w_b, v_w_b, name="adamw_w_b"),
        w_out=_adamw_sum(p_out, w_out, m_w_out, v_w_out, name="adamw_w_out"),
        w_ffn_in=[untr(t) for t in _adamw_sum(p_ffn, tr(w_ffn_in), tr(m_w_ffn_in), tr(v_w_ffn_in), name="adamw_w_ffn_in")],
        w_down=_adamw_sum(p_down, w_down, m_w_down, v_w_down, name="adamw_w_down"),
    )

    row = lambda t: t.reshape(1, -1) if t.ndim == 1 else t
    shard = lambda t: t.reshape(3, CONVW_BLK)
    sm_g = {nm: g[nm] for nm in SMALL_ROWS}
    sm_w = dict(norm1_g=norm1_g, lb_logits=lb_logits, hgrn_norm_g=hgrn_norm_g, attn_sinks=attn_sinks, norm2_g=norm2_g,
                conv_b=conv_b, final_g=row(final_g), conv_w=shard(conv_w))
    sm_m = dict(norm1_g=m_norm1_g, lb_logits=m_lb_logits, hgrn_norm_g=m_hgrn_norm_g, attn_sinks=m_attn_sinks, norm2_g=m_norm2_g,
                conv_b=m_conv_b, final_g=row(m_final_g), conv_w=shard(m_conv_w))
    sm_v = dict(norm1_g=v_norm1_g, lb_logits=v_lb_logits, hgrn_norm_g=v_hgrn_norm_g, attn_sinks=v_attn_sinks, norm2_g=v_norm2_g,
                conv_b=v_conv_b, final_g=row(v_final_g), conv_w=shard(v_conv_w))
    loss_total, sm_out = _small_step(sm_g, g["conv_w"], loss, sm_w, sm_m, sm_v, dev.astype(jnp.int32).reshape(1), name="small_step")
    shapes = dict(final_g=final_g.shape, conv_w=conv_w.shape)

    names = ("norm1_g", "w_in", "lb_logits", "hgrn_norm_g", "w_a", "attn_sinks", "w_b", "w_out", "norm2_g", "w_ffn_in", "conv_w", "conv_b", "w_down", "final_g")
    outs = [loss_total.reshape(()), grad_x]
    for kind in range(4):
        outs += [big[n][kind] if n in big else sm_out[n][kind].reshape(shapes.get(n, sm_out[n][kind].shape)) for n in names]
    return tuple(outs)
```

```python
import jax
import jax.numpy as jnp
from jax import lax
from jax.experimental import pallas as pl
from jax.experimental.pallas import tpu as pltpu

F32 = jnp.float32
BF16 = jnp.bfloat16

D_MODEL = 1024
HGRN_HEADS = 8
HGRN_DK = 128
CHUNK = 64
ATT_HEADS = 16
ATT_KV_HEADS = 2
ATT_HD = 64
ATT_GROUP = ATT_HEADS // ATT_KV_HEADS
WINDOW = 128
ROPE_DIM = ATT_HD // 4
ROPE_THETA = 500000.0
D_FF = 2816
EPS = 1e-6
NEG_INF = -1e30
N_DEV = 8

ADAM_LR = 0.001
ADAM_B1 = 0.9
ADAM_B2 = 0.999
ADAM_EPS = 1e-08
ADAM_WD = 0.01
ADAM_STEP = 10

MESH = pl.DeviceIdType.MESH
ANY = pl.BlockSpec(memory_space=pl.ANY)


def _pick(n, cands):
    for c in cands:
        if n % c == 0:
            return c
    return n


def _sigmoid(x):
    return 0.5 * jnp.tanh(0.5 * x) + 0.5


def _silu(x):
    hx = 0.5 * x
    return hx * jnp.tanh(hx) + hx


def _rms(x, g):
    return x * lax.rsqrt(jnp.mean(x * x, axis=-1, keepdims=True) + EPS) * g


def _dot(a, b, dims):
    return lax.dot_general(a, b, (dims, ((), ())), preferred_element_type=F32)


def _nn(a, b):
    return _dot(a, b, ((1,), (0,)))


def _nt(a, b):
    return _dot(a, b, ((1,), (1,)))


def _tn(a, b):
    return _dot(a, b, ((0,), (0,)))


def _params(*sem):
    return pltpu.CompilerParams(dimension_semantics=sem, vmem_limit_bytes=56 * 1024 * 1024)


def _matmul(a, b, *, ta=False, tb=False, out_dtype=F32, addend=None, after=None, into=None, o_noff=0, out_t=False,
            o_block_perm=lambda j: j, name, tm, tn, tk=None, n_extent=None, b_koff=0, b_noff=0):
    M, K = (a.shape[1], a.shape[0]) if ta else a.shape
    N = n_extent or (b.shape[0] if tb else b.shape[1])
    tm, tn, tk = min(tm, M), min(tn, N), min(tk or K, K)
    assert M % tm == 0 and N % tn == 0 and K % tk == 0, (name, M, N, K, tm, tn, tk)
    nk = K // tk
    use_scratch = nk > 1 and out_dtype != F32
    grid = (M // tm, N // tn, nk)
    a_spec = pl.BlockSpec((tk, tm), lambda i, j, k: (k, i)) if ta else pl.BlockSpec((tm, tk), lambda i, j, k: (i, k))
    b_spec = pl.BlockSpec((tn, tk), lambda i, j, k: (j + b_noff, k + b_koff)) if tb else pl.BlockSpec((tk, tn), lambda i, j, k: (k + b_koff, j + b_noff))
    o_spec = pl.BlockSpec((tm, tn), lambda i, j, k: (i, j))
    dims = ((0 if ta else 1,), (1 if tb else 0,))
    has_add = addend is not None

    n_in = 2 + has_add + (after is not None) + (into is not None)

    def body(*refs):
        a_ref, b_ref = refs[:2]
        c_ref = refs[2] if has_add else None
        o_ref = refs[n_in]
        part = _dot(a_ref[...], b_ref[...], dims)
        if nk == 1:
            if has_add:
                part = part + c_ref[...].astype(F32)
            o_ref[...] = (part.T if out_t else part).astype(out_dtype)
        else:
            acc_ref = refs[-1] if use_scratch else o_ref
            k = pl.program_id(2)

            @pl.when(k == 0)
            def _():
                acc_ref[...] = part + c_ref[...].astype(F32) if has_add else part

            @pl.when(k > 0)
            def _():
                acc_ref[...] += part

            if use_scratch:
                @pl.when(k == nk - 1)
                def _():
                    o_ref[...] = acc_ref[...].astype(out_dtype)

    in_specs = [a_spec, b_spec] + ([o_spec] if has_add else [])
    args = (a, b) + ((addend,) if has_add else ())
    if after is not None:
        in_specs.append(pl.BlockSpec(after.shape, lambda i, j, k: (0, 0)))
        args += (after,)
    aliases = {}
    if into is not None:
        in_specs.append(ANY)
        args += (into,)
        aliases = {len(args) - 1: 0}
    if out_t:
        assert nk == 1 and not has_add
        o_spec = pl.BlockSpec((tn, tm), lambda i, j, k: (o_block_perm(j) + o_noff, i))
    elif into is not None:
        o_spec = pl.BlockSpec((tm, tn), lambda i, j, k: (i, j + o_noff))
    return pl.pallas_call(
        body,
        name=name,
        grid=grid,
        in_specs=in_specs,
        out_specs=o_spec,
        out_shape=jax.ShapeDtypeStruct(into.shape if into is not None else ((N, M) if out_t else (M, N)), out_dtype),
        input_output_aliases=aliases,
        scratch_shapes=[pltpu.VMEM((tm, tn), F32)] if use_scratch else [],
        compiler_params=_params("parallel", "parallel", "arbitrary"),
    )(*args)


def _matmul_col_tiles(a, b_t, *, tm, tn, tc, name):
    M, K = a.shape
    N = b_t.shape[0]
    tm = min(tm, M)
    per_step = tn // tc

    def body(a_ref, b_ref, o_ref):
        res = _nt(a_ref[...], b_ref[...]).astype(BF16)
        for t in range(per_step):
            o_ref[t] = res[:, t * tc:(t + 1) * tc]

    return pl.pallas_call(
        body, name=name, grid=(M // tm, N // tn),
        in_specs=[pl.BlockSpec((tm, K), lambda i, j: (i, 0)), pl.BlockSpec((tn, K), lambda i, j: (j, 0))],
        out_specs=pl.BlockSpec((per_step, tm, tc), lambda i, j: (j, i, 0)),
        out_shape=jax.ShapeDtypeStruct((N // tc, M, tc), BF16),
        compiler_params=_params("parallel", "parallel"),
    )(a, b_t)


EPILOGUE_ROWS = 256


def _matmul_ep(pairs, *, tm, ins, in_specs, out_shapes, out_specs, sums=(), epilogue, aliases=None, name):
    M = pairs[0][0].shape[0]
    tm = min(tm, M)
    mm_specs, mm_args, dims = [], [], []
    for a, b, tb, koff in pairs:
        K = a.shape[1]
        N = b.shape[0] if tb else b.shape[1]
        mm_specs += [pl.BlockSpec((tm, K), lambda i: (i, 0)),
                     pl.BlockSpec((N, K), lambda i, koff=koff: (0, koff), pipeline_mode=pl.Buffered(1)) if tb else
                     pl.BlockSpec((K, N), lambda i, koff=koff: (koff, 0), pipeline_mode=pl.Buffered(1))]
        mm_args += [a, b]
        dims.append(((1,), (1 if tb else 0,)))
    n_mm = len(mm_args)
    n_in = n_mm + len(ins)
    rows = min(EPILOGUE_ROWS, tm)

    def body(*refs):
        in_refs, out_refs = refs[n_mm:n_in], refs[n_in:]

        def products(s):
            return [_dot(refs[2 * p][s * rows:(s + 1) * rows, :], refs[2 * p + 1][...], dims[p]) for p in range(len(pairs))]

        totals = {}
        accs = products(0)
        for s in range(tm // rows):
            ahead = products(s + 1) if (s + 1) * rows < tm else None
            outs = epilogue(*accs, *[r.at[pl.ds(s * rows, rows)] if r.shape[0] == tm else r for r in in_refs])
            for k, (ref, val) in enumerate(zip(out_refs, outs)):
                if val is None:
                    continue
                if k in sums:
                    totals[k] = val if s == 0 else totals[k] + val
                else:
                    ref[s * rows:(s + 1) * rows, :] = val.astype(ref.dtype)
            accs = ahead
        for k, total in totals.items():
            ref = out_refs[k]

            @pl.when(pl.program_id(0) == 0)
            def _():
                ref[...] = jnp.zeros_like(ref)

            ref[...] += total

    return pl.pallas_call(
        body, name=name, grid=(M // tm,),
        in_specs=mm_specs + list(in_specs),
        out_specs=list(out_specs), out_shape=list(out_shapes),
        input_output_aliases={n_mm + k: v for k, v in (aliases or {}).items()},
        compiler_params=_params("arbitrary"),
    )(*mm_args, *ins)


def _row_spec(tm, n):
    return pl.BlockSpec((tm, n), lambda i: (i, 0))


def _full_spec(shape):
    return pl.BlockSpec(shape, lambda i: tuple(0 for _ in shape))


MAX_DOT_COLS = 2048


def _resident_spec(shape):
    return pl.BlockSpec(shape, lambda i: tuple(0 for _ in shape), pipeline_mode=pl.Buffered(1))


def _norm_matmul(x, g, w_t, segments, *, tm, name):
    T, D = x.shape
    N = w_t.shape[0]
    tm = min(tm, T)
    chunks = [(c + o, r + o, min(MAX_DOT_COLS, n - o)) for c, r, n in segments for o in range(0, n, MAX_DOT_COLS)]

    def body(x_ref, g_ref, w_ref, u_ref, z_ref):
        u = _rms(x_ref[...], g_ref[...]).astype(BF16)
        u_ref[...] = u
        for c, r, n in chunks:
            z_ref[:, c:c + n] = _nt(u, w_ref[r:r + n, :]).astype(BF16)

    return pl.pallas_call(
        body, name=name, grid=(T // tm,),
        in_specs=[_row_spec(tm, D), _full_spec((1, D)), _resident_spec((N, D))],
        out_specs=[_row_spec(tm, D), _row_spec(tm, N)],
        out_shape=[jax.ShapeDtypeStruct((T, D), BF16), jax.ShapeDtypeStruct((T, N), BF16)],
        compiler_params=_params("parallel"),
    )(x, g, w_t)


def _matmul_norm_bwd(dz, w_t, segments, x, g, dres, after, *, tm, name):
    T, K = dz.shape
    D = w_t.shape[1]
    tm = min(tm, T)

    def body(dz_ref, w_ref, x_ref, g_ref, dr_ref, after_ref, dx_ref, dg_ref):
        @pl.when(pl.program_id(0) == 0)
        def _():
            dg_ref[...] = jnp.zeros_like(dg_ref)

        du = sum(_nn(dz_ref[:, c:c + n], w_ref[r:r + n, :]) for c, r, n in segments)
        _, vjp = jax.vjp(_rms, x_ref[...], g_ref[...])
        dx, dg = vjp(du)
        dx_ref[...] = dx + dr_ref[...]
        dg_ref[...] += dg

    row = _row_spec(tm, D)
    return pl.pallas_call(
        body, name=name, grid=(T // tm,),
        in_specs=[_row_spec(tm, K), _resident_spec((K, D)), row, _full_spec((1, D)), row, _full_spec(after.shape)],
        out_specs=[row, _full_spec((1, D))],
        out_shape=[jax.ShapeDtypeStruct((T, D), F32), jax.ShapeDtypeStruct((1, D), F32)],
        compiler_params=_params("arbitrary"),
    )(dz, w_t, x, g, dres, after)


def _norm_bwd_add(x, g, du, dres, *, with_bf16=True, name):
    T, D = x.shape
    tm = _pick(T, (512, 256, 128))

    def body(x_ref, g_ref, du_ref, dr_ref, dx_ref, *rest):
        dg_ref = rest[-1]
        _, vjp = jax.vjp(_rms, x_ref[...], g_ref[...])
        dx, dg = vjp(du_ref[...].astype(F32))
        dx = dx + dr_ref[...]
        dx_ref[...] = dx
        if with_bf16:
            rest[0][...] = dx.astype(BF16)

        @pl.when(pl.program_id(0) == 0)
        def _():
            dg_ref[...] = jnp.zeros_like(dg_ref)

        dg_ref[...] += dg

    row = _row_spec(tm, D)
    return pl.pallas_call(
        body, name=name, grid=(T // tm,),
        in_specs=[row, _full_spec((1, D)), row, row],
        out_specs=[row] + ([row] if with_bf16 else []) + [_full_spec((1, D))],
        out_shape=[jax.ShapeDtypeStruct((T, D), F32)] + ([jax.ShapeDtypeStruct((T, D), BF16)] if with_bf16 else []) + [jax.ShapeDtypeStruct((1, D), F32)],
        compiler_params=_params("arbitrary"),
    )(x, g, du, dres)


def _merge_fn(gates, a, b):
    ga = gates[:, :D_MODEL].astype(F32)
    gb = gates[:, D_MODEL:].astype(F32)
    return _sigmoid(ga) * a.astype(F32) + _sigmoid(gb) * b.astype(F32)


def _gates_spec(tm):
    return pl.BlockSpec((tm, W_GATES), lambda i: (i, O_GATES // W_GATES))


CONV_TC = 256


def _shift_down(x, n, rows):
    return jnp.where(rows >= n, pltpu.roll(x, n, 0), 0.0)


def _shift_up(x, n, rows, S):
    return jnp.where(rows < S - n, pltpu.roll(x, S - n, 0), 0.0)


def _conv_act_fwd(gu, conv_w, conv_b, *, name):
    _, B, S, tc = gu.shape
    nc = D_FF // tc

    def body(g_ref, up_ref, w_ref, b_ref, o_ref, a_ref):
        g = g_ref[...].astype(F32)
        rows = lax.broadcasted_iota(jnp.int32, g.shape, 0)
        w = w_ref[...]
        a = w[2:3] * g + w[1:2] * _shift_down(g, 1, rows) + w[0:1] * _shift_down(g, 2, rows) + b_ref[...]
        o_ref[...] = (_silu(a) * up_ref[...].astype(F32)).astype(BF16)
        a_ref[...] = a.astype(BF16)

    col = pl.BlockSpec((None, S, tc), lambda b, j: (b, 0, j))
    tile = lambda off: pl.BlockSpec((None, None, S, tc), lambda b, j: (j + off, b, 0, 0))
    return pl.pallas_call(
        body, name=name, grid=(B, nc),
        in_specs=[tile(0), tile(nc),
                  pl.BlockSpec((3, tc), lambda b, j: (0, j)),
                  pl.BlockSpec((1, tc), lambda b, j: (0, j))],
        out_specs=[col, tile(0)],
        out_shape=[jax.ShapeDtypeStruct((B, S, D_FF), BF16), jax.ShapeDtypeStruct((nc, B, S, tc), BF16)],
        compiler_params=_params("parallel", "parallel"),
    )(gu, gu, conv_w, conv_b)


def _conv_act_bwd(gu, a_pre, conv_w, dact, *, name):
    _, B, S, tc = gu.shape
    nc = D_FF // tc

    def body(g_ref, up_ref, a_ref, w_ref, da_ref, dg_ref, dup_ref, dw_ref, db_ref):
        g = g_ref[...].astype(F32)
        up, a, dact = up_ref[...], a_ref[...], da_ref[...]
        rows = lax.broadcasted_iota(jnp.int32, g.shape, 0)
        w = w_ref[...]
        sg = _sigmoid(a)
        dup_ref[...] = dact * a * sg
        da = (dact * up * sg * (1.0 + a * (1.0 - sg))).astype(F32)
        da1 = _shift_up(da, 1, rows, S)
        da2 = _shift_up(da, 2, rows, S)
        dg_ref[...] = (w[2:3] * da + w[1:2] * da1 + w[0:1] * da2).astype(BF16)

        @pl.when(pl.program_id(1) == 0)
        def _():
            dw_ref[...] = jnp.zeros_like(dw_ref)
            db_ref[...] = jnp.zeros_like(db_ref)

        dw_ref[0:1, :] += jnp.sum(da2 * g, axis=0, keepdims=True)
        dw_ref[1:2, :] += jnp.sum(da1 * g, axis=0, keepdims=True)
        dw_ref[2:3, :] += jnp.sum(da * g, axis=0, keepdims=True)
        db_ref[...] += jnp.sum(da, axis=0, keepdims=True)

    col = pl.BlockSpec((None, S, tc), lambda j, b: (b, 0, j))
    tile = lambda off: pl.BlockSpec((None, None, S, tc), lambda j, b: (j + off, b, 0, 0))
    return pl.pallas_call(
        body, name=name, grid=(nc, B),
        in_specs=[tile(0), tile(nc), tile(0),
                  pl.BlockSpec((3, tc), lambda j, b: (0, j)),
                  col],
        out_specs=[col, col, pl.BlockSpec((3, tc), lambda j, b: (0, j)), pl.BlockSpec((1, tc), lambda j, b: (0, j))],
        out_shape=[jax.ShapeDtypeStruct((B, S, D_FF), BF16), jax.ShapeDtypeStruct((B, S, D_FF), BF16),
                   jax.ShapeDtypeStruct((3, D_FF), F32), jax.ShapeDtypeStruct((1, D_FF), F32)],
        compiler_params=_params("parallel", "arbitrary"),
    )(gu, gu, a_pre, conv_w, dact)


HGRN_CPB = 8
HF = HGRN_HEADS * HGRN_DK


def _tri(n, upper=False):
    r = lax.broadcasted_iota(jnp.int32, (n, n), 0)
    c = lax.broadcasted_iota(jnp.int32, (n, n), 1)
    return (c >= r) if upper else (r >= c)


def _hs(h):
    return slice(h * HGRN_DK, (h + 1) * HGRN_DK)


def _cumsum_rows(tri_b, x):
    hi = x.astype(BF16)
    lo = (x - hi.astype(F32)).astype(BF16)
    return _nn(tri_b, hi) + _nn(tri_b, lo)


def _hgrn_col(seg, h):
    return slice(seg * HF + h * HGRN_DK, seg * HF + (h + 1) * HGRN_DK)


def _hgrn_gates(q, fz, lb):
    sg = _sigmoid(fz)
    return _sigmoid(q), sg, lb + (1.0 - lb) * sg


def _hgrn_decays(b, q, sq, f):
    qf = q * sq
    k = 1.0 - f
    bref = b[CHUNK // 2:CHUNK // 2 + 1, :]
    blast = b[CHUNK - 1:CHUNK, :]
    e1 = jnp.exp2(b - bref)
    e2 = jnp.exp2(bref - b)
    e3 = e1 * jnp.exp2(bref)
    e4 = e2 * jnp.exp2(blast - bref)
    return (e1, e2, e3, e4), qf * e1, k * e2, qf * e3, k * e4, jnp.exp2(blast)


def _hgrn_fwd(zh, lb, gn, *, name):
    B, S, _ = zh.shape
    cpb = HGRN_CPB
    ts = cpb * CHUNK
    nblk = S // ts

    def body(z_ref, lb_ref, gn_ref, o_ref, st_ref, state):
        @pl.when(pl.program_id(1) == 0)
        def _():
            state[...] = jnp.zeros_like(state)

        R = range(HGRN_HEADS)
        causal = _tri(CHUNK)
        tril_b = causal.astype(BF16)
        lbh = [lb_ref[:, _hs(h)] for h in R]
        for c in range(cpb):
            rows = slice(c * CHUNK, (c + 1) * CHUNK)
            q = [z_ref[rows, _hgrn_col(0, h)].astype(F32) for h in R]
            gates = [_hgrn_gates(q[h], z_ref[rows, _hgrn_col(1, h)].astype(F32), lbh[h]) for h in R]
            b = [_cumsum_rows(tril_b, jnp.log2(gates[h][2])) for h in R]
            v = [z_ref[rows, _hgrn_col(2, h)] for h in R]
            dec, q_in, k_in, q_out, k_st = [], [], [], [], []
            for h in R:
                _, qi, ki, qo, ks, d = _hgrn_decays(b[h], q[h], gates[h][0], gates[h][2])
                dec.append(d)
                for lst, t in zip((q_in, k_in, q_out, k_st), (qi, ki, qo, ks)):
                    lst.append(t.astype(BF16))
            a = [jnp.where(causal, _nt(q_in[h], k_in[h]), 0.0).astype(BF16) for h in R]
            st = [state[h] for h in R]
            for h in R:
                st_ref[c, h] = st[h]
            o = [_nn(a[h], v[h]) + _nt(q_out[h], st[h].astype(BF16)) for h in R]
            for h in R:
                state[h] = st[h] * dec[h] + _tn(v[h], k_st[h])
            for h in R:
                o_ref[rows, _hs(h)] = (_rms(o[h], gn_ref[...]) * _silu(z_ref[rows, _hgrn_col(3, h)].astype(F32))).astype(BF16)

    return pl.pallas_call(
        body, name=name, grid=(B, nblk),
        in_specs=[pl.BlockSpec((None, ts, 4 * HF), lambda b, s: (b, s, 0)),
                  pl.BlockSpec((1, HF), lambda b, s: (0, 0)),
                  pl.BlockSpec((1, HGRN_DK), lambda b, s: (0, 0))],
        out_specs=[pl.BlockSpec((None, ts, HF), lambda b, s: (b, s, 0)),
                   pl.BlockSpec((None, cpb, HGRN_HEADS, HGRN_DK, HGRN_DK), lambda b, s: (b, s, 0, 0, 0))],
        out_shape=[jax.ShapeDtypeStruct((B, S, HF), BF16),
                   jax.ShapeDtypeStruct((B, S // CHUNK, HGRN_HEADS, HGRN_DK, HGRN_DK), F32)],
        scratch_shapes=[pltpu.VMEM((HGRN_HEADS, HGRN_DK, HGRN_DK), F32)],
        compiler_params=_params("arbitrary", "arbitrary"),
    )(zh, lb, gn)


def _hgrn_bwd(zh, lb, gn, states, doa, dz, *, name):
    B, S, _ = zh.shape
    cpb = HGRN_CPB
    ts = cpb * CHUNK
    nblk = S // ts
    rev = lambda b, s: (b, nblk - 1 - s, 0)

    def body(z_ref, lb_ref, gn_ref, st_ref, do_ref, dz_in, dz_ref, dlb_ref, dgn_ref, dstate):
        @pl.when(pl.program_id(1) == 0)
        def _():
            dstate[...] = jnp.zeros_like(dstate)

        @pl.when((pl.program_id(0) == 0) & (pl.program_id(1) == 0))
        def _():
            dlb_ref[...] = jnp.zeros_like(dlb_ref)
            dgn_ref[...] = jnp.zeros_like(dgn_ref)

        R = range(HGRN_HEADS)
        causal = _tri(CHUNK)
        tril_b = causal.astype(BF16)
        triu_b = _tri(CHUNK, upper=True).astype(BF16)
        rowid = lax.broadcasted_iota(jnp.int32, (CHUNK, HGRN_DK), 0)
        lbh = [lb_ref[:, _hs(h)] for h in R]
        gn = gn_ref[...]
        for c in reversed(range(cpb)):
            rows = slice(c * CHUNK, (c + 1) * CHUNK)
            q = [z_ref[rows, _hgrn_col(0, h)].astype(F32) for h in R]
            gates = [_hgrn_gates(q[h], z_ref[rows, _hgrn_col(1, h)].astype(F32), lbh[h]) for h in R]
            b = [_cumsum_rows(tril_b, jnp.log2(gates[h][2])) for h in R]
            v = [z_ref[rows, _hgrn_col(2, h)] for h in R]
            pre = [_hgrn_decays(b[h], q[h], gates[h][0], gates[h][2]) for h in R]
            q_in_b, k_in_b, q_out_b, k_st_b = ([pre[h][i].astype(BF16) for h in R] for i in (1, 2, 3, 4))
            a_b = [jnp.where(causal, _nt(q_in_b[h], k_in_b[h]), 0.0).astype(BF16) for h in R]
            st = [st_ref[c, h] for h in R]
            st_b = [t.astype(BF16) for t in st]
            o = [_nn(a_b[h], v[h]) + _nt(q_out_b[h], st_b[h]) for h in R]
            do_l, dgn_acc = [], jnp.zeros_like(gn)
            for h in R:
                hg = z_ref[rows, _hgrn_col(3, h)].astype(F32)
                dout = do_ref[rows, _hs(h)].astype(F32)
                shg = _sigmoid(hg)
                on_h, norm_vjp = jax.vjp(_rms, o[h], gn)
                d_o, d_gn = norm_vjp(dout * (hg * shg))
                do_l.append(d_o)
                dgn_acc = dgn_acc + d_gn
                dz_ref[rows, _hgrn_col(3, h)] = (dout * on_h * shg * (1.0 + hg * (1.0 - shg))).astype(BF16)
            dgn_ref[...] += dgn_acc
            do_b = [t.astype(BF16) for t in do_l]
            dst = [dstate[h] for h in R]
            dst_b = [t.astype(BF16) for t in dst]
            da_b = [jnp.where(causal, _nt(do_b[h], v[h]), 0.0).astype(BF16) for h in R]
            dv = [_tn(a_b[h], do_b[h]) + _nt(k_st_b[h], dst_b[h]) for h in R]
            dq_in = [_nn(da_b[h], k_in_b[h]) for h in R]
            dk_in = [_tn(da_b[h], q_in_b[h]) for h in R]
            dq_out = [_nn(do_b[h], st_b[h]) for h in R]
            dk_st = [_nn(v[h], dst_b[h]) for h in R]
            for h in R:
                dz_ref[rows, _hgrn_col(2, h)] = dv[h].astype(BF16)
            db = []
            for h in R:
                _, q_in, k_in, q_out, k_st, dec = pre[h]
                ddec = jnp.sum(st[h] * dst[h], axis=0, keepdims=True)
                t_qin, t_kin, t_kst = dq_in[h] * q_in, dk_in[h] * k_in, dk_st[h] * k_st
                dbref = jnp.sum(t_kin - t_qin, axis=0, keepdims=True)
                dblast = jnp.sum(t_kst, axis=0, keepdims=True) + ddec * dec
                db.append(t_qin - t_kin + dq_out[h] * q_out - t_kst
                          + jnp.where(rowid == CHUNK // 2, dbref, 0.0) + jnp.where(rowid == CHUNK - 1, dblast, 0.0))
            for h in R:
                dstate[h] = dst[h] * pre[h][5] + _tn(do_b[h], q_out_b[h])
            dlogf = [_cumsum_rows(triu_b, db[h]) for h in R]
            for h in R:
                sq, sg, f = gates[h]
                e1, e2, e3, e4 = pre[h][0]
                dqf = dq_in[h] * e1 + dq_out[h] * e3
                dk = dk_in[h] * e2 + dk_st[h] * e4
                df_open = (dlogf[h] / f - dk) * (1.0 - sg)
                dlb_ref[:, _hs(h)] += jnp.sum(df_open, axis=0, keepdims=True)
                dz_ref[rows, _hgrn_col(1, h)] = (df_open * ((1.0 - lbh[h]) * sg)).astype(BF16)
                dz_ref[rows, _hgrn_col(0, h)] = (dqf * sq * (1.0 + q[h] * (1.0 - sq))).astype(BF16)

    return pl.pallas_call(
        body, name=name, grid=(B, nblk),
        in_specs=[pl.BlockSpec((None, ts, 4 * HF), rev),
                  pl.BlockSpec((1, HF), lambda b, s: (0, 0)),
                  pl.BlockSpec((1, HGRN_DK), lambda b, s: (0, 0)),
                  pl.BlockSpec((None, cpb, HGRN_HEADS, HGRN_DK, HGRN_DK), lambda b, s: (b, nblk - 1 - s, 0, 0, 0)),
                  pl.BlockSpec((None, ts, HF), rev),
                  ANY],
        out_specs=[pl.BlockSpec((None, ts, 4 * HF), rev),
                   pl.BlockSpec((1, HF), lambda b, s: (0, 0)),
                   pl.BlockSpec((1, HGRN_DK), lambda b, s: (0, 0))],
        out_shape=[jax.ShapeDtypeStruct(dz.shape, BF16),
                   jax.ShapeDtypeStruct((1, HF), F32),
                   jax.ShapeDtypeStruct((1, HGRN_DK), F32)],
        input_output_aliases={5: 0},
        scratch_shapes=[pltpu.VMEM((HGRN_HEADS, HGRN_DK, HGRN_DK), F32)],
        compiler_params=_params("arbitrary", "arbitrary"),
    )(zh, lb, gn, states, doa, dz)


KV_W = ATT_KV_HEADS * ATT_HD
ATT_SCALE = ATT_HD ** -0.5


def _rope(x, cos, sin, inverse=False):
    half = ROPE_DIM // 2
    outs = []
    for p in range(x.shape[1] // 128):
        xp = x[:, p * 128:(p + 1) * 128]
        lane = lax.broadcasted_iota(jnp.int32, xp.shape, 1) % ATT_HD
        sw = jnp.where(lane < half, pltpu.roll(xp, 128 - half, 1), pltpu.roll(xp, half, 1))
        outs.append(xp * cos - sw * sin if inverse else xp * cos + sw * sin)
    return outs[0] if len(outs) == 1 else jnp.concatenate(outs, axis=1)


PAIRS_PER_KV = ATT_GROUP // 2


def _swap_halves(x):
    return pltpu.roll(x, ATT_HD, 1)


def _kv_padded(t, low):
    sw = _swap_halves(t)
    zero = jnp.zeros_like(t)
    out = []
    for g in range(ATT_KV_HEADS):
        in_low, in_high = (t, sw) if g == 0 else (sw, t)
        out.append((jnp.where(low, in_low, zero).astype(BF16), jnp.where(low, zero, in_high).astype(BF16)))
    return out


def _swa_mask(first_block):
    qi = lax.broadcasted_iota(jnp.int32, (WINDOW, 2 * WINDOW), 0)
    mi = lax.broadcasted_iota(jnp.int32, (WINDOW, 2 * WINDOW), 1)
    band = (mi > qi) & (mi <= qi + WINDOW)
    return band & (jnp.logical_not(first_block) | (mi >= WINDOW))


def _swa_specs(nb):
    cur = lambda b, i: (b, i, 0)
    prev = lambda b, i: (b, jnp.maximum(i - 1, 0), 0)
    return cur, prev


def _swa_z_specs():
    q = pl.BlockSpec((None, WINDOW, W_AQ), lambda b, i: (b, i, O_AQ // W_AQ))
    kv_prev = pl.BlockSpec((None, WINDOW, W_AKV), lambda b, i: (b, jnp.maximum(i - 1, 0), O_AKV // W_AKV))
    kv_cur = pl.BlockSpec((None, WINDOW, W_AKV), lambda b, i: (b, i, O_AKV // W_AKV))
    return q, kv_prev, kv_cur


def _swa_fwd(z, cos, sin, sinks, *, name):
    B, S, _ = z.shape
    nb = S // WINDOW
    cur, prev = _swa_specs(nb)

    def body(q_ref, kvp_ref, kvc_ref, cp_ref, sp_ref, cc_ref, sc_ref, sink_ref, o_ref, lse_ref, qr_ref, kr_ref):
        cos_c, sin_c = cc_ref[...], sc_ref[...]
        q = (_rope(q_ref[...].astype(F32), cos_c, sin_c) * ATT_SCALE).astype(BF16)
        k = jnp.concatenate([_rope(kvp_ref[:, :KV_W].astype(F32), cp_ref[...], sp_ref[...]),
                             _rope(kvc_ref[:, :KV_W].astype(F32), cos_c, sin_c)], axis=0)
        qr_ref[...] = q
        kr_ref[...] = k[WINDOW:].astype(BF16)
        v = jnp.concatenate([kvp_ref[:, KV_W:], kvc_ref[:, KV_W:]], axis=0).astype(F32)
        low = lax.broadcasted_iota(jnp.int32, k.shape, 1) < ATT_HD
        kpad = _kv_padded(k, low)
        vpad = _kv_padded(v, low)
        mask = _swa_mask(pl.program_id(1) == 0)
        lses = []
        for g in range(ATT_KV_HEADS):
            pairs = range(g * PAIRS_PER_KV, (g + 1) * PAIRS_PER_KV)
            keys = [(p, e) for p in pairs for e in (0, 1)]
            qp = {p: q[:, p * 128:(p + 1) * 128] for p in pairs}
            s = {pe: jnp.where(mask, _nt(qp[pe[0]], kpad[g][pe[1]]), NEG_INF) for pe in keys}
            pr = {}
            for pe in keys:
                sink = sink_ref[0, 2 * pe[0] + pe[1]]
                m = jnp.maximum(jnp.max(s[pe], axis=1, keepdims=True), sink)
                ex = jnp.exp(s[pe] - m)
                den = jnp.sum(ex, axis=1, keepdims=True) + jnp.exp(sink - m)
                pr[pe] = (ex * (1.0 / den)).astype(BF16)
                lses.append(m + jnp.log(den))
            for p in pairs:
                o_ref[:, p * 128:(p + 1) * 128] = (_nn(pr[p, 0], vpad[g][0]) + _nn(pr[p, 1], vpad[g][1])).astype(BF16)
        lse_ref[...] = jnp.concatenate(lses, axis=1)

    tab = lambda im: pl.BlockSpec((None, WINDOW, 128), im)
    return pl.pallas_call(
        body, name=name, grid=(B, nb),
        in_specs=[*_swa_z_specs(),
                  tab(prev), tab(prev), tab(cur), tab(cur),
                  pl.BlockSpec(memory_space=pltpu.SMEM)],
        out_specs=[pl.BlockSpec((None, WINDOW, D_MODEL), cur), pl.BlockSpec((None, WINDOW, ATT_HEADS), cur),
                   pl.BlockSpec((None, WINDOW, D_MODEL), cur), pl.BlockSpec((None, WINDOW, KV_W), cur)],
        out_shape=[jax.ShapeDtypeStruct((B, S, D_MODEL), BF16), jax.ShapeDtypeStruct((B, S, ATT_HEADS), F32),
                   jax.ShapeDtypeStruct((B, S, D_MODEL), BF16), jax.ShapeDtypeStruct((B, S, KV_W), BF16)],
        compiler_params=_params("parallel", "parallel"),
    )(z, z, z, cos, sin, cos, sin, sinks)


def _swa_bwd(z, qr, kr, cos, sin, sinks, lse, dob, dz, *, name):
    B, S, _ = z.shape
    nb = S // WINDOW
    cur, prev = _swa_specs(nb)

    def body(q_ref, krp_ref, krc_ref, kvp_ref, kvc_ref, cp_ref, sp_ref, cc_ref, sc_ref, sink_ref, lse_ref, do_ref, dz_in,
             dq_ref, dkc_ref, dkp_ref, dsink_ref):
        @pl.when((pl.program_id(0) == 0) & (pl.program_id(1) == 0))
        def _():
            dsink_ref[...] = jnp.zeros_like(dsink_ref)

        cos_c, sin_c, cos_p, sin_p = cc_ref[...], sc_ref[...], cp_ref[...], sp_ref[...]
        q = q_ref[...]
        k = jnp.concatenate([krp_ref[...], krc_ref[...]], axis=0).astype(F32)
        v = jnp.concatenate([kvp_ref[:, KV_W:], kvc_ref[:, KV_W:]], axis=0).astype(F32)
        low = lax.broadcasted_iota(jnp.int32, k.shape, 1) < ATT_HD
        kpad = _kv_padded(k, low)
        vpad = _kv_padded(v, low)
        mask = _swa_mask(pl.program_id(1) == 0)
        lse = lse_ref[...]
        dq_parts, dk_sum, dv_sum, dsinks = [], [], [], []
        for g in range(ATT_KV_HEADS):
            pairs = range(g * PAIRS_PER_KV, (g + 1) * PAIRS_PER_KV)
            keys = [(p, e) for p in pairs for e in (0, 1)]
            qp = {p: q[:, p * 128:(p + 1) * 128] for p in pairs}
            dop = {p: do_ref[:, p * 128:(p + 1) * 128] for p in pairs}
            s = {pe: jnp.where(mask, _nt(qp[pe[0]], kpad[g][pe[1]]), NEG_INF) for pe in keys}
            dp = {pe: _nt(dop[pe[0]], vpad[g][pe[1]]) for pe in keys}
            pr, ds = {}, {}
            for pe in keys:
                h = 2 * pe[0] + pe[1]
                lse_h = lse[:, h:h + 1]
                pf = jnp.exp(s[pe] - lse_h)
                delta = jnp.sum(pf * dp[pe], axis=1, keepdims=True)
                ds[pe] = (pf * (dp[pe] - delta)).astype(BF16)
                pr[pe] = pf.astype(BF16)
                p_sink = jnp.exp(sink_ref[0, h] - lse_h)
                dsinks.append(-jnp.sum(p_sink * delta, axis=0, keepdims=True))
            for p in pairs:
                dq_parts.append((_nn(ds[p, 0], kpad[g][0]) + _nn(ds[p, 1], kpad[g][1])) * ATT_SCALE)
            x = [sum(_tn(ds[p, e], qp[p]) for p in pairs) for e in (0, 1)]
            y = [sum(_tn(pr[p, e], dop[p]) for p in pairs) for e in (0, 1)]
            zk = jnp.where(low, x[0], x[1])
            zv = jnp.where(low, y[0], y[1])
            dk_sum.append(zk + _swap_halves(zk))
            dv_sum.append(zv + _swap_halves(zv))
        dq_ref[...] = _rope(jnp.concatenate(dq_parts, axis=1), cos_c, sin_c, inverse=True).astype(BF16)
        dk = jnp.where(low, dk_sum[0], dk_sum[1])
        dv = jnp.where(low, dv_sum[0], dv_sum[1])
        dkp_ref[:, :KV_W] = _rope(dk[:WINDOW], cos_p, sin_p, inverse=True)
        dkp_ref[:, KV_W:] = dv[:WINDOW]
        dkc_ref[:, :KV_W] = _rope(dk[WINDOW:], cos_c, sin_c, inverse=True)
        dkc_ref[:, KV_W:] = dv[WINDOW:]
        dsink_ref[...] += jnp.concatenate(dsinks, axis=1)

    tab = lambda im: pl.BlockSpec((None, WINDOW, 128), im)
    return pl.pallas_call(
        body, name=name, grid=(B, nb),
        in_specs=[pl.BlockSpec((None, WINDOW, D_MODEL), cur), tab(prev), tab(cur),
                  *_swa_z_specs()[1:],
                  tab(prev), tab(prev), tab(cur), tab(cur),
                  pl.BlockSpec(memory_space=pltpu.SMEM),
                  pl.BlockSpec((None, WINDOW, ATT_HEADS), cur),
                  pl.BlockSpec((None, WINDOW, D_MODEL), cur),
                  ANY],
        out_specs=[_swa_z_specs()[0],
                   pl.BlockSpec((None, WINDOW, 2 * KV_W), cur), pl.BlockSpec((None, WINDOW, 2 * KV_W), cur),
                   pl.BlockSpec((1, ATT_HEADS), lambda b, i: (0, 0))],
        out_shape=[jax.ShapeDtypeStruct(dz.shape, BF16),
                   jax.ShapeDtypeStruct((B, S, 2 * KV_W), F32), jax.ShapeDtypeStruct((B, S, 2 * KV_W), F32),
                   jax.ShapeDtypeStruct((1, ATT_HEADS), F32)],
        input_output_aliases={12: 0},
        compiler_params=_params("arbitrary", "arbitrary"),
    )(qr, kr, kr, z, z, cos, sin, cos, sin, sinks, lse, dob, dz)


def _swa_dkv_combine(dkv_cur, dkv_prev, dz, *, name):
    B, S, W = dkv_cur.shape

    def body(c_ref, p_ref, dz_in, o_ref):
        rows = lax.broadcasted_iota(jnp.int32, (S, W), 0)
        o_ref[...] = (c_ref[...] + _shift_up(p_ref[...], WINDOW, rows, S)).astype(BF16)

    seq = pl.BlockSpec((None, S, W), lambda b: (b, 0, 0))
    return pl.pallas_call(
        body, name=name, grid=(B,),
        in_specs=[seq, seq, ANY], out_specs=pl.BlockSpec((None, S, W), lambda b: (b, 0, O_AKV // W_AKV)),
        out_shape=jax.ShapeDtypeStruct(dz.shape, BF16),
        input_output_aliases={2: 0},
        compiler_params=_params("parallel"),
    )(dkv_cur, dkv_prev, dz)


def _rope_tables(positions):
    half = ROPE_DIM // 2
    inv = ROPE_THETA ** (-2.0 * jnp.arange(half, dtype=F32) / ROPE_DIM)
    ang = positions.astype(F32)[..., None] * inv
    c, s = jnp.cos(ang), jnp.sin(ang)
    pad = jnp.zeros(ang.shape[:-1] + (ATT_HD - ROPE_DIM,), F32)
    cos = jnp.concatenate([c, c, pad + 1.0], axis=-1)
    sin = jnp.concatenate([-s, s, pad], axis=-1)
    return jnp.tile(cos, (1, 1, 2)), jnp.tile(sin, (1, 1, 2))


def _lower_bound(lb_logits, *, name):
    def body(l_ref, o_ref):
        l = l_ref[...]
        e = jnp.exp(l - jnp.max(l, axis=0, keepdims=True))
        o_ref[...] = e[0:1] / jnp.sum(e, axis=0, keepdims=True)

    return pl.pallas_call(body, name=name, out_shape=jax.ShapeDtypeStruct((1, lb_logits.shape[1]), F32))(lb_logits)


W_ZH, W_GATES, W_AQ, W_AKV = 4 * HF, 2 * D_MODEL, ATT_HEADS * ATT_HD, 2 * KV_W
O_ZH, O_GATES, O_AQ, O_AKV = 0, W_ZH, W_ZH + W_GATES, W_ZH + W_GATES + W_AQ
W_IN = W_ZH + W_GATES + W_AQ + W_AKV


W_IN_BLK = W_IN // N_DEV


def _reference_row_block(j, rows=256):
    nz, ng = W_ZH // rows, W_GATES // rows
    return jnp.where(j < nz, j, jnp.where(j < nz + ng, j + (W_AQ + W_AKV) // rows, j - ng))


W_IN_SEGMENTS = ((O_ZH, 0, W_ZH), (O_GATES, W_ZH + W_AQ + W_AKV, W_GATES), (O_AQ, W_ZH, W_AQ + W_AKV))


def _local_step(x, positions, target, small, w_in_t, rest_weights, emit, start_token):
    B, S, D = x.shape
    T = B * S
    x2 = x.reshape(T, D)
    cos, sin = _rope_tables(positions)
    lb = _lower_bound(small["lb_logits"], name="lb_fwd")
    zero = lambda tok: tok[0:1, 0:1]

    u1, z = _norm_matmul(x2, small["norm1_g"] + zero(start_token), w_in_t, W_IN_SEGMENTS, tm=512, name="norm1_mm_z")
    z3 = z.reshape(B, S, W_IN)
    oa, states = _hgrn_fwd(z3, lb, small["hgrn_norm_g"], name="hgrn_fwd")
    ob, lse, qr, kr = _swa_fwd(z3, cos, sin, small["attn_sinks"], name="swa_fwd")
    oa2 = oa.reshape(T, D)
    ob2 = ob.reshape(T, D)
    W = rest_weights("mix", ob)
    row = lambda tm, dtype=None: _row_spec(tm, D)
    tile = lambda dtype: jax.ShapeDtypeStruct((T, D), dtype)
    vec = _full_spec((1, D))
    vec_shape = jax.ShapeDtypeStruct((1, D), F32)

    def merge_ep(acc_a, acc_b, g_ref):
        pa, pb = acc_a.astype(BF16), acc_b.astype(BF16)
        return pa, pb, _merge_fn(g_ref[...], pa, pb)

    pa, pb, merged = _matmul_ep([(oa2, W["w_a"], False, 0), (ob2, W["w_b"], False, 0)], tm=1024, ins=[z], in_specs=[_gates_spec(1024)],
                                out_shapes=[tile(BF16)] * 3, out_specs=[row(1024)] * 3, epilogue=merge_ep, name="mm_pa_pb_merge")

    def resid_norm_ep(acc, x_ref, g_ref):
        hh = acc + x_ref[...]
        return hh, _rms(hh, g_ref[...])

    h, u2 = _matmul_ep([(merged, W["w_out"], False, 0)], tm=1024, ins=[x2, small["norm2_g"]], in_specs=[row(1024), vec],
                       out_shapes=[tile(F32), tile(BF16)], out_specs=[row(1024), row(1024)], epilogue=resid_norm_ep, name="mm_h_norm2")
    W.update(rest_weights("ffn", u2))
    gu3 = _matmul_col_tiles(u2, W["w_ffn_t"], tm=1024, tn=D_FF, tc=CONV_TC, name="mm_gu").reshape(2 * D_FF // CONV_TC, B, S, CONV_TC)
    act, a_pre = _conv_act_fwd(gu3, W["conv_w"], small["conv_b"], name="conv_act_fwd")
    act2 = act.reshape(T, D_FF)
    g = {}

    def loss_ep(acc, h_ref, g_ref, t_ref):
        y, vjp = jax.vjp(_rms, acc + h_ref[...], g_ref[...])
        err = y - t_ref[...]
        dx, dg = vjp(err * (1.0 / D))
        return dx, dx, dg, (0.5 / D) * jnp.sum(jnp.sum(err * err, axis=1, keepdims=True), axis=0, keepdims=True)

    dh2, dh2b, g["final_g"], loss = _matmul_ep(
        [(act2, W["w_down"], False, 0)], tm=1024, ins=[h, small["final_g"].reshape(1, D), target.reshape(T, D)], in_specs=[row(1024), vec, row(1024)],
        out_shapes=[tile(F32), tile(BF16), vec_shape, jax.ShapeDtypeStruct((1, 1), F32)],
        out_specs=[row(1024), row(1024), vec, _full_spec((1, 1))], sums=(2, 3), epilogue=loss_ep, name="mm_h2_loss")
    dact = _matmul(dh2b, W["w_down"], tb=True, out_dtype=BF16, name="mm_dact", tm=1024, tn=D_FF)
    dw_down_t = _matmul(dh2b, act2, ta=True, out_dtype=BF16, name="mm_dw_down", tm=1024, tn=256, tk=8192)
    dg_, dup, g["conv_w"], g["conv_b"] = _conv_act_bwd(gu3, a_pre, W["conv_w"], dact.reshape(B, S, D_FF), name="conv_act_bwd")
    dg2 = dg_.reshape(T, D_FF)
    dup2 = dup.reshape(T, D_FF)
    dw_ffn_t = _matmul(u2, dg2, ta=True, out_t=True, out_dtype=BF16, into=lax.empty((2 * D_FF, D), BF16), o_noff=0, name="mm_dw_ffn_g", tm=1024, tn=256, tk=8192)
    dw_ffn_t = _matmul(u2, dup2, ta=True, out_t=True, out_dtype=BF16, into=dw_ffn_t, o_noff=D_FF // 256, name="mm_dw_ffn_u", tm=1024, tn=256, tk=8192)
    tok = emit("ffn", dict(w_ffn_t=dw_ffn_t, w_down=dw_down_t.T))
    def norm2_bwd_ep(acc_g, acc_u, h_ref, g_ref, dh2_ref):
        _, vjp = jax.vjp(_rms, h_ref[...], g_ref[...])
        dx, dg = vjp(acc_g + acc_u)
        dx = dx + dh2_ref[...]
        return dx, dx, dg

    dh, dhb, g["norm2_g"] = _matmul_ep(
        [(dg2, W["w_ffn_t"], False, 0), (dup2, W["w_ffn_t"], False, 1)], tm=512, ins=[h, small["norm2_g"] + zero(tok), dh2], in_specs=[row(512), vec, row(512)],
        out_shapes=[tile(F32), tile(BF16), vec_shape], out_specs=[row(512), row(512), vec], sums=(2,), epilogue=norm2_bwd_ep, name="mm_du2_norm2_bwd")
    dw_out = _matmul(merged, dhb, ta=True, out_dtype=BF16, name="mm_dw_out", tm=1024, tn=1024, tk=2048)

    def merge_bwd_ep(acc, g_ref, pa_ref, pb_ref, dz_in):
        gt = g_ref[...].astype(F32)
        sa = _sigmoid(gt[:, :D_MODEL])
        sb = _sigmoid(gt[:, D_MODEL:])
        dgates = jnp.concatenate([acc * pa_ref[...].astype(F32) * sa * (1.0 - sa), acc * pb_ref[...].astype(F32) * sb * (1.0 - sb)], axis=1)
        return dgates, acc * sa, acc * sb

    dz, dpa, dpb = _matmul_ep(
        [(dhb, W["w_out"], True, 0)], tm=1024, ins=[z, pa, pb, lax.empty((T, W_IN), BF16)], in_specs=[_gates_spec(1024), row(1024), row(1024), ANY],
        out_shapes=[jax.ShapeDtypeStruct((T, W_IN), BF16), tile(BF16), tile(BF16)], out_specs=[_gates_spec(1024), row(1024), row(1024)],
        aliases={3: 0}, epilogue=merge_bwd_ep, name="mm_dmerged_merge_bwd")
    doa, dob = _matmul_ep([(dpa, W["w_a"], True, 0), (dpb, W["w_b"], True, 0)], tm=1024, ins=[], in_specs=[],
                          out_shapes=[tile(BF16)] * 2, out_specs=[row(1024)] * 2, epilogue=lambda da, db: (da, db), name="mm_doa_dob")
    dw_a = _matmul(oa2, dpa, ta=True, out_dtype=BF16, name="mm_dw_a", tm=1024, tn=1024, tk=2048)
    dw_b = _matmul(ob2, dpb, ta=True, out_dtype=BF16, name="mm_dw_b", tm=1024, tn=1024, tk=2048)
    tok = emit("mix", dict(w_out=dw_out, w_a=dw_a, w_b=dw_b))
    dz3, dkv_cur, dkv_prev, dsinks = _swa_bwd(z3, qr, kr, cos, sin, small["attn_sinks"] + zero(tok), lse, dob.reshape(B, S, D),
                                              dz.reshape(B, S, W_IN), name="swa_bwd")
    dz3 = _swa_dkv_combine(dkv_cur, dkv_prev, dz3, name="swa_dkv")
    g["attn_sinks"] = dsinks
    dz3, g["lb"], g["hgrn_norm_g"] = _hgrn_bwd(z3, lb, small["hgrn_norm_g"], states, doa.reshape(B, S, D), dz3, name="hgrn_bwd")
    dz = dz3.reshape(T, W_IN)
    dw_in_t = _matmul(u1, dz, ta=True, out_t=True, o_block_perm=_reference_row_block, out_dtype=BF16, name="mm_dw_in", tm=1024, tn=256, tk=8192)
    tok = emit("in", dict(w_in_t=dw_in_t))
    dx, g["norm1_g"] = _matmul_norm_bwd(dz, w_in_t, W_IN_SEGMENTS, x2, small["norm1_g"], dh, tok, tm=512, name="mm_du1_norm1_bwd")
    g["lb_logits"] = _lb_bwd(g.pop("lb"), lb, name="lb_bwd")
    return loss, dx.reshape(B, S, D), g


def _my_place():
    return lax.axis_index("x"), lax.axis_index("y"), lax.axis_index("c")


def _gather_blocks(x_ref, out_ref, send_sems, recv_sems, local_sem):
    x, y, c = _my_place()
    me, sibling = (x, y, c), (x, y, 1 - c)
    chips = [(1 - x, y), (x, 1 - y), (1 - x, 1 - y)]
    relayed = tuple(jnp.where(c == 0, a, b) for a, b in zip(chips[0], chips[1]))
    relay_to = tuple(jnp.where(c == 0, b, a) for a, b in zip(chips[0], chips[1]))

    def slot(px, py, pc):
        return out_ref.at[4 * px + 2 * py + pc]

    def copy(k, block, to, src=None):
        return pltpu.make_async_remote_copy(
            src_ref=slot(*block) if src is None else src, dst_ref=slot(*block),
            send_sem=send_sems.at[k], recv_sem=recv_sems.at[k], device_id=to, device_id_type=MESH)

    mine = pltpu.make_async_copy(x_ref, slot(*me), local_sem)
    mine.start()
    first = [copy(0, me, sibling, src=x_ref)]
    first += [copy(1 + j, me, (*chip, c), src=x_ref) for j, chip in enumerate(chips[:2])]
    for cp in first:
        cp.start()
    relay = copy(3, (*relayed, c), (*relay_to, c))
    passed = [copy(4 + j, (*chip, c), sibling) for j, chip in enumerate(chips)]
    for j, chip in enumerate(chips):
        copy(1 + j, (*chip, c), me).wait_recv()
        if j < 2:
            @pl.when(c == j)
            def _():
                relay.start()

        passed[j].start()
    copy(0, sibling, me).wait_recv()
    for j, chip in enumerate(chips):
        copy(4 + j, (*chip, 1 - c), me).wait_recv()
    for cp in first + [relay] + passed:
        cp.wait_send()
    mine.wait()


GATHER_SEMS = [pltpu.SemaphoreType.DMA((7,)), pltpu.SemaphoreType.DMA((7,)), pltpu.SemaphoreType.DMA]


def _all_gather(blk, *, name):
    return pl.pallas_call(
        _gather_body_fn(), name=name,
        out_shape=jax.ShapeDtypeStruct((N_DEV,) + blk.shape, blk.dtype),
        in_specs=[ANY], out_specs=ANY,
        scratch_shapes=GATHER_SEMS,
    )(blk)


def _gather_body_fn():
    def body(x_ref, out_ref, send_sems, recv_sems, local_sem):
        _gather_blocks(x_ref, out_ref, send_sems, recv_sems, local_sem)
    return body


SLAB_W = 1152
SMALL_SHAPES = dict(norm1_g=(1, D_MODEL), lb_logits=(2, HGRN_HEADS * HGRN_DK), hgrn_norm_g=(1, HGRN_DK), attn_sinks=(1, ATT_HEADS),
                    norm2_g=(1, D_MODEL), conv_b=(1, D_FF), final_g=(1, D_MODEL))
CONVW_BLK = D_FF // N_DEV
CONVW_STRIDE = SLAB_W // 3


def _slab_layout():
    layout, r = {}, 0
    for nm, (nr, w) in SMALL_SHAPES.items():
        layout[nm] = []
        for i in range(nr):
            for c0 in range(0, w, SLAB_W):
                layout[nm].append((r, i, c0, min(SLAB_W, w - c0)))
                r += 1
    return layout, r


SMALL_ROWS, _N_SMALL_ROWS = _slab_layout()
CONV_ROW0 = -(-_N_SMALL_ROWS // 8) * 8
LOSS_ROW = CONV_ROW0 + N_DEV
SLAB_ROWS = LOSS_ROW + 8


def _small_step(grads, g_conv_w, loss, params, moments, variances, dev, *, name):
    names = list(SMALL_ROWS)
    n = len(names)

    def body(dev_ref, *refs):
        g_refs = dict(zip(names, refs[:n]))
        gc_ref, loss_ref = refs[n], refs[n + 1]
        base = n + 2
        w_refs, m_refs, v_refs = (dict(zip(names + ["conv_w"], refs[base + i * (n + 1):base + (i + 1) * (n + 1)])) for i in range(3))
        o = base + 3 * (n + 1)
        gath_ref, loss_out = refs[o], refs[o + 1]
        outs = {nm: refs[o + 2 + 4 * i:o + 6 + 4 * i] for i, nm in enumerate(names + ["conv_w"])}
        slab, total, send_sems, recv_sems, local_sem = refs[-5:]

        slab[...] = jnp.zeros_like(slab)
        for nm, pieces in SMALL_ROWS.items():
            for r, i, c0, w in pieces:
                slab[r:r + 1, 0:w] = g_refs[nm][i:i + 1, c0:c0 + w]
        for p in range(N_DEV):
            for j in range(3):
                slab[CONV_ROW0 + p:CONV_ROW0 + p + 1, j * CONVW_STRIDE:j * CONVW_STRIDE + CONVW_BLK] = gc_ref[j:j + 1, p * CONVW_BLK:(p + 1) * CONVW_BLK]
        slab[LOSS_ROW:LOSS_ROW + 1, 0:1] = loss_ref[...]
        _gather_blocks(slab, gath_ref, send_sems, recv_sems, local_sem)
        acc = gath_ref[0]
        for p in range(1, N_DEV):
            acc = acc + gath_ref[p]
        total[...] = acc
        loss_out[...] = total[LOSS_ROW:LOSS_ROW + 1, 0:1]

        def update(nm, g, i, c0, w):
            at = (slice(i, i + 1), slice(c0, c0 + w))
            d, mn, vn = _adamw_math(w_refs[nm][at], g, m_refs[nm][at], v_refs[nm][at])
            for ref, val in zip(outs[nm], (g, d, mn, vn)):
                ref[at] = val

        for nm, pieces in SMALL_ROWS.items():
            for r, i, c0, w in pieces:
                update(nm, total[r:r + 1, 0:w], i, c0, w)
        conv_rows = total[CONV_ROW0:CONV_ROW0 + N_DEV, :]
        rowid = lax.broadcasted_iota(jnp.int32, conv_rows.shape, 0)
        mine = jnp.sum(jnp.where(rowid == dev_ref[0], conv_rows, 0.0), axis=0, keepdims=True)
        for j in range(3):
            update("conv_w", mine[:, j * CONVW_STRIDE:j * CONVW_STRIDE + CONVW_BLK], j, 0, CONVW_BLK)

    order = names + ["conv_w"]
    ins = [grads[nm] for nm in names] + [g_conv_w, loss]
    for d in (params, moments, variances):
        ins += [d[nm] for nm in order]
    vmem = pl.BlockSpec(memory_space=pltpu.VMEM)
    out_shape = [jax.ShapeDtypeStruct((N_DEV, SLAB_ROWS, SLAB_W), F32), jax.ShapeDtypeStruct((1, 1), F32)]
    for nm in order:
        out_shape += [jax.ShapeDtypeStruct(params[nm].shape, F32)] * 4
    res = pl.pallas_call(
        body, name=name,
        grid_spec=pltpu.PrefetchScalarGridSpec(
            num_scalar_prefetch=1, grid=(1,),
            in_specs=[vmem] * len(ins), out_specs=[vmem] * len(out_shape),
            scratch_shapes=[pltpu.VMEM((SLAB_ROWS, SLAB_W), F32), pltpu.VMEM((SLAB_ROWS, SLAB_W), F32)] + GATHER_SEMS),
        out_shape=out_shape,
    )(dev, *ins)
    return res[1], {nm: tuple(res[2 + 4 * i:6 + 4 * i]) for i, nm in enumerate(order)}


HBM_SPEC = pl.BlockSpec(memory_space=pltpu.HBM)
SEM_SPEC = pl.BlockSpec(memory_space=pltpu.SEMAPHORE)
DATAFLOW_EFFECT = pltpu.SideEffectType.DATAFLOW_SIDE_EFFECTING
N_PEERS = N_DEV - 1


def _peers(x, y, c):
    return [(1 - x if r & 4 else x, 1 - y if r & 2 else y, 1 - c if r & 1 else c) for r in range(1, N_DEV)]


def _exchange_start(srcs, scatter, *, after=None, name):
    n = len(srcs)
    lands = [lax.empty(a.shape if scatter else (N_DEV,) + a.shape, a.dtype) for a in srcs]
    extra = [] if after is None else [after]

    def body(*refs):
        src_refs, land_refs = refs[:n], refs[n:2 * n]
        send_sems, recv_sems, token = refs[2 * n + len(extra)], refs[2 * n + len(extra) + 1], refs[-1]
        x, y, c = _my_place()
        me = 4 * x + 2 * y + c
        for i in range(n):
            for r, (tx, ty, tc) in enumerate(_peers(x, y, c)):
                src = src_refs[i].at[4 * tx + 2 * ty + tc] if scatter else src_refs[i]
                pltpu.make_async_remote_copy(
                    src_ref=src, dst_ref=land_refs[i].at[me], send_sem=send_sems.at[N_PEERS * i + r],
                    recv_sem=recv_sems.at[N_PEERS * i + r], device_id=(tx, ty, tc), device_id_type=MESH).start()
        token[...] = jnp.zeros_like(token)

    thru = [pltpu.HBM(a.shape, a.dtype) for a in list(srcs) + lands]
    res = pl.pallas_call(
        body, name=name,
        out_shape=(pltpu.SemaphoreType.DMA((N_PEERS * n,)), pltpu.SemaphoreType.DMA((N_PEERS * n,)), *thru,
                   jax.ShapeDtypeStruct((8, 128), F32)),
        in_specs=[HBM_SPEC] * (2 * n) + [ANY] * len(extra),
        out_specs=(SEM_SPEC, SEM_SPEC, *([HBM_SPEC] * (2 * n)), pl.BlockSpec(memory_space=pltpu.VMEM)),
        input_output_aliases={i: 2 + i for i in range(2 * n)},
        compiler_params=pltpu.CompilerParams(has_side_effects=DATAFLOW_EFFECT),
    )(*[pltpu.with_memory_space_constraint(a, pltpu.HBM) for a in list(srcs) + lands], *extra)
    return (res[0], res[1], list(res[2:2 + n]), list(res[2 + n:2 + 2 * n]), scatter), res[-1]


def _exchange_wait(handle, after, *, name):
    send_sems, recv_sems, srcs, lands, scatter = handle
    n = len(srcs)

    def body(*refs):
        src_refs, land_refs = refs[:n], refs[n:2 * n]
        send_sems, recv_sems = refs[2 * n], refs[2 * n + 1]
        x, y, c = _my_place()
        for i in range(n):
            for r in range(N_PEERS):
                src = src_refs[i].at[0] if scatter else src_refs[i]
                cp = pltpu.make_async_remote_copy(
                    src_ref=src, dst_ref=land_refs[i].at[0], send_sem=send_sems.at[N_PEERS * i + r],
                    recv_sem=recv_sems.at[N_PEERS * i + r], device_id=(x, y, c), device_id_type=MESH)
                cp.wait_send()
                cp.wait_recv()

    thru = [pltpu.HBM(a.shape, a.dtype) for a in srcs + lands]
    res = pl.pallas_call(
        body, name=name, out_shape=tuple(thru),
        in_specs=[HBM_SPEC] * (2 * n) + [SEM_SPEC, SEM_SPEC, ANY], out_specs=tuple([HBM_SPEC] * (2 * n)),
        input_output_aliases={i: i for i in range(2 * n)},
        compiler_params=pltpu.CompilerParams(has_side_effects=DATAFLOW_EFFECT),
    )(*srcs, *lands, send_sems, recv_sems, after)
    return list(res[:n]), list(res[n:])


def _with_own(land, own, me):
    return lax.dynamic_update_index_in_dim(land, own, me, 0)


def _adamw_math(w, g, m, v):
    m = ADAM_B1 * m + (1.0 - ADAM_B1) * g
    v = ADAM_B2 * v + (1.0 - ADAM_B2) * (g * g)
    m_hat = m / (1.0 - ADAM_B1 ** ADAM_STEP)
    v_hat = v / (1.0 - ADAM_B2 ** ADAM_STEP)
    delta = -ADAM_LR * (m_hat / (jnp.sqrt(v_hat) + ADAM_EPS) + ADAM_WD * w)
    return delta, m, v


def _adamw_sum(parts, w, m, v, *, name):
    shape = w.shape
    R, n = shape[-2], shape[-1]
    w, m, v = (t.reshape(R, n) for t in (w, m, v))
    tr = _pick(R, (256, 464, 352, 128))

    def body(p_ref, w_ref, m_ref, v_ref, g_ref, d_ref, mo_ref, vo_ref):
        g = p_ref[0].astype(F32)
        for p in range(1, N_DEV):
            g = g + p_ref[p].astype(F32)
        d, mn, vn = _adamw_math(w_ref[...], g, m_ref[...], v_ref[...])
        g_ref[...] = g
        d_ref[...] = d
        mo_ref[...] = mn
        vo_ref[...] = vn

    row = pl.BlockSpec((tr, n), lambda i: (i, 0))
    outs = pl.pallas_call(
        body, name=name, grid=(R // tr,),
        in_specs=[pl.BlockSpec((N_DEV, tr, n), lambda i: (0, i, 0)), row, row, row],
        out_specs=[row, row, row, row],
        out_shape=[jax.ShapeDtypeStruct((R, n), F32)] * 4,
        compiler_params=_params("parallel"),
    )(parts, w, m, v)
    return [t.reshape(shape) for t in outs]


def _lb_bwd(dlb, lb, *, name):
    def body(d_ref, lb_ref, o_ref):
        t = d_ref[...] * lb_ref[...] * (1.0 - lb_ref[...])
        o_ref[0:1, :] = t
        o_ref[1:2, :] = -t

    return pl.pallas_call(body, name=name, out_shape=jax.ShapeDtypeStruct((2, lb.shape[1]), F32))(dlb, lb)


DOWN_BLK, ROW_BLK = D_FF // N_DEV, D_MODEL // N_DEV
W_FFN_BLK = 2 * D_FF // N_DEV
CONV_BITS_SHAPE = (16, 256)


def kernel(x, positions, norm1_g, w_in, lb_logits, hgrn_norm_g, w_a, attn_sinks, w_b, w_out, norm2_g, w_ffn_in, conv_w, conv_b, w_down, final_g, loss_target, m_norm1_g, m_w_in, m_lb_logits, m_hgrn_norm_g, m_w_a, m_attn_sinks, m_w_b, m_w_out, m_norm2_g, m_w_ffn_in, m_conv_w, m_conv_b, m_w_down, m_final_g, v_norm1_g, v_w_in, v_lb_logits, v_hgrn_norm_g, v_w_a, v_attn_sinks, v_w_b, v_w_out, v_norm2_g, v_w_ffn_in, v_conv_w, v_conv_b, v_w_down, v_final_g):
    xi, yi, ci = _my_place()
    dev = 4 * xi + 2 * yi + ci

    tr = lambda t: jnp.transpose(t[0])
    untr = lambda t: jnp.transpose(t)[None]
    w_in_blocks = _all_gather(tr(w_in).astype(BF16), name="ag_w_in")
    conv_bits = lax.bitcast_convert_type(conv_w, BF16).reshape(-1)
    conv_bits = jnp.pad(conv_bits, (0, CONV_BITS_SHAPE[0] * CONV_BITS_SHAPE[1] - conv_bits.shape[0])).reshape(CONV_BITS_SHAPE)
    w_in_full_t = w_in_blocks.reshape(W_IN, D_MODEL)
    gather_handles = {}
    gather_handles["mix"], tok_mix = _exchange_start([w_a[0].astype(BF16), w_b[0].astype(BF16), w_out[0].astype(BF16)], False,
                                                     after=w_in_full_t, name="ag_mix_start")
    gather_handles["ffn"], tok_ffn = _exchange_start([tr(w_ffn_in).astype(BF16), w_down[0].astype(BF16), conv_bits], False,
                                                     after=tok_mix, name="ag_ffn_start")
    start_token = tok_mix + tok_ffn

    def rest_weights(group, after):
        own, lands = _exchange_wait(gather_handles[group], after, name="ag_" + group + "_wait")
        full = [_with_own(l, o, dev) for l, o in zip(lands, own)]
        if group == "mix":
            return dict(zip(("w_a", "w_b", "w_out"), [t.reshape(D_MODEL, D_MODEL) for t in full]))
        bits = full[2].reshape(N_DEV, -1)[:, :3 * CONVW_BLK * 2].reshape(N_DEV, 3, CONVW_BLK, 2)
        return dict(w_ffn_t=full[0].reshape(2 * D_FF, D_MODEL), w_down=full[1].reshape(D_FF, D_MODEL),
                    conv_w=lax.bitcast_convert_type(bits, F32).transpose(1, 0, 2).reshape(3, D_FF))

    handles = {}

    def emit(group, gr):
        if group == "ffn":
            srcs = [gr["w_ffn_t"].reshape(N_DEV, W_FFN_BLK, D_MODEL), gr["w_down"].reshape(N_DEV, DOWN_BLK, D_MODEL)]
        elif group == "mix":
            srcs = [gr[n].reshape(N_DEV, ROW_BLK, D_MODEL) for n in ("w_out", "w_a", "w_b")]
        else:
            srcs = [gr["w_in_t"].reshape(N_DEV, W_IN_BLK, D_MODEL)]
        handles[group], token = _exchange_start(srcs, True, name="rs_" + group + "_start")
        return token

    small = dict(norm1_g=norm1_g, lb_logits=lb_logits, hgrn_norm_g=hgrn_norm_g, attn_sinks=attn_sinks, norm2_g=norm2_g,
                 conv_b=conv_b, final_g=final_g)
    loss, grad_x, g = _local_step(x, positions, loss_target, small, w_in_full_t, rest_weights, emit, start_token)

    def parts_of(group, after):
        srcs, lands = _exchange_wait(handles[group], after, name="rs_" + group + "_wait")
        return [_with_own(l, lax.dynamic_index_in_dim(s, dev, 0, keepdims=False), dev) for s, l in zip(srcs, lands)]

    p_ffn, p_down = parts_of("ffn", grad_x)
    p_out, p_a, p_b = parts_of("mix", grad_x)
    (p_in,) = parts_of("in", grad_x)
    big = dict(
        w_in=[untr(t) for t in _adamw_sum(p_in, tr(w_in), tr(m_w_in), tr(v_w_in), name="adamw_w_in")],
        w_a=_adamw_sum(p_a, w_a, m_w_a, v_w_a, name="adamw_w_a"),
        w_b=_adamw_sum(p_b, w_b, m_w_b, v_w_b, name="adamw_w_b"),
        w_out=_adamw_sum(p_out, w_out, m_w_out, v_w_out, name="adamw_w_out"),
        w_ffn_in=[untr(t) for t in _adamw_sum(p_ffn, tr(w_ffn_in), tr(m_w_ffn_in), tr(v_w_ffn_in), name="adamw_w_ffn_in")],
        w_down=_adamw_sum(p_down, w_down, m_w_down, v_w_down, name="adamw_w_down"),
    )

    row = lambda t: t.reshape(1, -1) if t.ndim == 1 else t
    shard = lambda t: t.reshape(3, CONVW_BLK)
    sm_g = {nm: g[nm] for nm in SMALL_ROWS}
    sm_w = dict(norm1_g=norm1_g, lb_logits=lb_logits, hgrn_norm_g=hgrn_norm_g, attn_sinks=attn_sinks, norm2_g=norm2_g,
                conv_b=conv_b, final_g=row(final_g), conv_w=shard(conv_w))
    sm_m = dict(norm1_g=m_norm1_g, lb_logits=m_lb_logits, hgrn_norm_g=m_hgrn_norm_g, attn_sinks=m_attn_sinks, norm2_g=m_norm2_g,
                conv_b=m_conv_b, final_g=row(m_final_g), conv_w=shard(m_conv_w))
    sm_v = dict(norm1_g=v_norm1_g, lb_logits=v_lb_logits, hgrn_norm_g=v_hgrn_norm_g, attn_sinks=v_attn_sinks, norm2_g=v_norm2_g,
                conv_b=v_conv_b, final_g=row(v_final_g), conv_w=shard(v_conv_w))
    loss_total, sm_out = _small_step(sm_g, g["conv_w"], loss, sm_w, sm_m, sm_v, dev.astype(jnp.int32).reshape(1), name="small_step")
    shapes = dict(final_g=final_g.shape, conv_w=conv_w.shape)

    names = ("norm1_g", "w_in", "lb_logits", "hgrn_norm_g", "w_a", "attn_sinks", "w_b", "w_out", "norm2_g", "w_ffn_in", "conv_w", "conv_b", "w_down", "final_g")
    outs = [loss_total.reshape(()), grad_x]
    for kind in range(4):
        outs += [big[n][kind] if n in big else sm_out[n][kind].reshape(shapes.get(n, sm_out[n][kind].shape)) for n in names]
    return tuple(outs)
```

```python
import jax
import jax.numpy as jnp
from jax import lax
from jax.experimental import pallas as pl
from jax.experimental.pallas import tpu as pltpu

F32 = jnp.float32
BF16 = jnp.bfloat16

D_MODEL = 1024
HGRN_HEADS = 8
HGRN_DK = 128
CHUNK = 64
ATT_HEADS = 16
ATT_KV_HEADS = 2
ATT_HD = 64
ATT_GROUP = ATT_HEADS // ATT_KV_HEADS
WINDOW = 128
ROPE_DIM = ATT_HD // 4
ROPE_THETA = 500000.0
D_FF = 2816
EPS = 1e-6
NEG_INF = -1e30
N_DEV = 8

ADAM_LR = 0.001
ADAM_B1 = 0.9
ADAM_B2 = 0.999
ADAM_EPS = 1e-08
ADAM_WD = 0.01
ADAM_STEP = 10

MESH = pl.DeviceIdType.MESH
ANY = pl.BlockSpec(memory_space=pl.ANY)


def _pick(n, cands):
    for c in cands:
        if n % c == 0:
            return c
    return n


def _sigmoid(x):
    return 0.5 * jnp.tanh(0.5 * x) + 0.5


def _silu(x):
    hx = 0.5 * x
    return hx * jnp.tanh(hx) + hx


def _rms(x, g):
    return x * lax.rsqrt(jnp.mean(x * x, axis=-1, keepdims=True) + EPS) * g


def _dot(a, b, dims):
    return lax.dot_general(a, b, (dims, ((), ())), preferred_element_type=F32)


def _nn(a, b):
    return _dot(a, b, ((1,), (0,)))


def _nt(a, b):
    return _dot(a, b, ((1,), (1,)))


def _tn(a, b):
    return _dot(a, b, ((0,), (0,)))


def _params(*sem):
    return pltpu.CompilerParams(dimension_semantics=sem, vmem_limit_bytes=56 * 1024 * 1024)


def _matmul(a, b, *, ta=False, tb=False, out_dtype=F32, addend=None, after=None, into=None, o_noff=0, out_t=False,
            o_block_perm=lambda j: j, name, tm, tn, tk=None, n_extent=None, b_koff=0, b_noff=0):
    M, K = (a.shape[1], a.shape[0]) if ta else a.shape
    N = n_extent or (b.shape[0] if tb else b.shape[1])
    tm, tn, tk = min(tm, M), min(tn, N), min(tk or K, K)
    assert M % tm == 0 and N % tn == 0 and K % tk == 0, (name, M, N, K, tm, tn, tk)
    nk = K // tk
    use_scratch = nk > 1 and out_dtype != F32
    grid = (M // tm, N // tn, nk)
    a_spec = pl.BlockSpec((tk, tm), lambda i, j, k: (k, i)) if ta else pl.BlockSpec((tm, tk), lambda i, j, k: (i, k))
    b_spec = pl.BlockSpec((tn, tk), lambda i, j, k: (j + b_noff, k + b_koff)) if tb else pl.BlockSpec((tk, tn), lambda i, j, k: (k + b_koff, j + b_noff))
    o_spec = pl.BlockSpec((tm, tn), lambda i, j, k: (i, j))
    dims = ((0 if ta else 1,), (1 if tb else 0,))
    has_add = addend is not None

    n_in = 2 + has_add + (after is not None) + (into is not None)

    def body(*refs):
        a_ref, b_ref = refs[:2]
        c_ref = refs[2] if has_add else None
        o_ref = refs[n_in]
        part = _dot(a_ref[...], b_ref[...], dims)
        if nk == 1:
            if has_add:
                part = part + c_ref[...].astype(F32)
            o_ref[...] = (part.T if out_t else part).astype(out_dtype)
        else:
            acc_ref = refs[-1] if use_scratch else o_ref
            k = pl.program_id(2)

            @pl.when(k == 0)
            def _():
                acc_ref[...] = part + c_ref[...].astype(F32) if has_add else part

            @pl.when(k > 0)
            def _():
                acc_ref[...] += part

            if use_scratch:
                @pl.when(k == nk - 1)
                def _():
                    o_ref[...] = acc_ref[...].astype(out_dtype)

    in_specs = [a_spec, b_spec] + ([o_spec] if has_add else [])
    args = (a, b) + ((addend,) if has_add else ())
    if after is not None:
        in_specs.append(pl.BlockSpec(after.shape, lambda i, j, k: (0, 0)))
        args += (after,)
    aliases = {}
    if into is not None:
        in_specs.append(ANY)
        args += (into,)
        aliases = {len(args) - 1: 0}
    if out_t:
        assert nk == 1 and not has_add
        o_spec = pl.BlockSpec((tn, tm), lambda i, j, k: (o_block_perm(j) + o_noff, i))
    elif into is not None:
        o_spec = pl.BlockSpec((tm, tn), lambda i, j, k: (i, j + o_noff))
    return pl.pallas_call(
        body,
        name=name,
        grid=grid,
        in_specs=in_specs,
        out_specs=o_spec,
        out_shape=jax.ShapeDtypeStruct(into.shape if into is not None else ((N, M) if out_t else (M, N)), out_dtype),
        input_output_aliases=aliases,
        scratch_shapes=[pltpu.VMEM((tm, tn), F32)] if use_scratch else [],
        compiler_params=_params("parallel", "parallel", "arbitrary"),
    )(*args)


def _matmul_col_tiles(a, b_t, *, tm, tn, tc, name):
    M, K = a.shape
    N = b_t.shape[0]
    tm = min(tm, M)
    per_step = tn // tc

    def body(a_ref, b_ref, o_ref):
        res = _nt(a_ref[...], b_ref[...]).astype(BF16)
        for t in range(per_step):
            o_ref[t] = res[:, t * tc:(t + 1) * tc]

    return pl.pallas_call(
        body, name=name, grid=(M // tm, N // tn),
        in_specs=[pl.BlockSpec((tm, K), lambda i, j: (i, 0)), pl.BlockSpec((tn, K), lambda i, j: (j, 0))],
        out_specs=pl.BlockSpec((per_step, tm, tc), lambda i, j: (j, i, 0)),
        out_shape=jax.ShapeDtypeStruct((N // tc, M, tc), BF16),
        compiler_params=_params("parallel", "parallel"),
    )(a, b_t)


EPILOGUE_ROWS = 256


def _matmul_ep(pairs, *, tm, ins, in_specs, out_shapes, out_specs, sums=(), epilogue, aliases=None, name):
    M = pairs[0][0].shape[0]
    tm = min(tm, M)
    mm_specs, mm_args, dims = [], [], []
    for a, b, tb, koff in pairs:
        K = a.shape[1]
        N = b.shape[0] if tb else b.shape[1]
        mm_specs += [pl.BlockSpec((tm, K), lambda i: (i, 0)),
                     pl.BlockSpec((N, K), lambda i, koff=koff: (0, koff)) if tb else pl.BlockSpec((K, N), lambda i, koff=koff: (koff, 0))]
        mm_args += [a, b]
        dims.append(((1,), (1 if tb else 0,)))
    n_mm = len(mm_args)
    n_in = n_mm + len(ins)
    rows = min(EPILOGUE_ROWS, tm)

    def body(*refs):
        in_refs, out_refs = refs[n_mm:n_in], refs[n_in:]

        def products(s):
            return [_dot(refs[2 * p][s * rows:(s + 1) * rows, :], refs[2 * p + 1][...], dims[p]) for p in range(len(pairs))]

        totals = {}
        accs = products(0)
        for s in range(tm // rows):
            ahead = products(s + 1) if (s + 1) * rows < tm else None
            outs = epilogue(*accs, *[r.at[pl.ds(s * rows, rows)] if r.shape[0] == tm else r for r in in_refs])
            for k, (ref, val) in enumerate(zip(out_refs, outs)):
                if val is None:
                    continue
                if k in sums:
                    totals[k] = val if s == 0 else totals[k] + val
                else:
                    ref[s * rows:(s + 1) * rows, :] = val.astype(ref.dtype)
            accs = ahead
        for k, total in totals.items():
            ref = out_refs[k]

            @pl.when(pl.program_id(0) == 0)
            def _():
                ref[...] = jnp.zeros_like(ref)

            ref[...] += total

    return pl.pallas_call(
        body, name=name, grid=(M // tm,),
        in_specs=mm_specs + list(in_specs),
        out_specs=list(out_specs), out_shape=list(out_shapes),
        input_output_aliases={n_mm + k: v for k, v in (aliases or {}).items()},
        compiler_params=_params("arbitrary"),
    )(*mm_args, *ins)


def _row_spec(tm, n):
    return pl.BlockSpec((tm, n), lambda i: (i, 0))


def _full_spec(shape):
    return pl.BlockSpec(shape, lambda i: tuple(0 for _ in shape))


MAX_DOT_COLS = 2048


def _resident_spec(shape):
    return pl.BlockSpec(shape, lambda i: tuple(0 for _ in shape), pipeline_mode=pl.Buffered(1))


def _norm_matmul(x, g, w_t, segments, *, tm, name):
    T, D = x.shape
    N = w_t.shape[0]
    tm = min(tm, T)
    chunks = [(c + o, r + o, min(MAX_DOT_COLS, n - o)) for c, r, n in segments for o in range(0, n, MAX_DOT_COLS)]

    def body(x_ref, g_ref, w_ref, u_ref, z_ref):
        u = _rms(x_ref[...], g_ref[...]).astype(BF16)
        u_ref[...] = u
        for c, r, n in chunks:
            z_ref[:, c:c + n] = _nt(u, w_ref[r:r + n, :]).astype(BF16)

    return pl.pallas_call(
        body, name=name, grid=(T // tm,),
        in_specs=[_row_spec(tm, D), _full_spec((1, D)), _resident_spec((N, D))],
        out_specs=[_row_spec(tm, D), _row_spec(tm, N)],
        out_shape=[jax.ShapeDtypeStruct((T, D), BF16), jax.ShapeDtypeStruct((T, N), BF16)],
        compiler_params=_params("parallel"),
    )(x, g, w_t)


def _matmul_norm_bwd(dz, w_t, segments, x, g, dres, after, *, tm, name):
    T, K = dz.shape
    D = w_t.shape[1]
    tm = min(tm, T)

    def body(dz_ref, w_ref, x_ref, g_ref, dr_ref, after_ref, dx_ref, dg_ref):
        @pl.when(pl.program_id(0) == 0)
        def _():
            dg_ref[...] = jnp.zeros_like(dg_ref)

        du = sum(_nn(dz_ref[:, c:c + n], w_ref[r:r + n, :]) for c, r, n in segments)
        _, vjp = jax.vjp(_rms, x_ref[...], g_ref[...])
        dx, dg = vjp(du)
        dx_ref[...] = dx + dr_ref[...]
        dg_ref[...] += dg

    row = _row_spec(tm, D)
    return pl.pallas_call(
        body, name=name, grid=(T // tm,),
        in_specs=[_row_spec(tm, K), _resident_spec((K, D)), row, _full_spec((1, D)), row, _full_spec(after.shape)],
        out_specs=[row, _full_spec((1, D))],
        out_shape=[jax.ShapeDtypeStruct((T, D), F32), jax.ShapeDtypeStruct((1, D), F32)],
        compiler_params=_params("arbitrary"),
    )(dz, w_t, x, g, dres, after)


def _merge_fn(gates, a, b):
    ga = gates[:, :D_MODEL].astype(F32)
    gb = gates[:, D_MODEL:].astype(F32)
    return _sigmoid(ga) * a.astype(F32) + _sigmoid(gb) * b.astype(F32)


def _gates_spec(tm):
    return pl.BlockSpec((tm, W_GATES), lambda i: (i, O_GATES // W_GATES))


CONV_TC = 256


def _shift_down(x, n, rows):
    return jnp.where(rows >= n, pltpu.roll(x, n, 0), 0.0)


def _shift_up(x, n, rows, S):
    return jnp.where(rows < S - n, pltpu.roll(x, S - n, 0), 0.0)


def _conv_act_fwd(gu, conv_w, conv_b, *, name):
    _, B, S, tc = gu.shape
    nc = D_FF // tc

    def body(g_ref, up_ref, w_ref, b_ref, o_ref, a_ref):
        g = g_ref[...].astype(F32)
        rows = lax.broadcasted_iota(jnp.int32, g.shape, 0)
        w = w_ref[...]
        a = w[2:3] * g + w[1:2] * _shift_down(g, 1, rows) + w[0:1] * _shift_down(g, 2, rows) + b_ref[...]
        o_ref[...] = (_silu(a) * up_ref[...].astype(F32)).astype(BF16)
        a_ref[...] = a.astype(BF16)

    col = pl.BlockSpec((None, S, tc), lambda b, j: (b, 0, j))
    tile = lambda off: pl.BlockSpec((None, None, S, tc), lambda b, j: (j + off, b, 0, 0))
    return pl.pallas_call(
        body, name=name, grid=(B, nc),
        in_specs=[tile(0), tile(nc),
                  pl.BlockSpec((3, tc), lambda b, j: (0, j)),
                  pl.BlockSpec((1, tc), lambda b, j: (0, j))],
        out_specs=[col, tile(0)],
        out_shape=[jax.ShapeDtypeStruct((B, S, D_FF), BF16), jax.ShapeDtypeStruct((nc, B, S, tc), BF16)],
        compiler_params=_params("parallel", "parallel"),
    )(gu, gu, conv_w, conv_b)


def _conv_act_bwd(gu, a_pre, conv_w, dact, *, name):
    _, B, S, tc = gu.shape
    nc = D_FF // tc

    def body(g_ref, up_ref, a_ref, w_ref, da_ref, dg_ref, dup_ref, dw_ref, db_ref):
        g = g_ref[...].astype(F32)
        up, a, dact = up_ref[...], a_ref[...], da_ref[...]
        rows = lax.broadcasted_iota(jnp.int32, g.shape, 0)
        w = w_ref[...]
        sg = _sigmoid(a)
        dup_ref[...] = dact * a * sg
        da = (dact * up * sg * (1.0 + a * (1.0 - sg))).astype(F32)
        da1 = _shift_up(da, 1, rows, S)
        da2 = _shift_up(da, 2, rows, S)
        dg_ref[...] = (w[2:3] * da + w[1:2] * da1 + w[0:1] * da2).astype(BF16)

        @pl.when(pl.program_id(1) == 0)
        def _():
            dw_ref[...] = jnp.zeros_like(dw_ref)
            db_ref[...] = jnp.zeros_like(db_ref)

        dw_ref[0:1, :] += jnp.sum(da2 * g, axis=0, keepdims=True)
        dw_ref[1:2, :] += jnp.sum(da1 * g, axis=0, keepdims=True)
        dw_ref[2:3, :] += jnp.sum(da * g, axis=0, keepdims=True)
        db_ref[...] += jnp.sum(da, axis=0, keepdims=True)

    col = pl.BlockSpec((None, S, tc), lambda j, b: (b, 0, j))
    tile = lambda off: pl.BlockSpec((None, None, S, tc), lambda j, b: (j + off, b, 0, 0))
    return pl.pallas_call(
        body, name=name, grid=(nc, B),
        in_specs=[tile(0), tile(nc), tile(0),
                  pl.BlockSpec((3, tc), lambda j, b: (0, j)),
                  col],
        out_specs=[col, col, pl.BlockSpec((3, tc), lambda j, b: (0, j)), pl.BlockSpec((1, tc), lambda j, b: (0, j))],
        out_shape=[jax.ShapeDtypeStruct((B, S, D_FF), BF16), jax.ShapeDtypeStruct((B, S, D_FF), BF16),
                   jax.ShapeDtypeStruct((3, D_FF), F32), jax.ShapeDtypeStruct((1, D_FF), F32)],
        compiler_params=_params("parallel", "arbitrary"),
    )(gu, gu, a_pre, conv_w, dact)


HGRN_CPB = 8
HF = HGRN_HEADS * HGRN_DK


def _tri(n, upper=False):
    r = lax.broadcasted_iota(jnp.int32, (n, n), 0)
    c = lax.broadcasted_iota(jnp.int32, (n, n), 1)
    return (c >= r) if upper else (r >= c)


def _hs(h):
    return slice(h * HGRN_DK, (h + 1) * HGRN_DK)


def _cumsum_rows(tri_b, x):
    hi = x.astype(BF16)
    lo = (x - hi.astype(F32)).astype(BF16)
    return _nn(tri_b, hi) + _nn(tri_b, lo)


def _hgrn_col(seg, h):
    return slice(seg * HF + h * HGRN_DK, seg * HF + (h + 1) * HGRN_DK)


def _hgrn_gates(q, fz, lb):
    sg = _sigmoid(fz)
    return _sigmoid(q), sg, lb + (1.0 - lb) * sg


def _hgrn_decays(b, q, sq, f):
    qf = q * sq
    k = 1.0 - f
    bref = b[CHUNK // 2:CHUNK // 2 + 1, :]
    blast = b[CHUNK - 1:CHUNK, :]
    e1 = jnp.exp2(b - bref)
    e2 = jnp.exp2(bref - b)
    e3 = e1 * jnp.exp2(bref)
    e4 = e2 * jnp.exp2(blast - bref)
    return (e1, e2, e3, e4), qf * e1, k * e2, qf * e3, k * e4, jnp.exp2(blast)


def _hgrn_fwd(zh, lb, gn, *, name):
    B, S, _ = zh.shape
    cpb = HGRN_CPB
    ts = cpb * CHUNK
    nblk = S // ts

    def body(z_ref, lb_ref, gn_ref, o_ref, st_ref, state):
        @pl.when(pl.program_id(1) == 0)
        def _():
            state[...] = jnp.zeros_like(state)

        R = range(HGRN_HEADS)
        causal = _tri(CHUNK)
        tril_b = causal.astype(BF16)
        lbh = [lb_ref[:, _hs(h)] for h in R]
        for c in range(cpb):
            rows = slice(c * CHUNK, (c + 1) * CHUNK)
            q = [z_ref[rows, _hgrn_col(0, h)].astype(F32) for h in R]
            gates = [_hgrn_gates(q[h], z_ref[rows, _hgrn_col(1, h)].astype(F32), lbh[h]) for h in R]
            b = [_cumsum_rows(tril_b, jnp.log2(gates[h][2])) for h in R]
            v = [z_ref[rows, _hgrn_col(2, h)] for h in R]
            dec, q_in, k_in, q_out, k_st = [], [], [], [], []
            for h in R:
                _, qi, ki, qo, ks, d = _hgrn_decays(b[h], q[h], gates[h][0], gates[h][2])
                dec.append(d)
                for lst, t in zip((q_in, k_in, q_out, k_st), (qi, ki, qo, ks)):
                    lst.append(t.astype(BF16))
            a = [jnp.where(causal, _nt(q_in[h], k_in[h]), 0.0).astype(BF16) for h in R]
            st = [state[h] for h in R]
            for h in R:
                st_ref[c, h] = st[h]
            o = [_nn(a[h], v[h]) + _nt(q_out[h], st[h].astype(BF16)) for h in R]
            for h in R:
                state[h] = st[h] * dec[h] + _tn(v[h], k_st[h])
            for h in R:
                o_ref[rows, _hs(h)] = (_rms(o[h], gn_ref[...]) * _silu(z_ref[rows, _hgrn_col(3, h)].astype(F32))).astype(BF16)

    return pl.pallas_call(
        body, name=name, grid=(B, nblk),
        in_specs=[pl.BlockSpec((None, ts, 4 * HF), lambda b, s: (b, s, 0)),
                  pl.BlockSpec((1, HF), lambda b, s: (0, 0)),
                  pl.BlockSpec((1, HGRN_DK), lambda b, s: (0, 0))],
        out_specs=[pl.BlockSpec((None, ts, HF), lambda b, s: (b, s, 0)),
                   pl.BlockSpec((None, cpb, HGRN_HEADS, HGRN_DK, HGRN_DK), lambda b, s: (b, s, 0, 0, 0))],
        out_shape=[jax.ShapeDtypeStruct((B, S, HF), BF16),
                   jax.ShapeDtypeStruct((B, S // CHUNK, HGRN_HEADS, HGRN_DK, HGRN_DK), F32)],
        scratch_shapes=[pltpu.VMEM((HGRN_HEADS, HGRN_DK, HGRN_DK), F32)],
        compiler_params=_params("arbitrary", "arbitrary"),
    )(zh, lb, gn)


def _hgrn_bwd(zh, lb, gn, states, doa, dz, *, name):
    B, S, _ = zh.shape
    cpb = HGRN_CPB
    ts = cpb * CHUNK
    nblk = S // ts
    rev = lambda b, s: (b, nblk - 1 - s, 0)

    def body(z_ref, lb_ref, gn_ref, st_ref, do_ref, dz_in, dz_ref, dlb_ref, dgn_ref, dstate):
        @pl.when(pl.program_id(1) == 0)
        def _():
            dstate[...] = jnp.zeros_like(dstate)

        @pl.when((pl.program_id(0) == 0) & (pl.program_id(1) == 0))
        def _():
            dlb_ref[...] = jnp.zeros_like(dlb_ref)
            dgn_ref[...] = jnp.zeros_like(dgn_ref)

        R = range(HGRN_HEADS)
        causal = _tri(CHUNK)
        tril_b = causal.astype(BF16)
        triu_b = _tri(CHUNK, upper=True).astype(BF16)
        rowid = lax.broadcasted_iota(jnp.int32, (CHUNK, HGRN_DK), 0)
        lbh = [lb_ref[:, _hs(h)] for h in R]
        gn = gn_ref[...]
        for c in reversed(range(cpb)):
            rows = slice(c * CHUNK, (c + 1) * CHUNK)
            q = [z_ref[rows, _hgrn_col(0, h)].astype(F32) for h in R]
            gates = [_hgrn_gates(q[h], z_ref[rows, _hgrn_col(1, h)].astype(F32), lbh[h]) for h in R]
            b = [_cumsum_rows(tril_b, jnp.log2(gates[h][2])) for h in R]
            v = [z_ref[rows, _hgrn_col(2, h)] for h in R]
            pre = [_hgrn_decays(b[h], q[h], gates[h][0], gates[h][2]) for h in R]
            q_in_b, k_in_b, q_out_b, k_st_b = ([pre[h][i].astype(BF16) for h in R] for i in (1, 2, 3, 4))
            a_b = [jnp.where(causal, _nt(q_in_b[h], k_in_b[h]), 0.0).astype(BF16) for h in R]
            st = [st_ref[c, h] for h in R]
            st_b = [t.astype(BF16) for t in st]
            o = [_nn(a_b[h], v[h]) + _nt(q_out_b[h], st_b[h]) for h in R]
            do_l, dgn_acc = [], jnp.zeros_like(gn)
            for h in R:
                hg = z_ref[rows, _hgrn_col(3, h)].astype(F32)
                dout = do_ref[rows, _hs(h)].astype(F32)
                shg = _sigmoid(hg)
                on_h, norm_vjp = jax.vjp(_rms, o[h], gn)
                d_o, d_gn = norm_vjp(dout * (hg * shg))
                do_l.append(d_o)
                dgn_acc = dgn_acc + d_gn
                dz_ref[rows, _hgrn_col(3, h)] = (dout * on_h * shg * (1.0 + hg * (1.0 - shg))).astype(BF16)
            dgn_ref[...] += dgn_acc
            do_b = [t.astype(BF16) for t in do_l]
            dst = [dstate[h] for h in R]
            dst_b = [t.astype(BF16) for t in dst]
            da_b = [jnp.where(causal, _nt(do_b[h], v[h]), 0.0).astype(BF16) for h in R]
            dv = [_tn(a_b[h], do_b[h]) + _nt(k_st_b[h], dst_b[h]) for h in R]
            dq_in = [_nn(da_b[h], k_in_b[h]) for h in R]
            dk_in = [_tn(da_b[h], q_in_b[h]) for h in R]
            dq_out = [_nn(do_b[h], st_b[h]) for h in R]
            dk_st = [_nn(v[h], dst_b[h]) for h in R]
            for h in R:
                dz_ref[rows, _hgrn_col(2, h)] = dv[h].astype(BF16)
            db = []
            for h in R:
                _, q_in, k_in, q_out, k_st, dec = pre[h]
                ddec = jnp.sum(st[h] * dst[h], axis=0, keepdims=True)
                t_qin, t_kin, t_kst = dq_in[h] * q_in, dk_in[h] * k_in, dk_st[h] * k_st
                dbref = jnp.sum(t_kin - t_qin, axis=0, keepdims=True)
                dblast = jnp.sum(t_kst, axis=0, keepdims=True) + ddec * dec
                db.append(t_qin - t_kin + dq_out[h] * q_out - t_kst
                          + jnp.where(rowid == CHUNK // 2, dbref, 0.0) + jnp.where(rowid == CHUNK - 1, dblast, 0.0))
            for h in R:
                dstate[h] = dst[h] * pre[h][5] + _tn(do_b[h], q_out_b[h])
            dlogf = [_cumsum_rows(triu_b, db[h]) for h in R]
            for h in R:
                sq, sg, f = gates[h]
                e1, e2, e3, e4 = pre[h][0]
                dqf = dq_in[h] * e1 + dq_out[h] * e3
                dk = dk_in[h] * e2 + dk_st[h] * e4
                df_open = (dlogf[h] / f - dk) * (1.0 - sg)
                dlb_ref[:, _hs(h)] += jnp.sum(df_open, axis=0, keepdims=True)
                dz_ref[rows, _hgrn_col(1, h)] = (df_open * ((1.0 - lbh[h]) * sg)).astype(BF16)
                dz_ref[rows, _hgrn_col(0, h)] = (dqf * sq * (1.0 + q[h] * (1.0 - sq))).astype(BF16)

    return pl.pallas_call(
        body, name=name, grid=(B, nblk),
        in_specs=[pl.BlockSpec((None, ts, 4 * HF), rev),
                  pl.BlockSpec((1, HF), lambda b, s: (0, 0)),
                  pl.BlockSpec((1, HGRN_DK), lambda b, s: (0, 0)),
                  pl.BlockSpec((None, cpb, HGRN_HEADS, HGRN_DK, HGRN_DK), lambda b, s: (b, nblk - 1 - s, 0, 0, 0)),
                  pl.BlockSpec((None, ts, HF), rev),
                  ANY],
        out_specs=[pl.BlockSpec((None, ts, 4 * HF), rev),
                   pl.BlockSpec((1, HF), lambda b, s: (0, 0)),
                   pl.BlockSpec((1, HGRN_DK), lambda b, s: (0, 0))],
        out_shape=[jax.ShapeDtypeStruct(dz.shape, BF16),
                   jax.ShapeDtypeStruct((1, HF), F32),
                   jax.ShapeDtypeStruct((1, HGRN_DK), F32)],
        input_output_aliases={5: 0},
        scratch_shapes=[pltpu.VMEM((HGRN_HEADS, HGRN_DK, HGRN_DK), F32)],
        compiler_params=_params("arbitrary", "arbitrary"),
    )(zh, lb, gn, states, doa, dz)


KV_W = ATT_KV_HEADS * ATT_HD
ATT_SCALE = ATT_HD ** -0.5


def _rope(x, cos, sin, inverse=False):
    half = ROPE_DIM // 2
    outs = []
    for p in range(x.shape[1] // 128):
        xp = x[:, p * 128:(p + 1) * 128]
        lane = lax.broadcasted_iota(jnp.int32, xp.shape, 1) % ATT_HD
        sw = jnp.where(lane < half, pltpu.roll(xp, 128 - half, 1), pltpu.roll(xp, half, 1))
        outs.append(xp * cos - sw * sin if inverse else xp * cos + sw * sin)
    return outs[0] if len(outs) == 1 else jnp.concatenate(outs, axis=1)


PAIRS_PER_KV = ATT_GROUP // 2


def _swap_halves(x):
    return pltpu.roll(x, ATT_HD, 1)


def _kv_padded(t, low):
    sw = _swap_halves(t)
    zero = jnp.zeros_like(t)
    out = []
    for g in range(ATT_KV_HEADS):
        in_low, in_high = (t, sw) if g == 0 else (sw, t)
        out.append((jnp.where(low, in_low, zero).astype(BF16), jnp.where(low, zero, in_high).astype(BF16)))
    return out


def _swa_mask(first_block):
    qi = lax.broadcasted_iota(jnp.int32, (WINDOW, 2 * WINDOW), 0)
    mi = lax.broadcasted_iota(jnp.int32, (WINDOW, 2 * WINDOW), 1)
    band = (mi > qi) & (mi <= qi + WINDOW)
    return band & (jnp.logical_not(first_block) | (mi >= WINDOW))


def _swa_specs(nb):
    cur = lambda b, i: (b, i, 0)
    prev = lambda b, i: (b, jnp.maximum(i - 1, 0), 0)
    return cur, prev


def _swa_z_specs():
    q = pl.BlockSpec((None, WINDOW, W_AQ), lambda b, i: (b, i, O_AQ // W_AQ))
    kv_prev = pl.BlockSpec((None, WINDOW, W_AKV), lambda b, i: (b, jnp.maximum(i - 1, 0), O_AKV // W_AKV))
    kv_cur = pl.BlockSpec((None, WINDOW, W_AKV), lambda b, i: (b, i, O_AKV // W_AKV))
    return q, kv_prev, kv_cur


def _swa_fwd(z, cos, sin, sinks, *, name):
    B, S, _ = z.shape
    nb = S // WINDOW
    cur, prev = _swa_specs(nb)

    def body(q_ref, kvp_ref, kvc_ref, cp_ref, sp_ref, cc_ref, sc_ref, sink_ref, o_ref, lse_ref, qr_ref, kr_ref):
        cos_c, sin_c = cc_ref[...], sc_ref[...]
        q = (_rope(q_ref[...].astype(F32), cos_c, sin_c) * ATT_SCALE).astype(BF16)
        k = jnp.concatenate([_rope(kvp_ref[:, :KV_W].astype(F32), cp_ref[...], sp_ref[...]),
                             _rope(kvc_ref[:, :KV_W].astype(F32), cos_c, sin_c)], axis=0)
        qr_ref[...] = q
        kr_ref[...] = k[WINDOW:].astype(BF16)
        v = jnp.concatenate([kvp_ref[:, KV_W:], kvc_ref[:, KV_W:]], axis=0).astype(F32)
        low = lax.broadcasted_iota(jnp.int32, k.shape, 1) < ATT_HD
        kpad = _kv_padded(k, low)
        vpad = _kv_padded(v, low)
        mask = _swa_mask(pl.program_id(1) == 0)
        lses = []
        for g in range(ATT_KV_HEADS):
            pairs = range(g * PAIRS_PER_KV, (g + 1) * PAIRS_PER_KV)
            keys = [(p, e) for p in pairs for e in (0, 1)]
            qp = {p: q[:, p * 128:(p + 1) * 128] for p in pairs}
            s = {pe: jnp.where(mask, _nt(qp[pe[0]], kpad[g][pe[1]]), NEG_INF) for pe in keys}
            pr = {}
            for pe in keys:
                sink = sink_ref[0, 2 * pe[0] + pe[1]]
                m = jnp.maximum(jnp.max(s[pe], axis=1, keepdims=True), sink)
                ex = jnp.exp(s[pe] - m)
                den = jnp.sum(ex, axis=1, keepdims=True) + jnp.exp(sink - m)
                pr[pe] = (ex * (1.0 / den)).astype(BF16)
                lses.append(m + jnp.log(den))
            for p in pairs:
                o_ref[:, p * 128:(p + 1) * 128] = (_nn(pr[p, 0], vpad[g][0]) + _nn(pr[p, 1], vpad[g][1])).astype(BF16)
        lse_ref[...] = jnp.concatenate(lses, axis=1)

    tab = lambda im: pl.BlockSpec((None, WINDOW, 128), im)
    return pl.pallas_call(
        body, name=name, grid=(B, nb),
        in_specs=[*_swa_z_specs(),
                  tab(prev), tab(prev), tab(cur), tab(cur),
                  pl.BlockSpec(memory_space=pltpu.SMEM)],
        out_specs=[pl.BlockSpec((None, WINDOW, D_MODEL), cur), pl.BlockSpec((None, WINDOW, ATT_HEADS), cur),
                   pl.BlockSpec((None, WINDOW, D_MODEL), cur), pl.BlockSpec((None, WINDOW, KV_W), cur)],
        out_shape=[jax.ShapeDtypeStruct((B, S, D_MODEL), BF16), jax.ShapeDtypeStruct((B, S, ATT_HEADS), F32),
                   jax.ShapeDtypeStruct((B, S, D_MODEL), BF16), jax.ShapeDtypeStruct((B, S, KV_W), BF16)],
        compiler_params=_params("parallel", "parallel"),
    )(z, z, z, cos, sin, cos, sin, sinks)


def _swa_bwd(z, qr, kr, cos, sin, sinks, lse, dob, dz, *, name):
    B, S, _ = z.shape
    nb = S // WINDOW
    cur, prev = _swa_specs(nb)

    def body(q_ref, krp_ref, krc_ref, kvp_ref, kvc_ref, cp_ref, sp_ref, cc_ref, sc_ref, sink_ref, lse_ref, do_ref, dz_in,
             dq_ref, dkc_ref, dkp_ref, dsink_ref):
        @pl.when((pl.program_id(0) == 0) & (pl.program_id(1) == 0))
        def _():
            dsink_ref[...] = jnp.zeros_like(dsink_ref)

        cos_c, sin_c, cos_p, sin_p = cc_ref[...], sc_ref[...], cp_ref[...], sp_ref[...]
        q = q_ref[...]
        k = jnp.concatenate([krp_ref[...], krc_ref[...]], axis=0).astype(F32)
        v = jnp.concatenate([kvp_ref[:, KV_W:], kvc_ref[:, KV_W:]], axis=0).astype(F32)
        low = lax.broadcasted_iota(jnp.int32, k.shape, 1) < ATT_HD
        kpad = _kv_padded(k, low)
        vpad = _kv_padded(v, low)
        mask = _swa_mask(pl.program_id(1) == 0)
        lse = lse_ref[...]
        dq_parts, dk_sum, dv_sum, dsinks = [], [], [], []
        for g in range(ATT_KV_HEADS):
            pairs = range(g * PAIRS_PER_KV, (g + 1) * PAIRS_PER_KV)
            keys = [(p, e) for p in pairs for e in (0, 1)]
            qp = {p: q[:, p * 128:(p + 1) * 128] for p in pairs}
            dop = {p: do_ref[:, p * 128:(p + 1) * 128] for p in pairs}
            s = {pe: jnp.where(mask, _nt(qp[pe[0]], kpad[g][pe[1]]), NEG_INF) for pe in keys}
            dp = {pe: _nt(dop[pe[0]], vpad[g][pe[1]]) for pe in keys}
            pr, ds = {}, {}
            for pe in keys:
                h = 2 * pe[0] + pe[1]
                lse_h = lse[:, h:h + 1]
                pf = jnp.exp(s[pe] - lse_h)
                delta = jnp.sum(pf * dp[pe], axis=1, keepdims=True)
                ds[pe] = (pf * (dp[pe] - delta)).astype(BF16)
                pr[pe] = pf.astype(BF16)
                p_sink = jnp.exp(sink_ref[0, h] - lse_h)
                dsinks.append(-jnp.sum(p_sink * delta, axis=0, keepdims=True))
            for p in pairs:
                dq_parts.append((_nn(ds[p, 0], kpad[g][0]) + _nn(ds[p, 1], kpad[g][1])) * ATT_SCALE)
            x = [sum(_tn(ds[p, e], qp[p]) for p in pairs) for e in (0, 1)]
            y = [sum(_tn(pr[p, e], dop[p]) for p in pairs) for e in (0, 1)]
            zk = jnp.where(low, x[0], x[1])
            zv = jnp.where(low, y[0], y[1])
            dk_sum.append(zk + _swap_halves(zk))
            dv_sum.append(zv + _swap_halves(zv))
        dq_ref[...] = _rope(jnp.concatenate(dq_parts, axis=1), cos_c, sin_c, inverse=True).astype(BF16)
        dk = jnp.where(low, dk_sum[0], dk_sum[1])
        dv = jnp.where(low, dv_sum[0], dv_sum[1])
        dkp_ref[:, :KV_W] = _rope(dk[:WINDOW], cos_p, sin_p, inverse=True)
        dkp_ref[:, KV_W:] = dv[:WINDOW]
        dkc_ref[:, :KV_W] = _rope(dk[WINDOW:], cos_c, sin_c, inverse=True)
        dkc_ref[:, KV_W:] = dv[WINDOW:]
        dsink_ref[...] += jnp.concatenate(dsinks, axis=1)

    tab = lambda im: pl.BlockSpec((None, WINDOW, 128), im)
    return pl.pallas_call(
        body, name=name, grid=(B, nb),
        in_specs=[pl.BlockSpec((None, WINDOW, D_MODEL), cur), tab(prev), tab(cur),
                  *_swa_z_specs()[1:],
                  tab(prev), tab(prev), tab(cur), tab(cur),
                  pl.BlockSpec(memory_space=pltpu.SMEM),
                  pl.BlockSpec((None, WINDOW, ATT_HEADS), cur),
                  pl.BlockSpec((None, WINDOW, D_MODEL), cur),
                  ANY],
        out_specs=[_swa_z_specs()[0],
                   pl.BlockSpec((None, WINDOW, 2 * KV_W), cur), pl.BlockSpec((None, WINDOW, 2 * KV_W), cur),
                   pl.BlockSpec((1, ATT_HEADS), lambda b, i: (0, 0))],
        out_shape=[jax.ShapeDtypeStruct(dz.shape, BF16),
                   jax.ShapeDtypeStruct((B, S, 2 * KV_W), F32), jax.ShapeDtypeStruct((B, S, 2 * KV_W), F32),
                   jax.ShapeDtypeStruct((1, ATT_HEADS), F32)],
        input_output_aliases={12: 0},
        compiler_params=_params("arbitrary", "arbitrary"),
    )(qr, kr, kr, z, z, cos, sin, cos, sin, sinks, lse, dob, dz)


def _swa_dkv_combine(dkv_cur, dkv_prev, dz, *, name):
    B, S, W = dkv_cur.shape

    def body(c_ref, p_ref, dz_in, o_ref):
        rows = lax.broadcasted_iota(jnp.int32, (S, W), 0)
        o_ref[...] = (c_ref[...] + _shift_up(p_ref[...], WINDOW, rows, S)).astype(BF16)

    seq = pl.BlockSpec((None, S, W), lambda b: (b, 0, 0))
    return pl.pallas_call(
        body, name=name, grid=(B,),
        in_specs=[seq, seq, ANY], out_specs=pl.BlockSpec((None, S, W), lambda b: (b, 0, O_AKV // W_AKV)),
        out_shape=jax.ShapeDtypeStruct(dz.shape, BF16),
        input_output_aliases={2: 0},
        compiler_params=_params("parallel"),
    )(dkv_cur, dkv_prev, dz)


def _rope_tables(positions):
    half = ROPE_DIM // 2
    inv = ROPE_THETA ** (-2.0 * jnp.arange(half, dtype=F32) / ROPE_DIM)
    lanes = [lane % ATT_HD for lane in range(2 * ATT_HD)]
    freq = jnp.array([j % half if j < ROPE_DIM else 0 for j in lanes])
    sign = jnp.array([(-1.0 if j < half else 1.0) if j < ROPE_DIM else 0.0 for j in lanes], F32)
    ang = positions.astype(F32)[..., None] * inv[freq]
    return jnp.where(sign != 0.0, jnp.cos(ang), 1.0), sign * jnp.sin(ang)


def _lower_bound(lb_logits, *, name):
    def body(l_ref, o_ref):
        l = l_ref[...]
        e = jnp.exp(l - jnp.max(l, axis=0, keepdims=True))
        o_ref[...] = e[0:1] / jnp.sum(e, axis=0, keepdims=True)

    return pl.pallas_call(body, name=name, out_shape=jax.ShapeDtypeStruct((1, lb_logits.shape[1]), F32))(lb_logits)


W_ZH, W_GATES, W_AQ, W_AKV = 4 * HF, 2 * D_MODEL, ATT_HEADS * ATT_HD, 2 * KV_W
O_ZH, O_GATES, O_AQ, O_AKV = 0, W_ZH, W_ZH + W_GATES, W_ZH + W_GATES + W_AQ
W_IN = W_ZH + W_GATES + W_AQ + W_AKV


W_IN_BLK = W_IN // N_DEV


def _reference_row_block(j, rows=256):
    nz, ng = W_ZH // rows, W_GATES // rows
    return jnp.where(j < nz, j, jnp.where(j < nz + ng, j + (W_AQ + W_AKV) // rows, j - ng))


W_IN_SEGMENTS = ((O_ZH, 0, W_ZH), (O_GATES, W_ZH + W_AQ + W_AKV, W_GATES), (O_AQ, W_ZH, W_AQ + W_AKV))


def _local_step(x, positions, target, small, w_in_t, rest_weights, emit, start_token):
    B, S, D = x.shape
    T = B * S
    x2 = x.reshape(T, D)
    cos, sin = _rope_tables(positions)
    lb = _lower_bound(small["lb_logits"], name="lb_fwd")
    zero = lambda tok: tok[0:1, 0:1]

    u1, z = _norm_matmul(x2, small["norm1_g"] + zero(start_token), w_in_t, W_IN_SEGMENTS, tm=512, name="norm1_mm_z")
    z3 = z.reshape(B, S, W_IN)
    oa, states = _hgrn_fwd(z3, lb, small["hgrn_norm_g"], name="hgrn_fwd")
    ob, lse, qr, kr = _swa_fwd(z3, cos, sin, small["attn_sinks"], name="swa_fwd")
    oa2 = oa.reshape(T, D)
    ob2 = ob.reshape(T, D)
    W = rest_weights("mix", ob)
    row = lambda tm, dtype=None: _row_spec(tm, D)
    tile = lambda dtype: jax.ShapeDtypeStruct((T, D), dtype)
    vec = _full_spec((1, D))
    vec_shape = jax.ShapeDtypeStruct((1, D), F32)

    def merge_ep(acc_a, acc_b, g_ref):
        pa, pb = acc_a.astype(BF16), acc_b.astype(BF16)
        return pa, pb, _merge_fn(g_ref[...], pa, pb)

    pa, pb, merged = _matmul_ep([(oa2, W["w_a"], False, 0), (ob2, W["w_b"], False, 0)], tm=1024, ins=[z], in_specs=[_gates_spec(1024)],
                                out_shapes=[tile(BF16)] * 3, out_specs=[row(1024)] * 3, epilogue=merge_ep, name="mm_pa_pb_merge")

    def resid_norm_ep(acc, x_ref, g_ref):
        hh = acc + x_ref[...]
        return hh, _rms(hh, g_ref[...])

    h, u2 = _matmul_ep([(merged, W["w_out"], False, 0)], tm=1024, ins=[x2, small["norm2_g"]], in_specs=[row(1024), vec],
                       out_shapes=[tile(F32), tile(BF16)], out_specs=[row(1024), row(1024)], epilogue=resid_norm_ep, name="mm_h_norm2")
    W.update(rest_weights("ffn", u2))
    gu3 = _matmul_col_tiles(u2, W["w_ffn_t"], tm=1024, tn=D_FF, tc=CONV_TC, name="mm_gu").reshape(2 * D_FF // CONV_TC, B, S, CONV_TC)
    act, a_pre = _conv_act_fwd(gu3, W["conv_w"], small["conv_b"], name="conv_act_fwd")
    act2 = act.reshape(T, D_FF)
    g = {}

    def loss_ep(acc, h_ref, g_ref, t_ref):
        y, vjp = jax.vjp(_rms, acc + h_ref[...], g_ref[...])
        err = y - t_ref[...]
        dx, dg = vjp(err * (1.0 / D))
        return dx, dx, dg, (0.5 / D) * jnp.sum(jnp.sum(err * err, axis=1, keepdims=True), axis=0, keepdims=True)

    dh2, dh2b, g["final_g"], loss = _matmul_ep(
        [(act2, W["w_down"], False, 0)], tm=512, ins=[h, small["final_g"].reshape(1, D), target.reshape(T, D)], in_specs=[row(512), vec, row(512)],
        out_shapes=[tile(F32), tile(BF16), vec_shape, jax.ShapeDtypeStruct((1, 1), F32)],
        out_specs=[row(512), row(512), vec, _full_spec((1, 1))], sums=(2, 3), epilogue=loss_ep, name="mm_h2_loss")
    dact = _matmul(dh2b, W["w_down"], tb=True, out_dtype=BF16, name="mm_dact", tm=1024, tn=D_FF)
    dw_down_t = _matmul(dh2b, act2, ta=True, out_dtype=BF16, name="mm_dw_down", tm=1024, tn=256, tk=8192)
    dg_, dup, g["conv_w"], g["conv_b"] = _conv_act_bwd(gu3, a_pre, W["conv_w"], dact.reshape(B, S, D_FF), name="conv_act_bwd")
    dg2 = dg_.reshape(T, D_FF)
    dup2 = dup.reshape(T, D_FF)
    dw_ffn_t = _matmul(u2, dg2, ta=True, out_t=True, out_dtype=BF16, into=lax.empty((2 * D_FF, D), BF16), o_noff=0, name="mm_dw_ffn_g", tm=1024, tn=256, tk=8192)
    dw_ffn_t = _matmul(u2, dup2, ta=True, out_t=True, out_dtype=BF16, into=dw_ffn_t, o_noff=D_FF // 256, name="mm_dw_ffn_u", tm=1024, tn=256, tk=8192)
    tok = emit("ffn", dict(w_ffn_t=dw_ffn_t, w_down=dw_down_t.T))
    def norm2_bwd_ep(acc_g, acc_u, h_ref, g_ref, dh2_ref):
        _, vjp = jax.vjp(_rms, h_ref[...], g_ref[...])
        dx, dg = vjp(acc_g + acc_u)
        dx = dx + dh2_ref[...]
        return dx, dx, dg

    dh, dhb, g["norm2_g"] = _matmul_ep(
        [(dg2, W["w_ffn_t"], False, 0), (dup2, W["w_ffn_t"], False, 1)], tm=512, ins=[h, small["norm2_g"] + zero(tok), dh2], in_specs=[row(512), vec, row(512)],
        out_shapes=[tile(F32), tile(BF16), vec_shape], out_specs=[row(512), row(512), vec], sums=(2,), epilogue=norm2_bwd_ep, name="mm_du2_norm2_bwd")
    dw_out = _matmul(merged, dhb, ta=True, out_dtype=BF16, name="mm_dw_out", tm=1024, tn=1024, tk=2048)

    def merge_bwd_ep(acc, g_ref, pa_ref, pb_ref, dz_in):
        gt = g_ref[...].astype(F32)
        sa = _sigmoid(gt[:, :D_MODEL])
        sb = _sigmoid(gt[:, D_MODEL:])
        dgates = jnp.concatenate([acc * pa_ref[...].astype(F32) * sa * (1.0 - sa), acc * pb_ref[...].astype(F32) * sb * (1.0 - sb)], axis=1)
        return dgates, acc * sa, acc * sb

    dz, dpa, dpb = _matmul_ep(
        [(dhb, W["w_out"], True, 0)], tm=512, ins=[z, pa, pb, lax.empty((T, W_IN), BF16)], in_specs=[_gates_spec(512), row(512), row(512), ANY],
        out_shapes=[jax.ShapeDtypeStruct((T, W_IN), BF16), tile(BF16), tile(BF16)], out_specs=[_gates_spec(512), row(512), row(512)],
        aliases={3: 0}, epilogue=merge_bwd_ep, name="mm_dmerged_merge_bwd")
    doa, dob = _matmul_ep([(dpa, W["w_a"], True, 0), (dpb, W["w_b"], True, 0)], tm=1024, ins=[], in_specs=[],
                          out_shapes=[tile(BF16)] * 2, out_specs=[row(1024)] * 2, epilogue=lambda da, db: (da, db), name="mm_doa_dob")
    dw_a = _matmul(oa2, dpa, ta=True, out_dtype=BF16, name="mm_dw_a", tm=1024, tn=1024, tk=2048)
    dw_b = _matmul(ob2, dpb, ta=True, out_dtype=BF16, name="mm_dw_b", tm=1024, tn=1024, tk=2048)
    tok = emit("mix", dict(w_out=dw_out, w_a=dw_a, w_b=dw_b))
    dz3, dkv_cur, dkv_prev, dsinks = _swa_bwd(z3, qr, kr, cos, sin, small["attn_sinks"] + zero(tok), lse, dob.reshape(B, S, D),
                                              dz.reshape(B, S, W_IN), name="swa_bwd")
    dz3 = _swa_dkv_combine(dkv_cur, dkv_prev, dz3, name="swa_dkv")
    g["attn_sinks"] = dsinks
    dz3, g["lb"], g["hgrn_norm_g"] = _hgrn_bwd(z3, lb, small["hgrn_norm_g"], states, doa.reshape(B, S, D), dz3, name="hgrn_bwd")
    dz = dz3.reshape(T, W_IN)
    dw_in_t = _matmul(u1, dz, ta=True, out_t=True, o_block_perm=_reference_row_block, out_dtype=BF16, name="mm_dw_in", tm=1024, tn=256, tk=8192)
    tok = emit("in", dict(w_in_t=dw_in_t))
    dx, g["norm1_g"] = _matmul_norm_bwd(dz, w_in_t, W_IN_SEGMENTS, x2, small["norm1_g"], dh, tok, tm=512, name="mm_du1_norm1_bwd")
    g["lb_logits"] = _lb_bwd(g.pop("lb"), lb, name="lb_bwd")
    return loss, dx.reshape(B, S, D), g


def _my_place():
    return lax.axis_index("x"), lax.axis_index("y"), lax.axis_index("c")


def _gather_blocks(x_ref, out_ref, send_sems, recv_sems, local_sem):
    x, y, c = _my_place()
    me, sibling = (x, y, c), (x, y, 1 - c)
    chips = [(1 - x, y), (x, 1 - y), (1 - x, 1 - y)]
    relayed = tuple(jnp.where(c == 0, a, b) for a, b in zip(chips[0], chips[1]))
    relay_to = tuple(jnp.where(c == 0, b, a) for a, b in zip(chips[0], chips[1]))

    def slot(px, py, pc):
        return out_ref.at[4 * px + 2 * py + pc]

    def copy(k, block, to, src=None):
        return pltpu.make_async_remote_copy(
            src_ref=slot(*block) if src is None else src, dst_ref=slot(*block),
            send_sem=send_sems.at[k], recv_sem=recv_sems.at[k], device_id=to, device_id_type=MESH)

    mine = pltpu.make_async_copy(x_ref, slot(*me), local_sem)
    mine.start()
    first = [copy(0, me, sibling, src=x_ref)]
    first += [copy(1 + j, me, (*chip, c), src=x_ref) for j, chip in enumerate(chips[:2])]
    for cp in first:
        cp.start()
    relay = copy(3, (*relayed, c), (*relay_to, c))
    passed = [copy(4 + j, (*chip, c), sibling) for j, chip in enumerate(chips)]
    for j, chip in enumerate(chips):
        copy(1 + j, (*chip, c), me).wait_recv()
        if j < 2:
            @pl.when(c == j)
            def _():
                relay.start()

        passed[j].start()
    copy(0, sibling, me).wait_recv()
    for j, chip in enumerate(chips):
        copy(4 + j, (*chip, 1 - c), me).wait_recv()
    for cp in first + [relay] + passed:
        cp.wait_send()
    mine.wait()


GATHER_SEMS = [pltpu.SemaphoreType.DMA((7,)), pltpu.SemaphoreType.DMA((7,)), pltpu.SemaphoreType.DMA]


def _all_gather(blk, *, name):
    return pl.pallas_call(
        _gather_body_fn(), name=name,
        out_shape=jax.ShapeDtypeStruct((N_DEV,) + blk.shape, blk.dtype),
        in_specs=[ANY], out_specs=ANY,
        scratch_shapes=GATHER_SEMS,
    )(blk)


def _gather_body_fn():
    def body(x_ref, out_ref, send_sems, recv_sems, local_sem):
        _gather_blocks(x_ref, out_ref, send_sems, recv_sems, local_sem)
    return body


SLAB_W = 1152
SMALL_SHAPES = dict(norm1_g=(1, D_MODEL), lb_logits=(2, HGRN_HEADS * HGRN_DK), hgrn_norm_g=(1, HGRN_DK), attn_sinks=(1, ATT_HEADS),
                    norm2_g=(1, D_MODEL), conv_b=(1, D_FF), final_g=(1, D_MODEL))
CONVW_BLK = D_FF // N_DEV
CONVW_STRIDE = SLAB_W // 3


def _slab_layout():
    layout, r = {}, 0
    for nm, (nr, w) in SMALL_SHAPES.items():
        layout[nm] = []
        for i in range(nr):
            for c0 in range(0, w, SLAB_W):
                layout[nm].append((r, i, c0, min(SLAB_W, w - c0)))
                r += 1
    return layout, r


SMALL_ROWS, _N_SMALL_ROWS = _slab_layout()
CONV_ROW0 = -(-_N_SMALL_ROWS // 8) * 8
LOSS_ROW = CONV_ROW0 + N_DEV
SLAB_ROWS = LOSS_ROW + 8


def _small_step(grads, g_conv_w, loss, params, moments, variances, dev, *, name):
    names = list(SMALL_ROWS)
    n = len(names)

    def body(dev_ref, *refs):
        g_refs = dict(zip(names, refs[:n]))
        gc_ref, loss_ref = refs[n], refs[n + 1]
        base = n + 2
        w_refs, m_refs, v_refs = (dict(zip(names + ["conv_w"], refs[base + i * (n + 1):base + (i + 1) * (n + 1)])) for i in range(3))
        o = base + 3 * (n + 1)
        gath_ref, loss_out = refs[o], refs[o + 1]
        outs = {nm: refs[o + 2 + 4 * i:o + 6 + 4 * i] for i, nm in enumerate(names + ["conv_w"])}
        slab, total, send_sems, recv_sems, local_sem = refs[-5:]

        slab[...] = jnp.zeros_like(slab)
        for nm, pieces in SMALL_ROWS.items():
            for r, i, c0, w in pieces:
                slab[r:r + 1, 0:w] = g_refs[nm][i:i + 1, c0:c0 + w]
        for p in range(N_DEV):
            for j in range(3):
                slab[CONV_ROW0 + p:CONV_ROW0 + p + 1, j * CONVW_STRIDE:j * CONVW_STRIDE + CONVW_BLK] = gc_ref[j:j + 1, p * CONVW_BLK:(p + 1) * CONVW_BLK]
        slab[LOSS_ROW:LOSS_ROW + 1, 0:1] = loss_ref[...]
        _gather_blocks(slab, gath_ref, send_sems, recv_sems, local_sem)
        acc = gath_ref[0]
        for p in range(1, N_DEV):
            acc = acc + gath_ref[p]
        total[...] = acc
        loss_out[...] = total[LOSS_ROW:LOSS_ROW + 1, 0:1]

        def update(nm, g, i, c0, w):
            at = (slice(i, i + 1), slice(c0, c0 + w))
            d, mn, vn = _adamw_math(w_refs[nm][at], g, m_refs[nm][at], v_refs[nm][at])
            for ref, val in zip(outs[nm], (g, d, mn, vn)):
                ref[at] = val

        for nm, pieces in SMALL_ROWS.items():
            for r, i, c0, w in pieces:
                update(nm, total[r:r + 1, 0:w], i, c0, w)
        conv_rows = total[CONV_ROW0:CONV_ROW0 + N_DEV, :]
        rowid = lax.broadcasted_iota(jnp.int32, conv_rows.shape, 0)
        mine = jnp.sum(jnp.where(rowid == dev_ref[0], conv_rows, 0.0), axis=0, keepdims=True)
        for j in range(3):
            update("conv_w", mine[:, j * CONVW_STRIDE:j * CONVW_STRIDE + CONVW_BLK], j, 0, CONVW_BLK)

    order = names + ["conv_w"]
    ins = [grads[nm] for nm in names] + [g_conv_w, loss]
    for d in (params, moments, variances):
        ins += [d[nm] for nm in order]
    vmem = pl.BlockSpec(memory_space=pltpu.VMEM)
    out_shape = [jax.ShapeDtypeStruct((N_DEV, SLAB_ROWS, SLAB_W), F32), jax.ShapeDtypeStruct((1, 1), F32)]
    for nm in order:
        out_shape += [jax.ShapeDtypeStruct(params[nm].shape, F32)] * 4
    res = pl.pallas_call(
        body, name=name,
        grid_spec=pltpu.PrefetchScalarGridSpec(
            num_scalar_prefetch=1, grid=(1,),
            in_specs=[vmem] * len(ins), out_specs=[vmem] * len(out_shape),
            scratch_shapes=[pltpu.VMEM((SLAB_ROWS, SLAB_W), F32), pltpu.VMEM((SLAB_ROWS, SLAB_W), F32)] + GATHER_SEMS),
        out_shape=out_shape,
    )(dev, *ins)
    return res[1], {nm: tuple(res[2 + 4 * i:6 + 4 * i]) for i, nm in enumerate(order)}


HBM_SPEC = pl.BlockSpec(memory_space=pltpu.HBM)
SEM_SPEC = pl.BlockSpec(memory_space=pltpu.SEMAPHORE)
DATAFLOW_EFFECT = pltpu.SideEffectType.DATAFLOW_SIDE_EFFECTING
N_PEERS = N_DEV - 1


def _peers(x, y, c):
    return [(1 - x if r & 4 else x, 1 - y if r & 2 else y, 1 - c if r & 1 else c) for r in range(1, N_DEV)]


def _exchange_start(srcs, scatter, *, after=None, name):
    n = len(srcs)
    lands = [lax.empty(a.shape if scatter else (N_DEV,) + a.shape, a.dtype) for a in srcs]
    extra = [] if after is None else [after]

    def body(*refs):
        src_refs, land_refs = refs[:n], refs[n:2 * n]
        send_sems, recv_sems, token = refs[2 * n + len(extra)], refs[2 * n + len(extra) + 1], refs[-1]
        x, y, c = _my_place()
        me = 4 * x + 2 * y + c
        for i in range(n):
            for r, (tx, ty, tc) in enumerate(_peers(x, y, c)):
                src = src_refs[i].at[4 * tx + 2 * ty + tc] if scatter else src_refs[i]
                pltpu.make_async_remote_copy(
                    src_ref=src, dst_ref=land_refs[i].at[me], send_sem=send_sems.at[N_PEERS * i + r],
                    recv_sem=recv_sems.at[N_PEERS * i + r], device_id=(tx, ty, tc), device_id_type=MESH).start()
        token[...] = jnp.zeros_like(token)

    thru = [pltpu.HBM(a.shape, a.dtype) for a in list(srcs) + lands]
    res = pl.pallas_call(
        body, name=name,
        out_shape=(pltpu.SemaphoreType.DMA((N_PEERS * n,)), pltpu.SemaphoreType.DMA((N_PEERS * n,)), *thru,
                   jax.ShapeDtypeStruct((8, 128), F32)),
        in_specs=[HBM_SPEC] * (2 * n) + [ANY] * len(extra),
        out_specs=(SEM_SPEC, SEM_SPEC, *([HBM_SPEC] * (2 * n)), pl.BlockSpec(memory_space=pltpu.VMEM)),
        input_output_aliases={i: 2 + i for i in range(2 * n)},
        compiler_params=pltpu.CompilerParams(has_side_effects=DATAFLOW_EFFECT),
    )(*[pltpu.with_memory_space_constraint(a, pltpu.HBM) for a in list(srcs) + lands], *extra)
    return (res[0], res[1], list(res[2:2 + n]), list(res[2 + n:2 + 2 * n]), scatter), res[-1]


def _exchange_wait(handle, after, *, name):
    send_sems, recv_sems, srcs, lands, scatter = handle
    n = len(srcs)

    def body(*refs):
        src_refs, land_refs = refs[:n], refs[n:2 * n]
        send_sems, recv_sems = refs[2 * n], refs[2 * n + 1]
        x, y, c = _my_place()
        for i in range(n):
            for r in range(N_PEERS):
                src = src_refs[i].at[0] if scatter else src_refs[i]
                cp = pltpu.make_async_remote_copy(
                    src_ref=src, dst_ref=land_refs[i].at[0], send_sem=send_sems.at[N_PEERS * i + r],
                    recv_sem=recv_sems.at[N_PEERS * i + r], device_id=(x, y, c), device_id_type=MESH)
                cp.wait_send()
                cp.wait_recv()

    thru = [pltpu.HBM(a.shape, a.dtype) for a in srcs + lands]
    res = pl.pallas_call(
        body, name=name, out_shape=tuple(thru),
        in_specs=[HBM_SPEC] * (2 * n) + [SEM_SPEC, SEM_SPEC, ANY], out_specs=tuple([HBM_SPEC] * (2 * n)),
        input_output_aliases={i: i for i in range(2 * n)},
        compiler_params=pltpu.CompilerParams(has_side_effects=DATAFLOW_EFFECT),
    )(*srcs, *lands, send_sems, recv_sems, after)
    return list(res[:n]), list(res[n:])


def _with_own(land, own, me):
    return lax.dynamic_update_index_in_dim(land, own, me, 0)


def _adamw_math(w, g, m, v):
    m = ADAM_B1 * m + (1.0 - ADAM_B1) * g
    v = ADAM_B2 * v + (1.0 - ADAM_B2) * (g * g)
    m_hat = m / (1.0 - ADAM_B1 ** ADAM_STEP)
    v_hat = v / (1.0 - ADAM_B2 ** ADAM_STEP)
    delta = -ADAM_LR * (m_hat / (jnp.sqrt(v_hat) + ADAM_EPS) + ADAM_WD * w)
    return delta, m, v


def _adamw_sum(parts, w, m, v, *, name):
    shape = w.shape
    R, n = shape[-2], shape[-1]
    w, m, v = (t.reshape(R, n) for t in (w, m, v))
    tr = _pick(R, (256, 464, 352, 128))

    def body(p_ref, w_ref, m_ref, v_ref, g_ref, d_ref, mo_ref, vo_ref):
        g = p_ref[0].astype(F32)
        for p in range(1, N_DEV):
            g = g + p_ref[p].astype(F32)
        d, mn, vn = _adamw_math(w_ref[...], g, m_ref[...], v_ref[...])
        g_ref[...] = g
        d_ref[...] = d
        mo_ref[...] = mn
        vo_ref[...] = vn

    row = pl.BlockSpec((tr, n), lambda i: (i, 0))
    outs = pl.pallas_call(
        body, name=name, grid=(R // tr,),
        in_specs=[pl.BlockSpec((N_DEV, tr, n), lambda i: (0, i, 0)), row, row, row],
        out_specs=[row, row, row, row],
        out_shape=[jax.ShapeDtypeStruct((R, n), F32)] * 4,
        compiler_params=_params("parallel"),
    )(parts, w, m, v)
    return [t.reshape(shape) for t in outs]


def _lb_bwd(dlb, lb, *, name):
    def body(d_ref, lb_ref, o_ref):
        t = d_ref[...] * lb_ref[...] * (1.0 - lb_ref[...])
        o_ref[0:1, :] = t
        o_ref[1:2, :] = -t

    return pl.pallas_call(body, name=name, out_shape=jax.ShapeDtypeStruct((2, lb.shape[1]), F32))(dlb, lb)


DOWN_BLK, ROW_BLK = D_FF // N_DEV, D_MODEL // N_DEV
W_FFN_BLK = 2 * D_FF // N_DEV
CONV_BITS_SHAPE = (16, 256)


def kernel(x, positions, norm1_g, w_in, lb_logits, hgrn_norm_g, w_a, attn_sinks, w_b, w_out, norm2_g, w_ffn_in, conv_w, conv_b, w_down, final_g, loss_target, m_norm1_g, m_w_in, m_lb_logits, m_hgrn_norm_g, m_w_a, m_attn_sinks, m_w_b, m_w_out, m_norm2_g, m_w_ffn_in, m_conv_w, m_conv_b, m_w_down, m_final_g, v_norm1_g, v_w_in, v_lb_logits, v_hgrn_norm_g, v_w_a, v_attn_sinks, v_w_b, v_w_out, v_norm2_g, v_w_ffn_in, v_conv_w, v_conv_b, v_w_down, v_final_g):
    xi, yi, ci = _my_place()
    dev = 4 * xi + 2 * yi + ci

    tr = lambda t: jnp.transpose(t[0])
    untr = lambda t: jnp.transpose(t)[None]
    w_in_blocks = _all_gather(tr(w_in).astype(BF16), name="ag_w_in")
    conv_bits = lax.bitcast_convert_type(conv_w, BF16).reshape(-1)
    conv_bits = jnp.pad(conv_bits, (0, CONV_BITS_SHAPE[0] * CONV_BITS_SHAPE[1] - conv_bits.shape[0])).reshape(CONV_BITS_SHAPE)
    w_in_full_t = w_in_blocks.reshape(W_IN, D_MODEL)
    gather_handles = {}
    gather_handles["mix"], tok_mix = _exchange_start([w_a[0].astype(BF16), w_b[0].astype(BF16), w_out[0].astype(BF16)], False,
                                                     after=w_in_full_t, name="ag_mix_start")
    gather_handles["ffn"], tok_ffn = _exchange_start([tr(w_ffn_in).astype(BF16), w_down[0].astype(BF16), conv_bits], False,
                                                     after=tok_mix, name="ag_ffn_start")
    start_token = tok_mix + tok_ffn

    def rest_weights(group, after):
        own, lands = _exchange_wait(gather_handles[group], after, name="ag_" + group + "_wait")
        full = [_with_own(l, o, dev) for l, o in zip(lands, own)]
        if group == "mix":
            return dict(zip(("w_a", "w_b", "w_out"), [t.reshape(D_MODEL, D_MODEL) for t in full]))
        bits = full[2].reshape(N_DEV, -1)[:, :3 * CONVW_BLK * 2].reshape(N_DEV, 3, CONVW_BLK, 2)
        return dict(w_ffn_t=full[0].reshape(2 * D_FF, D_MODEL), w_down=full[1].reshape(D_FF, D_MODEL),
                    conv_w=lax.bitcast_convert_type(bits, F32).transpose(1, 0, 2).reshape(3, D_FF))

    handles = {}

    def emit(group, gr):
        if group == "ffn":
            srcs = [gr["w_ffn_t"].reshape(N_DEV, W_FFN_BLK, D_MODEL), gr["w_down"].reshape(N_DEV, DOWN_BLK, D_MODEL)]
        elif group == "mix":
            srcs = [gr[n].reshape(N_DEV, ROW_BLK, D_MODEL) for n in ("w_out", "w_a", "w_b")]
        else:
            srcs = [gr["w_in_t"].reshape(N_DEV, W_IN_BLK, D_MODEL)]
        handles[group], token = _exchange_start(srcs, True, name="rs_" + group + "_start")
        return token

    small = dict(norm1_g=norm1_g, lb_logits=lb_logits, hgrn_norm_g=hgrn_norm_g, attn_sinks=attn_sinks, norm2_g=norm2_g,
                 conv_b=conv_b, final_g=final_g)
    loss, grad_x, g = _local_step(x, positions, loss_target, small, w_in_full_t, rest_weights, emit, start_token)

    def parts_of(group, after):
        srcs, lands = _exchange_wait(handles[group], after, name="rs_" + group + "_wait")
        return [_with_own(l, lax.dynamic_index_in_dim(s, dev, 0, keepdims=False), dev) for s, l in zip(srcs, lands)]

    p_ffn, p_down = parts_of("ffn", grad_x)
    p_out, p_a, p_b = parts_of("mix", grad_x)
    (p_in,) = parts_of("in", grad_x)
    big = dict(
        w_in=[untr(t) for t in _adamw_sum(p_in, tr(w_in), tr(m_w_in), tr(v_w_in), name="adamw_w_in")],
        w_a=_adamw_sum(p_a, w_a, m_w_a, v_w_a, name="adamw_w_a"),
        w_b=_adamw_sum(p_b, w_b, m_w_b, v_w_b, name="adamw_w_b"),
        w_out=_adamw_sum(p_out, w_out, m_w_out, v_w_out, name="adamw_w_out"),
        w_ffn_in=[untr(t) for t in _adamw_sum(p_ffn, tr(w_ffn_in), tr(m_w_ffn_in), tr(v_w_ffn_in), name="adamw_w_ffn_in")],
        w_down=_adamw_sum(p_down, w_down, m_w_down, v_w_down, name="adamw_w_down"),
    )

    row = lambda t: t.reshape(1, -1) if t.ndim == 1 else t
    shard = lambda t: t.reshape(3, CONVW_BLK)
    sm_g = {nm: g[nm] for nm in SMALL_ROWS}
    sm_w = dict(norm1_g=norm1_g, lb_logits=lb_logits, hgrn_norm_g=hgrn_norm_g, attn_sinks=attn_sinks, norm2_g=norm2_g,
                conv_b=conv_b, final_g=row(final_g), conv_w=shard(conv_w))
    sm_m = dict(norm1_g=m_norm1_g, lb_logits=m_lb_logits, hgrn_norm_g=m_hgrn_norm_g, attn_sinks=m_attn_sinks, norm2_g=m_norm2_g,
                conv_b=m_conv_b, final_g=row(m_final_g), conv_w=shard(m_conv_w))
    sm_v = dict(norm1_g=v_norm1_g, lb_logits=v_lb_logits, hgrn_norm_g=v_hgrn_norm_g, attn_sinks=v_attn_sinks, norm2_g=v_norm2_g,
                conv_b=v_conv_b, final_g=row(v_final_g), conv_w=shard(v_conv_w))
    loss_total, sm_out = _small_step(sm_g, g["conv_w"], loss, sm_w, sm_m, sm_v, dev.astype(jnp.int32).reshape(1), name="small_step")
    shapes = dict(final_g=final_g.shape, conv_w=conv_w.shape)

    names = ("norm1_g", "w_in", "lb_logits", "hgrn_norm_g", "w_a", "attn_sinks", "w_b", "w_out", "norm2_g", "w_ffn_in", "conv_w", "conv_b", "w_down", "final_g")
    outs = [loss_total.reshape(()), grad_x]
    for kind in range(4):
        outs += [big[n][kind] if n in big else sm_out[n][kind].reshape(shapes.get(n, sm_out[n][kind].shape)) for n in names]
    return tuple(outs)
```

```python
import jax
import jax.numpy as jnp
from jax import lax
from jax.experimental import pallas as pl
from jax.experimental.pallas import tpu as pltpu

F32 = jnp.float32
BF16 = jnp.bfloat16

D_MODEL = 1024
HGRN_HEADS = 8
HGRN_DK = 128
CHUNK = 64
ATT_HEADS = 16
ATT_KV_HEADS = 2
ATT_HD = 64
ATT_GROUP = ATT_HEADS // ATT_KV_HEADS
WINDOW = 128
ROPE_DIM = ATT_HD // 4
ROPE_THETA = 500000.0
D_FF = 2816
EPS = 1e-6
NEG_INF = -1e30
N_DEV = 8

ADAM_LR = 0.001
ADAM_B1 = 0.9
ADAM_B2 = 0.999
ADAM_EPS = 1e-08
ADAM_WD = 0.01
ADAM_STEP = 10

MESH = pl.DeviceIdType.MESH
ANY = pl.BlockSpec(memory_space=pl.ANY)


def _pick(n, cands):
    for c in cands:
        if n % c == 0:
            return c
    return n


def _sigmoid(x):
    return 0.5 * jnp.tanh(0.5 * x) + 0.5


def _silu(x):
    hx = 0.5 * x
    return hx * jnp.tanh(hx) + hx


def _rms(x, g):
    return x * lax.rsqrt(jnp.mean(x * x, axis=-1, keepdims=True) + EPS) * g


def _dot(a, b, dims):
    return lax.dot_general(a, b, (dims, ((), ())), preferred_element_type=F32)


def _nn(a, b):
    return _dot(a, b, ((1,), (0,)))


def _nt(a, b):
    return _dot(a, b, ((1,), (1,)))


def _tn(a, b):
    return _dot(a, b, ((0,), (0,)))


def _params(*sem):
    return pltpu.CompilerParams(dimension_semantics=sem, vmem_limit_bytes=56 * 1024 * 1024)


def _matmul(a, b, *, ta=False, tb=False, out_dtype=F32, addend=None, after=None, into=None, o_noff=0, out_t=False,
            o_block_perm=lambda j: j, name, tm, tn, tk=None, n_extent=None, b_koff=0, b_noff=0):
    M, K = (a.shape[1], a.shape[0]) if ta else a.shape
    N = n_extent or (b.shape[0] if tb else b.shape[1])
    tm, tn, tk = min(tm, M), min(tn, N), min(tk or K, K)
    assert M % tm == 0 and N % tn == 0 and K % tk == 0, (name, M, N, K, tm, tn, tk)
    nk = K // tk
    use_scratch = nk > 1 and out_dtype != F32
    grid = (M // tm, N // tn, nk)
    a_spec = pl.BlockSpec((tk, tm), lambda i, j, k: (k, i)) if ta else pl.BlockSpec((tm, tk), lambda i, j, k: (i, k))
    b_spec = pl.BlockSpec((tn, tk), lambda i, j, k: (j + b_noff, k + b_koff)) if tb else pl.BlockSpec((tk, tn), lambda i, j, k: (k + b_koff, j + b_noff))
    o_spec = pl.BlockSpec((tm, tn), lambda i, j, k: (i, j))
    dims = ((0 if ta else 1,), (1 if tb else 0,))
    has_add = addend is not None

    n_in = 2 + has_add + (after is not None) + (into is not None)

    def body(*refs):
        a_ref, b_ref = refs[:2]
        c_ref = refs[2] if has_add else None
        o_ref = refs[n_in]
        part = _dot(a_ref[...], b_ref[...], dims)
        if nk == 1:
            if has_add:
                part = part + c_ref[...].astype(F32)
            o_ref[...] = (part.T if out_t else part).astype(out_dtype)
        else:
            acc_ref = refs[-1] if use_scratch else o_ref
            k = pl.program_id(2)

            @pl.when(k == 0)
            def _():
                acc_ref[...] = part + c_ref[...].astype(F32) if has_add else part

            @pl.when(k > 0)
            def _():
                acc_ref[...] += part

            if use_scratch:
                @pl.when(k == nk - 1)
                def _():
                    o_ref[...] = acc_ref[...].astype(out_dtype)

    in_specs = [a_spec, b_spec] + ([o_spec] if has_add else [])
    args = (a, b) + ((addend,) if has_add else ())
    if after is not None:
        in_specs.append(pl.BlockSpec(after.shape, lambda i, j, k: (0, 0)))
        args += (after,)
    aliases = {}
    if into is not None:
        in_specs.append(ANY)
        args += (into,)
        aliases = {len(args) - 1: 0}
    if out_t:
        assert nk == 1 and not has_add
        o_spec = pl.BlockSpec((tn, tm), lambda i, j, k: (o_block_perm(j) + o_noff, i))
    elif into is not None:
        o_spec = pl.BlockSpec((tm, tn), lambda i, j, k: (i, j + o_noff))
    return pl.pallas_call(
        body,
        name=name,
        grid=grid,
        in_specs=in_specs,
        out_specs=o_spec,
        out_shape=jax.ShapeDtypeStruct(into.shape if into is not None else ((N, M) if out_t else (M, N)), out_dtype),
        input_output_aliases=aliases,
        scratch_shapes=[pltpu.VMEM((tm, tn), F32)] if use_scratch else [],
        compiler_params=_params("parallel", "parallel", "arbitrary"),
    )(*args)


def _matmul_col_tiles(a, b_t, *, tm, tn, tc, name):
    M, K = a.shape
    N = b_t.shape[0]
    tm = min(tm, M)
    per_step = tn // tc

    def body(a_ref, b_ref, o_ref):
        res = _nt(a_ref[...], b_ref[...]).astype(BF16)
        for t in range(per_step):
            o_ref[t] = res[:, t * tc:(t + 1) * tc]

    return pl.pallas_call(
        body, name=name, grid=(M // tm, N // tn),
        in_specs=[pl.BlockSpec((tm, K), lambda i, j: (i, 0)), pl.BlockSpec((tn, K), lambda i, j: (j, 0))],
        out_specs=pl.BlockSpec((per_step, tm, tc), lambda i, j: (j, i, 0)),
        out_shape=jax.ShapeDtypeStruct((N // tc, M, tc), BF16),
        compiler_params=_params("parallel", "parallel"),
    )(a, b_t)


EPILOGUE_ROWS = 256


def _matmul_ep(pairs, *, tm, ins, in_specs, out_shapes, out_specs, sums=(), epilogue, aliases=None, name):
    M = pairs[0][0].shape[0]
    tm = min(tm, M)
    mm_specs, mm_args, dims = [], [], []
    for a, b, tb, koff in pairs:
        K = a.shape[1]
        N = b.shape[0] if tb else b.shape[1]
        mm_specs += [pl.BlockSpec((tm, K), lambda i: (i, 0)),
                     pl.BlockSpec((N, K), lambda i, koff=koff: (0, koff)) if tb else pl.BlockSpec((K, N), lambda i, koff=koff: (koff, 0))]
        mm_args += [a, b]
        dims.append(((1,), (1 if tb else 0,)))
    n_mm = len(mm_args)
    n_in = n_mm + len(ins)
    rows = min(EPILOGUE_ROWS, tm)

    def body(*refs):
        in_refs, out_refs = refs[n_mm:n_in], refs[n_in:]

        def products(s):
            return [_dot(refs[2 * p][s * rows:(s + 1) * rows, :], refs[2 * p + 1][...], dims[p]) for p in range(len(pairs))]

        totals = {}
        accs = products(0)
        for s in range(tm // rows):
            ahead = products(s + 1) if (s + 1) * rows < tm else None
            outs = epilogue(*accs, *[r.at[pl.ds(s * rows, rows)] if r.shape[0] == tm else r for r in in_refs])
            for k, (ref, val) in enumerate(zip(out_refs, outs)):
                if val is None:
                    continue
                if k in sums:
                    totals[k] = val if s == 0 else totals[k] + val
                else:
                    ref[s * rows:(s + 1) * rows, :] = val.astype(ref.dtype)
            accs = ahead
        for k, total in totals.items():
            ref = out_refs[k]

            @pl.when(pl.program_id(0) == 0)
            def _():
                ref[...] = jnp.zeros_like(ref)

            ref[...] += total

    return pl.pallas_call(
        body, name=name, grid=(M // tm,),
        in_specs=mm_specs + list(in_specs),
        out_specs=list(out_specs), out_shape=list(out_shapes),
        input_output_aliases={n_mm + k: v for k, v in (aliases or {}).items()},
        compiler_params=_params("arbitrary"),
    )(*mm_args, *ins)


def _row_spec(tm, n):
    return pl.BlockSpec((tm, n), lambda i: (i, 0))


def _full_spec(shape):
    return pl.BlockSpec(shape, lambda i: tuple(0 for _ in shape))


MAX_DOT_COLS = 2048


def _resident_spec(shape):
    return pl.BlockSpec(shape, lambda i: tuple(0 for _ in shape), pipeline_mode=pl.Buffered(1))


def _norm_matmul(x, g, w_t, segments, *, tm, name):
    T, D = x.shape
    N = w_t.shape[0]
    tm = min(tm, T)
    chunks = [(c + o, r + o, min(MAX_DOT_COLS, n - o)) for c, r, n in segments for o in range(0, n, MAX_DOT_COLS)]

    def body(x_ref, g_ref, w_ref, u_ref, z_ref):
        u = _rms(x_ref[...], g_ref[...]).astype(BF16)
        u_ref[...] = u
        for c, r, n in chunks:
            z_ref[:, c:c + n] = _nt(u, w_ref[r:r + n, :]).astype(BF16)

    return pl.pallas_call(
        body, name=name, grid=(T // tm,),
        in_specs=[_row_spec(tm, D), _full_spec((1, D)), _resident_spec((N, D))],
        out_specs=[_row_spec(tm, D), _row_spec(tm, N)],
        out_shape=[jax.ShapeDtypeStruct((T, D), BF16), jax.ShapeDtypeStruct((T, N), BF16)],
        compiler_params=_params("parallel"),
    )(x, g, w_t)


def _matmul_norm_bwd(dz, w_t, segments, x, g, dres, after, *, tm, name):
    T, K = dz.shape
    D = w_t.shape[1]
    tm = min(tm, T)

    def body(dz_ref, w_ref, x_ref, g_ref, dr_ref, after_ref, dx_ref, dg_ref):
        @pl.when(pl.program_id(0) == 0)
        def _():
            dg_ref[...] = jnp.zeros_like(dg_ref)

        du = sum(_nn(dz_ref[:, c:c + n], w_ref[r:r + n, :]) for c, r, n in segments)
        _, vjp = jax.vjp(_rms, x_ref[...], g_ref[...])
        dx, dg = vjp(du)
        dx_ref[...] = dx + dr_ref[...]
        dg_ref[...] += dg

    row = _row_spec(tm, D)
    return pl.pallas_call(
        body, name=name, grid=(T // tm,),
        in_specs=[_row_spec(tm, K), _resident_spec((K, D)), row, _full_spec((1, D)), row, _full_spec(after.shape)],
        out_specs=[row, _full_spec((1, D))],
        out_shape=[jax.ShapeDtypeStruct((T, D), F32), jax.ShapeDtypeStruct((1, D), F32)],
        compiler_params=_params("arbitrary"),
    )(dz, w_t, x, g, dres, after)


def _merge_fn(gates, a, b):
    ga = gates[:, :D_MODEL].astype(F32)
    gb = gates[:, D_MODEL:].astype(F32)
    return _sigmoid(ga) * a.astype(F32) + _sigmoid(gb) * b.astype(F32)


def _gates_spec(tm):
    return pl.BlockSpec((tm, W_GATES), lambda i: (i, O_GATES // W_GATES))


CONV_TC = 256
INPUT_SLOTS = 3


def _shift_down(x, n, rows):
    return jnp.where(rows >= n, pltpu.roll(x, n, 0), 0.0)


def _shift_up(x, n, rows, S):
    return jnp.where(rows < S - n, pltpu.roll(x, S - n, 0), 0.0)


def _conv_act_fwd(gu, conv_w, conv_b, *, name):
    _, B, S, tc = gu.shape
    nc = D_FF // tc

    n_steps = B * nc

    def body(gu_ref, w_ref, b_ref, o_ref, a_ref, g_buf, up_buf, sems):
        t = pl.program_id(0) * nc + pl.program_id(1)

        def fetch(step):
            bb, jj, slot = step // nc, step % nc, step % INPUT_SLOTS
            return (pltpu.make_async_copy(gu_ref.at[jj, bb], g_buf.at[slot], sems.at[0, slot]),
                    pltpu.make_async_copy(gu_ref.at[jj + nc, bb], up_buf.at[slot], sems.at[1, slot]))

        @pl.when(t == 0)
        def _():
            for step in range(INPUT_SLOTS - 1):
                for cp in fetch(step):
                    cp.start()

        @pl.when(t + INPUT_SLOTS - 1 < n_steps)
        def _():
            for cp in fetch(t + INPUT_SLOTS - 1):
                cp.start()

        for cp in fetch(t):
            cp.wait()
        slot = t % INPUT_SLOTS
        g = g_buf[slot].astype(F32)
        rows = lax.broadcasted_iota(jnp.int32, g.shape, 0)
        w = w_ref[...]
        a = w[2:3] * g + w[1:2] * _shift_down(g, 1, rows) + w[0:1] * _shift_down(g, 2, rows) + b_ref[...]
        o_ref[...] = (_silu(a) * up_buf[slot].astype(F32)).astype(BF16)
        a_ref[...] = a.astype(BF16)

    col = pl.BlockSpec((None, S, tc), lambda b, j: (b, 0, j))
    tile = lambda off: pl.BlockSpec((None, None, S, tc), lambda b, j: (j + off, b, 0, 0))
    return pl.pallas_call(
        body, name=name, grid=(B, nc),
        in_specs=[ANY,
                  pl.BlockSpec((3, tc), lambda b, j: (0, j)),
                  pl.BlockSpec((1, tc), lambda b, j: (0, j))],
        out_specs=[col, tile(0)],
        out_shape=[jax.ShapeDtypeStruct((B, S, D_FF), BF16), jax.ShapeDtypeStruct((nc, B, S, tc), BF16)],
        scratch_shapes=[pltpu.VMEM((INPUT_SLOTS, S, tc), BF16), pltpu.VMEM((INPUT_SLOTS, S, tc), BF16),
                        pltpu.SemaphoreType.DMA((2, INPUT_SLOTS))],
        compiler_params=_params("arbitrary", "arbitrary"),
    )(gu, conv_w, conv_b)


def _conv_act_bwd(gu, a_pre, conv_w, dact, *, name):
    _, B, S, tc = gu.shape
    nc = D_FF // tc

    def body(g_ref, up_ref, a_ref, w_ref, da_ref, dg_ref, dup_ref, dw_ref, db_ref):
        g = g_ref[...].astype(F32)
        up, a, dact = up_ref[...], a_ref[...], da_ref[...]
        rows = lax.broadcasted_iota(jnp.int32, g.shape, 0)
        w = w_ref[...]
        sg = _sigmoid(a)
        dup_ref[...] = dact * a * sg
        da = (dact * up * sg * (1.0 + a * (1.0 - sg))).astype(F32)
        da1 = _shift_up(da, 1, rows, S)
        da2 = _shift_up(da, 2, rows, S)
        dg_ref[...] = (w[2:3] * da + w[1:2] * da1 + w[0:1] * da2).astype(BF16)

        @pl.when(pl.program_id(1) == 0)
        def _():
            dw_ref[...] = jnp.zeros_like(dw_ref)
            db_ref[...] = jnp.zeros_like(db_ref)

        dw_ref[0:1, :] += jnp.sum(da2 * g, axis=0, keepdims=True)
        dw_ref[1:2, :] += jnp.sum(da1 * g, axis=0, keepdims=True)
        dw_ref[2:3, :] += jnp.sum(da * g, axis=0, keepdims=True)
        db_ref[...] += jnp.sum(da, axis=0, keepdims=True)

    col = pl.BlockSpec((None, S, tc), lambda j, b: (b, 0, j))
    tile = lambda off: pl.BlockSpec((None, None, S, tc), lambda j, b: (j + off, b, 0, 0))
    return pl.pallas_call(
        body, name=name, grid=(nc, B),
        in_specs=[tile(0), tile(nc), tile(0),
                  pl.BlockSpec((3, tc), lambda j, b: (0, j)),
                  col],
        out_specs=[col, col, pl.BlockSpec((3, tc), lambda j, b: (0, j)), pl.BlockSpec((1, tc), lambda j, b: (0, j))],
        out_shape=[jax.ShapeDtypeStruct((B, S, D_FF), BF16), jax.ShapeDtypeStruct((B, S, D_FF), BF16),
                   jax.ShapeDtypeStruct((3, D_FF), F32), jax.ShapeDtypeStruct((1, D_FF), F32)],
        compiler_params=_params("parallel", "arbitrary"),
    )(gu, gu, a_pre, conv_w, dact)


HGRN_CPB = 8
HF = HGRN_HEADS * HGRN_DK


def _tri(n, upper=False):
    r = lax.broadcasted_iota(jnp.int32, (n, n), 0)
    c = lax.broadcasted_iota(jnp.int32, (n, n), 1)
    return (c >= r) if upper else (r >= c)


def _hs(h):
    return slice(h * HGRN_DK, (h + 1) * HGRN_DK)


def _cumsum_rows(tri_b, x):
    hi = x.astype(BF16)
    lo = (x - hi.astype(F32)).astype(BF16)
    return _nn(tri_b, hi) + _nn(tri_b, lo)


def _hgrn_col(seg, h):
    return slice(seg * HF + h * HGRN_DK, seg * HF + (h + 1) * HGRN_DK)


def _hgrn_gates(q, fz, lb):
    sg = _sigmoid(fz)
    return _sigmoid(q), sg, lb + (1.0 - lb) * sg


def _hgrn_decays(b, q, sq, f):
    qf = q * sq
    k = 1.0 - f
    bref = b[CHUNK // 2:CHUNK // 2 + 1, :]
    blast = b[CHUNK - 1:CHUNK, :]
    e1 = jnp.exp2(b - bref)
    e2 = jnp.exp2(bref - b)
    e3 = e1 * jnp.exp2(bref)
    e4 = e2 * jnp.exp2(blast - bref)
    return (e1, e2, e3, e4), qf * e1, k * e2, qf * e3, k * e4, jnp.exp2(blast)


def _hgrn_fwd(zh, lb, gn, *, name):
    B, S, _ = zh.shape
    cpb = HGRN_CPB
    ts = cpb * CHUNK
    nblk = S // ts

    def body(z_ref, lb_ref, gn_ref, o_ref, st_ref, state):
        @pl.when(pl.program_id(1) == 0)
        def _():
            state[...] = jnp.zeros_like(state)

        R = range(HGRN_HEADS)
        causal = _tri(CHUNK)
        tril_b = causal.astype(BF16)
        lbh = [lb_ref[:, _hs(h)] for h in R]
        for c in range(cpb):
            rows = slice(c * CHUNK, (c + 1) * CHUNK)
            q = [z_ref[rows, _hgrn_col(0, h)].astype(F32) for h in R]
            gates = [_hgrn_gates(q[h], z_ref[rows, _hgrn_col(1, h)].astype(F32), lbh[h]) for h in R]
            b = [_cumsum_rows(tril_b, jnp.log2(gates[h][2])) for h in R]
            v = [z_ref[rows, _hgrn_col(2, h)] for h in R]
            dec, q_in, k_in, q_out, k_st = [], [], [], [], []
            for h in R:
                _, qi, ki, qo, ks, d = _hgrn_decays(b[h], q[h], gates[h][0], gates[h][2])
                dec.append(d)
                for lst, t in zip((q_in, k_in, q_out, k_st), (qi, ki, qo, ks)):
                    lst.append(t.astype(BF16))
            a = [jnp.where(causal, _nt(q_in[h], k_in[h]), 0.0).astype(BF16) for h in R]
            st = [state[h] for h in R]
            for h in R:
                st_ref[c, h] = st[h]
            o = [_nn(a[h], v[h]) + _nt(q_out[h], st[h].astype(BF16)) for h in R]
            for h in R:
                state[h] = st[h] * dec[h] + _tn(v[h], k_st[h])
            for h in R:
                o_ref[rows, _hs(h)] = (_rms(o[h], gn_ref[...]) * _silu(z_ref[rows, _hgrn_col(3, h)].astype(F32))).astype(BF16)

    return pl.pallas_call(
        body, name=name, grid=(B, nblk),
        in_specs=[pl.BlockSpec((None, ts, 4 * HF), lambda b, s: (b, s, 0)),
                  pl.BlockSpec((1, HF), lambda b, s: (0, 0)),
                  pl.BlockSpec((1, HGRN_DK), lambda b, s: (0, 0))],
        out_specs=[pl.BlockSpec((None, ts, HF), lambda b, s: (b, s, 0)),
                   pl.BlockSpec((None, cpb, HGRN_HEADS, HGRN_DK, HGRN_DK), lambda b, s: (b, s, 0, 0, 0))],
        out_shape=[jax.ShapeDtypeStruct((B, S, HF), BF16),
                   jax.ShapeDtypeStruct((B, S // CHUNK, HGRN_HEADS, HGRN_DK, HGRN_DK), F32)],
        scratch_shapes=[pltpu.VMEM((HGRN_HEADS, HGRN_DK, HGRN_DK), F32)],
        compiler_params=_params("arbitrary", "arbitrary"),
    )(zh, lb, gn)


def _hgrn_bwd(zh, lb, gn, states, doa, dz, *, name):
    B, S, _ = zh.shape
    cpb = HGRN_CPB
    ts = cpb * CHUNK
    nblk = S // ts
    rev = lambda b, s: (b, nblk - 1 - s, 0)

    def body(z_ref, lb_ref, gn_ref, st_ref, do_ref, dz_in, dz_ref, dlb_ref, dgn_ref, dstate):
        @pl.when(pl.program_id(1) == 0)
        def _():
            dstate[...] = jnp.zeros_like(dstate)

        @pl.when((pl.program_id(0) == 0) & (pl.program_id(1) == 0))
        def _():
            dlb_ref[...] = jnp.zeros_like(dlb_ref)
            dgn_ref[...] = jnp.zeros_like(dgn_ref)

        R = range(HGRN_HEADS)
        causal = _tri(CHUNK)
        tril_b = causal.astype(BF16)
        triu_b = _tri(CHUNK, upper=True).astype(BF16)
        rowid = lax.broadcasted_iota(jnp.int32, (CHUNK, HGRN_DK), 0)
        lbh = [lb_ref[:, _hs(h)] for h in R]
        gn = gn_ref[...]
        for c in reversed(range(cpb)):
            rows = slice(c * CHUNK, (c + 1) * CHUNK)
            q = [z_ref[rows, _hgrn_col(0, h)].astype(F32) for h in R]
            gates = [_hgrn_gates(q[h], z_ref[rows, _hgrn_col(1, h)].astype(F32), lbh[h]) for h in R]
            b = [_cumsum_rows(tril_b, jnp.log2(gates[h][2])) for h in R]
            v = [z_ref[rows, _hgrn_col(2, h)] for h in R]
            pre = [_hgrn_decays(b[h], q[h], gates[h][0], gates[h][2]) for h in R]
            q_in_b, k_in_b, q_out_b, k_st_b = ([pre[h][i].astype(BF16) for h in R] for i in (1, 2, 3, 4))
            a_b = [jnp.where(causal, _nt(q_in_b[h], k_in_b[h]), 0.0).astype(BF16) for h in R]
            st = [st_ref[c, h] for h in R]
            st_b = [t.astype(BF16) for t in st]
            o = [_nn(a_b[h], v[h]) + _nt(q_out_b[h], st_b[h]) for h in R]
            do_l, dgn_acc = [], jnp.zeros_like(gn)
            for h in R:
                hg = z_ref[rows, _hgrn_col(3, h)].astype(F32)
                dout = do_ref[rows, _hs(h)].astype(F32)
                shg = _sigmoid(hg)
                on_h, norm_vjp = jax.vjp(_rms, o[h], gn)
                d_o, d_gn = norm_vjp(dout * (hg * shg))
                do_l.append(d_o)
                dgn_acc = dgn_acc + d_gn
                dz_ref[rows, _hgrn_col(3, h)] = (dout * on_h * shg * (1.0 + hg * (1.0 - shg))).astype(BF16)
            dgn_ref[...] += dgn_acc
            do_b = [t.astype(BF16) for t in do_l]
            dst = [dstate[h] for h in R]
            dst_b = [t.astype(BF16) for t in dst]
            da_b = [jnp.where(causal, _nt(do_b[h], v[h]), 0.0).astype(BF16) for h in R]
            dv = [_tn(a_b[h], do_b[h]) + _nt(k_st_b[h], dst_b[h]) for h in R]
            dq_in = [_nn(da_b[h], k_in_b[h]) for h in R]
            dk_in = [_tn(da_b[h], q_in_b[h]) for h in R]
            dq_out = [_nn(do_b[h], st_b[h]) for h in R]
            dk_st = [_nn(v[h], dst_b[h]) for h in R]
            for h in R:
                dz_ref[rows, _hgrn_col(2, h)] = dv[h].astype(BF16)
            db = []
            for h in R:
                _, q_in, k_in, q_out, k_st, dec = pre[h]
                ddec = jnp.sum(st[h] * dst[h], axis=0, keepdims=True)
                t_qin, t_kin, t_kst = dq_in[h] * q_in, dk_in[h] * k_in, dk_st[h] * k_st
                dbref = jnp.sum(t_kin - t_qin, axis=0, keepdims=True)
                dblast = jnp.sum(t_kst, axis=0, keepdims=True) + ddec * dec
                db.append(t_qin - t_kin + dq_out[h] * q_out - t_kst
                          + jnp.where(rowid == CHUNK // 2, dbref, 0.0) + jnp.where(rowid == CHUNK - 1, dblast, 0.0))
            for h in R:
                dstate[h] = dst[h] * pre[h][5] + _tn(do_b[h], q_out_b[h])
            dlogf = [_cumsum_rows(triu_b, db[h]) for h in R]
            for h in R:
                sq, sg, f = gates[h]
                e1, e2, e3, e4 = pre[h][0]
                dqf = dq_in[h] * e1 + dq_out[h] * e3
                dk = dk_in[h] * e2 + dk_st[h] * e4
                df_open = (dlogf[h] / f - dk) * (1.0 - sg)
                dlb_ref[:, _hs(h)] += jnp.sum(df_open, axis=0, keepdims=True)
                dz_ref[rows, _hgrn_col(1, h)] = (df_open * ((1.0 - lbh[h]) * sg)).astype(BF16)
                dz_ref[rows, _hgrn_col(0, h)] = (dqf * sq * (1.0 + q[h] * (1.0 - sq))).astype(BF16)

    return pl.pallas_call(
        body, name=name, grid=(B, nblk),
        in_specs=[pl.BlockSpec((None, ts, 4 * HF), rev),
                  pl.BlockSpec((1, HF), lambda b, s: (0, 0)),
                  pl.BlockSpec((1, HGRN_DK), lambda b, s: (0, 0)),
                  pl.BlockSpec((None, cpb, HGRN_HEADS, HGRN_DK, HGRN_DK), lambda b, s: (b, nblk - 1 - s, 0, 0, 0)),
                  pl.BlockSpec((None, ts, HF), rev),
                  ANY],
        out_specs=[pl.BlockSpec((None, ts, 4 * HF), rev),
                   pl.BlockSpec((1, HF), lambda b, s: (0, 0)),
                   pl.BlockSpec((1, HGRN_DK), lambda b, s: (0, 0))],
        out_shape=[jax.ShapeDtypeStruct(dz.shape, BF16),
                   jax.ShapeDtypeStruct((1, HF), F32),
                   jax.ShapeDtypeStruct((1, HGRN_DK), F32)],
        input_output_aliases={5: 0},
        scratch_shapes=[pltpu.VMEM((HGRN_HEADS, HGRN_DK, HGRN_DK), F32)],
        compiler_params=_params("arbitrary", "arbitrary"),
    )(zh, lb, gn, states, doa, dz)


KV_W = ATT_KV_HEADS * ATT_HD
ATT_SCALE = ATT_HD ** -0.5


def _rope(x, cos, sin, inverse=False):
    half = ROPE_DIM // 2
    outs = []
    for p in range(x.shape[1] // 128):
        xp = x[:, p * 128:(p + 1) * 128]
        lane = lax.broadcasted_iota(jnp.int32, xp.shape, 1) % ATT_HD
        sw = jnp.where(lane < half, pltpu.roll(xp, 128 - half, 1), pltpu.roll(xp, half, 1))
        outs.append(xp * cos - sw * sin if inverse else xp * cos + sw * sin)
    return outs[0] if len(outs) == 1 else jnp.concatenate(outs, axis=1)


PAIRS_PER_KV = ATT_GROUP // 2


def _swap_halves(x):
    return pltpu.roll(x, ATT_HD, 1)


def _kv_padded(t, low):
    sw = _swap_halves(t)
    zero = jnp.zeros_like(t)
    out = []
    for g in range(ATT_KV_HEADS):
        in_low, in_high = (t, sw) if g == 0 else (sw, t)
        out.append((jnp.where(low, in_low, zero).astype(BF16), jnp.where(low, zero, in_high).astype(BF16)))
    return out


def _swa_mask(first_block):
    qi = lax.broadcasted_iota(jnp.int32, (WINDOW, 2 * WINDOW), 0)
    mi = lax.broadcasted_iota(jnp.int32, (WINDOW, 2 * WINDOW), 1)
    band = (mi > qi) & (mi <= qi + WINDOW)
    return band & (jnp.logical_not(first_block) | (mi >= WINDOW))


def _swa_specs(nb):
    cur = lambda b, i: (b, i, 0)
    prev = lambda b, i: (b, jnp.maximum(i - 1, 0), 0)
    return cur, prev


def _swa_z_specs():
    q = pl.BlockSpec((None, WINDOW, W_AQ), lambda b, i: (b, i, O_AQ // W_AQ))
    kv_prev = pl.BlockSpec((None, WINDOW, W_AKV), lambda b, i: (b, jnp.maximum(i - 1, 0), O_AKV // W_AKV))
    kv_cur = pl.BlockSpec((None, WINDOW, W_AKV), lambda b, i: (b, i, O_AKV // W_AKV))
    return q, kv_prev, kv_cur


def _swa_fwd(z, cos, sin, sinks, *, name):
    B, S, _ = z.shape
    nb = S // WINDOW
    cur, prev = _swa_specs(nb)

    def body(q_ref, kvp_ref, kvc_ref, cp_ref, sp_ref, cc_ref, sc_ref, sink_ref, o_ref, lse_ref, qr_ref, kr_ref):
        cos_c, sin_c = cc_ref[...], sc_ref[...]
        q = (_rope(q_ref[...].astype(F32), cos_c, sin_c) * ATT_SCALE).astype(BF16)
        k = jnp.concatenate([_rope(kvp_ref[:, :KV_W].astype(F32), cp_ref[...], sp_ref[...]),
                             _rope(kvc_ref[:, :KV_W].astype(F32), cos_c, sin_c)], axis=0)
        qr_ref[...] = q
        kr_ref[...] = k[WINDOW:].astype(BF16)
        v = jnp.concatenate([kvp_ref[:, KV_W:], kvc_ref[:, KV_W:]], axis=0).astype(F32)
        low = lax.broadcasted_iota(jnp.int32, k.shape, 1) < ATT_HD
        kpad = _kv_padded(k, low)
        vpad = _kv_padded(v, low)
        mask = _swa_mask(pl.program_id(1) == 0)
        lses = []
        for g in range(ATT_KV_HEADS):
            pairs = range(g * PAIRS_PER_KV, (g + 1) * PAIRS_PER_KV)
            keys = [(p, e) for p in pairs for e in (0, 1)]
            qp = {p: q[:, p * 128:(p + 1) * 128] for p in pairs}
            s = {pe: jnp.where(mask, _nt(qp[pe[0]], kpad[g][pe[1]]), NEG_INF) for pe in keys}
            pr = {}
            for pe in keys:
                sink = sink_ref[0, 2 * pe[0] + pe[1]]
                m = jnp.maximum(jnp.max(s[pe], axis=1, keepdims=True), sink)
                ex = jnp.exp(s[pe] - m)
                den = jnp.sum(ex, axis=1, keepdims=True) + jnp.exp(sink - m)
                pr[pe] = (ex * (1.0 / den)).astype(BF16)
                lses.append(m + jnp.log(den))
            for p in pairs:
                o_ref[:, p * 128:(p + 1) * 128] = (_nn(pr[p, 0], vpad[g][0]) + _nn(pr[p, 1], vpad[g][1])).astype(BF16)
        lse_ref[...] = jnp.concatenate(lses, axis=1)

    tab = lambda im: pl.BlockSpec((None, WINDOW, 128), im)
    return pl.pallas_call(
        body, name=name, grid=(B, nb),
        in_specs=[*_swa_z_specs(),
                  tab(prev), tab(prev), tab(cur), tab(cur),
                  pl.BlockSpec(memory_space=pltpu.SMEM)],
        out_specs=[pl.BlockSpec((None, WINDOW, D_MODEL), cur), pl.BlockSpec((None, WINDOW, ATT_HEADS), cur),
                   pl.BlockSpec((None, WINDOW, D_MODEL), cur), pl.BlockSpec((None, WINDOW, KV_W), cur)],
        out_shape=[jax.ShapeDtypeStruct((B, S, D_MODEL), BF16), jax.ShapeDtypeStruct((B, S, ATT_HEADS), F32),
                   jax.ShapeDtypeStruct((B, S, D_MODEL), BF16), jax.ShapeDtypeStruct((B, S, KV_W), BF16)],
        compiler_params=_params("parallel", "parallel"),
    )(z, z, z, cos, sin, cos, sin, sinks)


def _swa_bwd(z, qr, kr, cos, sin, sinks, lse, dob, dz, *, name):
    B, S, _ = z.shape
    nb = S // WINDOW
    cur, prev = _swa_specs(nb)

    def body(q_ref, krp_ref, krc_ref, kvp_ref, kvc_ref, cp_ref, sp_ref, cc_ref, sc_ref, sink_ref, lse_ref, do_ref, dz_in,
             dq_ref, dkc_ref, dkp_ref, dsink_ref):
        @pl.when((pl.program_id(0) == 0) & (pl.program_id(1) == 0))
        def _():
            dsink_ref[...] = jnp.zeros_like(dsink_ref)

        cos_c, sin_c, cos_p, sin_p = cc_ref[...], sc_ref[...], cp_ref[...], sp_ref[...]
        q = q_ref[...]
        k = jnp.concatenate([krp_ref[...], krc_ref[...]], axis=0).astype(F32)
        v = jnp.concatenate([kvp_ref[:, KV_W:], kvc_ref[:, KV_W:]], axis=0).astype(F32)
        low = lax.broadcasted_iota(jnp.int32, k.shape, 1) < ATT_HD
        kpad = _kv_padded(k, low)
        vpad = _kv_padded(v, low)
        mask = _swa_mask(pl.program_id(1) == 0)
        lse = lse_ref[...]
        dq_parts, dk_sum, dv_sum, dsinks = [], [], [], []
        for g in range(ATT_KV_HEADS):
            pairs = range(g * PAIRS_PER_KV, (g + 1) * PAIRS_PER_KV)
            keys = [(p, e) for p in pairs for e in (0, 1)]
            qp = {p: q[:, p * 128:(p + 1) * 128] for p in pairs}
            dop = {p: do_ref[:, p * 128:(p + 1) * 128] for p in pairs}
            s = {pe: jnp.where(mask, _nt(qp[pe[0]], kpad[g][pe[1]]), NEG_INF) for pe in keys}
            dp = {pe: _nt(dop[pe[0]], vpad[g][pe[1]]) for pe in keys}
            pr, ds = {}, {}
            for pe in keys:
                h = 2 * pe[0] + pe[1]
                lse_h = lse[:, h:h + 1]
                pf = jnp.exp(s[pe] - lse_h)
                delta = jnp.sum(pf * dp[pe], axis=1, keepdims=True)
                ds[pe] = (pf * (dp[pe] - delta)).astype(BF16)
                pr[pe] = pf.astype(BF16)
                p_sink = jnp.exp(sink_ref[0, h] - lse_h)
                dsinks.append(-jnp.sum(p_sink * delta, axis=0, keepdims=True))
            for p in pairs:
                dq_parts.append((_nn(ds[p, 0], kpad[g][0]) + _nn(ds[p, 1], kpad[g][1])) * ATT_SCALE)
            x = [sum(_tn(ds[p, e], qp[p]) for p in pairs) for e in (0, 1)]
            y = [sum(_tn(pr[p, e], dop[p]) for p in pairs) for e in (0, 1)]
            zk = jnp.where(low, x[0], x[1])
            zv = jnp.where(low, y[0], y[1])
            dk_sum.append(zk + _swap_halves(zk))
            dv_sum.append(zv + _swap_halves(zv))
        dq_ref[...] = _rope(jnp.concatenate(dq_parts, axis=1), cos_c, sin_c, inverse=True).astype(BF16)
        dk = jnp.where(low, dk_sum[0], dk_sum[1])
        dv = jnp.where(low, dv_sum[0], dv_sum[1])
        dkp_ref[:, :KV_W] = _rope(dk[:WINDOW], cos_p, sin_p, inverse=True)
        dkp_ref[:, KV_W:] = dv[:WINDOW]
        dkc_ref[:, :KV_W] = _rope(dk[WINDOW:], cos_c, sin_c, inverse=True)
        dkc_ref[:, KV_W:] = dv[WINDOW:]
        dsink_ref[...] += jnp.concatenate(dsinks, axis=1)

    tab = lambda im: pl.BlockSpec((None, WINDOW, 128), im)
    return pl.pallas_call(
        body, name=name, grid=(B, nb),
        in_specs=[pl.BlockSpec((None, WINDOW, D_MODEL), cur), tab(prev), tab(cur),
                  *_swa_z_specs()[1:],
                  tab(prev), tab(prev), tab(cur), tab(cur),
                  pl.BlockSpec(memory_space=pltpu.SMEM),
                  pl.BlockSpec((None, WINDOW, ATT_HEADS), cur),
                  pl.BlockSpec((None, WINDOW, D_MODEL), cur),
                  ANY],
        out_specs=[_swa_z_specs()[0],
                   pl.BlockSpec((None, WINDOW, 2 * KV_W), cur), pl.BlockSpec((None, WINDOW, 2 * KV_W), cur),
                   pl.BlockSpec((1, ATT_HEADS), lambda b, i: (0, 0))],
        out_shape=[jax.ShapeDtypeStruct(dz.shape, BF16),
                   jax.ShapeDtypeStruct((B, S, 2 * KV_W), F32), jax.ShapeDtypeStruct((B, S, 2 * KV_W), F32),
                   jax.ShapeDtypeStruct((1, ATT_HEADS), F32)],
        input_output_aliases={12: 0},
        compiler_params=_params("arbitrary", "arbitrary"),
    )(qr, kr, kr, z, z, cos, sin, cos, sin, sinks, lse, dob, dz)


def _swa_dkv_combine(dkv_cur, dkv_prev, dz, *, name):
    B, S, W = dkv_cur.shape

    def body(c_ref, p_ref, dz_in, o_ref):
        rows = lax.broadcasted_iota(jnp.int32, (S, W), 0)
        o_ref[...] = (c_ref[...] + _shift_up(p_ref[...], WINDOW, rows, S)).astype(BF16)

    seq = pl.BlockSpec((None, S, W), lambda b: (b, 0, 0))
    return pl.pallas_call(
        body, name=name, grid=(B,),
        in_specs=[seq, seq, ANY], out_specs=pl.BlockSpec((None, S, W), lambda b: (b, 0, O_AKV // W_AKV)),
        out_shape=jax.ShapeDtypeStruct(dz.shape, BF16),
        input_output_aliases={2: 0},
        compiler_params=_params("parallel"),
    )(dkv_cur, dkv_prev, dz)


def _rope_tables(positions):
    half = ROPE_DIM // 2
    inv = ROPE_THETA ** (-2.0 * jnp.arange(half, dtype=F32) / ROPE_DIM)
    ang = positions.astype(F32)[..., None] * inv
    c, s = jnp.cos(ang), jnp.sin(ang)
    pad = jnp.zeros(ang.shape[:-1] + (ATT_HD - ROPE_DIM,), F32)
    cos = jnp.concatenate([c, c, pad + 1.0], axis=-1)
    sin = jnp.concatenate([-s, s, pad], axis=-1)
    return jnp.tile(cos, (1, 1, 2)), jnp.tile(sin, (1, 1, 2))


def _lower_bound(lb_logits, *, name):
    def body(l_ref, o_ref):
        l = l_ref[...]
        e = jnp.exp(l - jnp.max(l, axis=0, keepdims=True))
        o_ref[...] = e[0:1] / jnp.sum(e, axis=0, keepdims=True)

    return pl.pallas_call(body, name=name, out_shape=jax.ShapeDtypeStruct((1, lb_logits.shape[1]), F32))(lb_logits)


W_ZH, W_GATES, W_AQ, W_AKV = 4 * HF, 2 * D_MODEL, ATT_HEADS * ATT_HD, 2 * KV_W
O_ZH, O_GATES, O_AQ, O_AKV = 0, W_ZH, W_ZH + W_GATES, W_ZH + W_GATES + W_AQ
W_IN = W_ZH + W_GATES + W_AQ + W_AKV


W_IN_BLK = W_IN // N_DEV


def _reference_row_block(j, rows=256):
    nz, ng = W_ZH // rows, W_GATES // rows
    return jnp.where(j < nz, j, jnp.where(j < nz + ng, j + (W_AQ + W_AKV) // rows, j - ng))


W_IN_SEGMENTS = ((O_ZH, 0, W_ZH), (O_GATES, W_ZH + W_AQ + W_AKV, W_GATES), (O_AQ, W_ZH, W_AQ + W_AKV))


def _local_step(x, positions, target, small, w_in_t, rest_weights, emit, start_token):
    B, S, D = x.shape
    T = B * S
    x2 = x.reshape(T, D)
    cos, sin = _rope_tables(positions)
    lb = _lower_bound(small["lb_logits"], name="lb_fwd")
    zero = lambda tok: tok[0:1, 0:1]

    u1, z = _norm_matmul(x2, small["norm1_g"] + zero(start_token), w_in_t, W_IN_SEGMENTS, tm=512, name="norm1_mm_z")
    z3 = z.reshape(B, S, W_IN)
    oa, states = _hgrn_fwd(z3, lb, small["hgrn_norm_g"], name="hgrn_fwd")
    ob, lse, qr, kr = _swa_fwd(z3, cos, sin, small["attn_sinks"], name="swa_fwd")
    oa2 = oa.reshape(T, D)
    ob2 = ob.reshape(T, D)
    W = rest_weights("mix", ob)
    row = lambda tm, dtype=None: _row_spec(tm, D)
    tile = lambda dtype: jax.ShapeDtypeStruct((T, D), dtype)
    vec = _full_spec((1, D))
    vec_shape = jax.ShapeDtypeStruct((1, D), F32)

    def merge_ep(acc_a, acc_b, g_ref):
        pa, pb = acc_a.astype(BF16), acc_b.astype(BF16)
        return pa, pb, _merge_fn(g_ref[...], pa, pb)

    pa, pb, merged = _matmul_ep([(oa2, W["w_a"], False, 0), (ob2, W["w_b"], False, 0)], tm=1024, ins=[z], in_specs=[_gates_spec(1024)],
                                out_shapes=[tile(BF16)] * 3, out_specs=[row(1024)] * 3, epilogue=merge_ep, name="mm_pa_pb_merge")

    def resid_norm_ep(acc, x_ref, g_ref):
        hh = acc + x_ref[...]
        return hh, _rms(hh, g_ref[...])

    h, u2 = _matmul_ep([(merged, W["w_out"], False, 0)], tm=1024, ins=[x2, small["norm2_g"]], in_specs=[row(1024), vec],
                       out_shapes=[tile(F32), tile(BF16)], out_specs=[row(1024), row(1024)], epilogue=resid_norm_ep, name="mm_h_norm2")
    W.update(rest_weights("ffn", u2))
    gu3 = _matmul_col_tiles(u2, W["w_ffn_t"], tm=1024, tn=D_FF, tc=CONV_TC, name="mm_gu").reshape(2 * D_FF // CONV_TC, B, S, CONV_TC)
    act, a_pre = _conv_act_fwd(gu3, W["conv_w"], small["conv_b"], name="conv_act_fwd")
    act2 = act.reshape(T, D_FF)
    g = {}

    def loss_ep(acc, h_ref, g_ref, t_ref):
        y, vjp = jax.vjp(_rms, acc + h_ref[...], g_ref[...])
        err = y - t_ref[...]
        dx, dg = vjp(err * (1.0 / D))
        return dx, dx, dg, (0.5 / D) * jnp.sum(jnp.sum(err * err, axis=1, keepdims=True), axis=0, keepdims=True)

    dh2, dh2b, g["final_g"], loss = _matmul_ep(
        [(act2, W["w_down"], False, 0)], tm=512, ins=[h, small["final_g"].reshape(1, D), target.reshape(T, D)], in_specs=[row(512), vec, row(512)],
        out_shapes=[tile(F32), tile(BF16), vec_shape, jax.ShapeDtypeStruct((1, 1), F32)],
        out_specs=[row(512), row(512), vec, _full_spec((1, 1))], sums=(2, 3), epilogue=loss_ep, name="mm_h2_loss")
    dact = _matmul(dh2b, W["w_down"], tb=True, out_dtype=BF16, name="mm_dact", tm=1024, tn=D_FF)
    dw_down_t = _matmul(dh2b, act2, ta=True, out_dtype=BF16, name="mm_dw_down", tm=1024, tn=256, tk=8192)
    dg_, dup, g["conv_w"], g["conv_b"] = _conv_act_bwd(gu3, a_pre, W["conv_w"], dact.reshape(B, S, D_FF), name="conv_act_bwd")
    dg2 = dg_.reshape(T, D_FF)
    dup2 = dup.reshape(T, D_FF)
    dw_ffn_t = _matmul(u2, dg2, ta=True, out_t=True, out_dtype=BF16, into=lax.empty((2 * D_FF, D), BF16), o_noff=0, name="mm_dw_ffn_g", tm=1024, tn=256, tk=8192)
    dw_ffn_t = _matmul(u2, dup2, ta=True, out_t=True, out_dtype=BF16, into=dw_ffn_t, o_noff=D_FF // 256, name="mm_dw_ffn_u", tm=1024, tn=256, tk=8192)
    tok = emit("ffn", dict(w_ffn_t=dw_ffn_t, w_down=dw_down_t.T))
    def norm2_bwd_ep(acc_g, acc_u, h_ref, g_ref, dh2_ref):
        _, vjp = jax.vjp(_rms, h_ref[...], g_ref[...])
        dx, dg = vjp(acc_g + acc_u)
        dx = dx + dh2_ref[...]
        return dx, dx, dg

    dh, dhb, g["norm2_g"] = _matmul_ep(
        [(dg2, W["w_ffn_t"], False, 0), (dup2, W["w_ffn_t"], False, 1)], tm=512, ins=[h, small["norm2_g"] + zero(tok), dh2], in_specs=[row(512), vec, row(512)],
        out_shapes=[tile(F32), tile(BF16), vec_shape], out_specs=[row(512), row(512), vec], sums=(2,), epilogue=norm2_bwd_ep, name="mm_du2_norm2_bwd")
    dw_out = _matmul(merged, dhb, ta=True, out_dtype=BF16, name="mm_dw_out", tm=1024, tn=1024, tk=2048)

    def merge_bwd_ep(acc, g_ref, pa_ref, pb_ref, dz_in):
        gt = g_ref[...].astype(F32)
        sa = _sigmoid(gt[:, :D_MODEL])
        sb = _sigmoid(gt[:, D_MODEL:])
        dgates = jnp.concatenate([acc * pa_ref[...].astype(F32) * sa * (1.0 - sa), acc * pb_ref[...].astype(F32) * sb * (1.0 - sb)], axis=1)
        return dgates, acc * sa, acc * sb

    dz, dpa, dpb = _matmul_ep(
        [(dhb, W["w_out"], True, 0)], tm=512, ins=[z, pa, pb, lax.empty((T, W_IN), BF16)], in_specs=[_gates_spec(512), row(512), row(512), ANY],
        out_shapes=[jax.ShapeDtypeStruct((T, W_IN), BF16), tile(BF16), tile(BF16)], out_specs=[_gates_spec(512), row(512), row(512)],
        aliases={3: 0}, epilogue=merge_bwd_ep, name="mm_dmerged_merge_bwd")
    doa, dob = _matmul_ep([(dpa, W["w_a"], True, 0), (dpb, W["w_b"], True, 0)], tm=1024, ins=[], in_specs=[],
                          out_shapes=[tile(BF16)] * 2, out_specs=[row(1024)] * 2, epilogue=lambda da, db: (da, db), name="mm_doa_dob")
    dw_a = _matmul(oa2, dpa, ta=True, out_dtype=BF16, name="mm_dw_a", tm=1024, tn=1024, tk=2048)
    dw_b = _matmul(ob2, dpb, ta=True, out_dtype=BF16, name="mm_dw_b", tm=1024, tn=1024, tk=2048)
    tok = emit("mix", dict(w_out=dw_out, w_a=dw_a, w_b=dw_b))
    dz3, dkv_cur, dkv_prev, dsinks = _swa_bwd(z3, qr, kr, cos, sin, small["attn_sinks"] + zero(tok), lse, dob.reshape(B, S, D),
                                              dz.reshape(B, S, W_IN), name="swa_bwd")
    dz3 = _swa_dkv_combine(dkv_cur, dkv_prev, dz3, name="swa_dkv")
    g["attn_sinks"] = dsinks
    dz3, g["lb"], g["hgrn_norm_g"] = _hgrn_bwd(z3, lb, small["hgrn_norm_g"], states, doa.reshape(B, S, D), dz3, name="hgrn_bwd")
    dz = dz3.reshape(T, W_IN)
    dw_in_t = _matmul(u1, dz, ta=True, out_t=True, o_block_perm=_reference_row_block, out_dtype=BF16, name="mm_dw_in", tm=1024, tn=256, tk=8192)
    tok = emit("in", dict(w_in_t=dw_in_t))
    dx, g["norm1_g"] = _matmul_norm_bwd(dz, w_in_t, W_IN_SEGMENTS, x2, small["norm1_g"], dh, tok, tm=512, name="mm_du1_norm1_bwd")
    g["lb_logits"] = _lb_bwd(g.pop("lb"), lb, name="lb_bwd")
    return loss, dx.reshape(B, S, D), g


def _my_place():
    return lax.axis_index("x"), lax.axis_index("y"), lax.axis_index("c")


def _gather_blocks(x_ref, out_ref, send_sems, recv_sems, local_sem):
    x, y, c = _my_place()
    me, sibling = (x, y, c), (x, y, 1 - c)
    chips = [(1 - x, y), (x, 1 - y), (1 - x, 1 - y)]
    relayed = tuple(jnp.where(c == 0, a, b) for a, b in zip(chips[0], chips[1]))
    relay_to = tuple(jnp.where(c == 0, b, a) for a, b in zip(chips[0], chips[1]))

    def slot(px, py, pc):
        return out_ref.at[4 * px + 2 * py + pc]

    def copy(k, block, to, src=None):
        return pltpu.make_async_remote_copy(
            src_ref=slot(*block) if src is None else src, dst_ref=slot(*block),
            send_sem=send_sems.at[k], recv_sem=recv_sems.at[k], device_id=to, device_id_type=MESH)

    mine = pltpu.make_async_copy(x_ref, slot(*me), local_sem)
    mine.start()
    first = [copy(0, me, sibling, src=x_ref)]
    first += [copy(1 + j, me, (*chip, c), src=x_ref) for j, chip in enumerate(chips[:2])]
    for cp in first:
        cp.start()
    relay = copy(3, (*relayed, c), (*relay_to, c))
    passed = [copy(4 + j, (*chip, c), sibling) for j, chip in enumerate(chips)]
    for j, chip in enumerate(chips):
        copy(1 + j, (*chip, c), me).wait_recv()
        if j < 2:
            @pl.when(c == j)
            def _():
                relay.start()

        passed[j].start()
    copy(0, sibling, me).wait_recv()
    for j, chip in enumerate(chips):
        copy(4 + j, (*chip, 1 - c), me).wait_recv()
    for cp in first + [relay] + passed:
        cp.wait_send()
    mine.wait()


GATHER_SEMS = [pltpu.SemaphoreType.DMA((7,)), pltpu.SemaphoreType.DMA((7,)), pltpu.SemaphoreType.DMA]


def _all_gather(blk, *, name):
    return pl.pallas_call(
        _gather_body_fn(), name=name,
        out_shape=jax.ShapeDtypeStruct((N_DEV,) + blk.shape, blk.dtype),
        in_specs=[ANY], out_specs=ANY,
        scratch_shapes=GATHER_SEMS,
    )(blk)


def _gather_body_fn():
    def body(x_ref, out_ref, send_sems, recv_sems, local_sem):
        _gather_blocks(x_ref, out_ref, send_sems, recv_sems, local_sem)
    return body


SLAB_W = 1152
SMALL_SHAPES = dict(norm1_g=(1, D_MODEL), lb_logits=(2, HGRN_HEADS * HGRN_DK), hgrn_norm_g=(1, HGRN_DK), attn_sinks=(1, ATT_HEADS),
                    norm2_g=(1, D_MODEL), conv_b=(1, D_FF), final_g=(1, D_MODEL))
CONVW_BLK = D_FF // N_DEV
CONVW_STRIDE = SLAB_W // 3


def _slab_layout():
    layout, r = {}, 0
    for nm, (nr, w) in SMALL_SHAPES.items():
        layout[nm] = []
        for i in range(nr):
            for c0 in range(0, w, SLAB_W):
                layout[nm].append((r, i, c0, min(SLAB_W, w - c0)))
                r += 1
    return layout, r


SMALL_ROWS, _N_SMALL_ROWS = _slab_layout()
CONV_ROW0 = -(-_N_SMALL_ROWS // 8) * 8
LOSS_ROW = CONV_ROW0 + N_DEV
SLAB_ROWS = LOSS_ROW + 8


def _small_step(grads, g_conv_w, loss, params, moments, variances, dev, *, name):
    names = list(SMALL_ROWS)
    n = len(names)

    def body(dev_ref, *refs):
        g_refs = dict(zip(names, refs[:n]))
        gc_ref, loss_ref = refs[n], refs[n + 1]
        base = n + 2
        w_refs, m_refs, v_refs = (dict(zip(names + ["conv_w"], refs[base + i * (n + 1):base + (i + 1) * (n + 1)])) for i in range(3))
        o = base + 3 * (n + 1)
        gath_ref, loss_out = refs[o], refs[o + 1]
        outs = {nm: refs[o + 2 + 4 * i:o + 6 + 4 * i] for i, nm in enumerate(names + ["conv_w"])}
        slab, total, send_sems, recv_sems, local_sem = refs[-5:]

        slab[...] = jnp.zeros_like(slab)
        for nm, pieces in SMALL_ROWS.items():
            for r, i, c0, w in pieces:
                slab[r:r + 1, 0:w] = g_refs[nm][i:i + 1, c0:c0 + w]
        for p in range(N_DEV):
            for j in range(3):
                slab[CONV_ROW0 + p:CONV_ROW0 + p + 1, j * CONVW_STRIDE:j * CONVW_STRIDE + CONVW_BLK] = gc_ref[j:j + 1, p * CONVW_BLK:(p + 1) * CONVW_BLK]
        slab[LOSS_ROW:LOSS_ROW + 1, 0:1] = loss_ref[...]
        _gather_blocks(slab, gath_ref, send_sems, recv_sems, local_sem)
        acc = gath_ref[0]
        for p in range(1, N_DEV):
            acc = acc + gath_ref[p]
        total[...] = acc
        loss_out[...] = total[LOSS_ROW:LOSS_ROW + 1, 0:1]

        def update(nm, g, i, c0, w):
            at = (slice(i, i + 1), slice(c0, c0 + w))
            d, mn, vn = _adamw_math(w_refs[nm][at], g, m_refs[nm][at], v_refs[nm][at])
            for ref, val in zip(outs[nm], (g, d, mn, vn)):
                ref[at] = val

        for nm, pieces in SMALL_ROWS.items():
            for r, i, c0, w in pieces:
                update(nm, total[r:r + 1, 0:w], i, c0, w)
        conv_rows = total[CONV_ROW0:CONV_ROW0 + N_DEV, :]
        rowid = lax.broadcasted_iota(jnp.int32, conv_rows.shape, 0)
        mine = jnp.sum(jnp.where(rowid == dev_ref[0], conv_rows, 0.0), axis=0, keepdims=True)
        for j in range(3):
            update("conv_w", mine[:, j * CONVW_STRIDE:j * CONVW_STRIDE + CONVW_BLK], j, 0, CONVW_BLK)

    order = names + ["conv_w"]
    ins = [grads[nm] for nm in names] + [g_conv_w, loss]
    for d in (params, moments, variances):
        ins += [d[nm] for nm in order]
    vmem = pl.BlockSpec(memory_space=pltpu.VMEM)
    out_shape = [jax.ShapeDtypeStruct((N_DEV, SLAB_ROWS, SLAB_W), F32), jax.ShapeDtypeStruct((1, 1), F32)]
    for nm in order:
        out_shape += [jax.ShapeDtypeStruct(params[nm].shape, F32)] * 4
    res = pl.pallas_call(
        body, name=name,
        grid_spec=pltpu.PrefetchScalarGridSpec(
            num_scalar_prefetch=1, grid=(1,),
            in_specs=[vmem] * len(ins), out_specs=[vmem] * len(out_shape),
            scratch_shapes=[pltpu.VMEM((SLAB_ROWS, SLAB_W), F32), pltpu.VMEM((SLAB_ROWS, SLAB_W), F32)] + GATHER_SEMS),
        out_shape=out_shape,
    )(dev, *ins)
    return res[1], {nm: tuple(res[2 + 4 * i:6 + 4 * i]) for i, nm in enumerate(order)}


HBM_SPEC = pl.BlockSpec(memory_space=pltpu.HBM)
SEM_SPEC = pl.BlockSpec(memory_space=pltpu.SEMAPHORE)
DATAFLOW_EFFECT = pltpu.SideEffectType.DATAFLOW_SIDE_EFFECTING
N_PEERS = N_DEV - 1


def _peers(x, y, c):
    return [(1 - x if r & 4 else x, 1 - y if r & 2 else y, 1 - c if r & 1 else c) for r in range(1, N_DEV)]


def _exchange_start(srcs, scatter, *, after=None, name):
    n = len(srcs)
    lands = [lax.empty(a.shape if scatter else (N_DEV,) + a.shape, a.dtype) for a in srcs]
    extra = [] if after is None else [after]

    def body(*refs):
        src_refs, land_refs = refs[:n], refs[n:2 * n]
        send_sems, recv_sems, token = refs[2 * n + len(extra)], refs[2 * n + len(extra) + 1], refs[-1]
        x, y, c = _my_place()
        me = 4 * x + 2 * y + c
        for i in range(n):
            for r, (tx, ty, tc) in enumerate(_peers(x, y, c)):
                src = src_refs[i].at[4 * tx + 2 * ty + tc] if scatter else src_refs[i]
                pltpu.make_async_remote_copy(
                    src_ref=src, dst_ref=land_refs[i].at[me], send_sem=send_sems.at[N_PEERS * i + r],
                    recv_sem=recv_sems.at[N_PEERS * i + r], device_id=(tx, ty, tc), device_id_type=MESH).start()
        token[...] = jnp.zeros_like(token)

    thru = [pltpu.HBM(a.shape, a.dtype) for a in list(srcs) + lands]
    res = pl.pallas_call(
        body, name=name,
        out_shape=(pltpu.SemaphoreType.DMA((N_PEERS * n,)), pltpu.SemaphoreType.DMA((N_PEERS * n,)), *thru,
                   jax.ShapeDtypeStruct((8, 128), F32)),
        in_specs=[HBM_SPEC] * (2 * n) + [ANY] * len(extra),
        out_specs=(SEM_SPEC, SEM_SPEC, *([HBM_SPEC] * (2 * n)), pl.BlockSpec(memory_space=pltpu.VMEM)),
        input_output_aliases={i: 2 + i for i in range(2 * n)},
        compiler_params=pltpu.CompilerParams(has_side_effects=DATAFLOW_EFFECT),
    )(*[pltpu.with_memory_space_constraint(a, pltpu.HBM) for a in list(srcs) + lands], *extra)
    return (res[0], res[1], list(res[2:2 + n]), list(res[2 + n:2 + 2 * n]), scatter), res[-1]


def _exchange_wait(handle, after, *, name):
    send_sems, recv_sems, srcs, lands, scatter = handle
    n = len(srcs)

    def body(*refs):
        src_refs, land_refs = refs[:n], refs[n:2 * n]
        send_sems, recv_sems = refs[2 * n], refs[2 * n + 1]
        x, y, c = _my_place()
        for i in range(n):
            for r in range(N_PEERS):
                src = src_refs[i].at[0] if scatter else src_refs[i]
                cp = pltpu.make_async_remote_copy(
                    src_ref=src, dst_ref=land_refs[i].at[0], send_sem=send_sems.at[N_PEERS * i + r],
                    recv_sem=recv_sems.at[N_PEERS * i + r], device_id=(x, y, c), device_id_type=MESH)
                cp.wait_send()
                cp.wait_recv()

    thru = [pltpu.HBM(a.shape, a.dtype) for a in srcs + lands]
    res = pl.pallas_call(
        body, name=name, out_shape=tuple(thru),
        in_specs=[HBM_SPEC] * (2 * n) + [SEM_SPEC, SEM_SPEC, ANY], out_specs=tuple([HBM_SPEC] * (2 * n)),
        input_output_aliases={i: i for i in range(2 * n)},
        compiler_params=pltpu.CompilerParams(has_side_effects=DATAFLOW_EFFECT),
    )(*srcs, *lands, send_sems, recv_sems, after)
    return list(res[:n]), list(res[n:])


def _with_own(land, own, me):
    return lax.dynamic_update_index_in_dim(land, own, me, 0)


def _adamw_math(w, g, m, v):
    m = ADAM_B1 * m + (1.0 - ADAM_B1) * g
    v = ADAM_B2 * v + (1.0 - ADAM_B2) * (g * g)
    m_hat = m / (1.0 - ADAM_B1 ** ADAM_STEP)
    v_hat = v / (1.0 - ADAM_B2 ** ADAM_STEP)
    delta = -ADAM_LR * (m_hat / (jnp.sqrt(v_hat) + ADAM_EPS) + ADAM_WD * w)
    return delta, m, v


def _adamw_sum(parts, w, m, v, *, name):
    shape = w.shape
    R, n = shape[-2], shape[-1]
    w, m, v = (t.reshape(R, n) for t in (w, m, v))
    tr = _pick(R, (256, 464, 352, 128))

    def body(p_ref, w_ref, m_ref, v_ref, g_ref, d_ref, mo_ref, vo_ref):
        g = p_ref[0].astype(F32)
        for p in range(1, N_DEV):
            g = g + p_ref[p].astype(F32)
        d, mn, vn = _adamw_math(w_ref[...], g, m_ref[...], v_ref[...])
        g_ref[...] = g
        d_ref[...] = d
        mo_ref[...] = mn
        vo_ref[...] = vn

    row = pl.BlockSpec((tr, n), lambda i: (i, 0))
    outs = pl.pallas_call(
        body, name=name, grid=(R // tr,),
        in_specs=[pl.BlockSpec((N_DEV, tr, n), lambda i: (0, i, 0)), row, row, row],
        out_specs=[row, row, row, row],
        out_shape=[jax.ShapeDtypeStruct((R, n), F32)] * 4,
        compiler_params=_params("parallel"),
    )(parts, w, m, v)
    return [t.reshape(shape) for t in outs]


def _lb_bwd(dlb, lb, *, name):
    def body(d_ref, lb_ref, o_ref):
        t = d_ref[...] * lb_ref[...] * (1.0 - lb_ref[...])
        o_ref[0:1, :] = t
        o_ref[1:2, :] = -t

    return pl.pallas_call(body, name=name, out_shape=jax.ShapeDtypeStruct((2, lb.shape[1]), F32))(dlb, lb)


DOWN_BLK, ROW_BLK = D_FF // N_DEV, D_MODEL // N_DEV
W_FFN_BLK = 2 * D_FF // N_DEV
CONV_BITS_SHAPE = (16, 256)


def kernel(x, positions, norm1_g, w_in, lb_logits, hgrn_norm_g, w_a, attn_sinks, w_b, w_out, norm2_g, w_ffn_in, conv_w, conv_b, w_down, final_g, loss_target, m_norm1_g, m_w_in, m_lb_logits, m_hgrn_norm_g, m_w_a, m_attn_sinks, m_w_b, m_w_out, m_norm2_g, m_w_ffn_in, m_conv_w, m_conv_b, m_w_down, m_final_g, v_norm1_g, v_w_in, v_lb_logits, v_hgrn_norm_g, v_w_a, v_attn_sinks, v_w_b, v_w_out, v_norm2_g, v_w_ffn_in, v_conv_w, v_conv_b, v_w_down, v_final_g):
    xi, yi, ci = _my_place()
    dev = 4 * xi + 2 * yi + ci

    tr = lambda t: jnp.transpose(t[0])
    untr = lambda t: jnp.transpose(t)[None]
    w_in_blocks = _all_gather(tr(w_in).astype(BF16), name="ag_w_in")
    conv_bits = lax.bitcast_convert_type(conv_w, BF16).reshape(-1)
    conv_bits = jnp.pad(conv_bits, (0, CONV_BITS_SHAPE[0] * CONV_BITS_SHAPE[1] - conv_bits.shape[0])).reshape(CONV_BITS_SHAPE)
    w_in_full_t = w_in_blocks.reshape(W_IN, D_MODEL)
    gather_handles = {}
    gather_handles["mix"], tok_mix = _exchange_start([w_a[0].astype(BF16), w_b[0].astype(BF16), w_out[0].astype(BF16)], False,
                                                     after=w_in_full_t, name="ag_mix_start")
    gather_handles["ffn"], tok_ffn = _exchange_start([tr(w_ffn_in).astype(BF16), w_down[0].astype(BF16), conv_bits], False,
                                                     after=tok_mix, name="ag_ffn_start")
    start_token = tok_mix + tok_ffn

    def rest_weights(group, after):
        own, lands = _exchange_wait(gather_handles[group], after, name="ag_" + group + "_wait")
        full = [_with_own(l, o, dev) for l, o in zip(lands, own)]
        if group == "mix":
            return dict(zip(("w_a", "w_b", "w_out"), [t.reshape(D_MODEL, D_MODEL) for t in full]))
        bits = full[2].reshape(N_DEV, -1)[:, :3 * CONVW_BLK * 2].reshape(N_DEV, 3, CONVW_BLK, 2)
        return dict(w_ffn_t=full[0].reshape(2 * D_FF, D_MODEL), w_down=full[1].reshape(D_FF, D_MODEL),
                    conv_w=lax.bitcast_convert_type(bits, F32).transpose(1, 0, 2).reshape(3, D_FF))

    handles = {}

    def emit(group, gr):
        if group == "ffn":
            srcs = [gr["w_ffn_t"].reshape(N_DEV, W_FFN_BLK, D_MODEL), gr["w_down"].reshape(N_DEV, DOWN_BLK, D_MODEL)]
        elif group == "mix":
            srcs = [gr[n].reshape(N_DEV, ROW_BLK, D_MODEL) for n in ("w_out", "w_a", "w_b")]
        else:
            srcs = [gr["w_in_t"].reshape(N_DEV, W_IN_BLK, D_MODEL)]
        handles[group], token = _exchange_start(srcs, True, name="rs_" + group + "_start")
        return token

    small = dict(norm1_g=norm1_g, lb_logits=lb_logits, hgrn_norm_g=hgrn_norm_g, attn_sinks=attn_sinks, norm2_g=norm2_g,
                 conv_b=conv_b, final_g=final_g)
    loss, grad_x, g = _local_step(x, positions, loss_target, small, w_in_full_t, rest_weights, emit, start_token)

    def parts_of(group, after):
        srcs, lands = _exchange_wait(handles[group], after, name="rs_" + group + "_wait")
        return [_with_own(l, lax.dynamic_index_in_dim(s, dev, 0, keepdims=False), dev) for s, l in zip(srcs, lands)]

    p_ffn, p_down = parts_of("ffn", grad_x)
    p_out, p_a, p_b = parts_of("mix", grad_x)
    (p_in,) = parts_of("in", grad_x)
    big = dict(
        w_in=[untr(t) for t in _adamw_sum(p_in, tr(w_in), tr(m_w_in), tr(v_w_in), name="adamw_w_in")],
        w_a=_adamw_sum(p_a, w_a, m_w_a, v_w_a, name="adamw_w_a"),
        w_b=_adamw_sum(p_b, w_b, m_w_b, v_w_b, name="adamw_w_b"),
        w_out=_adamw_sum(p_out, w_out, m_w_out, v_w_out, name="adamw_w_out"),
        w_ffn_in=[untr(t) for t in _adamw_sum(p_ffn, tr(w_ffn_in), tr(m_w_ffn_in), tr(v_w_ffn_in), name="adamw_w_ffn_in")],
        w_down=_adamw_sum(p_down, w_down, m_w_down, v_w_down, name="adamw_w_down"),
    )

    row = lambda t: t.reshape(1, -1) if t.ndim == 1 else t
    shard = lambda t: t.reshape(3, CONVW_BLK)
    sm_g = {nm: g[nm] for nm in SMALL_ROWS}
    sm_w = dict(norm1_g=norm1_g, lb_logits=lb_logits, hgrn_norm_g=hgrn_norm_g, attn_sinks=attn_sinks, norm2_g=norm2_g,
                conv_b=conv_b, final_g=row(final_g), conv_w=shard(conv_w))
    sm_m = dict(norm1_g=m_norm1_g, lb_logits=m_lb_logits, hgrn_norm_g=m_hgrn_norm_g, attn_sinks=m_attn_sinks, norm2_g=m_norm2_g,
                conv_b=m_conv_b, final_g=row(m_final_g), conv_w=shard(m_conv_w))
    sm_v = dict(norm1_g=v_norm1_g, lb_logits=v_lb_logits, hgrn_norm_g=v_hgrn_norm_g, attn_sinks=v_attn_sinks, norm2_g=v_norm2_g,
                conv_b=v_conv_b, final_g=row(v_final_g), conv_w=shard(v_conv_w))
    loss_total, sm_out = _small_step(sm_g, g["conv_w"], loss, sm_w, sm_m, sm_v, dev.astype(jnp.int32).reshape(1), name="small_step")
    shapes = dict(final_g=final_g.shape, conv_w=conv_w.shape)

    names = ("norm1_g", "w_in", "lb_logits", "hgrn_norm_g", "w_a", "attn_sinks", "w_b", "w_out", "norm2_g", "w_ffn_in", "conv_w", "conv_b", "w_down", "final_g")
    outs = [loss_total.reshape(()), grad_x]
    for kind in range(4):
        outs += [big[n][kind] if n in big else sm_out[n][kind].reshape(shapes.get(n, sm_out[n][kind].shape)) for n in names]
    return tuple(outs)
```

```python
import jax
import jax.numpy as jnp
from jax import lax
from jax.experimental import pallas as pl
from jax.experimental.pallas import tpu as pltpu

F32 = jnp.float32
BF16 = jnp.bfloat16

D_MODEL = 1024
HGRN_HEADS = 8
HGRN_DK = 128
CHUNK = 64
ATT_HEADS = 16
ATT_KV_HEADS = 2
ATT_HD = 64
ATT_GROUP = ATT_HEADS // ATT_KV_HEADS
WINDOW = 128
ROPE_DIM = ATT_HD // 4
ROPE_THETA = 500000.0
D_FF = 2816
EPS = 1e-6
NEG_INF = -1e30
N_DEV = 8

ADAM_LR = 0.001
ADAM_B1 = 0.9
ADAM_B2 = 0.999
ADAM_EPS = 1e-08
ADAM_WD = 0.01
ADAM_STEP = 10

MESH = pl.DeviceIdType.MESH
ANY = pl.BlockSpec(memory_space=pl.ANY)


def _pick(n, cands):
    for c in cands:
        if n % c == 0:
            return c
    return n


def _sigmoid(x):
    return 0.5 * jnp.tanh(0.5 * x) + 0.5


def _silu(x):
    hx = 0.5 * x
    return hx * jnp.tanh(hx) + hx


def _rms(x, g):
    return x * lax.rsqrt(jnp.mean(x * x, axis=-1, keepdims=True) + EPS) * g


def _dot(a, b, dims):
    return lax.dot_general(a, b, (dims, ((), ())), preferred_element_type=F32)


def _nn(a, b):
    return _dot(a, b, ((1,), (0,)))


def _nt(a, b):
    return _dot(a, b, ((1,), (1,)))


def _tn(a, b):
    return _dot(a, b, ((0,), (0,)))


def _params(*sem):
    return pltpu.CompilerParams(dimension_semantics=sem, vmem_limit_bytes=56 * 1024 * 1024)


def _matmul(a, b, *, ta=False, tb=False, out_dtype=F32, addend=None, after=None, into=None, o_noff=0, out_t=False,
            o_block_perm=lambda j: j, name, tm, tn, tk=None, n_extent=None, b_koff=0, b_noff=0):
    M, K = (a.shape[1], a.shape[0]) if ta else a.shape
    N = n_extent or (b.shape[0] if tb else b.shape[1])
    tm, tn, tk = min(tm, M), min(tn, N), min(tk or K, K)
    assert M % tm == 0 and N % tn == 0 and K % tk == 0, (name, M, N, K, tm, tn, tk)
    nk = K // tk
    use_scratch = nk > 1 and out_dtype != F32
    grid = (M // tm, N // tn, nk)
    a_spec = pl.BlockSpec((tk, tm), lambda i, j, k: (k, i)) if ta else pl.BlockSpec((tm, tk), lambda i, j, k: (i, k))
    b_spec = pl.BlockSpec((tn, tk), lambda i, j, k: (j + b_noff, k + b_koff)) if tb else pl.BlockSpec((tk, tn), lambda i, j, k: (k + b_koff, j + b_noff))
    o_spec = pl.BlockSpec((tm, tn), lambda i, j, k: (i, j))
    dims = ((0 if ta else 1,), (1 if tb else 0,))
    has_add = addend is not None

    n_in = 2 + has_add + (after is not None) + (into is not None)

    def body(*refs):
        a_ref, b_ref = refs[:2]
        c_ref = refs[2] if has_add else None
        o_ref = refs[n_in]
        part = _dot(a_ref[...], b_ref[...], dims)
        if nk == 1:
            if has_add:
                part = part + c_ref[...].astype(F32)
            o_ref[...] = (part.T if out_t else part).astype(out_dtype)
        else:
            acc_ref = refs[-1] if use_scratch else o_ref
            k = pl.program_id(2)

            @pl.when(k == 0)
            def _():
                acc_ref[...] = part + c_ref[...].astype(F32) if has_add else part

            @pl.when(k > 0)
            def _():
                acc_ref[...] += part

            if use_scratch:
                @pl.when(k == nk - 1)
                def _():
                    o_ref[...] = acc_ref[...].astype(out_dtype)

    in_specs = [a_spec, b_spec] + ([o_spec] if has_add else [])
    args = (a, b) + ((addend,) if has_add else ())
    if after is not None:
        in_specs.append(pl.BlockSpec(after.shape, lambda i, j, k: (0, 0)))
        args += (after,)
    aliases = {}
    if into is not None:
        in_specs.append(ANY)
        args += (into,)
        aliases = {len(args) - 1: 0}
    if out_t:
        assert nk == 1 and not has_add
        o_spec = pl.BlockSpec((tn, tm), lambda i, j, k: (o_block_perm(j) + o_noff, i))
    elif into is not None:
        o_spec = pl.BlockSpec((tm, tn), lambda i, j, k: (i, j + o_noff))
    return pl.pallas_call(
        body,
        name=name,
        grid=grid,
        in_specs=in_specs,
        out_specs=o_spec,
        out_shape=jax.ShapeDtypeStruct(into.shape if into is not None else ((N, M) if out_t else (M, N)), out_dtype),
        input_output_aliases=aliases,
        scratch_shapes=[pltpu.VMEM((tm, tn), F32)] if use_scratch else [],
        compiler_params=_params("parallel", "parallel", "arbitrary"),
    )(*args)


def _matmul_col_tiles(a, b_t, *, tm, tn, tc, name):
    M, K = a.shape
    N = b_t.shape[0]
    tm = min(tm, M)
    per_step = tn // tc

    def body(a_ref, b_ref, o_ref):
        res = _nt(a_ref[...], b_ref[...]).astype(BF16)
        for t in range(per_step):
            o_ref[t] = res[:, t * tc:(t + 1) * tc]

    return pl.pallas_call(
        body, name=name, grid=(M // tm, N // tn),
        in_specs=[pl.BlockSpec((tm, K), lambda i, j: (i, 0)), pl.BlockSpec((tn, K), lambda i, j: (j, 0))],
        out_specs=pl.BlockSpec((per_step, tm, tc), lambda i, j: (j, i, 0)),
        out_shape=jax.ShapeDtypeStruct((N // tc, M, tc), BF16),
        compiler_params=_params("parallel", "parallel"),
    )(a, b_t)


EPILOGUE_ROWS = 256


def _matmul_ep(pairs, *, tm, ins, in_specs, out_shapes, out_specs, sums=(), epilogue, aliases=None, name):
    M = pairs[0][0].shape[0]
    tm = min(tm, M)
    mm_specs, mm_args, dims = [], [], []
    for a, b, tb, koff in pairs:
        K = a.shape[1]
        N = b.shape[0] if tb else b.shape[1]
        mm_specs += [pl.BlockSpec((tm, K), lambda i: (i, 0)),
                     pl.BlockSpec((N, K), lambda i, koff=koff: (0, koff)) if tb else pl.BlockSpec((K, N), lambda i, koff=koff: (koff, 0))]
        mm_args += [a, b]
        dims.append(((1,), (1 if tb else 0,)))
    n_mm = len(mm_args)
    n_in = n_mm + len(ins)
    rows = min(EPILOGUE_ROWS, tm)

    def body(*refs):
        in_refs, out_refs = refs[n_mm:n_in], refs[n_in:]

        def products(s):
            return [_dot(refs[2 * p][s * rows:(s + 1) * rows, :], refs[2 * p + 1][...], dims[p]) for p in range(len(pairs))]

        totals = {}
        accs = products(0)
        for s in range(tm // rows):
            ahead = products(s + 1) if (s + 1) * rows < tm else None
            outs = epilogue(*accs, *[r.at[pl.ds(s * rows, rows)] if r.shape[0] == tm else r for r in in_refs])
            for k, (ref, val) in enumerate(zip(out_refs, outs)):
                if val is None:
                    continue
                if k in sums:
                    totals[k] = val if s == 0 else totals[k] + val
                else:
                    ref[s * rows:(s + 1) * rows, :] = val.astype(ref.dtype)
            accs = ahead
        for k, total in totals.items():
            ref = out_refs[k]

            @pl.when(pl.program_id(0) == 0)
            def _():
                ref[...] = jnp.zeros_like(ref)

            ref[...] += total

    return pl.pallas_call(
        body, name=name, grid=(M // tm,),
        in_specs=mm_specs + list(in_specs),
        out_specs=list(out_specs), out_shape=list(out_shapes),
        input_output_aliases={n_mm + k: v for k, v in (aliases or {}).items()},
        compiler_params=_params("arbitrary"),
    )(*mm_args, *ins)


def _row_spec(tm, n):
    return pl.BlockSpec((tm, n), lambda i: (i, 0))


def _full_spec(shape):
    return pl.BlockSpec(shape, lambda i: tuple(0 for _ in shape))


MAX_DOT_COLS = 2048


def _resident_spec(shape):
    return pl.BlockSpec(shape, lambda i: tuple(0 for _ in shape), pipeline_mode=pl.Buffered(1))


def _norm_matmul(x, g, w_t, segments, *, tm, name):
    T, D = x.shape
    N = w_t.shape[0]
    tm = min(tm, T)
    chunks = [(c + o, r + o, min(MAX_DOT_COLS, n - o)) for c, r, n in segments for o in range(0, n, MAX_DOT_COLS)]

    def body(x_ref, g_ref, w_ref, u_ref, z_ref):
        u = _rms(x_ref[...], g_ref[...]).astype(BF16)
        u_ref[...] = u
        for c, r, n in chunks:
            z_ref[:, c:c + n] = _nt(u, w_ref[r:r + n, :]).astype(BF16)

    return pl.pallas_call(
        body, name=name, grid=(T // tm,),
        in_specs=[_row_spec(tm, D), _full_spec((1, D)), _resident_spec((N, D))],
        out_specs=[_row_spec(tm, D), _row_spec(tm, N)],
        out_shape=[jax.ShapeDtypeStruct((T, D), BF16), jax.ShapeDtypeStruct((T, N), BF16)],
        compiler_params=_params("parallel"),
    )(x, g, w_t)


def _matmul_norm_bwd(dz, w_t, segments, x, g, dres, after, *, tm, name):
    T, K = dz.shape
    D = w_t.shape[1]
    tm = min(tm, T)

    def body(dz_ref, w_ref, x_ref, g_ref, dr_ref, after_ref, dx_ref, dg_ref):
        @pl.when(pl.program_id(0) == 0)
        def _():
            dg_ref[...] = jnp.zeros_like(dg_ref)

        du = sum(_nn(dz_ref[:, c:c + n], w_ref[r:r + n, :]) for c, r, n in segments)
        _, vjp = jax.vjp(_rms, x_ref[...], g_ref[...])
        dx, dg = vjp(du)
        dx_ref[...] = dx + dr_ref[...]
        dg_ref[...] += dg

    row = _row_spec(tm, D)
    return pl.pallas_call(
        body, name=name, grid=(T // tm,),
        in_specs=[_row_spec(tm, K), _resident_spec((K, D)), row, _full_spec((1, D)), row, _full_spec(after.shape)],
        out_specs=[row, _full_spec((1, D))],
        out_shape=[jax.ShapeDtypeStruct((T, D), F32), jax.ShapeDtypeStruct((1, D), F32)],
        compiler_params=_params("arbitrary"),
    )(dz, w_t, x, g, dres, after)


def _merge_fn(gates, a, b):
    ga = gates[:, :D_MODEL].astype(F32)
    gb = gates[:, D_MODEL:].astype(F32)
    return _sigmoid(ga) * a.astype(F32) + _sigmoid(gb) * b.astype(F32)


def _gates_spec(tm):
    return pl.BlockSpec((tm, W_GATES), lambda i: (i, O_GATES // W_GATES))


CONV_TC = 256
INPUT_SLOTS = 3


def _shift_down(x, n, rows):
    return jnp.where(rows >= n, pltpu.roll(x, n, 0), 0.0)


def _shift_up(x, n, rows, S):
    return jnp.where(rows < S - n, pltpu.roll(x, S - n, 0), 0.0)


def _conv_act_fwd(gu, conv_w, conv_b, *, name):
    _, B, S, tc = gu.shape
    nc = D_FF // tc

    n_steps = B * nc

    def body(gu_ref, w_ref, b_ref, o_ref, a_ref, g_buf, up_buf, sems):
        t = pl.program_id(0) * nc + pl.program_id(1)

        def fetch(step):
            bb, jj, slot = step // nc, step % nc, step % INPUT_SLOTS
            return (pltpu.make_async_copy(gu_ref.at[jj, bb], g_buf.at[slot], sems.at[0, slot]),
                    pltpu.make_async_copy(gu_ref.at[jj + nc, bb], up_buf.at[slot], sems.at[1, slot]))

        @pl.when(t == 0)
        def _():
            for step in range(INPUT_SLOTS - 1):
                for cp in fetch(step):
                    cp.start()

        @pl.when(t + INPUT_SLOTS - 1 < n_steps)
        def _():
            for cp in fetch(t + INPUT_SLOTS - 1):
                cp.start()

        for cp in fetch(t):
            cp.wait()
        slot = t % INPUT_SLOTS
        g = g_buf[slot].astype(F32)
        rows = lax.broadcasted_iota(jnp.int32, g.shape, 0)
        w = w_ref[...]
        a = w[2:3] * g + w[1:2] * _shift_down(g, 1, rows) + w[0:1] * _shift_down(g, 2, rows) + b_ref[...]
        o_ref[...] = (_silu(a) * up_buf[slot].astype(F32)).astype(BF16)
        a_ref[...] = a.astype(BF16)

    col = pl.BlockSpec((None, S, tc), lambda b, j: (b, 0, j))
    tile = lambda off: pl.BlockSpec((None, None, S, tc), lambda b, j: (j + off, b, 0, 0))
    return pl.pallas_call(
        body, name=name, grid=(B, nc),
        in_specs=[ANY,
                  pl.BlockSpec((3, tc), lambda b, j: (0, j)),
                  pl.BlockSpec((1, tc), lambda b, j: (0, j))],
        out_specs=[col, tile(0)],
        out_shape=[jax.ShapeDtypeStruct((B, S, D_FF), BF16), jax.ShapeDtypeStruct((nc, B, S, tc), BF16)],
        scratch_shapes=[pltpu.VMEM((INPUT_SLOTS, S, tc), BF16), pltpu.VMEM((INPUT_SLOTS, S, tc), BF16),
                        pltpu.SemaphoreType.DMA((2, INPUT_SLOTS))],
        compiler_params=_params("arbitrary", "arbitrary"),
    )(gu, conv_w, conv_b)


def _conv_act_bwd(gu, a_pre, conv_w, dact, *, name):
    _, B, S, tc = gu.shape
    nc = D_FF // tc

    n_steps = nc * B

    def body(gu_ref, apre_ref, w_ref, da_ref, dg_ref, dup_ref, dw_ref, db_ref, g_buf, up_buf, a_buf, sems):
        t = pl.program_id(0) * B + pl.program_id(1)

        def fetch(step):
            jj, bb, slot = step // B, step % B, step % INPUT_SLOTS
            return (pltpu.make_async_copy(gu_ref.at[jj, bb], g_buf.at[slot], sems.at[0, slot]),
                    pltpu.make_async_copy(gu_ref.at[jj + nc, bb], up_buf.at[slot], sems.at[1, slot]),
                    pltpu.make_async_copy(apre_ref.at[jj, bb], a_buf.at[slot], sems.at[2, slot]))

        @pl.when(t == 0)
        def _():
            for step in range(INPUT_SLOTS - 1):
                for cp in fetch(step):
                    cp.start()

        @pl.when(t + INPUT_SLOTS - 1 < n_steps)
        def _():
            for cp in fetch(t + INPUT_SLOTS - 1):
                cp.start()

        for cp in fetch(t):
            cp.wait()
        slot = t % INPUT_SLOTS
        g = g_buf[slot].astype(F32)
        up, a, dact = up_buf[slot], a_buf[slot], da_ref[...]
        rows = lax.broadcasted_iota(jnp.int32, g.shape, 0)
        w = w_ref[...]
        sg = _sigmoid(a)
        dup_ref[...] = dact * a * sg
        da = (dact * up * sg * (1.0 + a * (1.0 - sg))).astype(F32)
        da1 = _shift_up(da, 1, rows, S)
        da2 = _shift_up(da, 2, rows, S)
        dg_ref[...] = (w[2:3] * da + w[1:2] * da1 + w[0:1] * da2).astype(BF16)

        @pl.when(pl.program_id(1) == 0)
        def _():
            dw_ref[...] = jnp.zeros_like(dw_ref)
            db_ref[...] = jnp.zeros_like(db_ref)

        dw_ref[0:1, :] += jnp.sum(da2 * g, axis=0, keepdims=True)
        dw_ref[1:2, :] += jnp.sum(da1 * g, axis=0, keepdims=True)
        dw_ref[2:3, :] += jnp.sum(da * g, axis=0, keepdims=True)
        db_ref[...] += jnp.sum(da, axis=0, keepdims=True)

    col = pl.BlockSpec((None, S, tc), lambda j, b: (b, 0, j))
    tile = lambda off: pl.BlockSpec((None, None, S, tc), lambda j, b: (j + off, b, 0, 0))
    return pl.pallas_call(
        body, name=name, grid=(nc, B),
        in_specs=[ANY, ANY,
                  pl.BlockSpec((3, tc), lambda j, b: (0, j)),
                  col],
        out_specs=[col, col, pl.BlockSpec((3, tc), lambda j, b: (0, j)), pl.BlockSpec((1, tc), lambda j, b: (0, j))],
        out_shape=[jax.ShapeDtypeStruct((B, S, D_FF), BF16), jax.ShapeDtypeStruct((B, S, D_FF), BF16),
                   jax.ShapeDtypeStruct((3, D_FF), F32), jax.ShapeDtypeStruct((1, D_FF), F32)],
        scratch_shapes=[pltpu.VMEM((INPUT_SLOTS, S, tc), BF16)] * 3 + [pltpu.SemaphoreType.DMA((3, INPUT_SLOTS))],
        compiler_params=_params("arbitrary", "arbitrary"),
    )(gu, a_pre, conv_w, dact)


HGRN_CPB = 8
HF = HGRN_HEADS * HGRN_DK


def _tri(n, upper=False):
    r = lax.broadcasted_iota(jnp.int32, (n, n), 0)
    c = lax.broadcasted_iota(jnp.int32, (n, n), 1)
    return (c >= r) if upper else (r >= c)


def _hs(h):
    return slice(h * HGRN_DK, (h + 1) * HGRN_DK)


def _cumsum_rows(tri_b, x):
    hi = x.astype(BF16)
    lo = (x - hi.astype(F32)).astype(BF16)
    return _nn(tri_b, hi) + _nn(tri_b, lo)


def _hgrn_col(seg, h):
    return slice(seg * HF + h * HGRN_DK, seg * HF + (h + 1) * HGRN_DK)


def _hgrn_gates(q, fz, lb):
    sg = _sigmoid(fz)
    return _sigmoid(q), sg, lb + (1.0 - lb) * sg


def _hgrn_decays(b, q, sq, f):
    qf = q * sq
    k = 1.0 - f
    bref = b[CHUNK // 2:CHUNK // 2 + 1, :]
    blast = b[CHUNK - 1:CHUNK, :]
    e1 = jnp.exp2(b - bref)
    e2 = jnp.exp2(bref - b)
    e3 = e1 * jnp.exp2(bref)
    e4 = e2 * jnp.exp2(blast - bref)
    return (e1, e2, e3, e4), qf * e1, k * e2, qf * e3, k * e4, jnp.exp2(blast)


def _hgrn_fwd(zh, lb, gn, *, name):
    B, S, _ = zh.shape
    cpb = HGRN_CPB
    ts = cpb * CHUNK
    nblk = S // ts

    def body(z_ref, lb_ref, gn_ref, o_ref, st_ref, state):
        @pl.when(pl.program_id(1) == 0)
        def _():
            state[...] = jnp.zeros_like(state)

        R = range(HGRN_HEADS)
        causal = _tri(CHUNK)
        tril_b = causal.astype(BF16)
        lbh = [lb_ref[:, _hs(h)] for h in R]
        for c in range(cpb):
            rows = slice(c * CHUNK, (c + 1) * CHUNK)
            q = [z_ref[rows, _hgrn_col(0, h)].astype(F32) for h in R]
            gates = [_hgrn_gates(q[h], z_ref[rows, _hgrn_col(1, h)].astype(F32), lbh[h]) for h in R]
            b = [_cumsum_rows(tril_b, jnp.log2(gates[h][2])) for h in R]
            v = [z_ref[rows, _hgrn_col(2, h)] for h in R]
            dec, q_in, k_in, q_out, k_st = [], [], [], [], []
            for h in R:
                _, qi, ki, qo, ks, d = _hgrn_decays(b[h], q[h], gates[h][0], gates[h][2])
                dec.append(d)
                for lst, t in zip((q_in, k_in, q_out, k_st), (qi, ki, qo, ks)):
                    lst.append(t.astype(BF16))
            a = [jnp.where(causal, _nt(q_in[h], k_in[h]), 0.0).astype(BF16) for h in R]
            st = [state[h] for h in R]
            for h in R:
                st_ref[c, h] = st[h]
            o = [_nn(a[h], v[h]) + _nt(q_out[h], st[h].astype(BF16)) for h in R]
            for h in R:
                state[h] = st[h] * dec[h] + _tn(v[h], k_st[h])
            for h in R:
                o_ref[rows, _hs(h)] = (_rms(o[h], gn_ref[...]) * _silu(z_ref[rows, _hgrn_col(3, h)].astype(F32))).astype(BF16)

    return pl.pallas_call(
        body, name=name, grid=(B, nblk),
        in_specs=[pl.BlockSpec((None, ts, 4 * HF), lambda b, s: (b, s, 0)),
                  pl.BlockSpec((1, HF), lambda b, s: (0, 0)),
                  pl.BlockSpec((1, HGRN_DK), lambda b, s: (0, 0))],
        out_specs=[pl.BlockSpec((None, ts, HF), lambda b, s: (b, s, 0)),
                   pl.BlockSpec((None, cpb, HGRN_HEADS, HGRN_DK, HGRN_DK), lambda b, s: (b, s, 0, 0, 0))],
        out_shape=[jax.ShapeDtypeStruct((B, S, HF), BF16),
                   jax.ShapeDtypeStruct((B, S // CHUNK, HGRN_HEADS, HGRN_DK, HGRN_DK), F32)],
        scratch_shapes=[pltpu.VMEM((HGRN_HEADS, HGRN_DK, HGRN_DK), F32)],
        compiler_params=_params("arbitrary", "arbitrary"),
    )(zh, lb, gn)


def _hgrn_bwd(zh, lb, gn, states, doa, dz, *, name):
    B, S, _ = zh.shape
    cpb = HGRN_CPB
    ts = cpb * CHUNK
    nblk = S // ts
    rev = lambda b, s: (b, nblk - 1 - s, 0)

    def body(z_ref, lb_ref, gn_ref, st_ref, do_ref, dz_in, dz_ref, dlb_ref, dgn_ref, dstate):
        @pl.when(pl.program_id(1) == 0)
        def _():
            dstate[...] = jnp.zeros_like(dstate)

        @pl.when((pl.program_id(0) == 0) & (pl.program_id(1) == 0))
        def _():
            dlb_ref[...] = jnp.zeros_like(dlb_ref)
            dgn_ref[...] = jnp.zeros_like(dgn_ref)

        R = range(HGRN_HEADS)
        causal = _tri(CHUNK)
        tril_b = causal.astype(BF16)
        triu_b = _tri(CHUNK, upper=True).astype(BF16)
        rowid = lax.broadcasted_iota(jnp.int32, (CHUNK, HGRN_DK), 0)
        lbh = [lb_ref[:, _hs(h)] for h in R]
        gn = gn_ref[...]
        for c in reversed(range(cpb)):
            rows = slice(c * CHUNK, (c + 1) * CHUNK)
            q = [z_ref[rows, _hgrn_col(0, h)].astype(F32) for h in R]
            gates = [_hgrn_gates(q[h], z_ref[rows, _hgrn_col(1, h)].astype(F32), lbh[h]) for h in R]
            b = [_cumsum_rows(tril_b, jnp.log2(gates[h][2])) for h in R]
            v = [z_ref[rows, _hgrn_col(2, h)] for h in R]
            pre = [_hgrn_decays(b[h], q[h], gates[h][0], gates[h][2]) for h in R]
            q_in_b, k_in_b, q_out_b, k_st_b = ([pre[h][i].astype(BF16) for h in R] for i in (1, 2, 3, 4))
            a_b = [jnp.where(causal, _nt(q_in_b[h], k_in_b[h]), 0.0).astype(BF16) for h in R]
            st = [st_ref[c, h] for h in R]
            st_b = [t.astype(BF16) for t in st]
            o = [_nn(a_b[h], v[h]) + _nt(q_out_b[h], st_b[h]) for h in R]
            do_l, dgn_acc = [], jnp.zeros_like(gn)
            for h in R:
                hg = z_ref[rows, _hgrn_col(3, h)].astype(F32)
                dout = do_ref[rows, _hs(h)].astype(F32)
                shg = _sigmoid(hg)
                on_h, norm_vjp = jax.vjp(_rms, o[h], gn)
                d_o, d_gn = norm_vjp(dout * (hg * shg))
                do_l.append(d_o)
                dgn_acc = dgn_acc + d_gn
                dz_ref[rows, _hgrn_col(3, h)] = (dout * on_h * shg * (1.0 + hg * (1.0 - shg))).astype(BF16)
            dgn_ref[...] += dgn_acc
            do_b = [t.astype(BF16) for t in do_l]
            dst = [dstate[h] for h in R]
            dst_b = [t.astype(BF16) for t in dst]
            da_b = [jnp.where(causal, _nt(do_b[h], v[h]), 0.0).astype(BF16) for h in R]
            dv = [_tn(a_b[h], do_b[h]) + _nt(k_st_b[h], dst_b[h]) for h in R]
            dq_in = [_nn(da_b[h], k_in_b[h]) for h in R]
            dk_in = [_tn(da_b[h], q_in_b[h]) for h in R]
            dq_out = [_nn(do_b[h], st_b[h]) for h in R]
            dk_st = [_nn(v[h], dst_b[h]) for h in R]
            for h in R:
                dz_ref[rows, _hgrn_col(2, h)] = dv[h].astype(BF16)
            db = []
            for h in R:
                _, q_in, k_in, q_out, k_st, dec = pre[h]
                ddec = jnp.sum(st[h] * dst[h], axis=0, keepdims=True)
                t_qin, t_kin, t_kst = dq_in[h] * q_in, dk_in[h] * k_in, dk_st[h] * k_st
                dbref = jnp.sum(t_kin - t_qin, axis=0, keepdims=True)
                dblast = jnp.sum(t_kst, axis=0, keepdims=True) + ddec * dec
                db.append(t_qin - t_kin + dq_out[h] * q_out - t_kst
                          + jnp.where(rowid == CHUNK // 2, dbref, 0.0) + jnp.where(rowid == CHUNK - 1, dblast, 0.0))
            for h in R:
                dstate[h] = dst[h] * pre[h][5] + _tn(do_b[h], q_out_b[h])
            dlogf = [_cumsum_rows(triu_b, db[h]) for h in R]
            for h in R:
                sq, sg, f = gates[h]
                e1, e2, e3, e4 = pre[h][0]
                dqf = dq_in[h] * e1 + dq_out[h] * e3
                dk = dk_in[h] * e2 + dk_st[h] * e4
                df_open = (dlogf[h] / f - dk) * (1.0 - sg)
                dlb_ref[:, _hs(h)] += jnp.sum(df_open, axis=0, keepdims=True)
                dz_ref[rows, _hgrn_col(1, h)] = (df_open * ((1.0 - lbh[h]) * sg)).astype(BF16)
                dz_ref[rows, _hgrn_col(0, h)] = (dqf * sq * (1.0 + q[h] * (1.0 - sq))).astype(BF16)

    return pl.pallas_call(
        body, name=name, grid=(B, nblk),
        in_specs=[pl.BlockSpec((None, ts, 4 * HF), rev),
                  pl.BlockSpec((1, HF), lambda b, s: (0, 0)),
                  pl.BlockSpec((1, HGRN_DK), lambda b, s: (0, 0)),
                  pl.BlockSpec((None, cpb, HGRN_HEADS, HGRN_DK, HGRN_DK), lambda b, s: (b, nblk - 1 - s, 0, 0, 0)),
                  pl.BlockSpec((None, ts, HF), rev),
                  ANY],
        out_specs=[pl.BlockSpec((None, ts, 4 * HF), rev),
                   pl.BlockSpec((1, HF), lambda b, s: (0, 0)),
                   pl.BlockSpec((1, HGRN_DK), lambda b, s: (0, 0))],
        out_shape=[jax.ShapeDtypeStruct(dz.shape, BF16),
                   jax.ShapeDtypeStruct((1, HF), F32),
                   jax.ShapeDtypeStruct((1, HGRN_DK), F32)],
        input_output_aliases={5: 0},
        scratch_shapes=[pltpu.VMEM((HGRN_HEADS, HGRN_DK, HGRN_DK), F32)],
        compiler_params=_params("arbitrary", "arbitrary"),
    )(zh, lb, gn, states, doa, dz)


KV_W = ATT_KV_HEADS * ATT_HD
ATT_SCALE = ATT_HD ** -0.5


def _rope(x, cos, sin, inverse=False):
    half = ROPE_DIM // 2
    outs = []
    for p in range(x.shape[1] // 128):
        xp = x[:, p * 128:(p + 1) * 128]
        lane = lax.broadcasted_iota(jnp.int32, xp.shape, 1) % ATT_HD
        sw = jnp.where(lane < half, pltpu.roll(xp, 128 - half, 1), pltpu.roll(xp, half, 1))
        outs.append(xp * cos - sw * sin if inverse else xp * cos + sw * sin)
    return outs[0] if len(outs) == 1 else jnp.concatenate(outs, axis=1)


PAIRS_PER_KV = ATT_GROUP // 2


def _swap_halves(x):
    return pltpu.roll(x, ATT_HD, 1)


def _kv_padded(t, low):
    sw = _swap_halves(t)
    zero = jnp.zeros_like(t)
    out = []
    for g in range(ATT_KV_HEADS):
        in_low, in_high = (t, sw) if g == 0 else (sw, t)
        out.append((jnp.where(low, in_low, zero).astype(BF16), jnp.where(low, zero, in_high).astype(BF16)))
    return out


def _swa_mask(first_block):
    qi = lax.broadcasted_iota(jnp.int32, (WINDOW, 2 * WINDOW), 0)
    mi = lax.broadcasted_iota(jnp.int32, (WINDOW, 2 * WINDOW), 1)
    band = (mi > qi) & (mi <= qi + WINDOW)
    return band & (jnp.logical_not(first_block) | (mi >= WINDOW))


def _swa_specs(nb):
    cur = lambda b, i: (b, i, 0)
    prev = lambda b, i: (b, jnp.maximum(i - 1, 0), 0)
    return cur, prev


def _swa_z_specs():
    q = pl.BlockSpec((None, WINDOW, W_AQ), lambda b, i: (b, i, O_AQ // W_AQ))
    kv_prev = pl.BlockSpec((None, WINDOW, W_AKV), lambda b, i: (b, jnp.maximum(i - 1, 0), O_AKV // W_AKV))
    kv_cur = pl.BlockSpec((None, WINDOW, W_AKV), lambda b, i: (b, i, O_AKV // W_AKV))
    return q, kv_prev, kv_cur


def _swa_fwd(z, cos, sin, sinks, *, name):
    B, S, _ = z.shape
    nb = S // WINDOW
    cur, prev = _swa_specs(nb)

    def body(q_ref, kvp_ref, kvc_ref, cp_ref, sp_ref, cc_ref, sc_ref, sink_ref, o_ref, lse_ref, qr_ref, kr_ref):
        cos_c, sin_c = cc_ref[...], sc_ref[...]
        q = (_rope(q_ref[...].astype(F32), cos_c, sin_c) * ATT_SCALE).astype(BF16)
        k = jnp.concatenate([_rope(kvp_ref[:, :KV_W].astype(F32), cp_ref[...], sp_ref[...]),
                             _rope(kvc_ref[:, :KV_W].astype(F32), cos_c, sin_c)], axis=0)
        qr_ref[...] = q
        kr_ref[...] = k[WINDOW:].astype(BF16)
        v = jnp.concatenate([kvp_ref[:, KV_W:], kvc_ref[:, KV_W:]], axis=0).astype(F32)
        low = lax.broadcasted_iota(jnp.int32, k.shape, 1) < ATT_HD
        kpad = _kv_padded(k, low)
        vpad = _kv_padded(v, low)
        mask = _swa_mask(pl.program_id(1) == 0)
        lses = []
        for g in range(ATT_KV_HEADS):
            pairs = range(g * PAIRS_PER_KV, (g + 1) * PAIRS_PER_KV)
            keys = [(p, e) for p in pairs for e in (0, 1)]
            qp = {p: q[:, p * 128:(p + 1) * 128] for p in pairs}
            s = {pe: jnp.where(mask, _nt(qp[pe[0]], kpad[g][pe[1]]), NEG_INF) for pe in keys}
            pr = {}
            for pe in keys:
                sink = sink_ref[0, 2 * pe[0] + pe[1]]
                m = jnp.maximum(jnp.max(s[pe], axis=1, keepdims=True), sink)
                ex = jnp.exp(s[pe] - m)
                den = jnp.sum(ex, axis=1, keepdims=True) + jnp.exp(sink - m)
                pr[pe] = (ex * (1.0 / den)).astype(BF16)
                lses.append(m + jnp.log(den))
            for p in pairs:
                o_ref[:, p * 128:(p + 1) * 128] = (_nn(pr[p, 0], vpad[g][0]) + _nn(pr[p, 1], vpad[g][1])).astype(BF16)
        lse_ref[...] = jnp.concatenate(lses, axis=1)

    tab = lambda im: pl.BlockSpec((None, WINDOW, 128), im)
    return pl.pallas_call(
        body, name=name, grid=(B, nb),
        in_specs=[*_swa_z_specs(),
                  tab(prev), tab(prev), tab(cur), tab(cur),
                  pl.BlockSpec(memory_space=pltpu.SMEM)],
        out_specs=[pl.BlockSpec((None, WINDOW, D_MODEL), cur), pl.BlockSpec((None, WINDOW, ATT_HEADS), cur),
                   pl.BlockSpec((None, WINDOW, D_MODEL), cur), pl.BlockSpec((None, WINDOW, KV_W), cur)],
        out_shape=[jax.ShapeDtypeStruct((B, S, D_MODEL), BF16), jax.ShapeDtypeStruct((B, S, ATT_HEADS), F32),
                   jax.ShapeDtypeStruct((B, S, D_MODEL), BF16), jax.ShapeDtypeStruct((B, S, KV_W), BF16)],
        compiler_params=_params("parallel", "parallel"),
    )(z, z, z, cos, sin, cos, sin, sinks)


def _swa_bwd(z, qr, kr, cos, sin, sinks, lse, dob, dz, *, name):
    B, S, _ = z.shape
    nb = S // WINDOW
    cur, prev = _swa_specs(nb)

    def body(q_ref, krp_ref, krc_ref, kvp_ref, kvc_ref, cp_ref, sp_ref, cc_ref, sc_ref, sink_ref, lse_ref, do_ref, dz_in,
             dq_ref, dkc_ref, dkp_ref, dsink_ref):
        @pl.when((pl.program_id(0) == 0) & (pl.program_id(1) == 0))
        def _():
            dsink_ref[...] = jnp.zeros_like(dsink_ref)

        cos_c, sin_c, cos_p, sin_p = cc_ref[...], sc_ref[...], cp_ref[...], sp_ref[...]
        q = q_ref[...]
        k = jnp.concatenate([krp_ref[...], krc_ref[...]], axis=0).astype(F32)
        v = jnp.concatenate([kvp_ref[:, KV_W:], kvc_ref[:, KV_W:]], axis=0).astype(F32)
        low = lax.broadcasted_iota(jnp.int32, k.shape, 1) < ATT_HD
        kpad = _kv_padded(k, low)
        vpad = _kv_padded(v, low)
        mask = _swa_mask(pl.program_id(1) == 0)
        lse = lse_ref[...]
        dq_parts, dk_sum, dv_sum, dsinks = [], [], [], []
        for g in range(ATT_KV_HEADS):
            pairs = range(g * PAIRS_PER_KV, (g + 1) * PAIRS_PER_KV)
            keys = [(p, e) for p in pairs for e in (0, 1)]
            qp = {p: q[:, p * 128:(p + 1) * 128] for p in pairs}
            dop = {p: do_ref[:, p * 128:(p + 1) * 128] for p in pairs}
            s = {pe: jnp.where(mask, _nt(qp[pe[0]], kpad[g][pe[1]]), NEG_INF) for pe in keys}
            dp = {pe: _nt(dop[pe[0]], vpad[g][pe[1]]) for pe in keys}
            pr, ds = {}, {}
            for pe in keys:
                h = 2 * pe[0] + pe[1]
                lse_h = lse[:, h:h + 1]
                pf = jnp.exp(s[pe] - lse_h)
                delta = jnp.sum(pf * dp[pe], axis=1, keepdims=True)
                ds[pe] = (pf * (dp[pe] - delta)).astype(BF16)
                pr[pe] = pf.astype(BF16)
                p_sink = jnp.exp(sink_ref[0, h] - lse_h)
                dsinks.append(-jnp.sum(p_sink * delta, axis=0, keepdims=True))
            for p in pairs:
                dq_parts.append((_nn(ds[p, 0], kpad[g][0]) + _nn(ds[p, 1], kpad[g][1])) * ATT_SCALE)
            x = [sum(_tn(ds[p, e], qp[p]) for p in pairs) for e in (0, 1)]
            y = [sum(_tn(pr[p, e], dop[p]) for p in pairs) for e in (0, 1)]
            zk = jnp.where(low, x[0], x[1])
            zv = jnp.where(low, y[0], y[1])
            dk_sum.append(zk + _swap_halves(zk))
            dv_sum.append(zv + _swap_halves(zv))
        dq_ref[...] = _rope(jnp.concatenate(dq_parts, axis=1), cos_c, sin_c, inverse=True).astype(BF16)
        dk = jnp.where(low, dk_sum[0], dk_sum[1])
        dv = jnp.where(low, dv_sum[0], dv_sum[1])
        dkp_ref[:, :KV_W] = _rope(dk[:WINDOW], cos_p, sin_p, inverse=True)
        dkp_ref[:, KV_W:] = dv[:WINDOW]
        dkc_ref[:, :KV_W] = _rope(dk[WINDOW:], cos_c, sin_c, inverse=True)
        dkc_ref[:, KV_W:] = dv[WINDOW:]
        dsink_ref[...] += jnp.concatenate(dsinks, axis=1)

    tab = lambda im: pl.BlockSpec((None, WINDOW, 128), im)
    return pl.pallas_call(
        body, name=name, grid=(B, nb),
        in_specs=[pl.BlockSpec((None, WINDOW, D_MODEL), cur), tab(prev), tab(cur),
                  *_swa_z_specs()[1:],
                  tab(prev), tab(prev), tab(cur), tab(cur),
                  pl.BlockSpec(memory_space=pltpu.SMEM),
                  pl.BlockSpec((None, WINDOW, ATT_HEADS), cur),
                  pl.BlockSpec((None, WINDOW, D_MODEL), cur),
                  ANY],
        out_specs=[_swa_z_specs()[0],
                   pl.BlockSpec((None, WINDOW, 2 * KV_W), cur), pl.BlockSpec((None, WINDOW, 2 * KV_W), cur),
                   pl.BlockSpec((1, ATT_HEADS), lambda b, i: (0, 0))],
        out_shape=[jax.ShapeDtypeStruct(dz.shape, BF16),
                   jax.ShapeDtypeStruct((B, S, 2 * KV_W), F32), jax.ShapeDtypeStruct((B, S, 2 * KV_W), F32),
                   jax.ShapeDtypeStruct((1, ATT_HEADS), F32)],
        input_output_aliases={12: 0},
        compiler_params=_params("arbitrary", "arbitrary"),
    )(qr, kr, kr, z, z, cos, sin, cos, sin, sinks, lse, dob, dz)


def _swa_dkv_combine(dkv_cur, dkv_prev, dz, *, name):
    B, S, W = dkv_cur.shape

    def body(c_ref, p_ref, dz_in, o_ref):
        rows = lax.broadcasted_iota(jnp.int32, (S, W), 0)
        o_ref[...] = (c_ref[...] + _shift_up(p_ref[...], WINDOW, rows, S)).astype(BF16)

    seq = pl.BlockSpec((None, S, W), lambda b: (b, 0, 0))
    return pl.pallas_call(
        body, name=name, grid=(B,),
        in_specs=[seq, seq, ANY], out_specs=pl.BlockSpec((None, S, W), lambda b: (b, 0, O_AKV // W_AKV)),
        out_shape=jax.ShapeDtypeStruct(dz.shape, BF16),
        input_output_aliases={2: 0},
        compiler_params=_params("parallel"),
    )(dkv_cur, dkv_prev, dz)


def _rope_tables(positions):
    half = ROPE_DIM // 2
    inv = ROPE_THETA ** (-2.0 * jnp.arange(half, dtype=F32) / ROPE_DIM)
    ang = positions.astype(F32)[..., None] * inv
    c, s = jnp.cos(ang), jnp.sin(ang)
    pad = jnp.zeros(ang.shape[:-1] + (ATT_HD - ROPE_DIM,), F32)
    cos = jnp.concatenate([c, c, pad + 1.0], axis=-1)
    sin = jnp.concatenate([-s, s, pad], axis=-1)
    return jnp.tile(cos, (1, 1, 2)), jnp.tile(sin, (1, 1, 2))


def _lower_bound(lb_logits, *, name):
    def body(l_ref, o_ref):
        l = l_ref[...]
        e = jnp.exp(l - jnp.max(l, axis=0, keepdims=True))
        o_ref[...] = e[0:1] / jnp.sum(e, axis=0, keepdims=True)

    return pl.pallas_call(body, name=name, out_shape=jax.ShapeDtypeStruct((1, lb_logits.shape[1]), F32))(lb_logits)


W_ZH, W_GATES, W_AQ, W_AKV = 4 * HF, 2 * D_MODEL, ATT_HEADS * ATT_HD, 2 * KV_W
O_ZH, O_GATES, O_AQ, O_AKV = 0, W_ZH, W_ZH + W_GATES, W_ZH + W_GATES + W_AQ
W_IN = W_ZH + W_GATES + W_AQ + W_AKV


W_IN_BLK = W_IN // N_DEV


def _reference_row_block(j, rows=256):
    nz, ng = W_ZH // rows, W_GATES // rows
    return jnp.where(j < nz, j, jnp.where(j < nz + ng, j + (W_AQ + W_AKV) // rows, j - ng))


W_IN_SEGMENTS = ((O_ZH, 0, W_ZH), (O_GATES, W_ZH + W_AQ + W_AKV, W_GATES), (O_AQ, W_ZH, W_AQ + W_AKV))


def _local_step(x, positions, target, small, w_in_t, rest_weights, emit, start_token):
    B, S, D = x.shape
    T = B * S
    x2 = x.reshape(T, D)
    cos, sin = _rope_tables(positions)
    lb = _lower_bound(small["lb_logits"], name="lb_fwd")
    zero = lambda tok: tok[0:1, 0:1]

    u1, z = _norm_matmul(x2, small["norm1_g"] + zero(start_token), w_in_t, W_IN_SEGMENTS, tm=512, name="norm1_mm_z")
    z3 = z.reshape(B, S, W_IN)
    oa, states = _hgrn_fwd(z3, lb, small["hgrn_norm_g"], name="hgrn_fwd")
    ob, lse, qr, kr = _swa_fwd(z3, cos, sin, small["attn_sinks"], name="swa_fwd")
    oa2 = oa.reshape(T, D)
    ob2 = ob.reshape(T, D)
    W = rest_weights("mix", ob)
    row = lambda tm, dtype=None: _row_spec(tm, D)
    tile = lambda dtype: jax.ShapeDtypeStruct((T, D), dtype)
    vec = _full_spec((1, D))
    vec_shape = jax.ShapeDtypeStruct((1, D), F32)

    def merge_ep(acc_a, acc_b, g_ref):
        pa, pb = acc_a.astype(BF16), acc_b.astype(BF16)
        return pa, pb, _merge_fn(g_ref[...], pa, pb)

    pa, pb, merged = _matmul_ep([(oa2, W["w_a"], False, 0), (ob2, W["w_b"], False, 0)], tm=1024, ins=[z], in_specs=[_gates_spec(1024)],
                                out_shapes=[tile(BF16)] * 3, out_specs=[row(1024)] * 3, epilogue=merge_ep, name="mm_pa_pb_merge")

    def resid_norm_ep(acc, x_ref, g_ref):
        hh = acc + x_ref[...]
        return hh, _rms(hh, g_ref[...])

    h, u2 = _matmul_ep([(merged, W["w_out"], False, 0)], tm=1024, ins=[x2, small["norm2_g"]], in_specs=[row(1024), vec],
                       out_shapes=[tile(F32), tile(BF16)], out_specs=[row(1024), row(1024)], epilogue=resid_norm_ep, name="mm_h_norm2")
    W.update(rest_weights("ffn", u2))
    gu3 = _matmul_col_tiles(u2, W["w_ffn_t"], tm=1024, tn=D_FF, tc=CONV_TC, name="mm_gu").reshape(2 * D_FF // CONV_TC, B, S, CONV_TC)
    act, a_pre = _conv_act_fwd(gu3, W["conv_w"], small["conv_b"], name="conv_act_fwd")
    act2 = act.reshape(T, D_FF)
    g = {}

    def loss_ep(acc, h_ref, g_ref, t_ref):
        y, vjp = jax.vjp(_rms, acc + h_ref[...], g_ref[...])
        err = y - t_ref[...]
        dx, dg = vjp(err * (1.0 / D))
        return dx, dx, dg, (0.5 / D) * jnp.sum(jnp.sum(err * err, axis=1, keepdims=True), axis=0, keepdims=True)

    dh2, dh2b, g["final_g"], loss = _matmul_ep(
        [(act2, W["w_down"], False, 0)], tm=512, ins=[h, small["final_g"].reshape(1, D), target.reshape(T, D)], in_specs=[row(512), vec, row(512)],
        out_shapes=[tile(F32), tile(BF16), vec_shape, jax.ShapeDtypeStruct((1, 1), F32)],
        out_specs=[row(512), row(512), vec, _full_spec((1, 1))], sums=(2, 3), epilogue=loss_ep, name="mm_h2_loss")
    dact = _matmul(dh2b, W["w_down"], tb=True, out_dtype=BF16, name="mm_dact", tm=1024, tn=D_FF)
    dw_down_t = _matmul(dh2b, act2, ta=True, out_dtype=BF16, name="mm_dw_down", tm=1024, tn=256, tk=8192)
    dg_, dup, g["conv_w"], g["conv_b"] = _conv_act_bwd(gu3, a_pre, W["conv_w"], dact.reshape(B, S, D_FF), name="conv_act_bwd")
    dg2 = dg_.reshape(T, D_FF)
    dup2 = dup.reshape(T, D_FF)
    dw_ffn_t = _matmul(u2, dg2, ta=True, out_t=True, out_dtype=BF16, into=lax.empty((2 * D_FF, D), BF16), o_noff=0, name="mm_dw_ffn_g", tm=1024, tn=256, tk=8192)
    dw_ffn_t = _matmul(u2, dup2, ta=True, out_t=True, out_dtype=BF16, into=dw_ffn_t, o_noff=D_FF // 256, name="mm_dw_ffn_u", tm=1024, tn=256, tk=8192)
    tok = emit("ffn", dict(w_ffn_t=dw_ffn_t, w_down=dw_down_t.T))
    def norm2_bwd_ep(acc_g, acc_u, h_ref, g_ref, dh2_ref):
        _, vjp = jax.vjp(_rms, h_ref[...], g_ref[...])
        dx, dg = vjp(acc_g + acc_u)
        dx = dx + dh2_ref[...]
        return dx, dx, dg

    dh, dhb, g["norm2_g"] = _matmul_ep(
        [(dg2, W["w_ffn_t"], False, 0), (dup2, W["w_ffn_t"], False, 1)], tm=512, ins=[h, small["norm2_g"] + zero(tok), dh2], in_specs=[row(512), vec, row(512)],
        out_shapes=[tile(F32), tile(BF16), vec_shape], out_specs=[row(512), row(512), vec], sums=(2,), epilogue=norm2_bwd_ep, name="mm_du2_norm2_bwd")
    dw_out = _matmul(merged, dhb, ta=True, out_dtype=BF16, name="mm_dw_out", tm=1024, tn=1024, tk=2048)

    def merge_bwd_ep(acc, g_ref, pa_ref, pb_ref, dz_in):
        gt = g_ref[...].astype(F32)
        sa = _sigmoid(gt[:, :D_MODEL])
        sb = _sigmoid(gt[:, D_MODEL:])
        dgates = jnp.concatenate([acc * pa_ref[...].astype(F32) * sa * (1.0 - sa), acc * pb_ref[...].astype(F32) * sb * (1.0 - sb)], axis=1)
        return dgates, acc * sa, acc * sb

    dz, dpa, dpb = _matmul_ep(
        [(dhb, W["w_out"], True, 0)], tm=512, ins=[z, pa, pb, lax.empty((T, W_IN), BF16)], in_specs=[_gates_spec(512), row(512), row(512), ANY],
        out_shapes=[jax.ShapeDtypeStruct((T, W_IN), BF16), tile(BF16), tile(BF16)], out_specs=[_gates_spec(512), row(512), row(512)],
        aliases={3: 0}, epilogue=merge_bwd_ep, name="mm_dmerged_merge_bwd")
    doa, dob = _matmul_ep([(dpa, W["w_a"], True, 0), (dpb, W["w_b"], True, 0)], tm=1024, ins=[], in_specs=[],
                          out_shapes=[tile(BF16)] * 2, out_specs=[row(1024)] * 2, epilogue=lambda da, db: (da, db), name="mm_doa_dob")
    dw_a = _matmul(oa2, dpa, ta=True, out_dtype=BF16, name="mm_dw_a", tm=1024, tn=1024, tk=2048)
    dw_b = _matmul(ob2, dpb, ta=True, out_dtype=BF16, name="mm_dw_b", tm=1024, tn=1024, tk=2048)
    tok = emit("mix", dict(w_out=dw_out, w_a=dw_a, w_b=dw_b))
    dz3, dkv_cur, dkv_prev, dsinks = _swa_bwd(z3, qr, kr, cos, sin, small["attn_sinks"] + zero(tok), lse, dob.reshape(B, S, D),
                                              dz.reshape(B, S, W_IN), name="swa_bwd")
    dz3 = _swa_dkv_combine(dkv_cur, dkv_prev, dz3, name="swa_dkv")
    g["attn_sinks"] = dsinks
    dz3, g["lb"], g["hgrn_norm_g"] = _hgrn_bwd(z3, lb, small["hgrn_norm_g"], states, doa.reshape(B, S, D), dz3, name="hgrn_bwd")
    dz = dz3.reshape(T, W_IN)
    dw_in_t = _matmul(u1, dz, ta=True, out_t=True, o_block_perm=_reference_row_block, out_dtype=BF16, name="mm_dw_in", tm=1024, tn=256, tk=8192)
    tok = emit("in", dict(w_in_t=dw_in_t))
    dx, g["norm1_g"] = _matmul_norm_bwd(dz, w_in_t, W_IN_SEGMENTS, x2, small["norm1_g"], dh, tok, tm=512, name="mm_du1_norm1_bwd")
    g["lb_logits"] = _lb_bwd(g.pop("lb"), lb, name="lb_bwd")
    return loss, dx.reshape(B, S, D), g


def _my_place():
    return lax.axis_index("x"), lax.axis_index("y"), lax.axis_index("c")


def _gather_blocks(x_ref, out_ref, send_sems, recv_sems, local_sem):
    x, y, c = _my_place()
    me, sibling = (x, y, c), (x, y, 1 - c)
    chips = [(1 - x, y), (x, 1 - y), (1 - x, 1 - y)]
    relayed = tuple(jnp.where(c == 0, a, b) for a, b in zip(chips[0], chips[1]))
    relay_to = tuple(jnp.where(c == 0, b, a) for a, b in zip(chips[0], chips[1]))

    def slot(px, py, pc):
        return out_ref.at[4 * px + 2 * py + pc]

    def copy(k, block, to, src=None):
        return pltpu.make_async_remote_copy(
            src_ref=slot(*block) if src is None else src, dst_ref=slot(*block),
            send_sem=send_sems.at[k], recv_sem=recv_sems.at[k], device_id=to, device_id_type=MESH)

    mine = pltpu.make_async_copy(x_ref, slot(*me), local_sem)
    mine.start()
    first = [copy(0, me, sibling, src=x_ref)]
    first += [copy(1 + j, me, (*chip, c), src=x_ref) for j, chip in enumerate(chips[:2])]
    for cp in first:
        cp.start()
    relay = copy(3, (*relayed, c), (*relay_to, c))
    passed = [copy(4 + j, (*chip, c), sibling) for j, chip in enumerate(chips)]
    for j, chip in enumerate(chips):
        copy(1 + j, (*chip, c), me).wait_recv()
        if j < 2:
            @pl.when(c == j)
            def _():
                relay.start()

        passed[j].start()
    copy(0, sibling, me).wait_recv()
    for j, chip in enumerate(chips):
        copy(4 + j, (*chip, 1 - c), me).wait_recv()
    for cp in first + [relay] + passed:
        cp.wait_send()
    mine.wait()


GATHER_SEMS = [pltpu.SemaphoreType.DMA((7,)), pltpu.SemaphoreType.DMA((7,)), pltpu.SemaphoreType.DMA]


def _all_gather(blk, *, name):
    return pl.pallas_call(
        _gather_body_fn(), name=name,
        out_shape=jax.ShapeDtypeStruct((N_DEV,) + blk.shape, blk.dtype),
        in_specs=[ANY], out_specs=ANY,
        scratch_shapes=GATHER_SEMS,
    )(blk)


def _gather_body_fn():
    def body(x_ref, out_ref, send_sems, recv_sems, local_sem):
        _gather_blocks(x_ref, out_ref, send_sems, recv_sems, local_sem)
    return body


SLAB_W = 1152
SMALL_SHAPES = dict(norm1_g=(1, D_MODEL), lb_logits=(2, HGRN_HEADS * HGRN_DK), hgrn_norm_g=(1, HGRN_DK), attn_sinks=(1, ATT_HEADS),
                    norm2_g=(1, D_MODEL), conv_b=(1, D_FF), final_g=(1, D_MODEL))
CONVW_BLK = D_FF // N_DEV
CONVW_STRIDE = SLAB_W // 3


def _slab_layout():
    layout, r = {}, 0
    for nm, (nr, w) in SMALL_SHAPES.items():
        layout[nm] = []
        for i in range(nr):
            for c0 in range(0, w, SLAB_W):
                layout[nm].append((r, i, c0, min(SLAB_W, w - c0)))
                r += 1
    return layout, r


SMALL_ROWS, _N_SMALL_ROWS = _slab_layout()
CONV_ROW0 = -(-_N_SMALL_ROWS // 8) * 8
LOSS_ROW = CONV_ROW0 + N_DEV
SLAB_ROWS = LOSS_ROW + 8


def _small_step(grads, g_conv_w, loss, params, moments, variances, dev, *, name):
    names = list(SMALL_ROWS)
    n = len(names)

    def body(dev_ref, *refs):
        g_refs = dict(zip(names, refs[:n]))
        gc_ref, loss_ref = refs[n], refs[n + 1]
        base = n + 2
        w_refs, m_refs, v_refs = (dict(zip(names + ["conv_w"], refs[base + i * (n + 1):base + (i + 1) * (n + 1)])) for i in range(3))
        o = base + 3 * (n + 1)
        gath_ref, loss_out = refs[o], refs[o + 1]
        outs = {nm: refs[o + 2 + 4 * i:o + 6 + 4 * i] for i, nm in enumerate(names + ["conv_w"])}
        slab, total, send_sems, recv_sems, local_sem = refs[-5:]

        slab[...] = jnp.zeros_like(slab)
        for nm, pieces in SMALL_ROWS.items():
            for r, i, c0, w in pieces:
                slab[r:r + 1, 0:w] = g_refs[nm][i:i + 1, c0:c0 + w]
        for p in range(N_DEV):
            for j in range(3):
                slab[CONV_ROW0 + p:CONV_ROW0 + p + 1, j * CONVW_STRIDE:j * CONVW_STRIDE + CONVW_BLK] = gc_ref[j:j + 1, p * CONVW_BLK:(p + 1) * CONVW_BLK]
        slab[LOSS_ROW:LOSS_ROW + 1, 0:1] = loss_ref[...]
        _gather_blocks(slab, gath_ref, send_sems, recv_sems, local_sem)
        acc = gath_ref[0]
        for p in range(1, N_DEV):
            acc = acc + gath_ref[p]
        total[...] = acc
        loss_out[...] = total[LOSS_ROW:LOSS_ROW + 1, 0:1]

        def update(nm, g, i, c0, w):
            at = (slice(i, i + 1), slice(c0, c0 + w))
            d, mn, vn = _adamw_math(w_refs[nm][at], g, m_refs[nm][at], v_refs[nm][at])
            for ref, val in zip(outs[nm], (g, d, mn, vn)):
                ref[at] = val

        for nm, pieces in SMALL_ROWS.items():
            for r, i, c0, w in pieces:
                update(nm, total[r:r + 1, 0:w], i, c0, w)
        conv_rows = total[CONV_ROW0:CONV_ROW0 + N_DEV, :]
        rowid = lax.broadcasted_iota(jnp.int32, conv_rows.shape, 0)
        mine = jnp.sum(jnp.where(rowid == dev_ref[0], conv_rows, 0.0), axis=0, keepdims=True)
        for j in range(3):
            update("conv_w", mine[:, j * CONVW_STRIDE:j * CONVW_STRIDE + CONVW_BLK], j, 0, CONVW_BLK)

    order = names + ["conv_w"]
    ins = [grads[nm] for nm in names] + [g_conv_w, loss]
    for d in (params, moments, variances):
        ins += [d[nm] for nm in order]
    vmem = pl.BlockSpec(memory_space=pltpu.VMEM)
    out_shape = [jax.ShapeDtypeStruct((N_DEV, SLAB_ROWS, SLAB_W), F32), jax.ShapeDtypeStruct((1, 1), F32)]
    for nm in order:
        out_shape += [jax.ShapeDtypeStruct(params[nm].shape, F32)] * 4
    res = pl.pallas_call(
        body, name=name,
        grid_spec=pltpu.PrefetchScalarGridSpec(
            num_scalar_prefetch=1, grid=(1,),
            in_specs=[vmem] * len(ins), out_specs=[vmem] * len(out_shape),
            scratch_shapes=[pltpu.VMEM((SLAB_ROWS, SLAB_W), F32), pltpu.VMEM((SLAB_ROWS, SLAB_W), F32)] + GATHER_SEMS),
        out_shape=out_shape,
    )(dev, *ins)
    return res[1], {nm: tuple(res[2 + 4 * i:6 + 4 * i]) for i, nm in enumerate(order)}


HBM_SPEC = pl.BlockSpec(memory_space=pltpu.HBM)
SEM_SPEC = pl.BlockSpec(memory_space=pltpu.SEMAPHORE)
DATAFLOW_EFFECT = pltpu.SideEffectType.DATAFLOW_SIDE_EFFECTING
N_PEERS = N_DEV - 1


def _peers(x, y, c):
    return [(1 - x if r & 4 else x, 1 - y if r & 2 else y, 1 - c if r & 1 else c) for r in range(1, N_DEV)]


def _exchange_start(srcs, scatter, *, after=None, name):
    n = len(srcs)
    lands = [lax.empty(a.shape if scatter else (N_DEV,) + a.shape, a.dtype) for a in srcs]
    extra = [] if after is None else [after]

    def body(*refs):
        src_refs, land_refs = refs[:n], refs[n:2 * n]
        send_sems, recv_sems, token = refs[2 * n + len(extra)], refs[2 * n + len(extra) + 1], refs[-1]
        x, y, c = _my_place()
        me = 4 * x + 2 * y + c
        for i in range(n):
            for r, (tx, ty, tc) in enumerate(_peers(x, y, c)):
                src = src_refs[i].at[4 * tx + 2 * ty + tc] if scatter else src_refs[i]
                pltpu.make_async_remote_copy(
                    src_ref=src, dst_ref=land_refs[i].at[me], send_sem=send_sems.at[N_PEERS * i + r],
                    recv_sem=recv_sems.at[N_PEERS * i + r], device_id=(tx, ty, tc), device_id_type=MESH).start()
        token[...] = jnp.zeros_like(token)

    thru = [pltpu.HBM(a.shape, a.dtype) for a in list(srcs) + lands]
    res = pl.pallas_call(
        body, name=name,
        out_shape=(pltpu.SemaphoreType.DMA((N_PEERS * n,)), pltpu.SemaphoreType.DMA((N_PEERS * n,)), *thru,
                   jax.ShapeDtypeStruct((8, 128), F32)),
        in_specs=[HBM_SPEC] * (2 * n) + [ANY] * len(extra),
        out_specs=(SEM_SPEC, SEM_SPEC, *([HBM_SPEC] * (2 * n)), pl.BlockSpec(memory_space=pltpu.VMEM)),
        input_output_aliases={i: 2 + i for i in range(2 * n)},
        compiler_params=pltpu.CompilerParams(has_side_effects=DATAFLOW_EFFECT),
    )(*[pltpu.with_memory_space_constraint(a, pltpu.HBM) for a in list(srcs) + lands], *extra)
    return (res[0], res[1], list(res[2:2 + n]), list(res[2 + n:2 + 2 * n]), scatter), res[-1]


def _exchange_wait(handle, after, *, name):
    send_sems, recv_sems, srcs, lands, scatter = handle
    n = len(srcs)

    def body(*refs):
        src_refs, land_refs = refs[:n], refs[n:2 * n]
        send_sems, recv_sems = refs[2 * n], refs[2 * n + 1]
        x, y, c = _my_place()
        for i in range(n):
            for r in range(N_PEERS):
                src = src_refs[i].at[0] if scatter else src_refs[i]
                cp = pltpu.make_async_remote_copy(
                    src_ref=src, dst_ref=land_refs[i].at[0], send_sem=send_sems.at[N_PEERS * i + r],
                    recv_sem=recv_sems.at[N_PEERS * i + r], device_id=(x, y, c), device_id_type=MESH)
                cp.wait_send()
                cp.wait_recv()

    thru = [pltpu.HBM(a.shape, a.dtype) for a in srcs + lands]
    res = pl.pallas_call(
        body, name=name, out_shape=tuple(thru),
        in_specs=[HBM_SPEC] * (2 * n) + [SEM_SPEC, SEM_SPEC, ANY], out_specs=tuple([HBM_SPEC] * (2 * n)),
        input_output_aliases={i: i for i in range(2 * n)},
        compiler_params=pltpu.CompilerParams(has_side_effects=DATAFLOW_EFFECT),
    )(*srcs, *lands, send_sems, recv_sems, after)
    return list(res[:n]), list(res[n:])


def _with_own(land, own, me):
    return lax.dynamic_update_index_in_dim(land, own, me, 0)


def _adamw_math(w, g, m, v):
    m = ADAM_B1 * m + (1.0 - ADAM_B1) * g
    v = ADAM_B2 * v + (1.0 - ADAM_B2) * (g * g)
    m_hat = m / (1.0 - ADAM_B1 ** ADAM_STEP)
    v_hat = v / (1.0 - ADAM_B2 ** ADAM_STEP)
    delta = -ADAM_LR * (m_hat / (jnp.sqrt(v_hat) + ADAM_EPS) + ADAM_WD * w)
    return delta, m, v


def _adamw_sum(parts, w, m, v, *, name):
    shape = w.shape
    R, n = shape[-2], shape[-1]
    w, m, v = (t.reshape(R, n) for t in (w, m, v))
    tr = _pick(R, (256, 464, 352, 128))

    def body(p_ref, w_ref, m_ref, v_ref, g_ref, d_ref, mo_ref, vo_ref):
        g = p_ref[0].astype(F32)
        for p in range(1, N_DEV):
            g = g + p_ref[p].astype(F32)
        d, mn, vn = _adamw_math(w_ref[...], g, m_ref[...], v_ref[...])
        g_ref[...] = g
        d_ref[...] = d
        mo_ref[...] = mn
        vo_ref[...] = vn

    row = pl.BlockSpec((tr, n), lambda i: (i, 0))
    outs = pl.pallas_call(
        body, name=name, grid=(R // tr,),
        in_specs=[pl.BlockSpec((N_DEV, tr, n), lambda i: (0, i, 0)), row, row, row],
        out_specs=[row, row, row, row],
        out_shape=[jax.ShapeDtypeStruct((R, n), F32)] * 4,
        compiler_params=_params("parallel"),
    )(parts, w, m, v)
    return [t.reshape(shape) for t in outs]


def _lb_bwd(dlb, lb, *, name):
    def body(d_ref, lb_ref, o_ref):
        t = d_ref[...] * lb_ref[...] * (1.0 - lb_ref[...])
        o_ref[0:1, :] = t
        o_ref[1:2, :] = -t

    return pl.pallas_call(body, name=name, out_shape=jax.ShapeDtypeStruct((2, lb.shape[1]), F32))(dlb, lb)


DOWN_BLK, ROW_BLK = D_FF // N_DEV, D_MODEL // N_DEV
W_FFN_BLK = 2 * D_FF // N_DEV
CONV_BITS_SHAPE = (16, 256)


def kernel(x, positions, norm1_g, w_in, lb_logits, hgrn_norm_g, w_a, attn_sinks, w_b, w_out, norm2_g, w_ffn_in, conv_w, conv_b, w_down, final_g, loss_target, m_norm1_g, m_w_in, m_lb_logits, m_hgrn_norm_g, m_w_a, m_attn_sinks, m_w_b, m_w_out, m_norm2_g, m_w_ffn_in, m_conv_w, m_conv_b, m_w_down, m_final_g, v_norm1_g, v_w_in, v_lb_logits, v_hgrn_norm_g, v_w_a, v_attn_sinks, v_w_b, v_w_out, v_norm2_g, v_w_ffn_in, v_conv_w, v_conv_b, v_w_down, v_final_g):
    xi, yi, ci = _my_place()
    dev = 4 * xi + 2 * yi + ci

    tr = lambda t: jnp.transpose(t[0])
    untr = lambda t: jnp.transpose(t)[None]
    w_in_blocks = _all_gather(tr(w_in).astype(BF16), name="ag_w_in")
    conv_bits = lax.bitcast_convert_type(conv_w, BF16).reshape(-1)
    conv_bits = jnp.pad(conv_bits, (0, CONV_BITS_SHAPE[0] * CONV_BITS_SHAPE[1] - conv_bits.shape[0])).reshape(CONV_BITS_SHAPE)
    w_in_full_t = w_in_blocks.reshape(W_IN, D_MODEL)
    gather_handles = {}
    gather_handles["mix"], tok_mix = _exchange_start([w_a[0].astype(BF16), w_b[0].astype(BF16), w_out[0].astype(BF16)], False,
                                                     after=w_in_full_t, name="ag_mix_start")
    gather_handles["ffn"], tok_ffn = _exchange_start([tr(w_ffn_in).astype(BF16), w_down[0].astype(BF16), conv_bits], False,
                                                     after=tok_mix, name="ag_ffn_start")
    start_token = tok_mix + tok_ffn

    def rest_weights(group, after):
        own, lands = _exchange_wait(gather_handles[group], after, name="ag_" + group + "_wait")
        full = [_with_own(l, o, dev) for l, o in zip(lands, own)]
        if group == "mix":
            return dict(zip(("w_a", "w_b", "w_out"), [t.reshape(D_MODEL, D_MODEL) for t in full]))
        bits = full[2].reshape(N_DEV, -1)[:, :3 * CONVW_BLK * 2].reshape(N_DEV, 3, CONVW_BLK, 2)
        return dict(w_ffn_t=full[0].reshape(2 * D_FF, D_MODEL), w_down=full[1].reshape(D_FF, D_MODEL),
                    conv_w=lax.bitcast_convert_type(bits, F32).transpose(1, 0, 2).reshape(3, D_FF))

    handles = {}

    def emit(group, gr):
        if group == "ffn":
            srcs = [gr["w_ffn_t"].reshape(N_DEV, W_FFN_BLK, D_MODEL), gr["w_down"].reshape(N_DEV, DOWN_BLK, D_MODEL)]
        elif group == "mix":
            srcs = [gr[n].reshape(N_DEV, ROW_BLK, D_MODEL) for n in ("w_out", "w_a", "w_b")]
        else:
            srcs = [gr["w_in_t"].reshape(N_DEV, W_IN_BLK, D_MODEL)]
        handles[group], token = _exchange_start(srcs, True, name="rs_" + group + "_start")
        return token

    small = dict(norm1_g=norm1_g, lb_logits=lb_logits, hgrn_norm_g=hgrn_norm_g, attn_sinks=attn_sinks, norm2_g=norm2_g,
                 conv_b=conv_b, final_g=final_g)
    loss, grad_x, g = _local_step(x, positions, loss_target, small, w_in_full_t, rest_weights, emit, start_token)

    def parts_of(group, after):
        srcs, lands = _exchange_wait(handles[group], after, name="rs_" + group + "_wait")
        return [_with_own(l, lax.dynamic_index_in_dim(s, dev, 0, keepdims=False), dev) for s, l in zip(srcs, lands)]

    p_ffn, p_down = parts_of("ffn", grad_x)
    p_out, p_a, p_b = parts_of("mix", grad_x)
    (p_in,) = parts_of("in", grad_x)
    big = dict(
        w_in=[untr(t) for t in _adamw_sum(p_in, tr(w_in), tr(m_w_in), tr(v_w_in), name="adamw_w_in")],
        w_a=_adamw_sum(p_a, w_a, m_w_a, v_w_a, name="adamw_w_a"),
        w_b=_adamw_sum(p_b, w_b, m_w_b, v_w_b, name="adamw_w_b"),
        w_out=_adamw_sum(p_out, w_out, m_w_out, v_w_out, name="adamw_w_out"),
        w_ffn_in=[untr(t) for t in _adamw_sum(p_ffn, tr(w_ffn_in), tr(m_w_ffn_in), tr(v_w_ffn_in), name="adamw_w_ffn_in")],
        w_down=_adamw_sum(p_down, w_down, m_w_down, v_w_down, name="adamw_w_down"),
    )

    row = lambda t: t.reshape(1, -1) if t.ndim == 1 else t
    shard = lambda t: t.reshape(3, CONVW_BLK)
    sm_g = {nm: g[nm] for nm in SMALL_ROWS}
    sm_w = dict(norm1_g=norm1_g, lb_logits=lb_logits, hgrn_norm_g=hgrn_norm_g, attn_sinks=attn_sinks, norm2_g=norm2_g,
                conv_b=conv_b, final_g=row(final_g), conv_w=shard(conv_w))
    sm_m = dict(norm1_g=m_norm1_g, lb_logits=m_lb_logits, hgrn_norm_g=m_hgrn_norm_g, attn_sinks=m_attn_sinks, norm2_g=m_norm2_g,
                conv_b=m_conv_b, final_g=row(m_final_g), conv_w=shard(m_conv_w))
    sm_v = dict(norm1_g=v_norm1_g, lb_logits=v_lb_logits, hgrn_norm_g=v_hgrn_norm_g, attn_sinks=v_attn_sinks, norm2_g=v_norm2_g,
                conv_b=v_conv_b, final_g=row(v_final_g), conv_w=shard(v_conv_w))
    loss_total, sm_out = _small_step(sm_g, g["conv_w"], loss, sm_w, sm_m, sm_v, dev.astype(jnp.int32).reshape(1), name="small_step")
    shapes = dict(final_g=final_g.shape, conv_w=conv_w.shape)

    names = ("norm1_g", "w_in", "lb_logits", "hgrn_norm_g", "w_a", "attn_sinks", "w_b", "w_out", "norm2_g", "w_ffn_in", "conv_w", "conv_b", "w_down", "final_g")
    outs = [loss_total.reshape(()), grad_x]
    for kind in range(4):
        outs += [big[n][kind] if n in big else sm_out[n][kind].reshape(shapes.get(n, sm_out[n][kind].shape)) for n in names]
    return tuple(outs)
```
